```python
import math
import jax, jax.numpy as jnp
from jax import lax
import numpy as np

D_MODEL = 1024
BATCH = 8
SEQ = 2048
DEPTH = 1

N_META = 16
GRID_W = 64
NA_WIDTH = D_MODEL // 2
S5_WIDTH = D_MODEL - NA_WIDTH
MIX_WIDTH = NA_WIDTH + S5_WIDTH
NA_HEAD_DIM = 64
NA_HEADS = NA_WIDTH // NA_HEAD_DIM
NA_KH_MAX = 8
NA_KW = 16
S5_GROUP = 16
S5_GROUPS = S5_WIDTH // S5_GROUP
S5_STATE = 64
D_FF = ((8 * D_MODEL // 3 + 127) // 128) * 128
RMS_EPS = 1e-6
DT_MIN = 1e-3
DT_MAX = 1e-1
NEG_INF = -1e30

kernel_name = "hybrid_natten_s5_macaron_block"


def rms_norm(x, g):
    xf = x.astype(jnp.float32)
    y = xf * lax.rsqrt(jnp.mean(xf * xf, axis=-1, keepdims=True) + RMS_EPS)
    return (y * g.astype(jnp.float32)).astype(x.dtype)


def swiglu(x, w_gate, w_up, w_down):
    return (jax.nn.silu(x @ w_gate) * (x @ w_up)) @ w_down


def _ssm_combine(e1, e2):
    a1, b1 = e1
    a2, b2 = e2
    return a1 * a2, a2 * b1 + b2


def s5_mixer(u, lam_re, lam_im, log_dt, b_re, b_im, c_re, c_im, d_skip, w_glu, b_glu):
    f32 = jnp.float32
    bsz, length, _ = u.shape
    uf = u.astype(f32).reshape(bsz, length, S5_GROUPS, S5_GROUP)
    uc = uf.astype(jnp.complex64)
    y = uf * d_skip.astype(f32).reshape(S5_GROUPS, S5_GROUP)
    for direction in range(2):
        lam = lax.complex(lam_re[direction].astype(f32), lam_im[direction].astype(f32))
        dt = jnp.exp(log_dt[direction].astype(f32))[:, None]
        lam_bar = jnp.exp(lam * dt)
        b = lax.complex(b_re[direction].astype(f32), b_im[direction].astype(f32))
        b_bar = ((lam_bar - 1.0) / lam)[..., None] * b
        bu = jnp.einsum('blgh,gph->blgp', uc, b_bar)
        a = jnp.broadcast_to(lam_bar, bu.shape)
        _, states = lax.associative_scan(_ssm_combine, (a, bu), axis=1, reverse=(direction == 1))
        y = y + jnp.einsum('blgp,ghp->blgh', jnp.real(states), c_re[direction].astype(f32)) \
              - jnp.einsum('blgp,ghp->blgh', jnp.imag(states), c_im[direction].astype(f32))
    y = jax.nn.gelu(y.reshape(bsz, length, S5_WIDTH))
    y = y * jax.nn.sigmoid(y @ w_glu.astype(f32) + b_glu.astype(f32))
    return y.astype(u.dtype)


def neighbourhood_attention(q, k, v, rpb):
    f32 = jnp.float32
    bsz, length = q.shape[0], q.shape[1]
    n_tok = length - N_META
    rows = n_tok // GRID_W
    kh = min(NA_KH_MAX, rows)
    kw = NA_KW
    scale = NA_HEAD_DIM ** -0.5
    qm, qt = q[:, :N_META], q[:, N_META:]
    km, kt = k[:, :N_META], k[:, N_META:]
    vm, vt = v[:, :N_META], v[:, N_META:]

    r = np.arange(rows)
    row_start = np.clip(r - kh // 2, 0, rows - kh)
    row_idx = row_start[:, None] + np.arange(kh)[None, :]
    c = np.arange(GRID_W)
    col_start = np.clip(c - kw // 2, 0, GRID_W - kw)
    col_in = (c[None, :] >= col_start[:, None]) & (c[None, :] < col_start[:, None] + kw)
    dr = row_idx - r[:, None] + NA_KH_MAX - 1
    dc = np.clip(c[None, :] - c[:, None] + kw - 1, 0, 2 * kw - 2)
    bias = rpb.astype(f32)[:, dr[:, None, :, None], dc[None, :, None, :]]
    bias = jnp.where(jnp.asarray(col_in)[None, None, :, None, :], bias, NEG_INF)
    bias = bias.reshape(NA_HEADS, rows, GRID_W, kh * GRID_W)

    qg = qt.reshape(bsz, rows, GRID_W, NA_HEADS, NA_HEAD_DIM)
    kg = kt.reshape(bsz, rows, GRID_W, NA_HEADS, NA_HEAD_DIM)
    vg = vt.reshape(bsz, rows, GRID_W, NA_HEADS, NA_HEAD_DIM)
    kb = jnp.take(kg, row_idx, axis=1).reshape(bsz, rows, kh * GRID_W, NA_HEADS, NA_HEAD_DIM)
    vb = jnp.take(vg, row_idx, axis=1).reshape(bsz, rows, kh * GRID_W, NA_HEADS, NA_HEAD_DIM)

    s_loc = jnp.einsum('brqhd,brkhd->bhrqk', qg, kb, preferred_element_type=f32) * scale + bias
    s_meta = jnp.einsum('brqhd,bmhd->bhrqm', qg, km, preferred_element_type=f32) * scale
    p = jax.nn.softmax(jnp.concatenate([s_loc, s_meta], axis=-1), axis=-1)
    p_loc, p_meta = p[..., :kh * GRID_W], p[..., kh * GRID_W:]
    o_tok = jnp.einsum('bhrqk,brkhd->brqhd', p_loc.astype(v.dtype), vb) \
          + jnp.einsum('bhrqm,bmhd->brqhd', p_meta.astype(v.dtype), vm)
    o_tok = o_tok.reshape(bsz, n_tok, NA_HEADS, NA_HEAD_DIM)

    s_mm = jnp.einsum('bqhd,bmhd->bhqm', qm, km, preferred_element_type=f32) * scale
    o_meta = jnp.einsum('bhqm,bmhd->bqhd', jax.nn.softmax(s_mm, axis=-1).astype(v.dtype), vm)
    return jnp.concatenate([o_meta, o_tok], axis=1).reshape(bsz, length, NA_WIDTH)


def _fwd_setup_inputs(seed: int = 0) -> dict:
    key = jax.random.key(seed)
    ks = iter(jax.random.split(key, 40))
    f32 = jnp.float32

    def nrm(shape, scale):
        return jax.random.normal(next(ks), shape, f32) * scale

    def gain(shape):
        return 1.0 + 0.02 * jax.random.normal(next(ks), shape, f32)

    L = DEPTH
    n_idx = jnp.arange(S5_STATE, dtype=f32)
    inp = {}
    inp['x'] = nrm((BATCH, SEQ, D_MODEL), 1.0)
    inp['meta_tokens'] = nrm((N_META, D_MODEL), 1.0)
    inp['ffn1_pre_g'] = gain((L, D_MODEL))
    inp['ffn1_post_g'] = gain((L, D_MODEL))
    inp['ffn1_w_gate'] = nrm((L, D_MODEL, D_FF), D_MODEL ** -0.5)
    inp['ffn1_w_up'] = nrm((L, D_MODEL, D_FF), D_MODEL ** -0.5)
    inp['ffn1_w_down'] = nrm((L, D_FF, D_MODEL), D_FF ** -0.5)
    inp['mix_pre_g'] = gain((L, D_MODEL))
    inp['w_in'] = nrm((L, D_MODEL, 3 * NA_WIDTH + S5_WIDTH), D_MODEL ** -0.5)
    inp['na_rpb'] = nrm((L, NA_HEADS, 2 * NA_KH_MAX - 1, 2 * NA_KW - 1), 0.05)
    inp['s5_lam_re'] = -0.5 + nrm((L, 2, S5_GROUPS, S5_STATE), 0.01)
    inp['s5_lam_im'] = math.pi * n_idx + nrm((L, 2, S5_GROUPS, S5_STATE), 0.01)
    inp['s5_log_dt'] = jax.random.uniform(next(ks), (L, 2, S5_GROUPS), f32, math.log(DT_MIN), math.log(DT_MAX))
    inp['s5_b_re'] = nrm((L, 2, S5_GROUPS, S5_STATE, S5_GROUP), (0.5 / S5_GROUP) ** 0.5)
    inp['s5_b_im'] = nrm((L, 2, S5_GROUPS, S5_STATE, S5_GROUP), (0.5 / S5_GROUP) ** 0.5)
    inp['s5_c_re'] = nrm((L, 2, S5_GROUPS, S5_GROUP, S5_STATE), (0.5 / S5_STATE) ** 0.5)
    inp['s5_c_im'] = nrm((L, 2, S5_GROUPS, S5_GROUP, S5_STATE), (0.5 / S5_STATE) ** 0.5)
    inp['s5_d'] = nrm((L, S5_WIDTH), 1.0)
    inp['s5_w_glu'] = nrm((L, S5_WIDTH, S5_WIDTH), S5_WIDTH ** -0.5)
    inp['s5_b_glu'] = nrm((L, S5_WIDTH), 0.02)
    inp['na_out_g'] = gain((L, NA_WIDTH))
    inp['s5_out_g'] = gain((L, S5_WIDTH))
    inp['w_out'] = nrm((L, MIX_WIDTH, D_MODEL), MIX_WIDTH ** -0.5)
    inp['mix_post_g'] = gain((L, D_MODEL))
    inp['ffn2_pre_g'] = gain((L, D_MODEL))
    inp['ffn2_post_g'] = gain((L, D_MODEL))
    inp['ffn2_w_gate'] = nrm((L, D_MODEL, D_FF), D_MODEL ** -0.5)
    inp['ffn2_w_up'] = nrm((L, D_MODEL, D_FF), D_MODEL ** -0.5)
    inp['ffn2_w_down'] = nrm((L, D_FF, D_MODEL), D_FF ** -0.5)
    inp['final_g'] = gain((L, D_MODEL))
    return inp


def _fwd_reference(x, meta_tokens, ffn1_pre_g, ffn1_post_g, ffn1_w_gate, ffn1_w_up, ffn1_w_down,
              mix_pre_g, w_in, na_rpb, s5_lam_re, s5_lam_im, s5_log_dt, s5_b_re, s5_b_im,
              s5_c_re, s5_c_im, s5_d, s5_w_glu, s5_b_glu, na_out_g, s5_out_g, w_out, mix_post_g,
              ffn2_pre_g, ffn2_post_g, ffn2_w_gate, ffn2_w_up, ffn2_w_down, final_g):
    bsz = x.shape[0]
    meta = jnp.broadcast_to(meta_tokens.astype(x.dtype)[None], (bsz, N_META, D_MODEL))
    h = jnp.concatenate([meta, x], axis=1)
    length = h.shape[1]
    for i in range(DEPTH):
        f = swiglu(rms_norm(h, ffn1_pre_g[i]), ffn1_w_gate[i], ffn1_w_up[i], ffn1_w_down[i])
        h = h + 0.5 * rms_norm(f, ffn1_post_g[i])
        a = rms_norm(h, mix_pre_g[i])
        proj = a @ w_in[i]
        q = proj[..., :NA_WIDTH].reshape(bsz, length, NA_HEADS, NA_HEAD_DIM)
        k = proj[..., NA_WIDTH:2 * NA_WIDTH].reshape(bsz, length, NA_HEADS, NA_HEAD_DIM)
        v = proj[..., 2 * NA_WIDTH:3 * NA_WIDTH].reshape(bsz, length, NA_HEADS, NA_HEAD_DIM)
        u = proj[..., 3 * NA_WIDTH:]
        o_na = neighbourhood_attention(q, k, v, na_rpb[i])
        o_s5 = s5_mixer(u, s5_lam_re[i], s5_lam_im[i], s5_log_dt[i], s5_b_re[i], s5_b_im[i],
                        s5_c_re[i], s5_c_im[i], s5_d[i], s5_w_glu[i], s5_b_glu[i])
        mix = jnp.concatenate([rms_norm(o_na, na_out_g[i]), rms_norm(o_s5, s5_out_g[i])], axis=-1) @ w_out[i]
        h = h + rms_norm(mix, mix_post_g[i])
        f = swiglu(rms_norm(h, ffn2_pre_g[i]), ffn2_w_gate[i], ffn2_w_up[i], ffn2_w_down[i])
        h = h + 0.5 * rms_norm(f, ffn2_post_g[i])
        h = rms_norm(h, final_g[i])
    return h[:, N_META:]


import jax as _jax
import jax.numpy as _jnp

TWIN_FORMAT = 'train_step'
FWD_PARAMS = ['x', 'meta_tokens', 'ffn1_pre_g', 'ffn1_post_g', 'ffn1_w_gate', 'ffn1_w_up', 'ffn1_w_down', 'mix_pre_g', 'w_in', 'na_rpb', 's5_lam_re', 's5_lam_im', 's5_log_dt', 's5_b_re', 's5_b_im', 's5_c_re', 's5_c_im', 's5_d', 's5_w_glu', 's5_b_glu', 'na_out_g', 's5_out_g', 'w_out', 'mix_post_g', 'ffn2_pre_g', 'ffn2_post_g', 'ffn2_w_gate', 'ffn2_w_up', 'ffn2_w_down', 'final_g']
TWIN_WEIGHTS = ['meta_tokens', 'ffn1_pre_g', 'ffn1_post_g', 'ffn1_w_gate', 'ffn1_w_up', 'ffn1_w_down', 'mix_pre_g', 'w_in', 'na_rpb', 's5_lam_re', 's5_lam_im', 's5_log_dt', 's5_b_re', 's5_b_im', 's5_c_re', 's5_c_im', 's5_d', 's5_w_glu', 's5_b_glu', 'na_out_g', 's5_out_g', 'w_out', 'mix_post_g', 'ffn2_pre_g', 'ffn2_post_g', 'ffn2_w_gate', 'ffn2_w_up', 'ffn2_w_down', 'final_g']
TWIN_DIFF_INPUT = 'x'
TWIN_INPUTS = ['x', 'meta_tokens', 'ffn1_pre_g', 'ffn1_post_g', 'ffn1_w_gate', 'ffn1_w_up', 'ffn1_w_down', 'mix_pre_g', 'w_in', 'na_rpb', 's5_lam_re', 's5_lam_im', 's5_log_dt', 's5_b_re', 's5_b_im', 's5_c_re', 's5_c_im', 's5_d', 's5_w_glu', 's5_b_glu', 'na_out_g', 's5_out_g', 'w_out', 'mix_post_g', 'ffn2_pre_g', 'ffn2_post_g', 'ffn2_w_gate', 'ffn2_w_up', 'ffn2_w_down', 'final_g', 'loss_target', 'm_meta_tokens', 'm_ffn1_pre_g', 'm_ffn1_post_g', 'm_ffn1_w_gate', 'm_ffn1_w_up', 'm_ffn1_w_down', 'm_mix_pre_g', 'm_w_in', 'm_na_rpb', 'm_s5_lam_re', 'm_s5_lam_im', 'm_s5_log_dt', 'm_s5_b_re', 'm_s5_b_im', 'm_s5_c_re', 'm_s5_c_im', 'm_s5_d', 'm_s5_w_glu', 'm_s5_b_glu', 'm_na_out_g', 'm_s5_out_g', 'm_w_out', 'm_mix_post_g', 'm_ffn2_pre_g', 'm_ffn2_post_g', 'm_ffn2_w_gate', 'm_ffn2_w_up', 'm_ffn2_w_down', 'm_final_g', 'v_meta_tokens', 'v_ffn1_pre_g', 'v_ffn1_post_g', 'v_ffn1_w_gate', 'v_ffn1_w_up', 'v_ffn1_w_down', 'v_mix_pre_g', 'v_w_in', 'v_na_rpb', 'v_s5_lam_re', 'v_s5_lam_im', 'v_s5_log_dt', 'v_s5_b_re', 'v_s5_b_im', 'v_s5_c_re', 'v_s5_c_im', 'v_s5_d', 'v_s5_w_glu', 'v_s5_b_glu', 'v_na_out_g', 'v_s5_out_g', 'v_w_out', 'v_mix_post_g', 'v_ffn2_pre_g', 'v_ffn2_post_g', 'v_ffn2_w_gate', 'v_ffn2_w_up', 'v_ffn2_w_down', 'v_final_g']
TWIN_OUTPUTS = ['loss', 'grad_x', 'grad_meta_tokens', 'grad_ffn1_pre_g', 'grad_ffn1_post_g', 'grad_ffn1_w_gate', 'grad_ffn1_w_up', 'grad_ffn1_w_down', 'grad_mix_pre_g', 'grad_w_in', 'grad_na_rpb', 'grad_s5_lam_re', 'grad_s5_lam_im', 'grad_s5_log_dt', 'grad_s5_b_re', 'grad_s5_b_im', 'grad_s5_c_re', 'grad_s5_c_im', 'grad_s5_d', 'grad_s5_w_glu', 'grad_s5_b_glu', 'grad_na_out_g', 'grad_s5_out_g', 'grad_w_out', 'grad_mix_post_g', 'grad_ffn2_pre_g', 'grad_ffn2_post_g', 'grad_ffn2_w_gate', 'grad_ffn2_w_up', 'grad_ffn2_w_down', 'grad_final_g', 'delta_meta_tokens', 'delta_ffn1_pre_g', 'delta_ffn1_post_g', 'delta_ffn1_w_gate', 'delta_ffn1_w_up', 'delta_ffn1_w_down', 'delta_mix_pre_g', 'delta_w_in', 'delta_na_rpb', 'delta_s5_lam_re', 'delta_s5_lam_im', 'delta_s5_log_dt', 'delta_s5_b_re', 'delta_s5_b_im', 'delta_s5_c_re', 'delta_s5_c_im', 'delta_s5_d', 'delta_s5_w_glu', 'delta_s5_b_glu', 'delta_na_out_g', 'delta_s5_out_g', 'delta_w_out', 'delta_mix_post_g', 'delta_ffn2_pre_g', 'delta_ffn2_post_g', 'delta_ffn2_w_gate', 'delta_ffn2_w_up', 'delta_ffn2_w_down', 'delta_final_g', 'new_m_meta_tokens', 'new_m_ffn1_pre_g', 'new_m_ffn1_post_g', 'new_m_ffn1_w_gate', 'new_m_ffn1_w_up', 'new_m_ffn1_w_down', 'new_m_mix_pre_g', 'new_m_w_in', 'new_m_na_rpb', 'new_m_s5_lam_re', 'new_m_s5_lam_im', 'new_m_s5_log_dt', 'new_m_s5_b_re', 'new_m_s5_b_im', 'new_m_s5_c_re', 'new_m_s5_c_im', 'new_m_s5_d', 'new_m_s5_w_glu', 'new_m_s5_b_glu', 'new_m_na_out_g', 'new_m_s5_out_g', 'new_m_w_out', 'new_m_mix_post_g', 'new_m_ffn2_pre_g', 'new_m_ffn2_post_g', 'new_m_ffn2_w_gate', 'new_m_ffn2_w_up', 'new_m_ffn2_w_down', 'new_m_final_g', 'new_v_meta_tokens', 'new_v_ffn1_pre_g', 'new_v_ffn1_post_g', 'new_v_ffn1_w_gate', 'new_v_ffn1_w_up', 'new_v_ffn1_w_down', 'new_v_mix_pre_g', 'new_v_w_in', 'new_v_na_rpb', 'new_v_s5_lam_re', 'new_v_s5_lam_im', 'new_v_s5_log_dt', 'new_v_s5_b_re', 'new_v_s5_b_im', 'new_v_s5_c_re', 'new_v_s5_c_im', 'new_v_s5_d', 'new_v_s5_w_glu', 'new_v_s5_b_glu', 'new_v_na_out_g', 'new_v_s5_out_g', 'new_v_w_out', 'new_v_mix_post_g', 'new_v_ffn2_pre_g', 'new_v_ffn2_post_g', 'new_v_ffn2_w_gate', 'new_v_ffn2_w_up', 'new_v_ffn2_w_down', 'new_v_final_g']
TWIN_LEAF_KINDS = {'loss': 'loss', 'grad_x': 'grad_x', 'grad_meta_tokens': 'grad_w', 'grad_ffn1_pre_g': 'grad_w', 'grad_ffn1_post_g': 'grad_w', 'grad_ffn1_w_gate': 'grad_w', 'grad_ffn1_w_up': 'grad_w', 'grad_ffn1_w_down': 'grad_w', 'grad_mix_pre_g': 'grad_w', 'grad_w_in': 'grad_w', 'grad_na_rpb': 'grad_w', 'grad_s5_lam_re': 'grad_w', 'grad_s5_lam_im': 'grad_w', 'grad_s5_log_dt': 'grad_w', 'grad_s5_b_re': 'grad_w', 'grad_s5_b_im': 'grad_w', 'grad_s5_c_re': 'grad_w', 'grad_s5_c_im': 'grad_w', 'grad_s5_d': 'grad_w', 'grad_s5_w_glu': 'grad_w', 'grad_s5_b_glu': 'grad_w', 'grad_na_out_g': 'grad_w', 'grad_s5_out_g': 'grad_w', 'grad_w_out': 'grad_w', 'grad_mix_post_g': 'grad_w', 'grad_ffn2_pre_g': 'grad_w', 'grad_ffn2_post_g': 'grad_w', 'grad_ffn2_w_gate': 'grad_w', 'grad_ffn2_w_up': 'grad_w', 'grad_ffn2_w_down': 'grad_w', 'grad_final_g': 'grad_w', 'delta_meta_tokens': 'delta_w', 'delta_ffn1_pre_g': 'delta_w', 'delta_ffn1_post_g': 'delta_w', 'delta_ffn1_w_gate': 'delta_w', 'delta_ffn1_w_up': 'delta_w', 'delta_ffn1_w_down': 'delta_w', 'delta_mix_pre_g': 'delta_w', 'delta_w_in': 'delta_w', 'delta_na_rpb': 'delta_w', 'delta_s5_lam_re': 'delta_w', 'delta_s5_lam_im': 'delta_w', 'delta_s5_log_dt': 'delta_w', 'delta_s5_b_re': 'delta_w', 'delta_s5_b_im': 'delta_w', 'delta_s5_c_re': 'delta_w', 'delta_s5_c_im': 'delta_w', 'delta_s5_d': 'delta_w', 'delta_s5_w_glu': 'delta_w', 'delta_s5_b_glu': 'delta_w', 'delta_na_out_g': 'delta_w', 'delta_s5_out_g': 'delta_w', 'delta_w_out': 'delta_w', 'delta_mix_post_g': 'delta_w', 'delta_ffn2_pre_g': 'delta_w', 'delta_ffn2_post_g': 'delta_w', 'delta_ffn2_w_gate': 'delta_w', 'delta_ffn2_w_up': 'delta_w', 'delta_ffn2_w_down': 'delta_w', 'delta_final_g': 'delta_w', 'new_m_meta_tokens': 'new_m', 'new_m_ffn1_pre_g': 'new_m', 'new_m_ffn1_post_g': 'new_m', 'new_m_ffn1_w_gate': 'new_m', 'new_m_ffn1_w_up': 'new_m', 'new_m_ffn1_w_down': 'new_m', 'new_m_mix_pre_g': 'new_m', 'new_m_w_in': 'new_m', 'new_m_na_rpb': 'new_m', 'new_m_s5_lam_re': 'new_m', 'new_m_s5_lam_im': 'new_m', 'new_m_s5_log_dt': 'new_m', 'new_m_s5_b_re': 'new_m', 'new_m_s5_b_im': 'new_m', 'new_m_s5_c_re': 'new_m', 'new_m_s5_c_im': 'new_m', 'new_m_s5_d': 'new_m', 'new_m_s5_w_glu': 'new_m', 'new_m_s5_b_glu': 'new_m', 'new_m_na_out_g': 'new_m', 'new_m_s5_out_g': 'new_m', 'new_m_w_out': 'new_m', 'new_m_mix_post_g': 'new_m', 'new_m_ffn2_pre_g': 'new_m', 'new_m_ffn2_post_g': 'new_m', 'new_m_ffn2_w_gate': 'new_m', 'new_m_ffn2_w_up': 'new_m', 'new_m_ffn2_w_down': 'new_m', 'new_m_final_g': 'new_m', 'new_v_meta_tokens': 'new_v', 'new_v_ffn1_pre_g': 'new_v', 'new_v_ffn1_post_g': 'new_v', 'new_v_ffn1_w_gate': 'new_v', 'new_v_ffn1_w_up': 'new_v', 'new_v_ffn1_w_down': 'new_v', 'new_v_mix_pre_g': 'new_v', 'new_v_w_in': 'new_v', 'new_v_na_rpb': 'new_v', 'new_v_s5_lam_re': 'new_v', 'new_v_s5_lam_im': 'new_v', 'new_v_s5_log_dt': 'new_v', 'new_v_s5_b_re': 'new_v', 'new_v_s5_b_im': 'new_v', 'new_v_s5_c_re': 'new_v', 'new_v_s5_c_im': 'new_v', 'new_v_s5_d': 'new_v', 'new_v_s5_w_glu': 'new_v', 'new_v_s5_b_glu': 'new_v', 'new_v_na_out_g': 'new_v', 'new_v_s5_out_g': 'new_v', 'new_v_w_out': 'new_v', 'new_v_mix_post_g': 'new_v', 'new_v_ffn2_pre_g': 'new_v', 'new_v_ffn2_post_g': 'new_v', 'new_v_ffn2_w_gate': 'new_v', 'new_v_ffn2_w_up': 'new_v', 'new_v_ffn2_w_down': 'new_v', 'new_v_final_g': 'new_v'}


def _forward(args):
    return _fwd_reference(*[args[k] for k in FWD_PARAMS])


def _output_shape():
    out = _jax.eval_shape(lambda: _forward(_fwd_setup_inputs(0)))
    return out.shape, out.dtype

N_MICROBATCH = 1
ADAM_LR = 0.001
ADAM_B1 = 0.9
ADAM_B2 = 0.999
ADAM_EPS = 1e-08
ADAM_WD = 0.01
ADAM_STEP = 10
PER_EXAMPLE_BATCH_AXIS = {'x': 0, 'loss_target': 0}
SHARED_INPUTS = []
_WEIGHT_DTYPES = {'meta_tokens': _jnp.float32, 'ffn1_pre_g': _jnp.float32, 'ffn1_post_g': _jnp.float32, 'ffn1_w_gate': _jnp.float32, 'ffn1_w_up': _jnp.float32, 'ffn1_w_down': _jnp.float32, 'mix_pre_g': _jnp.float32, 'w_in': _jnp.float32, 'na_rpb': _jnp.float32, 's5_lam_re': _jnp.float32, 's5_lam_im': _jnp.float32, 's5_log_dt': _jnp.float32, 's5_b_re': _jnp.float32, 's5_b_im': _jnp.float32, 's5_c_re': _jnp.float32, 's5_c_im': _jnp.float32, 's5_d': _jnp.float32, 's5_w_glu': _jnp.float32, 's5_b_glu': _jnp.float32, 'na_out_g': _jnp.float32, 's5_out_g': _jnp.float32, 'w_out': _jnp.float32, 'mix_post_g': _jnp.float32, 'ffn2_pre_g': _jnp.float32, 'ffn2_post_g': _jnp.float32, 'ffn2_w_gate': _jnp.float32, 'ffn2_w_up': _jnp.float32, 'ffn2_w_down': _jnp.float32, 'final_g': _jnp.float32}
MOMENT_SCALE = {'meta_tokens': 1.059649e-02, 'ffn1_pre_g': 1.042096e-01, 'ffn1_post_g': 8.141180e-02, 'ffn1_w_gate': 4.405728e-02, 'ffn1_w_up': 4.273118e-02, 'ffn1_w_down': 7.075376e-02, 'mix_pre_g': 1.263191e-01, 'w_in': 8.860705e-02, 'na_rpb': 2.970597e-02, 's5_lam_re': 4.785314e-03, 's5_lam_im': 4.896972e-03, 's5_log_dt': 4.467129e+00, 's5_b_re': 3.248140e-03, 's5_b_im': 3.180093e-03, 's5_c_re': 6.418746e-03, 's5_c_im': 6.434554e-03, 's5_d': 1.123603e-01, 's5_w_glu': 2.425824e-02, 's5_b_glu': 3.889281e-02, 'na_out_g': 8.456553e-02, 's5_out_g': 8.843141e-02, 'w_out': 9.034685e-02, 'mix_post_g': 2.280822e-01, 'ffn2_pre_g': 6.043003e-02, 'ffn2_post_g': 5.976434e-02, 'ffn2_w_gate': 2.467801e-02, 'ffn2_w_up': 2.397372e-02, 'ffn2_w_down': 3.968244e-02, 'final_g': 1.604506e+01}


def _to_microbatches(a, axis):
    t = _jnp.moveaxis(a, axis, 0)
    t = t.reshape((N_MICROBATCH, t.shape[0] // N_MICROBATCH) + t.shape[1:])
    return _jnp.moveaxis(t, 1, axis + 1)


def setup_inputs(seed: int = 0) -> dict:
    inp = _fwd_setup_inputs(seed)
    key = _jax.random.fold_in(_jax.random.key(seed), 7919)
    shape, _ = _output_shape()
    out = dict(inp)
    out["loss_target"] = _jax.random.normal(_jax.random.fold_in(key, 0), shape, _jnp.float32)
    for i, name in enumerate(TWIN_WEIGHTS):
        w = inp[name].astype(_jnp.float32)
        if MOMENT_SCALE is None:
            s = _jnp.sqrt(_jnp.mean(_jnp.square(w)) + 1e-30)
        else:
            s = MOMENT_SCALE[name]
        km, kv = _jax.random.split(_jax.random.fold_in(key, i + 1))
        out[name] = w
        out["m_" + name] = s * _jax.random.normal(km, w.shape, _jnp.float32)
        out["v_" + name] = (s * s) * _jax.random.uniform(kv, w.shape, _jnp.float32, 0.5, 1.5)
    if N_MICROBATCH > 1:
        for name, axis in PER_EXAMPLE_BATCH_AXIS.items():
            out[name] = _to_microbatches(out[name], axis)
    return {'x': out['x'], 'meta_tokens': out['meta_tokens'], 'ffn1_pre_g': out['ffn1_pre_g'], 'ffn1_post_g': out['ffn1_post_g'], 'ffn1_w_gate': out['ffn1_w_gate'], 'ffn1_w_up': out['ffn1_w_up'], 'ffn1_w_down': out['ffn1_w_down'], 'mix_pre_g': out['mix_pre_g'], 'w_in': out['w_in'], 'na_rpb': out['na_rpb'], 's5_lam_re': out['s5_lam_re'], 's5_lam_im': out['s5_lam_im'], 's5_log_dt': out['s5_log_dt'], 's5_b_re': out['s5_b_re'], 's5_b_im': out['s5_b_im'], 's5_c_re': out['s5_c_re'], 's5_c_im': out['s5_c_im'], 's5_d': out['s5_d'], 's5_w_glu': out['s5_w_glu'], 's5_b_glu': out['s5_b_glu'], 'na_out_g': out['na_out_g'], 's5_out_g': out['s5_out_g'], 'w_out': out['w_out'], 'mix_post_g': out['mix_post_g'], 'ffn2_pre_g': out['ffn2_pre_g'], 'ffn2_post_g': out['ffn2_post_g'], 'ffn2_w_gate': out['ffn2_w_gate'], 'ffn2_w_up': out['ffn2_w_up'], 'ffn2_w_down': out['ffn2_w_down'], 'final_g': out['final_g'], 'loss_target': out['loss_target'], 'm_meta_tokens': out['m_meta_tokens'], 'm_ffn1_pre_g': out['m_ffn1_pre_g'], 'm_ffn1_post_g': out['m_ffn1_post_g'], 'm_ffn1_w_gate': out['m_ffn1_w_gate'], 'm_ffn1_w_up': out['m_ffn1_w_up'], 'm_ffn1_w_down': out['m_ffn1_w_down'], 'm_mix_pre_g': out['m_mix_pre_g'], 'm_w_in': out['m_w_in'], 'm_na_rpb': out['m_na_rpb'], 'm_s5_lam_re': out['m_s5_lam_re'], 'm_s5_lam_im': out['m_s5_lam_im'], 'm_s5_log_dt': out['m_s5_log_dt'], 'm_s5_b_re': out['m_s5_b_re'], 'm_s5_b_im': out['m_s5_b_im'], 'm_s5_c_re': out['m_s5_c_re'], 'm_s5_c_im': out['m_s5_c_im'], 'm_s5_d': out['m_s5_d'], 'm_s5_w_glu': out['m_s5_w_glu'], 'm_s5_b_glu': out['m_s5_b_glu'], 'm_na_out_g': out['m_na_out_g'], 'm_s5_out_g': out['m_s5_out_g'], 'm_w_out': out['m_w_out'], 'm_mix_post_g': out['m_mix_post_g'], 'm_ffn2_pre_g': out['m_ffn2_pre_g'], 'm_ffn2_post_g': out['m_ffn2_post_g'], 'm_ffn2_w_gate': out['m_ffn2_w_gate'], 'm_ffn2_w_up': out['m_ffn2_w_up'], 'm_ffn2_w_down': out['m_ffn2_w_down'], 'm_final_g': out['m_final_g'], 'v_meta_tokens': out['v_meta_tokens'], 'v_ffn1_pre_g': out['v_ffn1_pre_g'], 'v_ffn1_post_g': out['v_ffn1_post_g'], 'v_ffn1_w_gate': out['v_ffn1_w_gate'], 'v_ffn1_w_up': out['v_ffn1_w_up'], 'v_ffn1_w_down': out['v_ffn1_w_down'], 'v_mix_pre_g': out['v_mix_pre_g'], 'v_w_in': out['v_w_in'], 'v_na_rpb': out['v_na_rpb'], 'v_s5_lam_re': out['v_s5_lam_re'], 'v_s5_lam_im': out['v_s5_lam_im'], 'v_s5_log_dt': out['v_s5_log_dt'], 'v_s5_b_re': out['v_s5_b_re'], 'v_s5_b_im': out['v_s5_b_im'], 'v_s5_c_re': out['v_s5_c_re'], 'v_s5_c_im': out['v_s5_c_im'], 'v_s5_d': out['v_s5_d'], 'v_s5_w_glu': out['v_s5_w_glu'], 'v_s5_b_glu': out['v_s5_b_glu'], 'v_na_out_g': out['v_na_out_g'], 'v_s5_out_g': out['v_s5_out_g'], 'v_w_out': out['v_w_out'], 'v_mix_post_g': out['v_mix_post_g'], 'v_ffn2_pre_g': out['v_ffn2_pre_g'], 'v_ffn2_post_g': out['v_ffn2_post_g'], 'v_ffn2_w_gate': out['v_ffn2_w_gate'], 'v_ffn2_w_up': out['v_ffn2_w_up'], 'v_ffn2_w_down': out['v_ffn2_w_down'], 'v_final_g': out['v_final_g']}


def _loss(weights, diff, rest, loss_target):
    with _jax.named_scope("forward"):
        args = {**rest, TWIN_DIFF_INPUT: diff, **{k: w.astype(_WEIGHT_DTYPES[k]) for k, w in weights.items()}}
        y = _forward(args)
    with _jax.named_scope("loss_head"):
        err = _jnp.square(y.astype(_jnp.float32) - loss_target)
        return 0.5 * _jnp.sum(_jnp.mean(err, axis=-1)) if err.ndim else 0.5 * err


def _adamw(w, g, m, v):
    m = ADAM_B1 * m + (1.0 - ADAM_B1) * g
    v = ADAM_B2 * v + (1.0 - ADAM_B2) * _jnp.square(g)
    m_hat = m / (1.0 - ADAM_B1 ** ADAM_STEP)
    v_hat = v / (1.0 - ADAM_B2 ** ADAM_STEP)
    delta = -ADAM_LR * (m_hat / (_jnp.sqrt(v_hat) + ADAM_EPS) + ADAM_WD * w)
    return delta, m, v


def reference(x, meta_tokens, ffn1_pre_g, ffn1_post_g, ffn1_w_gate, ffn1_w_up, ffn1_w_down, mix_pre_g, w_in, na_rpb, s5_lam_re, s5_lam_im, s5_log_dt, s5_b_re, s5_b_im, s5_c_re, s5_c_im, s5_d, s5_w_glu, s5_b_glu, na_out_g, s5_out_g, w_out, mix_post_g, ffn2_pre_g, ffn2_post_g, ffn2_w_gate, ffn2_w_up, ffn2_w_down, final_g, loss_target, m_meta_tokens, m_ffn1_pre_g, m_ffn1_post_g, m_ffn1_w_gate, m_ffn1_w_up, m_ffn1_w_down, m_mix_pre_g, m_w_in, m_na_rpb, m_s5_lam_re, m_s5_lam_im, m_s5_log_dt, m_s5_b_re, m_s5_b_im, m_s5_c_re, m_s5_c_im, m_s5_d, m_s5_w_glu, m_s5_b_glu, m_na_out_g, m_s5_out_g, m_w_out, m_mix_post_g, m_ffn2_pre_g, m_ffn2_post_g, m_ffn2_w_gate, m_ffn2_w_up, m_ffn2_w_down, m_final_g, v_meta_tokens, v_ffn1_pre_g, v_ffn1_post_g, v_ffn1_w_gate, v_ffn1_w_up, v_ffn1_w_down, v_mix_pre_g, v_w_in, v_na_rpb, v_s5_lam_re, v_s5_lam_im, v_s5_log_dt, v_s5_b_re, v_s5_b_im, v_s5_c_re, v_s5_c_im, v_s5_d, v_s5_w_glu, v_s5_b_glu, v_na_out_g, v_s5_out_g, v_w_out, v_mix_post_g, v_ffn2_pre_g, v_ffn2_post_g, v_ffn2_w_gate, v_ffn2_w_up, v_ffn2_w_down, v_final_g):
    given = dict(x=x, meta_tokens=meta_tokens, ffn1_pre_g=ffn1_pre_g, ffn1_post_g=ffn1_post_g, ffn1_w_gate=ffn1_w_gate, ffn1_w_up=ffn1_w_up, ffn1_w_down=ffn1_w_down, mix_pre_g=mix_pre_g, w_in=w_in, na_rpb=na_rpb, s5_lam_re=s5_lam_re, s5_lam_im=s5_lam_im, s5_log_dt=s5_log_dt, s5_b_re=s5_b_re, s5_b_im=s5_b_im, s5_c_re=s5_c_re, s5_c_im=s5_c_im, s5_d=s5_d, s5_w_glu=s5_w_glu, s5_b_glu=s5_b_glu, na_out_g=na_out_g, s5_out_g=s5_out_g, w_out=w_out, mix_post_g=mix_post_g, ffn2_pre_g=ffn2_pre_g, ffn2_post_g=ffn2_post_g, ffn2_w_gate=ffn2_w_gate, ffn2_w_up=ffn2_w_up, ffn2_w_down=ffn2_w_down, final_g=final_g, loss_target=loss_target, m_meta_tokens=m_meta_tokens, m_ffn1_pre_g=m_ffn1_pre_g, m_ffn1_post_g=m_ffn1_post_g, m_ffn1_w_gate=m_ffn1_w_gate, m_ffn1_w_up=m_ffn1_w_up, m_ffn1_w_down=m_ffn1_w_down, m_mix_pre_g=m_mix_pre_g, m_w_in=m_w_in, m_na_rpb=m_na_rpb, m_s5_lam_re=m_s5_lam_re, m_s5_lam_im=m_s5_lam_im, m_s5_log_dt=m_s5_log_dt, m_s5_b_re=m_s5_b_re, m_s5_b_im=m_s5_b_im, m_s5_c_re=m_s5_c_re, m_s5_c_im=m_s5_c_im, m_s5_d=m_s5_d, m_s5_w_glu=m_s5_w_glu, m_s5_b_glu=m_s5_b_glu, m_na_out_g=m_na_out_g, m_s5_out_g=m_s5_out_g, m_w_out=m_w_out, m_mix_post_g=m_mix_post_g, m_ffn2_pre_g=m_ffn2_pre_g, m_ffn2_post_g=m_ffn2_post_g, m_ffn2_w_gate=m_ffn2_w_gate, m_ffn2_w_up=m_ffn2_w_up, m_ffn2_w_down=m_ffn2_w_down, m_final_g=m_final_g, v_meta_tokens=v_meta_tokens, v_ffn1_pre_g=v_ffn1_pre_g, v_ffn1_post_g=v_ffn1_post_g, v_ffn1_w_gate=v_ffn1_w_gate, v_ffn1_w_up=v_ffn1_w_up, v_ffn1_w_down=v_ffn1_w_down, v_mix_pre_g=v_mix_pre_g, v_w_in=v_w_in, v_na_rpb=v_na_rpb, v_s5_lam_re=v_s5_lam_re, v_s5_lam_im=v_s5_lam_im, v_s5_log_dt=v_s5_log_dt, v_s5_b_re=v_s5_b_re, v_s5_b_im=v_s5_b_im, v_s5_c_re=v_s5_c_re, v_s5_c_im=v_s5_c_im, v_s5_d=v_s5_d, v_s5_w_glu=v_s5_w_glu, v_s5_b_glu=v_s5_b_glu, v_na_out_g=v_na_out_g, v_s5_out_g=v_s5_out_g, v_w_out=v_w_out, v_mix_post_g=v_mix_post_g, v_ffn2_pre_g=v_ffn2_pre_g, v_ffn2_post_g=v_ffn2_post_g, v_ffn2_w_gate=v_ffn2_w_gate, v_ffn2_w_up=v_ffn2_w_up, v_ffn2_w_down=v_ffn2_w_down, v_final_g=v_final_g)
    weights = {n: given[n] for n in TWIN_WEIGHTS}
    shared = {n: given[n] for n in SHARED_INPUTS}
    per_example = {n: given[n] for n in ['x']}
    grad_fn = _jax.value_and_grad(_loss, argnums=(0, 1))

    def one_microbatch(ex, loss_target):
        ex = dict(ex)
        diff = ex.pop(TWIN_DIFF_INPUT)
        return grad_fn(weights, diff, {**shared, **ex}, loss_target)

    if N_MICROBATCH == 1:
        loss, (grad_w, grad_x) = one_microbatch(per_example, given["loss_target"])
    else:
        def body(carry, xs):
            loss_sum, grad_sum = carry
            l_k, (gw_k, gx_k) = one_microbatch(xs[0], xs[1])
            with _jax.named_scope("update"):
                return (loss_sum + l_k, _jax.tree.map(_jnp.add, grad_sum, gw_k)), gx_k

        init = (_jnp.zeros((), _jnp.float32), _jax.tree.map(_jnp.zeros_like, weights))
        (loss, grad_w), grad_x = _jax.lax.scan(body, init, (per_example, given["loss_target"]))
    with _jax.named_scope("update"):
        delta_w, new_m, new_v = {}, {}, {}
        for n in TWIN_WEIGHTS:
            delta_w[n], new_m[n], new_v[n] = _adamw(weights[n], grad_w[n], given["m_" + n], given["v_" + n])
    return (loss, grad_x, *[grad_w[n] for n in TWIN_WEIGHTS], *[delta_w[n] for n in TWIN_WEIGHTS],
            *[new_m[n] for n in TWIN_WEIGHTS], *[new_v[n] for n in TWIN_WEIGHTS])
```

```python
import functools
import math

import numpy as np
import jax
import jax.numpy as jnp
from jax import lax
from jax.experimental import pallas as pl
from jax.experimental.pallas import tpu as pltpu

F32 = jnp.float32
BF16 = jnp.bfloat16

D = 1024
N_META = 16
GRID_W = 64
NA_W = 512
S5_W = 512
HEAD_DIM = 64
N_HEADS = 8
KH = 8
KW = 16
S5_G = 32
S5_P = 64
S5_H = 16
N_BUNDLE = 4
FF = 2816
N_CHIP = 4
FC = FF // N_CHIP
EPS = 1e-6
NEG_INF = -1e30
Q_ROWS = 4
K_ROWS = 12
QB = Q_ROWS * GRID_W
KB = K_ROWS * GRID_W
SCAN_CHUNK = 256

ADAM_LR = 0.001
ADAM_B1 = 0.9
ADAM_B2 = 0.999
ADAM_EPS = 1e-08
ADAM_WD = 0.01
ADAM_STEP = 10

NT = (((1,), (1,)), ((), ()))
TN = (((0,), (0,)), ((), ()))
MESH_ID = pl.DeviceIdType.MESH


def _cp(sem=None, vmem_mb=None):
    kw = {}
    if sem is not None:
        kw["dimension_semantics"] = sem
    if vmem_mb is not None:
        kw["vmem_limit_bytes"] = vmem_mb << 20
    return pltpu.CompilerParams(**kw)


def _full(shape):
    n = len(shape)
    return pl.BlockSpec(shape, lambda *_: (0,) * n)


def _rows(tm, w):
    return pl.BlockSpec((tm, w), lambda i: (i, 0))


ANY = pl.BlockSpec(memory_space=pl.ANY)


def _rms(x, g):
    r = lax.rsqrt(jnp.mean(x * x, axis=-1, keepdims=True) + EPS)
    return x * r * g


def _rms_bwd(x, g, dy):
    r = lax.rsqrt(jnp.mean(x * x, axis=-1, keepdims=True) + EPS)
    xh = x * r
    dg = jnp.sum(dy * xh, axis=0, keepdims=True)
    dyg = dy * g
    dx = r * (dyg - xh * jnp.mean(dyg * xh, axis=-1, keepdims=True))
    return dx, dg


def _dot(a, b):
    return jnp.dot(a, b, preferred_element_type=F32)


def _dg(a, b, dims):
    return lax.dot_general(a, b, dims, preferred_element_type=F32)


def _ffn_fwd(name, h, g_pre, g_post, wg, wu, wd, tm):
    tp = h.shape[0]
    nt = tp // tm

    def body(h_ref, gp_ref, gq_ref, wg_ref, wu_ref, wd_ref, hn_ref, gate_ref, up_ref, f_ref, xn_s, acc_s):
        c = pl.program_id(1)

        @pl.when(c == 0)
        def _():
            xn_s[...] = _rms(h_ref[...], gp_ref[...]).astype(BF16)
            acc_s[...] = jnp.zeros_like(acc_s)

        xn = xn_s[...]
        gate = _dot(xn, wg_ref[0])
        up = _dot(xn, wu_ref[0])
        gate_ref[0] = gate
        up_ref[0] = up
        act = (gate * jax.nn.sigmoid(gate) * up).astype(BF16)
        acc_s[...] += _dot(act, wd_ref[0])

        @pl.when(c == N_CHIP - 1)
        def _():
            f = acc_s[...]
            f_ref[...] = f
            hn_ref[...] = h_ref[...] + 0.5 * _rms(f, gq_ref[...])

    return pl.pallas_call(
        body, name=name, grid=(nt, N_CHIP),
        in_specs=[pl.BlockSpec((tm, D), lambda i, c: (i, 0)), _full((1, D)), _full((1, D)),
                  pl.BlockSpec((1, D, FC), lambda i, c: (c, 0, 0)),
                  pl.BlockSpec((1, D, FC), lambda i, c: (c, 0, 0)),
                  pl.BlockSpec((1, FC, D), lambda i, c: (c, 0, 0))],
        out_specs=[pl.BlockSpec((tm, D), lambda i, c: (i, 0)),
                   pl.BlockSpec((1, tm, FC), lambda i, c: (c, i, 0)),
                   pl.BlockSpec((1, tm, FC), lambda i, c: (c, i, 0)),
                   pl.BlockSpec((tm, D), lambda i, c: (i, 0))],
        out_shape=[jax.ShapeDtypeStruct((tp, D), F32), jax.ShapeDtypeStruct((N_CHIP, tp, FC), F32),
                   jax.ShapeDtypeStruct((N_CHIP, tp, FC), F32), jax.ShapeDtypeStruct((tp, D), F32)],
        scratch_shapes=[pltpu.VMEM((tm, D), BF16), pltpu.VMEM((tm, D), F32)],
        compiler_params=_cp(("arbitrary", "arbitrary"), 48),
    )(h, g_pre, g_post, wg, wu, wd)


def _ffn_bwd(name, h, g_pre, df, gate, up, wg, wu, wd, tm):
    tp = h.shape[0]
    nt = tp // tm

    def body(h_ref, gp_ref, df_ref, gate_ref, up_ref, wg_ref, wu_ref, wd_ref,
             dwg_ref, dwu_ref, dwd_ref, dxn_ref, ag, au, ad):
        c = pl.program_id(0)
        i = pl.program_id(1)

        @pl.when(i == 0)
        def _():
            ag[...] = jnp.zeros_like(ag)
            au[...] = jnp.zeros_like(au)
            ad[...] = jnp.zeros_like(ad)

        xn = _rms(h_ref[...], gp_ref[...]).astype(BF16)
        dfb = df_ref[...].astype(BF16)
        gt = gate_ref[0]
        u = up_ref[0]
        sg = jax.nn.sigmoid(gt)
        si = gt * sg
        act = (si * u).astype(BF16)
        dact = _dg(dfb, wd_ref[0], NT)
        ad[...] += _dg(act, dfb, TN)
        dgate = (dact * u * (sg * (1.0 + gt * (1.0 - sg)))).astype(BF16)
        dup = (dact * si).astype(BF16)
        ag[...] += _dg(xn, dgate, TN)
        au[...] += _dg(xn, dup, TN)
        dxn_ref[0] = _dg(dgate, wg_ref[0], NT) + _dg(dup, wu_ref[0], NT)

        @pl.when(i == nt - 1)
        def _():
            pltpu.sync_copy(ag, dwg_ref.at[c])
            pltpu.sync_copy(au, dwu_ref.at[c])
            pltpu.sync_copy(ad, dwd_ref.at[c])

    return pl.pallas_call(
        body, name=name, grid=(N_CHIP, nt),
        in_specs=[pl.BlockSpec((tm, D), lambda c, i: (i, 0)), _full((1, D)),
                  pl.BlockSpec((tm, D), lambda c, i: (i, 0)),
                  pl.BlockSpec((1, tm, FC), lambda c, i: (c, i, 0)),
                  pl.BlockSpec((1, tm, FC), lambda c, i: (c, i, 0)),
                  pl.BlockSpec((1, D, FC), lambda c, i: (c, 0, 0)),
                  pl.BlockSpec((1, D, FC), lambda c, i: (c, 0, 0)),
                  pl.BlockSpec((1, FC, D), lambda c, i: (c, 0, 0))],
        out_specs=[ANY, ANY, ANY, pl.BlockSpec((1, tm, D), lambda c, i: (c, i, 0))],
        out_shape=[jax.ShapeDtypeStruct((N_CHIP, D, FC), F32), jax.ShapeDtypeStruct((N_CHIP, D, FC), F32),
                   jax.ShapeDtypeStruct((N_CHIP, FC, D), F32), jax.ShapeDtypeStruct((N_CHIP, tp, D), F32)],
        scratch_shapes=[pltpu.VMEM((D, FC), F32), pltpu.VMEM((D, FC), F32), pltpu.VMEM((FC, D), F32)],
        compiler_params=_cp(("arbitrary", "arbitrary"), 56),
    )(h, g_pre, df, gate, up, wg, wu, wd)


def _ffn_pre_bwd(name, dh, dxn_part, h, g_pre, tm):
    tp = h.shape[0]
    nt = tp // tm

    def body(dh_ref, dxn_ref, h_ref, gp_ref, out_ref, dg_ref):
        i = pl.program_id(0)
        dxn = (dxn_ref[0] + dxn_ref[1]) + (dxn_ref[2] + dxn_ref[3])
        dx, dg = _rms_bwd(h_ref[...], gp_ref[...], dxn)
        out_ref[...] = dh_ref[...] + dx

        @pl.when(i == 0)
        def _():
            dg_ref[...] = jnp.zeros_like(dg_ref)

        dg_ref[...] += dg

    return pl.pallas_call(
        body, name=name, grid=(nt,),
        in_specs=[_rows(tm, D), pl.BlockSpec((N_CHIP, tm, D), lambda i: (0, i, 0)), _rows(tm, D), _full((1, D))],
        out_specs=[_rows(tm, D), _full((1, D))],
        out_shape=[jax.ShapeDtypeStruct((tp, D), F32), jax.ShapeDtypeStruct((1, D), F32)],
        compiler_params=_cp(("arbitrary",), 48),
    )(dh, dxn_part, h, g_pre)


def _mix_in(h, g, w_in, tm):
    tp = h.shape[0]

    def body(h_ref, g_ref, w_ref, q_ref, k_ref, v_ref, u_ref):
        a = _rms(h_ref[...], g_ref[...]).astype(BF16)
        q_ref[...] = _dot(a, w_ref[0]).astype(BF16)
        k_ref[...] = _dot(a, w_ref[1]).astype(BF16)
        v_ref[...] = _dot(a, w_ref[2]).astype(BF16)
        u_ref[...] = _dot(a, w_ref[3])

    return pl.pallas_call(
        body, name="mix_in", grid=(tp // tm,),
        in_specs=[_rows(tm, D), _full((1, D)), _full((N_CHIP, D, NA_W))],
        out_specs=[_rows(tm, NA_W)] * 4,
        out_shape=[jax.ShapeDtypeStruct((tp, NA_W), BF16)] * 3 + [jax.ShapeDtypeStruct((tp, S5_W), F32)],
        compiler_params=_cp(("arbitrary",), 40),
    )(h, g, w_in)


def _gelu(x):
    return jax.nn.gelu(x, approximate=True)


def _gelu_grad(x):
    k = math.sqrt(2.0 / math.pi)
    t = jnp.tanh(k * (x + 0.044715 * x * x * x))
    return 0.5 * (1.0 + t) + 0.5 * x * (1.0 - t * t) * k * (1.0 + 3.0 * 0.044715 * x * x)


def _mix_out(o_na, y_pre, h, w_glu, b_glu, g_na, g_s5, w_out, g_post, tm):
    tp = h.shape[0]

    def body(ona_ref, yp_ref, h_ref, wglu_ref, bglu_ref, gna_ref, gs5_ref, wout_ref, gpost_ref, hn_ref, mix_ref):
        y = _gelu(yp_ref[...])
        z = _dot(y.astype(BF16), wglu_ref[...]) + bglu_ref[...]
        o_s5 = y * jax.nn.sigmoid(z)
        n1 = _rms(ona_ref[...], gna_ref[...]).astype(BF16)
        n2 = _rms(o_s5, gs5_ref[...]).astype(BF16)
        mix = _dot(n1, wout_ref[0:NA_W, :]) + _dot(n2, wout_ref[NA_W:, :])
        mix_ref[...] = mix
        hn_ref[...] = h_ref[...] + _rms(mix, gpost_ref[...])

    return pl.pallas_call(
        body, name="mix_out", grid=(tp // tm,),
        in_specs=[_rows(tm, NA_W), _rows(tm, S5_W), _rows(tm, D), _full((S5_W, S5_W)), _full((1, S5_W)),
                  _full((1, NA_W)), _full((1, S5_W)), _full((D, D)), _full((1, D))],
        out_specs=[_rows(tm, D), _rows(tm, D)],
        out_shape=[jax.ShapeDtypeStruct((tp, D), F32)] * 2,
        compiler_params=_cp(("arbitrary",), 40),
    )(o_na, y_pre, h, w_glu, b_glu, g_na, g_s5, w_out, g_post)


def _mix_out_bwd(dh, mix, o_na, y_pre, w_glu, b_glu, g_na, g_s5, w_out, g_post, tm):
    tp = dh.shape[0]
    nt = tp // tm

    def body(dh_ref, mix_ref, ona_ref, yp_ref, wglu_ref, bglu_ref, gna_ref, gs5_ref, wout_ref, gpost_ref,
             dona_ref, dyp_ref, dwout_ref, dwglu_ref, dgpost_ref, dgna_ref, dgs5_ref, dbglu_ref, a_out, a_glu):
        i = pl.program_id(0)

        @pl.when(i == 0)
        def _():
            a_out[...] = jnp.zeros_like(a_out)
            a_glu[...] = jnp.zeros_like(a_glu)
            dgpost_ref[...] = jnp.zeros_like(dgpost_ref)
            dgna_ref[...] = jnp.zeros_like(dgna_ref)
            dgs5_ref[...] = jnp.zeros_like(dgs5_ref)
            dbglu_ref[...] = jnp.zeros_like(dbglu_ref)

        dmix, dgpost = _rms_bwd(mix_ref[...], gpost_ref[...], dh_ref[...])
        dgpost_ref[...] += dgpost
        yp = yp_ref[...]
        y = _gelu(yp)
        yb = y.astype(BF16)
        z = _dot(yb, wglu_ref[...]) + bglu_ref[...]
        sg = jax.nn.sigmoid(z)
        o_s5 = y * sg
        o_na = ona_ref[...]
        n1 = _rms(o_na, gna_ref[...]).astype(BF16)
        n2 = _rms(o_s5, gs5_ref[...]).astype(BF16)
        dmb = dmix.astype(BF16)
        a_out[0:NA_W, :] += _dg(n1, dmb, TN)
        a_out[NA_W:, :] += _dg(n2, dmb, TN)
        dn1 = _dg(dmb, wout_ref[0:NA_W, :], NT)
        dn2 = _dg(dmb, wout_ref[NA_W:, :], NT)
        dona, dgna = _rms_bwd(o_na, gna_ref[...], dn1)
        dona_ref[...] = dona
        dgna_ref[...] += dgna
        dos5, dgs5 = _rms_bwd(o_s5, gs5_ref[...], dn2)
        dgs5_ref[...] += dgs5
        dz = dos5 * y * (sg * (1.0 - sg))
        dbglu_ref[...] += jnp.sum(dz, axis=0, keepdims=True)
        dzb = dz.astype(BF16)
        a_glu[...] += _dg(yb, dzb, TN)
        dy = dos5 * sg + _dg(dzb, wglu_ref[...], NT)
        dyp_ref[...] = dy * _gelu_grad(yp)

        @pl.when(i == nt - 1)
        def _():
            pltpu.sync_copy(a_out, dwout_ref)
            pltpu.sync_copy(a_glu, dwglu_ref)

    return pl.pallas_call(
        body, name="mix_out_bwd", grid=(nt,),
        in_specs=[_rows(tm, D), _rows(tm, D), _rows(tm, NA_W), _rows(tm, S5_W), _full((S5_W, S5_W)),
                  _full((1, S5_W)), _full((1, NA_W)), _full((1, S5_W)), _full((D, D)), _full((1, D))],
        out_specs=[_rows(tm, NA_W), _rows(tm, S5_W), ANY, ANY, _full((1, D)), _full((1, NA_W)),
                   _full((1, S5_W)), _full((1, S5_W))],
        out_shape=[jax.ShapeDtypeStruct((tp, NA_W), F32), jax.ShapeDtypeStruct((tp, S5_W), F32),
                   jax.ShapeDtypeStruct((D, D), F32), jax.ShapeDtypeStruct((S5_W, S5_W), F32),
                   jax.ShapeDtypeStruct((1, D), F32), jax.ShapeDtypeStruct((1, NA_W), F32),
                   jax.ShapeDtypeStruct((1, S5_W), F32), jax.ShapeDtypeStruct((1, S5_W), F32)],
        scratch_shapes=[pltpu.VMEM((D, D), F32), pltpu.VMEM((S5_W, S5_W), F32)],
        compiler_params=_cp(("arbitrary",), 48),
    )(dh, mix, o_na, y_pre, w_glu, b_glu, g_na, g_s5, w_out, g_post)


def _mix_in_bwd(dq, dk, dv, du, h, g, w_in, dh, f1, g_post1, tm):
    tp = h.shape[0]
    nt = tp // tm

    def body(dq_ref, dk_ref, dv_ref, du_ref, h_ref, g_ref, w_ref, dh_ref, f_ref, gq_ref,
             dh1_ref, df_ref, dw_ref, dg_ref, dgq_ref, acc):
        i = pl.program_id(0)

        @pl.when(i == 0)
        def _():
            acc[...] = jnp.zeros_like(acc)
            dg_ref[...] = jnp.zeros_like(dg_ref)
            dgq_ref[...] = jnp.zeros_like(dgq_ref)

        x = h_ref[...]
        a = _rms(x, g_ref[...]).astype(BF16)
        da = jnp.zeros((tm, D), F32)
        for j, r in enumerate((dq_ref, dk_ref, dv_ref, du_ref)):
            dp = r[...].astype(BF16)
            da = da + _dg(dp, w_ref[j], NT)
            acc[j] += _dg(a, dp, TN)
        dx, dg = _rms_bwd(x, g_ref[...], da)
        dh1 = dh_ref[...] + dx
        dh1_ref[...] = dh1
        dg_ref[...] += dg
        df, dgq = _rms_bwd(f_ref[...], gq_ref[...], 0.5 * dh1)
        df_ref[...] = df
        dgq_ref[...] += dgq

        @pl.when(i == nt - 1)
        def _():
            pltpu.sync_copy(acc, dw_ref)

    return pl.pallas_call(
        body, name="mix_in_bwd", grid=(nt,),
        in_specs=[_rows(tm, NA_W)] * 4 + [_rows(tm, D), _full((1, D)), _full((N_CHIP, D, NA_W)), _rows(tm, D),
                                         _rows(tm, D), _full((1, D))],
        out_specs=[_rows(tm, D), _rows(tm, D), ANY, _full((1, D)), _full((1, D))],
        out_shape=[jax.ShapeDtypeStruct((tp, D), F32), jax.ShapeDtypeStruct((tp, D), F32),
                   jax.ShapeDtypeStruct((N_CHIP, D, NA_W), F32), jax.ShapeDtypeStruct((1, D), F32),
                   jax.ShapeDtypeStruct((1, D), F32)],
        scratch_shapes=[pltpu.VMEM((N_CHIP, D, NA_W), F32)],
        compiler_params=_cp(("arbitrary",), 48),
    )(dq, dk, dv, du, h, g, w_in, dh, f1, g_post1)


def _final_loss(h, g_final, target, f2, g_post2, n_tok, tm):
    tp = h.shape[0]

    def body(h_ref, g_ref, t_ref, f_ref, gq_ref, dh_ref, df_ref, loss_ref, dg_ref, dgq_ref):
        i = pl.program_id(0)

        @pl.when(i == 0)
        def _():
            loss_ref[...] = jnp.zeros_like(loss_ref)
            dg_ref[...] = jnp.zeros_like(dg_ref)
            dgq_ref[...] = jnp.zeros_like(dgq_ref)

        x = h_ref[...]
        y = _rms(x, g_ref[...])
        row = i * tm + lax.broadcasted_iota(jnp.int32, (tm, 1), 0)
        valid = (row >= N_META) & (row < N_META + n_tok)
        e = jnp.where(valid, y - t_ref[...], 0.0)
        loss_ref[...] += 0.5 * jnp.sum(jnp.mean(e * e, axis=-1, keepdims=True), axis=0, keepdims=True)
        dx, dg = _rms_bwd(x, g_ref[...], e * (1.0 / D))
        dh_ref[...] = dx
        dg_ref[...] += dg
        df, dgq = _rms_bwd(f_ref[...], gq_ref[...], 0.5 * dx)
        df_ref[...] = df
        dgq_ref[...] += dgq

    return pl.pallas_call(
        body, name="final_loss", grid=(tp // tm,),
        in_specs=[_rows(tm, D), _full((1, D)), _rows(tm, D), _rows(tm, D), _full((1, D))],
        out_specs=[_rows(tm, D), _rows(tm, D), _full((1, 1)), _full((1, D)), _full((1, D))],
        out_shape=[jax.ShapeDtypeStruct((tp, D), F32), jax.ShapeDtypeStruct((tp, D), F32),
                   jax.ShapeDtypeStruct((1, 1), F32), jax.ShapeDtypeStruct((1, D), F32),
                   jax.ShapeDtypeStruct((1, D), F32)],
        compiler_params=_cp(("arbitrary",), 40),
    )(h, g_final, target, f2, g_post2)


def _na_patterns(n_rows):
    pats = []
    for kind in range(3):
        pat = [[-1] * K_ROWS for _ in range(Q_ROWS)]
        for i in range(Q_ROWS):
            for jj in range(K_ROWS):
                if kind == 0 and jj < KH:
                    pat[i][jj] = jj - i + KH - 1
                elif kind == 1 and i <= jj < i + KH:
                    pat[i][jj] = jj - i + 3
                elif kind == 2 and K_ROWS - KH <= jj:
                    pat[i][jj] = jj - i - 1
        pats.append(pat)
    return pats


def _diag_onehot():
    q = np.arange(GRID_W)[:, None]
    kc = np.arange(GRID_W)[None, :]
    start = np.clip(q - KW // 2, 0, GRID_W - KW)
    col_in = (kc >= start) & (kc < start + KW)
    e = np.zeros((32, GRID_W, GRID_W), np.float32)
    for d in range(2 * KW - 1):
        e[d] = ((kc - q + KW - 1) == d) & col_in
    return e.reshape(32, GRID_W * GRID_W), col_in


def _rpb_expand(rpb2, e):
    def body(r_ref, e_ref, o_ref):
        o_ref[...] = jnp.dot(r_ref[...], e_ref[...], preferred_element_type=F32, precision=lax.Precision.HIGHEST)

    return pl.pallas_call(
        body, name="rpb_expand", out_shape=jax.ShapeDtypeStruct((rpb2.shape[0], e.shape[1]), F32),
        in_specs=[pl.BlockSpec(memory_space=pltpu.VMEM)] * 2, out_specs=pl.BlockSpec(memory_space=pltpu.VMEM),
    )(rpb2, e)


def _rpb_collapse(dtb2, et):
    def body(d_ref, e_ref, o_ref):
        o_ref[...] = jnp.dot(d_ref[...], e_ref[...], preferred_element_type=F32, precision=lax.Precision.HIGHEST)

    return pl.pallas_call(
        body, name="rpb_collapse", out_shape=jax.ShapeDtypeStruct((dtb2.shape[0], et.shape[1]), F32),
        in_specs=[pl.BlockSpec(memory_space=pltpu.VMEM)] * 2, out_specs=pl.BlockSpec(memory_space=pltpu.VMEM),
    )(dtb2, et)


def _bias_tables(rpb, n_rows):
    e, col_in = _diag_onehot()
    rpb2 = jnp.pad(rpb.reshape(N_HEADS * (2 * KH - 1), 2 * KW - 1), ((0, 0), (0, 1)))
    tb = _rpb_expand(rpb2, jnp.asarray(e)).reshape(N_HEADS, 2 * KH - 1, GRID_W, GRID_W)
    tb = jnp.where(jnp.asarray(col_in)[None, None], tb, NEG_INF)
    neg = jnp.full((N_HEADS, GRID_W, GRID_W), NEG_INF, F32)
    tabs = []
    for pat in _na_patterns(n_rows):
        rows = [jnp.concatenate([tb[:, dr] if dr >= 0 else neg for dr in pat[i]], axis=-1) for i in range(Q_ROWS)]
        tabs.append(jnp.concatenate(rows, axis=1))
    return jnp.stack(tabs)


def _attn_geometry(n_tok):
    n_rows = n_tok // GRID_W
    assert n_rows % Q_ROWS == 0 and n_rows >= K_ROWS
    return n_rows, n_rows // Q_ROWS


def _attn_probs(qh, kh, kmh, bias, scale):
    s = _dg(qh, kh, NT) * scale + bias
    sm = _dg(qh, kmh, NT) * scale
    m = jnp.maximum(jnp.max(s, axis=-1, keepdims=True), jnp.max(sm, axis=-1, keepdims=True))
    p = jnp.exp(s - m)
    pm = jnp.exp(sm - m)
    inv = 1.0 / (jnp.sum(p, axis=-1, keepdims=True) + jnp.sum(pm, axis=-1, keepdims=True))
    return p * inv, pm * inv


def _meta_probs(qmh, kmh, scale):
    s = _dg(qmh, kmh, NT) * scale
    p = jnp.exp(s - jnp.max(s, axis=-1, keepdims=True))
    return p / jnp.sum(p, axis=-1, keepdims=True)


def _step_rows(r, n_rows):
    q0 = pl.multiple_of(N_META + r * QB, 16)
    k0 = pl.multiple_of(N_META + jnp.clip(Q_ROWS * r - (K_ROWS - KH), 0, n_rows - K_ROWS) * GRID_W, 16)
    return q0, k0


def _attn_fwd(q, k, v, bias, n_tok):
    tp = q.shape[0]
    n_rows, n_steps = _attn_geometry(n_tok)
    scale = HEAD_DIM ** -0.5

    def body(q_ref, k_ref, v_ref, b_ref, o_ref):
        r = pl.program_id(1)
        km = k_ref[0:N_META, :]
        vm = v_ref[0:N_META, :]

        @pl.when(r == 0)
        def _():
            qm = q_ref[0:N_META, :]
            outs = []
            for hh in range(2):
                sl = slice(hh * HEAD_DIM, (hh + 1) * HEAD_DIM)
                p = _meta_probs(qm[:, sl], km[:, sl], scale)
                outs.append(_dot(p.astype(BF16), vm[:, sl]))
            o_ref[0:N_META, :] = jnp.concatenate(outs, axis=1)
            o_ref[N_META + n_tok:, :] = jnp.zeros((tp - N_META - n_tok, 2 * HEAD_DIM), F32)

        q0, k0 = _step_rows(r, n_rows)
        qb = q_ref[pl.ds(q0, QB), :]
        kb = k_ref[pl.ds(k0, KB), :]
        vb = v_ref[pl.ds(k0, KB), :]
        outs = []
        for hh in range(2):
            sl = slice(hh * HEAD_DIM, (hh + 1) * HEAD_DIM)
            p, pm = _attn_probs(qb[:, sl], kb[:, sl], km[:, sl], b_ref[0, hh], scale)
            outs.append(_dot(p.astype(BF16), vb[:, sl]) + _dot(pm.astype(BF16), vm[:, sl]))
        o_ref[pl.ds(q0, QB), :] = jnp.concatenate(outs, axis=1)

    def bias_map(hp, r):
        return (jnp.where(r == 0, 0, jnp.where(r == n_steps - 1, 2, 1)), hp, 0, 0)

    col = pl.BlockSpec((tp, 2 * HEAD_DIM), lambda hp, r: (0, hp))
    return pl.pallas_call(
        body, name="attn_fwd", grid=(N_HEADS // 2, n_steps),
        in_specs=[col, col, col, pl.BlockSpec((1, 2, QB, KB), bias_map)],
        out_specs=col, out_shape=jax.ShapeDtypeStruct((tp, NA_W), F32),
        compiler_params=_cp(("arbitrary", "arbitrary"), 40),
    )(q, k, v, bias)


def _attn_bwd(q, k, v, bias, do, n_tok):
    tp = q.shape[0]
    n_rows, n_steps = _attn_geometry(n_tok)
    scale = HEAD_DIM ** -0.5
    pats = _na_patterns(n_rows)

    def body(q_ref, k_ref, v_ref, b_ref, do_ref, dq_ref, dk_ref, dv_ref, dtb_ref):
        r = pl.program_id(1)
        km = k_ref[0:N_META, :]
        vm = v_ref[0:N_META, :]

        @pl.when(r == 0)
        def _():
            dk_ref[...] = jnp.zeros_like(dk_ref)
            dv_ref[...] = jnp.zeros_like(dv_ref)
            dtb_ref[...] = jnp.zeros_like(dtb_ref)
            dq_ref[N_META + n_tok:, :] = jnp.zeros((tp - N_META - n_tok, 2 * HEAD_DIM), F32)
            qm = q_ref[0:N_META, :]
            dom = do_ref[0:N_META, :].astype(BF16)
            dqs, dks, dvs = [], [], []
            for hh in range(2):
                sl = slice(hh * HEAD_DIM, (hh + 1) * HEAD_DIM)
                p = _meta_probs(qm[:, sl], km[:, sl], scale)
                dp = _dg(dom[:, sl], vm[:, sl], NT)
                ds = (p * (dp - jnp.sum(dp * p, axis=-1, keepdims=True))).astype(BF16)
                dvs.append(_dg(p.astype(BF16), dom[:, sl], TN))
                dqs.append(_dot(ds, km[:, sl]) * scale)
                dks.append(_dg(ds, qm[:, sl], TN) * scale)
            dq_ref[0:N_META, :] = jnp.concatenate(dqs, axis=1)
            dk_ref[0:N_META, :] += jnp.concatenate(dks, axis=1)
            dv_ref[0:N_META, :] += jnp.concatenate(dvs, axis=1)

        q0, k0 = _step_rows(r, n_rows)
        qb = q_ref[pl.ds(q0, QB), :]
        kb = k_ref[pl.ds(k0, KB), :]
        vb = v_ref[pl.ds(k0, KB), :]
        dob = do_ref[pl.ds(q0, QB), :].astype(BF16)
        dqs, dks, dvs, dkms, dvms, dss = [], [], [], [], [], []
        for hh in range(2):
            sl = slice(hh * HEAD_DIM, (hh + 1) * HEAD_DIM)
            qh, kh, vh, kmh, vmh, doh = qb[:, sl], kb[:, sl], vb[:, sl], km[:, sl], vm[:, sl], dob[:, sl]
            p, pm = _attn_probs(qh, kh, kmh, b_ref[0, hh], scale)
            dp = _dg(doh, vh, NT)
            dpm = _dg(doh, vmh, NT)
            delta = jnp.sum(dp * p, axis=-1, keepdims=True) + jnp.sum(dpm * pm, axis=-1, keepdims=True)
            ds = p * (dp - delta)
            dsb = ds.astype(BF16)
            dsmb = (pm * (dpm - delta)).astype(BF16)
            dss.append(ds)
            dvs.append(_dg(p.astype(BF16), doh, TN))
            dvms.append(_dg(pm.astype(BF16), doh, TN))
            dqs.append((_dot(dsb, kh) + _dot(dsmb, kmh)) * scale)
            dks.append(_dg(dsb, qh, TN) * scale)
            dkms.append(_dg(dsmb, qh, TN) * scale)
        dq_ref[pl.ds(q0, QB), :] = jnp.concatenate(dqs, axis=1)
        dk_ref[pl.ds(k0, KB), :] += jnp.concatenate(dks, axis=1)
        dv_ref[pl.ds(k0, KB), :] += jnp.concatenate(dvs, axis=1)
        dk_ref[0:N_META, :] += jnp.concatenate(dkms, axis=1)
        dv_ref[0:N_META, :] += jnp.concatenate(dvms, axis=1)

        def add_bias_grad(pat):
            for hh in range(2):
                for i in range(Q_ROWS):
                    for jj in range(K_ROWS):
                        if pat[i][jj] >= 0:
                            dtb_ref[hh, pat[i][jj]] += dss[hh][i * GRID_W:(i + 1) * GRID_W,
                                                               jj * GRID_W:(jj + 1) * GRID_W]

        @pl.when(r == 0)
        def _():
            add_bias_grad(pats[0])

        @pl.when((r > 0) & (r < n_steps - 1))
        def _():
            add_bias_grad(pats[1])

        @pl.when(r == n_steps - 1)
        def _():
            add_bias_grad(pats[2])

    def bias_map(hp, r):
        return (jnp.where(r == 0, 0, jnp.where(r == n_steps - 1, 2, 1)), hp, 0, 0)

    col = pl.BlockSpec((tp, 2 * HEAD_DIM), lambda hp, r: (0, hp))
    n_dr = 2 * KH - 1
    return pl.pallas_call(
        body, name="attn_bwd", grid=(N_HEADS // 2, n_steps),
        in_specs=[col, col, col, pl.BlockSpec((1, 2, QB, KB), bias_map), col],
        out_specs=[col, col, col, pl.BlockSpec((2, n_dr, GRID_W, GRID_W), lambda hp, r: (hp, 0, 0, 0))],
        out_shape=[jax.ShapeDtypeStruct((tp, NA_W), F32)] * 3 +
                  [jax.ShapeDtypeStruct((N_HEADS, n_dr, GRID_W, GRID_W), F32)],
        compiler_params=_cp(("arbitrary", "arbitrary"), 48),
    )(q, k, v, bias, do)


def _expand_onehot():
    ex = np.zeros((S5_P, S5_P * S5_H), np.float32)
    for p in range(S5_P):
        ex[p, p * S5_H:(p + 1) * S5_H] = 1.0
    return ex


def _s5_disc_math(lam_re, lam_im, log_dt, b_re, b_im, ex):
    dt = jnp.exp(log_dt)
    ea = jnp.exp(lam_re * dt)
    a_re = ea * jnp.cos(lam_im * dt)
    a_im = ea * jnp.sin(lam_im * dt)
    den = lam_re * lam_re + lam_im * lam_im
    c_re = ((a_re - 1.0) * lam_re + a_im * lam_im) / den
    c_im = (a_im * lam_re - (a_re - 1.0) * lam_im) / den
    ce_re = jnp.dot(c_re, ex, preferred_element_type=F32, precision=lax.Precision.HIGHEST)
    ce_im = jnp.dot(c_im, ex, preferred_element_type=F32, precision=lax.Precision.HIGHEST)
    return a_re, a_im, ce_re * b_re - ce_im * b_im, ce_re * b_im + ce_im * b_re


def _s5_disc(lam_re, lam_im, log_dt, b_re, b_im):
    ex = jnp.asarray(_expand_onehot())
    n = lam_re.shape[0]

    def body(lr, li, ld, br, bi, ex_ref, ar, ai, bbr, bbi):
        ar[...], ai[...], bbr[...], bbi[...] = _s5_disc_math(lr[...], li[...], ld[...], br[...], bi[...], ex_ref[...])

    vm = pl.BlockSpec(memory_space=pltpu.VMEM)
    return pl.pallas_call(
        body, name="s5_disc", in_specs=[vm] * 6, out_specs=[vm] * 4,
        out_shape=[jax.ShapeDtypeStruct((n, S5_P), F32)] * 2 + [jax.ShapeDtypeStruct((n, S5_P * S5_H), F32)] * 2,
    )(lam_re, lam_im, log_dt, b_re, b_im, ex)


def _s5_disc_bwd(lam_re, lam_im, log_dt, b_re, b_im, da_re, da_im, dbb_re, dbb_im):
    ex = jnp.asarray(_expand_onehot())
    n = lam_re.shape[0]

    def body(lr, li, ld, br, bi, ex_ref, dar, dai, dbr, dbi, o_lr, o_li, o_ld, o_br, o_bi):
        e = ex_ref[...]
        _, vjp = jax.vjp(lambda a, b, c, d, f: _s5_disc_math(a, b, c, d, f, e), lr[...], li[...], ld[...], br[...], bi[...])
        o_lr[...], o_li[...], o_ld[...], o_br[...], o_bi[...] = vjp((dar[...], dai[...], dbr[...], dbi[...]))

    vm = pl.BlockSpec(memory_space=pltpu.VMEM)
    return pl.pallas_call(
        body, name="s5_disc_bwd", in_specs=[vm] * 10, out_specs=[vm] * 5,
        out_shape=[jax.ShapeDtypeStruct((n, S5_P), F32)] * 2 + [jax.ShapeDtypeStruct((n, 1), F32)] +
                  [jax.ShapeDtypeStruct((n, S5_P * S5_H), F32)] * 2,
    )(lam_re, lam_im, log_dt, b_re, b_im, ex, da_re, da_im, dbb_re, dbb_im)


def _s5_matrices(a_re, a_im, bb_re, bb_im, c_re, c_im):
    gl = S5_G // N_BUNDLE
    eye = jnp.eye(gl, dtype=F32)
    half = gl * S5_P

    def in_mat(bb):
        t = bb.reshape(2, N_BUNDLE, gl, S5_P, S5_H).transpose(0, 1, 4, 2, 3)
        m = t[:, :, None] * eye[None, None, :, None, :, None]
        return m.reshape(2, N_BUNDLE, gl * S5_H, half)

    def out_mat(c):
        t = c.reshape(2, N_BUNDLE, gl, S5_H, S5_P).transpose(0, 1, 2, 4, 3)
        m = t[:, :, :, :, None, :] * eye[None, None, :, None, :, None]
        return m.reshape(2, N_BUNDLE, half, gl * S5_H)

    a = jnp.concatenate([a_re.reshape(2, N_BUNDLE, 1, half), a_im.reshape(2, N_BUNDLE, 1, half)], axis=-1)
    bm = jnp.concatenate([in_mat(bb_re), in_mat(bb_im)], axis=-1)
    cm = jnp.concatenate([out_mat(c_re), -out_mat(c_im)], axis=-2)
    return a, bm, cm


def _scan_chunks(length):
    return [(t0, min(SCAN_CHUNK, length - t0)) for t0 in range(0, length, SCAN_CHUNK)]


def _scan(src_ref, dst_ref, prev_ref, prev_off, n_rows, a_re, a_im, carry, reverse):
    half = a_re.shape[-1]
    n_blk = n_rows // 8
    rid = lax.broadcasted_iota(jnp.int32, (8, half), 0)

    def blk(i, carry):
        xr, xi = carry
        bi = (n_blk - 1 - i) if reverse else i
        off = pl.multiple_of(bi * 8, 8)
        v = src_ref[pl.ds(off, 8), :]
        o_r = jnp.zeros((8, half), F32)
        o_i = jnp.zeros((8, half), F32)
        p_r = jnp.zeros((8, half), F32)
        p_i = jnp.zeros((8, half), F32)
        for j in (range(7, -1, -1) if reverse else range(8)):
            if prev_ref is not None:
                p_r = jnp.where(rid == j, xr, p_r)
                p_i = jnp.where(rid == j, xi, p_i)
            nr = a_re * xr - a_im * xi + v[j:j + 1, :half]
            ni = a_re * xi + a_im * xr + v[j:j + 1, half:]
            xr, xi = nr, ni
            if dst_ref is not None:
                o_r = jnp.where(rid == j, xr, o_r)
                o_i = jnp.where(rid == j, xi, o_i)
        if dst_ref is not None:
            dst_ref[pl.ds(off, 8), :] = jnp.concatenate([o_r, o_i], axis=1)
        if prev_ref is not None:
            prev_ref[pl.ds(pl.multiple_of(prev_off + off, 8), 8), :] = jnp.concatenate([p_r, p_i], axis=1)
        return xr, xi

    return lax.fori_loop(0, n_blk, blk, carry)


def _s5_fwd(u, d_skip, a, bm, cm, length):
    tp = u.shape[0]
    cw = S5_W // N_BUNDLE
    sw = a.shape[-1]
    half = sw // 2
    chunks = _scan_chunks(length)

    def body(u_ref, d_ref, a_ref, bm_ref, cm_ref, y_ref, bu_s, xs_s):
        y_ref[...] = u_ref[...] * d_ref[...]
        for dr in range(2):
            a_re = a_ref[dr, 0, :, 0:half]
            a_im = a_ref[dr, 0, :, half:]
            carry = (jnp.zeros((1, half), F32), jnp.zeros((1, half), F32))
            for t0, n in (chunks if dr == 0 else chunks[::-1]):
                bu_s[0:n, :] = _dot(u_ref[t0:t0 + n, :].astype(BF16), bm_ref[dr, 0])
                carry = _scan(bu_s, xs_s, None, 0, n, a_re, a_im, carry, dr == 1)
                y_ref[t0:t0 + n, :] += _dot(xs_s[0:n, :].astype(BF16), cm_ref[dr, 0])

    return pl.pallas_call(
        body, name="s5_fwd", grid=(N_BUNDLE,),
        in_specs=[pl.BlockSpec((tp, cw), lambda b: (0, b)), pl.BlockSpec((1, cw), lambda b: (0, b)),
                  pl.BlockSpec((2, 1, 1, sw), lambda b: (0, b, 0, 0)),
                  pl.BlockSpec((2, 1, cw, sw), lambda b: (0, b, 0, 0)),
                  pl.BlockSpec((2, 1, sw, cw), lambda b: (0, b, 0, 0))],
        out_specs=pl.BlockSpec((tp, cw), lambda b: (0, b)),
        out_shape=jax.ShapeDtypeStruct((tp, S5_W), F32),
        scratch_shapes=[pltpu.VMEM((SCAN_CHUNK, sw), F32), pltpu.VMEM((SCAN_CHUNK, sw), F32)],
        compiler_params=_cp(("arbitrary",), 40),
    )(u, d_skip, a, bm, cm)


def _s5_bwd(u, dy, d_skip, a, bm, cm, length):
    tp = u.shape[0]
    cw = S5_W // N_BUNDLE
    sw = a.shape[-1]
    half = sw // 2
    chunks = _scan_chunks(length)

    def body(u_ref, dy_ref, d_ref, a_ref, bm_ref, cm_ref, du_ref, dd_ref, dbm_ref, dcm_ref, da_ref, bu_s, g_s, xp_s):
        du_ref[...] = dy_ref[...] * d_ref[...]
        dd_ref[...] = jnp.sum(dy_ref[...] * u_ref[...], axis=0, keepdims=True)
        dbm_ref[...] = jnp.zeros_like(dbm_ref)
        dcm_ref[...] = jnp.zeros_like(dcm_ref)
        zero = (jnp.zeros((1, half), F32), jnp.zeros((1, half), F32))
        for dr in range(2):
            a_re = a_ref[dr, 0, :, 0:half]
            a_im = a_ref[dr, 0, :, half:]
            seq = chunks if dr == 0 else chunks[::-1]
            carry = zero
            for t0, n in seq:
                bu_s[0:n, :] = _dot(u_ref[t0:t0 + n, :].astype(BF16), bm_ref[dr, 0])
                carry = _scan(bu_s, None, xp_s, t0, n, a_re, a_im, carry, dr == 1)
            carry = zero
            da_r = jnp.zeros((1, half), F32)
            da_i = jnp.zeros((1, half), F32)
            for t0, n in seq[::-1]:
                ub = u_ref[t0:t0 + n, :].astype(BF16)
                dyb = dy_ref[t0:t0 + n, :].astype(BF16)
                bu_s[0:n, :] = _dg(dyb, cm_ref[dr, 0], NT)
                carry = _scan(bu_s, g_s, None, 0, n, a_re, -a_im, carry, dr == 0)
                g = g_s[0:n, :]
                gb = g.astype(BF16)
                du_ref[t0:t0 + n, :] += _dg(gb, bm_ref[dr, 0], NT)
                dbm_ref[dr, 0] += _dg(ub, gb, TN)
                xp = xp_s[t0:t0 + n, :]
                xp_r, xp_i = xp[:, 0:half], xp[:, half:]
                g_r, g_i = g[:, 0:half], g[:, half:]
                bu = _dot(ub, bm_ref[dr, 0])
                x_r = a_re * xp_r - a_im * xp_i + bu[:, 0:half]
                x_i = a_re * xp_i + a_im * xp_r + bu[:, half:]
                dcm_ref[dr, 0] += _dg(jnp.concatenate([x_r, x_i], axis=1).astype(BF16), dyb, TN)
                da_r = da_r + jnp.sum(g_r * xp_r + g_i * xp_i, axis=0, keepdims=True)
                da_i = da_i + jnp.sum(g_i * xp_r - g_r * xp_i, axis=0, keepdims=True)
            da_ref[dr, 0] = jnp.concatenate([da_r, da_i], axis=1)

    lp = -(-length // 8) * 8
    return pl.pallas_call(
        body, name="s5_bwd", grid=(N_BUNDLE,),
        in_specs=[pl.BlockSpec((tp, cw), lambda b: (0, b)), pl.BlockSpec((tp, cw), lambda b: (0, b)),
                  pl.BlockSpec((1, cw), lambda b: (0, b)),
                  pl.BlockSpec((2, 1, 1, sw), lambda b: (0, b, 0, 0)),
                  pl.BlockSpec((2, 1, cw, sw), lambda b: (0, b, 0, 0)),
                  pl.BlockSpec((2, 1, sw, cw), lambda b: (0, b, 0, 0))],
        out_specs=[pl.BlockSpec((tp, cw), lambda b: (0, b)), pl.BlockSpec((1, cw), lambda b: (0, b)),
                   pl.BlockSpec((2, 1, cw, sw), lambda b: (0, b, 0, 0)),
                   pl.BlockSpec((2, 1, sw, cw), lambda b: (0, b, 0, 0)),
                   pl.BlockSpec((2, 1, 1, sw), lambda b: (0, b, 0, 0))],
        out_shape=[jax.ShapeDtypeStruct((tp, S5_W), F32), jax.ShapeDtypeStruct((1, S5_W), F32),
                   jax.ShapeDtypeStruct((2, N_BUNDLE, cw, sw), F32), jax.ShapeDtypeStruct((2, N_BUNDLE, sw, cw), F32),
                   jax.ShapeDtypeStruct((2, N_BUNDLE, 1, sw), F32)],
        scratch_shapes=[pltpu.VMEM((SCAN_CHUNK, sw), F32), pltpu.VMEM((SCAN_CHUNK, sw), F32),
                        pltpu.VMEM((lp, sw), F32)],
        compiler_params=_cp(("arbitrary",), 48),
    )(u, dy, d_skip, a, bm, cm)


def _row_tile(tp):
    return max(tm for tm in range(16, 449, 16) if tp % tm == 0)


def _local_step(x, target, meta, gains, w, s5, rpb):
    n_tok = x.shape[0]
    length = N_META + n_tok
    tp = length + 16
    tm = _row_tile(tp)
    tmb = tm // 2
    n_rows = n_tok // GRID_W
    pad = jnp.zeros((tp - length, D), F32)
    h0 = jnp.concatenate([meta, x, pad], axis=0)
    tgt = jnp.concatenate([jnp.zeros((N_META, D), F32), target, pad], axis=0)

    n2 = 2 * S5_G
    lam_re = s5["lam_re"].reshape(n2, S5_P)
    lam_im = s5["lam_im"].reshape(n2, S5_P)
    log_dt = s5["log_dt"].reshape(n2, 1)
    b_re = s5["b_re"].reshape(n2, S5_P * S5_H)
    b_im = s5["b_im"].reshape(n2, S5_P * S5_H)
    a_re, a_im, bb_re, bb_im = _s5_disc(lam_re, lam_im, log_dt, b_re, b_im)

    def mats(a_re, a_im, bb_re, bb_im, c_re, c_im):
        return _s5_matrices(a_re.reshape(2, S5_G, S5_P), a_im.reshape(2, S5_G, S5_P),
                            bb_re.reshape(2, S5_G, S5_P * S5_H), bb_im.reshape(2, S5_G, S5_P * S5_H), c_re, c_im)

    (a_m, bm, cm), mats_vjp = jax.vjp(mats, a_re, a_im, bb_re, bb_im, s5["c_re"], s5["c_im"])
    bm16 = bm.astype(BF16)
    cm16 = cm.astype(BF16)
    bias = _bias_tables(rpb, n_rows)

    h1, gate1, up1, f1 = _ffn_fwd("ffn1_fwd", h0, gains["ffn1_pre_g"], gains["ffn1_post_g"],
                                  w["ffn1_w_gate"], w["ffn1_w_up"], w["ffn1_w_down"], tm)
    q, k, v, u = _mix_in(h1, gains["mix_pre_g"], w["w_in"], tm)
    o_na = _attn_fwd(q, k, v, bias, n_tok)
    y_pre = _s5_fwd(u, gains["s5_d"], a_m, bm16, cm16, length)
    w_glu = w["s5_w_glu"].reshape(S5_W, S5_W)
    w_out = w["w_out"].reshape(D, D)
    h2, mix = _mix_out(o_na, y_pre, h1, w_glu, gains["s5_b_glu"], gains["na_out_g"], gains["s5_out_g"], w_out,
                       gains["mix_post_g"], tm)
    h3, gate2, up2, f2 = _ffn_fwd("ffn2_fwd", h2, gains["ffn2_pre_g"], gains["ffn2_post_g"],
                                  w["ffn2_w_gate"], w["ffn2_w_up"], w["ffn2_w_down"], tm)
    dh3, df2, loss, dg_final, dg_post2 = _final_loss(h3, gains["final_g"], tgt, f2, gains["ffn2_post_g"], n_tok, tm)

    dwg2, dwu2, dwd2, dxn2 = _ffn_bwd("ffn2_bwd", h2, gains["ffn2_pre_g"], df2, gate2, up2,
                                      w["ffn2_w_gate"], w["ffn2_w_up"], w["ffn2_w_down"], tmb)
    dh2, dg_pre2 = _ffn_pre_bwd("ffn2_pre_bwd", dh3, dxn2, h2, gains["ffn2_pre_g"], tm)
    do_na, dy_pre, dw_out, dw_glu, dg_mpost, dg_na, dg_s5, db_glu = _mix_out_bwd(
        dh2, mix, o_na, y_pre, w_glu, gains["s5_b_glu"], gains["na_out_g"], gains["s5_out_g"], w_out,
        gains["mix_post_g"], tm)
    dq, dk, dv, dtb = _attn_bwd(q, k, v, bias, do_na, n_tok)
    du, dd, dbm, dcm, da_m = _s5_bwd(u, dy_pre, gains["s5_d"], a_m, bm16, cm16, length)
    dh1, df1, dw_in, dg_mpre, dg_post1 = _mix_in_bwd(dq, dk, dv, du, h1, gains["mix_pre_g"], w["w_in"], dh2, f1,
                                                     gains["ffn1_post_g"], tm)
    dwg1, dwu1, dwd1, dxn1 = _ffn_bwd("ffn1_bwd", h0, gains["ffn1_pre_g"], df1, gate1, up1,
                                      w["ffn1_w_gate"], w["ffn1_w_up"], w["ffn1_w_down"], tmb)
    dh0, dg_pre1 = _ffn_pre_bwd("ffn1_pre_bwd", dh1, dxn1, h0, gains["ffn1_pre_g"], tm)

    e, _ = _diag_onehot()
    n_dr = 2 * KH - 1
    drpb = _rpb_collapse(dtb.reshape(N_HEADS * n_dr, GRID_W * GRID_W), jnp.asarray(e.T))
    drpb = drpb[:, :2 * KW - 1].reshape(N_HEADS, n_dr, 2 * KW - 1)
    da_re, da_im, dbb_re, dbb_im, dc_re, dc_im = mats_vjp((da_m, dbm, dcm))
    dlam_re, dlam_im, dlog_dt, db_re, db_im = _s5_disc_bwd(lam_re, lam_im, log_dt, b_re, b_im,
                                                            da_re, da_im, dbb_re, dbb_im)

    big = {"ffn1_w_gate": dwg1, "ffn1_w_up": dwu1, "ffn1_w_down": dwd1, "w_in": dw_in,
           "s5_w_glu": dw_glu.reshape(N_CHIP, S5_W // N_CHIP, S5_W), "w_out": dw_out.reshape(N_CHIP, D // N_CHIP, D),
           "ffn2_w_gate": dwg2, "ffn2_w_up": dwu2, "ffn2_w_down": dwd2}
    small = {"ffn1_pre_g": dg_pre1, "ffn1_post_g": dg_post1, "mix_pre_g": dg_mpre, "na_rpb": drpb,
             "s5_lam_re": dlam_re, "s5_lam_im": dlam_im, "s5_log_dt": dlog_dt, "s5_b_re": db_re, "s5_b_im": db_im,
             "s5_c_re": dc_re, "s5_c_im": dc_im, "s5_d": dd, "s5_b_glu": db_glu, "na_out_g": dg_na,
             "s5_out_g": dg_s5, "mix_post_g": dg_mpost, "ffn2_pre_g": dg_pre2, "ffn2_post_g": dg_post2,
             "final_g": dg_final}
    return loss[0, 0], dh0, big, small


def _mesh_pos():
    return lax.axis_index("x"), lax.axis_index("y"), lax.axis_index("c")


def _other_chips(x, y):
    return [(1 - x, y), (x, 1 - y), (1 - x, 1 - y)]


def _all_gather(halves):
    n = len(halves)

    def body(*refs):
        ins, outs = refs[:n], refs[n:2 * n]
        send_sems, recv_sems, local_sems = refs[2 * n:]
        x, y, c = _mesh_pos()
        me, sibling = (x, y, c), (x, y, 1 - c)
        chips = _other_chips(x, y)

        def rows(a, px, py, pc):
            rh = ins[a].shape[0]
            return outs[a].at[2 * px + py, pl.ds(pc * rh, rh), :]

        def copy(a, kind, block, to, src=None):
            return pltpu.make_async_remote_copy(
                src_ref=rows(a, *block) if src is None else src, dst_ref=rows(a, *block),
                send_sem=send_sems.at[a * 7 + kind], recv_sem=recv_sems.at[a * 7 + kind],
                device_id=to, device_id_type=MESH_ID)

        mine = [pltpu.make_async_copy(ins[a], rows(a, *me), local_sems.at[a]) for a in range(n)]
        for cp in mine:
            cp.start()
        first = []
        for a in range(n):
            for j, chip in enumerate(chips):
                first.append(copy(a, 1 + j, me, (*chip, c), src=ins[a]))
            first.append(copy(a, 0, me, sibling, src=ins[a]))
        for cp in first:
            cp.start()
        passed = []
        for a in range(n):
            for j, chip in enumerate(chips):
                copy(a, 1 + j, (*chip, c), me).wait_recv()
                cp = copy(a, 4 + j, (*chip, c), sibling)
                cp.start()
                passed.append(cp)
        for a in range(n):
            copy(a, 0, sibling, me).wait_recv()
            for j, chip in enumerate(chips):
                copy(a, 4 + j, (*chip, 1 - c), me).wait_recv()
        for cp in first + passed:
            cp.wait_send()
        for cp in mine:
            cp.wait()

    return pl.pallas_call(
        body, name="weight_all_gather",
        out_shape=[jax.ShapeDtypeStruct((N_CHIP, 2 * h.shape[0], h.shape[1]), h.dtype) for h in halves],
        in_specs=[ANY] * n, out_specs=[ANY] * n,
        scratch_shapes=[pltpu.SemaphoreType.DMA((7 * n,)), pltpu.SemaphoreType.DMA((7 * n,)),
                        pltpu.SemaphoreType.DMA((n,))],
    )(*halves)


def _pair_exchange(grads):
    n = len(grads)

    def body(*refs):
        ins, outs = refs[:n], refs[n:2 * n]
        send_sems, recv_sems = refs[2 * n:]
        x, y, c = _mesh_pos()
        cps = []
        for a in range(n):
            rh = ins[a].shape[1] // 2
            cp = pltpu.make_async_remote_copy(
                src_ref=ins[a].at[:, pl.ds((1 - c) * rh, rh), :], dst_ref=outs[a],
                send_sem=send_sems.at[a], recv_sem=recv_sems.at[a], device_id=(x, y, 1 - c), device_id_type=MESH_ID)
            cp.start()
            cps.append(cp)
        for cp in cps:
            cp.wait()

    return pl.pallas_call(
        body, name="grad_pair_exchange",
        out_shape=[jax.ShapeDtypeStruct((N_CHIP, g.shape[1] // 2, g.shape[2]), g.dtype) for g in grads],
        in_specs=[ANY] * n, out_specs=[ANY] * n,
        scratch_shapes=[pltpu.SemaphoreType.DMA((n,)), pltpu.SemaphoreType.DMA((n,))],
    )(*grads)


def _chip_sum(name, g, recv, c_arr):
    _, r, cc = g.shape
    rh = r // 2

    def body(c_ref, g_ref, r_ref, o_ref):
        o_ref[...] = (g_ref[...] + r_ref[...]).astype(BF16)

    return pl.pallas_call(
        body, name=name, out_shape=jax.ShapeDtypeStruct((N_CHIP, rh, cc), BF16),
        grid_spec=pltpu.PrefetchScalarGridSpec(
            num_scalar_prefetch=1, grid=(N_CHIP,),
            in_specs=[pl.BlockSpec((1, rh, cc), lambda j, c_ref: (j, c_ref[0], 0)),
                      pl.BlockSpec((1, rh, cc), lambda j, c_ref: (j, 0, 0))],
            out_specs=pl.BlockSpec((1, rh, cc), lambda j, c_ref: (j, 0, 0))),
        compiler_params=_cp(("arbitrary",), 32),
    )(c_arr, g, recv)


def _chip_scatter(sums):
    n = len(sums)

    def body(*refs):
        ins, outs = refs[:n], refs[n:2 * n]
        send_sems, recv_sems = refs[2 * n:]
        x, y, c = _mesh_pos()
        cps = []
        for a in range(n):
            for j, chip in enumerate(_other_chips(x, y)):
                cp = pltpu.make_async_remote_copy(
                    src_ref=ins[a].at[2 * chip[0] + chip[1]], dst_ref=outs[a].at[j],
                    send_sem=send_sems.at[3 * a + j], recv_sem=recv_sems.at[3 * a + j],
                    device_id=(*chip, c), device_id_type=MESH_ID)
                cp.start()
                cps.append(cp)
        for cp in cps:
            cp.wait()

    return pl.pallas_call(
        body, name="grad_chip_scatter",
        out_shape=[jax.ShapeDtypeStruct((3,) + s.shape[1:], s.dtype) for s in sums],
        in_specs=[ANY] * n, out_specs=[ANY] * n,
        scratch_shapes=[pltpu.SemaphoreType.DMA((3 * n,)), pltpu.SemaphoreType.DMA((3 * n,))],
    )(*sums)


def _total_sum(name, sums, recv3, k_arr):
    _, rh, cc = sums.shape

    def body(k_ref, s_ref, r_ref, o_ref):
        t = s_ref[0].astype(F32) + r_ref[0].astype(F32)
        t = t + r_ref[1].astype(F32)
        o_ref[...] = t + r_ref[2].astype(F32)

    return pl.pallas_call(
        body, name=name, out_shape=jax.ShapeDtypeStruct((rh, cc), F32),
        grid_spec=pltpu.PrefetchScalarGridSpec(
            num_scalar_prefetch=1, grid=(1,),
            in_specs=[pl.BlockSpec((1, rh, cc), lambda i, k_ref: (k_ref[0], 0, 0)),
                      pl.BlockSpec((3, rh, cc), lambda i, k_ref: (0, 0, 0))],
            out_specs=pl.BlockSpec((rh, cc), lambda i, k_ref: (0, 0))),
        compiler_params=_cp(("arbitrary",), 32),
    )(k_arr, sums, recv3)


def _pair_assemble(totals):
    n = len(totals)

    def body(*refs):
        ins, outs = refs[:n], refs[n:2 * n]
        send_sems, recv_sems, local_sems = refs[2 * n:]
        x, y, c = _mesh_pos()
        cps, mine = [], []
        for a in range(n):
            rh = ins[a].shape[0]
            here = outs[a].at[pl.ds(c * rh, rh), :]
            cp = pltpu.make_async_remote_copy(
                src_ref=ins[a], dst_ref=here, send_sem=send_sems.at[a], recv_sem=recv_sems.at[a],
                device_id=(x, y, 1 - c), device_id_type=MESH_ID)
            cp.start()
            cps.append(cp)
            lc = pltpu.make_async_copy(ins[a], here, local_sems.at[a])
            lc.start()
            mine.append(lc)
        for cp in cps:
            cp.wait()
        for lc in mine:
            lc.wait()

    return pl.pallas_call(
        body, name="grad_pair_assemble",
        out_shape=[jax.ShapeDtypeStruct((2 * t.shape[0], t.shape[1]), t.dtype) for t in totals],
        in_specs=[ANY] * n, out_specs=[ANY] * n,
        scratch_shapes=[pltpu.SemaphoreType.DMA((n,)), pltpu.SemaphoreType.DMA((n,)), pltpu.SemaphoreType.DMA((n,))],
    )(*totals)


def _small_allreduce(buf):
    shape = buf.shape

    def body(in_ref, out_ref, sib_s, csum_s, all_s, send_sems, recv_sems):
        x, y, c = _mesh_pos()
        k = 2 * x + y
        cp = pltpu.make_async_remote_copy(src_ref=in_ref, dst_ref=sib_s, send_sem=send_sems.at[0],
                                          recv_sem=recv_sems.at[0], device_id=(x, y, 1 - c), device_id_type=MESH_ID)
        cp.start()
        cp.wait()
        csum_s[...] = in_ref[...] + sib_s[...]
        all_s[k] = csum_s[...]
        cps = []
        for j, chip in enumerate(_other_chips(x, y)):
            cp = pltpu.make_async_remote_copy(src_ref=csum_s, dst_ref=all_s.at[k], send_sem=send_sems.at[1 + j],
                                              recv_sem=recv_sems.at[1 + j], device_id=(*chip, c),
                                              device_id_type=MESH_ID)
            cp.start()
            cps.append(cp)
        for cp in cps:
            cp.wait()
        out_ref[...] = ((all_s[0] + all_s[1]) + all_s[2]) + all_s[3]

    vm = pl.BlockSpec(memory_space=pltpu.VMEM)
    return pl.pallas_call(
        body, name="small_allreduce", out_shape=jax.ShapeDtypeStruct(shape, F32), in_specs=[vm], out_specs=vm,
        scratch_shapes=[pltpu.VMEM(shape, F32), pltpu.VMEM(shape, F32), pltpu.VMEM((N_CHIP,) + shape, F32),
                        pltpu.SemaphoreType.DMA((4,)), pltpu.SemaphoreType.DMA((4,))],
        compiler_params=_cp(None, 32),
    )(buf)


def _adamw_math(w, g, m, v):
    m = ADAM_B1 * m + (1.0 - ADAM_B1) * g
    v = ADAM_B2 * v + (1.0 - ADAM_B2) * (g * g)
    m_hat = m / (1.0 - ADAM_B1 ** ADAM_STEP)
    v_hat = v / (1.0 - ADAM_B2 ** ADAM_STEP)
    delta = -ADAM_LR * (m_hat / (jnp.sqrt(v_hat) + ADAM_EPS) + ADAM_WD * w)
    return delta, m, v


def _adamw(name, w, g, m, v):
    r, c = w.shape
    tr = max(t for t in range(8, 513, 8) if r % t == 0)

    def body(w_ref, g_ref, m_ref, v_ref, d_ref, mo_ref, vo_ref):
        d_ref[...], mo_ref[...], vo_ref[...] = _adamw_math(w_ref[...], g_ref[...], m_ref[...], v_ref[...])

    return pl.pallas_call(
        body, name=name, grid=(r // tr,), in_specs=[_rows(tr, c)] * 4, out_specs=[_rows(tr, c)] * 3,
        out_shape=[jax.ShapeDtypeStruct((r, c), F32)] * 3, compiler_params=_cp(("arbitrary",), 32),
    )(w, g, m, v)


def _pack(arrays):
    flat = jnp.concatenate([a.reshape(-1) for a in arrays])
    n = flat.shape[0]
    rows = -(-n // (8 * 128)) * 8
    return jnp.pad(flat, (0, rows * 128 - n)).reshape(rows, 128)


def _unpack(buf, shapes):
    flat = buf.reshape(-1)
    out, off = [], 0
    for s in shapes:
        n = int(np.prod(s))
        out.append(flat[off:off + n].reshape(s))
        off += n
    return out


WEIGHTS = ["meta_tokens", "ffn1_pre_g", "ffn1_post_g", "ffn1_w_gate", "ffn1_w_up", "ffn1_w_down", "mix_pre_g", "w_in",
           "na_rpb", "s5_lam_re", "s5_lam_im", "s5_log_dt", "s5_b_re", "s5_b_im", "s5_c_re", "s5_c_im", "s5_d",
           "s5_w_glu", "s5_b_glu", "na_out_g", "s5_out_g", "w_out", "mix_post_g", "ffn2_pre_g", "ffn2_post_g",
           "ffn2_w_gate", "ffn2_w_up", "ffn2_w_down", "final_g"]
BIG = ["ffn1_w_gate", "ffn1_w_up", "ffn1_w_down", "w_in", "s5_w_glu", "w_out", "ffn2_w_gate", "ffn2_w_up",
       "ffn2_w_down"]
GAINS = ["ffn1_pre_g", "ffn1_post_g", "mix_pre_g", "s5_d", "s5_b_glu", "na_out_g", "s5_out_g", "mix_post_g",
         "ffn2_pre_g", "ffn2_post_g", "final_g"]
SMALL = [n for n in WEIGHTS if n not in BIG]


def kernel(*args):
    names = ["x"] + WEIGHTS + ["loss_target"] + ["m_" + n for n in WEIGHTS] + ["v_" + n for n in WEIGHTS]
    assert len(args) == len(names)
    given = dict(zip(names, args))
    x_pos, y_pos, c_pos = _mesh_pos()
    k_pos = 2 * x_pos + y_pos
    c_arr = jnp.reshape(c_pos, (1,)).astype(jnp.int32)
    k_arr = jnp.reshape(k_pos, (1,)).astype(jnp.int32)

    def my_half(a):
        rh = a.shape[0] // 2
        return lax.dynamic_slice_in_dim(a, c_pos * rh, rh, 0)

    halves = [my_half(given[n][0]).astype(BF16) for n in BIG] + [my_half(given["meta_tokens"])]
    gathered = _all_gather(halves)
    w = dict(zip(BIG, gathered[:-1]))
    meta = gathered[-1].transpose(1, 0, 2).reshape(N_META, D)

    gains = {n: given[n] for n in GAINS}
    s5 = {n: given["s5_" + n][0] for n in ["lam_re", "lam_im", "log_dt", "b_re", "b_im", "c_re", "c_im"]}
    loss, dh0, big, small = _local_step(given["x"][0], given["loss_target"][0], meta, gains, w, s5,
                                        given["na_rpb"][0])
    loss = lax.psum(loss, ("x", "y", "c"))
    n_tok = given["x"].shape[1]
    grad_x = dh0[N_META:N_META + n_tok][None]

    grads = [big[n] for n in BIG]
    recv = _pair_exchange(grads)
    sums = [_chip_sum("chip_sum_" + n, g, r, c_arr) for n, g, r in zip(BIG, grads, recv)]
    recv3 = _chip_scatter(sums)
    totals = [_total_sum("total_sum_" + n, s, r, k_arr) for n, s, r in zip(BIG, sums, recv3)]
    pieces = dict(zip(BIG, _pair_assemble(totals)))

    small["meta_tokens"] = dh0[:N_META]
    red = _unpack(_small_allreduce(_pack([small[n] for n in SMALL])), [small[n].shape for n in SMALL])
    small = dict(zip(SMALL, red))
    mc = D // N_CHIP
    small["meta_tokens"] = lax.dynamic_slice_in_dim(small["meta_tokens"], k_pos * mc, mc, 1)

    out_g, out_d, out_m, out_v = {}, {}, {}, {}
    for n in BIG:
        shape = given[n].shape
        g2 = pieces[n]
        d2, m2, v2 = _adamw("adamw_" + n, given[n].reshape(g2.shape), g2, given["m_" + n].reshape(g2.shape),
                            given["v_" + n].reshape(g2.shape))
        out_g[n], out_d[n], out_m[n], out_v[n] = (t.reshape(shape) for t in (g2, d2, m2, v2))
    shapes = [given[n].shape for n in SMALL]
    gs = [small[n].reshape(given[n].shape) for n in SMALL]
    d2, m2, v2 = _adamw("adamw_small", _pack([given[n] for n in SMALL]), _pack(gs),
                        _pack([given["m_" + n] for n in SMALL]), _pack([given["v_" + n] for n in SMALL]))
    for n, g, dd, mm, vv in zip(SMALL, gs, _unpack(d2, shapes), _unpack(m2, shapes), _unpack(v2, shapes)):
        out_g[n], out_d[n], out_m[n], out_v[n] = g, dd, mm, vv
    return (loss, grad_x, *[out_g[n] for n in WEIGHTS], *[out_d[n] for n in WEIGHTS],
            *[out_m[n] for n in WEIGHTS], *[out_v[n] for n in WEIGHTS])
```

```python
import functools
import math

import numpy as np
import jax
import jax.numpy as jnp
from jax import lax
from jax.experimental import pallas as pl
from jax.experimental.pallas import tpu as pltpu

F32 = jnp.float32
BF16 = jnp.bfloat16

D = 1024
N_META = 16
GRID_W = 64
NA_W = 512
S5_W = 512
HEAD_DIM = 64
N_HEADS = 8
KH = 8
KW = 16
S5_G = 32
S5_P = 64
S5_H = 16
N_BUNDLE = 4
FF = 2816
N_CHIP = 4
FC = FF // N_CHIP
EPS = 1e-6
NEG_INF = -1e30
Q_ROWS = 4
K_ROWS = 12
QB = Q_ROWS * GRID_W
KB = K_ROWS * GRID_W
SCAN_CHUNK = 256

ADAM_LR = 0.001
ADAM_B1 = 0.9
ADAM_B2 = 0.999
ADAM_EPS = 1e-08
ADAM_WD = 0.01
ADAM_STEP = 10

NT = (((1,), (1,)), ((), ()))
TN = (((0,), (0,)), ((), ()))
MESH_ID = pl.DeviceIdType.MESH


def _cp(sem=None, vmem_mb=None):
    kw = {}
    if sem is not None:
        kw["dimension_semantics"] = sem
    if vmem_mb is not None:
        kw["vmem_limit_bytes"] = vmem_mb << 20
    return pltpu.CompilerParams(**kw)


def _full(shape):
    n = len(shape)
    return pl.BlockSpec(shape, lambda *_: (0,) * n)


def _rows(tm, w):
    return pl.BlockSpec((tm, w), lambda i: (i, 0))


ANY = pl.BlockSpec(memory_space=pl.ANY)


def _rms(x, g):
    r = lax.rsqrt(jnp.mean(x * x, axis=-1, keepdims=True) + EPS)
    return x * r * g


def _rms_bwd(x, g, dy):
    r = lax.rsqrt(jnp.mean(x * x, axis=-1, keepdims=True) + EPS)
    xh = x * r
    dg = jnp.sum(dy * xh, axis=0, keepdims=True)
    dyg = dy * g
    dx = r * (dyg - xh * jnp.mean(dyg * xh, axis=-1, keepdims=True))
    return dx, dg


def _dot(a, b):
    return jnp.dot(a, b, preferred_element_type=F32)


def _dg(a, b, dims):
    return lax.dot_general(a, b, dims, preferred_element_type=F32)


def _ffn_fwd(name, h, g_pre, g_post, wg, wu, wd, tm):
    tp = h.shape[0]
    nt = tp // tm

    def body(h_ref, gp_ref, gq_ref, wg_ref, wu_ref, wd_ref, hn_ref, gate_ref, up_ref, f_ref, xn_s, acc_s):
        c = pl.program_id(1)

        @pl.when(c == 0)
        def _():
            xn_s[...] = _rms(h_ref[...], gp_ref[...]).astype(BF16)
            acc_s[...] = jnp.zeros_like(acc_s)

        xn = xn_s[...]
        gate = _dg(xn, wg_ref[0], NT)
        up = _dg(xn, wu_ref[0], NT)
        gate_ref[0] = gate
        up_ref[0] = up
        act = (gate * jax.nn.sigmoid(gate) * up).astype(BF16)
        acc_s[...] += _dot(act, wd_ref[0])

        @pl.when(c == N_CHIP - 1)
        def _():
            f = acc_s[...]
            f_ref[...] = f
            hn_ref[...] = h_ref[...] + 0.5 * _rms(f, gq_ref[...])

    return pl.pallas_call(
        body, name=name, grid=(nt, N_CHIP),
        in_specs=[pl.BlockSpec((tm, D), lambda i, c: (i, 0)), _full((1, D)), _full((1, D))] +
                 [pl.BlockSpec((1, FC, D), lambda i, c: (c, 0, 0))] * 3,
        out_specs=[pl.BlockSpec((tm, D), lambda i, c: (i, 0)),
                   pl.BlockSpec((1, tm, FC), lambda i, c: (c, i, 0)),
                   pl.BlockSpec((1, tm, FC), lambda i, c: (c, i, 0)),
                   pl.BlockSpec((tm, D), lambda i, c: (i, 0))],
        out_shape=[jax.ShapeDtypeStruct((tp, D), F32), jax.ShapeDtypeStruct((N_CHIP, tp, FC), F32),
                   jax.ShapeDtypeStruct((N_CHIP, tp, FC), F32), jax.ShapeDtypeStruct((tp, D), F32)],
        scratch_shapes=[pltpu.VMEM((tm, D), BF16), pltpu.VMEM((tm, D), F32)],
        compiler_params=_cp(("arbitrary", "arbitrary"), 48),
    )(h, g_pre, g_post, wg, wu, wd)


def _ffn_bwd(name, h, g_pre, df, gate, up, wg, wu, wd, tm):
    tp = h.shape[0]
    nt = tp // tm

    def body(h_ref, gp_ref, df_ref, gate_ref, up_ref, wg_ref, wu_ref, wd_ref,
             dwg_ref, dwu_ref, dwd_ref, dxn_ref, ag, au, ad):
        c = pl.program_id(0)
        i = pl.program_id(1)

        @pl.when(i == 0)
        def _():
            ag[...] = jnp.zeros_like(ag)
            au[...] = jnp.zeros_like(au)
            ad[...] = jnp.zeros_like(ad)

        xn = _rms(h_ref[...], gp_ref[...]).astype(BF16)
        dfb = df_ref[...].astype(BF16)
        gt = gate_ref[0]
        u = up_ref[0]
        sg = jax.nn.sigmoid(gt)
        si = gt * sg
        act = (si * u).astype(BF16)
        dact = _dg(dfb, wd_ref[0], NT)
        ad[...] += _dg(act, dfb, TN)
        dgate = (dact * u * (sg * (1.0 + gt * (1.0 - sg)))).astype(BF16)
        dup = (dact * si).astype(BF16)
        ag[...] += _dg(dgate, xn, TN)
        au[...] += _dg(dup, xn, TN)
        dxn_ref[0] = _dot(dgate, wg_ref[0]) + _dot(dup, wu_ref[0])

        @pl.when(i == nt - 1)
        def _():
            pltpu.sync_copy(ag, dwg_ref.at[c])
            pltpu.sync_copy(au, dwu_ref.at[c])
            pltpu.sync_copy(ad, dwd_ref.at[c])

    return pl.pallas_call(
        body, name=name, grid=(N_CHIP, nt),
        in_specs=[pl.BlockSpec((tm, D), lambda c, i: (i, 0)), _full((1, D)),
                  pl.BlockSpec((tm, D), lambda c, i: (i, 0)),
                  pl.BlockSpec((1, tm, FC), lambda c, i: (c, i, 0)),
                  pl.BlockSpec((1, tm, FC), lambda c, i: (c, i, 0))] +
                 [pl.BlockSpec((1, FC, D), lambda c, i: (c, 0, 0))] * 3,
        out_specs=[ANY, ANY, ANY, pl.BlockSpec((1, tm, D), lambda c, i: (c, i, 0))],
        out_shape=[jax.ShapeDtypeStruct((N_CHIP, FC, D), F32)] * 3 + [jax.ShapeDtypeStruct((N_CHIP, tp, D), F32)],
        scratch_shapes=[pltpu.VMEM((FC, D), F32)] * 3,
        compiler_params=_cp(("arbitrary", "arbitrary"), 56),
    )(h, g_pre, df, gate, up, wg, wu, wd)


def _ffn_pre_bwd(name, dh, dxn_part, h, g_pre, tm):
    tp = h.shape[0]
    nt = tp // tm

    def body(dh_ref, dxn_ref, h_ref, gp_ref, out_ref, dg_ref):
        i = pl.program_id(0)
        dxn = (dxn_ref[0] + dxn_ref[1]) + (dxn_ref[2] + dxn_ref[3])
        dx, dg = _rms_bwd(h_ref[...], gp_ref[...], dxn)
        out_ref[...] = dh_ref[...] + dx

        @pl.when(i == 0)
        def _():
            dg_ref[...] = jnp.zeros_like(dg_ref)

        dg_ref[...] += dg

    return pl.pallas_call(
        body, name=name, grid=(nt,),
        in_specs=[_rows(tm, D), pl.BlockSpec((N_CHIP, tm, D), lambda i: (0, i, 0)), _rows(tm, D), _full((1, D))],
        out_specs=[_rows(tm, D), _full((1, D))],
        out_shape=[jax.ShapeDtypeStruct((tp, D), F32), jax.ShapeDtypeStruct((1, D), F32)],
        compiler_params=_cp(("arbitrary",), 48),
    )(dh, dxn_part, h, g_pre)


def _mix_in(h, g, w_in, tm):
    tp = h.shape[0]

    def body(h_ref, g_ref, w_ref, q_ref, k_ref, v_ref, u_ref):
        a = _rms(h_ref[...], g_ref[...]).astype(BF16)
        q_ref[...] = _dot(a, w_ref[0]).astype(BF16)
        k_ref[...] = _dot(a, w_ref[1]).astype(BF16)
        v_ref[...] = _dot(a, w_ref[2]).astype(BF16)
        u_ref[...] = _dot(a, w_ref[3])

    return pl.pallas_call(
        body, name="mix_in", grid=(tp // tm,),
        in_specs=[_rows(tm, D), _full((1, D)), _full((N_CHIP, D, NA_W))],
        out_specs=[_rows(tm, NA_W)] * 4,
        out_shape=[jax.ShapeDtypeStruct((tp, NA_W), BF16)] * 3 + [jax.ShapeDtypeStruct((tp, S5_W), F32)],
        compiler_params=_cp(("arbitrary",), 40),
    )(h, g, w_in)


def _gelu(x):
    return jax.nn.gelu(x, approximate=True)


def _gelu_grad(x):
    k = math.sqrt(2.0 / math.pi)
    t = jnp.tanh(k * (x + 0.044715 * x * x * x))
    return 0.5 * (1.0 + t) + 0.5 * x * (1.0 - t * t) * k * (1.0 + 3.0 * 0.044715 * x * x)


def _mix_out(o_na, y_pre, h, w_glu, b_glu, g_na, g_s5, w_out, g_post, tm):
    tp = h.shape[0]

    def body(ona_ref, yp_ref, h_ref, wglu_ref, bglu_ref, gna_ref, gs5_ref, wout_ref, gpost_ref, hn_ref, mix_ref):
        y = _gelu(yp_ref[...])
        z = _dot(y.astype(BF16), wglu_ref[...]) + bglu_ref[...]
        o_s5 = y * jax.nn.sigmoid(z)
        n1 = _rms(ona_ref[...], gna_ref[...]).astype(BF16)
        n2 = _rms(o_s5, gs5_ref[...]).astype(BF16)
        mix = _dot(n1, wout_ref[0:NA_W, :]) + _dot(n2, wout_ref[NA_W:, :])
        mix_ref[...] = mix
        hn_ref[...] = h_ref[...] + _rms(mix, gpost_ref[...])

    return pl.pallas_call(
        body, name="mix_out", grid=(tp // tm,),
        in_specs=[_rows(tm, NA_W), _rows(tm, S5_W), _rows(tm, D), _full((S5_W, S5_W)), _full((1, S5_W)),
                  _full((1, NA_W)), _full((1, S5_W)), _full((D, D)), _full((1, D))],
        out_specs=[_rows(tm, D), _rows(tm, D)],
        out_shape=[jax.ShapeDtypeStruct((tp, D), F32)] * 2,
        compiler_params=_cp(("arbitrary",), 40),
    )(o_na, y_pre, h, w_glu, b_glu, g_na, g_s5, w_out, g_post)


def _mix_out_bwd(dh, mix, o_na, y_pre, w_glu, b_glu, g_na, g_s5, w_out, g_post, tm):
    tp = dh.shape[0]
    nt = tp // tm

    def body(dh_ref, mix_ref, ona_ref, yp_ref, wglu_ref, bglu_ref, gna_ref, gs5_ref, wout_ref, gpost_ref,
             dona_ref, dyp_ref, dwout_ref, dwglu_ref, dgpost_ref, dgna_ref, dgs5_ref, dbglu_ref, a_out, a_glu):
        i = pl.program_id(0)

        @pl.when(i == 0)
        def _():
            a_out[...] = jnp.zeros_like(a_out)
            a_glu[...] = jnp.zeros_like(a_glu)
            dgpost_ref[...] = jnp.zeros_like(dgpost_ref)
            dgna_ref[...] = jnp.zeros_like(dgna_ref)
            dgs5_ref[...] = jnp.zeros_like(dgs5_ref)
            dbglu_ref[...] = jnp.zeros_like(dbglu_ref)

        dmix, dgpost = _rms_bwd(mix_ref[...], gpost_ref[...], dh_ref[...])
        dgpost_ref[...] += dgpost
        yp = yp_ref[...]
        y = _gelu(yp)
        yb = y.astype(BF16)
        z = _dot(yb, wglu_ref[...]) + bglu_ref[...]
        sg = jax.nn.sigmoid(z)
        o_s5 = y * sg
        o_na = ona_ref[...]
        n1 = _rms(o_na, gna_ref[...]).astype(BF16)
        n2 = _rms(o_s5, gs5_ref[...]).astype(BF16)
        dmb = dmix.astype(BF16)
        a_out[0:NA_W, :] += _dg(n1, dmb, TN)
        a_out[NA_W:, :] += _dg(n2, dmb, TN)
        dn1 = _dg(dmb, wout_ref[0:NA_W, :], NT)
        dn2 = _dg(dmb, wout_ref[NA_W:, :], NT)
        dona, dgna = _rms_bwd(o_na, gna_ref[...], dn1)
        dona_ref[...] = dona
        dgna_ref[...] += dgna
        dos5, dgs5 = _rms_bwd(o_s5, gs5_ref[...], dn2)
        dgs5_ref[...] += dgs5
        dz = dos5 * y * (sg * (1.0 - sg))
        dbglu_ref[...] += jnp.sum(dz, axis=0, keepdims=True)
        dzb = dz.astype(BF16)
        a_glu[...] += _dg(yb, dzb, TN)
        dy = dos5 * sg + _dg(dzb, wglu_ref[...], NT)
        dyp_ref[...] = dy * _gelu_grad(yp)

        @pl.when(i == nt - 1)
        def _():
            pltpu.sync_copy(a_out, dwout_ref)
            pltpu.sync_copy(a_glu, dwglu_ref)

    return pl.pallas_call(
        body, name="mix_out_bwd", grid=(nt,),
        in_specs=[_rows(tm, D), _rows(tm, D), _rows(tm, NA_W), _rows(tm, S5_W), _full((S5_W, S5_W)),
                  _full((1, S5_W)), _full((1, NA_W)), _full((1, S5_W)), _full((D, D)), _full((1, D))],
        out_specs=[_rows(tm, NA_W), _rows(tm, S5_W), ANY, ANY, _full((1, D)), _full((1, NA_W)),
                   _full((1, S5_W)), _full((1, S5_W))],
        out_shape=[jax.ShapeDtypeStruct((tp, NA_W), F32), jax.ShapeDtypeStruct((tp, S5_W), F32),
                   jax.ShapeDtypeStruct((D, D), F32), jax.ShapeDtypeStruct((S5_W, S5_W), F32),
                   jax.ShapeDtypeStruct((1, D), F32), jax.ShapeDtypeStruct((1, NA_W), F32),
                   jax.ShapeDtypeStruct((1, S5_W), F32), jax.ShapeDtypeStruct((1, S5_W), F32)],
        scratch_shapes=[pltpu.VMEM((D, D), F32), pltpu.VMEM((S5_W, S5_W), F32)],
        compiler_params=_cp(("arbitrary",), 48),
    )(dh, mix, o_na, y_pre, w_glu, b_glu, g_na, g_s5, w_out, g_post)


def _mix_in_bwd(dq, dk, dv, du, h, g, w_in, dh, f1, g_post1, tm):
    tp = h.shape[0]
    nt = tp // tm

    def body(dq_ref, dk_ref, dv_ref, du_ref, h_ref, g_ref, w_ref, dh_ref, f_ref, gq_ref,
             dh1_ref, df_ref, dw_ref, dg_ref, dgq_ref, acc):
        i = pl.program_id(0)

        @pl.when(i == 0)
        def _():
            acc[...] = jnp.zeros_like(acc)
            dg_ref[...] = jnp.zeros_like(dg_ref)
            dgq_ref[...] = jnp.zeros_like(dgq_ref)

        x = h_ref[...]
        a = _rms(x, g_ref[...]).astype(BF16)
        da = jnp.zeros((tm, D), F32)
        for j, r in enumerate((dq_ref, dk_ref, dv_ref, du_ref)):
            dp = r[...].astype(BF16)
            da = da + _dg(dp, w_ref[j], NT)
            acc[j] += _dg(a, dp, TN)
        dx, dg = _rms_bwd(x, g_ref[...], da)
        dh1 = dh_ref[...] + dx
        dh1_ref[...] = dh1
        dg_ref[...] += dg
        df, dgq = _rms_bwd(f_ref[...], gq_ref[...], 0.5 * dh1)
        df_ref[...] = df
        dgq_ref[...] += dgq

        @pl.when(i == nt - 1)
        def _():
            pltpu.sync_copy(acc, dw_ref)

    return pl.pallas_call(
        body, name="mix_in_bwd", grid=(nt,),
        in_specs=[_rows(tm, NA_W)] * 4 + [_rows(tm, D), _full((1, D)), _full((N_CHIP, D, NA_W)), _rows(tm, D),
                                         _rows(tm, D), _full((1, D))],
        out_specs=[_rows(tm, D), _rows(tm, D), ANY, _full((1, D)), _full((1, D))],
        out_shape=[jax.ShapeDtypeStruct((tp, D), F32), jax.ShapeDtypeStruct((tp, D), F32),
                   jax.ShapeDtypeStruct((N_CHIP, D, NA_W), F32), jax.ShapeDtypeStruct((1, D), F32),
                   jax.ShapeDtypeStruct((1, D), F32)],
        scratch_shapes=[pltpu.VMEM((N_CHIP, D, NA_W), F32)],
        compiler_params=_cp(("arbitrary",), 48),
    )(dq, dk, dv, du, h, g, w_in, dh, f1, g_post1)


def _final_loss(h, g_final, target, f2, g_post2, n_tok, tm):
    tp = h.shape[0]

    def body(h_ref, g_ref, t_ref, f_ref, gq_ref, dh_ref, df_ref, loss_ref, dg_ref, dgq_ref):
        i = pl.program_id(0)

        @pl.when(i == 0)
        def _():
            loss_ref[...] = jnp.zeros_like(loss_ref)
            dg_ref[...] = jnp.zeros_like(dg_ref)
            dgq_ref[...] = jnp.zeros_like(dgq_ref)

        x = h_ref[...]
        y = _rms(x, g_ref[...])
        row = i * tm + lax.broadcasted_iota(jnp.int32, (tm, 1), 0)
        valid = (row >= N_META) & (row < N_META + n_tok)
        e = jnp.where(valid, y - t_ref[...], 0.0)
        loss_ref[...] += 0.5 * jnp.sum(jnp.mean(e * e, axis=-1, keepdims=True), axis=0, keepdims=True)
        dx, dg = _rms_bwd(x, g_ref[...], e * (1.0 / D))
        dh_ref[...] = dx
        dg_ref[...] += dg
        df, dgq = _rms_bwd(f_ref[...], gq_ref[...], 0.5 * dx)
        df_ref[...] = df
        dgq_ref[...] += dgq

    return pl.pallas_call(
        body, name="final_loss", grid=(tp // tm,),
        in_specs=[_rows(tm, D), _full((1, D)), _rows(tm, D), _rows(tm, D), _full((1, D))],
        out_specs=[_rows(tm, D), _rows(tm, D), _full((1, 1)), _full((1, D)), _full((1, D))],
        out_shape=[jax.ShapeDtypeStruct((tp, D), F32), jax.ShapeDtypeStruct((tp, D), F32),
                   jax.ShapeDtypeStruct((1, 1), F32), jax.ShapeDtypeStruct((1, D), F32),
                   jax.ShapeDtypeStruct((1, D), F32)],
        compiler_params=_cp(("arbitrary",), 40),
    )(h, g_final, target, f2, g_post2)


def _na_patterns(n_rows):
    pats = []
    for kind in range(3):
        pat = [[-1] * K_ROWS for _ in range(Q_ROWS)]
        for i in range(Q_ROWS):
            for jj in range(K_ROWS):
                if kind == 0 and jj < KH:
                    pat[i][jj] = jj - i + KH - 1
                elif kind == 1 and i <= jj < i + KH:
                    pat[i][jj] = jj - i + 3
                elif kind == 2 and K_ROWS - KH <= jj:
                    pat[i][jj] = jj - i - 1
        pats.append(pat)
    return pats


def _diag_onehot():
    q = np.arange(GRID_W)[:, None]
    kc = np.arange(GRID_W)[None, :]
    start = np.clip(q - KW // 2, 0, GRID_W - KW)
    col_in = (kc >= start) & (kc < start + KW)
    e = np.zeros((32, GRID_W, GRID_W), np.float32)
    for d in range(2 * KW - 1):
        e[d] = ((kc - q + KW - 1) == d) & col_in
    return e.reshape(32, GRID_W * GRID_W), col_in


def _rpb_expand(rpb2, e):
    def body(r_ref, e_ref, o_ref):
        o_ref[...] = jnp.dot(r_ref[...], e_ref[...], preferred_element_type=F32, precision=lax.Precision.HIGHEST)

    return pl.pallas_call(
        body, name="rpb_expand", out_shape=jax.ShapeDtypeStruct((rpb2.shape[0], e.shape[1]), F32),
        in_specs=[pl.BlockSpec(memory_space=pltpu.VMEM)] * 2, out_specs=pl.BlockSpec(memory_space=pltpu.VMEM),
    )(rpb2, e)


def _rpb_collapse(dtb2, et):
    def body(d_ref, e_ref, o_ref):
        o_ref[...] = jnp.dot(d_ref[...], e_ref[...], preferred_element_type=F32, precision=lax.Precision.HIGHEST)

    return pl.pallas_call(
        body, name="rpb_collapse", out_shape=jax.ShapeDtypeStruct((dtb2.shape[0], et.shape[1]), F32),
        in_specs=[pl.BlockSpec(memory_space=pltpu.VMEM)] * 2, out_specs=pl.BlockSpec(memory_space=pltpu.VMEM),
    )(dtb2, et)


def _bias_tables(rpb, n_rows):
    e, col_in = _diag_onehot()
    rpb2 = jnp.pad(rpb.reshape(N_HEADS * (2 * KH - 1), 2 * KW - 1), ((0, 0), (0, 1)))
    tb = _rpb_expand(rpb2, jnp.asarray(e)).reshape(N_HEADS, 2 * KH - 1, GRID_W, GRID_W)
    tb = jnp.where(jnp.asarray(col_in)[None, None], tb, NEG_INF)
    neg = jnp.full((N_HEADS, GRID_W, GRID_W), NEG_INF, F32)
    tabs = []
    for pat in _na_patterns(n_rows):
        rows = [jnp.concatenate([tb[:, dr] if dr >= 0 else neg for dr in pat[i]], axis=-1) for i in range(Q_ROWS)]
        tabs.append(jnp.concatenate(rows, axis=1))
    return jnp.stack(tabs)


def _attn_geometry(n_tok):
    n_rows = n_tok // GRID_W
    assert n_rows % Q_ROWS == 0 and n_rows >= K_ROWS
    return n_rows, n_rows // Q_ROWS


def _attn_probs(qh, kh, kmh, bias, scale):
    s = _dg(qh, kh, NT) * scale + bias
    sm = _dg(qh, kmh, NT) * scale
    m = jnp.maximum(jnp.max(s, axis=-1, keepdims=True), jnp.max(sm, axis=-1, keepdims=True))
    p = jnp.exp(s - m)
    pm = jnp.exp(sm - m)
    inv = 1.0 / (jnp.sum(p, axis=-1, keepdims=True) + jnp.sum(pm, axis=-1, keepdims=True))
    return p * inv, pm * inv


def _meta_probs(qmh, kmh, scale):
    s = _dg(qmh, kmh, NT) * scale
    p = jnp.exp(s - jnp.max(s, axis=-1, keepdims=True))
    return p / jnp.sum(p, axis=-1, keepdims=True)


def _step_rows(r, n_rows):
    q0 = pl.multiple_of(N_META + r * QB, 16)
    k0 = pl.multiple_of(N_META + jnp.clip(Q_ROWS * r - (K_ROWS - KH), 0, n_rows - K_ROWS) * GRID_W, 16)
    return q0, k0


def _attn_fwd(q, k, v, bias, n_tok):
    tp = q.shape[0]
    n_rows, n_steps = _attn_geometry(n_tok)
    scale = HEAD_DIM ** -0.5

    def body(q_ref, k_ref, v_ref, b_ref, o_ref):
        r = pl.program_id(1)
        km = k_ref[0:N_META, :]
        vm = v_ref[0:N_META, :]

        @pl.when(r == 0)
        def _():
            qm = q_ref[0:N_META, :]
            outs = []
            for hh in range(2):
                sl = slice(hh * HEAD_DIM, (hh + 1) * HEAD_DIM)
                p = _meta_probs(qm[:, sl], km[:, sl], scale)
                outs.append(_dot(p.astype(BF16), vm[:, sl]))
            o_ref[0:N_META, :] = jnp.concatenate(outs, axis=1)
            o_ref[N_META + n_tok:, :] = jnp.zeros((tp - N_META - n_tok, 2 * HEAD_DIM), F32)

        q0, k0 = _step_rows(r, n_rows)
        qb = q_ref[pl.ds(q0, QB), :]
        kb = k_ref[pl.ds(k0, KB), :]
        vb = v_ref[pl.ds(k0, KB), :]
        outs = []
        for hh in range(2):
            sl = slice(hh * HEAD_DIM, (hh + 1) * HEAD_DIM)
            p, pm = _attn_probs(qb[:, sl], kb[:, sl], km[:, sl], b_ref[0, hh], scale)
            outs.append(_dot(p.astype(BF16), vb[:, sl]) + _dot(pm.astype(BF16), vm[:, sl]))
        o_ref[pl.ds(q0, QB), :] = jnp.concatenate(outs, axis=1)

    def bias_map(hp, r):
        return (jnp.where(r == 0, 0, jnp.where(r == n_steps - 1, 2, 1)), hp, 0, 0)

    col = pl.BlockSpec((tp, 2 * HEAD_DIM), lambda hp, r: (0, hp))
    return pl.pallas_call(
        body, name="attn_fwd", grid=(N_HEADS // 2, n_steps),
        in_specs=[col, col, col, pl.BlockSpec((1, 2, QB, KB), bias_map)],
        out_specs=col, out_shape=jax.ShapeDtypeStruct((tp, NA_W), F32),
        compiler_params=_cp(("arbitrary", "arbitrary"), 40),
    )(q, k, v, bias)


def _attn_bwd(q, k, v, bias, do, n_tok):
    tp = q.shape[0]
    n_rows, n_steps = _attn_geometry(n_tok)
    scale = HEAD_DIM ** -0.5
    pats = _na_patterns(n_rows)

    def body(q_ref, k_ref, v_ref, b_ref, do_ref, dq_ref, dk_ref, dv_ref, dtb_ref):
        r = pl.program_id(1)
        km = k_ref[0:N_META, :]
        vm = v_ref[0:N_META, :]

        @pl.when(r == 0)
        def _():
            dk_ref[...] = jnp.zeros_like(dk_ref)
            dv_ref[...] = jnp.zeros_like(dv_ref)
            dtb_ref[...] = jnp.zeros_like(dtb_ref)
            dq_ref[N_META + n_tok:, :] = jnp.zeros((tp - N_META - n_tok, 2 * HEAD_DIM), F32)
            qm = q_ref[0:N_META, :]
            dom = do_ref[0:N_META, :].astype(BF16)
            dqs, dks, dvs = [], [], []
            for hh in range(2):
                sl = slice(hh * HEAD_DIM, (hh + 1) * HEAD_DIM)
                p = _meta_probs(qm[:, sl], km[:, sl], scale)
                dp = _dg(dom[:, sl], vm[:, sl], NT)
                ds = (p * (dp - jnp.sum(dp * p, axis=-1, keepdims=True))).astype(BF16)
                dvs.append(_dg(p.astype(BF16), dom[:, sl], TN))
                dqs.append(_dot(ds, km[:, sl]) * scale)
                dks.append(_dg(ds, qm[:, sl], TN) * scale)
            dq_ref[0:N_META, :] = jnp.concatenate(dqs, axis=1)
            dk_ref[0:N_META, :] += jnp.concatenate(dks, axis=1)
            dv_ref[0:N_META, :] += jnp.concatenate(dvs, axis=1)

        q0, k0 = _step_rows(r, n_rows)
        qb = q_ref[pl.ds(q0, QB), :]
        kb = k_ref[pl.ds(k0, KB), :]
        vb = v_ref[pl.ds(k0, KB), :]
        dob = do_ref[pl.ds(q0, QB), :].astype(BF16)
        dqs, dks, dvs, dkms, dvms, dss = [], [], [], [], [], []
        for hh in range(2):
            sl = slice(hh * HEAD_DIM, (hh + 1) * HEAD_DIM)
            qh, kh, vh, kmh, vmh, doh = qb[:, sl], kb[:, sl], vb[:, sl], km[:, sl], vm[:, sl], dob[:, sl]
            p, pm = _attn_probs(qh, kh, kmh, b_ref[0, hh], scale)
            dp = _dg(doh, vh, NT)
            dpm = _dg(doh, vmh, NT)
            delta = jnp.sum(dp * p, axis=-1, keepdims=True) + jnp.sum(dpm * pm, axis=-1, keepdims=True)
            ds = p * (dp - delta)
            dsb = ds.astype(BF16)
            dsmb = (pm * (dpm - delta)).astype(BF16)
            dss.append(ds)
            dvs.append(_dg(p.astype(BF16), doh, TN))
            dvms.append(_dg(pm.astype(BF16), doh, TN))
            dqs.append((_dot(dsb, kh) + _dot(dsmb, kmh)) * scale)
            dks.append(_dg(dsb, qh, TN) * scale)
            dkms.append(_dg(dsmb, qh, TN) * scale)
        dq_ref[pl.ds(q0, QB), :] = jnp.concatenate(dqs, axis=1)
        dk_ref[pl.ds(k0, KB), :] += jnp.concatenate(dks, axis=1)
        dv_ref[pl.ds(k0, KB), :] += jnp.concatenate(dvs, axis=1)
        dk_ref[0:N_META, :] += jnp.concatenate(dkms, axis=1)
        dv_ref[0:N_META, :] += jnp.concatenate(dvms, axis=1)

        def add_bias_grad(pat):
            for hh in range(2):
                for i in range(Q_ROWS):
                    for jj in range(K_ROWS):
                        if pat[i][jj] >= 0:
                            dtb_ref[hh, pat[i][jj]] += dss[hh][i * GRID_W:(i + 1) * GRID_W,
                                                               jj * GRID_W:(jj + 1) * GRID_W]

        @pl.when(r == 0)
        def _():
            add_bias_grad(pats[0])

        @pl.when((r > 0) & (r < n_steps - 1))
        def _():
            add_bias_grad(pats[1])

        @pl.when(r == n_steps - 1)
        def _():
            add_bias_grad(pats[2])

    def bias_map(hp, r):
        return (jnp.where(r == 0, 0, jnp.where(r == n_steps - 1, 2, 1)), hp, 0, 0)

    col = pl.BlockSpec((tp, 2 * HEAD_DIM), lambda hp, r: (0, hp))
    n_dr = 2 * KH - 1
    return pl.pallas_call(
        body, name="attn_bwd", grid=(N_HEADS // 2, n_steps),
        in_specs=[col, col, col, pl.BlockSpec((1, 2, QB, KB), bias_map), col],
        out_specs=[col, col, col, pl.BlockSpec((2, n_dr, GRID_W, GRID_W), lambda hp, r: (hp, 0, 0, 0))],
        out_shape=[jax.ShapeDtypeStruct((tp, NA_W), F32)] * 3 +
                  [jax.ShapeDtypeStruct((N_HEADS, n_dr, GRID_W, GRID_W), F32)],
        compiler_params=_cp(("arbitrary", "arbitrary"), 48),
    )(q, k, v, bias, do)


def _expand_onehot():
    ex = np.zeros((S5_P, S5_P * S5_H), np.float32)
    for p in range(S5_P):
        ex[p, p * S5_H:(p + 1) * S5_H] = 1.0
    return ex


def _s5_disc_math(lam_re, lam_im, log_dt, b_re, b_im, ex):
    dt = jnp.exp(log_dt)
    ea = jnp.exp(lam_re * dt)
    a_re = ea * jnp.cos(lam_im * dt)
    a_im = ea * jnp.sin(lam_im * dt)
    den = lam_re * lam_re + lam_im * lam_im
    c_re = ((a_re - 1.0) * lam_re + a_im * lam_im) / den
    c_im = (a_im * lam_re - (a_re - 1.0) * lam_im) / den
    ce_re = jnp.dot(c_re, ex, preferred_element_type=F32, precision=lax.Precision.HIGHEST)
    ce_im = jnp.dot(c_im, ex, preferred_element_type=F32, precision=lax.Precision.HIGHEST)
    return a_re, a_im, ce_re * b_re - ce_im * b_im, ce_re * b_im + ce_im * b_re


def _s5_disc(lam_re, lam_im, log_dt, b_re, b_im):
    ex = jnp.asarray(_expand_onehot())
    n = lam_re.shape[0]

    def body(lr, li, ld, br, bi, ex_ref, ar, ai, bbr, bbi):
        ar[...], ai[...], bbr[...], bbi[...] = _s5_disc_math(lr[...], li[...], ld[...], br[...], bi[...], ex_ref[...])

    vm = pl.BlockSpec(memory_space=pltpu.VMEM)
    return pl.pallas_call(
        body, name="s5_disc", in_specs=[vm] * 6, out_specs=[vm] * 4,
        out_shape=[jax.ShapeDtypeStruct((n, S5_P), F32)] * 2 + [jax.ShapeDtypeStruct((n, S5_P * S5_H), F32)] * 2,
    )(lam_re, lam_im, log_dt, b_re, b_im, ex)


def _s5_disc_bwd(lam_re, lam_im, log_dt, b_re, b_im, da_re, da_im, dbb_re, dbb_im):
    ex = jnp.asarray(_expand_onehot())
    n = lam_re.shape[0]

    def body(lr, li, ld, br, bi, ex_ref, dar, dai, dbr, dbi, o_lr, o_li, o_ld, o_br, o_bi):
        e = ex_ref[...]
        _, vjp = jax.vjp(lambda a, b, c, d, f: _s5_disc_math(a, b, c, d, f, e), lr[...], li[...], ld[...], br[...], bi[...])
        o_lr[...], o_li[...], o_ld[...], o_br[...], o_bi[...] = vjp((dar[...], dai[...], dbr[...], dbi[...]))

    vm = pl.BlockSpec(memory_space=pltpu.VMEM)
    return pl.pallas_call(
        body, name="s5_disc_bwd", in_specs=[vm] * 10, out_specs=[vm] * 5,
        out_shape=[jax.ShapeDtypeStruct((n, S5_P), F32)] * 2 + [jax.ShapeDtypeStruct((n, 1), F32)] +
                  [jax.ShapeDtypeStruct((n, S5_P * S5_H), F32)] * 2,
    )(lam_re, lam_im, log_dt, b_re, b_im, ex, da_re, da_im, dbb_re, dbb_im)


def _s5_matrices(a_re, a_im, bb_re, bb_im, c_re, c_im):
    gl = S5_G // N_BUNDLE
    eye = jnp.eye(gl, dtype=F32)
    half = gl * S5_P

    def in_mat(bb):
        t = bb.reshape(2, N_BUNDLE, gl, S5_P, S5_H).transpose(0, 1, 4, 2, 3)
        m = t[:, :, None] * eye[None, None, :, None, :, None]
        return m.reshape(2, N_BUNDLE, gl * S5_H, half)

    def out_mat(c):
        t = c.reshape(2, N_BUNDLE, gl, S5_H, S5_P).transpose(0, 1, 2, 4, 3)
        m = t[:, :, :, :, None, :] * eye[None, None, :, None, :, None]
        return m.reshape(2, N_BUNDLE, half, gl * S5_H)

    a = jnp.concatenate([a_re.reshape(2, N_BUNDLE, 1, half), a_im.reshape(2, N_BUNDLE, 1, half)], axis=-1)
    bm = jnp.concatenate([in_mat(bb_re), in_mat(bb_im)], axis=-1)
    cm = jnp.concatenate([out_mat(c_re), -out_mat(c_im)], axis=-2)
    return a, bm, cm


def _scan_chunks(length):
    return [(t0, min(SCAN_CHUNK, length - t0)) for t0 in range(0, length, SCAN_CHUNK)]


def _scan(src_ref, dst_ref, prev_ref, prev_off, n_rows, a_re, a_im, carry, reverse):
    half = a_re.shape[-1]
    n_blk = n_rows // 8
    rid = lax.broadcasted_iota(jnp.int32, (8, half), 0)

    def blk(i, carry):
        xr, xi = carry
        bi = (n_blk - 1 - i) if reverse else i
        off = pl.multiple_of(bi * 8, 8)
        v = src_ref[pl.ds(off, 8), :]
        o_r = jnp.zeros((8, half), F32)
        o_i = jnp.zeros((8, half), F32)
        p_r = jnp.zeros((8, half), F32)
        p_i = jnp.zeros((8, half), F32)
        for j in (range(7, -1, -1) if reverse else range(8)):
            if prev_ref is not None:
                p_r = jnp.where(rid == j, xr, p_r)
                p_i = jnp.where(rid == j, xi, p_i)
            nr = a_re * xr - a_im * xi + v[j:j + 1, :half]
            ni = a_re * xi + a_im * xr + v[j:j + 1, half:]
            xr, xi = nr, ni
            if dst_ref is not None:
                o_r = jnp.where(rid == j, xr, o_r)
                o_i = jnp.where(rid == j, xi, o_i)
        if dst_ref is not None:
            dst_ref[pl.ds(off, 8), :] = jnp.concatenate([o_r, o_i], axis=1)
        if prev_ref is not None:
            prev_ref[pl.ds(pl.multiple_of(prev_off + off, 8), 8), :] = jnp.concatenate([p_r, p_i], axis=1)
        return xr, xi

    return lax.fori_loop(0, n_blk, blk, carry)


def _s5_fwd(u, d_skip, a, bm, cm, length):
    tp = u.shape[0]
    cw = S5_W // N_BUNDLE
    sw = a.shape[-1]
    half = sw // 2
    chunks = _scan_chunks(length)

    def body(u_ref, d_ref, a_ref, bm_ref, cm_ref, y_ref, bu_s, xs_s):
        y_ref[...] = u_ref[...] * d_ref[...]
        for dr in range(2):
            a_re = a_ref[dr, 0, :, 0:half]
            a_im = a_ref[dr, 0, :, half:]
            carry = (jnp.zeros((1, half), F32), jnp.zeros((1, half), F32))
            for t0, n in (chunks if dr == 0 else chunks[::-1]):
                bu_s[0:n, :] = _dot(u_ref[t0:t0 + n, :].astype(BF16), bm_ref[dr, 0])
                carry = _scan(bu_s, xs_s, None, 0, n, a_re, a_im, carry, dr == 1)
                y_ref[t0:t0 + n, :] += _dot(xs_s[0:n, :].astype(BF16), cm_ref[dr, 0])

    return pl.pallas_call(
        body, name="s5_fwd", grid=(N_BUNDLE,),
        in_specs=[pl.BlockSpec((tp, cw), lambda b: (0, b)), pl.BlockSpec((1, cw), lambda b: (0, b)),
                  pl.BlockSpec((2, 1, 1, sw), lambda b: (0, b, 0, 0)),
                  pl.BlockSpec((2, 1, cw, sw), lambda b: (0, b, 0, 0)),
                  pl.BlockSpec((2, 1, sw, cw), lambda b: (0, b, 0, 0))],
        out_specs=pl.BlockSpec((tp, cw), lambda b: (0, b)),
        out_shape=jax.ShapeDtypeStruct((tp, S5_W), F32),
        scratch_shapes=[pltpu.VMEM((SCAN_CHUNK, sw), F32), pltpu.VMEM((SCAN_CHUNK, sw), F32)],
        compiler_params=_cp(("arbitrary",), 40),
    )(u, d_skip, a, bm, cm)


def _s5_bwd(u, dy, d_skip, a, bm, cm, length):
    tp = u.shape[0]
    cw = S5_W // N_BUNDLE
    sw = a.shape[-1]
    half = sw // 2
    chunks = _scan_chunks(length)

    def body(u_ref, dy_ref, d_ref, a_ref, bm_ref, cm_ref, du_ref, dd_ref, dbm_ref, dcm_ref, da_ref, bu_s, g_s, xp_s):
        du_ref[...] = dy_ref[...] * d_ref[...]
        dd_ref[...] = jnp.sum(dy_ref[...] * u_ref[...], axis=0, keepdims=True)
        dbm_ref[...] = jnp.zeros_like(dbm_ref)
        dcm_ref[...] = jnp.zeros_like(dcm_ref)
        zero = (jnp.zeros((1, half), F32), jnp.zeros((1, half), F32))
        for dr in range(2):
            a_re = a_ref[dr, 0, :, 0:half]
            a_im = a_ref[dr, 0, :, half:]
            seq = chunks if dr == 0 else chunks[::-1]
            carry = zero
            for t0, n in seq:
                bu_s[0:n, :] = _dot(u_ref[t0:t0 + n, :].astype(BF16), bm_ref[dr, 0])
                carry = _scan(bu_s, None, xp_s, t0, n, a_re, a_im, carry, dr == 1)
            carry = zero
            da_r = jnp.zeros((1, half), F32)
            da_i = jnp.zeros((1, half), F32)
            for t0, n in seq[::-1]:
                ub = u_ref[t0:t0 + n, :].astype(BF16)
                dyb = dy_ref[t0:t0 + n, :].astype(BF16)
                bu_s[0:n, :] = _dg(dyb, cm_ref[dr, 0], NT)
                carry = _scan(bu_s, g_s, None, 0, n, a_re, -a_im, carry, dr == 0)
                g = g_s[0:n, :]
                gb = g.astype(BF16)
                du_ref[t0:t0 + n, :] += _dg(gb, bm_ref[dr, 0], NT)
                dbm_ref[dr, 0] += _dg(ub, gb, TN)
                xp = xp_s[t0:t0 + n, :]
                xp_r, xp_i = xp[:, 0:half], xp[:, half:]
                g_r, g_i = g[:, 0:half], g[:, half:]
                bu = _dot(ub, bm_ref[dr, 0])
                x_r = a_re * xp_r - a_im * xp_i + bu[:, 0:half]
                x_i = a_re * xp_i + a_im * xp_r + bu[:, half:]
                dcm_ref[dr, 0] += _dg(jnp.concatenate([x_r, x_i], axis=1).astype(BF16), dyb, TN)
                da_r = da_r + jnp.sum(g_r * xp_r + g_i * xp_i, axis=0, keepdims=True)
                da_i = da_i + jnp.sum(g_i * xp_r - g_r * xp_i, axis=0, keepdims=True)
            da_ref[dr, 0] = jnp.concatenate([da_r, da_i], axis=1)

    lp = -(-length // 8) * 8
    return pl.pallas_call(
        body, name="s5_bwd", grid=(N_BUNDLE,),
        in_specs=[pl.BlockSpec((tp, cw), lambda b: (0, b)), pl.BlockSpec((tp, cw), lambda b: (0, b)),
                  pl.BlockSpec((1, cw), lambda b: (0, b)),
                  pl.BlockSpec((2, 1, 1, sw), lambda b: (0, b, 0, 0)),
                  pl.BlockSpec((2, 1, cw, sw), lambda b: (0, b, 0, 0)),
                  pl.BlockSpec((2, 1, sw, cw), lambda b: (0, b, 0, 0))],
        out_specs=[pl.BlockSpec((tp, cw), lambda b: (0, b)), pl.BlockSpec((1, cw), lambda b: (0, b)),
                   pl.BlockSpec((2, 1, cw, sw), lambda b: (0, b, 0, 0)),
                   pl.BlockSpec((2, 1, sw, cw), lambda b: (0, b, 0, 0)),
                   pl.BlockSpec((2, 1, 1, sw), lambda b: (0, b, 0, 0))],
        out_shape=[jax.ShapeDtypeStruct((tp, S5_W), F32), jax.ShapeDtypeStruct((1, S5_W), F32),
                   jax.ShapeDtypeStruct((2, N_BUNDLE, cw, sw), F32), jax.ShapeDtypeStruct((2, N_BUNDLE, sw, cw), F32),
                   jax.ShapeDtypeStruct((2, N_BUNDLE, 1, sw), F32)],
        scratch_shapes=[pltpu.VMEM((SCAN_CHUNK, sw), F32), pltpu.VMEM((SCAN_CHUNK, sw), F32),
                        pltpu.VMEM((lp, sw), F32)],
        compiler_params=_cp(("arbitrary",), 48),
    )(u, dy, d_skip, a, bm, cm)


def _row_tile(tp):
    return max(tm for tm in range(16, 449, 16) if tp % tm == 0)


def _local_step(x, target, meta, gains, w, s5, rpb):
    n_tok = x.shape[0]
    length = N_META + n_tok
    tp = length + 16
    tm = _row_tile(tp)
    tmb = tm // 2
    n_rows = n_tok // GRID_W
    pad = jnp.zeros((tp - length, D), F32)
    h0 = jnp.concatenate([meta, x, pad], axis=0)
    tgt = jnp.concatenate([jnp.zeros((N_META, D), F32), target, pad], axis=0)

    n2 = 2 * S5_G
    lam_re = s5["lam_re"].reshape(n2, S5_P)
    lam_im = s5["lam_im"].reshape(n2, S5_P)
    log_dt = s5["log_dt"].reshape(n2, 1)
    b_re = s5["b_re"].reshape(n2, S5_P * S5_H)
    b_im = s5["b_im"].reshape(n2, S5_P * S5_H)
    a_re, a_im, bb_re, bb_im = _s5_disc(lam_re, lam_im, log_dt, b_re, b_im)

    def mats(a_re, a_im, bb_re, bb_im, c_re, c_im):
        return _s5_matrices(a_re.reshape(2, S5_G, S5_P), a_im.reshape(2, S5_G, S5_P),
                            bb_re.reshape(2, S5_G, S5_P * S5_H), bb_im.reshape(2, S5_G, S5_P * S5_H), c_re, c_im)

    (a_m, bm, cm), mats_vjp = jax.vjp(mats, a_re, a_im, bb_re, bb_im, s5["c_re"], s5["c_im"])
    bm16 = bm.astype(BF16)
    cm16 = cm.astype(BF16)
    bias = _bias_tables(rpb, n_rows)

    h1, gate1, up1, f1 = _ffn_fwd("ffn1_fwd", h0, gains["ffn1_pre_g"], gains["ffn1_post_g"],
                                  w["ffn1_w_gate"], w["ffn1_w_up"], w["ffn1_w_down"], tm)
    q, k, v, u = _mix_in(h1, gains["mix_pre_g"], w["w_in"], tm)
    o_na = _attn_fwd(q, k, v, bias, n_tok)
    y_pre = _s5_fwd(u, gains["s5_d"], a_m, bm16, cm16, length)
    w_glu = w["s5_w_glu"].reshape(S5_W, S5_W)
    w_out = w["w_out"].reshape(D, D)
    h2, mix = _mix_out(o_na, y_pre, h1, w_glu, gains["s5_b_glu"], gains["na_out_g"], gains["s5_out_g"], w_out,
                       gains["mix_post_g"], tm)
    h3, gate2, up2, f2 = _ffn_fwd("ffn2_fwd", h2, gains["ffn2_pre_g"], gains["ffn2_post_g"],
                                  w["ffn2_w_gate"], w["ffn2_w_up"], w["ffn2_w_down"], tm)
    dh3, df2, loss, dg_final, dg_post2 = _final_loss(h3, gains["final_g"], tgt, f2, gains["ffn2_post_g"], n_tok, tm)

    dwg2, dwu2, dwd2, dxn2 = _ffn_bwd("ffn2_bwd", h2, gains["ffn2_pre_g"], df2, gate2, up2,
                                      w["ffn2_w_gate"], w["ffn2_w_up"], w["ffn2_w_down"], tmb)
    dh2, dg_pre2 = _ffn_pre_bwd("ffn2_pre_bwd", dh3, dxn2, h2, gains["ffn2_pre_g"], tm)
    do_na, dy_pre, dw_out, dw_glu, dg_mpost, dg_na, dg_s5, db_glu = _mix_out_bwd(
        dh2, mix, o_na, y_pre, w_glu, gains["s5_b_glu"], gains["na_out_g"], gains["s5_out_g"], w_out,
        gains["mix_post_g"], tm)
    dq, dk, dv, dtb = _attn_bwd(q, k, v, bias, do_na, n_tok)
    du, dd, dbm, dcm, da_m = _s5_bwd(u, dy_pre, gains["s5_d"], a_m, bm16, cm16, length)
    dh1, df1, dw_in, dg_mpre, dg_post1 = _mix_in_bwd(dq, dk, dv, du, h1, gains["mix_pre_g"], w["w_in"], dh2, f1,
                                                     gains["ffn1_post_g"], tm)
    dwg1, dwu1, dwd1, dxn1 = _ffn_bwd("ffn1_bwd", h0, gains["ffn1_pre_g"], df1, gate1, up1,
                                      w["ffn1_w_gate"], w["ffn1_w_up"], w["ffn1_w_down"], tmb)
    dh0, dg_pre1 = _ffn_pre_bwd("ffn1_pre_bwd", dh1, dxn1, h0, gains["ffn1_pre_g"], tm)

    e, _ = _diag_onehot()
    n_dr = 2 * KH - 1
    drpb = _rpb_collapse(dtb.reshape(N_HEADS * n_dr, GRID_W * GRID_W), jnp.asarray(e.T))
    drpb = drpb[:, :2 * KW - 1].reshape(N_HEADS, n_dr, 2 * KW - 1)
    da_re, da_im, dbb_re, dbb_im, dc_re, dc_im = mats_vjp((da_m, dbm, dcm))
    dlam_re, dlam_im, dlog_dt, db_re, db_im = _s5_disc_bwd(lam_re, lam_im, log_dt, b_re, b_im,
                                                            da_re, da_im, dbb_re, dbb_im)

    big = {"ffn1_w_gate": dwg1, "ffn1_w_up": dwu1, "ffn1_w_down": dwd1, "w_in": dw_in,
           "s5_w_glu": dw_glu.reshape(N_CHIP, S5_W // N_CHIP, S5_W), "w_out": dw_out.reshape(N_CHIP, D // N_CHIP, D),
           "ffn2_w_gate": dwg2, "ffn2_w_up": dwu2, "ffn2_w_down": dwd2}
    small = {"ffn1_pre_g": dg_pre1, "ffn1_post_g": dg_post1, "mix_pre_g": dg_mpre, "na_rpb": drpb,
             "s5_lam_re": dlam_re, "s5_lam_im": dlam_im, "s5_log_dt": dlog_dt, "s5_b_re": db_re, "s5_b_im": db_im,
             "s5_c_re": dc_re, "s5_c_im": dc_im, "s5_d": dd, "s5_b_glu": db_glu, "na_out_g": dg_na,
             "s5_out_g": dg_s5, "mix_post_g": dg_mpost, "ffn2_pre_g": dg_pre2, "ffn2_post_g": dg_post2,
             "final_g": dg_final}
    return loss[0, 0], dh0, big, small


def _mesh_pos():
    return lax.axis_index("x"), lax.axis_index("y"), lax.axis_index("c")


def _other_chips(x, y):
    return [(1 - x, y), (x, 1 - y), (1 - x, 1 - y)]


def _all_gather(bufs):
    n = len(bufs)

    def body(*refs):
        outs = refs[n:2 * n]
        send_sems, recv_sems = refs[2 * n:]
        x, y, c = _mesh_pos()
        me, sibling = (x, y, c), (x, y, 1 - c)
        chips = _other_chips(x, y)

        def rows(a, px, py, pc):
            rh = outs[a].shape[1] // 2
            return outs[a].at[2 * px + py, pl.ds(pc * rh, rh), :]

        def copy(a, kind, block, to):
            return pltpu.make_async_remote_copy(
                src_ref=rows(a, *block), dst_ref=rows(a, *block),
                send_sem=send_sems.at[a * 7 + kind], recv_sem=recv_sems.at[a * 7 + kind],
                device_id=to, device_id_type=MESH_ID)

        first = []
        for a in range(n):
            for j, chip in enumerate(chips):
                first.append(copy(a, 1 + j, me, (*chip, c)))
            first.append(copy(a, 0, me, sibling))
        for cp in first:
            cp.start()
        passed = []
        for a in range(n):
            for j, chip in enumerate(chips):
                copy(a, 1 + j, (*chip, c), me).wait_recv()
                cp = copy(a, 4 + j, (*chip, c), sibling)
                cp.start()
                passed.append(cp)
        for a in range(n):
            copy(a, 0, sibling, me).wait_recv()
            for j, chip in enumerate(chips):
                copy(a, 4 + j, (*chip, 1 - c), me).wait_recv()
        for cp in first + passed:
            cp.wait_send()

    return pl.pallas_call(
        body, name="weight_all_gather",
        out_shape=[jax.ShapeDtypeStruct(b.shape, b.dtype) for b in bufs],
        in_specs=[ANY] * n, out_specs=[ANY] * n, input_output_aliases={a: a for a in range(n)},
        scratch_shapes=[pltpu.SemaphoreType.DMA((7 * n,)), pltpu.SemaphoreType.DMA((7 * n,))],
    )(*bufs)


def _own_half_buffer(piece, k_pos, c_pos, dtype):
    rh = piece.shape[0] // 2
    half = lax.dynamic_slice_in_dim(piece, c_pos * rh, rh, 0).astype(dtype)
    buf = lax.empty((N_CHIP,) + piece.shape, dtype)
    return lax.dynamic_update_slice(buf, half[None], (k_pos, c_pos * rh, 0))


def _pair_exchange(grads):
    n = len(grads)

    def body(*refs):
        ins, outs = refs[:n], refs[n:2 * n]
        send_sems, recv_sems = refs[2 * n:]
        x, y, c = _mesh_pos()
        cps = []
        for a in range(n):
            rh = ins[a].shape[1] // 2
            cp = pltpu.make_async_remote_copy(
                src_ref=ins[a].at[:, pl.ds((1 - c) * rh, rh), :], dst_ref=outs[a],
                send_sem=send_sems.at[a], recv_sem=recv_sems.at[a], device_id=(x, y, 1 - c), device_id_type=MESH_ID)
            cp.start()
            cps.append(cp)
        for cp in cps:
            cp.wait()

    return pl.pallas_call(
        body, name="grad_pair_exchange",
        out_shape=[jax.ShapeDtypeStruct((N_CHIP, g.shape[1] // 2, g.shape[2]), g.dtype) for g in grads],
        in_specs=[ANY] * n, out_specs=[ANY] * n,
        scratch_shapes=[pltpu.SemaphoreType.DMA((n,)), pltpu.SemaphoreType.DMA((n,))],
    )(*grads)


def _chip_sum(name, g, recv, c_arr):
    _, r, cc = g.shape
    rh = r // 2

    def body(c_ref, g_ref, r_ref, o_ref):
        o_ref[...] = (g_ref[...] + r_ref[...]).astype(BF16)

    return pl.pallas_call(
        body, name=name, out_shape=jax.ShapeDtypeStruct((N_CHIP, rh, cc), BF16),
        grid_spec=pltpu.PrefetchScalarGridSpec(
            num_scalar_prefetch=1, grid=(N_CHIP,),
            in_specs=[pl.BlockSpec((1, rh, cc), lambda j, c_ref: (j, c_ref[0], 0)),
                      pl.BlockSpec((1, rh, cc), lambda j, c_ref: (j, 0, 0))],
            out_specs=pl.BlockSpec((1, rh, cc), lambda j, c_ref: (j, 0, 0))),
        compiler_params=_cp(("arbitrary",), 32),
    )(c_arr, g, recv)


def _chip_scatter(sums):
    n = len(sums)

    def body(*refs):
        ins, outs = refs[:n], refs[n:2 * n]
        send_sems, recv_sems = refs[2 * n:]
        x, y, c = _mesh_pos()
        cps = []
        for a in range(n):
            for j, chip in enumerate(_other_chips(x, y)):
                cp = pltpu.make_async_remote_copy(
                    src_ref=ins[a].at[2 * chip[0] + chip[1]], dst_ref=outs[a].at[j],
                    send_sem=send_sems.at[3 * a + j], recv_sem=recv_sems.at[3 * a + j],
                    device_id=(*chip, c), device_id_type=MESH_ID)
                cp.start()
                cps.append(cp)
        for cp in cps:
            cp.wait()

    return pl.pallas_call(
        body, name="grad_chip_scatter",
        out_shape=[jax.ShapeDtypeStruct((3,) + s.shape[1:], s.dtype) for s in sums],
        in_specs=[ANY] * n, out_specs=[ANY] * n,
        scratch_shapes=[pltpu.SemaphoreType.DMA((3 * n,)), pltpu.SemaphoreType.DMA((3 * n,))],
    )(*sums)


def _total_sum(name, sums, recv3, kc_arr):
    _, rh, cc = sums.shape

    def body(kc_ref, s_ref, r_ref, o_ref):
        t = s_ref[0].astype(F32) + r_ref[0].astype(F32)
        t = t + r_ref[1].astype(F32)
        o_ref[...] = t + r_ref[2].astype(F32)

    return pl.pallas_call(
        body, name=name, out_shape=jax.ShapeDtypeStruct((2 * rh, cc), F32),
        grid_spec=pltpu.PrefetchScalarGridSpec(
            num_scalar_prefetch=1, grid=(1,),
            in_specs=[pl.BlockSpec((1, rh, cc), lambda i, kc_ref: (kc_ref[0], 0, 0)),
                      pl.BlockSpec((3, rh, cc), lambda i, kc_ref: (0, 0, 0))],
            out_specs=pl.BlockSpec((rh, cc), lambda i, kc_ref: (kc_ref[1], 0))),
        compiler_params=_cp(("arbitrary",), 32),
    )(kc_arr, sums, recv3)


def _pair_assemble(totals):
    n = len(totals)

    def body(*refs):
        outs = refs[n:2 * n]
        send_sems, recv_sems = refs[2 * n:]
        x, y, c = _mesh_pos()
        cps = []
        for a in range(n):
            rh = outs[a].shape[0] // 2
            here = outs[a].at[pl.ds(c * rh, rh), :]
            cp = pltpu.make_async_remote_copy(
                src_ref=here, dst_ref=here, send_sem=send_sems.at[a], recv_sem=recv_sems.at[a],
                device_id=(x, y, 1 - c), device_id_type=MESH_ID)
            cp.start()
            cps.append(cp)
        for cp in cps:
            cp.wait()

    return pl.pallas_call(
        body, name="grad_pair_assemble",
        out_shape=[jax.ShapeDtypeStruct(t.shape, t.dtype) for t in totals],
        in_specs=[ANY] * n, out_specs=[ANY] * n, input_output_aliases={a: a for a in range(n)},
        scratch_shapes=[pltpu.SemaphoreType.DMA((n,)), pltpu.SemaphoreType.DMA((n,))],
    )(*totals)


def _small_allreduce(buf):
    shape = buf.shape

    def body(in_ref, out_ref, sib_s, csum_s, all_s, send_sems, recv_sems):
        x, y, c = _mesh_pos()
        k = 2 * x + y
        cp = pltpu.make_async_remote_copy(src_ref=in_ref, dst_ref=sib_s, send_sem=send_sems.at[0],
                                          recv_sem=recv_sems.at[0], device_id=(x, y, 1 - c), device_id_type=MESH_ID)
        cp.start()
        cp.wait()
        csum_s[...] = in_ref[...] + sib_s[...]
        all_s[k] = csum_s[...]
        cps = []
        for j, chip in enumerate(_other_chips(x, y)):
            cp = pltpu.make_async_remote_copy(src_ref=csum_s, dst_ref=all_s.at[k], send_sem=send_sems.at[1 + j],
                                              recv_sem=recv_sems.at[1 + j], device_id=(*chip, c),
                                              device_id_type=MESH_ID)
            cp.start()
            cps.append(cp)
        for cp in cps:
            cp.wait()
        out_ref[...] = ((all_s[0] + all_s[1]) + all_s[2]) + all_s[3]

    vm = pl.BlockSpec(memory_space=pltpu.VMEM)
    return pl.pallas_call(
        body, name="small_allreduce", out_shape=jax.ShapeDtypeStruct(shape, F32), in_specs=[vm], out_specs=vm,
        scratch_shapes=[pltpu.VMEM(shape, F32), pltpu.VMEM(shape, F32), pltpu.VMEM((N_CHIP,) + shape, F32),
                        pltpu.SemaphoreType.DMA((4,)), pltpu.SemaphoreType.DMA((4,))],
        compiler_params=_cp(None, 32),
    )(buf)


def _adamw_math(w, g, m, v):
    m = ADAM_B1 * m + (1.0 - ADAM_B1) * g
    v = ADAM_B2 * v + (1.0 - ADAM_B2) * (g * g)
    m_hat = m / (1.0 - ADAM_B1 ** ADAM_STEP)
    v_hat = v / (1.0 - ADAM_B2 ** ADAM_STEP)
    delta = -ADAM_LR * (m_hat / (jnp.sqrt(v_hat) + ADAM_EPS) + ADAM_WD * w)
    return delta, m, v


def _adamw(name, w, g, m, v):
    r, c = w.shape
    tr = max(t for t in range(8, 513, 8) if r % t == 0)

    def body(w_ref, g_ref, m_ref, v_ref, d_ref, mo_ref, vo_ref):
        d_ref[...], mo_ref[...], vo_ref[...] = _adamw_math(w_ref[...], g_ref[...], m_ref[...], v_ref[...])

    return pl.pallas_call(
        body, name=name, grid=(r // tr,), in_specs=[_rows(tr, c)] * 4, out_specs=[_rows(tr, c)] * 3,
        out_shape=[jax.ShapeDtypeStruct((r, c), F32)] * 3, compiler_params=_cp(("arbitrary",), 32),
    )(w, g, m, v)


def _pack(arrays):
    flat = jnp.concatenate([a.reshape(-1) for a in arrays])
    n = flat.shape[0]
    rows = -(-n // (8 * 128)) * 8
    return jnp.pad(flat, (0, rows * 128 - n)).reshape(rows, 128)


def _unpack(buf, shapes):
    flat = buf.reshape(-1)
    out, off = [], 0
    for s in shapes:
        n = int(np.prod(s))
        out.append(flat[off:off + n].reshape(s))
        off += n
    return out


WEIGHTS = ["meta_tokens", "ffn1_pre_g", "ffn1_post_g", "ffn1_w_gate", "ffn1_w_up", "ffn1_w_down", "mix_pre_g", "w_in",
           "na_rpb", "s5_lam_re", "s5_lam_im", "s5_log_dt", "s5_b_re", "s5_b_im", "s5_c_re", "s5_c_im", "s5_d",
           "s5_w_glu", "s5_b_glu", "na_out_g", "s5_out_g", "w_out", "mix_post_g", "ffn2_pre_g", "ffn2_post_g",
           "ffn2_w_gate", "ffn2_w_up", "ffn2_w_down", "final_g"]
BIG = ["ffn1_w_gate", "ffn1_w_up", "ffn1_w_down", "w_in", "s5_w_glu", "w_out", "ffn2_w_gate", "ffn2_w_up",
       "ffn2_w_down"]
TRANSPOSED = ["ffn1_w_gate", "ffn1_w_up", "ffn2_w_gate", "ffn2_w_up"]
GAINS = ["ffn1_pre_g", "ffn1_post_g", "mix_pre_g", "s5_d", "s5_b_glu", "na_out_g", "s5_out_g", "mix_post_g",
         "ffn2_pre_g", "ffn2_post_g", "final_g"]
SMALL = [n for n in WEIGHTS if n not in BIG]


def kernel(*args):
    names = ["x"] + WEIGHTS + ["loss_target"] + ["m_" + n for n in WEIGHTS] + ["v_" + n for n in WEIGHTS]
    assert len(args) == len(names)
    given = dict(zip(names, args))
    x_pos, y_pos, c_pos = _mesh_pos()
    k_pos = 2 * x_pos + y_pos
    c_arr = jnp.reshape(c_pos, (1,)).astype(jnp.int32)
    kc_arr = jnp.stack([k_pos, c_pos]).astype(jnp.int32)

    def piece(name, a):
        return a[0].T if name in TRANSPOSED else a[0]

    def unpiece(name, a):
        return a.T[None] if name in TRANSPOSED else a[None]

    bufs = [_own_half_buffer(piece(n, given[n]), k_pos, c_pos, BF16) for n in BIG]
    bufs.append(_own_half_buffer(given["meta_tokens"], k_pos, c_pos, F32))
    gathered = _all_gather(bufs)
    w = dict(zip(BIG, gathered[:-1]))
    meta = gathered[-1].transpose(1, 0, 2).reshape(N_META, D)

    gains = {n: given[n] for n in GAINS}
    s5 = {n: given["s5_" + n][0] for n in ["lam_re", "lam_im", "log_dt", "b_re", "b_im", "c_re", "c_im"]}
    loss, dh0, big, small = _local_step(given["x"][0], given["loss_target"][0], meta, gains, w, s5,
                                        given["na_rpb"][0])
    loss = lax.psum(loss, ("x", "y", "c"))
    n_tok = given["x"].shape[1]
    grad_x = dh0[N_META:N_META + n_tok][None]

    grads = [big[n] for n in BIG]
    recv = _pair_exchange(grads)
    sums = [_chip_sum("chip_sum_" + n, g, r, c_arr) for n, g, r in zip(BIG, grads, recv)]
    recv3 = _chip_scatter(sums)
    totals = [_total_sum("total_sum_" + n, s, r, kc_arr) for n, s, r in zip(BIG, sums, recv3)]
    pieces = dict(zip(BIG, _pair_assemble(totals)))

    small["meta_tokens"] = dh0[:N_META]
    red = _unpack(_small_allreduce(_pack([small[n] for n in SMALL])), [small[n].shape for n in SMALL])
    small = dict(zip(SMALL, red))
    mc = D // N_CHIP
    small["meta_tokens"] = lax.dynamic_slice_in_dim(small["meta_tokens"], k_pos * mc, mc, 1)

    out_g, out_d, out_m, out_v = {}, {}, {}, {}
    for n in BIG:
        g2 = pieces[n]
        d2, m2, v2 = _adamw("adamw_" + n, piece(n, given[n]), g2, piece(n, given["m_" + n]),
                            piece(n, given["v_" + n]))
        out_g[n], out_d[n], out_m[n], out_v[n] = (unpiece(n, t) for t in (g2, d2, m2, v2))
    shapes = [given[n].shape for n in SMALL]
    gs = [small[n].reshape(given[n].shape) for n in SMALL]
    d2, m2, v2 = _adamw("adamw_small", _pack([given[n] for n in SMALL]), _pack(gs),
                        _pack([given["m_" + n] for n in SMALL]), _pack([given["v_" + n] for n in SMALL]))
    for n, g, dd, mm, vv in zip(SMALL, gs, _unpack(d2, shapes), _unpack(m2, shapes), _unpack(v2, shapes)):
        out_g[n], out_d[n], out_m[n], out_v[n] = g, dd, mm, vv
    return (loss, grad_x, *[out_g[n] for n in WEIGHTS], *[out_d[n] for n in WEIGHTS],
            *[out_m[n] for n in WEIGHTS], *[out_v[n] for n in WEIGHTS])
```

```python
import functools
import math

import numpy as np
import jax
import jax.numpy as jnp
from jax import lax
from jax.experimental import pallas as pl
from jax.experimental.pallas import tpu as pltpu

F32 = jnp.float32
BF16 = jnp.bfloat16

D = 1024
N_META = 16
GRID_W = 64
NA_W = 512
S5_W = 512
HEAD_DIM = 64
N_HEADS = 8
KH = 8
KW = 16
S5_G = 32
S5_P = 64
S5_H = 16
N_BUNDLE = 4
FF = 2816
N_CHIP = 4
FC = FF // N_CHIP
EPS = 1e-6
NEG_INF = -1e30
Q_ROWS = 4
K_ROWS = 12
QB = Q_ROWS * GRID_W
KB = K_ROWS * GRID_W
SCAN_CHUNK = 256

ADAM_LR = 0.001
ADAM_B1 = 0.9
ADAM_B2 = 0.999
ADAM_EPS = 1e-08
ADAM_WD = 0.01
ADAM_STEP = 10

NT = (((1,), (1,)), ((), ()))
TN = (((0,), (0,)), ((), ()))
MESH_ID = pl.DeviceIdType.MESH


def _cp(sem=None, vmem_mb=None):
    kw = {}
    if sem is not None:
        kw["dimension_semantics"] = sem
    if vmem_mb is not None:
        kw["vmem_limit_bytes"] = vmem_mb << 20
    return pltpu.CompilerParams(**kw)


def _full(shape):
    n = len(shape)
    return pl.BlockSpec(shape, lambda *_: (0,) * n)


def _rows(tm, w):
    return pl.BlockSpec((tm, w), lambda i: (i, 0))


ANY = pl.BlockSpec(memory_space=pl.ANY)


def _rms(x, g):
    r = lax.rsqrt(jnp.mean(x * x, axis=-1, keepdims=True) + EPS)
    return x * r * g


def _rms_bwd(x, g, dy):
    r = lax.rsqrt(jnp.mean(x * x, axis=-1, keepdims=True) + EPS)
    xh = x * r
    dg = jnp.sum(dy * xh, axis=0, keepdims=True)
    dyg = dy * g
    dx = r * (dyg - xh * jnp.mean(dyg * xh, axis=-1, keepdims=True))
    return dx, dg


def _dot(a, b):
    return jnp.dot(a, b, preferred_element_type=F32)


def _dg(a, b, dims):
    return lax.dot_general(a, b, dims, preferred_element_type=F32)


def _ffn_fwd(name, h, g_pre, g_post, wg, wu, wd, tm, comm=None, bounds=()):
    tp = h.shape[0]
    nt = tp // tm

    def body(h_ref, gp_ref, gq_ref, wg_ref, wu_ref, wd_ref, hn_ref, gate_ref, up_ref, f_ref, xn_s, acc_s):
        c = pl.program_id(1)

        @pl.when(c == 0)
        def _():
            xn_s[...] = _rms(h_ref[...], gp_ref[...]).astype(BF16)
            acc_s[...] = jnp.zeros_like(acc_s)

        xn = xn_s[...]
        gate = _dg(xn, wg_ref[0], NT)
        up = _dg(xn, wu_ref[0], NT)
        gate_ref[0] = gate
        up_ref[0] = up
        act = (gate * jax.nn.sigmoid(gate) * up).astype(BF16)
        acc_s[...] += _dot(act, wd_ref[0])

        @pl.when(c == N_CHIP - 1)
        def _():
            f = acc_s[...]
            f_ref[...] = f
            hn_ref[...] = h_ref[...] + 0.5 * _rms(f, gq_ref[...])

    return _call(
        body, comm, bounds, (h, g_pre, g_post, wg, wu, wd), name=name, grid=(nt, N_CHIP),
        in_specs=[pl.BlockSpec((tm, D), lambda i, c: (i, 0)), _full((1, D)), _full((1, D))] +
                 [pl.BlockSpec((1, FC, D), lambda i, c: (c, 0, 0))] * 3,
        out_specs=[pl.BlockSpec((tm, D), lambda i, c: (i, 0)),
                   pl.BlockSpec((1, tm, FC), lambda i, c: (c, i, 0)),
                   pl.BlockSpec((1, tm, FC), lambda i, c: (c, i, 0)),
                   pl.BlockSpec((tm, D), lambda i, c: (i, 0))],
        out_shape=[jax.ShapeDtypeStruct((tp, D), F32), jax.ShapeDtypeStruct((N_CHIP, tp, FC), F32),
                   jax.ShapeDtypeStruct((N_CHIP, tp, FC), F32), jax.ShapeDtypeStruct((tp, D), F32)],
        scratch_shapes=[pltpu.VMEM((tm, D), BF16), pltpu.VMEM((tm, D), F32)],
        compiler_params=_cp(("arbitrary", "arbitrary"), 48))


def _ffn_bwd(name, h, g_pre, df, gate, up, wg, wu, wd, tm):
    tp = h.shape[0]
    nt = tp // tm
    rh = FC // 2

    def body(h_ref, gp_ref, df_ref, gate_ref, up_ref, wg_ref, wu_ref, wd_ref,
             dwg_ref, dwu_ref, dwd_ref, dxn_ref, rg_ref, ru_ref, rd_ref, ag, au, ad, send_sems, recv_sems):
        c = pl.program_id(0)
        i = pl.program_id(1)

        def to_sibling(a, piece):
            x, y, core = _mesh_pos()
            dw_ref, r_ref = ((dwg_ref, rg_ref), (dwu_ref, ru_ref), (dwd_ref, rd_ref))[a]
            return _remote(dw_ref.at[piece, pl.ds((1 - core) * rh, rh), :], r_ref.at[piece], send_sems, recv_sems,
                           3 * piece + a, (x, y, 1 - core))

        @pl.when(i == 0)
        def _():
            ag[...] = jnp.zeros_like(ag)
            au[...] = jnp.zeros_like(au)
            ad[...] = jnp.zeros_like(ad)

        xn = _rms(h_ref[...], gp_ref[...]).astype(BF16)
        dfb = df_ref[...].astype(BF16)
        gt = gate_ref[0]
        u = up_ref[0]
        sg = jax.nn.sigmoid(gt)
        si = gt * sg
        act = (si * u).astype(BF16)
        dact = _dg(dfb, wd_ref[0], NT)
        ad[...] += _dg(act, dfb, TN)
        dgate = (dact * u * (sg * (1.0 + gt * (1.0 - sg)))).astype(BF16)
        dup = (dact * si).astype(BF16)
        ag[...] += _dg(dgate, xn, TN)
        au[...] += _dg(dup, xn, TN)
        dxn_ref[0] = _dot(dgate, wg_ref[0]) + _dot(dup, wu_ref[0])

        @pl.when(i == nt - 1)
        def _():
            pltpu.sync_copy(ag, dwg_ref.at[c])
            pltpu.sync_copy(au, dwu_ref.at[c])
            pltpu.sync_copy(ad, dwd_ref.at[c])
            for a in range(3):
                to_sibling(a, c).start()

        @pl.when((c == N_CHIP - 1) & (i == nt - 1))
        def _():
            for piece in range(N_CHIP):
                for a in range(3):
                    to_sibling(a, piece).wait()

    return pl.pallas_call(
        body, name=name, grid=(N_CHIP, nt),
        in_specs=[pl.BlockSpec((tm, D), lambda c, i: (i, 0)), _full((1, D)),
                  pl.BlockSpec((tm, D), lambda c, i: (i, 0)),
                  pl.BlockSpec((1, tm, FC), lambda c, i: (c, i, 0)),
                  pl.BlockSpec((1, tm, FC), lambda c, i: (c, i, 0))] +
                 [pl.BlockSpec((1, FC, D), lambda c, i: (c, 0, 0))] * 3,
        out_specs=[ANY, ANY, ANY, pl.BlockSpec((1, tm, D), lambda c, i: (c, i, 0)), ANY, ANY, ANY],
        out_shape=[jax.ShapeDtypeStruct((N_CHIP, FC, D), F32)] * 3 + [jax.ShapeDtypeStruct((N_CHIP, tp, D), F32)] +
                  [jax.ShapeDtypeStruct((N_CHIP, rh, D), F32)] * 3,
        scratch_shapes=[pltpu.VMEM((FC, D), F32)] * 3 +
                       [pltpu.SemaphoreType.DMA((3 * N_CHIP,)), pltpu.SemaphoreType.DMA((3 * N_CHIP,))],
        compiler_params=_cp(("arbitrary", "arbitrary"), 56),
    )(h, g_pre, df, gate, up, wg, wu, wd)


def _ffn_pre_bwd(name, dh, dxn_part, h, g_pre, tm, comm=None, bounds=()):
    tp = h.shape[0]
    nt = tp // tm

    def body(dh_ref, dxn_ref, h_ref, gp_ref, out_ref, dg_ref):
        i = pl.program_id(0)
        dxn = (dxn_ref[0] + dxn_ref[1]) + (dxn_ref[2] + dxn_ref[3])
        dx, dg = _rms_bwd(h_ref[...], gp_ref[...], dxn)
        out_ref[...] = dh_ref[...] + dx

        @pl.when(i == 0)
        def _():
            dg_ref[...] = jnp.zeros_like(dg_ref)

        dg_ref[...] += dg

    return _call(
        body, comm, bounds, (dh, dxn_part, h, g_pre), name=name, grid=(nt,),
        in_specs=[_rows(tm, D), pl.BlockSpec((N_CHIP, tm, D), lambda i: (0, i, 0)), _rows(tm, D), _full((1, D))],
        out_specs=[_rows(tm, D), _full((1, D))],
        out_shape=[jax.ShapeDtypeStruct((tp, D), F32), jax.ShapeDtypeStruct((1, D), F32)],
        compiler_params=_cp(("arbitrary",), 48))


def _mix_in(h, g, w_in, tm):
    tp = h.shape[0]

    def body(h_ref, g_ref, w_ref, q_ref, k_ref, v_ref, u_ref):
        a = _rms(h_ref[...], g_ref[...]).astype(BF16)
        q_ref[...] = _dot(a, w_ref[0]).astype(BF16)
        k_ref[...] = _dot(a, w_ref[1]).astype(BF16)
        v_ref[...] = _dot(a, w_ref[2]).astype(BF16)
        u_ref[...] = _dot(a, w_ref[3])

    return pl.pallas_call(
        body, name="mix_in", grid=(tp // tm,),
        in_specs=[_rows(tm, D), _full((1, D)), _full((N_CHIP, D, NA_W))],
        out_specs=[_rows(tm, NA_W)] * 4,
        out_shape=[jax.ShapeDtypeStruct((tp, NA_W), BF16)] * 3 + [jax.ShapeDtypeStruct((tp, S5_W), F32)],
        compiler_params=_cp(("arbitrary",), 40),
    )(h, g, w_in)


def _gelu(x):
    return jax.nn.gelu(x, approximate=True)


def _gelu_grad(x):
    k = math.sqrt(2.0 / math.pi)
    t = jnp.tanh(k * (x + 0.044715 * x * x * x))
    return 0.5 * (1.0 + t) + 0.5 * x * (1.0 - t * t) * k * (1.0 + 3.0 * 0.044715 * x * x)


def _mix_out(o_na, y_pre, h, w_glu, b_glu, g_na, g_s5, w_out, g_post, tm, comm=None, bounds=()):
    tp = h.shape[0]

    def body(ona_ref, yp_ref, h_ref, wglu_ref, bglu_ref, gna_ref, gs5_ref, wout_ref, gpost_ref, hn_ref, mix_ref):
        y = _gelu(yp_ref[...])
        z = _dot(y.astype(BF16), wglu_ref[...]) + bglu_ref[...]
        o_s5 = y * jax.nn.sigmoid(z)
        n1 = _rms(ona_ref[...], gna_ref[...]).astype(BF16)
        n2 = _rms(o_s5, gs5_ref[...]).astype(BF16)
        mix = _dot(n1, wout_ref[0:NA_W, :]) + _dot(n2, wout_ref[NA_W:, :])
        mix_ref[...] = mix
        hn_ref[...] = h_ref[...] + _rms(mix, gpost_ref[...])

    return _call(
        body, comm, bounds, (o_na, y_pre, h, w_glu, b_glu, g_na, g_s5, w_out, g_post), name="mix_out",
        grid=(tp // tm,),
        in_specs=[_rows(tm, NA_W), _rows(tm, S5_W), _rows(tm, D), _full((S5_W, S5_W)), _full((1, S5_W)),
                  _full((1, NA_W)), _full((1, S5_W)), _full((D, D)), _full((1, D))],
        out_specs=[_rows(tm, D), _rows(tm, D)],
        out_shape=[jax.ShapeDtypeStruct((tp, D), F32)] * 2,
        compiler_params=_cp(("arbitrary",), 40))


def _mix_out_bwd(dh, mix, o_na, y_pre, w_glu, b_glu, g_na, g_s5, w_out, g_post, tm):
    tp = dh.shape[0]
    nt = tp // tm

    def body(dh_ref, mix_ref, ona_ref, yp_ref, wglu_ref, bglu_ref, gna_ref, gs5_ref, wout_ref, gpost_ref,
             dona_ref, dyp_ref, dwout_ref, dwglu_ref, dgpost_ref, dgna_ref, dgs5_ref, dbglu_ref, a_out, a_glu):
        i = pl.program_id(0)

        @pl.when(i == 0)
        def _():
            a_out[...] = jnp.zeros_like(a_out)
            a_glu[...] = jnp.zeros_like(a_glu)
            dgpost_ref[...] = jnp.zeros_like(dgpost_ref)
            dgna_ref[...] = jnp.zeros_like(dgna_ref)
            dgs5_ref[...] = jnp.zeros_like(dgs5_ref)
            dbglu_ref[...] = jnp.zeros_like(dbglu_ref)

        dmix, dgpost = _rms_bwd(mix_ref[...], gpost_ref[...], dh_ref[...])
        dgpost_ref[...] += dgpost
        yp = yp_ref[...]
        y = _gelu(yp)
        yb = y.astype(BF16)
        z = _dot(yb, wglu_ref[...]) + bglu_ref[...]
        sg = jax.nn.sigmoid(z)
        o_s5 = y * sg
        o_na = ona_ref[...]
        n1 = _rms(o_na, gna_ref[...]).astype(BF16)
        n2 = _rms(o_s5, gs5_ref[...]).astype(BF16)
        dmb = dmix.astype(BF16)
        a_out[0:NA_W, :] += _dg(n1, dmb, TN)
        a_out[NA_W:, :] += _dg(n2, dmb, TN)
        dn1 = _dg(dmb, wout_ref[0:NA_W, :], NT)
        dn2 = _dg(dmb, wout_ref[NA_W:, :], NT)
        dona, dgna = _rms_bwd(o_na, gna_ref[...], dn1)
        dona_ref[...] = dona
        dgna_ref[...] += dgna
        dos5, dgs5 = _rms_bwd(o_s5, gs5_ref[...], dn2)
        dgs5_ref[...] += dgs5
        dz = dos5 * y * (sg * (1.0 - sg))
        dbglu_ref[...] += jnp.sum(dz, axis=0, keepdims=True)
        dzb = dz.astype(BF16)
        a_glu[...] += _dg(yb, dzb, TN)
        dy = dos5 * sg + _dg(dzb, wglu_ref[...], NT)
        dyp_ref[...] = dy * _gelu_grad(yp)

        @pl.when(i == nt - 1)
        def _():
            pltpu.sync_copy(a_out, dwout_ref)
            pltpu.sync_copy(a_glu, dwglu_ref)

    return pl.pallas_call(
        body, name="mix_out_bwd", grid=(nt,),
        in_specs=[_rows(tm, D), _rows(tm, D), _rows(tm, NA_W), _rows(tm, S5_W), _full((S5_W, S5_W)),
                  _full((1, S5_W)), _full((1, NA_W)), _full((1, S5_W)), _full((D, D)), _full((1, D))],
        out_specs=[_rows(tm, NA_W), _rows(tm, S5_W), ANY, ANY, _full((1, D)), _full((1, NA_W)),
                   _full((1, S5_W)), _full((1, S5_W))],
        out_shape=[jax.ShapeDtypeStruct((tp, NA_W), F32), jax.ShapeDtypeStruct((tp, S5_W), F32),
                   jax.ShapeDtypeStruct((D, D), F32), jax.ShapeDtypeStruct((S5_W, S5_W), F32),
                   jax.ShapeDtypeStruct((1, D), F32), jax.ShapeDtypeStruct((1, NA_W), F32),
                   jax.ShapeDtypeStruct((1, S5_W), F32), jax.ShapeDtypeStruct((1, S5_W), F32)],
        scratch_shapes=[pltpu.VMEM((D, D), F32), pltpu.VMEM((S5_W, S5_W), F32)],
        compiler_params=_cp(("arbitrary",), 48),
    )(dh, mix, o_na, y_pre, w_glu, b_glu, g_na, g_s5, w_out, g_post)


def _mix_in_bwd(dq, dk, dv, du, h, g, w_in, dh, f1, g_post1, tm, comm=None, bounds=()):
    tp = h.shape[0]
    nt = tp // tm

    def body(dq_ref, dk_ref, dv_ref, du_ref, h_ref, g_ref, w_ref, dh_ref, f_ref, gq_ref,
             dh1_ref, df_ref, dw_ref, dg_ref, dgq_ref, acc):
        i = pl.program_id(0)

        @pl.when(i == 0)
        def _():
            acc[...] = jnp.zeros_like(acc)
            dg_ref[...] = jnp.zeros_like(dg_ref)
            dgq_ref[...] = jnp.zeros_like(dgq_ref)

        x = h_ref[...]
        a = _rms(x, g_ref[...]).astype(BF16)
        da = jnp.zeros((tm, D), F32)
        for j, r in enumerate((dq_ref, dk_ref, dv_ref, du_ref)):
            dp = r[...].astype(BF16)
            da = da + _dg(dp, w_ref[j], NT)
            acc[j] += _dg(a, dp, TN)
        dx, dg = _rms_bwd(x, g_ref[...], da)
        dh1 = dh_ref[...] + dx
        dh1_ref[...] = dh1
        dg_ref[...] += dg
        df, dgq = _rms_bwd(f_ref[...], gq_ref[...], 0.5 * dh1)
        df_ref[...] = df
        dgq_ref[...] += dgq

        @pl.when(i == nt - 1)
        def _():
            pltpu.sync_copy(acc, dw_ref)

    return _call(
        body, comm, bounds, (dq, dk, dv, du, h, g, w_in, dh, f1, g_post1), name="mix_in_bwd", grid=(nt,),
        in_specs=[_rows(tm, NA_W)] * 4 + [_rows(tm, D), _full((1, D)), _full((N_CHIP, D, NA_W)), _rows(tm, D),
                                         _rows(tm, D), _full((1, D))],
        out_specs=[_rows(tm, D), _rows(tm, D), ANY, _full((1, D)), _full((1, D))],
        out_shape=[jax.ShapeDtypeStruct((tp, D), F32), jax.ShapeDtypeStruct((tp, D), F32),
                   jax.ShapeDtypeStruct((N_CHIP, D, NA_W), F32), jax.ShapeDtypeStruct((1, D), F32),
                   jax.ShapeDtypeStruct((1, D), F32)],
        scratch_shapes=[pltpu.VMEM((N_CHIP, D, NA_W), F32)],
        compiler_params=_cp(("arbitrary",), 48))


def _final_loss(h, g_final, target, f2, g_post2, n_tok, tm):
    tp = h.shape[0]

    def body(h_ref, g_ref, t_ref, f_ref, gq_ref, dh_ref, df_ref, loss_ref, dg_ref, dgq_ref):
        i = pl.program_id(0)

        @pl.when(i == 0)
        def _():
            loss_ref[...] = jnp.zeros_like(loss_ref)
            dg_ref[...] = jnp.zeros_like(dg_ref)
            dgq_ref[...] = jnp.zeros_like(dgq_ref)

        x = h_ref[...]
        y = _rms(x, g_ref[...])
        row = i * tm + lax.broadcasted_iota(jnp.int32, (tm, 1), 0)
        valid = (row >= N_META) & (row < N_META + n_tok)
        e = jnp.where(valid, y - t_ref[...], 0.0)
        loss_ref[...] += 0.5 * jnp.sum(jnp.mean(e * e, axis=-1, keepdims=True), axis=0, keepdims=True)
        dx, dg = _rms_bwd(x, g_ref[...], e * (1.0 / D))
        dh_ref[...] = dx
        dg_ref[...] += dg
        df, dgq = _rms_bwd(f_ref[...], gq_ref[...], 0.5 * dx)
        df_ref[...] = df
        dgq_ref[...] += dgq

    return pl.pallas_call(
        body, name="final_loss", grid=(tp // tm,),
        in_specs=[_rows(tm, D), _full((1, D)), _rows(tm, D), _rows(tm, D), _full((1, D))],
        out_specs=[_rows(tm, D), _rows(tm, D), _full((1, 1)), _full((1, D)), _full((1, D))],
        out_shape=[jax.ShapeDtypeStruct((tp, D), F32), jax.ShapeDtypeStruct((tp, D), F32),
                   jax.ShapeDtypeStruct((1, 1), F32), jax.ShapeDtypeStruct((1, D), F32),
                   jax.ShapeDtypeStruct((1, D), F32)],
        compiler_params=_cp(("arbitrary",), 40),
    )(h, g_final, target, f2, g_post2)


def _na_patterns(n_rows):
    pats = []
    for kind in range(3):
        pat = [[-1] * K_ROWS for _ in range(Q_ROWS)]
        for i in range(Q_ROWS):
            for jj in range(K_ROWS):
                if kind == 0 and jj < KH:
                    pat[i][jj] = jj - i + KH - 1
                elif kind == 1 and i <= jj < i + KH:
                    pat[i][jj] = jj - i + 3
                elif kind == 2 and K_ROWS - KH <= jj:
                    pat[i][jj] = jj - i - 1
        pats.append(pat)
    return pats


def _diag_onehot():
    q = np.arange(GRID_W)[:, None]
    kc = np.arange(GRID_W)[None, :]
    start = np.clip(q - KW // 2, 0, GRID_W - KW)
    col_in = (kc >= start) & (kc < start + KW)
    e = np.zeros((32, GRID_W, GRID_W), np.float32)
    for d in range(2 * KW - 1):
        e[d] = ((kc - q + KW - 1) == d) & col_in
    return e.reshape(32, GRID_W * GRID_W), col_in


def _rpb_expand(rpb2, e):
    def body(r_ref, e_ref, o_ref):
        o_ref[...] = jnp.dot(r_ref[...], e_ref[...], preferred_element_type=F32, precision=lax.Precision.HIGHEST)

    return pl.pallas_call(
        body, name="rpb_expand", out_shape=jax.ShapeDtypeStruct((rpb2.shape[0], e.shape[1]), F32),
        in_specs=[pl.BlockSpec(memory_space=pltpu.VMEM)] * 2, out_specs=pl.BlockSpec(memory_space=pltpu.VMEM),
    )(rpb2, e)


def _rpb_collapse(dtb2, et):
    def body(d_ref, e_ref, o_ref):
        o_ref[...] = jnp.dot(d_ref[...], e_ref[...], preferred_element_type=F32, precision=lax.Precision.HIGHEST)

    return pl.pallas_call(
        body, name="rpb_collapse", out_shape=jax.ShapeDtypeStruct((dtb2.shape[0], et.shape[1]), F32),
        in_specs=[pl.BlockSpec(memory_space=pltpu.VMEM)] * 2, out_specs=pl.BlockSpec(memory_space=pltpu.VMEM),
    )(dtb2, et)


def _bias_tables(rpb, n_rows):
    e, col_in = _diag_onehot()
    rpb2 = jnp.pad(rpb.reshape(N_HEADS * (2 * KH - 1), 2 * KW - 1), ((0, 0), (0, 1)))
    tb = _rpb_expand(rpb2, jnp.asarray(e)).reshape(N_HEADS, 2 * KH - 1, GRID_W, GRID_W)
    tb = jnp.where(jnp.asarray(col_in)[None, None], tb, NEG_INF)
    neg = jnp.full((N_HEADS, GRID_W, GRID_W), NEG_INF, F32)
    tabs = []
    for pat in _na_patterns(n_rows):
        rows = [jnp.concatenate([tb[:, dr] if dr >= 0 else neg for dr in pat[i]], axis=-1) for i in range(Q_ROWS)]
        tabs.append(jnp.concatenate(rows, axis=1))
    return jnp.stack(tabs)


def _attn_geometry(n_tok):
    n_rows = n_tok // GRID_W
    assert n_rows % Q_ROWS == 0 and n_rows >= K_ROWS
    return n_rows, n_rows // Q_ROWS


def _attn_probs(qh, kh, kmh, bias, scale):
    s = _dg(qh, kh, NT) * scale + bias
    sm = _dg(qh, kmh, NT) * scale
    m = jnp.maximum(jnp.max(s, axis=-1, keepdims=True), jnp.max(sm, axis=-1, keepdims=True))
    p = jnp.exp(s - m)
    pm = jnp.exp(sm - m)
    inv = 1.0 / (jnp.sum(p, axis=-1, keepdims=True) + jnp.sum(pm, axis=-1, keepdims=True))
    return p * inv, pm * inv


def _meta_probs(qmh, kmh, scale):
    s = _dg(qmh, kmh, NT) * scale
    p = jnp.exp(s - jnp.max(s, axis=-1, keepdims=True))
    return p / jnp.sum(p, axis=-1, keepdims=True)


def _step_rows(r, n_rows):
    q0 = pl.multiple_of(N_META + r * QB, 16)
    k0 = pl.multiple_of(N_META + jnp.clip(Q_ROWS * r - (K_ROWS - KH), 0, n_rows - K_ROWS) * GRID_W, 16)
    return q0, k0


def _attn_fwd(q, k, v, bias, n_tok, comm=None, bounds=()):
    tp = q.shape[0]
    n_rows, n_steps = _attn_geometry(n_tok)
    scale = HEAD_DIM ** -0.5

    def body(q_ref, k_ref, v_ref, b_ref, o_ref):
        r = pl.program_id(1)
        km = k_ref[0:N_META, :]
        vm = v_ref[0:N_META, :]

        @pl.when(r == 0)
        def _():
            qm = q_ref[0:N_META, :]
            outs = []
            for hh in range(2):
                sl = slice(hh * HEAD_DIM, (hh + 1) * HEAD_DIM)
                p = _meta_probs(qm[:, sl], km[:, sl], scale)
                outs.append(_dot(p.astype(BF16), vm[:, sl]))
            o_ref[0:N_META, :] = jnp.concatenate(outs, axis=1)
            o_ref[N_META + n_tok:, :] = jnp.zeros((tp - N_META - n_tok, 2 * HEAD_DIM), F32)

        q0, k0 = _step_rows(r, n_rows)
        qb = q_ref[pl.ds(q0, QB), :]
        kb = k_ref[pl.ds(k0, KB), :]
        vb = v_ref[pl.ds(k0, KB), :]
        outs = []
        for hh in range(2):
            sl = slice(hh * HEAD_DIM, (hh + 1) * HEAD_DIM)
            p, pm = _attn_probs(qb[:, sl], kb[:, sl], km[:, sl], b_ref[0, hh], scale)
            outs.append(_dot(p.astype(BF16), vb[:, sl]) + _dot(pm.astype(BF16), vm[:, sl]))
        o_ref[pl.ds(q0, QB), :] = jnp.concatenate(outs, axis=1)

    def bias_map(hp, r):
        return (jnp.where(r == 0, 0, jnp.where(r == n_steps - 1, 2, 1)), hp, 0, 0)

    col = pl.BlockSpec((tp, 2 * HEAD_DIM), lambda hp, r: (0, hp))
    return _call(
        body, comm, bounds, (q, k, v, bias), name="attn_fwd", grid=(N_HEADS // 2, n_steps),
        in_specs=[col, col, col, pl.BlockSpec((1, 2, QB, KB), bias_map)],
        out_specs=[col], out_shape=[jax.ShapeDtypeStruct((tp, NA_W), F32)],
        compiler_params=_cp(("arbitrary", "arbitrary"), 40))


def _attn_bwd(q, k, v, bias, do, n_tok, comm=None, bounds=()):
    tp = q.shape[0]
    n_rows, n_steps = _attn_geometry(n_tok)
    scale = HEAD_DIM ** -0.5
    pats = _na_patterns(n_rows)

    def body(q_ref, k_ref, v_ref, b_ref, do_ref, dq_ref, dk_ref, dv_ref, dtb_ref):
        r = pl.program_id(1)
        km = k_ref[0:N_META, :]
        vm = v_ref[0:N_META, :]

        @pl.when(r == 0)
        def _():
            dk_ref[...] = jnp.zeros_like(dk_ref)
            dv_ref[...] = jnp.zeros_like(dv_ref)
            dtb_ref[...] = jnp.zeros_like(dtb_ref)
            dq_ref[N_META + n_tok:, :] = jnp.zeros((tp - N_META - n_tok, 2 * HEAD_DIM), F32)
            qm = q_ref[0:N_META, :]
            dom = do_ref[0:N_META, :].astype(BF16)
            dqs, dks, dvs = [], [], []
            for hh in range(2):
                sl = slice(hh * HEAD_DIM, (hh + 1) * HEAD_DIM)
                p = _meta_probs(qm[:, sl], km[:, sl], scale)
                dp = _dg(dom[:, sl], vm[:, sl], NT)
                ds = (p * (dp - jnp.sum(dp * p, axis=-1, keepdims=True))).astype(BF16)
                dvs.append(_dg(p.astype(BF16), dom[:, sl], TN))
                dqs.append(_dot(ds, km[:, sl]) * scale)
                dks.append(_dg(ds, qm[:, sl], TN) * scale)
            dq_ref[0:N_META, :] = jnp.concatenate(dqs, axis=1)
            dk_ref[0:N_META, :] += jnp.concatenate(dks, axis=1)
            dv_ref[0:N_META, :] += jnp.concatenate(dvs, axis=1)

        q0, k0 = _step_rows(r, n_rows)
        qb = q_ref[pl.ds(q0, QB), :]
        kb = k_ref[pl.ds(k0, KB), :]
        vb = v_ref[pl.ds(k0, KB), :]
        dob = do_ref[pl.ds(q0, QB), :].astype(BF16)
        dqs, dks, dvs, dkms, dvms, dss = [], [], [], [], [], []
        for hh in range(2):
            sl = slice(hh * HEAD_DIM, (hh + 1) * HEAD_DIM)
            qh, kh, vh, kmh, vmh, doh = qb[:, sl], kb[:, sl], vb[:, sl], km[:, sl], vm[:, sl], dob[:, sl]
            p, pm = _attn_probs(qh, kh, kmh, b_ref[0, hh], scale)
            dp = _dg(doh, vh, NT)
            dpm = _dg(doh, vmh, NT)
            delta = jnp.sum(dp * p, axis=-1, keepdims=True) + jnp.sum(dpm * pm, axis=-1, keepdims=True)
            ds = p * (dp - delta)
            dsb = ds.astype(BF16)
            dsmb = (pm * (dpm - delta)).astype(BF16)
            dss.append(ds)
            dvs.append(_dg(p.astype(BF16), doh, TN))
            dvms.append(_dg(pm.astype(BF16), doh, TN))
            dqs.append((_dot(dsb, kh) + _dot(dsmb, kmh)) * scale)
            dks.append(_dg(dsb, qh, TN) * scale)
            dkms.append(_dg(dsmb, qh, TN) * scale)
        dq_ref[pl.ds(q0, QB), :] = jnp.concatenate(dqs, axis=1)
        dk_ref[pl.ds(k0, KB), :] += jnp.concatenate(dks, axis=1)
        dv_ref[pl.ds(k0, KB), :] += jnp.concatenate(dvs, axis=1)
        dk_ref[0:N_META, :] += jnp.concatenate(dkms, axis=1)
        dv_ref[0:N_META, :] += jnp.concatenate(dvms, axis=1)

        def add_bias_grad(pat):
            for hh in range(2):
                for i in range(Q_ROWS):
                    for jj in range(K_ROWS):
                        if pat[i][jj] >= 0:
                            dtb_ref[hh, pat[i][jj]] += dss[hh][i * GRID_W:(i + 1) * GRID_W,
                                                               jj * GRID_W:(jj + 1) * GRID_W]

        @pl.when(r == 0)
        def _():
            add_bias_grad(pats[0])

        @pl.when((r > 0) & (r < n_steps - 1))
        def _():
            add_bias_grad(pats[1])

        @pl.when(r == n_steps - 1)
        def _():
            add_bias_grad(pats[2])

    def bias_map(hp, r):
        return (jnp.where(r == 0, 0, jnp.where(r == n_steps - 1, 2, 1)), hp, 0, 0)

    col = pl.BlockSpec((tp, 2 * HEAD_DIM), lambda hp, r: (0, hp))
    n_dr = 2 * KH - 1
    return _call(
        body, comm, bounds, (q, k, v, bias, do), name="attn_bwd", grid=(N_HEADS // 2, n_steps),
        in_specs=[col, col, col, pl.BlockSpec((1, 2, QB, KB), bias_map), col],
        out_specs=[col, col, col, pl.BlockSpec((2, n_dr, GRID_W, GRID_W), lambda hp, r: (hp, 0, 0, 0))],
        out_shape=[jax.ShapeDtypeStruct((tp, NA_W), F32)] * 3 +
                  [jax.ShapeDtypeStruct((N_HEADS, n_dr, GRID_W, GRID_W), F32)],
        compiler_params=_cp(("arbitrary", "arbitrary"), 48))


def _expand_onehot():
    ex = np.zeros((S5_P, S5_P * S5_H), np.float32)
    for p in range(S5_P):
        ex[p, p * S5_H:(p + 1) * S5_H] = 1.0
    return ex


def _s5_disc_math(lam_re, lam_im, log_dt, b_re, b_im, ex):
    dt = jnp.exp(log_dt)
    ea = jnp.exp(lam_re * dt)
    a_re = ea * jnp.cos(lam_im * dt)
    a_im = ea * jnp.sin(lam_im * dt)
    den = lam_re * lam_re + lam_im * lam_im
    c_re = ((a_re - 1.0) * lam_re + a_im * lam_im) / den
    c_im = (a_im * lam_re - (a_re - 1.0) * lam_im) / den
    ce_re = jnp.dot(c_re, ex, preferred_element_type=F32, precision=lax.Precision.HIGHEST)
    ce_im = jnp.dot(c_im, ex, preferred_element_type=F32, precision=lax.Precision.HIGHEST)
    return a_re, a_im, ce_re * b_re - ce_im * b_im, ce_re * b_im + ce_im * b_re


def _s5_disc(lam_re, lam_im, log_dt, b_re, b_im):
    ex = jnp.asarray(_expand_onehot())
    n = lam_re.shape[0]

    def body(lr, li, ld, br, bi, ex_ref, ar, ai, bbr, bbi):
        ar[...], ai[...], bbr[...], bbi[...] = _s5_disc_math(lr[...], li[...], ld[...], br[...], bi[...], ex_ref[...])

    vm = pl.BlockSpec(memory_space=pltpu.VMEM)
    return pl.pallas_call(
        body, name="s5_disc", in_specs=[vm] * 6, out_specs=[vm] * 4,
        out_shape=[jax.ShapeDtypeStruct((n, S5_P), F32)] * 2 + [jax.ShapeDtypeStruct((n, S5_P * S5_H), F32)] * 2,
    )(lam_re, lam_im, log_dt, b_re, b_im, ex)


def _s5_disc_bwd(lam_re, lam_im, log_dt, b_re, b_im, da_re, da_im, dbb_re, dbb_im):
    ex = jnp.asarray(_expand_onehot())
    n = lam_re.shape[0]

    def body(lr, li, ld, br, bi, ex_ref, dar, dai, dbr, dbi, o_lr, o_li, o_ld, o_br, o_bi):
        e = ex_ref[...]
        _, vjp = jax.vjp(lambda a, b, c, d, f: _s5_disc_math(a, b, c, d, f, e), lr[...], li[...], ld[...], br[...], bi[...])
        o_lr[...], o_li[...], o_ld[...], o_br[...], o_bi[...] = vjp((dar[...], dai[...], dbr[...], dbi[...]))

    vm = pl.BlockSpec(memory_space=pltpu.VMEM)
    return pl.pallas_call(
        body, name="s5_disc_bwd", in_specs=[vm] * 10, out_specs=[vm] * 5,
        out_shape=[jax.ShapeDtypeStruct((n, S5_P), F32)] * 2 + [jax.ShapeDtypeStruct((n, 1), F32)] +
                  [jax.ShapeDtypeStruct((n, S5_P * S5_H), F32)] * 2,
    )(lam_re, lam_im, log_dt, b_re, b_im, ex, da_re, da_im, dbb_re, dbb_im)


def _s5_matrices(a_re, a_im, bb_re, bb_im, c_re, c_im):
    gl = S5_G // N_BUNDLE
    eye = jnp.eye(gl, dtype=F32)
    half = gl * S5_P

    def in_mat(bb):
        t = bb.reshape(2, N_BUNDLE, gl, S5_P, S5_H).transpose(0, 1, 4, 2, 3)
        m = t[:, :, None] * eye[None, None, :, None, :, None]
        return m.reshape(2, N_BUNDLE, gl * S5_H, half)

    def out_mat(c):
        t = c.reshape(2, N_BUNDLE, gl, S5_H, S5_P).transpose(0, 1, 2, 4, 3)
        m = t[:, :, :, :, None, :] * eye[None, None, :, None, :, None]
        return m.reshape(2, N_BUNDLE, half, gl * S5_H)

    a = jnp.concatenate([a_re.reshape(2, N_BUNDLE, 1, half), a_im.reshape(2, N_BUNDLE, 1, half)], axis=-1)
    bm = jnp.concatenate([in_mat(bb_re), in_mat(bb_im)], axis=-1)
    cm = jnp.concatenate([out_mat(c_re), -out_mat(c_im)], axis=-2)
    return a, bm, cm


def _scan_chunks(length):
    return [(t0, min(SCAN_CHUNK, length - t0)) for t0 in range(0, length, SCAN_CHUNK)]


def _scan(src_ref, dst_ref, prev_ref, prev_off, n_rows, a_re, a_im, carry, reverse):
    half = a_re.shape[-1]
    n_blk = n_rows // 8
    rid = lax.broadcasted_iota(jnp.int32, (8, half), 0)

    def blk(i, carry):
        xr, xi = carry
        bi = (n_blk - 1 - i) if reverse else i
        off = pl.multiple_of(bi * 8, 8)
        v = src_ref[pl.ds(off, 8), :]
        o_r = jnp.zeros((8, half), F32)
        o_i = jnp.zeros((8, half), F32)
        p_r = jnp.zeros((8, half), F32)
        p_i = jnp.zeros((8, half), F32)
        for j in (range(7, -1, -1) if reverse else range(8)):
            if prev_ref is not None:
                p_r = jnp.where(rid == j, xr, p_r)
                p_i = jnp.where(rid == j, xi, p_i)
            nr = a_re * xr - a_im * xi + v[j:j + 1, :half]
            ni = a_re * xi + a_im * xr + v[j:j + 1, half:]
            xr, xi = nr, ni
            if dst_ref is not None:
                o_r = jnp.where(rid == j, xr, o_r)
                o_i = jnp.where(rid == j, xi, o_i)
        if dst_ref is not None:
            dst_ref[pl.ds(off, 8), :] = jnp.concatenate([o_r, o_i], axis=1)
        if prev_ref is not None:
            prev_ref[pl.ds(pl.multiple_of(prev_off + off, 8), 8), :] = jnp.concatenate([p_r, p_i], axis=1)
        return xr, xi

    return lax.fori_loop(0, n_blk, blk, carry)


def _s5_fwd(u, d_skip, a, bm, cm, length, comm=None, bounds=()):
    tp = u.shape[0]
    cw = S5_W // N_BUNDLE
    sw = a.shape[-1]
    half = sw // 2
    chunks = _scan_chunks(length)

    def body(u_ref, d_ref, a_ref, bm_ref, cm_ref, y_ref, bu_s, xs_s):
        y_ref[...] = u_ref[...] * d_ref[...]
        for dr in range(2):
            a_re = a_ref[dr, 0, :, 0:half]
            a_im = a_ref[dr, 0, :, half:]
            carry = (jnp.zeros((1, half), F32), jnp.zeros((1, half), F32))
            for t0, n in (chunks if dr == 0 else chunks[::-1]):
                bu_s[0:n, :] = _dot(u_ref[t0:t0 + n, :].astype(BF16), bm_ref[dr, 0])
                carry = _scan(bu_s, xs_s, None, 0, n, a_re, a_im, carry, dr == 1)
                y_ref[t0:t0 + n, :] += _dot(xs_s[0:n, :].astype(BF16), cm_ref[dr, 0])

    return _call(
        body, comm, bounds, (u, d_skip, a, bm, cm), name="s5_fwd", grid=(N_BUNDLE,),
        in_specs=[pl.BlockSpec((tp, cw), lambda b: (0, b)), pl.BlockSpec((1, cw), lambda b: (0, b)),
                  pl.BlockSpec((2, 1, 1, sw), lambda b: (0, b, 0, 0)),
                  pl.BlockSpec((2, 1, cw, sw), lambda b: (0, b, 0, 0)),
                  pl.BlockSpec((2, 1, sw, cw), lambda b: (0, b, 0, 0))],
        out_specs=[pl.BlockSpec((tp, cw), lambda b: (0, b))],
        out_shape=[jax.ShapeDtypeStruct((tp, S5_W), F32)],
        scratch_shapes=[pltpu.VMEM((SCAN_CHUNK, sw), F32), pltpu.VMEM((SCAN_CHUNK, sw), F32)],
        compiler_params=_cp(("arbitrary",), 40))


def _s5_bwd(u, dy, d_skip, a, bm, cm, length):
    tp = u.shape[0]
    cw = S5_W // N_BUNDLE
    sw = a.shape[-1]
    half = sw // 2
    chunks = _scan_chunks(length)

    def body(u_ref, dy_ref, d_ref, a_ref, bm_ref, cm_ref, du_ref, dd_ref, dbm_ref, dcm_ref, da_ref, bu_s, g_s, xp_s):
        du_ref[...] = dy_ref[...] * d_ref[...]
        dd_ref[...] = jnp.sum(dy_ref[...] * u_ref[...], axis=0, keepdims=True)
        dbm_ref[...] = jnp.zeros_like(dbm_ref)
        dcm_ref[...] = jnp.zeros_like(dcm_ref)
        zero = (jnp.zeros((1, half), F32), jnp.zeros((1, half), F32))
        for dr in range(2):
            a_re = a_ref[dr, 0, :, 0:half]
            a_im = a_ref[dr, 0, :, half:]
            seq = chunks if dr == 0 else chunks[::-1]
            carry = zero
            for t0, n in seq:
                bu_s[0:n, :] = _dot(u_ref[t0:t0 + n, :].astype(BF16), bm_ref[dr, 0])
                carry = _scan(bu_s, None, xp_s, t0, n, a_re, a_im, carry, dr == 1)
            carry = zero
            da_r = jnp.zeros((1, half), F32)
            da_i = jnp.zeros((1, half), F32)
            for t0, n in seq[::-1]:
                ub = u_ref[t0:t0 + n, :].astype(BF16)
                dyb = dy_ref[t0:t0 + n, :].astype(BF16)
                bu_s[0:n, :] = _dg(dyb, cm_ref[dr, 0], NT)
                carry = _scan(bu_s, g_s, None, 0, n, a_re, -a_im, carry, dr == 0)
                g = g_s[0:n, :]
                gb = g.astype(BF16)
                du_ref[t0:t0 + n, :] += _dg(gb, bm_ref[dr, 0], NT)
                dbm_ref[dr, 0] += _dg(ub, gb, TN)
                xp = xp_s[t0:t0 + n, :]
                xp_r, xp_i = xp[:, 0:half], xp[:, half:]
                g_r, g_i = g[:, 0:half], g[:, half:]
                bu = _dot(ub, bm_ref[dr, 0])
                x_r = a_re * xp_r - a_im * xp_i + bu[:, 0:half]
                x_i = a_re * xp_i + a_im * xp_r + bu[:, half:]
                dcm_ref[dr, 0] += _dg(jnp.concatenate([x_r, x_i], axis=1).astype(BF16), dyb, TN)
                da_r = da_r + jnp.sum(g_r * xp_r + g_i * xp_i, axis=0, keepdims=True)
                da_i = da_i + jnp.sum(g_i * xp_r - g_r * xp_i, axis=0, keepdims=True)
            da_ref[dr, 0] = jnp.concatenate([da_r, da_i], axis=1)

    lp = -(-length // 8) * 8
    return pl.pallas_call(
        body, name="s5_bwd", grid=(N_BUNDLE,),
        in_specs=[pl.BlockSpec((tp, cw), lambda b: (0, b)), pl.BlockSpec((tp, cw), lambda b: (0, b)),
                  pl.BlockSpec((1, cw), lambda b: (0, b)),
                  pl.BlockSpec((2, 1, 1, sw), lambda b: (0, b, 0, 0)),
                  pl.BlockSpec((2, 1, cw, sw), lambda b: (0, b, 0, 0)),
                  pl.BlockSpec((2, 1, sw, cw), lambda b: (0, b, 0, 0))],
        out_specs=[pl.BlockSpec((tp, cw), lambda b: (0, b)), pl.BlockSpec((1, cw), lambda b: (0, b)),
                   pl.BlockSpec((2, 1, cw, sw), lambda b: (0, b, 0, 0)),
                   pl.BlockSpec((2, 1, sw, cw), lambda b: (0, b, 0, 0)),
                   pl.BlockSpec((2, 1, 1, sw), lambda b: (0, b, 0, 0))],
        out_shape=[jax.ShapeDtypeStruct((tp, S5_W), F32), jax.ShapeDtypeStruct((1, S5_W), F32),
                   jax.ShapeDtypeStruct((2, N_BUNDLE, cw, sw), F32), jax.ShapeDtypeStruct((2, N_BUNDLE, sw, cw), F32),
                   jax.ShapeDtypeStruct((2, N_BUNDLE, 1, sw), F32)],
        scratch_shapes=[pltpu.VMEM((SCAN_CHUNK, sw), F32), pltpu.VMEM((SCAN_CHUNK, sw), F32),
                        pltpu.VMEM((lp, sw), F32)],
        compiler_params=_cp(("arbitrary",), 48),
    )(u, dy, d_skip, a, bm, cm)


def _row_tile(tp):
    return max(tm for tm in range(16, 449, 16) if tp % tm == 0)


def _step(x, target, bufs, gains, s5, rpb, c_arr, kc_arr):
    first = ["ffn1_w_gate", "ffn1_w_up", "ffn1_w_down", "meta_tokens"]
    w = dict(zip(first, _run_comm("gather_ffn1", _gather_comm([bufs[n] for n in first]))))
    meta = w["meta_tokens"].transpose(1, 0, 2).reshape(N_META, D)
    n_tok = x.shape[0]
    length = N_META + n_tok
    tp = length + 16
    tm = _row_tile(tp)
    tmb = tm // 2
    n_rows = n_tok // GRID_W
    pad = jnp.zeros((tp - length, D), F32)
    h0 = jnp.concatenate([meta, x, pad], axis=0)
    tgt = jnp.concatenate([jnp.zeros((N_META, D), F32), target, pad], axis=0)

    n2 = 2 * S5_G
    lam_re = s5["lam_re"].reshape(n2, S5_P)
    lam_im = s5["lam_im"].reshape(n2, S5_P)
    log_dt = s5["log_dt"].reshape(n2, 1)
    b_re = s5["b_re"].reshape(n2, S5_P * S5_H)
    b_im = s5["b_im"].reshape(n2, S5_P * S5_H)
    a_re, a_im, bb_re, bb_im = _s5_disc(lam_re, lam_im, log_dt, b_re, b_im)

    def mats(a_re, a_im, bb_re, bb_im, c_re, c_im):
        return _s5_matrices(a_re.reshape(2, S5_G, S5_P), a_im.reshape(2, S5_G, S5_P),
                            bb_re.reshape(2, S5_G, S5_P * S5_H), bb_im.reshape(2, S5_G, S5_P * S5_H), c_re, c_im)

    (a_m, bm, cm), mats_vjp = jax.vjp(mats, a_re, a_im, bb_re, bb_im, s5["c_re"], s5["c_im"])
    bm16 = bm.astype(BF16)
    cm16 = cm.astype(BF16)
    bias = _bias_tables(rpb, n_rows)

    mid = ["w_in", "s5_w_glu", "w_out"]
    (h1, gate1, up1, f1), got = _ffn_fwd(
        "ffn1_fwd", h0, gains["ffn1_pre_g"], gains["ffn1_post_g"], w["ffn1_w_gate"], w["ffn1_w_up"], w["ffn1_w_down"],
        tm, _gather_comm([bufs[n] for n in mid]), (0, (tp // tm) * N_CHIP * 3 // 5))
    w.update(zip(mid, got))
    q, k, v, u = _mix_in(h1, gains["mix_pre_g"], w["w_in"], tm)
    (o_na,), (gate_ici,) = _attn_fwd(q, k, v, bias, n_tok, _gather_comm([bufs["ffn2_w_gate"]], pair=False), (0,))
    (y_pre,), (w["ffn2_w_gate"], up_ici, down_ici) = _s5_fwd(
        u, gains["s5_d"], a_m, bm16, cm16, length,
        _merge_comm(_gather_comm([gate_ici], ici=False),
                    _gather_comm([bufs["ffn2_w_up"], bufs["ffn2_w_down"]], pair=False)), (0,))
    w_glu = w["s5_w_glu"].reshape(S5_W, S5_W)
    w_out = w["w_out"].reshape(D, D)
    (h2, mix), (w["ffn2_w_up"], w["ffn2_w_down"]) = _mix_out(
        o_na, y_pre, h1, w_glu, gains["s5_b_glu"], gains["na_out_g"], gains["s5_out_g"], w_out, gains["mix_post_g"], tm,
        _gather_comm([up_ici, down_ici], ici=False), (0,))
    (h3, gate2, up2, f2), _ = _ffn_fwd("ffn2_fwd", h2, gains["ffn2_pre_g"], gains["ffn2_post_g"],
                                       w["ffn2_w_gate"], w["ffn2_w_up"], w["ffn2_w_down"], tm)
    dh3, df2, loss, dg_final, dg_post2 = _final_loss(h3, gains["final_g"], tgt, f2, gains["ffn2_post_g"], n_tok, tm)

    ffn2 = ["ffn2_w_gate", "ffn2_w_up", "ffn2_w_down"]
    ffn1 = ["ffn1_w_gate", "ffn1_w_up", "ffn1_w_down"]
    out2 = _ffn_bwd("ffn2_bwd", h2, gains["ffn2_pre_g"], df2, gate2, up2,
                    w["ffn2_w_gate"], w["ffn2_w_up"], w["ffn2_w_down"], tmb)
    dxn2 = out2[3]
    sums2 = [_chip_sum("chip_sum_" + n, g, r, c_arr) for n, g, r in zip(ffn2, out2[0:3], out2[4:7])]
    (dh2, dg_pre2), _ = _ffn_pre_bwd("ffn2_pre_bwd", dh3, dxn2, h2, gains["ffn2_pre_g"], tm)
    do_na, dy_pre, dw_out, dw_glu, dg_mpost, dg_na, dg_s5, db_glu = _mix_out_bwd(
        dh2, mix, o_na, y_pre, w_glu, gains["s5_b_glu"], gains["na_out_g"], gains["s5_out_g"], w_out,
        gains["mix_post_g"], tm)
    (dq, dk, dv, dtb), recv3 = _attn_bwd(q, k, v, bias, do_na, n_tok, _scatter_comm(sums2), (0,))
    totals2 = [_total_sum("total_sum_" + n, s, r, kc_arr) for n, s, r in zip(ffn2, sums2, recv3)]
    du, dd, dbm, dcm, da_m = _s5_bwd(u, dy_pre, gains["s5_d"], a_m, bm16, cm16, length)
    (dh1, df1, dw_in, dg_mpre, dg_post1), done2 = _mix_in_bwd(
        dq, dk, dv, du, h1, gains["mix_pre_g"], w["w_in"], dh2, f1, gains["ffn1_post_g"], tm,
        _assemble_comm(totals2), (0,))
    pieces = dict(zip(ffn2, done2))
    out1 = _ffn_bwd("ffn1_bwd", h0, gains["ffn1_pre_g"], df1, gate1, up1,
                    w["ffn1_w_gate"], w["ffn1_w_up"], w["ffn1_w_down"], tmb)
    rest = [dw_in, dw_glu.reshape(N_CHIP, S5_W // N_CHIP, S5_W), dw_out.reshape(N_CHIP, D // N_CHIP, D)]
    (dh0, dg_pre1), recv_rest = _ffn_pre_bwd("ffn1_pre_bwd", dh1, out1[3], h0, gains["ffn1_pre_g"], tm,
                                             _exchange_comm(rest), (0,))
    last = ffn1 + mid
    sums = [_chip_sum("chip_sum_" + n, g, r, c_arr)
            for n, g, r in zip(last, list(out1[0:3]) + rest, list(out1[4:7]) + list(recv_rest))]
    recv3 = _run_comm("grad_chip_scatter", _scatter_comm(sums))
    totals = [_total_sum("total_sum_" + n, s, r, kc_arr) for n, s, r in zip(last, sums, recv3)]
    pieces.update(zip(last, _run_comm("grad_pair_assemble", _assemble_comm(totals))))

    e, _ = _diag_onehot()
    n_dr = 2 * KH - 1
    drpb = _rpb_collapse(dtb.reshape(N_HEADS * n_dr, GRID_W * GRID_W), jnp.asarray(e.T))
    drpb = drpb[:, :2 * KW - 1].reshape(N_HEADS, n_dr, 2 * KW - 1)
    da_re, da_im, dbb_re, dbb_im, dc_re, dc_im = mats_vjp((da_m, dbm, dcm))
    dlam_re, dlam_im, dlog_dt, db_re, db_im = _s5_disc_bwd(lam_re, lam_im, log_dt, b_re, b_im,
                                                            da_re, da_im, dbb_re, dbb_im)

    small = {"ffn1_pre_g": dg_pre1, "ffn1_post_g": dg_post1, "mix_pre_g": dg_mpre, "na_rpb": drpb,
             "s5_lam_re": dlam_re, "s5_lam_im": dlam_im, "s5_log_dt": dlog_dt, "s5_b_re": db_re, "s5_b_im": db_im,
             "s5_c_re": dc_re, "s5_c_im": dc_im, "s5_d": dd, "s5_b_glu": db_glu, "na_out_g": dg_na,
             "s5_out_g": dg_s5, "mix_post_g": dg_mpost, "ffn2_pre_g": dg_pre2, "ffn2_post_g": dg_post2,
             "final_g": dg_final}
    return loss[0, 0], dh0, pieces, small


def _mesh_pos():
    return lax.axis_index("x"), lax.axis_index("y"), lax.axis_index("c")


def _other_chips(x, y):
    return [(1 - x, y), (x, 1 - y), (1 - x, 1 - y)]


class _Comm:
    def __init__(self, ins, out_shape, aliases, parts):
        self.ins, self.out_shape, self.aliases, self.parts = list(ins), list(out_shape), dict(aliases), list(parts)
        self.n_sems = sum(p[0] for p in parts)

    def bases(self):
        out, base = [], 0
        for n_sems, _, _ in self.parts:
            out.append(base)
            base += n_sems
        return out


def _run_comm(name, comm):
    n_i, n_o = len(comm.ins), len(comm.out_shape)

    def body(*refs):
        ins, outs = refs[:n_i], refs[n_i:n_i + n_o]
        send_sems, recv_sems = refs[n_i + n_o:]
        for base, (_, start, finish) in zip(comm.bases(), comm.parts):
            start(ins, outs, send_sems, recv_sems, base)
            finish(ins, outs, send_sems, recv_sems, base)

    return pl.pallas_call(
        body, name=name, out_shape=comm.out_shape, in_specs=[ANY] * n_i, out_specs=[ANY] * n_o,
        input_output_aliases=comm.aliases,
        scratch_shapes=[pltpu.SemaphoreType.DMA((comm.n_sems,)), pltpu.SemaphoreType.DMA((comm.n_sems,))],
    )(*comm.ins)


def _call(body, comm, bounds, args, *, name, grid, in_specs, out_specs, out_shape, scratch_shapes=(),
          compiler_params=None):
    in_specs, out_specs, out_shape, scratch_shapes = list(in_specs), list(out_specs), list(out_shape), list(scratch_shapes)
    if comm is None:
        return pl.pallas_call(body, name=name, grid=grid, in_specs=in_specs, out_specs=out_specs, out_shape=out_shape,
                              scratch_shapes=scratch_shapes, compiler_params=compiler_params)(*args), []
    n_in, n_out, n_scr = len(in_specs), len(out_specs), len(scratch_shapes)
    n_ci, n_co = len(comm.ins), len(comm.out_shape)
    n_steps = int(np.prod(grid))
    assert len(bounds) == len(comm.parts) and all(0 <= b < n_steps for b in bounds) and list(bounds) == sorted(bounds)

    def fused(*refs):
        a = n_in
        b = a + n_ci
        c = b + n_out
        d = c + n_co
        e = d + n_scr
        cargs = (refs[a:b], refs[c:d], refs[e], refs[e + 1])
        step = pl.program_id(0)
        for ax in range(1, len(grid)):
            step = step * grid[ax] + pl.program_id(ax)
        bases = comm.bases()
        for p, (_, start, finish) in enumerate(comm.parts):
            @pl.when(step == bounds[p])
            def _(p=p, start=start):
                if p > 0:
                    comm.parts[p - 1][2](*cargs, bases[p - 1])
                start(*cargs, bases[p])
        body(*(refs[:a] + refs[b:c] + refs[d:e]))

        @pl.when(step == n_steps - 1)
        def _():
            comm.parts[-1][2](*cargs, bases[-1])

    res = pl.pallas_call(
        fused, name=name, grid=grid, in_specs=in_specs + [ANY] * n_ci, out_specs=out_specs + [ANY] * n_co,
        out_shape=out_shape + comm.out_shape,
        scratch_shapes=scratch_shapes + [pltpu.SemaphoreType.DMA((comm.n_sems,)), pltpu.SemaphoreType.DMA((comm.n_sems,))],
        input_output_aliases={n_in + i: n_out + j for i, j in comm.aliases.items()},
        compiler_params=compiler_params)(*args, *comm.ins)
    return res[:n_out], res[n_out:]


def _remote(src, dst, send_sems, recv_sems, idx, to):
    return pltpu.make_async_remote_copy(src_ref=src, dst_ref=dst, send_sem=send_sems.at[idx],
                                        recv_sem=recv_sems.at[idx], device_id=to, device_id_type=MESH_ID)


def _gather_comm(bufs, ici=True, pair=True):
    n = len(bufs)

    def half(ref, k, pc):
        rh = ref.shape[1] // 2
        return ref.at[k, pl.ds(pc * rh, rh), :]

    def ici_start(ins, outs, ss, rs, base):
        x, y, c = _mesh_pos()
        for a in range(n):
            mine = half(outs[a], 2 * x + y, c)
            for j, chip in enumerate(_other_chips(x, y)):
                _remote(mine, mine, ss, rs, base + 3 * a + j, (*chip, c)).start()

    def ici_finish(ins, outs, ss, rs, base):
        x, y, c = _mesh_pos()
        for a in range(n):
            for j, chip in enumerate(_other_chips(x, y)):
                theirs = half(outs[a], 2 * chip[0] + chip[1], c)
                _remote(theirs, theirs, ss, rs, base + 3 * a + j, (*chip, c)).wait()

    def pair_copy(outs, ss, rs, base, a):
        x, y, c = _mesh_pos()
        rh = outs[a].shape[1] // 2
        held = outs[a].at[:, pl.ds(c * rh, rh), :]
        return _remote(held, held, ss, rs, base + a, (x, y, 1 - c))

    def pair_start(ins, outs, ss, rs, base):
        for a in range(n):
            pair_copy(outs, ss, rs, base, a).start()

    def pair_finish(ins, outs, ss, rs, base):
        for a in range(n):
            pair_copy(outs, ss, rs, base, a).wait()

    parts = ([(3 * n, ici_start, ici_finish)] if ici else []) + ([(n, pair_start, pair_finish)] if pair else [])
    return _Comm(bufs, [jax.ShapeDtypeStruct(b.shape, b.dtype) for b in bufs], {a: a for a in range(n)}, parts)


def _merge_comm(*comms):
    ins, shapes, aliases, subs, base = [], [], {}, [], 0
    for cm in comms:
        (n_sems, start, finish), = cm.parts
        i0, o0 = len(ins), len(shapes)
        subs.append((slice(i0, i0 + len(cm.ins)), slice(o0, o0 + len(cm.out_shape)), base, start, finish))
        aliases.update({i0 + i: o0 + j for i, j in cm.aliases.items()})
        ins += cm.ins
        shapes += cm.out_shape
        base += n_sems

    def start_all(ins_r, outs_r, ss, rs, b):
        for si, so, off, start, _ in subs:
            start(ins_r[si], outs_r[so], ss, rs, b + off)

    def finish_all(ins_r, outs_r, ss, rs, b):
        for si, so, off, _, finish in subs:
            finish(ins_r[si], outs_r[so], ss, rs, b + off)

    return _Comm(ins, shapes, aliases, [(base, start_all, finish_all)])


def _own_half_buffer(piece, k_pos, c_pos, dtype):
    rh = piece.shape[0] // 2
    half = lax.dynamic_slice_in_dim(piece, c_pos * rh, rh, 0).astype(dtype)
    buf = lax.empty((N_CHIP,) + piece.shape, dtype)
    return lax.dynamic_update_slice(buf, half[None], (k_pos, c_pos * rh, 0))


def _exchange_comm(grads):
    n = len(grads)

    def copy(ins, outs, ss, rs, base, a):
        x, y, c = _mesh_pos()
        rh = ins[a].shape[1] // 2
        return _remote(ins[a].at[:, pl.ds((1 - c) * rh, rh), :], outs[a], ss, rs, base + a, (x, y, 1 - c))

    def start(ins, outs, ss, rs, base):
        for a in range(n):
            copy(ins, outs, ss, rs, base, a).start()

    def finish(ins, outs, ss, rs, base):
        for a in range(n):
            copy(ins, outs, ss, rs, base, a).wait()

    shapes = [jax.ShapeDtypeStruct((N_CHIP, g.shape[1] // 2, g.shape[2]), g.dtype) for g in grads]
    return _Comm(grads, shapes, {}, [(n, start, finish)])


def _chip_sum(name, g, recv, c_arr):
    _, r, cc = g.shape
    rh = r // 2

    def body(c_ref, g_ref, r_ref, o_ref):
        o_ref[...] = (g_ref[...] + r_ref[...]).astype(BF16)

    return pl.pallas_call(
        body, name=name, out_shape=jax.ShapeDtypeStruct((N_CHIP, rh, cc), BF16),
        grid_spec=pltpu.PrefetchScalarGridSpec(
            num_scalar_prefetch=1, grid=(N_CHIP,),
            in_specs=[pl.BlockSpec((1, rh, cc), lambda j, c_ref: (j, c_ref[0], 0)),
                      pl.BlockSpec((1, rh, cc), lambda j, c_ref: (j, 0, 0))],
            out_specs=pl.BlockSpec((1, rh, cc), lambda j, c_ref: (j, 0, 0))),
        compiler_params=_cp(("arbitrary",), 32),
    )(c_arr, g, recv)


def _scatter_comm(sums):
    n = len(sums)

    def copies(ins, outs, ss, rs, base):
        x, y, c = _mesh_pos()
        return [_remote(ins[a].at[2 * chip[0] + chip[1]], outs[a].at[j], ss, rs, base + 3 * a + j, (*chip, c))
                for a in range(n) for j, chip in enumerate(_other_chips(x, y))]

    def start(ins, outs, ss, rs, base):
        for cp in copies(ins, outs, ss, rs, base):
            cp.start()

    def finish(ins, outs, ss, rs, base):
        for cp in copies(ins, outs, ss, rs, base):
            cp.wait()

    shapes = [jax.ShapeDtypeStruct((3,) + s.shape[1:], s.dtype) for s in sums]
    return _Comm(sums, shapes, {}, [(3 * n, start, finish)])


def _total_sum(name, sums, recv3, kc_arr):
    _, rh, cc = sums.shape

    def body(kc_ref, s_ref, r_ref, o_ref):
        t = s_ref[0].astype(F32) + r_ref[0].astype(F32)
        t = t + r_ref[1].astype(F32)
        o_ref[...] = t + r_ref[2].astype(F32)

    return pl.pallas_call(
        body, name=name, out_shape=jax.ShapeDtypeStruct((2 * rh, cc), F32),
        grid_spec=pltpu.PrefetchScalarGridSpec(
            num_scalar_prefetch=1, grid=(1,),
            in_specs=[pl.BlockSpec((1, rh, cc), lambda i, kc_ref: (kc_ref[0], 0, 0)),
                      pl.BlockSpec((3, rh, cc), lambda i, kc_ref: (0, 0, 0))],
            out_specs=pl.BlockSpec((rh, cc), lambda i, kc_ref: (kc_ref[1], 0))),
        compiler_params=_cp(("arbitrary",), 32),
    )(kc_arr, sums, recv3)


def _assemble_comm(totals):
    n = len(totals)

    def copy(outs, ss, rs, base, a):
        x, y, c = _mesh_pos()
        rh = outs[a].shape[0] // 2
        here = outs[a].at[pl.ds(c * rh, rh), :]
        return _remote(here, here, ss, rs, base + a, (x, y, 1 - c))

    def start(ins, outs, ss, rs, base):
        for a in range(n):
            copy(outs, ss, rs, base, a).start()

    def finish(ins, outs, ss, rs, base):
        for a in range(n):
            copy(outs, ss, rs, base, a).wait()

    shapes = [jax.ShapeDtypeStruct(t.shape, t.dtype) for t in totals]
    return _Comm(totals, shapes, {a: a for a in range(n)}, [(n, start, finish)])


def _small_allreduce(buf):
    shape = buf.shape

    def body(in_ref, out_ref, sib_s, csum_s, all_s, send_sems, recv_sems):
        x, y, c = _mesh_pos()
        k = 2 * x + y
        cp = pltpu.make_async_remote_copy(src_ref=in_ref, dst_ref=sib_s, send_sem=send_sems.at[0],
                                          recv_sem=recv_sems.at[0], device_id=(x, y, 1 - c), device_id_type=MESH_ID)
        cp.start()
        cp.wait()
        csum_s[...] = in_ref[...] + sib_s[...]
        all_s[k] = csum_s[...]
        cps = []
        for j, chip in enumerate(_other_chips(x, y)):
            cp = pltpu.make_async_remote_copy(src_ref=csum_s, dst_ref=all_s.at[k], send_sem=send_sems.at[1 + j],
                                              recv_sem=recv_sems.at[1 + j], device_id=(*chip, c),
                                              device_id_type=MESH_ID)
            cp.start()
            cps.append(cp)
        for cp in cps:
            cp.wait()
        out_ref[...] = ((all_s[0] + all_s[1]) + all_s[2]) + all_s[3]

    vm = pl.BlockSpec(memory_space=pltpu.VMEM)
    return pl.pallas_call(
        body, name="small_allreduce", out_shape=jax.ShapeDtypeStruct(shape, F32), in_specs=[vm], out_specs=vm,
        scratch_shapes=[pltpu.VMEM(shape, F32), pltpu.VMEM(shape, F32), pltpu.VMEM((N_CHIP,) + shape, F32),
                        pltpu.SemaphoreType.DMA((4,)), pltpu.SemaphoreType.DMA((4,))],
        compiler_params=_cp(None, 32),
    )(buf)


def _adamw_math(w, g, m, v):
    m = ADAM_B1 * m + (1.0 - ADAM_B1) * g
    v = ADAM_B2 * v + (1.0 - ADAM_B2) * (g * g)
    m_hat = m / (1.0 - ADAM_B1 ** ADAM_STEP)
    v_hat = v / (1.0 - ADAM_B2 ** ADAM_STEP)
    delta = -ADAM_LR * (m_hat / (jnp.sqrt(v_hat) + ADAM_EPS) + ADAM_WD * w)
    return delta, m, v


def _adamw(name, w, g, m, v):
    r, c = w.shape
    tr = max(t for t in range(8, 513, 8) if r % t == 0)

    def body(w_ref, g_ref, m_ref, v_ref, d_ref, mo_ref, vo_ref):
        d_ref[...], mo_ref[...], vo_ref[...] = _adamw_math(w_ref[...], g_ref[...], m_ref[...], v_ref[...])

    return pl.pallas_call(
        body, name=name, grid=(r // tr,), in_specs=[_rows(tr, c)] * 4, out_specs=[_rows(tr, c)] * 3,
        out_shape=[jax.ShapeDtypeStruct((r, c), F32)] * 3, compiler_params=_cp(("arbitrary",), 32),
    )(w, g, m, v)


def _pack(arrays):
    flat = jnp.concatenate([a.reshape(-1) for a in arrays])
    n = flat.shape[0]
    rows = -(-n // (8 * 128)) * 8
    return jnp.pad(flat, (0, rows * 128 - n)).reshape(rows, 128)


def _unpack(buf, shapes):
    flat = buf.reshape(-1)
    out, off = [], 0
    for s in shapes:
        n = int(np.prod(s))
        out.append(flat[off:off + n].reshape(s))
        off += n
    return out


WEIGHTS = ["meta_tokens", "ffn1_pre_g", "ffn1_post_g", "ffn1_w_gate", "ffn1_w_up", "ffn1_w_down", "mix_pre_g", "w_in",
           "na_rpb", "s5_lam_re", "s5_lam_im", "s5_log_dt", "s5_b_re", "s5_b_im", "s5_c_re", "s5_c_im", "s5_d",
           "s5_w_glu", "s5_b_glu", "na_out_g", "s5_out_g", "w_out", "mix_post_g", "ffn2_pre_g", "ffn2_post_g",
           "ffn2_w_gate", "ffn2_w_up", "ffn2_w_down", "final_g"]
BIG = ["ffn1_w_gate", "ffn1_w_up", "ffn1_w_down", "w_in", "s5_w_glu", "w_out", "ffn2_w_gate", "ffn2_w_up",
       "ffn2_w_down"]
TRANSPOSED = ["ffn1_w_gate", "ffn1_w_up", "ffn2_w_gate", "ffn2_w_up"]
GAINS = ["ffn1_pre_g", "ffn1_post_g", "mix_pre_g", "s5_d", "s5_b_glu", "na_out_g", "s5_out_g", "mix_post_g",
         "ffn2_pre_g", "ffn2_post_g", "final_g"]
SMALL = [n for n in WEIGHTS if n not in BIG]


def kernel(*args):
    names = ["x"] + WEIGHTS + ["loss_target"] + ["m_" + n for n in WEIGHTS] + ["v_" + n for n in WEIGHTS]
    assert len(args) == len(names)
    given = dict(zip(names, args))
    x_pos, y_pos, c_pos = _mesh_pos()
    k_pos = 2 * x_pos + y_pos
    c_arr = jnp.reshape(c_pos, (1,)).astype(jnp.int32)
    kc_arr = jnp.stack([k_pos, c_pos]).astype(jnp.int32)

    def piece(name, a):
        return a[0].T if name in TRANSPOSED else a[0]

    def unpiece(name, a):
        return a.T[None] if name in TRANSPOSED else a[None]

    bufs = {n: _own_half_buffer(piece(n, given[n]), k_pos, c_pos, BF16) for n in BIG}
    bufs["meta_tokens"] = _own_half_buffer(given["meta_tokens"], k_pos, c_pos, F32)

    gains = {n: given[n] for n in GAINS}
    s5 = {n: given["s5_" + n][0] for n in ["lam_re", "lam_im", "log_dt", "b_re", "b_im", "c_re", "c_im"]}
    loss, dh0, pieces, small = _step(given["x"][0], given["loss_target"][0], bufs, gains, s5, given["na_rpb"][0],
                                     c_arr, kc_arr)
    loss = lax.psum(loss, ("x", "y", "c"))
    n_tok = given["x"].shape[1]
    grad_x = dh0[N_META:N_META + n_tok][None]

    small["meta_tokens"] = dh0[:N_META]
    red = _unpack(_small_allreduce(_pack([small[n] for n in SMALL])), [small[n].shape for n in SMALL])
    small = dict(zip(SMALL, red))
    mc = D // N_CHIP
    small["meta_tokens"] = lax.dynamic_slice_in_dim(small["meta_tokens"], k_pos * mc, mc, 1)

    out_g, out_d, out_m, out_v = {}, {}, {}, {}
    for n in BIG:
        g2 = pieces[n]
        d2, m2, v2 = _adamw("adamw_" + n, piece(n, given[n]), g2, piece(n, given["m_" + n]),
                            piece(n, given["v_" + n]))
        out_g[n], out_d[n], out_m[n], out_v[n] = (unpiece(n, t) for t in (g2, d2, m2, v2))
    shapes = [given[n].shape for n in SMALL]
    gs = [small[n].reshape(given[n].shape) for n in SMALL]
    d2, m2, v2 = _adamw("adamw_small", _pack([given[n] for n in SMALL]), _pack(gs),
                        _pack([given["m_" + n] for n in SMALL]), _pack([given["v_" + n] for n in SMALL]))
    for n, g, dd, mm, vv in zip(SMALL, gs, _unpack(d2, shapes), _unpack(m2, shapes), _unpack(v2, shapes)):
        out_g[n], out_d[n], out_m[n], out_v[n] = g, dd, mm, vv
    return (loss, grad_x, *[out_g[n] for n in WEIGHTS], *[out_d[n] for n in WEIGHTS],
            *[out_m[n] for n in WEIGHTS], *[out_v[n] for n in WEIGHTS])
```

```python
import functools
import math

import numpy as np
import jax
import jax.numpy as jnp
from jax import lax
from jax.experimental import pallas as pl
from jax.experimental.pallas import tpu as pltpu

F32 = jnp.float32
BF16 = jnp.bfloat16

D = 1024
N_META = 16
GRID_W = 64
NA_W = 512
S5_W = 512
HEAD_DIM = 64
N_HEADS = 8
KH = 8
KW = 16
S5_G = 32
S5_P = 64
S5_H = 16
N_BUNDLE = 4
FF = 2816
N_CHIP = 4
FC = FF // N_CHIP
EPS = 1e-6
NEG_INF = -1e30
Q_ROWS = 4
K_ROWS = 12
QB = Q_ROWS * GRID_W
KB = K_ROWS * GRID_W
SCAN_CHUNK = 256

ADAM_LR = 0.001
ADAM_B1 = 0.9
ADAM_B2 = 0.999
ADAM_EPS = 1e-08
ADAM_WD = 0.01
ADAM_STEP = 10

NT = (((1,), (1,)), ((), ()))
TN = (((0,), (0,)), ((), ()))
MESH_ID = pl.DeviceIdType.MESH


def _cp(sem=None, vmem_mb=None):
    kw = {}
    if sem is not None:
        kw["dimension_semantics"] = sem
    if vmem_mb is not None:
        kw["vmem_limit_bytes"] = vmem_mb << 20
    return pltpu.CompilerParams(**kw)


def _full(shape):
    n = len(shape)
    return pl.BlockSpec(shape, lambda *_: (0,) * n)


def _rows(tm, w):
    return pl.BlockSpec((tm, w), lambda i: (i, 0))


ANY = pl.BlockSpec(memory_space=pl.ANY)


def _rms(x, g):
    r = lax.rsqrt(jnp.mean(x * x, axis=-1, keepdims=True) + EPS)
    return x * r * g


def _rms_bwd(x, g, dy):
    r = lax.rsqrt(jnp.mean(x * x, axis=-1, keepdims=True) + EPS)
    xh = x * r
    dg = jnp.sum(dy * xh, axis=0, keepdims=True)
    dyg = dy * g
    dx = r * (dyg - xh * jnp.mean(dyg * xh, axis=-1, keepdims=True))
    return dx, dg


def _dot(a, b):
    return jnp.dot(a, b, preferred_element_type=F32)


def _dg(a, b, dims):
    return lax.dot_general(a, b, dims, preferred_element_type=F32)


def _ffn_fwd(name, h, g_pre, g_post, wg, wu, wd, tm, comm=None, bounds=()):
    tp = h.shape[0]
    nt = tp // tm

    def body(h_ref, gp_ref, gq_ref, wg_ref, wu_ref, wd_ref, hn_ref, gate_ref, up_ref, f_ref, xn_s, acc_s):
        c = pl.program_id(1)

        @pl.when(c == 0)
        def _():
            xn_s[...] = _rms(h_ref[...], gp_ref[...]).astype(BF16)
            acc_s[...] = jnp.zeros_like(acc_s)

        xn = xn_s[...]
        gate = _dg(xn, wg_ref[0], NT)
        up = _dg(xn, wu_ref[0], NT)
        gate_ref[0] = gate
        up_ref[0] = up
        act = (gate * jax.nn.sigmoid(gate) * up).astype(BF16)
        acc_s[...] += _dot(act, wd_ref[0])

        @pl.when(c == N_CHIP - 1)
        def _():
            f = acc_s[...]
            f_ref[...] = f
            hn_ref[...] = h_ref[...] + 0.5 * _rms(f, gq_ref[...])

    return _call(
        body, comm, bounds, (h, g_pre, g_post, wg, wu, wd), name=name, grid=(nt, N_CHIP),
        in_specs=[pl.BlockSpec((tm, D), lambda i, c: (i, 0)), _full((1, D)), _full((1, D))] +
                 [pl.BlockSpec((1, FC, D), lambda i, c: (c, 0, 0))] * 3,
        out_specs=[pl.BlockSpec((tm, D), lambda i, c: (i, 0)),
                   pl.BlockSpec((1, tm, FC), lambda i, c: (c, i, 0)),
                   pl.BlockSpec((1, tm, FC), lambda i, c: (c, i, 0)),
                   pl.BlockSpec((tm, D), lambda i, c: (i, 0))],
        out_shape=[jax.ShapeDtypeStruct((tp, D), F32), jax.ShapeDtypeStruct((N_CHIP, tp, FC), F32),
                   jax.ShapeDtypeStruct((N_CHIP, tp, FC), F32), jax.ShapeDtypeStruct((tp, D), F32)],
        scratch_shapes=[pltpu.VMEM((tm, D), BF16), pltpu.VMEM((tm, D), F32)],
        compiler_params=_cp(("arbitrary", "arbitrary"), 48))


def _ffn_bwd(name, h, g_pre, df, gate, up, wg, wu, wd, tm):
    tp = h.shape[0]
    nt = tp // tm
    rh = FC // 2

    def body(h_ref, gp_ref, df_ref, gate_ref, up_ref, wg_ref, wu_ref, wd_ref,
             dwg_ref, dwu_ref, dwd_ref, dxn_ref, rg_ref, ru_ref, rd_ref, ag, au, ad, send_sems, recv_sems):
        c = pl.program_id(0)
        i = pl.program_id(1)

        def to_sibling(a, piece):
            x, y, core = _mesh_pos()
            dw_ref, r_ref = ((dwg_ref, rg_ref), (dwu_ref, ru_ref), (dwd_ref, rd_ref))[a]
            return _remote(dw_ref.at[piece, pl.ds((1 - core) * rh, rh), :], r_ref.at[piece], send_sems, recv_sems,
                           3 * piece + a, (x, y, 1 - core))

        @pl.when(i == 0)
        def _():
            ag[...] = jnp.zeros_like(ag)
            au[...] = jnp.zeros_like(au)
            ad[...] = jnp.zeros_like(ad)

        xn = _rms(h_ref[...], gp_ref[...]).astype(BF16)
        dfb = df_ref[...].astype(BF16)
        gt = gate_ref[0]
        u = up_ref[0]
        sg = jax.nn.sigmoid(gt)
        si = gt * sg
        act = (si * u).astype(BF16)
        dact = _dg(dfb, wd_ref[0], NT)
        ad[...] += _dg(act, dfb, TN)
        dgate = (dact * u * (sg * (1.0 + gt * (1.0 - sg)))).astype(BF16)
        dup = (dact * si).astype(BF16)
        ag[...] += _dg(dgate, xn, TN)
        au[...] += _dg(dup, xn, TN)
        dxn_ref[0] = _dot(dgate, wg_ref[0]) + _dot(dup, wu_ref[0])

        @pl.when(i == nt - 1)
        def _():
            pltpu.sync_copy(ag, dwg_ref.at[c])
            pltpu.sync_copy(au, dwu_ref.at[c])
            pltpu.sync_copy(ad, dwd_ref.at[c])
            for a in range(3):
                to_sibling(a, c).start()

        @pl.when((c == N_CHIP - 1) & (i == nt - 1))
        def _():
            for piece in range(N_CHIP):
                for a in range(3):
                    to_sibling(a, piece).wait()

    return pl.pallas_call(
        body, name=name, grid=(N_CHIP, nt),
        in_specs=[pl.BlockSpec((tm, D), lambda c, i: (i, 0)), _full((1, D)),
                  pl.BlockSpec((tm, D), lambda c, i: (i, 0)),
                  pl.BlockSpec((1, tm, FC), lambda c, i: (c, i, 0)),
                  pl.BlockSpec((1, tm, FC), lambda c, i: (c, i, 0))] +
                 [pl.BlockSpec((1, FC, D), lambda c, i: (c, 0, 0))] * 3,
        out_specs=[ANY, ANY, ANY, pl.BlockSpec((1, tm, D), lambda c, i: (c, i, 0)), ANY, ANY, ANY],
        out_shape=[jax.ShapeDtypeStruct((N_CHIP, FC, D), F32)] * 3 + [jax.ShapeDtypeStruct((N_CHIP, tp, D), F32)] +
                  [jax.ShapeDtypeStruct((N_CHIP, rh, D), F32)] * 3,
        scratch_shapes=[pltpu.VMEM((FC, D), F32)] * 3 +
                       [pltpu.SemaphoreType.DMA((3 * N_CHIP,)), pltpu.SemaphoreType.DMA((3 * N_CHIP,))],
        compiler_params=_cp(("arbitrary", "arbitrary"), 56),
    )(h, g_pre, df, gate, up, wg, wu, wd)


def _ffn_pre_bwd(name, dh, dxn_part, h, g_pre, tm, comm=None, bounds=()):
    tp = h.shape[0]
    nt = tp // tm

    def body(dh_ref, dxn_ref, h_ref, gp_ref, out_ref, dg_ref):
        i = pl.program_id(0)
        dxn = (dxn_ref[0] + dxn_ref[1]) + (dxn_ref[2] + dxn_ref[3])
        dx, dg = _rms_bwd(h_ref[...], gp_ref[...], dxn)
        out_ref[...] = dh_ref[...] + dx

        @pl.when(i == 0)
        def _():
            dg_ref[...] = jnp.zeros_like(dg_ref)

        dg_ref[...] += dg

    return _call(
        body, comm, bounds, (dh, dxn_part, h, g_pre), name=name, grid=(nt,),
        in_specs=[_rows(tm, D), pl.BlockSpec((N_CHIP, tm, D), lambda i: (0, i, 0)), _rows(tm, D), _full((1, D))],
        out_specs=[_rows(tm, D), _full((1, D))],
        out_shape=[jax.ShapeDtypeStruct((tp, D), F32), jax.ShapeDtypeStruct((1, D), F32)],
        compiler_params=_cp(("arbitrary",), 48))


def _mix_in(h, g, w_in, tm):
    tp = h.shape[0]

    def body(h_ref, g_ref, w_ref, q_ref, k_ref, v_ref, u_ref):
        a = _rms(h_ref[...], g_ref[...]).astype(BF16)
        q_ref[...] = _dot(a, w_ref[0]).astype(BF16)
        k_ref[...] = _dot(a, w_ref[1]).astype(BF16)
        v_ref[...] = _dot(a, w_ref[2]).astype(BF16)
        u_ref[...] = _dot(a, w_ref[3])

    return pl.pallas_call(
        body, name="mix_in", grid=(tp // tm,),
        in_specs=[_rows(tm, D), _full((1, D)), _full((N_CHIP, D, NA_W))],
        out_specs=[_rows(tm, NA_W)] * 4,
        out_shape=[jax.ShapeDtypeStruct((tp, NA_W), BF16)] * 3 + [jax.ShapeDtypeStruct((tp, S5_W), F32)],
        compiler_params=_cp(("arbitrary",), 40),
    )(h, g, w_in)


def _gelu(x):
    return jax.nn.gelu(x, approximate=True)


def _gelu_grad(x):
    k = math.sqrt(2.0 / math.pi)
    t = jnp.tanh(k * (x + 0.044715 * x * x * x))
    return 0.5 * (1.0 + t) + 0.5 * x * (1.0 - t * t) * k * (1.0 + 3.0 * 0.044715 * x * x)


def _mix_out(o_na, y_pre, h, w_glu, b_glu, g_na, g_s5, w_out, g_post, tm, comm=None, bounds=()):
    tp = h.shape[0]

    def body(ona_ref, yp_ref, h_ref, wglu_ref, bglu_ref, gna_ref, gs5_ref, wout_ref, gpost_ref, hn_ref, mix_ref):
        y = _gelu(yp_ref[...])
        z = _dot(y.astype(BF16), wglu_ref[...]) + bglu_ref[...]
        o_s5 = y * jax.nn.sigmoid(z)
        n1 = _rms(ona_ref[...], gna_ref[...]).astype(BF16)
        n2 = _rms(o_s5, gs5_ref[...]).astype(BF16)
        mix = _dot(n1, wout_ref[0:NA_W, :]) + _dot(n2, wout_ref[NA_W:, :])
        mix_ref[...] = mix
        hn_ref[...] = h_ref[...] + _rms(mix, gpost_ref[...])

    return _call(
        body, comm, bounds, (o_na, y_pre, h, w_glu, b_glu, g_na, g_s5, w_out, g_post), name="mix_out",
        grid=(tp // tm,),
        in_specs=[_rows(tm, NA_W), _rows(tm, S5_W), _rows(tm, D), _full((S5_W, S5_W)), _full((1, S5_W)),
                  _full((1, NA_W)), _full((1, S5_W)), _full((D, D)), _full((1, D))],
        out_specs=[_rows(tm, D), _rows(tm, D)],
        out_shape=[jax.ShapeDtypeStruct((tp, D), F32)] * 2,
        compiler_params=_cp(("arbitrary",), 40))


def _mix_out_bwd(dh, mix, o_na, y_pre, w_glu, b_glu, g_na, g_s5, w_out, g_post, tm):
    tp = dh.shape[0]
    nt = tp // tm

    def body(dh_ref, mix_ref, ona_ref, yp_ref, wglu_ref, bglu_ref, gna_ref, gs5_ref, wout_ref, gpost_ref,
             dona_ref, dyp_ref, dwout_ref, dwglu_ref, dgpost_ref, dgna_ref, dgs5_ref, dbglu_ref, a_out, a_glu):
        i = pl.program_id(0)

        @pl.when(i == 0)
        def _():
            a_out[...] = jnp.zeros_like(a_out)
            a_glu[...] = jnp.zeros_like(a_glu)
            dgpost_ref[...] = jnp.zeros_like(dgpost_ref)
            dgna_ref[...] = jnp.zeros_like(dgna_ref)
            dgs5_ref[...] = jnp.zeros_like(dgs5_ref)
            dbglu_ref[...] = jnp.zeros_like(dbglu_ref)

        dmix, dgpost = _rms_bwd(mix_ref[...], gpost_ref[...], dh_ref[...])
        dgpost_ref[...] += dgpost
        yp = yp_ref[...]
        y = _gelu(yp)
        yb = y.astype(BF16)
        z = _dot(yb, wglu_ref[...]) + bglu_ref[...]
        sg = jax.nn.sigmoid(z)
        o_s5 = y * sg
        o_na = ona_ref[...]
        n1 = _rms(o_na, gna_ref[...]).astype(BF16)
        n2 = _rms(o_s5, gs5_ref[...]).astype(BF16)
        dmb = dmix.astype(BF16)
        a_out[0:NA_W, :] += _dg(n1, dmb, TN)
        a_out[NA_W:, :] += _dg(n2, dmb, TN)
        dn1 = _dg(dmb, wout_ref[0:NA_W, :], NT)
        dn2 = _dg(dmb, wout_ref[NA_W:, :], NT)
        dona, dgna = _rms_bwd(o_na, gna_ref[...], dn1)
        dona_ref[...] = dona
        dgna_ref[...] += dgna
        dos5, dgs5 = _rms_bwd(o_s5, gs5_ref[...], dn2)
        dgs5_ref[...] += dgs5
        dz = dos5 * y * (sg * (1.0 - sg))
        dbglu_ref[...] += jnp.sum(dz, axis=0, keepdims=True)
        dzb = dz.astype(BF16)
        a_glu[...] += _dg(yb, dzb, TN)
        dy = dos5 * sg + _dg(dzb, wglu_ref[...], NT)
        dyp_ref[...] = dy * _gelu_grad(yp)

        @pl.when(i == nt - 1)
        def _():
            pltpu.sync_copy(a_out, dwout_ref)
            pltpu.sync_copy(a_glu, dwglu_ref)

    return pl.pallas_call(
        body, name="mix_out_bwd", grid=(nt,),
        in_specs=[_rows(tm, D), _rows(tm, D), _rows(tm, NA_W), _rows(tm, S5_W), _full((S5_W, S5_W)),
                  _full((1, S5_W)), _full((1, NA_W)), _full((1, S5_W)), _full((D, D)), _full((1, D))],
        out_specs=[_rows(tm, NA_W), _rows(tm, S5_W), ANY, ANY, _full((1, D)), _full((1, NA_W)),
                   _full((1, S5_W)), _full((1, S5_W))],
        out_shape=[jax.ShapeDtypeStruct((tp, NA_W), F32), jax.ShapeDtypeStruct((tp, S5_W), F32),
                   jax.ShapeDtypeStruct((D, D), F32), jax.ShapeDtypeStruct((S5_W, S5_W), F32),
                   jax.ShapeDtypeStruct((1, D), F32), jax.ShapeDtypeStruct((1, NA_W), F32),
                   jax.ShapeDtypeStruct((1, S5_W), F32), jax.ShapeDtypeStruct((1, S5_W), F32)],
        scratch_shapes=[pltpu.VMEM((D, D), F32), pltpu.VMEM((S5_W, S5_W), F32)],
        compiler_params=_cp(("arbitrary",), 48),
    )(dh, mix, o_na, y_pre, w_glu, b_glu, g_na, g_s5, w_out, g_post)


def _mix_in_bwd(dq, dk, dv, du, h, g, w_in, dh, f1, g_post1, tm, comm=None, bounds=()):
    tp = h.shape[0]
    nt = tp // tm

    def body(dq_ref, dk_ref, dv_ref, du_ref, h_ref, g_ref, w_ref, dh_ref, f_ref, gq_ref,
             dh1_ref, df_ref, dw_ref, dg_ref, dgq_ref, acc):
        i = pl.program_id(0)

        @pl.when(i == 0)
        def _():
            acc[...] = jnp.zeros_like(acc)
            dg_ref[...] = jnp.zeros_like(dg_ref)
            dgq_ref[...] = jnp.zeros_like(dgq_ref)

        x = h_ref[...]
        a = _rms(x, g_ref[...]).astype(BF16)
        da = jnp.zeros((tm, D), F32)
        for j, r in enumerate((dq_ref, dk_ref, dv_ref, du_ref)):
            dp = r[...].astype(BF16)
            da = da + _dg(dp, w_ref[j], NT)
            acc[j] += _dg(a, dp, TN)
        dx, dg = _rms_bwd(x, g_ref[...], da)
        dh1 = dh_ref[...] + dx
        dh1_ref[...] = dh1
        dg_ref[...] += dg
        df, dgq = _rms_bwd(f_ref[...], gq_ref[...], 0.5 * dh1)
        df_ref[...] = df
        dgq_ref[...] += dgq

        @pl.when(i == nt - 1)
        def _():
            pltpu.sync_copy(acc, dw_ref)

    return _call(
        body, comm, bounds, (dq, dk, dv, du, h, g, w_in, dh, f1, g_post1), name="mix_in_bwd", grid=(nt,),
        in_specs=[_rows(tm, NA_W)] * 4 + [_rows(tm, D), _full((1, D)), _full((N_CHIP, D, NA_W)), _rows(tm, D),
                                         _rows(tm, D), _full((1, D))],
        out_specs=[_rows(tm, D), _rows(tm, D), ANY, _full((1, D)), _full((1, D))],
        out_shape=[jax.ShapeDtypeStruct((tp, D), F32), jax.ShapeDtypeStruct((tp, D), F32),
                   jax.ShapeDtypeStruct((N_CHIP, D, NA_W), F32), jax.ShapeDtypeStruct((1, D), F32),
                   jax.ShapeDtypeStruct((1, D), F32)],
        scratch_shapes=[pltpu.VMEM((N_CHIP, D, NA_W), F32)],
        compiler_params=_cp(("arbitrary",), 48))


def _final_loss(h, g_final, target, f2, g_post2, n_tok, tm):
    tp = h.shape[0]

    def body(h_ref, g_ref, t_ref, f_ref, gq_ref, dh_ref, df_ref, loss_ref, dg_ref, dgq_ref):
        i = pl.program_id(0)

        @pl.when(i == 0)
        def _():
            loss_ref[...] = jnp.zeros_like(loss_ref)
            dg_ref[...] = jnp.zeros_like(dg_ref)
            dgq_ref[...] = jnp.zeros_like(dgq_ref)

        x = h_ref[...]
        y = _rms(x, g_ref[...])
        row = i * tm + lax.broadcasted_iota(jnp.int32, (tm, 1), 0)
        valid = (row >= N_META) & (row < N_META + n_tok)
        e = jnp.where(valid, y - t_ref[...], 0.0)
        loss_ref[...] += 0.5 * jnp.sum(jnp.mean(e * e, axis=-1, keepdims=True), axis=0, keepdims=True)
        dx, dg = _rms_bwd(x, g_ref[...], e * (1.0 / D))
        dh_ref[...] = dx
        dg_ref[...] += dg
        df, dgq = _rms_bwd(f_ref[...], gq_ref[...], 0.5 * dx)
        df_ref[...] = df
        dgq_ref[...] += dgq

    return pl.pallas_call(
        body, name="final_loss", grid=(tp // tm,),
        in_specs=[_rows(tm, D), _full((1, D)), _rows(tm, D), _rows(tm, D), _full((1, D))],
        out_specs=[_rows(tm, D), _rows(tm, D), _full((1, 1)), _full((1, D)), _full((1, D))],
        out_shape=[jax.ShapeDtypeStruct((tp, D), F32), jax.ShapeDtypeStruct((tp, D), F32),
                   jax.ShapeDtypeStruct((1, 1), F32), jax.ShapeDtypeStruct((1, D), F32),
                   jax.ShapeDtypeStruct((1, D), F32)],
        compiler_params=_cp(("arbitrary",), 40),
    )(h, g_final, target, f2, g_post2)


def _na_patterns(n_rows):
    pats = []
    for kind in range(3):
        pat = [[-1] * K_ROWS for _ in range(Q_ROWS)]
        for i in range(Q_ROWS):
            for jj in range(K_ROWS):
                if kind == 0 and jj < KH:
                    pat[i][jj] = jj - i + KH - 1
                elif kind == 1 and i <= jj < i + KH:
                    pat[i][jj] = jj - i + 3
                elif kind == 2 and K_ROWS - KH <= jj:
                    pat[i][jj] = jj - i - 1
        pats.append(pat)
    return pats


def _diag_onehot():
    q = np.arange(GRID_W)[:, None]
    kc = np.arange(GRID_W)[None, :]
    start = np.clip(q - KW // 2, 0, GRID_W - KW)
    col_in = (kc >= start) & (kc < start + KW)
    e = np.zeros((32, GRID_W, GRID_W), np.float32)
    for d in range(2 * KW - 1):
        e[d] = ((kc - q + KW - 1) == d) & col_in
    return e.reshape(32, GRID_W * GRID_W), col_in


def _rpb_collapse(dtb2, et):
    def body(d_ref, e_ref, o_ref):
        o_ref[...] = jnp.dot(d_ref[...], e_ref[...], preferred_element_type=F32, precision=lax.Precision.HIGHEST)

    return pl.pallas_call(
        body, name="rpb_collapse", out_shape=jax.ShapeDtypeStruct((dtb2.shape[0], et.shape[1]), F32),
        in_specs=[pl.BlockSpec(memory_space=pltpu.VMEM)] * 2, out_specs=pl.BlockSpec(memory_space=pltpu.VMEM),
    )(dtb2, et)


def _bias_tables(rpb, n_rows):
    n_dr, n_dc = 2 * KH - 1, 2 * KW - 1
    pats = _na_patterns(n_rows)

    def body(rpb_ref, o_ref):
        h = pl.program_id(0)
        q = lax.broadcasted_iota(jnp.int32, (GRID_W, GRID_W), 0)
        kc = lax.broadcasted_iota(jnp.int32, (GRID_W, GRID_W), 1)
        start = jnp.clip(q - KW // 2, 0, GRID_W - KW)
        col_in = (kc >= start) & (kc < start + KW)
        diff = kc - q + (KW - 1)
        neg = jnp.full((GRID_W, GRID_W), NEG_INF, F32)
        band = []
        for dr in range(n_dr):
            acc = neg
            for d in range(n_dc):
                acc = jnp.where((diff == d) & col_in, rpb_ref[(h * n_dr + dr) * n_dc + d], acc)
            band.append(acc)
        for kind, pat in enumerate(pats):
            for i in range(Q_ROWS):
                for jj in range(K_ROWS):
                    o_ref[kind, 0, i * GRID_W:(i + 1) * GRID_W, jj * GRID_W:(jj + 1) * GRID_W] = (
                        band[pat[i][jj]] if pat[i][jj] >= 0 else neg)

    return pl.pallas_call(
        body, name="bias_tables", grid=(N_HEADS,),
        in_specs=[pl.BlockSpec(memory_space=pltpu.SMEM)],
        out_specs=pl.BlockSpec((3, 1, QB, KB), lambda h: (0, h, 0, 0)),
        out_shape=jax.ShapeDtypeStruct((3, N_HEADS, QB, KB), F32),
        compiler_params=_cp(("arbitrary",), 32),
    )(rpb.reshape(-1))


def _attn_geometry(n_tok):
    n_rows = n_tok // GRID_W
    assert n_rows % Q_ROWS == 0 and n_rows >= K_ROWS
    return n_rows, n_rows // Q_ROWS


def _attn_probs(qh, kh, kmh, bias, scale):
    s = _dg(qh, kh, NT) * scale + bias
    sm = _dg(qh, kmh, NT) * scale
    m = jnp.maximum(jnp.max(s, axis=-1, keepdims=True), jnp.max(sm, axis=-1, keepdims=True))
    p = jnp.exp(s - m)
    pm = jnp.exp(sm - m)
    inv = 1.0 / (jnp.sum(p, axis=-1, keepdims=True) + jnp.sum(pm, axis=-1, keepdims=True))
    return p * inv, pm * inv


def _meta_probs(qmh, kmh, scale):
    s = _dg(qmh, kmh, NT) * scale
    p = jnp.exp(s - jnp.max(s, axis=-1, keepdims=True))
    return p / jnp.sum(p, axis=-1, keepdims=True)


def _step_rows(r, n_rows):
    q0 = pl.multiple_of(N_META + r * QB, 16)
    k0 = pl.multiple_of(N_META + jnp.clip(Q_ROWS * r - (K_ROWS - KH), 0, n_rows - K_ROWS) * GRID_W, 16)
    return q0, k0


def _attn_fwd(q, k, v, bias, n_tok, comm=None, bounds=()):
    tp = q.shape[0]
    n_rows, n_steps = _attn_geometry(n_tok)
    scale = HEAD_DIM ** -0.5

    def body(q_ref, k_ref, v_ref, b_ref, o_ref):
        r = pl.program_id(1)
        km = k_ref[0:N_META, :]
        vm = v_ref[0:N_META, :]

        @pl.when(r == 0)
        def _():
            qm = q_ref[0:N_META, :]
            outs = []
            for hh in range(2):
                sl = slice(hh * HEAD_DIM, (hh + 1) * HEAD_DIM)
                p = _meta_probs(qm[:, sl], km[:, sl], scale)
                outs.append(_dot(p.astype(BF16), vm[:, sl]))
            o_ref[0:N_META, :] = jnp.concatenate(outs, axis=1)
            o_ref[N_META + n_tok:, :] = jnp.zeros((tp - N_META - n_tok, 2 * HEAD_DIM), F32)

        q0, k0 = _step_rows(r, n_rows)
        qb = q_ref[pl.ds(q0, QB), :]
        kb = k_ref[pl.ds(k0, KB), :]
        vb = v_ref[pl.ds(k0, KB), :]
        outs = []
        for hh in range(2):
            sl = slice(hh * HEAD_DIM, (hh + 1) * HEAD_DIM)
            p, pm = _attn_probs(qb[:, sl], kb[:, sl], km[:, sl], b_ref[0, hh], scale)
            outs.append(_dot(p.astype(BF16), vb[:, sl]) + _dot(pm.astype(BF16), vm[:, sl]))
        o_ref[pl.ds(q0, QB), :] = jnp.concatenate(outs, axis=1)

    def bias_map(hp, r):
        return (jnp.where(r == 0, 0, jnp.where(r == n_steps - 1, 2, 1)), hp, 0, 0)

    col = pl.BlockSpec((tp, 2 * HEAD_DIM), lambda hp, r: (0, hp))
    return _call(
        body, comm, bounds, (q, k, v, bias), name="attn_fwd", grid=(N_HEADS // 2, n_steps),
        in_specs=[col, col, col, pl.BlockSpec((1, 2, QB, KB), bias_map)],
        out_specs=[col], out_shape=[jax.ShapeDtypeStruct((tp, NA_W), F32)],
        compiler_params=_cp(("arbitrary", "arbitrary"), 40))


def _attn_bwd(q, k, v, bias, do, n_tok, comm=None, bounds=()):
    tp = q.shape[0]
    n_rows, n_steps = _attn_geometry(n_tok)
    scale = HEAD_DIM ** -0.5
    pats = _na_patterns(n_rows)

    def body(q_ref, k_ref, v_ref, b_ref, do_ref, dq_ref, dk_ref, dv_ref, dtb_ref):
        r = pl.program_id(1)
        km = k_ref[0:N_META, :]
        vm = v_ref[0:N_META, :]

        @pl.when(r == 0)
        def _():
            dk_ref[...] = jnp.zeros_like(dk_ref)
            dv_ref[...] = jnp.zeros_like(dv_ref)
            dtb_ref[...] = jnp.zeros_like(dtb_ref)
            dq_ref[N_META + n_tok:, :] = jnp.zeros((tp - N_META - n_tok, 2 * HEAD_DIM), F32)
            qm = q_ref[0:N_META, :]
            dom = do_ref[0:N_META, :].astype(BF16)
            dqs, dks, dvs = [], [], []
            for hh in range(2):
                sl = slice(hh * HEAD_DIM, (hh + 1) * HEAD_DIM)
                p = _meta_probs(qm[:, sl], km[:, sl], scale)
                dp = _dg(dom[:, sl], vm[:, sl], NT)
                ds = (p * (dp - jnp.sum(dp * p, axis=-1, keepdims=True))).astype(BF16)
                dvs.append(_dg(p.astype(BF16), dom[:, sl], TN))
                dqs.append(_dot(ds, km[:, sl]) * scale)
                dks.append(_dg(ds, qm[:, sl], TN) * scale)
            dq_ref[0:N_META, :] = jnp.concatenate(dqs, axis=1)
            dk_ref[0:N_META, :] += jnp.concatenate(dks, axis=1)
            dv_ref[0:N_META, :] += jnp.concatenate(dvs, axis=1)

        q0, k0 = _step_rows(r, n_rows)
        qb = q_ref[pl.ds(q0, QB), :]
        kb = k_ref[pl.ds(k0, KB), :]
        vb = v_ref[pl.ds(k0, KB), :]
        dob = do_ref[pl.ds(q0, QB), :].astype(BF16)
        dqs, dks, dvs, dkms, dvms, dss = [], [], [], [], [], []
        for hh in range(2):
            sl = slice(hh * HEAD_DIM, (hh + 1) * HEAD_DIM)
            qh, kh, vh, kmh, vmh, doh = qb[:, sl], kb[:, sl], vb[:, sl], km[:, sl], vm[:, sl], dob[:, sl]
            p, pm = _attn_probs(qh, kh, kmh, b_ref[0, hh], scale)
            dp = _dg(doh, vh, NT)
            dpm = _dg(doh, vmh, NT)
            delta = jnp.sum(dp * p, axis=-1, keepdims=True) + jnp.sum(dpm * pm, axis=-1, keepdims=True)
            ds = p * (dp - delta)
            dsb = ds.astype(BF16)
            dsmb = (pm * (dpm - delta)).astype(BF16)
            dss.append(ds)
            dvs.append(_dg(p.astype(BF16), doh, TN))
            dvms.append(_dg(pm.astype(BF16), doh, TN))
            dqs.append((_dot(dsb, kh) + _dot(dsmb, kmh)) * scale)
            dks.append(_dg(dsb, qh, TN) * scale)
            dkms.append(_dg(dsmb, qh, TN) * scale)
        dq_ref[pl.ds(q0, QB), :] = jnp.concatenate(dqs, axis=1)
        dk_ref[pl.ds(k0, KB), :] += jnp.concatenate(dks, axis=1)
        dv_ref[pl.ds(k0, KB), :] += jnp.concatenate(dvs, axis=1)
        dk_ref[0:N_META, :] += jnp.concatenate(dkms, axis=1)
        dv_ref[0:N_META, :] += jnp.concatenate(dvms, axis=1)

        def add_bias_grad(pat):
            for hh in range(2):
                for i in range(Q_ROWS):
                    for jj in range(K_ROWS):
                        if pat[i][jj] >= 0:
                            dtb_ref[hh, pat[i][jj]] += dss[hh][i * GRID_W:(i + 1) * GRID_W,
                                                               jj * GRID_W:(jj + 1) * GRID_W]

        @pl.when(r == 0)
        def _():
            add_bias_grad(pats[0])

        @pl.when((r > 0) & (r < n_steps - 1))
        def _():
            add_bias_grad(pats[1])

        @pl.when(r == n_steps - 1)
        def _():
            add_bias_grad(pats[2])

    def bias_map(hp, r):
        return (jnp.where(r == 0, 0, jnp.where(r == n_steps - 1, 2, 1)), hp, 0, 0)

    col = pl.BlockSpec((tp, 2 * HEAD_DIM), lambda hp, r: (0, hp))
    n_dr = 2 * KH - 1
    return _call(
        body, comm, bounds, (q, k, v, bias, do), name="attn_bwd", grid=(N_HEADS // 2, n_steps),
        in_specs=[col, col, col, pl.BlockSpec((1, 2, QB, KB), bias_map), col],
        out_specs=[col, col, col, pl.BlockSpec((2, n_dr, GRID_W, GRID_W), lambda hp, r: (hp, 0, 0, 0))],
        out_shape=[jax.ShapeDtypeStruct((tp, NA_W), F32)] * 3 +
                  [jax.ShapeDtypeStruct((N_HEADS, n_dr, GRID_W, GRID_W), F32)],
        compiler_params=_cp(("arbitrary", "arbitrary"), 48))


def _expand_onehot():
    return np.tile(np.eye(S5_P, dtype=np.float32), (1, S5_H))


def _s5_disc_math(lam_re, lam_im, log_dt, b_re, b_im, ex):
    dt = jnp.exp(log_dt)
    ea = jnp.exp(lam_re * dt)
    a_re = ea * jnp.cos(lam_im * dt)
    a_im = ea * jnp.sin(lam_im * dt)
    den = lam_re * lam_re + lam_im * lam_im
    c_re = ((a_re - 1.0) * lam_re + a_im * lam_im) / den
    c_im = (a_im * lam_re - (a_re - 1.0) * lam_im) / den
    ce_re = jnp.dot(c_re, ex, preferred_element_type=F32, precision=lax.Precision.HIGHEST)
    ce_im = jnp.dot(c_im, ex, preferred_element_type=F32, precision=lax.Precision.HIGHEST)
    return a_re, a_im, ce_re * b_re - ce_im * b_im, ce_re * b_im + ce_im * b_re


def _s5_disc(lam_re, lam_im, log_dt, b_re, b_im):
    ex = jnp.asarray(_expand_onehot())
    n = lam_re.shape[0]

    def body(lr, li, ld, br, bi, ex_ref, ar, ai, bbr, bbi):
        ar[...], ai[...], bbr[...], bbi[...] = _s5_disc_math(lr[...], li[...], ld[...], br[...], bi[...], ex_ref[...])

    vm = pl.BlockSpec(memory_space=pltpu.VMEM)
    return pl.pallas_call(
        body, name="s5_disc", in_specs=[vm] * 6, out_specs=[vm] * 4,
        out_shape=[jax.ShapeDtypeStruct((n, S5_P), F32)] * 2 + [jax.ShapeDtypeStruct((n, S5_P * S5_H), F32)] * 2,
    )(lam_re, lam_im, log_dt, b_re, b_im, ex)


def _s5_disc_bwd(lam_re, lam_im, log_dt, b_re, b_im, da_re, da_im, dbb_re, dbb_im):
    ex = jnp.asarray(_expand_onehot())
    n = lam_re.shape[0]

    def body(lr, li, ld, br, bi, ex_ref, dar, dai, dbr, dbi, o_lr, o_li, o_ld, o_br, o_bi):
        e = ex_ref[...]
        _, vjp = jax.vjp(lambda a, b, c, d, f: _s5_disc_math(a, b, c, d, f, e), lr[...], li[...], ld[...], br[...], bi[...])
        o_lr[...], o_li[...], o_ld[...], o_br[...], o_bi[...] = vjp((dar[...], dai[...], dbr[...], dbi[...]))

    vm = pl.BlockSpec(memory_space=pltpu.VMEM)
    return pl.pallas_call(
        body, name="s5_disc_bwd", in_specs=[vm] * 10, out_specs=[vm] * 5,
        out_shape=[jax.ShapeDtypeStruct((n, S5_P), F32)] * 2 + [jax.ShapeDtypeStruct((n, 1), F32)] +
                  [jax.ShapeDtypeStruct((n, S5_P * S5_H), F32)] * 2,
    )(lam_re, lam_im, log_dt, b_re, b_im, ex, da_re, da_im, dbb_re, dbb_im)


def _s5_matrices(a_re, a_im, bb_re, bb_im, c_re, c_im):
    gl = S5_G // N_BUNDLE
    eye = jnp.eye(gl, dtype=F32)
    half = gl * S5_P

    def in_mat(bb):
        t = bb.reshape(2, N_BUNDLE, gl, S5_H, S5_P).transpose(0, 1, 3, 2, 4)
        m = t[:, :, None] * eye[None, None, :, None, :, None]
        return m.reshape(2, N_BUNDLE, gl * S5_H, half)

    def out_mat(c):
        t = c.reshape(2, N_BUNDLE, gl, S5_H, S5_P).transpose(0, 1, 2, 4, 3)
        m = t[:, :, :, :, None, :] * eye[None, None, :, None, :, None]
        return m.reshape(2, N_BUNDLE, half, gl * S5_H)

    a = jnp.concatenate([a_re.reshape(2, N_BUNDLE, 1, half), a_im.reshape(2, N_BUNDLE, 1, half)], axis=-1)
    bm = jnp.concatenate([in_mat(bb_re), in_mat(bb_im)], axis=-1)
    cm = jnp.concatenate([out_mat(c_re), -out_mat(c_im)], axis=-2)
    return a, bm, cm


def _scan_chunks(length):
    return [(t0, min(SCAN_CHUNK, length - t0)) for t0 in range(0, length, SCAN_CHUNK)]


def _scan(src_ref, dst_ref, prev_ref, prev_off, n_rows, a_re, a_im, carry, reverse):
    half = a_re.shape[-1]
    n_blk = n_rows // 8
    rid = lax.broadcasted_iota(jnp.int32, (8, half), 0)

    def blk(i, carry):
        xr, xi = carry
        bi = (n_blk - 1 - i) if reverse else i
        off = pl.multiple_of(bi * 8, 8)
        v = src_ref[pl.ds(off, 8), :]
        o_r = jnp.zeros((8, half), F32)
        o_i = jnp.zeros((8, half), F32)
        p_r = jnp.zeros((8, half), F32)
        p_i = jnp.zeros((8, half), F32)
        for j in (range(7, -1, -1) if reverse else range(8)):
            if prev_ref is not None:
                p_r = jnp.where(rid == j, xr, p_r)
                p_i = jnp.where(rid == j, xi, p_i)
            nr = a_re * xr - a_im * xi + v[j:j + 1, :half]
            ni = a_re * xi + a_im * xr + v[j:j + 1, half:]
            xr, xi = nr, ni
            if dst_ref is not None:
                o_r = jnp.where(rid == j, xr, o_r)
                o_i = jnp.where(rid == j, xi, o_i)
        if dst_ref is not None:
            dst_ref[pl.ds(off, 8), :] = jnp.concatenate([o_r, o_i], axis=1)
        if prev_ref is not None:
            prev_ref[pl.ds(pl.multiple_of(prev_off + off, 8), 8), :] = jnp.concatenate([p_r, p_i], axis=1)
        return xr, xi

    return lax.fori_loop(0, n_blk, blk, carry)


def _s5_fwd(u, d_skip, a, bm, cm, length, comm=None, bounds=()):
    tp = u.shape[0]
    cw = S5_W // N_BUNDLE
    sw = a.shape[-1]
    half = sw // 2
    chunks = _scan_chunks(length)

    def body(u_ref, d_ref, a_ref, bm_ref, cm_ref, y_ref, bu_s, xs_s):
        y_ref[...] = u_ref[...] * d_ref[...]
        for dr in range(2):
            a_re = a_ref[dr, 0, :, 0:half]
            a_im = a_ref[dr, 0, :, half:]
            carry = (jnp.zeros((1, half), F32), jnp.zeros((1, half), F32))
            for t0, n in (chunks if dr == 0 else chunks[::-1]):
                bu_s[0:n, :] = _dot(u_ref[t0:t0 + n, :].astype(BF16), bm_ref[dr, 0])
                carry = _scan(bu_s, xs_s, None, 0, n, a_re, a_im, carry, dr == 1)
                y_ref[t0:t0 + n, :] += _dot(xs_s[0:n, :].astype(BF16), cm_ref[dr, 0])

    return _call(
        body, comm, bounds, (u, d_skip, a, bm, cm), name="s5_fwd", grid=(N_BUNDLE,),
        in_specs=[pl.BlockSpec((tp, cw), lambda b: (0, b)), pl.BlockSpec((1, cw), lambda b: (0, b)),
                  pl.BlockSpec((2, 1, 1, sw), lambda b: (0, b, 0, 0)),
                  pl.BlockSpec((2, 1, cw, sw), lambda b: (0, b, 0, 0)),
                  pl.BlockSpec((2, 1, sw, cw), lambda b: (0, b, 0, 0))],
        out_specs=[pl.BlockSpec((tp, cw), lambda b: (0, b))],
        out_shape=[jax.ShapeDtypeStruct((tp, S5_W), F32)],
        scratch_shapes=[pltpu.VMEM((SCAN_CHUNK, sw), F32), pltpu.VMEM((SCAN_CHUNK, sw), F32)],
        compiler_params=_cp(("arbitrary",), 40))


def _s5_bwd(u, dy, d_skip, a, bm, cm, length):
    tp = u.shape[0]
    cw = S5_W // N_BUNDLE
    sw = a.shape[-1]
    half = sw // 2
    chunks = _scan_chunks(length)

    def body(u_ref, dy_ref, d_ref, a_ref, bm_ref, cm_ref, du_ref, dd_ref, dbm_ref, dcm_ref, da_ref, bu_s, g_s, xp_s):
        du_ref[...] = dy_ref[...] * d_ref[...]
        dd_ref[...] = jnp.sum(dy_ref[...] * u_ref[...], axis=0, keepdims=True)
        dbm_ref[...] = jnp.zeros_like(dbm_ref)
        dcm_ref[...] = jnp.zeros_like(dcm_ref)
        zero = (jnp.zeros((1, half), F32), jnp.zeros((1, half), F32))
        for dr in range(2):
            a_re = a_ref[dr, 0, :, 0:half]
            a_im = a_ref[dr, 0, :, half:]
            seq = chunks if dr == 0 else chunks[::-1]
            carry = zero
            for t0, n in seq:
                bu_s[0:n, :] = _dot(u_ref[t0:t0 + n, :].astype(BF16), bm_ref[dr, 0])
                carry = _scan(bu_s, None, xp_s, t0, n, a_re, a_im, carry, dr == 1)
            carry = zero
            da_r = jnp.zeros((1, half), F32)
            da_i = jnp.zeros((1, half), F32)
            for t0, n in seq[::-1]:
                ub = u_ref[t0:t0 + n, :].astype(BF16)
                dyb = dy_ref[t0:t0 + n, :].astype(BF16)
                bu_s[0:n, :] = _dg(dyb, cm_ref[dr, 0], NT)
                carry = _scan(bu_s, g_s, None, 0, n, a_re, -a_im, carry, dr == 0)
                g = g_s[0:n, :]
                gb = g.astype(BF16)
                du_ref[t0:t0 + n, :] += _dg(gb, bm_ref[dr, 0], NT)
                dbm_ref[dr, 0] += _dg(ub, gb, TN)
                xp = xp_s[t0:t0 + n, :]
                xp_r, xp_i = xp[:, 0:half], xp[:, half:]
                g_r, g_i = g[:, 0:half], g[:, half:]
                bu = _dot(ub, bm_ref[dr, 0])
                x_r = a_re * xp_r - a_im * xp_i + bu[:, 0:half]
                x_i = a_re * xp_i + a_im * xp_r + bu[:, half:]
                dcm_ref[dr, 0] += _dg(jnp.concatenate([x_r, x_i], axis=1).astype(BF16), dyb, TN)
                da_r = da_r + jnp.sum(g_r * xp_r + g_i * xp_i, axis=0, keepdims=True)
                da_i = da_i + jnp.sum(g_i * xp_r - g_r * xp_i, axis=0, keepdims=True)
            da_ref[dr, 0] = jnp.concatenate([da_r, da_i], axis=1)

    lp = -(-length // 8) * 8
    return pl.pallas_call(
        body, name="s5_bwd", grid=(N_BUNDLE,),
        in_specs=[pl.BlockSpec((tp, cw), lambda b: (0, b)), pl.BlockSpec((tp, cw), lambda b: (0, b)),
                  pl.BlockSpec((1, cw), lambda b: (0, b)),
                  pl.BlockSpec((2, 1, 1, sw), lambda b: (0, b, 0, 0)),
                  pl.BlockSpec((2, 1, cw, sw), lambda b: (0, b, 0, 0)),
                  pl.BlockSpec((2, 1, sw, cw), lambda b: (0, b, 0, 0))],
        out_specs=[pl.BlockSpec((tp, cw), lambda b: (0, b)), pl.BlockSpec((1, cw), lambda b: (0, b)),
                   pl.BlockSpec((2, 1, cw, sw), lambda b: (0, b, 0, 0)),
                   pl.BlockSpec((2, 1, sw, cw), lambda b: (0, b, 0, 0)),
                   pl.BlockSpec((2, 1, 1, sw), lambda b: (0, b, 0, 0))],
        out_shape=[jax.ShapeDtypeStruct((tp, S5_W), F32), jax.ShapeDtypeStruct((1, S5_W), F32),
                   jax.ShapeDtypeStruct((2, N_BUNDLE, cw, sw), F32), jax.ShapeDtypeStruct((2, N_BUNDLE, sw, cw), F32),
                   jax.ShapeDtypeStruct((2, N_BUNDLE, 1, sw), F32)],
        scratch_shapes=[pltpu.VMEM((SCAN_CHUNK, sw), F32), pltpu.VMEM((SCAN_CHUNK, sw), F32),
                        pltpu.VMEM((lp, sw), F32)],
        compiler_params=_cp(("arbitrary",), 48),
    )(u, dy, d_skip, a, bm, cm)


def _row_tile(tp):
    return max(tm for tm in range(16, 449, 16) if tp % tm == 0)


def _step(x, target, bufs, gains, s5, rpb, c_arr, kc_arr):
    first = ["ffn1_w_gate", "ffn1_w_up", "ffn1_w_down", "meta_tokens"]
    w = dict(zip(first, _run_comm("gather_ffn1", _gather_comm([bufs[n] for n in first]))))
    meta = w["meta_tokens"].transpose(1, 0, 2).reshape(N_META, D)
    n_tok = x.shape[0]
    length = N_META + n_tok
    tp = length + 16
    tm = _row_tile(tp)
    tmb = tm // 2
    n_rows = n_tok // GRID_W
    pad = jnp.zeros((tp - length, D), F32)
    h0 = jnp.concatenate([meta, x, pad], axis=0)
    tgt = jnp.concatenate([jnp.zeros((N_META, D), F32), target, pad], axis=0)

    n2 = 2 * S5_G
    lam_re = s5["lam_re"].reshape(n2, S5_P)
    lam_im = s5["lam_im"].reshape(n2, S5_P)
    log_dt = s5["log_dt"].reshape(n2, 1)
    b_re = s5["b_re"].transpose(0, 1, 3, 2).reshape(n2, S5_H * S5_P)
    b_im = s5["b_im"].transpose(0, 1, 3, 2).reshape(n2, S5_H * S5_P)
    a_re, a_im, bb_re, bb_im = _s5_disc(lam_re, lam_im, log_dt, b_re, b_im)

    def mats(a_re, a_im, bb_re, bb_im, c_re, c_im):
        return _s5_matrices(a_re.reshape(2, S5_G, S5_P), a_im.reshape(2, S5_G, S5_P),
                            bb_re.reshape(2, S5_G, S5_P * S5_H), bb_im.reshape(2, S5_G, S5_P * S5_H), c_re, c_im)

    (a_m, bm, cm), mats_vjp = jax.vjp(mats, a_re, a_im, bb_re, bb_im, s5["c_re"], s5["c_im"])
    bm16 = bm.astype(BF16)
    cm16 = cm.astype(BF16)
    bias = _bias_tables(rpb, n_rows)

    mid = ["w_in", "s5_w_glu", "w_out"]
    (h1, gate1, up1, f1), got = _ffn_fwd(
        "ffn1_fwd", h0, gains["ffn1_pre_g"], gains["ffn1_post_g"], w["ffn1_w_gate"], w["ffn1_w_up"], w["ffn1_w_down"],
        tm, _gather_comm([bufs[n] for n in mid]), (0, (tp // tm) * N_CHIP * 3 // 5))
    w.update(zip(mid, got))
    q, k, v, u = _mix_in(h1, gains["mix_pre_g"], w["w_in"], tm)
    (o_na,), (gate_ici,) = _attn_fwd(q, k, v, bias, n_tok, _gather_comm([bufs["ffn2_w_gate"]], pair=False), (0,))
    (y_pre,), (w["ffn2_w_gate"], up_ici, down_ici) = _s5_fwd(
        u, gains["s5_d"], a_m, bm16, cm16, length,
        _merge_comm(_gather_comm([gate_ici], ici=False),
                    _gather_comm([bufs["ffn2_w_up"], bufs["ffn2_w_down"]], pair=False)), (0,))
    w_glu = w["s5_w_glu"].reshape(S5_W, S5_W)
    w_out = w["w_out"].reshape(D, D)
    (h2, mix), (w["ffn2_w_up"], w["ffn2_w_down"]) = _mix_out(
        o_na, y_pre, h1, w_glu, gains["s5_b_glu"], gains["na_out_g"], gains["s5_out_g"], w_out, gains["mix_post_g"], tm,
        _gather_comm([up_ici, down_ici], ici=False), (0,))
    (h3, gate2, up2, f2), _ = _ffn_fwd("ffn2_fwd", h2, gains["ffn2_pre_g"], gains["ffn2_post_g"],
                                       w["ffn2_w_gate"], w["ffn2_w_up"], w["ffn2_w_down"], tm)
    dh3, df2, loss, dg_final, dg_post2 = _final_loss(h3, gains["final_g"], tgt, f2, gains["ffn2_post_g"], n_tok, tm)

    ffn2 = ["ffn2_w_gate", "ffn2_w_up", "ffn2_w_down"]
    ffn1 = ["ffn1_w_gate", "ffn1_w_up", "ffn1_w_down"]
    out2 = _ffn_bwd("ffn2_bwd", h2, gains["ffn2_pre_g"], df2, gate2, up2,
                    w["ffn2_w_gate"], w["ffn2_w_up"], w["ffn2_w_down"], tmb)
    dxn2 = out2[3]
    sums2 = [_chip_sum("chip_sum_" + n, g, r, c_arr) for n, g, r in zip(ffn2, out2[0:3], out2[4:7])]
    (dh2, dg_pre2), _ = _ffn_pre_bwd("ffn2_pre_bwd", dh3, dxn2, h2, gains["ffn2_pre_g"], tm)
    do_na, dy_pre, dw_out, dw_glu, dg_mpost, dg_na, dg_s5, db_glu = _mix_out_bwd(
        dh2, mix, o_na, y_pre, w_glu, gains["s5_b_glu"], gains["na_out_g"], gains["s5_out_g"], w_out,
        gains["mix_post_g"], tm)
    (dq, dk, dv, dtb), recv3 = _attn_bwd(q, k, v, bias, do_na, n_tok, _scatter_comm(sums2), (0,))
    totals2 = [_total_sum("total_sum_" + n, s, r, kc_arr) for n, s, r in zip(ffn2, sums2, recv3)]
    du, dd, dbm, dcm, da_m = _s5_bwd(u, dy_pre, gains["s5_d"], a_m, bm16, cm16, length)
    (dh1, df1, dw_in, dg_mpre, dg_post1), done2 = _mix_in_bwd(
        dq, dk, dv, du, h1, gains["mix_pre_g"], w["w_in"], dh2, f1, gains["ffn1_post_g"], tm,
        _assemble_comm(totals2), (0,))
    pieces = dict(zip(ffn2, done2))
    out1 = _ffn_bwd("ffn1_bwd", h0, gains["ffn1_pre_g"], df1, gate1, up1,
                    w["ffn1_w_gate"], w["ffn1_w_up"], w["ffn1_w_down"], tmb)
    rest = [dw_in, dw_glu.reshape(N_CHIP, S5_W // N_CHIP, S5_W), dw_out.reshape(N_CHIP, D // N_CHIP, D)]
    (dh0, dg_pre1), recv_rest = _ffn_pre_bwd("ffn1_pre_bwd", dh1, out1[3], h0, gains["ffn1_pre_g"], tm,
                                             _exchange_comm(rest), (0,))
    last = ffn1 + mid
    sums = [_chip_sum("chip_sum_" + n, g, r, c_arr)
            for n, g, r in zip(last, list(out1[0:3]) + rest, list(out1[4:7]) + list(recv_rest))]
    recv3 = _run_comm("grad_chip_scatter", _scatter_comm(sums))
    totals = [_total_sum("total_sum_" + n, s, r, kc_arr) for n, s, r in zip(last, sums, recv3)]
    pieces.update(zip(last, _run_comm("grad_pair_assemble", _assemble_comm(totals))))

    e, _ = _diag_onehot()
    n_dr = 2 * KH - 1
    drpb = _rpb_collapse(dtb.reshape(N_HEADS * n_dr, GRID_W * GRID_W), jnp.asarray(e.T))
    drpb = drpb[:, :2 * KW - 1].reshape(N_HEADS, n_dr, 2 * KW - 1).transpose(1, 0, 2).reshape(N_HEADS * n_dr, 2 * KW - 1)
    da_re, da_im, dbb_re, dbb_im, dc_re, dc_im = mats_vjp((da_m, dbm, dcm))
    dlam_re, dlam_im, dlog_dt, db_re, db_im = _s5_disc_bwd(lam_re, lam_im, log_dt, b_re, b_im,
                                                            da_re, da_im, dbb_re, dbb_im)

    small = {"ffn1_pre_g": dg_pre1, "ffn1_post_g": dg_post1, "mix_pre_g": dg_mpre, "na_rpb": drpb,
             "s5_lam_re": dlam_re, "s5_lam_im": dlam_im, "s5_log_dt": dlog_dt.reshape(2, S5_G),
             "s5_b_re": db_re.reshape(n2 * S5_H, S5_P), "s5_b_im": db_im.reshape(n2 * S5_H, S5_P),
             "s5_c_re": dc_re.reshape(n2 * S5_H, S5_P), "s5_c_im": dc_im.reshape(n2 * S5_H, S5_P), "s5_d": dd, "s5_b_glu": db_glu, "na_out_g": dg_na,
             "s5_out_g": dg_s5, "mix_post_g": dg_mpost, "ffn2_pre_g": dg_pre2, "ffn2_post_g": dg_post2,
             "final_g": dg_final}
    return loss[0, 0], dh0, pieces, small


def _mesh_pos():
    return lax.axis_index("x"), lax.axis_index("y"), lax.axis_index("c")


def _other_chips(x, y):
    return [(1 - x, y), (x, 1 - y), (1 - x, 1 - y)]


class _Comm:
    def __init__(self, ins, out_shape, aliases, parts):
        self.ins, self.out_shape, self.aliases, self.parts = list(ins), list(out_shape), dict(aliases), list(parts)
        self.n_sems = sum(p[0] for p in parts)

    def bases(self):
        out, base = [], 0
        for n_sems, _, _ in self.parts:
            out.append(base)
            base += n_sems
        return out


def _run_comm(name, comm):
    n_i, n_o = len(comm.ins), len(comm.out_shape)

    def body(*refs):
        ins, outs = refs[:n_i], refs[n_i:n_i + n_o]
        send_sems, recv_sems = refs[n_i + n_o:]
        for base, (_, start, finish) in zip(comm.bases(), comm.parts):
            start(ins, outs, send_sems, recv_sems, base)
            finish(ins, outs, send_sems, recv_sems, base)

    return pl.pallas_call(
        body, name=name, out_shape=comm.out_shape, in_specs=[ANY] * n_i, out_specs=[ANY] * n_o,
        input_output_aliases=comm.aliases,
        scratch_shapes=[pltpu.SemaphoreType.DMA((comm.n_sems,)), pltpu.SemaphoreType.DMA((comm.n_sems,))],
    )(*comm.ins)


def _call(body, comm, bounds, args, *, name, grid, in_specs, out_specs, out_shape, scratch_shapes=(),
          compiler_params=None):
    in_specs, out_specs, out_shape, scratch_shapes = list(in_specs), list(out_specs), list(out_shape), list(scratch_shapes)
    if comm is None:
        return pl.pallas_call(body, name=name, grid=grid, in_specs=in_specs, out_specs=out_specs, out_shape=out_shape,
                              scratch_shapes=scratch_shapes, compiler_params=compiler_params)(*args), []
    n_in, n_out, n_scr = len(in_specs), len(out_specs), len(scratch_shapes)
    n_ci, n_co = len(comm.ins), len(comm.out_shape)
    n_steps = int(np.prod(grid))
    assert len(bounds) == len(comm.parts) and all(0 <= b < n_steps for b in bounds) and list(bounds) == sorted(bounds)

    def fused(*refs):
        a = n_in
        b = a + n_ci
        c = b + n_out
        d = c + n_co
        e = d + n_scr
        cargs = (refs[a:b], refs[c:d], refs[e], refs[e + 1])
        step = pl.program_id(0)
        for ax in range(1, len(grid)):
            step = step * grid[ax] + pl.program_id(ax)
        bases = comm.bases()
        for p, (_, start, finish) in enumerate(comm.parts):
            @pl.when(step == bounds[p])
            def _(p=p, start=start):
                if p > 0:
                    comm.parts[p - 1][2](*cargs, bases[p - 1])
                start(*cargs, bases[p])
        body(*(refs[:a] + refs[b:c] + refs[d:e]))

        @pl.when(step == n_steps - 1)
        def _():
            comm.parts[-1][2](*cargs, bases[-1])

    res = pl.pallas_call(
        fused, name=name, grid=grid, in_specs=in_specs + [ANY] * n_ci, out_specs=out_specs + [ANY] * n_co,
        out_shape=out_shape + comm.out_shape,
        scratch_shapes=scratch_shapes + [pltpu.SemaphoreType.DMA((comm.n_sems,)), pltpu.SemaphoreType.DMA((comm.n_sems,))],
        input_output_aliases={n_in + i: n_out + j for i, j in comm.aliases.items()},
        compiler_params=compiler_params)(*args, *comm.ins)
    return res[:n_out], res[n_out:]


def _remote(src, dst, send_sems, recv_sems, idx, to):
    return pltpu.make_async_remote_copy(src_ref=src, dst_ref=dst, send_sem=send_sems.at[idx],
                                        recv_sem=recv_sems.at[idx], device_id=to, device_id_type=MESH_ID)


def _gather_comm(bufs, ici=True, pair=True):
    n = len(bufs)

    def half(ref, k, pc):
        rh = ref.shape[1] // 2
        return ref.at[k, pl.ds(pc * rh, rh), :]

    def ici_start(ins, outs, ss, rs, base):
        x, y, c = _mesh_pos()
        for a in range(n):
            mine = half(outs[a], 2 * x + y, c)
            for j, chip in enumerate(_other_chips(x, y)):
                _remote(mine, mine, ss, rs, base + 3 * a + j, (*chip, c)).start()

    def ici_finish(ins, outs, ss, rs, base):
        x, y, c = _mesh_pos()
        for a in range(n):
            for j, chip in enumerate(_other_chips(x, y)):
                theirs = half(outs[a], 2 * chip[0] + chip[1], c)
                _remote(theirs, theirs, ss, rs, base + 3 * a + j, (*chip, c)).wait()

    def pair_copy(outs, ss, rs, base, a):
        x, y, c = _mesh_pos()
        rh = outs[a].shape[1] // 2
        held = outs[a].at[:, pl.ds(c * rh, rh), :]
        return _remote(held, held, ss, rs, base + a, (x, y, 1 - c))

    def pair_start(ins, outs, ss, rs, base):
        for a in range(n):
            pair_copy(outs, ss, rs, base, a).start()

    def pair_finish(ins, outs, ss, rs, base):
        for a in range(n):
            pair_copy(outs, ss, rs, base, a).wait()

    parts = ([(3 * n, ici_start, ici_finish)] if ici else []) + ([(n, pair_start, pair_finish)] if pair else [])
    return _Comm(bufs, [jax.ShapeDtypeStruct(b.shape, b.dtype) for b in bufs], {a: a for a in range(n)}, parts)


def _merge_comm(*comms):
    ins, shapes, aliases, subs, base = [], [], {}, [], 0
    for cm in comms:
        (n_sems, start, finish), = cm.parts
        i0, o0 = len(ins), len(shapes)
        subs.append((slice(i0, i0 + len(cm.ins)), slice(o0, o0 + len(cm.out_shape)), base, start, finish))
        aliases.update({i0 + i: o0 + j for i, j in cm.aliases.items()})
        ins += cm.ins
        shapes += cm.out_shape
        base += n_sems

    def start_all(ins_r, outs_r, ss, rs, b):
        for si, so, off, start, _ in subs:
            start(ins_r[si], outs_r[so], ss, rs, b + off)

    def finish_all(ins_r, outs_r, ss, rs, b):
        for si, so, off, _, finish in subs:
            finish(ins_r[si], outs_r[so], ss, rs, b + off)

    return _Comm(ins, shapes, aliases, [(base, start_all, finish_all)])


def _own_half_buffers(pieces, dtypes, kc_arr):
    n = len(pieces)

    def body(kc_ref, *refs):
        for a in range(n):
            refs[n + a][0] = refs[a][...].astype(dtypes[a])

    def half(p):
        return p.shape[0] // 2, p.shape[1]

    return pl.pallas_call(
        body, name="own_halves",
        out_shape=[jax.ShapeDtypeStruct((N_CHIP,) + p.shape, dt) for p, dt in zip(pieces, dtypes)],
        grid_spec=pltpu.PrefetchScalarGridSpec(
            num_scalar_prefetch=1, grid=(1,),
            in_specs=[pl.BlockSpec(half(p), lambda i, kc: (kc[1], 0)) for p in pieces],
            out_specs=[pl.BlockSpec((1,) + half(p), lambda i, kc: (kc[0], kc[1], 0)) for p in pieces]),
        compiler_params=_cp(("arbitrary",), 48),
    )(kc_arr, *pieces)


def _exchange_comm(grads):
    n = len(grads)

    def copy(ins, outs, ss, rs, base, a):
        x, y, c = _mesh_pos()
        rh = ins[a].shape[1] // 2
        return _remote(ins[a].at[:, pl.ds((1 - c) * rh, rh), :], outs[a], ss, rs, base + a, (x, y, 1 - c))

    def start(ins, outs, ss, rs, base):
        for a in range(n):
            copy(ins, outs, ss, rs, base, a).start()

    def finish(ins, outs, ss, rs, base):
        for a in range(n):
            copy(ins, outs, ss, rs, base, a).wait()

    shapes = [jax.ShapeDtypeStruct((N_CHIP, g.shape[1] // 2, g.shape[2]), g.dtype) for g in grads]
    return _Comm(grads, shapes, {}, [(n, start, finish)])


def _chip_sum(name, g, recv, c_arr):
    _, r, cc = g.shape
    rh = r // 2

    def body(c_ref, g_ref, r_ref, o_ref):
        o_ref[...] = (g_ref[...] + r_ref[...]).astype(BF16)

    return pl.pallas_call(
        body, name=name, out_shape=jax.ShapeDtypeStruct((N_CHIP, rh, cc), BF16),
        grid_spec=pltpu.PrefetchScalarGridSpec(
            num_scalar_prefetch=1, grid=(N_CHIP,),
            in_specs=[pl.BlockSpec((1, rh, cc), lambda j, c_ref: (j, c_ref[0], 0)),
                      pl.BlockSpec((1, rh, cc), lambda j, c_ref: (j, 0, 0))],
            out_specs=pl.BlockSpec((1, rh, cc), lambda j, c_ref: (j, 0, 0))),
        compiler_params=_cp(("arbitrary",), 32),
    )(c_arr, g, recv)


def _scatter_comm(sums):
    n = len(sums)

    def copies(ins, outs, ss, rs, base):
        x, y, c = _mesh_pos()
        return [_remote(ins[a].at[2 * chip[0] + chip[1]], outs[a].at[j], ss, rs, base + 3 * a + j, (*chip, c))
                for a in range(n) for j, chip in enumerate(_other_chips(x, y))]

    def start(ins, outs, ss, rs, base):
        for cp in copies(ins, outs, ss, rs, base):
            cp.start()

    def finish(ins, outs, ss, rs, base):
        for cp in copies(ins, outs, ss, rs, base):
            cp.wait()

    shapes = [jax.ShapeDtypeStruct((3,) + s.shape[1:], s.dtype) for s in sums]
    return _Comm(sums, shapes, {}, [(3 * n, start, finish)])


def _total_sum(name, sums, recv3, kc_arr):
    _, rh, cc = sums.shape

    def body(kc_ref, s_ref, r_ref, o_ref):
        t = s_ref[0].astype(F32) + r_ref[0].astype(F32)
        t = t + r_ref[1].astype(F32)
        o_ref[...] = t + r_ref[2].astype(F32)

    return pl.pallas_call(
        body, name=name, out_shape=jax.ShapeDtypeStruct((2 * rh, cc), F32),
        grid_spec=pltpu.PrefetchScalarGridSpec(
            num_scalar_prefetch=1, grid=(1,),
            in_specs=[pl.BlockSpec((1, rh, cc), lambda i, kc_ref: (kc_ref[0], 0, 0)),
                      pl.BlockSpec((3, rh, cc), lambda i, kc_ref: (0, 0, 0))],
            out_specs=pl.BlockSpec((rh, cc), lambda i, kc_ref: (kc_ref[1], 0))),
        compiler_params=_cp(("arbitrary",), 32),
    )(kc_arr, sums, recv3)


def _assemble_comm(totals):
    n = len(totals)

    def copy(outs, ss, rs, base, a):
        x, y, c = _mesh_pos()
        rh = outs[a].shape[0] // 2
        here = outs[a].at[pl.ds(c * rh, rh), :]
        return _remote(here, here, ss, rs, base + a, (x, y, 1 - c))

    def start(ins, outs, ss, rs, base):
        for a in range(n):
            copy(outs, ss, rs, base, a).start()

    def finish(ins, outs, ss, rs, base):
        for a in range(n):
            copy(outs, ss, rs, base, a).wait()

    shapes = [jax.ShapeDtypeStruct(t.shape, t.dtype) for t in totals]
    return _Comm(totals, shapes, {a: a for a in range(n)}, [(n, start, finish)])


def _small_allreduce(arrays):
    n = len(arrays)

    def body(*refs):
        ins, outs, sib, csum, every = (refs[i * n:(i + 1) * n] for i in range(5))
        send_sems, recv_sems = refs[5 * n:]
        x, y, c = _mesh_pos()
        k = 2 * x + y
        cps = [_remote(ins[a], sib[a], send_sems, recv_sems, a, (x, y, 1 - c)) for a in range(n)]
        for cp in cps:
            cp.start()
        for cp in cps:
            cp.wait()
        for a in range(n):
            csum[a][...] = ins[a][...] + sib[a][...]
            every[a][k] = csum[a][...]
        cps = [_remote(csum[a], every[a].at[k], send_sems, recv_sems, n + 3 * a + j, (*chip, c))
               for a in range(n) for j, chip in enumerate(_other_chips(x, y))]
        for cp in cps:
            cp.start()
        for cp in cps:
            cp.wait()
        for a in range(n):
            outs[a][...] = ((every[a][0] + every[a][1]) + every[a][2]) + every[a][3]

    vm = pl.BlockSpec(memory_space=pltpu.VMEM)
    shapes = [a.shape for a in arrays]
    return pl.pallas_call(
        body, name="small_allreduce", out_shape=[jax.ShapeDtypeStruct(s, F32) for s in shapes],
        in_specs=[vm] * n, out_specs=[vm] * n,
        scratch_shapes=[pltpu.VMEM(s, F32) for s in shapes] * 2 + [pltpu.VMEM((N_CHIP,) + s, F32) for s in shapes] +
                       [pltpu.SemaphoreType.DMA((4 * n,)), pltpu.SemaphoreType.DMA((4 * n,))],
        compiler_params=_cp(None, 32),
    )(*arrays)


def _adamw_small(ws, gs, ms, vs):
    n = len(ws)

    def body(*refs):
        w, g, m, v, d, mo, vo = (refs[i * n:(i + 1) * n] for i in range(7))
        for a in range(n):
            d[a][...], mo[a][...], vo[a][...] = _adamw_math(w[a][...], g[a][...], m[a][...], v[a][...])

    vm = pl.BlockSpec(memory_space=pltpu.VMEM)
    res = pl.pallas_call(
        body, name="adamw_small", out_shape=[jax.ShapeDtypeStruct(w.shape, F32) for w in ws] * 3,
        in_specs=[vm] * (4 * n), out_specs=[vm] * (3 * n), compiler_params=_cp(None, 40),
    )(*ws, *gs, *ms, *vs)
    return res[:n], res[n:2 * n], res[2 * n:]


def _adamw_math(w, g, m, v):
    m = ADAM_B1 * m + (1.0 - ADAM_B1) * g
    v = ADAM_B2 * v + (1.0 - ADAM_B2) * (g * g)
    m_hat = m / (1.0 - ADAM_B1 ** ADAM_STEP)
    v_hat = v / (1.0 - ADAM_B2 ** ADAM_STEP)
    delta = -ADAM_LR * (m_hat / (jnp.sqrt(v_hat) + ADAM_EPS) + ADAM_WD * w)
    return delta, m, v


def _adamw(name, w, g, m, v):
    r, c = w.shape
    tr = max(t for t in range(8, 513, 8) if r % t == 0)

    def body(w_ref, g_ref, m_ref, v_ref, d_ref, mo_ref, vo_ref):
        d_ref[...], mo_ref[...], vo_ref[...] = _adamw_math(w_ref[...], g_ref[...], m_ref[...], v_ref[...])

    return pl.pallas_call(
        body, name=name, grid=(r // tr,), in_specs=[_rows(tr, c)] * 4, out_specs=[_rows(tr, c)] * 3,
        out_shape=[jax.ShapeDtypeStruct((r, c), F32)] * 3, compiler_params=_cp(("arbitrary",), 32),
    )(w, g, m, v)


def _as_matrix(name, a):
    if name == "na_rpb":
        return a[0].transpose(1, 0, 2).reshape(N_HEADS * (2 * KH - 1), 2 * KW - 1)
    if name in ("s5_b_re", "s5_b_im"):
        return a.transpose(0, 1, 2, 4, 3).reshape(2 * S5_G * S5_H, S5_P)
    if name in ("s5_c_re", "s5_c_im"):
        return a.reshape(2 * S5_G * S5_H, S5_P)
    if name in ("s5_lam_re", "s5_lam_im"):
        return a.reshape(2 * S5_G, S5_P)
    if name == "s5_log_dt":
        return a.reshape(2, S5_G)
    return a


def _from_matrix(name, m):
    if name == "na_rpb":
        return m.reshape(2 * KH - 1, N_HEADS, 2 * KW - 1).transpose(1, 0, 2)[None]
    if name in ("s5_b_re", "s5_b_im"):
        return m.reshape(1, 2, S5_G, S5_H, S5_P).transpose(0, 1, 2, 4, 3)
    if name in ("s5_c_re", "s5_c_im"):
        return m.reshape(1, 2, S5_G, S5_H, S5_P)
    if name in ("s5_lam_re", "s5_lam_im"):
        return m.reshape(1, 2, S5_G, S5_P)
    if name == "s5_log_dt":
        return m.reshape(1, 2, S5_G)
    return m


WEIGHTS = ["meta_tokens", "ffn1_pre_g", "ffn1_post_g", "ffn1_w_gate", "ffn1_w_up", "ffn1_w_down", "mix_pre_g", "w_in",
           "na_rpb", "s5_lam_re", "s5_lam_im", "s5_log_dt", "s5_b_re", "s5_b_im", "s5_c_re", "s5_c_im", "s5_d",
           "s5_w_glu", "s5_b_glu", "na_out_g", "s5_out_g", "w_out", "mix_post_g", "ffn2_pre_g", "ffn2_post_g",
           "ffn2_w_gate", "ffn2_w_up", "ffn2_w_down", "final_g"]
BIG = ["ffn1_w_gate", "ffn1_w_up", "ffn1_w_down", "w_in", "s5_w_glu", "w_out", "ffn2_w_gate", "ffn2_w_up",
       "ffn2_w_down"]
TRANSPOSED = ["ffn1_w_gate", "ffn1_w_up", "ffn2_w_gate", "ffn2_w_up"]
GAINS = ["ffn1_pre_g", "ffn1_post_g", "mix_pre_g", "s5_d", "s5_b_glu", "na_out_g", "s5_out_g", "mix_post_g",
         "ffn2_pre_g", "ffn2_post_g", "final_g"]
SMALL = [n for n in WEIGHTS if n not in BIG]


def kernel(*args):
    names = ["x"] + WEIGHTS + ["loss_target"] + ["m_" + n for n in WEIGHTS] + ["v_" + n for n in WEIGHTS]
    assert len(args) == len(names)
    given = dict(zip(names, args))
    x_pos, y_pos, c_pos = _mesh_pos()
    k_pos = 2 * x_pos + y_pos
    c_arr = jnp.reshape(c_pos, (1,)).astype(jnp.int32)
    kc_arr = jnp.stack([k_pos, c_pos]).astype(jnp.int32)

    def piece(name, a):
        return a[0].T if name in TRANSPOSED else a[0]

    def unpiece(name, a):
        return a.T[None] if name in TRANSPOSED else a[None]

    placed = BIG + ["meta_tokens"]
    bufs = dict(zip(placed, _own_half_buffers([piece(n, given[n]) for n in BIG] + [given["meta_tokens"]],
                                              [BF16] * len(BIG) + [F32], kc_arr)))

    gains = {n: given[n] for n in GAINS}
    s5 = {n: given["s5_" + n][0] for n in ["lam_re", "lam_im", "log_dt", "b_re", "b_im", "c_re", "c_im"]}
    loss, dh0, pieces, small = _step(given["x"][0], given["loss_target"][0], bufs, gains, s5, given["na_rpb"][0],
                                     c_arr, kc_arr)
    loss = lax.psum(loss, ("x", "y", "c"))
    n_tok = given["x"].shape[1]
    grad_x = dh0[N_META:N_META + n_tok][None]

    small["meta_tokens"] = dh0[:N_META]
    small = dict(zip(SMALL, _small_allreduce([small[n] for n in SMALL])))
    mc = D // N_CHIP
    small["meta_tokens"] = lax.dynamic_slice_in_dim(small["meta_tokens"], k_pos * mc, mc, 1)

    out_g, out_d, out_m, out_v = {}, {}, {}, {}
    for n in BIG:
        g2 = pieces[n]
        d2, m2, v2 = _adamw("adamw_" + n, piece(n, given[n]), g2, piece(n, given["m_" + n]),
                            piece(n, given["v_" + n]))
        out_g[n], out_d[n], out_m[n], out_v[n] = (unpiece(n, t) for t in (g2, d2, m2, v2))
    gs = [small[n] for n in SMALL]
    d2, m2, v2 = _adamw_small([_as_matrix(n, given[n]) for n in SMALL], gs,
                              [_as_matrix(n, given["m_" + n]) for n in SMALL],
                              [_as_matrix(n, given["v_" + n]) for n in SMALL])
    for n, g, dd, mm, vv in zip(SMALL, gs, d2, m2, v2):
        out_g[n], out_d[n], out_m[n], out_v[n] = (_from_matrix(n, t) for t in (g, dd, mm, vv))
    return (loss, grad_x, *[out_g[n] for n in WEIGHTS], *[out_d[n] for n in WEIGHTS],
            *[out_m[n] for n in WEIGHTS], *[out_v[n] for n in WEIGHTS])
```

```python
import functools
import math

import numpy as np
import jax
import jax.numpy as jnp
from jax import lax
from jax.experimental import pallas as pl
from jax.experimental.pallas import tpu as pltpu

F32 = jnp.float32
BF16 = jnp.bfloat16

D = 1024
N_META = 16
GRID_W = 64
NA_W = 512
S5_W = 512
HEAD_DIM = 64
N_HEADS = 8
KH = 8
KW = 16
S5_G = 32
S5_P = 64
S5_H = 16
N_BUNDLE = 4
FF = 2816
N_CHIP = 4
FC = FF // N_CHIP
EPS = 1e-6
NEG_INF = -1e30
Q_ROWS = 4
K_ROWS = 12
QB = Q_ROWS * GRID_W
KB = K_ROWS * GRID_W
SCAN_CHUNK = 256

ADAM_LR = 0.001
ADAM_B1 = 0.9
ADAM_B2 = 0.999
ADAM_EPS = 1e-08
ADAM_WD = 0.01
ADAM_STEP = 10

NT = (((1,), (1,)), ((), ()))
TN = (((0,), (0,)), ((), ()))
MESH_ID = pl.DeviceIdType.MESH


def _cp(sem=None, vmem_mb=None):
    kw = {}
    if sem is not None:
        kw["dimension_semantics"] = sem
    if vmem_mb is not None:
        kw["vmem_limit_bytes"] = vmem_mb << 20
    return pltpu.CompilerParams(**kw)


def _full(shape):
    n = len(shape)
    return pl.BlockSpec(shape, lambda *_: (0,) * n)


def _rows(tm, w):
    return pl.BlockSpec((tm, w), lambda i: (i, 0))


ANY = pl.BlockSpec(memory_space=pl.ANY)


def _rms(x, g):
    r = lax.rsqrt(jnp.mean(x * x, axis=-1, keepdims=True) + EPS)
    return x * r * g


def _rms_bwd(x, g, dy):
    r = lax.rsqrt(jnp.mean(x * x, axis=-1, keepdims=True) + EPS)
    xh = x * r
    dg = jnp.sum(dy * xh, axis=0, keepdims=True)
    dyg = dy * g
    dx = r * (dyg - xh * jnp.mean(dyg * xh, axis=-1, keepdims=True))
    return dx, dg


def _dot(a, b):
    return jnp.dot(a, b, preferred_element_type=F32)


def _dg(a, b, dims):
    return lax.dot_general(a, b, dims, preferred_element_type=F32)


def _ffn_fwd(name, h, g_pre, g_post, wg, wu, wd, tm, comm=None, bounds=()):
    tp = h.shape[0]
    nt = tp // tm

    def body(h_ref, gp_ref, gq_ref, wg_ref, wu_ref, wd_ref, hn_ref, gate_ref, up_ref, f_ref, xn_s, acc_s):
        c = pl.program_id(1)

        @pl.when(c == 0)
        def _():
            xn_s[...] = _rms(h_ref[...], gp_ref[...]).astype(BF16)
            acc_s[...] = jnp.zeros_like(acc_s)

        xn = xn_s[...]
        gate = _dg(xn, wg_ref[0], NT)
        up = _dg(xn, wu_ref[0], NT)
        gate_ref[0] = gate
        up_ref[0] = up
        act = (gate * jax.nn.sigmoid(gate) * up).astype(BF16)
        acc_s[...] += _dot(act, wd_ref[0])

        @pl.when(c == N_CHIP - 1)
        def _():
            f = acc_s[...]
            f_ref[...] = f
            hn_ref[...] = h_ref[...] + 0.5 * _rms(f, gq_ref[...])

    return _call(
        body, comm, bounds, (h, g_pre, g_post, wg, wu, wd), name=name, grid=(nt, N_CHIP),
        in_specs=[pl.BlockSpec((tm, D), lambda i, c: (i, 0)), _full((1, D)), _full((1, D))] +
                 [pl.BlockSpec((1, FC, D), lambda i, c: (c, 0, 0))] * 3,
        out_specs=[pl.BlockSpec((tm, D), lambda i, c: (i, 0)),
                   pl.BlockSpec((1, tm, FC), lambda i, c: (c, i, 0)),
                   pl.BlockSpec((1, tm, FC), lambda i, c: (c, i, 0)),
                   pl.BlockSpec((tm, D), lambda i, c: (i, 0))],
        out_shape=[jax.ShapeDtypeStruct((tp, D), F32), jax.ShapeDtypeStruct((N_CHIP, tp, FC), F32),
                   jax.ShapeDtypeStruct((N_CHIP, tp, FC), F32), jax.ShapeDtypeStruct((tp, D), F32)],
        scratch_shapes=[pltpu.VMEM((tm, D), BF16), pltpu.VMEM((tm, D), F32)],
        compiler_params=_cp(("arbitrary", "arbitrary"), 48))


def _ffn_bwd(name, h, g_pre, df, gate, up, wg, wu, wd, tm):
    tp = h.shape[0]
    nt = tp // tm
    rh = FC // 2

    def body(h_ref, gp_ref, df_ref, gate_ref, up_ref, wg_ref, wu_ref, wd_ref,
             dwg_ref, dwu_ref, dwd_ref, dxn_ref, rg_ref, ru_ref, rd_ref, ag, au, ad, send_sems, recv_sems):
        c = pl.program_id(0)
        i = pl.program_id(1)

        def to_sibling(a, piece):
            x, y, core = _mesh_pos()
            dw_ref, r_ref = ((dwg_ref, rg_ref), (dwu_ref, ru_ref), (dwd_ref, rd_ref))[a]
            return _remote(dw_ref.at[piece, pl.ds((1 - core) * rh, rh), :], r_ref.at[piece], send_sems, recv_sems,
                           3 * piece + a, (x, y, 1 - core))

        @pl.when(i == 0)
        def _():
            ag[...] = jnp.zeros_like(ag)
            au[...] = jnp.zeros_like(au)
            ad[...] = jnp.zeros_like(ad)

        xn = _rms(h_ref[...], gp_ref[...]).astype(BF16)
        dfb = df_ref[...].astype(BF16)
        gt = gate_ref[0]
        u = up_ref[0]
        sg = jax.nn.sigmoid(gt)
        si = gt * sg
        act = (si * u).astype(BF16)
        dact = _dg(dfb, wd_ref[0], NT)
        ad[...] += _dg(act, dfb, TN)
        dgate = (dact * u * (sg * (1.0 + gt * (1.0 - sg)))).astype(BF16)
        dup = (dact * si).astype(BF16)
        ag[...] += _dg(dgate, xn, TN)
        au[...] += _dg(dup, xn, TN)
        dxn_ref[0] = _dot(dgate, wg_ref[0]) + _dot(dup, wu_ref[0])

        @pl.when(i == nt - 1)
        def _():
            pltpu.sync_copy(ag, dwg_ref.at[c])
            pltpu.sync_copy(au, dwu_ref.at[c])
            pltpu.sync_copy(ad, dwd_ref.at[c])
            for a in range(3):
                to_sibling(a, c).start()

        @pl.when((c == N_CHIP - 1) & (i == nt - 1))
        def _():
            for piece in range(N_CHIP):
                for a in range(3):
                    to_sibling(a, piece).wait()

    return pl.pallas_call(
        body, name=name, grid=(N_CHIP, nt),
        in_specs=[pl.BlockSpec((tm, D), lambda c, i: (i, 0)), _full((1, D)),
                  pl.BlockSpec((tm, D), lambda c, i: (i, 0)),
                  pl.BlockSpec((1, tm, FC), lambda c, i: (c, i, 0)),
                  pl.BlockSpec((1, tm, FC), lambda c, i: (c, i, 0))] +
                 [pl.BlockSpec((1, FC, D), lambda c, i: (c, 0, 0))] * 3,
        out_specs=[ANY, ANY, ANY, pl.BlockSpec((1, tm, D), lambda c, i: (c, i, 0)), ANY, ANY, ANY],
        out_shape=[jax.ShapeDtypeStruct((N_CHIP, FC, D), F32)] * 3 + [jax.ShapeDtypeStruct((N_CHIP, tp, D), F32)] +
                  [jax.ShapeDtypeStruct((N_CHIP, rh, D), F32)] * 3,
        scratch_shapes=[pltpu.VMEM((FC, D), F32)] * 3 +
                       [pltpu.SemaphoreType.DMA((3 * N_CHIP,)), pltpu.SemaphoreType.DMA((3 * N_CHIP,))],
        compiler_params=_cp(("arbitrary", "arbitrary"), 58),
    )(h, g_pre, df, gate, up, wg, wu, wd)


def _ffn_pre_bwd(name, dh, dxn_part, h, g_pre, tm, comm=None, bounds=()):
    tp = h.shape[0]
    nt = tp // tm

    def body(dh_ref, dxn_ref, h_ref, gp_ref, out_ref, dg_ref):
        i = pl.program_id(0)
        dxn = (dxn_ref[0] + dxn_ref[1]) + (dxn_ref[2] + dxn_ref[3])
        dx, dg = _rms_bwd(h_ref[...], gp_ref[...], dxn)
        out_ref[...] = dh_ref[...] + dx

        @pl.when(i == 0)
        def _():
            dg_ref[...] = jnp.zeros_like(dg_ref)

        dg_ref[...] += dg

    return _call(
        body, comm, bounds, (dh, dxn_part, h, g_pre), name=name, grid=(nt,),
        in_specs=[_rows(tm, D), pl.BlockSpec((N_CHIP, tm, D), lambda i: (0, i, 0)), _rows(tm, D), _full((1, D))],
        out_specs=[_rows(tm, D), _full((1, D))],
        out_shape=[jax.ShapeDtypeStruct((tp, D), F32), jax.ShapeDtypeStruct((1, D), F32)],
        compiler_params=_cp(("arbitrary",), 48))


def _mix_in(h, g, w_in, tm):
    tp = h.shape[0]

    def body(h_ref, g_ref, w_ref, q_ref, k_ref, v_ref, u_ref):
        a = _rms(h_ref[...], g_ref[...]).astype(BF16)
        q_ref[...] = _dot(a, w_ref[0]).astype(BF16)
        k_ref[...] = _dot(a, w_ref[1]).astype(BF16)
        v_ref[...] = _dot(a, w_ref[2]).astype(BF16)
        u_ref[...] = _dot(a, w_ref[3])

    return pl.pallas_call(
        body, name="mix_in", grid=(tp // tm,),
        in_specs=[_rows(tm, D), _full((1, D)), _full((N_CHIP, D, NA_W))],
        out_specs=[_rows(tm, NA_W)] * 4,
        out_shape=[jax.ShapeDtypeStruct((tp, NA_W), BF16)] * 3 + [jax.ShapeDtypeStruct((tp, S5_W), F32)],
        compiler_params=_cp(("arbitrary",), 40),
    )(h, g, w_in)


def _gelu(x):
    return jax.nn.gelu(x, approximate=True)


def _gelu_grad(x):
    k = math.sqrt(2.0 / math.pi)
    t = jnp.tanh(k * (x + 0.044715 * x * x * x))
    return 0.5 * (1.0 + t) + 0.5 * x * (1.0 - t * t) * k * (1.0 + 3.0 * 0.044715 * x * x)


def _mix_out(o_na, y_pre, h, w_glu, b_glu, g_na, g_s5, w_out, g_post, tm, comm=None, bounds=()):
    tp = h.shape[0]

    def body(ona_ref, yp_ref, h_ref, wglu_ref, bglu_ref, gna_ref, gs5_ref, wout_ref, gpost_ref, hn_ref, mix_ref):
        y = _gelu(yp_ref[...])
        z = _dot(y.astype(BF16), wglu_ref[...]) + bglu_ref[...]
        o_s5 = y * jax.nn.sigmoid(z)
        n1 = _rms(ona_ref[...], gna_ref[...]).astype(BF16)
        n2 = _rms(o_s5, gs5_ref[...]).astype(BF16)
        mix = _dot(n1, wout_ref[0:NA_W, :]) + _dot(n2, wout_ref[NA_W:, :])
        mix_ref[...] = mix
        hn_ref[...] = h_ref[...] + _rms(mix, gpost_ref[...])

    return _call(
        body, comm, bounds, (o_na, y_pre, h, w_glu, b_glu, g_na, g_s5, w_out, g_post), name="mix_out",
        grid=(tp // tm,),
        in_specs=[_rows(tm, NA_W), _rows(tm, S5_W), _rows(tm, D), _full((S5_W, S5_W)), _full((1, S5_W)),
                  _full((1, NA_W)), _full((1, S5_W)), _full((D, D)), _full((1, D))],
        out_specs=[_rows(tm, D), _rows(tm, D)],
        out_shape=[jax.ShapeDtypeStruct((tp, D), F32)] * 2,
        compiler_params=_cp(("arbitrary",), 40))


def _mix_out_bwd(dh, mix, o_na, y_pre, w_glu, b_glu, g_na, g_s5, w_out, g_post, tm):
    tp = dh.shape[0]
    nt = tp // tm

    def body(dh_ref, mix_ref, ona_ref, yp_ref, wglu_ref, bglu_ref, gna_ref, gs5_ref, wout_ref, gpost_ref,
             dona_ref, dyp_ref, dwout_ref, dwglu_ref, dgpost_ref, dgna_ref, dgs5_ref, dbglu_ref, a_out, a_glu):
        i = pl.program_id(0)

        @pl.when(i == 0)
        def _():
            a_out[...] = jnp.zeros_like(a_out)
            a_glu[...] = jnp.zeros_like(a_glu)
            dgpost_ref[...] = jnp.zeros_like(dgpost_ref)
            dgna_ref[...] = jnp.zeros_like(dgna_ref)
            dgs5_ref[...] = jnp.zeros_like(dgs5_ref)
            dbglu_ref[...] = jnp.zeros_like(dbglu_ref)

        dmix, dgpost = _rms_bwd(mix_ref[...], gpost_ref[...], dh_ref[...])
        dgpost_ref[...] += dgpost
        yp = yp_ref[...]
        y = _gelu(yp)
        yb = y.astype(BF16)
        z = _dot(yb, wglu_ref[...]) + bglu_ref[...]
        sg = jax.nn.sigmoid(z)
        o_s5 = y * sg
        o_na = ona_ref[...]
        n1 = _rms(o_na, gna_ref[...]).astype(BF16)
        n2 = _rms(o_s5, gs5_ref[...]).astype(BF16)
        dmb = dmix.astype(BF16)
        a_out[0:NA_W, :] += _dg(n1, dmb, TN)
        a_out[NA_W:, :] += _dg(n2, dmb, TN)
        dn1 = _dg(dmb, wout_ref[0:NA_W, :], NT)
        dn2 = _dg(dmb, wout_ref[NA_W:, :], NT)
        dona, dgna = _rms_bwd(o_na, gna_ref[...], dn1)
        dona_ref[...] = dona
        dgna_ref[...] += dgna
        dos5, dgs5 = _rms_bwd(o_s5, gs5_ref[...], dn2)
        dgs5_ref[...] += dgs5
        dz = dos5 * y * (sg * (1.0 - sg))
        dbglu_ref[...] += jnp.sum(dz, axis=0, keepdims=True)
        dzb = dz.astype(BF16)
        a_glu[...] += _dg(yb, dzb, TN)
        dy = dos5 * sg + _dg(dzb, wglu_ref[...], NT)
        dyp_ref[...] = dy * _gelu_grad(yp)

        @pl.when(i == nt - 1)
        def _():
            pltpu.sync_copy(a_out, dwout_ref)
            pltpu.sync_copy(a_glu, dwglu_ref)

    return pl.pallas_call(
        body, name="mix_out_bwd", grid=(nt,),
        in_specs=[_rows(tm, D), _rows(tm, D), _rows(tm, NA_W), _rows(tm, S5_W), _full((S5_W, S5_W)),
                  _full((1, S5_W)), _full((1, NA_W)), _full((1, S5_W)), _full((D, D)), _full((1, D))],
        out_specs=[_rows(tm, NA_W), _rows(tm, S5_W), ANY, ANY, _full((1, D)), _full((1, NA_W)),
                   _full((1, S5_W)), _full((1, S5_W))],
        out_shape=[jax.ShapeDtypeStruct((tp, NA_W), F32), jax.ShapeDtypeStruct((tp, S5_W), F32),
                   jax.ShapeDtypeStruct((D, D), F32), jax.ShapeDtypeStruct((S5_W, S5_W), F32),
                   jax.ShapeDtypeStruct((1, D), F32), jax.ShapeDtypeStruct((1, NA_W), F32),
                   jax.ShapeDtypeStruct((1, S5_W), F32), jax.ShapeDtypeStruct((1, S5_W), F32)],
        scratch_shapes=[pltpu.VMEM((D, D), F32), pltpu.VMEM((S5_W, S5_W), F32)],
        compiler_params=_cp(("arbitrary",), 48),
    )(dh, mix, o_na, y_pre, w_glu, b_glu, g_na, g_s5, w_out, g_post)


def _mix_in_bwd(dq, dk, dv, du, h, g, w_in, dh, f1, g_post1, tm, comm=None, bounds=()):
    tp = h.shape[0]
    nt = tp // tm

    def body(dq_ref, dk_ref, dv_ref, du_ref, h_ref, g_ref, w_ref, dh_ref, f_ref, gq_ref,
             dh1_ref, df_ref, dw_ref, dg_ref, dgq_ref, acc):
        i = pl.program_id(0)

        @pl.when(i == 0)
        def _():
            acc[...] = jnp.zeros_like(acc)
            dg_ref[...] = jnp.zeros_like(dg_ref)
            dgq_ref[...] = jnp.zeros_like(dgq_ref)

        x = h_ref[...]
        a = _rms(x, g_ref[...]).astype(BF16)
        da = jnp.zeros((tm, D), F32)
        for j, r in enumerate((dq_ref, dk_ref, dv_ref, du_ref)):
            dp = r[...].astype(BF16)
            da = da + _dg(dp, w_ref[j], NT)
            acc[j] += _dg(a, dp, TN)
        dx, dg = _rms_bwd(x, g_ref[...], da)
        dh1 = dh_ref[...] + dx
        dh1_ref[...] = dh1
        dg_ref[...] += dg
        df, dgq = _rms_bwd(f_ref[...], gq_ref[...], 0.5 * dh1)
        df_ref[...] = df
        dgq_ref[...] += dgq

        @pl.when(i == nt - 1)
        def _():
            pltpu.sync_copy(acc, dw_ref)

    return _call(
        body, comm, bounds, (dq, dk, dv, du, h, g, w_in, dh, f1, g_post1), name="mix_in_bwd", grid=(nt,),
        in_specs=[_rows(tm, NA_W)] * 4 + [_rows(tm, D), _full((1, D)), _full((N_CHIP, D, NA_W)), _rows(tm, D),
                                         _rows(tm, D), _full((1, D))],
        out_specs=[_rows(tm, D), _rows(tm, D), ANY, _full((1, D)), _full((1, D))],
        out_shape=[jax.ShapeDtypeStruct((tp, D), F32), jax.ShapeDtypeStruct((tp, D), F32),
                   jax.ShapeDtypeStruct((N_CHIP, D, NA_W), F32), jax.ShapeDtypeStruct((1, D), F32),
                   jax.ShapeDtypeStruct((1, D), F32)],
        scratch_shapes=[pltpu.VMEM((N_CHIP, D, NA_W), F32)],
        compiler_params=_cp(("arbitrary",), 48))


def _final_loss(h, g_final, target, f2, g_post2, n_tok, tm):
    tp = h.shape[0]

    def body(h_ref, g_ref, t_ref, f_ref, gq_ref, dh_ref, df_ref, loss_ref, dg_ref, dgq_ref):
        i = pl.program_id(0)

        @pl.when(i == 0)
        def _():
            loss_ref[...] = jnp.zeros_like(loss_ref)
            dg_ref[...] = jnp.zeros_like(dg_ref)
            dgq_ref[...] = jnp.zeros_like(dgq_ref)

        x = h_ref[...]
        y = _rms(x, g_ref[...])
        row = i * tm + lax.broadcasted_iota(jnp.int32, (tm, 1), 0)
        valid = (row >= N_META) & (row < N_META + n_tok)
        e = jnp.where(valid, y - t_ref[...], 0.0)
        loss_ref[...] += 0.5 * jnp.sum(jnp.mean(e * e, axis=-1, keepdims=True), axis=0, keepdims=True)
        dx, dg = _rms_bwd(x, g_ref[...], e * (1.0 / D))
        dh_ref[...] = dx
        dg_ref[...] += dg
        df, dgq = _rms_bwd(f_ref[...], gq_ref[...], 0.5 * dx)
        df_ref[...] = df
        dgq_ref[...] += dgq

    return pl.pallas_call(
        body, name="final_loss", grid=(tp // tm,),
        in_specs=[_rows(tm, D), _full((1, D)), _rows(tm, D), _rows(tm, D), _full((1, D))],
        out_specs=[_rows(tm, D), _rows(tm, D), _full((1, 1)), _full((1, D)), _full((1, D))],
        out_shape=[jax.ShapeDtypeStruct((tp, D), F32), jax.ShapeDtypeStruct((tp, D), F32),
                   jax.ShapeDtypeStruct((1, 1), F32), jax.ShapeDtypeStruct((1, D), F32),
                   jax.ShapeDtypeStruct((1, D), F32)],
        compiler_params=_cp(("arbitrary",), 40),
    )(h, g_final, target, f2, g_post2)


def _na_patterns(n_rows):
    pats = []
    for kind in range(3):
        pat = [[-1] * K_ROWS for _ in range(Q_ROWS)]
        for i in range(Q_ROWS):
            for jj in range(K_ROWS):
                if kind == 0 and jj < KH:
                    pat[i][jj] = jj - i + KH - 1
                elif kind == 1 and i <= jj < i + KH:
                    pat[i][jj] = jj - i + 3
                elif kind == 2 and K_ROWS - KH <= jj:
                    pat[i][jj] = jj - i - 1
        pats.append(pat)
    return pats


def _diag_onehot():
    q = np.arange(GRID_W)[:, None]
    kc = np.arange(GRID_W)[None, :]
    start = np.clip(q - KW // 2, 0, GRID_W - KW)
    col_in = (kc >= start) & (kc < start + KW)
    e = np.zeros((32, GRID_W, GRID_W), np.float32)
    for d in range(2 * KW - 1):
        e[d] = ((kc - q + KW - 1) == d) & col_in
    return e.reshape(32, GRID_W * GRID_W), col_in


def _rpb_collapse(dtb2, et):
    def body(d_ref, e_ref, o_ref):
        o_ref[...] = jnp.dot(d_ref[...], e_ref[...], preferred_element_type=F32, precision=lax.Precision.HIGHEST)

    return pl.pallas_call(
        body, name="rpb_collapse", out_shape=jax.ShapeDtypeStruct((dtb2.shape[0], et.shape[1]), F32),
        in_specs=[pl.BlockSpec(memory_space=pltpu.VMEM)] * 2, out_specs=pl.BlockSpec(memory_space=pltpu.VMEM),
    )(dtb2, et)


def _bias_tables(rpb, n_rows):
    n_dr, n_dc = 2 * KH - 1, 2 * KW - 1
    pats = _na_patterns(n_rows)

    def body(rpb_ref, o_ref):
        h = pl.program_id(0)
        q = lax.broadcasted_iota(jnp.int32, (GRID_W, GRID_W), 0)
        kc = lax.broadcasted_iota(jnp.int32, (GRID_W, GRID_W), 1)
        start = jnp.clip(q - KW // 2, 0, GRID_W - KW)
        col_in = (kc >= start) & (kc < start + KW)
        diff = kc - q + (KW - 1)
        neg = jnp.full((GRID_W, GRID_W), NEG_INF, F32)
        band = []
        for dr in range(n_dr):
            acc = neg
            for d in range(n_dc):
                acc = jnp.where((diff == d) & col_in, rpb_ref[(h * n_dr + dr) * n_dc + d], acc)
            band.append(acc)
        for kind, pat in enumerate(pats):
            for i in range(Q_ROWS):
                for jj in range(K_ROWS):
                    o_ref[kind, 0, i * GRID_W:(i + 1) * GRID_W, jj * GRID_W:(jj + 1) * GRID_W] = (
                        band[pat[i][jj]] if pat[i][jj] >= 0 else neg)

    return pl.pallas_call(
        body, name="bias_tables", grid=(N_HEADS,),
        in_specs=[pl.BlockSpec(memory_space=pltpu.SMEM)],
        out_specs=pl.BlockSpec((3, 1, QB, KB), lambda h: (0, h, 0, 0)),
        out_shape=jax.ShapeDtypeStruct((3, N_HEADS, QB, KB), F32),
        compiler_params=_cp(("arbitrary",), 32),
    )(rpb.reshape(-1))


def _attn_geometry(n_tok):
    n_rows = n_tok // GRID_W
    assert n_rows % Q_ROWS == 0 and n_rows >= K_ROWS
    return n_rows, n_rows // Q_ROWS


def _attn_probs(qh, kh, kmh, bias, scale):
    s = _dg(qh, kh, NT) * scale + bias
    sm = _dg(qh, kmh, NT) * scale
    m = jnp.maximum(jnp.max(s, axis=-1, keepdims=True), jnp.max(sm, axis=-1, keepdims=True))
    p = jnp.exp(s - m)
    pm = jnp.exp(sm - m)
    inv = 1.0 / (jnp.sum(p, axis=-1, keepdims=True) + jnp.sum(pm, axis=-1, keepdims=True))
    return p * inv, pm * inv


def _meta_probs(qmh, kmh, scale):
    s = _dg(qmh, kmh, NT) * scale
    p = jnp.exp(s - jnp.max(s, axis=-1, keepdims=True))
    return p / jnp.sum(p, axis=-1, keepdims=True)


def _step_rows(r, n_rows):
    q0 = pl.multiple_of(N_META + r * QB, 16)
    k0 = pl.multiple_of(N_META + jnp.clip(Q_ROWS * r - (K_ROWS - KH), 0, n_rows - K_ROWS) * GRID_W, 16)
    return q0, k0


def _attn_fwd(q, k, v, bias, n_tok, comm=None, bounds=()):
    tp = q.shape[0]
    n_rows, n_steps = _attn_geometry(n_tok)
    scale = HEAD_DIM ** -0.5

    def body(q_ref, k_ref, v_ref, b_ref, o_ref):
        r = pl.program_id(1)
        km = k_ref[0:N_META, :]
        vm = v_ref[0:N_META, :]

        @pl.when(r == 0)
        def _():
            qm = q_ref[0:N_META, :]
            outs = []
            for hh in range(2):
                sl = slice(hh * HEAD_DIM, (hh + 1) * HEAD_DIM)
                p = _meta_probs(qm[:, sl], km[:, sl], scale)
                outs.append(_dot(p.astype(BF16), vm[:, sl]))
            o_ref[0:N_META, :] = jnp.concatenate(outs, axis=1)
            o_ref[N_META + n_tok:, :] = jnp.zeros((tp - N_META - n_tok, 2 * HEAD_DIM), F32)

        q0, k0 = _step_rows(r, n_rows)
        qb = q_ref[pl.ds(q0, QB), :]
        kb = k_ref[pl.ds(k0, KB), :]
        vb = v_ref[pl.ds(k0, KB), :]
        outs = []
        for hh in range(2):
            sl = slice(hh * HEAD_DIM, (hh + 1) * HEAD_DIM)
            p, pm = _attn_probs(qb[:, sl], kb[:, sl], km[:, sl], b_ref[0, hh], scale)
            outs.append(_dot(p.astype(BF16), vb[:, sl]) + _dot(pm.astype(BF16), vm[:, sl]))
        o_ref[pl.ds(q0, QB), :] = jnp.concatenate(outs, axis=1)

    def bias_map(hp, r):
        return (jnp.where(r == 0, 0, jnp.where(r == n_steps - 1, 2, 1)), hp, 0, 0)

    col = pl.BlockSpec((tp, 2 * HEAD_DIM), lambda hp, r: (0, hp))
    return _call(
        body, comm, bounds, (q, k, v, bias), name="attn_fwd", grid=(N_HEADS // 2, n_steps),
        in_specs=[col, col, col, pl.BlockSpec((1, 2, QB, KB), bias_map)],
        out_specs=[col], out_shape=[jax.ShapeDtypeStruct((tp, NA_W), F32)],
        compiler_params=_cp(("arbitrary", "arbitrary"), 40))


def _attn_bwd(q, k, v, bias, do, n_tok, comm=None, bounds=()):
    tp = q.shape[0]
    n_rows, n_steps = _attn_geometry(n_tok)
    scale = HEAD_DIM ** -0.5
    pats = _na_patterns(n_rows)

    def body(q_ref, k_ref, v_ref, b_ref, do_ref, dq_ref, dk_ref, dv_ref, dtb_ref):
        r = pl.program_id(1)
        km = k_ref[0:N_META, :]
        vm = v_ref[0:N_META, :]

        @pl.when(r == 0)
        def _():
            dk_ref[...] = jnp.zeros_like(dk_ref)
            dv_ref[...] = jnp.zeros_like(dv_ref)
            dtb_ref[...] = jnp.zeros_like(dtb_ref)
            dq_ref[N_META + n_tok:, :] = jnp.zeros((tp - N_META - n_tok, 2 * HEAD_DIM), F32)
            qm = q_ref[0:N_META, :]
            dom = do_ref[0:N_META, :].astype(BF16)
            dqs, dks, dvs = [], [], []
            for hh in range(2):
                sl = slice(hh * HEAD_DIM, (hh + 1) * HEAD_DIM)
                p = _meta_probs(qm[:, sl], km[:, sl], scale)
                dp = _dg(dom[:, sl], vm[:, sl], NT)
                ds = (p * (dp - jnp.sum(dp * p, axis=-1, keepdims=True))).astype(BF16)
                dvs.append(_dg(p.astype(BF16), dom[:, sl], TN))
                dqs.append(_dot(ds, km[:, sl]) * scale)
                dks.append(_dg(ds, qm[:, sl], TN) * scale)
            dq_ref[0:N_META, :] = jnp.concatenate(dqs, axis=1)
            dk_ref[0:N_META, :] += jnp.concatenate(dks, axis=1)
            dv_ref[0:N_META, :] += jnp.concatenate(dvs, axis=1)

        q0, k0 = _step_rows(r, n_rows)
        qb = q_ref[pl.ds(q0, QB), :]
        kb = k_ref[pl.ds(k0, KB), :]
        vb = v_ref[pl.ds(k0, KB), :]
        dob = do_ref[pl.ds(q0, QB), :].astype(BF16)
        dqs, dks, dvs, dkms, dvms, dss = [], [], [], [], [], []
        for hh in range(2):
            sl = slice(hh * HEAD_DIM, (hh + 1) * HEAD_DIM)
            qh, kh, vh, kmh, vmh, doh = qb[:, sl], kb[:, sl], vb[:, sl], km[:, sl], vm[:, sl], dob[:, sl]
            p, pm = _attn_probs(qh, kh, kmh, b_ref[0, hh], scale)
            dp = _dg(doh, vh, NT)
            dpm = _dg(doh, vmh, NT)
            delta = jnp.sum(dp * p, axis=-1, keepdims=True) + jnp.sum(dpm * pm, axis=-1, keepdims=True)
            ds = p * (dp - delta)
            dsb = ds.astype(BF16)
            dsmb = (pm * (dpm - delta)).astype(BF16)
            dss.append(ds)
            dvs.append(_dg(p.astype(BF16), doh, TN))
            dvms.append(_dg(pm.astype(BF16), doh, TN))
            dqs.append((_dot(dsb, kh) + _dot(dsmb, kmh)) * scale)
            dks.append(_dg(dsb, qh, TN) * scale)
            dkms.append(_dg(dsmb, qh, TN) * scale)
        dq_ref[pl.ds(q0, QB), :] = jnp.concatenate(dqs, axis=1)
        dk_ref[pl.ds(k0, KB), :] += jnp.concatenate(dks, axis=1)
        dv_ref[pl.ds(k0, KB), :] += jnp.concatenate(dvs, axis=1)
        dk_ref[0:N_META, :] += jnp.concatenate(dkms, axis=1)
        dv_ref[0:N_META, :] += jnp.concatenate(dvms, axis=1)

        def add_bias_grad(pat):
            for hh in range(2):
                for i in range(Q_ROWS):
                    for jj in range(K_ROWS):
                        if pat[i][jj] >= 0:
                            dtb_ref[hh, pat[i][jj]] += dss[hh][i * GRID_W:(i + 1) * GRID_W,
                                                               jj * GRID_W:(jj + 1) * GRID_W]

        @pl.when(r == 0)
        def _():
            add_bias_grad(pats[0])

        @pl.when((r > 0) & (r < n_steps - 1))
        def _():
            add_bias_grad(pats[1])

        @pl.when(r == n_steps - 1)
        def _():
            add_bias_grad(pats[2])

    def bias_map(hp, r):
        return (jnp.where(r == 0, 0, jnp.where(r == n_steps - 1, 2, 1)), hp, 0, 0)

    col = pl.BlockSpec((tp, 2 * HEAD_DIM), lambda hp, r: (0, hp))
    n_dr = 2 * KH - 1
    return _call(
        body, comm, bounds, (q, k, v, bias, do), name="attn_bwd", grid=(N_HEADS // 2, n_steps),
        in_specs=[col, col, col, pl.BlockSpec((1, 2, QB, KB), bias_map), col],
        out_specs=[col, col, col, pl.BlockSpec((2, n_dr, GRID_W, GRID_W), lambda hp, r: (hp, 0, 0, 0))],
        out_shape=[jax.ShapeDtypeStruct((tp, NA_W), F32)] * 3 +
                  [jax.ShapeDtypeStruct((N_HEADS, n_dr, GRID_W, GRID_W), F32)],
        compiler_params=_cp(("arbitrary", "arbitrary"), 48))


def _expand_onehot():
    return np.tile(np.eye(S5_P, dtype=np.float32), (1, S5_H))


def _s5_disc_math(lam_re, lam_im, log_dt, b_re, b_im, ex):
    dt = jnp.exp(log_dt)
    ea = jnp.exp(lam_re * dt)
    a_re = ea * jnp.cos(lam_im * dt)
    a_im = ea * jnp.sin(lam_im * dt)
    den = lam_re * lam_re + lam_im * lam_im
    c_re = ((a_re - 1.0) * lam_re + a_im * lam_im) / den
    c_im = (a_im * lam_re - (a_re - 1.0) * lam_im) / den
    ce_re = jnp.dot(c_re, ex, preferred_element_type=F32, precision=lax.Precision.HIGHEST)
    ce_im = jnp.dot(c_im, ex, preferred_element_type=F32, precision=lax.Precision.HIGHEST)
    return a_re, a_im, ce_re * b_re - ce_im * b_im, ce_re * b_im + ce_im * b_re


def _s5_disc(lam_re, lam_im, log_dt, b_re, b_im):
    ex = jnp.asarray(_expand_onehot())
    n = lam_re.shape[0]

    def body(lr, li, ld, br, bi, ex_ref, ar, ai, bbr, bbi):
        ar[...], ai[...], bbr[...], bbi[...] = _s5_disc_math(lr[...], li[...], ld[...], br[...], bi[...], ex_ref[...])

    vm = pl.BlockSpec(memory_space=pltpu.VMEM)
    return pl.pallas_call(
        body, name="s5_disc", in_specs=[vm] * 6, out_specs=[vm] * 4,
        out_shape=[jax.ShapeDtypeStruct((n, S5_P), F32)] * 2 + [jax.ShapeDtypeStruct((n, S5_P * S5_H), F32)] * 2,
    )(lam_re, lam_im, log_dt, b_re, b_im, ex)


def _s5_disc_bwd(lam_re, lam_im, log_dt, b_re, b_im, da_re, da_im, dbb_re, dbb_im):
    ex = jnp.asarray(_expand_onehot())
    n = lam_re.shape[0]

    def body(lr, li, ld, br, bi, ex_ref, dar, dai, dbr, dbi, o_lr, o_li, o_ld, o_br, o_bi):
        e = ex_ref[...]
        _, vjp = jax.vjp(lambda a, b, c, d, f: _s5_disc_math(a, b, c, d, f, e), lr[...], li[...], ld[...], br[...], bi[...])
        o_lr[...], o_li[...], o_ld[...], o_br[...], o_bi[...] = vjp((dar[...], dai[...], dbr[...], dbi[...]))

    vm = pl.BlockSpec(memory_space=pltpu.VMEM)
    return pl.pallas_call(
        body, name="s5_disc_bwd", in_specs=[vm] * 10, out_specs=[vm] * 5,
        out_shape=[jax.ShapeDtypeStruct((n, S5_P), F32)] * 2 + [jax.ShapeDtypeStruct((n, 1), F32)] +
                  [jax.ShapeDtypeStruct((n, S5_P * S5_H), F32)] * 2,
    )(lam_re, lam_im, log_dt, b_re, b_im, ex, da_re, da_im, dbb_re, dbb_im)


def _s5_matrices(a_re, a_im, bb_re, bb_im, c_re, c_im):
    gl = S5_G // N_BUNDLE
    eye = jnp.eye(gl, dtype=F32)
    half = gl * S5_P

    def in_mat(bb):
        t = bb.reshape(2, N_BUNDLE, gl, S5_H, S5_P).transpose(0, 1, 3, 2, 4)
        m = t[:, :, None] * eye[None, None, :, None, :, None]
        return m.reshape(2, N_BUNDLE, gl * S5_H, half)

    def out_mat(c):
        t = c.reshape(2, N_BUNDLE, gl, S5_H, S5_P).transpose(0, 1, 2, 4, 3)
        m = t[:, :, :, :, None, :] * eye[None, None, :, None, :, None]
        return m.reshape(2, N_BUNDLE, half, gl * S5_H)

    a = jnp.concatenate([a_re.reshape(2, N_BUNDLE, 1, half), a_im.reshape(2, N_BUNDLE, 1, half)], axis=-1)
    bm = jnp.concatenate([in_mat(bb_re), in_mat(bb_im)], axis=-1)
    cm = jnp.concatenate([out_mat(c_re), -out_mat(c_im)], axis=-2)
    return a, bm, cm


def _scan_chunks(length):
    return [(t0, min(SCAN_CHUNK, length - t0)) for t0 in range(0, length, SCAN_CHUNK)]


def _scan(src_ref, dst_ref, prev_ref, prev_off, n_rows, a_re, a_im, carry, reverse):
    half = a_re.shape[-1]
    n_blk = n_rows // 8
    rid = lax.broadcasted_iota(jnp.int32, (8, half), 0)

    def blk(i, carry):
        xr, xi = carry
        bi = (n_blk - 1 - i) if reverse else i
        off = pl.multiple_of(bi * 8, 8)
        v = src_ref[pl.ds(off, 8), :]
        o_r = jnp.zeros((8, half), F32)
        o_i = jnp.zeros((8, half), F32)
        p_r = jnp.zeros((8, half), F32)
        p_i = jnp.zeros((8, half), F32)
        for j in (range(7, -1, -1) if reverse else range(8)):
            if prev_ref is not None:
                p_r = jnp.where(rid == j, xr, p_r)
                p_i = jnp.where(rid == j, xi, p_i)
            nr = a_re * xr - a_im * xi + v[j:j + 1, :half]
            ni = a_re * xi + a_im * xr + v[j:j + 1, half:]
            xr, xi = nr, ni
            if dst_ref is not None:
                o_r = jnp.where(rid == j, xr, o_r)
                o_i = jnp.where(rid == j, xi, o_i)
        if dst_ref is not None:
            dst_ref[pl.ds(off, 8), :] = jnp.concatenate([o_r, o_i], axis=1)
        if prev_ref is not None:
            prev_ref[pl.ds(pl.multiple_of(prev_off + off, 8), 8), :] = jnp.concatenate([p_r, p_i], axis=1)
        return xr, xi

    return lax.fori_loop(0, n_blk, blk, carry)


def _s5_fwd(u, d_skip, a, bm, cm, length, comm=None, bounds=()):
    tp = u.shape[0]
    cw = S5_W // N_BUNDLE
    sw = a.shape[-1]
    half = sw // 2
    chunks = _scan_chunks(length)

    def body(u_ref, d_ref, a_ref, bm_ref, cm_ref, y_ref, bu_s, xs_s):
        y_ref[...] = u_ref[...] * d_ref[...]
        for dr in range(2):
            a_re = a_ref[dr, 0, :, 0:half]
            a_im = a_ref[dr, 0, :, half:]
            carry = (jnp.zeros((1, half), F32), jnp.zeros((1, half), F32))
            for t0, n in (chunks if dr == 0 else chunks[::-1]):
                bu_s[0:n, :] = _dot(u_ref[t0:t0 + n, :].astype(BF16), bm_ref[dr, 0])
                carry = _scan(bu_s, xs_s, None, 0, n, a_re, a_im, carry, dr == 1)
                y_ref[t0:t0 + n, :] += _dot(xs_s[0:n, :].astype(BF16), cm_ref[dr, 0])

    return _call(
        body, comm, bounds, (u, d_skip, a, bm, cm), name="s5_fwd", grid=(N_BUNDLE,),
        in_specs=[pl.BlockSpec((tp, cw), lambda b: (0, b)), pl.BlockSpec((1, cw), lambda b: (0, b)),
                  pl.BlockSpec((2, 1, 1, sw), lambda b: (0, b, 0, 0)),
                  pl.BlockSpec((2, 1, cw, sw), lambda b: (0, b, 0, 0)),
                  pl.BlockSpec((2, 1, sw, cw), lambda b: (0, b, 0, 0))],
        out_specs=[pl.BlockSpec((tp, cw), lambda b: (0, b))],
        out_shape=[jax.ShapeDtypeStruct((tp, S5_W), F32)],
        scratch_shapes=[pltpu.VMEM((SCAN_CHUNK, sw), F32), pltpu.VMEM((SCAN_CHUNK, sw), F32)],
        compiler_params=_cp(("arbitrary",), 40))


def _s5_bwd(u, dy, d_skip, a, bm, cm, length):
    tp = u.shape[0]
    cw = S5_W // N_BUNDLE
    sw = a.shape[-1]
    half = sw // 2
    chunks = _scan_chunks(length)

    def body(u_ref, dy_ref, d_ref, a_ref, bm_ref, cm_ref, du_ref, dd_ref, dbm_ref, dcm_ref, da_ref, bu_s, g_s, xp_s):
        du_ref[...] = dy_ref[...] * d_ref[...]
        dd_ref[...] = jnp.sum(dy_ref[...] * u_ref[...], axis=0, keepdims=True)
        dbm_ref[...] = jnp.zeros_like(dbm_ref)
        dcm_ref[...] = jnp.zeros_like(dcm_ref)
        zero = (jnp.zeros((1, half), F32), jnp.zeros((1, half), F32))
        for dr in range(2):
            a_re = a_ref[dr, 0, :, 0:half]
            a_im = a_ref[dr, 0, :, half:]
            seq = chunks if dr == 0 else chunks[::-1]
            carry = zero
            for t0, n in seq:
                bu_s[0:n, :] = _dot(u_ref[t0:t0 + n, :].astype(BF16), bm_ref[dr, 0])
                carry = _scan(bu_s, None, xp_s, t0, n, a_re, a_im, carry, dr == 1)
            carry = zero
            da_r = jnp.zeros((1, half), F32)
            da_i = jnp.zeros((1, half), F32)
            for t0, n in seq[::-1]:
                ub = u_ref[t0:t0 + n, :].astype(BF16)
                dyb = dy_ref[t0:t0 + n, :].astype(BF16)
                bu_s[0:n, :] = _dg(dyb, cm_ref[dr, 0], NT)
                carry = _scan(bu_s, g_s, None, 0, n, a_re, -a_im, carry, dr == 0)
                g = g_s[0:n, :]
                gb = g.astype(BF16)
                du_ref[t0:t0 + n, :] += _dg(gb, bm_ref[dr, 0], NT)
                dbm_ref[dr, 0] += _dg(ub, gb, TN)
                xp = xp_s[t0:t0 + n, :]
                xp_r, xp_i = xp[:, 0:half], xp[:, half:]
                g_r, g_i = g[:, 0:half], g[:, half:]
                bu = _dot(ub, bm_ref[dr, 0])
                x_r = a_re * xp_r - a_im * xp_i + bu[:, 0:half]
                x_i = a_re * xp_i + a_im * xp_r + bu[:, half:]
                dcm_ref[dr, 0] += _dg(jnp.concatenate([x_r, x_i], axis=1).astype(BF16), dyb, TN)
                da_r = da_r + jnp.sum(g_r * xp_r + g_i * xp_i, axis=0, keepdims=True)
                da_i = da_i + jnp.sum(g_i * xp_r - g_r * xp_i, axis=0, keepdims=True)
            da_ref[dr, 0] = jnp.concatenate([da_r, da_i], axis=1)

    lp = -(-length // 8) * 8
    return pl.pallas_call(
        body, name="s5_bwd", grid=(N_BUNDLE,),
        in_specs=[pl.BlockSpec((tp, cw), lambda b: (0, b)), pl.BlockSpec((tp, cw), lambda b: (0, b)),
                  pl.BlockSpec((1, cw), lambda b: (0, b)),
                  pl.BlockSpec((2, 1, 1, sw), lambda b: (0, b, 0, 0)),
                  pl.BlockSpec((2, 1, cw, sw), lambda b: (0, b, 0, 0)),
                  pl.BlockSpec((2, 1, sw, cw), lambda b: (0, b, 0, 0))],
        out_specs=[pl.BlockSpec((tp, cw), lambda b: (0, b)), pl.BlockSpec((1, cw), lambda b: (0, b)),
                   pl.BlockSpec((2, 1, cw, sw), lambda b: (0, b, 0, 0)),
                   pl.BlockSpec((2, 1, sw, cw), lambda b: (0, b, 0, 0)),
                   pl.BlockSpec((2, 1, 1, sw), lambda b: (0, b, 0, 0))],
        out_shape=[jax.ShapeDtypeStruct((tp, S5_W), F32), jax.ShapeDtypeStruct((1, S5_W), F32),
                   jax.ShapeDtypeStruct((2, N_BUNDLE, cw, sw), F32), jax.ShapeDtypeStruct((2, N_BUNDLE, sw, cw), F32),
                   jax.ShapeDtypeStruct((2, N_BUNDLE, 1, sw), F32)],
        scratch_shapes=[pltpu.VMEM((SCAN_CHUNK, sw), F32), pltpu.VMEM((SCAN_CHUNK, sw), F32),
                        pltpu.VMEM((lp, sw), F32)],
        compiler_params=_cp(("arbitrary",), 48),
    )(u, dy, d_skip, a, bm, cm)


def _row_tile(tp):
    return max(tm for tm in range(16, 449, 16) if tp % tm == 0)


def _step(x, target, bufs, gains, s5, rpb, c_arr, kc_arr):
    first = ["ffn1_w_gate", "ffn1_w_up", "ffn1_w_down", "meta_tokens"]
    w = dict(zip(first, _run_comm("gather_ffn1", _gather_comm([bufs[n] for n in first]))))
    meta = w["meta_tokens"].transpose(1, 0, 2).reshape(N_META, D)
    n_tok = x.shape[0]
    length = N_META + n_tok
    tp = length + 16
    tm = _row_tile(tp)
    tmb = tm
    n_rows = n_tok // GRID_W
    pad = jnp.zeros((tp - length, D), F32)
    h0 = jnp.concatenate([meta, x, pad], axis=0)
    tgt = jnp.concatenate([jnp.zeros((N_META, D), F32), target, pad], axis=0)

    n2 = 2 * S5_G
    lam_re = s5["lam_re"].reshape(n2, S5_P)
    lam_im = s5["lam_im"].reshape(n2, S5_P)
    log_dt = s5["log_dt"].reshape(n2, 1)
    b_re = s5["b_re"].transpose(0, 1, 3, 2).reshape(n2, S5_H * S5_P)
    b_im = s5["b_im"].transpose(0, 1, 3, 2).reshape(n2, S5_H * S5_P)
    a_re, a_im, bb_re, bb_im = _s5_disc(lam_re, lam_im, log_dt, b_re, b_im)

    def mats(a_re, a_im, bb_re, bb_im, c_re, c_im):
        return _s5_matrices(a_re.reshape(2, S5_G, S5_P), a_im.reshape(2, S5_G, S5_P),
                            bb_re.reshape(2, S5_G, S5_P * S5_H), bb_im.reshape(2, S5_G, S5_P * S5_H), c_re, c_im)

    (a_m, bm, cm), mats_vjp = jax.vjp(mats, a_re, a_im, bb_re, bb_im, s5["c_re"], s5["c_im"])
    bm16 = bm.astype(BF16)
    cm16 = cm.astype(BF16)
    bias = _bias_tables(rpb, n_rows)

    mid = ["w_in", "s5_w_glu", "w_out"]
    (h1, gate1, up1, f1), got = _ffn_fwd(
        "ffn1_fwd", h0, gains["ffn1_pre_g"], gains["ffn1_post_g"], w["ffn1_w_gate"], w["ffn1_w_up"], w["ffn1_w_down"],
        tm, _gather_comm([bufs[n] for n in mid]), (0, (tp // tm) * N_CHIP * 3 // 5))
    w.update(zip(mid, got))
    q, k, v, u = _mix_in(h1, gains["mix_pre_g"], w["w_in"], tm)
    (o_na,), (gate_ici,) = _attn_fwd(q, k, v, bias, n_tok, _gather_comm([bufs["ffn2_w_gate"]], pair=False), (0,))
    (y_pre,), (w["ffn2_w_gate"], up_ici, down_ici) = _s5_fwd(
        u, gains["s5_d"], a_m, bm16, cm16, length,
        _merge_comm(_gather_comm([gate_ici], ici=False),
                    _gather_comm([bufs["ffn2_w_up"], bufs["ffn2_w_down"]], pair=False)), (0,))
    w_glu = w["s5_w_glu"].reshape(S5_W, S5_W)
    w_out = w["w_out"].reshape(D, D)
    (h2, mix), (w["ffn2_w_up"], w["ffn2_w_down"]) = _mix_out(
        o_na, y_pre, h1, w_glu, gains["s5_b_glu"], gains["na_out_g"], gains["s5_out_g"], w_out, gains["mix_post_g"], tm,
        _gather_comm([up_ici, down_ici], ici=False), (0,))
    (h3, gate2, up2, f2), _ = _ffn_fwd("ffn2_fwd", h2, gains["ffn2_pre_g"], gains["ffn2_post_g"],
                                       w["ffn2_w_gate"], w["ffn2_w_up"], w["ffn2_w_down"], tm)
    dh3, df2, loss, dg_final, dg_post2 = _final_loss(h3, gains["final_g"], tgt, f2, gains["ffn2_post_g"], n_tok, tm)

    ffn2 = ["ffn2_w_gate", "ffn2_w_up", "ffn2_w_down"]
    ffn1 = ["ffn1_w_gate", "ffn1_w_up", "ffn1_w_down"]
    out2 = _ffn_bwd("ffn2_bwd", h2, gains["ffn2_pre_g"], df2, gate2, up2,
                    w["ffn2_w_gate"], w["ffn2_w_up"], w["ffn2_w_down"], tmb)
    dxn2 = out2[3]
    sums2 = [_chip_sum("chip_sum_" + n, g, r, c_arr) for n, g, r in zip(ffn2, out2[0:3], out2[4:7])]
    (dh2, dg_pre2), _ = _ffn_pre_bwd("ffn2_pre_bwd", dh3, dxn2, h2, gains["ffn2_pre_g"], tm)
    do_na, dy_pre, dw_out, dw_glu, dg_mpost, dg_na, dg_s5, db_glu = _mix_out_bwd(
        dh2, mix, o_na, y_pre, w_glu, gains["s5_b_glu"], gains["na_out_g"], gains["s5_out_g"], w_out,
        gains["mix_post_g"], tm)
    (dq, dk, dv, dtb), recv3 = _attn_bwd(q, k, v, bias, do_na, n_tok, _scatter_comm(sums2), (0,))
    totals2 = [_total_sum("total_sum_" + n, s, r, kc_arr) for n, s, r in zip(ffn2, sums2, recv3)]
    du, dd, dbm, dcm, da_m = _s5_bwd(u, dy_pre, gains["s5_d"], a_m, bm16, cm16, length)
    (dh1, df1, dw_in, dg_mpre, dg_post1), done2 = _mix_in_bwd(
        dq, dk, dv, du, h1, gains["mix_pre_g"], w["w_in"], dh2, f1, gains["ffn1_post_g"], tm,
        _assemble_comm(totals2), (0,))
    pieces = dict(zip(ffn2, done2))
    out1 = _ffn_bwd("ffn1_bwd", h0, gains["ffn1_pre_g"], df1, gate1, up1,
                    w["ffn1_w_gate"], w["ffn1_w_up"], w["ffn1_w_down"], tmb)
    rest = [dw_in, dw_glu.reshape(N_CHIP, S5_W // N_CHIP, S5_W), dw_out.reshape(N_CHIP, D // N_CHIP, D)]
    (dh0, dg_pre1), recv_rest = _ffn_pre_bwd("ffn1_pre_bwd", dh1, out1[3], h0, gains["ffn1_pre_g"], tm,
                                             _exchange_comm(rest), (0,))
    last = ffn1 + mid
    sums = [_chip_sum("chip_sum_" + n, g, r, c_arr)
            for n, g, r in zip(last, list(out1[0:3]) + rest, list(out1[4:7]) + list(recv_rest))]
    recv3 = _run_comm("grad_chip_scatter", _scatter_comm(sums))
    totals = [_total_sum("total_sum_" + n, s, r, kc_arr) for n, s, r in zip(last, sums, recv3)]
    pieces.update(zip(last, _run_comm("grad_pair_assemble", _assemble_comm(totals))))

    e, _ = _diag_onehot()
    n_dr = 2 * KH - 1
    drpb = _rpb_collapse(dtb.reshape(N_HEADS * n_dr, GRID_W * GRID_W), jnp.asarray(e.T))
    drpb = drpb[:, :2 * KW - 1].reshape(N_HEADS, n_dr, 2 * KW - 1).transpose(1, 0, 2).reshape(N_HEADS * n_dr, 2 * KW - 1)
    da_re, da_im, dbb_re, dbb_im, dc_re, dc_im = mats_vjp((da_m, dbm, dcm))
    dlam_re, dlam_im, dlog_dt, db_re, db_im = _s5_disc_bwd(lam_re, lam_im, log_dt, b_re, b_im,
                                                            da_re, da_im, dbb_re, dbb_im)

    small = {"ffn1_pre_g": dg_pre1, "ffn1_post_g": dg_post1, "mix_pre_g": dg_mpre, "na_rpb": drpb,
             "s5_lam_re": dlam_re, "s5_lam_im": dlam_im, "s5_log_dt": dlog_dt.reshape(2, S5_G),
             "s5_b_re": db_re.reshape(n2 * S5_H, S5_P), "s5_b_im": db_im.reshape(n2 * S5_H, S5_P),
             "s5_c_re": dc_re.reshape(n2 * S5_H, S5_P), "s5_c_im": dc_im.reshape(n2 * S5_H, S5_P), "s5_d": dd, "s5_b_glu": db_glu, "na_out_g": dg_na,
             "s5_out_g": dg_s5, "mix_post_g": dg_mpost, "ffn2_pre_g": dg_pre2, "ffn2_post_g": dg_post2,
             "final_g": dg_final}
    return loss[0, 0], dh0, pieces, small


def _mesh_pos():
    return lax.axis_index("x"), lax.axis_index("y"), lax.axis_index("c")


def _other_chips(x, y):
    return [(1 - x, y), (x, 1 - y), (1 - x, 1 - y)]


class _Comm:
    def __init__(self, ins, out_shape, aliases, parts):
        self.ins, self.out_shape, self.aliases, self.parts = list(ins), list(out_shape), dict(aliases), list(parts)
        self.n_sems = sum(p[0] for p in parts)

    def bases(self):
        out, base = [], 0
        for n_sems, _, _ in self.parts:
            out.append(base)
            base += n_sems
        return out


def _run_comm(name, comm):
    n_i, n_o = len(comm.ins), len(comm.out_shape)

    def body(*refs):
        ins, outs = refs[:n_i], refs[n_i:n_i + n_o]
        send_sems, recv_sems = refs[n_i + n_o:]
        for base, (_, start, finish) in zip(comm.bases(), comm.parts):
            start(ins, outs, send_sems, recv_sems, base)
            finish(ins, outs, send_sems, recv_sems, base)

    return pl.pallas_call(
        body, name=name, out_shape=comm.out_shape, in_specs=[ANY] * n_i, out_specs=[ANY] * n_o,
        input_output_aliases=comm.aliases,
        scratch_shapes=[pltpu.SemaphoreType.DMA((comm.n_sems,)), pltpu.SemaphoreType.DMA((comm.n_sems,))],
    )(*comm.ins)


def _call(body, comm, bounds, args, *, name, grid, in_specs, out_specs, out_shape, scratch_shapes=(),
          compiler_params=None):
    in_specs, out_specs, out_shape, scratch_shapes = list(in_specs), list(out_specs), list(out_shape), list(scratch_shapes)
    if comm is None:
        return pl.pallas_call(body, name=name, grid=grid, in_specs=in_specs, out_specs=out_specs, out_shape=out_shape,
                              scratch_shapes=scratch_shapes, compiler_params=compiler_params)(*args), []
    n_in, n_out, n_scr = len(in_specs), len(out_specs), len(scratch_shapes)
    n_ci, n_co = len(comm.ins), len(comm.out_shape)
    n_steps = int(np.prod(grid))
    assert len(bounds) == len(comm.parts) and all(0 <= b < n_steps for b in bounds) and list(bounds) == sorted(bounds)

    def fused(*refs):
        a = n_in
        b = a + n_ci
        c = b + n_out
        d = c + n_co
        e = d + n_scr
        cargs = (refs[a:b], refs[c:d], refs[e], refs[e + 1])
        step = pl.program_id(0)
        for ax in range(1, len(grid)):
            step = step * grid[ax] + pl.program_id(ax)
        bases = comm.bases()
        for p, (_, start, finish) in enumerate(comm.parts):
            @pl.when(step == bounds[p])
            def _(p=p, start=start):
                if p > 0:
                    comm.parts[p - 1][2](*cargs, bases[p - 1])
                start(*cargs, bases[p])
        body(*(refs[:a] + refs[b:c] + refs[d:e]))

        @pl.when(step == n_steps - 1)
        def _():
            comm.parts[-1][2](*cargs, bases[-1])

    res = pl.pallas_call(
        fused, name=name, grid=grid, in_specs=in_specs + [ANY] * n_ci, out_specs=out_specs + [ANY] * n_co,
        out_shape=out_shape + comm.out_shape,
        scratch_shapes=scratch_shapes + [pltpu.SemaphoreType.DMA((comm.n_sems,)), pltpu.SemaphoreType.DMA((comm.n_sems,))],
        input_output_aliases={n_in + i: n_out + j for i, j in comm.aliases.items()},
        compiler_params=compiler_params)(*args, *comm.ins)
    return res[:n_out], res[n_out:]


def _remote(src, dst, send_sems, recv_sems, idx, to):
    return pltpu.make_async_remote_copy(src_ref=src, dst_ref=dst, send_sem=send_sems.at[idx],
                                        recv_sem=recv_sems.at[idx], device_id=to, device_id_type=MESH_ID)


def _gather_comm(bufs, ici=True, pair=True):
    n = len(bufs)

    def half(ref, k, pc):
        rh = ref.shape[1] // 2
        return ref.at[k, pl.ds(pc * rh, rh), :]

    def ici_start(ins, outs, ss, rs, base):
        x, y, c = _mesh_pos()
        for a in range(n):
            mine = half(outs[a], 2 * x + y, c)
            for j, chip in enumerate(_other_chips(x, y)):
                _remote(mine, mine, ss, rs, base + 3 * a + j, (*chip, c)).start()

    def ici_finish(ins, outs, ss, rs, base):
        x, y, c = _mesh_pos()
        for a in range(n):
            for j, chip in enumerate(_other_chips(x, y)):
                theirs = half(outs[a], 2 * chip[0] + chip[1], c)
                _remote(theirs, theirs, ss, rs, base + 3 * a + j, (*chip, c)).wait()

    def pair_copy(outs, ss, rs, base, a):
        x, y, c = _mesh_pos()
        rh = outs[a].shape[1] // 2
        held = outs[a].at[:, pl.ds(c * rh, rh), :]
        return _remote(held, held, ss, rs, base + a, (x, y, 1 - c))

    def pair_start(ins, outs, ss, rs, base):
        for a in range(n):
            pair_copy(outs, ss, rs, base, a).start()

    def pair_finish(ins, outs, ss, rs, base):
        for a in range(n):
            pair_copy(outs, ss, rs, base, a).wait()

    parts = ([(3 * n, ici_start, ici_finish)] if ici else []) + ([(n, pair_start, pair_finish)] if pair else [])
    return _Comm(bufs, [jax.ShapeDtypeStruct(b.shape, b.dtype) for b in bufs], {a: a for a in range(n)}, parts)


def _merge_comm(*comms):
    ins, shapes, aliases, subs, base = [], [], {}, [], 0
    for cm in comms:
        (n_sems, start, finish), = cm.parts
        i0, o0 = len(ins), len(shapes)
        subs.append((slice(i0, i0 + len(cm.ins)), slice(o0, o0 + len(cm.out_shape)), base, start, finish))
        aliases.update({i0 + i: o0 + j for i, j in cm.aliases.items()})
        ins += cm.ins
        shapes += cm.out_shape
        base += n_sems

    def start_all(ins_r, outs_r, ss, rs, b):
        for si, so, off, start, _ in subs:
            start(ins_r[si], outs_r[so], ss, rs, b + off)

    def finish_all(ins_r, outs_r, ss, rs, b):
        for si, so, off, _, finish in subs:
            finish(ins_r[si], outs_r[so], ss, rs, b + off)

    return _Comm(ins, shapes, aliases, [(base, start_all, finish_all)])


def _own_half_buffers(pieces, dtypes, kc_arr):
    n = len(pieces)

    def body(kc_ref, *refs):
        for a in range(n):
            refs[n + a][0] = refs[a][...].astype(dtypes[a])

    def half(p):
        return p.shape[0] // 2, p.shape[1]

    return pl.pallas_call(
        body, name="own_halves",
        out_shape=[jax.ShapeDtypeStruct((N_CHIP,) + p.shape, dt) for p, dt in zip(pieces, dtypes)],
        grid_spec=pltpu.PrefetchScalarGridSpec(
            num_scalar_prefetch=1, grid=(1,),
            in_specs=[pl.BlockSpec(half(p), lambda i, kc: (kc[1], 0)) for p in pieces],
            out_specs=[pl.BlockSpec((1,) + half(p), lambda i, kc: (kc[0], kc[1], 0)) for p in pieces]),
        compiler_params=_cp(("arbitrary",), 48),
    )(kc_arr, *pieces)


def _exchange_comm(grads):
    n = len(grads)

    def copy(ins, outs, ss, rs, base, a):
        x, y, c = _mesh_pos()
        rh = ins[a].shape[1] // 2
        return _remote(ins[a].at[:, pl.ds((1 - c) * rh, rh), :], outs[a], ss, rs, base + a, (x, y, 1 - c))

    def start(ins, outs, ss, rs, base):
        for a in range(n):
            copy(ins, outs, ss, rs, base, a).start()

    def finish(ins, outs, ss, rs, base):
        for a in range(n):
            copy(ins, outs, ss, rs, base, a).wait()

    shapes = [jax.ShapeDtypeStruct((N_CHIP, g.shape[1] // 2, g.shape[2]), g.dtype) for g in grads]
    return _Comm(grads, shapes, {}, [(n, start, finish)])


def _chip_sum(name, g, recv, c_arr):
    _, r, cc = g.shape
    rh = r // 2

    def body(c_ref, g_ref, r_ref, o_ref):
        o_ref[...] = (g_ref[...] + r_ref[...]).astype(BF16)

    return pl.pallas_call(
        body, name=name, out_shape=jax.ShapeDtypeStruct((N_CHIP, rh, cc), BF16),
        grid_spec=pltpu.PrefetchScalarGridSpec(
            num_scalar_prefetch=1, grid=(N_CHIP,),
            in_specs=[pl.BlockSpec((1, rh, cc), lambda j, c_ref: (j, c_ref[0], 0)),
                      pl.BlockSpec((1, rh, cc), lambda j, c_ref: (j, 0, 0))],
            out_specs=pl.BlockSpec((1, rh, cc), lambda j, c_ref: (j, 0, 0))),
        compiler_params=_cp(("arbitrary",), 32),
    )(c_arr, g, recv)


def _scatter_comm(sums):
    n = len(sums)

    def copies(ins, outs, ss, rs, base):
        x, y, c = _mesh_pos()
        return [_remote(ins[a].at[2 * chip[0] + chip[1]], outs[a].at[j], ss, rs, base + 3 * a + j, (*chip, c))
                for a in range(n) for j, chip in enumerate(_other_chips(x, y))]

    def start(ins, outs, ss, rs, base):
        for cp in copies(ins, outs, ss, rs, base):
            cp.start()

    def finish(ins, outs, ss, rs, base):
        for cp in copies(ins, outs, ss, rs, base):
            cp.wait()

    shapes = [jax.ShapeDtypeStruct((3,) + s.shape[1:], s.dtype) for s in sums]
    return _Comm(sums, shapes, {}, [(3 * n, start, finish)])


def _total_sum(name, sums, recv3, kc_arr):
    _, rh, cc = sums.shape

    def body(kc_ref, s_ref, r_ref, o_ref):
        t = s_ref[0].astype(F32) + r_ref[0].astype(F32)
        t = t + r_ref[1].astype(F32)
        o_ref[...] = t + r_ref[2].astype(F32)

    return pl.pallas_call(
        body, name=name, out_shape=jax.ShapeDtypeStruct((2 * rh, cc), F32),
        grid_spec=pltpu.PrefetchScalarGridSpec(
            num_scalar_prefetch=1, grid=(1,),
            in_specs=[pl.BlockSpec((1, rh, cc), lambda i, kc_ref: (kc_ref[0], 0, 0)),
                      pl.BlockSpec((3, rh, cc), lambda i, kc_ref: (0, 0, 0))],
            out_specs=pl.BlockSpec((rh, cc), lambda i, kc_ref: (kc_ref[1], 0))),
        compiler_params=_cp(("arbitrary",), 32),
    )(kc_arr, sums, recv3)


def _assemble_comm(totals):
    n = len(totals)

    def copy(outs, ss, rs, base, a):
        x, y, c = _mesh_pos()
        rh = outs[a].shape[0] // 2
        here = outs[a].at[pl.ds(c * rh, rh), :]
        return _remote(here, here, ss, rs, base + a, (x, y, 1 - c))

    def start(ins, outs, ss, rs, base):
        for a in range(n):
            copy(outs, ss, rs, base, a).start()

    def finish(ins, outs, ss, rs, base):
        for a in range(n):
            copy(outs, ss, rs, base, a).wait()

    shapes = [jax.ShapeDtypeStruct(t.shape, t.dtype) for t in totals]
    return _Comm(totals, shapes, {a: a for a in range(n)}, [(n, start, finish)])


def _small_allreduce(arrays):
    n = len(arrays)
    shapes = [a.shape for a in arrays]
    narrow_w = 64
    groups = [[a for a in range(n) if shapes[a][1] > narrow_w], [a for a in range(n) if shapes[a][1] <= narrow_w]]
    widths = [max(shapes[a][1] for a in g) for g in groups]
    offs, rows = {}, []
    for g in groups:
        r = 0
        for a in g:
            offs[a] = r
            r += shapes[a][0]
        rows.append(-(-r // 8) * 8)
    n_g = len(groups)

    def body(*refs):
        ins, outs = refs[:n], refs[n:2 * n]
        pack, sib, csum, every = (refs[2 * n + i * n_g:2 * n + (i + 1) * n_g] for i in range(4))
        send_sems, recv_sems = refs[2 * n + 4 * n_g:]
        x, y, c = _mesh_pos()
        k = 2 * x + y
        for gi, g in enumerate(groups):
            pack[gi][...] = jnp.zeros_like(pack[gi])
            for a in g:
                pack[gi][offs[a]:offs[a] + shapes[a][0], 0:shapes[a][1]] = ins[a][...]
        cps = [_remote(pack[gi], sib[gi], send_sems, recv_sems, gi, (x, y, 1 - c)) for gi in range(n_g)]
        for cp in cps:
            cp.start()
        for cp in cps:
            cp.wait()
        for gi in range(n_g):
            csum[gi][...] = pack[gi][...] + sib[gi][...]
            every[gi][k] = csum[gi][...]
        cps = [_remote(csum[gi], every[gi].at[k], send_sems, recv_sems, n_g + 3 * gi + j, (*chip, c))
               for gi in range(n_g) for j, chip in enumerate(_other_chips(x, y))]
        for cp in cps:
            cp.start()
        for cp in cps:
            cp.wait()
        for gi, g in enumerate(groups):
            pack[gi][...] = ((every[gi][0] + every[gi][1]) + every[gi][2]) + every[gi][3]
            for a in g:
                outs[a][...] = pack[gi][offs[a]:offs[a] + shapes[a][0], 0:shapes[a][1]]

    vm = pl.BlockSpec(memory_space=pltpu.VMEM)
    bufs = [pltpu.VMEM((r, w), F32) for r, w in zip(rows, widths)]
    return pl.pallas_call(
        body, name="small_allreduce", out_shape=[jax.ShapeDtypeStruct(s, F32) for s in shapes],
        in_specs=[vm] * n, out_specs=[vm] * n,
        scratch_shapes=bufs * 3 + [pltpu.VMEM((N_CHIP, r, w), F32) for r, w in zip(rows, widths)] +
                       [pltpu.SemaphoreType.DMA((4 * n_g,)), pltpu.SemaphoreType.DMA((4 * n_g,))],
        compiler_params=_cp(None, 40),
    )(*arrays)


def _adamw_small(ws, gs, ms, vs):
    n = len(ws)

    def body(*refs):
        w, g, m, v, d, mo, vo = (refs[i * n:(i + 1) * n] for i in range(7))
        for a in range(n):
            d[a][...], mo[a][...], vo[a][...] = _adamw_math(w[a][...], g[a][...], m[a][...], v[a][...])

    vm = pl.BlockSpec(memory_space=pltpu.VMEM)
    res = pl.pallas_call(
        body, name="adamw_small", out_shape=[jax.ShapeDtypeStruct(w.shape, F32) for w in ws] * 3,
        in_specs=[vm] * (4 * n), out_specs=[vm] * (3 * n), compiler_params=_cp(None, 40),
    )(*ws, *gs, *ms, *vs)
    return res[:n], res[n:2 * n], res[2 * n:]


def _adamw_math(w, g, m, v):
    m = ADAM_B1 * m + (1.0 - ADAM_B1) * g
    v = ADAM_B2 * v + (1.0 - ADAM_B2) * (g * g)
    m_hat = m / (1.0 - ADAM_B1 ** ADAM_STEP)
    v_hat = v / (1.0 - ADAM_B2 ** ADAM_STEP)
    delta = -ADAM_LR * (m_hat / (jnp.sqrt(v_hat) + ADAM_EPS) + ADAM_WD * w)
    return delta, m, v


def _adamw(name, w, g, m, v):
    r, c = w.shape
    tr = max(t for t in range(8, 513, 8) if r % t == 0)

    def body(w_ref, g_ref, m_ref, v_ref, d_ref, mo_ref, vo_ref):
        d_ref[...], mo_ref[...], vo_ref[...] = _adamw_math(w_ref[...], g_ref[...], m_ref[...], v_ref[...])

    return pl.pallas_call(
        body, name=name, grid=(r // tr,), in_specs=[_rows(tr, c)] * 4, out_specs=[_rows(tr, c)] * 3,
        out_shape=[jax.ShapeDtypeStruct((r, c), F32)] * 3, compiler_params=_cp(("arbitrary",), 32),
    )(w, g, m, v)


def _as_matrix(name, a):
    if name == "na_rpb":
        return a[0].transpose(1, 0, 2).reshape(N_HEADS * (2 * KH - 1), 2 * KW - 1)
    if name in ("s5_b_re", "s5_b_im"):
        return a.transpose(0, 1, 2, 4, 3).reshape(2 * S5_G * S5_H, S5_P)
    if name in ("s5_c_re", "s5_c_im"):
        return a.reshape(2 * S5_G * S5_H, S5_P)
    if name in ("s5_lam_re", "s5_lam_im"):
        return a.reshape(2 * S5_G, S5_P)
    if name == "s5_log_dt":
        return a.reshape(2, S5_G)
    return a


def _from_matrix(name, m):
    if name == "na_rpb":
        return m.reshape(2 * KH - 1, N_HEADS, 2 * KW - 1).transpose(1, 0, 2)[None]
    if name in ("s5_b_re", "s5_b_im"):
        return m.reshape(1, 2, S5_G, S5_H, S5_P).transpose(0, 1, 2, 4, 3)
    if name in ("s5_c_re", "s5_c_im"):
        return m.reshape(1, 2, S5_G, S5_H, S5_P)
    if name in ("s5_lam_re", "s5_lam_im"):
        return m.reshape(1, 2, S5_G, S5_P)
    if name == "s5_log_dt":
        return m.reshape(1, 2, S5_G)
    return m


WEIGHTS = ["meta_tokens", "ffn1_pre_g", "ffn1_post_g", "ffn1_w_gate", "ffn1_w_up", "ffn1_w_down", "mix_pre_g", "w_in",
           "na_rpb", "s5_lam_re", "s5_lam_im", "s5_log_dt", "s5_b_re", "s5_b_im", "s5_c_re", "s5_c_im", "s5_d",
           "s5_w_glu", "s5_b_glu", "na_out_g", "s5_out_g", "w_out", "mix_post_g", "ffn2_pre_g", "ffn2_post_g",
           "ffn2_w_gate", "ffn2_w_up", "ffn2_w_down", "final_g"]
BIG = ["ffn1_w_gate", "ffn1_w_up", "ffn1_w_down", "w_in", "s5_w_glu", "w_out", "ffn2_w_gate", "ffn2_w_up",
       "ffn2_w_down"]
TRANSPOSED = ["ffn1_w_gate", "ffn1_w_up", "ffn2_w_gate", "ffn2_w_up"]
GAINS = ["ffn1_pre_g", "ffn1_post_g", "mix_pre_g", "s5_d", "s5_b_glu", "na_out_g", "s5_out_g", "mix_post_g",
         "ffn2_pre_g", "ffn2_post_g", "final_g"]
SMALL = [n for n in WEIGHTS if n not in BIG]


def kernel(*args):
    names = ["x"] + WEIGHTS + ["loss_target"] + ["m_" + n for n in WEIGHTS] + ["v_" + n for n in WEIGHTS]
    assert len(args) == len(names)
    given = dict(zip(names, args))
    x_pos, y_pos, c_pos = _mesh_pos()
    k_pos = 2 * x_pos + y_pos
    c_arr = jnp.reshape(c_pos, (1,)).astype(jnp.int32)
    kc_arr = jnp.stack([k_pos, c_pos]).astype(jnp.int32)

    def piece(name, a):
        return a[0].T if name in TRANSPOSED else a[0]

    def unpiece(name, a):
        return a.T[None] if name in TRANSPOSED else a[None]

    placed = BIG + ["meta_tokens"]
    bufs = dict(zip(placed, _own_half_buffers([piece(n, given[n]) for n in BIG] + [given["meta_tokens"]],
                                              [BF16] * len(BIG) + [F32], kc_arr)))

    gains = {n: given[n] for n in GAINS}
    s5 = {n: given["s5_" + n][0] for n in ["lam_re", "lam_im", "log_dt", "b_re", "b_im", "c_re", "c_im"]}
    loss, dh0, pieces, small = _step(given["x"][0], given["loss_target"][0], bufs, gains, s5, given["na_rpb"][0],
                                     c_arr, kc_arr)
    loss = lax.psum(loss, ("x", "y", "c"))
    n_tok = given["x"].shape[1]
    grad_x = dh0[N_META:N_META + n_tok][None]

    small["meta_tokens"] = dh0[:N_META]
    small = dict(zip(SMALL, _small_allreduce([small[n] for n in SMALL])))
    mc = D // N_CHIP
    small["meta_tokens"] = lax.dynamic_slice_in_dim(small["meta_tokens"], k_pos * mc, mc, 1)

    out_g, out_d, out_m, out_v = {}, {}, {}, {}
    for n in BIG:
        g2 = pieces[n]
        d2, m2, v2 = _adamw("adamw_" + n, piece(n, given[n]), g2, piece(n, given["m_" + n]),
                            piece(n, given["v_" + n]))
        out_g[n], out_d[n], out_m[n], out_v[n] = (unpiece(n, t) for t in (g2, d2, m2, v2))
    gs = [small[n] for n in SMALL]
    d2, m2, v2 = _adamw_small([_as_matrix(n, given[n]) for n in SMALL], gs,
                              [_as_matrix(n, given["m_" + n]) for n in SMALL],
                              [_as_matrix(n, given["v_" + n]) for n in SMALL])
    for n, g, dd, mm, vv in zip(SMALL, gs, d2, m2, v2):
        out_g[n], out_d[n], out_m[n], out_v[n] = (_from_matrix(n, t) for t in (g, dd, mm, vv))
    return (loss, grad_x, *[out_g[n] for n in WEIGHTS], *[out_d[n] for n in WEIGHTS],
            *[out_m[n] for n in WEIGHTS], *[out_v[n] for n in WEIGHTS])
```

```python
import functools
import math

import numpy as np
import jax
import jax.numpy as jnp
from jax import lax
from jax.experimental import pallas as pl
from jax.experimental.pallas import tpu as pltpu

F32 = jnp.float32
BF16 = jnp.bfloat16

D = 1024
N_META = 16
GRID_W = 64
NA_W = 512
S5_W = 512
HEAD_DIM = 64
N_HEADS = 8
KH = 8
KW = 16
S5_G = 32
S5_P = 64
S5_H = 16
N_BUNDLE = 4
FF = 2816
N_CHIP = 4
FC = FF // N_CHIP
EPS = 1e-6
NEG_INF = -1e30
Q_ROWS = 4
K_ROWS = 12
QB = Q_ROWS * GRID_W
KB = K_ROWS * GRID_W
SCAN_CHUNK = 256

ADAM_LR = 0.001
ADAM_B1 = 0.9
ADAM_B2 = 0.999
ADAM_EPS = 1e-08
ADAM_WD = 0.01
ADAM_STEP = 10

NT = (((1,), (1,)), ((), ()))
TN = (((0,), (0,)), ((), ()))
MESH_ID = pl.DeviceIdType.MESH


def _cp(sem=None, vmem_mb=None):
    kw = {}
    if sem is not None:
        kw["dimension_semantics"] = sem
    if vmem_mb is not None:
        kw["vmem_limit_bytes"] = vmem_mb << 20
    return pltpu.CompilerParams(**kw)


def _full(shape):
    n = len(shape)
    return pl.BlockSpec(shape, lambda *_: (0,) * n)


def _rows(tm, w):
    return pl.BlockSpec((tm, w), lambda i: (i, 0))


ANY = pl.BlockSpec(memory_space=pl.ANY)


def _rms(x, g):
    r = lax.rsqrt(jnp.mean(x * x, axis=-1, keepdims=True) + EPS)
    return x * r * g


def _rms_bwd(x, g, dy):
    r = lax.rsqrt(jnp.mean(x * x, axis=-1, keepdims=True) + EPS)
    xh = x * r
    dg = jnp.sum(dy * xh, axis=0, keepdims=True)
    dyg = dy * g
    dx = r * (dyg - xh * jnp.mean(dyg * xh, axis=-1, keepdims=True))
    return dx, dg


def _dot(a, b):
    return jnp.dot(a, b, preferred_element_type=F32)


def _dg(a, b, dims):
    return lax.dot_general(a, b, dims, preferred_element_type=F32)


def _ffn_fwd(name, h, g_pre, g_post, wg, wu, wd, tm, comm=None, bounds=()):
    tp = h.shape[0]
    nt = tp // tm

    def body(h_ref, gp_ref, gq_ref, wg_ref, wu_ref, wd_ref, hn_ref, gate_ref, up_ref, f_ref, xn_s, acc_s):
        c = pl.program_id(1)

        @pl.when(c == 0)
        def _():
            xn_s[...] = _rms(h_ref[...], gp_ref[...]).astype(BF16)
            acc_s[...] = jnp.zeros_like(acc_s)

        xn = xn_s[...]
        gate = _dg(xn, wg_ref[0], NT)
        up = _dg(xn, wu_ref[0], NT)
        gate_ref[0] = gate
        up_ref[0] = up
        act = (gate * jax.nn.sigmoid(gate) * up).astype(BF16)
        acc_s[...] += _dot(act, wd_ref[0])

        @pl.when(c == N_CHIP - 1)
        def _():
            f = acc_s[...]
            f_ref[...] = f
            hn_ref[...] = h_ref[...] + 0.5 * _rms(f, gq_ref[...])

    return _call(
        body, comm, bounds, (h, g_pre, g_post, wg, wu, wd), name=name, grid=(nt, N_CHIP),
        in_specs=[pl.BlockSpec((tm, D), lambda i, c: (i, 0)), _full((1, D)), _full((1, D))] +
                 [pl.BlockSpec((1, FC, D), lambda i, c: (c, 0, 0))] * 3,
        out_specs=[pl.BlockSpec((tm, D), lambda i, c: (i, 0)),
                   pl.BlockSpec((1, tm, FC), lambda i, c: (c, i, 0)),
                   pl.BlockSpec((1, tm, FC), lambda i, c: (c, i, 0)),
                   pl.BlockSpec((tm, D), lambda i, c: (i, 0))],
        out_shape=[jax.ShapeDtypeStruct((tp, D), F32), jax.ShapeDtypeStruct((N_CHIP, tp, FC), F32),
                   jax.ShapeDtypeStruct((N_CHIP, tp, FC), F32), jax.ShapeDtypeStruct((tp, D), F32)],
        scratch_shapes=[pltpu.VMEM((tm, D), BF16), pltpu.VMEM((tm, D), F32)],
        compiler_params=_cp(("arbitrary", "arbitrary"), 48))


def _ffn_bwd(name, h, g_pre, df, gate, up, wg, wu, wd, tm):
    tp = h.shape[0]
    nt = tp // tm
    rh = FC // 2

    def body(h_ref, gp_ref, df_ref, gate_ref, up_ref, wg_ref, wu_ref, wd_ref,
             dwg_ref, dwu_ref, dwd_ref, dxn_ref, rg_ref, ru_ref, rd_ref, ag, au, ad, send_sems, recv_sems):
        c = pl.program_id(0)
        i = pl.program_id(1)

        def to_sibling(a, piece):
            x, y, core = _mesh_pos()
            dw_ref, r_ref = ((dwg_ref, rg_ref), (dwu_ref, ru_ref), (dwd_ref, rd_ref))[a]
            return _remote(dw_ref.at[piece, pl.ds((1 - core) * rh, rh), :], r_ref.at[piece], send_sems, recv_sems,
                           3 * piece + a, (x, y, 1 - core))

        @pl.when(i == 0)
        def _():
            ag[...] = jnp.zeros_like(ag)
            au[...] = jnp.zeros_like(au)
            ad[...] = jnp.zeros_like(ad)

        xn = _rms(h_ref[...], gp_ref[...]).astype(BF16)
        dfb = df_ref[...].astype(BF16)
        gt = gate_ref[0]
        u = up_ref[0]
        sg = jax.nn.sigmoid(gt)
        si = gt * sg
        act = (si * u).astype(BF16)
        dact = _dg(dfb, wd_ref[0], NT)
        ad[...] += _dg(act, dfb, TN)
        dgate = (dact * u * (sg * (1.0 + gt * (1.0 - sg)))).astype(BF16)
        dup = (dact * si).astype(BF16)
        ag[...] += _dg(dgate, xn, TN)
        au[...] += _dg(dup, xn, TN)
        dxn_ref[0] = _dot(dgate, wg_ref[0]) + _dot(dup, wu_ref[0])

        @pl.when(i == nt - 1)
        def _():
            pltpu.sync_copy(ag, dwg_ref.at[c])
            pltpu.sync_copy(au, dwu_ref.at[c])
            pltpu.sync_copy(ad, dwd_ref.at[c])
            for a in range(3):
                to_sibling(a, c).start()

        @pl.when((c == N_CHIP - 1) & (i == nt - 1))
        def _():
            for piece in range(N_CHIP):
                for a in range(3):
                    to_sibling(a, piece).wait()

    return pl.pallas_call(
        body, name=name, grid=(N_CHIP, nt),
        in_specs=[pl.BlockSpec((tm, D), lambda c, i: (i, 0)), _full((1, D)),
                  pl.BlockSpec((tm, D), lambda c, i: (i, 0)),
                  pl.BlockSpec((1, tm, FC), lambda c, i: (c, i, 0)),
                  pl.BlockSpec((1, tm, FC), lambda c, i: (c, i, 0))] +
                 [pl.BlockSpec((1, FC, D), lambda c, i: (c, 0, 0))] * 3,
        out_specs=[ANY, ANY, ANY, pl.BlockSpec((1, tm, D), lambda c, i: (c, i, 0)), ANY, ANY, ANY],
        out_shape=[jax.ShapeDtypeStruct((N_CHIP, FC, D), F32)] * 3 + [jax.ShapeDtypeStruct((N_CHIP, tp, D), F32)] +
                  [jax.ShapeDtypeStruct((N_CHIP, rh, D), F32)] * 3,
        scratch_shapes=[pltpu.VMEM((FC, D), F32)] * 3 +
                       [pltpu.SemaphoreType.DMA((3 * N_CHIP,)), pltpu.SemaphoreType.DMA((3 * N_CHIP,))],
        compiler_params=_cp(("arbitrary", "arbitrary"), 58),
    )(h, g_pre, df, gate, up, wg, wu, wd)


def _ffn_pre_bwd(name, dh, dxn_part, h, g_pre, tm, comm=None, bounds=()):
    tp = h.shape[0]
    nt = tp // tm

    def body(dh_ref, dxn_ref, h_ref, gp_ref, out_ref, dg_ref):
        i = pl.program_id(0)
        dxn = (dxn_ref[0] + dxn_ref[1]) + (dxn_ref[2] + dxn_ref[3])
        dx, dg = _rms_bwd(h_ref[...], gp_ref[...], dxn)
        out_ref[...] = dh_ref[...] + dx

        @pl.when(i == 0)
        def _():
            dg_ref[...] = jnp.zeros_like(dg_ref)

        dg_ref[...] += dg

    return _call(
        body, comm, bounds, (dh, dxn_part, h, g_pre), name=name, grid=(nt,),
        in_specs=[_rows(tm, D), pl.BlockSpec((N_CHIP, tm, D), lambda i: (0, i, 0)), _rows(tm, D), _full((1, D))],
        out_specs=[_rows(tm, D), _full((1, D))],
        out_shape=[jax.ShapeDtypeStruct((tp, D), F32), jax.ShapeDtypeStruct((1, D), F32)],
        compiler_params=_cp(("arbitrary",), 48))


def _mix_in(h, g, w_in, tm):
    tp = h.shape[0]

    def body(h_ref, g_ref, w_ref, q_ref, k_ref, v_ref, u_ref):
        a = _rms(h_ref[...], g_ref[...]).astype(BF16)
        q_ref[...] = _dot(a, w_ref[0]).astype(BF16)
        k_ref[...] = _dot(a, w_ref[1]).astype(BF16)
        v_ref[...] = _dot(a, w_ref[2]).astype(BF16)
        u_ref[...] = _dot(a, w_ref[3])

    return pl.pallas_call(
        body, name="mix_in", grid=(tp // tm,),
        in_specs=[_rows(tm, D), _full((1, D)), _full((N_CHIP, D, NA_W))],
        out_specs=[_rows(tm, NA_W)] * 4,
        out_shape=[jax.ShapeDtypeStruct((tp, NA_W), BF16)] * 3 + [jax.ShapeDtypeStruct((tp, S5_W), F32)],
        compiler_params=_cp(("arbitrary",), 40),
    )(h, g, w_in)


def _gelu(x):
    return jax.nn.gelu(x, approximate=True)


def _gelu_grad(x):
    k = math.sqrt(2.0 / math.pi)
    t = jnp.tanh(k * (x + 0.044715 * x * x * x))
    return 0.5 * (1.0 + t) + 0.5 * x * (1.0 - t * t) * k * (1.0 + 3.0 * 0.044715 * x * x)


def _mix_out(o_na, y_pre, h, w_glu, b_glu, g_na, g_s5, w_out, g_post, tm, comm=None, bounds=()):
    tp = h.shape[0]

    def body(ona_ref, yp_ref, h_ref, wglu_ref, bglu_ref, gna_ref, gs5_ref, wout_ref, gpost_ref, hn_ref, mix_ref):
        y = _gelu(yp_ref[...])
        z = _dot(y.astype(BF16), wglu_ref[...]) + bglu_ref[...]
        o_s5 = y * jax.nn.sigmoid(z)
        n1 = _rms(ona_ref[...], gna_ref[...]).astype(BF16)
        n2 = _rms(o_s5, gs5_ref[...]).astype(BF16)
        mix = _dot(n1, wout_ref[0:NA_W, :]) + _dot(n2, wout_ref[NA_W:, :])
        mix_ref[...] = mix
        hn_ref[...] = h_ref[...] + _rms(mix, gpost_ref[...])

    return _call(
        body, comm, bounds, (o_na, y_pre, h, w_glu, b_glu, g_na, g_s5, w_out, g_post), name="mix_out",
        grid=(tp // tm,),
        in_specs=[_rows(tm, NA_W), _rows(tm, S5_W), _rows(tm, D), _full((S5_W, S5_W)), _full((1, S5_W)),
                  _full((1, NA_W)), _full((1, S5_W)), _full((D, D)), _full((1, D))],
        out_specs=[_rows(tm, D), _rows(tm, D)],
        out_shape=[jax.ShapeDtypeStruct((tp, D), F32)] * 2,
        compiler_params=_cp(("arbitrary",), 40))


def _mix_out_bwd(dh, mix, o_na, y_pre, w_glu, b_glu, g_na, g_s5, w_out, g_post, tm):
    tp = dh.shape[0]
    nt = tp // tm

    def body(dh_ref, mix_ref, ona_ref, yp_ref, wglu_ref, bglu_ref, gna_ref, gs5_ref, wout_ref, gpost_ref,
             dona_ref, dyp_ref, dwout_ref, dwglu_ref, dgpost_ref, dgna_ref, dgs5_ref, dbglu_ref, a_out, a_glu):
        i = pl.program_id(0)

        @pl.when(i == 0)
        def _():
            a_out[...] = jnp.zeros_like(a_out)
            a_glu[...] = jnp.zeros_like(a_glu)
            dgpost_ref[...] = jnp.zeros_like(dgpost_ref)
            dgna_ref[...] = jnp.zeros_like(dgna_ref)
            dgs5_ref[...] = jnp.zeros_like(dgs5_ref)
            dbglu_ref[...] = jnp.zeros_like(dbglu_ref)

        dmix, dgpost = _rms_bwd(mix_ref[...], gpost_ref[...], dh_ref[...])
        dgpost_ref[...] += dgpost
        yp = yp_ref[...]
        y = _gelu(yp)
        yb = y.astype(BF16)
        z = _dot(yb, wglu_ref[...]) + bglu_ref[...]
        sg = jax.nn.sigmoid(z)
        o_s5 = y * sg
        o_na = ona_ref[...]
        n1 = _rms(o_na, gna_ref[...]).astype(BF16)
        n2 = _rms(o_s5, gs5_ref[...]).astype(BF16)
        dmb = dmix.astype(BF16)
        a_out[0:NA_W, :] += _dg(n1, dmb, TN)
        a_out[NA_W:, :] += _dg(n2, dmb, TN)
        dn1 = _dg(dmb, wout_ref[0:NA_W, :], NT)
        dn2 = _dg(dmb, wout_ref[NA_W:, :], NT)
        dona, dgna = _rms_bwd(o_na, gna_ref[...], dn1)
        dona_ref[...] = dona
        dgna_ref[...] += dgna
        dos5, dgs5 = _rms_bwd(o_s5, gs5_ref[...], dn2)
        dgs5_ref[...] += dgs5
        dz = dos5 * y * (sg * (1.0 - sg))
        dbglu_ref[...] += jnp.sum(dz, axis=0, keepdims=True)
        dzb = dz.astype(BF16)
        a_glu[...] += _dg(yb, dzb, TN)
        dy = dos5 * sg + _dg(dzb, wglu_ref[...], NT)
        dyp_ref[...] = dy * _gelu_grad(yp)

        @pl.when(i == nt - 1)
        def _():
            pltpu.sync_copy(a_out, dwout_ref)
            pltpu.sync_copy(a_glu, dwglu_ref)

    return pl.pallas_call(
        body, name="mix_out_bwd", grid=(nt,),
        in_specs=[_rows(tm, D), _rows(tm, D), _rows(tm, NA_W), _rows(tm, S5_W), _full((S5_W, S5_W)),
                  _full((1, S5_W)), _full((1, NA_W)), _full((1, S5_W)), _full((D, D)), _full((1, D))],
        out_specs=[_rows(tm, NA_W), _rows(tm, S5_W), ANY, ANY, _full((1, D)), _full((1, NA_W)),
                   _full((1, S5_W)), _full((1, S5_W))],
        out_shape=[jax.ShapeDtypeStruct((tp, NA_W), F32), jax.ShapeDtypeStruct((tp, S5_W), F32),
                   jax.ShapeDtypeStruct((D, D), F32), jax.ShapeDtypeStruct((S5_W, S5_W), F32),
                   jax.ShapeDtypeStruct((1, D), F32), jax.ShapeDtypeStruct((1, NA_W), F32),
                   jax.ShapeDtypeStruct((1, S5_W), F32), jax.ShapeDtypeStruct((1, S5_W), F32)],
        scratch_shapes=[pltpu.VMEM((D, D), F32), pltpu.VMEM((S5_W, S5_W), F32)],
        compiler_params=_cp(("arbitrary",), 48),
    )(dh, mix, o_na, y_pre, w_glu, b_glu, g_na, g_s5, w_out, g_post)


def _mix_in_bwd(dq, dk, dv, du, h, g, w_in, dh, f1, g_post1, tm, comm=None, bounds=()):
    tp = h.shape[0]
    nt = tp // tm

    def body(dq_ref, dk_ref, dv_ref, du_ref, h_ref, g_ref, w_ref, dh_ref, f_ref, gq_ref,
             dh1_ref, df_ref, dw_ref, dg_ref, dgq_ref, acc):
        i = pl.program_id(0)

        @pl.when(i == 0)
        def _():
            acc[...] = jnp.zeros_like(acc)
            dg_ref[...] = jnp.zeros_like(dg_ref)
            dgq_ref[...] = jnp.zeros_like(dgq_ref)

        x = h_ref[...]
        a = _rms(x, g_ref[...]).astype(BF16)
        da = jnp.zeros((tm, D), F32)
        for j, r in enumerate((dq_ref, dk_ref, dv_ref, du_ref)):
            dp = r[...].astype(BF16)
            da = da + _dg(dp, w_ref[j], NT)
            acc[j] += _dg(a, dp, TN)
        dx, dg = _rms_bwd(x, g_ref[...], da)
        dh1 = dh_ref[...] + dx
        dh1_ref[...] = dh1
        dg_ref[...] += dg
        df, dgq = _rms_bwd(f_ref[...], gq_ref[...], 0.5 * dh1)
        df_ref[...] = df
        dgq_ref[...] += dgq

        @pl.when(i == nt - 1)
        def _():
            pltpu.sync_copy(acc, dw_ref)

    return _call(
        body, comm, bounds, (dq, dk, dv, du, h, g, w_in, dh, f1, g_post1), name="mix_in_bwd", grid=(nt,),
        in_specs=[_rows(tm, NA_W)] * 4 + [_rows(tm, D), _full((1, D)), _full((N_CHIP, D, NA_W)), _rows(tm, D),
                                         _rows(tm, D), _full((1, D))],
        out_specs=[_rows(tm, D), _rows(tm, D), ANY, _full((1, D)), _full((1, D))],
        out_shape=[jax.ShapeDtypeStruct((tp, D), F32), jax.ShapeDtypeStruct((tp, D), F32),
                   jax.ShapeDtypeStruct((N_CHIP, D, NA_W), F32), jax.ShapeDtypeStruct((1, D), F32),
                   jax.ShapeDtypeStruct((1, D), F32)],
        scratch_shapes=[pltpu.VMEM((N_CHIP, D, NA_W), F32)],
        compiler_params=_cp(("arbitrary",), 48))


def _final_loss(h, g_final, target, f2, g_post2, n_tok, tm):
    tp = h.shape[0]

    def body(h_ref, g_ref, t_ref, f_ref, gq_ref, dh_ref, df_ref, loss_ref, dg_ref, dgq_ref):
        i = pl.program_id(0)

        @pl.when(i == 0)
        def _():
            loss_ref[...] = jnp.zeros_like(loss_ref)
            dg_ref[...] = jnp.zeros_like(dg_ref)
            dgq_ref[...] = jnp.zeros_like(dgq_ref)

        x = h_ref[...]
        y = _rms(x, g_ref[...])
        row = i * tm + lax.broadcasted_iota(jnp.int32, (tm, 1), 0)
        valid = (row >= N_META) & (row < N_META + n_tok)
        e = jnp.where(valid, y - t_ref[...], 0.0)
        loss_ref[...] += 0.5 * jnp.sum(jnp.mean(e * e, axis=-1, keepdims=True), axis=0, keepdims=True)
        dx, dg = _rms_bwd(x, g_ref[...], e * (1.0 / D))
        dh_ref[...] = dx
        dg_ref[...] += dg
        df, dgq = _rms_bwd(f_ref[...], gq_ref[...], 0.5 * dx)
        df_ref[...] = df
        dgq_ref[...] += dgq

    return pl.pallas_call(
        body, name="final_loss", grid=(tp // tm,),
        in_specs=[_rows(tm, D), _full((1, D)), _rows(tm, D), _rows(tm, D), _full((1, D))],
        out_specs=[_rows(tm, D), _rows(tm, D), _full((1, 1)), _full((1, D)), _full((1, D))],
        out_shape=[jax.ShapeDtypeStruct((tp, D), F32), jax.ShapeDtypeStruct((tp, D), F32),
                   jax.ShapeDtypeStruct((1, 1), F32), jax.ShapeDtypeStruct((1, D), F32),
                   jax.ShapeDtypeStruct((1, D), F32)],
        compiler_params=_cp(("arbitrary",), 40),
    )(h, g_final, target, f2, g_post2)


def _na_patterns(n_rows):
    pats = []
    for kind in range(3):
        pat = [[-1] * K_ROWS for _ in range(Q_ROWS)]
        for i in range(Q_ROWS):
            for jj in range(K_ROWS):
                if kind == 0 and jj < KH:
                    pat[i][jj] = jj - i + KH - 1
                elif kind == 1 and i <= jj < i + KH:
                    pat[i][jj] = jj - i + 3
                elif kind == 2 and K_ROWS - KH <= jj:
                    pat[i][jj] = jj - i - 1
        pats.append(pat)
    return pats


def _diag_onehot():
    q = np.arange(GRID_W)[:, None]
    kc = np.arange(GRID_W)[None, :]
    start = np.clip(q - KW // 2, 0, GRID_W - KW)
    col_in = (kc >= start) & (kc < start + KW)
    e = np.zeros((32, GRID_W, GRID_W), np.float32)
    for d in range(2 * KW - 1):
        e[d] = ((kc - q + KW - 1) == d) & col_in
    return e.reshape(32, GRID_W * GRID_W), col_in


def _rpb_collapse(dtb2, et):
    def body(d_ref, e_ref, o_ref):
        o_ref[...] = jnp.dot(d_ref[...], e_ref[...], preferred_element_type=F32, precision=lax.Precision.HIGHEST)

    return pl.pallas_call(
        body, name="rpb_collapse", out_shape=jax.ShapeDtypeStruct((dtb2.shape[0], et.shape[1]), F32),
        in_specs=[pl.BlockSpec(memory_space=pltpu.VMEM)] * 2, out_specs=pl.BlockSpec(memory_space=pltpu.VMEM),
    )(dtb2, et)


def _bias_tables(rpb, n_rows):
    n_dr, n_dc = 2 * KH - 1, 2 * KW - 1
    pats = _na_patterns(n_rows)

    def body(rpb_ref, o_ref):
        h = pl.program_id(0)
        q = lax.broadcasted_iota(jnp.int32, (GRID_W, GRID_W), 0)
        kc = lax.broadcasted_iota(jnp.int32, (GRID_W, GRID_W), 1)
        start = jnp.clip(q - KW // 2, 0, GRID_W - KW)
        col_in = (kc >= start) & (kc < start + KW)
        diff = kc - q + (KW - 1)
        neg = jnp.full((GRID_W, GRID_W), NEG_INF, F32)
        band = []
        for dr in range(n_dr):
            acc = neg
            for d in range(n_dc):
                acc = jnp.where((diff == d) & col_in, rpb_ref[(h * n_dr + dr) * n_dc + d], acc)
            band.append(acc)
        for kind, pat in enumerate(pats):
            for i in range(Q_ROWS):
                for jj in range(K_ROWS):
                    o_ref[kind, 0, i * GRID_W:(i + 1) * GRID_W, jj * GRID_W:(jj + 1) * GRID_W] = (
                        band[pat[i][jj]] if pat[i][jj] >= 0 else neg)

    return pl.pallas_call(
        body, name="bias_tables", grid=(N_HEADS,),
        in_specs=[pl.BlockSpec(memory_space=pltpu.SMEM)],
        out_specs=pl.BlockSpec((3, 1, QB, KB), lambda h: (0, h, 0, 0)),
        out_shape=jax.ShapeDtypeStruct((3, N_HEADS, QB, KB), F32),
        compiler_params=_cp(("arbitrary",), 32),
    )(rpb.reshape(-1))


def _attn_geometry(n_tok):
    n_rows = n_tok // GRID_W
    assert n_rows % Q_ROWS == 0 and n_rows >= K_ROWS
    return n_rows, n_rows // Q_ROWS


def _attn_probs(qh, kh, kmh, bias, scale):
    s = _dg(qh, kh, NT) * scale + bias
    sm = _dg(qh, kmh, NT) * scale
    m = jnp.maximum(jnp.max(s, axis=-1, keepdims=True), jnp.max(sm, axis=-1, keepdims=True))
    p = jnp.exp(s - m)
    pm = jnp.exp(sm - m)
    inv = 1.0 / (jnp.sum(p, axis=-1, keepdims=True) + jnp.sum(pm, axis=-1, keepdims=True))
    return p * inv, pm * inv


def _meta_probs(qmh, kmh, scale):
    s = _dg(qmh, kmh, NT) * scale
    p = jnp.exp(s - jnp.max(s, axis=-1, keepdims=True))
    return p / jnp.sum(p, axis=-1, keepdims=True)


def _step_rows(r, n_rows):
    q0 = pl.multiple_of(N_META + r * QB, 16)
    k0 = pl.multiple_of(N_META + jnp.clip(Q_ROWS * r - (K_ROWS - KH), 0, n_rows - K_ROWS) * GRID_W, 16)
    return q0, k0


def _attn_fwd(q, k, v, bias, n_tok, comm=None, bounds=()):
    tp = q.shape[0]
    n_rows, n_steps = _attn_geometry(n_tok)
    scale = HEAD_DIM ** -0.5

    def body(q_ref, k_ref, v_ref, b_ref, o_ref):
        r = pl.program_id(1)
        km = k_ref[0:N_META, :]
        vm = v_ref[0:N_META, :]

        @pl.when(r == 0)
        def _():
            qm = q_ref[0:N_META, :]
            outs = []
            for hh in range(2):
                sl = slice(hh * HEAD_DIM, (hh + 1) * HEAD_DIM)
                p = _meta_probs(qm[:, sl], km[:, sl], scale)
                outs.append(_dot(p.astype(BF16), vm[:, sl]))
            o_ref[0:N_META, :] = jnp.concatenate(outs, axis=1)
            o_ref[N_META + n_tok:, :] = jnp.zeros((tp - N_META - n_tok, 2 * HEAD_DIM), F32)

        q0, k0 = _step_rows(r, n_rows)
        qb = q_ref[pl.ds(q0, QB), :]
        kb = k_ref[pl.ds(k0, KB), :]
        vb = v_ref[pl.ds(k0, KB), :]
        outs = []
        for hh in range(2):
            sl = slice(hh * HEAD_DIM, (hh + 1) * HEAD_DIM)
            p, pm = _attn_probs(qb[:, sl], kb[:, sl], km[:, sl], b_ref[0, hh], scale)
            outs.append(_dot(p.astype(BF16), vb[:, sl]) + _dot(pm.astype(BF16), vm[:, sl]))
        o_ref[pl.ds(q0, QB), :] = jnp.concatenate(outs, axis=1)

    def bias_map(hp, r):
        return (jnp.where(r == 0, 0, jnp.where(r == n_steps - 1, 2, 1)), hp, 0, 0)

    col = pl.BlockSpec((tp, 2 * HEAD_DIM), lambda hp, r: (0, hp))
    return _call(
        body, comm, bounds, (q, k, v, bias), name="attn_fwd", grid=(N_HEADS // 2, n_steps),
        in_specs=[col, col, col, pl.BlockSpec((1, 2, QB, KB), bias_map)],
        out_specs=[col], out_shape=[jax.ShapeDtypeStruct((tp, NA_W), F32)],
        compiler_params=_cp(("arbitrary", "arbitrary"), 40))


def _attn_bwd(q, k, v, bias, do, n_tok, comm=None, bounds=()):
    tp = q.shape[0]
    n_rows, n_steps = _attn_geometry(n_tok)
    scale = HEAD_DIM ** -0.5
    pats = _na_patterns(n_rows)

    def body(q_ref, k_ref, v_ref, b_ref, do_ref, dq_ref, dk_ref, dv_ref, dtb_ref):
        r = pl.program_id(1)
        km = k_ref[0:N_META, :]
        vm = v_ref[0:N_META, :]

        @pl.when(r == 0)
        def _():
            dk_ref[...] = jnp.zeros_like(dk_ref)
            dv_ref[...] = jnp.zeros_like(dv_ref)
            dtb_ref[...] = jnp.zeros_like(dtb_ref)
            dq_ref[N_META + n_tok:, :] = jnp.zeros((tp - N_META - n_tok, 2 * HEAD_DIM), F32)
            qm = q_ref[0:N_META, :]
            dom = do_ref[0:N_META, :].astype(BF16)
            dqs, dks, dvs = [], [], []
            for hh in range(2):
                sl = slice(hh * HEAD_DIM, (hh + 1) * HEAD_DIM)
                p = _meta_probs(qm[:, sl], km[:, sl], scale)
                dp = _dg(dom[:, sl], vm[:, sl], NT)
                ds = (p * (dp - jnp.sum(dp * p, axis=-1, keepdims=True))).astype(BF16)
                dvs.append(_dg(p.astype(BF16), dom[:, sl], TN))
                dqs.append(_dot(ds, km[:, sl]) * scale)
                dks.append(_dg(ds, qm[:, sl], TN) * scale)
            dq_ref[0:N_META, :] = jnp.concatenate(dqs, axis=1)
            dk_ref[0:N_META, :] += jnp.concatenate(dks, axis=1)
            dv_ref[0:N_META, :] += jnp.concatenate(dvs, axis=1)

        q0, k0 = _step_rows(r, n_rows)
        qb = q_ref[pl.ds(q0, QB), :]
        kb = k_ref[pl.ds(k0, KB), :]
        vb = v_ref[pl.ds(k0, KB), :]
        dob = do_ref[pl.ds(q0, QB), :].astype(BF16)
        dqs, dks, dvs, dkms, dvms, dss = [], [], [], [], [], []
        for hh in range(2):
            sl = slice(hh * HEAD_DIM, (hh + 1) * HEAD_DIM)
            qh, kh, vh, kmh, vmh, doh = qb[:, sl], kb[:, sl], vb[:, sl], km[:, sl], vm[:, sl], dob[:, sl]
            p, pm = _attn_probs(qh, kh, kmh, b_ref[0, hh], scale)
            dp = _dg(doh, vh, NT)
            dpm = _dg(doh, vmh, NT)
            delta = jnp.sum(dp * p, axis=-1, keepdims=True) + jnp.sum(dpm * pm, axis=-1, keepdims=True)
            ds = p * (dp - delta)
            dsb = ds.astype(BF16)
            dsmb = (pm * (dpm - delta)).astype(BF16)
            dss.append(ds)
            dvs.append(_dg(p.astype(BF16), doh, TN))
            dvms.append(_dg(pm.astype(BF16), doh, TN))
            dqs.append((_dot(dsb, kh) + _dot(dsmb, kmh)) * scale)
            dks.append(_dg(dsb, qh, TN) * scale)
            dkms.append(_dg(dsmb, qh, TN) * scale)
        dq_ref[pl.ds(q0, QB), :] = jnp.concatenate(dqs, axis=1)
        dk_ref[pl.ds(k0, KB), :] += jnp.concatenate(dks, axis=1)
        dv_ref[pl.ds(k0, KB), :] += jnp.concatenate(dvs, axis=1)
        dk_ref[0:N_META, :] += jnp.concatenate(dkms, axis=1)
        dv_ref[0:N_META, :] += jnp.concatenate(dvms, axis=1)

        def add_bias_grad(pat):
            for hh in range(2):
                for i in range(Q_ROWS):
                    for jj in range(K_ROWS):
                        if pat[i][jj] >= 0:
                            dtb_ref[hh, pat[i][jj]] += dss[hh][i * GRID_W:(i + 1) * GRID_W,
                                                               jj * GRID_W:(jj + 1) * GRID_W]

        @pl.when(r == 0)
        def _():
            add_bias_grad(pats[0])

        @pl.when((r > 0) & (r < n_steps - 1))
        def _():
            add_bias_grad(pats[1])

        @pl.when(r == n_steps - 1)
        def _():
            add_bias_grad(pats[2])

    def bias_map(hp, r):
        return (jnp.where(r == 0, 0, jnp.where(r == n_steps - 1, 2, 1)), hp, 0, 0)

    col = pl.BlockSpec((tp, 2 * HEAD_DIM), lambda hp, r: (0, hp))
    n_dr = 2 * KH - 1
    return _call(
        body, comm, bounds, (q, k, v, bias, do), name="attn_bwd", grid=(N_HEADS // 2, n_steps),
        in_specs=[col, col, col, pl.BlockSpec((1, 2, QB, KB), bias_map), col],
        out_specs=[col, col, col, pl.BlockSpec((2, n_dr, GRID_W, GRID_W), lambda hp, r: (hp, 0, 0, 0))],
        out_shape=[jax.ShapeDtypeStruct((tp, NA_W), F32)] * 3 +
                  [jax.ShapeDtypeStruct((N_HEADS, n_dr, GRID_W, GRID_W), F32)],
        compiler_params=_cp(("arbitrary", "arbitrary"), 48))


def _repeat_onehot():
    return np.repeat(np.eye(2 * S5_G, dtype=np.float32), S5_H, axis=0)


def _s5_disc_math(lam_re, lam_im, log_dt, b_re, b_im, rep):
    dt = jnp.exp(log_dt)
    ea = jnp.exp(lam_re * dt)
    a_re = ea * jnp.cos(lam_im * dt)
    a_im = ea * jnp.sin(lam_im * dt)
    den = lam_re * lam_re + lam_im * lam_im
    c_re = ((a_re - 1.0) * lam_re + a_im * lam_im) / den
    c_im = (a_im * lam_re - (a_re - 1.0) * lam_im) / den
    ce_re = jnp.dot(rep, c_re, preferred_element_type=F32, precision=lax.Precision.HIGHEST)
    ce_im = jnp.dot(rep, c_im, preferred_element_type=F32, precision=lax.Precision.HIGHEST)
    return a_re, a_im, ce_re * b_re - ce_im * b_im, ce_re * b_im + ce_im * b_re


def _s5_blocks():
    gl = S5_G // N_BUNDLE
    half = gl * S5_P
    out = []
    for d in range(2):
        for g in range(S5_G):
            b, k = divmod(g, gl)
            dg = d * S5_G + g
            out.append((d, b, slice(k * S5_H, (k + 1) * S5_H), slice(k * S5_P, (k + 1) * S5_P),
                        slice(half + k * S5_P, half + (k + 1) * S5_P), slice(dg * S5_H, (dg + 1) * S5_H),
                        slice(dg, dg + 1)))
    return out


def _s5_params(lam_re, lam_im, log_dt, b_re, b_im, c_re, c_im):
    cw, sw = S5_W // N_BUNDLE, 2 * (S5_G // N_BUNDLE) * S5_P

    def body(lr, li, ld, br, bi, cr, ci, rep_ref, a_ref, bm_ref, cm_ref):
        a_re, a_im, bb_re, bb_im = _s5_disc_math(lr[...], li[...], ld[...], br[...], bi[...], rep_ref[...])
        cc_re = cr[...]
        cc_im = ci[...]
        bm_ref[...] = jnp.zeros_like(bm_ref)
        cm_ref[...] = jnp.zeros_like(cm_ref)
        for d, b, rows, re, im, nat, one in _s5_blocks():
            bm_ref[d, b, rows, re] = bb_re[nat, :].astype(BF16)
            bm_ref[d, b, rows, im] = bb_im[nat, :].astype(BF16)
            cm_ref[d, b, rows, re] = cc_re[nat, :].astype(BF16)
            cm_ref[d, b, rows, im] = (-cc_im[nat, :]).astype(BF16)
            a_ref[d, b, :, re] = a_re[one, :]
            a_ref[d, b, :, im] = a_im[one, :]

    vm = pl.BlockSpec(memory_space=pltpu.VMEM)
    return pl.pallas_call(
        body, name="s5_params", in_specs=[vm] * 8, out_specs=[vm] * 3,
        out_shape=[jax.ShapeDtypeStruct((2, N_BUNDLE, 1, sw), F32), jax.ShapeDtypeStruct((2, N_BUNDLE, cw, sw), BF16),
                   jax.ShapeDtypeStruct((2, N_BUNDLE, cw, sw), BF16)],
    )(lam_re, lam_im, log_dt, b_re, b_im, c_re, c_im, jnp.asarray(_repeat_onehot()))


def _s5_params_bwd(lam_re, lam_im, log_dt, b_re, b_im, da, dbm, dcm):
    n, nb = 2 * S5_G, 2 * S5_G * S5_H

    def body(lr, li, ld, br, bi, rep_ref, da_ref, dbm_ref, dcm_ref, o_lr, o_li, o_ld, o_br, o_bi, o_cr, o_ci,
             dar_s, dai_s, dbr_s, dbi_s):
        for d, b, rows, re, im, nat, one in _s5_blocks():
            dbr_s[nat, :] = dbm_ref[d, b, rows, re]
            dbi_s[nat, :] = dbm_ref[d, b, rows, im]
            o_cr[nat, :] = dcm_ref[d, b, rows, re]
            o_ci[nat, :] = -dcm_ref[d, b, rows, im]
            dar_s[one, :] = da_ref[d, b, :, re]
            dai_s[one, :] = da_ref[d, b, :, im]
        rep = rep_ref[...]
        _, vjp = jax.vjp(lambda p, q, r, s, t: _s5_disc_math(p, q, r, s, t, rep),
                         lr[...], li[...], ld[...], br[...], bi[...])
        o_lr[...], o_li[...], o_ld[...], o_br[...], o_bi[...] = vjp((dar_s[...], dai_s[...], dbr_s[...], dbi_s[...]))

    vm = pl.BlockSpec(memory_space=pltpu.VMEM)
    return pl.pallas_call(
        body, name="s5_params_bwd", in_specs=[vm] * 9, out_specs=[vm] * 7,
        out_shape=[jax.ShapeDtypeStruct((n, S5_P), F32)] * 2 + [jax.ShapeDtypeStruct((n, 1), F32)] +
                  [jax.ShapeDtypeStruct((nb, S5_P), F32)] * 4,
        scratch_shapes=[pltpu.VMEM((n, S5_P), F32)] * 2 + [pltpu.VMEM((nb, S5_P), F32)] * 2,
    )(lam_re, lam_im, log_dt, b_re, b_im, jnp.asarray(_repeat_onehot()), da, dbm, dcm)


def _scan_chunks(length):
    return [(t0, min(SCAN_CHUNK, length - t0)) for t0 in range(0, length, SCAN_CHUNK)]


def _scan(src_ref, dst_ref, prev_ref, prev_off, n_rows, a_re, a_im, carry, reverse):
    half = a_re.shape[-1]
    n_blk = n_rows // 8
    rid = lax.broadcasted_iota(jnp.int32, (8, half), 0)

    def blk(i, carry):
        xr, xi = carry
        bi = (n_blk - 1 - i) if reverse else i
        off = pl.multiple_of(bi * 8, 8)
        v = src_ref[pl.ds(off, 8), :]
        o_r = jnp.zeros((8, half), F32)
        o_i = jnp.zeros((8, half), F32)
        p_r = jnp.zeros((8, half), F32)
        p_i = jnp.zeros((8, half), F32)
        for j in (range(7, -1, -1) if reverse else range(8)):
            if prev_ref is not None:
                p_r = jnp.where(rid == j, xr, p_r)
                p_i = jnp.where(rid == j, xi, p_i)
            nr = a_re * xr - a_im * xi + v[j:j + 1, :half]
            ni = a_re * xi + a_im * xr + v[j:j + 1, half:]
            xr, xi = nr, ni
            if dst_ref is not None:
                o_r = jnp.where(rid == j, xr, o_r)
                o_i = jnp.where(rid == j, xi, o_i)
        if dst_ref is not None:
            dst_ref[pl.ds(off, 8), :] = jnp.concatenate([o_r, o_i], axis=1)
        if prev_ref is not None:
            prev_ref[pl.ds(pl.multiple_of(prev_off + off, 8), 8), :] = jnp.concatenate([p_r, p_i], axis=1)
        return xr, xi

    return lax.fori_loop(0, n_blk, blk, carry)


def _s5_fwd(u, d_skip, a, bm, cm, length, comm=None, bounds=()):
    tp = u.shape[0]
    cw = S5_W // N_BUNDLE
    sw = a.shape[-1]
    half = sw // 2
    chunks = _scan_chunks(length)

    def body(u_ref, d_ref, a_ref, bm_ref, cm_ref, y_ref, bu_s, xs_s):
        y_ref[...] = u_ref[...] * d_ref[...]
        for dr in range(2):
            a_re = a_ref[dr, 0, :, 0:half]
            a_im = a_ref[dr, 0, :, half:]
            carry = (jnp.zeros((1, half), F32), jnp.zeros((1, half), F32))
            for t0, n in (chunks if dr == 0 else chunks[::-1]):
                bu_s[0:n, :] = _dot(u_ref[t0:t0 + n, :].astype(BF16), bm_ref[dr, 0])
                carry = _scan(bu_s, xs_s, None, 0, n, a_re, a_im, carry, dr == 1)
                y_ref[t0:t0 + n, :] += _dg(xs_s[0:n, :].astype(BF16), cm_ref[dr, 0], NT)

    return _call(
        body, comm, bounds, (u, d_skip, a, bm, cm), name="s5_fwd", grid=(N_BUNDLE,),
        in_specs=[pl.BlockSpec((tp, cw), lambda b: (0, b)), pl.BlockSpec((1, cw), lambda b: (0, b)),
                  pl.BlockSpec((2, 1, 1, sw), lambda b: (0, b, 0, 0)),
                  pl.BlockSpec((2, 1, cw, sw), lambda b: (0, b, 0, 0)),
                  pl.BlockSpec((2, 1, cw, sw), lambda b: (0, b, 0, 0))],
        out_specs=[pl.BlockSpec((tp, cw), lambda b: (0, b))],
        out_shape=[jax.ShapeDtypeStruct((tp, S5_W), F32)],
        scratch_shapes=[pltpu.VMEM((SCAN_CHUNK, sw), F32), pltpu.VMEM((SCAN_CHUNK, sw), F32)],
        compiler_params=_cp(("arbitrary",), 40))


def _s5_bwd(u, dy, d_skip, a, bm, cm, length):
    tp = u.shape[0]
    cw = S5_W // N_BUNDLE
    sw = a.shape[-1]
    half = sw // 2
    chunks = _scan_chunks(length)

    def body(u_ref, dy_ref, d_ref, a_ref, bm_ref, cm_ref, du_ref, dd_ref, dbm_ref, dcm_ref, da_ref, bu_s, g_s, xp_s):
        du_ref[...] = dy_ref[...] * d_ref[...]
        dd_ref[...] = jnp.sum(dy_ref[...] * u_ref[...], axis=0, keepdims=True)
        dbm_ref[...] = jnp.zeros_like(dbm_ref)
        dcm_ref[...] = jnp.zeros_like(dcm_ref)
        zero = (jnp.zeros((1, half), F32), jnp.zeros((1, half), F32))
        for dr in range(2):
            a_re = a_ref[dr, 0, :, 0:half]
            a_im = a_ref[dr, 0, :, half:]
            seq = chunks if dr == 0 else chunks[::-1]
            carry = zero
            for t0, n in seq:
                bu_s[0:n, :] = _dot(u_ref[t0:t0 + n, :].astype(BF16), bm_ref[dr, 0])
                carry = _scan(bu_s, None, xp_s, t0, n, a_re, a_im, carry, dr == 1)
            carry = zero
            da_r = jnp.zeros((1, half), F32)
            da_i = jnp.zeros((1, half), F32)
            for t0, n in seq[::-1]:
                ub = u_ref[t0:t0 + n, :].astype(BF16)
                dyb = dy_ref[t0:t0 + n, :].astype(BF16)
                bu_s[0:n, :] = _dot(dyb, cm_ref[dr, 0])
                carry = _scan(bu_s, g_s, None, 0, n, a_re, -a_im, carry, dr == 0)
                g = g_s[0:n, :]
                gb = g.astype(BF16)
                du_ref[t0:t0 + n, :] += _dg(gb, bm_ref[dr, 0], NT)
                dbm_ref[dr, 0] += _dg(ub, gb, TN)
                xp = xp_s[t0:t0 + n, :]
                xp_r, xp_i = xp[:, 0:half], xp[:, half:]
                g_r, g_i = g[:, 0:half], g[:, half:]
                bu = _dot(ub, bm_ref[dr, 0])
                x_r = a_re * xp_r - a_im * xp_i + bu[:, 0:half]
                x_i = a_re * xp_i + a_im * xp_r + bu[:, half:]
                dcm_ref[dr, 0] += _dg(dyb, jnp.concatenate([x_r, x_i], axis=1).astype(BF16), TN)
                da_r = da_r + jnp.sum(g_r * xp_r + g_i * xp_i, axis=0, keepdims=True)
                da_i = da_i + jnp.sum(g_i * xp_r - g_r * xp_i, axis=0, keepdims=True)
            da_ref[dr, 0] = jnp.concatenate([da_r, da_i], axis=1)

    lp = -(-length // 8) * 8
    return pl.pallas_call(
        body, name="s5_bwd", grid=(N_BUNDLE,),
        in_specs=[pl.BlockSpec((tp, cw), lambda b: (0, b)), pl.BlockSpec((tp, cw), lambda b: (0, b)),
                  pl.BlockSpec((1, cw), lambda b: (0, b)),
                  pl.BlockSpec((2, 1, 1, sw), lambda b: (0, b, 0, 0)),
                  pl.BlockSpec((2, 1, cw, sw), lambda b: (0, b, 0, 0)),
                  pl.BlockSpec((2, 1, cw, sw), lambda b: (0, b, 0, 0))],
        out_specs=[pl.BlockSpec((tp, cw), lambda b: (0, b)), pl.BlockSpec((1, cw), lambda b: (0, b)),
                   pl.BlockSpec((2, 1, cw, sw), lambda b: (0, b, 0, 0)),
                   pl.BlockSpec((2, 1, cw, sw), lambda b: (0, b, 0, 0)),
                   pl.BlockSpec((2, 1, 1, sw), lambda b: (0, b, 0, 0))],
        out_shape=[jax.ShapeDtypeStruct((tp, S5_W), F32), jax.ShapeDtypeStruct((1, S5_W), F32),
                   jax.ShapeDtypeStruct((2, N_BUNDLE, cw, sw), F32), jax.ShapeDtypeStruct((2, N_BUNDLE, cw, sw), F32),
                   jax.ShapeDtypeStruct((2, N_BUNDLE, 1, sw), F32)],
        scratch_shapes=[pltpu.VMEM((SCAN_CHUNK, sw), F32), pltpu.VMEM((SCAN_CHUNK, sw), F32),
                        pltpu.VMEM((lp, sw), F32)],
        compiler_params=_cp(("arbitrary",), 48),
    )(u, dy, d_skip, a, bm, cm)


def _row_tile(tp):
    return max(tm for tm in range(16, 449, 16) if tp % tm == 0)


def _step(x, target, bufs, gains, s5, rpb, c_arr, kc_arr):
    first = ["ffn1_w_gate", "ffn1_w_up", "ffn1_w_down", "meta_tokens"]
    w = dict(zip(first, _run_comm("gather_ffn1", _gather_comm([bufs[n] for n in first]))))
    meta = w["meta_tokens"].transpose(1, 0, 2).reshape(N_META, D)
    n_tok = x.shape[0]
    length = N_META + n_tok
    tp = length + 16
    tm = _row_tile(tp)
    tmb = tm
    n_rows = n_tok // GRID_W
    pad = jnp.zeros((tp - length, D), F32)
    h0 = jnp.concatenate([meta, x, pad], axis=0)
    tgt = jnp.concatenate([jnp.zeros((N_META, D), F32), target, pad], axis=0)

    s5p = (s5["lam_re"], s5["lam_im"], s5["log_dt"].reshape(2 * S5_G, 1), s5["b_re"], s5["b_im"])
    a_m, bm16, cm16 = _s5_params(*s5p, s5["c_re"], s5["c_im"])
    bias = _bias_tables(rpb, n_rows)

    mid = ["w_in", "s5_w_glu", "w_out"]
    (h1, gate1, up1, f1), got = _ffn_fwd(
        "ffn1_fwd", h0, gains["ffn1_pre_g"], gains["ffn1_post_g"], w["ffn1_w_gate"], w["ffn1_w_up"], w["ffn1_w_down"],
        tm, _gather_comm([bufs[n] for n in mid]), (0, (tp // tm) * N_CHIP * 3 // 5))
    w.update(zip(mid, got))
    q, k, v, u = _mix_in(h1, gains["mix_pre_g"], w["w_in"], tm)
    (o_na,), (gate_ici,) = _attn_fwd(q, k, v, bias, n_tok, _gather_comm([bufs["ffn2_w_gate"]], pair=False), (0,))
    (y_pre,), (w["ffn2_w_gate"], up_ici, down_ici) = _s5_fwd(
        u, gains["s5_d"], a_m, bm16, cm16, length,
        _merge_comm(_gather_comm([gate_ici], ici=False),
                    _gather_comm([bufs["ffn2_w_up"], bufs["ffn2_w_down"]], pair=False)), (0,))
    w_glu = w["s5_w_glu"].reshape(S5_W, S5_W)
    w_out = w["w_out"].reshape(D, D)
    (h2, mix), (w["ffn2_w_up"], w["ffn2_w_down"]) = _mix_out(
        o_na, y_pre, h1, w_glu, gains["s5_b_glu"], gains["na_out_g"], gains["s5_out_g"], w_out, gains["mix_post_g"], tm,
        _gather_comm([up_ici, down_ici], ici=False), (0,))
    (h3, gate2, up2, f2), _ = _ffn_fwd("ffn2_fwd", h2, gains["ffn2_pre_g"], gains["ffn2_post_g"],
                                       w["ffn2_w_gate"], w["ffn2_w_up"], w["ffn2_w_down"], tm)
    dh3, df2, loss, dg_final, dg_post2 = _final_loss(h3, gains["final_g"], tgt, f2, gains["ffn2_post_g"], n_tok, tm)

    ffn2 = ["ffn2_w_gate", "ffn2_w_up", "ffn2_w_down"]
    ffn1 = ["ffn1_w_gate", "ffn1_w_up", "ffn1_w_down"]
    out2 = _ffn_bwd("ffn2_bwd", h2, gains["ffn2_pre_g"], df2, gate2, up2,
                    w["ffn2_w_gate"], w["ffn2_w_up"], w["ffn2_w_down"], tmb)
    dxn2 = out2[3]
    sums2 = [_chip_sum("chip_sum_" + n, g, r, c_arr) for n, g, r in zip(ffn2, out2[0:3], out2[4:7])]
    (dh2, dg_pre2), _ = _ffn_pre_bwd("ffn2_pre_bwd", dh3, dxn2, h2, gains["ffn2_pre_g"], tm)
    do_na, dy_pre, dw_out, dw_glu, dg_mpost, dg_na, dg_s5, db_glu = _mix_out_bwd(
        dh2, mix, o_na, y_pre, w_glu, gains["s5_b_glu"], gains["na_out_g"], gains["s5_out_g"], w_out,
        gains["mix_post_g"], tm)
    (dq, dk, dv, dtb), recv3 = _attn_bwd(q, k, v, bias, do_na, n_tok, _scatter_comm(sums2), (0,))
    totals2 = [_total_sum("total_sum_" + n, s, r, kc_arr) for n, s, r in zip(ffn2, sums2, recv3)]
    du, dd, dbm, dcm, da_m = _s5_bwd(u, dy_pre, gains["s5_d"], a_m, bm16, cm16, length)
    (dh1, df1, dw_in, dg_mpre, dg_post1), done2 = _mix_in_bwd(
        dq, dk, dv, du, h1, gains["mix_pre_g"], w["w_in"], dh2, f1, gains["ffn1_post_g"], tm,
        _assemble_comm(totals2), (0,))
    pieces = dict(zip(ffn2, done2))
    out1 = _ffn_bwd("ffn1_bwd", h0, gains["ffn1_pre_g"], df1, gate1, up1,
                    w["ffn1_w_gate"], w["ffn1_w_up"], w["ffn1_w_down"], tmb)
    rest = [dw_in, dw_glu.reshape(N_CHIP, S5_W // N_CHIP, S5_W), dw_out.reshape(N_CHIP, D // N_CHIP, D)]
    (dh0, dg_pre1), recv_rest = _ffn_pre_bwd("ffn1_pre_bwd", dh1, out1[3], h0, gains["ffn1_pre_g"], tm,
                                             _exchange_comm(rest), (0,))
    last = ffn1 + mid
    sums = [_chip_sum("chip_sum_" + n, g, r, c_arr)
            for n, g, r in zip(last, list(out1[0:3]) + rest, list(out1[4:7]) + list(recv_rest))]
    recv3 = _run_comm("grad_chip_scatter", _scatter_comm(sums))
    totals = [_total_sum("total_sum_" + n, s, r, kc_arr) for n, s, r in zip(last, sums, recv3)]
    pieces.update(zip(last, _run_comm("grad_pair_assemble", _assemble_comm(totals))))

    e, _ = _diag_onehot()
    n_dr = 2 * KH - 1
    drpb = _rpb_collapse(dtb.reshape(N_HEADS * n_dr, GRID_W * GRID_W), jnp.asarray(e.T))
    drpb = drpb[:, :2 * KW - 1].reshape(N_HEADS, n_dr, 2 * KW - 1).transpose(1, 0, 2).reshape(N_HEADS * n_dr, 2 * KW - 1)
    dlam_re, dlam_im, dlog_dt, db_re, db_im, dc_re, dc_im = _s5_params_bwd(*s5p, da_m, dbm, dcm)

    small = {"ffn1_pre_g": dg_pre1, "ffn1_post_g": dg_post1, "mix_pre_g": dg_mpre, "na_rpb": drpb,
             "s5_lam_re": dlam_re, "s5_lam_im": dlam_im, "s5_log_dt": dlog_dt.reshape(2, S5_G),
             "s5_b_re": db_re, "s5_b_im": db_im, "s5_c_re": dc_re, "s5_c_im": dc_im,
             "s5_d": dd, "s5_b_glu": db_glu, "na_out_g": dg_na,
             "s5_out_g": dg_s5, "mix_post_g": dg_mpost, "ffn2_pre_g": dg_pre2, "ffn2_post_g": dg_post2,
             "final_g": dg_final}
    return loss[0, 0], dh0, pieces, small


def _mesh_pos():
    return lax.axis_index("x"), lax.axis_index("y"), lax.axis_index("c")


def _other_chips(x, y):
    return [(1 - x, y), (x, 1 - y), (1 - x, 1 - y)]


class _Comm:
    def __init__(self, ins, out_shape, aliases, parts):
        self.ins, self.out_shape, self.aliases, self.parts = list(ins), list(out_shape), dict(aliases), list(parts)
        self.n_sems = sum(p[0] for p in parts)

    def bases(self):
        out, base = [], 0
        for n_sems, _, _ in self.parts:
            out.append(base)
            base += n_sems
        return out


def _run_comm(name, comm):
    n_i, n_o = len(comm.ins), len(comm.out_shape)

    def body(*refs):
        ins, outs = refs[:n_i], refs[n_i:n_i + n_o]
        send_sems, recv_sems = refs[n_i + n_o:]
        for base, (_, start, finish) in zip(comm.bases(), comm.parts):
            start(ins, outs, send_sems, recv_sems, base)
            finish(ins, outs, send_sems, recv_sems, base)

    return pl.pallas_call(
        body, name=name, out_shape=comm.out_shape, in_specs=[ANY] * n_i, out_specs=[ANY] * n_o,
        input_output_aliases=comm.aliases,
        scratch_shapes=[pltpu.SemaphoreType.DMA((comm.n_sems,)), pltpu.SemaphoreType.DMA((comm.n_sems,))],
    )(*comm.ins)


def _call(body, comm, bounds, args, *, name, grid, in_specs, out_specs, out_shape, scratch_shapes=(),
          compiler_params=None):
    in_specs, out_specs, out_shape, scratch_shapes = list(in_specs), list(out_specs), list(out_shape), list(scratch_shapes)
    if comm is None:
        return pl.pallas_call(body, name=name, grid=grid, in_specs=in_specs, out_specs=out_specs, out_shape=out_shape,
                              scratch_shapes=scratch_shapes, compiler_params=compiler_params)(*args), []
    n_in, n_out, n_scr = len(in_specs), len(out_specs), len(scratch_shapes)
    n_ci, n_co = len(comm.ins), len(comm.out_shape)
    n_steps = int(np.prod(grid))
    assert len(bounds) == len(comm.parts) and all(0 <= b < n_steps for b in bounds) and list(bounds) == sorted(bounds)

    def fused(*refs):
        a = n_in
        b = a + n_ci
        c = b + n_out
        d = c + n_co
        e = d + n_scr
        cargs = (refs[a:b], refs[c:d], refs[e], refs[e + 1])
        step = pl.program_id(0)
        for ax in range(1, len(grid)):
            step = step * grid[ax] + pl.program_id(ax)
        bases = comm.bases()
        for p, (_, start, finish) in enumerate(comm.parts):
            @pl.when(step == bounds[p])
            def _(p=p, start=start):
                if p > 0:
                    comm.parts[p - 1][2](*cargs, bases[p - 1])
                start(*cargs, bases[p])
        body(*(refs[:a] + refs[b:c] + refs[d:e]))

        @pl.when(step == n_steps - 1)
        def _():
            comm.parts[-1][2](*cargs, bases[-1])

    res = pl.pallas_call(
        fused, name=name, grid=grid, in_specs=in_specs + [ANY] * n_ci, out_specs=out_specs + [ANY] * n_co,
        out_shape=out_shape + comm.out_shape,
        scratch_shapes=scratch_shapes + [pltpu.SemaphoreType.DMA((comm.n_sems,)), pltpu.SemaphoreType.DMA((comm.n_sems,))],
        input_output_aliases={n_in + i: n_out + j for i, j in comm.aliases.items()},
        compiler_params=compiler_params)(*args, *comm.ins)
    return res[:n_out], res[n_out:]


def _remote(src, dst, send_sems, recv_sems, idx, to):
    return pltpu.make_async_remote_copy(src_ref=src, dst_ref=dst, send_sem=send_sems.at[idx],
                                        recv_sem=recv_sems.at[idx], device_id=to, device_id_type=MESH_ID)


def _gather_comm(bufs, ici=True, pair=True):
    n = len(bufs)

    def half(ref, k, pc):
        rh = ref.shape[1] // 2
        return ref.at[k, pl.ds(pc * rh, rh), :]

    def ici_start(ins, outs, ss, rs, base):
        x, y, c = _mesh_pos()
        for a in range(n):
            mine = half(outs[a], 2 * x + y, c)
            for j, chip in enumerate(_other_chips(x, y)):
                _remote(mine, mine, ss, rs, base + 3 * a + j, (*chip, c)).start()

    def ici_finish(ins, outs, ss, rs, base):
        x, y, c = _mesh_pos()
        for a in range(n):
            for j, chip in enumerate(_other_chips(x, y)):
                theirs = half(outs[a], 2 * chip[0] + chip[1], c)
                _remote(theirs, theirs, ss, rs, base + 3 * a + j, (*chip, c)).wait()

    def pair_copy(outs, ss, rs, base, a):
        x, y, c = _mesh_pos()
        rh = outs[a].shape[1] // 2
        held = outs[a].at[:, pl.ds(c * rh, rh), :]
        return _remote(held, held, ss, rs, base + a, (x, y, 1 - c))

    def pair_start(ins, outs, ss, rs, base):
        for a in range(n):
            pair_copy(outs, ss, rs, base, a).start()

    def pair_finish(ins, outs, ss, rs, base):
        for a in range(n):
            pair_copy(outs, ss, rs, base, a).wait()

    parts = ([(3 * n, ici_start, ici_finish)] if ici else []) + ([(n, pair_start, pair_finish)] if pair else [])
    return _Comm(bufs, [jax.ShapeDtypeStruct(b.shape, b.dtype) for b in bufs], {a: a for a in range(n)}, parts)


def _merge_comm(*comms):
    ins, shapes, aliases, subs, base = [], [], {}, [], 0
    for cm in comms:
        (n_sems, start, finish), = cm.parts
        i0, o0 = len(ins), len(shapes)
        subs.append((slice(i0, i0 + len(cm.ins)), slice(o0, o0 + len(cm.out_shape)), base, start, finish))
        aliases.update({i0 + i: o0 + j for i, j in cm.aliases.items()})
        ins += cm.ins
        shapes += cm.out_shape
        base += n_sems

    def start_all(ins_r, outs_r, ss, rs, b):
        for si, so, off, start, _ in subs:
            start(ins_r[si], outs_r[so], ss, rs, b + off)

    def finish_all(ins_r, outs_r, ss, rs, b):
        for si, so, off, _, finish in subs:
            finish(ins_r[si], outs_r[so], ss, rs, b + off)

    return _Comm(ins, shapes, aliases, [(base, start_all, finish_all)])


def _own_half_buffers(pieces, dtypes, kc_arr):
    n = len(pieces)

    def body(kc_ref, *refs):
        for a in range(n):
            refs[n + a][0] = refs[a][...].astype(dtypes[a])

    def half(p):
        return p.shape[0] // 2, p.shape[1]

    return pl.pallas_call(
        body, name="own_halves",
        out_shape=[jax.ShapeDtypeStruct((N_CHIP,) + p.shape, dt) for p, dt in zip(pieces, dtypes)],
        grid_spec=pltpu.PrefetchScalarGridSpec(
            num_scalar_prefetch=1, grid=(1,),
            in_specs=[pl.BlockSpec(half(p), lambda i, kc: (kc[1], 0)) for p in pieces],
            out_specs=[pl.BlockSpec((1,) + half(p), lambda i, kc: (kc[0], kc[1], 0)) for p in pieces]),
        compiler_params=_cp(("arbitrary",), 48),
    )(kc_arr, *pieces)


def _exchange_comm(grads):
    n = len(grads)

    def copy(ins, outs, ss, rs, base, a):
        x, y, c = _mesh_pos()
        rh = ins[a].shape[1] // 2
        return _remote(ins[a].at[:, pl.ds((1 - c) * rh, rh), :], outs[a], ss, rs, base + a, (x, y, 1 - c))

    def start(ins, outs, ss, rs, base):
        for a in range(n):
            copy(ins, outs, ss, rs, base, a).start()

    def finish(ins, outs, ss, rs, base):
        for a in range(n):
            copy(ins, outs, ss, rs, base, a).wait()

    shapes = [jax.ShapeDtypeStruct((N_CHIP, g.shape[1] // 2, g.shape[2]), g.dtype) for g in grads]
    return _Comm(grads, shapes, {}, [(n, start, finish)])


def _chip_sum(name, g, recv, c_arr):
    _, r, cc = g.shape
    rh = r // 2

    def body(c_ref, g_ref, r_ref, o_ref):
        o_ref[...] = (g_ref[...] + r_ref[...]).astype(BF16)

    return pl.pallas_call(
        body, name=name, out_shape=jax.ShapeDtypeStruct((N_CHIP, rh, cc), BF16),
        grid_spec=pltpu.PrefetchScalarGridSpec(
            num_scalar_prefetch=1, grid=(N_CHIP,),
            in_specs=[pl.BlockSpec((1, rh, cc), lambda j, c_ref: (j, c_ref[0], 0)),
                      pl.BlockSpec((1, rh, cc), lambda j, c_ref: (j, 0, 0))],
            out_specs=pl.BlockSpec((1, rh, cc), lambda j, c_ref: (j, 0, 0))),
        compiler_params=_cp(("arbitrary",), 32),
    )(c_arr, g, recv)


def _scatter_comm(sums):
    n = len(sums)

    def copies(ins, outs, ss, rs, base):
        x, y, c = _mesh_pos()
        return [_remote(ins[a].at[2 * chip[0] + chip[1]], outs[a].at[j], ss, rs, base + 3 * a + j, (*chip, c))
                for a in range(n) for j, chip in enumerate(_other_chips(x, y))]

    def start(ins, outs, ss, rs, base):
        for cp in copies(ins, outs, ss, rs, base):
            cp.start()

    def finish(ins, outs, ss, rs, base):
        for cp in copies(ins, outs, ss, rs, base):
            cp.wait()

    shapes = [jax.ShapeDtypeStruct((3,) + s.shape[1:], s.dtype) for s in sums]
    return _Comm(sums, shapes, {}, [(3 * n, start, finish)])


def _total_sum(name, sums, recv3, kc_arr):
    _, rh, cc = sums.shape

    def body(kc_ref, s_ref, r_ref, o_ref):
        t = s_ref[0].astype(F32) + r_ref[0].astype(F32)
        t = t + r_ref[1].astype(F32)
        o_ref[...] = t + r_ref[2].astype(F32)

    return pl.pallas_call(
        body, name=name, out_shape=jax.ShapeDtypeStruct((2 * rh, cc), F32),
        grid_spec=pltpu.PrefetchScalarGridSpec(
            num_scalar_prefetch=1, grid=(1,),
            in_specs=[pl.BlockSpec((1, rh, cc), lambda i, kc_ref: (kc_ref[0], 0, 0)),
                      pl.BlockSpec((3, rh, cc), lambda i, kc_ref: (0, 0, 0))],
            out_specs=pl.BlockSpec((rh, cc), lambda i, kc_ref: (kc_ref[1], 0))),
        compiler_params=_cp(("arbitrary",), 32),
    )(kc_arr, sums, recv3)


def _assemble_comm(totals):
    n = len(totals)

    def copy(outs, ss, rs, base, a):
        x, y, c = _mesh_pos()
        rh = outs[a].shape[0] // 2
        here = outs[a].at[pl.ds(c * rh, rh), :]
        return _remote(here, here, ss, rs, base + a, (x, y, 1 - c))

    def start(ins, outs, ss, rs, base):
        for a in range(n):
            copy(outs, ss, rs, base, a).start()

    def finish(ins, outs, ss, rs, base):
        for a in range(n):
            copy(outs, ss, rs, base, a).wait()

    shapes = [jax.ShapeDtypeStruct(t.shape, t.dtype) for t in totals]
    return _Comm(totals, shapes, {a: a for a in range(n)}, [(n, start, finish)])


def _small_allreduce(arrays):
    n = len(arrays)
    shapes = [a.shape for a in arrays]
    narrow_w = 64
    groups = [[a for a in range(n) if shapes[a][1] > narrow_w], [a for a in range(n) if shapes[a][1] <= narrow_w]]
    widths = [max(shapes[a][1] for a in groups[0]), 2 * narrow_w]
    offs, cols, heights = {}, {}, [0, 0]
    for a in groups[0]:
        offs[a], cols[a] = heights[0], 0
        heights[0] += shapes[a][0]
    rows = [-(-heights[0] // 8) * 8]
    heights = [0, 0]
    for a in sorted(groups[1], key=lambda a: -shapes[a][0]):
        side = 0 if heights[0] <= heights[1] else 1
        offs[a], cols[a] = heights[side], side * narrow_w
        heights[side] += shapes[a][0]
    rows.append(-(-max(heights) // 8) * 8)
    n_g = len(groups)

    def window(ref, a):
        return ref.at[offs[a]:offs[a] + shapes[a][0], cols[a]:cols[a] + shapes[a][1]]

    def body(*refs):
        ins, outs = refs[:n], refs[n:2 * n]
        pack, sib, csum, every = (refs[2 * n + i * n_g:2 * n + (i + 1) * n_g] for i in range(4))
        send_sems, recv_sems = refs[2 * n + 4 * n_g:]
        x, y, c = _mesh_pos()
        k = 2 * x + y
        for gi, g in enumerate(groups):
            pack[gi][...] = jnp.zeros_like(pack[gi])
            for a in g:
                window(pack[gi], a)[...] = ins[a][...]
        cps = [_remote(pack[gi], sib[gi], send_sems, recv_sems, gi, (x, y, 1 - c)) for gi in range(n_g)]
        for cp in cps:
            cp.start()
        for cp in cps:
            cp.wait()
        for gi in range(n_g):
            csum[gi][...] = pack[gi][...] + sib[gi][...]
            every[gi][k] = csum[gi][...]
        cps = [_remote(csum[gi], every[gi].at[k], send_sems, recv_sems, n_g + 3 * gi + j, (*chip, c))
               for gi in range(n_g) for j, chip in enumerate(_other_chips(x, y))]
        for cp in cps:
            cp.start()
        for cp in cps:
            cp.wait()
        for gi, g in enumerate(groups):
            pack[gi][...] = ((every[gi][0] + every[gi][1]) + every[gi][2]) + every[gi][3]
            for a in g:
                outs[a][...] = window(pack[gi], a)[...]

    vm = pl.BlockSpec(memory_space=pltpu.VMEM)
    bufs = [pltpu.VMEM((r, w), F32) for r, w in zip(rows, widths)]
    return pl.pallas_call(
        body, name="small_allreduce", out_shape=[jax.ShapeDtypeStruct(s, F32) for s in shapes],
        in_specs=[vm] * n, out_specs=[vm] * n,
        scratch_shapes=bufs * 3 + [pltpu.VMEM((N_CHIP, r, w), F32) for r, w in zip(rows, widths)] +
                       [pltpu.SemaphoreType.DMA((4 * n_g,)), pltpu.SemaphoreType.DMA((4 * n_g,))],
        compiler_params=_cp(None, 40),
    )(*arrays)


def _adamw_small(ws, gs, ms, vs):
    n = len(ws)

    def body(*refs):
        w, g, m, v, d, mo, vo = (refs[i * n:(i + 1) * n] for i in range(7))
        for a in range(n):
            d[a][...], mo[a][...], vo[a][...] = _adamw_math(w[a][...], g[a][...], m[a][...], v[a][...])

    vm = pl.BlockSpec(memory_space=pltpu.VMEM)
    res = pl.pallas_call(
        body, name="adamw_small", out_shape=[jax.ShapeDtypeStruct(w.shape, F32) for w in ws] * 3,
        in_specs=[vm] * (4 * n), out_specs=[vm] * (3 * n), compiler_params=_cp(None, 40),
    )(*ws, *gs, *ms, *vs)
    return res[:n], res[n:2 * n], res[2 * n:]


def _adamw_math(w, g, m, v):
    m = ADAM_B1 * m + (1.0 - ADAM_B1) * g
    v = ADAM_B2 * v + (1.0 - ADAM_B2) * (g * g)
    m_hat = m / (1.0 - ADAM_B1 ** ADAM_STEP)
    v_hat = v / (1.0 - ADAM_B2 ** ADAM_STEP)
    delta = -ADAM_LR * (m_hat / (jnp.sqrt(v_hat) + ADAM_EPS) + ADAM_WD * w)
    return delta, m, v


def _adamw(name, w, g, m, v):
    r, c = w.shape
    tr = max(t for t in range(8, 513, 8) if r % t == 0)

    def body(w_ref, g_ref, m_ref, v_ref, d_ref, mo_ref, vo_ref):
        d_ref[...], mo_ref[...], vo_ref[...] = _adamw_math(w_ref[...], g_ref[...], m_ref[...], v_ref[...])

    return pl.pallas_call(
        body, name=name, grid=(r // tr,), in_specs=[_rows(tr, c)] * 4, out_specs=[_rows(tr, c)] * 3,
        out_shape=[jax.ShapeDtypeStruct((r, c), F32)] * 3, compiler_params=_cp(("arbitrary",), 32),
    )(w, g, m, v)


def _as_matrix(name, a):
    if name == "na_rpb":
        return a[0].transpose(1, 0, 2).reshape(N_HEADS * (2 * KH - 1), 2 * KW - 1)
    if name in ("s5_b_re", "s5_b_im"):
        return a.transpose(0, 1, 2, 4, 3).reshape(2 * S5_G * S5_H, S5_P)
    if name in ("s5_c_re", "s5_c_im"):
        return a.reshape(2 * S5_G * S5_H, S5_P)
    if name in ("s5_lam_re", "s5_lam_im"):
        return a.reshape(2 * S5_G, S5_P)
    if name == "s5_log_dt":
        return a.reshape(2, S5_G)
    return a


def _from_matrix(name, m):
    if name == "na_rpb":
        return m.reshape(2 * KH - 1, N_HEADS, 2 * KW - 1).transpose(1, 0, 2)[None]
    if name in ("s5_b_re", "s5_b_im"):
        return m.reshape(1, 2, S5_G, S5_H, S5_P).transpose(0, 1, 2, 4, 3)
    if name in ("s5_c_re", "s5_c_im"):
        return m.reshape(1, 2, S5_G, S5_H, S5_P)
    if name in ("s5_lam_re", "s5_lam_im"):
        return m.reshape(1, 2, S5_G, S5_P)
    if name == "s5_log_dt":
        return m.reshape(1, 2, S5_G)
    return m


WEIGHTS = ["meta_tokens", "ffn1_pre_g", "ffn1_post_g", "ffn1_w_gate", "ffn1_w_up", "ffn1_w_down", "mix_pre_g", "w_in",
           "na_rpb", "s5_lam_re", "s5_lam_im", "s5_log_dt", "s5_b_re", "s5_b_im", "s5_c_re", "s5_c_im", "s5_d",
           "s5_w_glu", "s5_b_glu", "na_out_g", "s5_out_g", "w_out", "mix_post_g", "ffn2_pre_g", "ffn2_post_g",
           "ffn2_w_gate", "ffn2_w_up", "ffn2_w_down", "final_g"]
BIG = ["ffn1_w_gate", "ffn1_w_up", "ffn1_w_down", "w_in", "s5_w_glu", "w_out", "ffn2_w_gate", "ffn2_w_up",
       "ffn2_w_down"]
TRANSPOSED = ["ffn1_w_gate", "ffn1_w_up", "ffn2_w_gate", "ffn2_w_up"]
GAINS = ["ffn1_pre_g", "ffn1_post_g", "mix_pre_g", "s5_d", "s5_b_glu", "na_out_g", "s5_out_g", "mix_post_g",
         "ffn2_pre_g", "ffn2_post_g", "final_g"]
SMALL = [n for n in WEIGHTS if n not in BIG]


def kernel(*args):
    names = ["x"] + WEIGHTS + ["loss_target"] + ["m_" + n for n in WEIGHTS] + ["v_" + n for n in WEIGHTS]
    assert len(args) == len(names)
    given = dict(zip(names, args))
    x_pos, y_pos, c_pos = _mesh_pos()
    k_pos = 2 * x_pos + y_pos
    c_arr = jnp.reshape(c_pos, (1,)).astype(jnp.int32)
    kc_arr = jnp.stack([k_pos, c_pos]).astype(jnp.int32)

    def piece(name, a):
        return a[0].T if name in TRANSPOSED else a[0]

    def unpiece(name, a):
        return a.T[None] if name in TRANSPOSED else a[None]

    placed = BIG + ["meta_tokens"]
    bufs = dict(zip(placed, _own_half_buffers([piece(n, given[n]) for n in BIG] + [given["meta_tokens"]],
                                              [BF16] * len(BIG) + [F32], kc_arr)))

    gains = {n: given[n] for n in GAINS}
    s5 = {n: _as_matrix("s5_" + n, given["s5_" + n])
          for n in ["lam_re", "lam_im", "log_dt", "b_re", "b_im", "c_re", "c_im"]}
    loss, dh0, pieces, small = _step(given["x"][0], given["loss_target"][0], bufs, gains, s5, given["na_rpb"][0],
                                     c_arr, kc_arr)
    loss = lax.psum(loss, ("x", "y", "c"))
    n_tok = given["x"].shape[1]
    grad_x = dh0[N_META:N_META + n_tok][None]

    small["meta_tokens"] = dh0[:N_META]
    small = dict(zip(SMALL, _small_allreduce([small[n] for n in SMALL])))
    mc = D // N_CHIP
    small["meta_tokens"] = lax.dynamic_slice_in_dim(small["meta_tokens"], k_pos * mc, mc, 1)

    out_g, out_d, out_m, out_v = {}, {}, {}, {}
    for n in BIG:
        g2 = pieces[n]
        d2, m2, v2 = _adamw("adamw_" + n, piece(n, given[n]), g2, piece(n, given["m_" + n]),
                            piece(n, given["v_" + n]))
        out_g[n], out_d[n], out_m[n], out_v[n] = (unpiece(n, t) for t in (g2, d2, m2, v2))
    gs = [small[n] for n in SMALL]
    d2, m2, v2 = _adamw_small([_as_matrix(n, given[n]) for n in SMALL], gs,
                              [_as_matrix(n, given["m_" + n]) for n in SMALL],
                              [_as_matrix(n, given["v_" + n]) for n in SMALL])
    for n, g, dd, mm, vv in zip(SMALL, gs, d2, m2, v2):
        out_g[n], out_d[n], out_m[n], out_v[n] = (_from_matrix(n, t) for t in (g, dd, mm, vv))
    return (loss, grad_x, *[out_g[n] for n in WEIGHTS], *[out_d[n] for n in WEIGHTS],
            *[out_m[n] for n in WEIGHTS], *[out_v[n] for n in WEIGHTS])
```

```python
import functools
import math

import numpy as np
import jax
import jax.numpy as jnp
from jax import lax
from jax.experimental import pallas as pl
from jax.experimental.pallas import tpu as pltpu

F32 = jnp.float32
BF16 = jnp.bfloat16

D = 1024
N_META = 16
GRID_W = 64
NA_W = 512
S5_W = 512
HEAD_DIM = 64
N_HEADS = 8
KH = 8
KW = 16
S5_G = 32
S5_P = 64
S5_H = 16
N_BUNDLE = 4
FF = 2816
N_CHIP = 4
FC = FF // N_CHIP
EPS = 1e-6
NEG_INF = -1e30
Q_ROWS = 4
K_ROWS = 12
QB = Q_ROWS * GRID_W
KB = K_ROWS * GRID_W
SCAN_CHUNK = 256

ADAM_LR = 0.001
ADAM_B1 = 0.9
ADAM_B2 = 0.999
ADAM_EPS = 1e-08
ADAM_WD = 0.01
ADAM_STEP = 10

NT = (((1,), (1,)), ((), ()))
TN = (((0,), (0,)), ((), ()))
MESH_ID = pl.DeviceIdType.MESH


def _cp(sem=None, vmem_mb=None):
    kw = {}
    if sem is not None:
        kw["dimension_semantics"] = sem
    if vmem_mb is not None:
        kw["vmem_limit_bytes"] = vmem_mb << 20
    return pltpu.CompilerParams(**kw)


def _full(shape):
    n = len(shape)
    return pl.BlockSpec(shape, lambda *_: (0,) * n)


def _rows(tm, w):
    return pl.BlockSpec((tm, w), lambda i: (i, 0))


ANY = pl.BlockSpec(memory_space=pl.ANY)


def _rms(x, g):
    r = lax.rsqrt(jnp.mean(x * x, axis=-1, keepdims=True) + EPS)
    return x * r * g


def _rms_bwd(x, g, dy):
    r = lax.rsqrt(jnp.mean(x * x, axis=-1, keepdims=True) + EPS)
    xh = x * r
    dg = jnp.sum(dy * xh, axis=0, keepdims=True)
    dyg = dy * g
    dx = r * (dyg - xh * jnp.mean(dyg * xh, axis=-1, keepdims=True))
    return dx, dg


def _out(shape, dtype):
    return pltpu.HBM(tuple(shape), dtype)


def _in_hbm(*args):
    return [pltpu.with_memory_space_constraint(a, pltpu.HBM) if jnp.issubdtype(a.dtype, jnp.floating) else a
            for a in args]


def _dot(a, b):
    return jnp.dot(a, b, preferred_element_type=F32)


def _dg(a, b, dims):
    return lax.dot_general(a, b, dims, preferred_element_type=F32)


def _ffn_fwd(name, h, g_pre, g_post, wg, wu, wd, tm, comm=None, bounds=()):
    tp = h.shape[0]
    nt = tp // tm

    def body(h_ref, gp_ref, gq_ref, wg_ref, wu_ref, wd_ref, hn_ref, gate_ref, up_ref, f_ref, xn_s, acc_s):
        c = pl.program_id(1)

        @pl.when(c == 0)
        def _():
            xn_s[...] = _rms(h_ref[...], gp_ref[...]).astype(BF16)
            acc_s[...] = jnp.zeros_like(acc_s)

        xn = xn_s[...]
        gate = _dg(xn, wg_ref[0], NT)
        up = _dg(xn, wu_ref[0], NT)
        gate_ref[0] = gate
        up_ref[0] = up
        act = (gate * jax.nn.sigmoid(gate) * up).astype(BF16)
        acc_s[...] += _dot(act, wd_ref[0])

        @pl.when(c == N_CHIP - 1)
        def _():
            f = acc_s[...]
            f_ref[...] = f
            hn_ref[...] = h_ref[...] + 0.5 * _rms(f, gq_ref[...])

    return _call(
        body, comm, bounds, (h, g_pre, g_post, wg, wu, wd), name=name, grid=(nt, N_CHIP),
        in_specs=[pl.BlockSpec((tm, D), lambda i, c: (i, 0)), _full((1, D)), _full((1, D))] +
                 [pl.BlockSpec((1, FC, D), lambda i, c: (c, 0, 0))] * 3,
        out_specs=[pl.BlockSpec((tm, D), lambda i, c: (i, 0)),
                   pl.BlockSpec((1, tm, FC), lambda i, c: (c, i, 0)),
                   pl.BlockSpec((1, tm, FC), lambda i, c: (c, i, 0)),
                   pl.BlockSpec((tm, D), lambda i, c: (i, 0))],
        out_shape=[_out((tp, D), F32), _out((N_CHIP, tp, FC), F32),
                   _out((N_CHIP, tp, FC), F32), _out((tp, D), F32)],
        scratch_shapes=[pltpu.VMEM((tm, D), BF16), pltpu.VMEM((tm, D), F32)],
        compiler_params=_cp(("arbitrary", "arbitrary"), 48))


def _ffn_bwd(name, h, g_pre, df, gate, up, wg, wu, wd, tm):
    tp = h.shape[0]
    nt = tp // tm
    rh = FC // 2

    def body(h_ref, gp_ref, df_ref, gate_ref, up_ref, wg_ref, wu_ref, wd_ref,
             dwg_ref, dwu_ref, dwd_ref, dxn_ref, rg_ref, ru_ref, rd_ref, ag, au, ad, send_sems, recv_sems):
        c = pl.program_id(0)
        i = pl.program_id(1)

        def to_sibling(a, piece):
            x, y, core = _mesh_pos()
            dw_ref, r_ref = ((dwg_ref, rg_ref), (dwu_ref, ru_ref), (dwd_ref, rd_ref))[a]
            return _remote(dw_ref.at[piece, pl.ds((1 - core) * rh, rh), :], r_ref.at[piece], send_sems, recv_sems,
                           3 * piece + a, (x, y, 1 - core))

        @pl.when(i == 0)
        def _():
            ag[...] = jnp.zeros_like(ag)
            au[...] = jnp.zeros_like(au)
            ad[...] = jnp.zeros_like(ad)

        xn = _rms(h_ref[...], gp_ref[...]).astype(BF16)
        dfb = df_ref[...].astype(BF16)
        gt = gate_ref[0]
        u = up_ref[0]
        sg = jax.nn.sigmoid(gt)
        si = gt * sg
        act = (si * u).astype(BF16)
        dact = _dg(dfb, wd_ref[0], NT)
        ad[...] += _dg(act, dfb, TN)
        dgate = (dact * u * (sg * (1.0 + gt * (1.0 - sg)))).astype(BF16)
        dup = (dact * si).astype(BF16)
        ag[...] += _dg(dgate, xn, TN)
        au[...] += _dg(dup, xn, TN)
        dxn_ref[0] = _dot(dgate, wg_ref[0]) + _dot(dup, wu_ref[0])

        @pl.when(i == nt - 1)
        def _():
            pltpu.sync_copy(ag, dwg_ref.at[c])
            pltpu.sync_copy(au, dwu_ref.at[c])
            pltpu.sync_copy(ad, dwd_ref.at[c])
            for a in range(3):
                to_sibling(a, c).start()

        @pl.when((c == N_CHIP - 1) & (i == nt - 1))
        def _():
            for piece in range(N_CHIP):
                for a in range(3):
                    to_sibling(a, piece).wait()

    return pl.pallas_call(
        body, name=name, grid=(N_CHIP, nt),
        in_specs=[pl.BlockSpec((tm, D), lambda c, i: (i, 0)), _full((1, D)),
                  pl.BlockSpec((tm, D), lambda c, i: (i, 0)),
                  pl.BlockSpec((1, tm, FC), lambda c, i: (c, i, 0)),
                  pl.BlockSpec((1, tm, FC), lambda c, i: (c, i, 0))] +
                 [pl.BlockSpec((1, FC, D), lambda c, i: (c, 0, 0))] * 3,
        out_specs=[ANY, ANY, ANY, pl.BlockSpec((1, tm, D), lambda c, i: (c, i, 0)), ANY, ANY, ANY],
        out_shape=[_out((N_CHIP, FC, D), F32)] * 3 + [_out((N_CHIP, tp, D), F32)] +
                  [_out((N_CHIP, rh, D), F32)] * 3,
        scratch_shapes=[pltpu.VMEM((FC, D), F32)] * 3 +
                       [pltpu.SemaphoreType.DMA((3 * N_CHIP,)), pltpu.SemaphoreType.DMA((3 * N_CHIP,))],
        compiler_params=_cp(("arbitrary", "arbitrary"), 58),
    )(*_in_hbm(h, g_pre, df, gate, up, wg, wu, wd))


def _ffn_pre_bwd(name, dh, dxn_part, h, g_pre, tm, comm=None, bounds=()):
    tp = h.shape[0]
    nt = tp // tm

    def body(dh_ref, dxn_ref, h_ref, gp_ref, out_ref, dg_ref):
        i = pl.program_id(0)
        dxn = (dxn_ref[0] + dxn_ref[1]) + (dxn_ref[2] + dxn_ref[3])
        dx, dg = _rms_bwd(h_ref[...], gp_ref[...], dxn)
        out_ref[...] = dh_ref[...] + dx

        @pl.when(i == 0)
        def _():
            dg_ref[...] = jnp.zeros_like(dg_ref)

        dg_ref[...] += dg

    return _call(
        body, comm, bounds, (dh, dxn_part, h, g_pre), name=name, grid=(nt,),
        in_specs=[_rows(tm, D), pl.BlockSpec((N_CHIP, tm, D), lambda i: (0, i, 0)), _rows(tm, D), _full((1, D))],
        out_specs=[_rows(tm, D), _full((1, D))],
        out_shape=[_out((tp, D), F32), _out((1, D), F32)],
        compiler_params=_cp(("arbitrary",), 48))


def _mix_in(h, g, w_in, tm):
    tp = h.shape[0]

    def body(h_ref, g_ref, w_ref, q_ref, k_ref, v_ref, u_ref):
        a = _rms(h_ref[...], g_ref[...]).astype(BF16)
        q_ref[...] = _dot(a, w_ref[0]).astype(BF16)
        k_ref[...] = _dot(a, w_ref[1]).astype(BF16)
        v_ref[...] = _dot(a, w_ref[2]).astype(BF16)
        u_ref[...] = _dot(a, w_ref[3])

    return pl.pallas_call(
        body, name="mix_in", grid=(tp // tm,),
        in_specs=[_rows(tm, D), _full((1, D)), _full((N_CHIP, D, NA_W))],
        out_specs=[_rows(tm, NA_W)] * 4,
        out_shape=[_out((tp, NA_W), BF16)] * 3 + [_out((tp, S5_W), F32)],
        compiler_params=_cp(("arbitrary",), 40),
    )(*_in_hbm(h, g, w_in))


def _gelu(x):
    return jax.nn.gelu(x, approximate=True)


def _gelu_grad(x):
    k = math.sqrt(2.0 / math.pi)
    t = jnp.tanh(k * (x + 0.044715 * x * x * x))
    return 0.5 * (1.0 + t) + 0.5 * x * (1.0 - t * t) * k * (1.0 + 3.0 * 0.044715 * x * x)


def _mix_out(o_na, y_pre, h, w_glu, b_glu, g_na, g_s5, w_out, g_post, tm, comm=None, bounds=()):
    tp = h.shape[0]

    def body(ona_ref, yp_ref, h_ref, wglu_ref, bglu_ref, gna_ref, gs5_ref, wout_ref, gpost_ref, hn_ref, mix_ref):
        y = _gelu(yp_ref[...])
        z = _dot(y.astype(BF16), wglu_ref[...]) + bglu_ref[...]
        o_s5 = y * jax.nn.sigmoid(z)
        n1 = _rms(ona_ref[...], gna_ref[...]).astype(BF16)
        n2 = _rms(o_s5, gs5_ref[...]).astype(BF16)
        mix = _dot(n1, wout_ref[0:NA_W, :]) + _dot(n2, wout_ref[NA_W:, :])
        mix_ref[...] = mix
        hn_ref[...] = h_ref[...] + _rms(mix, gpost_ref[...])

    return _call(
        body, comm, bounds, (o_na, y_pre, h, w_glu, b_glu, g_na, g_s5, w_out, g_post), name="mix_out",
        grid=(tp // tm,),
        in_specs=[_rows(tm, NA_W), _rows(tm, S5_W), _rows(tm, D), _full((S5_W, S5_W)), _full((1, S5_W)),
                  _full((1, NA_W)), _full((1, S5_W)), _full((D, D)), _full((1, D))],
        out_specs=[_rows(tm, D), _rows(tm, D)],
        out_shape=[_out((tp, D), F32)] * 2,
        compiler_params=_cp(("arbitrary",), 40))


def _mix_out_bwd(dh, mix, o_na, y_pre, w_glu, b_glu, g_na, g_s5, w_out, g_post, tm):
    tp = dh.shape[0]
    nt = tp // tm

    def body(dh_ref, mix_ref, ona_ref, yp_ref, wglu_ref, bglu_ref, gna_ref, gs5_ref, wout_ref, gpost_ref,
             dona_ref, dyp_ref, dwout_ref, dwglu_ref, dgpost_ref, dgna_ref, dgs5_ref, dbglu_ref, a_out, a_glu):
        i = pl.program_id(0)

        @pl.when(i == 0)
        def _():
            a_out[...] = jnp.zeros_like(a_out)
            a_glu[...] = jnp.zeros_like(a_glu)
            dgpost_ref[...] = jnp.zeros_like(dgpost_ref)
            dgna_ref[...] = jnp.zeros_like(dgna_ref)
            dgs5_ref[...] = jnp.zeros_like(dgs5_ref)
            dbglu_ref[...] = jnp.zeros_like(dbglu_ref)

        dmix, dgpost = _rms_bwd(mix_ref[...], gpost_ref[...], dh_ref[...])
        dgpost_ref[...] += dgpost
        yp = yp_ref[...]
        y = _gelu(yp)
        yb = y.astype(BF16)
        z = _dot(yb, wglu_ref[...]) + bglu_ref[...]
        sg = jax.nn.sigmoid(z)
        o_s5 = y * sg
        o_na = ona_ref[...]
        n1 = _rms(o_na, gna_ref[...]).astype(BF16)
        n2 = _rms(o_s5, gs5_ref[...]).astype(BF16)
        dmb = dmix.astype(BF16)
        a_out[0:NA_W, :] += _dg(n1, dmb, TN)
        a_out[NA_W:, :] += _dg(n2, dmb, TN)
        dn1 = _dg(dmb, wout_ref[0:NA_W, :], NT)
        dn2 = _dg(dmb, wout_ref[NA_W:, :], NT)
        dona, dgna = _rms_bwd(o_na, gna_ref[...], dn1)
        dona_ref[...] = dona
        dgna_ref[...] += dgna
        dos5, dgs5 = _rms_bwd(o_s5, gs5_ref[...], dn2)
        dgs5_ref[...] += dgs5
        dz = dos5 * y * (sg * (1.0 - sg))
        dbglu_ref[...] += jnp.sum(dz, axis=0, keepdims=True)
        dzb = dz.astype(BF16)
        a_glu[...] += _dg(yb, dzb, TN)
        dy = dos5 * sg + _dg(dzb, wglu_ref[...], NT)
        dyp_ref[...] = dy * _gelu_grad(yp)

        @pl.when(i == nt - 1)
        def _():
            pltpu.sync_copy(a_out, dwout_ref)
            pltpu.sync_copy(a_glu, dwglu_ref)

    return pl.pallas_call(
        body, name="mix_out_bwd", grid=(nt,),
        in_specs=[_rows(tm, D), _rows(tm, D), _rows(tm, NA_W), _rows(tm, S5_W), _full((S5_W, S5_W)),
                  _full((1, S5_W)), _full((1, NA_W)), _full((1, S5_W)), _full((D, D)), _full((1, D))],
        out_specs=[_rows(tm, NA_W), _rows(tm, S5_W), ANY, ANY, _full((1, D)), _full((1, NA_W)),
                   _full((1, S5_W)), _full((1, S5_W))],
        out_shape=[_out((tp, NA_W), F32), _out((tp, S5_W), F32),
                   _out((D, D), F32), _out((S5_W, S5_W), F32),
                   _out((1, D), F32), _out((1, NA_W), F32),
                   _out((1, S5_W), F32), _out((1, S5_W), F32)],
        scratch_shapes=[pltpu.VMEM((D, D), F32), pltpu.VMEM((S5_W, S5_W), F32)],
        compiler_params=_cp(("arbitrary",), 48),
    )(*_in_hbm(dh, mix, o_na, y_pre, w_glu, b_glu, g_na, g_s5, w_out, g_post))


def _mix_in_bwd(dq, dk, dv, du, h, g, w_in, dh, f1, g_post1, tm, comm=None, bounds=()):
    tp = h.shape[0]
    nt = tp // tm

    def body(dq_ref, dk_ref, dv_ref, du_ref, h_ref, g_ref, w_ref, dh_ref, f_ref, gq_ref,
             dh1_ref, df_ref, dw_ref, dg_ref, dgq_ref, acc):
        i = pl.program_id(0)

        @pl.when(i == 0)
        def _():
            acc[...] = jnp.zeros_like(acc)
            dg_ref[...] = jnp.zeros_like(dg_ref)
            dgq_ref[...] = jnp.zeros_like(dgq_ref)

        x = h_ref[...]
        a = _rms(x, g_ref[...]).astype(BF16)
        da = jnp.zeros((tm, D), F32)
        for j, r in enumerate((dq_ref, dk_ref, dv_ref, du_ref)):
            dp = r[...].astype(BF16)
            da = da + _dg(dp, w_ref[j], NT)
            acc[j] += _dg(a, dp, TN)
        dx, dg = _rms_bwd(x, g_ref[...], da)
        dh1 = dh_ref[...] + dx
        dh1_ref[...] = dh1
        dg_ref[...] += dg
        df, dgq = _rms_bwd(f_ref[...], gq_ref[...], 0.5 * dh1)
        df_ref[...] = df
        dgq_ref[...] += dgq

        @pl.when(i == nt - 1)
        def _():
            pltpu.sync_copy(acc, dw_ref)

    return _call(
        body, comm, bounds, (dq, dk, dv, du, h, g, w_in, dh, f1, g_post1), name="mix_in_bwd", grid=(nt,),
        in_specs=[_rows(tm, NA_W)] * 4 + [_rows(tm, D), _full((1, D)), _full((N_CHIP, D, NA_W)), _rows(tm, D),
                                         _rows(tm, D), _full((1, D))],
        out_specs=[_rows(tm, D), _rows(tm, D), ANY, _full((1, D)), _full((1, D))],
        out_shape=[_out((tp, D), F32), _out((tp, D), F32),
                   _out((N_CHIP, D, NA_W), F32), _out((1, D), F32),
                   _out((1, D), F32)],
        scratch_shapes=[pltpu.VMEM((N_CHIP, D, NA_W), F32)],
        compiler_params=_cp(("arbitrary",), 48))


def _final_loss(h, g_final, target, f2, g_post2, n_tok, tm):
    tp = h.shape[0]

    def body(h_ref, g_ref, t_ref, f_ref, gq_ref, dh_ref, df_ref, loss_ref, dg_ref, dgq_ref):
        i = pl.program_id(0)

        @pl.when(i == 0)
        def _():
            loss_ref[...] = jnp.zeros_like(loss_ref)
            dg_ref[...] = jnp.zeros_like(dg_ref)
            dgq_ref[...] = jnp.zeros_like(dgq_ref)

        x = h_ref[...]
        y = _rms(x, g_ref[...])
        row = i * tm + lax.broadcasted_iota(jnp.int32, (tm, 1), 0)
        valid = (row >= N_META) & (row < N_META + n_tok)
        e = jnp.where(valid, y - t_ref[...], 0.0)
        loss_ref[...] += 0.5 * jnp.sum(jnp.mean(e * e, axis=-1, keepdims=True), axis=0, keepdims=True)
        dx, dg = _rms_bwd(x, g_ref[...], e * (1.0 / D))
        dh_ref[...] = dx
        dg_ref[...] += dg
        df, dgq = _rms_bwd(f_ref[...], gq_ref[...], 0.5 * dx)
        df_ref[...] = df
        dgq_ref[...] += dgq

    return pl.pallas_call(
        body, name="final_loss", grid=(tp // tm,),
        in_specs=[_rows(tm, D), _full((1, D)), _rows(tm, D), _rows(tm, D), _full((1, D))],
        out_specs=[_rows(tm, D), _rows(tm, D), _full((1, 1)), _full((1, D)), _full((1, D))],
        out_shape=[_out((tp, D), F32), _out((tp, D), F32),
                   _out((1, 1), F32), _out((1, D), F32),
                   _out((1, D), F32)],
        compiler_params=_cp(("arbitrary",), 40),
    )(*_in_hbm(h, g_final, target, f2, g_post2))


def _na_patterns(n_rows):
    pats = []
    for kind in range(3):
        pat = [[-1] * K_ROWS for _ in range(Q_ROWS)]
        for i in range(Q_ROWS):
            for jj in range(K_ROWS):
                if kind == 0 and jj < KH:
                    pat[i][jj] = jj - i + KH - 1
                elif kind == 1 and i <= jj < i + KH:
                    pat[i][jj] = jj - i + 3
                elif kind == 2 and K_ROWS - KH <= jj:
                    pat[i][jj] = jj - i - 1
        pats.append(pat)
    return pats


def _diag_onehot():
    q = np.arange(GRID_W)[:, None]
    kc = np.arange(GRID_W)[None, :]
    start = np.clip(q - KW // 2, 0, GRID_W - KW)
    col_in = (kc >= start) & (kc < start + KW)
    e = np.zeros((32, GRID_W, GRID_W), np.float32)
    for d in range(2 * KW - 1):
        e[d] = ((kc - q + KW - 1) == d) & col_in
    return e.reshape(32, GRID_W * GRID_W), col_in


def _rpb_collapse(dtb2, et):
    def body(d_ref, e_ref, o_ref):
        o_ref[...] = jnp.dot(d_ref[...], e_ref[...], preferred_element_type=F32, precision=lax.Precision.HIGHEST)

    out = (dtb2.shape[0], et.shape[1])
    return pl.pallas_call(
        body, name="rpb_collapse", grid=(1,), out_shape=_out(out, F32),
        in_specs=[_full(dtb2.shape), _full(et.shape)], out_specs=_full(out),
    )(*_in_hbm(dtb2, et))


def _bias_tables(rpb, n_rows):
    n_dr, n_dc = 2 * KH - 1, 2 * KW - 1
    pats = _na_patterns(n_rows)

    def body(rpb_ref, o_ref):
        h = pl.program_id(0)
        q = lax.broadcasted_iota(jnp.int32, (GRID_W, GRID_W), 0)
        kc = lax.broadcasted_iota(jnp.int32, (GRID_W, GRID_W), 1)
        start = jnp.clip(q - KW // 2, 0, GRID_W - KW)
        col_in = (kc >= start) & (kc < start + KW)
        diff = kc - q + (KW - 1)
        neg = jnp.full((GRID_W, GRID_W), NEG_INF, F32)
        band = []
        for dr in range(n_dr):
            acc = neg
            for d in range(n_dc):
                acc = jnp.where((diff == d) & col_in, rpb_ref[(h * n_dr + dr) * n_dc + d], acc)
            band.append(acc)
        for kind, pat in enumerate(pats):
            for i in range(Q_ROWS):
                for jj in range(K_ROWS):
                    o_ref[kind, 0, i * GRID_W:(i + 1) * GRID_W, jj * GRID_W:(jj + 1) * GRID_W] = (
                        band[pat[i][jj]] if pat[i][jj] >= 0 else neg)

    return pl.pallas_call(
        body, name="bias_tables", grid=(N_HEADS,),
        in_specs=[pl.BlockSpec(memory_space=pltpu.SMEM)],
        out_specs=pl.BlockSpec((3, 1, QB, KB), lambda h: (0, h, 0, 0)),
        out_shape=_out((3, N_HEADS, QB, KB), F32),
        compiler_params=_cp(("arbitrary",), 32),
    )(rpb.reshape(-1))


def _attn_geometry(n_tok):
    n_rows = n_tok // GRID_W
    assert n_rows % Q_ROWS == 0 and n_rows >= K_ROWS
    return n_rows, n_rows // Q_ROWS


def _attn_probs(qh, kh, kmh, bias, scale):
    s = _dg(qh, kh, NT) * scale + bias
    sm = _dg(qh, kmh, NT) * scale
    m = jnp.maximum(jnp.max(s, axis=-1, keepdims=True), jnp.max(sm, axis=-1, keepdims=True))
    p = jnp.exp(s - m)
    pm = jnp.exp(sm - m)
    inv = 1.0 / (jnp.sum(p, axis=-1, keepdims=True) + jnp.sum(pm, axis=-1, keepdims=True))
    return p * inv, pm * inv


def _meta_probs(qmh, kmh, scale):
    s = _dg(qmh, kmh, NT) * scale
    p = jnp.exp(s - jnp.max(s, axis=-1, keepdims=True))
    return p / jnp.sum(p, axis=-1, keepdims=True)


def _step_rows(r, n_rows):
    q0 = pl.multiple_of(N_META + r * QB, 16)
    k0 = pl.multiple_of(N_META + jnp.clip(Q_ROWS * r - (K_ROWS - KH), 0, n_rows - K_ROWS) * GRID_W, 16)
    return q0, k0


def _attn_fwd(q, k, v, bias, n_tok, comm=None, bounds=()):
    tp = q.shape[0]
    n_rows, n_steps = _attn_geometry(n_tok)
    scale = HEAD_DIM ** -0.5

    def body(q_ref, k_ref, v_ref, b_ref, o_ref):
        r = pl.program_id(1)
        km = k_ref[0:N_META, :]
        vm = v_ref[0:N_META, :]

        @pl.when(r == 0)
        def _():
            qm = q_ref[0:N_META, :]
            outs = []
            for hh in range(2):
                sl = slice(hh * HEAD_DIM, (hh + 1) * HEAD_DIM)
                p = _meta_probs(qm[:, sl], km[:, sl], scale)
                outs.append(_dot(p.astype(BF16), vm[:, sl]))
            o_ref[0:N_META, :] = jnp.concatenate(outs, axis=1)
            o_ref[N_META + n_tok:, :] = jnp.zeros((tp - N_META - n_tok, 2 * HEAD_DIM), F32)

        q0, k0 = _step_rows(r, n_rows)
        qb = q_ref[pl.ds(q0, QB), :]
        kb = k_ref[pl.ds(k0, KB), :]
        vb = v_ref[pl.ds(k0, KB), :]
        outs = []
        for hh in range(2):
            sl = slice(hh * HEAD_DIM, (hh + 1) * HEAD_DIM)
            p, pm = _attn_probs(qb[:, sl], kb[:, sl], km[:, sl], b_ref[0, hh], scale)
            outs.append(_dot(p.astype(BF16), vb[:, sl]) + _dot(pm.astype(BF16), vm[:, sl]))
        o_ref[pl.ds(q0, QB), :] = jnp.concatenate(outs, axis=1)

    def bias_map(hp, r):
        return (jnp.where(r == 0, 0, jnp.where(r == n_steps - 1, 2, 1)), hp, 0, 0)

    col = pl.BlockSpec((tp, 2 * HEAD_DIM), lambda hp, r: (0, hp))
    return _call(
        body, comm, bounds, (q, k, v, bias), name="attn_fwd", grid=(N_HEADS // 2, n_steps),
        in_specs=[col, col, col, pl.BlockSpec((1, 2, QB, KB), bias_map)],
        out_specs=[col], out_shape=[_out((tp, NA_W), F32)],
        compiler_params=_cp(("arbitrary", "arbitrary"), 40))


def _attn_bwd(q, k, v, bias, do, n_tok, comm=None, bounds=()):
    tp = q.shape[0]
    n_rows, n_steps = _attn_geometry(n_tok)
    scale = HEAD_DIM ** -0.5
    pats = _na_patterns(n_rows)

    def body(q_ref, k_ref, v_ref, b_ref, do_ref, dq_ref, dk_ref, dv_ref, dtb_ref):
        r = pl.program_id(1)
        km = k_ref[0:N_META, :]
        vm = v_ref[0:N_META, :]

        @pl.when(r == 0)
        def _():
            dk_ref[...] = jnp.zeros_like(dk_ref)
            dv_ref[...] = jnp.zeros_like(dv_ref)
            dtb_ref[...] = jnp.zeros_like(dtb_ref)
            dq_ref[N_META + n_tok:, :] = jnp.zeros((tp - N_META - n_tok, 2 * HEAD_DIM), F32)
            qm = q_ref[0:N_META, :]
            dom = do_ref[0:N_META, :].astype(BF16)
            dqs, dks, dvs = [], [], []
            for hh in range(2):
                sl = slice(hh * HEAD_DIM, (hh + 1) * HEAD_DIM)
                p = _meta_probs(qm[:, sl], km[:, sl], scale)
                dp = _dg(dom[:, sl], vm[:, sl], NT)
                ds = (p * (dp - jnp.sum(dp * p, axis=-1, keepdims=True))).astype(BF16)
                dvs.append(_dg(p.astype(BF16), dom[:, sl], TN))
                dqs.append(_dot(ds, km[:, sl]) * scale)
                dks.append(_dg(ds, qm[:, sl], TN) * scale)
            dq_ref[0:N_META, :] = jnp.concatenate(dqs, axis=1)
            dk_ref[0:N_META, :] += jnp.concatenate(dks, axis=1)
            dv_ref[0:N_META, :] += jnp.concatenate(dvs, axis=1)

        q0, k0 = _step_rows(r, n_rows)
        qb = q_ref[pl.ds(q0, QB), :]
        kb = k_ref[pl.ds(k0, KB), :]
        vb = v_ref[pl.ds(k0, KB), :]
        dob = do_ref[pl.ds(q0, QB), :].astype(BF16)
        dqs, dks, dvs, dkms, dvms, dss = [], [], [], [], [], []
        for hh in range(2):
            sl = slice(hh * HEAD_DIM, (hh + 1) * HEAD_DIM)
            qh, kh, vh, kmh, vmh, doh = qb[:, sl], kb[:, sl], vb[:, sl], km[:, sl], vm[:, sl], dob[:, sl]
            p, pm = _attn_probs(qh, kh, kmh, b_ref[0, hh], scale)
            dp = _dg(doh, vh, NT)
            dpm = _dg(doh, vmh, NT)
            delta = jnp.sum(dp * p, axis=-1, keepdims=True) + jnp.sum(dpm * pm, axis=-1, keepdims=True)
            ds = p * (dp - delta)
            dsb = ds.astype(BF16)
            dsmb = (pm * (dpm - delta)).astype(BF16)
            dss.append(ds)
            dvs.append(_dg(p.astype(BF16), doh, TN))
            dvms.append(_dg(pm.astype(BF16), doh, TN))
            dqs.append((_dot(dsb, kh) + _dot(dsmb, kmh)) * scale)
            dks.append(_dg(dsb, qh, TN) * scale)
            dkms.append(_dg(dsmb, qh, TN) * scale)
        dq_ref[pl.ds(q0, QB), :] = jnp.concatenate(dqs, axis=1)
        dk_ref[pl.ds(k0, KB), :] += jnp.concatenate(dks, axis=1)
        dv_ref[pl.ds(k0, KB), :] += jnp.concatenate(dvs, axis=1)
        dk_ref[0:N_META, :] += jnp.concatenate(dkms, axis=1)
        dv_ref[0:N_META, :] += jnp.concatenate(dvms, axis=1)

        def add_bias_grad(pat):
            for hh in range(2):
                for i in range(Q_ROWS):
                    for jj in range(K_ROWS):
                        if pat[i][jj] >= 0:
                            dtb_ref[hh, pat[i][jj]] += dss[hh][i * GRID_W:(i + 1) * GRID_W,
                                                               jj * GRID_W:(jj + 1) * GRID_W]

        @pl.when(r == 0)
        def _():
            add_bias_grad(pats[0])

        @pl.when((r > 0) & (r < n_steps - 1))
        def _():
            add_bias_grad(pats[1])

        @pl.when(r == n_steps - 1)
        def _():
            add_bias_grad(pats[2])

    def bias_map(hp, r):
        return (jnp.where(r == 0, 0, jnp.where(r == n_steps - 1, 2, 1)), hp, 0, 0)

    col = pl.BlockSpec((tp, 2 * HEAD_DIM), lambda hp, r: (0, hp))
    n_dr = 2 * KH - 1
    return _call(
        body, comm, bounds, (q, k, v, bias, do), name="attn_bwd", grid=(N_HEADS // 2, n_steps),
        in_specs=[col, col, col, pl.BlockSpec((1, 2, QB, KB), bias_map), col],
        out_specs=[col, col, col, pl.BlockSpec((2, n_dr, GRID_W, GRID_W), lambda hp, r: (hp, 0, 0, 0))],
        out_shape=[_out((tp, NA_W), F32)] * 3 +
                  [_out((N_HEADS, n_dr, GRID_W, GRID_W), F32)],
        compiler_params=_cp(("arbitrary", "arbitrary"), 48))


def _repeat_onehot():
    return np.repeat(np.eye(2 * S5_G, dtype=np.float32), S5_H, axis=0)


def _s5_disc_math(lam_re, lam_im, log_dt, b_re, b_im, rep):
    dt = jnp.exp(log_dt)
    ea = jnp.exp(lam_re * dt)
    a_re = ea * jnp.cos(lam_im * dt)
    a_im = ea * jnp.sin(lam_im * dt)
    den = lam_re * lam_re + lam_im * lam_im
    c_re = ((a_re - 1.0) * lam_re + a_im * lam_im) / den
    c_im = (a_im * lam_re - (a_re - 1.0) * lam_im) / den
    ce_re = jnp.dot(rep, c_re, preferred_element_type=F32, precision=lax.Precision.HIGHEST)
    ce_im = jnp.dot(rep, c_im, preferred_element_type=F32, precision=lax.Precision.HIGHEST)
    return a_re, a_im, ce_re * b_re - ce_im * b_im, ce_re * b_im + ce_im * b_re


def _s5_blocks():
    gl = S5_G // N_BUNDLE
    half = gl * S5_P
    out = []
    for d in range(2):
        for g in range(S5_G):
            b, k = divmod(g, gl)
            dg = d * S5_G + g
            out.append((d, b, slice(k * S5_H, (k + 1) * S5_H), slice(k * S5_P, (k + 1) * S5_P),
                        slice(half + k * S5_P, half + (k + 1) * S5_P), slice(dg * S5_H, (dg + 1) * S5_H),
                        slice(dg, dg + 1)))
    return out


def _s5_params(lam_re, lam_im, log_dt, b_re, b_im, c_re, c_im):
    cw, sw = S5_W // N_BUNDLE, 2 * (S5_G // N_BUNDLE) * S5_P

    def body(lr, li, ld, br, bi, cr, ci, rep_ref, a_ref, bm_ref, cm_ref):
        a_re, a_im, bb_re, bb_im = _s5_disc_math(lr[...], li[...], ld[...], br[...], bi[...], rep_ref[...])
        cc_re = cr[...]
        cc_im = ci[...]
        bm_ref[...] = jnp.zeros_like(bm_ref)
        cm_ref[...] = jnp.zeros_like(cm_ref)
        for d, b, rows, re, im, nat, one in _s5_blocks():
            bm_ref[d, b, rows, re] = bb_re[nat, :].astype(BF16)
            bm_ref[d, b, rows, im] = bb_im[nat, :].astype(BF16)
            cm_ref[d, b, rows, re] = cc_re[nat, :].astype(BF16)
            cm_ref[d, b, rows, im] = (-cc_im[nat, :]).astype(BF16)
            a_ref[d, b, :, re] = a_re[one, :]
            a_ref[d, b, :, im] = a_im[one, :]

    args = (lam_re, lam_im, log_dt, b_re, b_im, c_re, c_im, jnp.asarray(_repeat_onehot()))
    outs = [((2, N_BUNDLE, 1, sw), F32), ((2, N_BUNDLE, cw, sw), BF16), ((2, N_BUNDLE, cw, sw), BF16)]
    return pl.pallas_call(
        body, name="s5_params", grid=(1,), in_specs=[_full(a.shape) for a in args],
        out_specs=[_full(s) for s, _ in outs], out_shape=[_out(s, dt) for s, dt in outs],
    )(*_in_hbm(*args))


def _s5_params_bwd(lam_re, lam_im, log_dt, b_re, b_im, da, dbm, dcm):
    n, nb = 2 * S5_G, 2 * S5_G * S5_H

    def body(lr, li, ld, br, bi, rep_ref, da_ref, dbm_ref, dcm_ref, o_lr, o_li, o_ld, o_br, o_bi, o_cr, o_ci,
             dar_s, dai_s, dbr_s, dbi_s):
        for d, b, rows, re, im, nat, one in _s5_blocks():
            dbr_s[nat, :] = dbm_ref[d, b, rows, re]
            dbi_s[nat, :] = dbm_ref[d, b, rows, im]
            o_cr[nat, :] = dcm_ref[d, b, rows, re]
            o_ci[nat, :] = -dcm_ref[d, b, rows, im]
            dar_s[one, :] = da_ref[d, b, :, re]
            dai_s[one, :] = da_ref[d, b, :, im]
        rep = rep_ref[...]
        _, vjp = jax.vjp(lambda p, q, r, s, t: _s5_disc_math(p, q, r, s, t, rep),
                         lr[...], li[...], ld[...], br[...], bi[...])
        o_lr[...], o_li[...], o_ld[...], o_br[...], o_bi[...] = vjp((dar_s[...], dai_s[...], dbr_s[...], dbi_s[...]))

    args = (lam_re, lam_im, log_dt, b_re, b_im, jnp.asarray(_repeat_onehot()), da, dbm, dcm)
    outs = [(n, S5_P)] * 2 + [(n, 1)] + [(nb, S5_P)] * 4
    return pl.pallas_call(
        body, name="s5_params_bwd", grid=(1,), in_specs=[_full(a.shape) for a in args],
        out_specs=[_full(s) for s in outs], out_shape=[_out(s, F32) for s in outs],
        scratch_shapes=[pltpu.VMEM((n, S5_P), F32)] * 2 + [pltpu.VMEM((nb, S5_P), F32)] * 2,
    )(*_in_hbm(*args))


def _scan_chunks(length):
    return [(t0, min(SCAN_CHUNK, length - t0)) for t0 in range(0, length, SCAN_CHUNK)]


def _scan(src_ref, dst_ref, prev_ref, prev_off, n_rows, a_re, a_im, carry, reverse):
    half = a_re.shape[-1]
    n_blk = n_rows // 8
    rid = lax.broadcasted_iota(jnp.int32, (8, half), 0)

    def blk(i, carry):
        xr, xi = carry
        bi = (n_blk - 1 - i) if reverse else i
        off = pl.multiple_of(bi * 8, 8)
        v = src_ref[pl.ds(off, 8), :]
        o_r = jnp.zeros((8, half), F32)
        o_i = jnp.zeros((8, half), F32)
        p_r = jnp.zeros((8, half), F32)
        p_i = jnp.zeros((8, half), F32)
        for j in (range(7, -1, -1) if reverse else range(8)):
            if prev_ref is not None:
                p_r = jnp.where(rid == j, xr, p_r)
                p_i = jnp.where(rid == j, xi, p_i)
            nr = a_re * xr - a_im * xi + v[j:j + 1, :half]
            ni = a_re * xi + a_im * xr + v[j:j + 1, half:]
            xr, xi = nr, ni
            if dst_ref is not None:
                o_r = jnp.where(rid == j, xr, o_r)
                o_i = jnp.where(rid == j, xi, o_i)
        if dst_ref is not None:
            dst_ref[pl.ds(off, 8), :] = jnp.concatenate([o_r, o_i], axis=1)
        if prev_ref is not None:
            prev_ref[pl.ds(pl.multiple_of(prev_off + off, 8), 8), :] = jnp.concatenate([p_r, p_i], axis=1)
        return xr, xi

    return lax.fori_loop(0, n_blk, blk, carry)


def _s5_fwd(u, d_skip, a, bm, cm, length, comm=None, bounds=()):
    tp = u.shape[0]
    cw = S5_W // N_BUNDLE
    sw = a.shape[-1]
    half = sw // 2
    chunks = _scan_chunks(length)

    def body(u_ref, d_ref, a_ref, bm_ref, cm_ref, y_ref, bu_s, xs_s):
        y_ref[...] = u_ref[...] * d_ref[...]
        for dr in range(2):
            a_re = a_ref[dr, 0, :, 0:half]
            a_im = a_ref[dr, 0, :, half:]
            carry = (jnp.zeros((1, half), F32), jnp.zeros((1, half), F32))
            for t0, n in (chunks if dr == 0 else chunks[::-1]):
                bu_s[0:n, :] = _dot(u_ref[t0:t0 + n, :].astype(BF16), bm_ref[dr, 0])
                carry = _scan(bu_s, xs_s, None, 0, n, a_re, a_im, carry, dr == 1)
                y_ref[t0:t0 + n, :] += _dg(xs_s[0:n, :].astype(BF16), cm_ref[dr, 0], NT)

    return _call(
        body, comm, bounds, (u, d_skip, a, bm, cm), name="s5_fwd", grid=(N_BUNDLE,),
        in_specs=[pl.BlockSpec((tp, cw), lambda b: (0, b)), pl.BlockSpec((1, cw), lambda b: (0, b)),
                  pl.BlockSpec((2, 1, 1, sw), lambda b: (0, b, 0, 0)),
                  pl.BlockSpec((2, 1, cw, sw), lambda b: (0, b, 0, 0)),
                  pl.BlockSpec((2, 1, cw, sw), lambda b: (0, b, 0, 0))],
        out_specs=[pl.BlockSpec((tp, cw), lambda b: (0, b))],
        out_shape=[_out((tp, S5_W), F32)],
        scratch_shapes=[pltpu.VMEM((SCAN_CHUNK, sw), F32), pltpu.VMEM((SCAN_CHUNK, sw), F32)],
        compiler_params=_cp(("arbitrary",), 40))


def _s5_bwd(u, dy, d_skip, a, bm, cm, length):
    tp = u.shape[0]
    cw = S5_W // N_BUNDLE
    sw = a.shape[-1]
    half = sw // 2
    chunks = _scan_chunks(length)

    def body(u_ref, dy_ref, d_ref, a_ref, bm_ref, cm_ref, du_ref, dd_ref, dbm_ref, dcm_ref, da_ref, bu_s, g_s, xp_s):
        du_ref[...] = dy_ref[...] * d_ref[...]
        dd_ref[...] = jnp.sum(dy_ref[...] * u_ref[...], axis=0, keepdims=True)
        dbm_ref[...] = jnp.zeros_like(dbm_ref)
        dcm_ref[...] = jnp.zeros_like(dcm_ref)
        zero = (jnp.zeros((1, half), F32), jnp.zeros((1, half), F32))
        for dr in range(2):
            a_re = a_ref[dr, 0, :, 0:half]
            a_im = a_ref[dr, 0, :, half:]
            seq = chunks if dr == 0 else chunks[::-1]
            carry = zero
            for t0, n in seq:
                bu_s[0:n, :] = _dot(u_ref[t0:t0 + n, :].astype(BF16), bm_ref[dr, 0])
                carry = _scan(bu_s, None, xp_s, t0, n, a_re, a_im, carry, dr == 1)
            carry = zero
            da_r = jnp.zeros((1, half), F32)
            da_i = jnp.zeros((1, half), F32)
            for t0, n in seq[::-1]:
                ub = u_ref[t0:t0 + n, :].astype(BF16)
                dyb = dy_ref[t0:t0 + n, :].astype(BF16)
                bu_s[0:n, :] = _dot(dyb, cm_ref[dr, 0])
                carry = _scan(bu_s, g_s, None, 0, n, a_re, -a_im, carry, dr == 0)
                g = g_s[0:n, :]
                gb = g.astype(BF16)
                du_ref[t0:t0 + n, :] += _dg(gb, bm_ref[dr, 0], NT)
                dbm_ref[dr, 0] += _dg(ub, gb, TN)
                xp = xp_s[t0:t0 + n, :]
                xp_r, xp_i = xp[:, 0:half], xp[:, half:]
                g_r, g_i = g[:, 0:half], g[:, half:]
                bu = _dot(ub, bm_ref[dr, 0])
                x_r = a_re * xp_r - a_im * xp_i + bu[:, 0:half]
                x_i = a_re * xp_i + a_im * xp_r + bu[:, half:]
                dcm_ref[dr, 0] += _dg(dyb, jnp.concatenate([x_r, x_i], axis=1).astype(BF16), TN)
                da_r = da_r + jnp.sum(g_r * xp_r + g_i * xp_i, axis=0, keepdims=True)
                da_i = da_i + jnp.sum(g_i * xp_r - g_r * xp_i, axis=0, keepdims=True)
            da_ref[dr, 0] = jnp.concatenate([da_r, da_i], axis=1)

    lp = -(-length // 8) * 8
    return pl.pallas_call(
        body, name="s5_bwd", grid=(N_BUNDLE,),
        in_specs=[pl.BlockSpec((tp, cw), lambda b: (0, b)), pl.BlockSpec((tp, cw), lambda b: (0, b)),
                  pl.BlockSpec((1, cw), lambda b: (0, b)),
                  pl.BlockSpec((2, 1, 1, sw), lambda b: (0, b, 0, 0)),
                  pl.BlockSpec((2, 1, cw, sw), lambda b: (0, b, 0, 0)),
                  pl.BlockSpec((2, 1, cw, sw), lambda b: (0, b, 0, 0))],
        out_specs=[pl.BlockSpec((tp, cw), lambda b: (0, b)), pl.BlockSpec((1, cw), lambda b: (0, b)),
                   pl.BlockSpec((2, 1, cw, sw), lambda b: (0, b, 0, 0)),
                   pl.BlockSpec((2, 1, cw, sw), lambda b: (0, b, 0, 0)),
                   pl.BlockSpec((2, 1, 1, sw), lambda b: (0, b, 0, 0))],
        out_shape=[_out((tp, S5_W), F32), _out((1, S5_W), F32),
                   _out((2, N_BUNDLE, cw, sw), F32), _out((2, N_BUNDLE, cw, sw), F32),
                   _out((2, N_BUNDLE, 1, sw), F32)],
        scratch_shapes=[pltpu.VMEM((SCAN_CHUNK, sw), F32), pltpu.VMEM((SCAN_CHUNK, sw), F32),
                        pltpu.VMEM((lp, sw), F32)],
        compiler_params=_cp(("arbitrary",), 48),
    )(*_in_hbm(u, dy, d_skip, a, bm, cm))


def _row_tile(tp):
    return max(tm for tm in range(16, 449, 16) if tp % tm == 0)


def _step(x, target, bufs, gains, s5, rpb, c_arr, kc_arr):
    first = ["ffn1_w_gate", "ffn1_w_up", "ffn1_w_down", "meta_tokens"]
    w = dict(zip(first, _run_comm("gather_ffn1", _gather_comm([bufs[n] for n in first]))))
    meta = w["meta_tokens"].transpose(1, 0, 2).reshape(N_META, D)
    n_tok = x.shape[0]
    length = N_META + n_tok
    tp = length + 16
    tm = _row_tile(tp)
    tmb = tm
    n_rows = n_tok // GRID_W
    pad = jnp.zeros((tp - length, D), F32)
    h0 = jnp.concatenate([meta, x, pad], axis=0)
    tgt = jnp.concatenate([jnp.zeros((N_META, D), F32), target, pad], axis=0)

    s5p = (s5["lam_re"], s5["lam_im"], s5["log_dt"].reshape(2 * S5_G, 1), s5["b_re"], s5["b_im"])
    a_m, bm16, cm16 = _s5_params(*s5p, s5["c_re"], s5["c_im"])
    bias = _bias_tables(rpb, n_rows)

    mid = ["w_in", "s5_w_glu", "w_out"]
    (h1, gate1, up1, f1), got = _ffn_fwd(
        "ffn1_fwd", h0, gains["ffn1_pre_g"], gains["ffn1_post_g"], w["ffn1_w_gate"], w["ffn1_w_up"], w["ffn1_w_down"],
        tm, _gather_comm([bufs[n] for n in mid]), (0, (tp // tm) * N_CHIP * 3 // 5))
    w.update(zip(mid, got))
    q, k, v, u = _mix_in(h1, gains["mix_pre_g"], w["w_in"], tm)
    (o_na,), (gate_ici,) = _attn_fwd(q, k, v, bias, n_tok, _gather_comm([bufs["ffn2_w_gate"]], pair=False), (0,))
    (y_pre,), (w["ffn2_w_gate"], up_ici, down_ici) = _s5_fwd(
        u, gains["s5_d"], a_m, bm16, cm16, length,
        _merge_comm(_gather_comm([gate_ici], ici=False),
                    _gather_comm([bufs["ffn2_w_up"], bufs["ffn2_w_down"]], pair=False)), (0,))
    w_glu = w["s5_w_glu"].reshape(S5_W, S5_W)
    w_out = w["w_out"].reshape(D, D)
    (h2, mix), (w["ffn2_w_up"], w["ffn2_w_down"]) = _mix_out(
        o_na, y_pre, h1, w_glu, gains["s5_b_glu"], gains["na_out_g"], gains["s5_out_g"], w_out, gains["mix_post_g"], tm,
        _gather_comm([up_ici, down_ici], ici=False), (0,))
    (h3, gate2, up2, f2), _ = _ffn_fwd("ffn2_fwd", h2, gains["ffn2_pre_g"], gains["ffn2_post_g"],
                                       w["ffn2_w_gate"], w["ffn2_w_up"], w["ffn2_w_down"], tm)
    dh3, df2, loss, dg_final, dg_post2 = _final_loss(h3, gains["final_g"], tgt, f2, gains["ffn2_post_g"], n_tok, tm)

    ffn2 = ["ffn2_w_gate", "ffn2_w_up", "ffn2_w_down"]
    ffn1 = ["ffn1_w_gate", "ffn1_w_up", "ffn1_w_down"]
    out2 = _ffn_bwd("ffn2_bwd", h2, gains["ffn2_pre_g"], df2, gate2, up2,
                    w["ffn2_w_gate"], w["ffn2_w_up"], w["ffn2_w_down"], tmb)
    dxn2 = out2[3]
    sums2 = [_chip_sum("chip_sum_" + n, g, r, c_arr) for n, g, r in zip(ffn2, out2[0:3], out2[4:7])]
    (dh2, dg_pre2), _ = _ffn_pre_bwd("ffn2_pre_bwd", dh3, dxn2, h2, gains["ffn2_pre_g"], tm)
    do_na, dy_pre, dw_out, dw_glu, dg_mpost, dg_na, dg_s5, db_glu = _mix_out_bwd(
        dh2, mix, o_na, y_pre, w_glu, gains["s5_b_glu"], gains["na_out_g"], gains["s5_out_g"], w_out,
        gains["mix_post_g"], tm)
    (dq, dk, dv, dtb), recv3 = _attn_bwd(q, k, v, bias, do_na, n_tok, _scatter_comm(sums2), (0,))
    totals2 = [_total_sum("total_sum_" + n, s, r, kc_arr) for n, s, r in zip(ffn2, sums2, recv3)]
    du, dd, dbm, dcm, da_m = _s5_bwd(u, dy_pre, gains["s5_d"], a_m, bm16, cm16, length)
    (dh1, df1, dw_in, dg_mpre, dg_post1), done2 = _mix_in_bwd(
        dq, dk, dv, du, h1, gains["mix_pre_g"], w["w_in"], dh2, f1, gains["ffn1_post_g"], tm,
        _assemble_comm(totals2), (0,))
    pieces = dict(zip(ffn2, done2))
    out1 = _ffn_bwd("ffn1_bwd", h0, gains["ffn1_pre_g"], df1, gate1, up1,
                    w["ffn1_w_gate"], w["ffn1_w_up"], w["ffn1_w_down"], tmb)
    rest = [dw_in, dw_glu.reshape(N_CHIP, S5_W // N_CHIP, S5_W), dw_out.reshape(N_CHIP, D // N_CHIP, D)]
    (dh0, dg_pre1), recv_rest = _ffn_pre_bwd("ffn1_pre_bwd", dh1, out1[3], h0, gains["ffn1_pre_g"], tm,
                                             _exchange_comm(rest), (0,))
    last = ffn1 + mid
    sums = [_chip_sum("chip_sum_" + n, g, r, c_arr)
            for n, g, r in zip(last, list(out1[0:3]) + rest, list(out1[4:7]) + list(recv_rest))]
    recv3 = _run_comm("grad_chip_scatter", _scatter_comm(sums))
    totals = [_total_sum("total_sum_" + n, s, r, kc_arr) for n, s, r in zip(last, sums, recv3)]
    pieces.update(zip(last, _run_comm("grad_pair_assemble", _assemble_comm(totals))))

    e, _ = _diag_onehot()
    n_dr = 2 * KH - 1
    drpb = _rpb_collapse(dtb.reshape(N_HEADS * n_dr, GRID_W * GRID_W), jnp.asarray(e.T))
    drpb = drpb[:, :2 * KW - 1].reshape(N_HEADS, n_dr, 2 * KW - 1).transpose(1, 0, 2).reshape(N_HEADS * n_dr, 2 * KW - 1)
    dlam_re, dlam_im, dlog_dt, db_re, db_im, dc_re, dc_im = _s5_params_bwd(*s5p, da_m, dbm, dcm)

    small = {"ffn1_pre_g": dg_pre1, "ffn1_post_g": dg_post1, "mix_pre_g": dg_mpre, "na_rpb": drpb,
             "s5_lam_re": dlam_re, "s5_lam_im": dlam_im, "s5_log_dt": dlog_dt.reshape(2, S5_G),
             "s5_b_re": db_re, "s5_b_im": db_im, "s5_c_re": dc_re, "s5_c_im": dc_im,
             "s5_d": dd, "s5_b_glu": db_glu, "na_out_g": dg_na,
             "s5_out_g": dg_s5, "mix_post_g": dg_mpost, "ffn2_pre_g": dg_pre2, "ffn2_post_g": dg_post2,
             "final_g": dg_final}
    return loss[0, 0], dh0, pieces, small


def _mesh_pos():
    return lax.axis_index("x"), lax.axis_index("y"), lax.axis_index("c")


def _other_chips(x, y):
    return [(1 - x, y), (x, 1 - y), (1 - x, 1 - y)]


class _Comm:
    def __init__(self, ins, out_shape, aliases, parts):
        self.ins, self.out_shape, self.aliases, self.parts = list(ins), list(out_shape), dict(aliases), list(parts)
        self.n_sems = sum(p[0] for p in parts)

    def bases(self):
        out, base = [], 0
        for n_sems, _, _ in self.parts:
            out.append(base)
            base += n_sems
        return out


def _run_comm(name, comm):
    n_i, n_o = len(comm.ins), len(comm.out_shape)

    def body(*refs):
        ins, outs = refs[:n_i], refs[n_i:n_i + n_o]
        send_sems, recv_sems = refs[n_i + n_o:]
        for base, (_, start, finish) in zip(comm.bases(), comm.parts):
            start(ins, outs, send_sems, recv_sems, base)
            finish(ins, outs, send_sems, recv_sems, base)

    return pl.pallas_call(
        body, name=name, out_shape=comm.out_shape, in_specs=[ANY] * n_i, out_specs=[ANY] * n_o,
        input_output_aliases=comm.aliases,
        scratch_shapes=[pltpu.SemaphoreType.DMA((comm.n_sems,)), pltpu.SemaphoreType.DMA((comm.n_sems,))],
    )(*_in_hbm(*comm.ins))


def _call(body, comm, bounds, args, *, name, grid, in_specs, out_specs, out_shape, scratch_shapes=(),
          compiler_params=None):
    in_specs, out_specs, out_shape, scratch_shapes = list(in_specs), list(out_specs), list(out_shape), list(scratch_shapes)
    if comm is None:
        return pl.pallas_call(body, name=name, grid=grid, in_specs=in_specs, out_specs=out_specs, out_shape=out_shape,
                              scratch_shapes=scratch_shapes, compiler_params=compiler_params)(*_in_hbm(*args)), []
    n_in, n_out, n_scr = len(in_specs), len(out_specs), len(scratch_shapes)
    n_ci, n_co = len(comm.ins), len(comm.out_shape)
    n_steps = int(np.prod(grid))
    assert len(bounds) == len(comm.parts) and all(0 <= b < n_steps for b in bounds) and list(bounds) == sorted(bounds)

    def fused(*refs):
        a = n_in
        b = a + n_ci
        c = b + n_out
        d = c + n_co
        e = d + n_scr
        cargs = (refs[a:b], refs[c:d], refs[e], refs[e + 1])
        step = pl.program_id(0)
        for ax in range(1, len(grid)):
            step = step * grid[ax] + pl.program_id(ax)
        bases = comm.bases()
        for p, (_, start, finish) in enumerate(comm.parts):
            @pl.when(step == bounds[p])
            def _(p=p, start=start):
                if p > 0:
                    comm.parts[p - 1][2](*cargs, bases[p - 1])
                start(*cargs, bases[p])
        body(*(refs[:a] + refs[b:c] + refs[d:e]))

        @pl.when(step == n_steps - 1)
        def _():
            comm.parts[-1][2](*cargs, bases[-1])

    res = pl.pallas_call(
        fused, name=name, grid=grid, in_specs=in_specs + [ANY] * n_ci, out_specs=out_specs + [ANY] * n_co,
        out_shape=out_shape + comm.out_shape,
        scratch_shapes=scratch_shapes + [pltpu.SemaphoreType.DMA((comm.n_sems,)), pltpu.SemaphoreType.DMA((comm.n_sems,))],
        input_output_aliases={n_in + i: n_out + j for i, j in comm.aliases.items()},
        compiler_params=compiler_params)(*_in_hbm(*args, *comm.ins))
    return res[:n_out], res[n_out:]


def _remote(src, dst, send_sems, recv_sems, idx, to):
    return pltpu.make_async_remote_copy(src_ref=src, dst_ref=dst, send_sem=send_sems.at[idx],
                                        recv_sem=recv_sems.at[idx], device_id=to, device_id_type=MESH_ID)


def _gather_comm(bufs, ici=True, pair=True):
    n = len(bufs)

    def half(ref, k, pc):
        rh = ref.shape[1] // 2
        return ref.at[k, pl.ds(pc * rh, rh), :]

    def ici_start(ins, outs, ss, rs, base):
        x, y, c = _mesh_pos()
        for a in range(n):
            mine = half(outs[a], 2 * x + y, c)
            for j, chip in enumerate(_other_chips(x, y)):
                _remote(mine, mine, ss, rs, base + 3 * a + j, (*chip, c)).start()

    def ici_finish(ins, outs, ss, rs, base):
        x, y, c = _mesh_pos()
        for a in range(n):
            for j, chip in enumerate(_other_chips(x, y)):
                theirs = half(outs[a], 2 * chip[0] + chip[1], c)
                _remote(theirs, theirs, ss, rs, base + 3 * a + j, (*chip, c)).wait()

    def pair_copy(outs, ss, rs, base, a):
        x, y, c = _mesh_pos()
        rh = outs[a].shape[1] // 2
        held = outs[a].at[:, pl.ds(c * rh, rh), :]
        return _remote(held, held, ss, rs, base + a, (x, y, 1 - c))

    def pair_start(ins, outs, ss, rs, base):
        for a in range(n):
            pair_copy(outs, ss, rs, base, a).start()

    def pair_finish(ins, outs, ss, rs, base):
        for a in range(n):
            pair_copy(outs, ss, rs, base, a).wait()

    parts = ([(3 * n, ici_start, ici_finish)] if ici else []) + ([(n, pair_start, pair_finish)] if pair else [])
    return _Comm(bufs, [_out(b.shape, b.dtype) for b in bufs], {a: a for a in range(n)}, parts)


def _merge_comm(*comms):
    ins, shapes, aliases, subs, base = [], [], {}, [], 0
    for cm in comms:
        (n_sems, start, finish), = cm.parts
        i0, o0 = len(ins), len(shapes)
        subs.append((slice(i0, i0 + len(cm.ins)), slice(o0, o0 + len(cm.out_shape)), base, start, finish))
        aliases.update({i0 + i: o0 + j for i, j in cm.aliases.items()})
        ins += cm.ins
        shapes += cm.out_shape
        base += n_sems

    def start_all(ins_r, outs_r, ss, rs, b):
        for si, so, off, start, _ in subs:
            start(ins_r[si], outs_r[so], ss, rs, b + off)

    def finish_all(ins_r, outs_r, ss, rs, b):
        for si, so, off, _, finish in subs:
            finish(ins_r[si], outs_r[so], ss, rs, b + off)

    return _Comm(ins, shapes, aliases, [(base, start_all, finish_all)])


def _own_half_buffers(pieces, dtypes, kc_arr):
    n = len(pieces)

    def body(kc_ref, *refs):
        for a in range(n):
            refs[n + a][0] = refs[a][...].astype(dtypes[a])

    def half(p):
        return p.shape[0] // 2, p.shape[1]

    return pl.pallas_call(
        body, name="own_halves",
        out_shape=[_out((N_CHIP,) + p.shape, dt) for p, dt in zip(pieces, dtypes)],
        grid_spec=pltpu.PrefetchScalarGridSpec(
            num_scalar_prefetch=1, grid=(1,),
            in_specs=[pl.BlockSpec(half(p), lambda i, kc: (kc[1], 0)) for p in pieces],
            out_specs=[pl.BlockSpec((1,) + half(p), lambda i, kc: (kc[0], kc[1], 0)) for p in pieces]),
        compiler_params=_cp(("arbitrary",), 48),
    )(kc_arr, *_in_hbm(*pieces))


def _exchange_comm(grads):
    n = len(grads)

    def copy(ins, outs, ss, rs, base, a):
        x, y, c = _mesh_pos()
        rh = ins[a].shape[1] // 2
        return _remote(ins[a].at[:, pl.ds((1 - c) * rh, rh), :], outs[a], ss, rs, base + a, (x, y, 1 - c))

    def start(ins, outs, ss, rs, base):
        for a in range(n):
            copy(ins, outs, ss, rs, base, a).start()

    def finish(ins, outs, ss, rs, base):
        for a in range(n):
            copy(ins, outs, ss, rs, base, a).wait()

    shapes = [_out((N_CHIP, g.shape[1] // 2, g.shape[2]), g.dtype) for g in grads]
    return _Comm(grads, shapes, {}, [(n, start, finish)])


def _chip_sum(name, g, recv, c_arr):
    _, r, cc = g.shape
    rh = r // 2

    def body(c_ref, g_ref, r_ref, o_ref):
        o_ref[...] = (g_ref[...] + r_ref[...]).astype(BF16)

    return pl.pallas_call(
        body, name=name, out_shape=_out((N_CHIP, rh, cc), BF16),
        grid_spec=pltpu.PrefetchScalarGridSpec(
            num_scalar_prefetch=1, grid=(N_CHIP,),
            in_specs=[pl.BlockSpec((1, rh, cc), lambda j, c_ref: (j, c_ref[0], 0)),
                      pl.BlockSpec((1, rh, cc), lambda j, c_ref: (j, 0, 0))],
            out_specs=pl.BlockSpec((1, rh, cc), lambda j, c_ref: (j, 0, 0))),
        compiler_params=_cp(("arbitrary",), 32),
    )(c_arr, *_in_hbm(g, recv))


def _scatter_comm(sums):
    n = len(sums)

    def copies(ins, outs, ss, rs, base):
        x, y, c = _mesh_pos()
        return [_remote(ins[a].at[2 * chip[0] + chip[1]], outs[a].at[j], ss, rs, base + 3 * a + j, (*chip, c))
                for a in range(n) for j, chip in enumerate(_other_chips(x, y))]

    def start(ins, outs, ss, rs, base):
        for cp in copies(ins, outs, ss, rs, base):
            cp.start()

    def finish(ins, outs, ss, rs, base):
        for cp in copies(ins, outs, ss, rs, base):
            cp.wait()

    shapes = [_out((3,) + s.shape[1:], s.dtype) for s in sums]
    return _Comm(sums, shapes, {}, [(3 * n, start, finish)])


def _total_sum(name, sums, recv3, kc_arr):
    _, rh, cc = sums.shape

    def body(kc_ref, s_ref, r_ref, o_ref):
        t = s_ref[0].astype(F32) + r_ref[0].astype(F32)
        t = t + r_ref[1].astype(F32)
        o_ref[...] = t + r_ref[2].astype(F32)

    return pl.pallas_call(
        body, name=name, out_shape=_out((2 * rh, cc), F32),
        grid_spec=pltpu.PrefetchScalarGridSpec(
            num_scalar_prefetch=1, grid=(1,),
            in_specs=[pl.BlockSpec((1, rh, cc), lambda i, kc_ref: (kc_ref[0], 0, 0)),
                      pl.BlockSpec((3, rh, cc), lambda i, kc_ref: (0, 0, 0))],
            out_specs=pl.BlockSpec((rh, cc), lambda i, kc_ref: (kc_ref[1], 0))),
        compiler_params=_cp(("arbitrary",), 32),
    )(kc_arr, *_in_hbm(sums, recv3))


def _assemble_comm(totals):
    n = len(totals)

    def copy(outs, ss, rs, base, a):
        x, y, c = _mesh_pos()
        rh = outs[a].shape[0] // 2
        here = outs[a].at[pl.ds(c * rh, rh), :]
        return _remote(here, here, ss, rs, base + a, (x, y, 1 - c))

    def start(ins, outs, ss, rs, base):
        for a in range(n):
            copy(outs, ss, rs, base, a).start()

    def finish(ins, outs, ss, rs, base):
        for a in range(n):
            copy(outs, ss, rs, base, a).wait()

    shapes = [_out(t.shape, t.dtype) for t in totals]
    return _Comm(totals, shapes, {a: a for a in range(n)}, [(n, start, finish)])


def _small_allreduce(arrays):
    n = len(arrays)
    shapes = [a.shape for a in arrays]
    narrow_w = 64
    groups = [[a for a in range(n) if shapes[a][1] > narrow_w], [a for a in range(n) if shapes[a][1] <= narrow_w]]
    widths = [max(shapes[a][1] for a in groups[0]), 2 * narrow_w]
    offs, cols, heights = {}, {}, [0, 0]
    for a in groups[0]:
        offs[a], cols[a] = heights[0], 0
        heights[0] += shapes[a][0]
    rows = [-(-heights[0] // 8) * 8]
    heights = [0, 0]
    for a in sorted(groups[1], key=lambda a: -shapes[a][0]):
        side = 0 if heights[0] <= heights[1] else 1
        offs[a], cols[a] = heights[side], side * narrow_w
        heights[side] += shapes[a][0]
    rows.append(-(-max(heights) // 8) * 8)
    n_g = len(groups)

    def window(ref, a):
        return ref.at[offs[a]:offs[a] + shapes[a][0], cols[a]:cols[a] + shapes[a][1]]

    def body(*refs):
        ins, outs = refs[:n], refs[n:2 * n]
        pack, sib, csum, every = (refs[2 * n + i * n_g:2 * n + (i + 1) * n_g] for i in range(4))
        send_sems, recv_sems = refs[2 * n + 4 * n_g:]
        x, y, c = _mesh_pos()
        k = 2 * x + y
        for gi, g in enumerate(groups):
            pack[gi][...] = jnp.zeros_like(pack[gi])
            for a in g:
                window(pack[gi], a)[...] = ins[a][...]
        cps = [_remote(pack[gi], sib[gi], send_sems, recv_sems, gi, (x, y, 1 - c)) for gi in range(n_g)]
        for cp in cps:
            cp.start()
        for cp in cps:
            cp.wait()
        for gi in range(n_g):
            csum[gi][...] = pack[gi][...] + sib[gi][...]
            every[gi][k] = csum[gi][...]
        cps = [_remote(csum[gi], every[gi].at[k], send_sems, recv_sems, n_g + 3 * gi + j, (*chip, c))
               for gi in range(n_g) for j, chip in enumerate(_other_chips(x, y))]
        for cp in cps:
            cp.start()
        for cp in cps:
            cp.wait()
        for gi, g in enumerate(groups):
            pack[gi][...] = ((every[gi][0] + every[gi][1]) + every[gi][2]) + every[gi][3]
            for a in g:
                outs[a][...] = window(pack[gi], a)[...]

    bufs = [pltpu.VMEM((r, w), F32) for r, w in zip(rows, widths)]
    return pl.pallas_call(
        body, name="small_allreduce", grid=(1,), out_shape=[_out(s, F32) for s in shapes],
        in_specs=[_full(s) for s in shapes], out_specs=[_full(s) for s in shapes],
        scratch_shapes=bufs * 3 + [pltpu.VMEM((N_CHIP, r, w), F32) for r, w in zip(rows, widths)] +
                       [pltpu.SemaphoreType.DMA((4 * n_g,)), pltpu.SemaphoreType.DMA((4 * n_g,))],
        compiler_params=_cp(("arbitrary",), 40),
    )(*_in_hbm(*arrays))


def _adamw_small(ws, gs, ms, vs):
    n = len(ws)

    def body(*refs):
        w, g, m, v, d, mo, vo = (refs[i * n:(i + 1) * n] for i in range(7))
        for a in range(n):
            d[a][...], mo[a][...], vo[a][...] = _adamw_math(w[a][...], g[a][...], m[a][...], v[a][...])

    specs = [_full(w.shape) for w in ws]
    res = pl.pallas_call(
        body, name="adamw_small", grid=(1,), out_shape=[_out(w.shape, F32) for w in ws] * 3,
        in_specs=specs * 4, out_specs=specs * 3, compiler_params=_cp(("arbitrary",), 40),
    )(*_in_hbm(*ws, *gs, *ms, *vs))
    return res[:n], res[n:2 * n], res[2 * n:]


def _adamw_math(w, g, m, v):
    m = ADAM_B1 * m + (1.0 - ADAM_B1) * g
    v = ADAM_B2 * v + (1.0 - ADAM_B2) * (g * g)
    m_hat = m / (1.0 - ADAM_B1 ** ADAM_STEP)
    v_hat = v / (1.0 - ADAM_B2 ** ADAM_STEP)
    delta = -ADAM_LR * (m_hat / (jnp.sqrt(v_hat) + ADAM_EPS) + ADAM_WD * w)
    return delta, m, v


def _adamw(name, w, g, m, v):
    r, c = w.shape
    tr = max(t for t in range(8, 513, 8) if r % t == 0)

    def body(w_ref, g_ref, m_ref, v_ref, d_ref, mo_ref, vo_ref):
        d_ref[...], mo_ref[...], vo_ref[...] = _adamw_math(w_ref[...], g_ref[...], m_ref[...], v_ref[...])

    return pl.pallas_call(
        body, name=name, grid=(r // tr,), in_specs=[_rows(tr, c)] * 4, out_specs=[_rows(tr, c)] * 3,
        out_shape=[_out((r, c), F32)] * 3, compiler_params=_cp(("arbitrary",), 32),
    )(*_in_hbm(w, g, m, v))


def _as_matrix(name, a):
    if name == "na_rpb":
        return a[0].transpose(1, 0, 2).reshape(N_HEADS * (2 * KH - 1), 2 * KW - 1)
    if name in ("s5_b_re", "s5_b_im"):
        return a.transpose(0, 1, 2, 4, 3).reshape(2 * S5_G * S5_H, S5_P)
    if name in ("s5_c_re", "s5_c_im"):
        return a.reshape(2 * S5_G * S5_H, S5_P)
    if name in ("s5_lam_re", "s5_lam_im"):
        return a.reshape(2 * S5_G, S5_P)
    if name == "s5_log_dt":
        return a.reshape(2, S5_G)
    return a


def _from_matrix(name, m):
    if name == "na_rpb":
        return m.reshape(2 * KH - 1, N_HEADS, 2 * KW - 1).transpose(1, 0, 2)[None]
    if name in ("s5_b_re", "s5_b_im"):
        return m.reshape(1, 2, S5_G, S5_H, S5_P).transpose(0, 1, 2, 4, 3)
    if name in ("s5_c_re", "s5_c_im"):
        return m.reshape(1, 2, S5_G, S5_H, S5_P)
    if name in ("s5_lam_re", "s5_lam_im"):
        return m.reshape(1, 2, S5_G, S5_P)
    if name == "s5_log_dt":
        return m.reshape(1, 2, S5_G)
    return m


WEIGHTS = ["meta_tokens", "ffn1_pre_g", "ffn1_post_g", "ffn1_w_gate", "ffn1_w_up", "ffn1_w_down", "mix_pre_g", "w_in",
           "na_rpb", "s5_lam_re", "s5_lam_im", "s5_log_dt", "s5_b_re", "s5_b_im", "s5_c_re", "s5_c_im", "s5_d",
           "s5_w_glu", "s5_b_glu", "na_out_g", "s5_out_g", "w_out", "mix_post_g", "ffn2_pre_g", "ffn2_post_g",
           "ffn2_w_gate", "ffn2_w_up", "ffn2_w_down", "final_g"]
BIG = ["ffn1_w_gate", "ffn1_w_up", "ffn1_w_down", "w_in", "s5_w_glu", "w_out", "ffn2_w_gate", "ffn2_w_up",
       "ffn2_w_down"]
TRANSPOSED = ["ffn1_w_gate", "ffn1_w_up", "ffn2_w_gate", "ffn2_w_up"]
GAINS = ["ffn1_pre_g", "ffn1_post_g", "mix_pre_g", "s5_d", "s5_b_glu", "na_out_g", "s5_out_g", "mix_post_g",
         "ffn2_pre_g", "ffn2_post_g", "final_g"]
SMALL = [n for n in WEIGHTS if n not in BIG]


def kernel(*args):
    names = ["x"] + WEIGHTS + ["loss_target"] + ["m_" + n for n in WEIGHTS] + ["v_" + n for n in WEIGHTS]
    assert len(args) == len(names)
    given = dict(zip(names, args))
    x_pos, y_pos, c_pos = _mesh_pos()
    k_pos = 2 * x_pos + y_pos
    c_arr = jnp.reshape(c_pos, (1,)).astype(jnp.int32)
    kc_arr = jnp.stack([k_pos, c_pos]).astype(jnp.int32)

    def piece(name, a):
        return a[0].T if name in TRANSPOSED else a[0]

    def unpiece(name, a):
        return a.T[None] if name in TRANSPOSED else a[None]

    placed = BIG + ["meta_tokens"]
    bufs = dict(zip(placed, _own_half_buffers([piece(n, given[n]) for n in BIG] + [given["meta_tokens"]],
                                              [BF16] * len(BIG) + [F32], kc_arr)))

    gains = {n: given[n] for n in GAINS}
    s5 = {n: _as_matrix("s5_" + n, given["s5_" + n])
          for n in ["lam_re", "lam_im", "log_dt", "b_re", "b_im", "c_re", "c_im"]}
    loss, dh0, pieces, small = _step(given["x"][0], given["loss_target"][0], bufs, gains, s5, given["na_rpb"][0],
                                     c_arr, kc_arr)
    loss = lax.psum(loss, ("x", "y", "c"))
    n_tok = given["x"].shape[1]
    grad_x = dh0[N_META:N_META + n_tok][None]

    small["meta_tokens"] = dh0[:N_META]
    small = dict(zip(SMALL, _small_allreduce([small[n] for n in SMALL])))
    mc = D // N_CHIP
    small["meta_tokens"] = lax.dynamic_slice_in_dim(small["meta_tokens"], k_pos * mc, mc, 1)

    out_g, out_d, out_m, out_v = {}, {}, {}, {}
    for n in BIG:
        g2 = pieces[n]
        d2, m2, v2 = _adamw("adamw_" + n, piece(n, given[n]), g2, piece(n, given["m_" + n]),
                            piece(n, given["v_" + n]))
        out_g[n], out_d[n], out_m[n], out_v[n] = (unpiece(n, t) for t in (g2, d2, m2, v2))
    gs = [small[n] for n in SMALL]
    d2, m2, v2 = _adamw_small([_as_matrix(n, given[n]) for n in SMALL], gs,
                              [_as_matrix(n, given["m_" + n]) for n in SMALL],
                              [_as_matrix(n, given["v_" + n]) for n in SMALL])
    for n, g, dd, mm, vv in zip(SMALL, gs, d2, m2, v2):
        out_g[n], out_d[n], out_m[n], out_v[n] = (_from_matrix(n, t) for t in (g, dd, mm, vv))
    return (loss, grad_x, *[out_g[n] for n in WEIGHTS], *[out_d[n] for n in WEIGHTS],
            *[out_m[n] for n in WEIGHTS], *[out_v[n] for n in WEIGHTS])
```

```python
import functools
import math

import numpy as np
import jax
import jax.numpy as jnp
from jax import lax
from jax.experimental import pallas as pl
from jax.experimental.pallas import tpu as pltpu

F32 = jnp.float32
BF16 = jnp.bfloat16

D = 1024
N_META = 16
GRID_W = 64
NA_W = 512
S5_W = 512
HEAD_DIM = 64
N_HEADS = 8
KH = 8
KW = 16
S5_G = 32
S5_P = 64
S5_H = 16
N_BUNDLE = 4
FF = 2816
N_CHIP = 4
FC = FF // N_CHIP
EPS = 1e-6
NEG_INF = -1e30
Q_ROWS = 4
K_ROWS = 12
QB = Q_ROWS * GRID_W
KB = K_ROWS * GRID_W
SCAN_CHUNK = 256

ADAM_LR = 0.001
ADAM_B1 = 0.9
ADAM_B2 = 0.999
ADAM_EPS = 1e-08
ADAM_WD = 0.01
ADAM_STEP = 10

NT = (((1,), (1,)), ((), ()))
TN = (((0,), (0,)), ((), ()))
MESH_ID = pl.DeviceIdType.MESH


def _cp(sem=None, vmem_mb=None):
    kw = {}
    if sem is not None:
        kw["dimension_semantics"] = sem
    if vmem_mb is not None:
        kw["vmem_limit_bytes"] = vmem_mb << 20
    return pltpu.CompilerParams(**kw)


def _full(shape):
    n = len(shape)
    return pl.BlockSpec(shape, lambda *_: (0,) * n)


def _rows(tm, w):
    return pl.BlockSpec((tm, w), lambda i: (i, 0))


ANY = pl.BlockSpec(memory_space=pl.ANY)


def _rms(x, g):
    r = lax.rsqrt(jnp.mean(x * x, axis=-1, keepdims=True) + EPS)
    return x * r * g


def _rms_bwd(x, g, dy):
    r = lax.rsqrt(jnp.mean(x * x, axis=-1, keepdims=True) + EPS)
    xh = x * r
    dg = jnp.sum(dy * xh, axis=0, keepdims=True)
    dyg = dy * g
    dx = r * (dyg - xh * jnp.mean(dyg * xh, axis=-1, keepdims=True))
    return dx, dg


def _out(shape, dtype):
    return pltpu.HBM(tuple(shape), dtype)


def _in_hbm(*args):
    return [pltpu.with_memory_space_constraint(a, pltpu.HBM) if jnp.issubdtype(a.dtype, jnp.floating) else a
            for a in args]


def _dot(a, b):
    return jnp.dot(a, b, preferred_element_type=F32)


def _dg(a, b, dims):
    return lax.dot_general(a, b, dims, preferred_element_type=F32)


def _ffn_fwd(name, h, g_pre, g_post, wg, wu, wd, tm, comm=None, bounds=()):
    tp = h.shape[0]
    nt = tp // tm

    def body(h_ref, gp_ref, gq_ref, wg_ref, wu_ref, wd_ref, hn_ref, gate_ref, up_ref, f_ref, xn_s, acc_s):
        c = pl.program_id(1)

        @pl.when(c == 0)
        def _():
            xn_s[...] = _rms(h_ref[...], gp_ref[...]).astype(BF16)
            acc_s[...] = jnp.zeros_like(acc_s)

        xn = xn_s[...]
        gate = _dg(xn, wg_ref[0], NT)
        up = _dg(xn, wu_ref[0], NT)
        gate_ref[0] = gate
        up_ref[0] = up
        act = (gate * jax.nn.sigmoid(gate) * up).astype(BF16)
        acc_s[...] += _dot(act, wd_ref[0])

        @pl.when(c == N_CHIP - 1)
        def _():
            f = acc_s[...]
            f_ref[...] = f
            hn_ref[...] = h_ref[...] + 0.5 * _rms(f, gq_ref[...])

    return _call(
        body, comm, bounds, (h, g_pre, g_post, wg, wu, wd), name=name, grid=(nt, N_CHIP),
        in_specs=[pl.BlockSpec((tm, D), lambda i, c: (i, 0)), _full((1, D)), _full((1, D))] +
                 [pl.BlockSpec((1, FC, D), lambda i, c: (c, 0, 0))] * 3,
        out_specs=[pl.BlockSpec((tm, D), lambda i, c: (i, 0)),
                   pl.BlockSpec((1, tm, FC), lambda i, c: (c, i, 0)),
                   pl.BlockSpec((1, tm, FC), lambda i, c: (c, i, 0)),
                   pl.BlockSpec((tm, D), lambda i, c: (i, 0))],
        out_shape=[_out((tp, D), F32), _out((N_CHIP, tp, FC), F32),
                   _out((N_CHIP, tp, FC), F32), _out((tp, D), F32)],
        scratch_shapes=[pltpu.VMEM((tm, D), BF16), pltpu.VMEM((tm, D), F32)],
        compiler_params=_cp(("arbitrary", "arbitrary"), 48))


def _ffn_bwd(name, h, g_pre, df, gate, up, wg, wu, wd, tm):
    tp = h.shape[0]
    nt = tp // tm
    rh = FC // 2

    def body(h_ref, gp_ref, df_ref, gate_ref, up_ref, wg_ref, wu_ref, wd_ref,
             dwg_ref, dwu_ref, dwd_ref, dxn_ref, rg_ref, ru_ref, rd_ref, ag, au, ad, send_sems, recv_sems):
        c = pl.program_id(0)
        i = pl.program_id(1)

        def to_sibling(a, piece):
            x, y, core = _mesh_pos()
            dw_ref, r_ref = ((dwg_ref, rg_ref), (dwu_ref, ru_ref), (dwd_ref, rd_ref))[a]
            return _remote(dw_ref.at[piece, pl.ds((1 - core) * rh, rh), :], r_ref.at[piece], send_sems, recv_sems,
                           3 * piece + a, (x, y, 1 - core))

        @pl.when(i == 0)
        def _():
            ag[...] = jnp.zeros_like(ag)
            au[...] = jnp.zeros_like(au)
            ad[...] = jnp.zeros_like(ad)

        xn = _rms(h_ref[...], gp_ref[...]).astype(BF16)
        dfb = df_ref[...].astype(BF16)
        gt = gate_ref[0]
        u = up_ref[0]
        sg = jax.nn.sigmoid(gt)
        si = gt * sg
        act = (si * u).astype(BF16)
        dact = _dg(dfb, wd_ref[0], NT)
        ad[...] += _dg(act, dfb, TN)
        dgate = (dact * u * (sg * (1.0 + gt * (1.0 - sg)))).astype(BF16)
        dup = (dact * si).astype(BF16)
        ag[...] += _dg(dgate, xn, TN)
        au[...] += _dg(dup, xn, TN)
        dxn_ref[0] = _dot(dgate, wg_ref[0]) + _dot(dup, wu_ref[0])

        @pl.when(i == nt - 1)
        def _():
            pltpu.sync_copy(ag, dwg_ref.at[c])
            pltpu.sync_copy(au, dwu_ref.at[c])
            pltpu.sync_copy(ad, dwd_ref.at[c])
            for a in range(3):
                to_sibling(a, c).start()

        @pl.when((c == N_CHIP - 1) & (i == nt - 1))
        def _():
            for piece in range(N_CHIP):
                for a in range(3):
                    to_sibling(a, piece).wait()

    return pl.pallas_call(
        body, name=name, grid=(N_CHIP, nt),
        in_specs=[pl.BlockSpec((tm, D), lambda c, i: (i, 0)), _full((1, D)),
                  pl.BlockSpec((tm, D), lambda c, i: (i, 0)),
                  pl.BlockSpec((1, tm, FC), lambda c, i: (c, i, 0)),
                  pl.BlockSpec((1, tm, FC), lambda c, i: (c, i, 0))] +
                 [pl.BlockSpec((1, FC, D), lambda c, i: (c, 0, 0))] * 3,
        out_specs=[ANY, ANY, ANY, pl.BlockSpec((1, tm, D), lambda c, i: (c, i, 0)), ANY, ANY, ANY],
        out_shape=[_out((N_CHIP, FC, D), F32)] * 3 + [_out((N_CHIP, tp, D), F32)] +
                  [_out((N_CHIP, rh, D), F32)] * 3,
        scratch_shapes=[pltpu.VMEM((FC, D), F32)] * 3 +
                       [pltpu.SemaphoreType.DMA((3 * N_CHIP,)), pltpu.SemaphoreType.DMA((3 * N_CHIP,))],
        compiler_params=_cp(("arbitrary", "arbitrary"), 58),
    )(*_in_hbm(h, g_pre, df, gate, up, wg, wu, wd))


def _ffn_pre_bwd(name, dh, dxn_part, h, g_pre, tm, comm=None, bounds=()):
    tp = h.shape[0]
    nt = tp // tm

    def body(dh_ref, dxn_ref, h_ref, gp_ref, out_ref, dg_ref):
        i = pl.program_id(0)
        dxn = (dxn_ref[0] + dxn_ref[1]) + (dxn_ref[2] + dxn_ref[3])
        dx, dg = _rms_bwd(h_ref[...], gp_ref[...], dxn)
        out_ref[...] = dh_ref[...] + dx

        @pl.when(i == 0)
        def _():
            dg_ref[...] = jnp.zeros_like(dg_ref)

        dg_ref[...] += dg

    return _call(
        body, comm, bounds, (dh, dxn_part, h, g_pre), name=name, grid=(nt,),
        in_specs=[_rows(tm, D), pl.BlockSpec((N_CHIP, tm, D), lambda i: (0, i, 0)), _rows(tm, D), _full((1, D))],
        out_specs=[_rows(tm, D), _full((1, D))],
        out_shape=[_out((tp, D), F32), _out((1, D), F32)],
        compiler_params=_cp(("arbitrary",), 48))


def _mix_in(h, g, w_in, tm):
    tp = h.shape[0]

    def body(h_ref, g_ref, w_ref, q_ref, k_ref, v_ref, u_ref):
        a = _rms(h_ref[...], g_ref[...]).astype(BF16)
        q_ref[...] = _dot(a, w_ref[0]).astype(BF16)
        k_ref[...] = _dot(a, w_ref[1]).astype(BF16)
        v_ref[...] = _dot(a, w_ref[2]).astype(BF16)
        u_ref[...] = _dot(a, w_ref[3])

    return pl.pallas_call(
        body, name="mix_in", grid=(tp // tm,),
        in_specs=[_rows(tm, D), _full((1, D)), _full((N_CHIP, D, NA_W))],
        out_specs=[_rows(tm, NA_W)] * 4,
        out_shape=[_out((tp, NA_W), BF16)] * 3 + [_out((tp, S5_W), F32)],
        compiler_params=_cp(("arbitrary",), 40),
    )(*_in_hbm(h, g, w_in))


def _gelu(x):
    return jax.nn.gelu(x, approximate=True)


def _gelu_grad(x):
    k = math.sqrt(2.0 / math.pi)
    t = jnp.tanh(k * (x + 0.044715 * x * x * x))
    return 0.5 * (1.0 + t) + 0.5 * x * (1.0 - t * t) * k * (1.0 + 3.0 * 0.044715 * x * x)


def _mix_out(o_na, y_pre, h, w_glu, b_glu, g_na, g_s5, w_out, g_post, tm, comm=None, bounds=()):
    tp = h.shape[0]

    def body(ona_ref, yp_ref, h_ref, wglu_ref, bglu_ref, gna_ref, gs5_ref, wout_ref, gpost_ref, hn_ref, mix_ref):
        y = _gelu(yp_ref[...])
        z = _dot(y.astype(BF16), wglu_ref[...]) + bglu_ref[...]
        o_s5 = y * jax.nn.sigmoid(z)
        n1 = _rms(ona_ref[...], gna_ref[...]).astype(BF16)
        n2 = _rms(o_s5, gs5_ref[...]).astype(BF16)
        mix = _dot(n1, wout_ref[0:NA_W, :]) + _dot(n2, wout_ref[NA_W:, :])
        mix_ref[...] = mix
        hn_ref[...] = h_ref[...] + _rms(mix, gpost_ref[...])

    return _call(
        body, comm, bounds, (o_na, y_pre, h, w_glu, b_glu, g_na, g_s5, w_out, g_post), name="mix_out",
        grid=(tp // tm,),
        in_specs=[_rows(tm, NA_W), _rows(tm, S5_W), _rows(tm, D), _full((S5_W, S5_W)), _full((1, S5_W)),
                  _full((1, NA_W)), _full((1, S5_W)), _full((D, D)), _full((1, D))],
        out_specs=[_rows(tm, D), _rows(tm, D)],
        out_shape=[_out((tp, D), F32)] * 2,
        compiler_params=_cp(("arbitrary",), 40))


def _mix_out_bwd(dh, mix, o_na, y_pre, w_glu, b_glu, g_na, g_s5, w_out, g_post, tm):
    tp = dh.shape[0]
    nt = tp // tm

    def body(dh_ref, mix_ref, ona_ref, yp_ref, wglu_ref, bglu_ref, gna_ref, gs5_ref, wout_ref, gpost_ref,
             dona_ref, dyp_ref, dwout_ref, dwglu_ref, dgpost_ref, dgna_ref, dgs5_ref, dbglu_ref, a_out, a_glu):
        i = pl.program_id(0)

        @pl.when(i == 0)
        def _():
            a_out[...] = jnp.zeros_like(a_out)
            a_glu[...] = jnp.zeros_like(a_glu)
            dgpost_ref[...] = jnp.zeros_like(dgpost_ref)
            dgna_ref[...] = jnp.zeros_like(dgna_ref)
            dgs5_ref[...] = jnp.zeros_like(dgs5_ref)
            dbglu_ref[...] = jnp.zeros_like(dbglu_ref)

        dmix, dgpost = _rms_bwd(mix_ref[...], gpost_ref[...], dh_ref[...])
        dgpost_ref[...] += dgpost
        yp = yp_ref[...]
        y = _gelu(yp)
        yb = y.astype(BF16)
        z = _dot(yb, wglu_ref[...]) + bglu_ref[...]
        sg = jax.nn.sigmoid(z)
        o_s5 = y * sg
        o_na = ona_ref[...]
        n1 = _rms(o_na, gna_ref[...]).astype(BF16)
        n2 = _rms(o_s5, gs5_ref[...]).astype(BF16)
        dmb = dmix.astype(BF16)
        a_out[0:NA_W, :] += _dg(n1, dmb, TN)
        a_out[NA_W:, :] += _dg(n2, dmb, TN)
        dn1 = _dg(dmb, wout_ref[0:NA_W, :], NT)
        dn2 = _dg(dmb, wout_ref[NA_W:, :], NT)
        dona, dgna = _rms_bwd(o_na, gna_ref[...], dn1)
        dona_ref[...] = dona
        dgna_ref[...] += dgna
        dos5, dgs5 = _rms_bwd(o_s5, gs5_ref[...], dn2)
        dgs5_ref[...] += dgs5
        dz = dos5 * y * (sg * (1.0 - sg))
        dbglu_ref[...] += jnp.sum(dz, axis=0, keepdims=True)
        dzb = dz.astype(BF16)
        a_glu[...] += _dg(yb, dzb, TN)
        dy = dos5 * sg + _dg(dzb, wglu_ref[...], NT)
        dyp_ref[...] = dy * _gelu_grad(yp)

        @pl.when(i == nt - 1)
        def _():
            pltpu.sync_copy(a_out, dwout_ref)
            pltpu.sync_copy(a_glu, dwglu_ref)

    return pl.pallas_call(
        body, name="mix_out_bwd", grid=(nt,),
        in_specs=[_rows(tm, D), _rows(tm, D), _rows(tm, NA_W), _rows(tm, S5_W), _full((S5_W, S5_W)),
                  _full((1, S5_W)), _full((1, NA_W)), _full((1, S5_W)), _full((D, D)), _full((1, D))],
        out_specs=[_rows(tm, NA_W), _rows(tm, S5_W), ANY, ANY, _full((1, D)), _full((1, NA_W)),
                   _full((1, S5_W)), _full((1, S5_W))],
        out_shape=[_out((tp, NA_W), F32), _out((tp, S5_W), F32),
                   _out((D, D), F32), _out((S5_W, S5_W), F32),
                   _out((1, D), F32), _out((1, NA_W), F32),
                   _out((1, S5_W), F32), _out((1, S5_W), F32)],
        scratch_shapes=[pltpu.VMEM((D, D), F32), pltpu.VMEM((S5_W, S5_W), F32)],
        compiler_params=_cp(("arbitrary",), 48),
    )(*_in_hbm(dh, mix, o_na, y_pre, w_glu, b_glu, g_na, g_s5, w_out, g_post))


def _mix_in_bwd(dq, dk, dv, du, h, g, w_in, dh, f1, g_post1, tm, comm=None, bounds=()):
    tp = h.shape[0]
    nt = tp // tm

    def body(dq_ref, dk_ref, dv_ref, du_ref, h_ref, g_ref, w_ref, dh_ref, f_ref, gq_ref,
             dh1_ref, df_ref, dw_ref, dg_ref, dgq_ref, acc):
        i = pl.program_id(0)

        @pl.when(i == 0)
        def _():
            acc[...] = jnp.zeros_like(acc)
            dg_ref[...] = jnp.zeros_like(dg_ref)
            dgq_ref[...] = jnp.zeros_like(dgq_ref)

        x = h_ref[...]
        a = _rms(x, g_ref[...]).astype(BF16)
        da = jnp.zeros((tm, D), F32)
        for j, r in enumerate((dq_ref, dk_ref, dv_ref, du_ref)):
            dp = r[...].astype(BF16)
            da = da + _dg(dp, w_ref[j], NT)
            acc[j] += _dg(a, dp, TN)
        dx, dg = _rms_bwd(x, g_ref[...], da)
        dh1 = dh_ref[...] + dx
        dh1_ref[...] = dh1
        dg_ref[...] += dg
        df, dgq = _rms_bwd(f_ref[...], gq_ref[...], 0.5 * dh1)
        df_ref[...] = df
        dgq_ref[...] += dgq

        @pl.when(i == nt - 1)
        def _():
            pltpu.sync_copy(acc, dw_ref)

    return _call(
        body, comm, bounds, (dq, dk, dv, du, h, g, w_in, dh, f1, g_post1), name="mix_in_bwd", grid=(nt,),
        in_specs=[_rows(tm, NA_W)] * 4 + [_rows(tm, D), _full((1, D)), _full((N_CHIP, D, NA_W)), _rows(tm, D),
                                         _rows(tm, D), _full((1, D))],
        out_specs=[_rows(tm, D), _rows(tm, D), ANY, _full((1, D)), _full((1, D))],
        out_shape=[_out((tp, D), F32), _out((tp, D), F32),
                   _out((N_CHIP, D, NA_W), F32), _out((1, D), F32),
                   _out((1, D), F32)],
        scratch_shapes=[pltpu.VMEM((N_CHIP, D, NA_W), F32)],
        compiler_params=_cp(("arbitrary",), 48))


def _final_loss(h, g_final, target, f2, g_post2, n_tok, tm):
    tp = h.shape[0]

    def body(h_ref, g_ref, t_ref, f_ref, gq_ref, dh_ref, df_ref, loss_ref, dg_ref, dgq_ref):
        i = pl.program_id(0)

        @pl.when(i == 0)
        def _():
            loss_ref[...] = jnp.zeros_like(loss_ref)
            dg_ref[...] = jnp.zeros_like(dg_ref)
            dgq_ref[...] = jnp.zeros_like(dgq_ref)

        x = h_ref[...]
        y = _rms(x, g_ref[...])
        row = i * tm + lax.broadcasted_iota(jnp.int32, (tm, 1), 0)
        valid = (row >= N_META) & (row < N_META + n_tok)
        e = jnp.where(valid, y - t_ref[...], 0.0)
        loss_ref[...] += 0.5 * jnp.sum(jnp.mean(e * e, axis=-1, keepdims=True), axis=0, keepdims=True)
        dx, dg = _rms_bwd(x, g_ref[...], e * (1.0 / D))
        dh_ref[...] = dx
        dg_ref[...] += dg
        df, dgq = _rms_bwd(f_ref[...], gq_ref[...], 0.5 * dx)
        df_ref[...] = df
        dgq_ref[...] += dgq

    return pl.pallas_call(
        body, name="final_loss", grid=(tp // tm,),
        in_specs=[_rows(tm, D), _full((1, D)), _rows(tm, D), _rows(tm, D), _full((1, D))],
        out_specs=[_rows(tm, D), _rows(tm, D), _full((1, 1)), _full((1, D)), _full((1, D))],
        out_shape=[_out((tp, D), F32), _out((tp, D), F32),
                   _out((1, 1), F32), _out((1, D), F32),
                   _out((1, D), F32)],
        compiler_params=_cp(("arbitrary",), 40),
    )(*_in_hbm(h, g_final, target, f2, g_post2))


def _na_patterns(n_rows):
    pats = []
    for kind in range(3):
        pat = [[-1] * K_ROWS for _ in range(Q_ROWS)]
        for i in range(Q_ROWS):
            for jj in range(K_ROWS):
                if kind == 0 and jj < KH:
                    pat[i][jj] = jj - i + KH - 1
                elif kind == 1 and i <= jj < i + KH:
                    pat[i][jj] = jj - i + 3
                elif kind == 2 and K_ROWS - KH <= jj:
                    pat[i][jj] = jj - i - 1
        pats.append(pat)
    return pats


def _diag_onehot():
    q = np.arange(GRID_W)[:, None]
    kc = np.arange(GRID_W)[None, :]
    start = np.clip(q - KW // 2, 0, GRID_W - KW)
    col_in = (kc >= start) & (kc < start + KW)
    e = np.zeros((32, GRID_W, GRID_W), np.float32)
    for d in range(2 * KW - 1):
        e[d] = ((kc - q + KW - 1) == d) & col_in
    return e.reshape(32, GRID_W * GRID_W), col_in


def _rpb_collapse(dtb2, et):
    def body(d_ref, e_ref, o_ref):
        o_ref[...] = jnp.dot(d_ref[...], e_ref[...], preferred_element_type=F32, precision=lax.Precision.HIGHEST)

    out = (dtb2.shape[0], et.shape[1])
    return pl.pallas_call(
        body, name="rpb_collapse", grid=(1,), out_shape=_out(out, F32),
        in_specs=[_full(dtb2.shape), _full(et.shape)], out_specs=_full(out),
    )(*_in_hbm(dtb2, et))


def _bias_tables(rpb, n_rows):
    n_dr, n_dc = 2 * KH - 1, 2 * KW - 1
    pats = _na_patterns(n_rows)

    def body(rpb_ref, o_ref):
        h = pl.program_id(0)
        q = lax.broadcasted_iota(jnp.int32, (GRID_W, GRID_W), 0)
        kc = lax.broadcasted_iota(jnp.int32, (GRID_W, GRID_W), 1)
        start = jnp.clip(q - KW // 2, 0, GRID_W - KW)
        col_in = (kc >= start) & (kc < start + KW)
        diff = kc - q + (KW - 1)
        neg = jnp.full((GRID_W, GRID_W), NEG_INF, F32)
        band = []
        for dr in range(n_dr):
            acc = neg
            for d in range(n_dc):
                acc = jnp.where((diff == d) & col_in, rpb_ref[(h * n_dr + dr) * n_dc + d], acc)
            band.append(acc)
        for kind, pat in enumerate(pats):
            for i in range(Q_ROWS):
                for jj in range(K_ROWS):
                    o_ref[kind, 0, i * GRID_W:(i + 1) * GRID_W, jj * GRID_W:(jj + 1) * GRID_W] = (
                        band[pat[i][jj]] if pat[i][jj] >= 0 else neg)

    return pl.pallas_call(
        body, name="bias_tables", grid=(N_HEADS,),
        in_specs=[pl.BlockSpec(memory_space=pltpu.SMEM)],
        out_specs=pl.BlockSpec((3, 1, QB, KB), lambda h: (0, h, 0, 0)),
        out_shape=_out((3, N_HEADS, QB, KB), F32),
        compiler_params=_cp(("arbitrary",), 32),
    )(rpb.reshape(-1))


def _attn_geometry(n_tok):
    n_rows = n_tok // GRID_W
    assert n_rows % Q_ROWS == 0 and n_rows >= K_ROWS
    return n_rows, n_rows // Q_ROWS


def _attn_probs(qh, kh, kmh, bias, scale):
    s = _dg(qh, kh, NT) * scale + bias
    sm = _dg(qh, kmh, NT) * scale
    m = jnp.maximum(jnp.max(s, axis=-1, keepdims=True), jnp.max(sm, axis=-1, keepdims=True))
    p = jnp.exp(s - m)
    pm = jnp.exp(sm - m)
    inv = 1.0 / (jnp.sum(p, axis=-1, keepdims=True) + jnp.sum(pm, axis=-1, keepdims=True))
    return p * inv, pm * inv


def _meta_probs(qmh, kmh, scale):
    s = _dg(qmh, kmh, NT) * scale
    p = jnp.exp(s - jnp.max(s, axis=-1, keepdims=True))
    return p / jnp.sum(p, axis=-1, keepdims=True)


def _step_rows(r, n_rows):
    q0 = pl.multiple_of(N_META + r * QB, 16)
    k0 = pl.multiple_of(N_META + jnp.clip(Q_ROWS * r - (K_ROWS - KH), 0, n_rows - K_ROWS) * GRID_W, 16)
    return q0, k0


def _attn_fwd(q, k, v, bias, n_tok, comm=None, bounds=()):
    tp = q.shape[0]
    n_rows, n_steps = _attn_geometry(n_tok)
    scale = HEAD_DIM ** -0.5

    def body(q_ref, k_ref, v_ref, b_ref, o_ref):
        r = pl.program_id(1)
        km = k_ref[0:N_META, :]
        vm = v_ref[0:N_META, :]

        @pl.when(r == 0)
        def _():
            qm = q_ref[0:N_META, :]
            outs = []
            for hh in range(2):
                sl = slice(hh * HEAD_DIM, (hh + 1) * HEAD_DIM)
                p = _meta_probs(qm[:, sl], km[:, sl], scale)
                outs.append(_dot(p.astype(BF16), vm[:, sl]))
            o_ref[0:N_META, :] = jnp.concatenate(outs, axis=1)
            o_ref[N_META + n_tok:, :] = jnp.zeros((tp - N_META - n_tok, 2 * HEAD_DIM), F32)

        q0, k0 = _step_rows(r, n_rows)
        qb = q_ref[pl.ds(q0, QB), :]
        kb = k_ref[pl.ds(k0, KB), :]
        vb = v_ref[pl.ds(k0, KB), :]
        outs = []
        for hh in range(2):
            sl = slice(hh * HEAD_DIM, (hh + 1) * HEAD_DIM)
            p, pm = _attn_probs(qb[:, sl], kb[:, sl], km[:, sl], b_ref[0, hh], scale)
            outs.append(_dot(p.astype(BF16), vb[:, sl]) + _dot(pm.astype(BF16), vm[:, sl]))
        o_ref[pl.ds(q0, QB), :] = jnp.concatenate(outs, axis=1)

    def bias_map(hp, r):
        return (jnp.where(r == 0, 0, jnp.where(r == n_steps - 1, 2, 1)), hp, 0, 0)

    col = pl.BlockSpec((tp, 2 * HEAD_DIM), lambda hp, r: (0, hp))
    return _call(
        body, comm, bounds, (q, k, v, bias), name="attn_fwd", grid=(N_HEADS // 2, n_steps),
        in_specs=[col, col, col, pl.BlockSpec((1, 2, QB, KB), bias_map)],
        out_specs=[col], out_shape=[_out((tp, NA_W), F32)],
        compiler_params=_cp(("arbitrary", "arbitrary"), 40))


def _attn_bwd(q, k, v, bias, do, n_tok, comm=None, bounds=()):
    tp = q.shape[0]
    n_rows, n_steps = _attn_geometry(n_tok)
    scale = HEAD_DIM ** -0.5
    pats = _na_patterns(n_rows)

    def body(q_ref, k_ref, v_ref, b_ref, do_ref, dq_ref, dk_ref, dv_ref, dtb_ref):
        r = pl.program_id(1)
        km = k_ref[0:N_META, :]
        vm = v_ref[0:N_META, :]

        @pl.when(r == 0)
        def _():
            dk_ref[...] = jnp.zeros_like(dk_ref)
            dv_ref[...] = jnp.zeros_like(dv_ref)
            dtb_ref[...] = jnp.zeros_like(dtb_ref)
            dq_ref[N_META + n_tok:, :] = jnp.zeros((tp - N_META - n_tok, 2 * HEAD_DIM), F32)
            qm = q_ref[0:N_META, :]
            dom = do_ref[0:N_META, :].astype(BF16)
            dqs, dks, dvs = [], [], []
            for hh in range(2):
                sl = slice(hh * HEAD_DIM, (hh + 1) * HEAD_DIM)
                p = _meta_probs(qm[:, sl], km[:, sl], scale)
                dp = _dg(dom[:, sl], vm[:, sl], NT)
                ds = (p * (dp - jnp.sum(dp * p, axis=-1, keepdims=True))).astype(BF16)
                dvs.append(_dg(p.astype(BF16), dom[:, sl], TN))
                dqs.append(_dot(ds, km[:, sl]) * scale)
                dks.append(_dg(ds, qm[:, sl], TN) * scale)
            dq_ref[0:N_META, :] = jnp.concatenate(dqs, axis=1)
            dk_ref[0:N_META, :] += jnp.concatenate(dks, axis=1)
            dv_ref[0:N_META, :] += jnp.concatenate(dvs, axis=1)

        q0, k0 = _step_rows(r, n_rows)
        qb = q_ref[pl.ds(q0, QB), :]
        kb = k_ref[pl.ds(k0, KB), :]
        vb = v_ref[pl.ds(k0, KB), :]
        dob = do_ref[pl.ds(q0, QB), :].astype(BF16)
        dqs, dks, dvs, dkms, dvms, dss = [], [], [], [], [], []
        for hh in range(2):
            sl = slice(hh * HEAD_DIM, (hh + 1) * HEAD_DIM)
            qh, kh, vh, kmh, vmh, doh = qb[:, sl], kb[:, sl], vb[:, sl], km[:, sl], vm[:, sl], dob[:, sl]
            p, pm = _attn_probs(qh, kh, kmh, b_ref[0, hh], scale)
            dp = _dg(doh, vh, NT)
            dpm = _dg(doh, vmh, NT)
            delta = jnp.sum(dp * p, axis=-1, keepdims=True) + jnp.sum(dpm * pm, axis=-1, keepdims=True)
            ds = p * (dp - delta)
            dsb = ds.astype(BF16)
            dsmb = (pm * (dpm - delta)).astype(BF16)
            dss.append(ds)
            dvs.append(_dg(p.astype(BF16), doh, TN))
            dvms.append(_dg(pm.astype(BF16), doh, TN))
            dqs.append((_dot(dsb, kh) + _dot(dsmb, kmh)) * scale)
            dks.append(_dg(dsb, qh, TN) * scale)
            dkms.append(_dg(dsmb, qh, TN) * scale)
        dq_ref[pl.ds(q0, QB), :] = jnp.concatenate(dqs, axis=1)
        dk_ref[pl.ds(k0, KB), :] += jnp.concatenate(dks, axis=1)
        dv_ref[pl.ds(k0, KB), :] += jnp.concatenate(dvs, axis=1)
        dk_ref[0:N_META, :] += jnp.concatenate(dkms, axis=1)
        dv_ref[0:N_META, :] += jnp.concatenate(dvms, axis=1)

        def add_bias_grad(pat):
            for hh in range(2):
                for i in range(Q_ROWS):
                    for jj in range(K_ROWS):
                        if pat[i][jj] >= 0:
                            dtb_ref[hh, pat[i][jj]] += dss[hh][i * GRID_W:(i + 1) * GRID_W,
                                                               jj * GRID_W:(jj + 1) * GRID_W]

        @pl.when(r == 0)
        def _():
            add_bias_grad(pats[0])

        @pl.when((r > 0) & (r < n_steps - 1))
        def _():
            add_bias_grad(pats[1])

        @pl.when(r == n_steps - 1)
        def _():
            add_bias_grad(pats[2])

    def bias_map(hp, r):
        return (jnp.where(r == 0, 0, jnp.where(r == n_steps - 1, 2, 1)), hp, 0, 0)

    col = pl.BlockSpec((tp, 2 * HEAD_DIM), lambda hp, r: (0, hp))
    n_dr = 2 * KH - 1
    return _call(
        body, comm, bounds, (q, k, v, bias, do), name="attn_bwd", grid=(N_HEADS // 2, n_steps),
        in_specs=[col, col, col, pl.BlockSpec((1, 2, QB, KB), bias_map), col],
        out_specs=[col, col, col, pl.BlockSpec((2, n_dr, GRID_W, GRID_W), lambda hp, r: (hp, 0, 0, 0))],
        out_shape=[_out((tp, NA_W), F32)] * 3 +
                  [_out((N_HEADS, n_dr, GRID_W, GRID_W), F32)],
        compiler_params=_cp(("arbitrary", "arbitrary"), 48))


def _repeat_onehot():
    return np.repeat(np.eye(2 * S5_G, dtype=np.float32), S5_H, axis=0)


def _s5_disc_math(lam_re, lam_im, log_dt, b_re, b_im, rep):
    dt = jnp.exp(log_dt)
    ea = jnp.exp(lam_re * dt)
    a_re = ea * jnp.cos(lam_im * dt)
    a_im = ea * jnp.sin(lam_im * dt)
    den = lam_re * lam_re + lam_im * lam_im
    c_re = ((a_re - 1.0) * lam_re + a_im * lam_im) / den
    c_im = (a_im * lam_re - (a_re - 1.0) * lam_im) / den
    ce_re = jnp.dot(rep, c_re, preferred_element_type=F32, precision=lax.Precision.HIGHEST)
    ce_im = jnp.dot(rep, c_im, preferred_element_type=F32, precision=lax.Precision.HIGHEST)
    return a_re, a_im, ce_re * b_re - ce_im * b_im, ce_re * b_im + ce_im * b_re


def _s5_blocks():
    gl = S5_G // N_BUNDLE
    half = gl * S5_P
    out = []
    for d in range(2):
        for g in range(S5_G):
            b, k = divmod(g, gl)
            dg = d * S5_G + g
            out.append((d, b, slice(k * S5_H, (k + 1) * S5_H), slice(k * S5_P, (k + 1) * S5_P),
                        slice(half + k * S5_P, half + (k + 1) * S5_P), slice(dg * S5_H, (dg + 1) * S5_H),
                        slice(dg, dg + 1)))
    return out


def _s5_params(lam_re, lam_im, log_dt, b_re, b_im, c_re, c_im):
    cw, sw = S5_W // N_BUNDLE, 2 * (S5_G // N_BUNDLE) * S5_P

    def body(lr, li, ld, br, bi, cr, ci, rep_ref, a_ref, a1_ref, a2_ref, bm_ref, cm_ref):
        a_re, a_im, bb_re, bb_im = _s5_disc_math(lr[...], li[...], ld[...], br[...], bi[...], rep_ref[...])
        cc_re = cr[...]
        cc_im = ci[...]
        bm_ref[...] = jnp.zeros_like(bm_ref)
        cm_ref[...] = jnp.zeros_like(cm_ref)
        for d, b, rows, re, im, nat, one in _s5_blocks():
            bm_ref[d, b, rows, re] = bb_re[nat, :].astype(BF16)
            bm_ref[d, b, rows, im] = bb_im[nat, :].astype(BF16)
            cm_ref[d, b, rows, re] = cc_re[nat, :].astype(BF16)
            cm_ref[d, b, rows, im] = (-cc_im[nat, :]).astype(BF16)
            a_ref[d, b, :, re] = a_re[one, :]
            a_ref[d, b, :, im] = a_im[one, :]
            k = rows.start // S5_H
            lanes = slice((k % 2) * S5_P, (k % 2 + 1) * S5_P)
            for part, (v1, v2) in enumerate(((a_re[one, :], a_im[one, :]), (a_re[one, :], -a_im[one, :]))):
                sub = slice(4 * part + k // 2, 4 * part + k // 2 + 1)
                a1_ref[d, b, sub, lanes] = v1
                a2_ref[d, b, sub, lanes] = v2

    args = (lam_re, lam_im, log_dt, b_re, b_im, c_re, c_im, jnp.asarray(_repeat_onehot()))
    outs = [((2, N_BUNDLE, 1, sw), F32)] + [((2, N_BUNDLE, 8, 128), F32)] * 2 + [((2, N_BUNDLE, cw, sw), BF16)] * 2
    return pl.pallas_call(
        body, name="s5_params", grid=(1,), in_specs=[_full(a.shape) for a in args],
        out_specs=[_full(s) for s, _ in outs], out_shape=[_out(s, dt) for s, dt in outs],
    )(*_in_hbm(*args))


def _s5_params_bwd(lam_re, lam_im, log_dt, b_re, b_im, da, dbm, dcm):
    n, nb = 2 * S5_G, 2 * S5_G * S5_H

    def body(lr, li, ld, br, bi, rep_ref, da_ref, dbm_ref, dcm_ref, o_lr, o_li, o_ld, o_br, o_bi, o_cr, o_ci,
             dar_s, dai_s, dbr_s, dbi_s):
        for d, b, rows, re, im, nat, one in _s5_blocks():
            dbr_s[nat, :] = dbm_ref[d, b, rows, re]
            dbi_s[nat, :] = dbm_ref[d, b, rows, im]
            o_cr[nat, :] = dcm_ref[d, b, rows, re]
            o_ci[nat, :] = -dcm_ref[d, b, rows, im]
            dar_s[one, :] = da_ref[d, b, :, re]
            dai_s[one, :] = da_ref[d, b, :, im]
        rep = rep_ref[...]
        _, vjp = jax.vjp(lambda p, q, r, s, t: _s5_disc_math(p, q, r, s, t, rep),
                         lr[...], li[...], ld[...], br[...], bi[...])
        o_lr[...], o_li[...], o_ld[...], o_br[...], o_bi[...] = vjp((dar_s[...], dai_s[...], dbr_s[...], dbi_s[...]))

    args = (lam_re, lam_im, log_dt, b_re, b_im, jnp.asarray(_repeat_onehot()), da, dbm, dcm)
    outs = [(n, S5_P)] * 2 + [(n, 1)] + [(nb, S5_P)] * 4
    return pl.pallas_call(
        body, name="s5_params_bwd", grid=(1,), in_specs=[_full(a.shape) for a in args],
        out_specs=[_full(s) for s in outs], out_shape=[_out(s, F32) for s in outs],
        scratch_shapes=[pltpu.VMEM((n, S5_P), F32)] * 2 + [pltpu.VMEM((nb, S5_P), F32)] * 2,
    )(*_in_hbm(*args))


def _tiles_store(ref, base, val):
    for i in range(val.shape[0] // 8):
        for c in range(8):
            ref[pl.ds(base + (8 * i + c) * 8, 8), :] = val[8 * i:8 * i + 8, 128 * c:128 * (c + 1)]


def _tiles_load(ref, base, n):
    return jnp.concatenate(
        [jnp.concatenate([ref[pl.ds(base + (8 * i + c) * 8, 8), :] for c in range(8)], axis=1) for i in range(n // 8)],
        axis=0)


def _time_rows(base, t):
    return pl.ds(base + (t // 8) * 64 + t % 8, 8, stride=8)


def _scan(chains, n):
    xs = [c["x"] for c in chains]
    for k in range(n):
        for ci, c in enumerate(chains):
            t = n - 1 - k if c["reverse"] else k
            if c["prev"] is not None:
                c["prev"][_time_rows(c["prev_base"], t), :] = xs[ci]
            xs[ci] = c["a1"] * xs[ci] + pltpu.roll(c["a2"] * xs[ci], 4, axis=0) + c["src"][_time_rows(0, t), :]
            if c["dst"] is not None:
                c["dst"][_time_rows(0, t), :] = xs[ci]
    return xs


def _chain(x, a1, a2, src, dst=None, prev=None, prev_base=0, reverse=False):
    return dict(x=x, a1=a1, a2=a2, src=src, dst=dst, prev=prev, prev_base=prev_base, reverse=reverse)


def _s5_fwd(u, d_skip, a1, a2, bm, cm, length, comm=None, bounds=()):
    tp = u.shape[0]
    cw = S5_W // N_BUNDLE
    sw = bm.shape[-1]
    n_full, n_tail = divmod(length, SCAN_CHUNK)
    t_tail = n_full * SCAN_CHUNK

    def body(u_ref, d_ref, a1_ref, a2_ref, bm_ref, cm_ref, y_ref, in_f, in_b, xs_f, xs_b):
        y_ref[...] = u_ref[...] * d_ref[...]
        ins, xss = (in_f, in_b), (xs_f, xs_b)

        def load(dr, t0, n):
            _tiles_store(ins[dr], 0, _dot(u_ref[pl.ds(t0, n), :].astype(BF16), bm_ref[dr, 0]))

        def chain(dr, x):
            return _chain(x, a1_ref[dr, 0], a2_ref[dr, 0], ins[dr], dst=xss[dr], reverse=dr == 1)

        def emit(dr, t0, n):
            y_ref[pl.ds(t0, n), :] += _dg(_tiles_load(xss[dr], 0, n).astype(BF16), cm_ref[dr, 0], NT)

        zero = jnp.zeros((8, 128), F32)
        xb = zero
        if n_tail:
            load(1, t_tail, n_tail)
            xb, = _scan([chain(1, xb)], n_tail)
            emit(1, t_tail, n_tail)

        def pair(i, carry):
            t0s = (pl.multiple_of(i * SCAN_CHUNK, SCAN_CHUNK), pl.multiple_of((n_full - 1 - i) * SCAN_CHUNK, SCAN_CHUNK))
            for dr in range(2):
                load(dr, t0s[dr], SCAN_CHUNK)
            carry = _scan([chain(dr, carry[dr]) for dr in range(2)], SCAN_CHUNK)
            for dr in range(2):
                emit(dr, t0s[dr], SCAN_CHUNK)
            return tuple(carry)

        xf, _ = lax.fori_loop(0, n_full, pair, (zero, xb))
        if n_tail:
            load(0, t_tail, n_tail)
            _scan([chain(0, xf)], n_tail)
            emit(0, t_tail, n_tail)

    tile = pl.BlockSpec((2, 1, 8, 128), lambda b: (0, b, 0, 0))
    return _call(
        body, comm, bounds, (u, d_skip, a1, a2, bm, cm), name="s5_fwd", grid=(N_BUNDLE,),
        in_specs=[pl.BlockSpec((tp, cw), lambda b: (0, b)), pl.BlockSpec((1, cw), lambda b: (0, b)), tile, tile,
                  pl.BlockSpec((2, 1, cw, sw), lambda b: (0, b, 0, 0)),
                  pl.BlockSpec((2, 1, cw, sw), lambda b: (0, b, 0, 0))],
        out_specs=[pl.BlockSpec((tp, cw), lambda b: (0, b))],
        out_shape=[_out((tp, S5_W), F32)],
        scratch_shapes=[pltpu.VMEM((SCAN_CHUNK * 8, 128), F32)] * 4,
        compiler_params=_cp(("arbitrary",), 40))


def _s5_bwd(u, dy, d_skip, a, a1, a2, bm, cm, length):
    tp = u.shape[0]
    cw = S5_W // N_BUNDLE
    sw = bm.shape[-1]
    half = sw // 2
    n_full, n_tail = divmod(length, SCAN_CHUNK)
    t_tail = n_full * SCAN_CHUNK

    def body(u_ref, dy_ref, d_ref, a_ref, a1_ref, a2_ref, bm_ref, cm_ref, du_ref, dd_ref, dbm_ref, dcm_ref, da_ref,
             in_f, in_b, g_f, g_b, xp_f, xp_b):
        du_ref[...] = dy_ref[...] * d_ref[...]
        dd_ref[...] = jnp.sum(dy_ref[...] * u_ref[...], axis=0, keepdims=True)
        dbm_ref[...] = jnp.zeros_like(dbm_ref)
        dcm_ref[...] = jnp.zeros_like(dcm_ref)
        da_ref[...] = jnp.zeros_like(da_ref)
        ins, gs, xps = (in_f, in_b), (g_f, g_b), (xp_f, xp_b)
        zero = jnp.zeros((8, 128), F32)

        def both(i):
            return (pl.multiple_of(i * SCAN_CHUNK, SCAN_CHUNK), pl.multiple_of((n_full - 1 - i) * SCAN_CHUNK, SCAN_CHUNK))

        def load_u(dr, t0, n):
            _tiles_store(ins[dr], 0, _dot(u_ref[pl.ds(t0, n), :].astype(BF16), bm_ref[dr, 0]))

        def fwd(dr, t0, x):
            return _chain(x, a1_ref[dr, 0], a2_ref[dr, 0], ins[dr], prev=xps[dr], prev_base=t0 * 8, reverse=dr == 1)

        xb = zero
        if n_tail:
            load_u(1, t_tail, n_tail)
            xb, = _scan([fwd(1, t_tail, xb)], n_tail)

        def fwd_pair(i, carry):
            t0s = both(i)
            for dr in range(2):
                load_u(dr, t0s[dr], SCAN_CHUNK)
            return tuple(_scan([fwd(dr, t0s[dr], carry[dr]) for dr in range(2)], SCAN_CHUNK))

        xf, _ = lax.fori_loop(0, n_full, fwd_pair, (zero, xb))
        if n_tail:
            load_u(0, t_tail, n_tail)
            _scan([fwd(0, t_tail, xf)], n_tail)

        def load_dy(dr, t0, n):
            _tiles_store(ins[dr], 0, _dot(dy_ref[pl.ds(t0, n), :].astype(BF16), cm_ref[dr, 0]))

        def adj(dr, g):
            return _chain(g, a1_ref[dr, 0], -a2_ref[dr, 0], ins[dr], dst=gs[dr], reverse=dr == 0)

        def emit(dr, t0, n):
            rows = pl.ds(t0, n)
            ub = u_ref[rows, :].astype(BF16)
            dyb = dy_ref[rows, :].astype(BF16)
            g = _tiles_load(gs[dr], 0, n)
            gb = g.astype(BF16)
            du_ref[rows, :] += _dg(gb, bm_ref[dr, 0], NT)
            dbm_ref[dr, 0] += _dg(ub, gb, TN)
            xp = _tiles_load(xps[dr], t0 * 8, n)
            xp_r, xp_i = xp[:, 0:half], xp[:, half:]
            g_r, g_i = g[:, 0:half], g[:, half:]
            a_re = a_ref[dr, 0, :, 0:half]
            a_im = a_ref[dr, 0, :, half:]
            bu = _dot(ub, bm_ref[dr, 0])
            x_r = a_re * xp_r - a_im * xp_i + bu[:, 0:half]
            x_i = a_re * xp_i + a_im * xp_r + bu[:, half:]
            dcm_ref[dr, 0] += _dg(dyb, jnp.concatenate([x_r, x_i], axis=1).astype(BF16), TN)
            da_ref[dr, 0] += jnp.concatenate([jnp.sum(g_r * xp_r + g_i * xp_i, axis=0, keepdims=True),
                                              jnp.sum(g_i * xp_r - g_r * xp_i, axis=0, keepdims=True)], axis=1)

        gf = zero
        if n_tail:
            load_dy(0, t_tail, n_tail)
            gf, = _scan([adj(0, gf)], n_tail)
            emit(0, t_tail, n_tail)

        def adj_pair(i, carry):
            t0b, t0f = both(i)
            load_dy(0, t0f, SCAN_CHUNK)
            load_dy(1, t0b, SCAN_CHUNK)
            carry = _scan([adj(dr, carry[dr]) for dr in range(2)], SCAN_CHUNK)
            emit(0, t0f, SCAN_CHUNK)
            emit(1, t0b, SCAN_CHUNK)
            return tuple(carry)

        _, gb_last = lax.fori_loop(0, n_full, adj_pair, (gf, zero))
        if n_tail:
            load_dy(1, t_tail, n_tail)
            _scan([adj(1, gb_last)], n_tail)
            emit(1, t_tail, n_tail)

    lp = -(-length // 8) * 8
    tile = pl.BlockSpec((2, 1, 8, 128), lambda b: (0, b, 0, 0))
    return pl.pallas_call(
        body, name="s5_bwd", grid=(N_BUNDLE,),
        in_specs=[pl.BlockSpec((tp, cw), lambda b: (0, b)), pl.BlockSpec((tp, cw), lambda b: (0, b)),
                  pl.BlockSpec((1, cw), lambda b: (0, b)),
                  pl.BlockSpec((2, 1, 1, sw), lambda b: (0, b, 0, 0)), tile, tile,
                  pl.BlockSpec((2, 1, cw, sw), lambda b: (0, b, 0, 0)),
                  pl.BlockSpec((2, 1, cw, sw), lambda b: (0, b, 0, 0))],
        out_specs=[pl.BlockSpec((tp, cw), lambda b: (0, b)), pl.BlockSpec((1, cw), lambda b: (0, b)),
                   pl.BlockSpec((2, 1, cw, sw), lambda b: (0, b, 0, 0)),
                   pl.BlockSpec((2, 1, cw, sw), lambda b: (0, b, 0, 0)),
                   pl.BlockSpec((2, 1, 1, sw), lambda b: (0, b, 0, 0))],
        out_shape=[_out((tp, S5_W), F32), _out((1, S5_W), F32),
                   _out((2, N_BUNDLE, cw, sw), F32), _out((2, N_BUNDLE, cw, sw), F32),
                   _out((2, N_BUNDLE, 1, sw), F32)],
        scratch_shapes=[pltpu.VMEM((SCAN_CHUNK * 8, 128), F32)] * 4 + [pltpu.VMEM((lp * 8, 128), F32)] * 2,
        compiler_params=_cp(("arbitrary",), 56),
    )(*_in_hbm(u, dy, d_skip, a, a1, a2, bm, cm))


def _row_tile(tp):
    return max(tm for tm in range(16, 449, 16) if tp % tm == 0)


def _step(x, target, bufs, gains, s5, rpb, c_arr, kc_arr):
    first = ["ffn1_w_gate", "ffn1_w_up", "ffn1_w_down", "meta_tokens"]
    w = dict(zip(first, _run_comm("gather_ffn1", _gather_comm([bufs[n] for n in first]))))
    meta = w["meta_tokens"].transpose(1, 0, 2).reshape(N_META, D)
    n_tok = x.shape[0]
    length = N_META + n_tok
    tp = length + 16
    tm = _row_tile(tp)
    tmb = tm
    n_rows = n_tok // GRID_W
    pad = jnp.zeros((tp - length, D), F32)
    h0 = jnp.concatenate([meta, x, pad], axis=0)
    tgt = jnp.concatenate([jnp.zeros((N_META, D), F32), target, pad], axis=0)

    s5p = (s5["lam_re"], s5["lam_im"], s5["log_dt"].reshape(2 * S5_G, 1), s5["b_re"], s5["b_im"])
    a_m, a1_m, a2_m, bm16, cm16 = _s5_params(*s5p, s5["c_re"], s5["c_im"])
    bias = _bias_tables(rpb, n_rows)

    mid = ["w_in", "s5_w_glu", "w_out"]
    (h1, gate1, up1, f1), got = _ffn_fwd(
        "ffn1_fwd", h0, gains["ffn1_pre_g"], gains["ffn1_post_g"], w["ffn1_w_gate"], w["ffn1_w_up"], w["ffn1_w_down"],
        tm, _gather_comm([bufs[n] for n in mid]), (0, (tp // tm) * N_CHIP * 3 // 5))
    w.update(zip(mid, got))
    q, k, v, u = _mix_in(h1, gains["mix_pre_g"], w["w_in"], tm)
    (o_na,), (gate_ici,) = _attn_fwd(q, k, v, bias, n_tok, _gather_comm([bufs["ffn2_w_gate"]], pair=False), (0,))
    (y_pre,), (w["ffn2_w_gate"], up_ici, down_ici) = _s5_fwd(
        u, gains["s5_d"], a1_m, a2_m, bm16, cm16, length,
        _merge_comm(_gather_comm([gate_ici], ici=False),
                    _gather_comm([bufs["ffn2_w_up"], bufs["ffn2_w_down"]], pair=False)), (0,))
    w_glu = w["s5_w_glu"].reshape(S5_W, S5_W)
    w_out = w["w_out"].reshape(D, D)
    (h2, mix), (w["ffn2_w_up"], w["ffn2_w_down"]) = _mix_out(
        o_na, y_pre, h1, w_glu, gains["s5_b_glu"], gains["na_out_g"], gains["s5_out_g"], w_out, gains["mix_post_g"], tm,
        _gather_comm([up_ici, down_ici], ici=False), (0,))
    (h3, gate2, up2, f2), _ = _ffn_fwd("ffn2_fwd", h2, gains["ffn2_pre_g"], gains["ffn2_post_g"],
                                       w["ffn2_w_gate"], w["ffn2_w_up"], w["ffn2_w_down"], tm)
    dh3, df2, loss, dg_final, dg_post2 = _final_loss(h3, gains["final_g"], tgt, f2, gains["ffn2_post_g"], n_tok, tm)

    ffn2 = ["ffn2_w_gate", "ffn2_w_up", "ffn2_w_down"]
    ffn1 = ["ffn1_w_gate", "ffn1_w_up", "ffn1_w_down"]
    out2 = _ffn_bwd("ffn2_bwd", h2, gains["ffn2_pre_g"], df2, gate2, up2,
                    w["ffn2_w_gate"], w["ffn2_w_up"], w["ffn2_w_down"], tmb)
    dxn2 = out2[3]
    sums2 = [_chip_sum("chip_sum_" + n, g, r, c_arr) for n, g, r in zip(ffn2, out2[0:3], out2[4:7])]
    (dh2, dg_pre2), _ = _ffn_pre_bwd("ffn2_pre_bwd", dh3, dxn2, h2, gains["ffn2_pre_g"], tm)
    do_na, dy_pre, dw_out, dw_glu, dg_mpost, dg_na, dg_s5, db_glu = _mix_out_bwd(
        dh2, mix, o_na, y_pre, w_glu, gains["s5_b_glu"], gains["na_out_g"], gains["s5_out_g"], w_out,
        gains["mix_post_g"], tm)
    (dq, dk, dv, dtb), recv3 = _attn_bwd(q, k, v, bias, do_na, n_tok, _scatter_comm(sums2), (0,))
    totals2 = [_total_sum("total_sum_" + n, s, r, kc_arr) for n, s, r in zip(ffn2, sums2, recv3)]
    du, dd, dbm, dcm, da_m = _s5_bwd(u, dy_pre, gains["s5_d"], a_m, a1_m, a2_m, bm16, cm16, length)
    (dh1, df1, dw_in, dg_mpre, dg_post1), done2 = _mix_in_bwd(
        dq, dk, dv, du, h1, gains["mix_pre_g"], w["w_in"], dh2, f1, gains["ffn1_post_g"], tm,
        _assemble_comm(totals2), (0,))
    pieces = dict(zip(ffn2, done2))
    out1 = _ffn_bwd("ffn1_bwd", h0, gains["ffn1_pre_g"], df1, gate1, up1,
                    w["ffn1_w_gate"], w["ffn1_w_up"], w["ffn1_w_down"], tmb)
    rest = [dw_in, dw_glu.reshape(N_CHIP, S5_W // N_CHIP, S5_W), dw_out.reshape(N_CHIP, D // N_CHIP, D)]
    (dh0, dg_pre1), recv_rest = _ffn_pre_bwd("ffn1_pre_bwd", dh1, out1[3], h0, gains["ffn1_pre_g"], tm,
                                             _exchange_comm(rest), (0,))
    last = ffn1 + mid
    sums = [_chip_sum("chip_sum_" + n, g, r, c_arr)
            for n, g, r in zip(last, list(out1[0:3]) + rest, list(out1[4:7]) + list(recv_rest))]
    recv3 = _run_comm("grad_chip_scatter", _scatter_comm(sums))
    totals = [_total_sum("total_sum_" + n, s, r, kc_arr) for n, s, r in zip(last, sums, recv3)]
    pieces.update(zip(last, _run_comm("grad_pair_assemble", _assemble_comm(totals))))

    e, _ = _diag_onehot()
    n_dr = 2 * KH - 1
    drpb = _rpb_collapse(dtb.reshape(N_HEADS * n_dr, GRID_W * GRID_W), jnp.asarray(e.T))
    drpb = drpb[:, :2 * KW - 1].reshape(N_HEADS, n_dr, 2 * KW - 1).transpose(1, 0, 2).reshape(N_HEADS * n_dr, 2 * KW - 1)
    dlam_re, dlam_im, dlog_dt, db_re, db_im, dc_re, dc_im = _s5_params_bwd(*s5p, da_m, dbm, dcm)

    small = {"ffn1_pre_g": dg_pre1, "ffn1_post_g": dg_post1, "mix_pre_g": dg_mpre, "na_rpb": drpb,
             "s5_lam_re": dlam_re, "s5_lam_im": dlam_im, "s5_log_dt": dlog_dt.reshape(2, S5_G),
             "s5_b_re": db_re, "s5_b_im": db_im, "s5_c_re": dc_re, "s5_c_im": dc_im,
             "s5_d": dd, "s5_b_glu": db_glu, "na_out_g": dg_na,
             "s5_out_g": dg_s5, "mix_post_g": dg_mpost, "ffn2_pre_g": dg_pre2, "ffn2_post_g": dg_post2,
             "final_g": dg_final}
    return loss[0, 0], dh0, pieces, small


def _mesh_pos():
    return lax.axis_index("x"), lax.axis_index("y"), lax.axis_index("c")


def _other_chips(x, y):
    return [(1 - x, y), (x, 1 - y), (1 - x, 1 - y)]


class _Comm:
    def __init__(self, ins, out_shape, aliases, parts):
        self.ins, self.out_shape, self.aliases, self.parts = list(ins), list(out_shape), dict(aliases), list(parts)
        self.n_sems = sum(p[0] for p in parts)

    def bases(self):
        out, base = [], 0
        for n_sems, _, _ in self.parts:
            out.append(base)
            base += n_sems
        return out


def _run_comm(name, comm):
    n_i, n_o = len(comm.ins), len(comm.out_shape)

    def body(*refs):
        ins, outs = refs[:n_i], refs[n_i:n_i + n_o]
        send_sems, recv_sems = refs[n_i + n_o:]
        for base, (_, start, finish) in zip(comm.bases(), comm.parts):
            start(ins, outs, send_sems, recv_sems, base)
            finish(ins, outs, send_sems, recv_sems, base)

    return pl.pallas_call(
        body, name=name, out_shape=comm.out_shape, in_specs=[ANY] * n_i, out_specs=[ANY] * n_o,
        input_output_aliases=comm.aliases,
        scratch_shapes=[pltpu.SemaphoreType.DMA((comm.n_sems,)), pltpu.SemaphoreType.DMA((comm.n_sems,))],
    )(*_in_hbm(*comm.ins))


def _call(body, comm, bounds, args, *, name, grid, in_specs, out_specs, out_shape, scratch_shapes=(),
          compiler_params=None):
    in_specs, out_specs, out_shape, scratch_shapes = list(in_specs), list(out_specs), list(out_shape), list(scratch_shapes)
    if comm is None:
        return pl.pallas_call(body, name=name, grid=grid, in_specs=in_specs, out_specs=out_specs, out_shape=out_shape,
                              scratch_shapes=scratch_shapes, compiler_params=compiler_params)(*_in_hbm(*args)), []
    n_in, n_out, n_scr = len(in_specs), len(out_specs), len(scratch_shapes)
    n_ci, n_co = len(comm.ins), len(comm.out_shape)
    n_steps = int(np.prod(grid))
    assert len(bounds) == len(comm.parts) and all(0 <= b < n_steps for b in bounds) and list(bounds) == sorted(bounds)

    def fused(*refs):
        a = n_in
        b = a + n_ci
        c = b + n_out
        d = c + n_co
        e = d + n_scr
        cargs = (refs[a:b], refs[c:d], refs[e], refs[e + 1])
        step = pl.program_id(0)
        for ax in range(1, len(grid)):
            step = step * grid[ax] + pl.program_id(ax)
        bases = comm.bases()
        for p, (_, start, finish) in enumerate(comm.parts):
            @pl.when(step == bounds[p])
            def _(p=p, start=start):
                if p > 0:
                    comm.parts[p - 1][2](*cargs, bases[p - 1])
                start(*cargs, bases[p])
        body(*(refs[:a] + refs[b:c] + refs[d:e]))

        @pl.when(step == n_steps - 1)
        def _():
            comm.parts[-1][2](*cargs, bases[-1])

    res = pl.pallas_call(
        fused, name=name, grid=grid, in_specs=in_specs + [ANY] * n_ci, out_specs=out_specs + [ANY] * n_co,
        out_shape=out_shape + comm.out_shape,
        scratch_shapes=scratch_shapes + [pltpu.SemaphoreType.DMA((comm.n_sems,)), pltpu.SemaphoreType.DMA((comm.n_sems,))],
        input_output_aliases={n_in + i: n_out + j for i, j in comm.aliases.items()},
        compiler_params=compiler_params)(*_in_hbm(*args, *comm.ins))
    return res[:n_out], res[n_out:]


def _remote(src, dst, send_sems, recv_sems, idx, to):
    return pltpu.make_async_remote_copy(src_ref=src, dst_ref=dst, send_sem=send_sems.at[idx],
                                        recv_sem=recv_sems.at[idx], device_id=to, device_id_type=MESH_ID)


def _gather_comm(bufs, ici=True, pair=True):
    n = len(bufs)

    def half(ref, k, pc):
        rh = ref.shape[1] // 2
        return ref.at[k, pl.ds(pc * rh, rh), :]

    def ici_start(ins, outs, ss, rs, base):
        x, y, c = _mesh_pos()
        for a in range(n):
            mine = half(outs[a], 2 * x + y, c)
            for j, chip in enumerate(_other_chips(x, y)):
                _remote(mine, mine, ss, rs, base + 3 * a + j, (*chip, c)).start()

    def ici_finish(ins, outs, ss, rs, base):
        x, y, c = _mesh_pos()
        for a in range(n):
            for j, chip in enumerate(_other_chips(x, y)):
                theirs = half(outs[a], 2 * chip[0] + chip[1], c)
                _remote(theirs, theirs, ss, rs, base + 3 * a + j, (*chip, c)).wait()

    def pair_copy(outs, ss, rs, base, a):
        x, y, c = _mesh_pos()
        rh = outs[a].shape[1] // 2
        held = outs[a].at[:, pl.ds(c * rh, rh), :]
        return _remote(held, held, ss, rs, base + a, (x, y, 1 - c))

    def pair_start(ins, outs, ss, rs, base):
        for a in range(n):
            pair_copy(outs, ss, rs, base, a).start()

    def pair_finish(ins, outs, ss, rs, base):
        for a in range(n):
            pair_copy(outs, ss, rs, base, a).wait()

    parts = ([(3 * n, ici_start, ici_finish)] if ici else []) + ([(n, pair_start, pair_finish)] if pair else [])
    return _Comm(bufs, [_out(b.shape, b.dtype) for b in bufs], {a: a for a in range(n)}, parts)


def _merge_comm(*comms):
    ins, shapes, aliases, subs, base = [], [], {}, [], 0
    for cm in comms:
        (n_sems, start, finish), = cm.parts
        i0, o0 = len(ins), len(shapes)
        subs.append((slice(i0, i0 + len(cm.ins)), slice(o0, o0 + len(cm.out_shape)), base, start, finish))
        aliases.update({i0 + i: o0 + j for i, j in cm.aliases.items()})
        ins += cm.ins
        shapes += cm.out_shape
        base += n_sems

    def start_all(ins_r, outs_r, ss, rs, b):
        for si, so, off, start, _ in subs:
            start(ins_r[si], outs_r[so], ss, rs, b + off)

    def finish_all(ins_r, outs_r, ss, rs, b):
        for si, so, off, _, finish in subs:
            finish(ins_r[si], outs_r[so], ss, rs, b + off)

    return _Comm(ins, shapes, aliases, [(base, start_all, finish_all)])


def _own_half_buffers(pieces, dtypes, kc_arr):
    n = len(pieces)

    def body(kc_ref, *refs):
        for a in range(n):
            refs[n + a][0] = refs[a][...].astype(dtypes[a])

    def half(p):
        return p.shape[0] // 2, p.shape[1]

    return pl.pallas_call(
        body, name="own_halves",
        out_shape=[_out((N_CHIP,) + p.shape, dt) for p, dt in zip(pieces, dtypes)],
        grid_spec=pltpu.PrefetchScalarGridSpec(
            num_scalar_prefetch=1, grid=(1,),
            in_specs=[pl.BlockSpec(half(p), lambda i, kc: (kc[1], 0)) for p in pieces],
            out_specs=[pl.BlockSpec((1,) + half(p), lambda i, kc: (kc[0], kc[1], 0)) for p in pieces]),
        compiler_params=_cp(("arbitrary",), 48),
    )(kc_arr, *_in_hbm(*pieces))


def _exchange_comm(grads):
    n = len(grads)

    def copy(ins, outs, ss, rs, base, a):
        x, y, c = _mesh_pos()
        rh = ins[a].shape[1] // 2
        return _remote(ins[a].at[:, pl.ds((1 - c) * rh, rh), :], outs[a], ss, rs, base + a, (x, y, 1 - c))

    def start(ins, outs, ss, rs, base):
        for a in range(n):
            copy(ins, outs, ss, rs, base, a).start()

    def finish(ins, outs, ss, rs, base):
        for a in range(n):
            copy(ins, outs, ss, rs, base, a).wait()

    shapes = [_out((N_CHIP, g.shape[1] // 2, g.shape[2]), g.dtype) for g in grads]
    return _Comm(grads, shapes, {}, [(n, start, finish)])


def _chip_sum(name, g, recv, c_arr):
    _, r, cc = g.shape
    rh = r // 2

    def body(c_ref, g_ref, r_ref, o_ref):
        o_ref[...] = (g_ref[...] + r_ref[...]).astype(BF16)

    return pl.pallas_call(
        body, name=name, out_shape=_out((N_CHIP, rh, cc), BF16),
        grid_spec=pltpu.PrefetchScalarGridSpec(
            num_scalar_prefetch=1, grid=(N_CHIP,),
            in_specs=[pl.BlockSpec((1, rh, cc), lambda j, c_ref: (j, c_ref[0], 0)),
                      pl.BlockSpec((1, rh, cc), lambda j, c_ref: (j, 0, 0))],
            out_specs=pl.BlockSpec((1, rh, cc), lambda j, c_ref: (j, 0, 0))),
        compiler_params=_cp(("arbitrary",), 32),
    )(c_arr, *_in_hbm(g, recv))


def _scatter_comm(sums):
    n = len(sums)

    def copies(ins, outs, ss, rs, base):
        x, y, c = _mesh_pos()
        return [_remote(ins[a].at[2 * chip[0] + chip[1]], outs[a].at[j], ss, rs, base + 3 * a + j, (*chip, c))
                for a in range(n) for j, chip in enumerate(_other_chips(x, y))]

    def start(ins, outs, ss, rs, base):
        for cp in copies(ins, outs, ss, rs, base):
            cp.start()

    def finish(ins, outs, ss, rs, base):
        for cp in copies(ins, outs, ss, rs, base):
            cp.wait()

    shapes = [_out((3,) + s.shape[1:], s.dtype) for s in sums]
    return _Comm(sums, shapes, {}, [(3 * n, start, finish)])


def _total_sum(name, sums, recv3, kc_arr):
    _, rh, cc = sums.shape

    def body(kc_ref, s_ref, r_ref, o_ref):
        t = s_ref[0].astype(F32) + r_ref[0].astype(F32)
        t = t + r_ref[1].astype(F32)
        o_ref[...] = t + r_ref[2].astype(F32)

    return pl.pallas_call(
        body, name=name, out_shape=_out((2 * rh, cc), F32),
        grid_spec=pltpu.PrefetchScalarGridSpec(
            num_scalar_prefetch=1, grid=(1,),
            in_specs=[pl.BlockSpec((1, rh, cc), lambda i, kc_ref: (kc_ref[0], 0, 0)),
                      pl.BlockSpec((3, rh, cc), lambda i, kc_ref: (0, 0, 0))],
            out_specs=pl.BlockSpec((rh, cc), lambda i, kc_ref: (kc_ref[1], 0))),
        compiler_params=_cp(("arbitrary",), 32),
    )(kc_arr, *_in_hbm(sums, recv3))


def _assemble_comm(totals):
    n = len(totals)

    def copy(outs, ss, rs, base, a):
        x, y, c = _mesh_pos()
        rh = outs[a].shape[0] // 2
        here = outs[a].at[pl.ds(c * rh, rh), :]
        return _remote(here, here, ss, rs, base + a, (x, y, 1 - c))

    def start(ins, outs, ss, rs, base):
        for a in range(n):
            copy(outs, ss, rs, base, a).start()

    def finish(ins, outs, ss, rs, base):
        for a in range(n):
            copy(outs, ss, rs, base, a).wait()

    shapes = [_out(t.shape, t.dtype) for t in totals]
    return _Comm(totals, shapes, {a: a for a in range(n)}, [(n, start, finish)])


def _small_allreduce(arrays):
    n = len(arrays)
    shapes = [a.shape for a in arrays]
    narrow_w = 64
    groups = [[a for a in range(n) if shapes[a][1] > narrow_w], [a for a in range(n) if shapes[a][1] <= narrow_w]]
    widths = [max(shapes[a][1] for a in groups[0]), 2 * narrow_w]
    offs, cols, heights = {}, {}, [0, 0]
    for a in groups[0]:
        offs[a], cols[a] = heights[0], 0
        heights[0] += shapes[a][0]
    rows = [-(-heights[0] // 8) * 8]
    heights = [0, 0]
    for a in sorted(groups[1], key=lambda a: -shapes[a][0]):
        side = 0 if heights[0] <= heights[1] else 1
        offs[a], cols[a] = heights[side], side * narrow_w
        heights[side] += shapes[a][0]
    rows.append(-(-max(heights) // 8) * 8)
    n_g = len(groups)

    def window(ref, a):
        return ref.at[offs[a]:offs[a] + shapes[a][0], cols[a]:cols[a] + shapes[a][1]]

    def body(*refs):
        ins, outs = refs[:n], refs[n:2 * n]
        pack, sib, csum, every = (refs[2 * n + i * n_g:2 * n + (i + 1) * n_g] for i in range(4))
        send_sems, recv_sems = refs[2 * n + 4 * n_g:]
        x, y, c = _mesh_pos()
        k = 2 * x + y
        for gi, g in enumerate(groups):
            pack[gi][...] = jnp.zeros_like(pack[gi])
            for a in g:
                window(pack[gi], a)[...] = ins[a][...]
        cps = [_remote(pack[gi], sib[gi], send_sems, recv_sems, gi, (x, y, 1 - c)) for gi in range(n_g)]
        for cp in cps:
            cp.start()
        for cp in cps:
            cp.wait()
        for gi in range(n_g):
            csum[gi][...] = pack[gi][...] + sib[gi][...]
            every[gi][k] = csum[gi][...]
        cps = [_remote(csum[gi], every[gi].at[k], send_sems, recv_sems, n_g + 3 * gi + j, (*chip, c))
               for gi in range(n_g) for j, chip in enumerate(_other_chips(x, y))]
        for cp in cps:
            cp.start()
        for cp in cps:
            cp.wait()
        for gi, g in enumerate(groups):
            pack[gi][...] = ((every[gi][0] + every[gi][1]) + every[gi][2]) + every[gi][3]
            for a in g:
                outs[a][...] = window(pack[gi], a)[...]

    bufs = [pltpu.VMEM((r, w), F32) for r, w in zip(rows, widths)]
    return pl.pallas_call(
        body, name="small_allreduce", grid=(1,), out_shape=[_out(s, F32) for s in shapes],
        in_specs=[_full(s) for s in shapes], out_specs=[_full(s) for s in shapes],
        scratch_shapes=bufs * 3 + [pltpu.VMEM((N_CHIP, r, w), F32) for r, w in zip(rows, widths)] +
                       [pltpu.SemaphoreType.DMA((4 * n_g,)), pltpu.SemaphoreType.DMA((4 * n_g,))],
        compiler_params=_cp(("arbitrary",), 40),
    )(*_in_hbm(*arrays))


def _adamw_small(ws, gs, ms, vs):
    n = len(ws)

    def body(*refs):
        w, g, m, v, d, mo, vo = (refs[i * n:(i + 1) * n] for i in range(7))
        for a in range(n):
            d[a][...], mo[a][...], vo[a][...] = _adamw_math(w[a][...], g[a][...], m[a][...], v[a][...])

    specs = [_full(w.shape) for w in ws]
    res = pl.pallas_call(
        body, name="adamw_small", grid=(1,), out_shape=[_out(w.shape, F32) for w in ws] * 3,
        in_specs=specs * 4, out_specs=specs * 3, compiler_params=_cp(("arbitrary",), 40),
    )(*_in_hbm(*ws, *gs, *ms, *vs))
    return res[:n], res[n:2 * n], res[2 * n:]


def _adamw_math(w, g, m, v):
    m = ADAM_B1 * m + (1.0 - ADAM_B1) * g
    v = ADAM_B2 * v + (1.0 - ADAM_B2) * (g * g)
    m_hat = m / (1.0 - ADAM_B1 ** ADAM_STEP)
    v_hat = v / (1.0 - ADAM_B2 ** ADAM_STEP)
    delta = -ADAM_LR * (m_hat / (jnp.sqrt(v_hat) + ADAM_EPS) + ADAM_WD * w)
    return delta, m, v


def _adamw(name, w, g, m, v):
    r, c = w.shape
    tr = max(t for t in range(8, 513, 8) if r % t == 0)

    def body(w_ref, g_ref, m_ref, v_ref, d_ref, mo_ref, vo_ref):
        d_ref[...], mo_ref[...], vo_ref[...] = _adamw_math(w_ref[...], g_ref[...], m_ref[...], v_ref[...])

    return pl.pallas_call(
        body, name=name, grid=(r // tr,), in_specs=[_rows(tr, c)] * 4, out_specs=[_rows(tr, c)] * 3,
        out_shape=[_out((r, c), F32)] * 3, compiler_params=_cp(("arbitrary",), 32),
    )(*_in_hbm(w, g, m, v))


def _as_matrix(name, a):
    if name == "na_rpb":
        return a[0].transpose(1, 0, 2).reshape(N_HEADS * (2 * KH - 1), 2 * KW - 1)
    if name in ("s5_b_re", "s5_b_im"):
        return a.transpose(0, 1, 2, 4, 3).reshape(2 * S5_G * S5_H, S5_P)
    if name in ("s5_c_re", "s5_c_im"):
        return a.reshape(2 * S5_G * S5_H, S5_P)
    if name in ("s5_lam_re", "s5_lam_im"):
        return a.reshape(2 * S5_G, S5_P)
    if name == "s5_log_dt":
        return a.reshape(2, S5_G)
    return a


def _from_matrix(name, m):
    if name == "na_rpb":
        return m.reshape(2 * KH - 1, N_HEADS, 2 * KW - 1).transpose(1, 0, 2)[None]
    if name in ("s5_b_re", "s5_b_im"):
        return m.reshape(1, 2, S5_G, S5_H, S5_P).transpose(0, 1, 2, 4, 3)
    if name in ("s5_c_re", "s5_c_im"):
        return m.reshape(1, 2, S5_G, S5_H, S5_P)
    if name in ("s5_lam_re", "s5_lam_im"):
        return m.reshape(1, 2, S5_G, S5_P)
    if name == "s5_log_dt":
        return m.reshape(1, 2, S5_G)
    return m


WEIGHTS = ["meta_tokens", "ffn1_pre_g", "ffn1_post_g", "ffn1_w_gate", "ffn1_w_up", "ffn1_w_down", "mix_pre_g", "w_in",
           "na_rpb", "s5_lam_re", "s5_lam_im", "s5_log_dt", "s5_b_re", "s5_b_im", "s5_c_re", "s5_c_im", "s5_d",
           "s5_w_glu", "s5_b_glu", "na_out_g", "s5_out_g", "w_out", "mix_post_g", "ffn2_pre_g", "ffn2_post_g",
           "ffn2_w_gate", "ffn2_w_up", "ffn2_w_down", "final_g"]
BIG = ["ffn1_w_gate", "ffn1_w_up", "ffn1_w_down", "w_in", "s5_w_glu", "w_out", "ffn2_w_gate", "ffn2_w_up",
       "ffn2_w_down"]
TRANSPOSED = ["ffn1_w_gate", "ffn1_w_up", "ffn2_w_gate", "ffn2_w_up"]
GAINS = ["ffn1_pre_g", "ffn1_post_g", "mix_pre_g", "s5_d", "s5_b_glu", "na_out_g", "s5_out_g", "mix_post_g",
         "ffn2_pre_g", "ffn2_post_g", "final_g"]
SMALL = [n for n in WEIGHTS if n not in BIG]


def kernel(*args):
    names = ["x"] + WEIGHTS + ["loss_target"] + ["m_" + n for n in WEIGHTS] + ["v_" + n for n in WEIGHTS]
    assert len(args) == len(names)
    given = dict(zip(names, args))
    x_pos, y_pos, c_pos = _mesh_pos()
    k_pos = 2 * x_pos + y_pos
    c_arr = jnp.reshape(c_pos, (1,)).astype(jnp.int32)
    kc_arr = jnp.stack([k_pos, c_pos]).astype(jnp.int32)

    def piece(name, a):
        return a[0].T if name in TRANSPOSED else a[0]

    def unpiece(name, a):
        return a.T[None] if name in TRANSPOSED else a[None]

    placed = BIG + ["meta_tokens"]
    bufs = dict(zip(placed, _own_half_buffers([piece(n, given[n]) for n in BIG] + [given["meta_tokens"]],
                                              [BF16] * len(BIG) + [F32], kc_arr)))

    gains = {n: given[n] for n in GAINS}
    s5 = {n: _as_matrix("s5_" + n, given["s5_" + n])
          for n in ["lam_re", "lam_im", "log_dt", "b_re", "b_im", "c_re", "c_im"]}
    loss, dh0, pieces, small = _step(given["x"][0], given["loss_target"][0], bufs, gains, s5, given["na_rpb"][0],
                                     c_arr, kc_arr)
    loss = lax.psum(loss, ("x", "y", "c"))
    n_tok = given["x"].shape[1]
    grad_x = dh0[N_META:N_META + n_tok][None]

    small["meta_tokens"] = dh0[:N_META]
    small = dict(zip(SMALL, _small_allreduce([small[n] for n in SMALL])))
    mc = D // N_CHIP
    small["meta_tokens"] = lax.dynamic_slice_in_dim(small["meta_tokens"], k_pos * mc, mc, 1)

    out_g, out_d, out_m, out_v = {}, {}, {}, {}
    for n in BIG:
        g2 = pieces[n]
        d2, m2, v2 = _adamw("adamw_" + n, piece(n, given[n]), g2, piece(n, given["m_" + n]),
                            piece(n, given["v_" + n]))
        out_g[n], out_d[n], out_m[n], out_v[n] = (unpiece(n, t) for t in (g2, d2, m2, v2))
    gs = [small[n] for n in SMALL]
    d2, m2, v2 = _adamw_small([_as_matrix(n, given[n]) for n in SMALL], gs,
                              [_as_matrix(n, given["m_" + n]) for n in SMALL],
                              [_as_matrix(n, given["v_" + n]) for n in SMALL])
    for n, g, dd, mm, vv in zip(SMALL, gs, d2, m2, v2):
        out_g[n], out_d[n], out_m[n], out_v[n] = (_from_matrix(n, t) for t in (g, dd, mm, vv))
    return (loss, grad_x, *[out_g[n] for n in WEIGHTS], *[out_d[n] for n in WEIGHTS],
            *[out_m[n] for n in WEIGHTS], *[out_v[n] for n in WEIGHTS])
```

```python
import functools
import math

import numpy as np
import jax
import jax.numpy as jnp
from jax import lax
from jax.experimental import pallas as pl
from jax.experimental.pallas import tpu as pltpu

F32 = jnp.float32
BF16 = jnp.bfloat16

D = 1024
N_META = 16
GRID_W = 64
NA_W = 512
S5_W = 512
HEAD_DIM = 64
N_HEADS = 8
KH = 8
KW = 16
S5_G = 32
S5_P = 64
S5_H = 16
N_BUNDLE = 4
FF = 2816
N_CHIP = 4
FC = FF // N_CHIP
EPS = 1e-6
NEG_INF = -1e30
Q_ROWS = 4
K_ROWS = 12
QB = Q_ROWS * GRID_W
KB = K_ROWS * GRID_W
SCAN_CHUNK = 256

ADAM_LR = 0.001
ADAM_B1 = 0.9
ADAM_B2 = 0.999
ADAM_EPS = 1e-08
ADAM_WD = 0.01
ADAM_STEP = 10

NT = (((1,), (1,)), ((), ()))
TN = (((0,), (0,)), ((), ()))
MESH_ID = pl.DeviceIdType.MESH


def _cp(sem=None, vmem_mb=None):
    kw = {}
    if sem is not None:
        kw["dimension_semantics"] = sem
    if vmem_mb is not None:
        kw["vmem_limit_bytes"] = vmem_mb << 20
    return pltpu.CompilerParams(**kw)


def _full(shape):
    n = len(shape)
    return pl.BlockSpec(shape, lambda *_: (0,) * n)


def _rows(tm, w):
    return pl.BlockSpec((tm, w), lambda i: (i, 0))


ANY = pl.BlockSpec(memory_space=pl.ANY)


def _rms(x, g):
    r = lax.rsqrt(jnp.mean(x * x, axis=-1, keepdims=True) + EPS)
    return x * r * g


def _rms_bwd(x, g, dy):
    r = lax.rsqrt(jnp.mean(x * x, axis=-1, keepdims=True) + EPS)
    xh = x * r
    dg = jnp.sum(dy * xh, axis=0, keepdims=True)
    dyg = dy * g
    dx = r * (dyg - xh * jnp.mean(dyg * xh, axis=-1, keepdims=True))
    return dx, dg


def _out(shape, dtype):
    return pltpu.HBM(tuple(shape), dtype)


def _in_hbm(*args):
    return [pltpu.with_memory_space_constraint(a, pltpu.HBM) if jnp.issubdtype(a.dtype, jnp.floating) else a
            for a in args]


def _dot(a, b):
    return jnp.dot(a, b, preferred_element_type=F32)


def _dg(a, b, dims):
    return lax.dot_general(a, b, dims, preferred_element_type=F32)


def _ffn_fwd(name, h, g_pre, g_post, wg, wu, wd, tm, comm=None, bounds=()):
    tp = h.shape[0]
    nt = tp // tm

    def body(h_ref, gp_ref, gq_ref, wg_ref, wu_ref, wd_ref, hn_ref, gate_ref, up_ref, f_ref, xn_s, acc_s):
        c = pl.program_id(1)

        @pl.when(c == 0)
        def _():
            xn_s[...] = _rms(h_ref[...], gp_ref[...]).astype(BF16)
            acc_s[...] = jnp.zeros_like(acc_s)

        xn = xn_s[...]
        gate = _dg(xn, wg_ref[0], NT)
        up = _dg(xn, wu_ref[0], NT)
        gate_ref[0] = gate
        up_ref[0] = up
        act = (gate * jax.nn.sigmoid(gate) * up).astype(BF16)
        acc_s[...] += _dot(act, wd_ref[0])

        @pl.when(c == N_CHIP - 1)
        def _():
            f = acc_s[...]
            f_ref[...] = f
            hn_ref[...] = h_ref[...] + 0.5 * _rms(f, gq_ref[...])

    return _call(
        body, comm, bounds, (h, g_pre, g_post, wg, wu, wd), name=name, grid=(nt, N_CHIP),
        in_specs=[pl.BlockSpec((tm, D), lambda i, c: (i, 0)), _full((1, D)), _full((1, D))] +
                 [pl.BlockSpec((1, FC, D), lambda i, c: (c, 0, 0))] * 3,
        out_specs=[pl.BlockSpec((tm, D), lambda i, c: (i, 0)),
                   pl.BlockSpec((1, tm, FC), lambda i, c: (c, i, 0)),
                   pl.BlockSpec((1, tm, FC), lambda i, c: (c, i, 0)),
                   pl.BlockSpec((tm, D), lambda i, c: (i, 0))],
        out_shape=[_out((tp, D), F32), _out((N_CHIP, tp, FC), F32),
                   _out((N_CHIP, tp, FC), F32), _out((tp, D), F32)],
        scratch_shapes=[pltpu.VMEM((tm, D), BF16), pltpu.VMEM((tm, D), F32)],
        compiler_params=_cp(("arbitrary", "arbitrary"), 48))


def _ffn_bwd(name, h, g_pre, df, gate, up, wg, wu, wd, tm):
    tp = h.shape[0]
    nt = tp // tm
    rh = FC // 2

    def body(h_ref, gp_ref, df_ref, gate_ref, up_ref, wg_ref, wu_ref, wd_ref,
             dwg_ref, dwu_ref, dwd_ref, dxn_ref, rg_ref, ru_ref, rd_ref, ag, au, ad, send_sems, recv_sems):
        c = pl.program_id(0)
        i = pl.program_id(1)

        def to_sibling(a, piece):
            x, y, core = _mesh_pos()
            dw_ref, r_ref = ((dwg_ref, rg_ref), (dwu_ref, ru_ref), (dwd_ref, rd_ref))[a]
            return _remote(dw_ref.at[piece, pl.ds((1 - core) * rh, rh), :], r_ref.at[piece], send_sems, recv_sems,
                           3 * piece + a, (x, y, 1 - core))

        @pl.when(i == 0)
        def _():
            ag[...] = jnp.zeros_like(ag)
            au[...] = jnp.zeros_like(au)
            ad[...] = jnp.zeros_like(ad)

        xn = _rms(h_ref[...], gp_ref[...]).astype(BF16)
        dfb = df_ref[...].astype(BF16)
        gt = gate_ref[0]
        u = up_ref[0]
        sg = jax.nn.sigmoid(gt)
        si = gt * sg
        act = (si * u).astype(BF16)
        dact = _dg(dfb, wd_ref[0], NT)
        ad[...] += _dg(act, dfb, TN)
        dgate = (dact * u * (sg * (1.0 + gt * (1.0 - sg)))).astype(BF16)
        dup = (dact * si).astype(BF16)
        ag[...] += _dg(dgate, xn, TN)
        au[...] += _dg(dup, xn, TN)
        dxn_ref[0] = _dot(dgate, wg_ref[0]) + _dot(dup, wu_ref[0])

        @pl.when(i == nt - 1)
        def _():
            pltpu.sync_copy(ag, dwg_ref.at[c])
            pltpu.sync_copy(au, dwu_ref.at[c])
            pltpu.sync_copy(ad, dwd_ref.at[c])
            for a in range(3):
                to_sibling(a, c).start()

        @pl.when((c == N_CHIP - 1) & (i == nt - 1))
        def _():
            for piece in range(N_CHIP):
                for a in range(3):
                    to_sibling(a, piece).wait()

    return pl.pallas_call(
        body, name=name, grid=(N_CHIP, nt),
        in_specs=[pl.BlockSpec((tm, D), lambda c, i: (i, 0)), _full((1, D)),
                  pl.BlockSpec((tm, D), lambda c, i: (i, 0)),
                  pl.BlockSpec((1, tm, FC), lambda c, i: (c, i, 0)),
                  pl.BlockSpec((1, tm, FC), lambda c, i: (c, i, 0))] +
                 [pl.BlockSpec((1, FC, D), lambda c, i: (c, 0, 0))] * 3,
        out_specs=[ANY, ANY, ANY, pl.BlockSpec((1, tm, D), lambda c, i: (c, i, 0)), ANY, ANY, ANY],
        out_shape=[_out((N_CHIP, FC, D), F32)] * 3 + [_out((N_CHIP, tp, D), F32)] +
                  [_out((N_CHIP, rh, D), F32)] * 3,
        scratch_shapes=[pltpu.VMEM((FC, D), F32)] * 3 +
                       [pltpu.SemaphoreType.DMA((3 * N_CHIP,)), pltpu.SemaphoreType.DMA((3 * N_CHIP,))],
        compiler_params=_cp(("arbitrary", "arbitrary"), 58),
    )(*_in_hbm(h, g_pre, df, gate, up, wg, wu, wd))


def _ffn_pre_bwd(name, dh, dxn_part, h, g_pre, tm, comm=None, bounds=()):
    tp = h.shape[0]
    nt = tp // tm

    def body(dh_ref, dxn_ref, h_ref, gp_ref, out_ref, dg_ref):
        i = pl.program_id(0)
        dxn = (dxn_ref[0] + dxn_ref[1]) + (dxn_ref[2] + dxn_ref[3])
        dx, dg = _rms_bwd(h_ref[...], gp_ref[...], dxn)
        out_ref[...] = dh_ref[...] + dx

        @pl.when(i == 0)
        def _():
            dg_ref[...] = jnp.zeros_like(dg_ref)

        dg_ref[...] += dg

    return _call(
        body, comm, bounds, (dh, dxn_part, h, g_pre), name=name, grid=(nt,),
        in_specs=[_rows(tm, D), pl.BlockSpec((N_CHIP, tm, D), lambda i: (0, i, 0)), _rows(tm, D), _full((1, D))],
        out_specs=[_rows(tm, D), _full((1, D))],
        out_shape=[_out((tp, D), F32), _out((1, D), F32)],
        compiler_params=_cp(("arbitrary",), 48))


def _mix_in(h, g, w_in, tm):
    tp = h.shape[0]

    def body(h_ref, g_ref, w_ref, q_ref, k_ref, v_ref, u_ref):
        a = _rms(h_ref[...], g_ref[...]).astype(BF16)
        q_ref[...] = _dot(a, w_ref[0]).astype(BF16)
        k_ref[...] = _dot(a, w_ref[1]).astype(BF16)
        v_ref[...] = _dot(a, w_ref[2]).astype(BF16)
        u_ref[...] = _dot(a, w_ref[3])

    return pl.pallas_call(
        body, name="mix_in", grid=(tp // tm,),
        in_specs=[_rows(tm, D), _full((1, D)), _full((N_CHIP, D, NA_W))],
        out_specs=[_rows(tm, NA_W)] * 4,
        out_shape=[_out((tp, NA_W), BF16)] * 3 + [_out((tp, S5_W), F32)],
        compiler_params=_cp(("arbitrary",), 40),
    )(*_in_hbm(h, g, w_in))


def _gelu(x):
    return jax.nn.gelu(x, approximate=True)


def _gelu_grad(x):
    k = math.sqrt(2.0 / math.pi)
    t = jnp.tanh(k * (x + 0.044715 * x * x * x))
    return 0.5 * (1.0 + t) + 0.5 * x * (1.0 - t * t) * k * (1.0 + 3.0 * 0.044715 * x * x)


def _mix_out(o_na, y_pre, h, w_glu, b_glu, g_na, g_s5, w_out, g_post, tm, comm=None, bounds=()):
    tp = h.shape[0]

    def body(ona_ref, yp_ref, h_ref, wglu_ref, bglu_ref, gna_ref, gs5_ref, wout_ref, gpost_ref, hn_ref, mix_ref):
        y = _gelu(yp_ref[...])
        z = _dot(y.astype(BF16), wglu_ref[...]) + bglu_ref[...]
        o_s5 = y * jax.nn.sigmoid(z)
        n1 = _rms(ona_ref[...], gna_ref[...]).astype(BF16)
        n2 = _rms(o_s5, gs5_ref[...]).astype(BF16)
        mix = _dot(n1, wout_ref[0:NA_W, :]) + _dot(n2, wout_ref[NA_W:, :])
        mix_ref[...] = mix
        hn_ref[...] = h_ref[...] + _rms(mix, gpost_ref[...])

    return _call(
        body, comm, bounds, (o_na, y_pre, h, w_glu, b_glu, g_na, g_s5, w_out, g_post), name="mix_out",
        grid=(tp // tm,),
        in_specs=[_rows(tm, NA_W), _rows(tm, S5_W), _rows(tm, D), _full((S5_W, S5_W)), _full((1, S5_W)),
                  _full((1, NA_W)), _full((1, S5_W)), _full((D, D)), _full((1, D))],
        out_specs=[_rows(tm, D), _rows(tm, D)],
        out_shape=[_out((tp, D), F32)] * 2,
        compiler_params=_cp(("arbitrary",), 40))


def _mix_out_bwd(dh, mix, o_na, y_pre, w_glu, b_glu, g_na, g_s5, w_out, g_post, tm):
    tp = dh.shape[0]
    nt = tp // tm

    def body(dh_ref, mix_ref, ona_ref, yp_ref, wglu_ref, bglu_ref, gna_ref, gs5_ref, wout_ref, gpost_ref,
             dona_ref, dyp_ref, dwout_ref, dwglu_ref, dgpost_ref, dgna_ref, dgs5_ref, dbglu_ref, a_out, a_glu):
        i = pl.program_id(0)

        @pl.when(i == 0)
        def _():
            a_out[...] = jnp.zeros_like(a_out)
            a_glu[...] = jnp.zeros_like(a_glu)
            dgpost_ref[...] = jnp.zeros_like(dgpost_ref)
            dgna_ref[...] = jnp.zeros_like(dgna_ref)
            dgs5_ref[...] = jnp.zeros_like(dgs5_ref)
            dbglu_ref[...] = jnp.zeros_like(dbglu_ref)

        dmix, dgpost = _rms_bwd(mix_ref[...], gpost_ref[...], dh_ref[...])
        dgpost_ref[...] += dgpost
        yp = yp_ref[...]
        y = _gelu(yp)
        yb = y.astype(BF16)
        z = _dot(yb, wglu_ref[...]) + bglu_ref[...]
        sg = jax.nn.sigmoid(z)
        o_s5 = y * sg
        o_na = ona_ref[...]
        n1 = _rms(o_na, gna_ref[...]).astype(BF16)
        n2 = _rms(o_s5, gs5_ref[...]).astype(BF16)
        dmb = dmix.astype(BF16)
        a_out[0:NA_W, :] += _dg(n1, dmb, TN)
        a_out[NA_W:, :] += _dg(n2, dmb, TN)
        dn1 = _dg(dmb, wout_ref[0:NA_W, :], NT)
        dn2 = _dg(dmb, wout_ref[NA_W:, :], NT)
        dona, dgna = _rms_bwd(o_na, gna_ref[...], dn1)
        dona_ref[...] = dona
        dgna_ref[...] += dgna
        dos5, dgs5 = _rms_bwd(o_s5, gs5_ref[...], dn2)
        dgs5_ref[...] += dgs5
        dz = dos5 * y * (sg * (1.0 - sg))
        dbglu_ref[...] += jnp.sum(dz, axis=0, keepdims=True)
        dzb = dz.astype(BF16)
        a_glu[...] += _dg(yb, dzb, TN)
        dy = dos5 * sg + _dg(dzb, wglu_ref[...], NT)
        dyp_ref[...] = dy * _gelu_grad(yp)

        @pl.when(i == nt - 1)
        def _():
            pltpu.sync_copy(a_out, dwout_ref)
            pltpu.sync_copy(a_glu, dwglu_ref)

    return pl.pallas_call(
        body, name="mix_out_bwd", grid=(nt,),
        in_specs=[_rows(tm, D), _rows(tm, D), _rows(tm, NA_W), _rows(tm, S5_W), _full((S5_W, S5_W)),
                  _full((1, S5_W)), _full((1, NA_W)), _full((1, S5_W)), _full((D, D)), _full((1, D))],
        out_specs=[_rows(tm, NA_W), _rows(tm, S5_W), ANY, ANY, _full((1, D)), _full((1, NA_W)),
                   _full((1, S5_W)), _full((1, S5_W))],
        out_shape=[_out((tp, NA_W), F32), _out((tp, S5_W), F32),
                   _out((D, D), F32), _out((S5_W, S5_W), F32),
                   _out((1, D), F32), _out((1, NA_W), F32),
                   _out((1, S5_W), F32), _out((1, S5_W), F32)],
        scratch_shapes=[pltpu.VMEM((D, D), F32), pltpu.VMEM((S5_W, S5_W), F32)],
        compiler_params=_cp(("arbitrary",), 48),
    )(*_in_hbm(dh, mix, o_na, y_pre, w_glu, b_glu, g_na, g_s5, w_out, g_post))


def _mix_in_bwd(dq, dk, dv, du, h, g, w_in, dh, f1, g_post1, tm, comm=None, bounds=()):
    tp = h.shape[0]
    nt = tp // tm

    def body(dq_ref, dk_ref, dv_ref, du_ref, h_ref, g_ref, w_ref, dh_ref, f_ref, gq_ref,
             dh1_ref, df_ref, dw_ref, dg_ref, dgq_ref, acc):
        i = pl.program_id(0)

        @pl.when(i == 0)
        def _():
            acc[...] = jnp.zeros_like(acc)
            dg_ref[...] = jnp.zeros_like(dg_ref)
            dgq_ref[...] = jnp.zeros_like(dgq_ref)

        x = h_ref[...]
        a = _rms(x, g_ref[...]).astype(BF16)
        da = jnp.zeros((tm, D), F32)
        for j, r in enumerate((dq_ref, dk_ref, dv_ref, du_ref)):
            dp = r[...].astype(BF16)
            da = da + _dg(dp, w_ref[j], NT)
            acc[j] += _dg(a, dp, TN)
        dx, dg = _rms_bwd(x, g_ref[...], da)
        dh1 = dh_ref[...] + dx
        dh1_ref[...] = dh1
        dg_ref[...] += dg
        df, dgq = _rms_bwd(f_ref[...], gq_ref[...], 0.5 * dh1)
        df_ref[...] = df
        dgq_ref[...] += dgq

        @pl.when(i == nt - 1)
        def _():
            pltpu.sync_copy(acc, dw_ref)

    return _call(
        body, comm, bounds, (dq, dk, dv, du, h, g, w_in, dh, f1, g_post1), name="mix_in_bwd", grid=(nt,),
        in_specs=[_rows(tm, NA_W)] * 4 + [_rows(tm, D), _full((1, D)), _full((N_CHIP, D, NA_W)), _rows(tm, D),
                                         _rows(tm, D), _full((1, D))],
        out_specs=[_rows(tm, D), _rows(tm, D), ANY, _full((1, D)), _full((1, D))],
        out_shape=[_out((tp, D), F32), _out((tp, D), F32),
                   _out((N_CHIP, D, NA_W), F32), _out((1, D), F32),
                   _out((1, D), F32)],
        scratch_shapes=[pltpu.VMEM((N_CHIP, D, NA_W), F32)],
        compiler_params=_cp(("arbitrary",), 48))


def _final_loss(h, g_final, target, f2, g_post2, n_tok, tm):
    tp = h.shape[0]

    def body(h_ref, g_ref, t_ref, f_ref, gq_ref, dh_ref, df_ref, loss_ref, dg_ref, dgq_ref):
        i = pl.program_id(0)

        @pl.when(i == 0)
        def _():
            loss_ref[...] = jnp.zeros_like(loss_ref)
            dg_ref[...] = jnp.zeros_like(dg_ref)
            dgq_ref[...] = jnp.zeros_like(dgq_ref)

        x = h_ref[...]
        y = _rms(x, g_ref[...])
        row = i * tm + lax.broadcasted_iota(jnp.int32, (tm, 1), 0)
        valid = (row >= N_META) & (row < N_META + n_tok)
        e = jnp.where(valid, y - t_ref[...], 0.0)
        loss_ref[...] += 0.5 * jnp.sum(jnp.mean(e * e, axis=-1, keepdims=True), axis=0, keepdims=True)
        dx, dg = _rms_bwd(x, g_ref[...], e * (1.0 / D))
        dh_ref[...] = dx
        dg_ref[...] += dg
        df, dgq = _rms_bwd(f_ref[...], gq_ref[...], 0.5 * dx)
        df_ref[...] = df
        dgq_ref[...] += dgq

    return pl.pallas_call(
        body, name="final_loss", grid=(tp // tm,),
        in_specs=[_rows(tm, D), _full((1, D)), _rows(tm, D), _rows(tm, D), _full((1, D))],
        out_specs=[_rows(tm, D), _rows(tm, D), _full((1, 1)), _full((1, D)), _full((1, D))],
        out_shape=[_out((tp, D), F32), _out((tp, D), F32),
                   _out((1, 1), F32), _out((1, D), F32),
                   _out((1, D), F32)],
        compiler_params=_cp(("arbitrary",), 40),
    )(*_in_hbm(h, g_final, target, f2, g_post2))


def _na_patterns(n_rows):
    pats = []
    for kind in range(3):
        pat = [[-1] * K_ROWS for _ in range(Q_ROWS)]
        for i in range(Q_ROWS):
            for jj in range(K_ROWS):
                if kind == 0 and jj < KH:
                    pat[i][jj] = jj - i + KH - 1
                elif kind == 1 and i <= jj < i + KH:
                    pat[i][jj] = jj - i + 3
                elif kind == 2 and K_ROWS - KH <= jj:
                    pat[i][jj] = jj - i - 1
        pats.append(pat)
    return pats


def _diag_onehot():
    q = np.arange(GRID_W)[:, None]
    kc = np.arange(GRID_W)[None, :]
    start = np.clip(q - KW // 2, 0, GRID_W - KW)
    col_in = (kc >= start) & (kc < start + KW)
    e = np.zeros((32, GRID_W, GRID_W), np.float32)
    for d in range(2 * KW - 1):
        e[d] = ((kc - q + KW - 1) == d) & col_in
    return e.reshape(32, GRID_W * GRID_W), col_in


def _rpb_collapse(dtb2, et):
    def body(d_ref, e_ref, o_ref):
        o_ref[...] = jnp.dot(d_ref[...], e_ref[...], preferred_element_type=F32, precision=lax.Precision.HIGHEST)

    out = (dtb2.shape[0], et.shape[1])
    return pl.pallas_call(
        body, name="rpb_collapse", grid=(1,), out_shape=_out(out, F32),
        in_specs=[_full(dtb2.shape), _full(et.shape)], out_specs=_full(out),
    )(*_in_hbm(dtb2, et))


def _bias_tables(rpb, n_rows):
    n_dr, n_dc = 2 * KH - 1, 2 * KW - 1
    pats = _na_patterns(n_rows)

    def body(rpb_ref, o_ref):
        h = pl.program_id(0)
        q = lax.broadcasted_iota(jnp.int32, (GRID_W, GRID_W), 0)
        kc = lax.broadcasted_iota(jnp.int32, (GRID_W, GRID_W), 1)
        start = jnp.clip(q - KW // 2, 0, GRID_W - KW)
        col_in = (kc >= start) & (kc < start + KW)
        diff = kc - q + (KW - 1)
        neg = jnp.full((GRID_W, GRID_W), NEG_INF, F32)
        band = []
        for dr in range(n_dr):
            acc = neg
            for d in range(n_dc):
                acc = jnp.where((diff == d) & col_in, rpb_ref[(h * n_dr + dr) * n_dc + d], acc)
            band.append(acc)
        for kind, pat in enumerate(pats):
            for i in range(Q_ROWS):
                for jj in range(K_ROWS):
                    o_ref[kind, 0, i * GRID_W:(i + 1) * GRID_W, jj * GRID_W:(jj + 1) * GRID_W] = (
                        band[pat[i][jj]] if pat[i][jj] >= 0 else neg)

    return pl.pallas_call(
        body, name="bias_tables", grid=(N_HEADS,),
        in_specs=[pl.BlockSpec(memory_space=pltpu.SMEM)],
        out_specs=pl.BlockSpec((3, 1, QB, KB), lambda h: (0, h, 0, 0)),
        out_shape=_out((3, N_HEADS, QB, KB), F32),
        compiler_params=_cp(("arbitrary",), 32),
    )(rpb.reshape(-1))


def _attn_geometry(n_tok):
    n_rows = n_tok // GRID_W
    assert n_rows % Q_ROWS == 0 and n_rows >= K_ROWS
    return n_rows, n_rows // Q_ROWS


def _attn_probs(qh, kh, kmh, bias, scale):
    s = _dg(qh, kh, NT) * scale + bias
    sm = _dg(qh, kmh, NT) * scale
    m = jnp.maximum(jnp.max(s, axis=-1, keepdims=True), jnp.max(sm, axis=-1, keepdims=True))
    p = jnp.exp(s - m)
    pm = jnp.exp(sm - m)
    inv = 1.0 / (jnp.sum(p, axis=-1, keepdims=True) + jnp.sum(pm, axis=-1, keepdims=True))
    return p * inv, pm * inv


def _meta_probs(qmh, kmh, scale):
    s = _dg(qmh, kmh, NT) * scale
    p = jnp.exp(s - jnp.max(s, axis=-1, keepdims=True))
    return p / jnp.sum(p, axis=-1, keepdims=True)


def _step_rows(r, n_rows):
    q0 = pl.multiple_of(N_META + r * QB, 16)
    k0 = pl.multiple_of(N_META + jnp.clip(Q_ROWS * r - (K_ROWS - KH), 0, n_rows - K_ROWS) * GRID_W, 16)
    return q0, k0


def _attn_fwd(q, k, v, bias, n_tok, comm=None, bounds=()):
    tp = q.shape[0]
    n_rows, n_steps = _attn_geometry(n_tok)
    scale = HEAD_DIM ** -0.5

    def body(q_ref, k_ref, v_ref, b_ref, o_ref):
        r = pl.program_id(1)
        km = k_ref[0:N_META, :]
        vm = v_ref[0:N_META, :]

        @pl.when(r == 0)
        def _():
            qm = q_ref[0:N_META, :]
            outs = []
            for hh in range(2):
                sl = slice(hh * HEAD_DIM, (hh + 1) * HEAD_DIM)
                p = _meta_probs(qm[:, sl], km[:, sl], scale)
                outs.append(_dot(p.astype(BF16), vm[:, sl]))
            o_ref[0:N_META, :] = jnp.concatenate(outs, axis=1)
            o_ref[N_META + n_tok:, :] = jnp.zeros((tp - N_META - n_tok, 2 * HEAD_DIM), F32)

        q0, k0 = _step_rows(r, n_rows)
        qb = q_ref[pl.ds(q0, QB), :]
        kb = k_ref[pl.ds(k0, KB), :]
        vb = v_ref[pl.ds(k0, KB), :]
        outs = []
        for hh in range(2):
            sl = slice(hh * HEAD_DIM, (hh + 1) * HEAD_DIM)
            p, pm = _attn_probs(qb[:, sl], kb[:, sl], km[:, sl], b_ref[0, hh], scale)
            outs.append(_dot(p.astype(BF16), vb[:, sl]) + _dot(pm.astype(BF16), vm[:, sl]))
        o_ref[pl.ds(q0, QB), :] = jnp.concatenate(outs, axis=1)

    def bias_map(hp, r):
        return (jnp.where(r == 0, 0, jnp.where(r == n_steps - 1, 2, 1)), hp, 0, 0)

    col = pl.BlockSpec((tp, 2 * HEAD_DIM), lambda hp, r: (0, hp))
    return _call(
        body, comm, bounds, (q, k, v, bias), name="attn_fwd", grid=(N_HEADS // 2, n_steps),
        in_specs=[col, col, col, pl.BlockSpec((1, 2, QB, KB), bias_map)],
        out_specs=[col], out_shape=[_out((tp, NA_W), F32)],
        compiler_params=_cp(("arbitrary", "arbitrary"), 40))


def _attn_bwd(q, k, v, bias, do, n_tok, comm=None, bounds=()):
    tp = q.shape[0]
    n_rows, n_steps = _attn_geometry(n_tok)
    scale = HEAD_DIM ** -0.5
    pats = _na_patterns(n_rows)

    def body(q_ref, k_ref, v_ref, b_ref, do_ref, dq_ref, dk_ref, dv_ref, dtb_ref):
        r = pl.program_id(1)
        km = k_ref[0:N_META, :]
        vm = v_ref[0:N_META, :]

        @pl.when(r == 0)
        def _():
            dk_ref[...] = jnp.zeros_like(dk_ref)
            dv_ref[...] = jnp.zeros_like(dv_ref)
            dtb_ref[...] = jnp.zeros_like(dtb_ref)
            dq_ref[N_META + n_tok:, :] = jnp.zeros((tp - N_META - n_tok, 2 * HEAD_DIM), F32)
            qm = q_ref[0:N_META, :]
            dom = do_ref[0:N_META, :].astype(BF16)
            dqs, dks, dvs = [], [], []
            for hh in range(2):
                sl = slice(hh * HEAD_DIM, (hh + 1) * HEAD_DIM)
                p = _meta_probs(qm[:, sl], km[:, sl], scale)
                dp = _dg(dom[:, sl], vm[:, sl], NT)
                ds = (p * (dp - jnp.sum(dp * p, axis=-1, keepdims=True))).astype(BF16)
                dvs.append(_dg(p.astype(BF16), dom[:, sl], TN))
                dqs.append(_dot(ds, km[:, sl]) * scale)
                dks.append(_dg(ds, qm[:, sl], TN) * scale)
            dq_ref[0:N_META, :] = jnp.concatenate(dqs, axis=1)
            dk_ref[0:N_META, :] += jnp.concatenate(dks, axis=1)
            dv_ref[0:N_META, :] += jnp.concatenate(dvs, axis=1)

        q0, k0 = _step_rows(r, n_rows)
        qb = q_ref[pl.ds(q0, QB), :]
        kb = k_ref[pl.ds(k0, KB), :]
        vb = v_ref[pl.ds(k0, KB), :]
        dob = do_ref[pl.ds(q0, QB), :].astype(BF16)
        dqs, dks, dvs, dkms, dvms, dss = [], [], [], [], [], []
        for hh in range(2):
            sl = slice(hh * HEAD_DIM, (hh + 1) * HEAD_DIM)
            qh, kh, vh, kmh, vmh, doh = qb[:, sl], kb[:, sl], vb[:, sl], km[:, sl], vm[:, sl], dob[:, sl]
            p, pm = _attn_probs(qh, kh, kmh, b_ref[0, hh], scale)
            dp = _dg(doh, vh, NT)
            dpm = _dg(doh, vmh, NT)
            delta = jnp.sum(dp * p, axis=-1, keepdims=True) + jnp.sum(dpm * pm, axis=-1, keepdims=True)
            ds = p * (dp - delta)
            dsb = ds.astype(BF16)
            dsmb = (pm * (dpm - delta)).astype(BF16)
            dss.append(ds)
            dvs.append(_dg(p.astype(BF16), doh, TN))
            dvms.append(_dg(pm.astype(BF16), doh, TN))
            dqs.append((_dot(dsb, kh) + _dot(dsmb, kmh)) * scale)
            dks.append(_dg(dsb, qh, TN) * scale)
            dkms.append(_dg(dsmb, qh, TN) * scale)
        dq_ref[pl.ds(q0, QB), :] = jnp.concatenate(dqs, axis=1)
        dk_ref[pl.ds(k0, KB), :] += jnp.concatenate(dks, axis=1)
        dv_ref[pl.ds(k0, KB), :] += jnp.concatenate(dvs, axis=1)
        dk_ref[0:N_META, :] += jnp.concatenate(dkms, axis=1)
        dv_ref[0:N_META, :] += jnp.concatenate(dvms, axis=1)

        def add_bias_grad(pat):
            for hh in range(2):
                for i in range(Q_ROWS):
                    for jj in range(K_ROWS):
                        if pat[i][jj] >= 0:
                            dtb_ref[hh, pat[i][jj]] += dss[hh][i * GRID_W:(i + 1) * GRID_W,
                                                               jj * GRID_W:(jj + 1) * GRID_W]

        @pl.when(r == 0)
        def _():
            add_bias_grad(pats[0])

        @pl.when((r > 0) & (r < n_steps - 1))
        def _():
            add_bias_grad(pats[1])

        @pl.when(r == n_steps - 1)
        def _():
            add_bias_grad(pats[2])

    def bias_map(hp, r):
        return (jnp.where(r == 0, 0, jnp.where(r == n_steps - 1, 2, 1)), hp, 0, 0)

    col = pl.BlockSpec((tp, 2 * HEAD_DIM), lambda hp, r: (0, hp))
    n_dr = 2 * KH - 1
    return _call(
        body, comm, bounds, (q, k, v, bias, do), name="attn_bwd", grid=(N_HEADS // 2, n_steps),
        in_specs=[col, col, col, pl.BlockSpec((1, 2, QB, KB), bias_map), col],
        out_specs=[col, col, col, pl.BlockSpec((2, n_dr, GRID_W, GRID_W), lambda hp, r: (hp, 0, 0, 0))],
        out_shape=[_out((tp, NA_W), F32)] * 3 +
                  [_out((N_HEADS, n_dr, GRID_W, GRID_W), F32)],
        compiler_params=_cp(("arbitrary", "arbitrary"), 48))


def _repeat_onehot():
    return np.repeat(np.eye(2 * S5_G, dtype=np.float32), S5_H, axis=0)


def _s5_disc_math(lam_re, lam_im, log_dt, b_re, b_im, rep):
    dt = jnp.exp(log_dt)
    ea = jnp.exp(lam_re * dt)
    a_re = ea * jnp.cos(lam_im * dt)
    a_im = ea * jnp.sin(lam_im * dt)
    den = lam_re * lam_re + lam_im * lam_im
    c_re = ((a_re - 1.0) * lam_re + a_im * lam_im) / den
    c_im = (a_im * lam_re - (a_re - 1.0) * lam_im) / den
    ce_re = jnp.dot(rep, c_re, preferred_element_type=F32, precision=lax.Precision.HIGHEST)
    ce_im = jnp.dot(rep, c_im, preferred_element_type=F32, precision=lax.Precision.HIGHEST)
    return a_re, a_im, ce_re * b_re - ce_im * b_im, ce_re * b_im + ce_im * b_re


def _s5_blocks():
    gl = S5_G // N_BUNDLE
    half = gl * S5_P
    out = []
    for d in range(2):
        for g in range(S5_G):
            b, k = divmod(g, gl)
            dg = d * S5_G + g
            out.append((d, b, slice(k * S5_H, (k + 1) * S5_H), slice(k * S5_P, (k + 1) * S5_P),
                        slice(half + k * S5_P, half + (k + 1) * S5_P), slice(dg * S5_H, (dg + 1) * S5_H),
                        slice(dg, dg + 1)))
    return out


def _s5_params(lam_re, lam_im, log_dt, b_re, b_im, c_re, c_im):
    cw, sw = S5_W // N_BUNDLE, 2 * (S5_G // N_BUNDLE) * S5_P

    def body(lr, li, ld, br, bi, cr, ci, rep_ref, a_ref, a1_ref, a2_ref, bm_ref, cm_ref):
        a_re, a_im, bb_re, bb_im = _s5_disc_math(lr[...], li[...], ld[...], br[...], bi[...], rep_ref[...])
        cc_re = cr[...]
        cc_im = ci[...]
        bm_ref[...] = jnp.zeros_like(bm_ref)
        cm_ref[...] = jnp.zeros_like(cm_ref)
        for d, b, rows, re, im, nat, one in _s5_blocks():
            bm_ref[d, b, rows, re] = bb_re[nat, :].astype(BF16)
            bm_ref[d, b, rows, im] = bb_im[nat, :].astype(BF16)
            cm_ref[d, b, rows, re] = cc_re[nat, :].astype(BF16)
            cm_ref[d, b, rows, im] = (-cc_im[nat, :]).astype(BF16)
            a_ref[d, b, :, re] = a_re[one, :]
            a_ref[d, b, :, im] = a_im[one, :]
            k = rows.start // S5_H
            lanes = slice((k % 2) * S5_P, (k % 2 + 1) * S5_P)
            for part, (v1, v2) in enumerate(((a_re[one, :], a_im[one, :]), (a_re[one, :], -a_im[one, :]))):
                sub = slice(4 * part + k // 2, 4 * part + k // 2 + 1)
                a1_ref[d, b, sub, lanes] = v1
                a2_ref[d, b, sub, lanes] = v2

    args = (lam_re, lam_im, log_dt, b_re, b_im, c_re, c_im, jnp.asarray(_repeat_onehot()))
    outs = [((2, N_BUNDLE, 1, sw), F32)] + [((2, N_BUNDLE, 8, 128), F32)] * 2 + [((2, N_BUNDLE, cw, sw), BF16)] * 2
    return pl.pallas_call(
        body, name="s5_params", grid=(1,), in_specs=[_full(a.shape) for a in args],
        out_specs=[_full(s) for s, _ in outs], out_shape=[_out(s, dt) for s, dt in outs],
    )(*_in_hbm(*args))


def _s5_params_bwd(lam_re, lam_im, log_dt, b_re, b_im, da, dbm, dcm):
    n, nb = 2 * S5_G, 2 * S5_G * S5_H

    def body(lr, li, ld, br, bi, rep_ref, da_ref, dbm_ref, dcm_ref, o_lr, o_li, o_ld, o_br, o_bi, o_cr, o_ci,
             dar_s, dai_s, dbr_s, dbi_s):
        for d, b, rows, re, im, nat, one in _s5_blocks():
            dbr_s[nat, :] = dbm_ref[d, b, rows, re]
            dbi_s[nat, :] = dbm_ref[d, b, rows, im]
            o_cr[nat, :] = dcm_ref[d, b, rows, re]
            o_ci[nat, :] = -dcm_ref[d, b, rows, im]
            dar_s[one, :] = da_ref[d, b, :, re]
            dai_s[one, :] = da_ref[d, b, :, im]
        rep = rep_ref[...]
        _, vjp = jax.vjp(lambda p, q, r, s, t: _s5_disc_math(p, q, r, s, t, rep),
                         lr[...], li[...], ld[...], br[...], bi[...])
        o_lr[...], o_li[...], o_ld[...], o_br[...], o_bi[...] = vjp((dar_s[...], dai_s[...], dbr_s[...], dbi_s[...]))

    args = (lam_re, lam_im, log_dt, b_re, b_im, jnp.asarray(_repeat_onehot()), da, dbm, dcm)
    outs = [(n, S5_P)] * 2 + [(n, 1)] + [(nb, S5_P)] * 4
    return pl.pallas_call(
        body, name="s5_params_bwd", grid=(1,), in_specs=[_full(a.shape) for a in args],
        out_specs=[_full(s) for s in outs], out_shape=[_out(s, F32) for s in outs],
        scratch_shapes=[pltpu.VMEM((n, S5_P), F32)] * 2 + [pltpu.VMEM((nb, S5_P), F32)] * 2,
    )(*_in_hbm(*args))


def _tiles_store(ref, base, val):
    for i in range(val.shape[0] // 8):
        for c in range(8):
            ref[pl.ds(base + (8 * i + c) * 8, 8), :] = val[8 * i:8 * i + 8, 128 * c:128 * (c + 1)]


def _tiles_load(ref, base, n):
    return jnp.concatenate(
        [jnp.concatenate([ref[pl.ds(base + (8 * i + c) * 8, 8), :] for c in range(8)], axis=1) for i in range(n // 8)],
        axis=0)


def _time_rows(base, t):
    return pl.ds(base + (t // 8) * 64 + t % 8, 8, stride=8)


def _scan(chains, n):
    xs = [c["x"] for c in chains]
    for k in range(n):
        for ci, c in enumerate(chains):
            t = n - 1 - k if c["reverse"] else k
            if c["prev"] is not None:
                c["prev"][_time_rows(c["prev_base"], t), :] = xs[ci]
            xs[ci] = c["a1"] * xs[ci] + pltpu.roll(c["a2"] * xs[ci], 4, axis=0) + c["src"][_time_rows(0, t), :]
            if c["dst"] is not None:
                c["dst"][_time_rows(0, t), :] = xs[ci]
    return xs


def _chain(x, a1, a2, src, dst=None, prev=None, prev_base=0, reverse=False):
    return dict(x=x, a1=a1, a2=a2, src=src, dst=dst, prev=prev, prev_base=prev_base, reverse=reverse)


def _s5_fwd(u, d_skip, a1, a2, bm, cm, length, comm=None, bounds=()):
    tp = u.shape[0]
    cw = S5_W // N_BUNDLE
    sw = bm.shape[-1]
    n_full, n_tail = divmod(length, SCAN_CHUNK)
    t_tail = n_full * SCAN_CHUNK

    nbs = N_BUNDLE

    def body(u_ref, d_ref, a1_ref, a2_ref, bm_ref, cm_ref, y_ref, bnd_ref, *scratch):
        y_ref[...] = u_ref[...] * d_ref[...]
        ins, xss = (scratch[0:nbs], scratch[nbs:2 * nbs]), (scratch[2 * nbs:3 * nbs], scratch[3 * nbs:])
        cols = [slice(b * cw, (b + 1) * cw) for b in range(nbs)]

        def keep(dr, chunk, xs):
            for b in range(nbs):
                bnd_ref[dr, b, chunk] = xs[b]

        def load(dr, t0, n):
            for b in range(nbs):
                _tiles_store(ins[dr][b], 0, _dot(u_ref[pl.ds(t0, n), cols[b]].astype(BF16), bm_ref[dr, b]))

        def chains(dr, xs):
            return [_chain(xs[b], a1_ref[dr, b], a2_ref[dr, b], ins[dr][b], dst=xss[dr][b], reverse=dr == 1)
                    for b in range(nbs)]

        def emit(dr, t0, n):
            for b in range(nbs):
                y_ref[pl.ds(t0, n), cols[b]] += _dg(_tiles_load(xss[dr][b], 0, n).astype(BF16), cm_ref[dr, b], NT)

        zero = (jnp.zeros((8, 128), F32),) * nbs
        xb = zero
        if n_tail:
            keep(1, n_full, xb)
            load(1, t_tail, n_tail)
            xb = tuple(_scan(chains(1, xb), n_tail))
            emit(1, t_tail, n_tail)

        def pair(i, carry):
            j = n_full - 1 - i
            t0s = (pl.multiple_of(i * SCAN_CHUNK, SCAN_CHUNK), pl.multiple_of(j * SCAN_CHUNK, SCAN_CHUNK))
            keep(0, i, carry[0])
            keep(1, j, carry[1])
            for dr in range(2):
                load(dr, t0s[dr], SCAN_CHUNK)
            out = _scan(chains(0, carry[0]) + chains(1, carry[1]), SCAN_CHUNK)
            for dr in range(2):
                emit(dr, t0s[dr], SCAN_CHUNK)
            return tuple(out[:nbs]), tuple(out[nbs:])

        xf, _ = lax.fori_loop(0, n_full, pair, (zero, xb))
        if n_tail:
            keep(0, n_full, xf)
            load(0, t_tail, n_tail)
            _scan(chains(0, xf), n_tail)
            emit(0, t_tail, n_tail)

    n_chunks = n_full + (1 if n_tail else 0)
    tile = pl.BlockSpec((2, nbs, 8, 128), lambda b: (0, b, 0, 0))
    return _call(
        body, comm, bounds, (u, d_skip, a1, a2, bm, cm), name="s5_fwd", grid=(N_BUNDLE // nbs,),
        in_specs=[pl.BlockSpec((tp, nbs * cw), lambda b: (0, b)), pl.BlockSpec((1, nbs * cw), lambda b: (0, b)),
                  tile, tile, pl.BlockSpec((2, nbs, cw, sw), lambda b: (0, b, 0, 0)),
                  pl.BlockSpec((2, nbs, cw, sw), lambda b: (0, b, 0, 0))],
        out_specs=[pl.BlockSpec((tp, nbs * cw), lambda b: (0, b)),
                   pl.BlockSpec((2, nbs, n_chunks, 8, 128), lambda b: (0, b, 0, 0, 0))],
        out_shape=[_out((tp, S5_W), F32), _out((2, N_BUNDLE, n_chunks, 8, 128), F32)],
        scratch_shapes=[pltpu.VMEM((SCAN_CHUNK * 8, 128), F32)] * (4 * nbs),
        compiler_params=_cp(("arbitrary",), 48))


def _s5_bwd(u, dy, d_skip, a, a1, a2, bm, cm, bnd, length):
    tp = u.shape[0]
    cw = S5_W // N_BUNDLE
    sw = bm.shape[-1]
    half = sw // 2
    n_full, n_tail = divmod(length, SCAN_CHUNK)
    t_tail = n_full * SCAN_CHUNK
    n_chunks = bnd.shape[2]
    nbs = 2

    def body(u_ref, dy_ref, d_ref, a_ref, a1_ref, a2_ref, bm_ref, cm_ref, bnd_ref, du_ref, dd_ref, dbm_ref, dcm_ref,
             da_ref, *scratch):
        du_ref[...] = dy_ref[...] * d_ref[...]
        dd_ref[...] = jnp.sum(dy_ref[...] * u_ref[...], axis=0, keepdims=True)
        dbm_ref[...] = jnp.zeros_like(dbm_ref)
        dcm_ref[...] = jnp.zeros_like(dcm_ref)
        da_ref[...] = jnp.zeros_like(da_ref)
        bu_s, dx_s, g_s, xp_s = ([scratch[(k * 2 + dr) * nbs:(k * 2 + dr + 1) * nbs] for dr in range(2)] for k in range(4))
        cols = [slice(b * cw, (b + 1) * cw) for b in range(nbs)]

        def chains(dr, chunk, t0, n, gs):
            out = []
            for b in range(nbs):
                _tiles_store(bu_s[dr][b], 0, _dot(u_ref[pl.ds(t0, n), cols[b]].astype(BF16), bm_ref[dr, b]))
                _tiles_store(dx_s[dr][b], 0, _dot(dy_ref[pl.ds(t0, n), cols[b]].astype(BF16), cm_ref[dr, b]))
                out.append(_chain(bnd_ref[dr, b, chunk], a1_ref[dr, b], a2_ref[dr, b], bu_s[dr][b],
                                  prev=xp_s[dr][b], reverse=dr == 1))
                out.append(_chain(gs[b], a1_ref[dr, b], -a2_ref[dr, b], dx_s[dr][b], dst=g_s[dr][b], reverse=dr == 0))
            return out

        def emit(dr, t0, n):
            rows = pl.ds(t0, n)
            for b in range(nbs):
                ub = u_ref[rows, cols[b]].astype(BF16)
                dyb = dy_ref[rows, cols[b]].astype(BF16)
                g = _tiles_load(g_s[dr][b], 0, n)
                gb = g.astype(BF16)
                du_ref[rows, cols[b]] += _dg(gb, bm_ref[dr, b], NT)
                dbm_ref[dr, b] += _dg(ub, gb, TN)
                xp = _tiles_load(xp_s[dr][b], 0, n)
                xp_r, xp_i = xp[:, 0:half], xp[:, half:]
                g_r, g_i = g[:, 0:half], g[:, half:]
                a_re = a_ref[dr, b, :, 0:half]
                a_im = a_ref[dr, b, :, half:]
                bu = _dot(ub, bm_ref[dr, b])
                x_r = a_re * xp_r - a_im * xp_i + bu[:, 0:half]
                x_i = a_re * xp_i + a_im * xp_r + bu[:, half:]
                dcm_ref[dr, b] += _dg(dyb, jnp.concatenate([x_r, x_i], axis=1).astype(BF16), TN)
                da_ref[dr, b] += jnp.concatenate([jnp.sum(g_r * xp_r + g_i * xp_i, axis=0, keepdims=True),
                                                  jnp.sum(g_i * xp_r - g_r * xp_i, axis=0, keepdims=True)], axis=1)

        def adjoints(out):
            return tuple(out[1::2])

        zero = (jnp.zeros((8, 128), F32),) * nbs
        g0 = zero
        if n_tail:
            g0 = adjoints(_scan(chains(0, n_full, t_tail, n_tail, g0), n_tail))
            emit(0, t_tail, n_tail)

        def pair(i, carry):
            j = n_full - 1 - i
            t0 = (pl.multiple_of(j * SCAN_CHUNK, SCAN_CHUNK), pl.multiple_of(i * SCAN_CHUNK, SCAN_CHUNK))
            both = chains(0, j, t0[0], SCAN_CHUNK, carry[0]) + chains(1, i, t0[1], SCAN_CHUNK, carry[1])
            out = _scan(both, SCAN_CHUNK)
            emit(0, t0[0], SCAN_CHUNK)
            emit(1, t0[1], SCAN_CHUNK)
            return adjoints(out[:2 * nbs]), adjoints(out[2 * nbs:])

        _, g1 = lax.fori_loop(0, n_full, pair, (g0, zero))
        if n_tail:
            _scan(chains(1, n_full, t_tail, n_tail, g1), n_tail)
            emit(1, t_tail, n_tail)

    tile = pl.BlockSpec((2, nbs, 8, 128), lambda b: (0, b, 0, 0))
    wide = pl.BlockSpec((2, nbs, cw, sw), lambda b: (0, b, 0, 0))
    col = pl.BlockSpec((tp, nbs * cw), lambda b: (0, b))
    row = pl.BlockSpec((1, nbs * cw), lambda b: (0, b))
    arow = pl.BlockSpec((2, nbs, 1, sw), lambda b: (0, b, 0, 0))
    return pl.pallas_call(
        body, name="s5_bwd", grid=(N_BUNDLE // nbs,),
        in_specs=[col, col, row, arow, tile, tile, wide, wide,
                  pl.BlockSpec((2, nbs, n_chunks, 8, 128), lambda b: (0, b, 0, 0, 0))],
        out_specs=[col, row, wide, wide, arow],
        out_shape=[_out((tp, S5_W), F32), _out((1, S5_W), F32),
                   _out((2, N_BUNDLE, cw, sw), F32), _out((2, N_BUNDLE, cw, sw), F32),
                   _out((2, N_BUNDLE, 1, sw), F32)],
        scratch_shapes=[pltpu.VMEM((SCAN_CHUNK * 8, 128), F32)] * (8 * nbs),
        compiler_params=_cp(("arbitrary",), 56),
    )(*_in_hbm(u, dy, d_skip, a, a1, a2, bm, cm, bnd))


def _row_tile(tp):
    return max(tm for tm in range(16, 449, 16) if tp % tm == 0)


def _step(x, target, bufs, gains, s5, rpb, c_arr, kc_arr):
    first = ["ffn1_w_gate", "ffn1_w_up", "ffn1_w_down", "meta_tokens"]
    w = dict(zip(first, _run_comm("gather_ffn1", _gather_comm([bufs[n] for n in first]))))
    meta = w["meta_tokens"].transpose(1, 0, 2).reshape(N_META, D)
    n_tok = x.shape[0]
    length = N_META + n_tok
    tp = length + 16
    tm = _row_tile(tp)
    tmb = tm
    n_rows = n_tok // GRID_W
    pad = jnp.zeros((tp - length, D), F32)
    h0 = jnp.concatenate([meta, x, pad], axis=0)
    tgt = jnp.concatenate([jnp.zeros((N_META, D), F32), target, pad], axis=0)

    s5p = (s5["lam_re"], s5["lam_im"], s5["log_dt"].reshape(2 * S5_G, 1), s5["b_re"], s5["b_im"])
    a_m, a1_m, a2_m, bm16, cm16 = _s5_params(*s5p, s5["c_re"], s5["c_im"])
    bias = _bias_tables(rpb, n_rows)

    mid = ["w_in", "s5_w_glu", "w_out"]
    (h1, gate1, up1, f1), got = _ffn_fwd(
        "ffn1_fwd", h0, gains["ffn1_pre_g"], gains["ffn1_post_g"], w["ffn1_w_gate"], w["ffn1_w_up"], w["ffn1_w_down"],
        tm, _gather_comm([bufs[n] for n in mid]), (0, (tp // tm) * N_CHIP * 3 // 5))
    w.update(zip(mid, got))
    q, k, v, u = _mix_in(h1, gains["mix_pre_g"], w["w_in"], tm)
    (o_na,), (gate_ici,) = _attn_fwd(q, k, v, bias, n_tok, _gather_comm([bufs["ffn2_w_gate"]], pair=False), (0,))
    (y_pre, s5_bnd), (w["ffn2_w_gate"], up_ici, down_ici) = _s5_fwd(
        u, gains["s5_d"], a1_m, a2_m, bm16, cm16, length,
        _merge_comm(_gather_comm([gate_ici], ici=False),
                    _gather_comm([bufs["ffn2_w_up"], bufs["ffn2_w_down"]], pair=False)), (0,))
    w_glu = w["s5_w_glu"].reshape(S5_W, S5_W)
    w_out = w["w_out"].reshape(D, D)
    (h2, mix), (w["ffn2_w_up"], w["ffn2_w_down"]) = _mix_out(
        o_na, y_pre, h1, w_glu, gains["s5_b_glu"], gains["na_out_g"], gains["s5_out_g"], w_out, gains["mix_post_g"], tm,
        _gather_comm([up_ici, down_ici], ici=False), (0,))
    (h3, gate2, up2, f2), _ = _ffn_fwd("ffn2_fwd", h2, gains["ffn2_pre_g"], gains["ffn2_post_g"],
                                       w["ffn2_w_gate"], w["ffn2_w_up"], w["ffn2_w_down"], tm)
    dh3, df2, loss, dg_final, dg_post2 = _final_loss(h3, gains["final_g"], tgt, f2, gains["ffn2_post_g"], n_tok, tm)

    ffn2 = ["ffn2_w_gate", "ffn2_w_up", "ffn2_w_down"]
    ffn1 = ["ffn1_w_gate", "ffn1_w_up", "ffn1_w_down"]
    out2 = _ffn_bwd("ffn2_bwd", h2, gains["ffn2_pre_g"], df2, gate2, up2,
                    w["ffn2_w_gate"], w["ffn2_w_up"], w["ffn2_w_down"], tmb)
    dxn2 = out2[3]
    sums2 = [_chip_sum("chip_sum_" + n, g, r, c_arr) for n, g, r in zip(ffn2, out2[0:3], out2[4:7])]
    (dh2, dg_pre2), _ = _ffn_pre_bwd("ffn2_pre_bwd", dh3, dxn2, h2, gains["ffn2_pre_g"], tm)
    do_na, dy_pre, dw_out, dw_glu, dg_mpost, dg_na, dg_s5, db_glu = _mix_out_bwd(
        dh2, mix, o_na, y_pre, w_glu, gains["s5_b_glu"], gains["na_out_g"], gains["s5_out_g"], w_out,
        gains["mix_post_g"], tm)
    (dq, dk, dv, dtb), recv3 = _attn_bwd(q, k, v, bias, do_na, n_tok, _scatter_comm(sums2), (0,))
    totals2 = [_total_sum("total_sum_" + n, s, r, kc_arr) for n, s, r in zip(ffn2, sums2, recv3)]
    du, dd, dbm, dcm, da_m = _s5_bwd(u, dy_pre, gains["s5_d"], a_m, a1_m, a2_m, bm16, cm16, s5_bnd, length)
    (dh1, df1, dw_in, dg_mpre, dg_post1), done2 = _mix_in_bwd(
        dq, dk, dv, du, h1, gains["mix_pre_g"], w["w_in"], dh2, f1, gains["ffn1_post_g"], tm,
        _assemble_comm(totals2), (0,))
    pieces = dict(zip(ffn2, done2))
    out1 = _ffn_bwd("ffn1_bwd", h0, gains["ffn1_pre_g"], df1, gate1, up1,
                    w["ffn1_w_gate"], w["ffn1_w_up"], w["ffn1_w_down"], tmb)
    rest = [dw_in, dw_glu.reshape(N_CHIP, S5_W // N_CHIP, S5_W), dw_out.reshape(N_CHIP, D // N_CHIP, D)]
    (dh0, dg_pre1), recv_rest = _ffn_pre_bwd("ffn1_pre_bwd", dh1, out1[3], h0, gains["ffn1_pre_g"], tm,
                                             _exchange_comm(rest), (0,))
    last = ffn1 + mid
    sums = [_chip_sum("chip_sum_" + n, g, r, c_arr)
            for n, g, r in zip(last, list(out1[0:3]) + rest, list(out1[4:7]) + list(recv_rest))]
    recv3 = _run_comm("grad_chip_scatter", _scatter_comm(sums))
    totals = [_total_sum("total_sum_" + n, s, r, kc_arr) for n, s, r in zip(last, sums, recv3)]
    pieces.update(zip(last, _run_comm("grad_pair_assemble", _assemble_comm(totals))))

    e, _ = _diag_onehot()
    n_dr = 2 * KH - 1
    drpb = _rpb_collapse(dtb.reshape(N_HEADS * n_dr, GRID_W * GRID_W), jnp.asarray(e.T))
    drpb = drpb[:, :2 * KW - 1].reshape(N_HEADS, n_dr, 2 * KW - 1).transpose(1, 0, 2).reshape(N_HEADS * n_dr, 2 * KW - 1)
    dlam_re, dlam_im, dlog_dt, db_re, db_im, dc_re, dc_im = _s5_params_bwd(*s5p, da_m, dbm, dcm)

    small = {"ffn1_pre_g": dg_pre1, "ffn1_post_g": dg_post1, "mix_pre_g": dg_mpre, "na_rpb": drpb,
             "s5_lam_re": dlam_re, "s5_lam_im": dlam_im, "s5_log_dt": dlog_dt.reshape(2, S5_G),
             "s5_b_re": db_re, "s5_b_im": db_im, "s5_c_re": dc_re, "s5_c_im": dc_im,
             "s5_d": dd, "s5_b_glu": db_glu, "na_out_g": dg_na,
             "s5_out_g": dg_s5, "mix_post_g": dg_mpost, "ffn2_pre_g": dg_pre2, "ffn2_post_g": dg_post2,
             "final_g": dg_final}
    return loss[0, 0], dh0, pieces, small


def _mesh_pos():
    return lax.axis_index("x"), lax.axis_index("y"), lax.axis_index("c")


def _other_chips(x, y):
    return [(1 - x, y), (x, 1 - y), (1 - x, 1 - y)]


class _Comm:
    def __init__(self, ins, out_shape, aliases, parts):
        self.ins, self.out_shape, self.aliases, self.parts = list(ins), list(out_shape), dict(aliases), list(parts)
        self.n_sems = sum(p[0] for p in parts)

    def bases(self):
        out, base = [], 0
        for n_sems, _, _ in self.parts:
            out.append(base)
            base += n_sems
        return out


def _run_comm(name, comm):
    n_i, n_o = len(comm.ins), len(comm.out_shape)

    def body(*refs):
        ins, outs = refs[:n_i], refs[n_i:n_i + n_o]
        send_sems, recv_sems = refs[n_i + n_o:]
        for base, (_, start, finish) in zip(comm.bases(), comm.parts):
            start(ins, outs, send_sems, recv_sems, base)
            finish(ins, outs, send_sems, recv_sems, base)

    return pl.pallas_call(
        body, name=name, out_shape=comm.out_shape, in_specs=[ANY] * n_i, out_specs=[ANY] * n_o,
        input_output_aliases=comm.aliases,
        scratch_shapes=[pltpu.SemaphoreType.DMA((comm.n_sems,)), pltpu.SemaphoreType.DMA((comm.n_sems,))],
    )(*_in_hbm(*comm.ins))


def _call(body, comm, bounds, args, *, name, grid, in_specs, out_specs, out_shape, scratch_shapes=(),
          compiler_params=None):
    in_specs, out_specs, out_shape, scratch_shapes = list(in_specs), list(out_specs), list(out_shape), list(scratch_shapes)
    if comm is None:
        return pl.pallas_call(body, name=name, grid=grid, in_specs=in_specs, out_specs=out_specs, out_shape=out_shape,
                              scratch_shapes=scratch_shapes, compiler_params=compiler_params)(*_in_hbm(*args)), []
    n_in, n_out, n_scr = len(in_specs), len(out_specs), len(scratch_shapes)
    n_ci, n_co = len(comm.ins), len(comm.out_shape)
    n_steps = int(np.prod(grid))
    assert len(bounds) == len(comm.parts) and all(0 <= b < n_steps for b in bounds) and list(bounds) == sorted(bounds)

    def fused(*refs):
        a = n_in
        b = a + n_ci
        c = b + n_out
        d = c + n_co
        e = d + n_scr
        cargs = (refs[a:b], refs[c:d], refs[e], refs[e + 1])
        step = pl.program_id(0)
        for ax in range(1, len(grid)):
            step = step * grid[ax] + pl.program_id(ax)
        bases = comm.bases()
        for p, (_, start, finish) in enumerate(comm.parts):
            @pl.when(step == bounds[p])
            def _(p=p, start=start):
                if p > 0:
                    comm.parts[p - 1][2](*cargs, bases[p - 1])
                start(*cargs, bases[p])
        body(*(refs[:a] + refs[b:c] + refs[d:e]))

        @pl.when(step == n_steps - 1)
        def _():
            comm.parts[-1][2](*cargs, bases[-1])

    res = pl.pallas_call(
        fused, name=name, grid=grid, in_specs=in_specs + [ANY] * n_ci, out_specs=out_specs + [ANY] * n_co,
        out_shape=out_shape + comm.out_shape,
        scratch_shapes=scratch_shapes + [pltpu.SemaphoreType.DMA((comm.n_sems,)), pltpu.SemaphoreType.DMA((comm.n_sems,))],
        input_output_aliases={n_in + i: n_out + j for i, j in comm.aliases.items()},
        compiler_params=compiler_params)(*_in_hbm(*args, *comm.ins))
    return res[:n_out], res[n_out:]


def _remote(src, dst, send_sems, recv_sems, idx, to):
    return pltpu.make_async_remote_copy(src_ref=src, dst_ref=dst, send_sem=send_sems.at[idx],
                                        recv_sem=recv_sems.at[idx], device_id=to, device_id_type=MESH_ID)


def _gather_comm(bufs, ici=True, pair=True):
    n = len(bufs)

    def half(ref, k, pc):
        rh = ref.shape[1] // 2
        return ref.at[k, pl.ds(pc * rh, rh), :]

    def ici_start(ins, outs, ss, rs, base):
        x, y, c = _mesh_pos()
        for a in range(n):
            mine = half(outs[a], 2 * x + y, c)
            for j, chip in enumerate(_other_chips(x, y)):
                _remote(mine, mine, ss, rs, base + 3 * a + j, (*chip, c)).start()

    def ici_finish(ins, outs, ss, rs, base):
        x, y, c = _mesh_pos()
        for a in range(n):
            for j, chip in enumerate(_other_chips(x, y)):
                theirs = half(outs[a], 2 * chip[0] + chip[1], c)
                _remote(theirs, theirs, ss, rs, base + 3 * a + j, (*chip, c)).wait()

    def pair_copy(outs, ss, rs, base, a):
        x, y, c = _mesh_pos()
        rh = outs[a].shape[1] // 2
        held = outs[a].at[:, pl.ds(c * rh, rh), :]
        return _remote(held, held, ss, rs, base + a, (x, y, 1 - c))

    def pair_start(ins, outs, ss, rs, base):
        for a in range(n):
            pair_copy(outs, ss, rs, base, a).start()

    def pair_finish(ins, outs, ss, rs, base):
        for a in range(n):
            pair_copy(outs, ss, rs, base, a).wait()

    parts = ([(3 * n, ici_start, ici_finish)] if ici else []) + ([(n, pair_start, pair_finish)] if pair else [])
    return _Comm(bufs, [_out(b.shape, b.dtype) for b in bufs], {a: a for a in range(n)}, parts)


def _merge_comm(*comms):
    ins, shapes, aliases, subs, base = [], [], {}, [], 0
    for cm in comms:
        (n_sems, start, finish), = cm.parts
        i0, o0 = len(ins), len(shapes)
        subs.append((slice(i0, i0 + len(cm.ins)), slice(o0, o0 + len(cm.out_shape)), base, start, finish))
        aliases.update({i0 + i: o0 + j for i, j in cm.aliases.items()})
        ins += cm.ins
        shapes += cm.out_shape
        base += n_sems

    def start_all(ins_r, outs_r, ss, rs, b):
        for si, so, off, start, _ in subs:
            start(ins_r[si], outs_r[so], ss, rs, b + off)

    def finish_all(ins_r, outs_r, ss, rs, b):
        for si, so, off, _, finish in subs:
            finish(ins_r[si], outs_r[so], ss, rs, b + off)

    return _Comm(ins, shapes, aliases, [(base, start_all, finish_all)])


def _own_half_buffers(pieces, dtypes, kc_arr):
    n = len(pieces)

    def body(kc_ref, *refs):
        for a in range(n):
            refs[n + a][0] = refs[a][...].astype(dtypes[a])

    def half(p):
        return p.shape[0] // 2, p.shape[1]

    return pl.pallas_call(
        body, name="own_halves",
        out_shape=[_out((N_CHIP,) + p.shape, dt) for p, dt in zip(pieces, dtypes)],
        grid_spec=pltpu.PrefetchScalarGridSpec(
            num_scalar_prefetch=1, grid=(1,),
            in_specs=[pl.BlockSpec(half(p), lambda i, kc: (kc[1], 0)) for p in pieces],
            out_specs=[pl.BlockSpec((1,) + half(p), lambda i, kc: (kc[0], kc[1], 0)) for p in pieces]),
        compiler_params=_cp(("arbitrary",), 48),
    )(kc_arr, *_in_hbm(*pieces))


def _exchange_comm(grads):
    n = len(grads)

    def copy(ins, outs, ss, rs, base, a):
        x, y, c = _mesh_pos()
        rh = ins[a].shape[1] // 2
        return _remote(ins[a].at[:, pl.ds((1 - c) * rh, rh), :], outs[a], ss, rs, base + a, (x, y, 1 - c))

    def start(ins, outs, ss, rs, base):
        for a in range(n):
            copy(ins, outs, ss, rs, base, a).start()

    def finish(ins, outs, ss, rs, base):
        for a in range(n):
            copy(ins, outs, ss, rs, base, a).wait()

    shapes = [_out((N_CHIP, g.shape[1] // 2, g.shape[2]), g.dtype) for g in grads]
    return _Comm(grads, shapes, {}, [(n, start, finish)])


def _chip_sum(name, g, recv, c_arr):
    _, r, cc = g.shape
    rh = r // 2

    def body(c_ref, g_ref, r_ref, o_ref):
        o_ref[...] = (g_ref[...] + r_ref[...]).astype(BF16)

    return pl.pallas_call(
        body, name=name, out_shape=_out((N_CHIP, rh, cc), BF16),
        grid_spec=pltpu.PrefetchScalarGridSpec(
            num_scalar_prefetch=1, grid=(N_CHIP,),
            in_specs=[pl.BlockSpec((1, rh, cc), lambda j, c_ref: (j, c_ref[0], 0)),
                      pl.BlockSpec((1, rh, cc), lambda j, c_ref: (j, 0, 0))],
            out_specs=pl.BlockSpec((1, rh, cc), lambda j, c_ref: (j, 0, 0))),
        compiler_params=_cp(("arbitrary",), 32),
    )(c_arr, *_in_hbm(g, recv))


def _scatter_comm(sums):
    n = len(sums)

    def copies(ins, outs, ss, rs, base):
        x, y, c = _mesh_pos()
        return [_remote(ins[a].at[2 * chip[0] + chip[1]], outs[a].at[j], ss, rs, base + 3 * a + j, (*chip, c))
                for a in range(n) for j, chip in enumerate(_other_chips(x, y))]

    def start(ins, outs, ss, rs, base):
        for cp in copies(ins, outs, ss, rs, base):
            cp.start()

    def finish(ins, outs, ss, rs, base):
        for cp in copies(ins, outs, ss, rs, base):
            cp.wait()

    shapes = [_out((3,) + s.shape[1:], s.dtype) for s in sums]
    return _Comm(sums, shapes, {}, [(3 * n, start, finish)])


def _total_sum(name, sums, recv3, kc_arr):
    _, rh, cc = sums.shape

    def body(kc_ref, s_ref, r_ref, o_ref):
        t = s_ref[0].astype(F32) + r_ref[0].astype(F32)
        t = t + r_ref[1].astype(F32)
        o_ref[...] = t + r_ref[2].astype(F32)

    return pl.pallas_call(
        body, name=name, out_shape=_out((2 * rh, cc), F32),
        grid_spec=pltpu.PrefetchScalarGridSpec(
            num_scalar_prefetch=1, grid=(1,),
            in_specs=[pl.BlockSpec((1, rh, cc), lambda i, kc_ref: (kc_ref[0], 0, 0)),
                      pl.BlockSpec((3, rh, cc), lambda i, kc_ref: (0, 0, 0))],
            out_specs=pl.BlockSpec((rh, cc), lambda i, kc_ref: (kc_ref[1], 0))),
        compiler_params=_cp(("arbitrary",), 32),
    )(kc_arr, *_in_hbm(sums, recv3))


def _assemble_comm(totals):
    n = len(totals)

    def copy(outs, ss, rs, base, a):
        x, y, c = _mesh_pos()
        rh = outs[a].shape[0] // 2
        here = outs[a].at[pl.ds(c * rh, rh), :]
        return _remote(here, here, ss, rs, base + a, (x, y, 1 - c))

    def start(ins, outs, ss, rs, base):
        for a in range(n):
            copy(outs, ss, rs, base, a).start()

    def finish(ins, outs, ss, rs, base):
        for a in range(n):
            copy(outs, ss, rs, base, a).wait()

    shapes = [_out(t.shape, t.dtype) for t in totals]
    return _Comm(totals, shapes, {a: a for a in range(n)}, [(n, start, finish)])


def _small_allreduce(arrays):
    n = len(arrays)
    shapes = [a.shape for a in arrays]
    narrow_w = 64
    groups = [[a for a in range(n) if shapes[a][1] > narrow_w], [a for a in range(n) if shapes[a][1] <= narrow_w]]
    widths = [max(shapes[a][1] for a in groups[0]), 2 * narrow_w]
    offs, cols, heights = {}, {}, [0, 0]
    for a in groups[0]:
        offs[a], cols[a] = heights[0], 0
        heights[0] += shapes[a][0]
    rows = [-(-heights[0] // 8) * 8]
    heights = [0, 0]
    for a in sorted(groups[1], key=lambda a: -shapes[a][0]):
        side = 0 if heights[0] <= heights[1] else 1
        offs[a], cols[a] = heights[side], side * narrow_w
        heights[side] += shapes[a][0]
    rows.append(-(-max(heights) // 8) * 8)
    n_g = len(groups)

    def window(ref, a):
        return ref.at[offs[a]:offs[a] + shapes[a][0], cols[a]:cols[a] + shapes[a][1]]

    def body(*refs):
        ins, outs = refs[:n], refs[n:2 * n]
        pack, sib, csum, every = (refs[2 * n + i * n_g:2 * n + (i + 1) * n_g] for i in range(4))
        send_sems, recv_sems = refs[2 * n + 4 * n_g:]
        x, y, c = _mesh_pos()
        k = 2 * x + y
        for gi, g in enumerate(groups):
            pack[gi][...] = jnp.zeros_like(pack[gi])
            for a in g:
                window(pack[gi], a)[...] = ins[a][...]
        cps = [_remote(pack[gi], sib[gi], send_sems, recv_sems, gi, (x, y, 1 - c)) for gi in range(n_g)]
        for cp in cps:
            cp.start()
        for cp in cps:
            cp.wait()
        for gi in range(n_g):
            csum[gi][...] = pack[gi][...] + sib[gi][...]
            every[gi][k] = csum[gi][...]
        cps = [_remote(csum[gi], every[gi].at[k], send_sems, recv_sems, n_g + 3 * gi + j, (*chip, c))
               for gi in range(n_g) for j, chip in enumerate(_other_chips(x, y))]
        for cp in cps:
            cp.start()
        for cp in cps:
            cp.wait()
        for gi, g in enumerate(groups):
            pack[gi][...] = ((every[gi][0] + every[gi][1]) + every[gi][2]) + every[gi][3]
            for a in g:
                outs[a][...] = window(pack[gi], a)[...]

    bufs = [pltpu.VMEM((r, w), F32) for r, w in zip(rows, widths)]
    return pl.pallas_call(
        body, name="small_allreduce", grid=(1,), out_shape=[_out(s, F32) for s in shapes],
        in_specs=[_full(s) for s in shapes], out_specs=[_full(s) for s in shapes],
        scratch_shapes=bufs * 3 + [pltpu.VMEM((N_CHIP, r, w), F32) for r, w in zip(rows, widths)] +
                       [pltpu.SemaphoreType.DMA((4 * n_g,)), pltpu.SemaphoreType.DMA((4 * n_g,))],
        compiler_params=_cp(("arbitrary",), 40),
    )(*_in_hbm(*arrays))


def _adamw_small(ws, gs, ms, vs):
    n = len(ws)

    def body(*refs):
        w, g, m, v, d, mo, vo = (refs[i * n:(i + 1) * n] for i in range(7))
        for a in range(n):
            d[a][...], mo[a][...], vo[a][...] = _adamw_math(w[a][...], g[a][...], m[a][...], v[a][...])

    specs = [_full(w.shape) for w in ws]
    res = pl.pallas_call(
        body, name="adamw_small", grid=(1,), out_shape=[_out(w.shape, F32) for w in ws] * 3,
        in_specs=specs * 4, out_specs=specs * 3, compiler_params=_cp(("arbitrary",), 40),
    )(*_in_hbm(*ws, *gs, *ms, *vs))
    return res[:n], res[n:2 * n], res[2 * n:]


def _adamw_math(w, g, m, v):
    m = ADAM_B1 * m + (1.0 - ADAM_B1) * g
    v = ADAM_B2 * v + (1.0 - ADAM_B2) * (g * g)
    m_hat = m / (1.0 - ADAM_B1 ** ADAM_STEP)
    v_hat = v / (1.0 - ADAM_B2 ** ADAM_STEP)
    delta = -ADAM_LR * (m_hat / (jnp.sqrt(v_hat) + ADAM_EPS) + ADAM_WD * w)
    return delta, m, v


def _adamw(name, w, g, m, v):
    r, c = w.shape
    tr = max(t for t in range(8, 513, 8) if r % t == 0)

    def body(w_ref, g_ref, m_ref, v_ref, d_ref, mo_ref, vo_ref):
        d_ref[...], mo_ref[...], vo_ref[...] = _adamw_math(w_ref[...], g_ref[...], m_ref[...], v_ref[...])

    return pl.pallas_call(
        body, name=name, grid=(r // tr,), in_specs=[_rows(tr, c)] * 4, out_specs=[_rows(tr, c)] * 3,
        out_shape=[_out((r, c), F32)] * 3, compiler_params=_cp(("arbitrary",), 32),
    )(*_in_hbm(w, g, m, v))


def _as_matrix(name, a):
    if name == "na_rpb":
        return a[0].transpose(1, 0, 2).reshape(N_HEADS * (2 * KH - 1), 2 * KW - 1)
    if name in ("s5_b_re", "s5_b_im"):
        return a.transpose(0, 1, 2, 4, 3).reshape(2 * S5_G * S5_H, S5_P)
    if name in ("s5_c_re", "s5_c_im"):
        return a.reshape(2 * S5_G * S5_H, S5_P)
    if name in ("s5_lam_re", "s5_lam_im"):
        return a.reshape(2 * S5_G, S5_P)
    if name == "s5_log_dt":
        return a.reshape(2, S5_G)
    return a


def _from_matrix(name, m):
    if name == "na_rpb":
        return m.reshape(2 * KH - 1, N_HEADS, 2 * KW - 1).transpose(1, 0, 2)[None]
    if name in ("s5_b_re", "s5_b_im"):
        return m.reshape(1, 2, S5_G, S5_H, S5_P).transpose(0, 1, 2, 4, 3)
    if name in ("s5_c_re", "s5_c_im"):
        return m.reshape(1, 2, S5_G, S5_H, S5_P)
    if name in ("s5_lam_re", "s5_lam_im"):
        return m.reshape(1, 2, S5_G, S5_P)
    if name == "s5_log_dt":
        return m.reshape(1, 2, S5_G)
    return m


WEIGHTS = ["meta_tokens", "ffn1_pre_g", "ffn1_post_g", "ffn1_w_gate", "ffn1_w_up", "ffn1_w_down", "mix_pre_g", "w_in",
           "na_rpb", "s5_lam_re", "s5_lam_im", "s5_log_dt", "s5_b_re", "s5_b_im", "s5_c_re", "s5_c_im", "s5_d",
           "s5_w_glu", "s5_b_glu", "na_out_g", "s5_out_g", "w_out", "mix_post_g", "ffn2_pre_g", "ffn2_post_g",
           "ffn2_w_gate", "ffn2_w_up", "ffn2_w_down", "final_g"]
BIG = ["ffn1_w_gate", "ffn1_w_up", "ffn1_w_down", "w_in", "s5_w_glu", "w_out", "ffn2_w_gate", "ffn2_w_up",
       "ffn2_w_down"]
TRANSPOSED = ["ffn1_w_gate", "ffn1_w_up", "ffn2_w_gate", "ffn2_w_up"]
GAINS = ["ffn1_pre_g", "ffn1_post_g", "mix_pre_g", "s5_d", "s5_b_glu", "na_out_g", "s5_out_g", "mix_post_g",
         "ffn2_pre_g", "ffn2_post_g", "final_g"]
SMALL = [n for n in WEIGHTS if n not in BIG]


def kernel(*args):
    names = ["x"] + WEIGHTS + ["loss_target"] + ["m_" + n for n in WEIGHTS] + ["v_" + n for n in WEIGHTS]
    assert len(args) == len(names)
    given = dict(zip(names, args))
    x_pos, y_pos, c_pos = _mesh_pos()
    k_pos = 2 * x_pos + y_pos
    c_arr = jnp.reshape(c_pos, (1,)).astype(jnp.int32)
    kc_arr = jnp.stack([k_pos, c_pos]).astype(jnp.int32)

    def piece(name, a):
        return a[0].T if name in TRANSPOSED else a[0]

    def unpiece(name, a):
        return a.T[None] if name in TRANSPOSED else a[None]

    placed = BIG + ["meta_tokens"]
    bufs = dict(zip(placed, _own_half_buffers([piece(n, given[n]) for n in BIG] + [given["meta_tokens"]],
                                              [BF16] * len(BIG) + [F32], kc_arr)))

    gains = {n: given[n] for n in GAINS}
    s5 = {n: _as_matrix("s5_" + n, given["s5_" + n])
          for n in ["lam_re", "lam_im", "log_dt", "b_re", "b_im", "c_re", "c_im"]}
    loss, dh0, pieces, small = _step(given["x"][0], given["loss_target"][0], bufs, gains, s5, given["na_rpb"][0],
                                     c_arr, kc_arr)
    loss = lax.psum(loss, ("x", "y", "c"))
    n_tok = given["x"].shape[1]
    grad_x = dh0[N_META:N_META + n_tok][None]

    small["meta_tokens"] = dh0[:N_META]
    small = dict(zip(SMALL, _small_allreduce([small[n] for n in SMALL])))
    mc = D // N_CHIP
    small["meta_tokens"] = lax.dynamic_slice_in_dim(small["meta_tokens"], k_pos * mc, mc, 1)

    out_g, out_d, out_m, out_v = {}, {}, {}, {}
    for n in BIG:
        g2 = pieces[n]
        d2, m2, v2 = _adamw("adamw_" + n, piece(n, given[n]), g2, piece(n, given["m_" + n]),
                            piece(n, given["v_" + n]))
        out_g[n], out_d[n], out_m[n], out_v[n] = (unpiece(n, t) for t in (g2, d2, m2, v2))
    gs = [small[n] for n in SMALL]
    d2, m2, v2 = _adamw_small([_as_matrix(n, given[n]) for n in SMALL], gs,
                              [_as_matrix(n, given["m_" + n]) for n in SMALL],
                              [_as_matrix(n, given["v_" + n]) for n in SMALL])
    for n, g, dd, mm, vv in zip(SMALL, gs, d2, m2, v2):
        out_g[n], out_d[n], out_m[n], out_v[n] = (_from_matrix(n, t) for t in (g, dd, mm, vv))
    return (loss, grad_x, *[out_g[n] for n in WEIGHTS], *[out_d[n] for n in WEIGHTS],
            *[out_m[n] for n in WEIGHTS], *[out_v[n] for n in WEIGHTS])
```

```python
import functools
import math

import numpy as np
import jax
import jax.numpy as jnp
from jax import lax
from jax.experimental import pallas as pl
from jax.experimental.pallas import tpu as pltpu

F32 = jnp.float32
BF16 = jnp.bfloat16

D = 1024
N_META = 16
GRID_W = 64
NA_W = 512
S5_W = 512
HEAD_DIM = 64
N_HEADS = 8
KH = 8
KW = 16
S5_G = 32
S5_P = 64
S5_H = 16
N_BUNDLE = 4
FF = 2816
N_CHIP = 4
FC = FF // N_CHIP
EPS = 1e-6
NEG_INF = -1e30
Q_ROWS = 4
K_ROWS = 12
QB = Q_ROWS * GRID_W
KB = K_ROWS * GRID_W
SCAN_CHUNK = 256

ADAM_LR = 0.001
ADAM_B1 = 0.9
ADAM_B2 = 0.999
ADAM_EPS = 1e-08
ADAM_WD = 0.01
ADAM_STEP = 10

NT = (((1,), (1,)), ((), ()))
TN = (((0,), (0,)), ((), ()))
MESH_ID = pl.DeviceIdType.MESH


def _cp(sem=None, vmem_mb=None):
    kw = {}
    if sem is not None:
        kw["dimension_semantics"] = sem
    if vmem_mb is not None:
        kw["vmem_limit_bytes"] = vmem_mb << 20
    return pltpu.CompilerParams(**kw)


def _full(shape):
    n = len(shape)
    return pl.BlockSpec(shape, lambda *_: (0,) * n)


def _rows(tm, w):
    return pl.BlockSpec((tm, w), lambda i: (i, 0))


ANY = pl.BlockSpec(memory_space=pl.ANY)


def _rms(x, g):
    r = lax.rsqrt(jnp.mean(x * x, axis=-1, keepdims=True) + EPS)
    return x * r * g


def _rms_bwd(x, g, dy):
    r = lax.rsqrt(jnp.mean(x * x, axis=-1, keepdims=True) + EPS)
    xh = x * r
    dg = jnp.sum(dy * xh, axis=0, keepdims=True)
    dyg = dy * g
    dx = r * (dyg - xh * jnp.mean(dyg * xh, axis=-1, keepdims=True))
    return dx, dg


def _out(shape, dtype):
    return pltpu.HBM(tuple(shape), dtype)


def _in_hbm(*args):
    return [pltpu.with_memory_space_constraint(a, pltpu.HBM) if jnp.issubdtype(a.dtype, jnp.floating) and a.ndim > 1
            else a for a in args]


def _dot(a, b):
    return jnp.dot(a, b, preferred_element_type=F32)


def _dg(a, b, dims):
    return lax.dot_general(a, b, dims, preferred_element_type=F32)


def _ffn_fwd(name, h, g_pre, g_post, wg, wu, wd, tm, comm=None, bounds=()):
    tp = h.shape[0]
    nt = tp // tm

    def body(h_ref, gp_ref, gq_ref, wg_ref, wu_ref, wd_ref, hn_ref, gate_ref, up_ref, f_ref, xn_s, acc_s):
        c = pl.program_id(1)

        @pl.when(c == 0)
        def _():
            xn_s[...] = _rms(h_ref[...], gp_ref[...]).astype(BF16)
            acc_s[...] = jnp.zeros_like(acc_s)

        xn = xn_s[...]
        gate = _dg(xn, wg_ref[0], NT)
        up = _dg(xn, wu_ref[0], NT)
        gate_ref[0] = gate
        up_ref[0] = up
        act = (gate * jax.nn.sigmoid(gate) * up).astype(BF16)
        acc_s[...] += _dot(act, wd_ref[0])

        @pl.when(c == N_CHIP - 1)
        def _():
            f = acc_s[...]
            f_ref[...] = f
            hn_ref[...] = h_ref[...] + 0.5 * _rms(f, gq_ref[...])

    return _call(
        body, comm, bounds, (h, g_pre, g_post, wg, wu, wd), name=name, grid=(nt, N_CHIP),
        in_specs=[pl.BlockSpec((tm, D), lambda i, c: (i, 0)), _full((1, D)), _full((1, D))] +
                 [pl.BlockSpec((1, FC, D), lambda i, c: (c, 0, 0))] * 3,
        out_specs=[pl.BlockSpec((tm, D), lambda i, c: (i, 0)),
                   pl.BlockSpec((1, tm, FC), lambda i, c: (c, i, 0)),
                   pl.BlockSpec((1, tm, FC), lambda i, c: (c, i, 0)),
                   pl.BlockSpec((tm, D), lambda i, c: (i, 0))],
        out_shape=[_out((tp, D), F32), _out((N_CHIP, tp, FC), F32),
                   _out((N_CHIP, tp, FC), F32), _out((tp, D), F32)],
        scratch_shapes=[pltpu.VMEM((tm, D), BF16), pltpu.VMEM((tm, D), F32)],
        compiler_params=_cp(("arbitrary", "arbitrary"), 48))


def _ffn_bwd(name, h, g_pre, df, gate, up, wg, wu, wd, tm):
    tp = h.shape[0]
    nt = tp // tm
    rh = FC // 2

    def body(h_ref, gp_ref, df_ref, gate_ref, up_ref, wg_ref, wu_ref, wd_ref,
             dwg_ref, dwu_ref, dwd_ref, dxn_ref, rg_ref, ru_ref, rd_ref, ag, au, ad, send_sems, recv_sems):
        c = pl.program_id(0)
        i = pl.program_id(1)

        def to_sibling(a, piece):
            x, y, core = _mesh_pos()
            dw_ref, r_ref = ((dwg_ref, rg_ref), (dwu_ref, ru_ref), (dwd_ref, rd_ref))[a]
            return _remote(dw_ref.at[piece, pl.ds((1 - core) * rh, rh), :], r_ref.at[piece], send_sems, recv_sems,
                           3 * piece + a, (x, y, 1 - core))

        @pl.when(i == 0)
        def _():
            ag[...] = jnp.zeros_like(ag)
            au[...] = jnp.zeros_like(au)
            ad[...] = jnp.zeros_like(ad)

        xn = _rms(h_ref[...], gp_ref[...]).astype(BF16)
        dfb = df_ref[...].astype(BF16)
        gt = gate_ref[0]
        u = up_ref[0]
        sg = jax.nn.sigmoid(gt)
        si = gt * sg
        act = (si * u).astype(BF16)
        dact = _dg(dfb, wd_ref[0], NT)
        ad[...] += _dg(act, dfb, TN)
        dgate = (dact * u * (sg * (1.0 + gt * (1.0 - sg)))).astype(BF16)
        dup = (dact * si).astype(BF16)
        ag[...] += _dg(dgate, xn, TN)
        au[...] += _dg(dup, xn, TN)
        dxn_ref[0] = _dot(dgate, wg_ref[0]) + _dot(dup, wu_ref[0])

        @pl.when(i == nt - 1)
        def _():
            pltpu.sync_copy(ag, dwg_ref.at[c])
            pltpu.sync_copy(au, dwu_ref.at[c])
            pltpu.sync_copy(ad, dwd_ref.at[c])
            for a in range(3):
                to_sibling(a, c).start()

        @pl.when((c == N_CHIP - 1) & (i == nt - 1))
        def _():
            for piece in range(N_CHIP):
                for a in range(3):
                    to_sibling(a, piece).wait()

    return pl.pallas_call(
        body, name=name, grid=(N_CHIP, nt),
        in_specs=[pl.BlockSpec((tm, D), lambda c, i: (i, 0)), _full((1, D)),
                  pl.BlockSpec((tm, D), lambda c, i: (i, 0)),
                  pl.BlockSpec((1, tm, FC), lambda c, i: (c, i, 0)),
                  pl.BlockSpec((1, tm, FC), lambda c, i: (c, i, 0))] +
                 [pl.BlockSpec((1, FC, D), lambda c, i: (c, 0, 0))] * 3,
        out_specs=[ANY, ANY, ANY, pl.BlockSpec((1, tm, D), lambda c, i: (c, i, 0)), ANY, ANY, ANY],
        out_shape=[_out((N_CHIP, FC, D), F32)] * 3 + [_out((N_CHIP, tp, D), F32)] +
                  [_out((N_CHIP, rh, D), F32)] * 3,
        scratch_shapes=[pltpu.VMEM((FC, D), F32)] * 3 +
                       [pltpu.SemaphoreType.DMA((3 * N_CHIP,)), pltpu.SemaphoreType.DMA((3 * N_CHIP,))],
        compiler_params=_cp(("arbitrary", "arbitrary"), 58),
    )(*_in_hbm(h, g_pre, df, gate, up, wg, wu, wd))


def _ffn_pre_bwd(name, dh, dxn_part, h, g_pre, tm, comm=None, bounds=()):
    tp = h.shape[0]
    nt = tp // tm

    def body(dh_ref, dxn_ref, h_ref, gp_ref, out_ref, dg_ref):
        i = pl.program_id(0)
        dxn = (dxn_ref[0] + dxn_ref[1]) + (dxn_ref[2] + dxn_ref[3])
        dx, dg = _rms_bwd(h_ref[...], gp_ref[...], dxn)
        out_ref[...] = dh_ref[...] + dx

        @pl.when(i == 0)
        def _():
            dg_ref[...] = jnp.zeros_like(dg_ref)

        dg_ref[...] += dg

    return _call(
        body, comm, bounds, (dh, dxn_part, h, g_pre), name=name, grid=(nt,),
        in_specs=[_rows(tm, D), pl.BlockSpec((N_CHIP, tm, D), lambda i: (0, i, 0)), _rows(tm, D), _full((1, D))],
        out_specs=[_rows(tm, D), _full((1, D))],
        out_shape=[_out((tp, D), F32), _out((1, D), F32)],
        compiler_params=_cp(("arbitrary",), 48))


def _mix_in(h, g, w_in, tm):
    tp = h.shape[0]

    def body(h_ref, g_ref, w_ref, q_ref, k_ref, v_ref, u_ref):
        a = _rms(h_ref[...], g_ref[...]).astype(BF16)
        q_ref[...] = _dot(a, w_ref[0]).astype(BF16)
        k_ref[...] = _dot(a, w_ref[1]).astype(BF16)
        v_ref[...] = _dot(a, w_ref[2]).astype(BF16)
        u_ref[...] = _dot(a, w_ref[3])

    return pl.pallas_call(
        body, name="mix_in", grid=(tp // tm,),
        in_specs=[_rows(tm, D), _full((1, D)), _full((N_CHIP, D, NA_W))],
        out_specs=[_rows(tm, NA_W)] * 4,
        out_shape=[_out((tp, NA_W), BF16)] * 3 + [_out((tp, S5_W), F32)],
        compiler_params=_cp(("arbitrary",), 40),
    )(*_in_hbm(h, g, w_in))


def _gelu(x):
    return jax.nn.gelu(x, approximate=True)


def _gelu_grad(x):
    k = math.sqrt(2.0 / math.pi)
    t = jnp.tanh(k * (x + 0.044715 * x * x * x))
    return 0.5 * (1.0 + t) + 0.5 * x * (1.0 - t * t) * k * (1.0 + 3.0 * 0.044715 * x * x)


def _mix_out(o_na, y_pre, h, w_glu, b_glu, g_na, g_s5, w_out, g_post, tm, comm=None, bounds=()):
    tp = h.shape[0]

    def body(ona_ref, yp_ref, h_ref, wglu_ref, bglu_ref, gna_ref, gs5_ref, wout_ref, gpost_ref, hn_ref, mix_ref):
        y = _gelu(yp_ref[...])
        z = _dot(y.astype(BF16), wglu_ref[...]) + bglu_ref[...]
        o_s5 = y * jax.nn.sigmoid(z)
        n1 = _rms(ona_ref[...], gna_ref[...]).astype(BF16)
        n2 = _rms(o_s5, gs5_ref[...]).astype(BF16)
        mix = _dot(n1, wout_ref[0:NA_W, :]) + _dot(n2, wout_ref[NA_W:, :])
        mix_ref[...] = mix
        hn_ref[...] = h_ref[...] + _rms(mix, gpost_ref[...])

    return _call(
        body, comm, bounds, (o_na, y_pre, h, w_glu, b_glu, g_na, g_s5, w_out, g_post), name="mix_out",
        grid=(tp // tm,),
        in_specs=[_rows(tm, NA_W), _rows(tm, S5_W), _rows(tm, D), _full((S5_W, S5_W)), _full((1, S5_W)),
                  _full((1, NA_W)), _full((1, S5_W)), _full((D, D)), _full((1, D))],
        out_specs=[_rows(tm, D), _rows(tm, D)],
        out_shape=[_out((tp, D), F32)] * 2,
        compiler_params=_cp(("arbitrary",), 40))


def _mix_out_bwd(dh, mix, o_na, y_pre, w_glu, b_glu, g_na, g_s5, w_out, g_post, tm):
    tp = dh.shape[0]
    nt = tp // tm

    def body(dh_ref, mix_ref, ona_ref, yp_ref, wglu_ref, bglu_ref, gna_ref, gs5_ref, wout_ref, gpost_ref,
             dona_ref, dyp_ref, dwout_ref, dwglu_ref, dgpost_ref, dgna_ref, dgs5_ref, dbglu_ref, a_out, a_glu):
        i = pl.program_id(0)

        @pl.when(i == 0)
        def _():
            a_out[...] = jnp.zeros_like(a_out)
            a_glu[...] = jnp.zeros_like(a_glu)
            dgpost_ref[...] = jnp.zeros_like(dgpost_ref)
            dgna_ref[...] = jnp.zeros_like(dgna_ref)
            dgs5_ref[...] = jnp.zeros_like(dgs5_ref)
            dbglu_ref[...] = jnp.zeros_like(dbglu_ref)

        dmix, dgpost = _rms_bwd(mix_ref[...], gpost_ref[...], dh_ref[...])
        dgpost_ref[...] += dgpost
        yp = yp_ref[...]
        y = _gelu(yp)
        yb = y.astype(BF16)
        z = _dot(yb, wglu_ref[...]) + bglu_ref[...]
        sg = jax.nn.sigmoid(z)
        o_s5 = y * sg
        o_na = ona_ref[...]
        n1 = _rms(o_na, gna_ref[...]).astype(BF16)
        n2 = _rms(o_s5, gs5_ref[...]).astype(BF16)
        dmb = dmix.astype(BF16)
        a_out[0:NA_W, :] += _dg(n1, dmb, TN)
        a_out[NA_W:, :] += _dg(n2, dmb, TN)
        dn1 = _dg(dmb, wout_ref[0:NA_W, :], NT)
        dn2 = _dg(dmb, wout_ref[NA_W:, :], NT)
        dona, dgna = _rms_bwd(o_na, gna_ref[...], dn1)
        dona_ref[...] = dona
        dgna_ref[...] += dgna
        dos5, dgs5 = _rms_bwd(o_s5, gs5_ref[...], dn2)
        dgs5_ref[...] += dgs5
        dz = dos5 * y * (sg * (1.0 - sg))
        dbglu_ref[...] += jnp.sum(dz, axis=0, keepdims=True)
        dzb = dz.astype(BF16)
        a_glu[...] += _dg(yb, dzb, TN)
        dy = dos5 * sg + _dg(dzb, wglu_ref[...], NT)
        dyp_ref[...] = dy * _gelu_grad(yp)

        @pl.when(i == nt - 1)
        def _():
            pltpu.sync_copy(a_out, dwout_ref)
            pltpu.sync_copy(a_glu, dwglu_ref)

    return pl.pallas_call(
        body, name="mix_out_bwd", grid=(nt,),
        in_specs=[_rows(tm, D), _rows(tm, D), _rows(tm, NA_W), _rows(tm, S5_W), _full((S5_W, S5_W)),
                  _full((1, S5_W)), _full((1, NA_W)), _full((1, S5_W)), _full((D, D)), _full((1, D))],
        out_specs=[_rows(tm, NA_W), _rows(tm, S5_W), ANY, ANY, _full((1, D)), _full((1, NA_W)),
                   _full((1, S5_W)), _full((1, S5_W))],
        out_shape=[_out((tp, NA_W), F32), _out((tp, S5_W), F32),
                   _out((D, D), F32), _out((S5_W, S5_W), F32),
                   _out((1, D), F32), _out((1, NA_W), F32),
                   _out((1, S5_W), F32), _out((1, S5_W), F32)],
        scratch_shapes=[pltpu.VMEM((D, D), F32), pltpu.VMEM((S5_W, S5_W), F32)],
        compiler_params=_cp(("arbitrary",), 48),
    )(*_in_hbm(dh, mix, o_na, y_pre, w_glu, b_glu, g_na, g_s5, w_out, g_post))


def _mix_in_bwd(dq, dk, dv, du, h, g, w_in, dh, f1, g_post1, tm, comm=None, bounds=()):
    tp = h.shape[0]
    nt = tp // tm

    def body(dq_ref, dk_ref, dv_ref, du_ref, h_ref, g_ref, w_ref, dh_ref, f_ref, gq_ref,
             dh1_ref, df_ref, dw_ref, dg_ref, dgq_ref, acc):
        i = pl.program_id(0)

        @pl.when(i == 0)
        def _():
            acc[...] = jnp.zeros_like(acc)
            dg_ref[...] = jnp.zeros_like(dg_ref)
            dgq_ref[...] = jnp.zeros_like(dgq_ref)

        x = h_ref[...]
        a = _rms(x, g_ref[...]).astype(BF16)
        da = jnp.zeros((tm, D), F32)
        for j, r in enumerate((dq_ref, dk_ref, dv_ref, du_ref)):
            dp = r[...].astype(BF16)
            da = da + _dg(dp, w_ref[j], NT)
            acc[j] += _dg(a, dp, TN)
        dx, dg = _rms_bwd(x, g_ref[...], da)
        dh1 = dh_ref[...] + dx
        dh1_ref[...] = dh1
        dg_ref[...] += dg
        df, dgq = _rms_bwd(f_ref[...], gq_ref[...], 0.5 * dh1)
        df_ref[...] = df
        dgq_ref[...] += dgq

        @pl.when(i == nt - 1)
        def _():
            pltpu.sync_copy(acc, dw_ref)

    return _call(
        body, comm, bounds, (dq, dk, dv, du, h, g, w_in, dh, f1, g_post1), name="mix_in_bwd", grid=(nt,),
        in_specs=[_rows(tm, NA_W)] * 4 + [_rows(tm, D), _full((1, D)), _full((N_CHIP, D, NA_W)), _rows(tm, D),
                                         _rows(tm, D), _full((1, D))],
        out_specs=[_rows(tm, D), _rows(tm, D), ANY, _full((1, D)), _full((1, D))],
        out_shape=[_out((tp, D), F32), _out((tp, D), F32),
                   _out((N_CHIP, D, NA_W), F32), _out((1, D), F32),
                   _out((1, D), F32)],
        scratch_shapes=[pltpu.VMEM((N_CHIP, D, NA_W), F32)],
        compiler_params=_cp(("arbitrary",), 48))


def _final_loss(h, g_final, target, f2, g_post2, n_tok, tm):
    tp = h.shape[0]

    def body(h_ref, g_ref, t_ref, f_ref, gq_ref, dh_ref, df_ref, loss_ref, dg_ref, dgq_ref):
        i = pl.program_id(0)

        @pl.when(i == 0)
        def _():
            loss_ref[...] = jnp.zeros_like(loss_ref)
            dg_ref[...] = jnp.zeros_like(dg_ref)
            dgq_ref[...] = jnp.zeros_like(dgq_ref)

        x = h_ref[...]
        y = _rms(x, g_ref[...])
        row = i * tm + lax.broadcasted_iota(jnp.int32, (tm, 1), 0)
        valid = (row >= N_META) & (row < N_META + n_tok)
        e = jnp.where(valid, y - t_ref[...], 0.0)
        loss_ref[...] += 0.5 * jnp.sum(jnp.mean(e * e, axis=-1, keepdims=True), axis=0, keepdims=True)
        dx, dg = _rms_bwd(x, g_ref[...], e * (1.0 / D))
        dh_ref[...] = dx
        dg_ref[...] += dg
        df, dgq = _rms_bwd(f_ref[...], gq_ref[...], 0.5 * dx)
        df_ref[...] = df
        dgq_ref[...] += dgq

    return pl.pallas_call(
        body, name="final_loss", grid=(tp // tm,),
        in_specs=[_rows(tm, D), _full((1, D)), _rows(tm, D), _rows(tm, D), _full((1, D))],
        out_specs=[_rows(tm, D), _rows(tm, D), _full((1, 1)), _full((1, D)), _full((1, D))],
        out_shape=[_out((tp, D), F32), _out((tp, D), F32),
                   _out((1, 1), F32), _out((1, D), F32),
                   _out((1, D), F32)],
        compiler_params=_cp(("arbitrary",), 40),
    )(*_in_hbm(h, g_final, target, f2, g_post2))


def _na_patterns(n_rows):
    pats = []
    for kind in range(3):
        pat = [[-1] * K_ROWS for _ in range(Q_ROWS)]
        for i in range(Q_ROWS):
            for jj in range(K_ROWS):
                if kind == 0 and jj < KH:
                    pat[i][jj] = jj - i + KH - 1
                elif kind == 1 and i <= jj < i + KH:
                    pat[i][jj] = jj - i + 3
                elif kind == 2 and K_ROWS - KH <= jj:
                    pat[i][jj] = jj - i - 1
        pats.append(pat)
    return pats


def _diag_onehot():
    q = np.arange(GRID_W)[:, None]
    kc = np.arange(GRID_W)[None, :]
    start = np.clip(q - KW // 2, 0, GRID_W - KW)
    col_in = (kc >= start) & (kc < start + KW)
    e = np.zeros((32, GRID_W, GRID_W), np.float32)
    for d in range(2 * KW - 1):
        e[d] = ((kc - q + KW - 1) == d) & col_in
    return e.reshape(32, GRID_W * GRID_W), col_in


def _rpb_collapse(dtb2, et):
    def body(d_ref, e_ref, o_ref):
        o_ref[...] = jnp.dot(d_ref[...], e_ref[...], preferred_element_type=F32, precision=lax.Precision.HIGHEST)

    out = (dtb2.shape[0], et.shape[1])
    return pl.pallas_call(
        body, name="rpb_collapse", grid=(1,), out_shape=_out(out, F32),
        in_specs=[_full(dtb2.shape), _full(et.shape)], out_specs=_full(out),
    )(*_in_hbm(dtb2, et))


def _bias_tables(rpb, n_rows, comm=None, bounds=()):
    n_dr, n_dc = 2 * KH - 1, 2 * KW - 1
    pats = _na_patterns(n_rows)

    def body(rpb_ref, o_ref):
        h = pl.program_id(0)
        q = lax.broadcasted_iota(jnp.int32, (GRID_W, GRID_W), 0)
        kc = lax.broadcasted_iota(jnp.int32, (GRID_W, GRID_W), 1)
        start = jnp.clip(q - KW // 2, 0, GRID_W - KW)
        col_in = (kc >= start) & (kc < start + KW)
        diff = kc - q + (KW - 1)
        neg = jnp.full((GRID_W, GRID_W), NEG_INF, F32)
        band = []
        for dr in range(n_dr):
            acc = neg
            for d in range(n_dc):
                acc = jnp.where((diff == d) & col_in, rpb_ref[(h * n_dr + dr) * n_dc + d], acc)
            band.append(acc)
        for kind, pat in enumerate(pats):
            for i in range(Q_ROWS):
                for jj in range(K_ROWS):
                    o_ref[kind, 0, i * GRID_W:(i + 1) * GRID_W, jj * GRID_W:(jj + 1) * GRID_W] = (
                        band[pat[i][jj]] if pat[i][jj] >= 0 else neg)

    (bias,), got = _call(
        body, comm, bounds, (rpb.reshape(-1),), name="bias_tables", grid=(N_HEADS,),
        in_specs=[pl.BlockSpec(memory_space=pltpu.SMEM)],
        out_specs=[pl.BlockSpec((3, 1, QB, KB), lambda h: (0, h, 0, 0))],
        out_shape=[_out((3, N_HEADS, QB, KB), F32)],
        compiler_params=_cp(("arbitrary",), 32))
    return bias, got


def _attn_geometry(n_tok):
    n_rows = n_tok // GRID_W
    assert n_rows % Q_ROWS == 0 and n_rows >= K_ROWS
    return n_rows, n_rows // Q_ROWS


def _attn_probs(qh, kh, kmh, bias, scale):
    s = _dg(qh, kh, NT) * scale + bias
    sm = _dg(qh, kmh, NT) * scale
    m = jnp.maximum(jnp.max(s, axis=-1, keepdims=True), jnp.max(sm, axis=-1, keepdims=True))
    p = jnp.exp(s - m)
    pm = jnp.exp(sm - m)
    inv = 1.0 / (jnp.sum(p, axis=-1, keepdims=True) + jnp.sum(pm, axis=-1, keepdims=True))
    return p * inv, pm * inv


def _meta_probs(qmh, kmh, scale):
    s = _dg(qmh, kmh, NT) * scale
    p = jnp.exp(s - jnp.max(s, axis=-1, keepdims=True))
    return p / jnp.sum(p, axis=-1, keepdims=True)


def _step_rows(r, n_rows):
    q0 = pl.multiple_of(N_META + r * QB, 16)
    k0 = pl.multiple_of(N_META + jnp.clip(Q_ROWS * r - (K_ROWS - KH), 0, n_rows - K_ROWS) * GRID_W, 16)
    return q0, k0


def _attn_fwd(q, k, v, bias, n_tok, comm=None, bounds=()):
    tp = q.shape[0]
    n_rows, n_steps = _attn_geometry(n_tok)
    scale = HEAD_DIM ** -0.5

    def body(q_ref, k_ref, v_ref, b_ref, o_ref):
        r = pl.program_id(1)
        km = k_ref[0:N_META, :]
        vm = v_ref[0:N_META, :]

        @pl.when(r == 0)
        def _():
            qm = q_ref[0:N_META, :]
            outs = []
            for hh in range(2):
                sl = slice(hh * HEAD_DIM, (hh + 1) * HEAD_DIM)
                p = _meta_probs(qm[:, sl], km[:, sl], scale)
                outs.append(_dot(p.astype(BF16), vm[:, sl]))
            o_ref[0:N_META, :] = jnp.concatenate(outs, axis=1)
            o_ref[N_META + n_tok:, :] = jnp.zeros((tp - N_META - n_tok, 2 * HEAD_DIM), F32)

        q0, k0 = _step_rows(r, n_rows)
        qb = q_ref[pl.ds(q0, QB), :]
        kb = k_ref[pl.ds(k0, KB), :]
        vb = v_ref[pl.ds(k0, KB), :]
        outs = []
        for hh in range(2):
            sl = slice(hh * HEAD_DIM, (hh + 1) * HEAD_DIM)
            p, pm = _attn_probs(qb[:, sl], kb[:, sl], km[:, sl], b_ref[0, hh], scale)
            outs.append(_dot(p.astype(BF16), vb[:, sl]) + _dot(pm.astype(BF16), vm[:, sl]))
        o_ref[pl.ds(q0, QB), :] = jnp.concatenate(outs, axis=1)

    def bias_map(hp, r):
        return (jnp.where(r == 0, 0, jnp.where(r == n_steps - 1, 2, 1)), hp, 0, 0)

    col = pl.BlockSpec((tp, 2 * HEAD_DIM), lambda hp, r: (0, hp))
    return _call(
        body, comm, bounds, (q, k, v, bias), name="attn_fwd", grid=(N_HEADS // 2, n_steps),
        in_specs=[col, col, col, pl.BlockSpec((1, 2, QB, KB), bias_map)],
        out_specs=[col], out_shape=[_out((tp, NA_W), F32)],
        compiler_params=_cp(("arbitrary", "arbitrary"), 40))


def _attn_bwd(q, k, v, bias, do, n_tok, comm=None, bounds=()):
    tp = q.shape[0]
    n_rows, n_steps = _attn_geometry(n_tok)
    scale = HEAD_DIM ** -0.5
    pats = _na_patterns(n_rows)

    def body(q_ref, k_ref, v_ref, b_ref, do_ref, dq_ref, dk_ref, dv_ref, dtb_ref):
        r = pl.program_id(1)
        km = k_ref[0:N_META, :]
        vm = v_ref[0:N_META, :]

        @pl.when(r == 0)
        def _():
            dk_ref[...] = jnp.zeros_like(dk_ref)
            dv_ref[...] = jnp.zeros_like(dv_ref)
            dtb_ref[...] = jnp.zeros_like(dtb_ref)
            dq_ref[N_META + n_tok:, :] = jnp.zeros((tp - N_META - n_tok, 2 * HEAD_DIM), F32)
            qm = q_ref[0:N_META, :]
            dom = do_ref[0:N_META, :].astype(BF16)
            dqs, dks, dvs = [], [], []
            for hh in range(2):
                sl = slice(hh * HEAD_DIM, (hh + 1) * HEAD_DIM)
                p = _meta_probs(qm[:, sl], km[:, sl], scale)
                dp = _dg(dom[:, sl], vm[:, sl], NT)
                ds = (p * (dp - jnp.sum(dp * p, axis=-1, keepdims=True))).astype(BF16)
                dvs.append(_dg(p.astype(BF16), dom[:, sl], TN))
                dqs.append(_dot(ds, km[:, sl]) * scale)
                dks.append(_dg(ds, qm[:, sl], TN) * scale)
            dq_ref[0:N_META, :] = jnp.concatenate(dqs, axis=1)
            dk_ref[0:N_META, :] += jnp.concatenate(dks, axis=1)
            dv_ref[0:N_META, :] += jnp.concatenate(dvs, axis=1)

        q0, k0 = _step_rows(r, n_rows)
        qb = q_ref[pl.ds(q0, QB), :]
        kb = k_ref[pl.ds(k0, KB), :]
        vb = v_ref[pl.ds(k0, KB), :]
        dob = do_ref[pl.ds(q0, QB), :].astype(BF16)
        dqs, dks, dvs, dkms, dvms, dss = [], [], [], [], [], []
        for hh in range(2):
            sl = slice(hh * HEAD_DIM, (hh + 1) * HEAD_DIM)
            qh, kh, vh, kmh, vmh, doh = qb[:, sl], kb[:, sl], vb[:, sl], km[:, sl], vm[:, sl], dob[:, sl]
            p, pm = _attn_probs(qh, kh, kmh, b_ref[0, hh], scale)
            dp = _dg(doh, vh, NT)
            dpm = _dg(doh, vmh, NT)
            delta = jnp.sum(dp * p, axis=-1, keepdims=True) + jnp.sum(dpm * pm, axis=-1, keepdims=True)
            ds = p * (dp - delta)
            dsb = ds.astype(BF16)
            dsmb = (pm * (dpm - delta)).astype(BF16)
            dss.append(ds)
            dvs.append(_dg(p.astype(BF16), doh, TN))
            dvms.append(_dg(pm.astype(BF16), doh, TN))
            dqs.append((_dot(dsb, kh) + _dot(dsmb, kmh)) * scale)
            dks.append(_dg(dsb, qh, TN) * scale)
            dkms.append(_dg(dsmb, qh, TN) * scale)
        dq_ref[pl.ds(q0, QB), :] = jnp.concatenate(dqs, axis=1)
        dk_ref[pl.ds(k0, KB), :] += jnp.concatenate(dks, axis=1)
        dv_ref[pl.ds(k0, KB), :] += jnp.concatenate(dvs, axis=1)
        dk_ref[0:N_META, :] += jnp.concatenate(dkms, axis=1)
        dv_ref[0:N_META, :] += jnp.concatenate(dvms, axis=1)

        def add_bias_grad(pat):
            for hh in range(2):
                for i in range(Q_ROWS):
                    for jj in range(K_ROWS):
                        if pat[i][jj] >= 0:
                            dtb_ref[hh, pat[i][jj]] += dss[hh][i * GRID_W:(i + 1) * GRID_W,
                                                               jj * GRID_W:(jj + 1) * GRID_W]

        @pl.when(r == 0)
        def _():
            add_bias_grad(pats[0])

        @pl.when((r > 0) & (r < n_steps - 1))
        def _():
            add_bias_grad(pats[1])

        @pl.when(r == n_steps - 1)
        def _():
            add_bias_grad(pats[2])

    def bias_map(hp, r):
        return (jnp.where(r == 0, 0, jnp.where(r == n_steps - 1, 2, 1)), hp, 0, 0)

    col = pl.BlockSpec((tp, 2 * HEAD_DIM), lambda hp, r: (0, hp))
    n_dr = 2 * KH - 1
    return _call(
        body, comm, bounds, (q, k, v, bias, do), name="attn_bwd", grid=(N_HEADS // 2, n_steps),
        in_specs=[col, col, col, pl.BlockSpec((1, 2, QB, KB), bias_map), col],
        out_specs=[col, col, col, pl.BlockSpec((2, n_dr, GRID_W, GRID_W), lambda hp, r: (hp, 0, 0, 0))],
        out_shape=[_out((tp, NA_W), F32)] * 3 +
                  [_out((N_HEADS, n_dr, GRID_W, GRID_W), F32)],
        compiler_params=_cp(("arbitrary", "arbitrary"), 48))


def _repeat_onehot():
    return np.repeat(np.eye(2 * S5_G, dtype=np.float32), S5_H, axis=0)


def _s5_disc_math(lam_re, lam_im, log_dt, b_re, b_im, rep):
    dt = jnp.exp(log_dt)
    ea = jnp.exp(lam_re * dt)
    a_re = ea * jnp.cos(lam_im * dt)
    a_im = ea * jnp.sin(lam_im * dt)
    den = lam_re * lam_re + lam_im * lam_im
    c_re = ((a_re - 1.0) * lam_re + a_im * lam_im) / den
    c_im = (a_im * lam_re - (a_re - 1.0) * lam_im) / den
    ce_re = jnp.dot(rep, c_re, preferred_element_type=F32, precision=lax.Precision.HIGHEST)
    ce_im = jnp.dot(rep, c_im, preferred_element_type=F32, precision=lax.Precision.HIGHEST)
    return a_re, a_im, ce_re * b_re - ce_im * b_im, ce_re * b_im + ce_im * b_re


def _s5_blocks():
    gl = S5_G // N_BUNDLE
    half = gl * S5_P
    out = []
    for d in range(2):
        for g in range(S5_G):
            b, k = divmod(g, gl)
            dg = d * S5_G + g
            out.append((d, b, slice(k * S5_H, (k + 1) * S5_H), slice(k * S5_P, (k + 1) * S5_P),
                        slice(half + k * S5_P, half + (k + 1) * S5_P), slice(dg * S5_H, (dg + 1) * S5_H),
                        slice(dg, dg + 1)))
    return out


def _s5_params(lam_re, lam_im, log_dt, b_re, b_im, c_re, c_im):
    cw, sw = S5_W // N_BUNDLE, 2 * (S5_G // N_BUNDLE) * S5_P

    def body(lr, li, ld, br, bi, cr, ci, rep_ref, a_ref, a1_ref, a2_ref, bm_ref, cm_ref):
        a_re, a_im, bb_re, bb_im = _s5_disc_math(lr[...], li[...], ld[...], br[...], bi[...], rep_ref[...])
        cc_re = cr[...]
        cc_im = ci[...]
        bm_ref[...] = jnp.zeros_like(bm_ref)
        cm_ref[...] = jnp.zeros_like(cm_ref)
        for d, b, rows, re, im, nat, one in _s5_blocks():
            bm_ref[d, b, rows, re] = bb_re[nat, :].astype(BF16)
            bm_ref[d, b, rows, im] = bb_im[nat, :].astype(BF16)
            cm_ref[d, b, rows, re] = cc_re[nat, :].astype(BF16)
            cm_ref[d, b, rows, im] = (-cc_im[nat, :]).astype(BF16)
            a_ref[d, b, :, re] = a_re[one, :]
            a_ref[d, b, :, im] = a_im[one, :]
            k = rows.start // S5_H
            lanes = slice((k % 2) * S5_P, (k % 2 + 1) * S5_P)
            for part, (v1, v2) in enumerate(((a_re[one, :], a_im[one, :]), (a_re[one, :], -a_im[one, :]))):
                sub = slice(4 * part + k // 2, 4 * part + k // 2 + 1)
                a1_ref[d, b, sub, lanes] = v1
                a2_ref[d, b, sub, lanes] = v2

    args = (lam_re, lam_im, log_dt, b_re, b_im, c_re, c_im, jnp.asarray(_repeat_onehot()))
    outs = [((2, N_BUNDLE, 1, sw), F32)] + [((2, N_BUNDLE, 8, 128), F32)] * 2 + [((2, N_BUNDLE, cw, sw), BF16)] * 2
    return pl.pallas_call(
        body, name="s5_params", grid=(1,), in_specs=[_full(a.shape) for a in args],
        out_specs=[_full(s) for s, _ in outs], out_shape=[_out(s, dt) for s, dt in outs],
    )(*_in_hbm(*args))


def _s5_params_bwd(lam_re, lam_im, log_dt, b_re, b_im, da, dbm, dcm):
    n, nb = 2 * S5_G, 2 * S5_G * S5_H

    def body(lr, li, ld, br, bi, rep_ref, da_ref, dbm_ref, dcm_ref, o_lr, o_li, o_ld, o_br, o_bi, o_cr, o_ci,
             dar_s, dai_s, dbr_s, dbi_s):
        for d, b, rows, re, im, nat, one in _s5_blocks():
            dbr_s[nat, :] = dbm_ref[d, b, rows, re]
            dbi_s[nat, :] = dbm_ref[d, b, rows, im]
            o_cr[nat, :] = dcm_ref[d, b, rows, re]
            o_ci[nat, :] = -dcm_ref[d, b, rows, im]
            dar_s[one, :] = da_ref[d, b, :, re]
            dai_s[one, :] = da_ref[d, b, :, im]
        rep = rep_ref[...]
        _, vjp = jax.vjp(lambda p, q, r, s, t: _s5_disc_math(p, q, r, s, t, rep),
                         lr[...], li[...], ld[...], br[...], bi[...])
        o_lr[...], o_li[...], o_ld[...], o_br[...], o_bi[...] = vjp((dar_s[...], dai_s[...], dbr_s[...], dbi_s[...]))

    args = (lam_re, lam_im, log_dt, b_re, b_im, jnp.asarray(_repeat_onehot()), da, dbm, dcm)
    outs = [(n, S5_P)] * 2 + [(n, 1)] + [(nb, S5_P)] * 4
    return pl.pallas_call(
        body, name="s5_params_bwd", grid=(1,), in_specs=[_full(a.shape) for a in args],
        out_specs=[_full(s) for s in outs], out_shape=[_out(s, F32) for s in outs],
        scratch_shapes=[pltpu.VMEM((n, S5_P), F32)] * 2 + [pltpu.VMEM((nb, S5_P), F32)] * 2,
    )(*_in_hbm(*args))


def _tiles_store(ref, base, val):
    for i in range(val.shape[0] // 8):
        for c in range(8):
            ref[pl.ds(base + (8 * i + c) * 8, 8), :] = val[8 * i:8 * i + 8, 128 * c:128 * (c + 1)]


def _tiles_load(ref, base, n):
    return jnp.concatenate(
        [jnp.concatenate([ref[pl.ds(base + (8 * i + c) * 8, 8), :] for c in range(8)], axis=1) for i in range(n // 8)],
        axis=0)


def _time_rows(base, t):
    return pl.ds(base + (t // 8) * 64 + t % 8, 8, stride=8)


def _scan(chains, n):
    xs = [c["x"] for c in chains]
    for k in range(n):
        for ci, c in enumerate(chains):
            t = n - 1 - k if c["reverse"] else k
            if c["prev"] is not None:
                c["prev"][_time_rows(c["prev_base"], t), :] = xs[ci]
            xs[ci] = c["a1"] * xs[ci] + pltpu.roll(c["a2"] * xs[ci], 4, axis=0) + c["src"][_time_rows(0, t), :]
            if c["dst"] is not None:
                c["dst"][_time_rows(0, t), :] = xs[ci]
    return xs


def _chain(x, a1, a2, src, dst=None, prev=None, prev_base=0, reverse=False):
    return dict(x=x, a1=a1, a2=a2, src=src, dst=dst, prev=prev, prev_base=prev_base, reverse=reverse)


def _s5_fwd(u, d_skip, a1, a2, bm, cm, length, comm=None, bounds=()):
    tp = u.shape[0]
    cw = S5_W // N_BUNDLE
    sw = bm.shape[-1]
    n_full, n_tail = divmod(length, SCAN_CHUNK)
    t_tail = n_full * SCAN_CHUNK

    nbs = N_BUNDLE

    def body(u_ref, d_ref, a1_ref, a2_ref, bm_ref, cm_ref, y_ref, bnd_ref, *scratch):
        y_ref[...] = u_ref[...] * d_ref[...]
        ins, xss = (scratch[0:nbs], scratch[nbs:2 * nbs]), (scratch[2 * nbs:3 * nbs], scratch[3 * nbs:])
        cols = [slice(b * cw, (b + 1) * cw) for b in range(nbs)]

        def keep(dr, chunk, xs):
            for b in range(nbs):
                bnd_ref[dr, b, chunk] = xs[b]

        def load(dr, t0, n):
            for b in range(nbs):
                _tiles_store(ins[dr][b], 0, _dot(u_ref[pl.ds(t0, n), cols[b]].astype(BF16), bm_ref[dr, b]))

        def chains(dr, xs):
            return [_chain(xs[b], a1_ref[dr, b], a2_ref[dr, b], ins[dr][b], dst=xss[dr][b], reverse=dr == 1)
                    for b in range(nbs)]

        def emit(dr, t0, n):
            for b in range(nbs):
                y_ref[pl.ds(t0, n), cols[b]] += _dg(_tiles_load(xss[dr][b], 0, n).astype(BF16), cm_ref[dr, b], NT)

        zero = (jnp.zeros((8, 128), F32),) * nbs
        xb = zero
        if n_tail:
            keep(1, n_full, xb)
            load(1, t_tail, n_tail)
            xb = tuple(_scan(chains(1, xb), n_tail))
            emit(1, t_tail, n_tail)

        def pair(i, carry):
            j = n_full - 1 - i
            t0s = (pl.multiple_of(i * SCAN_CHUNK, SCAN_CHUNK), pl.multiple_of(j * SCAN_CHUNK, SCAN_CHUNK))
            keep(0, i, carry[0])
            keep(1, j, carry[1])
            for dr in range(2):
                load(dr, t0s[dr], SCAN_CHUNK)
            out = _scan(chains(0, carry[0]) + chains(1, carry[1]), SCAN_CHUNK)
            for dr in range(2):
                emit(dr, t0s[dr], SCAN_CHUNK)
            return tuple(out[:nbs]), tuple(out[nbs:])

        xf, _ = lax.fori_loop(0, n_full, pair, (zero, xb))
        if n_tail:
            keep(0, n_full, xf)
            load(0, t_tail, n_tail)
            _scan(chains(0, xf), n_tail)
            emit(0, t_tail, n_tail)

    n_chunks = n_full + (1 if n_tail else 0)
    tile = pl.BlockSpec((2, nbs, 8, 128), lambda b: (0, b, 0, 0))
    return _call(
        body, comm, bounds, (u, d_skip, a1, a2, bm, cm), name="s5_fwd", grid=(N_BUNDLE // nbs,),
        in_specs=[pl.BlockSpec((tp, nbs * cw), lambda b: (0, b)), pl.BlockSpec((1, nbs * cw), lambda b: (0, b)),
                  tile, tile, pl.BlockSpec((2, nbs, cw, sw), lambda b: (0, b, 0, 0)),
                  pl.BlockSpec((2, nbs, cw, sw), lambda b: (0, b, 0, 0))],
        out_specs=[pl.BlockSpec((tp, nbs * cw), lambda b: (0, b)),
                   pl.BlockSpec((2, nbs, n_chunks, 8, 128), lambda b: (0, b, 0, 0, 0))],
        out_shape=[_out((tp, S5_W), F32), _out((2, N_BUNDLE, n_chunks, 8, 128), F32)],
        scratch_shapes=[pltpu.VMEM((SCAN_CHUNK * 8, 128), F32)] * (4 * nbs),
        compiler_params=_cp(("arbitrary",), 48))


def _s5_bwd(u, dy, d_skip, a, a1, a2, bm, cm, bnd, length):
    tp = u.shape[0]
    cw = S5_W // N_BUNDLE
    sw = bm.shape[-1]
    half = sw // 2
    n_full, n_tail = divmod(length, SCAN_CHUNK)
    t_tail = n_full * SCAN_CHUNK
    n_chunks = bnd.shape[2]
    nbs = 2

    def body(u_ref, dy_ref, d_ref, a_ref, a1_ref, a2_ref, bm_ref, cm_ref, bnd_ref, du_ref, dd_ref, dbm_ref, dcm_ref,
             da_ref, *scratch):
        du_ref[...] = dy_ref[...] * d_ref[...]
        dd_ref[...] = jnp.sum(dy_ref[...] * u_ref[...], axis=0, keepdims=True)
        dbm_ref[...] = jnp.zeros_like(dbm_ref)
        dcm_ref[...] = jnp.zeros_like(dcm_ref)
        da_ref[...] = jnp.zeros_like(da_ref)
        bu_s, dx_s, g_s, xp_s = ([scratch[(k * 2 + dr) * nbs:(k * 2 + dr + 1) * nbs] for dr in range(2)] for k in range(4))
        cols = [slice(b * cw, (b + 1) * cw) for b in range(nbs)]

        def chains(dr, chunk, t0, n, gs):
            out = []
            for b in range(nbs):
                _tiles_store(bu_s[dr][b], 0, _dot(u_ref[pl.ds(t0, n), cols[b]].astype(BF16), bm_ref[dr, b]))
                _tiles_store(dx_s[dr][b], 0, _dot(dy_ref[pl.ds(t0, n), cols[b]].astype(BF16), cm_ref[dr, b]))
                out.append(_chain(bnd_ref[dr, b, chunk], a1_ref[dr, b], a2_ref[dr, b], bu_s[dr][b],
                                  prev=xp_s[dr][b], reverse=dr == 1))
                out.append(_chain(gs[b], a1_ref[dr, b], -a2_ref[dr, b], dx_s[dr][b], dst=g_s[dr][b], reverse=dr == 0))
            return out

        def emit(dr, t0, n):
            rows = pl.ds(t0, n)
            for b in range(nbs):
                ub = u_ref[rows, cols[b]].astype(BF16)
                dyb = dy_ref[rows, cols[b]].astype(BF16)
                g = _tiles_load(g_s[dr][b], 0, n)
                gb = g.astype(BF16)
                du_ref[rows, cols[b]] += _dg(gb, bm_ref[dr, b], NT)
                dbm_ref[dr, b] += _dg(ub, gb, TN)
                xp = _tiles_load(xp_s[dr][b], 0, n)
                xp_r, xp_i = xp[:, 0:half], xp[:, half:]
                g_r, g_i = g[:, 0:half], g[:, half:]
                a_re = a_ref[dr, b, :, 0:half]
                a_im = a_ref[dr, b, :, half:]
                bu = _dot(ub, bm_ref[dr, b])
                x_r = a_re * xp_r - a_im * xp_i + bu[:, 0:half]
                x_i = a_re * xp_i + a_im * xp_r + bu[:, half:]
                dcm_ref[dr, b] += _dg(dyb, jnp.concatenate([x_r, x_i], axis=1).astype(BF16), TN)
                da_ref[dr, b] += jnp.concatenate([jnp.sum(g_r * xp_r + g_i * xp_i, axis=0, keepdims=True),
                                                  jnp.sum(g_i * xp_r - g_r * xp_i, axis=0, keepdims=True)], axis=1)

        def adjoints(out):
            return tuple(out[1::2])

        zero = (jnp.zeros((8, 128), F32),) * nbs
        g0 = zero
        if n_tail:
            g0 = adjoints(_scan(chains(0, n_full, t_tail, n_tail, g0), n_tail))
            emit(0, t_tail, n_tail)

        def pair(i, carry):
            j = n_full - 1 - i
            t0 = (pl.multiple_of(j * SCAN_CHUNK, SCAN_CHUNK), pl.multiple_of(i * SCAN_CHUNK, SCAN_CHUNK))
            both = chains(0, j, t0[0], SCAN_CHUNK, carry[0]) + chains(1, i, t0[1], SCAN_CHUNK, carry[1])
            out = _scan(both, SCAN_CHUNK)
            emit(0, t0[0], SCAN_CHUNK)
            emit(1, t0[1], SCAN_CHUNK)
            return adjoints(out[:2 * nbs]), adjoints(out[2 * nbs:])

        _, g1 = lax.fori_loop(0, n_full, pair, (g0, zero))
        if n_tail:
            _scan(chains(1, n_full, t_tail, n_tail, g1), n_tail)
            emit(1, t_tail, n_tail)

    tile = pl.BlockSpec((2, nbs, 8, 128), lambda b: (0, b, 0, 0))
    wide = pl.BlockSpec((2, nbs, cw, sw), lambda b: (0, b, 0, 0))
    col = pl.BlockSpec((tp, nbs * cw), lambda b: (0, b))
    row = pl.BlockSpec((1, nbs * cw), lambda b: (0, b))
    arow = pl.BlockSpec((2, nbs, 1, sw), lambda b: (0, b, 0, 0))
    return pl.pallas_call(
        body, name="s5_bwd", grid=(N_BUNDLE // nbs,),
        in_specs=[col, col, row, arow, tile, tile, wide, wide,
                  pl.BlockSpec((2, nbs, n_chunks, 8, 128), lambda b: (0, b, 0, 0, 0))],
        out_specs=[col, row, wide, wide, arow],
        out_shape=[_out((tp, S5_W), F32), _out((1, S5_W), F32),
                   _out((2, N_BUNDLE, cw, sw), F32), _out((2, N_BUNDLE, cw, sw), F32),
                   _out((2, N_BUNDLE, 1, sw), F32)],
        scratch_shapes=[pltpu.VMEM((SCAN_CHUNK * 8, 128), F32)] * (8 * nbs),
        compiler_params=_cp(("arbitrary",), 56),
    )(*_in_hbm(u, dy, d_skip, a, a1, a2, bm, cm, bnd))


def _row_tile(tp):
    return max(tm for tm in range(16, 449, 16) if tp % tm == 0)


def _step(x, target, bufs, gains, s5, rpb, c_arr, kc_arr):
    n_tok = x.shape[0]
    first = ["ffn1_w_gate", "ffn1_w_up", "ffn1_w_down", "meta_tokens"]
    bias, got = _bias_tables(rpb, n_tok // GRID_W, _gather_comm([bufs[n] for n in first]), (0, N_HEADS - 1))
    w = dict(zip(first, got))
    meta = w["meta_tokens"].transpose(1, 0, 2).reshape(N_META, D)
    length = N_META + n_tok
    tp = length + 16
    tm = _row_tile(tp)
    tmb = tm
    n_rows = n_tok // GRID_W
    pad = jnp.zeros((tp - length, D), F32)
    h0 = jnp.concatenate([meta, x, pad], axis=0)
    tgt = jnp.concatenate([jnp.zeros((N_META, D), F32), target, pad], axis=0)

    s5p = (s5["lam_re"], s5["lam_im"], s5["log_dt"].reshape(2 * S5_G, 1), s5["b_re"], s5["b_im"])
    a_m, a1_m, a2_m, bm16, cm16 = _s5_params(*s5p, s5["c_re"], s5["c_im"])

    mid = ["w_in", "s5_w_glu", "w_out"]
    (h1, gate1, up1, f1), got = _ffn_fwd(
        "ffn1_fwd", h0, gains["ffn1_pre_g"], gains["ffn1_post_g"], w["ffn1_w_gate"], w["ffn1_w_up"], w["ffn1_w_down"],
        tm, _gather_comm([bufs[n] for n in mid]), (0, (tp // tm) * N_CHIP * 3 // 5))
    w.update(zip(mid, got))
    q, k, v, u = _mix_in(h1, gains["mix_pre_g"], w["w_in"], tm)
    (o_na,), (gate_ici, up_ici) = _attn_fwd(
        q, k, v, bias, n_tok, _gather_comm([bufs["ffn2_w_gate"], bufs["ffn2_w_up"]], pair=False), (0,))
    (y_pre, s5_bnd), (w["ffn2_w_gate"], w["ffn2_w_up"], down_ici) = _s5_fwd(
        u, gains["s5_d"], a1_m, a2_m, bm16, cm16, length,
        _merge_comm(_gather_comm([gate_ici, up_ici], ici=False),
                    _gather_comm([bufs["ffn2_w_down"]], pair=False)), (0,))
    w_glu = w["s5_w_glu"].reshape(S5_W, S5_W)
    w_out = w["w_out"].reshape(D, D)
    (h2, mix), (w["ffn2_w_down"],) = _mix_out(
        o_na, y_pre, h1, w_glu, gains["s5_b_glu"], gains["na_out_g"], gains["s5_out_g"], w_out, gains["mix_post_g"], tm,
        _gather_comm([down_ici], ici=False), (0,))
    (h3, gate2, up2, f2), _ = _ffn_fwd("ffn2_fwd", h2, gains["ffn2_pre_g"], gains["ffn2_post_g"],
                                       w["ffn2_w_gate"], w["ffn2_w_up"], w["ffn2_w_down"], tm)
    dh3, df2, loss, dg_final, dg_post2 = _final_loss(h3, gains["final_g"], tgt, f2, gains["ffn2_post_g"], n_tok, tm)

    ffn2 = ["ffn2_w_gate", "ffn2_w_up", "ffn2_w_down"]
    ffn1 = ["ffn1_w_gate", "ffn1_w_up", "ffn1_w_down"]
    out2 = _ffn_bwd("ffn2_bwd", h2, gains["ffn2_pre_g"], df2, gate2, up2,
                    w["ffn2_w_gate"], w["ffn2_w_up"], w["ffn2_w_down"], tmb)
    dxn2 = out2[3]
    sums2 = [_chip_sum("chip_sum_" + n, g, r, c_arr) for n, g, r in zip(ffn2, out2[0:3], out2[4:7])]
    (dh2, dg_pre2), _ = _ffn_pre_bwd("ffn2_pre_bwd", dh3, dxn2, h2, gains["ffn2_pre_g"], tm)
    do_na, dy_pre, dw_out, dw_glu, dg_mpost, dg_na, dg_s5, db_glu = _mix_out_bwd(
        dh2, mix, o_na, y_pre, w_glu, gains["s5_b_glu"], gains["na_out_g"], gains["s5_out_g"], w_out,
        gains["mix_post_g"], tm)
    (dq, dk, dv, dtb), recv3 = _attn_bwd(q, k, v, bias, do_na, n_tok, _scatter_comm(sums2), (0,))
    totals2 = [_total_sum("total_sum_" + n, s, r, kc_arr) for n, s, r in zip(ffn2, sums2, recv3)]
    du, dd, dbm, dcm, da_m = _s5_bwd(u, dy_pre, gains["s5_d"], a_m, a1_m, a2_m, bm16, cm16, s5_bnd, length)
    (dh1, df1, dw_in, dg_mpre, dg_post1), done2 = _mix_in_bwd(
        dq, dk, dv, du, h1, gains["mix_pre_g"], w["w_in"], dh2, f1, gains["ffn1_post_g"], tm,
        _assemble_comm(totals2), (0,))
    pieces = dict(zip(ffn2, done2))
    out1 = _ffn_bwd("ffn1_bwd", h0, gains["ffn1_pre_g"], df1, gate1, up1,
                    w["ffn1_w_gate"], w["ffn1_w_up"], w["ffn1_w_down"], tmb)
    rest = [dw_in, dw_glu.reshape(N_CHIP, S5_W // N_CHIP, S5_W), dw_out.reshape(N_CHIP, D // N_CHIP, D)]
    (dh0, dg_pre1), recv_rest = _ffn_pre_bwd("ffn1_pre_bwd", dh1, out1[3], h0, gains["ffn1_pre_g"], tm,
                                             _exchange_comm(rest), (0,))
    last = ffn1 + mid
    sums = [_chip_sum("chip_sum_" + n, g, r, c_arr)
            for n, g, r in zip(last, list(out1[0:3]) + rest, list(out1[4:7]) + list(recv_rest))]

    e, _ = _diag_onehot()
    n_dr = 2 * KH - 1
    drpb = _rpb_collapse(dtb.reshape(N_HEADS * n_dr, GRID_W * GRID_W), jnp.asarray(e.T))
    drpb = drpb[:, :2 * KW - 1].reshape(N_HEADS, n_dr, 2 * KW - 1).transpose(1, 0, 2).reshape(N_HEADS * n_dr, 2 * KW - 1)
    dlam_re, dlam_im, dlog_dt, db_re, db_im, dc_re, dc_im = _s5_params_bwd(*s5p, da_m, dbm, dcm)

    small = {"ffn1_pre_g": dg_pre1, "ffn1_post_g": dg_post1, "mix_pre_g": dg_mpre, "na_rpb": drpb,
             "s5_lam_re": dlam_re, "s5_lam_im": dlam_im, "s5_log_dt": dlog_dt.reshape(2, S5_G),
             "s5_b_re": db_re, "s5_b_im": db_im, "s5_c_re": dc_re, "s5_c_im": dc_im,
             "s5_d": dd, "s5_b_glu": db_glu, "na_out_g": dg_na,
             "s5_out_g": dg_s5, "mix_post_g": dg_mpost, "ffn2_pre_g": dg_pre2, "ffn2_post_g": dg_post2,
             "final_g": dg_final}
    return loss[0, 0], dh0, pieces, small, last, sums


def _mesh_pos():
    return lax.axis_index("x"), lax.axis_index("y"), lax.axis_index("c")


def _other_chips(x, y):
    return [(1 - x, y), (x, 1 - y), (1 - x, 1 - y)]


class _Comm:
    def __init__(self, ins, out_shape, aliases, parts):
        self.ins, self.out_shape, self.aliases, self.parts = list(ins), list(out_shape), dict(aliases), list(parts)
        self.n_sems = sum(p[0] for p in parts)

    def bases(self):
        out, base = [], 0
        for n_sems, _, _ in self.parts:
            out.append(base)
            base += n_sems
        return out


def _run_comm(name, comm):
    n_i, n_o = len(comm.ins), len(comm.out_shape)

    def body(*refs):
        ins, outs = refs[:n_i], refs[n_i:n_i + n_o]
        send_sems, recv_sems = refs[n_i + n_o:]
        for base, (_, start, finish) in zip(comm.bases(), comm.parts):
            start(ins, outs, send_sems, recv_sems, base)
            finish(ins, outs, send_sems, recv_sems, base)

    return pl.pallas_call(
        body, name=name, out_shape=comm.out_shape, in_specs=[ANY] * n_i, out_specs=[ANY] * n_o,
        input_output_aliases=comm.aliases,
        scratch_shapes=[pltpu.SemaphoreType.DMA((comm.n_sems,)), pltpu.SemaphoreType.DMA((comm.n_sems,))],
    )(*_in_hbm(*comm.ins))


def _call(body, comm, bounds, args, *, name, grid, in_specs, out_specs, out_shape, scratch_shapes=(),
          compiler_params=None):
    in_specs, out_specs, out_shape, scratch_shapes = list(in_specs), list(out_specs), list(out_shape), list(scratch_shapes)
    if comm is None:
        return pl.pallas_call(body, name=name, grid=grid, in_specs=in_specs, out_specs=out_specs, out_shape=out_shape,
                              scratch_shapes=scratch_shapes, compiler_params=compiler_params)(*_in_hbm(*args)), []
    n_in, n_out, n_scr = len(in_specs), len(out_specs), len(scratch_shapes)
    n_ci, n_co = len(comm.ins), len(comm.out_shape)
    n_steps = int(np.prod(grid))
    assert len(bounds) == len(comm.parts) and all(0 <= b < n_steps for b in bounds) and list(bounds) == sorted(bounds)

    def fused(*refs):
        a = n_in
        b = a + n_ci
        c = b + n_out
        d = c + n_co
        e = d + n_scr
        cargs = (refs[a:b], refs[c:d], refs[e], refs[e + 1])
        step = pl.program_id(0)
        for ax in range(1, len(grid)):
            step = step * grid[ax] + pl.program_id(ax)
        bases = comm.bases()
        for p, (_, start, finish) in enumerate(comm.parts):
            @pl.when(step == bounds[p])
            def _(p=p, start=start):
                if p > 0:
                    comm.parts[p - 1][2](*cargs, bases[p - 1])
                start(*cargs, bases[p])
        body(*(refs[:a] + refs[b:c] + refs[d:e]))

        @pl.when(step == n_steps - 1)
        def _():
            comm.parts[-1][2](*cargs, bases[-1])

    res = pl.pallas_call(
        fused, name=name, grid=grid, in_specs=in_specs + [ANY] * n_ci, out_specs=out_specs + [ANY] * n_co,
        out_shape=out_shape + comm.out_shape,
        scratch_shapes=scratch_shapes + [pltpu.SemaphoreType.DMA((comm.n_sems,)), pltpu.SemaphoreType.DMA((comm.n_sems,))],
        input_output_aliases={n_in + i: n_out + j for i, j in comm.aliases.items()},
        compiler_params=compiler_params)(*_in_hbm(*args, *comm.ins))
    return res[:n_out], res[n_out:]


def _remote(src, dst, send_sems, recv_sems, idx, to):
    return pltpu.make_async_remote_copy(src_ref=src, dst_ref=dst, send_sem=send_sems.at[idx],
                                        recv_sem=recv_sems.at[idx], device_id=to, device_id_type=MESH_ID)


def _gather_comm(bufs, ici=True, pair=True):
    n = len(bufs)

    def half(ref, k, pc):
        rh = ref.shape[1] // 2
        return ref.at[k, pl.ds(pc * rh, rh), :]

    def ici_start(ins, outs, ss, rs, base):
        x, y, c = _mesh_pos()
        for a in range(n):
            mine = half(outs[a], 2 * x + y, c)
            for j, chip in enumerate(_other_chips(x, y)):
                _remote(mine, mine, ss, rs, base + 3 * a + j, (*chip, c)).start()

    def ici_finish(ins, outs, ss, rs, base):
        x, y, c = _mesh_pos()
        for a in range(n):
            for j, chip in enumerate(_other_chips(x, y)):
                theirs = half(outs[a], 2 * chip[0] + chip[1], c)
                _remote(theirs, theirs, ss, rs, base + 3 * a + j, (*chip, c)).wait()

    def pair_copy(outs, ss, rs, base, a):
        x, y, c = _mesh_pos()
        rh = outs[a].shape[1] // 2
        held = outs[a].at[:, pl.ds(c * rh, rh), :]
        return _remote(held, held, ss, rs, base + a, (x, y, 1 - c))

    def pair_start(ins, outs, ss, rs, base):
        for a in range(n):
            pair_copy(outs, ss, rs, base, a).start()

    def pair_finish(ins, outs, ss, rs, base):
        for a in range(n):
            pair_copy(outs, ss, rs, base, a).wait()

    parts = ([(3 * n, ici_start, ici_finish)] if ici else []) + ([(n, pair_start, pair_finish)] if pair else [])
    return _Comm(bufs, [_out(b.shape, b.dtype) for b in bufs], {a: a for a in range(n)}, parts)


def _merge_comm(*comms):
    ins, shapes, aliases, subs, base = [], [], {}, [], 0
    for cm in comms:
        (n_sems, start, finish), = cm.parts
        i0, o0 = len(ins), len(shapes)
        subs.append((slice(i0, i0 + len(cm.ins)), slice(o0, o0 + len(cm.out_shape)), base, start, finish))
        aliases.update({i0 + i: o0 + j for i, j in cm.aliases.items()})
        ins += cm.ins
        shapes += cm.out_shape
        base += n_sems

    def start_all(ins_r, outs_r, ss, rs, b):
        for si, so, off, start, _ in subs:
            start(ins_r[si], outs_r[so], ss, rs, b + off)

    def finish_all(ins_r, outs_r, ss, rs, b):
        for si, so, off, _, finish in subs:
            finish(ins_r[si], outs_r[so], ss, rs, b + off)

    return _Comm(ins, shapes, aliases, [(base, start_all, finish_all)])


def _own_half_buffers(pieces, dtypes, kc_arr):
    n = len(pieces)

    def body(kc_ref, *refs):
        for a in range(n):
            refs[n + a][0] = refs[a][...].astype(dtypes[a])

    def half(p):
        return p.shape[0] // 2, p.shape[1]

    return pl.pallas_call(
        body, name="own_halves",
        out_shape=[_out((N_CHIP,) + p.shape, dt) for p, dt in zip(pieces, dtypes)],
        grid_spec=pltpu.PrefetchScalarGridSpec(
            num_scalar_prefetch=1, grid=(1,),
            in_specs=[pl.BlockSpec(half(p), lambda i, kc: (kc[1], 0)) for p in pieces],
            out_specs=[pl.BlockSpec((1,) + half(p), lambda i, kc: (kc[0], kc[1], 0)) for p in pieces]),
        compiler_params=_cp(("arbitrary",), 48),
    )(kc_arr, *_in_hbm(*pieces))


def _exchange_comm(grads):
    n = len(grads)

    def copy(ins, outs, ss, rs, base, a):
        x, y, c = _mesh_pos()
        rh = ins[a].shape[1] // 2
        return _remote(ins[a].at[:, pl.ds((1 - c) * rh, rh), :], outs[a], ss, rs, base + a, (x, y, 1 - c))

    def start(ins, outs, ss, rs, base):
        for a in range(n):
            copy(ins, outs, ss, rs, base, a).start()

    def finish(ins, outs, ss, rs, base):
        for a in range(n):
            copy(ins, outs, ss, rs, base, a).wait()

    shapes = [_out((N_CHIP, g.shape[1] // 2, g.shape[2]), g.dtype) for g in grads]
    return _Comm(grads, shapes, {}, [(n, start, finish)])


def _chip_sum(name, g, recv, c_arr):
    _, r, cc = g.shape
    rh = r // 2

    def body(c_ref, g_ref, r_ref, o_ref):
        o_ref[...] = (g_ref[...] + r_ref[...]).astype(BF16)

    return pl.pallas_call(
        body, name=name, out_shape=_out((N_CHIP, rh, cc), BF16),
        grid_spec=pltpu.PrefetchScalarGridSpec(
            num_scalar_prefetch=1, grid=(N_CHIP,),
            in_specs=[pl.BlockSpec((1, rh, cc), lambda j, c_ref: (j, c_ref[0], 0)),
                      pl.BlockSpec((1, rh, cc), lambda j, c_ref: (j, 0, 0))],
            out_specs=pl.BlockSpec((1, rh, cc), lambda j, c_ref: (j, 0, 0))),
        compiler_params=_cp(("arbitrary",), 32),
    )(c_arr, *_in_hbm(g, recv))


def _scatter_comm(sums):
    n = len(sums)

    def copies(ins, outs, ss, rs, base):
        x, y, c = _mesh_pos()
        return [_remote(ins[a].at[2 * chip[0] + chip[1]], outs[a].at[j], ss, rs, base + 3 * a + j, (*chip, c))
                for a in range(n) for j, chip in enumerate(_other_chips(x, y))]

    def start(ins, outs, ss, rs, base):
        for cp in copies(ins, outs, ss, rs, base):
            cp.start()

    def finish(ins, outs, ss, rs, base):
        for cp in copies(ins, outs, ss, rs, base):
            cp.wait()

    shapes = [_out((3,) + s.shape[1:], s.dtype) for s in sums]
    return _Comm(sums, shapes, {}, [(3 * n, start, finish)])


def _total_sum(name, sums, recv3, kc_arr):
    _, rh, cc = sums.shape

    def body(kc_ref, s_ref, r_ref, o_ref):
        t = s_ref[0].astype(F32) + r_ref[0].astype(F32)
        t = t + r_ref[1].astype(F32)
        o_ref[...] = t + r_ref[2].astype(F32)

    return pl.pallas_call(
        body, name=name, out_shape=_out((2 * rh, cc), F32),
        grid_spec=pltpu.PrefetchScalarGridSpec(
            num_scalar_prefetch=1, grid=(1,),
            in_specs=[pl.BlockSpec((1, rh, cc), lambda i, kc_ref: (kc_ref[0], 0, 0)),
                      pl.BlockSpec((3, rh, cc), lambda i, kc_ref: (0, 0, 0))],
            out_specs=pl.BlockSpec((rh, cc), lambda i, kc_ref: (kc_ref[1], 0))),
        compiler_params=_cp(("arbitrary",), 32),
    )(kc_arr, *_in_hbm(sums, recv3))


def _assemble_comm(totals):
    n = len(totals)

    def copy(outs, ss, rs, base, a):
        x, y, c = _mesh_pos()
        rh = outs[a].shape[0] // 2
        here = outs[a].at[pl.ds(c * rh, rh), :]
        return _remote(here, here, ss, rs, base + a, (x, y, 1 - c))

    def start(ins, outs, ss, rs, base):
        for a in range(n):
            copy(outs, ss, rs, base, a).start()

    def finish(ins, outs, ss, rs, base):
        for a in range(n):
            copy(outs, ss, rs, base, a).wait()

    shapes = [_out(t.shape, t.dtype) for t in totals]
    return _Comm(totals, shapes, {a: a for a in range(n)}, [(n, start, finish)])


def _small_allreduce(arrays, comm):
    n = len(arrays)
    shapes = [a.shape for a in arrays]
    narrow_w = 64
    groups = [[a for a in range(n) if shapes[a][1] > narrow_w], [a for a in range(n) if shapes[a][1] <= narrow_w]]
    widths = [max(shapes[a][1] for a in groups[0]), 2 * narrow_w]
    offs, cols, heights = {}, {}, [0, 0]
    for a in groups[0]:
        offs[a], cols[a] = heights[0], 0
        heights[0] += shapes[a][0]
    rows = [-(-heights[0] // 8) * 8]
    heights = [0, 0]
    for a in sorted(groups[1], key=lambda a: -shapes[a][0]):
        side = 0 if heights[0] <= heights[1] else 1
        offs[a], cols[a] = heights[side], side * narrow_w
        heights[side] += shapes[a][0]
    rows.append(-(-max(heights) // 8) * 8)
    n_g = len(groups)

    def window(ref, a):
        return ref.at[offs[a]:offs[a] + shapes[a][0], cols[a]:cols[a] + shapes[a][1]]

    def body(*refs):
        ins, outs = refs[:n], refs[n:2 * n]
        pack, sib, csum, every = (refs[2 * n + i * n_g:2 * n + (i + 1) * n_g] for i in range(4))
        send_sems, recv_sems = refs[2 * n + 4 * n_g:]
        x, y, c = _mesh_pos()
        k = 2 * x + y
        for gi, g in enumerate(groups):
            pack[gi][...] = jnp.zeros_like(pack[gi])
            for a in g:
                window(pack[gi], a)[...] = ins[a][...]
        cps = [_remote(pack[gi], sib[gi], send_sems, recv_sems, gi, (x, y, 1 - c)) for gi in range(n_g)]
        for cp in cps:
            cp.start()
        for cp in cps:
            cp.wait()
        for gi in range(n_g):
            csum[gi][...] = pack[gi][...] + sib[gi][...]
            every[gi][k] = csum[gi][...]
        cps = [_remote(csum[gi], every[gi].at[k], send_sems, recv_sems, n_g + 3 * gi + j, (*chip, c))
               for gi in range(n_g) for j, chip in enumerate(_other_chips(x, y))]
        for cp in cps:
            cp.start()
        for cp in cps:
            cp.wait()
        for gi, g in enumerate(groups):
            pack[gi][...] = ((every[gi][0] + every[gi][1]) + every[gi][2]) + every[gi][3]
            for a in g:
                outs[a][...] = window(pack[gi], a)[...]

    bufs = [pltpu.VMEM((r, w), F32) for r, w in zip(rows, widths)]
    return _call(
        body, comm, (0,), arrays, name="small_allreduce", grid=(1,), out_shape=[_out(s, F32) for s in shapes],
        in_specs=[_full(s) for s in shapes], out_specs=[_full(s) for s in shapes],
        scratch_shapes=bufs * 3 + [pltpu.VMEM((N_CHIP, r, w), F32) for r, w in zip(rows, widths)] +
                       [pltpu.SemaphoreType.DMA((4 * n_g,)), pltpu.SemaphoreType.DMA((4 * n_g,))],
        compiler_params=_cp(("arbitrary",), 40))


def _adamw_small(ws, gs, ms, vs, comm):
    n = len(ws)

    def body(*refs):
        w, g, m, v, d, mo, vo = (refs[i * n:(i + 1) * n] for i in range(7))
        for a in range(n):
            d[a][...], mo[a][...], vo[a][...] = _adamw_math(w[a][...], g[a][...], m[a][...], v[a][...])

    specs = [_full(w.shape) for w in ws]
    res, got = _call(
        body, comm, (0,), (*ws, *gs, *ms, *vs), name="adamw_small", grid=(1,),
        out_shape=[_out(w.shape, F32) for w in ws] * 3,
        in_specs=specs * 4, out_specs=specs * 3, compiler_params=_cp(("arbitrary",), 40))
    return (res[:n], res[n:2 * n], res[2 * n:]), got


def _adamw_math(w, g, m, v):
    m = ADAM_B1 * m + (1.0 - ADAM_B1) * g
    v = ADAM_B2 * v + (1.0 - ADAM_B2) * (g * g)
    m_hat = m / (1.0 - ADAM_B1 ** ADAM_STEP)
    v_hat = v / (1.0 - ADAM_B2 ** ADAM_STEP)
    delta = -ADAM_LR * (m_hat / (jnp.sqrt(v_hat) + ADAM_EPS) + ADAM_WD * w)
    return delta, m, v


def _adamw(name, w, g, m, v):
    r, c = w.shape
    tr = max(t for t in range(8, 513, 8) if r % t == 0)

    def body(w_ref, g_ref, m_ref, v_ref, d_ref, mo_ref, vo_ref):
        d_ref[...], mo_ref[...], vo_ref[...] = _adamw_math(w_ref[...], g_ref[...], m_ref[...], v_ref[...])

    return pl.pallas_call(
        body, name=name, grid=(r // tr,), in_specs=[_rows(tr, c)] * 4, out_specs=[_rows(tr, c)] * 3,
        out_shape=[_out((r, c), F32)] * 3, compiler_params=_cp(("arbitrary",), 32),
    )(*_in_hbm(w, g, m, v))


def _as_matrix(name, a):
    if name == "na_rpb":
        return a[0].transpose(1, 0, 2).reshape(N_HEADS * (2 * KH - 1), 2 * KW - 1)
    if name in ("s5_b_re", "s5_b_im"):
        return a.transpose(0, 1, 2, 4, 3).reshape(2 * S5_G * S5_H, S5_P)
    if name in ("s5_c_re", "s5_c_im"):
        return a.reshape(2 * S5_G * S5_H, S5_P)
    if name in ("s5_lam_re", "s5_lam_im"):
        return a.reshape(2 * S5_G, S5_P)
    if name == "s5_log_dt":
        return a.reshape(2, S5_G)
    return a


def _from_matrix(name, m):
    if name == "na_rpb":
        return m.reshape(2 * KH - 1, N_HEADS, 2 * KW - 1).transpose(1, 0, 2)[None]
    if name in ("s5_b_re", "s5_b_im"):
        return m.reshape(1, 2, S5_G, S5_H, S5_P).transpose(0, 1, 2, 4, 3)
    if name in ("s5_c_re", "s5_c_im"):
        return m.reshape(1, 2, S5_G, S5_H, S5_P)
    if name in ("s5_lam_re", "s5_lam_im"):
        return m.reshape(1, 2, S5_G, S5_P)
    if name == "s5_log_dt":
        return m.reshape(1, 2, S5_G)
    return m


WEIGHTS = ["meta_tokens", "ffn1_pre_g", "ffn1_post_g", "ffn1_w_gate", "ffn1_w_up", "ffn1_w_down", "mix_pre_g", "w_in",
           "na_rpb", "s5_lam_re", "s5_lam_im", "s5_log_dt", "s5_b_re", "s5_b_im", "s5_c_re", "s5_c_im", "s5_d",
           "s5_w_glu", "s5_b_glu", "na_out_g", "s5_out_g", "w_out", "mix_post_g", "ffn2_pre_g", "ffn2_post_g",
           "ffn2_w_gate", "ffn2_w_up", "ffn2_w_down", "final_g"]
BIG = ["ffn1_w_gate", "ffn1_w_up", "ffn1_w_down", "w_in", "s5_w_glu", "w_out", "ffn2_w_gate", "ffn2_w_up",
       "ffn2_w_down"]
TRANSPOSED = ["ffn1_w_gate", "ffn1_w_up", "ffn2_w_gate", "ffn2_w_up"]
GAINS = ["ffn1_pre_g", "ffn1_post_g", "mix_pre_g", "s5_d", "s5_b_glu", "na_out_g", "s5_out_g", "mix_post_g",
         "ffn2_pre_g", "ffn2_post_g", "final_g"]
SMALL = [n for n in WEIGHTS if n not in BIG]


def kernel(*args):
    names = ["x"] + WEIGHTS + ["loss_target"] + ["m_" + n for n in WEIGHTS] + ["v_" + n for n in WEIGHTS]
    assert len(args) == len(names)
    given = dict(zip(names, args))
    x_pos, y_pos, c_pos = _mesh_pos()
    k_pos = 2 * x_pos + y_pos
    c_arr = jnp.reshape(c_pos, (1,)).astype(jnp.int32)
    kc_arr = jnp.stack([k_pos, c_pos]).astype(jnp.int32)

    def piece(name, a):
        return a[0].T if name in TRANSPOSED else a[0]

    def unpiece(name, a):
        return a.T[None] if name in TRANSPOSED else a[None]

    placed = BIG + ["meta_tokens"]
    bufs = dict(zip(placed, _own_half_buffers([piece(n, given[n]) for n in BIG] + [given["meta_tokens"]],
                                              [BF16] * len(BIG) + [F32], kc_arr)))

    gains = {n: given[n] for n in GAINS}
    s5 = {n: _as_matrix("s5_" + n, given["s5_" + n])
          for n in ["lam_re", "lam_im", "log_dt", "b_re", "b_im", "c_re", "c_im"]}
    loss, dh0, pieces, small, last, sums = _step(given["x"][0], given["loss_target"][0], bufs, gains, s5,
                                                 given["na_rpb"][0], c_arr, kc_arr)
    loss = lax.psum(loss, ("x", "y", "c"))
    n_tok = given["x"].shape[1]
    grad_x = dh0[N_META:N_META + n_tok][None]

    small["meta_tokens"] = dh0[:N_META]
    red, recv3 = _small_allreduce([small[n] for n in SMALL], _scatter_comm(sums))
    small = dict(zip(SMALL, red))
    mc = D // N_CHIP
    small["meta_tokens"] = lax.dynamic_slice_in_dim(small["meta_tokens"], k_pos * mc, mc, 1)
    totals = [_total_sum("total_sum_" + n, s, r, kc_arr) for n, s, r in zip(last, sums, recv3)]
    gs = [small[n] for n in SMALL]
    (d2, m2, v2), done = _adamw_small([_as_matrix(n, given[n]) for n in SMALL], gs,
                                      [_as_matrix(n, given["m_" + n]) for n in SMALL],
                                      [_as_matrix(n, given["v_" + n]) for n in SMALL], _assemble_comm(totals))
    pieces.update(zip(last, done))

    out_g, out_d, out_m, out_v = {}, {}, {}, {}
    for n, g, dd, mm, vv in zip(SMALL, gs, d2, m2, v2):
        out_g[n], out_d[n], out_m[n], out_v[n] = (_from_matrix(n, t) for t in (g, dd, mm, vv))
    for n in BIG:
        g2 = pieces[n]
        d2, m2, v2 = _adamw("adamw_" + n, piece(n, given[n]), g2, piece(n, given["m_" + n]),
                            piece(n, given["v_" + n]))
        out_g[n], out_d[n], out_m[n], out_v[n] = (unpiece(n, t) for t in (g2, d2, m2, v2))
    return (loss, grad_x, *[out_g[n] for n in WEIGHTS], *[out_d[n] for n in WEIGHTS],
            *[out_m[n] for n in WEIGHTS], *[out_v[n] for n in WEIGHTS])
```

```python
import functools
import math

import numpy as np
import jax
import jax.numpy as jnp
from jax import lax
from jax.experimental import pallas as pl
from jax.experimental.pallas import tpu as pltpu

F32 = jnp.float32
BF16 = jnp.bfloat16

D = 1024
N_META = 16
GRID_W = 64
NA_W = 512
S5_W = 512
HEAD_DIM = 64
N_HEADS = 8
KH = 8
KW = 16
S5_G = 32
S5_P = 64
S5_H = 16
N_BUNDLE = 4
FF = 2816
N_CHIP = 4
FC = FF // N_CHIP
EPS = 1e-6
NEG_INF = -1e30
Q_ROWS = 4
K_ROWS = 12
QB = Q_ROWS * GRID_W
KB = K_ROWS * GRID_W
SCAN_CHUNK = 256

ADAM_LR = 0.001
ADAM_B1 = 0.9
ADAM_B2 = 0.999
ADAM_EPS = 1e-08
ADAM_WD = 0.01
ADAM_STEP = 10

NT = (((1,), (1,)), ((), ()))
TN = (((0,), (0,)), ((), ()))
MESH_ID = pl.DeviceIdType.MESH


def _cp(sem=None, vmem_mb=None):
    kw = {}
    if sem is not None:
        kw["dimension_semantics"] = sem
    if vmem_mb is not None:
        kw["vmem_limit_bytes"] = vmem_mb << 20
    return pltpu.CompilerParams(**kw)


def _full(shape):
    n = len(shape)
    return pl.BlockSpec(shape, lambda *_: (0,) * n)


def _rows(tm, w):
    return pl.BlockSpec((tm, w), lambda i: (i, 0))


ANY = pl.BlockSpec(memory_space=pl.ANY)


def _rms(x, g):
    r = lax.rsqrt(jnp.mean(x * x, axis=-1, keepdims=True) + EPS)
    return x * r * g


def _rms_bwd(x, g, dy):
    r = lax.rsqrt(jnp.mean(x * x, axis=-1, keepdims=True) + EPS)
    xh = x * r
    dg = jnp.sum(dy * xh, axis=0, keepdims=True)
    dyg = dy * g
    dx = r * (dyg - xh * jnp.mean(dyg * xh, axis=-1, keepdims=True))
    return dx, dg


def _out(shape, dtype):
    return pltpu.HBM(tuple(shape), dtype)


def _in_hbm(*args):
    return [pltpu.with_memory_space_constraint(a, pltpu.HBM) if jnp.issubdtype(a.dtype, jnp.floating) and a.ndim > 1
            else a for a in args]


def _dot(a, b):
    return jnp.dot(a, b, preferred_element_type=F32)


def _dg(a, b, dims):
    return lax.dot_general(a, b, dims, preferred_element_type=F32)


def _ffn_fwd(name, h, g_pre, g_post, wg, wu, wd, tm, comm=None, bounds=()):
    tp = h.shape[0]
    nt = tp // tm

    def body(h_ref, gp_ref, gq_ref, wg_ref, wu_ref, wd_ref, hn_ref, gate_ref, up_ref, f_ref, xn_s, acc_s):
        c = pl.program_id(1)

        @pl.when(c == 0)
        def _():
            xn_s[...] = _rms(h_ref[...], gp_ref[...]).astype(BF16)
            acc_s[...] = jnp.zeros_like(acc_s)

        xn = xn_s[...]
        gate = _dg(xn, wg_ref[0], NT)
        up = _dg(xn, wu_ref[0], NT)
        gate_ref[0] = gate
        up_ref[0] = up
        act = (gate * jax.nn.sigmoid(gate) * up).astype(BF16)
        acc_s[...] += _dot(act, wd_ref[0])

        @pl.when(c == N_CHIP - 1)
        def _():
            f = acc_s[...]
            f_ref[...] = f
            hn_ref[...] = h_ref[...] + 0.5 * _rms(f, gq_ref[...])

    return _call(
        body, comm, bounds, (h, g_pre, g_post, wg, wu, wd), name=name, grid=(nt, N_CHIP),
        in_specs=[pl.BlockSpec((tm, D), lambda i, c: (i, 0)), _full((1, D)), _full((1, D))] +
                 [pl.BlockSpec((1, FC, D), lambda i, c: (c, 0, 0))] * 3,
        out_specs=[pl.BlockSpec((tm, D), lambda i, c: (i, 0)),
                   pl.BlockSpec((1, tm, FC), lambda i, c: (c, i, 0)),
                   pl.BlockSpec((1, tm, FC), lambda i, c: (c, i, 0)),
                   pl.BlockSpec((tm, D), lambda i, c: (i, 0))],
        out_shape=[_out((tp, D), F32), _out((N_CHIP, tp, FC), F32),
                   _out((N_CHIP, tp, FC), F32), _out((tp, D), F32)],
        scratch_shapes=[pltpu.VMEM((tm, D), BF16), pltpu.VMEM((tm, D), F32)],
        compiler_params=_cp(("arbitrary", "arbitrary"), 48))


def _ffn_bwd(name, h, g_pre, df, gate, up, wg, wu, wd, tm, comm=None, bounds=()):
    tp = h.shape[0]
    nt = tp // tm
    rh = FC // 2

    def body(h_ref, gp_ref, df_ref, gate_ref, up_ref, wg_ref, wu_ref, wd_ref,
             dwg_ref, dwu_ref, dwd_ref, dxn_ref, rg_ref, ru_ref, rd_ref, ag, au, ad, send_sems, recv_sems):
        c = pl.program_id(0)
        i = pl.program_id(1)

        def to_sibling(a, piece):
            x, y, core = _mesh_pos()
            dw_ref, r_ref = ((dwg_ref, rg_ref), (dwu_ref, ru_ref), (dwd_ref, rd_ref))[a]
            return _remote(dw_ref.at[piece, pl.ds((1 - core) * rh, rh), :], r_ref.at[piece], send_sems, recv_sems,
                           3 * piece + a, (x, y, 1 - core))

        @pl.when(i == 0)
        def _():
            ag[...] = jnp.zeros_like(ag)
            au[...] = jnp.zeros_like(au)
            ad[...] = jnp.zeros_like(ad)

        xn = _rms(h_ref[...], gp_ref[...]).astype(BF16)
        dfb = df_ref[...].astype(BF16)
        gt = gate_ref[0]
        u = up_ref[0]
        sg = jax.nn.sigmoid(gt)
        si = gt * sg
        act = (si * u).astype(BF16)
        dact = _dg(dfb, wd_ref[0], NT)
        ad[...] += _dg(act, dfb, TN)
        dgate = (dact * u * (sg * (1.0 + gt * (1.0 - sg)))).astype(BF16)
        dup = (dact * si).astype(BF16)
        ag[...] += _dg(dgate, xn, TN)
        au[...] += _dg(dup, xn, TN)
        dxn_ref[0] = _dot(dgate, wg_ref[0]) + _dot(dup, wu_ref[0])

        @pl.when(i == nt - 1)
        def _():
            pltpu.sync_copy(ag, dwg_ref.at[c])
            pltpu.sync_copy(au, dwu_ref.at[c])
            pltpu.sync_copy(ad, dwd_ref.at[c])
            for a in range(3):
                to_sibling(a, c).start()

        @pl.when((c == N_CHIP - 1) & (i == nt - 1))
        def _():
            for piece in range(N_CHIP):
                for a in range(3):
                    to_sibling(a, piece).wait()

    return _call(
        body, comm, bounds, (h, g_pre, df, gate, up, wg, wu, wd), name=name, grid=(N_CHIP, nt),
        in_specs=[pl.BlockSpec((tm, D), lambda c, i: (i, 0)), _full((1, D)),
                  pl.BlockSpec((tm, D), lambda c, i: (i, 0)),
                  pl.BlockSpec((1, tm, FC), lambda c, i: (c, i, 0)),
                  pl.BlockSpec((1, tm, FC), lambda c, i: (c, i, 0))] +
                 [pl.BlockSpec((1, FC, D), lambda c, i: (c, 0, 0))] * 3,
        out_specs=[ANY, ANY, ANY, pl.BlockSpec((1, tm, D), lambda c, i: (c, i, 0)), ANY, ANY, ANY],
        out_shape=[_out((N_CHIP, FC, D), F32)] * 3 + [_out((N_CHIP, tp, D), F32)] +
                  [_out((N_CHIP, rh, D), F32)] * 3,
        scratch_shapes=[pltpu.VMEM((FC, D), F32)] * 3 +
                       [pltpu.SemaphoreType.DMA((3 * N_CHIP,)), pltpu.SemaphoreType.DMA((3 * N_CHIP,))],
        compiler_params=_cp(("arbitrary", "arbitrary"), 58))


def _ffn_pre_bwd(name, dh, dxn_part, h, g_pre, tm, comm=None, bounds=()):
    tp = h.shape[0]
    nt = tp // tm

    def body(dh_ref, dxn_ref, h_ref, gp_ref, out_ref, dg_ref):
        i = pl.program_id(0)
        dxn = (dxn_ref[0] + dxn_ref[1]) + (dxn_ref[2] + dxn_ref[3])
        dx, dg = _rms_bwd(h_ref[...], gp_ref[...], dxn)
        out_ref[...] = dh_ref[...] + dx

        @pl.when(i == 0)
        def _():
            dg_ref[...] = jnp.zeros_like(dg_ref)

        dg_ref[...] += dg

    return _call(
        body, comm, bounds, (dh, dxn_part, h, g_pre), name=name, grid=(nt,),
        in_specs=[_rows(tm, D), pl.BlockSpec((N_CHIP, tm, D), lambda i: (0, i, 0)), _rows(tm, D), _full((1, D))],
        out_specs=[_rows(tm, D), _full((1, D))],
        out_shape=[_out((tp, D), F32), _out((1, D), F32)],
        compiler_params=_cp(("arbitrary",), 48))


def _mix_in(h, g, w_in, tm):
    tp = h.shape[0]

    def body(h_ref, g_ref, w_ref, q_ref, k_ref, v_ref, u_ref):
        a = _rms(h_ref[...], g_ref[...]).astype(BF16)
        q_ref[...] = _dot(a, w_ref[0]).astype(BF16)
        k_ref[...] = _dot(a, w_ref[1]).astype(BF16)
        v_ref[...] = _dot(a, w_ref[2]).astype(BF16)
        u_ref[...] = _dot(a, w_ref[3])

    return pl.pallas_call(
        body, name="mix_in", grid=(tp // tm,),
        in_specs=[_rows(tm, D), _full((1, D)), _full((N_CHIP, D, NA_W))],
        out_specs=[_rows(tm, NA_W)] * 4,
        out_shape=[_out((tp, NA_W), BF16)] * 3 + [_out((tp, S5_W), F32)],
        compiler_params=_cp(("arbitrary",), 40),
    )(*_in_hbm(h, g, w_in))


def _gelu(x):
    return jax.nn.gelu(x, approximate=True)


def _gelu_grad(x):
    k = math.sqrt(2.0 / math.pi)
    t = jnp.tanh(k * (x + 0.044715 * x * x * x))
    return 0.5 * (1.0 + t) + 0.5 * x * (1.0 - t * t) * k * (1.0 + 3.0 * 0.044715 * x * x)


def _mix_out(o_na, y_pre, h, w_glu, b_glu, g_na, g_s5, w_out, g_post, tm, comm=None, bounds=()):
    tp = h.shape[0]

    def body(ona_ref, yp_ref, h_ref, wglu_ref, bglu_ref, gna_ref, gs5_ref, wout_ref, gpost_ref, hn_ref, mix_ref):
        y = _gelu(yp_ref[...])
        z = _dot(y.astype(BF16), wglu_ref[...]) + bglu_ref[...]
        o_s5 = y * jax.nn.sigmoid(z)
        n1 = _rms(ona_ref[...], gna_ref[...]).astype(BF16)
        n2 = _rms(o_s5, gs5_ref[...]).astype(BF16)
        mix = _dot(n1, wout_ref[0:NA_W, :]) + _dot(n2, wout_ref[NA_W:, :])
        mix_ref[...] = mix
        hn_ref[...] = h_ref[...] + _rms(mix, gpost_ref[...])

    return _call(
        body, comm, bounds, (o_na, y_pre, h, w_glu, b_glu, g_na, g_s5, w_out, g_post), name="mix_out",
        grid=(tp // tm,),
        in_specs=[_rows(tm, NA_W), _rows(tm, S5_W), _rows(tm, D), _full((S5_W, S5_W)), _full((1, S5_W)),
                  _full((1, NA_W)), _full((1, S5_W)), _full((D, D)), _full((1, D))],
        out_specs=[_rows(tm, D), _rows(tm, D)],
        out_shape=[_out((tp, D), F32)] * 2,
        compiler_params=_cp(("arbitrary",), 40))


def _mix_out_bwd(dh, mix, o_na, y_pre, w_glu, b_glu, g_na, g_s5, w_out, g_post, tm):
    tp = dh.shape[0]
    nt = tp // tm

    def body(dh_ref, mix_ref, ona_ref, yp_ref, wglu_ref, bglu_ref, gna_ref, gs5_ref, wout_ref, gpost_ref,
             dona_ref, dyp_ref, dwout_ref, dwglu_ref, dgpost_ref, dgna_ref, dgs5_ref, dbglu_ref, a_out, a_glu):
        i = pl.program_id(0)

        @pl.when(i == 0)
        def _():
            a_out[...] = jnp.zeros_like(a_out)
            a_glu[...] = jnp.zeros_like(a_glu)
            dgpost_ref[...] = jnp.zeros_like(dgpost_ref)
            dgna_ref[...] = jnp.zeros_like(dgna_ref)
            dgs5_ref[...] = jnp.zeros_like(dgs5_ref)
            dbglu_ref[...] = jnp.zeros_like(dbglu_ref)

        dmix, dgpost = _rms_bwd(mix_ref[...], gpost_ref[...], dh_ref[...])
        dgpost_ref[...] += dgpost
        yp = yp_ref[...]
        y = _gelu(yp)
        yb = y.astype(BF16)
        z = _dot(yb, wglu_ref[...]) + bglu_ref[...]
        sg = jax.nn.sigmoid(z)
        o_s5 = y * sg
        o_na = ona_ref[...]
        n1 = _rms(o_na, gna_ref[...]).astype(BF16)
        n2 = _rms(o_s5, gs5_ref[...]).astype(BF16)
        dmb = dmix.astype(BF16)
        a_out[0:NA_W, :] += _dg(n1, dmb, TN)
        a_out[NA_W:, :] += _dg(n2, dmb, TN)
        dn1 = _dg(dmb, wout_ref[0:NA_W, :], NT)
        dn2 = _dg(dmb, wout_ref[NA_W:, :], NT)
        dona, dgna = _rms_bwd(o_na, gna_ref[...], dn1)
        dona_ref[...] = dona
        dgna_ref[...] += dgna
        dos5, dgs5 = _rms_bwd(o_s5, gs5_ref[...], dn2)
        dgs5_ref[...] += dgs5
        dz = dos5 * y * (sg * (1.0 - sg))
        dbglu_ref[...] += jnp.sum(dz, axis=0, keepdims=True)
        dzb = dz.astype(BF16)
        a_glu[...] += _dg(yb, dzb, TN)
        dy = dos5 * sg + _dg(dzb, wglu_ref[...], NT)
        dyp_ref[...] = dy * _gelu_grad(yp)

        @pl.when(i == nt - 1)
        def _():
            pltpu.sync_copy(a_out, dwout_ref)
            pltpu.sync_copy(a_glu, dwglu_ref)

    return pl.pallas_call(
        body, name="mix_out_bwd", grid=(nt,),
        in_specs=[_rows(tm, D), _rows(tm, D), _rows(tm, NA_W), _rows(tm, S5_W), _full((S5_W, S5_W)),
                  _full((1, S5_W)), _full((1, NA_W)), _full((1, S5_W)), _full((D, D)), _full((1, D))],
        out_specs=[_rows(tm, NA_W), _rows(tm, S5_W), ANY, ANY, _full((1, D)), _full((1, NA_W)),
                   _full((1, S5_W)), _full((1, S5_W))],
        out_shape=[_out((tp, NA_W), F32), _out((tp, S5_W), F32),
                   _out((D, D), F32), _out((S5_W, S5_W), F32),
                   _out((1, D), F32), _out((1, NA_W), F32),
                   _out((1, S5_W), F32), _out((1, S5_W), F32)],
        scratch_shapes=[pltpu.VMEM((D, D), F32), pltpu.VMEM((S5_W, S5_W), F32)],
        compiler_params=_cp(("arbitrary",), 48),
    )(*_in_hbm(dh, mix, o_na, y_pre, w_glu, b_glu, g_na, g_s5, w_out, g_post))


def _mix_in_bwd(dq, dk, dv, du, h, g, w_in, dh, f1, g_post1, tm, comm=None, bounds=()):
    tp = h.shape[0]
    nt = tp // tm

    def body(dq_ref, dk_ref, dv_ref, du_ref, h_ref, g_ref, w_ref, dh_ref, f_ref, gq_ref,
             dh1_ref, df_ref, dw_ref, dg_ref, dgq_ref, acc):
        i = pl.program_id(0)

        @pl.when(i == 0)
        def _():
            acc[...] = jnp.zeros_like(acc)
            dg_ref[...] = jnp.zeros_like(dg_ref)
            dgq_ref[...] = jnp.zeros_like(dgq_ref)

        x = h_ref[...]
        a = _rms(x, g_ref[...]).astype(BF16)
        da = jnp.zeros((tm, D), F32)
        for j, r in enumerate((dq_ref, dk_ref, dv_ref, du_ref)):
            dp = r[...].astype(BF16)
            da = da + _dg(dp, w_ref[j], NT)
            acc[j] += _dg(a, dp, TN)
        dx, dg = _rms_bwd(x, g_ref[...], da)
        dh1 = dh_ref[...] + dx
        dh1_ref[...] = dh1
        dg_ref[...] += dg
        df, dgq = _rms_bwd(f_ref[...], gq_ref[...], 0.5 * dh1)
        df_ref[...] = df
        dgq_ref[...] += dgq

        @pl.when(i == nt - 1)
        def _():
            pltpu.sync_copy(acc, dw_ref)

    return _call(
        body, comm, bounds, (dq, dk, dv, du, h, g, w_in, dh, f1, g_post1), name="mix_in_bwd", grid=(nt,),
        in_specs=[_rows(tm, NA_W)] * 4 + [_rows(tm, D), _full((1, D)), _full((N_CHIP, D, NA_W)), _rows(tm, D),
                                         _rows(tm, D), _full((1, D))],
        out_specs=[_rows(tm, D), _rows(tm, D), ANY, _full((1, D)), _full((1, D))],
        out_shape=[_out((tp, D), F32), _out((tp, D), F32),
                   _out((N_CHIP, D, NA_W), F32), _out((1, D), F32),
                   _out((1, D), F32)],
        scratch_shapes=[pltpu.VMEM((N_CHIP, D, NA_W), F32)],
        compiler_params=_cp(("arbitrary",), 48))


def _final_loss(h, g_final, target, f2, g_post2, n_tok, tm):
    tp = h.shape[0]

    def body(h_ref, g_ref, t_ref, f_ref, gq_ref, dh_ref, df_ref, loss_ref, dg_ref, dgq_ref):
        i = pl.program_id(0)

        @pl.when(i == 0)
        def _():
            loss_ref[...] = jnp.zeros_like(loss_ref)
            dg_ref[...] = jnp.zeros_like(dg_ref)
            dgq_ref[...] = jnp.zeros_like(dgq_ref)

        x = h_ref[...]
        y = _rms(x, g_ref[...])
        row = i * tm + lax.broadcasted_iota(jnp.int32, (tm, 1), 0)
        valid = (row >= N_META) & (row < N_META + n_tok)
        e = jnp.where(valid, y - t_ref[...], 0.0)
        loss_ref[...] += 0.5 * jnp.sum(jnp.mean(e * e, axis=-1, keepdims=True), axis=0, keepdims=True)
        dx, dg = _rms_bwd(x, g_ref[...], e * (1.0 / D))
        dh_ref[...] = dx
        dg_ref[...] += dg
        df, dgq = _rms_bwd(f_ref[...], gq_ref[...], 0.5 * dx)
        df_ref[...] = df
        dgq_ref[...] += dgq

    return pl.pallas_call(
        body, name="final_loss", grid=(tp // tm,),
        in_specs=[_rows(tm, D), _full((1, D)), _rows(tm, D), _rows(tm, D), _full((1, D))],
        out_specs=[_rows(tm, D), _rows(tm, D), _full((1, 1)), _full((1, D)), _full((1, D))],
        out_shape=[_out((tp, D), F32), _out((tp, D), F32),
                   _out((1, 1), F32), _out((1, D), F32),
                   _out((1, D), F32)],
        compiler_params=_cp(("arbitrary",), 40),
    )(*_in_hbm(h, g_final, target, f2, g_post2))


def _na_patterns(n_rows):
    pats = []
    for kind in range(3):
        pat = [[-1] * K_ROWS for _ in range(Q_ROWS)]
        for i in range(Q_ROWS):
            for jj in range(K_ROWS):
                if kind == 0 and jj < KH:
                    pat[i][jj] = jj - i + KH - 1
                elif kind == 1 and i <= jj < i + KH:
                    pat[i][jj] = jj - i + 3
                elif kind == 2 and K_ROWS - KH <= jj:
                    pat[i][jj] = jj - i - 1
        pats.append(pat)
    return pats


def _diag_onehot():
    q = np.arange(GRID_W)[:, None]
    kc = np.arange(GRID_W)[None, :]
    start = np.clip(q - KW // 2, 0, GRID_W - KW)
    col_in = (kc >= start) & (kc < start + KW)
    e = np.zeros((32, GRID_W, GRID_W), np.float32)
    for d in range(2 * KW - 1):
        e[d] = ((kc - q + KW - 1) == d) & col_in
    return e.reshape(32, GRID_W * GRID_W), col_in


def _rpb_collapse(dtb2, et):
    def body(d_ref, e_ref, o_ref):
        o_ref[...] = jnp.dot(d_ref[...], e_ref[...], preferred_element_type=F32, precision=lax.Precision.HIGHEST)

    out = (dtb2.shape[0], et.shape[1])
    return pl.pallas_call(
        body, name="rpb_collapse", grid=(1,), out_shape=_out(out, F32),
        in_specs=[_full(dtb2.shape), _full(et.shape)], out_specs=_full(out),
    )(*_in_hbm(dtb2, et))


def _bias_tables(rpb, n_rows, comm=None, bounds=()):
    n_dr, n_dc = 2 * KH - 1, 2 * KW - 1
    pats = _na_patterns(n_rows)

    def body(rpb_ref, o_ref):
        h = pl.program_id(0)
        q = lax.broadcasted_iota(jnp.int32, (GRID_W, GRID_W), 0)
        kc = lax.broadcasted_iota(jnp.int32, (GRID_W, GRID_W), 1)
        start = jnp.clip(q - KW // 2, 0, GRID_W - KW)
        col_in = (kc >= start) & (kc < start + KW)
        diff = kc - q + (KW - 1)
        neg = jnp.full((GRID_W, GRID_W), NEG_INF, F32)
        band = []
        for dr in range(n_dr):
            acc = neg
            for d in range(n_dc):
                acc = jnp.where((diff == d) & col_in, rpb_ref[(h * n_dr + dr) * n_dc + d], acc)
            band.append(acc)
        for kind, pat in enumerate(pats):
            for i in range(Q_ROWS):
                for jj in range(K_ROWS):
                    o_ref[kind, 0, i * GRID_W:(i + 1) * GRID_W, jj * GRID_W:(jj + 1) * GRID_W] = (
                        band[pat[i][jj]] if pat[i][jj] >= 0 else neg)

    (bias,), got = _call(
        body, comm, bounds, (rpb.reshape(-1),), name="bias_tables", grid=(N_HEADS,),
        in_specs=[pl.BlockSpec(memory_space=pltpu.SMEM)],
        out_specs=[pl.BlockSpec((3, 1, QB, KB), lambda h: (0, h, 0, 0))],
        out_shape=[_out((3, N_HEADS, QB, KB), F32)],
        compiler_params=_cp(("arbitrary",), 32))
    return bias, got


def _attn_geometry(n_tok):
    n_rows = n_tok // GRID_W
    assert n_rows % Q_ROWS == 0 and n_rows >= K_ROWS
    return n_rows, n_rows // Q_ROWS


def _attn_probs(qh, kh, kmh, bias, scale):
    s = _dg(qh, kh, NT) * scale + bias
    sm = _dg(qh, kmh, NT) * scale
    m = jnp.maximum(jnp.max(s, axis=-1, keepdims=True), jnp.max(sm, axis=-1, keepdims=True))
    p = jnp.exp(s - m)
    pm = jnp.exp(sm - m)
    inv = 1.0 / (jnp.sum(p, axis=-1, keepdims=True) + jnp.sum(pm, axis=-1, keepdims=True))
    return p * inv, pm * inv


def _meta_probs(qmh, kmh, scale):
    s = _dg(qmh, kmh, NT) * scale
    p = jnp.exp(s - jnp.max(s, axis=-1, keepdims=True))
    return p / jnp.sum(p, axis=-1, keepdims=True)


def _step_rows(r, n_rows):
    q0 = pl.multiple_of(N_META + r * QB, 16)
    k0 = pl.multiple_of(N_META + jnp.clip(Q_ROWS * r - (K_ROWS - KH), 0, n_rows - K_ROWS) * GRID_W, 16)
    return q0, k0


def _attn_fwd(q, k, v, bias, n_tok, comm=None, bounds=()):
    tp = q.shape[0]
    n_rows, n_steps = _attn_geometry(n_tok)
    scale = HEAD_DIM ** -0.5

    def body(q_ref, k_ref, v_ref, b_ref, o_ref):
        r = pl.program_id(1)
        km = k_ref[0:N_META, :]
        vm = v_ref[0:N_META, :]

        @pl.when(r == 0)
        def _():
            qm = q_ref[0:N_META, :]
            outs = []
            for hh in range(2):
                sl = slice(hh * HEAD_DIM, (hh + 1) * HEAD_DIM)
                p = _meta_probs(qm[:, sl], km[:, sl], scale)
                outs.append(_dot(p.astype(BF16), vm[:, sl]))
            o_ref[0:N_META, :] = jnp.concatenate(outs, axis=1)
            o_ref[N_META + n_tok:, :] = jnp.zeros((tp - N_META - n_tok, 2 * HEAD_DIM), F32)

        q0, k0 = _step_rows(r, n_rows)
        qb = q_ref[pl.ds(q0, QB), :]
        kb = k_ref[pl.ds(k0, KB), :]
        vb = v_ref[pl.ds(k0, KB), :]
        outs = []
        for hh in range(2):
            sl = slice(hh * HEAD_DIM, (hh + 1) * HEAD_DIM)
            p, pm = _attn_probs(qb[:, sl], kb[:, sl], km[:, sl], b_ref[0, hh], scale)
            outs.append(_dot(p.astype(BF16), vb[:, sl]) + _dot(pm.astype(BF16), vm[:, sl]))
        o_ref[pl.ds(q0, QB), :] = jnp.concatenate(outs, axis=1)

    def bias_map(hp, r):
        return (jnp.where(r == 0, 0, jnp.where(r == n_steps - 1, 2, 1)), hp, 0, 0)

    col = pl.BlockSpec((tp, 2 * HEAD_DIM), lambda hp, r: (0, hp))
    return _call(
        body, comm, bounds, (q, k, v, bias), name="attn_fwd", grid=(N_HEADS // 2, n_steps),
        in_specs=[col, col, col, pl.BlockSpec((1, 2, QB, KB), bias_map)],
        out_specs=[col], out_shape=[_out((tp, NA_W), F32)],
        compiler_params=_cp(("arbitrary", "arbitrary"), 40))


def _attn_bwd(q, k, v, bias, do, n_tok, comm=None, bounds=()):
    tp = q.shape[0]
    n_rows, n_steps = _attn_geometry(n_tok)
    scale = HEAD_DIM ** -0.5
    pats = _na_patterns(n_rows)

    def body(q_ref, k_ref, v_ref, b_ref, do_ref, dq_ref, dk_ref, dv_ref, dtb_ref):
        r = pl.program_id(1)
        km = k_ref[0:N_META, :]
        vm = v_ref[0:N_META, :]

        @pl.when(r == 0)
        def _():
            dk_ref[...] = jnp.zeros_like(dk_ref)
            dv_ref[...] = jnp.zeros_like(dv_ref)
            dtb_ref[...] = jnp.zeros_like(dtb_ref)
            dq_ref[N_META + n_tok:, :] = jnp.zeros((tp - N_META - n_tok, 2 * HEAD_DIM), F32)
            qm = q_ref[0:N_META, :]
            dom = do_ref[0:N_META, :].astype(BF16)
            dqs, dks, dvs = [], [], []
            for hh in range(2):
                sl = slice(hh * HEAD_DIM, (hh + 1) * HEAD_DIM)
                p = _meta_probs(qm[:, sl], km[:, sl], scale)
                dp = _dg(dom[:, sl], vm[:, sl], NT)
                ds = (p * (dp - jnp.sum(dp * p, axis=-1, keepdims=True))).astype(BF16)
                dvs.append(_dg(p.astype(BF16), dom[:, sl], TN))
                dqs.append(_dot(ds, km[:, sl]) * scale)
                dks.append(_dg(ds, qm[:, sl], TN) * scale)
            dq_ref[0:N_META, :] = jnp.concatenate(dqs, axis=1)
            dk_ref[0:N_META, :] += jnp.concatenate(dks, axis=1)
            dv_ref[0:N_META, :] += jnp.concatenate(dvs, axis=1)

        q0, k0 = _step_rows(r, n_rows)
        qb = q_ref[pl.ds(q0, QB), :]
        kb = k_ref[pl.ds(k0, KB), :]
        vb = v_ref[pl.ds(k0, KB), :]
        dob = do_ref[pl.ds(q0, QB), :].astype(BF16)
        dqs, dks, dvs, dkms, dvms, dss = [], [], [], [], [], []
        for hh in range(2):
            sl = slice(hh * HEAD_DIM, (hh + 1) * HEAD_DIM)
            qh, kh, vh, kmh, vmh, doh = qb[:, sl], kb[:, sl], vb[:, sl], km[:, sl], vm[:, sl], dob[:, sl]
            p, pm = _attn_probs(qh, kh, kmh, b_ref[0, hh], scale)
            dp = _dg(doh, vh, NT)
            dpm = _dg(doh, vmh, NT)
            delta = jnp.sum(dp * p, axis=-1, keepdims=True) + jnp.sum(dpm * pm, axis=-1, keepdims=True)
            ds = p * (dp - delta)
            dsb = ds.astype(BF16)
            dsmb = (pm * (dpm - delta)).astype(BF16)
            dss.append(ds)
            dvs.append(_dg(p.astype(BF16), doh, TN))
            dvms.append(_dg(pm.astype(BF16), doh, TN))
            dqs.append((_dot(dsb, kh) + _dot(dsmb, kmh)) * scale)
            dks.append(_dg(dsb, qh, TN) * scale)
            dkms.append(_dg(dsmb, qh, TN) * scale)
        dq_ref[pl.ds(q0, QB), :] = jnp.concatenate(dqs, axis=1)
        dk_ref[pl.ds(k0, KB), :] += jnp.concatenate(dks, axis=1)
        dv_ref[pl.ds(k0, KB), :] += jnp.concatenate(dvs, axis=1)
        dk_ref[0:N_META, :] += jnp.concatenate(dkms, axis=1)
        dv_ref[0:N_META, :] += jnp.concatenate(dvms, axis=1)

        def add_bias_grad(pat):
            for hh in range(2):
                for i in range(Q_ROWS):
                    for jj in range(K_ROWS):
                        if pat[i][jj] >= 0:
                            dtb_ref[hh, pat[i][jj]] += dss[hh][i * GRID_W:(i + 1) * GRID_W,
                                                               jj * GRID_W:(jj + 1) * GRID_W]

        @pl.when(r == 0)
        def _():
            add_bias_grad(pats[0])

        @pl.when((r > 0) & (r < n_steps - 1))
        def _():
            add_bias_grad(pats[1])

        @pl.when(r == n_steps - 1)
        def _():
            add_bias_grad(pats[2])

    def bias_map(hp, r):
        return (jnp.where(r == 0, 0, jnp.where(r == n_steps - 1, 2, 1)), hp, 0, 0)

    col = pl.BlockSpec((tp, 2 * HEAD_DIM), lambda hp, r: (0, hp))
    n_dr = 2 * KH - 1
    return _call(
        body, comm, bounds, (q, k, v, bias, do), name="attn_bwd", grid=(N_HEADS // 2, n_steps),
        in_specs=[col, col, col, pl.BlockSpec((1, 2, QB, KB), bias_map), col],
        out_specs=[col, col, col, pl.BlockSpec((2, n_dr, GRID_W, GRID_W), lambda hp, r: (hp, 0, 0, 0))],
        out_shape=[_out((tp, NA_W), F32)] * 3 +
                  [_out((N_HEADS, n_dr, GRID_W, GRID_W), F32)],
        compiler_params=_cp(("arbitrary", "arbitrary"), 48))


def _repeat_onehot():
    return np.repeat(np.eye(2 * S5_G, dtype=np.float32), S5_H, axis=0)


def _s5_disc_math(lam_re, lam_im, log_dt, b_re, b_im, rep):
    dt = jnp.exp(log_dt)
    ea = jnp.exp(lam_re * dt)
    a_re = ea * jnp.cos(lam_im * dt)
    a_im = ea * jnp.sin(lam_im * dt)
    den = lam_re * lam_re + lam_im * lam_im
    c_re = ((a_re - 1.0) * lam_re + a_im * lam_im) / den
    c_im = (a_im * lam_re - (a_re - 1.0) * lam_im) / den
    ce_re = jnp.dot(rep, c_re, preferred_element_type=F32, precision=lax.Precision.HIGHEST)
    ce_im = jnp.dot(rep, c_im, preferred_element_type=F32, precision=lax.Precision.HIGHEST)
    return a_re, a_im, ce_re * b_re - ce_im * b_im, ce_re * b_im + ce_im * b_re


def _s5_blocks():
    gl = S5_G // N_BUNDLE
    half = gl * S5_P
    out = []
    for d in range(2):
        for g in range(S5_G):
            b, k = divmod(g, gl)
            dg = d * S5_G + g
            out.append((d, b, slice(k * S5_H, (k + 1) * S5_H), slice(k * S5_P, (k + 1) * S5_P),
                        slice(half + k * S5_P, half + (k + 1) * S5_P), slice(dg * S5_H, (dg + 1) * S5_H),
                        slice(dg, dg + 1)))
    return out


def _s5_params(lam_re, lam_im, log_dt, b_re, b_im, c_re, c_im):
    cw, sw = S5_W // N_BUNDLE, 2 * (S5_G // N_BUNDLE) * S5_P

    def body(lr, li, ld, br, bi, cr, ci, rep_ref, a_ref, a1_ref, a2_ref, bm_ref, cm_ref):
        a_re, a_im, bb_re, bb_im = _s5_disc_math(lr[...], li[...], ld[...], br[...], bi[...], rep_ref[...])
        cc_re = cr[...]
        cc_im = ci[...]
        bm_ref[...] = jnp.zeros_like(bm_ref)
        cm_ref[...] = jnp.zeros_like(cm_ref)
        for d, b, rows, re, im, nat, one in _s5_blocks():
            bm_ref[d, b, rows, re] = bb_re[nat, :].astype(BF16)
            bm_ref[d, b, rows, im] = bb_im[nat, :].astype(BF16)
            cm_ref[d, b, rows, re] = cc_re[nat, :].astype(BF16)
            cm_ref[d, b, rows, im] = (-cc_im[nat, :]).astype(BF16)
            a_ref[d, b, :, re] = a_re[one, :]
            a_ref[d, b, :, im] = a_im[one, :]
            k = rows.start // S5_H
            lanes = slice((k % 2) * S5_P, (k % 2 + 1) * S5_P)
            for part, (v1, v2) in enumerate(((a_re[one, :], a_im[one, :]), (a_re[one, :], -a_im[one, :]))):
                sub = slice(4 * part + k // 2, 4 * part + k // 2 + 1)
                a1_ref[d, b, sub, lanes] = v1
                a2_ref[d, b, sub, lanes] = v2

    args = (lam_re, lam_im, log_dt, b_re, b_im, c_re, c_im, jnp.asarray(_repeat_onehot()))
    outs = [((2, N_BUNDLE, 1, sw), F32)] + [((2, N_BUNDLE, 8, 128), F32)] * 2 + [((2, N_BUNDLE, cw, sw), BF16)] * 2
    return pl.pallas_call(
        body, name="s5_params", grid=(1,), in_specs=[_full(a.shape) for a in args],
        out_specs=[_full(s) for s, _ in outs], out_shape=[_out(s, dt) for s, dt in outs],
    )(*_in_hbm(*args))


def _s5_params_bwd(lam_re, lam_im, log_dt, b_re, b_im, da, dbm, dcm):
    n, nb = 2 * S5_G, 2 * S5_G * S5_H

    def body(lr, li, ld, br, bi, rep_ref, da_ref, dbm_ref, dcm_ref, o_lr, o_li, o_ld, o_br, o_bi, o_cr, o_ci,
             dar_s, dai_s, dbr_s, dbi_s):
        for d, b, rows, re, im, nat, one in _s5_blocks():
            dbr_s[nat, :] = dbm_ref[d, b, rows, re]
            dbi_s[nat, :] = dbm_ref[d, b, rows, im]
            o_cr[nat, :] = dcm_ref[d, b, rows, re]
            o_ci[nat, :] = -dcm_ref[d, b, rows, im]
            dar_s[one, :] = da_ref[d, b, :, re]
            dai_s[one, :] = da_ref[d, b, :, im]
        rep = rep_ref[...]
        _, vjp = jax.vjp(lambda p, q, r, s, t: _s5_disc_math(p, q, r, s, t, rep),
                         lr[...], li[...], ld[...], br[...], bi[...])
        o_lr[...], o_li[...], o_ld[...], o_br[...], o_bi[...] = vjp((dar_s[...], dai_s[...], dbr_s[...], dbi_s[...]))

    args = (lam_re, lam_im, log_dt, b_re, b_im, jnp.asarray(_repeat_onehot()), da, dbm, dcm)
    outs = [(n, S5_P)] * 2 + [(n, 1)] + [(nb, S5_P)] * 4
    return pl.pallas_call(
        body, name="s5_params_bwd", grid=(1,), in_specs=[_full(a.shape) for a in args],
        out_specs=[_full(s) for s in outs], out_shape=[_out(s, F32) for s in outs],
        scratch_shapes=[pltpu.VMEM((n, S5_P), F32)] * 2 + [pltpu.VMEM((nb, S5_P), F32)] * 2,
    )(*_in_hbm(*args))


def _tiles_store(ref, base, val):
    for i in range(val.shape[0] // 8):
        for c in range(8):
            ref[pl.ds(base + (8 * i + c) * 8, 8), :] = val[8 * i:8 * i + 8, 128 * c:128 * (c + 1)]


def _tiles_load(ref, base, n):
    return jnp.concatenate(
        [jnp.concatenate([ref[pl.ds(base + (8 * i + c) * 8, 8), :] for c in range(8)], axis=1) for i in range(n // 8)],
        axis=0)


def _time_rows(base, t):
    return pl.ds(base + (t // 8) * 64 + t % 8, 8, stride=8)


def _scan(chains, n):
    xs = [c["x"] for c in chains]
    for k in range(n):
        for ci, c in enumerate(chains):
            t = n - 1 - k if c["reverse"] else k
            if c["prev"] is not None:
                c["prev"][_time_rows(c["prev_base"], t), :] = xs[ci]
            xs[ci] = c["a1"] * xs[ci] + pltpu.roll(c["a2"] * xs[ci], 4, axis=0) + c["src"][_time_rows(0, t), :]
            if c["dst"] is not None:
                c["dst"][_time_rows(0, t), :] = xs[ci]
    return xs


def _chain(x, a1, a2, src, dst=None, prev=None, prev_base=0, reverse=False):
    return dict(x=x, a1=a1, a2=a2, src=src, dst=dst, prev=prev, prev_base=prev_base, reverse=reverse)


def _s5_fwd(u, d_skip, a1, a2, bm, cm, length, comm=None, bounds=()):
    tp = u.shape[0]
    cw = S5_W // N_BUNDLE
    sw = bm.shape[-1]
    n_full, n_tail = divmod(length, SCAN_CHUNK)
    t_tail = n_full * SCAN_CHUNK

    nbs = N_BUNDLE

    def body(u_ref, d_ref, a1_ref, a2_ref, bm_ref, cm_ref, y_ref, bnd_ref, *scratch):
        y_ref[...] = u_ref[...] * d_ref[...]
        ins, xss = (scratch[0:nbs], scratch[nbs:2 * nbs]), (scratch[2 * nbs:3 * nbs], scratch[3 * nbs:])
        cols = [slice(b * cw, (b + 1) * cw) for b in range(nbs)]

        def keep(dr, chunk, xs):
            for b in range(nbs):
                bnd_ref[dr, b, chunk] = xs[b]

        def load(dr, t0, n):
            for b in range(nbs):
                _tiles_store(ins[dr][b], 0, _dot(u_ref[pl.ds(t0, n), cols[b]].astype(BF16), bm_ref[dr, b]))

        def chains(dr, xs):
            return [_chain(xs[b], a1_ref[dr, b], a2_ref[dr, b], ins[dr][b], dst=xss[dr][b], reverse=dr == 1)
                    for b in range(nbs)]

        def emit(dr, t0, n):
            for b in range(nbs):
                y_ref[pl.ds(t0, n), cols[b]] += _dg(_tiles_load(xss[dr][b], 0, n).astype(BF16), cm_ref[dr, b], NT)

        zero = (jnp.zeros((8, 128), F32),) * nbs
        xb = zero
        if n_tail:
            keep(1, n_full, xb)
            load(1, t_tail, n_tail)
            xb = tuple(_scan(chains(1, xb), n_tail))
            emit(1, t_tail, n_tail)

        def pair(i, carry):
            j = n_full - 1 - i
            t0s = (pl.multiple_of(i * SCAN_CHUNK, SCAN_CHUNK), pl.multiple_of(j * SCAN_CHUNK, SCAN_CHUNK))
            keep(0, i, carry[0])
            keep(1, j, carry[1])
            for dr in range(2):
                load(dr, t0s[dr], SCAN_CHUNK)
            out = _scan(chains(0, carry[0]) + chains(1, carry[1]), SCAN_CHUNK)
            for dr in range(2):
                emit(dr, t0s[dr], SCAN_CHUNK)
            return tuple(out[:nbs]), tuple(out[nbs:])

        xf, _ = lax.fori_loop(0, n_full, pair, (zero, xb))
        if n_tail:
            keep(0, n_full, xf)
            load(0, t_tail, n_tail)
            _scan(chains(0, xf), n_tail)
            emit(0, t_tail, n_tail)

    n_chunks = n_full + (1 if n_tail else 0)
    tile = pl.BlockSpec((2, nbs, 8, 128), lambda b: (0, b, 0, 0))
    return _call(
        body, comm, bounds, (u, d_skip, a1, a2, bm, cm), name="s5_fwd", grid=(N_BUNDLE // nbs,),
        in_specs=[pl.BlockSpec((tp, nbs * cw), lambda b: (0, b)), pl.BlockSpec((1, nbs * cw), lambda b: (0, b)),
                  tile, tile, pl.BlockSpec((2, nbs, cw, sw), lambda b: (0, b, 0, 0)),
                  pl.BlockSpec((2, nbs, cw, sw), lambda b: (0, b, 0, 0))],
        out_specs=[pl.BlockSpec((tp, nbs * cw), lambda b: (0, b)),
                   pl.BlockSpec((2, nbs, n_chunks, 8, 128), lambda b: (0, b, 0, 0, 0))],
        out_shape=[_out((tp, S5_W), F32), _out((2, N_BUNDLE, n_chunks, 8, 128), F32)],
        scratch_shapes=[pltpu.VMEM((SCAN_CHUNK * 8, 128), F32)] * (4 * nbs),
        compiler_params=_cp(("arbitrary",), 48))


def _s5_bwd(u, dy, d_skip, a, a1, a2, bm, cm, bnd, length):
    tp = u.shape[0]
    cw = S5_W // N_BUNDLE
    sw = bm.shape[-1]
    half = sw // 2
    n_full, n_tail = divmod(length, SCAN_CHUNK)
    t_tail = n_full * SCAN_CHUNK
    n_chunks = bnd.shape[2]
    nbs = 2

    def body(u_ref, dy_ref, d_ref, a_ref, a1_ref, a2_ref, bm_ref, cm_ref, bnd_ref, du_ref, dd_ref, dbm_ref, dcm_ref,
             da_ref, *scratch):
        du_ref[...] = dy_ref[...] * d_ref[...]
        dd_ref[...] = jnp.sum(dy_ref[...] * u_ref[...], axis=0, keepdims=True)
        dbm_ref[...] = jnp.zeros_like(dbm_ref)
        dcm_ref[...] = jnp.zeros_like(dcm_ref)
        da_ref[...] = jnp.zeros_like(da_ref)
        bu_s, dx_s, g_s, xp_s = ([scratch[(k * 2 + dr) * nbs:(k * 2 + dr + 1) * nbs] for dr in range(2)] for k in range(4))
        cols = [slice(b * cw, (b + 1) * cw) for b in range(nbs)]

        def chains(dr, chunk, t0, n, gs):
            out = []
            for b in range(nbs):
                _tiles_store(bu_s[dr][b], 0, _dot(u_ref[pl.ds(t0, n), cols[b]].astype(BF16), bm_ref[dr, b]))
                _tiles_store(dx_s[dr][b], 0, _dot(dy_ref[pl.ds(t0, n), cols[b]].astype(BF16), cm_ref[dr, b]))
                out.append(_chain(bnd_ref[dr, b, chunk], a1_ref[dr, b], a2_ref[dr, b], bu_s[dr][b],
                                  prev=xp_s[dr][b], reverse=dr == 1))
                out.append(_chain(gs[b], a1_ref[dr, b], -a2_ref[dr, b], dx_s[dr][b], dst=g_s[dr][b], reverse=dr == 0))
            return out

        def emit(dr, t0, n):
            rows = pl.ds(t0, n)
            for b in range(nbs):
                ub = u_ref[rows, cols[b]].astype(BF16)
                dyb = dy_ref[rows, cols[b]].astype(BF16)
                g = _tiles_load(g_s[dr][b], 0, n)
                gb = g.astype(BF16)
                du_ref[rows, cols[b]] += _dg(gb, bm_ref[dr, b], NT)
                dbm_ref[dr, b] += _dg(ub, gb, TN)
                xp = _tiles_load(xp_s[dr][b], 0, n)
                xp_r, xp_i = xp[:, 0:half], xp[:, half:]
                g_r, g_i = g[:, 0:half], g[:, half:]
                a_re = a_ref[dr, b, :, 0:half]
                a_im = a_ref[dr, b, :, half:]
                bu = _dot(ub, bm_ref[dr, b])
                x_r = a_re * xp_r - a_im * xp_i + bu[:, 0:half]
                x_i = a_re * xp_i + a_im * xp_r + bu[:, half:]
                dcm_ref[dr, b] += _dg(dyb, jnp.concatenate([x_r, x_i], axis=1).astype(BF16), TN)
                da_ref[dr, b] += jnp.concatenate([jnp.sum(g_r * xp_r + g_i * xp_i, axis=0, keepdims=True),
                                                  jnp.sum(g_i * xp_r - g_r * xp_i, axis=0, keepdims=True)], axis=1)

        def adjoints(out):
            return tuple(out[1::2])

        zero = (jnp.zeros((8, 128), F32),) * nbs
        g0 = zero
        if n_tail:
            g0 = adjoints(_scan(chains(0, n_full, t_tail, n_tail, g0), n_tail))
            emit(0, t_tail, n_tail)

        def pair(i, carry):
            j = n_full - 1 - i
            t0 = (pl.multiple_of(j * SCAN_CHUNK, SCAN_CHUNK), pl.multiple_of(i * SCAN_CHUNK, SCAN_CHUNK))
            both = chains(0, j, t0[0], SCAN_CHUNK, carry[0]) + chains(1, i, t0[1], SCAN_CHUNK, carry[1])
            out = _scan(both, SCAN_CHUNK)
            emit(0, t0[0], SCAN_CHUNK)
            emit(1, t0[1], SCAN_CHUNK)
            return adjoints(out[:2 * nbs]), adjoints(out[2 * nbs:])

        _, g1 = lax.fori_loop(0, n_full, pair, (g0, zero))
        if n_tail:
            _scan(chains(1, n_full, t_tail, n_tail, g1), n_tail)
            emit(1, t_tail, n_tail)

    tile = pl.BlockSpec((2, nbs, 8, 128), lambda b: (0, b, 0, 0))
    wide = pl.BlockSpec((2, nbs, cw, sw), lambda b: (0, b, 0, 0))
    col = pl.BlockSpec((tp, nbs * cw), lambda b: (0, b))
    row = pl.BlockSpec((1, nbs * cw), lambda b: (0, b))
    arow = pl.BlockSpec((2, nbs, 1, sw), lambda b: (0, b, 0, 0))
    return pl.pallas_call(
        body, name="s5_bwd", grid=(N_BUNDLE // nbs,),
        in_specs=[col, col, row, arow, tile, tile, wide, wide,
                  pl.BlockSpec((2, nbs, n_chunks, 8, 128), lambda b: (0, b, 0, 0, 0))],
        out_specs=[col, row, wide, wide, arow],
        out_shape=[_out((tp, S5_W), F32), _out((1, S5_W), F32),
                   _out((2, N_BUNDLE, cw, sw), F32), _out((2, N_BUNDLE, cw, sw), F32),
                   _out((2, N_BUNDLE, 1, sw), F32)],
        scratch_shapes=[pltpu.VMEM((SCAN_CHUNK * 8, 128), F32)] * (8 * nbs),
        compiler_params=_cp(("arbitrary",), 56),
    )(*_in_hbm(u, dy, d_skip, a, a1, a2, bm, cm, bnd))


def _row_tile(tp):
    return max(tm for tm in range(16, 449, 16) if tp % tm == 0)


def _step(x, target, bufs, gains, s5, rpb, c_arr, kc_arr, me_arr):
    n_tok = x.shape[0]
    first = ["ffn1_w_gate", "ffn1_w_up", "ffn1_w_down", "meta_tokens"]
    bias, got = _bias_tables(rpb, n_tok // GRID_W, _gather_comm([bufs[n] for n in first]), (0, N_HEADS - 1))
    w = dict(zip(first, got))
    meta = w["meta_tokens"].transpose(1, 0, 2).reshape(N_META, D)
    length = N_META + n_tok
    tp = length + 16
    tm = _row_tile(tp)
    tmb = tm
    n_rows = n_tok // GRID_W
    pad = jnp.zeros((tp - length, D), F32)
    h0 = jnp.concatenate([meta, x, pad], axis=0)
    tgt = jnp.concatenate([jnp.zeros((N_META, D), F32), target, pad], axis=0)

    s5p = (s5["lam_re"], s5["lam_im"], s5["log_dt"].reshape(2 * S5_G, 1), s5["b_re"], s5["b_im"])
    a_m, a1_m, a2_m, bm16, cm16 = _s5_params(*s5p, s5["c_re"], s5["c_im"])

    mid = ["w_in", "s5_w_glu", "w_out"]
    (h1, gate1, up1, f1), got = _ffn_fwd(
        "ffn1_fwd", h0, gains["ffn1_pre_g"], gains["ffn1_post_g"], w["ffn1_w_gate"], w["ffn1_w_up"], w["ffn1_w_down"],
        tm, _gather_comm([bufs[n] for n in mid]), (0, (tp // tm) * N_CHIP * 3 // 5))
    w.update(zip(mid, got))
    q, k, v, u = _mix_in(h1, gains["mix_pre_g"], w["w_in"], tm)
    (o_na,), (gate_ici, up_ici) = _attn_fwd(
        q, k, v, bias, n_tok, _gather_comm([bufs["ffn2_w_gate"], bufs["ffn2_w_up"]], pair=False), (0,))
    (y_pre, s5_bnd), (w["ffn2_w_gate"], w["ffn2_w_up"], down_ici) = _s5_fwd(
        u, gains["s5_d"], a1_m, a2_m, bm16, cm16, length,
        _merge_comm(_gather_comm([gate_ici, up_ici], ici=False),
                    _gather_comm([bufs["ffn2_w_down"]], pair=False)), (0,))
    w_glu = w["s5_w_glu"].reshape(S5_W, S5_W)
    w_out = w["w_out"].reshape(D, D)
    (h2, mix), (w["ffn2_w_down"],) = _mix_out(
        o_na, y_pre, h1, w_glu, gains["s5_b_glu"], gains["na_out_g"], gains["s5_out_g"], w_out, gains["mix_post_g"], tm,
        _gather_comm([down_ici], ici=False), (0,))
    (h3, gate2, up2, f2), _ = _ffn_fwd("ffn2_fwd", h2, gains["ffn2_pre_g"], gains["ffn2_post_g"],
                                       w["ffn2_w_gate"], w["ffn2_w_up"], w["ffn2_w_down"], tm)
    dh3, df2, loss, dg_final, dg_post2 = _final_loss(h3, gains["final_g"], tgt, f2, gains["ffn2_post_g"], n_tok, tm)

    ffn2 = ["ffn2_w_gate", "ffn2_w_up", "ffn2_w_down"]
    ffn1 = ["ffn1_w_gate", "ffn1_w_up", "ffn1_w_down"]
    out2, _ = _ffn_bwd("ffn2_bwd", h2, gains["ffn2_pre_g"], df2, gate2, up2,
                       w["ffn2_w_gate"], w["ffn2_w_up"], w["ffn2_w_down"], tmb)
    dxn2 = out2[3]
    sums2 = [_chip_sum("chip_sum_" + n, g, r, c_arr) for n, g, r in zip(ffn2, out2[0:3], out2[4:7])]
    (dh2, dg_pre2), _ = _ffn_pre_bwd("ffn2_pre_bwd", dh3, dxn2, h2, gains["ffn2_pre_g"], tm)
    do_na, dy_pre, dw_out, dw_glu, dg_mpost, dg_na, dg_s5, db_glu = _mix_out_bwd(
        dh2, mix, o_na, y_pre, w_glu, gains["s5_b_glu"], gains["na_out_g"], gains["s5_out_g"], w_out,
        gains["mix_post_g"], tm)
    (dq, dk, dv, dtb), recv3 = _attn_bwd(q, k, v, bias, do_na, n_tok, _scatter_comm(sums2), (0,))
    totals2 = [_total_sum("total_sum_" + n, s, r, kc_arr) for n, s, r in zip(ffn2, sums2, recv3)]
    du, dd, dbm, dcm, da_m = _s5_bwd(u, dy_pre, gains["s5_d"], a_m, a1_m, a2_m, bm16, cm16, s5_bnd, length)
    (dh1, df1, dw_in, dg_mpre, dg_post1), done2 = _mix_in_bwd(
        dq, dk, dv, du, h1, gains["mix_pre_g"], w["w_in"], dh2, f1, gains["ffn1_post_g"], tm,
        _assemble_comm(totals2), (0,))
    pieces = dict(zip(ffn2, done2))

    e, _ = _diag_onehot()
    n_dr = 2 * KH - 1
    drpb = _rpb_collapse(dtb.reshape(N_HEADS * n_dr, GRID_W * GRID_W), jnp.asarray(e.T))
    drpb = drpb[:, :2 * KW - 1].reshape(N_HEADS, n_dr, 2 * KW - 1).transpose(1, 0, 2).reshape(N_HEADS * n_dr, 2 * KW - 1)
    dlam_re, dlam_im, dlog_dt, db_re, db_im, dc_re, dc_im = _s5_params_bwd(*s5p, da_m, dbm, dcm)
    early = {"ffn1_post_g": dg_post1, "mix_pre_g": dg_mpre, "na_rpb": drpb,
             "s5_lam_re": dlam_re, "s5_lam_im": dlam_im, "s5_log_dt": dlog_dt.reshape(2, S5_G),
             "s5_b_re": db_re, "s5_b_im": db_im, "s5_c_re": dc_re, "s5_c_im": dc_im,
             "s5_d": dd, "s5_b_glu": db_glu, "na_out_g": dg_na,
             "s5_out_g": dg_s5, "mix_post_g": dg_mpost, "ffn2_pre_g": dg_pre2, "ffn2_post_g": dg_post2,
             "final_g": dg_final}
    names = list(early)
    slots = _small_pack([early[n] for n in names], me_arr)

    out1, slots = _ffn_bwd("ffn1_bwd", h0, gains["ffn1_pre_g"], df1, gate1, up1,
                           w["ffn1_w_gate"], w["ffn1_w_up"], w["ffn1_w_down"], tmb, _spread_comm(slots), (0,))
    small = dict(zip(names, _small_total(slots, [early[n].shape for n in names])))
    rest = [dw_in, dw_glu.reshape(N_CHIP, S5_W // N_CHIP, S5_W), dw_out.reshape(N_CHIP, D // N_CHIP, D)]
    (dh0, dg_pre1), recv_rest = _ffn_pre_bwd("ffn1_pre_bwd", dh1, out1[3], h0, gains["ffn1_pre_g"], tm,
                                             _exchange_comm(rest), (0,))
    last = ffn1 + mid
    sums = [_chip_sum("chip_sum_" + n, g, r, c_arr)
            for n, g, r in zip(last, list(out1[0:3]) + rest, list(out1[4:7]) + list(recv_rest))]
    return loss[0, 0], dh0, pieces, small, {"ffn1_pre_g": dg_pre1}, last, sums


def _mesh_pos():
    return lax.axis_index("x"), lax.axis_index("y"), lax.axis_index("c")


def _other_chips(x, y):
    return [(1 - x, y), (x, 1 - y), (1 - x, 1 - y)]


class _Comm:
    def __init__(self, ins, out_shape, aliases, parts):
        self.ins, self.out_shape, self.aliases, self.parts = list(ins), list(out_shape), dict(aliases), list(parts)
        self.n_sems = sum(p[0] for p in parts)

    def bases(self):
        out, base = [], 0
        for n_sems, _, _ in self.parts:
            out.append(base)
            base += n_sems
        return out


def _run_comm(name, comm):
    n_i, n_o = len(comm.ins), len(comm.out_shape)

    def body(*refs):
        ins, outs = refs[:n_i], refs[n_i:n_i + n_o]
        send_sems, recv_sems = refs[n_i + n_o:]
        for base, (_, start, finish) in zip(comm.bases(), comm.parts):
            start(ins, outs, send_sems, recv_sems, base)
            finish(ins, outs, send_sems, recv_sems, base)

    return pl.pallas_call(
        body, name=name, out_shape=comm.out_shape, in_specs=[ANY] * n_i, out_specs=[ANY] * n_o,
        input_output_aliases=comm.aliases,
        scratch_shapes=[pltpu.SemaphoreType.DMA((comm.n_sems,)), pltpu.SemaphoreType.DMA((comm.n_sems,))],
    )(*_in_hbm(*comm.ins))


def _call(body, comm, bounds, args, *, name, grid, in_specs, out_specs, out_shape, scratch_shapes=(),
          compiler_params=None):
    in_specs, out_specs, out_shape, scratch_shapes = list(in_specs), list(out_specs), list(out_shape), list(scratch_shapes)
    if comm is None:
        return pl.pallas_call(body, name=name, grid=grid, in_specs=in_specs, out_specs=out_specs, out_shape=out_shape,
                              scratch_shapes=scratch_shapes, compiler_params=compiler_params)(*_in_hbm(*args)), []
    n_in, n_out, n_scr = len(in_specs), len(out_specs), len(scratch_shapes)
    n_ci, n_co = len(comm.ins), len(comm.out_shape)
    n_steps = int(np.prod(grid))
    assert len(bounds) == len(comm.parts) and all(0 <= b < n_steps for b in bounds) and list(bounds) == sorted(bounds)

    def fused(*refs):
        a = n_in
        b = a + n_ci
        c = b + n_out
        d = c + n_co
        e = d + n_scr
        cargs = (refs[a:b], refs[c:d], refs[e], refs[e + 1])
        step = pl.program_id(0)
        for ax in range(1, len(grid)):
            step = step * grid[ax] + pl.program_id(ax)
        bases = comm.bases()
        for p, (_, start, finish) in enumerate(comm.parts):
            @pl.when(step == bounds[p])
            def _(p=p, start=start):
                if p > 0:
                    comm.parts[p - 1][2](*cargs, bases[p - 1])
                start(*cargs, bases[p])
        body(*(refs[:a] + refs[b:c] + refs[d:e]))

        @pl.when(step == n_steps - 1)
        def _():
            comm.parts[-1][2](*cargs, bases[-1])

    res = pl.pallas_call(
        fused, name=name, grid=grid, in_specs=in_specs + [ANY] * n_ci, out_specs=out_specs + [ANY] * n_co,
        out_shape=out_shape + comm.out_shape,
        scratch_shapes=scratch_shapes + [pltpu.SemaphoreType.DMA((comm.n_sems,)), pltpu.SemaphoreType.DMA((comm.n_sems,))],
        input_output_aliases={n_in + i: n_out + j for i, j in comm.aliases.items()},
        compiler_params=compiler_params)(*_in_hbm(*args, *comm.ins))
    return res[:n_out], res[n_out:]


def _remote(src, dst, send_sems, recv_sems, idx, to):
    return pltpu.make_async_remote_copy(src_ref=src, dst_ref=dst, send_sem=send_sems.at[idx],
                                        recv_sem=recv_sems.at[idx], device_id=to, device_id_type=MESH_ID)


def _gather_comm(bufs, ici=True, pair=True):
    n = len(bufs)

    def half(ref, k, pc):
        rh = ref.shape[1] // 2
        return ref.at[k, pl.ds(pc * rh, rh), :]

    def ici_start(ins, outs, ss, rs, base):
        x, y, c = _mesh_pos()
        for a in range(n):
            mine = half(outs[a], 2 * x + y, c)
            for j, chip in enumerate(_other_chips(x, y)):
                _remote(mine, mine, ss, rs, base + 3 * a + j, (*chip, c)).start()

    def ici_finish(ins, outs, ss, rs, base):
        x, y, c = _mesh_pos()
        for a in range(n):
            for j, chip in enumerate(_other_chips(x, y)):
                theirs = half(outs[a], 2 * chip[0] + chip[1], c)
                _remote(theirs, theirs, ss, rs, base + 3 * a + j, (*chip, c)).wait()

    def pair_copy(outs, ss, rs, base, a):
        x, y, c = _mesh_pos()
        rh = outs[a].shape[1] // 2
        held = outs[a].at[:, pl.ds(c * rh, rh), :]
        return _remote(held, held, ss, rs, base + a, (x, y, 1 - c))

    def pair_start(ins, outs, ss, rs, base):
        for a in range(n):
            pair_copy(outs, ss, rs, base, a).start()

    def pair_finish(ins, outs, ss, rs, base):
        for a in range(n):
            pair_copy(outs, ss, rs, base, a).wait()

    parts = ([(3 * n, ici_start, ici_finish)] if ici else []) + ([(n, pair_start, pair_finish)] if pair else [])
    return _Comm(bufs, [_out(b.shape, b.dtype) for b in bufs], {a: a for a in range(n)}, parts)


def _merge_comm(*comms):
    ins, shapes, aliases, subs, base = [], [], {}, [], 0
    for cm in comms:
        (n_sems, start, finish), = cm.parts
        i0, o0 = len(ins), len(shapes)
        subs.append((slice(i0, i0 + len(cm.ins)), slice(o0, o0 + len(cm.out_shape)), base, start, finish))
        aliases.update({i0 + i: o0 + j for i, j in cm.aliases.items()})
        ins += cm.ins
        shapes += cm.out_shape
        base += n_sems

    def start_all(ins_r, outs_r, ss, rs, b):
        for si, so, off, start, _ in subs:
            start(ins_r[si], outs_r[so], ss, rs, b + off)

    def finish_all(ins_r, outs_r, ss, rs, b):
        for si, so, off, _, finish in subs:
            finish(ins_r[si], outs_r[so], ss, rs, b + off)

    return _Comm(ins, shapes, aliases, [(base, start_all, finish_all)])


def _own_half_buffers(pieces, dtypes, kc_arr):
    n = len(pieces)

    def body(kc_ref, *refs):
        for a in range(n):
            refs[n + a][0] = refs[a][...].astype(dtypes[a])

    def half(p):
        return p.shape[0] // 2, p.shape[1]

    return pl.pallas_call(
        body, name="own_halves",
        out_shape=[_out((N_CHIP,) + p.shape, dt) for p, dt in zip(pieces, dtypes)],
        grid_spec=pltpu.PrefetchScalarGridSpec(
            num_scalar_prefetch=1, grid=(1,),
            in_specs=[pl.BlockSpec(half(p), lambda i, kc: (kc[1], 0)) for p in pieces],
            out_specs=[pl.BlockSpec((1,) + half(p), lambda i, kc: (kc[0], kc[1], 0)) for p in pieces]),
        compiler_params=_cp(("arbitrary",), 48),
    )(kc_arr, *_in_hbm(*pieces))


def _exchange_comm(grads):
    n = len(grads)

    def copy(ins, outs, ss, rs, base, a):
        x, y, c = _mesh_pos()
        rh = ins[a].shape[1] // 2
        return _remote(ins[a].at[:, pl.ds((1 - c) * rh, rh), :], outs[a], ss, rs, base + a, (x, y, 1 - c))

    def start(ins, outs, ss, rs, base):
        for a in range(n):
            copy(ins, outs, ss, rs, base, a).start()

    def finish(ins, outs, ss, rs, base):
        for a in range(n):
            copy(ins, outs, ss, rs, base, a).wait()

    shapes = [_out((N_CHIP, g.shape[1] // 2, g.shape[2]), g.dtype) for g in grads]
    return _Comm(grads, shapes, {}, [(n, start, finish)])


def _chip_sum(name, g, recv, c_arr):
    _, r, cc = g.shape
    rh = r // 2

    def body(c_ref, g_ref, r_ref, o_ref):
        o_ref[...] = (g_ref[...] + r_ref[...]).astype(BF16)

    return pl.pallas_call(
        body, name=name, out_shape=_out((N_CHIP, rh, cc), BF16),
        grid_spec=pltpu.PrefetchScalarGridSpec(
            num_scalar_prefetch=1, grid=(N_CHIP,),
            in_specs=[pl.BlockSpec((1, rh, cc), lambda j, c_ref: (j, c_ref[0], 0)),
                      pl.BlockSpec((1, rh, cc), lambda j, c_ref: (j, 0, 0))],
            out_specs=pl.BlockSpec((1, rh, cc), lambda j, c_ref: (j, 0, 0))),
        compiler_params=_cp(("arbitrary",), 32),
    )(c_arr, *_in_hbm(g, recv))


def _scatter_comm(sums):
    n = len(sums)

    def copies(ins, outs, ss, rs, base):
        x, y, c = _mesh_pos()
        return [_remote(ins[a].at[2 * chip[0] + chip[1]], outs[a].at[j], ss, rs, base + 3 * a + j, (*chip, c))
                for a in range(n) for j, chip in enumerate(_other_chips(x, y))]

    def start(ins, outs, ss, rs, base):
        for cp in copies(ins, outs, ss, rs, base):
            cp.start()

    def finish(ins, outs, ss, rs, base):
        for cp in copies(ins, outs, ss, rs, base):
            cp.wait()

    shapes = [_out((3,) + s.shape[1:], s.dtype) for s in sums]
    return _Comm(sums, shapes, {}, [(3 * n, start, finish)])


def _total_sum(name, sums, recv3, kc_arr):
    _, rh, cc = sums.shape

    def body(kc_ref, s_ref, r_ref, o_ref):
        t = s_ref[0].astype(F32) + r_ref[0].astype(F32)
        t = t + r_ref[1].astype(F32)
        o_ref[...] = t + r_ref[2].astype(F32)

    return pl.pallas_call(
        body, name=name, out_shape=_out((2 * rh, cc), F32),
        grid_spec=pltpu.PrefetchScalarGridSpec(
            num_scalar_prefetch=1, grid=(1,),
            in_specs=[pl.BlockSpec((1, rh, cc), lambda i, kc_ref: (kc_ref[0], 0, 0)),
                      pl.BlockSpec((3, rh, cc), lambda i, kc_ref: (0, 0, 0))],
            out_specs=pl.BlockSpec((rh, cc), lambda i, kc_ref: (kc_ref[1], 0))),
        compiler_params=_cp(("arbitrary",), 32),
    )(kc_arr, *_in_hbm(sums, recv3))


def _assemble_comm(totals):
    n = len(totals)

    def copy(outs, ss, rs, base, a):
        x, y, c = _mesh_pos()
        rh = outs[a].shape[0] // 2
        here = outs[a].at[pl.ds(c * rh, rh), :]
        return _remote(here, here, ss, rs, base + a, (x, y, 1 - c))

    def start(ins, outs, ss, rs, base):
        for a in range(n):
            copy(outs, ss, rs, base, a).start()

    def finish(ins, outs, ss, rs, base):
        for a in range(n):
            copy(outs, ss, rs, base, a).wait()

    shapes = [_out(t.shape, t.dtype) for t in totals]
    return _Comm(totals, shapes, {a: a for a in range(n)}, [(n, start, finish)])


def _small_layout(shapes):
    n = len(shapes)
    narrow_w = 64
    wide = [a for a in range(n) if shapes[a][1] > narrow_w]
    narrow = sorted((a for a in range(n) if shapes[a][1] <= narrow_w), key=lambda a: -shapes[a][0])
    offs, cols, groups, widths, rows = {}, {}, [], [], []
    if wide:
        r = 0
        for a in wide:
            offs[a], cols[a] = r, 0
            r += shapes[a][0]
        groups.append(wide)
        widths.append(max(shapes[a][1] for a in wide))
        rows.append(-(-r // 8) * 8)
    if narrow:
        heights = [0, 0]
        for a in narrow:
            side = 0 if heights[0] <= heights[1] else 1
            offs[a], cols[a] = heights[side], side * narrow_w
            heights[side] += shapes[a][0]
        groups.append(narrow)
        widths.append(2 * narrow_w)
        rows.append(-(-max(heights) // 8) * 8)

    def window(ref, a):
        return ref.at[offs[a]:offs[a] + shapes[a][0], cols[a]:cols[a] + shapes[a][1]]

    return groups, widths, rows, window


def _small_pack(arrays, me_arr):
    shapes = [a.shape for a in arrays]
    groups, widths, rows, window = _small_layout(shapes)
    n, n_g = len(arrays), len(groups)

    def body(me_ref, *refs):
        ins, outs = refs[:n], refs[n:]
        for gi, g in enumerate(groups):
            outs[gi][...] = jnp.zeros_like(outs[gi])
            for a in g:
                window(outs[gi].at[0], a)[...] = ins[a][...]

    return pl.pallas_call(
        body, name="small_pack", out_shape=[_out((8, r, w), F32) for r, w in zip(rows, widths)],
        grid_spec=pltpu.PrefetchScalarGridSpec(
            num_scalar_prefetch=1, grid=(1,), in_specs=[pl.BlockSpec(s, lambda i, me: (0, 0)) for s in shapes],
            out_specs=[pl.BlockSpec((1, r, w), lambda i, me: (me[0], 0, 0)) for r, w in zip(rows, widths)]),
        compiler_params=_cp(("arbitrary",), 32),
    )(me_arr, *_in_hbm(*arrays))


def _spread_comm(slots):
    n = len(slots)
    flips = [(dx, dy, dc) for dx in range(2) for dy in range(2) for dc in range(2)][1:]

    def copies(outs, ss, rs, base):
        x, y, c = _mesh_pos()
        mine = 4 * x + 2 * y + c
        return [_remote(outs[a].at[mine], outs[a].at[mine], ss, rs, base + 7 * a + f,
                        (x ^ dx, y ^ dy, c ^ dc)) for a in range(n) for f, (dx, dy, dc) in enumerate(flips)]

    def start(ins, outs, ss, rs, base):
        for cp in copies(outs, ss, rs, base):
            cp.start()

    def finish(ins, outs, ss, rs, base):
        for cp in copies(outs, ss, rs, base):
            cp.wait()

    return _Comm(slots, [_out(s.shape, s.dtype) for s in slots], {a: a for a in range(n)}, [(7 * n, start, finish)])


def _small_total(slots, shapes):
    groups, widths, rows, window = _small_layout(shapes)
    n, n_g = len(shapes), len(groups)

    def body(*refs):
        ins, outs, acc = refs[:n_g], refs[n_g:n_g + n], refs[n_g + n:]
        for gi, g in enumerate(groups):
            t = ins[gi][0] + ins[gi][1]
            for d in range(2, 8):
                t = t + ins[gi][d]
            acc[gi][...] = t
            for a in g:
                outs[a][...] = window(acc[gi], a)[...]

    return pl.pallas_call(
        body, name="small_total", grid=(1,), out_shape=[_out(s, F32) for s in shapes],
        in_specs=[_full(s.shape) for s in slots], out_specs=[_full(s) for s in shapes],
        scratch_shapes=[pltpu.VMEM((r, w), F32) for r, w in zip(rows, widths)],
        compiler_params=_cp(("arbitrary",), 48),
    )(*_in_hbm(*slots))


def _small_allreduce(arrays, comm):
    n = len(arrays)
    shapes = [a.shape for a in arrays]
    groups, widths, rows, window = _small_layout(shapes)
    n_g = len(groups)

    def body(*refs):
        ins, outs = refs[:n], refs[n:2 * n]
        pack, sib, csum, every = (refs[2 * n + i * n_g:2 * n + (i + 1) * n_g] for i in range(4))
        send_sems, recv_sems = refs[2 * n + 4 * n_g:]
        x, y, c = _mesh_pos()
        k = 2 * x + y
        for gi, g in enumerate(groups):
            pack[gi][...] = jnp.zeros_like(pack[gi])
            for a in g:
                window(pack[gi], a)[...] = ins[a][...]
        cps = [_remote(pack[gi], sib[gi], send_sems, recv_sems, gi, (x, y, 1 - c)) for gi in range(n_g)]
        for cp in cps:
            cp.start()
        for cp in cps:
            cp.wait()
        for gi in range(n_g):
            csum[gi][...] = pack[gi][...] + sib[gi][...]
            every[gi][k] = csum[gi][...]
        cps = [_remote(csum[gi], every[gi].at[k], send_sems, recv_sems, n_g + 3 * gi + j, (*chip, c))
               for gi in range(n_g) for j, chip in enumerate(_other_chips(x, y))]
        for cp in cps:
            cp.start()
        for cp in cps:
            cp.wait()
        for gi, g in enumerate(groups):
            pack[gi][...] = ((every[gi][0] + every[gi][1]) + every[gi][2]) + every[gi][3]
            for a in g:
                outs[a][...] = window(pack[gi], a)[...]

    bufs = [pltpu.VMEM((r, w), F32) for r, w in zip(rows, widths)]
    return _call(
        body, comm, (0,), arrays, name="small_allreduce", grid=(1,), out_shape=[_out(s, F32) for s in shapes],
        in_specs=[_full(s) for s in shapes], out_specs=[_full(s) for s in shapes],
        scratch_shapes=bufs * 3 + [pltpu.VMEM((N_CHIP, r, w), F32) for r, w in zip(rows, widths)] +
                       [pltpu.SemaphoreType.DMA((4 * n_g,)), pltpu.SemaphoreType.DMA((4 * n_g,))],
        compiler_params=_cp(("arbitrary",), 40))


def _adamw_small(ws, gs, ms, vs, comm):
    n = len(ws)

    def body(*refs):
        w, g, m, v, d, mo, vo = (refs[i * n:(i + 1) * n] for i in range(7))
        for a in range(n):
            d[a][...], mo[a][...], vo[a][...] = _adamw_math(w[a][...], g[a][...], m[a][...], v[a][...])

    specs = [_full(w.shape) for w in ws]
    res, got = _call(
        body, comm, (0,), (*ws, *gs, *ms, *vs), name="adamw_small", grid=(1,),
        out_shape=[_out(w.shape, F32) for w in ws] * 3,
        in_specs=specs * 4, out_specs=specs * 3, compiler_params=_cp(("arbitrary",), 40))
    return (res[:n], res[n:2 * n], res[2 * n:]), got


def _adamw_math(w, g, m, v):
    m = ADAM_B1 * m + (1.0 - ADAM_B1) * g
    v = ADAM_B2 * v + (1.0 - ADAM_B2) * (g * g)
    m_hat = m / (1.0 - ADAM_B1 ** ADAM_STEP)
    v_hat = v / (1.0 - ADAM_B2 ** ADAM_STEP)
    delta = -ADAM_LR * (m_hat / (jnp.sqrt(v_hat) + ADAM_EPS) + ADAM_WD * w)
    return delta, m, v


def _adamw(name, w, g, m, v):
    r, c = w.shape
    tr = max(t for t in range(8, 513, 8) if r % t == 0)

    def body(w_ref, g_ref, m_ref, v_ref, d_ref, mo_ref, vo_ref):
        d_ref[...], mo_ref[...], vo_ref[...] = _adamw_math(w_ref[...], g_ref[...], m_ref[...], v_ref[...])

    return pl.pallas_call(
        body, name=name, grid=(r // tr,), in_specs=[_rows(tr, c)] * 4, out_specs=[_rows(tr, c)] * 3,
        out_shape=[_out((r, c), F32)] * 3, compiler_params=_cp(("arbitrary",), 32),
    )(*_in_hbm(w, g, m, v))


def _as_matrix(name, a):
    if name == "na_rpb":
        return a[0].transpose(1, 0, 2).reshape(N_HEADS * (2 * KH - 1), 2 * KW - 1)
    if name in ("s5_b_re", "s5_b_im"):
        return a.transpose(0, 1, 2, 4, 3).reshape(2 * S5_G * S5_H, S5_P)
    if name in ("s5_c_re", "s5_c_im"):
        return a.reshape(2 * S5_G * S5_H, S5_P)
    if name in ("s5_lam_re", "s5_lam_im"):
        return a.reshape(2 * S5_G, S5_P)
    if name == "s5_log_dt":
        return a.reshape(2, S5_G)
    return a


def _from_matrix(name, m):
    if name == "na_rpb":
        return m.reshape(2 * KH - 1, N_HEADS, 2 * KW - 1).transpose(1, 0, 2)[None]
    if name in ("s5_b_re", "s5_b_im"):
        return m.reshape(1, 2, S5_G, S5_H, S5_P).transpose(0, 1, 2, 4, 3)
    if name in ("s5_c_re", "s5_c_im"):
        return m.reshape(1, 2, S5_G, S5_H, S5_P)
    if name in ("s5_lam_re", "s5_lam_im"):
        return m.reshape(1, 2, S5_G, S5_P)
    if name == "s5_log_dt":
        return m.reshape(1, 2, S5_G)
    return m


WEIGHTS = ["meta_tokens", "ffn1_pre_g", "ffn1_post_g", "ffn1_w_gate", "ffn1_w_up", "ffn1_w_down", "mix_pre_g", "w_in",
           "na_rpb", "s5_lam_re", "s5_lam_im", "s5_log_dt", "s5_b_re", "s5_b_im", "s5_c_re", "s5_c_im", "s5_d",
           "s5_w_glu", "s5_b_glu", "na_out_g", "s5_out_g", "w_out", "mix_post_g", "ffn2_pre_g", "ffn2_post_g",
           "ffn2_w_gate", "ffn2_w_up", "ffn2_w_down", "final_g"]
BIG = ["ffn1_w_gate", "ffn1_w_up", "ffn1_w_down", "w_in", "s5_w_glu", "w_out", "ffn2_w_gate", "ffn2_w_up",
       "ffn2_w_down"]
TRANSPOSED = ["ffn1_w_gate", "ffn1_w_up", "ffn2_w_gate", "ffn2_w_up"]
GAINS = ["ffn1_pre_g", "ffn1_post_g", "mix_pre_g", "s5_d", "s5_b_glu", "na_out_g", "s5_out_g", "mix_post_g",
         "ffn2_pre_g", "ffn2_post_g", "final_g"]
SMALL = [n for n in WEIGHTS if n not in BIG]


def kernel(*args):
    names = ["x"] + WEIGHTS + ["loss_target"] + ["m_" + n for n in WEIGHTS] + ["v_" + n for n in WEIGHTS]
    assert len(args) == len(names)
    given = dict(zip(names, args))
    x_pos, y_pos, c_pos = _mesh_pos()
    k_pos = 2 * x_pos + y_pos
    c_arr = jnp.reshape(c_pos, (1,)).astype(jnp.int32)
    kc_arr = jnp.stack([k_pos, c_pos]).astype(jnp.int32)

    def piece(name, a):
        return a[0].T if name in TRANSPOSED else a[0]

    def unpiece(name, a):
        return a.T[None] if name in TRANSPOSED else a[None]

    placed = BIG + ["meta_tokens"]
    bufs = dict(zip(placed, _own_half_buffers([piece(n, given[n]) for n in BIG] + [given["meta_tokens"]],
                                              [BF16] * len(BIG) + [F32], kc_arr)))

    gains = {n: given[n] for n in GAINS}
    s5 = {n: _as_matrix("s5_" + n, given["s5_" + n])
          for n in ["lam_re", "lam_im", "log_dt", "b_re", "b_im", "c_re", "c_im"]}
    me_arr = jnp.reshape(4 * x_pos + 2 * y_pos + c_pos, (1,)).astype(jnp.int32)
    loss, dh0, pieces, small, late, last, sums = _step(given["x"][0], given["loss_target"][0], bufs, gains, s5,
                                                       given["na_rpb"][0], c_arr, kc_arr, me_arr)
    loss = lax.psum(loss, ("x", "y", "c"))
    n_tok = given["x"].shape[1]
    grad_x = dh0[N_META:N_META + n_tok][None]

    late["meta_tokens"] = dh0[:N_META]
    red, recv3 = _small_allreduce(list(late.values()), _scatter_comm(sums))
    small.update(zip(late, red))
    mc = D // N_CHIP
    small["meta_tokens"] = lax.dynamic_slice_in_dim(small["meta_tokens"], k_pos * mc, mc, 1)
    totals = [_total_sum("total_sum_" + n, s, r, kc_arr) for n, s, r in zip(last, sums, recv3)]
    gs = [small[n] for n in SMALL]
    (d2, m2, v2), done = _adamw_small([_as_matrix(n, given[n]) for n in SMALL], gs,
                                      [_as_matrix(n, given["m_" + n]) for n in SMALL],
                                      [_as_matrix(n, given["v_" + n]) for n in SMALL], _assemble_comm(totals))
    pieces.update(zip(last, done))

    out_g, out_d, out_m, out_v = {}, {}, {}, {}
    for n, g, dd, mm, vv in zip(SMALL, gs, d2, m2, v2):
        out_g[n], out_d[n], out_m[n], out_v[n] = (_from_matrix(n, t) for t in (g, dd, mm, vv))
    for n in BIG:
        g2 = pieces[n]
        d2, m2, v2 = _adamw("adamw_" + n, piece(n, given[n]), g2, piece(n, given["m_" + n]),
                            piece(n, given["v_" + n]))
        out_g[n], out_d[n], out_m[n], out_v[n] = (unpiece(n, t) for t in (g2, d2, m2, v2))
    return (loss, grad_x, *[out_g[n] for n in WEIGHTS], *[out_d[n] for n in WEIGHTS],
            *[out_m[n] for n in WEIGHTS], *[out_v[n] for n in WEIGHTS])
```

```python
import functools
import math

import numpy as np
import jax
import jax.numpy as jnp
from jax import lax
from jax.experimental import pallas as pl
from jax.experimental.pallas import tpu as pltpu

F32 = jnp.float32
BF16 = jnp.bfloat16

D = 1024
N_META = 16
GRID_W = 64
NA_W = 512
S5_W = 512
HEAD_DIM = 64
N_HEADS = 8
KH = 8
KW = 16
S5_G = 32
S5_P = 64
S5_H = 16
N_BUNDLE = 4
FF = 2816
N_CHIP = 4
FC = FF // N_CHIP
EPS = 1e-6
NEG_INF = -1e30
Q_ROWS = 4
K_ROWS = 12
QB = Q_ROWS * GRID_W
KB = K_ROWS * GRID_W
SCAN_CHUNK = 256

ADAM_LR = 0.001
ADAM_B1 = 0.9
ADAM_B2 = 0.999
ADAM_EPS = 1e-08
ADAM_WD = 0.01
ADAM_STEP = 10

NT = (((1,), (1,)), ((), ()))
TN = (((0,), (0,)), ((), ()))
MESH_ID = pl.DeviceIdType.MESH


def _cp(sem=None, vmem_mb=None):
    kw = {}
    if sem is not None:
        kw["dimension_semantics"] = sem
    if vmem_mb is not None:
        kw["vmem_limit_bytes"] = vmem_mb << 20
    return pltpu.CompilerParams(**kw)


def _full(shape):
    n = len(shape)
    return pl.BlockSpec(shape, lambda *_: (0,) * n)


def _rows(tm, w):
    return pl.BlockSpec((tm, w), lambda i: (i, 0))


ANY = pl.BlockSpec(memory_space=pl.ANY)


def _rms(x, g):
    r = lax.rsqrt(jnp.mean(x * x, axis=-1, keepdims=True) + EPS)
    return x * r * g


def _rms_bwd(x, g, dy):
    r = lax.rsqrt(jnp.mean(x * x, axis=-1, keepdims=True) + EPS)
    xh = x * r
    dg = jnp.sum(dy * xh, axis=0, keepdims=True)
    dyg = dy * g
    dx = r * (dyg - xh * jnp.mean(dyg * xh, axis=-1, keepdims=True))
    return dx, dg


def _out(shape, dtype):
    return pltpu.HBM(tuple(shape), dtype)


def _in_hbm(*args):
    return [pltpu.with_memory_space_constraint(a, pltpu.HBM) if jnp.issubdtype(a.dtype, jnp.floating) and a.ndim > 1
            else a for a in args]


def _dot(a, b):
    return jnp.dot(a, b, preferred_element_type=F32)


def _dg(a, b, dims):
    return lax.dot_general(a, b, dims, preferred_element_type=F32)


def _ffn_fwd(name, h, g_pre, g_post, wg, wu, wd, tm, comm=None, bounds=()):
    tp = h.shape[0]
    nt = tp // tm

    def body(h_ref, gp_ref, gq_ref, wg_ref, wu_ref, wd_ref, hn_ref, gate_ref, up_ref, f_ref, xn_s, acc_s):
        c = pl.program_id(1)

        @pl.when(c == 0)
        def _():
            xn_s[...] = _rms(h_ref[...], gp_ref[...]).astype(BF16)
            acc_s[...] = jnp.zeros_like(acc_s)

        xn = xn_s[...]
        gate = _dg(xn, wg_ref[0], NT)
        up = _dg(xn, wu_ref[0], NT)
        gate_ref[0] = gate
        up_ref[0] = up
        act = (gate * jax.nn.sigmoid(gate) * up).astype(BF16)
        acc_s[...] += _dot(act, wd_ref[0])

        @pl.when(c == N_CHIP - 1)
        def _():
            f = acc_s[...]
            f_ref[...] = f
            hn_ref[...] = h_ref[...] + 0.5 * _rms(f, gq_ref[...])

    return _call(
        body, comm, bounds, (h, g_pre, g_post, wg, wu, wd), name=name, grid=(nt, N_CHIP),
        in_specs=[pl.BlockSpec((tm, D), lambda i, c: (i, 0)), _full((1, D)), _full((1, D))] +
                 [pl.BlockSpec((1, FC, D), lambda i, c: (c, 0, 0))] * 3,
        out_specs=[pl.BlockSpec((tm, D), lambda i, c: (i, 0)),
                   pl.BlockSpec((1, tm, FC), lambda i, c: (c, i, 0)),
                   pl.BlockSpec((1, tm, FC), lambda i, c: (c, i, 0)),
                   pl.BlockSpec((tm, D), lambda i, c: (i, 0))],
        out_shape=[_out((tp, D), F32), _out((N_CHIP, tp, FC), F32),
                   _out((N_CHIP, tp, FC), F32), _out((tp, D), F32)],
        scratch_shapes=[pltpu.VMEM((tm, D), BF16), pltpu.VMEM((tm, D), F32)],
        compiler_params=_cp(("arbitrary", "arbitrary"), 48))


def _ffn_bwd(name, h, g_pre, df, gate, up, wg, wu, wd, tm, comm=None, bounds=()):
    tp = h.shape[0]
    nt = tp // tm
    rh = FC // 2

    def body(h_ref, gp_ref, df_ref, gate_ref, up_ref, wg_ref, wu_ref, wd_ref,
             dwg_ref, dwu_ref, dwd_ref, dxn_ref, rg_ref, ru_ref, rd_ref, ag, au, ad, send_sems, recv_sems):
        c = pl.program_id(0)
        i = pl.program_id(1)

        def to_sibling(a, piece):
            x, y, core = _mesh_pos()
            dw_ref, r_ref = ((dwg_ref, rg_ref), (dwu_ref, ru_ref), (dwd_ref, rd_ref))[a]
            return _remote(dw_ref.at[piece, pl.ds((1 - core) * rh, rh), :], r_ref.at[piece], send_sems, recv_sems,
                           3 * piece + a, (x, y, 1 - core))

        @pl.when(i == 0)
        def _():
            ag[...] = jnp.zeros_like(ag)
            au[...] = jnp.zeros_like(au)
            ad[...] = jnp.zeros_like(ad)

        xn = _rms(h_ref[...], gp_ref[...]).astype(BF16)
        dfb = df_ref[...].astype(BF16)
        gt = gate_ref[0]
        u = up_ref[0]
        sg = jax.nn.sigmoid(gt)
        si = gt * sg
        act = (si * u).astype(BF16)
        dact = _dg(dfb, wd_ref[0], NT)
        ad[...] += _dg(act, dfb, TN)
        dgate = (dact * u * (sg * (1.0 + gt * (1.0 - sg)))).astype(BF16)
        dup = (dact * si).astype(BF16)
        ag[...] += _dg(dgate, xn, TN)
        au[...] += _dg(dup, xn, TN)
        dxn_ref[0] = _dot(dgate, wg_ref[0]) + _dot(dup, wu_ref[0])

        @pl.when(i == nt - 1)
        def _():
            pltpu.sync_copy(ag, dwg_ref.at[c])
            pltpu.sync_copy(au, dwu_ref.at[c])
            pltpu.sync_copy(ad, dwd_ref.at[c])
            for a in range(3):
                to_sibling(a, c).start()

        @pl.when((c == N_CHIP - 1) & (i == nt - 1))
        def _():
            for piece in range(N_CHIP):
                for a in range(3):
                    to_sibling(a, piece).wait()

    return _call(
        body, comm, bounds, (h, g_pre, df, gate, up, wg, wu, wd), name=name, grid=(N_CHIP, nt),
        in_specs=[pl.BlockSpec((tm, D), lambda c, i: (i, 0)), _full((1, D)),
                  pl.BlockSpec((tm, D), lambda c, i: (i, 0)),
                  pl.BlockSpec((1, tm, FC), lambda c, i: (c, i, 0)),
                  pl.BlockSpec((1, tm, FC), lambda c, i: (c, i, 0))] +
                 [pl.BlockSpec((1, FC, D), lambda c, i: (c, 0, 0))] * 3,
        out_specs=[ANY, ANY, ANY, pl.BlockSpec((1, tm, D), lambda c, i: (c, i, 0)), ANY, ANY, ANY],
        out_shape=[_out((N_CHIP, FC, D), F32)] * 3 + [_out((N_CHIP, tp, D), F32)] +
                  [_out((N_CHIP, rh, D), F32)] * 3,
        scratch_shapes=[pltpu.VMEM((FC, D), F32)] * 3 +
                       [pltpu.SemaphoreType.DMA((3 * N_CHIP,)), pltpu.SemaphoreType.DMA((3 * N_CHIP,))],
        compiler_params=_cp(("arbitrary", "arbitrary"), 58))


def _ffn_pre_bwd(name, dh, dxn_part, h, g_pre, tm, comm=None, bounds=()):
    tp = h.shape[0]
    nt = tp // tm

    def body(dh_ref, dxn_ref, h_ref, gp_ref, out_ref, dg_ref):
        i = pl.program_id(0)
        dxn = (dxn_ref[0] + dxn_ref[1]) + (dxn_ref[2] + dxn_ref[3])
        dx, dg = _rms_bwd(h_ref[...], gp_ref[...], dxn)
        out_ref[...] = dh_ref[...] + dx

        @pl.when(i == 0)
        def _():
            dg_ref[...] = jnp.zeros_like(dg_ref)

        dg_ref[...] += dg

    return _call(
        body, comm, bounds, (dh, dxn_part, h, g_pre), name=name, grid=(nt,),
        in_specs=[_rows(tm, D), pl.BlockSpec((N_CHIP, tm, D), lambda i: (0, i, 0)), _rows(tm, D), _full((1, D))],
        out_specs=[_rows(tm, D), _full((1, D))],
        out_shape=[_out((tp, D), F32), _out((1, D), F32)],
        compiler_params=_cp(("arbitrary",), 48))


def _mix_in(h, g, w_in, tm):
    tp = h.shape[0]

    def body(h_ref, g_ref, w_ref, q_ref, k_ref, v_ref, u_ref):
        a = _rms(h_ref[...], g_ref[...]).astype(BF16)
        q_ref[...] = _dot(a, w_ref[0]).astype(BF16)
        k_ref[...] = _dot(a, w_ref[1]).astype(BF16)
        v_ref[...] = _dot(a, w_ref[2]).astype(BF16)
        u_ref[...] = _dot(a, w_ref[3])

    return pl.pallas_call(
        body, name="mix_in", grid=(tp // tm,),
        in_specs=[_rows(tm, D), _full((1, D)), _full((N_CHIP, D, NA_W))],
        out_specs=[_rows(tm, NA_W)] * 4,
        out_shape=[_out((tp, NA_W), BF16)] * 3 + [_out((tp, S5_W), F32)],
        compiler_params=_cp(("arbitrary",), 40),
    )(*_in_hbm(h, g, w_in))


def _gelu(x):
    return jax.nn.gelu(x, approximate=True)


def _gelu_grad(x):
    k = math.sqrt(2.0 / math.pi)
    t = jnp.tanh(k * (x + 0.044715 * x * x * x))
    return 0.5 * (1.0 + t) + 0.5 * x * (1.0 - t * t) * k * (1.0 + 3.0 * 0.044715 * x * x)


def _mix_out(o_na, y_pre, h, w_glu, b_glu, g_na, g_s5, w_out, g_post, tm, comm=None, bounds=()):
    tp = h.shape[0]

    def body(ona_ref, yp_ref, h_ref, wglu_ref, bglu_ref, gna_ref, gs5_ref, wout_ref, gpost_ref, hn_ref, mix_ref):
        y = _gelu(yp_ref[...])
        z = _dot(y.astype(BF16), wglu_ref[...]) + bglu_ref[...]
        o_s5 = y * jax.nn.sigmoid(z)
        n1 = _rms(ona_ref[...], gna_ref[...]).astype(BF16)
        n2 = _rms(o_s5, gs5_ref[...]).astype(BF16)
        mix = _dot(n1, wout_ref[0:NA_W, :]) + _dot(n2, wout_ref[NA_W:, :])
        mix_ref[...] = mix
        hn_ref[...] = h_ref[...] + _rms(mix, gpost_ref[...])

    return _call(
        body, comm, bounds, (o_na, y_pre, h, w_glu, b_glu, g_na, g_s5, w_out, g_post), name="mix_out",
        grid=(tp // tm,),
        in_specs=[_rows(tm, NA_W), _rows(tm, S5_W), _rows(tm, D), _full((S5_W, S5_W)), _full((1, S5_W)),
                  _full((1, NA_W)), _full((1, S5_W)), _full((D, D)), _full((1, D))],
        out_specs=[_rows(tm, D), _rows(tm, D)],
        out_shape=[_out((tp, D), F32)] * 2,
        compiler_params=_cp(("arbitrary",), 40))


def _mix_out_bwd(dh, mix, o_na, y_pre, w_glu, b_glu, g_na, g_s5, w_out, g_post, tm):
    tp = dh.shape[0]
    nt = tp // tm

    def body(dh_ref, mix_ref, ona_ref, yp_ref, wglu_ref, bglu_ref, gna_ref, gs5_ref, wout_ref, gpost_ref,
             dona_ref, dyp_ref, dwout_ref, dwglu_ref, dgpost_ref, dgna_ref, dgs5_ref, dbglu_ref, a_out, a_glu):
        i = pl.program_id(0)

        @pl.when(i == 0)
        def _():
            a_out[...] = jnp.zeros_like(a_out)
            a_glu[...] = jnp.zeros_like(a_glu)
            dgpost_ref[...] = jnp.zeros_like(dgpost_ref)
            dgna_ref[...] = jnp.zeros_like(dgna_ref)
            dgs5_ref[...] = jnp.zeros_like(dgs5_ref)
            dbglu_ref[...] = jnp.zeros_like(dbglu_ref)

        dmix, dgpost = _rms_bwd(mix_ref[...], gpost_ref[...], dh_ref[...])
        dgpost_ref[...] += dgpost
        yp = yp_ref[...]
        y = _gelu(yp)
        yb = y.astype(BF16)
        z = _dot(yb, wglu_ref[...]) + bglu_ref[...]
        sg = jax.nn.sigmoid(z)
        o_s5 = y * sg
        o_na = ona_ref[...]
        n1 = _rms(o_na, gna_ref[...]).astype(BF16)
        n2 = _rms(o_s5, gs5_ref[...]).astype(BF16)
        dmb = dmix.astype(BF16)
        a_out[0:NA_W, :] += _dg(n1, dmb, TN)
        a_out[NA_W:, :] += _dg(n2, dmb, TN)
        dn1 = _dg(dmb, wout_ref[0:NA_W, :], NT)
        dn2 = _dg(dmb, wout_ref[NA_W:, :], NT)
        dona, dgna = _rms_bwd(o_na, gna_ref[...], dn1)
        dona_ref[...] = dona
        dgna_ref[...] += dgna
        dos5, dgs5 = _rms_bwd(o_s5, gs5_ref[...], dn2)
        dgs5_ref[...] += dgs5
        dz = dos5 * y * (sg * (1.0 - sg))
        dbglu_ref[...] += jnp.sum(dz, axis=0, keepdims=True)
        dzb = dz.astype(BF16)
        a_glu[...] += _dg(yb, dzb, TN)
        dy = dos5 * sg + _dg(dzb, wglu_ref[...], NT)
        dyp_ref[...] = dy * _gelu_grad(yp)

        @pl.when(i == nt - 1)
        def _():
            pltpu.sync_copy(a_out, dwout_ref)
            pltpu.sync_copy(a_glu, dwglu_ref)

    return pl.pallas_call(
        body, name="mix_out_bwd", grid=(nt,),
        in_specs=[_rows(tm, D), _rows(tm, D), _rows(tm, NA_W), _rows(tm, S5_W), _full((S5_W, S5_W)),
                  _full((1, S5_W)), _full((1, NA_W)), _full((1, S5_W)), _full((D, D)), _full((1, D))],
        out_specs=[_rows(tm, NA_W), _rows(tm, S5_W), ANY, ANY, _full((1, D)), _full((1, NA_W)),
                   _full((1, S5_W)), _full((1, S5_W))],
        out_shape=[_out((tp, NA_W), F32), _out((tp, S5_W), F32),
                   _out((D, D), F32), _out((S5_W, S5_W), F32),
                   _out((1, D), F32), _out((1, NA_W), F32),
                   _out((1, S5_W), F32), _out((1, S5_W), F32)],
        scratch_shapes=[pltpu.VMEM((D, D), F32), pltpu.VMEM((S5_W, S5_W), F32)],
        compiler_params=_cp(("arbitrary",), 48),
    )(*_in_hbm(dh, mix, o_na, y_pre, w_glu, b_glu, g_na, g_s5, w_out, g_post))


def _mix_in_bwd(dq, dk, dv, du, h, g, w_in, dh, f1, g_post1, tm, comm=None, bounds=()):
    tp = h.shape[0]
    nt = tp // tm

    def body(dq_ref, dk_ref, dv_ref, du_ref, h_ref, g_ref, w_ref, dh_ref, f_ref, gq_ref,
             dh1_ref, df_ref, dw_ref, dg_ref, dgq_ref, acc):
        i = pl.program_id(0)

        @pl.when(i == 0)
        def _():
            acc[...] = jnp.zeros_like(acc)
            dg_ref[...] = jnp.zeros_like(dg_ref)
            dgq_ref[...] = jnp.zeros_like(dgq_ref)

        x = h_ref[...]
        a = _rms(x, g_ref[...]).astype(BF16)
        da = jnp.zeros((tm, D), F32)
        for j, r in enumerate((dq_ref, dk_ref, dv_ref, du_ref)):
            dp = r[...].astype(BF16)
            da = da + _dg(dp, w_ref[j], NT)
            acc[j] += _dg(a, dp, TN)
        dx, dg = _rms_bwd(x, g_ref[...], da)
        dh1 = dh_ref[...] + dx
        dh1_ref[...] = dh1
        dg_ref[...] += dg
        df, dgq = _rms_bwd(f_ref[...], gq_ref[...], 0.5 * dh1)
        df_ref[...] = df
        dgq_ref[...] += dgq

        @pl.when(i == nt - 1)
        def _():
            pltpu.sync_copy(acc, dw_ref)

    return _call(
        body, comm, bounds, (dq, dk, dv, du, h, g, w_in, dh, f1, g_post1), name="mix_in_bwd", grid=(nt,),
        in_specs=[_rows(tm, NA_W)] * 4 + [_rows(tm, D), _full((1, D)), _full((N_CHIP, D, NA_W)), _rows(tm, D),
                                         _rows(tm, D), _full((1, D))],
        out_specs=[_rows(tm, D), _rows(tm, D), ANY, _full((1, D)), _full((1, D))],
        out_shape=[_out((tp, D), F32), _out((tp, D), F32),
                   _out((N_CHIP, D, NA_W), F32), _out((1, D), F32),
                   _out((1, D), F32)],
        scratch_shapes=[pltpu.VMEM((N_CHIP, D, NA_W), F32)],
        compiler_params=_cp(("arbitrary",), 48))


def _final_loss(h, g_final, target, f2, g_post2, n_tok, tm):
    tp = h.shape[0]

    def body(h_ref, g_ref, t_ref, f_ref, gq_ref, dh_ref, df_ref, loss_ref, dg_ref, dgq_ref):
        i = pl.program_id(0)

        @pl.when(i == 0)
        def _():
            loss_ref[...] = jnp.zeros_like(loss_ref)
            dg_ref[...] = jnp.zeros_like(dg_ref)
            dgq_ref[...] = jnp.zeros_like(dgq_ref)

        x = h_ref[...]
        y = _rms(x, g_ref[...])
        row = i * tm + lax.broadcasted_iota(jnp.int32, (tm, 1), 0)
        valid = (row >= N_META) & (row < N_META + n_tok)
        e = jnp.where(valid, y - t_ref[...], 0.0)
        loss_ref[...] += 0.5 * jnp.sum(jnp.mean(e * e, axis=-1, keepdims=True), axis=0, keepdims=True)
        dx, dg = _rms_bwd(x, g_ref[...], e * (1.0 / D))
        dh_ref[...] = dx
        dg_ref[...] += dg
        df, dgq = _rms_bwd(f_ref[...], gq_ref[...], 0.5 * dx)
        df_ref[...] = df
        dgq_ref[...] += dgq

    return pl.pallas_call(
        body, name="final_loss", grid=(tp // tm,),
        in_specs=[_rows(tm, D), _full((1, D)), _rows(tm, D), _rows(tm, D), _full((1, D))],
        out_specs=[_rows(tm, D), _rows(tm, D), _full((1, 1)), _full((1, D)), _full((1, D))],
        out_shape=[_out((tp, D), F32), _out((tp, D), F32),
                   _out((1, 1), F32), _out((1, D), F32),
                   _out((1, D), F32)],
        compiler_params=_cp(("arbitrary",), 40),
    )(*_in_hbm(h, g_final, target, f2, g_post2))


def _na_patterns(n_rows):
    pats = []
    for kind in range(3):
        pat = [[-1] * K_ROWS for _ in range(Q_ROWS)]
        for i in range(Q_ROWS):
            for jj in range(K_ROWS):
                if kind == 0 and jj < KH:
                    pat[i][jj] = jj - i + KH - 1
                elif kind == 1 and i <= jj < i + KH:
                    pat[i][jj] = jj - i + 3
                elif kind == 2 and K_ROWS - KH <= jj:
                    pat[i][jj] = jj - i - 1
        pats.append(pat)
    return pats


def _diag_onehot():
    q = np.arange(GRID_W)[:, None]
    kc = np.arange(GRID_W)[None, :]
    start = np.clip(q - KW // 2, 0, GRID_W - KW)
    col_in = (kc >= start) & (kc < start + KW)
    e = np.zeros((32, GRID_W, GRID_W), np.float32)
    for d in range(2 * KW - 1):
        e[d] = ((kc - q + KW - 1) == d) & col_in
    return e.reshape(32, GRID_W * GRID_W), col_in


def _rpb_collapse(dtb2, et):
    def body(d_ref, e_ref, o_ref):
        o_ref[...] = jnp.dot(d_ref[...], e_ref[...], preferred_element_type=F32, precision=lax.Precision.HIGHEST)

    out = (dtb2.shape[0], et.shape[1])
    return pl.pallas_call(
        body, name="rpb_collapse", grid=(1,), out_shape=_out(out, F32),
        in_specs=[_full(dtb2.shape), _full(et.shape)], out_specs=_full(out),
    )(*_in_hbm(dtb2, et))


def _bias_tables(rpb, n_rows, comm=None, bounds=()):
    n_dr, n_dc = 2 * KH - 1, 2 * KW - 1
    pats = _na_patterns(n_rows)

    def body(rpb_ref, o_ref):
        h = pl.program_id(0)
        q = lax.broadcasted_iota(jnp.int32, (GRID_W, GRID_W), 0)
        kc = lax.broadcasted_iota(jnp.int32, (GRID_W, GRID_W), 1)
        start = jnp.clip(q - KW // 2, 0, GRID_W - KW)
        col_in = (kc >= start) & (kc < start + KW)
        diff = kc - q + (KW - 1)
        neg = jnp.full((GRID_W, GRID_W), NEG_INF, F32)
        band = []
        for dr in range(n_dr):
            acc = neg
            for d in range(n_dc):
                acc = jnp.where((diff == d) & col_in, rpb_ref[(h * n_dr + dr) * n_dc + d], acc)
            band.append(acc)
        for kind, pat in enumerate(pats):
            for i in range(Q_ROWS):
                for jj in range(K_ROWS):
                    o_ref[kind, 0, i * GRID_W:(i + 1) * GRID_W, jj * GRID_W:(jj + 1) * GRID_W] = (
                        band[pat[i][jj]] if pat[i][jj] >= 0 else neg)

    (bias,), got = _call(
        body, comm, bounds, (rpb.reshape(-1),), name="bias_tables", grid=(N_HEADS,),
        in_specs=[pl.BlockSpec(memory_space=pltpu.SMEM)],
        out_specs=[pl.BlockSpec((3, 1, QB, KB), lambda h: (0, h, 0, 0))],
        out_shape=[_out((3, N_HEADS, QB, KB), F32)],
        compiler_params=_cp(("arbitrary",), 32))
    return bias, got


def _attn_geometry(n_tok):
    n_rows = n_tok // GRID_W
    assert n_rows % Q_ROWS == 0 and n_rows >= K_ROWS
    return n_rows, n_rows // Q_ROWS


def _attn_probs(qh, kh, kmh, bias, scale):
    s = _dg(qh, kh, NT) * scale + bias
    sm = _dg(qh, kmh, NT) * scale
    m = jnp.maximum(jnp.max(s, axis=-1, keepdims=True), jnp.max(sm, axis=-1, keepdims=True))
    p = jnp.exp(s - m)
    pm = jnp.exp(sm - m)
    inv = 1.0 / (jnp.sum(p, axis=-1, keepdims=True) + jnp.sum(pm, axis=-1, keepdims=True))
    return p * inv, pm * inv


def _meta_probs(qmh, kmh, scale):
    s = _dg(qmh, kmh, NT) * scale
    p = jnp.exp(s - jnp.max(s, axis=-1, keepdims=True))
    return p / jnp.sum(p, axis=-1, keepdims=True)


def _step_rows(r, n_rows):
    q0 = pl.multiple_of(N_META + r * QB, 16)
    k0 = pl.multiple_of(N_META + jnp.clip(Q_ROWS * r - (K_ROWS - KH), 0, n_rows - K_ROWS) * GRID_W, 16)
    return q0, k0


def _attn_fwd(q, k, v, bias, n_tok, comm=None, bounds=()):
    tp = q.shape[0]
    n_rows, n_steps = _attn_geometry(n_tok)
    scale = HEAD_DIM ** -0.5

    def body(q_ref, k_ref, v_ref, b_ref, o_ref):
        r = pl.program_id(1)
        km = k_ref[0:N_META, :]
        vm = v_ref[0:N_META, :]

        @pl.when(r == 0)
        def _():
            qm = q_ref[0:N_META, :]
            outs = []
            for hh in range(2):
                sl = slice(hh * HEAD_DIM, (hh + 1) * HEAD_DIM)
                p = _meta_probs(qm[:, sl], km[:, sl], scale)
                outs.append(_dot(p.astype(BF16), vm[:, sl]))
            o_ref[0:N_META, :] = jnp.concatenate(outs, axis=1)
            o_ref[N_META + n_tok:, :] = jnp.zeros((tp - N_META - n_tok, 2 * HEAD_DIM), F32)

        q0, k0 = _step_rows(r, n_rows)
        qb = q_ref[pl.ds(q0, QB), :]
        kb = k_ref[pl.ds(k0, KB), :]
        vb = v_ref[pl.ds(k0, KB), :]
        outs = []
        for hh in range(2):
            sl = slice(hh * HEAD_DIM, (hh + 1) * HEAD_DIM)
            p, pm = _attn_probs(qb[:, sl], kb[:, sl], km[:, sl], b_ref[0, hh], scale)
            outs.append(_dot(p.astype(BF16), vb[:, sl]) + _dot(pm.astype(BF16), vm[:, sl]))
        o_ref[pl.ds(q0, QB), :] = jnp.concatenate(outs, axis=1)

    def bias_map(hp, r):
        return (jnp.where(r == 0, 0, jnp.where(r == n_steps - 1, 2, 1)), hp, 0, 0)

    col = pl.BlockSpec((tp, 2 * HEAD_DIM), lambda hp, r: (0, hp))
    return _call(
        body, comm, bounds, (q, k, v, bias), name="attn_fwd", grid=(N_HEADS // 2, n_steps),
        in_specs=[col, col, col, pl.BlockSpec((1, 2, QB, KB), bias_map)],
        out_specs=[col], out_shape=[_out((tp, NA_W), F32)],
        compiler_params=_cp(("arbitrary", "arbitrary"), 40))


def _attn_bwd(q, k, v, bias, do, n_tok, comm=None, bounds=()):
    tp = q.shape[0]
    n_rows, n_steps = _attn_geometry(n_tok)
    scale = HEAD_DIM ** -0.5
    pats = _na_patterns(n_rows)

    def body(q_ref, k_ref, v_ref, b_ref, do_ref, dq_ref, dk_ref, dv_ref, dtb_ref):
        r = pl.program_id(1)
        km = k_ref[0:N_META, :]
        vm = v_ref[0:N_META, :]

        @pl.when(r == 0)
        def _():
            dk_ref[...] = jnp.zeros_like(dk_ref)
            dv_ref[...] = jnp.zeros_like(dv_ref)
            dtb_ref[...] = jnp.zeros_like(dtb_ref)
            dq_ref[N_META + n_tok:, :] = jnp.zeros((tp - N_META - n_tok, 2 * HEAD_DIM), F32)
            qm = q_ref[0:N_META, :]
            dom = do_ref[0:N_META, :].astype(BF16)
            dqs, dks, dvs = [], [], []
            for hh in range(2):
                sl = slice(hh * HEAD_DIM, (hh + 1) * HEAD_DIM)
                p = _meta_probs(qm[:, sl], km[:, sl], scale)
                dp = _dg(dom[:, sl], vm[:, sl], NT)
                ds = (p * (dp - jnp.sum(dp * p, axis=-1, keepdims=True))).astype(BF16)
                dvs.append(_dg(p.astype(BF16), dom[:, sl], TN))
                dqs.append(_dot(ds, km[:, sl]) * scale)
                dks.append(_dg(ds, qm[:, sl], TN) * scale)
            dq_ref[0:N_META, :] = jnp.concatenate(dqs, axis=1)
            dk_ref[0:N_META, :] += jnp.concatenate(dks, axis=1)
            dv_ref[0:N_META, :] += jnp.concatenate(dvs, axis=1)

        q0, k0 = _step_rows(r, n_rows)
        qb = q_ref[pl.ds(q0, QB), :]
        kb = k_ref[pl.ds(k0, KB), :]
        vb = v_ref[pl.ds(k0, KB), :]
        dob = do_ref[pl.ds(q0, QB), :].astype(BF16)
        dqs, dks, dvs, dkms, dvms, dss = [], [], [], [], [], []
        for hh in range(2):
            sl = slice(hh * HEAD_DIM, (hh + 1) * HEAD_DIM)
            qh, kh, vh, kmh, vmh, doh = qb[:, sl], kb[:, sl], vb[:, sl], km[:, sl], vm[:, sl], dob[:, sl]
            p, pm = _attn_probs(qh, kh, kmh, b_ref[0, hh], scale)
            dp = _dg(doh, vh, NT)
            dpm = _dg(doh, vmh, NT)
            delta = jnp.sum(dp * p, axis=-1, keepdims=True) + jnp.sum(dpm * pm, axis=-1, keepdims=True)
            ds = p * (dp - delta)
            dsb = ds.astype(BF16)
            dsmb = (pm * (dpm - delta)).astype(BF16)
            dss.append(ds)
            dvs.append(_dg(p.astype(BF16), doh, TN))
            dvms.append(_dg(pm.astype(BF16), doh, TN))
            dqs.append((_dot(dsb, kh) + _dot(dsmb, kmh)) * scale)
            dks.append(_dg(dsb, qh, TN) * scale)
            dkms.append(_dg(dsmb, qh, TN) * scale)
        dq_ref[pl.ds(q0, QB), :] = jnp.concatenate(dqs, axis=1)
        dk_ref[pl.ds(k0, KB), :] += jnp.concatenate(dks, axis=1)
        dv_ref[pl.ds(k0, KB), :] += jnp.concatenate(dvs, axis=1)
        dk_ref[0:N_META, :] += jnp.concatenate(dkms, axis=1)
        dv_ref[0:N_META, :] += jnp.concatenate(dvms, axis=1)

        def add_bias_grad(pat):
            for hh in range(2):
                for i in range(Q_ROWS):
                    for jj in range(K_ROWS):
                        if pat[i][jj] >= 0:
                            dtb_ref[hh, pat[i][jj]] += dss[hh][i * GRID_W:(i + 1) * GRID_W,
                                                               jj * GRID_W:(jj + 1) * GRID_W]

        @pl.when(r == 0)
        def _():
            add_bias_grad(pats[0])

        @pl.when((r > 0) & (r < n_steps - 1))
        def _():
            add_bias_grad(pats[1])

        @pl.when(r == n_steps - 1)
        def _():
            add_bias_grad(pats[2])

    def bias_map(hp, r):
        return (jnp.where(r == 0, 0, jnp.where(r == n_steps - 1, 2, 1)), hp, 0, 0)

    col = pl.BlockSpec((tp, 2 * HEAD_DIM), lambda hp, r: (0, hp))
    n_dr = 2 * KH - 1
    return _call(
        body, comm, bounds, (q, k, v, bias, do), name="attn_bwd", grid=(N_HEADS // 2, n_steps),
        in_specs=[col, col, col, pl.BlockSpec((1, 2, QB, KB), bias_map), col],
        out_specs=[col, col, col, pl.BlockSpec((2, n_dr, GRID_W, GRID_W), lambda hp, r: (hp, 0, 0, 0))],
        out_shape=[_out((tp, NA_W), F32)] * 3 +
                  [_out((N_HEADS, n_dr, GRID_W, GRID_W), F32)],
        compiler_params=_cp(("arbitrary", "arbitrary"), 48))


def _repeat_onehot():
    return np.repeat(np.eye(2 * S5_G, dtype=np.float32), S5_H, axis=0)


def _s5_disc_math(lam_re, lam_im, log_dt, b_re, b_im, rep):
    dt = jnp.exp(log_dt)
    ea = jnp.exp(lam_re * dt)
    a_re = ea * jnp.cos(lam_im * dt)
    a_im = ea * jnp.sin(lam_im * dt)
    den = lam_re * lam_re + lam_im * lam_im
    c_re = ((a_re - 1.0) * lam_re + a_im * lam_im) / den
    c_im = (a_im * lam_re - (a_re - 1.0) * lam_im) / den
    ce_re = jnp.dot(rep, c_re, preferred_element_type=F32, precision=lax.Precision.HIGHEST)
    ce_im = jnp.dot(rep, c_im, preferred_element_type=F32, precision=lax.Precision.HIGHEST)
    return a_re, a_im, ce_re * b_re - ce_im * b_im, ce_re * b_im + ce_im * b_re


def _s5_blocks():
    gl = S5_G // N_BUNDLE
    half = gl * S5_P
    out = []
    for d in range(2):
        for g in range(S5_G):
            b, k = divmod(g, gl)
            dg = d * S5_G + g
            out.append((d, b, slice(k * S5_H, (k + 1) * S5_H), slice(k * S5_P, (k + 1) * S5_P),
                        slice(half + k * S5_P, half + (k + 1) * S5_P), slice(dg * S5_H, (dg + 1) * S5_H),
                        slice(dg, dg + 1)))
    return out


def _s5_params(lam_re, lam_im, log_dt, b_re, b_im, c_re, c_im):
    cw, sw = S5_W // N_BUNDLE, 2 * (S5_G // N_BUNDLE) * S5_P

    def body(lr, li, ld, br, bi, cr, ci, rep_ref, a_ref, a1_ref, a2_ref, bm_ref, cm_ref):
        a_re, a_im, bb_re, bb_im = _s5_disc_math(lr[...], li[...], ld[...], br[...], bi[...], rep_ref[...])
        cc_re = cr[...]
        cc_im = ci[...]
        bm_ref[...] = jnp.zeros_like(bm_ref)
        cm_ref[...] = jnp.zeros_like(cm_ref)
        for d, b, rows, re, im, nat, one in _s5_blocks():
            bm_ref[d, b, rows, re] = bb_re[nat, :].astype(BF16)
            bm_ref[d, b, rows, im] = bb_im[nat, :].astype(BF16)
            cm_ref[d, b, rows, re] = cc_re[nat, :].astype(BF16)
            cm_ref[d, b, rows, im] = (-cc_im[nat, :]).astype(BF16)
            a_ref[d, b, :, re] = a_re[one, :]
            a_ref[d, b, :, im] = a_im[one, :]
            k = rows.start // S5_H
            lanes = slice((k % 2) * S5_P, (k % 2 + 1) * S5_P)
            for part, (v1, v2) in enumerate(((a_re[one, :], a_im[one, :]), (a_re[one, :], -a_im[one, :]))):
                sub = slice(4 * part + k // 2, 4 * part + k // 2 + 1)
                a1_ref[d, b, sub, lanes] = v1
                a2_ref[d, b, sub, lanes] = v2

    args = (lam_re, lam_im, log_dt, b_re, b_im, c_re, c_im, jnp.asarray(_repeat_onehot()))
    outs = [((2, N_BUNDLE, 1, sw), F32)] + [((2, N_BUNDLE, 8, 128), F32)] * 2 + [((2, N_BUNDLE, cw, sw), BF16)] * 2
    return pl.pallas_call(
        body, name="s5_params", grid=(1,), in_specs=[_full(a.shape) for a in args],
        out_specs=[_full(s) for s, _ in outs], out_shape=[_out(s, dt) for s, dt in outs],
    )(*_in_hbm(*args))


def _s5_params_bwd(lam_re, lam_im, log_dt, b_re, b_im, da, dbm, dcm):
    n, nb = 2 * S5_G, 2 * S5_G * S5_H

    def body(lr, li, ld, br, bi, rep_ref, da_ref, dbm_ref, dcm_ref, o_lr, o_li, o_ld, o_br, o_bi, o_cr, o_ci,
             dar_s, dai_s, dbr_s, dbi_s):
        for d, b, rows, re, im, nat, one in _s5_blocks():
            dbr_s[nat, :] = dbm_ref[d, b, rows, re]
            dbi_s[nat, :] = dbm_ref[d, b, rows, im]
            o_cr[nat, :] = dcm_ref[d, b, rows, re]
            o_ci[nat, :] = -dcm_ref[d, b, rows, im]
            dar_s[one, :] = da_ref[d, b, :, re]
            dai_s[one, :] = da_ref[d, b, :, im]
        rep = rep_ref[...]
        _, vjp = jax.vjp(lambda p, q, r, s, t: _s5_disc_math(p, q, r, s, t, rep),
                         lr[...], li[...], ld[...], br[...], bi[...])
        o_lr[...], o_li[...], o_ld[...], o_br[...], o_bi[...] = vjp((dar_s[...], dai_s[...], dbr_s[...], dbi_s[...]))

    args = (lam_re, lam_im, log_dt, b_re, b_im, jnp.asarray(_repeat_onehot()), da, dbm, dcm)
    outs = [(n, S5_P)] * 2 + [(n, 1)] + [(nb, S5_P)] * 4
    return pl.pallas_call(
        body, name="s5_params_bwd", grid=(1,), in_specs=[_full(a.shape) for a in args],
        out_specs=[_full(s) for s in outs], out_shape=[_out(s, F32) for s in outs],
        scratch_shapes=[pltpu.VMEM((n, S5_P), F32)] * 2 + [pltpu.VMEM((nb, S5_P), F32)] * 2,
    )(*_in_hbm(*args))


def _tiles_store(ref, base, val):
    for i in range(val.shape[0] // 8):
        for c in range(8):
            ref[pl.ds(base + (8 * i + c) * 8, 8), :] = val[8 * i:8 * i + 8, 128 * c:128 * (c + 1)]


def _tiles_load(ref, base, n):
    return jnp.concatenate(
        [jnp.concatenate([ref[pl.ds(base + (8 * i + c) * 8, 8), :] for c in range(8)], axis=1) for i in range(n // 8)],
        axis=0)


def _time_rows(base, t):
    return pl.ds(base + (t // 8) * 64 + t % 8, 8, stride=8)


def _scan(chains, n):
    xs = [c["x"] for c in chains]
    for k in range(n):
        for ci, c in enumerate(chains):
            t = n - 1 - k if c["reverse"] else k
            if c["prev"] is not None:
                c["prev"][_time_rows(c["prev_base"], t), :] = xs[ci]
            xs[ci] = c["a1"] * xs[ci] + pltpu.roll(c["a2"] * xs[ci], 4, axis=0) + c["src"][_time_rows(0, t), :]
            if c["dst"] is not None:
                c["dst"][_time_rows(0, t), :] = xs[ci]
    return xs


def _chain(x, a1, a2, src, dst=None, prev=None, prev_base=0, reverse=False):
    return dict(x=x, a1=a1, a2=a2, src=src, dst=dst, prev=prev, prev_base=prev_base, reverse=reverse)


def _s5_fwd(u, d_skip, a1, a2, bm, cm, length, comm=None, bounds=()):
    tp = u.shape[0]
    cw = S5_W // N_BUNDLE
    sw = bm.shape[-1]
    n_full, n_tail = divmod(length, SCAN_CHUNK)
    t_tail = n_full * SCAN_CHUNK

    nbs = N_BUNDLE

    def body(u_ref, d_ref, a1_ref, a2_ref, bm_ref, cm_ref, y_ref, bnd_ref, *scratch):
        y_ref[...] = u_ref[...] * d_ref[...]
        ins, xss = (scratch[0:nbs], scratch[nbs:2 * nbs]), (scratch[2 * nbs:3 * nbs], scratch[3 * nbs:])
        cols = [slice(b * cw, (b + 1) * cw) for b in range(nbs)]

        def keep(dr, chunk, xs):
            for b in range(nbs):
                bnd_ref[dr, b, chunk] = xs[b]

        def load(dr, t0, n):
            for b in range(nbs):
                _tiles_store(ins[dr][b], 0, _dot(u_ref[pl.ds(t0, n), cols[b]].astype(BF16), bm_ref[dr, b]))

        def chains(dr, xs):
            return [_chain(xs[b], a1_ref[dr, b], a2_ref[dr, b], ins[dr][b], dst=xss[dr][b], reverse=dr == 1)
                    for b in range(nbs)]

        def emit(dr, t0, n):
            for b in range(nbs):
                y_ref[pl.ds(t0, n), cols[b]] += _dg(_tiles_load(xss[dr][b], 0, n).astype(BF16), cm_ref[dr, b], NT)

        zero = (jnp.zeros((8, 128), F32),) * nbs
        xb = zero
        if n_tail:
            keep(1, n_full, xb)
            load(1, t_tail, n_tail)
            xb = tuple(_scan(chains(1, xb), n_tail))
            emit(1, t_tail, n_tail)

        def pair(i, carry):
            j = n_full - 1 - i
            t0s = (pl.multiple_of(i * SCAN_CHUNK, SCAN_CHUNK), pl.multiple_of(j * SCAN_CHUNK, SCAN_CHUNK))
            keep(0, i, carry[0])
            keep(1, j, carry[1])
            for dr in range(2):
                load(dr, t0s[dr], SCAN_CHUNK)
            out = _scan(chains(0, carry[0]) + chains(1, carry[1]), SCAN_CHUNK)
            for dr in range(2):
                emit(dr, t0s[dr], SCAN_CHUNK)
            return tuple(out[:nbs]), tuple(out[nbs:])

        xf, _ = lax.fori_loop(0, n_full, pair, (zero, xb))
        if n_tail:
            keep(0, n_full, xf)
            load(0, t_tail, n_tail)
            _scan(chains(0, xf), n_tail)
            emit(0, t_tail, n_tail)

    n_chunks = n_full + (1 if n_tail else 0)
    tile = pl.BlockSpec((2, nbs, 8, 128), lambda b: (0, b, 0, 0))
    return _call(
        body, comm, bounds, (u, d_skip, a1, a2, bm, cm), name="s5_fwd", grid=(N_BUNDLE // nbs,),
        in_specs=[pl.BlockSpec((tp, nbs * cw), lambda b: (0, b)), pl.BlockSpec((1, nbs * cw), lambda b: (0, b)),
                  tile, tile, pl.BlockSpec((2, nbs, cw, sw), lambda b: (0, b, 0, 0)),
                  pl.BlockSpec((2, nbs, cw, sw), lambda b: (0, b, 0, 0))],
        out_specs=[pl.BlockSpec((tp, nbs * cw), lambda b: (0, b)),
                   pl.BlockSpec((2, nbs, n_chunks, 8, 128), lambda b: (0, b, 0, 0, 0))],
        out_shape=[_out((tp, S5_W), F32), _out((2, N_BUNDLE, n_chunks, 8, 128), F32)],
        scratch_shapes=[pltpu.VMEM((SCAN_CHUNK * 8, 128), F32)] * (4 * nbs),
        compiler_params=_cp(("arbitrary",), 48))


def _s5_bwd(u, dy, d_skip, a, a1, a2, bm, cm, bnd, length):
    tp = u.shape[0]
    cw = S5_W // N_BUNDLE
    sw = bm.shape[-1]
    half = sw // 2
    n_full, n_tail = divmod(length, SCAN_CHUNK)
    t_tail = n_full * SCAN_CHUNK
    n_chunks = bnd.shape[2]
    nbs = 2

    def body(u_ref, dy_ref, d_ref, a_ref, a1_ref, a2_ref, bm_ref, cm_ref, bnd_ref, du_ref, dd_ref, dbm_ref, dcm_ref,
             da_ref, *scratch):
        du_ref[...] = dy_ref[...] * d_ref[...]
        dd_ref[...] = jnp.sum(dy_ref[...] * u_ref[...], axis=0, keepdims=True)
        dbm_ref[...] = jnp.zeros_like(dbm_ref)
        dcm_ref[...] = jnp.zeros_like(dcm_ref)
        da_ref[...] = jnp.zeros_like(da_ref)
        bu_s, dx_s, g_s, xp_s = ([scratch[(k * 2 + dr) * nbs:(k * 2 + dr + 1) * nbs] for dr in range(2)] for k in range(4))
        cols = [slice(b * cw, (b + 1) * cw) for b in range(nbs)]

        def chains(dr, chunk, t0, n, gs):
            out = []
            for b in range(nbs):
                _tiles_store(bu_s[dr][b], 0, _dot(u_ref[pl.ds(t0, n), cols[b]].astype(BF16), bm_ref[dr, b]))
                _tiles_store(dx_s[dr][b], 0, _dot(dy_ref[pl.ds(t0, n), cols[b]].astype(BF16), cm_ref[dr, b]))
                out.append(_chain(bnd_ref[dr, b, chunk], a1_ref[dr, b], a2_ref[dr, b], bu_s[dr][b],
                                  prev=xp_s[dr][b], reverse=dr == 1))
                out.append(_chain(gs[b], a1_ref[dr, b], -a2_ref[dr, b], dx_s[dr][b], dst=g_s[dr][b], reverse=dr == 0))
            return out

        def emit(dr, t0, n):
            rows = pl.ds(t0, n)
            for b in range(nbs):
                ub = u_ref[rows, cols[b]].astype(BF16)
                dyb = dy_ref[rows, cols[b]].astype(BF16)
                g = _tiles_load(g_s[dr][b], 0, n)
                gb = g.astype(BF16)
                du_ref[rows, cols[b]] += _dg(gb, bm_ref[dr, b], NT)
                dbm_ref[dr, b] += _dg(ub, gb, TN)
                xp = _tiles_load(xp_s[dr][b], 0, n)
                xp_r, xp_i = xp[:, 0:half], xp[:, half:]
                g_r, g_i = g[:, 0:half], g[:, half:]
                a_re = a_ref[dr, b, :, 0:half]
                a_im = a_ref[dr, b, :, half:]
                bu = _dot(ub, bm_ref[dr, b])
                x_r = a_re * xp_r - a_im * xp_i + bu[:, 0:half]
                x_i = a_re * xp_i + a_im * xp_r + bu[:, half:]
                dcm_ref[dr, b] += _dg(dyb, jnp.concatenate([x_r, x_i], axis=1).astype(BF16), TN)
                da_ref[dr, b] += jnp.concatenate([jnp.sum(g_r * xp_r + g_i * xp_i, axis=0, keepdims=True),
                                                  jnp.sum(g_i * xp_r - g_r * xp_i, axis=0, keepdims=True)], axis=1)

        def adjoints(out):
            return tuple(out[1::2])

        zero = (jnp.zeros((8, 128), F32),) * nbs
        g0 = zero
        if n_tail:
            g0 = adjoints(_scan(chains(0, n_full, t_tail, n_tail, g0), n_tail))
            emit(0, t_tail, n_tail)

        def pair(i, carry):
            j = n_full - 1 - i
            t0 = (pl.multiple_of(j * SCAN_CHUNK, SCAN_CHUNK), pl.multiple_of(i * SCAN_CHUNK, SCAN_CHUNK))
            both = chains(0, j, t0[0], SCAN_CHUNK, carry[0]) + chains(1, i, t0[1], SCAN_CHUNK, carry[1])
            out = _scan(both, SCAN_CHUNK)
            emit(0, t0[0], SCAN_CHUNK)
            emit(1, t0[1], SCAN_CHUNK)
            return adjoints(out[:2 * nbs]), adjoints(out[2 * nbs:])

        _, g1 = lax.fori_loop(0, n_full, pair, (g0, zero))
        if n_tail:
            _scan(chains(1, n_full, t_tail, n_tail, g1), n_tail)
            emit(1, t_tail, n_tail)

    tile = pl.BlockSpec((2, nbs, 8, 128), lambda b: (0, b, 0, 0))
    wide = pl.BlockSpec((2, nbs, cw, sw), lambda b: (0, b, 0, 0))
    col = pl.BlockSpec((tp, nbs * cw), lambda b: (0, b))
    row = pl.BlockSpec((1, nbs * cw), lambda b: (0, b))
    arow = pl.BlockSpec((2, nbs, 1, sw), lambda b: (0, b, 0, 0))
    return pl.pallas_call(
        body, name="s5_bwd", grid=(N_BUNDLE // nbs,),
        in_specs=[col, col, row, arow, tile, tile, wide, wide,
                  pl.BlockSpec((2, nbs, n_chunks, 8, 128), lambda b: (0, b, 0, 0, 0))],
        out_specs=[col, row, wide, wide, arow],
        out_shape=[_out((tp, S5_W), F32), _out((1, S5_W), F32),
                   _out((2, N_BUNDLE, cw, sw), F32), _out((2, N_BUNDLE, cw, sw), F32),
                   _out((2, N_BUNDLE, 1, sw), F32)],
        scratch_shapes=[pltpu.VMEM((SCAN_CHUNK * 8, 128), F32)] * (8 * nbs),
        compiler_params=_cp(("arbitrary",), 56),
    )(*_in_hbm(u, dy, d_skip, a, a1, a2, bm, cm, bnd))


def _row_tile(tp):
    return max(tm for tm in range(16, 449, 16) if tp % tm == 0)


def _step(x, target, bufs, gains, s5, rpb, c_arr, kc_arr, me_arr):
    n_tok = x.shape[0]
    first = ["ffn1_w_gate", "ffn1_w_up", "ffn1_w_down", "meta_tokens"]
    bias, got = _bias_tables(rpb, n_tok // GRID_W, _gather_comm([bufs[n] for n in first]), (0, N_HEADS - 1))
    w = dict(zip(first, got))
    meta = w["meta_tokens"].transpose(1, 0, 2).reshape(N_META, D)
    length = N_META + n_tok
    tp = length + 16
    tm = _row_tile(tp)
    tmb = tm
    n_rows = n_tok // GRID_W
    pad = jnp.zeros((tp - length, D), F32)
    h0 = jnp.concatenate([meta, x, pad], axis=0)
    tgt = jnp.concatenate([jnp.zeros((N_META, D), F32), target, pad], axis=0)

    s5p = (s5["lam_re"], s5["lam_im"], s5["log_dt"].reshape(2 * S5_G, 1), s5["b_re"], s5["b_im"])
    a_m, a1_m, a2_m, bm16, cm16 = _s5_params(*s5p, s5["c_re"], s5["c_im"])

    mid = ["w_in", "s5_w_glu", "w_out"]
    (h1, gate1, up1, f1), got = _ffn_fwd(
        "ffn1_fwd", h0, gains["ffn1_pre_g"], gains["ffn1_post_g"], w["ffn1_w_gate"], w["ffn1_w_up"], w["ffn1_w_down"],
        tm, _gather_comm([bufs[n] for n in mid]), (0, (tp // tm) * N_CHIP * 3 // 5))
    w.update(zip(mid, got))
    q, k, v, u = _mix_in(h1, gains["mix_pre_g"], w["w_in"], tm)
    (o_na,), (gate_ici, up_ici) = _attn_fwd(
        q, k, v, bias, n_tok, _gather_comm([bufs["ffn2_w_gate"], bufs["ffn2_w_up"]], pair=False), (0,))
    (y_pre, s5_bnd), (w["ffn2_w_gate"], w["ffn2_w_up"], down_ici) = _s5_fwd(
        u, gains["s5_d"], a1_m, a2_m, bm16, cm16, length,
        _merge_comm(_gather_comm([gate_ici, up_ici], ici=False),
                    _gather_comm([bufs["ffn2_w_down"]], pair=False)), (0,))
    w_glu = w["s5_w_glu"].reshape(S5_W, S5_W)
    w_out = w["w_out"].reshape(D, D)
    (h2, mix), (w["ffn2_w_down"],) = _mix_out(
        o_na, y_pre, h1, w_glu, gains["s5_b_glu"], gains["na_out_g"], gains["s5_out_g"], w_out, gains["mix_post_g"], tm,
        _gather_comm([down_ici], ici=False), (0,))
    (h3, gate2, up2, f2), _ = _ffn_fwd("ffn2_fwd", h2, gains["ffn2_pre_g"], gains["ffn2_post_g"],
                                       w["ffn2_w_gate"], w["ffn2_w_up"], w["ffn2_w_down"], tm)
    dh3, df2, loss, dg_final, dg_post2 = _final_loss(h3, gains["final_g"], tgt, f2, gains["ffn2_post_g"], n_tok, tm)

    ffn2 = ["ffn2_w_gate", "ffn2_w_up", "ffn2_w_down"]
    ffn1 = ["ffn1_w_gate", "ffn1_w_up", "ffn1_w_down"]
    out2, _ = _ffn_bwd("ffn2_bwd", h2, gains["ffn2_pre_g"], df2, gate2, up2,
                       w["ffn2_w_gate"], w["ffn2_w_up"], w["ffn2_w_down"], tmb)
    dxn2 = out2[3]
    sums2 = [_chip_sum("chip_sum_" + n, g, r, c_arr) for n, g, r in zip(ffn2, out2[0:3], out2[4:7])]
    (dh2, dg_pre2), _ = _ffn_pre_bwd("ffn2_pre_bwd", dh3, dxn2, h2, gains["ffn2_pre_g"], tm)
    do_na, dy_pre, dw_out, dw_glu, dg_mpost, dg_na, dg_s5, db_glu = _mix_out_bwd(
        dh2, mix, o_na, y_pre, w_glu, gains["s5_b_glu"], gains["na_out_g"], gains["s5_out_g"], w_out,
        gains["mix_post_g"], tm)
    (dq, dk, dv, dtb), recv3 = _attn_bwd(q, k, v, bias, do_na, n_tok, _scatter_comm(sums2), (0,))
    totals2 = [_total_sum("total_sum_" + n, s, r, kc_arr) for n, s, r in zip(ffn2, sums2, recv3)]
    du, dd, dbm, dcm, da_m = _s5_bwd(u, dy_pre, gains["s5_d"], a_m, a1_m, a2_m, bm16, cm16, s5_bnd, length)
    (dh1, df1, dw_in, dg_mpre, dg_post1), done2 = _mix_in_bwd(
        dq, dk, dv, du, h1, gains["mix_pre_g"], w["w_in"], dh2, f1, gains["ffn1_post_g"], tm,
        _assemble_comm(totals2), (0,))
    pieces = dict(zip(ffn2, done2))

    e, _ = _diag_onehot()
    n_dr = 2 * KH - 1
    drpb = _rpb_collapse(dtb.reshape(N_HEADS * n_dr, GRID_W * GRID_W), jnp.asarray(e.T))
    drpb = drpb[:, :2 * KW - 1].reshape(N_HEADS, n_dr, 2 * KW - 1).transpose(1, 0, 2).reshape(N_HEADS * n_dr, 2 * KW - 1)
    dlam_re, dlam_im, dlog_dt, db_re, db_im, dc_re, dc_im = _s5_params_bwd(*s5p, da_m, dbm, dcm)
    early = {"ffn1_post_g": dg_post1, "mix_pre_g": dg_mpre, "na_rpb": drpb,
             "s5_lam_re": dlam_re, "s5_lam_im": dlam_im, "s5_log_dt": dlog_dt.reshape(2, S5_G),
             "s5_b_re": db_re, "s5_b_im": db_im, "s5_c_re": dc_re, "s5_c_im": dc_im,
             "s5_d": dd, "s5_b_glu": db_glu, "na_out_g": dg_na,
             "s5_out_g": dg_s5, "mix_post_g": dg_mpost, "ffn2_pre_g": dg_pre2, "ffn2_post_g": dg_post2,
             "final_g": dg_final}
    names = list(early)
    slots = _small_pack([early[n] for n in names], me_arr)

    out1, slots = _ffn_bwd("ffn1_bwd", h0, gains["ffn1_pre_g"], df1, gate1, up1,
                           w["ffn1_w_gate"], w["ffn1_w_up"], w["ffn1_w_down"], tmb, _spread_comm(slots), (0,))
    small = dict(zip(names, _small_total(slots, [early[n].shape for n in names])))
    sums1 = [_chip_sum("chip_sum_" + n, g, r, c_arr) for n, g, r in zip(ffn1, out1[0:3], out1[4:7])]
    flight = _scatter_start(sums1)
    token = flight[4]
    rest = [dw_in, dw_glu.reshape(N_CHIP, S5_W // N_CHIP, S5_W), dw_out.reshape(N_CHIP, D // N_CHIP, D)]
    (dh0, dg_pre1), recv_rest = _ffn_pre_bwd("ffn1_pre_bwd", dh1, out1[3], h0, gains["ffn1_pre_g"] + token[0:1, 0:1],
                                             tm, _exchange_comm(rest), (0,))
    sums = [_chip_sum("chip_sum_" + n, g, r, c_arr) for n, g, r in zip(mid, rest, recv_rest)]
    return loss[0, 0], dh0, pieces, small, {"ffn1_pre_g": dg_pre1}, (ffn1, flight[:4]), (mid, sums)


def _mesh_pos():
    return lax.axis_index("x"), lax.axis_index("y"), lax.axis_index("c")


def _other_chips(x, y):
    return [(1 - x, y), (x, 1 - y), (1 - x, 1 - y)]


class _Comm:
    def __init__(self, ins, out_shape, aliases, parts):
        self.ins, self.out_shape, self.aliases, self.parts = list(ins), list(out_shape), dict(aliases), list(parts)
        self.n_sems = sum(p[0] for p in parts)

    def bases(self):
        out, base = [], 0
        for n_sems, _, _ in self.parts:
            out.append(base)
            base += n_sems
        return out


def _run_comm(name, comm):
    n_i, n_o = len(comm.ins), len(comm.out_shape)

    def body(*refs):
        ins, outs = refs[:n_i], refs[n_i:n_i + n_o]
        send_sems, recv_sems = refs[n_i + n_o:]
        for base, (_, start, finish) in zip(comm.bases(), comm.parts):
            start(ins, outs, send_sems, recv_sems, base)
            finish(ins, outs, send_sems, recv_sems, base)

    return pl.pallas_call(
        body, name=name, out_shape=comm.out_shape, in_specs=[ANY] * n_i, out_specs=[ANY] * n_o,
        input_output_aliases=comm.aliases,
        scratch_shapes=[pltpu.SemaphoreType.DMA((comm.n_sems,)), pltpu.SemaphoreType.DMA((comm.n_sems,))],
    )(*_in_hbm(*comm.ins))


def _call(body, comm, bounds, args, *, name, grid, in_specs, out_specs, out_shape, scratch_shapes=(),
          compiler_params=None):
    in_specs, out_specs, out_shape, scratch_shapes = list(in_specs), list(out_specs), list(out_shape), list(scratch_shapes)
    if comm is None:
        return pl.pallas_call(body, name=name, grid=grid, in_specs=in_specs, out_specs=out_specs, out_shape=out_shape,
                              scratch_shapes=scratch_shapes, compiler_params=compiler_params)(*_in_hbm(*args)), []
    n_in, n_out, n_scr = len(in_specs), len(out_specs), len(scratch_shapes)
    n_ci, n_co = len(comm.ins), len(comm.out_shape)
    n_steps = int(np.prod(grid))
    assert len(bounds) == len(comm.parts) and all(0 <= b < n_steps for b in bounds) and list(bounds) == sorted(bounds)

    def fused(*refs):
        a = n_in
        b = a + n_ci
        c = b + n_out
        d = c + n_co
        e = d + n_scr
        cargs = (refs[a:b], refs[c:d], refs[e], refs[e + 1])
        step = pl.program_id(0)
        for ax in range(1, len(grid)):
            step = step * grid[ax] + pl.program_id(ax)
        bases = comm.bases()
        for p, (_, start, finish) in enumerate(comm.parts):
            @pl.when(step == bounds[p])
            def _(p=p, start=start):
                if p > 0:
                    comm.parts[p - 1][2](*cargs, bases[p - 1])
                start(*cargs, bases[p])
        body(*(refs[:a] + refs[b:c] + refs[d:e]))

        @pl.when(step == n_steps - 1)
        def _():
            comm.parts[-1][2](*cargs, bases[-1])

    res = pl.pallas_call(
        fused, name=name, grid=grid, in_specs=in_specs + [ANY] * n_ci, out_specs=out_specs + [ANY] * n_co,
        out_shape=out_shape + comm.out_shape,
        scratch_shapes=scratch_shapes + [pltpu.SemaphoreType.DMA((comm.n_sems,)), pltpu.SemaphoreType.DMA((comm.n_sems,))],
        input_output_aliases={n_in + i: n_out + j for i, j in comm.aliases.items()},
        compiler_params=compiler_params)(*_in_hbm(*args, *comm.ins))
    return res[:n_out], res[n_out:]


def _remote(src, dst, send_sems, recv_sems, idx, to):
    return pltpu.make_async_remote_copy(src_ref=src, dst_ref=dst, send_sem=send_sems.at[idx],
                                        recv_sem=recv_sems.at[idx], device_id=to, device_id_type=MESH_ID)


def _gather_comm(bufs, ici=True, pair=True):
    n = len(bufs)

    def half(ref, k, pc):
        rh = ref.shape[1] // 2
        return ref.at[k, pl.ds(pc * rh, rh), :]

    def ici_start(ins, outs, ss, rs, base):
        x, y, c = _mesh_pos()
        for a in range(n):
            mine = half(outs[a], 2 * x + y, c)
            for j, chip in enumerate(_other_chips(x, y)):
                _remote(mine, mine, ss, rs, base + 3 * a + j, (*chip, c)).start()

    def ici_finish(ins, outs, ss, rs, base):
        x, y, c = _mesh_pos()
        for a in range(n):
            for j, chip in enumerate(_other_chips(x, y)):
                theirs = half(outs[a], 2 * chip[0] + chip[1], c)
                _remote(theirs, theirs, ss, rs, base + 3 * a + j, (*chip, c)).wait()

    def pair_copy(outs, ss, rs, base, a):
        x, y, c = _mesh_pos()
        rh = outs[a].shape[1] // 2
        held = outs[a].at[:, pl.ds(c * rh, rh), :]
        return _remote(held, held, ss, rs, base + a, (x, y, 1 - c))

    def pair_start(ins, outs, ss, rs, base):
        for a in range(n):
            pair_copy(outs, ss, rs, base, a).start()

    def pair_finish(ins, outs, ss, rs, base):
        for a in range(n):
            pair_copy(outs, ss, rs, base, a).wait()

    parts = ([(3 * n, ici_start, ici_finish)] if ici else []) + ([(n, pair_start, pair_finish)] if pair else [])
    return _Comm(bufs, [_out(b.shape, b.dtype) for b in bufs], {a: a for a in range(n)}, parts)


def _merge_comm(*comms):
    ins, shapes, aliases, subs, base = [], [], {}, [], 0
    for cm in comms:
        (n_sems, start, finish), = cm.parts
        i0, o0 = len(ins), len(shapes)
        subs.append((slice(i0, i0 + len(cm.ins)), slice(o0, o0 + len(cm.out_shape)), base, start, finish))
        aliases.update({i0 + i: o0 + j for i, j in cm.aliases.items()})
        ins += cm.ins
        shapes += cm.out_shape
        base += n_sems

    def start_all(ins_r, outs_r, ss, rs, b):
        for si, so, off, start, _ in subs:
            start(ins_r[si], outs_r[so], ss, rs, b + off)

    def finish_all(ins_r, outs_r, ss, rs, b):
        for si, so, off, _, finish in subs:
            finish(ins_r[si], outs_r[so], ss, rs, b + off)

    return _Comm(ins, shapes, aliases, [(base, start_all, finish_all)])


def _own_half_buffers(pieces, dtypes, kc_arr):
    n = len(pieces)

    def body(kc_ref, *refs):
        for a in range(n):
            refs[n + a][0] = refs[a][...].astype(dtypes[a])

    def half(p):
        return p.shape[0] // 2, p.shape[1]

    return pl.pallas_call(
        body, name="own_halves",
        out_shape=[_out((N_CHIP,) + p.shape, dt) for p, dt in zip(pieces, dtypes)],
        grid_spec=pltpu.PrefetchScalarGridSpec(
            num_scalar_prefetch=1, grid=(1,),
            in_specs=[pl.BlockSpec(half(p), lambda i, kc: (kc[1], 0)) for p in pieces],
            out_specs=[pl.BlockSpec((1,) + half(p), lambda i, kc: (kc[0], kc[1], 0)) for p in pieces]),
        compiler_params=_cp(("arbitrary",), 48),
    )(kc_arr, *_in_hbm(*pieces))


def _exchange_comm(grads):
    n = len(grads)

    def copy(ins, outs, ss, rs, base, a):
        x, y, c = _mesh_pos()
        rh = ins[a].shape[1] // 2
        return _remote(ins[a].at[:, pl.ds((1 - c) * rh, rh), :], outs[a], ss, rs, base + a, (x, y, 1 - c))

    def start(ins, outs, ss, rs, base):
        for a in range(n):
            copy(ins, outs, ss, rs, base, a).start()

    def finish(ins, outs, ss, rs, base):
        for a in range(n):
            copy(ins, outs, ss, rs, base, a).wait()

    shapes = [_out((N_CHIP, g.shape[1] // 2, g.shape[2]), g.dtype) for g in grads]
    return _Comm(grads, shapes, {}, [(n, start, finish)])


def _chip_sum(name, g, recv, c_arr):
    _, r, cc = g.shape
    rh = r // 2

    def body(c_ref, g_ref, r_ref, o_ref):
        o_ref[...] = (g_ref[...] + r_ref[...]).astype(BF16)

    return pl.pallas_call(
        body, name=name, out_shape=_out((N_CHIP, rh, cc), BF16),
        grid_spec=pltpu.PrefetchScalarGridSpec(
            num_scalar_prefetch=1, grid=(N_CHIP,),
            in_specs=[pl.BlockSpec((1, rh, cc), lambda j, c_ref: (j, c_ref[0], 0)),
                      pl.BlockSpec((1, rh, cc), lambda j, c_ref: (j, 0, 0))],
            out_specs=pl.BlockSpec((1, rh, cc), lambda j, c_ref: (j, 0, 0))),
        compiler_params=_cp(("arbitrary",), 32),
    )(c_arr, *_in_hbm(g, recv))


def _scatter_comm(sums):
    n = len(sums)

    def copies(ins, outs, ss, rs, base):
        x, y, c = _mesh_pos()
        return [_remote(ins[a].at[2 * chip[0] + chip[1]], outs[a].at[j], ss, rs, base + 3 * a + j, (*chip, c))
                for a in range(n) for j, chip in enumerate(_other_chips(x, y))]

    def start(ins, outs, ss, rs, base):
        for cp in copies(ins, outs, ss, rs, base):
            cp.start()

    def finish(ins, outs, ss, rs, base):
        for cp in copies(ins, outs, ss, rs, base):
            cp.wait()

    shapes = [_out((3,) + s.shape[1:], s.dtype) for s in sums]
    return _Comm(sums, shapes, {}, [(3 * n, start, finish)])


def _scatter_copies(ins, lands, send_sems, recv_sems):
    x, y, c = _mesh_pos()
    return [_remote(ins[a].at[2 * chip[0] + chip[1]], lands[a].at[j], send_sems, recv_sems, 3 * a + j, (*chip, c))
            for a in range(len(ins)) for j, chip in enumerate(_other_chips(x, y))]


def _scatter_start(sums):
    n = len(sums)
    lands = [lax.empty((3,) + s.shape[1:], s.dtype) for s in sums]
    hbm = pl.BlockSpec(memory_space=pltpu.HBM)
    sem = pl.BlockSpec(memory_space=pltpu.SEMAPHORE)

    def body(*refs):
        ins, land_refs = refs[:n], refs[n:2 * n]
        send_sems, recv_sems = refs[2 * n], refs[2 * n + 1]
        token = refs[-1]
        for cp in _scatter_copies(ins, land_refs, send_sems, recv_sems):
            cp.start()
        token[...] = jnp.zeros_like(token)

    res = pl.pallas_call(
        body, name="ffn1_scatter_start",
        out_shape=(pltpu.SemaphoreType.DMA((3 * n,)), pltpu.SemaphoreType.DMA((3 * n,)),
                   *[pltpu.HBM(s.shape, s.dtype) for s in sums], *[pltpu.HBM(ld.shape, ld.dtype) for ld in lands],
                   jax.ShapeDtypeStruct((8, 128), F32)),
        in_specs=[hbm] * (2 * n), out_specs=(sem, sem, *[hbm] * (2 * n), pl.BlockSpec(memory_space=pltpu.VMEM)),
        input_output_aliases={i: 2 + i for i in range(2 * n)},
        compiler_params=pltpu.CompilerParams(has_side_effects=pltpu.SideEffectType.DATAFLOW_SIDE_EFFECTING),
    )(*[pltpu.with_memory_space_constraint(a, pltpu.HBM) for a in list(sums) + lands])
    return res[0], res[1], list(res[2:2 + n]), list(res[2 + n:2 + 2 * n]), res[-1]


def _scatter_wait(send_sems, recv_sems, sums, lands, after):
    n = len(sums)
    hbm = pl.BlockSpec(memory_space=pltpu.HBM)
    sem = pl.BlockSpec(memory_space=pltpu.SEMAPHORE)

    def body(*refs):
        ins, land_refs = refs[:n], refs[n:2 * n]
        for cp in _scatter_copies(ins, land_refs, refs[2 * n], refs[2 * n + 1]):
            cp.wait_send()
            cp.wait_recv()

    res = pl.pallas_call(
        body, name="ffn1_scatter_wait",
        out_shape=tuple([pltpu.HBM(s.shape, s.dtype) for s in sums] + [pltpu.HBM(ld.shape, ld.dtype) for ld in lands]),
        in_specs=[hbm] * (2 * n) + [sem, sem, pl.BlockSpec(memory_space=pl.ANY)], out_specs=tuple([hbm] * (2 * n)),
        input_output_aliases={i: i for i in range(2 * n)},
        compiler_params=pltpu.CompilerParams(has_side_effects=pltpu.SideEffectType.DATAFLOW_SIDE_EFFECTING),
    )(*sums, *lands, send_sems, recv_sems, after)
    return list(res[:n]), list(res[n:])


def _total_sum(name, sums, recv3, kc_arr):
    _, rh, cc = sums.shape

    def body(kc_ref, s_ref, r_ref, o_ref):
        t = s_ref[0].astype(F32) + r_ref[0].astype(F32)
        t = t + r_ref[1].astype(F32)
        o_ref[...] = t + r_ref[2].astype(F32)

    return pl.pallas_call(
        body, name=name, out_shape=_out((2 * rh, cc), F32),
        grid_spec=pltpu.PrefetchScalarGridSpec(
            num_scalar_prefetch=1, grid=(1,),
            in_specs=[pl.BlockSpec((1, rh, cc), lambda i, kc_ref: (kc_ref[0], 0, 0)),
                      pl.BlockSpec((3, rh, cc), lambda i, kc_ref: (0, 0, 0))],
            out_specs=pl.BlockSpec((rh, cc), lambda i, kc_ref: (kc_ref[1], 0))),
        compiler_params=_cp(("arbitrary",), 32),
    )(kc_arr, *_in_hbm(sums, recv3))


def _assemble_comm(totals):
    n = len(totals)

    def copy(outs, ss, rs, base, a):
        x, y, c = _mesh_pos()
        rh = outs[a].shape[0] // 2
        here = outs[a].at[pl.ds(c * rh, rh), :]
        return _remote(here, here, ss, rs, base + a, (x, y, 1 - c))

    def start(ins, outs, ss, rs, base):
        for a in range(n):
            copy(outs, ss, rs, base, a).start()

    def finish(ins, outs, ss, rs, base):
        for a in range(n):
            copy(outs, ss, rs, base, a).wait()

    shapes = [_out(t.shape, t.dtype) for t in totals]
    return _Comm(totals, shapes, {a: a for a in range(n)}, [(n, start, finish)])


def _small_layout(shapes):
    n = len(shapes)
    narrow_w = 64
    wide = [a for a in range(n) if shapes[a][1] > narrow_w]
    narrow = sorted((a for a in range(n) if shapes[a][1] <= narrow_w), key=lambda a: -shapes[a][0])
    offs, cols, groups, widths, rows = {}, {}, [], [], []
    if wide:
        r = 0
        for a in wide:
            offs[a], cols[a] = r, 0
            r += shapes[a][0]
        groups.append(wide)
        widths.append(max(shapes[a][1] for a in wide))
        rows.append(-(-r // 8) * 8)
    if narrow:
        heights = [0, 0]
        for a in narrow:
            side = 0 if heights[0] <= heights[1] else 1
            offs[a], cols[a] = heights[side], side * narrow_w
            heights[side] += shapes[a][0]
        groups.append(narrow)
        widths.append(2 * narrow_w)
        rows.append(-(-max(heights) // 8) * 8)

    def window(ref, a):
        return ref.at[offs[a]:offs[a] + shapes[a][0], cols[a]:cols[a] + shapes[a][1]]

    return groups, widths, rows, window


def _small_pack(arrays, me_arr):
    shapes = [a.shape for a in arrays]
    groups, widths, rows, window = _small_layout(shapes)
    n, n_g = len(arrays), len(groups)

    def body(me_ref, *refs):
        ins, outs = refs[:n], refs[n:]
        for gi, g in enumerate(groups):
            outs[gi][...] = jnp.zeros_like(outs[gi])
            for a in g:
                window(outs[gi].at[0], a)[...] = ins[a][...]

    return pl.pallas_call(
        body, name="small_pack", out_shape=[_out((8, r, w), F32) for r, w in zip(rows, widths)],
        grid_spec=pltpu.PrefetchScalarGridSpec(
            num_scalar_prefetch=1, grid=(1,), in_specs=[pl.BlockSpec(s, lambda i, me: (0, 0)) for s in shapes],
            out_specs=[pl.BlockSpec((1, r, w), lambda i, me: (me[0], 0, 0)) for r, w in zip(rows, widths)]),
        compiler_params=_cp(("arbitrary",), 32),
    )(me_arr, *_in_hbm(*arrays))


def _spread_comm(slots):
    n = len(slots)
    flips = [(dx, dy, dc) for dx in range(2) for dy in range(2) for dc in range(2)][1:]

    def copies(outs, ss, rs, base):
        x, y, c = _mesh_pos()
        mine = 4 * x + 2 * y + c
        return [_remote(outs[a].at[mine], outs[a].at[mine], ss, rs, base + 7 * a + f,
                        (x ^ dx, y ^ dy, c ^ dc)) for a in range(n) for f, (dx, dy, dc) in enumerate(flips)]

    def start(ins, outs, ss, rs, base):
        for cp in copies(outs, ss, rs, base):
            cp.start()

    def finish(ins, outs, ss, rs, base):
        for cp in copies(outs, ss, rs, base):
            cp.wait()

    return _Comm(slots, [_out(s.shape, s.dtype) for s in slots], {a: a for a in range(n)}, [(7 * n, start, finish)])


def _small_total(slots, shapes):
    groups, widths, rows, window = _small_layout(shapes)
    n, n_g = len(shapes), len(groups)

    def body(*refs):
        ins, outs, acc = refs[:n_g], refs[n_g:n_g + n], refs[n_g + n:]
        for gi, g in enumerate(groups):
            t = ins[gi][0] + ins[gi][1]
            for d in range(2, 8):
                t = t + ins[gi][d]
            acc[gi][...] = t
            for a in g:
                outs[a][...] = window(acc[gi], a)[...]

    return pl.pallas_call(
        body, name="small_total", grid=(1,), out_shape=[_out(s, F32) for s in shapes],
        in_specs=[_full(s.shape) for s in slots], out_specs=[_full(s) for s in shapes],
        scratch_shapes=[pltpu.VMEM((r, w), F32) for r, w in zip(rows, widths)],
        compiler_params=_cp(("arbitrary",), 48),
    )(*_in_hbm(*slots))


def _small_allreduce(arrays, comm):
    n = len(arrays)
    shapes = [a.shape for a in arrays]
    groups, widths, rows, window = _small_layout(shapes)
    n_g = len(groups)

    def body(*refs):
        ins, outs = refs[:n], refs[n:2 * n]
        pack, sib, csum, every = (refs[2 * n + i * n_g:2 * n + (i + 1) * n_g] for i in range(4))
        send_sems, recv_sems = refs[2 * n + 4 * n_g:]
        x, y, c = _mesh_pos()
        k = 2 * x + y
        for gi, g in enumerate(groups):
            pack[gi][...] = jnp.zeros_like(pack[gi])
            for a in g:
                window(pack[gi], a)[...] = ins[a][...]
        cps = [_remote(pack[gi], sib[gi], send_sems, recv_sems, gi, (x, y, 1 - c)) for gi in range(n_g)]
        for cp in cps:
            cp.start()
        for cp in cps:
            cp.wait()
        for gi in range(n_g):
            csum[gi][...] = pack[gi][...] + sib[gi][...]
            every[gi][k] = csum[gi][...]
        cps = [_remote(csum[gi], every[gi].at[k], send_sems, recv_sems, n_g + 3 * gi + j, (*chip, c))
               for gi in range(n_g) for j, chip in enumerate(_other_chips(x, y))]
        for cp in cps:
            cp.start()
        for cp in cps:
            cp.wait()
        for gi, g in enumerate(groups):
            pack[gi][...] = ((every[gi][0] + every[gi][1]) + every[gi][2]) + every[gi][3]
            for a in g:
                outs[a][...] = window(pack[gi], a)[...]

    bufs = [pltpu.VMEM((r, w), F32) for r, w in zip(rows, widths)]
    return _call(
        body, comm, (0,), arrays, name="small_allreduce", grid=(1,), out_shape=[_out(s, F32) for s in shapes],
        in_specs=[_full(s) for s in shapes], out_specs=[_full(s) for s in shapes],
        scratch_shapes=bufs * 3 + [pltpu.VMEM((N_CHIP, r, w), F32) for r, w in zip(rows, widths)] +
                       [pltpu.SemaphoreType.DMA((4 * n_g,)), pltpu.SemaphoreType.DMA((4 * n_g,))],
        compiler_params=_cp(("arbitrary",), 40))


def _adamw_small(ws, gs, ms, vs, comm):
    n = len(ws)

    def body(*refs):
        w, g, m, v, d, mo, vo = (refs[i * n:(i + 1) * n] for i in range(7))
        for a in range(n):
            d[a][...], mo[a][...], vo[a][...] = _adamw_math(w[a][...], g[a][...], m[a][...], v[a][...])

    specs = [_full(w.shape) for w in ws]
    res, got = _call(
        body, comm, (0,), (*ws, *gs, *ms, *vs), name="adamw_small", grid=(1,),
        out_shape=[_out(w.shape, F32) for w in ws] * 3,
        in_specs=specs * 4, out_specs=specs * 3, compiler_params=_cp(("arbitrary",), 40))
    return (res[:n], res[n:2 * n], res[2 * n:]), got


def _adamw_math(w, g, m, v):
    m = ADAM_B1 * m + (1.0 - ADAM_B1) * g
    v = ADAM_B2 * v + (1.0 - ADAM_B2) * (g * g)
    m_hat = m / (1.0 - ADAM_B1 ** ADAM_STEP)
    v_hat = v / (1.0 - ADAM_B2 ** ADAM_STEP)
    delta = -ADAM_LR * (m_hat / (jnp.sqrt(v_hat) + ADAM_EPS) + ADAM_WD * w)
    return delta, m, v


def _adamw(name, w, g, m, v):
    r, c = w.shape
    tr = max(t for t in range(8, 513, 8) if r % t == 0)

    def body(w_ref, g_ref, m_ref, v_ref, d_ref, mo_ref, vo_ref):
        d_ref[...], mo_ref[...], vo_ref[...] = _adamw_math(w_ref[...], g_ref[...], m_ref[...], v_ref[...])

    return pl.pallas_call(
        body, name=name, grid=(r // tr,), in_specs=[_rows(tr, c)] * 4, out_specs=[_rows(tr, c)] * 3,
        out_shape=[_out((r, c), F32)] * 3, compiler_params=_cp(("arbitrary",), 32),
    )(*_in_hbm(w, g, m, v))


def _as_matrix(name, a):
    if name == "na_rpb":
        return a[0].transpose(1, 0, 2).reshape(N_HEADS * (2 * KH - 1), 2 * KW - 1)
    if name in ("s5_b_re", "s5_b_im"):
        return a.transpose(0, 1, 2, 4, 3).reshape(2 * S5_G * S5_H, S5_P)
    if name in ("s5_c_re", "s5_c_im"):
        return a.reshape(2 * S5_G * S5_H, S5_P)
    if name in ("s5_lam_re", "s5_lam_im"):
        return a.reshape(2 * S5_G, S5_P)
    if name == "s5_log_dt":
        return a.reshape(2, S5_G)
    return a


def _from_matrix(name, m):
    if name == "na_rpb":
        return m.reshape(2 * KH - 1, N_HEADS, 2 * KW - 1).transpose(1, 0, 2)[None]
    if name in ("s5_b_re", "s5_b_im"):
        return m.reshape(1, 2, S5_G, S5_H, S5_P).transpose(0, 1, 2, 4, 3)
    if name in ("s5_c_re", "s5_c_im"):
        return m.reshape(1, 2, S5_G, S5_H, S5_P)
    if name in ("s5_lam_re", "s5_lam_im"):
        return m.reshape(1, 2, S5_G, S5_P)
    if name == "s5_log_dt":
        return m.reshape(1, 2, S5_G)
    return m


WEIGHTS = ["meta_tokens", "ffn1_pre_g", "ffn1_post_g", "ffn1_w_gate", "ffn1_w_up", "ffn1_w_down", "mix_pre_g", "w_in",
           "na_rpb", "s5_lam_re", "s5_lam_im", "s5_log_dt", "s5_b_re", "s5_b_im", "s5_c_re", "s5_c_im", "s5_d",
           "s5_w_glu", "s5_b_glu", "na_out_g", "s5_out_g", "w_out", "mix_post_g", "ffn2_pre_g", "ffn2_post_g",
           "ffn2_w_gate", "ffn2_w_up", "ffn2_w_down", "final_g"]
BIG = ["ffn1_w_gate", "ffn1_w_up", "ffn1_w_down", "w_in", "s5_w_glu", "w_out", "ffn2_w_gate", "ffn2_w_up",
       "ffn2_w_down"]
TRANSPOSED = ["ffn1_w_gate", "ffn1_w_up", "ffn2_w_gate", "ffn2_w_up"]
GAINS = ["ffn1_pre_g", "ffn1_post_g", "mix_pre_g", "s5_d", "s5_b_glu", "na_out_g", "s5_out_g", "mix_post_g",
         "ffn2_pre_g", "ffn2_post_g", "final_g"]
SMALL = [n for n in WEIGHTS if n not in BIG]


def kernel(*args):
    names = ["x"] + WEIGHTS + ["loss_target"] + ["m_" + n for n in WEIGHTS] + ["v_" + n for n in WEIGHTS]
    assert len(args) == len(names)
    given = dict(zip(names, args))
    x_pos, y_pos, c_pos = _mesh_pos()
    k_pos = 2 * x_pos + y_pos
    c_arr = jnp.reshape(c_pos, (1,)).astype(jnp.int32)
    kc_arr = jnp.stack([k_pos, c_pos]).astype(jnp.int32)

    def piece(name, a):
        return a[0].T if name in TRANSPOSED else a[0]

    def unpiece(name, a):
        return a.T[None] if name in TRANSPOSED else a[None]

    placed = BIG + ["meta_tokens"]
    bufs = dict(zip(placed, _own_half_buffers([piece(n, given[n]) for n in BIG] + [given["meta_tokens"]],
                                              [BF16] * len(BIG) + [F32], kc_arr)))

    gains = {n: given[n] for n in GAINS}
    s5 = {n: _as_matrix("s5_" + n, given["s5_" + n])
          for n in ["lam_re", "lam_im", "log_dt", "b_re", "b_im", "c_re", "c_im"]}
    me_arr = jnp.reshape(4 * x_pos + 2 * y_pos + c_pos, (1,)).astype(jnp.int32)
    loss, dh0, pieces, small, late, (ffn1, flight), (mid, sums) = _step(
        given["x"][0], given["loss_target"][0], bufs, gains, s5, given["na_rpb"][0], c_arr, kc_arr, me_arr)
    loss = lax.psum(loss, ("x", "y", "c"))
    n_tok = given["x"].shape[1]
    grad_x = dh0[N_META:N_META + n_tok][None]

    late["meta_tokens"] = dh0[:N_META]
    red, recv3 = _small_allreduce(list(late.values()), _scatter_comm(sums))
    small.update(zip(late, red))
    mc = D // N_CHIP
    small["meta_tokens"] = lax.dynamic_slice_in_dim(small["meta_tokens"], k_pos * mc, mc, 1)
    send_sems, recv_sems, sums1, lands1 = flight
    sums1, recv3_1 = _scatter_wait(send_sems, recv_sems, sums1, lands1, red[0])
    last = ffn1 + mid
    totals = [_total_sum("total_sum_" + n, s, r, kc_arr)
              for n, s, r in zip(last, sums1 + list(sums), recv3_1 + list(recv3))]
    gs = [small[n] for n in SMALL]
    (d2, m2, v2), done = _adamw_small([_as_matrix(n, given[n]) for n in SMALL], gs,
                                      [_as_matrix(n, given["m_" + n]) for n in SMALL],
                                      [_as_matrix(n, given["v_" + n]) for n in SMALL], _assemble_comm(totals))
    pieces.update(zip(last, done))

    out_g, out_d, out_m, out_v = {}, {}, {}, {}
    for n, g, dd, mm, vv in zip(SMALL, gs, d2, m2, v2):
        out_g[n], out_d[n], out_m[n], out_v[n] = (_from_matrix(n, t) for t in (g, dd, mm, vv))
    for n in BIG:
        g2 = pieces[n]
        d2, m2, v2 = _adamw("adamw_" + n, piece(n, given[n]), g2, piece(n, given["m_" + n]),
                            piece(n, given["v_" + n]))
        out_g[n], out_d[n], out_m[n], out_v[n] = (unpiece(n, t) for t in (g2, d2, m2, v2))
    return (loss, grad_x, *[out_g[n] for n in WEIGHTS], *[out_d[n] for n in WEIGHTS],
            *[out_m[n] for n in WEIGHTS], *[out_v[n] for n in WEIGHTS])
```

```python
import functools
import math

import numpy as np
import jax
import jax.numpy as jnp
from jax import lax
from jax.experimental import pallas as pl
from jax.experimental.pallas import tpu as pltpu

F32 = jnp.float32
BF16 = jnp.bfloat16

D = 1024
N_META = 16
GRID_W = 64
NA_W = 512
S5_W = 512
HEAD_DIM = 64
N_HEADS = 8
KH = 8
KW = 16
S5_G = 32
S5_P = 64
S5_H = 16
N_BUNDLE = 4
FF = 2816
N_CHIP = 4
FC = FF // N_CHIP
EPS = 1e-6
NEG_INF = -1e30
Q_ROWS = 4
K_ROWS = 12
QB = Q_ROWS * GRID_W
KB = K_ROWS * GRID_W
SCAN_CHUNK = 256

ADAM_LR = 0.001
ADAM_B1 = 0.9
ADAM_B2 = 0.999
ADAM_EPS = 1e-08
ADAM_WD = 0.01
ADAM_STEP = 10

NT = (((1,), (1,)), ((), ()))
TN = (((0,), (0,)), ((), ()))
MESH_ID = pl.DeviceIdType.MESH


def _cp(sem=None, vmem_mb=None):
    kw = {}
    if sem is not None:
        kw["dimension_semantics"] = sem
    if vmem_mb is not None:
        kw["vmem_limit_bytes"] = vmem_mb << 20
    return pltpu.CompilerParams(**kw)


def _full(shape):
    n = len(shape)
    return pl.BlockSpec(shape, lambda *_: (0,) * n)


def _rows(tm, w):
    return pl.BlockSpec((tm, w), lambda i: (i, 0))


ANY = pl.BlockSpec(memory_space=pl.ANY)


def _rms(x, g):
    r = lax.rsqrt(jnp.mean(x * x, axis=-1, keepdims=True) + EPS)
    return x * r * g


def _rms_bwd(x, g, dy):
    r = lax.rsqrt(jnp.mean(x * x, axis=-1, keepdims=True) + EPS)
    xh = x * r
    dg = jnp.sum(dy * xh, axis=0, keepdims=True)
    dyg = dy * g
    dx = r * (dyg - xh * jnp.mean(dyg * xh, axis=-1, keepdims=True))
    return dx, dg


def _out(shape, dtype):
    return pltpu.HBM(tuple(shape), dtype)


def _in_hbm(*args):
    return [pltpu.with_memory_space_constraint(a, pltpu.HBM) if jnp.issubdtype(a.dtype, jnp.floating) and a.ndim > 1
            else a for a in args]


def _dot(a, b):
    return jnp.dot(a, b, preferred_element_type=F32)


def _dg(a, b, dims):
    return lax.dot_general(a, b, dims, preferred_element_type=F32)


def _ffn_fwd(name, h, g_pre, g_post, wg, wu, wd, tm, comm=None, bounds=()):
    tp = h.shape[0]
    nt = tp // tm

    def body(h_ref, gp_ref, gq_ref, wg_ref, wu_ref, wd_ref, hn_ref, gate_ref, up_ref, f_ref, xn_s, acc_s):
        c = pl.program_id(1)

        @pl.when(c == 0)
        def _():
            xn_s[...] = _rms(h_ref[...], gp_ref[...]).astype(BF16)
            acc_s[...] = jnp.zeros_like(acc_s)

        xn = xn_s[...]
        gate = _dg(xn, wg_ref[0], NT)
        up = _dg(xn, wu_ref[0], NT)
        gate_ref[0] = gate
        up_ref[0] = up
        act = (gate * jax.nn.sigmoid(gate) * up).astype(BF16)
        acc_s[...] += _dot(act, wd_ref[0])

        @pl.when(c == N_CHIP - 1)
        def _():
            f = acc_s[...]
            f_ref[...] = f
            hn_ref[...] = h_ref[...] + 0.5 * _rms(f, gq_ref[...])

    return _call(
        body, comm, bounds, (h, g_pre, g_post, wg, wu, wd), name=name, grid=(nt, N_CHIP),
        in_specs=[pl.BlockSpec((tm, D), lambda i, c: (i, 0)), _full((1, D)), _full((1, D))] +
                 [pl.BlockSpec((1, FC, D), lambda i, c: (c, 0, 0))] * 3,
        out_specs=[pl.BlockSpec((tm, D), lambda i, c: (i, 0)),
                   pl.BlockSpec((1, tm, FC), lambda i, c: (c, i, 0)),
                   pl.BlockSpec((1, tm, FC), lambda i, c: (c, i, 0)),
                   pl.BlockSpec((tm, D), lambda i, c: (i, 0))],
        out_shape=[_out((tp, D), F32), _out((N_CHIP, tp, FC), F32),
                   _out((N_CHIP, tp, FC), F32), _out((tp, D), F32)],
        scratch_shapes=[pltpu.VMEM((tm, D), BF16), pltpu.VMEM((tm, D), F32)],
        compiler_params=_cp(("arbitrary", "arbitrary"), 48))


def _ffn_bwd(name, h, g_pre, df, gate, up, wg, wu, wd, tm, comm=None, bounds=()):
    tp = h.shape[0]
    nt = tp // tm
    rh = FC // 2

    def body(h_ref, gp_ref, df_ref, gate_ref, up_ref, wg_ref, wu_ref, wd_ref,
             dwg_ref, dwu_ref, dwd_ref, dxn_ref, rg_ref, ru_ref, rd_ref, ag, au, ad, send_sems, recv_sems):
        c = pl.program_id(0)
        i = pl.program_id(1)

        def to_sibling(a, piece):
            x, y, core = _mesh_pos()
            dw_ref, r_ref = ((dwg_ref, rg_ref), (dwu_ref, ru_ref), (dwd_ref, rd_ref))[a]
            return _remote(dw_ref.at[piece, pl.ds((1 - core) * rh, rh), :], r_ref.at[piece], send_sems, recv_sems,
                           3 * piece + a, (x, y, 1 - core))

        @pl.when(i == 0)
        def _():
            ag[...] = jnp.zeros_like(ag)
            au[...] = jnp.zeros_like(au)
            ad[...] = jnp.zeros_like(ad)

        xn = _rms(h_ref[...], gp_ref[...]).astype(BF16)
        dfb = df_ref[...].astype(BF16)
        gt = gate_ref[0]
        u = up_ref[0]
        sg = jax.nn.sigmoid(gt)
        si = gt * sg
        act = (si * u).astype(BF16)
        dact = _dg(dfb, wd_ref[0], NT)
        ad[...] += _dg(act, dfb, TN)
        dgate = (dact * u * (sg * (1.0 + gt * (1.0 - sg)))).astype(BF16)
        dup = (dact * si).astype(BF16)
        ag[...] += _dg(dgate, xn, TN)
        au[...] += _dg(dup, xn, TN)
        dxn_ref[0] = _dot(dgate, wg_ref[0]) + _dot(dup, wu_ref[0])

        @pl.when(i == nt - 1)
        def _():
            pltpu.sync_copy(ag, dwg_ref.at[c])
            pltpu.sync_copy(au, dwu_ref.at[c])
            pltpu.sync_copy(ad, dwd_ref.at[c])
            for a in range(3):
                to_sibling(a, c).start()

        @pl.when((c == N_CHIP - 1) & (i == nt - 1))
        def _():
            for piece in range(N_CHIP):
                for a in range(3):
                    to_sibling(a, piece).wait()

    return _call(
        body, comm, bounds, (h, g_pre, df, gate, up, wg, wu, wd), name=name, grid=(N_CHIP, nt),
        in_specs=[pl.BlockSpec((tm, D), lambda c, i: (i, 0)), _full((1, D)),
                  pl.BlockSpec((tm, D), lambda c, i: (i, 0)),
                  pl.BlockSpec((1, tm, FC), lambda c, i: (c, i, 0)),
                  pl.BlockSpec((1, tm, FC), lambda c, i: (c, i, 0))] +
                 [pl.BlockSpec((1, FC, D), lambda c, i: (c, 0, 0))] * 3,
        out_specs=[ANY, ANY, ANY, pl.BlockSpec((1, tm, D), lambda c, i: (c, i, 0)), ANY, ANY, ANY],
        out_shape=[_out((N_CHIP, FC, D), F32)] * 3 + [_out((N_CHIP, tp, D), F32)] +
                  [_out((N_CHIP, rh, D), F32)] * 3,
        scratch_shapes=[pltpu.VMEM((FC, D), F32)] * 3 +
                       [pltpu.SemaphoreType.DMA((3 * N_CHIP,)), pltpu.SemaphoreType.DMA((3 * N_CHIP,))],
        compiler_params=_cp(("arbitrary", "arbitrary"), 58))


def _ffn_pre_bwd(name, dh, dxn_part, h, g_pre, tm, comm=None, bounds=()):
    tp = h.shape[0]
    nt = tp // tm

    def body(dh_ref, dxn_ref, h_ref, gp_ref, out_ref, dg_ref):
        i = pl.program_id(0)
        dxn = (dxn_ref[0] + dxn_ref[1]) + (dxn_ref[2] + dxn_ref[3])
        dx, dg = _rms_bwd(h_ref[...], gp_ref[...], dxn)
        out_ref[...] = dh_ref[...] + dx

        @pl.when(i == 0)
        def _():
            dg_ref[...] = jnp.zeros_like(dg_ref)

        dg_ref[...] += dg

    return _call(
        body, comm, bounds, (dh, dxn_part, h, g_pre), name=name, grid=(nt,),
        in_specs=[_rows(tm, D), pl.BlockSpec((N_CHIP, tm, D), lambda i: (0, i, 0)), _rows(tm, D), _full((1, D))],
        out_specs=[_rows(tm, D), _full((1, D))],
        out_shape=[_out((tp, D), F32), _out((1, D), F32)],
        compiler_params=_cp(("arbitrary",), 48))


def _mix_in(h, g, w_in, tm):
    tp = h.shape[0]

    def body(h_ref, g_ref, w_ref, q_ref, k_ref, v_ref, u_ref):
        a = _rms(h_ref[...], g_ref[...]).astype(BF16)
        q_ref[...] = _dot(a, w_ref[0]).astype(BF16)
        k_ref[...] = _dot(a, w_ref[1]).astype(BF16)
        v_ref[...] = _dot(a, w_ref[2]).astype(BF16)
        u_ref[...] = _dot(a, w_ref[3])

    return pl.pallas_call(
        body, name="mix_in", grid=(tp // tm,),
        in_specs=[_rows(tm, D), _full((1, D)), _full((N_CHIP, D, NA_W))],
        out_specs=[_rows(tm, NA_W)] * 4,
        out_shape=[_out((tp, NA_W), BF16)] * 3 + [_out((tp, S5_W), F32)],
        compiler_params=_cp(("arbitrary",), 40),
    )(*_in_hbm(h, g, w_in))


def _gelu(x):
    return jax.nn.gelu(x, approximate=True)


def _gelu_grad(x):
    k = math.sqrt(2.0 / math.pi)
    t = jnp.tanh(k * (x + 0.044715 * x * x * x))
    return 0.5 * (1.0 + t) + 0.5 * x * (1.0 - t * t) * k * (1.0 + 3.0 * 0.044715 * x * x)


def _mix_out(o_na, y_pre, h, w_glu, b_glu, g_na, g_s5, w_out, g_post, tm, comm=None, bounds=()):
    tp = h.shape[0]

    def body(ona_ref, yp_ref, h_ref, wglu_ref, bglu_ref, gna_ref, gs5_ref, wout_ref, gpost_ref, hn_ref, mix_ref):
        y = _gelu(yp_ref[...])
        z = _dot(y.astype(BF16), wglu_ref[...]) + bglu_ref[...]
        o_s5 = y * jax.nn.sigmoid(z)
        n1 = _rms(ona_ref[...], gna_ref[...]).astype(BF16)
        n2 = _rms(o_s5, gs5_ref[...]).astype(BF16)
        mix = _dot(n1, wout_ref[0:NA_W, :]) + _dot(n2, wout_ref[NA_W:, :])
        mix_ref[...] = mix
        hn_ref[...] = h_ref[...] + _rms(mix, gpost_ref[...])

    return _call(
        body, comm, bounds, (o_na, y_pre, h, w_glu, b_glu, g_na, g_s5, w_out, g_post), name="mix_out",
        grid=(tp // tm,),
        in_specs=[_rows(tm, NA_W), _rows(tm, S5_W), _rows(tm, D), _full((S5_W, S5_W)), _full((1, S5_W)),
                  _full((1, NA_W)), _full((1, S5_W)), _full((D, D)), _full((1, D))],
        out_specs=[_rows(tm, D), _rows(tm, D)],
        out_shape=[_out((tp, D), F32)] * 2,
        compiler_params=_cp(("arbitrary",), 40))


def _mix_out_bwd(dh, mix, o_na, y_pre, w_glu, b_glu, g_na, g_s5, w_out, g_post, tm):
    tp = dh.shape[0]
    nt = tp // tm

    def body(dh_ref, mix_ref, ona_ref, yp_ref, wglu_ref, bglu_ref, gna_ref, gs5_ref, wout_ref, gpost_ref,
             dona_ref, dyp_ref, dwout_ref, dwglu_ref, dgpost_ref, dgna_ref, dgs5_ref, dbglu_ref, a_out, a_glu):
        i = pl.program_id(0)

        @pl.when(i == 0)
        def _():
            a_out[...] = jnp.zeros_like(a_out)
            a_glu[...] = jnp.zeros_like(a_glu)
            dgpost_ref[...] = jnp.zeros_like(dgpost_ref)
            dgna_ref[...] = jnp.zeros_like(dgna_ref)
            dgs5_ref[...] = jnp.zeros_like(dgs5_ref)
            dbglu_ref[...] = jnp.zeros_like(dbglu_ref)

        dmix, dgpost = _rms_bwd(mix_ref[...], gpost_ref[...], dh_ref[...])
        dgpost_ref[...] += dgpost
        yp = yp_ref[...]
        y = _gelu(yp)
        yb = y.astype(BF16)
        z = _dot(yb, wglu_ref[...]) + bglu_ref[...]
        sg = jax.nn.sigmoid(z)
        o_s5 = y * sg
        o_na = ona_ref[...]
        n1 = _rms(o_na, gna_ref[...]).astype(BF16)
        n2 = _rms(o_s5, gs5_ref[...]).astype(BF16)
        dmb = dmix.astype(BF16)
        a_out[0:NA_W, :] += _dg(n1, dmb, TN)
        a_out[NA_W:, :] += _dg(n2, dmb, TN)
        dn1 = _dg(dmb, wout_ref[0:NA_W, :], NT)
        dn2 = _dg(dmb, wout_ref[NA_W:, :], NT)
        dona, dgna = _rms_bwd(o_na, gna_ref[...], dn1)
        dona_ref[...] = dona
        dgna_ref[...] += dgna
        dos5, dgs5 = _rms_bwd(o_s5, gs5_ref[...], dn2)
        dgs5_ref[...] += dgs5
        dz = dos5 * y * (sg * (1.0 - sg))
        dbglu_ref[...] += jnp.sum(dz, axis=0, keepdims=True)
        dzb = dz.astype(BF16)
        a_glu[...] += _dg(yb, dzb, TN)
        dy = dos5 * sg + _dg(dzb, wglu_ref[...], NT)
        dyp_ref[...] = dy * _gelu_grad(yp)

        @pl.when(i == nt - 1)
        def _():
            pltpu.sync_copy(a_out, dwout_ref)
            pltpu.sync_copy(a_glu, dwglu_ref)

    return pl.pallas_call(
        body, name="mix_out_bwd", grid=(nt,),
        in_specs=[_rows(tm, D), _rows(tm, D), _rows(tm, NA_W), _rows(tm, S5_W), _full((S5_W, S5_W)),
                  _full((1, S5_W)), _full((1, NA_W)), _full((1, S5_W)), _full((D, D)), _full((1, D))],
        out_specs=[_rows(tm, NA_W), _rows(tm, S5_W), ANY, ANY, _full((1, D)), _full((1, NA_W)),
                   _full((1, S5_W)), _full((1, S5_W))],
        out_shape=[_out((tp, NA_W), F32), _out((tp, S5_W), F32),
                   _out((D, D), F32), _out((S5_W, S5_W), F32),
                   _out((1, D), F32), _out((1, NA_W), F32),
                   _out((1, S5_W), F32), _out((1, S5_W), F32)],
        scratch_shapes=[pltpu.VMEM((D, D), F32), pltpu.VMEM((S5_W, S5_W), F32)],
        compiler_params=_cp(("arbitrary",), 48),
    )(*_in_hbm(dh, mix, o_na, y_pre, w_glu, b_glu, g_na, g_s5, w_out, g_post))


def _mix_in_bwd(dq, dk, dv, du, h, g, w_in, dh, f1, g_post1, tm, comm=None, bounds=()):
    tp = h.shape[0]
    nt = tp // tm

    def body(dq_ref, dk_ref, dv_ref, du_ref, h_ref, g_ref, w_ref, dh_ref, f_ref, gq_ref,
             dh1_ref, df_ref, dw_ref, dg_ref, dgq_ref, acc):
        i = pl.program_id(0)

        @pl.when(i == 0)
        def _():
            acc[...] = jnp.zeros_like(acc)
            dg_ref[...] = jnp.zeros_like(dg_ref)
            dgq_ref[...] = jnp.zeros_like(dgq_ref)

        x = h_ref[...]
        a = _rms(x, g_ref[...]).astype(BF16)
        da = jnp.zeros((tm, D), F32)
        for j, r in enumerate((dq_ref, dk_ref, dv_ref, du_ref)):
            dp = r[...].astype(BF16)
            da = da + _dg(dp, w_ref[j], NT)
            acc[j] += _dg(a, dp, TN)
        dx, dg = _rms_bwd(x, g_ref[...], da)
        dh1 = dh_ref[...] + dx
        dh1_ref[...] = dh1
        dg_ref[...] += dg
        df, dgq = _rms_bwd(f_ref[...], gq_ref[...], 0.5 * dh1)
        df_ref[...] = df
        dgq_ref[...] += dgq

        @pl.when(i == nt - 1)
        def _():
            pltpu.sync_copy(acc, dw_ref)

    return _call(
        body, comm, bounds, (dq, dk, dv, du, h, g, w_in, dh, f1, g_post1), name="mix_in_bwd", grid=(nt,),
        in_specs=[_rows(tm, NA_W)] * 4 + [_rows(tm, D), _full((1, D)), _full((N_CHIP, D, NA_W)), _rows(tm, D),
                                         _rows(tm, D), _full((1, D))],
        out_specs=[_rows(tm, D), _rows(tm, D), ANY, _full((1, D)), _full((1, D))],
        out_shape=[_out((tp, D), F32), _out((tp, D), F32),
                   _out((N_CHIP, D, NA_W), F32), _out((1, D), F32),
                   _out((1, D), F32)],
        scratch_shapes=[pltpu.VMEM((N_CHIP, D, NA_W), F32)],
        compiler_params=_cp(("arbitrary",), 48))


def _final_loss(h, g_final, target, f2, g_post2, n_tok, tm):
    tp = h.shape[0]

    def body(h_ref, g_ref, t_ref, f_ref, gq_ref, dh_ref, df_ref, loss_ref, dg_ref, dgq_ref):
        i = pl.program_id(0)

        @pl.when(i == 0)
        def _():
            loss_ref[...] = jnp.zeros_like(loss_ref)
            dg_ref[...] = jnp.zeros_like(dg_ref)
            dgq_ref[...] = jnp.zeros_like(dgq_ref)

        x = h_ref[...]
        y = _rms(x, g_ref[...])
        row = i * tm + lax.broadcasted_iota(jnp.int32, (tm, 1), 0)
        valid = (row >= N_META) & (row < N_META + n_tok)
        e = jnp.where(valid, y - t_ref[...], 0.0)
        loss_ref[...] += 0.5 * jnp.sum(jnp.mean(e * e, axis=-1, keepdims=True), axis=0, keepdims=True)
        dx, dg = _rms_bwd(x, g_ref[...], e * (1.0 / D))
        dh_ref[...] = dx
        dg_ref[...] += dg
        df, dgq = _rms_bwd(f_ref[...], gq_ref[...], 0.5 * dx)
        df_ref[...] = df
        dgq_ref[...] += dgq

    return pl.pallas_call(
        body, name="final_loss", grid=(tp // tm,),
        in_specs=[_rows(tm, D), _full((1, D)), _rows(tm, D), _rows(tm, D), _full((1, D))],
        out_specs=[_rows(tm, D), _rows(tm, D), _full((1, 1)), _full((1, D)), _full((1, D))],
        out_shape=[_out((tp, D), F32), _out((tp, D), F32),
                   _out((1, 1), F32), _out((1, D), F32),
                   _out((1, D), F32)],
        compiler_params=_cp(("arbitrary",), 40),
    )(*_in_hbm(h, g_final, target, f2, g_post2))


def _na_patterns(n_rows):
    pats = []
    for kind in range(3):
        pat = [[-1] * K_ROWS for _ in range(Q_ROWS)]
        for i in range(Q_ROWS):
            for jj in range(K_ROWS):
                if kind == 0 and jj < KH:
                    pat[i][jj] = jj - i + KH - 1
                elif kind == 1 and i <= jj < i + KH:
                    pat[i][jj] = jj - i + 3
                elif kind == 2 and K_ROWS - KH <= jj:
                    pat[i][jj] = jj - i - 1
        pats.append(pat)
    return pats


def _diag_onehot():
    q = np.arange(GRID_W)[:, None]
    kc = np.arange(GRID_W)[None, :]
    start = np.clip(q - KW // 2, 0, GRID_W - KW)
    col_in = (kc >= start) & (kc < start + KW)
    e = np.zeros((32, GRID_W, GRID_W), np.float32)
    for d in range(2 * KW - 1):
        e[d] = ((kc - q + KW - 1) == d) & col_in
    return e.reshape(32, GRID_W * GRID_W), col_in


def _rpb_collapse(dtb2, et):
    def body(d_ref, e_ref, o_ref):
        o_ref[...] = jnp.dot(d_ref[...], e_ref[...], preferred_element_type=F32, precision=lax.Precision.HIGHEST)

    out = (dtb2.shape[0], et.shape[1])
    return pl.pallas_call(
        body, name="rpb_collapse", grid=(1,), out_shape=_out(out, F32),
        in_specs=[_full(dtb2.shape), _full(et.shape)], out_specs=_full(out),
    )(*_in_hbm(dtb2, et))


def _bias_tables(rpb, n_rows, comm=None, bounds=()):
    n_dr, n_dc = 2 * KH - 1, 2 * KW - 1
    pats = _na_patterns(n_rows)

    def body(rpb_ref, o_ref):
        h = pl.program_id(0)
        q = lax.broadcasted_iota(jnp.int32, (GRID_W, GRID_W), 0)
        kc = lax.broadcasted_iota(jnp.int32, (GRID_W, GRID_W), 1)
        start = jnp.clip(q - KW // 2, 0, GRID_W - KW)
        col_in = (kc >= start) & (kc < start + KW)
        diff = kc - q + (KW - 1)
        neg = jnp.full((GRID_W, GRID_W), NEG_INF, F32)
        band = []
        for dr in range(n_dr):
            acc = neg
            for d in range(n_dc):
                acc = jnp.where((diff == d) & col_in, rpb_ref[(h * n_dr + dr) * n_dc + d], acc)
            band.append(acc)
        for kind, pat in enumerate(pats):
            for i in range(Q_ROWS):
                for jj in range(K_ROWS):
                    o_ref[kind, 0, i * GRID_W:(i + 1) * GRID_W, jj * GRID_W:(jj + 1) * GRID_W] = (
                        band[pat[i][jj]] if pat[i][jj] >= 0 else neg)

    (bias,), got = _call(
        body, comm, bounds, (rpb.reshape(-1),), name="bias_tables", grid=(N_HEADS,),
        in_specs=[pl.BlockSpec(memory_space=pltpu.SMEM)],
        out_specs=[pl.BlockSpec((3, 1, QB, KB), lambda h: (0, h, 0, 0))],
        out_shape=[_out((3, N_HEADS, QB, KB), F32)],
        compiler_params=_cp(("arbitrary",), 32))
    return bias, got


def _attn_geometry(n_tok):
    n_rows = n_tok // GRID_W
    assert n_rows % Q_ROWS == 0 and n_rows >= K_ROWS
    return n_rows, n_rows // Q_ROWS


def _attn_probs(qh, kh, kmh, bias, scale):
    s = _dg(qh, kh, NT) * scale + bias
    sm = _dg(qh, kmh, NT) * scale
    m = jnp.maximum(jnp.max(s, axis=-1, keepdims=True), jnp.max(sm, axis=-1, keepdims=True))
    p = jnp.exp(s - m)
    pm = jnp.exp(sm - m)
    inv = 1.0 / (jnp.sum(p, axis=-1, keepdims=True) + jnp.sum(pm, axis=-1, keepdims=True))
    return p * inv, pm * inv


def _meta_probs(qmh, kmh, scale):
    s = _dg(qmh, kmh, NT) * scale
    p = jnp.exp(s - jnp.max(s, axis=-1, keepdims=True))
    return p / jnp.sum(p, axis=-1, keepdims=True)


def _step_rows(r, n_rows):
    q0 = pl.multiple_of(N_META + r * QB, 16)
    k0 = pl.multiple_of(N_META + jnp.clip(Q_ROWS * r - (K_ROWS - KH), 0, n_rows - K_ROWS) * GRID_W, 16)
    return q0, k0


def _attn_fwd(q, k, v, bias, n_tok, comm=None, bounds=()):
    tp = q.shape[0]
    n_rows, n_steps = _attn_geometry(n_tok)
    scale = HEAD_DIM ** -0.5

    def body(q_ref, k_ref, v_ref, b_ref, o_ref):
        r = pl.program_id(1)
        km = k_ref[0:N_META, :]
        vm = v_ref[0:N_META, :]

        @pl.when(r == 0)
        def _():
            qm = q_ref[0:N_META, :]
            outs = []
            for hh in range(2):
                sl = slice(hh * HEAD_DIM, (hh + 1) * HEAD_DIM)
                p = _meta_probs(qm[:, sl], km[:, sl], scale)
                outs.append(_dot(p.astype(BF16), vm[:, sl]))
            o_ref[0:N_META, :] = jnp.concatenate(outs, axis=1)
            o_ref[N_META + n_tok:, :] = jnp.zeros((tp - N_META - n_tok, 2 * HEAD_DIM), F32)

        q0, k0 = _step_rows(r, n_rows)
        qb = q_ref[pl.ds(q0, QB), :]
        kb = k_ref[pl.ds(k0, KB), :]
        vb = v_ref[pl.ds(k0, KB), :]
        outs = []
        for hh in range(2):
            sl = slice(hh * HEAD_DIM, (hh + 1) * HEAD_DIM)
            p, pm = _attn_probs(qb[:, sl], kb[:, sl], km[:, sl], b_ref[0, hh], scale)
            outs.append(_dot(p.astype(BF16), vb[:, sl]) + _dot(pm.astype(BF16), vm[:, sl]))
        o_ref[pl.ds(q0, QB), :] = jnp.concatenate(outs, axis=1)

    def bias_map(hp, r):
        return (jnp.where(r == 0, 0, jnp.where(r == n_steps - 1, 2, 1)), hp, 0, 0)

    col = pl.BlockSpec((tp, 2 * HEAD_DIM), lambda hp, r: (0, hp))
    return _call(
        body, comm, bounds, (q, k, v, bias), name="attn_fwd", grid=(N_HEADS // 2, n_steps),
        in_specs=[col, col, col, pl.BlockSpec((1, 2, QB, KB), bias_map)],
        out_specs=[col], out_shape=[_out((tp, NA_W), F32)],
        compiler_params=_cp(("arbitrary", "arbitrary"), 40))


def _attn_bwd(q, k, v, bias, do, n_tok, comm=None, bounds=()):
    tp = q.shape[0]
    n_rows, n_steps = _attn_geometry(n_tok)
    scale = HEAD_DIM ** -0.5
    pats = _na_patterns(n_rows)

    def body(q_ref, k_ref, v_ref, b_ref, do_ref, dq_ref, dk_ref, dv_ref, dtb_ref):
        r = pl.program_id(1)
        km = k_ref[0:N_META, :]
        vm = v_ref[0:N_META, :]

        @pl.when(r == 0)
        def _():
            dk_ref[...] = jnp.zeros_like(dk_ref)
            dv_ref[...] = jnp.zeros_like(dv_ref)
            dtb_ref[...] = jnp.zeros_like(dtb_ref)
            dq_ref[N_META + n_tok:, :] = jnp.zeros((tp - N_META - n_tok, 2 * HEAD_DIM), F32)
            qm = q_ref[0:N_META, :]
            dom = do_ref[0:N_META, :].astype(BF16)
            dqs, dks, dvs = [], [], []
            for hh in range(2):
                sl = slice(hh * HEAD_DIM, (hh + 1) * HEAD_DIM)
                p = _meta_probs(qm[:, sl], km[:, sl], scale)
                dp = _dg(dom[:, sl], vm[:, sl], NT)
                ds = (p * (dp - jnp.sum(dp * p, axis=-1, keepdims=True))).astype(BF16)
                dvs.append(_dg(p.astype(BF16), dom[:, sl], TN))
                dqs.append(_dot(ds, km[:, sl]) * scale)
                dks.append(_dg(ds, qm[:, sl], TN) * scale)
            dq_ref[0:N_META, :] = jnp.concatenate(dqs, axis=1)
            dk_ref[0:N_META, :] += jnp.concatenate(dks, axis=1)
            dv_ref[0:N_META, :] += jnp.concatenate(dvs, axis=1)

        q0, k0 = _step_rows(r, n_rows)
        qb = q_ref[pl.ds(q0, QB), :]
        kb = k_ref[pl.ds(k0, KB), :]
        vb = v_ref[pl.ds(k0, KB), :]
        dob = do_ref[pl.ds(q0, QB), :].astype(BF16)
        dqs, dks, dvs, dkms, dvms, dss = [], [], [], [], [], []
        for hh in range(2):
            sl = slice(hh * HEAD_DIM, (hh + 1) * HEAD_DIM)
            qh, kh, vh, kmh, vmh, doh = qb[:, sl], kb[:, sl], vb[:, sl], km[:, sl], vm[:, sl], dob[:, sl]
            p, pm = _attn_probs(qh, kh, kmh, b_ref[0, hh], scale)
            dp = _dg(doh, vh, NT)
            dpm = _dg(doh, vmh, NT)
            delta = jnp.sum(dp * p, axis=-1, keepdims=True) + jnp.sum(dpm * pm, axis=-1, keepdims=True)
            ds = p * (dp - delta)
            dsb = ds.astype(BF16)
            dsmb = (pm * (dpm - delta)).astype(BF16)
            dss.append(ds)
            dvs.append(_dg(p.astype(BF16), doh, TN))
            dvms.append(_dg(pm.astype(BF16), doh, TN))
            dqs.append((_dot(dsb, kh) + _dot(dsmb, kmh)) * scale)
            dks.append(_dg(dsb, qh, TN) * scale)
            dkms.append(_dg(dsmb, qh, TN) * scale)
        dq_ref[pl.ds(q0, QB), :] = jnp.concatenate(dqs, axis=1)
        dk_ref[pl.ds(k0, KB), :] += jnp.concatenate(dks, axis=1)
        dv_ref[pl.ds(k0, KB), :] += jnp.concatenate(dvs, axis=1)
        dk_ref[0:N_META, :] += jnp.concatenate(dkms, axis=1)
        dv_ref[0:N_META, :] += jnp.concatenate(dvms, axis=1)

        def add_bias_grad(pat):
            for hh in range(2):
                for i in range(Q_ROWS):
                    for jj in range(K_ROWS):
                        if pat[i][jj] >= 0:
                            dtb_ref[hh, pat[i][jj]] += dss[hh][i * GRID_W:(i + 1) * GRID_W,
                                                               jj * GRID_W:(jj + 1) * GRID_W]

        @pl.when(r == 0)
        def _():
            add_bias_grad(pats[0])

        @pl.when((r > 0) & (r < n_steps - 1))
        def _():
            add_bias_grad(pats[1])

        @pl.when(r == n_steps - 1)
        def _():
            add_bias_grad(pats[2])

    def bias_map(hp, r):
        return (jnp.where(r == 0, 0, jnp.where(r == n_steps - 1, 2, 1)), hp, 0, 0)

    col = pl.BlockSpec((tp, 2 * HEAD_DIM), lambda hp, r: (0, hp))
    n_dr = 2 * KH - 1
    return _call(
        body, comm, bounds, (q, k, v, bias, do), name="attn_bwd", grid=(N_HEADS // 2, n_steps),
        in_specs=[col, col, col, pl.BlockSpec((1, 2, QB, KB), bias_map), col],
        out_specs=[col, col, col, pl.BlockSpec((2, n_dr, GRID_W, GRID_W), lambda hp, r: (hp, 0, 0, 0))],
        out_shape=[_out((tp, NA_W), F32)] * 3 +
                  [_out((N_HEADS, n_dr, GRID_W, GRID_W), F32)],
        compiler_params=_cp(("arbitrary", "arbitrary"), 48))


def _repeat_onehot():
    return np.repeat(np.eye(2 * S5_G, dtype=np.float32), S5_H, axis=0)


def _s5_disc_math(lam_re, lam_im, log_dt, b_re, b_im, rep):
    dt = jnp.exp(log_dt)
    ea = jnp.exp(lam_re * dt)
    a_re = ea * jnp.cos(lam_im * dt)
    a_im = ea * jnp.sin(lam_im * dt)
    den = lam_re * lam_re + lam_im * lam_im
    c_re = ((a_re - 1.0) * lam_re + a_im * lam_im) / den
    c_im = (a_im * lam_re - (a_re - 1.0) * lam_im) / den
    ce_re = jnp.dot(rep, c_re, preferred_element_type=F32, precision=lax.Precision.HIGHEST)
    ce_im = jnp.dot(rep, c_im, preferred_element_type=F32, precision=lax.Precision.HIGHEST)
    return a_re, a_im, ce_re * b_re - ce_im * b_im, ce_re * b_im + ce_im * b_re


def _s5_blocks():
    gl = S5_G // N_BUNDLE
    half = gl * S5_P
    out = []
    for d in range(2):
        for g in range(S5_G):
            b, k = divmod(g, gl)
            dg = d * S5_G + g
            out.append((d, b, slice(k * S5_H, (k + 1) * S5_H), slice(k * S5_P, (k + 1) * S5_P),
                        slice(half + k * S5_P, half + (k + 1) * S5_P), slice(dg * S5_H, (dg + 1) * S5_H),
                        slice(dg, dg + 1)))
    return out


def _s5_params(lam_re, lam_im, log_dt, b_re, b_im, c_re, c_im):
    cw, sw = S5_W // N_BUNDLE, 2 * (S5_G // N_BUNDLE) * S5_P

    def body(lr, li, ld, br, bi, cr, ci, rep_ref, a_ref, a1_ref, a2_ref, bm_ref, cm_ref):
        a_re, a_im, bb_re, bb_im = _s5_disc_math(lr[...], li[...], ld[...], br[...], bi[...], rep_ref[...])
        cc_re = cr[...]
        cc_im = ci[...]
        bm_ref[...] = jnp.zeros_like(bm_ref)
        cm_ref[...] = jnp.zeros_like(cm_ref)
        for d, b, rows, re, im, nat, one in _s5_blocks():
            bm_ref[d, b, rows, re] = bb_re[nat, :].astype(BF16)
            bm_ref[d, b, rows, im] = bb_im[nat, :].astype(BF16)
            cm_ref[d, b, rows, re] = cc_re[nat, :].astype(BF16)
            cm_ref[d, b, rows, im] = (-cc_im[nat, :]).astype(BF16)
            a_ref[d, b, :, re] = a_re[one, :]
            a_ref[d, b, :, im] = a_im[one, :]
            k = rows.start // S5_H
            lanes = slice((k % 2) * S5_P, (k % 2 + 1) * S5_P)
            for part, (v1, v2) in enumerate(((a_re[one, :], a_im[one, :]), (a_re[one, :], -a_im[one, :]))):
                sub = slice(4 * part + k // 2, 4 * part + k // 2 + 1)
                a1_ref[d, b, sub, lanes] = v1
                a2_ref[d, b, sub, lanes] = v2

    args = (lam_re, lam_im, log_dt, b_re, b_im, c_re, c_im, jnp.asarray(_repeat_onehot()))
    outs = [((2, N_BUNDLE, 1, sw), F32)] + [((2, N_BUNDLE, 8, 128), F32)] * 2 + [((2, N_BUNDLE, cw, sw), BF16)] * 2
    return pl.pallas_call(
        body, name="s5_params", grid=(1,), in_specs=[_full(a.shape) for a in args],
        out_specs=[_full(s) for s, _ in outs], out_shape=[_out(s, dt) for s, dt in outs],
    )(*_in_hbm(*args))


def _s5_params_bwd(lam_re, lam_im, log_dt, b_re, b_im, da, dbm, dcm):
    n, nb = 2 * S5_G, 2 * S5_G * S5_H

    def body(lr, li, ld, br, bi, rep_ref, da_ref, dbm_ref, dcm_ref, o_lr, o_li, o_ld, o_br, o_bi, o_cr, o_ci,
             dar_s, dai_s, dbr_s, dbi_s):
        for d, b, rows, re, im, nat, one in _s5_blocks():
            dbr_s[nat, :] = dbm_ref[d, b, rows, re]
            dbi_s[nat, :] = dbm_ref[d, b, rows, im]
            o_cr[nat, :] = dcm_ref[d, b, rows, re]
            o_ci[nat, :] = -dcm_ref[d, b, rows, im]
            dar_s[one, :] = da_ref[d, b, :, re]
            dai_s[one, :] = da_ref[d, b, :, im]
        rep = rep_ref[...]
        _, vjp = jax.vjp(lambda p, q, r, s, t: _s5_disc_math(p, q, r, s, t, rep),
                         lr[...], li[...], ld[...], br[...], bi[...])
        o_lr[...], o_li[...], o_ld[...], o_br[...], o_bi[...] = vjp((dar_s[...], dai_s[...], dbr_s[...], dbi_s[...]))

    args = (lam_re, lam_im, log_dt, b_re, b_im, jnp.asarray(_repeat_onehot()), da, dbm, dcm)
    outs = [(n, S5_P)] * 2 + [(n, 1)] + [(nb, S5_P)] * 4
    return pl.pallas_call(
        body, name="s5_params_bwd", grid=(1,), in_specs=[_full(a.shape) for a in args],
        out_specs=[_full(s) for s in outs], out_shape=[_out(s, F32) for s in outs],
        scratch_shapes=[pltpu.VMEM((n, S5_P), F32)] * 2 + [pltpu.VMEM((nb, S5_P), F32)] * 2,
    )(*_in_hbm(*args))


def _tiles_store(ref, base, val):
    for i in range(val.shape[0] // 8):
        for c in range(8):
            ref[pl.ds(base + (8 * i + c) * 8, 8), :] = val[8 * i:8 * i + 8, 128 * c:128 * (c + 1)]


def _tiles_load(ref, base, n):
    return jnp.concatenate(
        [jnp.concatenate([ref[pl.ds(base + (8 * i + c) * 8, 8), :] for c in range(8)], axis=1) for i in range(n // 8)],
        axis=0)


def _time_rows(base, t):
    return pl.ds(base + (t // 8) * 64 + t % 8, 8, stride=8)


def _scan(chains, n):
    xs = [c["x"] for c in chains]
    for k in range(n):
        for ci, c in enumerate(chains):
            t = n - 1 - k if c["reverse"] else k
            if c["prev"] is not None:
                c["prev"][_time_rows(c["prev_base"], t), :] = xs[ci]
            xs[ci] = c["a1"] * xs[ci] + pltpu.roll(c["a2"] * xs[ci], 4, axis=0) + c["src"][_time_rows(0, t), :]
            if c["dst"] is not None:
                c["dst"][_time_rows(0, t), :] = xs[ci]
    return xs


def _chain(x, a1, a2, src, dst=None, prev=None, prev_base=0, reverse=False):
    return dict(x=x, a1=a1, a2=a2, src=src, dst=dst, prev=prev, prev_base=prev_base, reverse=reverse)


def _s5_fwd(u, d_skip, a1, a2, bm, cm, length, comm=None, bounds=()):
    tp = u.shape[0]
    cw = S5_W // N_BUNDLE
    sw = bm.shape[-1]
    n_full, n_tail = divmod(length, SCAN_CHUNK)
    t_tail = n_full * SCAN_CHUNK

    nbs = N_BUNDLE

    def body(u_ref, d_ref, a1_ref, a2_ref, bm_ref, cm_ref, y_ref, bnd_ref, *scratch):
        y_ref[...] = u_ref[...] * d_ref[...]
        ins, xss = (scratch[0:nbs], scratch[nbs:2 * nbs]), (scratch[2 * nbs:3 * nbs], scratch[3 * nbs:])
        cols = [slice(b * cw, (b + 1) * cw) for b in range(nbs)]

        def keep(dr, chunk, xs):
            for b in range(nbs):
                bnd_ref[dr, b, chunk] = xs[b]

        def load(dr, t0, n):
            for b in range(nbs):
                _tiles_store(ins[dr][b], 0, _dot(u_ref[pl.ds(t0, n), cols[b]].astype(BF16), bm_ref[dr, b]))

        def chains(dr, xs):
            return [_chain(xs[b], a1_ref[dr, b], a2_ref[dr, b], ins[dr][b], dst=xss[dr][b], reverse=dr == 1)
                    for b in range(nbs)]

        def emit(dr, t0, n):
            for b in range(nbs):
                y_ref[pl.ds(t0, n), cols[b]] += _dg(_tiles_load(xss[dr][b], 0, n).astype(BF16), cm_ref[dr, b], NT)

        zero = (jnp.zeros((8, 128), F32),) * nbs
        xb = zero
        if n_tail:
            keep(1, n_full, xb)
            load(1, t_tail, n_tail)
            xb = tuple(_scan(chains(1, xb), n_tail))
            emit(1, t_tail, n_tail)

        def pair(i, carry):
            j = n_full - 1 - i
            t0s = (pl.multiple_of(i * SCAN_CHUNK, SCAN_CHUNK), pl.multiple_of(j * SCAN_CHUNK, SCAN_CHUNK))
            keep(0, i, carry[0])
            keep(1, j, carry[1])
            for dr in range(2):
                load(dr, t0s[dr], SCAN_CHUNK)
            out = _scan(chains(0, carry[0]) + chains(1, carry[1]), SCAN_CHUNK)
            for dr in range(2):
                emit(dr, t0s[dr], SCAN_CHUNK)
            return tuple(out[:nbs]), tuple(out[nbs:])

        xf, _ = lax.fori_loop(0, n_full, pair, (zero, xb))
        if n_tail:
            keep(0, n_full, xf)
            load(0, t_tail, n_tail)
            _scan(chains(0, xf), n_tail)
            emit(0, t_tail, n_tail)

    n_chunks = n_full + (1 if n_tail else 0)
    tile = pl.BlockSpec((2, nbs, 8, 128), lambda b: (0, b, 0, 0))
    return _call(
        body, comm, bounds, (u, d_skip, a1, a2, bm, cm), name="s5_fwd", grid=(N_BUNDLE // nbs,),
        in_specs=[pl.BlockSpec((tp, nbs * cw), lambda b: (0, b)), pl.BlockSpec((1, nbs * cw), lambda b: (0, b)),
                  tile, tile, pl.BlockSpec((2, nbs, cw, sw), lambda b: (0, b, 0, 0)),
                  pl.BlockSpec((2, nbs, cw, sw), lambda b: (0, b, 0, 0))],
        out_specs=[pl.BlockSpec((tp, nbs * cw), lambda b: (0, b)),
                   pl.BlockSpec((2, nbs, n_chunks, 8, 128), lambda b: (0, b, 0, 0, 0))],
        out_shape=[_out((tp, S5_W), F32), _out((2, N_BUNDLE, n_chunks, 8, 128), F32)],
        scratch_shapes=[pltpu.VMEM((SCAN_CHUNK * 8, 128), F32)] * (4 * nbs),
        compiler_params=_cp(("arbitrary",), 48))


def _s5_bwd(u, dy, d_skip, a, a1, a2, bm, cm, bnd, length):
    tp = u.shape[0]
    cw = S5_W // N_BUNDLE
    sw = bm.shape[-1]
    half = sw // 2
    n_full, n_tail = divmod(length, SCAN_CHUNK)
    t_tail = n_full * SCAN_CHUNK
    n_chunks = bnd.shape[2]
    nbs = 2

    def body(u_ref, dy_ref, d_ref, a_ref, a1_ref, a2_ref, bm_ref, cm_ref, bnd_ref, du_ref, dd_ref, dbm_ref, dcm_ref,
             da_ref, *scratch):
        du_ref[...] = dy_ref[...] * d_ref[...]
        dd_ref[...] = jnp.sum(dy_ref[...] * u_ref[...], axis=0, keepdims=True)
        dbm_ref[...] = jnp.zeros_like(dbm_ref)
        dcm_ref[...] = jnp.zeros_like(dcm_ref)
        da_ref[...] = jnp.zeros_like(da_ref)
        bu_s, dx_s, g_s, xp_s = ([scratch[(k * 2 + dr) * nbs:(k * 2 + dr + 1) * nbs] for dr in range(2)] for k in range(4))
        cols = [slice(b * cw, (b + 1) * cw) for b in range(nbs)]

        def chains(dr, chunk, t0, n, gs):
            out = []
            for b in range(nbs):
                _tiles_store(bu_s[dr][b], 0, _dot(u_ref[pl.ds(t0, n), cols[b]].astype(BF16), bm_ref[dr, b]))
                _tiles_store(dx_s[dr][b], 0, _dot(dy_ref[pl.ds(t0, n), cols[b]].astype(BF16), cm_ref[dr, b]))
                out.append(_chain(bnd_ref[dr, b, chunk], a1_ref[dr, b], a2_ref[dr, b], bu_s[dr][b],
                                  prev=xp_s[dr][b], reverse=dr == 1))
                out.append(_chain(gs[b], a1_ref[dr, b], -a2_ref[dr, b], dx_s[dr][b], dst=g_s[dr][b], reverse=dr == 0))
            return out

        def emit(dr, t0, n):
            rows = pl.ds(t0, n)
            for b in range(nbs):
                ub = u_ref[rows, cols[b]].astype(BF16)
                dyb = dy_ref[rows, cols[b]].astype(BF16)
                g = _tiles_load(g_s[dr][b], 0, n)
                gb = g.astype(BF16)
                du_ref[rows, cols[b]] += _dg(gb, bm_ref[dr, b], NT)
                dbm_ref[dr, b] += _dg(ub, gb, TN)
                xp = _tiles_load(xp_s[dr][b], 0, n)
                xp_r, xp_i = xp[:, 0:half], xp[:, half:]
                g_r, g_i = g[:, 0:half], g[:, half:]
                a_re = a_ref[dr, b, :, 0:half]
                a_im = a_ref[dr, b, :, half:]
                bu = _dot(ub, bm_ref[dr, b])
                x_r = a_re * xp_r - a_im * xp_i + bu[:, 0:half]
                x_i = a_re * xp_i + a_im * xp_r + bu[:, half:]
                dcm_ref[dr, b] += _dg(dyb, jnp.concatenate([x_r, x_i], axis=1).astype(BF16), TN)
                da_ref[dr, b] += jnp.concatenate([jnp.sum(g_r * xp_r + g_i * xp_i, axis=0, keepdims=True),
                                                  jnp.sum(g_i * xp_r - g_r * xp_i, axis=0, keepdims=True)], axis=1)

        def adjoints(out):
            return tuple(out[1::2])

        zero = (jnp.zeros((8, 128), F32),) * nbs
        g0 = zero
        if n_tail:
            g0 = adjoints(_scan(chains(0, n_full, t_tail, n_tail, g0), n_tail))
            emit(0, t_tail, n_tail)

        def pair(i, carry):
            j = n_full - 1 - i
            t0 = (pl.multiple_of(j * SCAN_CHUNK, SCAN_CHUNK), pl.multiple_of(i * SCAN_CHUNK, SCAN_CHUNK))
            both = chains(0, j, t0[0], SCAN_CHUNK, carry[0]) + chains(1, i, t0[1], SCAN_CHUNK, carry[1])
            out = _scan(both, SCAN_CHUNK)
            emit(0, t0[0], SCAN_CHUNK)
            emit(1, t0[1], SCAN_CHUNK)
            return adjoints(out[:2 * nbs]), adjoints(out[2 * nbs:])

        _, g1 = lax.fori_loop(0, n_full, pair, (g0, zero))
        if n_tail:
            _scan(chains(1, n_full, t_tail, n_tail, g1), n_tail)
            emit(1, t_tail, n_tail)

    tile = pl.BlockSpec((2, nbs, 8, 128), lambda b: (0, b, 0, 0))
    wide = pl.BlockSpec((2, nbs, cw, sw), lambda b: (0, b, 0, 0))
    col = pl.BlockSpec((tp, nbs * cw), lambda b: (0, b))
    row = pl.BlockSpec((1, nbs * cw), lambda b: (0, b))
    arow = pl.BlockSpec((2, nbs, 1, sw), lambda b: (0, b, 0, 0))
    return pl.pallas_call(
        body, name="s5_bwd", grid=(N_BUNDLE // nbs,),
        in_specs=[col, col, row, arow, tile, tile, wide, wide,
                  pl.BlockSpec((2, nbs, n_chunks, 8, 128), lambda b: (0, b, 0, 0, 0))],
        out_specs=[col, row, wide, wide, arow],
        out_shape=[_out((tp, S5_W), F32), _out((1, S5_W), F32),
                   _out((2, N_BUNDLE, cw, sw), F32), _out((2, N_BUNDLE, cw, sw), F32),
                   _out((2, N_BUNDLE, 1, sw), F32)],
        scratch_shapes=[pltpu.VMEM((SCAN_CHUNK * 8, 128), F32)] * (8 * nbs),
        compiler_params=_cp(("arbitrary",), 56),
    )(*_in_hbm(u, dy, d_skip, a, a1, a2, bm, cm, bnd))


def _row_tile(tp):
    return max(tm for tm in range(16, 449, 16) if tp % tm == 0)


def _step(x, target, bufs, gains, s5, rpb, c_arr, kc_arr, me_arr):
    n_tok = x.shape[0]
    first = ["ffn1_w_gate", "ffn1_w_up", "ffn1_w_down", "meta_tokens"]
    bias, got = _bias_tables(rpb, n_tok // GRID_W, _gather_comm([bufs[n] for n in first]), (0, N_HEADS - 1))
    w = dict(zip(first, got))
    meta = w["meta_tokens"].transpose(1, 0, 2).reshape(N_META, D)
    length = N_META + n_tok
    tp = length + 16
    tm = _row_tile(tp)
    tmb = tm
    n_rows = n_tok // GRID_W
    pad = jnp.zeros((tp - length, D), F32)
    h0 = jnp.concatenate([meta, x, pad], axis=0)
    tgt = jnp.concatenate([jnp.zeros((N_META, D), F32), target, pad], axis=0)

    s5p = (s5["lam_re"], s5["lam_im"], s5["log_dt"].reshape(2 * S5_G, 1), s5["b_re"], s5["b_im"])
    a_m, a1_m, a2_m, bm16, cm16 = _s5_params(*s5p, s5["c_re"], s5["c_im"])

    mid = ["w_in", "s5_w_glu", "w_out"]
    (h1, gate1, up1, f1), got = _ffn_fwd(
        "ffn1_fwd", h0, gains["ffn1_pre_g"], gains["ffn1_post_g"], w["ffn1_w_gate"], w["ffn1_w_up"], w["ffn1_w_down"],
        tm, _gather_comm([bufs[n] for n in mid]), (0, (tp // tm) * N_CHIP * 3 // 5))
    w.update(zip(mid, got))
    q, k, v, u = _mix_in(h1, gains["mix_pre_g"], w["w_in"], tm)
    (o_na,), (gate_ici, up_ici) = _attn_fwd(
        q, k, v, bias, n_tok, _gather_comm([bufs["ffn2_w_gate"], bufs["ffn2_w_up"]], pair=False), (0,))
    (y_pre, s5_bnd), (w["ffn2_w_gate"], w["ffn2_w_up"], down_ici) = _s5_fwd(
        u, gains["s5_d"], a1_m, a2_m, bm16, cm16, length,
        _merge_comm(_gather_comm([gate_ici, up_ici], ici=False),
                    _gather_comm([bufs["ffn2_w_down"]], pair=False)), (0,))
    w_glu = w["s5_w_glu"].reshape(S5_W, S5_W)
    w_out = w["w_out"].reshape(D, D)
    (h2, mix), (w["ffn2_w_down"],) = _mix_out(
        o_na, y_pre, h1, w_glu, gains["s5_b_glu"], gains["na_out_g"], gains["s5_out_g"], w_out, gains["mix_post_g"], tm,
        _gather_comm([down_ici], ici=False), (0,))
    (h3, gate2, up2, f2), _ = _ffn_fwd("ffn2_fwd", h2, gains["ffn2_pre_g"], gains["ffn2_post_g"],
                                       w["ffn2_w_gate"], w["ffn2_w_up"], w["ffn2_w_down"], tm)
    dh3, df2, loss, dg_final, dg_post2 = _final_loss(h3, gains["final_g"], tgt, f2, gains["ffn2_post_g"], n_tok, tm)

    ffn2 = ["ffn2_w_gate", "ffn2_w_up", "ffn2_w_down"]
    ffn1 = ["ffn1_w_gate", "ffn1_w_up", "ffn1_w_down"]
    out2, _ = _ffn_bwd("ffn2_bwd", h2, gains["ffn2_pre_g"], df2, gate2, up2,
                       w["ffn2_w_gate"], w["ffn2_w_up"], w["ffn2_w_down"], tmb)
    dxn2 = out2[3]
    sums2 = [_chip_sum("chip_sum_" + n, g, r, c_arr) for n, g, r in zip(ffn2, out2[0:3], out2[4:7])]
    (dh2, dg_pre2), _ = _ffn_pre_bwd("ffn2_pre_bwd", dh3, dxn2, h2, gains["ffn2_pre_g"], tm)
    do_na, dy_pre, dw_out, dw_glu, dg_mpost, dg_na, dg_s5, db_glu = _mix_out_bwd(
        dh2, mix, o_na, y_pre, w_glu, gains["s5_b_glu"], gains["na_out_g"], gains["s5_out_g"], w_out,
        gains["mix_post_g"], tm)
    (dq, dk, dv, dtb), recv3 = _attn_bwd(q, k, v, bias, do_na, n_tok, _scatter_comm(sums2), (0,))
    totals2 = [_total_sum("total_sum_" + n, s, r, kc_arr) for n, s, r in zip(ffn2, sums2, recv3)]
    du, dd, dbm, dcm, da_m = _s5_bwd(u, dy_pre, gains["s5_d"], a_m, a1_m, a2_m, bm16, cm16, s5_bnd, length)
    (dh1, df1, dw_in, dg_mpre, dg_post1), done2 = _mix_in_bwd(
        dq, dk, dv, du, h1, gains["mix_pre_g"], w["w_in"], dh2, f1, gains["ffn1_post_g"], tm,
        _assemble_comm(totals2), (0,))
    pieces = dict(zip(ffn2, done2))

    e, _ = _diag_onehot()
    n_dr = 2 * KH - 1
    drpb = _rpb_collapse(dtb.reshape(N_HEADS * n_dr, GRID_W * GRID_W), jnp.asarray(e.T))
    drpb = drpb[:, :2 * KW - 1].reshape(N_HEADS, n_dr, 2 * KW - 1).transpose(1, 0, 2).reshape(N_HEADS * n_dr, 2 * KW - 1)
    dlam_re, dlam_im, dlog_dt, db_re, db_im, dc_re, dc_im = _s5_params_bwd(*s5p, da_m, dbm, dcm)
    early = {"ffn1_post_g": dg_post1, "mix_pre_g": dg_mpre, "na_rpb": drpb,
             "s5_lam_re": dlam_re, "s5_lam_im": dlam_im, "s5_log_dt": dlog_dt.reshape(2, S5_G),
             "s5_b_re": db_re, "s5_b_im": db_im, "s5_c_re": dc_re, "s5_c_im": dc_im,
             "s5_d": dd, "s5_b_glu": db_glu, "na_out_g": dg_na,
             "s5_out_g": dg_s5, "mix_post_g": dg_mpost, "ffn2_pre_g": dg_pre2, "ffn2_post_g": dg_post2,
             "final_g": dg_final}
    names = list(early)
    slots = _small_pack([early[n] for n in names], me_arr)

    out1, slots = _ffn_bwd("ffn1_bwd", h0, gains["ffn1_pre_g"], df1, gate1, up1,
                           w["ffn1_w_gate"], w["ffn1_w_up"], w["ffn1_w_down"], tmb, _spread_comm(slots), (0,))
    small = dict(zip(names, _small_total(slots, [early[n].shape for n in names])))
    sums1 = [_chip_sum("chip_sum_" + n, g, r, c_arr) for n, g, r in zip(ffn1, out1[0:3], out1[4:7])]
    flight = _scatter_start(sums1)
    token = flight[4]
    rest = [dw_in, dw_glu.reshape(N_CHIP, S5_W // N_CHIP, S5_W), dw_out.reshape(N_CHIP, D // N_CHIP, D)]
    (dh0, dg_pre1), recv_rest = _ffn_pre_bwd("ffn1_pre_bwd", dh1, out1[3], h0, gains["ffn1_pre_g"] + token[0:1, 0:1],
                                             tm, _exchange_comm(rest), (0,))
    sums = [_chip_sum("chip_sum_" + n, g, r, c_arr) for n, g, r in zip(mid, rest, recv_rest)]
    return loss[0, 0], dh0, pieces, small, {"ffn1_pre_g": dg_pre1}, (ffn1, flight[:4]), (mid, sums)


def _mesh_pos():
    return lax.axis_index("x"), lax.axis_index("y"), lax.axis_index("c")


def _other_chips(x, y):
    return [(1 - x, y), (x, 1 - y), (1 - x, 1 - y)]


class _Comm:
    def __init__(self, ins, out_shape, aliases, parts):
        self.ins, self.out_shape, self.aliases, self.parts = list(ins), list(out_shape), dict(aliases), list(parts)
        self.n_sems = sum(p[0] for p in parts)

    def bases(self):
        out, base = [], 0
        for n_sems, _, _ in self.parts:
            out.append(base)
            base += n_sems
        return out


def _run_comm(name, comm):
    n_i, n_o = len(comm.ins), len(comm.out_shape)

    def body(*refs):
        ins, outs = refs[:n_i], refs[n_i:n_i + n_o]
        send_sems, recv_sems = refs[n_i + n_o:]
        for base, (_, start, finish) in zip(comm.bases(), comm.parts):
            start(ins, outs, send_sems, recv_sems, base)
            finish(ins, outs, send_sems, recv_sems, base)

    return pl.pallas_call(
        body, name=name, out_shape=comm.out_shape, in_specs=[ANY] * n_i, out_specs=[ANY] * n_o,
        input_output_aliases=comm.aliases,
        scratch_shapes=[pltpu.SemaphoreType.DMA((comm.n_sems,)), pltpu.SemaphoreType.DMA((comm.n_sems,))],
    )(*_in_hbm(*comm.ins))


def _call(body, comm, bounds, args, *, name, grid, in_specs, out_specs, out_shape, scratch_shapes=(),
          compiler_params=None):
    in_specs, out_specs, out_shape, scratch_shapes = list(in_specs), list(out_specs), list(out_shape), list(scratch_shapes)
    if comm is None:
        return pl.pallas_call(body, name=name, grid=grid, in_specs=in_specs, out_specs=out_specs, out_shape=out_shape,
                              scratch_shapes=scratch_shapes, compiler_params=compiler_params)(*_in_hbm(*args)), []
    n_in, n_out, n_scr = len(in_specs), len(out_specs), len(scratch_shapes)
    n_ci, n_co = len(comm.ins), len(comm.out_shape)
    n_steps = int(np.prod(grid))
    assert len(bounds) == len(comm.parts) and all(0 <= b < n_steps for b in bounds) and list(bounds) == sorted(bounds)

    def fused(*refs):
        a = n_in
        b = a + n_ci
        c = b + n_out
        d = c + n_co
        e = d + n_scr
        cargs = (refs[a:b], refs[c:d], refs[e], refs[e + 1])
        step = pl.program_id(0)
        for ax in range(1, len(grid)):
            step = step * grid[ax] + pl.program_id(ax)
        bases = comm.bases()
        for p, (_, start, finish) in enumerate(comm.parts):
            @pl.when(step == bounds[p])
            def _(p=p, start=start):
                if p > 0:
                    comm.parts[p - 1][2](*cargs, bases[p - 1])
                start(*cargs, bases[p])
        body(*(refs[:a] + refs[b:c] + refs[d:e]))

        @pl.when(step == n_steps - 1)
        def _():
            comm.parts[-1][2](*cargs, bases[-1])

    res = pl.pallas_call(
        fused, name=name, grid=grid, in_specs=in_specs + [ANY] * n_ci, out_specs=out_specs + [ANY] * n_co,
        out_shape=out_shape + comm.out_shape,
        scratch_shapes=scratch_shapes + [pltpu.SemaphoreType.DMA((comm.n_sems,)), pltpu.SemaphoreType.DMA((comm.n_sems,))],
        input_output_aliases={n_in + i: n_out + j for i, j in comm.aliases.items()},
        compiler_params=compiler_params)(*_in_hbm(*args, *comm.ins))
    return res[:n_out], res[n_out:]


def _remote(src, dst, send_sems, recv_sems, idx, to):
    return pltpu.make_async_remote_copy(src_ref=src, dst_ref=dst, send_sem=send_sems.at[idx],
                                        recv_sem=recv_sems.at[idx], device_id=to, device_id_type=MESH_ID)


def _gather_comm(bufs, ici=True, pair=True):
    n = len(bufs)

    def half(ref, k, pc):
        rh = ref.shape[1] // 2
        return ref.at[k, pl.ds(pc * rh, rh), :]

    def ici_start(ins, outs, ss, rs, base):
        x, y, c = _mesh_pos()
        for a in range(n):
            mine = half(outs[a], 2 * x + y, c)
            for j, chip in enumerate(_other_chips(x, y)):
                _remote(mine, mine, ss, rs, base + 3 * a + j, (*chip, c)).start()

    def ici_finish(ins, outs, ss, rs, base):
        x, y, c = _mesh_pos()
        for a in range(n):
            for j, chip in enumerate(_other_chips(x, y)):
                theirs = half(outs[a], 2 * chip[0] + chip[1], c)
                _remote(theirs, theirs, ss, rs, base + 3 * a + j, (*chip, c)).wait()

    def pair_copy(outs, ss, rs, base, a):
        x, y, c = _mesh_pos()
        rh = outs[a].shape[1] // 2
        held = outs[a].at[:, pl.ds(c * rh, rh), :]
        return _remote(held, held, ss, rs, base + a, (x, y, 1 - c))

    def pair_start(ins, outs, ss, rs, base):
        for a in range(n):
            pair_copy(outs, ss, rs, base, a).start()

    def pair_finish(ins, outs, ss, rs, base):
        for a in range(n):
            pair_copy(outs, ss, rs, base, a).wait()

    parts = ([(3 * n, ici_start, ici_finish)] if ici else []) + ([(n, pair_start, pair_finish)] if pair else [])
    return _Comm(bufs, [_out(b.shape, b.dtype) for b in bufs], {a: a for a in range(n)}, parts)


def _merge_comm(*comms):
    ins, shapes, aliases, subs, base = [], [], {}, [], 0
    for cm in comms:
        (n_sems, start, finish), = cm.parts
        i0, o0 = len(ins), len(shapes)
        subs.append((slice(i0, i0 + len(cm.ins)), slice(o0, o0 + len(cm.out_shape)), base, start, finish))
        aliases.update({i0 + i: o0 + j for i, j in cm.aliases.items()})
        ins += cm.ins
        shapes += cm.out_shape
        base += n_sems

    def start_all(ins_r, outs_r, ss, rs, b):
        for si, so, off, start, _ in subs:
            start(ins_r[si], outs_r[so], ss, rs, b + off)

    def finish_all(ins_r, outs_r, ss, rs, b):
        for si, so, off, _, finish in subs:
            finish(ins_r[si], outs_r[so], ss, rs, b + off)

    return _Comm(ins, shapes, aliases, [(base, start_all, finish_all)])


def _own_half_buffers(pieces, dtypes, kc_arr):
    n = len(pieces)

    def body(kc_ref, *refs):
        for a in range(n):
            refs[n + a][0] = refs[a][...].astype(dtypes[a])

    def half(p):
        return p.shape[0] // 2, p.shape[1]

    return pl.pallas_call(
        body, name="own_halves",
        out_shape=[_out((N_CHIP,) + p.shape, dt) for p, dt in zip(pieces, dtypes)],
        grid_spec=pltpu.PrefetchScalarGridSpec(
            num_scalar_prefetch=1, grid=(1,),
            in_specs=[pl.BlockSpec(half(p), lambda i, kc: (kc[1], 0)) for p in pieces],
            out_specs=[pl.BlockSpec((1,) + half(p), lambda i, kc: (kc[0], kc[1], 0)) for p in pieces]),
        compiler_params=_cp(("arbitrary",), 48),
    )(kc_arr, *_in_hbm(*pieces))


def _exchange_comm(grads):
    n = len(grads)

    def copy(ins, outs, ss, rs, base, a):
        x, y, c = _mesh_pos()
        rh = ins[a].shape[1] // 2
        return _remote(ins[a].at[:, pl.ds((1 - c) * rh, rh), :], outs[a], ss, rs, base + a, (x, y, 1 - c))

    def start(ins, outs, ss, rs, base):
        for a in range(n):
            copy(ins, outs, ss, rs, base, a).start()

    def finish(ins, outs, ss, rs, base):
        for a in range(n):
            copy(ins, outs, ss, rs, base, a).wait()

    shapes = [_out((N_CHIP, g.shape[1] // 2, g.shape[2]), g.dtype) for g in grads]
    return _Comm(grads, shapes, {}, [(n, start, finish)])


def _chip_sum(name, g, recv, c_arr):
    _, r, cc = g.shape
    rh = r // 2

    def body(c_ref, g_ref, r_ref, o_ref):
        o_ref[...] = (g_ref[...] + r_ref[...]).astype(BF16)

    return pl.pallas_call(
        body, name=name, out_shape=_out((N_CHIP, rh, cc), BF16),
        grid_spec=pltpu.PrefetchScalarGridSpec(
            num_scalar_prefetch=1, grid=(N_CHIP,),
            in_specs=[pl.BlockSpec((1, rh, cc), lambda j, c_ref: (j, c_ref[0], 0)),
                      pl.BlockSpec((1, rh, cc), lambda j, c_ref: (j, 0, 0))],
            out_specs=pl.BlockSpec((1, rh, cc), lambda j, c_ref: (j, 0, 0))),
        compiler_params=_cp(("arbitrary",), 32),
    )(c_arr, *_in_hbm(g, recv))


def _scatter_comm(sums):
    n = len(sums)

    def copies(ins, outs, ss, rs, base):
        x, y, c = _mesh_pos()
        return [_remote(ins[a].at[2 * chip[0] + chip[1]], outs[a].at[j], ss, rs, base + 3 * a + j, (*chip, c))
                for a in range(n) for j, chip in enumerate(_other_chips(x, y))]

    def start(ins, outs, ss, rs, base):
        for cp in copies(ins, outs, ss, rs, base):
            cp.start()

    def finish(ins, outs, ss, rs, base):
        for cp in copies(ins, outs, ss, rs, base):
            cp.wait()

    shapes = [_out((3,) + s.shape[1:], s.dtype) for s in sums]
    return _Comm(sums, shapes, {}, [(3 * n, start, finish)])


def _scatter_copies(ins, lands, send_sems, recv_sems):
    x, y, c = _mesh_pos()
    return [_remote(ins[a].at[2 * chip[0] + chip[1]], lands[a].at[j], send_sems, recv_sems, 3 * a + j, (*chip, c))
            for a in range(len(ins)) for j, chip in enumerate(_other_chips(x, y))]


def _scatter_start(sums):
    n = len(sums)
    lands = [lax.empty((3,) + s.shape[1:], s.dtype) for s in sums]
    hbm = pl.BlockSpec(memory_space=pltpu.HBM)
    sem = pl.BlockSpec(memory_space=pltpu.SEMAPHORE)

    def body(*refs):
        ins, land_refs = refs[:n], refs[n:2 * n]
        send_sems, recv_sems = refs[2 * n], refs[2 * n + 1]
        token = refs[-1]
        for cp in _scatter_copies(ins, land_refs, send_sems, recv_sems):
            cp.start()
        token[...] = jnp.zeros_like(token)

    res = pl.pallas_call(
        body, name="ffn1_scatter_start",
        out_shape=(pltpu.SemaphoreType.DMA((3 * n,)), pltpu.SemaphoreType.DMA((3 * n,)),
                   *[pltpu.HBM(s.shape, s.dtype) for s in sums], *[pltpu.HBM(ld.shape, ld.dtype) for ld in lands],
                   jax.ShapeDtypeStruct((8, 128), F32)),
        in_specs=[hbm] * (2 * n), out_specs=(sem, sem, *[hbm] * (2 * n), pl.BlockSpec(memory_space=pltpu.VMEM)),
        input_output_aliases={i: 2 + i for i in range(2 * n)},
        compiler_params=pltpu.CompilerParams(has_side_effects=pltpu.SideEffectType.DATAFLOW_SIDE_EFFECTING),
    )(*[pltpu.with_memory_space_constraint(a, pltpu.HBM) for a in list(sums) + lands])
    return res[0], res[1], list(res[2:2 + n]), list(res[2 + n:2 + 2 * n]), res[-1]


def _scatter_wait(send_sems, recv_sems, sums, lands, after):
    n = len(sums)
    hbm = pl.BlockSpec(memory_space=pltpu.HBM)
    sem = pl.BlockSpec(memory_space=pltpu.SEMAPHORE)

    def body(*refs):
        ins, land_refs = refs[:n], refs[n:2 * n]
        for cp in _scatter_copies(ins, land_refs, refs[2 * n], refs[2 * n + 1]):
            cp.wait_send()
            cp.wait_recv()

    res = pl.pallas_call(
        body, name="ffn1_scatter_wait",
        out_shape=tuple([pltpu.HBM(s.shape, s.dtype) for s in sums] + [pltpu.HBM(ld.shape, ld.dtype) for ld in lands]),
        in_specs=[hbm] * (2 * n) + [sem, sem, pl.BlockSpec(memory_space=pl.ANY)], out_specs=tuple([hbm] * (2 * n)),
        input_output_aliases={i: i for i in range(2 * n)},
        compiler_params=pltpu.CompilerParams(has_side_effects=pltpu.SideEffectType.DATAFLOW_SIDE_EFFECTING),
    )(*sums, *lands, send_sems, recv_sems, after)
    return list(res[:n]), list(res[n:])


def _total_sum(name, sums, recv3, kc_arr):
    _, rh, cc = sums.shape

    def body(kc_ref, s_ref, r_ref, o_ref):
        t = s_ref[0].astype(F32) + r_ref[0].astype(F32)
        t = t + r_ref[1].astype(F32)
        o_ref[...] = t + r_ref[2].astype(F32)

    return pl.pallas_call(
        body, name=name, out_shape=_out((2 * rh, cc), F32),
        grid_spec=pltpu.PrefetchScalarGridSpec(
            num_scalar_prefetch=1, grid=(1,),
            in_specs=[pl.BlockSpec((1, rh, cc), lambda i, kc_ref: (kc_ref[0], 0, 0)),
                      pl.BlockSpec((3, rh, cc), lambda i, kc_ref: (0, 0, 0))],
            out_specs=pl.BlockSpec((rh, cc), lambda i, kc_ref: (kc_ref[1], 0))),
        compiler_params=_cp(("arbitrary",), 32),
    )(kc_arr, *_in_hbm(sums, recv3))


def _assemble_comm(totals):
    n = len(totals)

    def copy(outs, ss, rs, base, a):
        x, y, c = _mesh_pos()
        rh = outs[a].shape[0] // 2
        here = outs[a].at[pl.ds(c * rh, rh), :]
        return _remote(here, here, ss, rs, base + a, (x, y, 1 - c))

    def start(ins, outs, ss, rs, base):
        for a in range(n):
            copy(outs, ss, rs, base, a).start()

    def finish(ins, outs, ss, rs, base):
        for a in range(n):
            copy(outs, ss, rs, base, a).wait()

    shapes = [_out(t.shape, t.dtype) for t in totals]
    return _Comm(totals, shapes, {a: a for a in range(n)}, [(n, start, finish)])


def _small_layout(shapes):
    n = len(shapes)
    narrow_w = 64
    wide = [a for a in range(n) if shapes[a][1] > narrow_w]
    narrow = sorted((a for a in range(n) if shapes[a][1] <= narrow_w), key=lambda a: -shapes[a][0])
    offs, cols, groups, widths, rows = {}, {}, [], [], []
    if wide:
        r = 0
        for a in wide:
            offs[a], cols[a] = r, 0
            r += shapes[a][0]
        groups.append(wide)
        widths.append(max(shapes[a][1] for a in wide))
        rows.append(-(-r // 8) * 8)
    if narrow:
        heights = [0, 0]
        for a in narrow:
            side = 0 if heights[0] <= heights[1] else 1
            offs[a], cols[a] = heights[side], side * narrow_w
            heights[side] += shapes[a][0]
        groups.append(narrow)
        widths.append(2 * narrow_w)
        rows.append(-(-max(heights) // 8) * 8)

    def window(ref, a):
        return ref.at[offs[a]:offs[a] + shapes[a][0], cols[a]:cols[a] + shapes[a][1]]

    return groups, widths, rows, window


def _small_pack(arrays, me_arr):
    shapes = [a.shape for a in arrays]
    groups, widths, rows, window = _small_layout(shapes)
    n, n_g = len(arrays), len(groups)

    def body(me_ref, *refs):
        ins, outs = refs[:n], refs[n:]
        for gi, g in enumerate(groups):
            outs[gi][...] = jnp.zeros_like(outs[gi])
            for a in g:
                window(outs[gi].at[0], a)[...] = ins[a][...]

    return pl.pallas_call(
        body, name="small_pack", out_shape=[_out((8, r, w), F32) for r, w in zip(rows, widths)],
        grid_spec=pltpu.PrefetchScalarGridSpec(
            num_scalar_prefetch=1, grid=(1,), in_specs=[pl.BlockSpec(s, lambda i, me: (0, 0)) for s in shapes],
            out_specs=[pl.BlockSpec((1, r, w), lambda i, me: (me[0], 0, 0)) for r, w in zip(rows, widths)]),
        compiler_params=_cp(("arbitrary",), 32),
    )(me_arr, *_in_hbm(*arrays))


def _spread_comm(slots):
    n = len(slots)
    flips = [(dx, dy, dc) for dx in range(2) for dy in range(2) for dc in range(2)][1:]

    def copies(outs, ss, rs, base):
        x, y, c = _mesh_pos()
        mine = 4 * x + 2 * y + c
        return [_remote(outs[a].at[mine], outs[a].at[mine], ss, rs, base + 7 * a + f,
                        (x ^ dx, y ^ dy, c ^ dc)) for a in range(n) for f, (dx, dy, dc) in enumerate(flips)]

    def start(ins, outs, ss, rs, base):
        for cp in copies(outs, ss, rs, base):
            cp.start()

    def finish(ins, outs, ss, rs, base):
        for cp in copies(outs, ss, rs, base):
            cp.wait()

    return _Comm(slots, [_out(s.shape, s.dtype) for s in slots], {a: a for a in range(n)}, [(7 * n, start, finish)])


def _small_total(slots, shapes):
    groups, widths, rows, window = _small_layout(shapes)
    n, n_g = len(shapes), len(groups)

    def body(*refs):
        ins, outs, acc = refs[:n_g], refs[n_g:n_g + n], refs[n_g + n:]
        for gi, g in enumerate(groups):
            t = ins[gi][0] + ins[gi][1]
            for d in range(2, 8):
                t = t + ins[gi][d]
            acc[gi][...] = t
            for a in g:
                outs[a][...] = window(acc[gi], a)[...]

    return pl.pallas_call(
        body, name="small_total", grid=(1,), out_shape=[_out(s, F32) for s in shapes],
        in_specs=[_full(s.shape) for s in slots], out_specs=[_full(s) for s in shapes],
        scratch_shapes=[pltpu.VMEM((r, w), F32) for r, w in zip(rows, widths)],
        compiler_params=_cp(("arbitrary",), 48),
    )(*_in_hbm(*slots))


def _small_allreduce(arrays, comm):
    n = len(arrays)
    shapes = [a.shape for a in arrays]
    groups, widths, rows, window = _small_layout(shapes)
    n_g = len(groups)

    def body(*refs):
        ins, outs = refs[:n], refs[n:2 * n]
        pack, sib, csum, every = (refs[2 * n + i * n_g:2 * n + (i + 1) * n_g] for i in range(4))
        send_sems, recv_sems = refs[2 * n + 4 * n_g:]
        x, y, c = _mesh_pos()
        k = 2 * x + y
        for gi, g in enumerate(groups):
            pack[gi][...] = jnp.zeros_like(pack[gi])
            for a in g:
                window(pack[gi], a)[...] = ins[a][...]
        cps = [_remote(pack[gi], sib[gi], send_sems, recv_sems, gi, (x, y, 1 - c)) for gi in range(n_g)]
        for cp in cps:
            cp.start()
        for cp in cps:
            cp.wait()
        for gi in range(n_g):
            csum[gi][...] = pack[gi][...] + sib[gi][...]
            every[gi][k] = csum[gi][...]
        cps = [_remote(csum[gi], every[gi].at[k], send_sems, recv_sems, n_g + 3 * gi + j, (*chip, c))
               for gi in range(n_g) for j, chip in enumerate(_other_chips(x, y))]
        for cp in cps:
            cp.start()
        for cp in cps:
            cp.wait()
        for gi, g in enumerate(groups):
            pack[gi][...] = ((every[gi][0] + every[gi][1]) + every[gi][2]) + every[gi][3]
            for a in g:
                outs[a][...] = window(pack[gi], a)[...]

    bufs = [pltpu.VMEM((r, w), F32) for r, w in zip(rows, widths)]
    return _call(
        body, comm, (0,), arrays, name="small_allreduce", grid=(1,), out_shape=[_out(s, F32) for s in shapes],
        in_specs=[_full(s) for s in shapes], out_specs=[_full(s) for s in shapes],
        scratch_shapes=bufs * 3 + [pltpu.VMEM((N_CHIP, r, w), F32) for r, w in zip(rows, widths)] +
                       [pltpu.SemaphoreType.DMA((4 * n_g,)), pltpu.SemaphoreType.DMA((4 * n_g,))],
        compiler_params=_cp(("arbitrary",), 40))


def _adamw_small(ws, gs, ms, vs, comm):
    n = len(ws)

    def body(*refs):
        w, g, m, v, d, mo, vo = (refs[i * n:(i + 1) * n] for i in range(7))
        for a in range(n):
            d[a][...], mo[a][...], vo[a][...] = _adamw_math(w[a][...], g[a][...], m[a][...], v[a][...])

    specs = [_full(w.shape) for w in ws]
    res, got = _call(
        body, comm, (0,), (*ws, *gs, *ms, *vs), name="adamw_small", grid=(1,),
        out_shape=[_out(w.shape, F32) for w in ws] * 3,
        in_specs=specs * 4, out_specs=specs * 3, compiler_params=_cp(("arbitrary",), 40))
    return (res[:n], res[n:2 * n], res[2 * n:]), got


def _adamw_math(w, g, m, v):
    m = ADAM_B1 * m + (1.0 - ADAM_B1) * g
    v = ADAM_B2 * v + (1.0 - ADAM_B2) * (g * g)
    m_hat = m / (1.0 - ADAM_B1 ** ADAM_STEP)
    v_hat = v / (1.0 - ADAM_B2 ** ADAM_STEP)
    delta = -ADAM_LR * (m_hat / (jnp.sqrt(v_hat) + ADAM_EPS) + ADAM_WD * w)
    return delta, m, v


def _adamw(name, w, g, m, v):
    r, c = w.shape
    tr = max(t for t in range(8, 513, 8) if r % t == 0)

    def body(w_ref, g_ref, m_ref, v_ref, d_ref, mo_ref, vo_ref):
        d_ref[...], mo_ref[...], vo_ref[...] = _adamw_math(w_ref[...], g_ref[...], m_ref[...], v_ref[...])

    return pl.pallas_call(
        body, name=name, grid=(r // tr,), in_specs=[_rows(tr, c)] * 4, out_specs=[_rows(tr, c)] * 3,
        out_shape=[_out((r, c), F32)] * 3, compiler_params=_cp(("arbitrary",), 32),
    )(*_in_hbm(w, g, m, v))


def _as_matrix(name, a):
    if name == "na_rpb":
        return a[0].transpose(1, 0, 2).reshape(N_HEADS * (2 * KH - 1), 2 * KW - 1)
    if name in ("s5_b_re", "s5_b_im"):
        return a.transpose(0, 1, 2, 4, 3).reshape(2 * S5_G * S5_H, S5_P)
    if name in ("s5_c_re", "s5_c_im"):
        return a.reshape(2 * S5_G * S5_H, S5_P)
    if name in ("s5_lam_re", "s5_lam_im"):
        return a.reshape(2 * S5_G, S5_P)
    if name == "s5_log_dt":
        return a.reshape(2, S5_G)
    return a


def _from_matrix(name, m):
    if name == "na_rpb":
        return m.reshape(2 * KH - 1, N_HEADS, 2 * KW - 1).transpose(1, 0, 2)[None]
    if name in ("s5_b_re", "s5_b_im"):
        return m.reshape(1, 2, S5_G, S5_H, S5_P).transpose(0, 1, 2, 4, 3)
    if name in ("s5_c_re", "s5_c_im"):
        return m.reshape(1, 2, S5_G, S5_H, S5_P)
    if name in ("s5_lam_re", "s5_lam_im"):
        return m.reshape(1, 2, S5_G, S5_P)
    if name == "s5_log_dt":
        return m.reshape(1, 2, S5_G)
    return m


WEIGHTS = ["meta_tokens", "ffn1_pre_g", "ffn1_post_g", "ffn1_w_gate", "ffn1_w_up", "ffn1_w_down", "mix_pre_g", "w_in",
           "na_rpb", "s5_lam_re", "s5_lam_im", "s5_log_dt", "s5_b_re", "s5_b_im", "s5_c_re", "s5_c_im", "s5_d",
           "s5_w_glu", "s5_b_glu", "na_out_g", "s5_out_g", "w_out", "mix_post_g", "ffn2_pre_g", "ffn2_post_g",
           "ffn2_w_gate", "ffn2_w_up", "ffn2_w_down", "final_g"]
BIG = ["ffn1_w_gate", "ffn1_w_up", "ffn1_w_down", "w_in", "s5_w_glu", "w_out", "ffn2_w_gate", "ffn2_w_up",
       "ffn2_w_down"]
TRANSPOSED = ["ffn1_w_gate", "ffn1_w_up", "ffn2_w_gate", "ffn2_w_up"]
GAINS = ["ffn1_pre_g", "ffn1_post_g", "mix_pre_g", "s5_d", "s5_b_glu", "na_out_g", "s5_out_g", "mix_post_g",
         "ffn2_pre_g", "ffn2_post_g", "final_g"]
SMALL = [n for n in WEIGHTS if n not in BIG]


def kernel(*args):
    names = ["x"] + WEIGHTS + ["loss_target"] + ["m_" + n for n in WEIGHTS] + ["v_" + n for n in WEIGHTS]
    assert len(args) == len(names)
    given = dict(zip(names, args))
    x_pos, y_pos, c_pos = _mesh_pos()
    k_pos = 2 * x_pos + y_pos
    c_arr = jnp.reshape(c_pos, (1,)).astype(jnp.int32)
    kc_arr = jnp.stack([k_pos, c_pos]).astype(jnp.int32)

    def piece(name, a):
        return a[0].T if name in TRANSPOSED else a[0]

    def unpiece(name, a):
        return a.T[None] if name in TRANSPOSED else a[None]

    placed = BIG + ["meta_tokens"]
    bufs = dict(zip(placed, _own_half_buffers([piece(n, given[n]) for n in BIG] + [given["meta_tokens"]],
                                              [BF16] * len(BIG) + [F32], kc_arr)))

    gains = {n: given[n] for n in GAINS}
    s5 = {n: _as_matrix("s5_" + n, given["s5_" + n])
          for n in ["lam_re", "lam_im", "log_dt", "b_re", "b_im", "c_re", "c_im"]}
    me_arr = jnp.reshape(4 * x_pos + 2 * y_pos + c_pos, (1,)).astype(jnp.int32)
    loss, dh0, pieces, small, late, (ffn1, flight), (mid, sums) = _step(
        given["x"][0], given["loss_target"][0], bufs, gains, s5, given["na_rpb"][0], c_arr, kc_arr, me_arr)
    loss = lax.psum(loss, ("x", "y", "c"))
    n_tok = given["x"].shape[1]
    grad_x = dh0[N_META:N_META + n_tok][None]

    late["meta_tokens"] = dh0[:N_META]
    out_g, out_d, out_m, out_v = {}, {}, {}, {}

    def update_big(n):
        g2 = pieces[n]
        d2, m2, v2 = _adamw("adamw_" + n, piece(n, given[n]), g2, piece(n, given["m_" + n]),
                            piece(n, given["v_" + n]))
        out_g[n], out_d[n], out_m[n], out_v[n] = (unpiece(n, t) for t in (g2, d2, m2, v2))
        return v2

    scatter = _scatter_comm(sums)
    scatter.ins += [update_big(n) for n in pieces] + [small["final_g"]]
    red, recv3 = _small_allreduce(list(late.values()), scatter)
    small.update(zip(late, red))
    mc = D // N_CHIP
    small["meta_tokens"] = lax.dynamic_slice_in_dim(small["meta_tokens"], k_pos * mc, mc, 1)
    send_sems, recv_sems, sums1, lands1 = flight
    sums1, recv3_1 = _scatter_wait(send_sems, recv_sems, sums1, lands1, red[0])
    last = ffn1 + mid
    totals = [_total_sum("total_sum_" + n, s, r, kc_arr)
              for n, s, r in zip(last, sums1 + list(sums), recv3_1 + list(recv3))]
    gs = [small[n] for n in SMALL]
    (d2, m2, v2), done = _adamw_small([_as_matrix(n, given[n]) for n in SMALL], gs,
                                      [_as_matrix(n, given["m_" + n]) for n in SMALL],
                                      [_as_matrix(n, given["v_" + n]) for n in SMALL], _assemble_comm(totals))
    pieces.update(zip(last, done))

    for n, g, dd, mm, vv in zip(SMALL, gs, d2, m2, v2):
        out_g[n], out_d[n], out_m[n], out_v[n] = (_from_matrix(n, t) for t in (g, dd, mm, vv))
    for n in last:
        update_big(n)
    return (loss, grad_x, *[out_g[n] for n in WEIGHTS], *[out_d[n] for n in WEIGHTS],
            *[out_m[n] for n in WEIGHTS], *[out_v[n] for n in WEIGHTS])
```

```python
import functools
import math

import numpy as np
import jax
import jax.numpy as jnp
from jax import lax
from jax.experimental import pallas as pl
from jax.experimental.pallas import tpu as pltpu

F32 = jnp.float32
BF16 = jnp.bfloat16

D = 1024
N_META = 16
GRID_W = 64
NA_W = 512
S5_W = 512
HEAD_DIM = 64
N_HEADS = 8
KH = 8
KW = 16
S5_G = 32
S5_P = 64
S5_H = 16
N_BUNDLE = 4
FF = 2816
N_CHIP = 4
FC = FF // N_CHIP
EPS = 1e-6
NEG_INF = -1e30
Q_ROWS = 4
K_ROWS = 12
QB = Q_ROWS * GRID_W
KB = K_ROWS * GRID_W
SCAN_CHUNK = 256

ADAM_LR = 0.001
ADAM_B1 = 0.9
ADAM_B2 = 0.999
ADAM_EPS = 1e-08
ADAM_WD = 0.01
ADAM_STEP = 10

NT = (((1,), (1,)), ((), ()))
TN = (((0,), (0,)), ((), ()))
MESH_ID = pl.DeviceIdType.MESH


def _cp(sem=None, vmem_mb=None):
    kw = {}
    if sem is not None:
        kw["dimension_semantics"] = sem
    if vmem_mb is not None:
        kw["vmem_limit_bytes"] = vmem_mb << 20
    return pltpu.CompilerParams(**kw)


def _full(shape):
    n = len(shape)
    return pl.BlockSpec(shape, lambda *_: (0,) * n)


def _rows(tm, w):
    return pl.BlockSpec((tm, w), lambda i: (i, 0))


ANY = pl.BlockSpec(memory_space=pl.ANY)


def _rms(x, g):
    r = lax.rsqrt(jnp.mean(x * x, axis=-1, keepdims=True) + EPS)
    return x * r * g


def _rms_bwd(x, g, dy):
    r = lax.rsqrt(jnp.mean(x * x, axis=-1, keepdims=True) + EPS)
    xh = x * r
    dg = jnp.sum(dy * xh, axis=0, keepdims=True)
    dyg = dy * g
    dx = r * (dyg - xh * jnp.mean(dyg * xh, axis=-1, keepdims=True))
    return dx, dg


def _out(shape, dtype):
    return pltpu.HBM(tuple(shape), dtype)


def _in_hbm(*args):
    return [pltpu.with_memory_space_constraint(a, pltpu.HBM) if jnp.issubdtype(a.dtype, jnp.floating) and a.ndim > 1
            else a for a in args]


def _dot(a, b):
    return jnp.dot(a, b, preferred_element_type=F32)


def _dg(a, b, dims):
    return lax.dot_general(a, b, dims, preferred_element_type=F32)


def _ffn_fwd(name, h, g_pre, g_post, wg, wu, wd, tm, comm=None, bounds=()):
    tp = h.shape[0]
    nt = tp // tm

    def body(h_ref, gp_ref, gq_ref, wg_ref, wu_ref, wd_ref, hn_ref, gate_ref, up_ref, f_ref, xn_s, acc_s):
        c = pl.program_id(1)

        @pl.when(c == 0)
        def _():
            xn_s[...] = _rms(h_ref[...], gp_ref[...]).astype(BF16)
            acc_s[...] = jnp.zeros_like(acc_s)

        xn = xn_s[...]
        gate = _dg(xn, wg_ref[0], NT)
        up = _dg(xn, wu_ref[0], NT)
        gate_ref[0] = gate
        up_ref[0] = up
        act = (gate * jax.nn.sigmoid(gate) * up).astype(BF16)
        acc_s[...] += _dot(act, wd_ref[0])

        @pl.when(c == N_CHIP - 1)
        def _():
            f = acc_s[...]
            f_ref[...] = f
            hn_ref[...] = h_ref[...] + 0.5 * _rms(f, gq_ref[...])

    return _call(
        body, comm, bounds, (h, g_pre, g_post, wg, wu, wd), name=name, grid=(nt, N_CHIP),
        in_specs=[pl.BlockSpec((tm, D), lambda i, c: (i, 0)), _full((1, D)), _full((1, D))] +
                 [pl.BlockSpec((1, FC, D), lambda i, c: (c, 0, 0))] * 3,
        out_specs=[pl.BlockSpec((tm, D), lambda i, c: (i, 0)),
                   pl.BlockSpec((1, tm, FC), lambda i, c: (c, i, 0)),
                   pl.BlockSpec((1, tm, FC), lambda i, c: (c, i, 0)),
                   pl.BlockSpec((tm, D), lambda i, c: (i, 0))],
        out_shape=[_out((tp, D), F32), _out((N_CHIP, tp, FC), F32),
                   _out((N_CHIP, tp, FC), F32), _out((tp, D), F32)],
        scratch_shapes=[pltpu.VMEM((tm, D), BF16), pltpu.VMEM((tm, D), F32)],
        compiler_params=_cp(("arbitrary", "arbitrary"), 48))


def _ffn_bwd(name, h, g_pre, df, gate, up, wg, wu, wd, tm, comm=None, bounds=()):
    tp = h.shape[0]
    nt = tp // tm
    rh = FC // 2

    def body(h_ref, gp_ref, df_ref, gate_ref, up_ref, wg_ref, wu_ref, wd_ref,
             dwg_ref, dwu_ref, dwd_ref, dxn_ref, rg_ref, ru_ref, rd_ref, ag, au, ad, send_sems, recv_sems):
        c = pl.program_id(0)
        i = pl.program_id(1)

        def to_sibling(a, piece):
            x, y, core = _mesh_pos()
            dw_ref, r_ref = ((dwg_ref, rg_ref), (dwu_ref, ru_ref), (dwd_ref, rd_ref))[a]
            return _remote(dw_ref.at[piece, pl.ds((1 - core) * rh, rh), :], r_ref.at[piece], send_sems, recv_sems,
                           3 * piece + a, (x, y, 1 - core))

        @pl.when(i == 0)
        def _():
            ag[...] = jnp.zeros_like(ag)
            au[...] = jnp.zeros_like(au)
            ad[...] = jnp.zeros_like(ad)

        xn = _rms(h_ref[...], gp_ref[...]).astype(BF16)
        dfb = df_ref[...].astype(BF16)
        gt = gate_ref[0]
        u = up_ref[0]
        sg = jax.nn.sigmoid(gt)
        si = gt * sg
        act = (si * u).astype(BF16)
        dact = _dg(dfb, wd_ref[0], NT)
        ad[...] += _dg(act, dfb, TN)
        dgate = (dact * u * (sg * (1.0 + gt * (1.0 - sg)))).astype(BF16)
        dup = (dact * si).astype(BF16)
        ag[...] += _dg(dgate, xn, TN)
        au[...] += _dg(dup, xn, TN)
        dxn_ref[0] = _dot(dgate, wg_ref[0]) + _dot(dup, wu_ref[0])

        @pl.when(i == nt - 1)
        def _():
            pltpu.sync_copy(ag, dwg_ref.at[c])
            pltpu.sync_copy(au, dwu_ref.at[c])
            pltpu.sync_copy(ad, dwd_ref.at[c])
            for a in range(3):
                to_sibling(a, c).start()

        @pl.when((c == N_CHIP - 1) & (i == nt - 1))
        def _():
            for piece in range(N_CHIP):
                for a in range(3):
                    to_sibling(a, piece).wait()

    return _call(
        body, comm, bounds, (h, g_pre, df, gate, up, wg, wu, wd), name=name, grid=(N_CHIP, nt),
        in_specs=[pl.BlockSpec((tm, D), lambda c, i: (i, 0)), _full((1, D)),
                  pl.BlockSpec((tm, D), lambda c, i: (i, 0)),
                  pl.BlockSpec((1, tm, FC), lambda c, i: (c, i, 0)),
                  pl.BlockSpec((1, tm, FC), lambda c, i: (c, i, 0))] +
                 [pl.BlockSpec((1, FC, D), lambda c, i: (c, 0, 0))] * 3,
        out_specs=[ANY, ANY, ANY, pl.BlockSpec((1, tm, D), lambda c, i: (c, i, 0)), ANY, ANY, ANY],
        out_shape=[_out((N_CHIP, FC, D), F32)] * 3 + [_out((N_CHIP, tp, D), F32)] +
                  [_out((N_CHIP, rh, D), F32)] * 3,
        scratch_shapes=[pltpu.VMEM((FC, D), F32)] * 3 +
                       [pltpu.SemaphoreType.DMA((3 * N_CHIP,)), pltpu.SemaphoreType.DMA((3 * N_CHIP,))],
        compiler_params=_cp(("arbitrary", "arbitrary"), 58))


def _ffn_pre_bwd(name, dh, dxn_part, h, g_pre, tm, comm=None, bounds=()):
    tp = h.shape[0]
    nt = tp // tm

    def body(dh_ref, dxn_ref, h_ref, gp_ref, out_ref, dg_ref):
        i = pl.program_id(0)
        dxn = (dxn_ref[0] + dxn_ref[1]) + (dxn_ref[2] + dxn_ref[3])
        dx, dg = _rms_bwd(h_ref[...], gp_ref[...], dxn)
        out_ref[...] = dh_ref[...] + dx

        @pl.when(i == 0)
        def _():
            dg_ref[...] = jnp.zeros_like(dg_ref)

        dg_ref[...] += dg

    return _call(
        body, comm, bounds, (dh, dxn_part, h, g_pre), name=name, grid=(nt,),
        in_specs=[_rows(tm, D), pl.BlockSpec((N_CHIP, tm, D), lambda i: (0, i, 0)), _rows(tm, D), _full((1, D))],
        out_specs=[_rows(tm, D), _full((1, D))],
        out_shape=[_out((tp, D), F32), _out((1, D), F32)],
        compiler_params=_cp(("arbitrary",), 48))


def _mix_in(h, g, w_in, tm):
    tp = h.shape[0]

    def body(h_ref, g_ref, w_ref, q_ref, k_ref, v_ref, u_ref):
        a = _rms(h_ref[...], g_ref[...]).astype(BF16)
        q_ref[...] = _dot(a, w_ref[0]).astype(BF16)
        k_ref[...] = _dot(a, w_ref[1]).astype(BF16)
        v_ref[...] = _dot(a, w_ref[2]).astype(BF16)
        u_ref[...] = _dot(a, w_ref[3])

    return pl.pallas_call(
        body, name="mix_in", grid=(tp // tm,),
        in_specs=[_rows(tm, D), _full((1, D)), _full((N_CHIP, D, NA_W))],
        out_specs=[_rows(tm, NA_W)] * 4,
        out_shape=[_out((tp, NA_W), BF16)] * 3 + [_out((tp, S5_W), F32)],
        compiler_params=_cp(("arbitrary",), 40),
    )(*_in_hbm(h, g, w_in))


def _gelu(x):
    return jax.nn.gelu(x, approximate=True)


def _gelu_grad(x):
    k = math.sqrt(2.0 / math.pi)
    t = jnp.tanh(k * (x + 0.044715 * x * x * x))
    return 0.5 * (1.0 + t) + 0.5 * x * (1.0 - t * t) * k * (1.0 + 3.0 * 0.044715 * x * x)


def _mix_out(o_na, y_pre, h, w_glu, b_glu, g_na, g_s5, w_out, g_post, tm, comm=None, bounds=()):
    tp = h.shape[0]

    def body(ona_ref, yp_ref, h_ref, wglu_ref, bglu_ref, gna_ref, gs5_ref, wout_ref, gpost_ref, hn_ref, mix_ref):
        y = _gelu(yp_ref[...])
        z = _dot(y.astype(BF16), wglu_ref[...]) + bglu_ref[...]
        o_s5 = y * jax.nn.sigmoid(z)
        n1 = _rms(ona_ref[...], gna_ref[...]).astype(BF16)
        n2 = _rms(o_s5, gs5_ref[...]).astype(BF16)
        mix = _dot(n1, wout_ref[0:NA_W, :]) + _dot(n2, wout_ref[NA_W:, :])
        mix_ref[...] = mix
        hn_ref[...] = h_ref[...] + _rms(mix, gpost_ref[...])

    return _call(
        body, comm, bounds, (o_na, y_pre, h, w_glu, b_glu, g_na, g_s5, w_out, g_post), name="mix_out",
        grid=(tp // tm,),
        in_specs=[_rows(tm, NA_W), _rows(tm, S5_W), _rows(tm, D), _full((S5_W, S5_W)), _full((1, S5_W)),
                  _full((1, NA_W)), _full((1, S5_W)), _full((D, D)), _full((1, D))],
        out_specs=[_rows(tm, D), _rows(tm, D)],
        out_shape=[_out((tp, D), F32)] * 2,
        compiler_params=_cp(("arbitrary",), 40))


def _mix_out_bwd(dh, mix, o_na, y_pre, w_glu, b_glu, g_na, g_s5, w_out, g_post, tm):
    tp = dh.shape[0]
    nt = tp // tm

    def body(dh_ref, mix_ref, ona_ref, yp_ref, wglu_ref, bglu_ref, gna_ref, gs5_ref, wout_ref, gpost_ref,
             dona_ref, dyp_ref, dwout_ref, dwglu_ref, dgpost_ref, dgna_ref, dgs5_ref, dbglu_ref, a_out, a_glu):
        i = pl.program_id(0)

        @pl.when(i == 0)
        def _():
            a_out[...] = jnp.zeros_like(a_out)
            a_glu[...] = jnp.zeros_like(a_glu)
            dgpost_ref[...] = jnp.zeros_like(dgpost_ref)
            dgna_ref[...] = jnp.zeros_like(dgna_ref)
            dgs5_ref[...] = jnp.zeros_like(dgs5_ref)
            dbglu_ref[...] = jnp.zeros_like(dbglu_ref)

        dmix, dgpost = _rms_bwd(mix_ref[...], gpost_ref[...], dh_ref[...])
        dgpost_ref[...] += dgpost
        yp = yp_ref[...]
        y = _gelu(yp)
        yb = y.astype(BF16)
        z = _dot(yb, wglu_ref[...]) + bglu_ref[...]
        sg = jax.nn.sigmoid(z)
        o_s5 = y * sg
        o_na = ona_ref[...]
        n1 = _rms(o_na, gna_ref[...]).astype(BF16)
        n2 = _rms(o_s5, gs5_ref[...]).astype(BF16)
        dmb = dmix.astype(BF16)
        a_out[0:NA_W, :] += _dg(n1, dmb, TN)
        a_out[NA_W:, :] += _dg(n2, dmb, TN)
        dn1 = _dg(dmb, wout_ref[0:NA_W, :], NT)
        dn2 = _dg(dmb, wout_ref[NA_W:, :], NT)
        dona, dgna = _rms_bwd(o_na, gna_ref[...], dn1)
        dona_ref[...] = dona
        dgna_ref[...] += dgna
        dos5, dgs5 = _rms_bwd(o_s5, gs5_ref[...], dn2)
        dgs5_ref[...] += dgs5
        dz = dos5 * y * (sg * (1.0 - sg))
        dbglu_ref[...] += jnp.sum(dz, axis=0, keepdims=True)
        dzb = dz.astype(BF16)
        a_glu[...] += _dg(yb, dzb, TN)
        dy = dos5 * sg + _dg(dzb, wglu_ref[...], NT)
        dyp_ref[...] = dy * _gelu_grad(yp)

        @pl.when(i == nt - 1)
        def _():
            pltpu.sync_copy(a_out, dwout_ref)
            pltpu.sync_copy(a_glu, dwglu_ref)

    return pl.pallas_call(
        body, name="mix_out_bwd", grid=(nt,),
        in_specs=[_rows(tm, D), _rows(tm, D), _rows(tm, NA_W), _rows(tm, S5_W), _full((S5_W, S5_W)),
                  _full((1, S5_W)), _full((1, NA_W)), _full((1, S5_W)), _full((D, D)), _full((1, D))],
        out_specs=[_rows(tm, NA_W), _rows(tm, S5_W), ANY, ANY, _full((1, D)), _full((1, NA_W)),
                   _full((1, S5_W)), _full((1, S5_W))],
        out_shape=[_out((tp, NA_W), F32), _out((tp, S5_W), F32),
                   _out((D, D), F32), _out((S5_W, S5_W), F32),
                   _out((1, D), F32), _out((1, NA_W), F32),
                   _out((1, S5_W), F32), _out((1, S5_W), F32)],
        scratch_shapes=[pltpu.VMEM((D, D), F32), pltpu.VMEM((S5_W, S5_W), F32)],
        compiler_params=_cp(("arbitrary",), 48),
    )(*_in_hbm(dh, mix, o_na, y_pre, w_glu, b_glu, g_na, g_s5, w_out, g_post))


def _mix_in_bwd(dq, dk, dv, du, h, g, w_in, dh, f1, g_post1, tm, comm=None, bounds=()):
    tp = h.shape[0]
    nt = tp // tm

    def body(dq_ref, dk_ref, dv_ref, du_ref, h_ref, g_ref, w_ref, dh_ref, f_ref, gq_ref,
             dh1_ref, df_ref, dw_ref, dg_ref, dgq_ref, acc):
        i = pl.program_id(0)

        @pl.when(i == 0)
        def _():
            acc[...] = jnp.zeros_like(acc)
            dg_ref[...] = jnp.zeros_like(dg_ref)
            dgq_ref[...] = jnp.zeros_like(dgq_ref)

        x = h_ref[...]
        a = _rms(x, g_ref[...]).astype(BF16)
        da = jnp.zeros((tm, D), F32)
        for j, r in enumerate((dq_ref, dk_ref, dv_ref, du_ref)):
            dp = r[...].astype(BF16)
            da = da + _dg(dp, w_ref[j], NT)
            acc[j] += _dg(a, dp, TN)
        dx, dg = _rms_bwd(x, g_ref[...], da)
        dh1 = dh_ref[...] + dx
        dh1_ref[...] = dh1
        dg_ref[...] += dg
        df, dgq = _rms_bwd(f_ref[...], gq_ref[...], 0.5 * dh1)
        df_ref[...] = df
        dgq_ref[...] += dgq

        @pl.when(i == nt - 1)
        def _():
            pltpu.sync_copy(acc, dw_ref)

    return _call(
        body, comm, bounds, (dq, dk, dv, du, h, g, w_in, dh, f1, g_post1), name="mix_in_bwd", grid=(nt,),
        in_specs=[_rows(tm, NA_W)] * 4 + [_rows(tm, D), _full((1, D)), _full((N_CHIP, D, NA_W)), _rows(tm, D),
                                         _rows(tm, D), _full((1, D))],
        out_specs=[_rows(tm, D), _rows(tm, D), ANY, _full((1, D)), _full((1, D))],
        out_shape=[_out((tp, D), F32), _out((tp, D), F32),
                   _out((N_CHIP, D, NA_W), F32), _out((1, D), F32),
                   _out((1, D), F32)],
        scratch_shapes=[pltpu.VMEM((N_CHIP, D, NA_W), F32)],
        compiler_params=_cp(("arbitrary",), 48))


def _final_loss(h, g_final, target, f2, g_post2, n_tok, tm):
    tp = h.shape[0]

    def body(h_ref, g_ref, t_ref, f_ref, gq_ref, dh_ref, df_ref, loss_ref, dg_ref, dgq_ref):
        i = pl.program_id(0)

        @pl.when(i == 0)
        def _():
            loss_ref[...] = jnp.zeros_like(loss_ref)
            dg_ref[...] = jnp.zeros_like(dg_ref)
            dgq_ref[...] = jnp.zeros_like(dgq_ref)

        x = h_ref[...]
        y = _rms(x, g_ref[...])
        row = i * tm + lax.broadcasted_iota(jnp.int32, (tm, 1), 0)
        valid = (row >= N_META) & (row < N_META + n_tok)
        e = jnp.where(valid, y - t_ref[...], 0.0)
        loss_ref[...] += 0.5 * jnp.sum(jnp.mean(e * e, axis=-1, keepdims=True), axis=0, keepdims=True)
        dx, dg = _rms_bwd(x, g_ref[...], e * (1.0 / D))
        dh_ref[...] = dx
        dg_ref[...] += dg
        df, dgq = _rms_bwd(f_ref[...], gq_ref[...], 0.5 * dx)
        df_ref[...] = df
        dgq_ref[...] += dgq

    return pl.pallas_call(
        body, name="final_loss", grid=(tp // tm,),
        in_specs=[_rows(tm, D), _full((1, D)), _rows(tm, D), _rows(tm, D), _full((1, D))],
        out_specs=[_rows(tm, D), _rows(tm, D), _full((1, 1)), _full((1, D)), _full((1, D))],
        out_shape=[_out((tp, D), F32), _out((tp, D), F32),
                   _out((1, 1), F32), _out((1, D), F32),
                   _out((1, D), F32)],
        compiler_params=_cp(("arbitrary",), 40),
    )(*_in_hbm(h, g_final, target, f2, g_post2))


def _na_patterns(n_rows):
    pats = []
    for kind in range(3):
        pat = [[-1] * K_ROWS for _ in range(Q_ROWS)]
        for i in range(Q_ROWS):
            for jj in range(K_ROWS):
                if kind == 0 and jj < KH:
                    pat[i][jj] = jj - i + KH - 1
                elif kind == 1 and i <= jj < i + KH:
                    pat[i][jj] = jj - i + 3
                elif kind == 2 and K_ROWS - KH <= jj:
                    pat[i][jj] = jj - i - 1
        pats.append(pat)
    return pats


def _diag_onehot():
    q = np.arange(GRID_W)[:, None]
    kc = np.arange(GRID_W)[None, :]
    start = np.clip(q - KW // 2, 0, GRID_W - KW)
    col_in = (kc >= start) & (kc < start + KW)
    e = np.zeros((32, GRID_W, GRID_W), np.float32)
    for d in range(2 * KW - 1):
        e[d] = ((kc - q + KW - 1) == d) & col_in
    return e.reshape(32, GRID_W * GRID_W), col_in


def _rpb_collapse(dtb2, et):
    def body(d_ref, e_ref, o_ref):
        o_ref[...] = jnp.dot(d_ref[...], e_ref[...], preferred_element_type=F32, precision=lax.Precision.HIGHEST)

    out = (dtb2.shape[0], et.shape[1])
    return pl.pallas_call(
        body, name="rpb_collapse", grid=(1,), out_shape=_out(out, F32),
        in_specs=[_full(dtb2.shape), _full(et.shape)], out_specs=_full(out),
    )(*_in_hbm(dtb2, et))


def _bias_tables(rpb, n_rows, comm=None, bounds=()):
    n_dr, n_dc = 2 * KH - 1, 2 * KW - 1
    pats = _na_patterns(n_rows)

    def body(rpb_ref, o_ref):
        h = pl.program_id(0)
        q = lax.broadcasted_iota(jnp.int32, (GRID_W, GRID_W), 0)
        kc = lax.broadcasted_iota(jnp.int32, (GRID_W, GRID_W), 1)
        start = jnp.clip(q - KW // 2, 0, GRID_W - KW)
        col_in = (kc >= start) & (kc < start + KW)
        diff = kc - q + (KW - 1)
        neg = jnp.full((GRID_W, GRID_W), NEG_INF, F32)
        band = []
        for dr in range(n_dr):
            acc = neg
            for d in range(n_dc):
                acc = jnp.where((diff == d) & col_in, rpb_ref[(h * n_dr + dr) * n_dc + d], acc)
            band.append(acc)
        for kind, pat in enumerate(pats):
            for i in range(Q_ROWS):
                for jj in range(K_ROWS):
                    o_ref[kind, 0, i * GRID_W:(i + 1) * GRID_W, jj * GRID_W:(jj + 1) * GRID_W] = (
                        band[pat[i][jj]] if pat[i][jj] >= 0 else neg)

    (bias,), got = _call(
        body, comm, bounds, (rpb.reshape(-1),), name="bias_tables", grid=(N_HEADS,),
        in_specs=[pl.BlockSpec(memory_space=pltpu.SMEM)],
        out_specs=[pl.BlockSpec((3, 1, QB, KB), lambda h: (0, h, 0, 0))],
        out_shape=[_out((3, N_HEADS, QB, KB), F32)],
        compiler_params=_cp(("arbitrary",), 32))
    return bias, got


def _attn_geometry(n_tok):
    n_rows = n_tok // GRID_W
    assert n_rows % Q_ROWS == 0 and n_rows >= K_ROWS
    return n_rows, n_rows // Q_ROWS


def _attn_probs(qh, kh, kmh, bias, scale):
    s = _dg(qh, kh, NT) * scale + bias
    sm = _dg(qh, kmh, NT) * scale
    m = jnp.maximum(jnp.max(s, axis=-1, keepdims=True), jnp.max(sm, axis=-1, keepdims=True))
    p = jnp.exp(s - m)
    pm = jnp.exp(sm - m)
    inv = 1.0 / (jnp.sum(p, axis=-1, keepdims=True) + jnp.sum(pm, axis=-1, keepdims=True))
    return p * inv, pm * inv


def _meta_probs(qmh, kmh, scale):
    s = _dg(qmh, kmh, NT) * scale
    p = jnp.exp(s - jnp.max(s, axis=-1, keepdims=True))
    return p / jnp.sum(p, axis=-1, keepdims=True)


def _step_rows(r, n_rows):
    q0 = pl.multiple_of(N_META + r * QB, 16)
    k0 = pl.multiple_of(N_META + jnp.clip(Q_ROWS * r - (K_ROWS - KH), 0, n_rows - K_ROWS) * GRID_W, 16)
    return q0, k0


def _attn_fwd(q, k, v, bias, n_tok, comm=None, bounds=()):
    tp = q.shape[0]
    n_rows, n_steps = _attn_geometry(n_tok)
    scale = HEAD_DIM ** -0.5

    def body(q_ref, k_ref, v_ref, b_ref, o_ref):
        r = pl.program_id(1)
        km = k_ref[0:N_META, :]
        vm = v_ref[0:N_META, :]

        @pl.when(r == 0)
        def _():
            qm = q_ref[0:N_META, :]
            outs = []
            for hh in range(2):
                sl = slice(hh * HEAD_DIM, (hh + 1) * HEAD_DIM)
                p = _meta_probs(qm[:, sl], km[:, sl], scale)
                outs.append(_dot(p.astype(BF16), vm[:, sl]))
            o_ref[0:N_META, :] = jnp.concatenate(outs, axis=1)
            o_ref[N_META + n_tok:, :] = jnp.zeros((tp - N_META - n_tok, 2 * HEAD_DIM), F32)

        q0, k0 = _step_rows(r, n_rows)
        qb = q_ref[pl.ds(q0, QB), :]
        kb = k_ref[pl.ds(k0, KB), :]
        vb = v_ref[pl.ds(k0, KB), :]
        outs = []
        for hh in range(2):
            sl = slice(hh * HEAD_DIM, (hh + 1) * HEAD_DIM)
            p, pm = _attn_probs(qb[:, sl], kb[:, sl], km[:, sl], b_ref[0, hh], scale)
            outs.append(_dot(p.astype(BF16), vb[:, sl]) + _dot(pm.astype(BF16), vm[:, sl]))
        o_ref[pl.ds(q0, QB), :] = jnp.concatenate(outs, axis=1)

    def bias_map(hp, r):
        return (jnp.where(r == 0, 0, jnp.where(r == n_steps - 1, 2, 1)), hp, 0, 0)

    col = pl.BlockSpec((tp, 2 * HEAD_DIM), lambda hp, r: (0, hp))
    return _call(
        body, comm, bounds, (q, k, v, bias), name="attn_fwd", grid=(N_HEADS // 2, n_steps),
        in_specs=[col, col, col, pl.BlockSpec((1, 2, QB, KB), bias_map)],
        out_specs=[col], out_shape=[_out((tp, NA_W), F32)],
        compiler_params=_cp(("arbitrary", "arbitrary"), 40))


def _attn_bwd(q, k, v, bias, do, n_tok, comm=None, bounds=()):
    tp = q.shape[0]
    n_rows, n_steps = _attn_geometry(n_tok)
    scale = HEAD_DIM ** -0.5
    pats = _na_patterns(n_rows)

    def body(q_ref, k_ref, v_ref, b_ref, do_ref, dq_ref, dk_ref, dv_ref, dtb_ref):
        r = pl.program_id(1)
        km = k_ref[0:N_META, :]
        vm = v_ref[0:N_META, :]

        @pl.when(r == 0)
        def _():
            dk_ref[...] = jnp.zeros_like(dk_ref)
            dv_ref[...] = jnp.zeros_like(dv_ref)
            dtb_ref[...] = jnp.zeros_like(dtb_ref)
            dq_ref[N_META + n_tok:, :] = jnp.zeros((tp - N_META - n_tok, 2 * HEAD_DIM), F32)
            qm = q_ref[0:N_META, :]
            dom = do_ref[0:N_META, :].astype(BF16)
            dqs, dks, dvs = [], [], []
            for hh in range(2):
                sl = slice(hh * HEAD_DIM, (hh + 1) * HEAD_DIM)
                p = _meta_probs(qm[:, sl], km[:, sl], scale)
                dp = _dg(dom[:, sl], vm[:, sl], NT)
                ds = (p * (dp - jnp.sum(dp * p, axis=-1, keepdims=True))).astype(BF16)
                dvs.append(_dg(p.astype(BF16), dom[:, sl], TN))
                dqs.append(_dot(ds, km[:, sl]) * scale)
                dks.append(_dg(ds, qm[:, sl], TN) * scale)
            dq_ref[0:N_META, :] = jnp.concatenate(dqs, axis=1)
            dk_ref[0:N_META, :] += jnp.concatenate(dks, axis=1)
            dv_ref[0:N_META, :] += jnp.concatenate(dvs, axis=1)

        q0, k0 = _step_rows(r, n_rows)
        qb = q_ref[pl.ds(q0, QB), :]
        kb = k_ref[pl.ds(k0, KB), :]
        vb = v_ref[pl.ds(k0, KB), :]
        dob = do_ref[pl.ds(q0, QB), :].astype(BF16)
        dqs, dks, dvs, dkms, dvms, dss = [], [], [], [], [], []
        for hh in range(2):
            sl = slice(hh * HEAD_DIM, (hh + 1) * HEAD_DIM)
            qh, kh, vh, kmh, vmh, doh = qb[:, sl], kb[:, sl], vb[:, sl], km[:, sl], vm[:, sl], dob[:, sl]
            p, pm = _attn_probs(qh, kh, kmh, b_ref[0, hh], scale)
            dp = _dg(doh, vh, NT)
            dpm = _dg(doh, vmh, NT)
            delta = jnp.sum(dp * p, axis=-1, keepdims=True) + jnp.sum(dpm * pm, axis=-1, keepdims=True)
            ds = p * (dp - delta)
            dsb = ds.astype(BF16)
            dsmb = (pm * (dpm - delta)).astype(BF16)
            dss.append(ds)
            dvs.append(_dg(p.astype(BF16), doh, TN))
            dvms.append(_dg(pm.astype(BF16), doh, TN))
            dqs.append((_dot(dsb, kh) + _dot(dsmb, kmh)) * scale)
            dks.append(_dg(dsb, qh, TN) * scale)
            dkms.append(_dg(dsmb, qh, TN) * scale)
        dq_ref[pl.ds(q0, QB), :] = jnp.concatenate(dqs, axis=1)
        dk_ref[pl.ds(k0, KB), :] += jnp.concatenate(dks, axis=1)
        dv_ref[pl.ds(k0, KB), :] += jnp.concatenate(dvs, axis=1)
        dk_ref[0:N_META, :] += jnp.concatenate(dkms, axis=1)
        dv_ref[0:N_META, :] += jnp.concatenate(dvms, axis=1)

        def add_bias_grad(pat):
            for hh in range(2):
                for i in range(Q_ROWS):
                    for jj in range(K_ROWS):
                        if pat[i][jj] >= 0:
                            dtb_ref[hh, pat[i][jj]] += dss[hh][i * GRID_W:(i + 1) * GRID_W,
                                                               jj * GRID_W:(jj + 1) * GRID_W]

        @pl.when(r == 0)
        def _():
            add_bias_grad(pats[0])

        @pl.when((r > 0) & (r < n_steps - 1))
        def _():
            add_bias_grad(pats[1])

        @pl.when(r == n_steps - 1)
        def _():
            add_bias_grad(pats[2])

    def bias_map(hp, r):
        return (jnp.where(r == 0, 0, jnp.where(r == n_steps - 1, 2, 1)), hp, 0, 0)

    col = pl.BlockSpec((tp, 2 * HEAD_DIM), lambda hp, r: (0, hp))
    n_dr = 2 * KH - 1
    return _call(
        body, comm, bounds, (q, k, v, bias, do), name="attn_bwd", grid=(N_HEADS // 2, n_steps),
        in_specs=[col, col, col, pl.BlockSpec((1, 2, QB, KB), bias_map), col],
        out_specs=[col, col, col, pl.BlockSpec((2, n_dr, GRID_W, GRID_W), lambda hp, r: (hp, 0, 0, 0))],
        out_shape=[_out((tp, NA_W), F32)] * 3 +
                  [_out((N_HEADS, n_dr, GRID_W, GRID_W), F32)],
        compiler_params=_cp(("arbitrary", "arbitrary"), 48))


def _repeat_onehot():
    return np.repeat(np.eye(2 * S5_G, dtype=np.float32), S5_H, axis=0)


def _s5_disc_math(lam_re, lam_im, log_dt, b_re, b_im, rep):
    dt = jnp.exp(log_dt)
    ea = jnp.exp(lam_re * dt)
    a_re = ea * jnp.cos(lam_im * dt)
    a_im = ea * jnp.sin(lam_im * dt)
    den = lam_re * lam_re + lam_im * lam_im
    c_re = ((a_re - 1.0) * lam_re + a_im * lam_im) / den
    c_im = (a_im * lam_re - (a_re - 1.0) * lam_im) / den
    ce_re = jnp.dot(rep, c_re, preferred_element_type=F32, precision=lax.Precision.HIGHEST)
    ce_im = jnp.dot(rep, c_im, preferred_element_type=F32, precision=lax.Precision.HIGHEST)
    return a_re, a_im, ce_re * b_re - ce_im * b_im, ce_re * b_im + ce_im * b_re


def _s5_blocks():
    gl = S5_G // N_BUNDLE
    half = gl * S5_P
    out = []
    for d in range(2):
        for g in range(S5_G):
            b, k = divmod(g, gl)
            dg = d * S5_G + g
            out.append((d, b, slice(k * S5_H, (k + 1) * S5_H), slice(k * S5_P, (k + 1) * S5_P),
                        slice(half + k * S5_P, half + (k + 1) * S5_P), slice(dg * S5_H, (dg + 1) * S5_H),
                        slice(dg, dg + 1)))
    return out


def _s5_params(lam_re, lam_im, log_dt, b_re, b_im, c_re, c_im):
    cw, sw = S5_W // N_BUNDLE, 2 * (S5_G // N_BUNDLE) * S5_P

    def body(lr, li, ld, br, bi, cr, ci, rep_ref, a1_ref, a2_ref, bm_ref, cm_ref):
        a_re, a_im, bb_re, bb_im = _s5_disc_math(lr[...], li[...], ld[...], br[...], bi[...], rep_ref[...])
        cc_re = cr[...]
        cc_im = ci[...]
        bm_ref[...] = jnp.zeros_like(bm_ref)
        cm_ref[...] = jnp.zeros_like(cm_ref)
        for d, b, rows, re, im, nat, one in _s5_blocks():
            bm_ref[d, b, rows, re] = bb_re[nat, :].astype(BF16)
            bm_ref[d, b, rows, im] = bb_im[nat, :].astype(BF16)
            cm_ref[d, b, rows, re] = cc_re[nat, :].astype(BF16)
            cm_ref[d, b, rows, im] = (-cc_im[nat, :]).astype(BF16)
            k = rows.start // S5_H
            lanes = slice((k % 2) * S5_P, (k % 2 + 1) * S5_P)
            for part, (v1, v2) in enumerate(((a_re[one, :], a_im[one, :]), (a_re[one, :], -a_im[one, :]))):
                sub = slice(4 * part + k // 2, 4 * part + k // 2 + 1)
                a1_ref[d, b, sub, lanes] = v1
                a2_ref[d, b, sub, lanes] = v2

    args = (lam_re, lam_im, log_dt, b_re, b_im, c_re, c_im, jnp.asarray(_repeat_onehot()))
    outs = [((2, N_BUNDLE, 8, 128), F32)] * 2 + [((2, N_BUNDLE, cw, sw), BF16)] * 2
    return pl.pallas_call(
        body, name="s5_params", grid=(1,), in_specs=[_full(a.shape) for a in args],
        out_specs=[_full(s) for s, _ in outs], out_shape=[_out(s, dt) for s, dt in outs],
    )(*_in_hbm(*args))


def _s5_params_bwd(lam_re, lam_im, log_dt, b_re, b_im, da, dbm, dcm):
    n, nb = 2 * S5_G, 2 * S5_G * S5_H

    def body(lr, li, ld, br, bi, rep_ref, da_ref, dbm_ref, dcm_ref, o_lr, o_li, o_ld, o_br, o_bi, o_cr, o_ci,
             dar_s, dai_s, dbr_s, dbi_s):
        for d, b, rows, re, im, nat, one in _s5_blocks():
            dbr_s[nat, :] = dbm_ref[d, b, rows, re]
            dbi_s[nat, :] = dbm_ref[d, b, rows, im]
            o_cr[nat, :] = dcm_ref[d, b, rows, re]
            o_ci[nat, :] = -dcm_ref[d, b, rows, im]
            dar_s[one, :] = da_ref[d, b, :, re]
            dai_s[one, :] = da_ref[d, b, :, im]
        rep = rep_ref[...]
        _, vjp = jax.vjp(lambda p, q, r, s, t: _s5_disc_math(p, q, r, s, t, rep),
                         lr[...], li[...], ld[...], br[...], bi[...])
        o_lr[...], o_li[...], o_ld[...], o_br[...], o_bi[...] = vjp((dar_s[...], dai_s[...], dbr_s[...], dbi_s[...]))

    args = (lam_re, lam_im, log_dt, b_re, b_im, jnp.asarray(_repeat_onehot()), da, dbm, dcm)
    outs = [(n, S5_P)] * 2 + [(n, 1)] + [(nb, S5_P)] * 4
    return pl.pallas_call(
        body, name="s5_params_bwd", grid=(1,), in_specs=[_full(a.shape) for a in args],
        out_specs=[_full(s) for s in outs], out_shape=[_out(s, F32) for s in outs],
        scratch_shapes=[pltpu.VMEM((n, S5_P), F32)] * 2 + [pltpu.VMEM((nb, S5_P), F32)] * 2,
    )(*_in_hbm(*args))


def _tiles_store(ref, base, val):
    for i in range(val.shape[0] // 8):
        for c in range(8):
            ref[pl.ds(base + (8 * i + c) * 8, 8), :] = val[8 * i:8 * i + 8, 128 * c:128 * (c + 1)]


def _tiles_load(ref, base, n):
    return jnp.concatenate(
        [jnp.concatenate([ref[pl.ds(base + (8 * i + c) * 8, 8), :] for c in range(8)], axis=1) for i in range(n // 8)],
        axis=0)


def _time_rows(base, t):
    return pl.ds(base + (t // 8) * 64 + t % 8, 8, stride=8)


def _scan(chains, n):
    xs = [c["x"] for c in chains]
    for k in range(n):
        for ci, c in enumerate(chains):
            t = n - 1 - k if c["reverse"] else k
            if c["prev"] is not None:
                c["prev"][_time_rows(c["prev_base"], t), :] = xs[ci]
            xs[ci] = c["a1"] * xs[ci] + pltpu.roll(c["a2"] * xs[ci], 4, axis=0) + c["src"][_time_rows(0, t), :]
            if c["dst"] is not None:
                c["dst"][_time_rows(0, t), :] = xs[ci]
    return xs


def _chain(x, a1, a2, src, dst=None, prev=None, prev_base=0, reverse=False):
    return dict(x=x, a1=a1, a2=a2, src=src, dst=dst, prev=prev, prev_base=prev_base, reverse=reverse)


def _s5_fwd(u, d_skip, a1, a2, bm, cm, length, comm=None, bounds=()):
    tp = u.shape[0]
    cw = S5_W // N_BUNDLE
    sw = bm.shape[-1]
    n_full, n_tail = divmod(length, SCAN_CHUNK)
    t_tail = n_full * SCAN_CHUNK

    nbs = N_BUNDLE

    def body(u_ref, d_ref, a1_ref, a2_ref, bm_ref, cm_ref, y_ref, bnd_ref, *scratch):
        y_ref[...] = u_ref[...] * d_ref[...]
        ins, xss = (scratch[0:nbs], scratch[nbs:2 * nbs]), (scratch[2 * nbs:3 * nbs], scratch[3 * nbs:])
        cols = [slice(b * cw, (b + 1) * cw) for b in range(nbs)]

        def keep(dr, chunk, xs):
            for b in range(nbs):
                bnd_ref[dr, b, chunk] = xs[b]

        def load(dr, t0, n):
            for b in range(nbs):
                _tiles_store(ins[dr][b], 0, _dot(u_ref[pl.ds(t0, n), cols[b]].astype(BF16), bm_ref[dr, b]))

        def chains(dr, xs):
            return [_chain(xs[b], a1_ref[dr, b], a2_ref[dr, b], ins[dr][b], dst=xss[dr][b], reverse=dr == 1)
                    for b in range(nbs)]

        def emit(dr, t0, n):
            for b in range(nbs):
                y_ref[pl.ds(t0, n), cols[b]] += _dg(_tiles_load(xss[dr][b], 0, n).astype(BF16), cm_ref[dr, b], NT)

        zero = (jnp.zeros((8, 128), F32),) * nbs
        xb = zero
        if n_tail:
            keep(1, n_full, xb)
            load(1, t_tail, n_tail)
            xb = tuple(_scan(chains(1, xb), n_tail))
            emit(1, t_tail, n_tail)

        def pair(i, carry):
            j = n_full - 1 - i
            t0s = (pl.multiple_of(i * SCAN_CHUNK, SCAN_CHUNK), pl.multiple_of(j * SCAN_CHUNK, SCAN_CHUNK))
            keep(0, i, carry[0])
            keep(1, j, carry[1])
            for dr in range(2):
                load(dr, t0s[dr], SCAN_CHUNK)
            out = _scan(chains(0, carry[0]) + chains(1, carry[1]), SCAN_CHUNK)
            for dr in range(2):
                emit(dr, t0s[dr], SCAN_CHUNK)
            return tuple(out[:nbs]), tuple(out[nbs:])

        xf, _ = lax.fori_loop(0, n_full, pair, (zero, xb))
        if n_tail:
            keep(0, n_full, xf)
            load(0, t_tail, n_tail)
            _scan(chains(0, xf), n_tail)
            emit(0, t_tail, n_tail)

    n_chunks = n_full + (1 if n_tail else 0)
    tile = pl.BlockSpec((2, nbs, 8, 128), lambda b: (0, b, 0, 0))
    return _call(
        body, comm, bounds, (u, d_skip, a1, a2, bm, cm), name="s5_fwd", grid=(N_BUNDLE // nbs,),
        in_specs=[pl.BlockSpec((tp, nbs * cw), lambda b: (0, b)), pl.BlockSpec((1, nbs * cw), lambda b: (0, b)),
                  tile, tile, pl.BlockSpec((2, nbs, cw, sw), lambda b: (0, b, 0, 0)),
                  pl.BlockSpec((2, nbs, cw, sw), lambda b: (0, b, 0, 0))],
        out_specs=[pl.BlockSpec((tp, nbs * cw), lambda b: (0, b)),
                   pl.BlockSpec((2, nbs, n_chunks, 8, 128), lambda b: (0, b, 0, 0, 0))],
        out_shape=[_out((tp, S5_W), F32), _out((2, N_BUNDLE, n_chunks, 8, 128), F32)],
        scratch_shapes=[pltpu.VMEM((SCAN_CHUNK * 8, 128), F32)] * (4 * nbs),
        compiler_params=_cp(("arbitrary",), 48))


def _s5_bwd(u, dy, d_skip, a1, a2, bm, cm, bnd, length):
    tp = u.shape[0]
    cw = S5_W // N_BUNDLE
    sw = bm.shape[-1]
    half = sw // 2
    n_full, n_tail = divmod(length, SCAN_CHUNK)
    t_tail = n_full * SCAN_CHUNK
    n_chunks = bnd.shape[2]
    nbs = 2

    def body(u_ref, dy_ref, d_ref, a1_ref, a2_ref, bm_ref, cm_ref, bnd_ref, du_ref, dd_ref, dbm_ref, dcm_ref,
             da_ref, *scratch):
        du_ref[...] = dy_ref[...] * d_ref[...]
        dd_ref[...] = jnp.sum(dy_ref[...] * u_ref[...], axis=0, keepdims=True)
        dbm_ref[...] = jnp.zeros_like(dbm_ref)
        dcm_ref[...] = jnp.zeros_like(dcm_ref)
        da_ref[...] = jnp.zeros_like(da_ref)
        bu_s, dx_s, g_s, xp_s, x_s = ([scratch[(k * 2 + dr) * nbs:(k * 2 + dr + 1) * nbs] for dr in range(2)]
                                      for k in range(5))
        cols = [slice(b * cw, (b + 1) * cw) for b in range(nbs)]

        def chains(dr, chunk, t0, n, gs):
            out = []
            for b in range(nbs):
                _tiles_store(bu_s[dr][b], 0, _dot(u_ref[pl.ds(t0, n), cols[b]].astype(BF16), bm_ref[dr, b]))
                _tiles_store(dx_s[dr][b], 0, _dot(dy_ref[pl.ds(t0, n), cols[b]].astype(BF16), cm_ref[dr, b]))
                out.append(_chain(bnd_ref[dr, b, chunk], a1_ref[dr, b], a2_ref[dr, b], bu_s[dr][b],
                                  dst=x_s[dr][b], prev=xp_s[dr][b], reverse=dr == 1))
                out.append(_chain(gs[b], a1_ref[dr, b], -a2_ref[dr, b], dx_s[dr][b], dst=g_s[dr][b], reverse=dr == 0))
            return out

        def emit(dr, t0, n):
            rows = pl.ds(t0, n)
            for b in range(nbs):
                ub = u_ref[rows, cols[b]].astype(BF16)
                dyb = dy_ref[rows, cols[b]].astype(BF16)
                g = _tiles_load(g_s[dr][b], 0, n)
                gb = g.astype(BF16)
                du_ref[rows, cols[b]] += _dg(gb, bm_ref[dr, b], NT)
                dbm_ref[dr, b] += _dg(ub, gb, TN)
                xp = _tiles_load(xp_s[dr][b], 0, n)
                xp_r, xp_i = xp[:, 0:half], xp[:, half:]
                g_r, g_i = g[:, 0:half], g[:, half:]
                dcm_ref[dr, b] += _dg(dyb, _tiles_load(x_s[dr][b], 0, n).astype(BF16), TN)
                da_ref[dr, b] += jnp.concatenate([jnp.sum(g_r * xp_r + g_i * xp_i, axis=0, keepdims=True),
                                                  jnp.sum(g_i * xp_r - g_r * xp_i, axis=0, keepdims=True)], axis=1)

        def adjoints(out):
            return tuple(out[1::2])

        zero = (jnp.zeros((8, 128), F32),) * nbs
        g0 = zero
        if n_tail:
            g0 = adjoints(_scan(chains(0, n_full, t_tail, n_tail, g0), n_tail))
            emit(0, t_tail, n_tail)

        def pair(i, carry):
            j = n_full - 1 - i
            t0 = (pl.multiple_of(j * SCAN_CHUNK, SCAN_CHUNK), pl.multiple_of(i * SCAN_CHUNK, SCAN_CHUNK))
            both = chains(0, j, t0[0], SCAN_CHUNK, carry[0]) + chains(1, i, t0[1], SCAN_CHUNK, carry[1])
            out = _scan(both, SCAN_CHUNK)
            emit(0, t0[0], SCAN_CHUNK)
            emit(1, t0[1], SCAN_CHUNK)
            return adjoints(out[:2 * nbs]), adjoints(out[2 * nbs:])

        _, g1 = lax.fori_loop(0, n_full, pair, (g0, zero))
        if n_tail:
            _scan(chains(1, n_full, t_tail, n_tail, g1), n_tail)
            emit(1, t_tail, n_tail)

    tile = pl.BlockSpec((2, nbs, 8, 128), lambda b: (0, b, 0, 0))
    wide = pl.BlockSpec((2, nbs, cw, sw), lambda b: (0, b, 0, 0))
    col = pl.BlockSpec((tp, nbs * cw), lambda b: (0, b))
    row = pl.BlockSpec((1, nbs * cw), lambda b: (0, b))
    arow = pl.BlockSpec((2, nbs, 1, sw), lambda b: (0, b, 0, 0))
    return pl.pallas_call(
        body, name="s5_bwd", grid=(N_BUNDLE // nbs,),
        in_specs=[col, col, row, tile, tile, wide, wide,
                  pl.BlockSpec((2, nbs, n_chunks, 8, 128), lambda b: (0, b, 0, 0, 0))],
        out_specs=[col, row, wide, wide, arow],
        out_shape=[_out((tp, S5_W), F32), _out((1, S5_W), F32),
                   _out((2, N_BUNDLE, cw, sw), F32), _out((2, N_BUNDLE, cw, sw), F32),
                   _out((2, N_BUNDLE, 1, sw), F32)],
        scratch_shapes=[pltpu.VMEM((SCAN_CHUNK * 8, 128), F32)] * (10 * nbs),
        compiler_params=_cp(("arbitrary",), 56),
    )(*_in_hbm(u, dy, d_skip, a1, a2, bm, cm, bnd))


def _row_tile(tp):
    return max(tm for tm in range(16, 449, 16) if tp % tm == 0)


def _step(x, target, bufs, gains, s5, rpb, c_arr, kc_arr, me_arr):
    n_tok = x.shape[0]
    first = ["ffn1_w_gate", "ffn1_w_up", "ffn1_w_down", "meta_tokens"]
    bias, got = _bias_tables(rpb, n_tok // GRID_W, _gather_comm([bufs[n] for n in first]), (0, N_HEADS - 1))
    w = dict(zip(first, got))
    meta = w["meta_tokens"].transpose(1, 0, 2).reshape(N_META, D)
    length = N_META + n_tok
    tp = length + 16
    tm = _row_tile(tp)
    tmb = tm
    n_rows = n_tok // GRID_W
    pad = jnp.zeros((tp - length, D), F32)
    h0 = jnp.concatenate([meta, x, pad], axis=0)
    tgt = jnp.concatenate([jnp.zeros((N_META, D), F32), target, pad], axis=0)

    s5p = (s5["lam_re"], s5["lam_im"], s5["log_dt"].reshape(2 * S5_G, 1), s5["b_re"], s5["b_im"])
    a1_m, a2_m, bm16, cm16 = _s5_params(*s5p, s5["c_re"], s5["c_im"])

    mid = ["w_in", "s5_w_glu", "w_out"]
    (h1, gate1, up1, f1), got = _ffn_fwd(
        "ffn1_fwd", h0, gains["ffn1_pre_g"], gains["ffn1_post_g"], w["ffn1_w_gate"], w["ffn1_w_up"], w["ffn1_w_down"],
        tm, _gather_comm([bufs[n] for n in mid]), (0, (tp // tm) * N_CHIP * 3 // 5))
    w.update(zip(mid, got))
    q, k, v, u = _mix_in(h1, gains["mix_pre_g"], w["w_in"], tm)
    (o_na,), (gate_ici, up_ici) = _attn_fwd(
        q, k, v, bias, n_tok, _gather_comm([bufs["ffn2_w_gate"], bufs["ffn2_w_up"]], pair=False), (0,))
    (y_pre, s5_bnd), (w["ffn2_w_gate"], w["ffn2_w_up"], down_ici) = _s5_fwd(
        u, gains["s5_d"], a1_m, a2_m, bm16, cm16, length,
        _merge_comm(_gather_comm([gate_ici, up_ici], ici=False),
                    _gather_comm([bufs["ffn2_w_down"]], pair=False)), (0,))
    w_glu = w["s5_w_glu"].reshape(S5_W, S5_W)
    w_out = w["w_out"].reshape(D, D)
    (h2, mix), (w["ffn2_w_down"],) = _mix_out(
        o_na, y_pre, h1, w_glu, gains["s5_b_glu"], gains["na_out_g"], gains["s5_out_g"], w_out, gains["mix_post_g"], tm,
        _gather_comm([down_ici], ici=False), (0,))
    (h3, gate2, up2, f2), _ = _ffn_fwd("ffn2_fwd", h2, gains["ffn2_pre_g"], gains["ffn2_post_g"],
                                       w["ffn2_w_gate"], w["ffn2_w_up"], w["ffn2_w_down"], tm)
    dh3, df2, loss, dg_final, dg_post2 = _final_loss(h3, gains["final_g"], tgt, f2, gains["ffn2_post_g"], n_tok, tm)

    ffn2 = ["ffn2_w_gate", "ffn2_w_up", "ffn2_w_down"]
    ffn1 = ["ffn1_w_gate", "ffn1_w_up", "ffn1_w_down"]
    out2, _ = _ffn_bwd("ffn2_bwd", h2, gains["ffn2_pre_g"], df2, gate2, up2,
                       w["ffn2_w_gate"], w["ffn2_w_up"], w["ffn2_w_down"], tmb)
    dxn2 = out2[3]
    sums2 = [_chip_sum("chip_sum_" + n, g, r, c_arr) for n, g, r in zip(ffn2, out2[0:3], out2[4:7])]
    (dh2, dg_pre2), _ = _ffn_pre_bwd("ffn2_pre_bwd", dh3, dxn2, h2, gains["ffn2_pre_g"], tm)
    do_na, dy_pre, dw_out, dw_glu, dg_mpost, dg_na, dg_s5, db_glu = _mix_out_bwd(
        dh2, mix, o_na, y_pre, w_glu, gains["s5_b_glu"], gains["na_out_g"], gains["s5_out_g"], w_out,
        gains["mix_post_g"], tm)
    (dq, dk, dv, dtb), recv3 = _attn_bwd(q, k, v, bias, do_na, n_tok, _scatter_comm(sums2), (0,))
    totals2 = [_total_sum("total_sum_" + n, s, r, kc_arr) for n, s, r in zip(ffn2, sums2, recv3)]
    du, dd, dbm, dcm, da_m = _s5_bwd(u, dy_pre, gains["s5_d"], a1_m, a2_m, bm16, cm16, s5_bnd, length)
    (dh1, df1, dw_in, dg_mpre, dg_post1), done2 = _mix_in_bwd(
        dq, dk, dv, du, h1, gains["mix_pre_g"], w["w_in"], dh2, f1, gains["ffn1_post_g"], tm,
        _assemble_comm(totals2), (0,))
    pieces = dict(zip(ffn2, done2))

    e, _ = _diag_onehot()
    n_dr = 2 * KH - 1
    drpb = _rpb_collapse(dtb.reshape(N_HEADS * n_dr, GRID_W * GRID_W), jnp.asarray(e.T))
    drpb = drpb[:, :2 * KW - 1].reshape(N_HEADS, n_dr, 2 * KW - 1).transpose(1, 0, 2).reshape(N_HEADS * n_dr, 2 * KW - 1)
    dlam_re, dlam_im, dlog_dt, db_re, db_im, dc_re, dc_im = _s5_params_bwd(*s5p, da_m, dbm, dcm)
    early = {"ffn1_post_g": dg_post1, "mix_pre_g": dg_mpre, "na_rpb": drpb,
             "s5_lam_re": dlam_re, "s5_lam_im": dlam_im, "s5_log_dt": dlog_dt.reshape(2, S5_G),
             "s5_b_re": db_re, "s5_b_im": db_im, "s5_c_re": dc_re, "s5_c_im": dc_im,
             "s5_d": dd, "s5_b_glu": db_glu, "na_out_g": dg_na,
             "s5_out_g": dg_s5, "mix_post_g": dg_mpost, "ffn2_pre_g": dg_pre2, "ffn2_post_g": dg_post2,
             "final_g": dg_final}
    names = list(early)
    slots = _small_pack([early[n] for n in names], me_arr)

    out1, slots = _ffn_bwd("ffn1_bwd", h0, gains["ffn1_pre_g"], df1, gate1, up1,
                           w["ffn1_w_gate"], w["ffn1_w_up"], w["ffn1_w_down"], tmb, _spread_comm(slots), (0,))
    small = dict(zip(names, _small_total(slots, [early[n].shape for n in names])))
    sums1 = [_chip_sum("chip_sum_" + n, g, r, c_arr) for n, g, r in zip(ffn1, out1[0:3], out1[4:7])]
    flight = _scatter_start(sums1)
    token = flight[4]
    rest = [dw_in, dw_glu.reshape(N_CHIP, S5_W // N_CHIP, S5_W), dw_out.reshape(N_CHIP, D // N_CHIP, D)]
    (dh0, dg_pre1), recv_rest = _ffn_pre_bwd("ffn1_pre_bwd", dh1, out1[3], h0, gains["ffn1_pre_g"] + token[0:1, 0:1],
                                             tm, _exchange_comm(rest), (0,))
    sums = [_chip_sum("chip_sum_" + n, g, r, c_arr) for n, g, r in zip(mid, rest, recv_rest)]
    return loss[0, 0], dh0, pieces, small, {"ffn1_pre_g": dg_pre1}, (ffn1, flight[:4]), (mid, sums)


def _mesh_pos():
    return lax.axis_index("x"), lax.axis_index("y"), lax.axis_index("c")


def _other_chips(x, y):
    return [(1 - x, y), (x, 1 - y), (1 - x, 1 - y)]


class _Comm:
    def __init__(self, ins, out_shape, aliases, parts):
        self.ins, self.out_shape, self.aliases, self.parts = list(ins), list(out_shape), dict(aliases), list(parts)
        self.n_sems = sum(p[0] for p in parts)

    def bases(self):
        out, base = [], 0
        for n_sems, _, _ in self.parts:
            out.append(base)
            base += n_sems
        return out


def _run_comm(name, comm):
    n_i, n_o = len(comm.ins), len(comm.out_shape)

    def body(*refs):
        ins, outs = refs[:n_i], refs[n_i:n_i + n_o]
        send_sems, recv_sems = refs[n_i + n_o:]
        for base, (_, start, finish) in zip(comm.bases(), comm.parts):
            start(ins, outs, send_sems, recv_sems, base)
            finish(ins, outs, send_sems, recv_sems, base)

    return pl.pallas_call(
        body, name=name, out_shape=comm.out_shape, in_specs=[ANY] * n_i, out_specs=[ANY] * n_o,
        input_output_aliases=comm.aliases,
        scratch_shapes=[pltpu.SemaphoreType.DMA((comm.n_sems,)), pltpu.SemaphoreType.DMA((comm.n_sems,))],
    )(*_in_hbm(*comm.ins))


def _call(body, comm, bounds, args, *, name, grid, in_specs, out_specs, out_shape, scratch_shapes=(),
          compiler_params=None):
    in_specs, out_specs, out_shape, scratch_shapes = list(in_specs), list(out_specs), list(out_shape), list(scratch_shapes)
    if comm is None:
        return pl.pallas_call(body, name=name, grid=grid, in_specs=in_specs, out_specs=out_specs, out_shape=out_shape,
                              scratch_shapes=scratch_shapes, compiler_params=compiler_params)(*_in_hbm(*args)), []
    n_in, n_out, n_scr = len(in_specs), len(out_specs), len(scratch_shapes)
    n_ci, n_co = len(comm.ins), len(comm.out_shape)
    n_steps = int(np.prod(grid))
    assert len(bounds) == len(comm.parts) and all(0 <= b < n_steps for b in bounds) and list(bounds) == sorted(bounds)

    def fused(*refs):
        a = n_in
        b = a + n_ci
        c = b + n_out
        d = c + n_co
        e = d + n_scr
        cargs = (refs[a:b], refs[c:d], refs[e], refs[e + 1])
        step = pl.program_id(0)
        for ax in range(1, len(grid)):
            step = step * grid[ax] + pl.program_id(ax)
        bases = comm.bases()
        for p, (_, start, finish) in enumerate(comm.parts):
            @pl.when(step == bounds[p])
            def _(p=p, start=start):
                if p > 0:
                    comm.parts[p - 1][2](*cargs, bases[p - 1])
                start(*cargs, bases[p])
        body(*(refs[:a] + refs[b:c] + refs[d:e]))

        @pl.when(step == n_steps - 1)
        def _():
            comm.parts[-1][2](*cargs, bases[-1])

    res = pl.pallas_call(
        fused, name=name, grid=grid, in_specs=in_specs + [ANY] * n_ci, out_specs=out_specs + [ANY] * n_co,
        out_shape=out_shape + comm.out_shape,
        scratch_shapes=scratch_shapes + [pltpu.SemaphoreType.DMA((comm.n_sems,)), pltpu.SemaphoreType.DMA((comm.n_sems,))],
        input_output_aliases={n_in + i: n_out + j for i, j in comm.aliases.items()},
        compiler_params=compiler_params)(*_in_hbm(*args, *comm.ins))
    return res[:n_out], res[n_out:]


def _remote(src, dst, send_sems, recv_sems, idx, to):
    return pltpu.make_async_remote_copy(src_ref=src, dst_ref=dst, send_sem=send_sems.at[idx],
                                        recv_sem=recv_sems.at[idx], device_id=to, device_id_type=MESH_ID)


def _gather_comm(bufs, ici=True, pair=True):
    n = len(bufs)

    def half(ref, k, pc):
        rh = ref.shape[1] // 2
        return ref.at[k, pl.ds(pc * rh, rh), :]

    def ici_start(ins, outs, ss, rs, base):
        x, y, c = _mesh_pos()
        for a in range(n):
            mine = half(outs[a], 2 * x + y, c)
            for j, chip in enumerate(_other_chips(x, y)):
                _remote(mine, mine, ss, rs, base + 3 * a + j, (*chip, c)).start()

    def ici_finish(ins, outs, ss, rs, base):
        x, y, c = _mesh_pos()
        for a in range(n):
            for j, chip in enumerate(_other_chips(x, y)):
                theirs = half(outs[a], 2 * chip[0] + chip[1], c)
                _remote(theirs, theirs, ss, rs, base + 3 * a + j, (*chip, c)).wait()

    def pair_copy(outs, ss, rs, base, a):
        x, y, c = _mesh_pos()
        rh = outs[a].shape[1] // 2
        held = outs[a].at[:, pl.ds(c * rh, rh), :]
        return _remote(held, held, ss, rs, base + a, (x, y, 1 - c))

    def pair_start(ins, outs, ss, rs, base):
        for a in range(n):
            pair_copy(outs, ss, rs, base, a).start()

    def pair_finish(ins, outs, ss, rs, base):
        for a in range(n):
            pair_copy(outs, ss, rs, base, a).wait()

    parts = ([(3 * n, ici_start, ici_finish)] if ici else []) + ([(n, pair_start, pair_finish)] if pair else [])
    return _Comm(bufs, [_out(b.shape, b.dtype) for b in bufs], {a: a for a in range(n)}, parts)


def _merge_comm(*comms):
    ins, shapes, aliases, subs, base = [], [], {}, [], 0
    for cm in comms:
        (n_sems, start, finish), = cm.parts
        i0, o0 = len(ins), len(shapes)
        subs.append((slice(i0, i0 + len(cm.ins)), slice(o0, o0 + len(cm.out_shape)), base, start, finish))
        aliases.update({i0 + i: o0 + j for i, j in cm.aliases.items()})
        ins += cm.ins
        shapes += cm.out_shape
        base += n_sems

    def start_all(ins_r, outs_r, ss, rs, b):
        for si, so, off, start, _ in subs:
            start(ins_r[si], outs_r[so], ss, rs, b + off)

    def finish_all(ins_r, outs_r, ss, rs, b):
        for si, so, off, _, finish in subs:
            finish(ins_r[si], outs_r[so], ss, rs, b + off)

    return _Comm(ins, shapes, aliases, [(base, start_all, finish_all)])


def _own_half_buffers(pieces, dtypes, kc_arr):
    n = len(pieces)

    def body(kc_ref, *refs):
        for a in range(n):
            refs[n + a][0] = refs[a][...].astype(dtypes[a])

    def half(p):
        return p.shape[0] // 2, p.shape[1]

    return pl.pallas_call(
        body, name="own_halves",
        out_shape=[_out((N_CHIP,) + p.shape, dt) for p, dt in zip(pieces, dtypes)],
        grid_spec=pltpu.PrefetchScalarGridSpec(
            num_scalar_prefetch=1, grid=(1,),
            in_specs=[pl.BlockSpec(half(p), lambda i, kc: (kc[1], 0)) for p in pieces],
            out_specs=[pl.BlockSpec((1,) + half(p), lambda i, kc: (kc[0], kc[1], 0)) for p in pieces]),
        compiler_params=_cp(("arbitrary",), 48),
    )(kc_arr, *_in_hbm(*pieces))


def _exchange_comm(grads):
    n = len(grads)

    def copy(ins, outs, ss, rs, base, a):
        x, y, c = _mesh_pos()
        rh = ins[a].shape[1] // 2
        return _remote(ins[a].at[:, pl.ds((1 - c) * rh, rh), :], outs[a], ss, rs, base + a, (x, y, 1 - c))

    def start(ins, outs, ss, rs, base):
        for a in range(n):
            copy(ins, outs, ss, rs, base, a).start()

    def finish(ins, outs, ss, rs, base):
        for a in range(n):
            copy(ins, outs, ss, rs, base, a).wait()

    shapes = [_out((N_CHIP, g.shape[1] // 2, g.shape[2]), g.dtype) for g in grads]
    return _Comm(grads, shapes, {}, [(n, start, finish)])


def _chip_sum(name, g, recv, c_arr):
    _, r, cc = g.shape
    rh = r // 2

    def body(c_ref, g_ref, r_ref, o_ref):
        o_ref[...] = (g_ref[...] + r_ref[...]).astype(BF16)

    return pl.pallas_call(
        body, name=name, out_shape=_out((N_CHIP, rh, cc), BF16),
        grid_spec=pltpu.PrefetchScalarGridSpec(
            num_scalar_prefetch=1, grid=(N_CHIP,),
            in_specs=[pl.BlockSpec((1, rh, cc), lambda j, c_ref: (j, c_ref[0], 0)),
                      pl.BlockSpec((1, rh, cc), lambda j, c_ref: (j, 0, 0))],
            out_specs=pl.BlockSpec((1, rh, cc), lambda j, c_ref: (j, 0, 0))),
        compiler_params=_cp(("arbitrary",), 32),
    )(c_arr, *_in_hbm(g, recv))


def _scatter_comm(sums):
    n = len(sums)

    def copies(ins, outs, ss, rs, base):
        x, y, c = _mesh_pos()
        return [_remote(ins[a].at[2 * chip[0] + chip[1]], outs[a].at[j], ss, rs, base + 3 * a + j, (*chip, c))
                for a in range(n) for j, chip in enumerate(_other_chips(x, y))]

    def start(ins, outs, ss, rs, base):
        for cp in copies(ins, outs, ss, rs, base):
            cp.start()

    def finish(ins, outs, ss, rs, base):
        for cp in copies(ins, outs, ss, rs, base):
            cp.wait()

    shapes = [_out((3,) + s.shape[1:], s.dtype) for s in sums]
    return _Comm(sums, shapes, {}, [(3 * n, start, finish)])


def _scatter_copies(ins, lands, send_sems, recv_sems):
    x, y, c = _mesh_pos()
    return [_remote(ins[a].at[2 * chip[0] + chip[1]], lands[a].at[j], send_sems, recv_sems, 3 * a + j, (*chip, c))
            for a in range(len(ins)) for j, chip in enumerate(_other_chips(x, y))]


def _scatter_start(sums):
    n = len(sums)
    lands = [lax.empty((3,) + s.shape[1:], s.dtype) for s in sums]
    hbm = pl.BlockSpec(memory_space=pltpu.HBM)
    sem = pl.BlockSpec(memory_space=pltpu.SEMAPHORE)

    def body(*refs):
        ins, land_refs = refs[:n], refs[n:2 * n]
        send_sems, recv_sems = refs[2 * n], refs[2 * n + 1]
        token = refs[-1]
        for cp in _scatter_copies(ins, land_refs, send_sems, recv_sems):
            cp.start()
        token[...] = jnp.zeros_like(token)

    res = pl.pallas_call(
        body, name="ffn1_scatter_start",
        out_shape=(pltpu.SemaphoreType.DMA((3 * n,)), pltpu.SemaphoreType.DMA((3 * n,)),
                   *[pltpu.HBM(s.shape, s.dtype) for s in sums], *[pltpu.HBM(ld.shape, ld.dtype) for ld in lands],
                   jax.ShapeDtypeStruct((8, 128), F32)),
        in_specs=[hbm] * (2 * n), out_specs=(sem, sem, *[hbm] * (2 * n), pl.BlockSpec(memory_space=pltpu.VMEM)),
        input_output_aliases={i: 2 + i for i in range(2 * n)},
        compiler_params=pltpu.CompilerParams(has_side_effects=pltpu.SideEffectType.DATAFLOW_SIDE_EFFECTING),
    )(*[pltpu.with_memory_space_constraint(a, pltpu.HBM) for a in list(sums) + lands])
    return res[0], res[1], list(res[2:2 + n]), list(res[2 + n:2 + 2 * n]), res[-1]


def _scatter_wait(send_sems, recv_sems, sums, lands, after):
    n = len(sums)
    hbm = pl.BlockSpec(memory_space=pltpu.HBM)
    sem = pl.BlockSpec(memory_space=pltpu.SEMAPHORE)

    def body(*refs):
        ins, land_refs = refs[:n], refs[n:2 * n]
        for cp in _scatter_copies(ins, land_refs, refs[2 * n], refs[2 * n + 1]):
            cp.wait_send()
            cp.wait_recv()

    res = pl.pallas_call(
        body, name="ffn1_scatter_wait",
        out_shape=tuple([pltpu.HBM(s.shape, s.dtype) for s in sums] + [pltpu.HBM(ld.shape, ld.dtype) for ld in lands]),
        in_specs=[hbm] * (2 * n) + [sem, sem, pl.BlockSpec(memory_space=pl.ANY)], out_specs=tuple([hbm] * (2 * n)),
        input_output_aliases={i: i for i in range(2 * n)},
        compiler_params=pltpu.CompilerParams(has_side_effects=pltpu.SideEffectType.DATAFLOW_SIDE_EFFECTING),
    )(*sums, *lands, send_sems, recv_sems, after)
    return list(res[:n]), list(res[n:])


def _total_sum(name, sums, recv3, kc_arr):
    _, rh, cc = sums.shape

    def body(kc_ref, s_ref, r_ref, o_ref):
        t = s_ref[0].astype(F32) + r_ref[0].astype(F32)
        t = t + r_ref[1].astype(F32)
        o_ref[...] = t + r_ref[2].astype(F32)

    return pl.pallas_call(
        body, name=name, out_shape=_out((2 * rh, cc), F32),
        grid_spec=pltpu.PrefetchScalarGridSpec(
            num_scalar_prefetch=1, grid=(1,),
            in_specs=[pl.BlockSpec((1, rh, cc), lambda i, kc_ref: (kc_ref[0], 0, 0)),
                      pl.BlockSpec((3, rh, cc), lambda i, kc_ref: (0, 0, 0))],
            out_specs=pl.BlockSpec((rh, cc), lambda i, kc_ref: (kc_ref[1], 0))),
        compiler_params=_cp(("arbitrary",), 32),
    )(kc_arr, *_in_hbm(sums, recv3))


def _assemble_comm(totals):
    n = len(totals)

    def copy(outs, ss, rs, base, a):
        x, y, c = _mesh_pos()
        rh = outs[a].shape[0] // 2
        here = outs[a].at[pl.ds(c * rh, rh), :]
        return _remote(here, here, ss, rs, base + a, (x, y, 1 - c))

    def start(ins, outs, ss, rs, base):
        for a in range(n):
            copy(outs, ss, rs, base, a).start()

    def finish(ins, outs, ss, rs, base):
        for a in range(n):
            copy(outs, ss, rs, base, a).wait()

    shapes = [_out(t.shape, t.dtype) for t in totals]
    return _Comm(totals, shapes, {a: a for a in range(n)}, [(n, start, finish)])


def _small_layout(shapes):
    n = len(shapes)
    narrow_w = 64
    wide = [a for a in range(n) if shapes[a][1] > narrow_w]
    narrow = sorted((a for a in range(n) if shapes[a][1] <= narrow_w), key=lambda a: -shapes[a][0])
    offs, cols, groups, widths, rows = {}, {}, [], [], []
    if wide:
        r = 0
        for a in wide:
            offs[a], cols[a] = r, 0
            r += shapes[a][0]
        groups.append(wide)
        widths.append(max(shapes[a][1] for a in wide))
        rows.append(-(-r // 8) * 8)
    if narrow:
        heights = [0, 0]
        for a in narrow:
            side = 0 if heights[0] <= heights[1] else 1
            offs[a], cols[a] = heights[side], side * narrow_w
            heights[side] += shapes[a][0]
        groups.append(narrow)
        widths.append(2 * narrow_w)
        rows.append(-(-max(heights) // 8) * 8)

    def window(ref, a):
        return ref.at[offs[a]:offs[a] + shapes[a][0], cols[a]:cols[a] + shapes[a][1]]

    return groups, widths, rows, window


def _small_pack(arrays, me_arr):
    shapes = [a.shape for a in arrays]
    groups, widths, rows, window = _small_layout(shapes)
    n, n_g = len(arrays), len(groups)

    def body(me_ref, *refs):
        ins, outs = refs[:n], refs[n:]
        for gi, g in enumerate(groups):
            outs[gi][...] = jnp.zeros_like(outs[gi])
            for a in g:
                window(outs[gi].at[0], a)[...] = ins[a][...]

    return pl.pallas_call(
        body, name="small_pack", out_shape=[_out((8, r, w), F32) for r, w in zip(rows, widths)],
        grid_spec=pltpu.PrefetchScalarGridSpec(
            num_scalar_prefetch=1, grid=(1,), in_specs=[pl.BlockSpec(s, lambda i, me: (0, 0)) for s in shapes],
            out_specs=[pl.BlockSpec((1, r, w), lambda i, me: (me[0], 0, 0)) for r, w in zip(rows, widths)]),
        compiler_params=_cp(("arbitrary",), 32),
    )(me_arr, *_in_hbm(*arrays))


def _spread_comm(slots):
    n = len(slots)
    flips = [(dx, dy, dc) for dx in range(2) for dy in range(2) for dc in range(2)][1:]

    def copies(outs, ss, rs, base):
        x, y, c = _mesh_pos()
        mine = 4 * x + 2 * y + c
        return [_remote(outs[a].at[mine], outs[a].at[mine], ss, rs, base + 7 * a + f,
                        (x ^ dx, y ^ dy, c ^ dc)) for a in range(n) for f, (dx, dy, dc) in enumerate(flips)]

    def start(ins, outs, ss, rs, base):
        for cp in copies(outs, ss, rs, base):
            cp.start()

    def finish(ins, outs, ss, rs, base):
        for cp in copies(outs, ss, rs, base):
            cp.wait()

    return _Comm(slots, [_out(s.shape, s.dtype) for s in slots], {a: a for a in range(n)}, [(7 * n, start, finish)])


def _small_total(slots, shapes):
    groups, widths, rows, window = _small_layout(shapes)
    n, n_g = len(shapes), len(groups)

    def body(*refs):
        ins, outs, acc = refs[:n_g], refs[n_g:n_g + n], refs[n_g + n:]
        for gi, g in enumerate(groups):
            t = ins[gi][0] + ins[gi][1]
            for d in range(2, 8):
                t = t + ins[gi][d]
            acc[gi][...] = t
            for a in g:
                outs[a][...] = window(acc[gi], a)[...]

    return pl.pallas_call(
        body, name="small_total", grid=(1,), out_shape=[_out(s, F32) for s in shapes],
        in_specs=[_full(s.shape) for s in slots], out_specs=[_full(s) for s in shapes],
        scratch_shapes=[pltpu.VMEM((r, w), F32) for r, w in zip(rows, widths)],
        compiler_params=_cp(("arbitrary",), 48),
    )(*_in_hbm(*slots))


def _small_allreduce(arrays, comm):
    n = len(arrays)
    shapes = [a.shape for a in arrays]
    groups, widths, rows, window = _small_layout(shapes)
    n_g = len(groups)

    def body(*refs):
        ins, outs = refs[:n], refs[n:2 * n]
        pack, sib, csum, every = (refs[2 * n + i * n_g:2 * n + (i + 1) * n_g] for i in range(4))
        send_sems, recv_sems = refs[2 * n + 4 * n_g:]
        x, y, c = _mesh_pos()
        k = 2 * x + y
        for gi, g in enumerate(groups):
            pack[gi][...] = jnp.zeros_like(pack[gi])
            for a in g:
                window(pack[gi], a)[...] = ins[a][...]
        cps = [_remote(pack[gi], sib[gi], send_sems, recv_sems, gi, (x, y, 1 - c)) for gi in range(n_g)]
        for cp in cps:
            cp.start()
        for cp in cps:
            cp.wait()
        for gi in range(n_g):
            csum[gi][...] = pack[gi][...] + sib[gi][...]
            every[gi][k] = csum[gi][...]
        cps = [_remote(csum[gi], every[gi].at[k], send_sems, recv_sems, n_g + 3 * gi + j, (*chip, c))
               for gi in range(n_g) for j, chip in enumerate(_other_chips(x, y))]
        for cp in cps:
            cp.start()
        for cp in cps:
            cp.wait()
        for gi, g in enumerate(groups):
            pack[gi][...] = ((every[gi][0] + every[gi][1]) + every[gi][2]) + every[gi][3]
            for a in g:
                outs[a][...] = window(pack[gi], a)[...]

    bufs = [pltpu.VMEM((r, w), F32) for r, w in zip(rows, widths)]
    return _call(
        body, comm, (0,), arrays, name="small_allreduce", grid=(1,), out_shape=[_out(s, F32) for s in shapes],
        in_specs=[_full(s) for s in shapes], out_specs=[_full(s) for s in shapes],
        scratch_shapes=bufs * 3 + [pltpu.VMEM((N_CHIP, r, w), F32) for r, w in zip(rows, widths)] +
                       [pltpu.SemaphoreType.DMA((4 * n_g,)), pltpu.SemaphoreType.DMA((4 * n_g,))],
        compiler_params=_cp(("arbitrary",), 40))


def _adamw_small(ws, gs, ms, vs, comm):
    n = len(ws)

    def body(*refs):
        w, g, m, v, d, mo, vo = (refs[i * n:(i + 1) * n] for i in range(7))
        for a in range(n):
            d[a][...], mo[a][...], vo[a][...] = _adamw_math(w[a][...], g[a][...], m[a][...], v[a][...])

    specs = [_full(w.shape) for w in ws]
    res, got = _call(
        body, comm, (0,), (*ws, *gs, *ms, *vs), name="adamw_small", grid=(1,),
        out_shape=[_out(w.shape, F32) for w in ws] * 3,
        in_specs=specs * 4, out_specs=specs * 3, compiler_params=_cp(("arbitrary",), 40))
    return (res[:n], res[n:2 * n], res[2 * n:]), got


def _adamw_math(w, g, m, v):
    m = ADAM_B1 * m + (1.0 - ADAM_B1) * g
    v = ADAM_B2 * v + (1.0 - ADAM_B2) * (g * g)
    m_hat = m / (1.0 - ADAM_B1 ** ADAM_STEP)
    v_hat = v / (1.0 - ADAM_B2 ** ADAM_STEP)
    delta = -ADAM_LR * (m_hat / (jnp.sqrt(v_hat) + ADAM_EPS) + ADAM_WD * w)
    return delta, m, v


def _adamw(name, w, g, m, v):
    r, c = w.shape
    tr = max(t for t in range(8, 513, 8) if r % t == 0)

    def body(w_ref, g_ref, m_ref, v_ref, d_ref, mo_ref, vo_ref):
        d_ref[...], mo_ref[...], vo_ref[...] = _adamw_math(w_ref[...], g_ref[...], m_ref[...], v_ref[...])

    return pl.pallas_call(
        body, name=name, grid=(r // tr,), in_specs=[_rows(tr, c)] * 4, out_specs=[_rows(tr, c)] * 3,
        out_shape=[_out((r, c), F32)] * 3, compiler_params=_cp(("arbitrary",), 32),
    )(*_in_hbm(w, g, m, v))


def _as_matrix(name, a):
    if name == "na_rpb":
        return a[0].transpose(1, 0, 2).reshape(N_HEADS * (2 * KH - 1), 2 * KW - 1)
    if name in ("s5_b_re", "s5_b_im"):
        return a.transpose(0, 1, 2, 4, 3).reshape(2 * S5_G * S5_H, S5_P)
    if name in ("s5_c_re", "s5_c_im"):
        return a.reshape(2 * S5_G * S5_H, S5_P)
    if name in ("s5_lam_re", "s5_lam_im"):
        return a.reshape(2 * S5_G, S5_P)
    if name == "s5_log_dt":
        return a.reshape(2, S5_G)
    return a


def _from_matrix(name, m):
    if name == "na_rpb":
        return m.reshape(2 * KH - 1, N_HEADS, 2 * KW - 1).transpose(1, 0, 2)[None]
    if name in ("s5_b_re", "s5_b_im"):
        return m.reshape(1, 2, S5_G, S5_H, S5_P).transpose(0, 1, 2, 4, 3)
    if name in ("s5_c_re", "s5_c_im"):
        return m.reshape(1, 2, S5_G, S5_H, S5_P)
    if name in ("s5_lam_re", "s5_lam_im"):
        return m.reshape(1, 2, S5_G, S5_P)
    if name == "s5_log_dt":
        return m.reshape(1, 2, S5_G)
    return m


WEIGHTS = ["meta_tokens", "ffn1_pre_g", "ffn1_post_g", "ffn1_w_gate", "ffn1_w_up", "ffn1_w_down", "mix_pre_g", "w_in",
           "na_rpb", "s5_lam_re", "s5_lam_im", "s5_log_dt", "s5_b_re", "s5_b_im", "s5_c_re", "s5_c_im", "s5_d",
           "s5_w_glu", "s5_b_glu", "na_out_g", "s5_out_g", "w_out", "mix_post_g", "ffn2_pre_g", "ffn2_post_g",
           "ffn2_w_gate", "ffn2_w_up", "ffn2_w_down", "final_g"]
BIG = ["ffn1_w_gate", "ffn1_w_up", "ffn1_w_down", "w_in", "s5_w_glu", "w_out", "ffn2_w_gate", "ffn2_w_up",
       "ffn2_w_down"]
TRANSPOSED = ["ffn1_w_gate", "ffn1_w_up", "ffn2_w_gate", "ffn2_w_up"]
GAINS = ["ffn1_pre_g", "ffn1_post_g", "mix_pre_g", "s5_d", "s5_b_glu", "na_out_g", "s5_out_g", "mix_post_g",
         "ffn2_pre_g", "ffn2_post_g", "final_g"]
SMALL = [n for n in WEIGHTS if n not in BIG]


def kernel(*args):
    names = ["x"] + WEIGHTS + ["loss_target"] + ["m_" + n for n in WEIGHTS] + ["v_" + n for n in WEIGHTS]
    assert len(args) == len(names)
    given = dict(zip(names, args))
    x_pos, y_pos, c_pos = _mesh_pos()
    k_pos = 2 * x_pos + y_pos
    c_arr = jnp.reshape(c_pos, (1,)).astype(jnp.int32)
    kc_arr = jnp.stack([k_pos, c_pos]).astype(jnp.int32)

    def piece(name, a):
        return a[0].T if name in TRANSPOSED else a[0]

    def unpiece(name, a):
        return a.T[None] if name in TRANSPOSED else a[None]

    placed = BIG + ["meta_tokens"]
    bufs = dict(zip(placed, _own_half_buffers([piece(n, given[n]) for n in BIG] + [given["meta_tokens"]],
                                              [BF16] * len(BIG) + [F32], kc_arr)))

    gains = {n: given[n] for n in GAINS}
    s5 = {n: _as_matrix("s5_" + n, given["s5_" + n])
          for n in ["lam_re", "lam_im", "log_dt", "b_re", "b_im", "c_re", "c_im"]}
    me_arr = jnp.reshape(4 * x_pos + 2 * y_pos + c_pos, (1,)).astype(jnp.int32)
    loss, dh0, pieces, small, late, (ffn1, flight), (mid, sums) = _step(
        given["x"][0], given["loss_target"][0], bufs, gains, s5, given["na_rpb"][0], c_arr, kc_arr, me_arr)
    loss = lax.psum(loss, ("x", "y", "c"))
    n_tok = given["x"].shape[1]
    grad_x = dh0[N_META:N_META + n_tok][None]

    late["meta_tokens"] = dh0[:N_META]
    out_g, out_d, out_m, out_v = {}, {}, {}, {}

    def update_big(n):
        g2 = pieces[n]
        d2, m2, v2 = _adamw("adamw_" + n, piece(n, given[n]), g2, piece(n, given["m_" + n]),
                            piece(n, given["v_" + n]))
        out_g[n], out_d[n], out_m[n], out_v[n] = (unpiece(n, t) for t in (g2, d2, m2, v2))
        return v2

    scatter = _scatter_comm(sums)
    scatter.ins += [update_big(n) for n in pieces] + [small["final_g"]]
    red, recv3 = _small_allreduce(list(late.values()), scatter)
    small.update(zip(late, red))
    mc = D // N_CHIP
    small["meta_tokens"] = lax.dynamic_slice_in_dim(small["meta_tokens"], k_pos * mc, mc, 1)
    send_sems, recv_sems, sums1, lands1 = flight
    sums1, recv3_1 = _scatter_wait(send_sems, recv_sems, sums1, lands1, red[0])
    last = ffn1 + mid
    totals = [_total_sum("total_sum_" + n, s, r, kc_arr)
              for n, s, r in zip(last, sums1 + list(sums), recv3_1 + list(recv3))]
    gs = [small[n] for n in SMALL]
    (d2, m2, v2), done = _adamw_small([_as_matrix(n, given[n]) for n in SMALL], gs,
                                      [_as_matrix(n, given["m_" + n]) for n in SMALL],
                                      [_as_matrix(n, given["v_" + n]) for n in SMALL], _assemble_comm(totals))
    pieces.update(zip(last, done))

    for n, g, dd, mm, vv in zip(SMALL, gs, d2, m2, v2):
        out_g[n], out_d[n], out_m[n], out_v[n] = (_from_matrix(n, t) for t in (g, dd, mm, vv))
    for n in last:
        update_big(n)
    return (loss, grad_x, *[out_g[n] for n in WEIGHTS], *[out_d[n] for n in WEIGHTS],
            *[out_m[n] for n in WEIGHTS], *[out_v[n] for n in WEIGHTS])
```

```python
import functools
import math

import numpy as np
import jax
import jax.numpy as jnp
from jax import lax
from jax.experimental import pallas as pl
from jax.experimental.pallas import tpu as pltpu

F32 = jnp.float32
BF16 = jnp.bfloat16

D = 1024
N_META = 16
GRID_W = 64
NA_W = 512
S5_W = 512
HEAD_DIM = 64
N_HEADS = 8
KH = 8
KW = 16
S5_G = 32
S5_P = 64
S5_H = 16
N_BUNDLE = 4
FF = 2816
N_CHIP = 4
FC = FF // N_CHIP
EPS = 1e-6
NEG_INF = -1e30
Q_ROWS = 4
K_ROWS = 12
QB = Q_ROWS * GRID_W
KB = K_ROWS * GRID_W
SCAN_CHUNK = 256

ADAM_LR = 0.001
ADAM_B1 = 0.9
ADAM_B2 = 0.999
ADAM_EPS = 1e-08
ADAM_WD = 0.01
ADAM_STEP = 10

NT = (((1,), (1,)), ((), ()))
TN = (((0,), (0,)), ((), ()))
MESH_ID = pl.DeviceIdType.MESH


def _cp(sem=None, vmem_mb=None):
    kw = {}
    if sem is not None:
        kw["dimension_semantics"] = sem
    if vmem_mb is not None:
        kw["vmem_limit_bytes"] = vmem_mb << 20
    return pltpu.CompilerParams(**kw)


def _full(shape):
    n = len(shape)
    return pl.BlockSpec(shape, lambda *_: (0,) * n)


def _rows(tm, w):
    return pl.BlockSpec((tm, w), lambda i: (i, 0))


ANY = pl.BlockSpec(memory_space=pl.ANY)


def _rms(x, g):
    r = lax.rsqrt(jnp.mean(x * x, axis=-1, keepdims=True) + EPS)
    return x * r * g


def _rms_bwd(x, g, dy):
    r = lax.rsqrt(jnp.mean(x * x, axis=-1, keepdims=True) + EPS)
    xh = x * r
    dg = jnp.sum(dy * xh, axis=0, keepdims=True)
    dyg = dy * g
    dx = r * (dyg - xh * jnp.mean(dyg * xh, axis=-1, keepdims=True))
    return dx, dg


def _out(shape, dtype):
    return pltpu.HBM(tuple(shape), dtype)


def _in_hbm(*args):
    return [pltpu.with_memory_space_constraint(a, pltpu.HBM) if jnp.issubdtype(a.dtype, jnp.floating) and a.ndim > 1
            else a for a in args]


def _dot(a, b):
    return jnp.dot(a, b, preferred_element_type=F32)


def _dg(a, b, dims):
    return lax.dot_general(a, b, dims, preferred_element_type=F32)


def _ffn_fwd(name, h, g_pre, g_post, wg, wu, wd, tm, comm=None, bounds=()):
    tp = h.shape[0]
    nt = tp // tm

    def body(h_ref, gp_ref, gq_ref, wg_ref, wu_ref, wd_ref, hn_ref, gate_ref, up_ref, f_ref, xn_s, acc_s):
        c = pl.program_id(1)

        @pl.when(c == 0)
        def _():
            xn_s[...] = _rms(h_ref[...], gp_ref[...]).astype(BF16)
            acc_s[...] = jnp.zeros_like(acc_s)

        xn = xn_s[...]
        gate = _dg(xn, wg_ref[0], NT)
        up = _dg(xn, wu_ref[0], NT)
        gate_ref[0] = gate
        up_ref[0] = up
        act = (gate * jax.nn.sigmoid(gate) * up).astype(BF16)
        acc_s[...] += _dot(act, wd_ref[0])

        @pl.when(c == N_CHIP - 1)
        def _():
            f = acc_s[...]
            f_ref[...] = f
            hn_ref[...] = h_ref[...] + 0.5 * _rms(f, gq_ref[...])

    return _call(
        body, comm, bounds, (h, g_pre, g_post, wg, wu, wd), name=name, grid=(nt, N_CHIP),
        in_specs=[pl.BlockSpec((tm, D), lambda i, c: (i, 0)), _full((1, D)), _full((1, D))] +
                 [pl.BlockSpec((1, FC, D), lambda i, c: (c, 0, 0))] * 3,
        out_specs=[pl.BlockSpec((tm, D), lambda i, c: (i, 0)),
                   pl.BlockSpec((1, tm, FC), lambda i, c: (c, i, 0)),
                   pl.BlockSpec((1, tm, FC), lambda i, c: (c, i, 0)),
                   pl.BlockSpec((tm, D), lambda i, c: (i, 0))],
        out_shape=[_out((tp, D), F32), _out((N_CHIP, tp, FC), F32),
                   _out((N_CHIP, tp, FC), F32), _out((tp, D), F32)],
        scratch_shapes=[pltpu.VMEM((tm, D), BF16), pltpu.VMEM((tm, D), F32)],
        compiler_params=_cp(("arbitrary", "arbitrary"), 48))


def _ffn_bwd(name, h, g_pre, df, gate, up, wg, wu, wd, tm, comm=None, bounds=()):
    tp = h.shape[0]
    nt = tp // tm
    rh = FC // 2

    def body(h_ref, gp_ref, df_ref, gate_ref, up_ref, wg_ref, wu_ref, wd_ref,
             dwg_ref, dwu_ref, dwd_ref, dxn_ref, rg_ref, ru_ref, rd_ref, ag, au, ad, send_sems, recv_sems):
        c = pl.program_id(0)
        i = pl.program_id(1)

        def to_sibling(a, piece):
            x, y, core = _mesh_pos()
            dw_ref, r_ref = ((dwg_ref, rg_ref), (dwu_ref, ru_ref), (dwd_ref, rd_ref))[a]
            return _remote(dw_ref.at[piece, pl.ds((1 - core) * rh, rh), :], r_ref.at[piece], send_sems, recv_sems,
                           3 * piece + a, (x, y, 1 - core))

        @pl.when(i == 0)
        def _():
            ag[...] = jnp.zeros_like(ag)
            au[...] = jnp.zeros_like(au)
            ad[...] = jnp.zeros_like(ad)

        xn = _rms(h_ref[...], gp_ref[...]).astype(BF16)
        dfb = df_ref[...].astype(BF16)
        gt = gate_ref[0]
        u = up_ref[0]
        sg = jax.nn.sigmoid(gt)
        si = gt * sg
        act = (si * u).astype(BF16)
        dact = _dg(dfb, wd_ref[0], NT)
        ad[...] += _dg(act, dfb, TN)
        dgate = (dact * u * (sg * (1.0 + gt * (1.0 - sg)))).astype(BF16)
        dup = (dact * si).astype(BF16)
        ag[...] += _dg(dgate, xn, TN)
        au[...] += _dg(dup, xn, TN)
        dxn_ref[0] = _dot(dgate, wg_ref[0]) + _dot(dup, wu_ref[0])

        @pl.when(i == nt - 1)
        def _():
            pltpu.sync_copy(ag, dwg_ref.at[c])
            pltpu.sync_copy(au, dwu_ref.at[c])
            pltpu.sync_copy(ad, dwd_ref.at[c])
            for a in range(3):
                to_sibling(a, c).start()

        @pl.when((c == N_CHIP - 1) & (i == nt - 1))
        def _():
            for piece in range(N_CHIP):
                for a in range(3):
                    to_sibling(a, piece).wait()

    return _call(
        body, comm, bounds, (h, g_pre, df, gate, up, wg, wu, wd), name=name, grid=(N_CHIP, nt),
        in_specs=[pl.BlockSpec((tm, D), lambda c, i: (i, 0)), _full((1, D)),
                  pl.BlockSpec((tm, D), lambda c, i: (i, 0)),
                  pl.BlockSpec((1, tm, FC), lambda c, i: (c, i, 0)),
                  pl.BlockSpec((1, tm, FC), lambda c, i: (c, i, 0))] +
                 [pl.BlockSpec((1, FC, D), lambda c, i: (c, 0, 0))] * 3,
        out_specs=[ANY, ANY, ANY, pl.BlockSpec((1, tm, D), lambda c, i: (c, i, 0)), ANY, ANY, ANY],
        out_shape=[_out((N_CHIP, FC, D), F32)] * 3 + [_out((N_CHIP, tp, D), F32)] +
                  [_out((N_CHIP, rh, D), F32)] * 3,
        scratch_shapes=[pltpu.VMEM((FC, D), F32)] * 3 +
                       [pltpu.SemaphoreType.DMA((3 * N_CHIP,)), pltpu.SemaphoreType.DMA((3 * N_CHIP,))],
        compiler_params=_cp(("arbitrary", "arbitrary"), 58))


def _ffn_pre_bwd(name, dh, dxn_part, h, g_pre, tm, comm=None, bounds=()):
    tp = h.shape[0]
    nt = tp // tm

    def body(dh_ref, dxn_ref, h_ref, gp_ref, out_ref, dg_ref):
        i = pl.program_id(0)
        dxn = (dxn_ref[0] + dxn_ref[1]) + (dxn_ref[2] + dxn_ref[3])
        dx, dg = _rms_bwd(h_ref[...], gp_ref[...], dxn)
        out_ref[...] = dh_ref[...] + dx

        @pl.when(i == 0)
        def _():
            dg_ref[...] = jnp.zeros_like(dg_ref)

        dg_ref[...] += dg

    return _call(
        body, comm, bounds, (dh, dxn_part, h, g_pre), name=name, grid=(nt,),
        in_specs=[_rows(tm, D), pl.BlockSpec((N_CHIP, tm, D), lambda i: (0, i, 0)), _rows(tm, D), _full((1, D))],
        out_specs=[_rows(tm, D), _full((1, D))],
        out_shape=[_out((tp, D), F32), _out((1, D), F32)],
        compiler_params=_cp(("arbitrary",), 48))


def _mix_in(h, g, w_in, tm):
    tp = h.shape[0]

    def body(h_ref, g_ref, w_ref, q_ref, k_ref, v_ref, u_ref):
        a = _rms(h_ref[...], g_ref[...]).astype(BF16)
        q_ref[...] = _dot(a, w_ref[0]).astype(BF16)
        k_ref[...] = _dot(a, w_ref[1]).astype(BF16)
        v_ref[...] = _dot(a, w_ref[2]).astype(BF16)
        u_ref[...] = _dot(a, w_ref[3])

    return pl.pallas_call(
        body, name="mix_in", grid=(tp // tm,),
        in_specs=[_rows(tm, D), _full((1, D)), _full((N_CHIP, D, NA_W))],
        out_specs=[_rows(tm, NA_W)] * 4,
        out_shape=[_out((tp, NA_W), BF16)] * 3 + [_out((tp, S5_W), F32)],
        compiler_params=_cp(("arbitrary",), 40),
    )(*_in_hbm(h, g, w_in))


def _gelu(x):
    return jax.nn.gelu(x, approximate=True)


def _gelu_grad(x):
    k = math.sqrt(2.0 / math.pi)
    t = jnp.tanh(k * (x + 0.044715 * x * x * x))
    return 0.5 * (1.0 + t) + 0.5 * x * (1.0 - t * t) * k * (1.0 + 3.0 * 0.044715 * x * x)


def _mix_out(o_na, y_pre, h, w_glu, b_glu, g_na, g_s5, w_out, g_post, tm, comm=None, bounds=()):
    tp = h.shape[0]

    def body(ona_ref, yp_ref, h_ref, wglu_ref, bglu_ref, gna_ref, gs5_ref, wout_ref, gpost_ref, hn_ref, mix_ref):
        y = _gelu(yp_ref[...])
        z = _dot(y.astype(BF16), wglu_ref[...]) + bglu_ref[...]
        o_s5 = y * jax.nn.sigmoid(z)
        n1 = _rms(ona_ref[...], gna_ref[...]).astype(BF16)
        n2 = _rms(o_s5, gs5_ref[...]).astype(BF16)
        mix = _dot(n1, wout_ref[0:NA_W, :]) + _dot(n2, wout_ref[NA_W:, :])
        mix_ref[...] = mix
        hn_ref[...] = h_ref[...] + _rms(mix, gpost_ref[...])

    return _call(
        body, comm, bounds, (o_na, y_pre, h, w_glu, b_glu, g_na, g_s5, w_out, g_post), name="mix_out",
        grid=(tp // tm,),
        in_specs=[_rows(tm, NA_W), _rows(tm, S5_W), _rows(tm, D), _full((S5_W, S5_W)), _full((1, S5_W)),
                  _full((1, NA_W)), _full((1, S5_W)), _full((D, D)), _full((1, D))],
        out_specs=[_rows(tm, D), _rows(tm, D)],
        out_shape=[_out((tp, D), F32)] * 2,
        compiler_params=_cp(("arbitrary",), 40))


def _mix_out_bwd(dh, mix, o_na, y_pre, w_glu, b_glu, g_na, g_s5, w_out, g_post, tm):
    tp = dh.shape[0]
    nt = tp // tm

    def body(dh_ref, mix_ref, ona_ref, yp_ref, wglu_ref, bglu_ref, gna_ref, gs5_ref, wout_ref, gpost_ref,
             dona_ref, dyp_ref, dwout_ref, dwglu_ref, dgpost_ref, dgna_ref, dgs5_ref, dbglu_ref, a_out, a_glu):
        i = pl.program_id(0)

        @pl.when(i == 0)
        def _():
            a_out[...] = jnp.zeros_like(a_out)
            a_glu[...] = jnp.zeros_like(a_glu)
            dgpost_ref[...] = jnp.zeros_like(dgpost_ref)
            dgna_ref[...] = jnp.zeros_like(dgna_ref)
            dgs5_ref[...] = jnp.zeros_like(dgs5_ref)
            dbglu_ref[...] = jnp.zeros_like(dbglu_ref)

        dmix, dgpost = _rms_bwd(mix_ref[...], gpost_ref[...], dh_ref[...])
        dgpost_ref[...] += dgpost
        yp = yp_ref[...]
        y = _gelu(yp)
        yb = y.astype(BF16)
        z = _dot(yb, wglu_ref[...]) + bglu_ref[...]
        sg = jax.nn.sigmoid(z)
        o_s5 = y * sg
        o_na = ona_ref[...]
        n1 = _rms(o_na, gna_ref[...]).astype(BF16)
        n2 = _rms(o_s5, gs5_ref[...]).astype(BF16)
        dmb = dmix.astype(BF16)
        a_out[0:NA_W, :] += _dg(n1, dmb, TN)
        a_out[NA_W:, :] += _dg(n2, dmb, TN)
        dn1 = _dg(dmb, wout_ref[0:NA_W, :], NT)
        dn2 = _dg(dmb, wout_ref[NA_W:, :], NT)
        dona, dgna = _rms_bwd(o_na, gna_ref[...], dn1)
        dona_ref[...] = dona
        dgna_ref[...] += dgna
        dos5, dgs5 = _rms_bwd(o_s5, gs5_ref[...], dn2)
        dgs5_ref[...] += dgs5
        dz = dos5 * y * (sg * (1.0 - sg))
        dbglu_ref[...] += jnp.sum(dz, axis=0, keepdims=True)
        dzb = dz.astype(BF16)
        a_glu[...] += _dg(yb, dzb, TN)
        dy = dos5 * sg + _dg(dzb, wglu_ref[...], NT)
        dyp_ref[...] = dy * _gelu_grad(yp)

        @pl.when(i == nt - 1)
        def _():
            pltpu.sync_copy(a_out, dwout_ref)
            pltpu.sync_copy(a_glu, dwglu_ref)

    return pl.pallas_call(
        body, name="mix_out_bwd", grid=(nt,),
        in_specs=[_rows(tm, D), _rows(tm, D), _rows(tm, NA_W), _rows(tm, S5_W), _full((S5_W, S5_W)),
                  _full((1, S5_W)), _full((1, NA_W)), _full((1, S5_W)), _full((D, D)), _full((1, D))],
        out_specs=[_rows(tm, NA_W), _rows(tm, S5_W), ANY, ANY, _full((1, D)), _full((1, NA_W)),
                   _full((1, S5_W)), _full((1, S5_W))],
        out_shape=[_out((tp, NA_W), F32), _out((tp, S5_W), F32),
                   _out((D, D), F32), _out((S5_W, S5_W), F32),
                   _out((1, D), F32), _out((1, NA_W), F32),
                   _out((1, S5_W), F32), _out((1, S5_W), F32)],
        scratch_shapes=[pltpu.VMEM((D, D), F32), pltpu.VMEM((S5_W, S5_W), F32)],
        compiler_params=_cp(("arbitrary",), 48),
    )(*_in_hbm(dh, mix, o_na, y_pre, w_glu, b_glu, g_na, g_s5, w_out, g_post))


def _mix_in_bwd(dq, dk, dv, du, h, g, w_in, dh, f1, g_post1, tm, comm=None, bounds=()):
    tp = h.shape[0]
    nt = tp // tm

    def body(dq_ref, dk_ref, dv_ref, du_ref, h_ref, g_ref, w_ref, dh_ref, f_ref, gq_ref,
             dh1_ref, df_ref, dw_ref, dg_ref, dgq_ref, acc):
        i = pl.program_id(0)

        @pl.when(i == 0)
        def _():
            acc[...] = jnp.zeros_like(acc)
            dg_ref[...] = jnp.zeros_like(dg_ref)
            dgq_ref[...] = jnp.zeros_like(dgq_ref)

        x = h_ref[...]
        a = _rms(x, g_ref[...]).astype(BF16)
        da = jnp.zeros((tm, D), F32)
        for j, r in enumerate((dq_ref, dk_ref, dv_ref, du_ref)):
            dp = r[...].astype(BF16)
            da = da + _dg(dp, w_ref[j], NT)
            acc[j] += _dg(a, dp, TN)
        dx, dg = _rms_bwd(x, g_ref[...], da)
        dh1 = dh_ref[...] + dx
        dh1_ref[...] = dh1
        dg_ref[...] += dg
        df, dgq = _rms_bwd(f_ref[...], gq_ref[...], 0.5 * dh1)
        df_ref[...] = df
        dgq_ref[...] += dgq

        @pl.when(i == nt - 1)
        def _():
            pltpu.sync_copy(acc, dw_ref)

    return _call(
        body, comm, bounds, (dq, dk, dv, du, h, g, w_in, dh, f1, g_post1), name="mix_in_bwd", grid=(nt,),
        in_specs=[_rows(tm, NA_W)] * 4 + [_rows(tm, D), _full((1, D)), _full((N_CHIP, D, NA_W)), _rows(tm, D),
                                         _rows(tm, D), _full((1, D))],
        out_specs=[_rows(tm, D), _rows(tm, D), ANY, _full((1, D)), _full((1, D))],
        out_shape=[_out((tp, D), F32), _out((tp, D), F32),
                   _out((N_CHIP, D, NA_W), F32), _out((1, D), F32),
                   _out((1, D), F32)],
        scratch_shapes=[pltpu.VMEM((N_CHIP, D, NA_W), F32)],
        compiler_params=_cp(("arbitrary",), 48))


def _final_loss(h, g_final, target, f2, g_post2, n_tok, tm):
    tp = h.shape[0]

    def body(h_ref, g_ref, t_ref, f_ref, gq_ref, dh_ref, df_ref, loss_ref, dg_ref, dgq_ref):
        i = pl.program_id(0)

        @pl.when(i == 0)
        def _():
            loss_ref[...] = jnp.zeros_like(loss_ref)
            dg_ref[...] = jnp.zeros_like(dg_ref)
            dgq_ref[...] = jnp.zeros_like(dgq_ref)

        x = h_ref[...]
        y = _rms(x, g_ref[...])
        row = i * tm + lax.broadcasted_iota(jnp.int32, (tm, 1), 0)
        valid = (row >= N_META) & (row < N_META + n_tok)
        e = jnp.where(valid, y - t_ref[...], 0.0)
        loss_ref[...] += 0.5 * jnp.sum(jnp.mean(e * e, axis=-1, keepdims=True), axis=0, keepdims=True)
        dx, dg = _rms_bwd(x, g_ref[...], e * (1.0 / D))
        dh_ref[...] = dx
        dg_ref[...] += dg
        df, dgq = _rms_bwd(f_ref[...], gq_ref[...], 0.5 * dx)
        df_ref[...] = df
        dgq_ref[...] += dgq

    return pl.pallas_call(
        body, name="final_loss", grid=(tp // tm,),
        in_specs=[_rows(tm, D), _full((1, D)), _rows(tm, D), _rows(tm, D), _full((1, D))],
        out_specs=[_rows(tm, D), _rows(tm, D), _full((1, 1)), _full((1, D)), _full((1, D))],
        out_shape=[_out((tp, D), F32), _out((tp, D), F32),
                   _out((1, 1), F32), _out((1, D), F32),
                   _out((1, D), F32)],
        compiler_params=_cp(("arbitrary",), 40),
    )(*_in_hbm(h, g_final, target, f2, g_post2))


def _na_patterns(n_rows):
    pats = []
    for kind in range(3):
        pat = [[-1] * K_ROWS for _ in range(Q_ROWS)]
        for i in range(Q_ROWS):
            for jj in range(K_ROWS):
                if kind == 0 and jj < KH:
                    pat[i][jj] = jj - i + KH - 1
                elif kind == 1 and i <= jj < i + KH:
                    pat[i][jj] = jj - i + 3
                elif kind == 2 and K_ROWS - KH <= jj:
                    pat[i][jj] = jj - i - 1
        pats.append(pat)
    return pats


def _diag_onehot():
    q = np.arange(GRID_W)[:, None]
    kc = np.arange(GRID_W)[None, :]
    start = np.clip(q - KW // 2, 0, GRID_W - KW)
    col_in = (kc >= start) & (kc < start + KW)
    e = np.zeros((32, GRID_W, GRID_W), np.float32)
    for d in range(2 * KW - 1):
        e[d] = ((kc - q + KW - 1) == d) & col_in
    return e.reshape(32, GRID_W * GRID_W), col_in


def _rpb_collapse(dtb2, et):
    def body(d_ref, e_ref, o_ref):
        o_ref[...] = jnp.dot(d_ref[...], e_ref[...], preferred_element_type=F32, precision=lax.Precision.HIGHEST)

    out = (dtb2.shape[0], et.shape[1])
    return pl.pallas_call(
        body, name="rpb_collapse", grid=(1,), out_shape=_out(out, F32),
        in_specs=[_full(dtb2.shape), _full(et.shape)], out_specs=_full(out),
    )(*_in_hbm(dtb2, et))


def _bias_tables(rpb, n_rows, comm=None, bounds=()):
    n_dr, n_dc = 2 * KH - 1, 2 * KW - 1
    pats = _na_patterns(n_rows)

    def body(rpb_ref, o_ref):
        h = pl.program_id(0)
        q = lax.broadcasted_iota(jnp.int32, (GRID_W, GRID_W), 0)
        kc = lax.broadcasted_iota(jnp.int32, (GRID_W, GRID_W), 1)
        start = jnp.clip(q - KW // 2, 0, GRID_W - KW)
        col_in = (kc >= start) & (kc < start + KW)
        diff = kc - q + (KW - 1)
        neg = jnp.full((GRID_W, GRID_W), NEG_INF, F32)
        band = []
        for dr in range(n_dr):
            acc = neg
            for d in range(n_dc):
                acc = jnp.where((diff == d) & col_in, rpb_ref[(h * n_dr + dr) * n_dc + d], acc)
            band.append(acc)
        for kind, pat in enumerate(pats):
            for i in range(Q_ROWS):
                for jj in range(K_ROWS):
                    o_ref[kind, 0, i * GRID_W:(i + 1) * GRID_W, jj * GRID_W:(jj + 1) * GRID_W] = (
                        band[pat[i][jj]] if pat[i][jj] >= 0 else neg)

    (bias,), got = _call(
        body, comm, bounds, (rpb.reshape(-1),), name="bias_tables", grid=(N_HEADS,),
        in_specs=[pl.BlockSpec(memory_space=pltpu.SMEM)],
        out_specs=[pl.BlockSpec((3, 1, QB, KB), lambda h: (0, h, 0, 0))],
        out_shape=[_out((3, N_HEADS, QB, KB), F32)],
        compiler_params=_cp(("arbitrary",), 32))
    return bias, got


def _attn_geometry(n_tok):
    n_rows = n_tok // GRID_W
    assert n_rows % Q_ROWS == 0 and n_rows >= K_ROWS
    return n_rows, n_rows // Q_ROWS


def _attn_probs(qh, kh, kmh, bias, scale):
    s = _dg(qh, kh, NT) * scale + bias
    sm = _dg(qh, kmh, NT) * scale
    m = jnp.maximum(jnp.max(s, axis=-1, keepdims=True), jnp.max(sm, axis=-1, keepdims=True))
    p = jnp.exp(s - m)
    pm = jnp.exp(sm - m)
    inv = 1.0 / (jnp.sum(p, axis=-1, keepdims=True) + jnp.sum(pm, axis=-1, keepdims=True))
    return p * inv, pm * inv


def _meta_probs(qmh, kmh, scale):
    s = _dg(qmh, kmh, NT) * scale
    p = jnp.exp(s - jnp.max(s, axis=-1, keepdims=True))
    return p / jnp.sum(p, axis=-1, keepdims=True)


def _step_rows(r, n_rows):
    q0 = pl.multiple_of(N_META + r * QB, 16)
    k0 = pl.multiple_of(N_META + jnp.clip(Q_ROWS * r - (K_ROWS - KH), 0, n_rows - K_ROWS) * GRID_W, 16)
    return q0, k0


def _attn_fwd(q, k, v, bias, n_tok, comm=None, bounds=()):
    tp = q.shape[0]
    n_rows, n_steps = _attn_geometry(n_tok)
    scale = HEAD_DIM ** -0.5

    def body(q_ref, k_ref, v_ref, b_ref, o_ref):
        r = pl.program_id(1)
        km = k_ref[0:N_META, :]
        vm = v_ref[0:N_META, :]

        @pl.when(r == 0)
        def _():
            qm = q_ref[0:N_META, :]
            outs = []
            for hh in range(2):
                sl = slice(hh * HEAD_DIM, (hh + 1) * HEAD_DIM)
                p = _meta_probs(qm[:, sl], km[:, sl], scale)
                outs.append(_dot(p.astype(BF16), vm[:, sl]))
            o_ref[0:N_META, :] = jnp.concatenate(outs, axis=1)
            o_ref[N_META + n_tok:, :] = jnp.zeros((tp - N_META - n_tok, 2 * HEAD_DIM), F32)

        q0, k0 = _step_rows(r, n_rows)
        qb = q_ref[pl.ds(q0, QB), :]
        kb = k_ref[pl.ds(k0, KB), :]
        vb = v_ref[pl.ds(k0, KB), :]
        outs = []
        for hh in range(2):
            sl = slice(hh * HEAD_DIM, (hh + 1) * HEAD_DIM)
            p, pm = _attn_probs(qb[:, sl], kb[:, sl], km[:, sl], b_ref[0, hh], scale)
            outs.append(_dot(p.astype(BF16), vb[:, sl]) + _dot(pm.astype(BF16), vm[:, sl]))
        o_ref[pl.ds(q0, QB), :] = jnp.concatenate(outs, axis=1)

    def bias_map(hp, r):
        return (jnp.where(r == 0, 0, jnp.where(r == n_steps - 1, 2, 1)), hp, 0, 0)

    col = pl.BlockSpec((tp, 2 * HEAD_DIM), lambda hp, r: (0, hp))
    return _call(
        body, comm, bounds, (q, k, v, bias), name="attn_fwd", grid=(N_HEADS // 2, n_steps),
        in_specs=[col, col, col, pl.BlockSpec((1, 2, QB, KB), bias_map)],
        out_specs=[col], out_shape=[_out((tp, NA_W), F32)],
        compiler_params=_cp(("arbitrary", "arbitrary"), 40))


def _attn_bwd(q, k, v, bias, do, n_tok, comm=None, bounds=()):
    tp = q.shape[0]
    n_rows, n_steps = _attn_geometry(n_tok)
    scale = HEAD_DIM ** -0.5
    pats = _na_patterns(n_rows)

    def body(q_ref, k_ref, v_ref, b_ref, do_ref, dq_ref, dk_ref, dv_ref, dtb_ref):
        r = pl.program_id(1)
        km = k_ref[0:N_META, :]
        vm = v_ref[0:N_META, :]

        @pl.when(r == 0)
        def _():
            dk_ref[...] = jnp.zeros_like(dk_ref)
            dv_ref[...] = jnp.zeros_like(dv_ref)
            dtb_ref[...] = jnp.zeros_like(dtb_ref)
            dq_ref[N_META + n_tok:, :] = jnp.zeros((tp - N_META - n_tok, 2 * HEAD_DIM), F32)
            qm = q_ref[0:N_META, :]
            dom = do_ref[0:N_META, :].astype(BF16)
            dqs, dks, dvs = [], [], []
            for hh in range(2):
                sl = slice(hh * HEAD_DIM, (hh + 1) * HEAD_DIM)
                p = _meta_probs(qm[:, sl], km[:, sl], scale)
                dp = _dg(dom[:, sl], vm[:, sl], NT)
                ds = (p * (dp - jnp.sum(dp * p, axis=-1, keepdims=True))).astype(BF16)
                dvs.append(_dg(p.astype(BF16), dom[:, sl], TN))
                dqs.append(_dot(ds, km[:, sl]) * scale)
                dks.append(_dg(ds, qm[:, sl], TN) * scale)
            dq_ref[0:N_META, :] = jnp.concatenate(dqs, axis=1)
            dk_ref[0:N_META, :] += jnp.concatenate(dks, axis=1)
            dv_ref[0:N_META, :] += jnp.concatenate(dvs, axis=1)

        q0, k0 = _step_rows(r, n_rows)
        qb = q_ref[pl.ds(q0, QB), :]
        kb = k_ref[pl.ds(k0, KB), :]
        vb = v_ref[pl.ds(k0, KB), :]
        dob = do_ref[pl.ds(q0, QB), :].astype(BF16)
        dqs, dks, dvs, dkms, dvms, dss = [], [], [], [], [], []
        for hh in range(2):
            sl = slice(hh * HEAD_DIM, (hh + 1) * HEAD_DIM)
            qh, kh, vh, kmh, vmh, doh = qb[:, sl], kb[:, sl], vb[:, sl], km[:, sl], vm[:, sl], dob[:, sl]
            p, pm = _attn_probs(qh, kh, kmh, b_ref[0, hh], scale)
            dp = _dg(doh, vh, NT)
            dpm = _dg(doh, vmh, NT)
            delta = jnp.sum(dp * p, axis=-1, keepdims=True) + jnp.sum(dpm * pm, axis=-1, keepdims=True)
            ds = p * (dp - delta)
            dsb = ds.astype(BF16)
            dsmb = (pm * (dpm - delta)).astype(BF16)
            dss.append(ds)
            dvs.append(_dg(p.astype(BF16), doh, TN))
            dvms.append(_dg(pm.astype(BF16), doh, TN))
            dqs.append((_dot(dsb, kh) + _dot(dsmb, kmh)) * scale)
            dks.append(_dg(dsb, qh, TN) * scale)
            dkms.append(_dg(dsmb, qh, TN) * scale)
        dq_ref[pl.ds(q0, QB), :] = jnp.concatenate(dqs, axis=1)
        dk_ref[pl.ds(k0, KB), :] += jnp.concatenate(dks, axis=1)
        dv_ref[pl.ds(k0, KB), :] += jnp.concatenate(dvs, axis=1)
        dk_ref[0:N_META, :] += jnp.concatenate(dkms, axis=1)
        dv_ref[0:N_META, :] += jnp.concatenate(dvms, axis=1)

        def add_bias_grad(pat):
            for hh in range(2):
                for i in range(Q_ROWS):
                    for jj in range(K_ROWS):
                        if pat[i][jj] >= 0:
                            dtb_ref[hh, pat[i][jj]] += dss[hh][i * GRID_W:(i + 1) * GRID_W,
                                                               jj * GRID_W:(jj + 1) * GRID_W]

        @pl.when(r == 0)
        def _():
            add_bias_grad(pats[0])

        @pl.when((r > 0) & (r < n_steps - 1))
        def _():
            add_bias_grad(pats[1])

        @pl.when(r == n_steps - 1)
        def _():
            add_bias_grad(pats[2])

    def bias_map(hp, r):
        return (jnp.where(r == 0, 0, jnp.where(r == n_steps - 1, 2, 1)), hp, 0, 0)

    col = pl.BlockSpec((tp, 2 * HEAD_DIM), lambda hp, r: (0, hp))
    n_dr = 2 * KH - 1
    return _call(
        body, comm, bounds, (q, k, v, bias, do), name="attn_bwd", grid=(N_HEADS // 2, n_steps),
        in_specs=[col, col, col, pl.BlockSpec((1, 2, QB, KB), bias_map), col],
        out_specs=[col, col, col, pl.BlockSpec((2, n_dr, GRID_W, GRID_W), lambda hp, r: (hp, 0, 0, 0))],
        out_shape=[_out((tp, NA_W), F32)] * 3 +
                  [_out((N_HEADS, n_dr, GRID_W, GRID_W), F32)],
        compiler_params=_cp(("arbitrary", "arbitrary"), 48))


def _repeat_onehot():
    return np.repeat(np.eye(2 * S5_G, dtype=np.float32), S5_H, axis=0)


def _s5_disc_math(lam_re, lam_im, log_dt, b_re, b_im, rep):
    dt = jnp.exp(log_dt)
    ea = jnp.exp(lam_re * dt)
    a_re = ea * jnp.cos(lam_im * dt)
    a_im = ea * jnp.sin(lam_im * dt)
    den = lam_re * lam_re + lam_im * lam_im
    c_re = ((a_re - 1.0) * lam_re + a_im * lam_im) / den
    c_im = (a_im * lam_re - (a_re - 1.0) * lam_im) / den
    ce_re = jnp.dot(rep, c_re, preferred_element_type=F32, precision=lax.Precision.HIGHEST)
    ce_im = jnp.dot(rep, c_im, preferred_element_type=F32, precision=lax.Precision.HIGHEST)
    return a_re, a_im, ce_re * b_re - ce_im * b_im, ce_re * b_im + ce_im * b_re


def _s5_blocks():
    gl = S5_G // N_BUNDLE
    half = gl * S5_P
    out = []
    for d in range(2):
        for g in range(S5_G):
            b, k = divmod(g, gl)
            dg = d * S5_G + g
            out.append((d, b, slice(k * S5_H, (k + 1) * S5_H), slice(k * S5_P, (k + 1) * S5_P),
                        slice(half + k * S5_P, half + (k + 1) * S5_P), slice(dg * S5_H, (dg + 1) * S5_H),
                        slice(dg, dg + 1)))
    return out


def _s5_params(lam_re, lam_im, log_dt, b_re, b_im, c_re, c_im):
    cw, sw = S5_W // N_BUNDLE, 2 * (S5_G // N_BUNDLE) * S5_P

    def body(lr, li, ld, br, bi, cr, ci, rep_ref, a1_ref, a2_ref, bm_ref, cm_ref):
        a_re, a_im, bb_re, bb_im = _s5_disc_math(lr[...], li[...], ld[...], br[...], bi[...], rep_ref[...])
        cc_re = cr[...]
        cc_im = ci[...]
        bm_ref[...] = jnp.zeros_like(bm_ref)
        cm_ref[...] = jnp.zeros_like(cm_ref)
        for d, b, rows, re, im, nat, one in _s5_blocks():
            bm_ref[d, b, rows, re] = bb_re[nat, :].astype(BF16)
            bm_ref[d, b, rows, im] = bb_im[nat, :].astype(BF16)
            cm_ref[d, b, rows, re] = cc_re[nat, :].astype(BF16)
            cm_ref[d, b, rows, im] = (-cc_im[nat, :]).astype(BF16)
            k = rows.start // S5_H
            lanes = slice((k % 2) * S5_P, (k % 2 + 1) * S5_P)
            for part, (v1, v2) in enumerate(((a_re[one, :], a_im[one, :]), (a_re[one, :], -a_im[one, :]))):
                sub = slice(4 * part + k // 2, 4 * part + k // 2 + 1)
                a1_ref[d, b, sub, lanes] = v1
                a2_ref[d, b, sub, lanes] = v2

    args = (lam_re, lam_im, log_dt, b_re, b_im, c_re, c_im, jnp.asarray(_repeat_onehot()))
    outs = [((2, N_BUNDLE, 8, 128), F32)] * 2 + [((2, N_BUNDLE, cw, sw), BF16)] * 2
    return pl.pallas_call(
        body, name="s5_params", grid=(1,), in_specs=[_full(a.shape) for a in args],
        out_specs=[_full(s) for s, _ in outs], out_shape=[_out(s, dt) for s, dt in outs],
    )(*_in_hbm(*args))


def _s5_params_bwd(lam_re, lam_im, log_dt, b_re, b_im, da, dbm, dcm):
    n, nb = 2 * S5_G, 2 * S5_G * S5_H

    def body(lr, li, ld, br, bi, rep_ref, da_ref, dbm_ref, dcm_ref, o_lr, o_li, o_ld, o_br, o_bi, o_cr, o_ci,
             dar_s, dai_s, dbr_s, dbi_s):
        for d, b, rows, re, im, nat, one in _s5_blocks():
            dbr_s[nat, :] = dbm_ref[d, b, rows, re]
            dbi_s[nat, :] = dbm_ref[d, b, rows, im]
            o_cr[nat, :] = dcm_ref[d, b, rows, re]
            o_ci[nat, :] = -dcm_ref[d, b, rows, im]
            dar_s[one, :] = da_ref[d, b, :, re]
            dai_s[one, :] = da_ref[d, b, :, im]
        rep = rep_ref[...]
        _, vjp = jax.vjp(lambda p, q, r, s, t: _s5_disc_math(p, q, r, s, t, rep),
                         lr[...], li[...], ld[...], br[...], bi[...])
        o_lr[...], o_li[...], o_ld[...], o_br[...], o_bi[...] = vjp((dar_s[...], dai_s[...], dbr_s[...], dbi_s[...]))

    args = (lam_re, lam_im, log_dt, b_re, b_im, jnp.asarray(_repeat_onehot()), da, dbm, dcm)
    outs = [(n, S5_P)] * 2 + [(n, 1)] + [(nb, S5_P)] * 4
    return pl.pallas_call(
        body, name="s5_params_bwd", grid=(1,), in_specs=[_full(a.shape) for a in args],
        out_specs=[_full(s) for s in outs], out_shape=[_out(s, F32) for s in outs],
        scratch_shapes=[pltpu.VMEM((n, S5_P), F32)] * 2 + [pltpu.VMEM((nb, S5_P), F32)] * 2,
    )(*_in_hbm(*args))


def _tiles_store(ref, base, val):
    for i in range(val.shape[0] // 8):
        for c in range(8):
            ref[pl.ds(base + (8 * i + c) * 8, 8), :] = val[8 * i:8 * i + 8, 128 * c:128 * (c + 1)]


def _tiles_load(ref, base, n):
    return jnp.concatenate(
        [jnp.concatenate([ref[pl.ds(base + (8 * i + c) * 8, 8), :] for c in range(8)], axis=1) for i in range(n // 8)],
        axis=0)


def _time_rows(base, t):
    return pl.ds(base + (t // 8) * 64 + t % 8, 8, stride=8)


def _scan(chains, n):
    xs = [c["x"] for c in chains]
    for k in range(n):
        for ci, c in enumerate(chains):
            t = n - 1 - k if c["reverse"] else k
            if c["prev"] is not None:
                c["prev"][_time_rows(c["prev_base"], t), :] = xs[ci]
            xs[ci] = c["a1"] * xs[ci] + pltpu.roll(c["a2"] * xs[ci], 4, axis=0) + c["src"][_time_rows(0, t), :]
            if c["dst"] is not None:
                c["dst"][_time_rows(0, t), :] = xs[ci]
    return xs


def _chain(x, a1, a2, src, dst=None, prev=None, prev_base=0, reverse=False):
    return dict(x=x, a1=a1, a2=a2, src=src, dst=dst, prev=prev, prev_base=prev_base, reverse=reverse)


def _s5_fwd(u, d_skip, a1, a2, bm, cm, length, comm=None, bounds=()):
    tp = u.shape[0]
    cw = S5_W // N_BUNDLE
    sw = bm.shape[-1]
    n_full, n_tail = divmod(length, SCAN_CHUNK)
    t_tail = n_full * SCAN_CHUNK

    nbs = N_BUNDLE

    def body(u_ref, d_ref, a1_ref, a2_ref, bm_ref, cm_ref, y_ref, bnd_ref, *scratch):
        y_ref[...] = u_ref[...] * d_ref[...]
        ins, xss = (scratch[0:nbs], scratch[nbs:2 * nbs]), (scratch[2 * nbs:3 * nbs], scratch[3 * nbs:])
        cols = [slice(b * cw, (b + 1) * cw) for b in range(nbs)]

        def keep(dr, chunk, xs):
            for b in range(nbs):
                bnd_ref[dr, b, chunk] = xs[b]

        def load(dr, t0, n):
            for b in range(nbs):
                _tiles_store(ins[dr][b], 0, _dot(u_ref[pl.ds(t0, n), cols[b]].astype(BF16), bm_ref[dr, b]))

        def chains(dr, xs):
            return [_chain(xs[b], a1_ref[dr, b], a2_ref[dr, b], ins[dr][b], dst=xss[dr][b], reverse=dr == 1)
                    for b in range(nbs)]

        def emit(dr, t0, n):
            for b in range(nbs):
                y_ref[pl.ds(t0, n), cols[b]] += _dg(_tiles_load(xss[dr][b], 0, n).astype(BF16), cm_ref[dr, b], NT)

        zero = (jnp.zeros((8, 128), F32),) * nbs
        xb = zero
        if n_tail:
            keep(1, n_full, xb)
            load(1, t_tail, n_tail)
            xb = tuple(_scan(chains(1, xb), n_tail))
            emit(1, t_tail, n_tail)

        def pair(i, carry):
            j = n_full - 1 - i
            t0s = (pl.multiple_of(i * SCAN_CHUNK, SCAN_CHUNK), pl.multiple_of(j * SCAN_CHUNK, SCAN_CHUNK))
            keep(0, i, carry[0])
            keep(1, j, carry[1])
            for dr in range(2):
                load(dr, t0s[dr], SCAN_CHUNK)
            out = _scan(chains(0, carry[0]) + chains(1, carry[1]), SCAN_CHUNK)
            for dr in range(2):
                emit(dr, t0s[dr], SCAN_CHUNK)
            return tuple(out[:nbs]), tuple(out[nbs:])

        xf, _ = lax.fori_loop(0, n_full, pair, (zero, xb))
        if n_tail:
            keep(0, n_full, xf)
            load(0, t_tail, n_tail)
            _scan(chains(0, xf), n_tail)
            emit(0, t_tail, n_tail)

    n_chunks = n_full + (1 if n_tail else 0)
    tile = pl.BlockSpec((2, nbs, 8, 128), lambda b: (0, b, 0, 0))
    return _call(
        body, comm, bounds, (u, d_skip, a1, a2, bm, cm), name="s5_fwd", grid=(N_BUNDLE // nbs,),
        in_specs=[pl.BlockSpec((tp, nbs * cw), lambda b: (0, b)), pl.BlockSpec((1, nbs * cw), lambda b: (0, b)),
                  tile, tile, pl.BlockSpec((2, nbs, cw, sw), lambda b: (0, b, 0, 0)),
                  pl.BlockSpec((2, nbs, cw, sw), lambda b: (0, b, 0, 0))],
        out_specs=[pl.BlockSpec((tp, nbs * cw), lambda b: (0, b)),
                   pl.BlockSpec((2, nbs, n_chunks, 8, 128), lambda b: (0, b, 0, 0, 0))],
        out_shape=[_out((tp, S5_W), F32), _out((2, N_BUNDLE, n_chunks, 8, 128), F32)],
        scratch_shapes=[pltpu.VMEM((SCAN_CHUNK * 8, 128), F32)] * (4 * nbs),
        compiler_params=_cp(("arbitrary",), 48))


def _s5_bwd(u, dy, d_skip, a1, a2, bm, cm, bnd, length):
    tp = u.shape[0]
    cw = S5_W // N_BUNDLE
    sw = bm.shape[-1]
    half = sw // 2
    n_full, n_tail = divmod(length, SCAN_CHUNK)
    t_tail = n_full * SCAN_CHUNK
    n_chunks = bnd.shape[2]
    nbs = 2

    def body(u_ref, dy_ref, d_ref, a1_ref, a2_ref, bm_ref, cm_ref, bnd_ref, du_ref, dd_ref, dbm_ref, dcm_ref,
             da_ref, *scratch):
        du_ref[...] = dy_ref[...] * d_ref[...]
        dd_ref[...] = jnp.sum(dy_ref[...] * u_ref[...], axis=0, keepdims=True)
        dbm_ref[...] = jnp.zeros_like(dbm_ref)
        dcm_ref[...] = jnp.zeros_like(dcm_ref)
        da_ref[...] = jnp.zeros_like(da_ref)
        bu_s, dx_s, g_s, xp_s, x_s = ([scratch[(k * 2 + dr) * nbs:(k * 2 + dr + 1) * nbs] for dr in range(2)]
                                      for k in range(5))
        cols = [slice(b * cw, (b + 1) * cw) for b in range(nbs)]

        def chains(dr, chunk, t0, n, gs):
            out = []
            for b in range(nbs):
                _tiles_store(bu_s[dr][b], 0, _dot(u_ref[pl.ds(t0, n), cols[b]].astype(BF16), bm_ref[dr, b]))
                _tiles_store(dx_s[dr][b], 0, _dot(dy_ref[pl.ds(t0, n), cols[b]].astype(BF16), cm_ref[dr, b]))
                out.append(_chain(bnd_ref[dr, b, chunk], a1_ref[dr, b], a2_ref[dr, b], bu_s[dr][b],
                                  dst=x_s[dr][b], prev=xp_s[dr][b], reverse=dr == 1))
                out.append(_chain(gs[b], a1_ref[dr, b], -a2_ref[dr, b], dx_s[dr][b], dst=g_s[dr][b], reverse=dr == 0))
            return out

        def emit(dr, t0, n):
            rows = pl.ds(t0, n)
            for b in range(nbs):
                ub = u_ref[rows, cols[b]].astype(BF16)
                dyb = dy_ref[rows, cols[b]].astype(BF16)
                g = _tiles_load(g_s[dr][b], 0, n)
                gb = g.astype(BF16)
                du_ref[rows, cols[b]] += _dg(gb, bm_ref[dr, b], NT)
                dbm_ref[dr, b] += _dg(ub, gb, TN)
                xp = _tiles_load(xp_s[dr][b], 0, n)
                xp_r, xp_i = xp[:, 0:half], xp[:, half:]
                g_r, g_i = g[:, 0:half], g[:, half:]
                dcm_ref[dr, b] += _dg(dyb, _tiles_load(x_s[dr][b], 0, n).astype(BF16), TN)
                da_ref[dr, b] += jnp.concatenate([jnp.sum(g_r * xp_r + g_i * xp_i, axis=0, keepdims=True),
                                                  jnp.sum(g_i * xp_r - g_r * xp_i, axis=0, keepdims=True)], axis=1)

        def adjoints(out):
            return tuple(out[1::2])

        zero = (jnp.zeros((8, 128), F32),) * nbs
        g0 = zero
        if n_tail:
            g0 = adjoints(_scan(chains(0, n_full, t_tail, n_tail, g0), n_tail))
            emit(0, t_tail, n_tail)

        def pair(i, carry):
            j = n_full - 1 - i
            t0 = (pl.multiple_of(j * SCAN_CHUNK, SCAN_CHUNK), pl.multiple_of(i * SCAN_CHUNK, SCAN_CHUNK))
            both = chains(0, j, t0[0], SCAN_CHUNK, carry[0]) + chains(1, i, t0[1], SCAN_CHUNK, carry[1])
            out = _scan(both, SCAN_CHUNK)
            emit(0, t0[0], SCAN_CHUNK)
            emit(1, t0[1], SCAN_CHUNK)
            return adjoints(out[:2 * nbs]), adjoints(out[2 * nbs:])

        _, g1 = lax.fori_loop(0, n_full, pair, (g0, zero))
        if n_tail:
            _scan(chains(1, n_full, t_tail, n_tail, g1), n_tail)
            emit(1, t_tail, n_tail)

    tile = pl.BlockSpec((2, nbs, 8, 128), lambda b: (0, b, 0, 0))
    wide = pl.BlockSpec((2, nbs, cw, sw), lambda b: (0, b, 0, 0))
    col = pl.BlockSpec((tp, nbs * cw), lambda b: (0, b))
    row = pl.BlockSpec((1, nbs * cw), lambda b: (0, b))
    arow = pl.BlockSpec((2, nbs, 1, sw), lambda b: (0, b, 0, 0))
    return pl.pallas_call(
        body, name="s5_bwd", grid=(N_BUNDLE // nbs,),
        in_specs=[col, col, row, tile, tile, wide, wide,
                  pl.BlockSpec((2, nbs, n_chunks, 8, 128), lambda b: (0, b, 0, 0, 0))],
        out_specs=[col, row, wide, wide, arow],
        out_shape=[_out((tp, S5_W), F32), _out((1, S5_W), F32),
                   _out((2, N_BUNDLE, cw, sw), F32), _out((2, N_BUNDLE, cw, sw), F32),
                   _out((2, N_BUNDLE, 1, sw), F32)],
        scratch_shapes=[pltpu.VMEM((SCAN_CHUNK * 8, 128), F32)] * (10 * nbs),
        compiler_params=_cp(("arbitrary",), 56),
    )(*_in_hbm(u, dy, d_skip, a1, a2, bm, cm, bnd))


def _row_tile(tp):
    return max(tm for tm in range(16, 449, 16) if tp % tm == 0)


def _step(x, target, bufs, gains, s5, rpb, c_arr, kc_arr, me_arr):
    n_tok = x.shape[0]
    first = ["ffn1_w_gate", "ffn1_w_up", "ffn1_w_down", "meta_tokens"]
    bias, got = _bias_tables(rpb, n_tok // GRID_W, _gather_comm([bufs[n] for n in first]), (0, N_HEADS - 1))
    w = dict(zip(first, got))
    meta = w["meta_tokens"].transpose(1, 0, 2).reshape(N_META, D)
    length = N_META + n_tok
    tp = length + 16
    tm = _row_tile(tp)
    tmb = tm
    n_rows = n_tok // GRID_W
    pad = jnp.zeros((tp - length, D), F32)
    h0 = jnp.concatenate([meta, x, pad], axis=0)
    tgt = jnp.concatenate([jnp.zeros((N_META, D), F32), target, pad], axis=0)

    s5p = (s5["lam_re"], s5["lam_im"], s5["log_dt"].reshape(2 * S5_G, 1), s5["b_re"], s5["b_im"])
    a1_m, a2_m, bm16, cm16 = _s5_params(*s5p, s5["c_re"], s5["c_im"])

    mid = ["w_in", "s5_w_glu", "w_out"]
    (h1, gate1, up1, f1), got = _ffn_fwd(
        "ffn1_fwd", h0, gains["ffn1_pre_g"], gains["ffn1_post_g"], w["ffn1_w_gate"], w["ffn1_w_up"], w["ffn1_w_down"],
        tm, _gather_comm([bufs[n] for n in mid]), (0, (tp // tm) * N_CHIP * 3 // 5))
    w.update(zip(mid, got))
    q, k, v, u = _mix_in(h1, gains["mix_pre_g"], w["w_in"], tm)
    (o_na,), (gate_ici, up_ici) = _attn_fwd(
        q, k, v, bias, n_tok, _gather_comm([bufs["ffn2_w_gate"], bufs["ffn2_w_up"]], pair=False), (0,))
    (y_pre, s5_bnd), (w["ffn2_w_gate"], w["ffn2_w_up"], down_ici) = _s5_fwd(
        u, gains["s5_d"], a1_m, a2_m, bm16, cm16, length,
        _merge_comm(_gather_comm([gate_ici, up_ici], ici=False),
                    _gather_comm([bufs["ffn2_w_down"]], pair=False)), (0,))
    w_glu = w["s5_w_glu"].reshape(S5_W, S5_W)
    w_out = w["w_out"].reshape(D, D)
    (h2, mix), (w["ffn2_w_down"],) = _mix_out(
        o_na, y_pre, h1, w_glu, gains["s5_b_glu"], gains["na_out_g"], gains["s5_out_g"], w_out, gains["mix_post_g"], tm,
        _gather_comm([down_ici], ici=False), (0,))
    (h3, gate2, up2, f2), _ = _ffn_fwd("ffn2_fwd", h2, gains["ffn2_pre_g"], gains["ffn2_post_g"],
                                       w["ffn2_w_gate"], w["ffn2_w_up"], w["ffn2_w_down"], tm)
    dh3, df2, loss, dg_final, dg_post2 = _final_loss(h3, gains["final_g"], tgt, f2, gains["ffn2_post_g"], n_tok, tm)

    ffn2 = ["ffn2_w_gate", "ffn2_w_up", "ffn2_w_down"]
    ffn1 = ["ffn1_w_gate", "ffn1_w_up", "ffn1_w_down"]
    out2, _ = _ffn_bwd("ffn2_bwd", h2, gains["ffn2_pre_g"], df2, gate2, up2,
                       w["ffn2_w_gate"], w["ffn2_w_up"], w["ffn2_w_down"], tmb)
    dxn2 = out2[3]
    sums2 = [_chip_sum("chip_sum_" + n, g, r, c_arr) for n, g, r in zip(ffn2, out2[0:3], out2[4:7])]
    (dh2, dg_pre2), _ = _ffn_pre_bwd("ffn2_pre_bwd", dh3, dxn2, h2, gains["ffn2_pre_g"], tm)
    do_na, dy_pre, dw_out, dw_glu, dg_mpost, dg_na, dg_s5, db_glu = _mix_out_bwd(
        dh2, mix, o_na, y_pre, w_glu, gains["s5_b_glu"], gains["na_out_g"], gains["s5_out_g"], w_out,
        gains["mix_post_g"], tm)
    (dq, dk, dv, dtb), recv3 = _attn_bwd(q, k, v, bias, do_na, n_tok, _scatter_comm(sums2), (0,))
    totals2 = [_total_sum("total_sum_" + n, s, r, kc_arr) for n, s, r in zip(ffn2, sums2, recv3)]
    du, dd, dbm, dcm, da_m = _s5_bwd(u, dy_pre, gains["s5_d"], a1_m, a2_m, bm16, cm16, s5_bnd, length)
    (dh1, df1, dw_in, dg_mpre, dg_post1), done2 = _mix_in_bwd(
        dq, dk, dv, du, h1, gains["mix_pre_g"], w["w_in"], dh2, f1, gains["ffn1_post_g"], tm,
        _assemble_comm(totals2), (0,))
    pieces = dict(zip(ffn2, done2))

    e, _ = _diag_onehot()
    n_dr = 2 * KH - 1
    drpb = _rpb_collapse(dtb.reshape(N_HEADS * n_dr, GRID_W * GRID_W), jnp.asarray(e.T))
    drpb = drpb[:, :2 * KW - 1].reshape(N_HEADS, n_dr, 2 * KW - 1).transpose(1, 0, 2).reshape(N_HEADS * n_dr, 2 * KW - 1)
    dlam_re, dlam_im, dlog_dt, db_re, db_im, dc_re, dc_im = _s5_params_bwd(*s5p, da_m, dbm, dcm)
    early = {"ffn1_post_g": dg_post1, "mix_pre_g": dg_mpre, "na_rpb": drpb,
             "s5_lam_re": dlam_re, "s5_lam_im": dlam_im, "s5_log_dt": dlog_dt.reshape(2, S5_G),
             "s5_b_re": db_re, "s5_b_im": db_im, "s5_c_re": dc_re, "s5_c_im": dc_im,
             "s5_d": dd, "s5_b_glu": db_glu, "na_out_g": dg_na,
             "s5_out_g": dg_s5, "mix_post_g": dg_mpost, "ffn2_pre_g": dg_pre2, "ffn2_post_g": dg_post2,
             "final_g": dg_final}
    names = list(early)
    slots = _small_pack([early[n] for n in names], me_arr)

    out1, slots = _ffn_bwd("ffn1_bwd", h0, gains["ffn1_pre_g"], df1, gate1, up1,
                           w["ffn1_w_gate"], w["ffn1_w_up"], w["ffn1_w_down"], tmb, _spread_comm(slots), (0,))
    small = dict(zip(names, _small_total(slots, [early[n].shape for n in names])))
    sums1 = [_chip_sum("chip_sum_" + n, g, r, c_arr) for n, g, r in zip(ffn1, out1[0:3], out1[4:7])]
    flight1 = _scatter_start("ffn1", sums1)
    token = flight1[4]
    rest = [dw_in, dw_glu.reshape(N_CHIP, S5_W // N_CHIP, S5_W), dw_out.reshape(N_CHIP, D // N_CHIP, D)]
    (dh0, dg_pre1), recv_rest = _ffn_pre_bwd("ffn1_pre_bwd", dh1, out1[3], h0, gains["ffn1_pre_g"] + token[0:1, 0:1],
                                             tm, _exchange_comm(rest), (0,))
    sums = [_chip_sum("chip_sum_" + n, g, r, c_arr) for n, g, r in zip(mid, rest, recv_rest)]
    flight2 = _scatter_start("rest", sums)
    return loss[0, 0], dh0, pieces, small, {"ffn1_pre_g": dg_pre1}, (ffn1, flight1[:4]), (mid, flight2[:4])


def _mesh_pos():
    return lax.axis_index("x"), lax.axis_index("y"), lax.axis_index("c")


def _other_chips(x, y):
    return [(1 - x, y), (x, 1 - y), (1 - x, 1 - y)]


class _Comm:
    def __init__(self, ins, out_shape, aliases, parts):
        self.ins, self.out_shape, self.aliases, self.parts = list(ins), list(out_shape), dict(aliases), list(parts)
        self.n_sems = sum(p[0] for p in parts)

    def bases(self):
        out, base = [], 0
        for n_sems, _, _ in self.parts:
            out.append(base)
            base += n_sems
        return out


def _run_comm(name, comm):
    n_i, n_o = len(comm.ins), len(comm.out_shape)

    def body(*refs):
        ins, outs = refs[:n_i], refs[n_i:n_i + n_o]
        send_sems, recv_sems = refs[n_i + n_o:]
        for base, (_, start, finish) in zip(comm.bases(), comm.parts):
            start(ins, outs, send_sems, recv_sems, base)
            finish(ins, outs, send_sems, recv_sems, base)

    return pl.pallas_call(
        body, name=name, out_shape=comm.out_shape, in_specs=[ANY] * n_i, out_specs=[ANY] * n_o,
        input_output_aliases=comm.aliases,
        scratch_shapes=[pltpu.SemaphoreType.DMA((comm.n_sems,)), pltpu.SemaphoreType.DMA((comm.n_sems,))],
    )(*_in_hbm(*comm.ins))


def _call(body, comm, bounds, args, *, name, grid, in_specs, out_specs, out_shape, scratch_shapes=(),
          compiler_params=None):
    in_specs, out_specs, out_shape, scratch_shapes = list(in_specs), list(out_specs), list(out_shape), list(scratch_shapes)
    if comm is None:
        return pl.pallas_call(body, name=name, grid=grid, in_specs=in_specs, out_specs=out_specs, out_shape=out_shape,
                              scratch_shapes=scratch_shapes, compiler_params=compiler_params)(*_in_hbm(*args)), []
    n_in, n_out, n_scr = len(in_specs), len(out_specs), len(scratch_shapes)
    n_ci, n_co = len(comm.ins), len(comm.out_shape)
    n_steps = int(np.prod(grid))
    assert len(bounds) == len(comm.parts) and all(0 <= b < n_steps for b in bounds) and list(bounds) == sorted(bounds)

    def fused(*refs):
        a = n_in
        b = a + n_ci
        c = b + n_out
        d = c + n_co
        e = d + n_scr
        cargs = (refs[a:b], refs[c:d], refs[e], refs[e + 1])
        step = pl.program_id(0)
        for ax in range(1, len(grid)):
            step = step * grid[ax] + pl.program_id(ax)
        bases = comm.bases()
        for p, (_, start, finish) in enumerate(comm.parts):
            @pl.when(step == bounds[p])
            def _(p=p, start=start):
                if p > 0:
                    comm.parts[p - 1][2](*cargs, bases[p - 1])
                start(*cargs, bases[p])
        body(*(refs[:a] + refs[b:c] + refs[d:e]))

        @pl.when(step == n_steps - 1)
        def _():
            comm.parts[-1][2](*cargs, bases[-1])

    res = pl.pallas_call(
        fused, name=name, grid=grid, in_specs=in_specs + [ANY] * n_ci, out_specs=out_specs + [ANY] * n_co,
        out_shape=out_shape + comm.out_shape,
        scratch_shapes=scratch_shapes + [pltpu.SemaphoreType.DMA((comm.n_sems,)), pltpu.SemaphoreType.DMA((comm.n_sems,))],
        input_output_aliases={n_in + i: n_out + j for i, j in comm.aliases.items()},
        compiler_params=compiler_params)(*_in_hbm(*args, *comm.ins))
    return res[:n_out], res[n_out:]


def _remote(src, dst, send_sems, recv_sems, idx, to):
    return pltpu.make_async_remote_copy(src_ref=src, dst_ref=dst, send_sem=send_sems.at[idx],
                                        recv_sem=recv_sems.at[idx], device_id=to, device_id_type=MESH_ID)


def _gather_comm(bufs, ici=True, pair=True):
    n = len(bufs)

    def half(ref, k, pc):
        rh = ref.shape[1] // 2
        return ref.at[k, pl.ds(pc * rh, rh), :]

    def ici_start(ins, outs, ss, rs, base):
        x, y, c = _mesh_pos()
        for a in range(n):
            mine = half(outs[a], 2 * x + y, c)
            for j, chip in enumerate(_other_chips(x, y)):
                _remote(mine, mine, ss, rs, base + 3 * a + j, (*chip, c)).start()

    def ici_finish(ins, outs, ss, rs, base):
        x, y, c = _mesh_pos()
        for a in range(n):
            for j, chip in enumerate(_other_chips(x, y)):
                theirs = half(outs[a], 2 * chip[0] + chip[1], c)
                _remote(theirs, theirs, ss, rs, base + 3 * a + j, (*chip, c)).wait()

    def pair_copy(outs, ss, rs, base, a):
        x, y, c = _mesh_pos()
        rh = outs[a].shape[1] // 2
        held = outs[a].at[:, pl.ds(c * rh, rh), :]
        return _remote(held, held, ss, rs, base + a, (x, y, 1 - c))

    def pair_start(ins, outs, ss, rs, base):
        for a in range(n):
            pair_copy(outs, ss, rs, base, a).start()

    def pair_finish(ins, outs, ss, rs, base):
        for a in range(n):
            pair_copy(outs, ss, rs, base, a).wait()

    parts = ([(3 * n, ici_start, ici_finish)] if ici else []) + ([(n, pair_start, pair_finish)] if pair else [])
    return _Comm(bufs, [_out(b.shape, b.dtype) for b in bufs], {a: a for a in range(n)}, parts)


def _merge_comm(*comms):
    ins, shapes, aliases, subs, base = [], [], {}, [], 0
    for cm in comms:
        (n_sems, start, finish), = cm.parts
        i0, o0 = len(ins), len(shapes)
        subs.append((slice(i0, i0 + len(cm.ins)), slice(o0, o0 + len(cm.out_shape)), base, start, finish))
        aliases.update({i0 + i: o0 + j for i, j in cm.aliases.items()})
        ins += cm.ins
        shapes += cm.out_shape
        base += n_sems

    def start_all(ins_r, outs_r, ss, rs, b):
        for si, so, off, start, _ in subs:
            start(ins_r[si], outs_r[so], ss, rs, b + off)

    def finish_all(ins_r, outs_r, ss, rs, b):
        for si, so, off, _, finish in subs:
            finish(ins_r[si], outs_r[so], ss, rs, b + off)

    return _Comm(ins, shapes, aliases, [(base, start_all, finish_all)])


def _own_half_buffers(pieces, dtypes, kc_arr):
    n = len(pieces)

    def body(kc_ref, *refs):
        for a in range(n):
            refs[n + a][0] = refs[a][...].astype(dtypes[a])

    def half(p):
        return p.shape[0] // 2, p.shape[1]

    return pl.pallas_call(
        body, name="own_halves",
        out_shape=[_out((N_CHIP,) + p.shape, dt) for p, dt in zip(pieces, dtypes)],
        grid_spec=pltpu.PrefetchScalarGridSpec(
            num_scalar_prefetch=1, grid=(1,),
            in_specs=[pl.BlockSpec(half(p), lambda i, kc: (kc[1], 0)) for p in pieces],
            out_specs=[pl.BlockSpec((1,) + half(p), lambda i, kc: (kc[0], kc[1], 0)) for p in pieces]),
        compiler_params=_cp(("arbitrary",), 48),
    )(kc_arr, *_in_hbm(*pieces))


def _exchange_comm(grads):
    n = len(grads)

    def copy(ins, outs, ss, rs, base, a):
        x, y, c = _mesh_pos()
        rh = ins[a].shape[1] // 2
        return _remote(ins[a].at[:, pl.ds((1 - c) * rh, rh), :], outs[a], ss, rs, base + a, (x, y, 1 - c))

    def start(ins, outs, ss, rs, base):
        for a in range(n):
            copy(ins, outs, ss, rs, base, a).start()

    def finish(ins, outs, ss, rs, base):
        for a in range(n):
            copy(ins, outs, ss, rs, base, a).wait()

    shapes = [_out((N_CHIP, g.shape[1] // 2, g.shape[2]), g.dtype) for g in grads]
    return _Comm(grads, shapes, {}, [(n, start, finish)])


def _chip_sum(name, g, recv, c_arr):
    _, r, cc = g.shape
    rh = r // 2

    def body(c_ref, g_ref, r_ref, o_ref):
        o_ref[...] = (g_ref[...] + r_ref[...]).astype(BF16)

    return pl.pallas_call(
        body, name=name, out_shape=_out((N_CHIP, rh, cc), BF16),
        grid_spec=pltpu.PrefetchScalarGridSpec(
            num_scalar_prefetch=1, grid=(N_CHIP,),
            in_specs=[pl.BlockSpec((1, rh, cc), lambda j, c_ref: (j, c_ref[0], 0)),
                      pl.BlockSpec((1, rh, cc), lambda j, c_ref: (j, 0, 0))],
            out_specs=pl.BlockSpec((1, rh, cc), lambda j, c_ref: (j, 0, 0))),
        compiler_params=_cp(("arbitrary",), 32),
    )(c_arr, *_in_hbm(g, recv))


def _scatter_comm(sums):
    n = len(sums)

    def copies(ins, outs, ss, rs, base):
        x, y, c = _mesh_pos()
        return [_remote(ins[a].at[2 * chip[0] + chip[1]], outs[a].at[j], ss, rs, base + 3 * a + j, (*chip, c))
                for a in range(n) for j, chip in enumerate(_other_chips(x, y))]

    def start(ins, outs, ss, rs, base):
        for cp in copies(ins, outs, ss, rs, base):
            cp.start()

    def finish(ins, outs, ss, rs, base):
        for cp in copies(ins, outs, ss, rs, base):
            cp.wait()

    shapes = [_out((3,) + s.shape[1:], s.dtype) for s in sums]
    return _Comm(sums, shapes, {}, [(3 * n, start, finish)])


def _scatter_copies(ins, lands, send_sems, recv_sems):
    x, y, c = _mesh_pos()
    return [_remote(ins[a].at[2 * chip[0] + chip[1]], lands[a].at[j], send_sems, recv_sems, 3 * a + j, (*chip, c))
            for a in range(len(ins)) for j, chip in enumerate(_other_chips(x, y))]


def _scatter_start(name, sums):
    n = len(sums)
    lands = [lax.empty((3,) + s.shape[1:], s.dtype) for s in sums]
    hbm = pl.BlockSpec(memory_space=pltpu.HBM)
    sem = pl.BlockSpec(memory_space=pltpu.SEMAPHORE)

    def body(*refs):
        ins, land_refs = refs[:n], refs[n:2 * n]
        send_sems, recv_sems = refs[2 * n], refs[2 * n + 1]
        token = refs[-1]
        for cp in _scatter_copies(ins, land_refs, send_sems, recv_sems):
            cp.start()
        token[...] = jnp.zeros_like(token)

    res = pl.pallas_call(
        body, name=name + "_scatter_start",
        out_shape=(pltpu.SemaphoreType.DMA((3 * n,)), pltpu.SemaphoreType.DMA((3 * n,)),
                   *[pltpu.HBM(s.shape, s.dtype) for s in sums], *[pltpu.HBM(ld.shape, ld.dtype) for ld in lands],
                   jax.ShapeDtypeStruct((8, 128), F32)),
        in_specs=[hbm] * (2 * n), out_specs=(sem, sem, *[hbm] * (2 * n), pl.BlockSpec(memory_space=pltpu.VMEM)),
        input_output_aliases={i: 2 + i for i in range(2 * n)},
        compiler_params=pltpu.CompilerParams(has_side_effects=pltpu.SideEffectType.DATAFLOW_SIDE_EFFECTING),
    )(*[pltpu.with_memory_space_constraint(a, pltpu.HBM) for a in list(sums) + lands])
    return res[0], res[1], list(res[2:2 + n]), list(res[2 + n:2 + 2 * n]), res[-1]


def _scatter_wait(name, send_sems, recv_sems, sums, lands, after):
    n = len(sums)
    hbm = pl.BlockSpec(memory_space=pltpu.HBM)
    sem = pl.BlockSpec(memory_space=pltpu.SEMAPHORE)

    def body(*refs):
        ins, land_refs = refs[:n], refs[n:2 * n]
        for cp in _scatter_copies(ins, land_refs, refs[2 * n], refs[2 * n + 1]):
            cp.wait_send()
            cp.wait_recv()

    res = pl.pallas_call(
        body, name=name + "_scatter_wait",
        out_shape=tuple([pltpu.HBM(s.shape, s.dtype) for s in sums] + [pltpu.HBM(ld.shape, ld.dtype) for ld in lands]),
        in_specs=[hbm] * (2 * n) + [sem, sem, pl.BlockSpec(memory_space=pl.ANY)], out_specs=tuple([hbm] * (2 * n)),
        input_output_aliases={i: i for i in range(2 * n)},
        compiler_params=pltpu.CompilerParams(has_side_effects=pltpu.SideEffectType.DATAFLOW_SIDE_EFFECTING),
    )(*sums, *lands, send_sems, recv_sems, after)
    return list(res[:n]), list(res[n:])


def _total_sum(name, sums, recv3, kc_arr):
    _, rh, cc = sums.shape

    def body(kc_ref, s_ref, r_ref, o_ref):
        t = s_ref[0].astype(F32) + r_ref[0].astype(F32)
        t = t + r_ref[1].astype(F32)
        o_ref[...] = t + r_ref[2].astype(F32)

    return pl.pallas_call(
        body, name=name, out_shape=_out((2 * rh, cc), F32),
        grid_spec=pltpu.PrefetchScalarGridSpec(
            num_scalar_prefetch=1, grid=(1,),
            in_specs=[pl.BlockSpec((1, rh, cc), lambda i, kc_ref: (kc_ref[0], 0, 0)),
                      pl.BlockSpec((3, rh, cc), lambda i, kc_ref: (0, 0, 0))],
            out_specs=pl.BlockSpec((rh, cc), lambda i, kc_ref: (kc_ref[1], 0))),
        compiler_params=_cp(("arbitrary",), 32),
    )(kc_arr, *_in_hbm(sums, recv3))


def _assemble_comm(totals):
    n = len(totals)

    def copy(outs, ss, rs, base, a):
        x, y, c = _mesh_pos()
        rh = outs[a].shape[0] // 2
        here = outs[a].at[pl.ds(c * rh, rh), :]
        return _remote(here, here, ss, rs, base + a, (x, y, 1 - c))

    def start(ins, outs, ss, rs, base):
        for a in range(n):
            copy(outs, ss, rs, base, a).start()

    def finish(ins, outs, ss, rs, base):
        for a in range(n):
            copy(outs, ss, rs, base, a).wait()

    shapes = [_out(t.shape, t.dtype) for t in totals]
    return _Comm(totals, shapes, {a: a for a in range(n)}, [(n, start, finish)])


def _small_layout(shapes):
    n = len(shapes)
    narrow_w = 64
    wide = [a for a in range(n) if shapes[a][1] > narrow_w]
    narrow = sorted((a for a in range(n) if shapes[a][1] <= narrow_w), key=lambda a: -shapes[a][0])
    offs, cols, groups, widths, rows = {}, {}, [], [], []
    if wide:
        r = 0
        for a in wide:
            offs[a], cols[a] = r, 0
            r += shapes[a][0]
        groups.append(wide)
        widths.append(max(shapes[a][1] for a in wide))
        rows.append(-(-r // 8) * 8)
    if narrow:
        heights = [0, 0]
        for a in narrow:
            side = 0 if heights[0] <= heights[1] else 1
            offs[a], cols[a] = heights[side], side * narrow_w
            heights[side] += shapes[a][0]
        groups.append(narrow)
        widths.append(2 * narrow_w)
        rows.append(-(-max(heights) // 8) * 8)

    def window(ref, a):
        return ref.at[offs[a]:offs[a] + shapes[a][0], cols[a]:cols[a] + shapes[a][1]]

    return groups, widths, rows, window


def _small_pack(arrays, me_arr):
    shapes = [a.shape for a in arrays]
    groups, widths, rows, window = _small_layout(shapes)
    n, n_g = len(arrays), len(groups)

    def body(me_ref, *refs):
        ins, outs = refs[:n], refs[n:]
        for gi, g in enumerate(groups):
            outs[gi][...] = jnp.zeros_like(outs[gi])
            for a in g:
                window(outs[gi].at[0], a)[...] = ins[a][...]

    return pl.pallas_call(
        body, name="small_pack", out_shape=[_out((8, r, w), F32) for r, w in zip(rows, widths)],
        grid_spec=pltpu.PrefetchScalarGridSpec(
            num_scalar_prefetch=1, grid=(1,), in_specs=[pl.BlockSpec(s, lambda i, me: (0, 0)) for s in shapes],
            out_specs=[pl.BlockSpec((1, r, w), lambda i, me: (me[0], 0, 0)) for r, w in zip(rows, widths)]),
        compiler_params=_cp(("arbitrary",), 32),
    )(me_arr, *_in_hbm(*arrays))


def _spread_comm(slots):
    n = len(slots)
    flips = [(dx, dy, dc) for dx in range(2) for dy in range(2) for dc in range(2)][1:]

    def copies(outs, ss, rs, base):
        x, y, c = _mesh_pos()
        mine = 4 * x + 2 * y + c
        return [_remote(outs[a].at[mine], outs[a].at[mine], ss, rs, base + 7 * a + f,
                        (x ^ dx, y ^ dy, c ^ dc)) for a in range(n) for f, (dx, dy, dc) in enumerate(flips)]

    def start(ins, outs, ss, rs, base):
        for cp in copies(outs, ss, rs, base):
            cp.start()

    def finish(ins, outs, ss, rs, base):
        for cp in copies(outs, ss, rs, base):
            cp.wait()

    return _Comm(slots, [_out(s.shape, s.dtype) for s in slots], {a: a for a in range(n)}, [(7 * n, start, finish)])


def _small_total(slots, shapes):
    groups, widths, rows, window = _small_layout(shapes)
    n, n_g = len(shapes), len(groups)

    def body(*refs):
        ins, outs, acc = refs[:n_g], refs[n_g:n_g + n], refs[n_g + n:]
        for gi, g in enumerate(groups):
            t = ins[gi][0] + ins[gi][1]
            for d in range(2, 8):
                t = t + ins[gi][d]
            acc[gi][...] = t
            for a in g:
                outs[a][...] = window(acc[gi], a)[...]

    return pl.pallas_call(
        body, name="small_total", grid=(1,), out_shape=[_out(s, F32) for s in shapes],
        in_specs=[_full(s.shape) for s in slots], out_specs=[_full(s) for s in shapes],
        scratch_shapes=[pltpu.VMEM((r, w), F32) for r, w in zip(rows, widths)],
        compiler_params=_cp(("arbitrary",), 48),
    )(*_in_hbm(*slots))


def _small_allreduce(arrays, comm):
    n = len(arrays)
    shapes = [a.shape for a in arrays]
    groups, widths, rows, window = _small_layout(shapes)
    n_g = len(groups)

    def body(*refs):
        ins, outs = refs[:n], refs[n:2 * n]
        pack, sib, csum, every = (refs[2 * n + i * n_g:2 * n + (i + 1) * n_g] for i in range(4))
        send_sems, recv_sems = refs[2 * n + 4 * n_g:]
        x, y, c = _mesh_pos()
        k = 2 * x + y
        for gi, g in enumerate(groups):
            pack[gi][...] = jnp.zeros_like(pack[gi])
            for a in g:
                window(pack[gi], a)[...] = ins[a][...]
        cps = [_remote(pack[gi], sib[gi], send_sems, recv_sems, gi, (x, y, 1 - c)) for gi in range(n_g)]
        for cp in cps:
            cp.start()
        for cp in cps:
            cp.wait()
        for gi in range(n_g):
            csum[gi][...] = pack[gi][...] + sib[gi][...]
            every[gi][k] = csum[gi][...]
        cps = [_remote(csum[gi], every[gi].at[k], send_sems, recv_sems, n_g + 3 * gi + j, (*chip, c))
               for gi in range(n_g) for j, chip in enumerate(_other_chips(x, y))]
        for cp in cps:
            cp.start()
        for cp in cps:
            cp.wait()
        for gi, g in enumerate(groups):
            pack[gi][...] = ((every[gi][0] + every[gi][1]) + every[gi][2]) + every[gi][3]
            for a in g:
                outs[a][...] = window(pack[gi], a)[...]

    bufs = [pltpu.VMEM((r, w), F32) for r, w in zip(rows, widths)]
    return _call(
        body, comm, (0,), arrays, name="small_allreduce", grid=(1,), out_shape=[_out(s, F32) for s in shapes],
        in_specs=[_full(s) for s in shapes], out_specs=[_full(s) for s in shapes],
        scratch_shapes=bufs * 3 + [pltpu.VMEM((N_CHIP, r, w), F32) for r, w in zip(rows, widths)] +
                       [pltpu.SemaphoreType.DMA((4 * n_g,)), pltpu.SemaphoreType.DMA((4 * n_g,))],
        compiler_params=_cp(("arbitrary",), 40))


def _adamw_small(ws, gs, ms, vs, comm):
    n = len(ws)

    def body(*refs):
        w, g, m, v, d, mo, vo = (refs[i * n:(i + 1) * n] for i in range(7))
        for a in range(n):
            d[a][...], mo[a][...], vo[a][...] = _adamw_math(w[a][...], g[a][...], m[a][...], v[a][...])

    specs = [_full(w.shape) for w in ws]
    res, got = _call(
        body, comm, (0,), (*ws, *gs, *ms, *vs), name="adamw_small", grid=(1,),
        out_shape=[_out(w.shape, F32) for w in ws] * 3,
        in_specs=specs * 4, out_specs=specs * 3, compiler_params=_cp(("arbitrary",), 40))
    return (res[:n], res[n:2 * n], res[2 * n:]), got


def _adamw_math(w, g, m, v):
    m = ADAM_B1 * m + (1.0 - ADAM_B1) * g
    v = ADAM_B2 * v + (1.0 - ADAM_B2) * (g * g)
    m_hat = m / (1.0 - ADAM_B1 ** ADAM_STEP)
    v_hat = v / (1.0 - ADAM_B2 ** ADAM_STEP)
    delta = -ADAM_LR * (m_hat / (jnp.sqrt(v_hat) + ADAM_EPS) + ADAM_WD * w)
    return delta, m, v


def _adamw(name, w, g, m, v):
    r, c = w.shape
    tr = max(t for t in range(8, 513, 8) if r % t == 0)

    def body(w_ref, g_ref, m_ref, v_ref, d_ref, mo_ref, vo_ref):
        d_ref[...], mo_ref[...], vo_ref[...] = _adamw_math(w_ref[...], g_ref[...], m_ref[...], v_ref[...])

    return pl.pallas_call(
        body, name=name, grid=(r // tr,), in_specs=[_rows(tr, c)] * 4, out_specs=[_rows(tr, c)] * 3,
        out_shape=[_out((r, c), F32)] * 3, compiler_params=_cp(("arbitrary",), 32),
    )(*_in_hbm(w, g, m, v))


def _as_matrix(name, a):
    if name == "na_rpb":
        return a[0].transpose(1, 0, 2).reshape(N_HEADS * (2 * KH - 1), 2 * KW - 1)
    if name in ("s5_b_re", "s5_b_im"):
        return a.transpose(0, 1, 2, 4, 3).reshape(2 * S5_G * S5_H, S5_P)
    if name in ("s5_c_re", "s5_c_im"):
        return a.reshape(2 * S5_G * S5_H, S5_P)
    if name in ("s5_lam_re", "s5_lam_im"):
        return a.reshape(2 * S5_G, S5_P)
    if name == "s5_log_dt":
        return a.reshape(2, S5_G)
    return a


def _from_matrix(name, m):
    if name == "na_rpb":
        return m.reshape(2 * KH - 1, N_HEADS, 2 * KW - 1).transpose(1, 0, 2)[None]
    if name in ("s5_b_re", "s5_b_im"):
        return m.reshape(1, 2, S5_G, S5_H, S5_P).transpose(0, 1, 2, 4, 3)
    if name in ("s5_c_re", "s5_c_im"):
        return m.reshape(1, 2, S5_G, S5_H, S5_P)
    if name in ("s5_lam_re", "s5_lam_im"):
        return m.reshape(1, 2, S5_G, S5_P)
    if name == "s5_log_dt":
        return m.reshape(1, 2, S5_G)
    return m


WEIGHTS = ["meta_tokens", "ffn1_pre_g", "ffn1_post_g", "ffn1_w_gate", "ffn1_w_up", "ffn1_w_down", "mix_pre_g", "w_in",
           "na_rpb", "s5_lam_re", "s5_lam_im", "s5_log_dt", "s5_b_re", "s5_b_im", "s5_c_re", "s5_c_im", "s5_d",
           "s5_w_glu", "s5_b_glu", "na_out_g", "s5_out_g", "w_out", "mix_post_g", "ffn2_pre_g", "ffn2_post_g",
           "ffn2_w_gate", "ffn2_w_up", "ffn2_w_down", "final_g"]
BIG = ["ffn1_w_gate", "ffn1_w_up", "ffn1_w_down", "w_in", "s5_w_glu", "w_out", "ffn2_w_gate", "ffn2_w_up",
       "ffn2_w_down"]
TRANSPOSED = ["ffn1_w_gate", "ffn1_w_up", "ffn2_w_gate", "ffn2_w_up"]
GAINS = ["ffn1_pre_g", "ffn1_post_g", "mix_pre_g", "s5_d", "s5_b_glu", "na_out_g", "s5_out_g", "mix_post_g",
         "ffn2_pre_g", "ffn2_post_g", "final_g"]
SMALL = [n for n in WEIGHTS if n not in BIG]


def kernel(*args):
    names = ["x"] + WEIGHTS + ["loss_target"] + ["m_" + n for n in WEIGHTS] + ["v_" + n for n in WEIGHTS]
    assert len(args) == len(names)
    given = dict(zip(names, args))
    x_pos, y_pos, c_pos = _mesh_pos()
    k_pos = 2 * x_pos + y_pos
    c_arr = jnp.reshape(c_pos, (1,)).astype(jnp.int32)
    kc_arr = jnp.stack([k_pos, c_pos]).astype(jnp.int32)

    def piece(name, a):
        return a[0].T if name in TRANSPOSED else a[0]

    def unpiece(name, a):
        return a.T[None] if name in TRANSPOSED else a[None]

    placed = BIG + ["meta_tokens"]
    bufs = dict(zip(placed, _own_half_buffers([piece(n, given[n]) for n in BIG] + [given["meta_tokens"]],
                                              [BF16] * len(BIG) + [F32], kc_arr)))

    gains = {n: given[n] for n in GAINS}
    s5 = {n: _as_matrix("s5_" + n, given["s5_" + n])
          for n in ["lam_re", "lam_im", "log_dt", "b_re", "b_im", "c_re", "c_im"]}
    me_arr = jnp.reshape(4 * x_pos + 2 * y_pos + c_pos, (1,)).astype(jnp.int32)
    loss, dh0, pieces, small, late, (ffn1, flight1), (mid, flight2) = _step(
        given["x"][0], given["loss_target"][0], bufs, gains, s5, given["na_rpb"][0], c_arr, kc_arr, me_arr)
    loss = lax.psum(loss, ("x", "y", "c"))
    n_tok = given["x"].shape[1]
    grad_x = dh0[N_META:N_META + n_tok][None]

    late["meta_tokens"] = dh0[:N_META]
    out_g, out_d, out_m, out_v = {}, {}, {}, {}

    def update_big(n):
        g2 = pieces[n]
        d2, m2, v2 = _adamw("adamw_" + n, piece(n, given[n]), g2, piece(n, given["m_" + n]),
                            piece(n, given["v_" + n]))
        out_g[n], out_d[n], out_m[n], out_v[n] = (unpiece(n, t) for t in (g2, d2, m2, v2))
        return v2

    done2 = [update_big(n) for n in pieces]
    sums1, recv1 = _scatter_wait("ffn1", *flight1, done2[-1])
    totals1 = [_total_sum("total_sum_" + n, s, r, kc_arr) for n, s, r in zip(ffn1, sums1, recv1)]
    pieces.update(zip(ffn1, _run_comm("ffn1_pair_assemble", _assemble_comm(totals1))))
    done1 = [update_big(n) for n in ffn1]
    late_arrays = list(late.values())
    late_arrays[0], _ = lax.optimization_barrier((late_arrays[0], (done1[-1], small["final_g"])))
    red, _ = _small_allreduce(late_arrays, None)
    small.update(zip(late, red))
    mc = D // N_CHIP
    small["meta_tokens"] = lax.dynamic_slice_in_dim(small["meta_tokens"], k_pos * mc, mc, 1)
    sums_rest, recv_rest = _scatter_wait("rest", *flight2, red[0])
    totals = [_total_sum("total_sum_" + n, s, r, kc_arr) for n, s, r in zip(mid, sums_rest, recv_rest)]
    gs = [small[n] for n in SMALL]
    (d2, m2, v2), done = _adamw_small([_as_matrix(n, given[n]) for n in SMALL], gs,
                                      [_as_matrix(n, given["m_" + n]) for n in SMALL],
                                      [_as_matrix(n, given["v_" + n]) for n in SMALL], _assemble_comm(totals))
    pieces.update(zip(mid, done))

    for n, g, dd, mm, vv in zip(SMALL, gs, d2, m2, v2):
        out_g[n], out_d[n], out_m[n], out_v[n] = (_from_matrix(n, t) for t in (g, dd, mm, vv))
    for n in mid:
        update_big(n)
    return (loss, grad_x, *[out_g[n] for n in WEIGHTS], *[out_d[n] for n in WEIGHTS],
            *[out_m[n] for n in WEIGHTS], *[out_v[n] for n in WEIGHTS])
```

```python
import math

import numpy as np
import jax
import jax.numpy as jnp
from jax import lax
from jax.experimental import pallas as pl
from jax.experimental.pallas import tpu as pltpu

F32 = jnp.float32
BF16 = jnp.bfloat16

D = 1024
N_META = 16
GRID_W = 64
NA_W = 512
S5_W = 512
HEAD_DIM = 64
N_HEADS = 8
KH = 8
KW = 16
S5_G = 32
S5_P = 64
S5_H = 16
N_BUNDLE = 4
FF = 2816
N_CHIP = 4
FC = FF // N_CHIP
EPS = 1e-6
NEG_INF = -1e30
Q_ROWS = 4
K_ROWS = 12
QB = Q_ROWS * GRID_W
KB = K_ROWS * GRID_W
SCAN_CHUNK = 256

ADAM_LR = 0.001
ADAM_B1 = 0.9
ADAM_B2 = 0.999
ADAM_EPS = 1e-08
ADAM_WD = 0.01
ADAM_STEP = 10

NT = (((1,), (1,)), ((), ()))
TN = (((0,), (0,)), ((), ()))
MESH_ID = pl.DeviceIdType.MESH


def _cp(sem=None, vmem_mb=None):
    kw = {}
    if sem is not None:
        kw["dimension_semantics"] = sem
    if vmem_mb is not None:
        kw["vmem_limit_bytes"] = vmem_mb << 20
    return pltpu.CompilerParams(**kw)


def _full(shape):
    n = len(shape)
    return pl.BlockSpec(shape, lambda *_: (0,) * n)


def _rows(tm, w):
    return pl.BlockSpec((tm, w), lambda i: (i, 0))


ANY = pl.BlockSpec(memory_space=pl.ANY)


def _rms(x, g):
    r = lax.rsqrt(jnp.mean(x * x, axis=-1, keepdims=True) + EPS)
    return x * r * g


def _rms_bwd(x, g, dy):
    r = lax.rsqrt(jnp.mean(x * x, axis=-1, keepdims=True) + EPS)
    xh = x * r
    dg = jnp.sum(dy * xh, axis=0, keepdims=True)
    dyg = dy * g
    dx = r * (dyg - xh * jnp.mean(dyg * xh, axis=-1, keepdims=True))
    return dx, dg


def _out(shape, dtype):
    return pltpu.HBM(tuple(shape), dtype)


def _in_hbm(*args):
    return [pltpu.with_memory_space_constraint(a, pltpu.HBM) if jnp.issubdtype(a.dtype, jnp.floating) and a.ndim > 1
            else a for a in args]


def _dot(a, b):
    return jnp.dot(a, b, preferred_element_type=F32)


def _dg(a, b, dims):
    return lax.dot_general(a, b, dims, preferred_element_type=F32)


def _ffn_fwd(name, h, g_pre, g_post, wg, wu, wd, tm, comm=None, bounds=()):
    tp = h.shape[0]
    nt = tp // tm

    def body(h_ref, gp_ref, gq_ref, wg_ref, wu_ref, wd_ref, hn_ref, gate_ref, up_ref, f_ref, xn_s, acc_s):
        c = pl.program_id(1)

        @pl.when(c == 0)
        def _():
            xn_s[...] = _rms(h_ref[...], gp_ref[...]).astype(BF16)
            acc_s[...] = jnp.zeros_like(acc_s)

        xn = xn_s[...]
        gate = _dg(xn, wg_ref[0], NT)
        up = _dg(xn, wu_ref[0], NT)
        gate_ref[0] = gate
        up_ref[0] = up
        act = (gate * jax.nn.sigmoid(gate) * up).astype(BF16)
        acc_s[...] += _dot(act, wd_ref[0])

        @pl.when(c == N_CHIP - 1)
        def _():
            f = acc_s[...]
            f_ref[...] = f
            hn_ref[...] = h_ref[...] + 0.5 * _rms(f, gq_ref[...])

    return _call(
        body, comm, bounds, (h, g_pre, g_post, wg, wu, wd), name=name, grid=(nt, N_CHIP),
        in_specs=[pl.BlockSpec((tm, D), lambda i, c: (i, 0)), _full((1, D)), _full((1, D))] +
                 [pl.BlockSpec((1, FC, D), lambda i, c: (c, 0, 0))] * 3,
        out_specs=[pl.BlockSpec((tm, D), lambda i, c: (i, 0)),
                   pl.BlockSpec((1, tm, FC), lambda i, c: (c, i, 0)),
                   pl.BlockSpec((1, tm, FC), lambda i, c: (c, i, 0)),
                   pl.BlockSpec((tm, D), lambda i, c: (i, 0))],
        out_shape=[_out((tp, D), F32), _out((N_CHIP, tp, FC), F32),
                   _out((N_CHIP, tp, FC), F32), _out((tp, D), F32)],
        scratch_shapes=[pltpu.VMEM((tm, D), BF16), pltpu.VMEM((tm, D), F32)],
        compiler_params=_cp(("arbitrary", "arbitrary"), 48))


def _ffn_bwd(name, h, g_pre, df, gate, up, wg, wu, wd, tm, comm=None, bounds=()):
    tp = h.shape[0]
    nt = tp // tm
    rh = FC // 2

    def body(h_ref, gp_ref, df_ref, gate_ref, up_ref, wg_ref, wu_ref, wd_ref,
             dwg_ref, dwu_ref, dwd_ref, dxn_ref, rg_ref, ru_ref, rd_ref, ag, au, ad, send_sems, recv_sems):
        c = pl.program_id(0)
        i = pl.program_id(1)

        def to_sibling(a, piece):
            x, y, core = _mesh_pos()
            dw_ref, r_ref = ((dwg_ref, rg_ref), (dwu_ref, ru_ref), (dwd_ref, rd_ref))[a]
            return _remote(dw_ref.at[piece, pl.ds((1 - core) * rh, rh), :], r_ref.at[piece], send_sems, recv_sems,
                           3 * piece + a, (x, y, 1 - core))

        @pl.when(i == 0)
        def _():
            ag[...] = jnp.zeros_like(ag)
            au[...] = jnp.zeros_like(au)
            ad[...] = jnp.zeros_like(ad)

        xn = _rms(h_ref[...], gp_ref[...]).astype(BF16)
        dfb = df_ref[...].astype(BF16)
        gt = gate_ref[0]
        u = up_ref[0]
        sg = jax.nn.sigmoid(gt)
        si = gt * sg
        act = (si * u).astype(BF16)
        dact = _dg(dfb, wd_ref[0], NT)
        ad[...] += _dg(act, dfb, TN)
        dgate = (dact * u * (sg * (1.0 + gt * (1.0 - sg)))).astype(BF16)
        dup = (dact * si).astype(BF16)
        ag[...] += _dg(dgate, xn, TN)
        au[...] += _dg(dup, xn, TN)
        dxn_ref[0] = _dot(dgate, wg_ref[0]) + _dot(dup, wu_ref[0])

        @pl.when(i == nt - 1)
        def _():
            pltpu.sync_copy(ag, dwg_ref.at[c])
            pltpu.sync_copy(au, dwu_ref.at[c])
            pltpu.sync_copy(ad, dwd_ref.at[c])
            for a in range(3):
                to_sibling(a, c).start()

        @pl.when((c == N_CHIP - 1) & (i == nt - 1))
        def _():
            for piece in range(N_CHIP):
                for a in range(3):
                    to_sibling(a, piece).wait()

    return _call(
        body, comm, bounds, (h, g_pre, df, gate, up, wg, wu, wd), name=name, grid=(N_CHIP, nt),
        in_specs=[pl.BlockSpec((tm, D), lambda c, i: (i, 0)), _full((1, D)),
                  pl.BlockSpec((tm, D), lambda c, i: (i, 0)),
                  pl.BlockSpec((1, tm, FC), lambda c, i: (c, i, 0)),
                  pl.BlockSpec((1, tm, FC), lambda c, i: (c, i, 0))] +
                 [pl.BlockSpec((1, FC, D), lambda c, i: (c, 0, 0))] * 3,
        out_specs=[ANY, ANY, ANY, pl.BlockSpec((1, tm, D), lambda c, i: (c, i, 0)), ANY, ANY, ANY],
        out_shape=[_out((N_CHIP, FC, D), F32)] * 3 + [_out((N_CHIP, tp, D), F32)] +
                  [_out((N_CHIP, rh, D), F32)] * 3,
        scratch_shapes=[pltpu.VMEM((FC, D), F32)] * 3 +
                       [pltpu.SemaphoreType.DMA((3 * N_CHIP,)), pltpu.SemaphoreType.DMA((3 * N_CHIP,))],
        compiler_params=_cp(("arbitrary", "arbitrary"), 58))


def _ffn_pre_bwd(name, dh, dxn_part, h, g_pre, tm, comm=None, bounds=()):
    tp = h.shape[0]
    nt = tp // tm

    def body(dh_ref, dxn_ref, h_ref, gp_ref, out_ref, dg_ref):
        i = pl.program_id(0)
        dxn = (dxn_ref[0] + dxn_ref[1]) + (dxn_ref[2] + dxn_ref[3])
        dx, dg = _rms_bwd(h_ref[...], gp_ref[...], dxn)
        out_ref[...] = dh_ref[...] + dx

        @pl.when(i == 0)
        def _():
            dg_ref[...] = jnp.zeros_like(dg_ref)

        dg_ref[...] += dg

    return _call(
        body, comm, bounds, (dh, dxn_part, h, g_pre), name=name, grid=(nt,),
        in_specs=[_rows(tm, D), pl.BlockSpec((N_CHIP, tm, D), lambda i: (0, i, 0)), _rows(tm, D), _full((1, D))],
        out_specs=[_rows(tm, D), _full((1, D))],
        out_shape=[_out((tp, D), F32), _out((1, D), F32)],
        compiler_params=_cp(("arbitrary",), 48))


def _mix_in(h, g, w_in, tm):
    tp = h.shape[0]

    def body(h_ref, g_ref, w_ref, q_ref, k_ref, v_ref, u_ref):
        a = _rms(h_ref[...], g_ref[...]).astype(BF16)
        q_ref[...] = _dot(a, w_ref[0]).astype(BF16)
        k_ref[...] = _dot(a, w_ref[1]).astype(BF16)
        v_ref[...] = _dot(a, w_ref[2]).astype(BF16)
        u_ref[...] = _dot(a, w_ref[3])

    return pl.pallas_call(
        body, name="mix_in", grid=(tp // tm,),
        in_specs=[_rows(tm, D), _full((1, D)), _full((N_CHIP, D, NA_W))],
        out_specs=[_rows(tm, NA_W)] * 4,
        out_shape=[_out((tp, NA_W), BF16)] * 3 + [_out((tp, S5_W), F32)],
        compiler_params=_cp(("arbitrary",), 40),
    )(*_in_hbm(h, g, w_in))


def _gelu(x):
    return jax.nn.gelu(x, approximate=True)


def _gelu_grad(x):
    k = math.sqrt(2.0 / math.pi)
    t = jnp.tanh(k * (x + 0.044715 * x * x * x))
    return 0.5 * (1.0 + t) + 0.5 * x * (1.0 - t * t) * k * (1.0 + 3.0 * 0.044715 * x * x)


def _mix_out(o_na, y_pre, h, w_glu, b_glu, g_na, g_s5, w_out, g_post, tm, comm=None, bounds=()):
    tp = h.shape[0]

    def body(ona_ref, yp_ref, h_ref, wglu_ref, bglu_ref, gna_ref, gs5_ref, wout_ref, gpost_ref, hn_ref, mix_ref):
        y = _gelu(yp_ref[...])
        z = _dot(y.astype(BF16), wglu_ref[...]) + bglu_ref[...]
        o_s5 = y * jax.nn.sigmoid(z)
        n1 = _rms(ona_ref[...], gna_ref[...]).astype(BF16)
        n2 = _rms(o_s5, gs5_ref[...]).astype(BF16)
        mix = _dot(n1, wout_ref[0:NA_W, :]) + _dot(n2, wout_ref[NA_W:, :])
        mix_ref[...] = mix
        hn_ref[...] = h_ref[...] + _rms(mix, gpost_ref[...])

    return _call(
        body, comm, bounds, (o_na, y_pre, h, w_glu, b_glu, g_na, g_s5, w_out, g_post), name="mix_out",
        grid=(tp // tm,),
        in_specs=[_rows(tm, NA_W), _rows(tm, S5_W), _rows(tm, D), _full((S5_W, S5_W)), _full((1, S5_W)),
                  _full((1, NA_W)), _full((1, S5_W)), _full((D, D)), _full((1, D))],
        out_specs=[_rows(tm, D), _rows(tm, D)],
        out_shape=[_out((tp, D), F32)] * 2,
        compiler_params=_cp(("arbitrary",), 40))


def _mix_out_bwd(dh, mix, o_na, y_pre, w_glu, b_glu, g_na, g_s5, w_out, g_post, tm):
    tp = dh.shape[0]
    nt = tp // tm

    def body(dh_ref, mix_ref, ona_ref, yp_ref, wglu_ref, bglu_ref, gna_ref, gs5_ref, wout_ref, gpost_ref,
             dona_ref, dyp_ref, dwout_ref, dwglu_ref, dgpost_ref, dgna_ref, dgs5_ref, dbglu_ref, a_out, a_glu):
        i = pl.program_id(0)

        @pl.when(i == 0)
        def _():
            a_out[...] = jnp.zeros_like(a_out)
            a_glu[...] = jnp.zeros_like(a_glu)
            dgpost_ref[...] = jnp.zeros_like(dgpost_ref)
            dgna_ref[...] = jnp.zeros_like(dgna_ref)
            dgs5_ref[...] = jnp.zeros_like(dgs5_ref)
            dbglu_ref[...] = jnp.zeros_like(dbglu_ref)

        dmix, dgpost = _rms_bwd(mix_ref[...], gpost_ref[...], dh_ref[...])
        dgpost_ref[...] += dgpost
        yp = yp_ref[...]
        y = _gelu(yp)
        yb = y.astype(BF16)
        z = _dot(yb, wglu_ref[...]) + bglu_ref[...]
        sg = jax.nn.sigmoid(z)
        o_s5 = y * sg
        o_na = ona_ref[...]
        n1 = _rms(o_na, gna_ref[...]).astype(BF16)
        n2 = _rms(o_s5, gs5_ref[...]).astype(BF16)
        dmb = dmix.astype(BF16)
        a_out[0:NA_W, :] += _dg(n1, dmb, TN)
        a_out[NA_W:, :] += _dg(n2, dmb, TN)
        dn1 = _dg(dmb, wout_ref[0:NA_W, :], NT)
        dn2 = _dg(dmb, wout_ref[NA_W:, :], NT)
        dona, dgna = _rms_bwd(o_na, gna_ref[...], dn1)
        dona_ref[...] = dona
        dgna_ref[...] += dgna
        dos5, dgs5 = _rms_bwd(o_s5, gs5_ref[...], dn2)
        dgs5_ref[...] += dgs5
        dz = dos5 * y * (sg * (1.0 - sg))
        dbglu_ref[...] += jnp.sum(dz, axis=0, keepdims=True)
        dzb = dz.astype(BF16)
        a_glu[...] += _dg(yb, dzb, TN)
        dy = dos5 * sg + _dg(dzb, wglu_ref[...], NT)
        dyp_ref[...] = dy * _gelu_grad(yp)

        @pl.when(i == nt - 1)
        def _():
            pltpu.sync_copy(a_out, dwout_ref)
            pltpu.sync_copy(a_glu, dwglu_ref)

    return pl.pallas_call(
        body, name="mix_out_bwd", grid=(nt,),
        in_specs=[_rows(tm, D), _rows(tm, D), _rows(tm, NA_W), _rows(tm, S5_W), _full((S5_W, S5_W)),
                  _full((1, S5_W)), _full((1, NA_W)), _full((1, S5_W)), _full((D, D)), _full((1, D))],
        out_specs=[_rows(tm, NA_W), _rows(tm, S5_W), ANY, ANY, _full((1, D)), _full((1, NA_W)),
                   _full((1, S5_W)), _full((1, S5_W))],
        out_shape=[_out((tp, NA_W), F32), _out((tp, S5_W), F32),
                   _out((D, D), F32), _out((S5_W, S5_W), F32),
                   _out((1, D), F32), _out((1, NA_W), F32),
                   _out((1, S5_W), F32), _out((1, S5_W), F32)],
        scratch_shapes=[pltpu.VMEM((D, D), F32), pltpu.VMEM((S5_W, S5_W), F32)],
        compiler_params=_cp(("arbitrary",), 48),
    )(*_in_hbm(dh, mix, o_na, y_pre, w_glu, b_glu, g_na, g_s5, w_out, g_post))


def _mix_in_bwd(dq, dk, dv, du, h, g, w_in, dh, f1, g_post1, tm, comm=None, bounds=()):
    tp = h.shape[0]
    nt = tp // tm

    def body(dq_ref, dk_ref, dv_ref, du_ref, h_ref, g_ref, w_ref, dh_ref, f_ref, gq_ref,
             dh1_ref, df_ref, dw_ref, dg_ref, dgq_ref, acc):
        i = pl.program_id(0)

        @pl.when(i == 0)
        def _():
            acc[...] = jnp.zeros_like(acc)
            dg_ref[...] = jnp.zeros_like(dg_ref)
            dgq_ref[...] = jnp.zeros_like(dgq_ref)

        x = h_ref[...]
        a = _rms(x, g_ref[...]).astype(BF16)
        da = jnp.zeros((tm, D), F32)
        for j, r in enumerate((dq_ref, dk_ref, dv_ref, du_ref)):
            dp = r[...].astype(BF16)
            da = da + _dg(dp, w_ref[j], NT)
            acc[j] += _dg(a, dp, TN)
        dx, dg = _rms_bwd(x, g_ref[...], da)
        dh1 = dh_ref[...] + dx
        dh1_ref[...] = dh1
        dg_ref[...] += dg
        df, dgq = _rms_bwd(f_ref[...], gq_ref[...], 0.5 * dh1)
        df_ref[...] = df
        dgq_ref[...] += dgq

        @pl.when(i == nt - 1)
        def _():
            pltpu.sync_copy(acc, dw_ref)

    return _call(
        body, comm, bounds, (dq, dk, dv, du, h, g, w_in, dh, f1, g_post1), name="mix_in_bwd", grid=(nt,),
        in_specs=[_rows(tm, NA_W)] * 4 + [_rows(tm, D), _full((1, D)), _full((N_CHIP, D, NA_W)), _rows(tm, D),
                                         _rows(tm, D), _full((1, D))],
        out_specs=[_rows(tm, D), _rows(tm, D), ANY, _full((1, D)), _full((1, D))],
        out_shape=[_out((tp, D), F32), _out((tp, D), F32),
                   _out((N_CHIP, D, NA_W), F32), _out((1, D), F32),
                   _out((1, D), F32)],
        scratch_shapes=[pltpu.VMEM((N_CHIP, D, NA_W), F32)],
        compiler_params=_cp(("arbitrary",), 48))


def _final_loss(h, g_final, target, f2, g_post2, n_tok, tm):
    tp = h.shape[0]

    def body(h_ref, g_ref, t_ref, f_ref, gq_ref, dh_ref, df_ref, loss_ref, dg_ref, dgq_ref):
        i = pl.program_id(0)

        @pl.when(i == 0)
        def _():
            loss_ref[...] = jnp.zeros_like(loss_ref)
            dg_ref[...] = jnp.zeros_like(dg_ref)
            dgq_ref[...] = jnp.zeros_like(dgq_ref)

        x = h_ref[...]
        y = _rms(x, g_ref[...])
        row = i * tm + lax.broadcasted_iota(jnp.int32, (tm, 1), 0)
        valid = (row >= N_META) & (row < N_META + n_tok)
        e = jnp.where(valid, y - t_ref[...], 0.0)
        loss_ref[...] += 0.5 * jnp.sum(jnp.mean(e * e, axis=-1, keepdims=True), axis=0, keepdims=True)
        dx, dg = _rms_bwd(x, g_ref[...], e * (1.0 / D))
        dh_ref[...] = dx
        dg_ref[...] += dg
        df, dgq = _rms_bwd(f_ref[...], gq_ref[...], 0.5 * dx)
        df_ref[...] = df
        dgq_ref[...] += dgq

    return pl.pallas_call(
        body, name="final_loss", grid=(tp // tm,),
        in_specs=[_rows(tm, D), _full((1, D)), _rows(tm, D), _rows(tm, D), _full((1, D))],
        out_specs=[_rows(tm, D), _rows(tm, D), _full((1, 1)), _full((1, D)), _full((1, D))],
        out_shape=[_out((tp, D), F32), _out((tp, D), F32),
                   _out((1, 1), F32), _out((1, D), F32),
                   _out((1, D), F32)],
        compiler_params=_cp(("arbitrary",), 40),
    )(*_in_hbm(h, g_final, target, f2, g_post2))


def _na_patterns(n_rows):
    pats = []
    for kind in range(3):
        pat = [[-1] * K_ROWS for _ in range(Q_ROWS)]
        for i in range(Q_ROWS):
            for jj in range(K_ROWS):
                if kind == 0 and jj < KH:
                    pat[i][jj] = jj - i + KH - 1
                elif kind == 1 and i <= jj < i + KH:
                    pat[i][jj] = jj - i + 3
                elif kind == 2 and K_ROWS - KH <= jj:
                    pat[i][jj] = jj - i - 1
        pats.append(pat)
    return pats


def _diag_onehot():
    q = np.arange(GRID_W)[:, None]
    kc = np.arange(GRID_W)[None, :]
    start = np.clip(q - KW // 2, 0, GRID_W - KW)
    col_in = (kc >= start) & (kc < start + KW)
    e = np.zeros((32, GRID_W, GRID_W), np.float32)
    for d in range(2 * KW - 1):
        e[d] = ((kc - q + KW - 1) == d) & col_in
    return e.reshape(32, GRID_W * GRID_W), col_in


def _rpb_collapse(dtb2, et):
    def body(d_ref, e_ref, o_ref):
        o_ref[...] = jnp.dot(d_ref[...], e_ref[...], preferred_element_type=F32, precision=lax.Precision.HIGHEST)

    out = (dtb2.shape[0], et.shape[1])
    return pl.pallas_call(
        body, name="rpb_collapse", grid=(1,), out_shape=_out(out, F32),
        in_specs=[_full(dtb2.shape), _full(et.shape)], out_specs=_full(out),
    )(*_in_hbm(dtb2, et))


def _bias_tables(rpb, n_rows, comm=None, bounds=()):
    n_dr, n_dc = 2 * KH - 1, 2 * KW - 1
    pats = _na_patterns(n_rows)

    def body(rpb_ref, o_ref):
        h = pl.program_id(0)
        q = lax.broadcasted_iota(jnp.int32, (GRID_W, GRID_W), 0)
        kc = lax.broadcasted_iota(jnp.int32, (GRID_W, GRID_W), 1)
        start = jnp.clip(q - KW // 2, 0, GRID_W - KW)
        col_in = (kc >= start) & (kc < start + KW)
        diff = kc - q + (KW - 1)
        neg = jnp.full((GRID_W, GRID_W), NEG_INF, F32)
        band = []
        for dr in range(n_dr):
            acc = neg
            for d in range(n_dc):
                acc = jnp.where((diff == d) & col_in, rpb_ref[(h * n_dr + dr) * n_dc + d], acc)
            band.append(acc)
        for kind, pat in enumerate(pats):
            for i in range(Q_ROWS):
                for jj in range(K_ROWS):
                    o_ref[kind, 0, i * GRID_W:(i + 1) * GRID_W, jj * GRID_W:(jj + 1) * GRID_W] = (
                        band[pat[i][jj]] if pat[i][jj] >= 0 else neg)

    (bias,), got = _call(
        body, comm, bounds, (rpb.reshape(-1),), name="bias_tables", grid=(N_HEADS,),
        in_specs=[pl.BlockSpec(memory_space=pltpu.SMEM)],
        out_specs=[pl.BlockSpec((3, 1, QB, KB), lambda h: (0, h, 0, 0))],
        out_shape=[_out((3, N_HEADS, QB, KB), F32)],
        compiler_params=_cp(("arbitrary",), 32))
    return bias, got


def _attn_geometry(n_tok):
    n_rows = n_tok // GRID_W
    assert n_rows % Q_ROWS == 0 and n_rows >= K_ROWS
    return n_rows, n_rows // Q_ROWS


def _attn_probs(qh, kh, kmh, bias, scale):
    s = _dg(qh, kh, NT) * scale + bias
    sm = _dg(qh, kmh, NT) * scale
    m = jnp.maximum(jnp.max(s, axis=-1, keepdims=True), jnp.max(sm, axis=-1, keepdims=True))
    p = jnp.exp(s - m)
    pm = jnp.exp(sm - m)
    inv = 1.0 / (jnp.sum(p, axis=-1, keepdims=True) + jnp.sum(pm, axis=-1, keepdims=True))
    return p * inv, pm * inv


def _meta_probs(qmh, kmh, scale):
    s = _dg(qmh, kmh, NT) * scale
    p = jnp.exp(s - jnp.max(s, axis=-1, keepdims=True))
    return p / jnp.sum(p, axis=-1, keepdims=True)


def _step_rows(r, n_rows):
    q0 = pl.multiple_of(N_META + r * QB, 16)
    k0 = pl.multiple_of(N_META + jnp.clip(Q_ROWS * r - (K_ROWS - KH), 0, n_rows - K_ROWS) * GRID_W, 16)
    return q0, k0


def _attn_fwd(q, k, v, bias, n_tok, comm=None, bounds=()):
    tp = q.shape[0]
    n_rows, n_steps = _attn_geometry(n_tok)
    scale = HEAD_DIM ** -0.5

    def body(q_ref, k_ref, v_ref, b_ref, o_ref):
        r = pl.program_id(1)
        km = k_ref[0:N_META, :]
        vm = v_ref[0:N_META, :]

        @pl.when(r == 0)
        def _():
            qm = q_ref[0:N_META, :]
            outs = []
            for hh in range(2):
                sl = slice(hh * HEAD_DIM, (hh + 1) * HEAD_DIM)
                p = _meta_probs(qm[:, sl], km[:, sl], scale)
                outs.append(_dot(p.astype(BF16), vm[:, sl]))
            o_ref[0:N_META, :] = jnp.concatenate(outs, axis=1)
            o_ref[N_META + n_tok:, :] = jnp.zeros((tp - N_META - n_tok, 2 * HEAD_DIM), F32)

        q0, k0 = _step_rows(r, n_rows)
        qb = q_ref[pl.ds(q0, QB), :]
        kb = k_ref[pl.ds(k0, KB), :]
        vb = v_ref[pl.ds(k0, KB), :]
        outs = []
        for hh in range(2):
            sl = slice(hh * HEAD_DIM, (hh + 1) * HEAD_DIM)
            p, pm = _attn_probs(qb[:, sl], kb[:, sl], km[:, sl], b_ref[0, hh], scale)
            outs.append(_dot(p.astype(BF16), vb[:, sl]) + _dot(pm.astype(BF16), vm[:, sl]))
        o_ref[pl.ds(q0, QB), :] = jnp.concatenate(outs, axis=1)

    def bias_map(hp, r):
        return (jnp.where(r == 0, 0, jnp.where(r == n_steps - 1, 2, 1)), hp, 0, 0)

    col = pl.BlockSpec((tp, 2 * HEAD_DIM), lambda hp, r: (0, hp))
    return _call(
        body, comm, bounds, (q, k, v, bias), name="attn_fwd", grid=(N_HEADS // 2, n_steps),
        in_specs=[col, col, col, pl.BlockSpec((1, 2, QB, KB), bias_map)],
        out_specs=[col], out_shape=[_out((tp, NA_W), F32)],
        compiler_params=_cp(("arbitrary", "arbitrary"), 40))


def _attn_bwd(q, k, v, bias, do, n_tok, comm=None, bounds=()):
    tp = q.shape[0]
    n_rows, n_steps = _attn_geometry(n_tok)
    scale = HEAD_DIM ** -0.5
    pats = _na_patterns(n_rows)

    def body(q_ref, k_ref, v_ref, b_ref, do_ref, dq_ref, dk_ref, dv_ref, dtb_ref):
        r = pl.program_id(1)
        km = k_ref[0:N_META, :]
        vm = v_ref[0:N_META, :]

        @pl.when(r == 0)
        def _():
            dk_ref[...] = jnp.zeros_like(dk_ref)
            dv_ref[...] = jnp.zeros_like(dv_ref)
            dtb_ref[...] = jnp.zeros_like(dtb_ref)
            dq_ref[N_META + n_tok:, :] = jnp.zeros((tp - N_META - n_tok, 2 * HEAD_DIM), F32)
            qm = q_ref[0:N_META, :]
            dom = do_ref[0:N_META, :].astype(BF16)
            dqs, dks, dvs = [], [], []
            for hh in range(2):
                sl = slice(hh * HEAD_DIM, (hh + 1) * HEAD_DIM)
                p = _meta_probs(qm[:, sl], km[:, sl], scale)
                dp = _dg(dom[:, sl], vm[:, sl], NT)
                ds = (p * (dp - jnp.sum(dp * p, axis=-1, keepdims=True))).astype(BF16)
                dvs.append(_dg(p.astype(BF16), dom[:, sl], TN))
                dqs.append(_dot(ds, km[:, sl]) * scale)
                dks.append(_dg(ds, qm[:, sl], TN) * scale)
            dq_ref[0:N_META, :] = jnp.concatenate(dqs, axis=1)
            dk_ref[0:N_META, :] += jnp.concatenate(dks, axis=1)
            dv_ref[0:N_META, :] += jnp.concatenate(dvs, axis=1)

        q0, k0 = _step_rows(r, n_rows)
        qb = q_ref[pl.ds(q0, QB), :]
        kb = k_ref[pl.ds(k0, KB), :]
        vb = v_ref[pl.ds(k0, KB), :]
        dob = do_ref[pl.ds(q0, QB), :].astype(BF16)
        dqs, dks, dvs, dkms, dvms, dss = [], [], [], [], [], []
        for hh in range(2):
            sl = slice(hh * HEAD_DIM, (hh + 1) * HEAD_DIM)
            qh, kh, vh, kmh, vmh, doh = qb[:, sl], kb[:, sl], vb[:, sl], km[:, sl], vm[:, sl], dob[:, sl]
            p, pm = _attn_probs(qh, kh, kmh, b_ref[0, hh], scale)
            dp = _dg(doh, vh, NT)
            dpm = _dg(doh, vmh, NT)
            delta = jnp.sum(dp * p, axis=-1, keepdims=True) + jnp.sum(dpm * pm, axis=-1, keepdims=True)
            ds = p * (dp - delta)
            dsb = ds.astype(BF16)
            dsmb = (pm * (dpm - delta)).astype(BF16)
            dss.append(ds)
            dvs.append(_dg(p.astype(BF16), doh, TN))
            dvms.append(_dg(pm.astype(BF16), doh, TN))
            dqs.append((_dot(dsb, kh) + _dot(dsmb, kmh)) * scale)
            dks.append(_dg(dsb, qh, TN) * scale)
            dkms.append(_dg(dsmb, qh, TN) * scale)
        dq_ref[pl.ds(q0, QB), :] = jnp.concatenate(dqs, axis=1)
        dk_ref[pl.ds(k0, KB), :] += jnp.concatenate(dks, axis=1)
        dv_ref[pl.ds(k0, KB), :] += jnp.concatenate(dvs, axis=1)
        dk_ref[0:N_META, :] += jnp.concatenate(dkms, axis=1)
        dv_ref[0:N_META, :] += jnp.concatenate(dvms, axis=1)

        def add_bias_grad(pat):
            for hh in range(2):
                for i in range(Q_ROWS):
                    for jj in range(K_ROWS):
                        if pat[i][jj] >= 0:
                            dtb_ref[hh, pat[i][jj]] += dss[hh][i * GRID_W:(i + 1) * GRID_W,
                                                               jj * GRID_W:(jj + 1) * GRID_W]

        @pl.when(r == 0)
        def _():
            add_bias_grad(pats[0])

        @pl.when((r > 0) & (r < n_steps - 1))
        def _():
            add_bias_grad(pats[1])

        @pl.when(r == n_steps - 1)
        def _():
            add_bias_grad(pats[2])

    def bias_map(hp, r):
        return (jnp.where(r == 0, 0, jnp.where(r == n_steps - 1, 2, 1)), hp, 0, 0)

    col = pl.BlockSpec((tp, 2 * HEAD_DIM), lambda hp, r: (0, hp))
    n_dr = 2 * KH - 1
    return _call(
        body, comm, bounds, (q, k, v, bias, do), name="attn_bwd", grid=(N_HEADS // 2, n_steps),
        in_specs=[col, col, col, pl.BlockSpec((1, 2, QB, KB), bias_map), col],
        out_specs=[col, col, col, pl.BlockSpec((2, n_dr, GRID_W, GRID_W), lambda hp, r: (hp, 0, 0, 0))],
        out_shape=[_out((tp, NA_W), F32)] * 3 +
                  [_out((N_HEADS, n_dr, GRID_W, GRID_W), F32)],
        compiler_params=_cp(("arbitrary", "arbitrary"), 48))


def _repeat_onehot():
    return np.repeat(np.eye(2 * S5_G, dtype=np.float32), S5_H, axis=0)


def _s5_disc_math(lam_re, lam_im, log_dt, b_re, b_im, rep):
    dt = jnp.exp(log_dt)
    ea = jnp.exp(lam_re * dt)
    a_re = ea * jnp.cos(lam_im * dt)
    a_im = ea * jnp.sin(lam_im * dt)
    den = lam_re * lam_re + lam_im * lam_im
    c_re = ((a_re - 1.0) * lam_re + a_im * lam_im) / den
    c_im = (a_im * lam_re - (a_re - 1.0) * lam_im) / den
    ce_re = jnp.dot(rep, c_re, preferred_element_type=F32, precision=lax.Precision.HIGHEST)
    ce_im = jnp.dot(rep, c_im, preferred_element_type=F32, precision=lax.Precision.HIGHEST)
    return a_re, a_im, ce_re * b_re - ce_im * b_im, ce_re * b_im + ce_im * b_re


def _s5_blocks():
    gl = S5_G // N_BUNDLE
    half = gl * S5_P
    out = []
    for d in range(2):
        for g in range(S5_G):
            b, k = divmod(g, gl)
            dg = d * S5_G + g
            out.append((d, b, slice(k * S5_H, (k + 1) * S5_H), slice(k * S5_P, (k + 1) * S5_P),
                        slice(half + k * S5_P, half + (k + 1) * S5_P), slice(dg * S5_H, (dg + 1) * S5_H),
                        slice(dg, dg + 1)))
    return out


def _s5_params(lam_re, lam_im, log_dt, b_re, b_im, c_re, c_im):
    cw, sw = S5_W // N_BUNDLE, 2 * (S5_G // N_BUNDLE) * S5_P

    def body(lr, li, ld, br, bi, cr, ci, rep_ref, a1_ref, a2_ref, bm_ref, cm_ref):
        a_re, a_im, bb_re, bb_im = _s5_disc_math(lr[...], li[...], ld[...], br[...], bi[...], rep_ref[...])
        cc_re = cr[...]
        cc_im = ci[...]
        bm_ref[...] = jnp.zeros_like(bm_ref)
        cm_ref[...] = jnp.zeros_like(cm_ref)
        for d, b, rows, re, im, nat, one in _s5_blocks():
            bm_ref[d, b, rows, re] = bb_re[nat, :].astype(BF16)
            bm_ref[d, b, rows, im] = bb_im[nat, :].astype(BF16)
            cm_ref[d, b, rows, re] = cc_re[nat, :].astype(BF16)
            cm_ref[d, b, rows, im] = (-cc_im[nat, :]).astype(BF16)
            k = rows.start // S5_H
            lanes = slice((k % 2) * S5_P, (k % 2 + 1) * S5_P)
            for part, (v1, v2) in enumerate(((a_re[one, :], a_im[one, :]), (a_re[one, :], -a_im[one, :]))):
                sub = slice(4 * part + k // 2, 4 * part + k // 2 + 1)
                a1_ref[d, b, sub, lanes] = v1
                a2_ref[d, b, sub, lanes] = v2

    args = (lam_re, lam_im, log_dt, b_re, b_im, c_re, c_im, jnp.asarray(_repeat_onehot()))
    outs = [((2, N_BUNDLE, 8, 128), F32)] * 2 + [((2, N_BUNDLE, cw, sw), BF16)] * 2
    return pl.pallas_call(
        body, name="s5_params", grid=(1,), in_specs=[_full(a.shape) for a in args],
        out_specs=[_full(s) for s, _ in outs], out_shape=[_out(s, dt) for s, dt in outs],
    )(*_in_hbm(*args))


def _s5_params_bwd(lam_re, lam_im, log_dt, b_re, b_im, da, dbm, dcm):
    n, nb = 2 * S5_G, 2 * S5_G * S5_H

    def body(lr, li, ld, br, bi, rep_ref, da_ref, dbm_ref, dcm_ref, o_lr, o_li, o_ld, o_br, o_bi, o_cr, o_ci,
             dar_s, dai_s, dbr_s, dbi_s):
        for d, b, rows, re, im, nat, one in _s5_blocks():
            dbr_s[nat, :] = dbm_ref[d, b, rows, re]
            dbi_s[nat, :] = dbm_ref[d, b, rows, im]
            o_cr[nat, :] = dcm_ref[d, b, rows, re]
            o_ci[nat, :] = -dcm_ref[d, b, rows, im]
            dar_s[one, :] = da_ref[d, b, :, re]
            dai_s[one, :] = da_ref[d, b, :, im]
        rep = rep_ref[...]
        _, vjp = jax.vjp(lambda p, q, r, s, t: _s5_disc_math(p, q, r, s, t, rep),
                         lr[...], li[...], ld[...], br[...], bi[...])
        o_lr[...], o_li[...], o_ld[...], o_br[...], o_bi[...] = vjp((dar_s[...], dai_s[...], dbr_s[...], dbi_s[...]))

    args = (lam_re, lam_im, log_dt, b_re, b_im, jnp.asarray(_repeat_onehot()), da, dbm, dcm)
    outs = [(n, S5_P)] * 2 + [(n, 1)] + [(nb, S5_P)] * 4
    return pl.pallas_call(
        body, name="s5_params_bwd", grid=(1,), in_specs=[_full(a.shape) for a in args],
        out_specs=[_full(s) for s in outs], out_shape=[_out(s, F32) for s in outs],
        scratch_shapes=[pltpu.VMEM((n, S5_P), F32)] * 2 + [pltpu.VMEM((nb, S5_P), F32)] * 2,
    )(*_in_hbm(*args))


def _tiles_store(ref, base, val):
    for i in range(val.shape[0] // 8):
        for c in range(8):
            ref[pl.ds(base + (8 * i + c) * 8, 8), :] = val[8 * i:8 * i + 8, 128 * c:128 * (c + 1)]


def _tiles_load(ref, base, n):
    return jnp.concatenate(
        [jnp.concatenate([ref[pl.ds(base + (8 * i + c) * 8, 8), :] for c in range(8)], axis=1) for i in range(n // 8)],
        axis=0)


def _time_rows(base, t):
    return pl.ds(base + (t // 8) * 64 + t % 8, 8, stride=8)


def _scan(chains, n):
    xs = [c["x"] for c in chains]
    for k in range(n):
        for ci, c in enumerate(chains):
            t = n - 1 - k if c["reverse"] else k
            if c["prev"] is not None:
                c["prev"][_time_rows(c["prev_base"], t), :] = xs[ci]
            xs[ci] = c["a1"] * xs[ci] + pltpu.roll(c["a2"] * xs[ci], 4, axis=0) + c["src"][_time_rows(0, t), :]
            if c["dst"] is not None:
                c["dst"][_time_rows(0, t), :] = xs[ci]
    return xs


def _chain(x, a1, a2, src, dst=None, prev=None, prev_base=0, reverse=False):
    return dict(x=x, a1=a1, a2=a2, src=src, dst=dst, prev=prev, prev_base=prev_base, reverse=reverse)


def _s5_fwd(u, d_skip, a1, a2, bm, cm, length, comm=None, bounds=()):
    tp = u.shape[0]
    cw = S5_W // N_BUNDLE
    sw = bm.shape[-1]
    n_full, n_tail = divmod(length, SCAN_CHUNK)
    t_tail = n_full * SCAN_CHUNK

    nbs = N_BUNDLE

    def body(u_ref, d_ref, a1_ref, a2_ref, bm_ref, cm_ref, y_ref, bnd_ref, *scratch):
        y_ref[...] = u_ref[...] * d_ref[...]
        ins, xss = (scratch[0:nbs], scratch[nbs:2 * nbs]), (scratch[2 * nbs:3 * nbs], scratch[3 * nbs:])
        cols = [slice(b * cw, (b + 1) * cw) for b in range(nbs)]

        def keep(dr, chunk, xs):
            for b in range(nbs):
                bnd_ref[dr, b, chunk] = xs[b]

        def load(dr, t0, n):
            for b in range(nbs):
                _tiles_store(ins[dr][b], 0, _dot(u_ref[pl.ds(t0, n), cols[b]].astype(BF16), bm_ref[dr, b]))

        def chains(dr, xs):
            return [_chain(xs[b], a1_ref[dr, b], a2_ref[dr, b], ins[dr][b], dst=xss[dr][b], reverse=dr == 1)
                    for b in range(nbs)]

        def emit(dr, t0, n):
            for b in range(nbs):
                y_ref[pl.ds(t0, n), cols[b]] += _dg(_tiles_load(xss[dr][b], 0, n).astype(BF16), cm_ref[dr, b], NT)

        zero = (jnp.zeros((8, 128), F32),) * nbs
        xb = zero
        if n_tail:
            keep(1, n_full, xb)
            load(1, t_tail, n_tail)
            xb = tuple(_scan(chains(1, xb), n_tail))
            emit(1, t_tail, n_tail)

        def pair(i, carry):
            j = n_full - 1 - i
            t0s = (pl.multiple_of(i * SCAN_CHUNK, SCAN_CHUNK), pl.multiple_of(j * SCAN_CHUNK, SCAN_CHUNK))
            keep(0, i, carry[0])
            keep(1, j, carry[1])
            for dr in range(2):
                load(dr, t0s[dr], SCAN_CHUNK)
            out = _scan(chains(0, carry[0]) + chains(1, carry[1]), SCAN_CHUNK)
            for dr in range(2):
                emit(dr, t0s[dr], SCAN_CHUNK)
            return tuple(out[:nbs]), tuple(out[nbs:])

        xf, _ = lax.fori_loop(0, n_full, pair, (zero, xb))
        if n_tail:
            keep(0, n_full, xf)
            load(0, t_tail, n_tail)
            _scan(chains(0, xf), n_tail)
            emit(0, t_tail, n_tail)

    n_chunks = n_full + (1 if n_tail else 0)
    tile = pl.BlockSpec((2, nbs, 8, 128), lambda b: (0, b, 0, 0))
    return _call(
        body, comm, bounds, (u, d_skip, a1, a2, bm, cm), name="s5_fwd", grid=(N_BUNDLE // nbs,),
        in_specs=[pl.BlockSpec((tp, nbs * cw), lambda b: (0, b)), pl.BlockSpec((1, nbs * cw), lambda b: (0, b)),
                  tile, tile, pl.BlockSpec((2, nbs, cw, sw), lambda b: (0, b, 0, 0)),
                  pl.BlockSpec((2, nbs, cw, sw), lambda b: (0, b, 0, 0))],
        out_specs=[pl.BlockSpec((tp, nbs * cw), lambda b: (0, b)),
                   pl.BlockSpec((2, nbs, n_chunks, 8, 128), lambda b: (0, b, 0, 0, 0))],
        out_shape=[_out((tp, S5_W), F32), _out((2, N_BUNDLE, n_chunks, 8, 128), F32)],
        scratch_shapes=[pltpu.VMEM((SCAN_CHUNK * 8, 128), F32)] * (4 * nbs),
        compiler_params=_cp(("arbitrary",), 48))


def _s5_bwd(u, dy, d_skip, a1, a2, bm, cm, bnd, length):
    tp = u.shape[0]
    cw = S5_W // N_BUNDLE
    sw = bm.shape[-1]
    half = sw // 2
    n_full, n_tail = divmod(length, SCAN_CHUNK)
    t_tail = n_full * SCAN_CHUNK
    n_chunks = bnd.shape[2]
    nbs = 2

    def body(u_ref, dy_ref, d_ref, a1_ref, a2_ref, bm_ref, cm_ref, bnd_ref, du_ref, dd_ref, dbm_ref, dcm_ref,
             da_ref, *scratch):
        du_ref[...] = dy_ref[...] * d_ref[...]
        dd_ref[...] = jnp.sum(dy_ref[...] * u_ref[...], axis=0, keepdims=True)
        dbm_ref[...] = jnp.zeros_like(dbm_ref)
        dcm_ref[...] = jnp.zeros_like(dcm_ref)
        da_ref[...] = jnp.zeros_like(da_ref)
        bu_s, dx_s, g_s, xp_s, x_s = ([scratch[(k * 2 + dr) * nbs:(k * 2 + dr + 1) * nbs] for dr in range(2)]
                                      for k in range(5))
        cols = [slice(b * cw, (b + 1) * cw) for b in range(nbs)]

        def chains(dr, chunk, t0, n, gs):
            out = []
            for b in range(nbs):
                _tiles_store(bu_s[dr][b], 0, _dot(u_ref[pl.ds(t0, n), cols[b]].astype(BF16), bm_ref[dr, b]))
                _tiles_store(dx_s[dr][b], 0, _dot(dy_ref[pl.ds(t0, n), cols[b]].astype(BF16), cm_ref[dr, b]))
                out.append(_chain(bnd_ref[dr, b, chunk], a1_ref[dr, b], a2_ref[dr, b], bu_s[dr][b],
                                  dst=x_s[dr][b], prev=xp_s[dr][b], reverse=dr == 1))
                out.append(_chain(gs[b], a1_ref[dr, b], -a2_ref[dr, b], dx_s[dr][b], dst=g_s[dr][b], reverse=dr == 0))
            return out

        def emit(dr, t0, n):
            rows = pl.ds(t0, n)
            for b in range(nbs):
                ub = u_ref[rows, cols[b]].astype(BF16)
                dyb = dy_ref[rows, cols[b]].astype(BF16)
                g = _tiles_load(g_s[dr][b], 0, n)
                gb = g.astype(BF16)
                du_ref[rows, cols[b]] += _dg(gb, bm_ref[dr, b], NT)
                dbm_ref[dr, b] += _dg(ub, gb, TN)
                xp = _tiles_load(xp_s[dr][b], 0, n)
                xp_r, xp_i = xp[:, 0:half], xp[:, half:]
                g_r, g_i = g[:, 0:half], g[:, half:]
                dcm_ref[dr, b] += _dg(dyb, _tiles_load(x_s[dr][b], 0, n).astype(BF16), TN)
                da_ref[dr, b] += jnp.concatenate([jnp.sum(g_r * xp_r + g_i * xp_i, axis=0, keepdims=True),
                                                  jnp.sum(g_i * xp_r - g_r * xp_i, axis=0, keepdims=True)], axis=1)

        def adjoints(out):
            return tuple(out[1::2])

        zero = (jnp.zeros((8, 128), F32),) * nbs
        g0 = zero
        if n_tail:
            g0 = adjoints(_scan(chains(0, n_full, t_tail, n_tail, g0), n_tail))
            emit(0, t_tail, n_tail)

        def pair(i, carry):
            j = n_full - 1 - i
            t0 = (pl.multiple_of(j * SCAN_CHUNK, SCAN_CHUNK), pl.multiple_of(i * SCAN_CHUNK, SCAN_CHUNK))
            both = chains(0, j, t0[0], SCAN_CHUNK, carry[0]) + chains(1, i, t0[1], SCAN_CHUNK, carry[1])
            out = _scan(both, SCAN_CHUNK)
            emit(0, t0[0], SCAN_CHUNK)
            emit(1, t0[1], SCAN_CHUNK)
            return adjoints(out[:2 * nbs]), adjoints(out[2 * nbs:])

        _, g1 = lax.fori_loop(0, n_full, pair, (g0, zero))
        if n_tail:
            _scan(chains(1, n_full, t_tail, n_tail, g1), n_tail)
            emit(1, t_tail, n_tail)

    tile = pl.BlockSpec((2, nbs, 8, 128), lambda b: (0, b, 0, 0))
    wide = pl.BlockSpec((2, nbs, cw, sw), lambda b: (0, b, 0, 0))
    col = pl.BlockSpec((tp, nbs * cw), lambda b: (0, b))
    row = pl.BlockSpec((1, nbs * cw), lambda b: (0, b))
    arow = pl.BlockSpec((2, nbs, 1, sw), lambda b: (0, b, 0, 0))
    return pl.pallas_call(
        body, name="s5_bwd", grid=(N_BUNDLE // nbs,),
        in_specs=[col, col, row, tile, tile, wide, wide,
                  pl.BlockSpec((2, nbs, n_chunks, 8, 128), lambda b: (0, b, 0, 0, 0))],
        out_specs=[col, row, wide, wide, arow],
        out_shape=[_out((tp, S5_W), F32), _out((1, S5_W), F32),
                   _out((2, N_BUNDLE, cw, sw), F32), _out((2, N_BUNDLE, cw, sw), F32),
                   _out((2, N_BUNDLE, 1, sw), F32)],
        scratch_shapes=[pltpu.VMEM((SCAN_CHUNK * 8, 128), F32)] * (10 * nbs),
        compiler_params=_cp(("arbitrary",), 56),
    )(*_in_hbm(u, dy, d_skip, a1, a2, bm, cm, bnd))


def _row_tile(tp):
    return max(tm for tm in range(16, 449, 16) if tp % tm == 0)


def _step(x, target, bufs, gains, s5, rpb, c_arr, kc_arr, me_arr):
    n_tok = x.shape[0]
    first = ["ffn1_w_gate", "ffn1_w_up", "ffn1_w_down", "meta_tokens"]
    bias, got = _bias_tables(rpb, n_tok // GRID_W, _gather_comm([bufs[n] for n in first]), (0, N_HEADS - 1))
    w = dict(zip(first, got))
    meta = w["meta_tokens"].transpose(1, 0, 2).reshape(N_META, D)
    length = N_META + n_tok
    tp = length + 16
    tm = _row_tile(tp)
    tmb = tm
    n_rows = n_tok // GRID_W
    pad = jnp.zeros((tp - length, D), F32)
    h0 = jnp.concatenate([meta, x, pad], axis=0)
    tgt = jnp.concatenate([jnp.zeros((N_META, D), F32), target, pad], axis=0)

    lam_re, _ = lax.optimization_barrier((s5["lam_re"], bias))
    s5p = (lam_re, s5["lam_im"], s5["log_dt"].reshape(2 * S5_G, 1), s5["b_re"], s5["b_im"])
    a1_m, a2_m, bm16, cm16 = _s5_params(*s5p, s5["c_re"], s5["c_im"])

    mid = ["w_in", "s5_w_glu", "w_out"]
    (h1, gate1, up1, f1), got = _ffn_fwd(
        "ffn1_fwd", h0, gains["ffn1_pre_g"], gains["ffn1_post_g"], w["ffn1_w_gate"], w["ffn1_w_up"], w["ffn1_w_down"],
        tm, _gather_comm([bufs[n] for n in mid]), (0, (tp // tm) * N_CHIP * 3 // 5))
    w.update(zip(mid, got))
    q, k, v, u = _mix_in(h1, gains["mix_pre_g"], w["w_in"], tm)
    (o_na,), (gate_ici, up_ici) = _attn_fwd(
        q, k, v, bias, n_tok, _gather_comm([bufs["ffn2_w_gate"], bufs["ffn2_w_up"]], pair=False), (0,))
    (y_pre, s5_bnd), (w["ffn2_w_gate"], w["ffn2_w_up"], down_ici) = _s5_fwd(
        u, gains["s5_d"], a1_m, a2_m, bm16, cm16, length,
        _merge_comm(_gather_comm([gate_ici, up_ici], ici=False),
                    _gather_comm([bufs["ffn2_w_down"]], pair=False)), (0,))
    w_glu = w["s5_w_glu"].reshape(S5_W, S5_W)
    w_out = w["w_out"].reshape(D, D)
    (h2, mix), (w["ffn2_w_down"],) = _mix_out(
        o_na, y_pre, h1, w_glu, gains["s5_b_glu"], gains["na_out_g"], gains["s5_out_g"], w_out, gains["mix_post_g"], tm,
        _gather_comm([down_ici], ici=False), (0,))
    (h3, gate2, up2, f2), _ = _ffn_fwd("ffn2_fwd", h2, gains["ffn2_pre_g"], gains["ffn2_post_g"],
                                       w["ffn2_w_gate"], w["ffn2_w_up"], w["ffn2_w_down"], tm)
    dh3, df2, loss, dg_final, dg_post2 = _final_loss(h3, gains["final_g"], tgt, f2, gains["ffn2_post_g"], n_tok, tm)

    ffn2 = ["ffn2_w_gate", "ffn2_w_up", "ffn2_w_down"]
    ffn1 = ["ffn1_w_gate", "ffn1_w_up", "ffn1_w_down"]
    out2, _ = _ffn_bwd("ffn2_bwd", h2, gains["ffn2_pre_g"], df2, gate2, up2,
                       w["ffn2_w_gate"], w["ffn2_w_up"], w["ffn2_w_down"], tmb)
    dxn2 = out2[3]
    sums2 = [_chip_sum("chip_sum_" + n, g, r, c_arr) for n, g, r in zip(ffn2, out2[0:3], out2[4:7])]
    (dh2, dg_pre2), _ = _ffn_pre_bwd("ffn2_pre_bwd", dh3, dxn2, h2, gains["ffn2_pre_g"], tm)
    do_na, dy_pre, dw_out, dw_glu, dg_mpost, dg_na, dg_s5, db_glu = _mix_out_bwd(
        dh2, mix, o_na, y_pre, w_glu, gains["s5_b_glu"], gains["na_out_g"], gains["s5_out_g"], w_out,
        gains["mix_post_g"], tm)
    (dq, dk, dv, dtb), recv3 = _attn_bwd(q, k, v, bias, do_na, n_tok, _scatter_comm(sums2), (0,))
    totals2 = [_total_sum("total_sum_" + n, s, r, kc_arr) for n, s, r in zip(ffn2, sums2, recv3)]
    du, dd, dbm, dcm, da_m = _s5_bwd(u, dy_pre, gains["s5_d"], a1_m, a2_m, bm16, cm16, s5_bnd, length)
    (dh1, df1, dw_in, dg_mpre, dg_post1), done2 = _mix_in_bwd(
        dq, dk, dv, du, h1, gains["mix_pre_g"], w["w_in"], dh2, f1, gains["ffn1_post_g"], tm,
        _assemble_comm(totals2), (0,))
    pieces = dict(zip(ffn2, done2))

    e, _ = _diag_onehot()
    n_dr = 2 * KH - 1
    drpb = _rpb_collapse(dtb.reshape(N_HEADS * n_dr, GRID_W * GRID_W), jnp.asarray(e.T))
    drpb = drpb[:, :2 * KW - 1].reshape(N_HEADS, n_dr, 2 * KW - 1).transpose(1, 0, 2).reshape(N_HEADS * n_dr, 2 * KW - 1)
    dlam_re, dlam_im, dlog_dt, db_re, db_im, dc_re, dc_im = _s5_params_bwd(*s5p, da_m, dbm, dcm)
    early = {"ffn1_post_g": dg_post1, "mix_pre_g": dg_mpre, "na_rpb": drpb,
             "s5_lam_re": dlam_re, "s5_lam_im": dlam_im, "s5_log_dt": dlog_dt.reshape(2, S5_G),
             "s5_b_re": db_re, "s5_b_im": db_im, "s5_c_re": dc_re, "s5_c_im": dc_im,
             "s5_d": dd, "s5_b_glu": db_glu, "na_out_g": dg_na,
             "s5_out_g": dg_s5, "mix_post_g": dg_mpost, "ffn2_pre_g": dg_pre2, "ffn2_post_g": dg_post2,
             "final_g": dg_final}
    names = list(early)
    slots = _small_pack([early[n] for n in names], me_arr)

    out1, slots = _ffn_bwd("ffn1_bwd", h0, gains["ffn1_pre_g"], df1, gate1, up1,
                           w["ffn1_w_gate"], w["ffn1_w_up"], w["ffn1_w_down"], tmb, _spread_comm(slots), (0,))
    small = dict(zip(names, _small_total(slots, [early[n].shape for n in names])))
    sums1 = [_chip_sum("chip_sum_" + n, g, r, c_arr) for n, g, r in zip(ffn1, out1[0:3], out1[4:7])]
    flight1 = _scatter_start("ffn1", sums1)
    token = flight1[4]
    rest = [dw_in, dw_glu.reshape(N_CHIP, S5_W // N_CHIP, S5_W), dw_out.reshape(N_CHIP, D // N_CHIP, D)]
    (dh0, dg_pre1), recv_rest = _ffn_pre_bwd("ffn1_pre_bwd", dh1, out1[3], h0, gains["ffn1_pre_g"] + token[0:1, 0:1],
                                             tm, _exchange_comm(rest), (0,))
    sums = [_chip_sum("chip_sum_" + n, g, r, c_arr) for n, g, r in zip(mid, rest, recv_rest)]
    flight2 = _scatter_start("rest", sums)
    return loss[0, 0], dh0, pieces, small, {"ffn1_pre_g": dg_pre1}, (ffn1, flight1[:4]), (mid, flight2[:4])


def _mesh_pos():
    return lax.axis_index("x"), lax.axis_index("y"), lax.axis_index("c")


def _other_chips(x, y):
    return [(1 - x, y), (x, 1 - y), (1 - x, 1 - y)]


class _Comm:
    def __init__(self, ins, out_shape, aliases, parts):
        self.ins, self.out_shape, self.aliases, self.parts = list(ins), list(out_shape), dict(aliases), list(parts)
        self.n_sems = sum(p[0] for p in parts)

    def bases(self):
        out, base = [], 0
        for n_sems, _, _ in self.parts:
            out.append(base)
            base += n_sems
        return out


def _run_comm(name, comm):
    n_i, n_o = len(comm.ins), len(comm.out_shape)

    def body(*refs):
        ins, outs = refs[:n_i], refs[n_i:n_i + n_o]
        send_sems, recv_sems = refs[n_i + n_o:]
        for base, (_, start, finish) in zip(comm.bases(), comm.parts):
            start(ins, outs, send_sems, recv_sems, base)
            finish(ins, outs, send_sems, recv_sems, base)

    return pl.pallas_call(
        body, name=name, out_shape=comm.out_shape, in_specs=[ANY] * n_i, out_specs=[ANY] * n_o,
        input_output_aliases=comm.aliases,
        scratch_shapes=[pltpu.SemaphoreType.DMA((comm.n_sems,)), pltpu.SemaphoreType.DMA((comm.n_sems,))],
    )(*_in_hbm(*comm.ins))


def _call(body, comm, bounds, args, *, name, grid, in_specs, out_specs, out_shape, scratch_shapes=(),
          compiler_params=None):
    in_specs, out_specs, out_shape, scratch_shapes = list(in_specs), list(out_specs), list(out_shape), list(scratch_shapes)
    if comm is None:
        return pl.pallas_call(body, name=name, grid=grid, in_specs=in_specs, out_specs=out_specs, out_shape=out_shape,
                              scratch_shapes=scratch_shapes, compiler_params=compiler_params)(*_in_hbm(*args)), []
    n_in, n_out, n_scr = len(in_specs), len(out_specs), len(scratch_shapes)
    n_ci, n_co = len(comm.ins), len(comm.out_shape)
    n_steps = int(np.prod(grid))
    assert len(bounds) == len(comm.parts) and all(0 <= b < n_steps for b in bounds) and list(bounds) == sorted(bounds)

    def fused(*refs):
        a = n_in
        b = a + n_ci
        c = b + n_out
        d = c + n_co
        e = d + n_scr
        cargs = (refs[a:b], refs[c:d], refs[e], refs[e + 1])
        step = pl.program_id(0)
        for ax in range(1, len(grid)):
            step = step * grid[ax] + pl.program_id(ax)
        bases = comm.bases()
        for p, (_, start, finish) in enumerate(comm.parts):
            @pl.when(step == bounds[p])
            def _(p=p, start=start):
                if p > 0:
                    comm.parts[p - 1][2](*cargs, bases[p - 1])
                start(*cargs, bases[p])
        body(*(refs[:a] + refs[b:c] + refs[d:e]))

        @pl.when(step == n_steps - 1)
        def _():
            comm.parts[-1][2](*cargs, bases[-1])

    res = pl.pallas_call(
        fused, name=name, grid=grid, in_specs=in_specs + [ANY] * n_ci, out_specs=out_specs + [ANY] * n_co,
        out_shape=out_shape + comm.out_shape,
        scratch_shapes=scratch_shapes + [pltpu.SemaphoreType.DMA((comm.n_sems,)), pltpu.SemaphoreType.DMA((comm.n_sems,))],
        input_output_aliases={n_in + i: n_out + j for i, j in comm.aliases.items()},
        compiler_params=compiler_params)(*_in_hbm(*args, *comm.ins))
    return res[:n_out], res[n_out:]


def _remote(src, dst, send_sems, recv_sems, idx, to):
    return pltpu.make_async_remote_copy(src_ref=src, dst_ref=dst, send_sem=send_sems.at[idx],
                                        recv_sem=recv_sems.at[idx], device_id=to, device_id_type=MESH_ID)


def _gather_comm(bufs, ici=True, pair=True):
    n = len(bufs)

    def half(ref, k, pc):
        rh = ref.shape[1] // 2
        return ref.at[k, pl.ds(pc * rh, rh), :]

    def ici_start(ins, outs, ss, rs, base):
        x, y, c = _mesh_pos()
        for a in range(n):
            mine = half(outs[a], 2 * x + y, c)
            for j, chip in enumerate(_other_chips(x, y)):
                _remote(mine, mine, ss, rs, base + 3 * a + j, (*chip, c)).start()

    def ici_finish(ins, outs, ss, rs, base):
        x, y, c = _mesh_pos()
        for a in range(n):
            for j, chip in enumerate(_other_chips(x, y)):
                theirs = half(outs[a], 2 * chip[0] + chip[1], c)
                _remote(theirs, theirs, ss, rs, base + 3 * a + j, (*chip, c)).wait()

    def pair_copy(outs, ss, rs, base, a):
        x, y, c = _mesh_pos()
        rh = outs[a].shape[1] // 2
        held = outs[a].at[:, pl.ds(c * rh, rh), :]
        return _remote(held, held, ss, rs, base + a, (x, y, 1 - c))

    def pair_start(ins, outs, ss, rs, base):
        for a in range(n):
            pair_copy(outs, ss, rs, base, a).start()

    def pair_finish(ins, outs, ss, rs, base):
        for a in range(n):
            pair_copy(outs, ss, rs, base, a).wait()

    parts = ([(3 * n, ici_start, ici_finish)] if ici else []) + ([(n, pair_start, pair_finish)] if pair else [])
    return _Comm(bufs, [_out(b.shape, b.dtype) for b in bufs], {a: a for a in range(n)}, parts)


def _merge_comm(*comms):
    ins, shapes, aliases, subs, base = [], [], {}, [], 0
    for cm in comms:
        (n_sems, start, finish), = cm.parts
        i0, o0 = len(ins), len(shapes)
        subs.append((slice(i0, i0 + len(cm.ins)), slice(o0, o0 + len(cm.out_shape)), base, start, finish))
        aliases.update({i0 + i: o0 + j for i, j in cm.aliases.items()})
        ins += cm.ins
        shapes += cm.out_shape
        base += n_sems

    def start_all(ins_r, outs_r, ss, rs, b):
        for si, so, off, start, _ in subs:
            start(ins_r[si], outs_r[so], ss, rs, b + off)

    def finish_all(ins_r, outs_r, ss, rs, b):
        for si, so, off, _, finish in subs:
            finish(ins_r[si], outs_r[so], ss, rs, b + off)

    return _Comm(ins, shapes, aliases, [(base, start_all, finish_all)])


def _own_half_buffers(pieces, dtypes, kc_arr):
    n = len(pieces)

    def body(kc_ref, *refs):
        for a in range(n):
            refs[n + a][0] = refs[a][...].astype(dtypes[a])

    def half(p):
        return p.shape[0] // 2, p.shape[1]

    return pl.pallas_call(
        body, name="own_halves",
        out_shape=[_out((N_CHIP,) + p.shape, dt) for p, dt in zip(pieces, dtypes)],
        grid_spec=pltpu.PrefetchScalarGridSpec(
            num_scalar_prefetch=1, grid=(1,),
            in_specs=[pl.BlockSpec(half(p), lambda i, kc: (kc[1], 0)) for p in pieces],
            out_specs=[pl.BlockSpec((1,) + half(p), lambda i, kc: (kc[0], kc[1], 0)) for p in pieces]),
        compiler_params=_cp(("arbitrary",), 48),
    )(kc_arr, *_in_hbm(*pieces))


def _exchange_comm(grads):
    n = len(grads)

    def copy(ins, outs, ss, rs, base, a):
        x, y, c = _mesh_pos()
        rh = ins[a].shape[1] // 2
        return _remote(ins[a].at[:, pl.ds((1 - c) * rh, rh), :], outs[a], ss, rs, base + a, (x, y, 1 - c))

    def start(ins, outs, ss, rs, base):
        for a in range(n):
            copy(ins, outs, ss, rs, base, a).start()

    def finish(ins, outs, ss, rs, base):
        for a in range(n):
            copy(ins, outs, ss, rs, base, a).wait()

    shapes = [_out((N_CHIP, g.shape[1] // 2, g.shape[2]), g.dtype) for g in grads]
    return _Comm(grads, shapes, {}, [(n, start, finish)])


def _chip_sum(name, g, recv, c_arr):
    _, r, cc = g.shape
    rh = r // 2

    def body(c_ref, g_ref, r_ref, o_ref):
        o_ref[...] = (g_ref[...] + r_ref[...]).astype(BF16)

    return pl.pallas_call(
        body, name=name, out_shape=_out((N_CHIP, rh, cc), BF16),
        grid_spec=pltpu.PrefetchScalarGridSpec(
            num_scalar_prefetch=1, grid=(N_CHIP,),
            in_specs=[pl.BlockSpec((1, rh, cc), lambda j, c_ref: (j, c_ref[0], 0)),
                      pl.BlockSpec((1, rh, cc), lambda j, c_ref: (j, 0, 0))],
            out_specs=pl.BlockSpec((1, rh, cc), lambda j, c_ref: (j, 0, 0))),
        compiler_params=_cp(("arbitrary",), 32),
    )(c_arr, *_in_hbm(g, recv))


def _scatter_comm(sums):
    n = len(sums)

    def copies(ins, outs, ss, rs, base):
        x, y, c = _mesh_pos()
        return [_remote(ins[a].at[2 * chip[0] + chip[1]], outs[a].at[j], ss, rs, base + 3 * a + j, (*chip, c))
                for a in range(n) for j, chip in enumerate(_other_chips(x, y))]

    def start(ins, outs, ss, rs, base):
        for cp in copies(ins, outs, ss, rs, base):
            cp.start()

    def finish(ins, outs, ss, rs, base):
        for cp in copies(ins, outs, ss, rs, base):
            cp.wait()

    shapes = [_out((3,) + s.shape[1:], s.dtype) for s in sums]
    return _Comm(sums, shapes, {}, [(3 * n, start, finish)])


def _scatter_copies(ins, lands, send_sems, recv_sems):
    x, y, c = _mesh_pos()
    return [_remote(ins[a].at[2 * chip[0] + chip[1]], lands[a].at[j], send_sems, recv_sems, 3 * a + j, (*chip, c))
            for a in range(len(ins)) for j, chip in enumerate(_other_chips(x, y))]


def _scatter_start(name, sums):
    n = len(sums)
    lands = [lax.empty((3,) + s.shape[1:], s.dtype) for s in sums]
    hbm = pl.BlockSpec(memory_space=pltpu.HBM)
    sem = pl.BlockSpec(memory_space=pltpu.SEMAPHORE)

    def body(*refs):
        ins, land_refs = refs[:n], refs[n:2 * n]
        send_sems, recv_sems = refs[2 * n], refs[2 * n + 1]
        token = refs[-1]
        for cp in _scatter_copies(ins, land_refs, send_sems, recv_sems):
            cp.start()
        token[...] = jnp.zeros_like(token)

    res = pl.pallas_call(
        body, name=name + "_scatter_start",
        out_shape=(pltpu.SemaphoreType.DMA((3 * n,)), pltpu.SemaphoreType.DMA((3 * n,)),
                   *[pltpu.HBM(s.shape, s.dtype) for s in sums], *[pltpu.HBM(ld.shape, ld.dtype) for ld in lands],
                   jax.ShapeDtypeStruct((8, 128), F32)),
        in_specs=[hbm] * (2 * n), out_specs=(sem, sem, *[hbm] * (2 * n), pl.BlockSpec(memory_space=pltpu.VMEM)),
        input_output_aliases={i: 2 + i for i in range(2 * n)},
        compiler_params=pltpu.CompilerParams(has_side_effects=pltpu.SideEffectType.DATAFLOW_SIDE_EFFECTING),
    )(*[pltpu.with_memory_space_constraint(a, pltpu.HBM) for a in list(sums) + lands])
    return res[0], res[1], list(res[2:2 + n]), list(res[2 + n:2 + 2 * n]), res[-1]


def _scatter_wait(name, send_sems, recv_sems, sums, lands, after):
    n = len(sums)
    hbm = pl.BlockSpec(memory_space=pltpu.HBM)
    sem = pl.BlockSpec(memory_space=pltpu.SEMAPHORE)

    def body(*refs):
        ins, land_refs = refs[:n], refs[n:2 * n]
        for cp in _scatter_copies(ins, land_refs, refs[2 * n], refs[2 * n + 1]):
            cp.wait_send()
            cp.wait_recv()

    res = pl.pallas_call(
        body, name=name + "_scatter_wait",
        out_shape=tuple([pltpu.HBM(s.shape, s.dtype) for s in sums] + [pltpu.HBM(ld.shape, ld.dtype) for ld in lands]),
        in_specs=[hbm] * (2 * n) + [sem, sem, pl.BlockSpec(memory_space=pl.ANY)], out_specs=tuple([hbm] * (2 * n)),
        input_output_aliases={i: i for i in range(2 * n)},
        compiler_params=pltpu.CompilerParams(has_side_effects=pltpu.SideEffectType.DATAFLOW_SIDE_EFFECTING),
    )(*sums, *lands, send_sems, recv_sems, after)
    return list(res[:n]), list(res[n:])


def _total_sum(name, sums, recv3, kc_arr):
    _, rh, cc = sums.shape

    def body(kc_ref, s_ref, r_ref, o_ref):
        t = s_ref[0].astype(F32) + r_ref[0].astype(F32)
        t = t + r_ref[1].astype(F32)
        o_ref[...] = t + r_ref[2].astype(F32)

    return pl.pallas_call(
        body, name=name, out_shape=_out((2 * rh, cc), F32),
        grid_spec=pltpu.PrefetchScalarGridSpec(
            num_scalar_prefetch=1, grid=(1,),
            in_specs=[pl.BlockSpec((1, rh, cc), lambda i, kc_ref: (kc_ref[0], 0, 0)),
                      pl.BlockSpec((3, rh, cc), lambda i, kc_ref: (0, 0, 0))],
            out_specs=pl.BlockSpec((rh, cc), lambda i, kc_ref: (kc_ref[1], 0))),
        compiler_params=_cp(("arbitrary",), 32),
    )(kc_arr, *_in_hbm(sums, recv3))


def _assemble_comm(totals):
    n = len(totals)

    def copy(outs, ss, rs, base, a):
        x, y, c = _mesh_pos()
        rh = outs[a].shape[0] // 2
        here = outs[a].at[pl.ds(c * rh, rh), :]
        return _remote(here, here, ss, rs, base + a, (x, y, 1 - c))

    def start(ins, outs, ss, rs, base):
        for a in range(n):
            copy(outs, ss, rs, base, a).start()

    def finish(ins, outs, ss, rs, base):
        for a in range(n):
            copy(outs, ss, rs, base, a).wait()

    shapes = [_out(t.shape, t.dtype) for t in totals]
    return _Comm(totals, shapes, {a: a for a in range(n)}, [(n, start, finish)])


def _small_layout(shapes):
    n = len(shapes)
    narrow_w = 64
    wide = [a for a in range(n) if shapes[a][1] > narrow_w]
    narrow = sorted((a for a in range(n) if shapes[a][1] <= narrow_w), key=lambda a: -shapes[a][0])
    offs, cols, groups, widths, rows = {}, {}, [], [], []
    if wide:
        r = 0
        for a in wide:
            offs[a], cols[a] = r, 0
            r += shapes[a][0]
        groups.append(wide)
        widths.append(max(shapes[a][1] for a in wide))
        rows.append(-(-r // 8) * 8)
    if narrow:
        heights = [0, 0]
        for a in narrow:
            side = 0 if heights[0] <= heights[1] else 1
            offs[a], cols[a] = heights[side], side * narrow_w
            heights[side] += shapes[a][0]
        groups.append(narrow)
        widths.append(2 * narrow_w)
        rows.append(-(-max(heights) // 8) * 8)

    def window(ref, a):
        return ref.at[offs[a]:offs[a] + shapes[a][0], cols[a]:cols[a] + shapes[a][1]]

    return groups, widths, rows, window


def _small_pack(arrays, me_arr):
    shapes = [a.shape for a in arrays]
    groups, widths, rows, window = _small_layout(shapes)
    n, n_g = len(arrays), len(groups)

    def body(me_ref, *refs):
        ins, outs = refs[:n], refs[n:]
        for gi, g in enumerate(groups):
            outs[gi][...] = jnp.zeros_like(outs[gi])
            for a in g:
                window(outs[gi].at[0], a)[...] = ins[a][...]

    return pl.pallas_call(
        body, name="small_pack", out_shape=[_out((8, r, w), F32) for r, w in zip(rows, widths)],
        grid_spec=pltpu.PrefetchScalarGridSpec(
            num_scalar_prefetch=1, grid=(1,), in_specs=[pl.BlockSpec(s, lambda i, me: (0, 0)) for s in shapes],
            out_specs=[pl.BlockSpec((1, r, w), lambda i, me: (me[0], 0, 0)) for r, w in zip(rows, widths)]),
        compiler_params=_cp(("arbitrary",), 32),
    )(me_arr, *_in_hbm(*arrays))


def _spread_comm(slots):
    n = len(slots)
    flips = [(dx, dy, dc) for dx in range(2) for dy in range(2) for dc in range(2)][1:]

    def copies(outs, ss, rs, base):
        x, y, c = _mesh_pos()
        mine = 4 * x + 2 * y + c
        return [_remote(outs[a].at[mine], outs[a].at[mine], ss, rs, base + 7 * a + f,
                        (x ^ dx, y ^ dy, c ^ dc)) for a in range(n) for f, (dx, dy, dc) in enumerate(flips)]

    def start(ins, outs, ss, rs, base):
        for cp in copies(outs, ss, rs, base):
            cp.start()

    def finish(ins, outs, ss, rs, base):
        for cp in copies(outs, ss, rs, base):
            cp.wait()

    return _Comm(slots, [_out(s.shape, s.dtype) for s in slots], {a: a for a in range(n)}, [(7 * n, start, finish)])


def _small_total(slots, shapes):
    groups, widths, rows, window = _small_layout(shapes)
    n, n_g = len(shapes), len(groups)

    def body(*refs):
        ins, outs, acc = refs[:n_g], refs[n_g:n_g + n], refs[n_g + n:]
        for gi, g in enumerate(groups):
            t = ins[gi][0] + ins[gi][1]
            for d in range(2, 8):
                t = t + ins[gi][d]
            acc[gi][...] = t
            for a in g:
                outs[a][...] = window(acc[gi], a)[...]

    return pl.pallas_call(
        body, name="small_total", grid=(1,), out_shape=[_out(s, F32) for s in shapes],
        in_specs=[_full(s.shape) for s in slots], out_specs=[_full(s) for s in shapes],
        scratch_shapes=[pltpu.VMEM((r, w), F32) for r, w in zip(rows, widths)],
        compiler_params=_cp(("arbitrary",), 48),
    )(*_in_hbm(*slots))


def _small_allreduce(arrays, comm):
    n = len(arrays)
    shapes = [a.shape for a in arrays]
    groups, widths, rows, window = _small_layout(shapes)
    n_g = len(groups)

    def body(*refs):
        ins, outs = refs[:n], refs[n:2 * n]
        pack, sib, csum, every = (refs[2 * n + i * n_g:2 * n + (i + 1) * n_g] for i in range(4))
        send_sems, recv_sems = refs[2 * n + 4 * n_g:]
        x, y, c = _mesh_pos()
        k = 2 * x + y
        for gi, g in enumerate(groups):
            pack[gi][...] = jnp.zeros_like(pack[gi])
            for a in g:
                window(pack[gi], a)[...] = ins[a][...]
        cps = [_remote(pack[gi], sib[gi], send_sems, recv_sems, gi, (x, y, 1 - c)) for gi in range(n_g)]
        for cp in cps:
            cp.start()
        for cp in cps:
            cp.wait()
        for gi in range(n_g):
            csum[gi][...] = pack[gi][...] + sib[gi][...]
            every[gi][k] = csum[gi][...]
        cps = [_remote(csum[gi], every[gi].at[k], send_sems, recv_sems, n_g + 3 * gi + j, (*chip, c))
               for gi in range(n_g) for j, chip in enumerate(_other_chips(x, y))]
        for cp in cps:
            cp.start()
        for cp in cps:
            cp.wait()
        for gi, g in enumerate(groups):
            pack[gi][...] = ((every[gi][0] + every[gi][1]) + every[gi][2]) + every[gi][3]
            for a in g:
                outs[a][...] = window(pack[gi], a)[...]

    bufs = [pltpu.VMEM((r, w), F32) for r, w in zip(rows, widths)]
    return _call(
        body, comm, (0,), arrays, name="small_allreduce", grid=(1,), out_shape=[_out(s, F32) for s in shapes],
        in_specs=[_full(s) for s in shapes], out_specs=[_full(s) for s in shapes],
        scratch_shapes=bufs * 3 + [pltpu.VMEM((N_CHIP, r, w), F32) for r, w in zip(rows, widths)] +
                       [pltpu.SemaphoreType.DMA((4 * n_g,)), pltpu.SemaphoreType.DMA((4 * n_g,))],
        compiler_params=_cp(("arbitrary",), 40))


def _adamw_small(ws, gs, ms, vs, comm):
    n = len(ws)

    def body(*refs):
        w, g, m, v, d, mo, vo = (refs[i * n:(i + 1) * n] for i in range(7))
        for a in range(n):
            d[a][...], mo[a][...], vo[a][...] = _adamw_math(w[a][...], g[a][...], m[a][...], v[a][...])

    specs = [_full(w.shape) for w in ws]
    res, got = _call(
        body, comm, (0,), (*ws, *gs, *ms, *vs), name="adamw_small", grid=(1,),
        out_shape=[_out(w.shape, F32) for w in ws] * 3,
        in_specs=specs * 4, out_specs=specs * 3, compiler_params=_cp(("arbitrary",), 40))
    return (res[:n], res[n:2 * n], res[2 * n:]), got


def _adamw_math(w, g, m, v):
    m = ADAM_B1 * m + (1.0 - ADAM_B1) * g
    v = ADAM_B2 * v + (1.0 - ADAM_B2) * (g * g)
    m_hat = m / (1.0 - ADAM_B1 ** ADAM_STEP)
    v_hat = v / (1.0 - ADAM_B2 ** ADAM_STEP)
    delta = -ADAM_LR * (m_hat / (jnp.sqrt(v_hat) + ADAM_EPS) + ADAM_WD * w)
    return delta, m, v


def _adamw(name, w, g, m, v):
    r, c = w.shape
    tr = max(t for t in range(8, 513, 8) if r % t == 0)

    def body(w_ref, g_ref, m_ref, v_ref, d_ref, mo_ref, vo_ref):
        d_ref[...], mo_ref[...], vo_ref[...] = _adamw_math(w_ref[...], g_ref[...], m_ref[...], v_ref[...])

    return pl.pallas_call(
        body, name=name, grid=(r // tr,), in_specs=[_rows(tr, c)] * 4, out_specs=[_rows(tr, c)] * 3,
        out_shape=[_out((r, c), F32)] * 3, compiler_params=_cp(("arbitrary",), 32),
    )(*_in_hbm(w, g, m, v))


def _as_matrix(name, a):
    if name == "na_rpb":
        return a[0].transpose(1, 0, 2).reshape(N_HEADS * (2 * KH - 1), 2 * KW - 1)
    if name in ("s5_b_re", "s5_b_im"):
        return a.transpose(0, 1, 2, 4, 3).reshape(2 * S5_G * S5_H, S5_P)
    if name in ("s5_c_re", "s5_c_im"):
        return a.reshape(2 * S5_G * S5_H, S5_P)
    if name in ("s5_lam_re", "s5_lam_im"):
        return a.reshape(2 * S5_G, S5_P)
    if name == "s5_log_dt":
        return a.reshape(2, S5_G)
    return a


def _from_matrix(name, m):
    if name == "na_rpb":
        return m.reshape(2 * KH - 1, N_HEADS, 2 * KW - 1).transpose(1, 0, 2)[None]
    if name in ("s5_b_re", "s5_b_im"):
        return m.reshape(1, 2, S5_G, S5_H, S5_P).transpose(0, 1, 2, 4, 3)
    if name in ("s5_c_re", "s5_c_im"):
        return m.reshape(1, 2, S5_G, S5_H, S5_P)
    if name in ("s5_lam_re", "s5_lam_im"):
        return m.reshape(1, 2, S5_G, S5_P)
    if name == "s5_log_dt":
        return m.reshape(1, 2, S5_G)
    return m


WEIGHTS = ["meta_tokens", "ffn1_pre_g", "ffn1_post_g", "ffn1_w_gate", "ffn1_w_up", "ffn1_w_down", "mix_pre_g", "w_in",
           "na_rpb", "s5_lam_re", "s5_lam_im", "s5_log_dt", "s5_b_re", "s5_b_im", "s5_c_re", "s5_c_im", "s5_d",
           "s5_w_glu", "s5_b_glu", "na_out_g", "s5_out_g", "w_out", "mix_post_g", "ffn2_pre_g", "ffn2_post_g",
           "ffn2_w_gate", "ffn2_w_up", "ffn2_w_down", "final_g"]
BIG = ["ffn1_w_gate", "ffn1_w_up", "ffn1_w_down", "w_in", "s5_w_glu", "w_out", "ffn2_w_gate", "ffn2_w_up",
       "ffn2_w_down"]
TRANSPOSED = ["ffn1_w_gate", "ffn1_w_up", "ffn2_w_gate", "ffn2_w_up"]
GAINS = ["ffn1_pre_g", "ffn1_post_g", "mix_pre_g", "s5_d", "s5_b_glu", "na_out_g", "s5_out_g", "mix_post_g",
         "ffn2_pre_g", "ffn2_post_g", "final_g"]
SMALL = [n for n in WEIGHTS if n not in BIG]


def kernel(*args):
    names = ["x"] + WEIGHTS + ["loss_target"] + ["m_" + n for n in WEIGHTS] + ["v_" + n for n in WEIGHTS]
    assert len(args) == len(names)
    given = dict(zip(names, args))
    x_pos, y_pos, c_pos = _mesh_pos()
    k_pos = 2 * x_pos + y_pos
    c_arr = jnp.reshape(c_pos, (1,)).astype(jnp.int32)
    kc_arr = jnp.stack([k_pos, c_pos]).astype(jnp.int32)

    def piece(name, a):
        return a[0].T if name in TRANSPOSED else a[0]

    def unpiece(name, a):
        return a.T[None] if name in TRANSPOSED else a[None]

    placed = BIG + ["meta_tokens"]
    bufs = dict(zip(placed, _own_half_buffers([piece(n, given[n]) for n in BIG] + [given["meta_tokens"]],
                                              [BF16] * len(BIG) + [F32], kc_arr)))

    gains = {n: given[n] for n in GAINS}
    s5 = {n: _as_matrix("s5_" + n, given["s5_" + n])
          for n in ["lam_re", "lam_im", "log_dt", "b_re", "b_im", "c_re", "c_im"]}
    me_arr = jnp.reshape(4 * x_pos + 2 * y_pos + c_pos, (1,)).astype(jnp.int32)
    loss, dh0, pieces, small, late, (ffn1, flight1), (mid, flight2) = _step(
        given["x"][0], given["loss_target"][0], bufs, gains, s5, given["na_rpb"][0], c_arr, kc_arr, me_arr)
    loss = lax.psum(loss, ("x", "y", "c"))
    n_tok = given["x"].shape[1]
    grad_x = dh0[N_META:N_META + n_tok][None]

    late["meta_tokens"] = dh0[:N_META]
    out_g, out_d, out_m, out_v = {}, {}, {}, {}

    def update_big(n):
        g2 = pieces[n]
        d2, m2, v2 = _adamw("adamw_" + n, piece(n, given[n]), g2, piece(n, given["m_" + n]),
                            piece(n, given["v_" + n]))
        out_g[n], out_d[n], out_m[n], out_v[n] = (unpiece(n, t) for t in (g2, d2, m2, v2))
        return v2

    done2 = [update_big(n) for n in pieces]
    sums1, recv1 = _scatter_wait("ffn1", *flight1, done2[-1])
    totals1 = [_total_sum("total_sum_" + n, s, r, kc_arr) for n, s, r in zip(ffn1, sums1, recv1)]
    pieces.update(zip(ffn1, _run_comm("ffn1_pair_assemble", _assemble_comm(totals1))))
    done1 = [update_big(n) for n in ffn1]
    late_arrays = list(late.values())
    late_arrays[0], _ = lax.optimization_barrier((late_arrays[0], (done1[-1], small["final_g"])))
    red, _ = _small_allreduce(late_arrays, None)
    small.update(zip(late, red))
    mc = D // N_CHIP
    small["meta_tokens"] = lax.dynamic_slice_in_dim(small["meta_tokens"], k_pos * mc, mc, 1)
    sums_rest, recv_rest = _scatter_wait("rest", *flight2, red[0])
    totals = [_total_sum("total_sum_" + n, s, r, kc_arr) for n, s, r in zip(mid, sums_rest, recv_rest)]
    gs = [small[n] for n in SMALL]
    (d2, m2, v2), done = _adamw_small([_as_matrix(n, given[n]) for n in SMALL], gs,
                                      [_as_matrix(n, given["m_" + n]) for n in SMALL],
                                      [_as_matrix(n, given["v_" + n]) for n in SMALL], _assemble_comm(totals))
    pieces.update(zip(mid, done))

    for n, g, dd, mm, vv in zip(SMALL, gs, d2, m2, v2):
        out_g[n], out_d[n], out_m[n], out_v[n] = (_from_matrix(n, t) for t in (g, dd, mm, vv))
    for n in mid:
        update_big(n)
    return (loss, grad_x, *[out_g[n] for n in WEIGHTS], *[out_d[n] for n in WEIGHTS],
            *[out_m[n] for n in WEIGHTS], *[out_v[n] for n in WEIGHTS])
```

```python
import math

import numpy as np
import jax
import jax.numpy as jnp
from jax import lax
from jax.experimental import pallas as pl
from jax.experimental.pallas import tpu as pltpu

F32 = jnp.float32
BF16 = jnp.bfloat16

D = 1024
N_META = 16
GRID_W = 64
NA_W = 512
S5_W = 512
HEAD_DIM = 64
N_HEADS = 8
KH = 8
KW = 16
S5_G = 32
S5_P = 64
S5_H = 16
N_BUNDLE = 4
FF = 2816
N_CHIP = 4
FC = FF // N_CHIP
EPS = 1e-6
NEG_INF = -1e30
Q_ROWS = 4
K_ROWS = 12
QB = Q_ROWS * GRID_W
KB = K_ROWS * GRID_W
SCAN_CHUNK = 256

ADAM_LR = 0.001
ADAM_B1 = 0.9
ADAM_B2 = 0.999
ADAM_EPS = 1e-08
ADAM_WD = 0.01
ADAM_STEP = 10

NT = (((1,), (1,)), ((), ()))
TN = (((0,), (0,)), ((), ()))
MESH_ID = pl.DeviceIdType.MESH


def _cp(sem=None, vmem_mb=None):
    kw = {}
    if sem is not None:
        kw["dimension_semantics"] = sem
    if vmem_mb is not None:
        kw["vmem_limit_bytes"] = vmem_mb << 20
    return pltpu.CompilerParams(**kw)


def _full(shape):
    n = len(shape)
    return pl.BlockSpec(shape, lambda *_: (0,) * n)


def _rows(tm, w):
    return pl.BlockSpec((tm, w), lambda i: (i, 0))


ANY = pl.BlockSpec(memory_space=pl.ANY)


def _rms(x, g):
    r = lax.rsqrt(jnp.mean(x * x, axis=-1, keepdims=True) + EPS)
    return x * r * g


def _rms_bwd(x, g, dy):
    r = lax.rsqrt(jnp.mean(x * x, axis=-1, keepdims=True) + EPS)
    xh = x * r
    dg = jnp.sum(dy * xh, axis=0, keepdims=True)
    dyg = dy * g
    dx = r * (dyg - xh * jnp.mean(dyg * xh, axis=-1, keepdims=True))
    return dx, dg


def _out(shape, dtype):
    return pltpu.HBM(tuple(shape), dtype)


def _in_hbm(*args):
    return [pltpu.with_memory_space_constraint(a, pltpu.HBM) if jnp.issubdtype(a.dtype, jnp.floating) and a.ndim > 1
            else a for a in args]


def _dot(a, b):
    return jnp.dot(a, b, preferred_element_type=F32)


def _dg(a, b, dims):
    return lax.dot_general(a, b, dims, preferred_element_type=F32)


def _ffn_fwd(name, h, g_pre, g_post, wg, wu, wd, tm, comm=None, bounds=()):
    tp = h.shape[0]
    nt = tp // tm

    def body(h_ref, gp_ref, gq_ref, wg_ref, wu_ref, wd_ref, hn_ref, gate_ref, up_ref, f_ref, xn_s, acc_s):
        c = pl.program_id(1)

        @pl.when(c == 0)
        def _():
            xn_s[...] = _rms(h_ref[...], gp_ref[...]).astype(BF16)
            acc_s[...] = jnp.zeros_like(acc_s)

        xn = xn_s[...]
        gate = _dg(xn, wg_ref[0], NT)
        up = _dg(xn, wu_ref[0], NT)
        gate_ref[0] = gate
        up_ref[0] = up
        act = (gate * jax.nn.sigmoid(gate) * up).astype(BF16)
        acc_s[...] += _dot(act, wd_ref[0])

        @pl.when(c == N_CHIP - 1)
        def _():
            f = acc_s[...]
            f_ref[...] = f
            hn_ref[...] = h_ref[...] + 0.5 * _rms(f, gq_ref[...])

    return _call(
        body, comm, bounds, (h, g_pre, g_post, wg, wu, wd), name=name, grid=(nt, N_CHIP),
        in_specs=[pl.BlockSpec((tm, D), lambda i, c: (i, 0)), _full((1, D)), _full((1, D))] +
                 [pl.BlockSpec((1, FC, D), lambda i, c: (c, 0, 0))] * 3,
        out_specs=[pl.BlockSpec((tm, D), lambda i, c: (i, 0)),
                   pl.BlockSpec((1, tm, FC), lambda i, c: (c, i, 0)),
                   pl.BlockSpec((1, tm, FC), lambda i, c: (c, i, 0)),
                   pl.BlockSpec((tm, D), lambda i, c: (i, 0))],
        out_shape=[_out((tp, D), F32), _out((N_CHIP, tp, FC), F32),
                   _out((N_CHIP, tp, FC), F32), _out((tp, D), F32)],
        scratch_shapes=[pltpu.VMEM((tm, D), BF16), pltpu.VMEM((tm, D), F32)],
        compiler_params=_cp(("arbitrary", "arbitrary"), 48))


def _ffn_bwd(name, h, g_pre, df, gate, up, wg, wu, wd, tm, comm=None, bounds=()):
    tp = h.shape[0]
    nt = tp // tm
    rh = FC // 2

    def body(h_ref, gp_ref, df_ref, gate_ref, up_ref, wg_ref, wu_ref, wd_ref,
             dwg_ref, dwu_ref, dwd_ref, dxn_ref, rg_ref, ru_ref, rd_ref, ag, au, ad, send_sems, recv_sems):
        c = pl.program_id(0)
        i = pl.program_id(1)

        def to_sibling(a, piece):
            x, y, core = _mesh_pos()
            dw_ref, r_ref = ((dwg_ref, rg_ref), (dwu_ref, ru_ref), (dwd_ref, rd_ref))[a]
            return _remote(dw_ref.at[piece, pl.ds((1 - core) * rh, rh), :], r_ref.at[piece], send_sems, recv_sems,
                           3 * piece + a, (x, y, 1 - core))

        @pl.when(i == 0)
        def _():
            ag[...] = jnp.zeros_like(ag)
            au[...] = jnp.zeros_like(au)
            ad[...] = jnp.zeros_like(ad)

        xn = _rms(h_ref[...], gp_ref[...]).astype(BF16)
        dfb = df_ref[...].astype(BF16)
        gt = gate_ref[0]
        u = up_ref[0]
        sg = jax.nn.sigmoid(gt)
        si = gt * sg
        act = (si * u).astype(BF16)
        dact = _dg(dfb, wd_ref[0], NT)
        ad[...] += _dg(act, dfb, TN)
        dgate = (dact * u * (sg * (1.0 + gt * (1.0 - sg)))).astype(BF16)
        dup = (dact * si).astype(BF16)
        ag[...] += _dg(dgate, xn, TN)
        au[...] += _dg(dup, xn, TN)
        dxn_ref[0] = _dot(dgate, wg_ref[0]) + _dot(dup, wu_ref[0])

        @pl.when(i == nt - 1)
        def _():
            pltpu.sync_copy(ag, dwg_ref.at[c])
            pltpu.sync_copy(au, dwu_ref.at[c])
            pltpu.sync_copy(ad, dwd_ref.at[c])
            for a in range(3):
                to_sibling(a, c).start()

        @pl.when((c == N_CHIP - 1) & (i == nt - 1))
        def _():
            for piece in range(N_CHIP):
                for a in range(3):
                    to_sibling(a, piece).wait()

    return _call(
        body, comm, bounds, (h, g_pre, df, gate, up, wg, wu, wd), name=name, grid=(N_CHIP, nt),
        in_specs=[pl.BlockSpec((tm, D), lambda c, i: (i, 0)), _full((1, D)),
                  pl.BlockSpec((tm, D), lambda c, i: (i, 0)),
                  pl.BlockSpec((1, tm, FC), lambda c, i: (c, i, 0)),
                  pl.BlockSpec((1, tm, FC), lambda c, i: (c, i, 0))] +
                 [pl.BlockSpec((1, FC, D), lambda c, i: (c, 0, 0))] * 3,
        out_specs=[ANY, ANY, ANY, pl.BlockSpec((1, tm, D), lambda c, i: (c, i, 0)), ANY, ANY, ANY],
        out_shape=[_out((N_CHIP, FC, D), F32)] * 3 + [_out((N_CHIP, tp, D), F32)] +
                  [_out((N_CHIP, rh, D), F32)] * 3,
        scratch_shapes=[pltpu.VMEM((FC, D), F32)] * 3 +
                       [pltpu.SemaphoreType.DMA((3 * N_CHIP,)), pltpu.SemaphoreType.DMA((3 * N_CHIP,))],
        compiler_params=_cp(("arbitrary", "arbitrary"), 58))


def _ffn_pre_bwd(name, dh, dxn_part, h, g_pre, tm, comm=None, bounds=()):
    tp = h.shape[0]
    nt = tp // tm

    def body(dh_ref, dxn_ref, h_ref, gp_ref, out_ref, dg_ref):
        i = pl.program_id(0)
        dxn = (dxn_ref[0] + dxn_ref[1]) + (dxn_ref[2] + dxn_ref[3])
        dx, dg = _rms_bwd(h_ref[...], gp_ref[...], dxn)
        out_ref[...] = dh_ref[...] + dx

        @pl.when(i == 0)
        def _():
            dg_ref[...] = jnp.zeros_like(dg_ref)

        dg_ref[...] += dg

    return _call(
        body, comm, bounds, (dh, dxn_part, h, g_pre), name=name, grid=(nt,),
        in_specs=[_rows(tm, D), pl.BlockSpec((N_CHIP, tm, D), lambda i: (0, i, 0)), _rows(tm, D), _full((1, D))],
        out_specs=[_rows(tm, D), _full((1, D))],
        out_shape=[_out((tp, D), F32), _out((1, D), F32)],
        compiler_params=_cp(("arbitrary",), 48))


def _mix_in(h, g, w_in, tm):
    tp = h.shape[0]

    def body(h_ref, g_ref, w_ref, q_ref, k_ref, v_ref, u_ref):
        a = _rms(h_ref[...], g_ref[...]).astype(BF16)
        q_ref[...] = _dot(a, w_ref[0]).astype(BF16)
        k_ref[...] = _dot(a, w_ref[1]).astype(BF16)
        v_ref[...] = _dot(a, w_ref[2]).astype(BF16)
        u_ref[...] = _dot(a, w_ref[3])

    return pl.pallas_call(
        body, name="mix_in", grid=(tp // tm,),
        in_specs=[_rows(tm, D), _full((1, D)), _full((N_CHIP, D, NA_W))],
        out_specs=[_rows(tm, NA_W)] * 4,
        out_shape=[_out((tp, NA_W), BF16)] * 3 + [_out((tp, S5_W), F32)],
        compiler_params=_cp(("arbitrary",), 40),
    )(*_in_hbm(h, g, w_in))


def _gelu(x):
    return jax.nn.gelu(x, approximate=True)


def _gelu_grad(x):
    k = math.sqrt(2.0 / math.pi)
    t = jnp.tanh(k * (x + 0.044715 * x * x * x))
    return 0.5 * (1.0 + t) + 0.5 * x * (1.0 - t * t) * k * (1.0 + 3.0 * 0.044715 * x * x)


def _mix_out(o_na, y_pre, h, w_glu, b_glu, g_na, g_s5, w_out, g_post, tm, comm=None, bounds=()):
    tp = h.shape[0]

    def body(ona_ref, yp_ref, h_ref, wglu_ref, bglu_ref, gna_ref, gs5_ref, wout_ref, gpost_ref, hn_ref, mix_ref):
        y = _gelu(yp_ref[...])
        z = _dot(y.astype(BF16), wglu_ref[...]) + bglu_ref[...]
        o_s5 = y * jax.nn.sigmoid(z)
        n1 = _rms(ona_ref[...], gna_ref[...]).astype(BF16)
        n2 = _rms(o_s5, gs5_ref[...]).astype(BF16)
        mix = _dot(n1, wout_ref[0:NA_W, :]) + _dot(n2, wout_ref[NA_W:, :])
        mix_ref[...] = mix
        hn_ref[...] = h_ref[...] + _rms(mix, gpost_ref[...])

    return _call(
        body, comm, bounds, (o_na, y_pre, h, w_glu, b_glu, g_na, g_s5, w_out, g_post), name="mix_out",
        grid=(tp // tm,),
        in_specs=[_rows(tm, NA_W), _rows(tm, S5_W), _rows(tm, D), _full((S5_W, S5_W)), _full((1, S5_W)),
                  _full((1, NA_W)), _full((1, S5_W)), _full((D, D)), _full((1, D))],
        out_specs=[_rows(tm, D), _rows(tm, D)],
        out_shape=[_out((tp, D), F32)] * 2,
        compiler_params=_cp(("arbitrary",), 40))


def _mix_out_bwd(dh, mix, o_na, y_pre, w_glu, b_glu, g_na, g_s5, w_out, g_post, tm):
    tp = dh.shape[0]
    nt = tp // tm

    def body(dh_ref, mix_ref, ona_ref, yp_ref, wglu_ref, bglu_ref, gna_ref, gs5_ref, wout_ref, gpost_ref,
             dona_ref, dyp_ref, dwout_ref, dwglu_ref, dgpost_ref, dgna_ref, dgs5_ref, dbglu_ref, a_out, a_glu):
        i = pl.program_id(0)

        @pl.when(i == 0)
        def _():
            a_out[...] = jnp.zeros_like(a_out)
            a_glu[...] = jnp.zeros_like(a_glu)
            dgpost_ref[...] = jnp.zeros_like(dgpost_ref)
            dgna_ref[...] = jnp.zeros_like(dgna_ref)
            dgs5_ref[...] = jnp.zeros_like(dgs5_ref)
            dbglu_ref[...] = jnp.zeros_like(dbglu_ref)

        dmix, dgpost = _rms_bwd(mix_ref[...], gpost_ref[...], dh_ref[...])
        dgpost_ref[...] += dgpost
        yp = yp_ref[...]
        y = _gelu(yp)
        yb = y.astype(BF16)
        z = _dot(yb, wglu_ref[...]) + bglu_ref[...]
        sg = jax.nn.sigmoid(z)
        o_s5 = y * sg
        o_na = ona_ref[...]
        n1 = _rms(o_na, gna_ref[...]).astype(BF16)
        n2 = _rms(o_s5, gs5_ref[...]).astype(BF16)
        dmb = dmix.astype(BF16)
        a_out[0:NA_W, :] += _dg(n1, dmb, TN)
        a_out[NA_W:, :] += _dg(n2, dmb, TN)
        dn1 = _dg(dmb, wout_ref[0:NA_W, :], NT)
        dn2 = _dg(dmb, wout_ref[NA_W:, :], NT)
        dona, dgna = _rms_bwd(o_na, gna_ref[...], dn1)
        dona_ref[...] = dona
        dgna_ref[...] += dgna
        dos5, dgs5 = _rms_bwd(o_s5, gs5_ref[...], dn2)
        dgs5_ref[...] += dgs5
        dz = dos5 * y * (sg * (1.0 - sg))
        dbglu_ref[...] += jnp.sum(dz, axis=0, keepdims=True)
        dzb = dz.astype(BF16)
        a_glu[...] += _dg(yb, dzb, TN)
        dy = dos5 * sg + _dg(dzb, wglu_ref[...], NT)
        dyp_ref[...] = dy * _gelu_grad(yp)

        @pl.when(i == nt - 1)
        def _():
            pltpu.sync_copy(a_out, dwout_ref)
            pltpu.sync_copy(a_glu, dwglu_ref)

    return pl.pallas_call(
        body, name="mix_out_bwd", grid=(nt,),
        in_specs=[_rows(tm, D), _rows(tm, D), _rows(tm, NA_W), _rows(tm, S5_W), _full((S5_W, S5_W)),
                  _full((1, S5_W)), _full((1, NA_W)), _full((1, S5_W)), _full((D, D)), _full((1, D))],
        out_specs=[_rows(tm, NA_W), _rows(tm, S5_W), ANY, ANY, _full((1, D)), _full((1, NA_W)),
                   _full((1, S5_W)), _full((1, S5_W))],
        out_shape=[_out((tp, NA_W), F32), _out((tp, S5_W), F32),
                   _out((D, D), F32), _out((S5_W, S5_W), F32),
                   _out((1, D), F32), _out((1, NA_W), F32),
                   _out((1, S5_W), F32), _out((1, S5_W), F32)],
        scratch_shapes=[pltpu.VMEM((D, D), F32), pltpu.VMEM((S5_W, S5_W), F32)],
        compiler_params=_cp(("arbitrary",), 48),
    )(*_in_hbm(dh, mix, o_na, y_pre, w_glu, b_glu, g_na, g_s5, w_out, g_post))


def _mix_in_bwd(dq, dk, dv, du, h, g, w_in, dh, f1, g_post1, tm, comm=None, bounds=()):
    tp = h.shape[0]
    nt = tp // tm

    def body(dq_ref, dk_ref, dv_ref, du_ref, h_ref, g_ref, w_ref, dh_ref, f_ref, gq_ref,
             dh1_ref, df_ref, dw_ref, dg_ref, dgq_ref, acc):
        i = pl.program_id(0)

        @pl.when(i == 0)
        def _():
            acc[...] = jnp.zeros_like(acc)
            dg_ref[...] = jnp.zeros_like(dg_ref)
            dgq_ref[...] = jnp.zeros_like(dgq_ref)

        x = h_ref[...]
        a = _rms(x, g_ref[...]).astype(BF16)
        da = jnp.zeros((tm, D), F32)
        for j, r in enumerate((dq_ref, dk_ref, dv_ref, du_ref)):
            dp = r[...].astype(BF16)
            da = da + _dg(dp, w_ref[j], NT)
            acc[j] += _dg(a, dp, TN)
        dx, dg = _rms_bwd(x, g_ref[...], da)
        dh1 = dh_ref[...] + dx
        dh1_ref[...] = dh1
        dg_ref[...] += dg
        df, dgq = _rms_bwd(f_ref[...], gq_ref[...], 0.5 * dh1)
        df_ref[...] = df
        dgq_ref[...] += dgq

        @pl.when(i == nt - 1)
        def _():
            pltpu.sync_copy(acc, dw_ref)

    return _call(
        body, comm, bounds, (dq, dk, dv, du, h, g, w_in, dh, f1, g_post1), name="mix_in_bwd", grid=(nt,),
        in_specs=[_rows(tm, NA_W)] * 4 + [_rows(tm, D), _full((1, D)), _full((N_CHIP, D, NA_W)), _rows(tm, D),
                                         _rows(tm, D), _full((1, D))],
        out_specs=[_rows(tm, D), _rows(tm, D), ANY, _full((1, D)), _full((1, D))],
        out_shape=[_out((tp, D), F32), _out((tp, D), F32),
                   _out((N_CHIP, D, NA_W), F32), _out((1, D), F32),
                   _out((1, D), F32)],
        scratch_shapes=[pltpu.VMEM((N_CHIP, D, NA_W), F32)],
        compiler_params=_cp(("arbitrary",), 48))


def _final_loss(h, g_final, target, f2, g_post2, n_tok, tm):
    tp = h.shape[0]

    def body(h_ref, g_ref, t_ref, f_ref, gq_ref, dh_ref, df_ref, loss_ref, dg_ref, dgq_ref):
        i = pl.program_id(0)

        @pl.when(i == 0)
        def _():
            loss_ref[...] = jnp.zeros_like(loss_ref)
            dg_ref[...] = jnp.zeros_like(dg_ref)
            dgq_ref[...] = jnp.zeros_like(dgq_ref)

        x = h_ref[...]
        y = _rms(x, g_ref[...])
        row = i * tm + lax.broadcasted_iota(jnp.int32, (tm, 1), 0)
        valid = (row >= N_META) & (row < N_META + n_tok)
        e = jnp.where(valid, y - t_ref[...], 0.0)
        loss_ref[...] += 0.5 * jnp.sum(jnp.mean(e * e, axis=-1, keepdims=True), axis=0, keepdims=True)
        dx, dg = _rms_bwd(x, g_ref[...], e * (1.0 / D))
        dh_ref[...] = dx
        dg_ref[...] += dg
        df, dgq = _rms_bwd(f_ref[...], gq_ref[...], 0.5 * dx)
        df_ref[...] = df
        dgq_ref[...] += dgq

    return pl.pallas_call(
        body, name="final_loss", grid=(tp // tm,),
        in_specs=[_rows(tm, D), _full((1, D)), _rows(tm, D), _rows(tm, D), _full((1, D))],
        out_specs=[_rows(tm, D), _rows(tm, D), _full((1, 1)), _full((1, D)), _full((1, D))],
        out_shape=[_out((tp, D), F32), _out((tp, D), F32),
                   _out((1, 1), F32), _out((1, D), F32),
                   _out((1, D), F32)],
        compiler_params=_cp(("arbitrary",), 40),
    )(*_in_hbm(h, g_final, target, f2, g_post2))


def _na_patterns(n_rows):
    pats = []
    for kind in range(3):
        pat = [[-1] * K_ROWS for _ in range(Q_ROWS)]
        for i in range(Q_ROWS):
            for jj in range(K_ROWS):
                if kind == 0 and jj < KH:
                    pat[i][jj] = jj - i + KH - 1
                elif kind == 1 and i <= jj < i + KH:
                    pat[i][jj] = jj - i + 3
                elif kind == 2 and K_ROWS - KH <= jj:
                    pat[i][jj] = jj - i - 1
        pats.append(pat)
    return pats


def _diag_onehot():
    q = np.arange(GRID_W)[:, None]
    kc = np.arange(GRID_W)[None, :]
    start = np.clip(q - KW // 2, 0, GRID_W - KW)
    col_in = (kc >= start) & (kc < start + KW)
    e = np.zeros((32, GRID_W, GRID_W), np.float32)
    for d in range(2 * KW - 1):
        e[d] = ((kc - q + KW - 1) == d) & col_in
    return e.reshape(32, GRID_W * GRID_W), col_in


def _rpb_collapse(dtb2, et):
    def body(d_ref, e_ref, o_ref):
        o_ref[...] = jnp.dot(d_ref[...], e_ref[...], preferred_element_type=F32, precision=lax.Precision.HIGHEST)

    out = (dtb2.shape[0], et.shape[1])
    return pl.pallas_call(
        body, name="rpb_collapse", grid=(1,), out_shape=_out(out, F32),
        in_specs=[_full(dtb2.shape), _full(et.shape)], out_specs=_full(out),
    )(*_in_hbm(dtb2, et))


def _bias_tables(rpb, n_rows, comm=None, bounds=()):
    n_dr, n_dc = 2 * KH - 1, 2 * KW - 1
    pats = _na_patterns(n_rows)

    def body(rpb_ref, o_ref):
        h = pl.program_id(0)
        q = lax.broadcasted_iota(jnp.int32, (GRID_W, GRID_W), 0)
        kc = lax.broadcasted_iota(jnp.int32, (GRID_W, GRID_W), 1)
        start = jnp.clip(q - KW // 2, 0, GRID_W - KW)
        col_in = (kc >= start) & (kc < start + KW)
        diff = kc - q + (KW - 1)
        neg = jnp.full((GRID_W, GRID_W), NEG_INF, F32)
        band = []
        for dr in range(n_dr):
            acc = neg
            for d in range(n_dc):
                acc = jnp.where((diff == d) & col_in, rpb_ref[(h * n_dr + dr) * n_dc + d], acc)
            band.append(acc)
        for kind, pat in enumerate(pats):
            for i in range(Q_ROWS):
                for jj in range(K_ROWS):
                    o_ref[kind, 0, i * GRID_W:(i + 1) * GRID_W, jj * GRID_W:(jj + 1) * GRID_W] = (
                        band[pat[i][jj]] if pat[i][jj] >= 0 else neg)

    (bias,), got = _call(
        body, comm, bounds, (rpb.reshape(-1),), name="bias_tables", grid=(N_HEADS,),
        in_specs=[pl.BlockSpec(memory_space=pltpu.SMEM)],
        out_specs=[pl.BlockSpec((3, 1, QB, KB), lambda h: (0, h, 0, 0))],
        out_shape=[_out((3, N_HEADS, QB, KB), F32)],
        compiler_params=_cp(("arbitrary",), 32))
    return bias, got


def _attn_geometry(n_tok):
    n_rows = n_tok // GRID_W
    assert n_rows % Q_ROWS == 0 and n_rows >= K_ROWS
    return n_rows, n_rows // Q_ROWS


def _attn_probs(qh, kh, kmh, bias, scale):
    s = _dg(qh, kh, NT) * scale + bias
    sm = _dg(qh, kmh, NT) * scale
    m = jnp.maximum(jnp.max(s, axis=-1, keepdims=True), jnp.max(sm, axis=-1, keepdims=True))
    p = jnp.exp(s - m)
    pm = jnp.exp(sm - m)
    inv = 1.0 / (jnp.sum(p, axis=-1, keepdims=True) + jnp.sum(pm, axis=-1, keepdims=True))
    return p * inv, pm * inv


def _meta_probs(qmh, kmh, scale):
    s = _dg(qmh, kmh, NT) * scale
    p = jnp.exp(s - jnp.max(s, axis=-1, keepdims=True))
    return p / jnp.sum(p, axis=-1, keepdims=True)


def _step_rows(r, n_rows):
    q0 = pl.multiple_of(N_META + r * QB, 16)
    k0 = pl.multiple_of(N_META + jnp.clip(Q_ROWS * r - (K_ROWS - KH), 0, n_rows - K_ROWS) * GRID_W, 16)
    return q0, k0


def _attn_fwd(q, k, v, bias, n_tok, comm=None, bounds=()):
    tp = q.shape[0]
    n_rows, n_steps = _attn_geometry(n_tok)
    scale = HEAD_DIM ** -0.5

    def body(q_ref, k_ref, v_ref, b_ref, o_ref):
        r = pl.program_id(1)
        km = k_ref[0:N_META, :]
        vm = v_ref[0:N_META, :]

        @pl.when(r == 0)
        def _():
            qm = q_ref[0:N_META, :]
            outs = []
            for hh in range(2):
                sl = slice(hh * HEAD_DIM, (hh + 1) * HEAD_DIM)
                p = _meta_probs(qm[:, sl], km[:, sl], scale)
                outs.append(_dot(p.astype(BF16), vm[:, sl]))
            o_ref[0:N_META, :] = jnp.concatenate(outs, axis=1)
            o_ref[N_META + n_tok:, :] = jnp.zeros((tp - N_META - n_tok, 2 * HEAD_DIM), F32)

        q0, k0 = _step_rows(r, n_rows)
        qb = q_ref[pl.ds(q0, QB), :]
        kb = k_ref[pl.ds(k0, KB), :]
        vb = v_ref[pl.ds(k0, KB), :]
        outs = []
        for hh in range(2):
            sl = slice(hh * HEAD_DIM, (hh + 1) * HEAD_DIM)
            p, pm = _attn_probs(qb[:, sl], kb[:, sl], km[:, sl], b_ref[0, hh], scale)
            outs.append(_dot(p.astype(BF16), vb[:, sl]) + _dot(pm.astype(BF16), vm[:, sl]))
        o_ref[pl.ds(q0, QB), :] = jnp.concatenate(outs, axis=1)

    def bias_map(hp, r):
        return (jnp.where(r == 0, 0, jnp.where(r == n_steps - 1, 2, 1)), hp, 0, 0)

    col = pl.BlockSpec((tp, 2 * HEAD_DIM), lambda hp, r: (0, hp))
    return _call(
        body, comm, bounds, (q, k, v, bias), name="attn_fwd", grid=(N_HEADS // 2, n_steps),
        in_specs=[col, col, col, pl.BlockSpec((1, 2, QB, KB), bias_map)],
        out_specs=[col], out_shape=[_out((tp, NA_W), F32)],
        compiler_params=_cp(("arbitrary", "arbitrary"), 40))


def _attn_bwd(q, k, v, bias, do, n_tok, comm=None, bounds=()):
    tp = q.shape[0]
    n_rows, n_steps = _attn_geometry(n_tok)
    scale = HEAD_DIM ** -0.5
    pats = _na_patterns(n_rows)

    def body(q_ref, k_ref, v_ref, b_ref, do_ref, dq_ref, dk_ref, dv_ref, dtb_ref):
        r = pl.program_id(1)
        km = k_ref[0:N_META, :]
        vm = v_ref[0:N_META, :]

        @pl.when(r == 0)
        def _():
            dk_ref[...] = jnp.zeros_like(dk_ref)
            dv_ref[...] = jnp.zeros_like(dv_ref)
            dtb_ref[...] = jnp.zeros_like(dtb_ref)
            dq_ref[N_META + n_tok:, :] = jnp.zeros((tp - N_META - n_tok, 2 * HEAD_DIM), F32)
            qm = q_ref[0:N_META, :]
            dom = do_ref[0:N_META, :].astype(BF16)
            dqs, dks, dvs = [], [], []
            for hh in range(2):
                sl = slice(hh * HEAD_DIM, (hh + 1) * HEAD_DIM)
                p = _meta_probs(qm[:, sl], km[:, sl], scale)
                dp = _dg(dom[:, sl], vm[:, sl], NT)
                ds = (p * (dp - jnp.sum(dp * p, axis=-1, keepdims=True))).astype(BF16)
                dvs.append(_dg(p.astype(BF16), dom[:, sl], TN))
                dqs.append(_dot(ds, km[:, sl]) * scale)
                dks.append(_dg(ds, qm[:, sl], TN) * scale)
            dq_ref[0:N_META, :] = jnp.concatenate(dqs, axis=1)
            dk_ref[0:N_META, :] += jnp.concatenate(dks, axis=1)
            dv_ref[0:N_META, :] += jnp.concatenate(dvs, axis=1)

        q0, k0 = _step_rows(r, n_rows)
        qb = q_ref[pl.ds(q0, QB), :]
        kb = k_ref[pl.ds(k0, KB), :]
        vb = v_ref[pl.ds(k0, KB), :]
        dob = do_ref[pl.ds(q0, QB), :].astype(BF16)
        dqs, dks, dvs, dkms, dvms, dss = [], [], [], [], [], []
        for hh in range(2):
            sl = slice(hh * HEAD_DIM, (hh + 1) * HEAD_DIM)
            qh, kh, vh, kmh, vmh, doh = qb[:, sl], kb[:, sl], vb[:, sl], km[:, sl], vm[:, sl], dob[:, sl]
            p, pm = _attn_probs(qh, kh, kmh, b_ref[0, hh], scale)
            dp = _dg(doh, vh, NT)
            dpm = _dg(doh, vmh, NT)
            delta = jnp.sum(dp * p, axis=-1, keepdims=True) + jnp.sum(dpm * pm, axis=-1, keepdims=True)
            ds = p * (dp - delta)
            dsb = ds.astype(BF16)
            dsmb = (pm * (dpm - delta)).astype(BF16)
            dss.append(ds)
            dvs.append(_dg(p.astype(BF16), doh, TN))
            dvms.append(_dg(pm.astype(BF16), doh, TN))
            dqs.append((_dot(dsb, kh) + _dot(dsmb, kmh)) * scale)
            dks.append(_dg(dsb, qh, TN) * scale)
            dkms.append(_dg(dsmb, qh, TN) * scale)
        dq_ref[pl.ds(q0, QB), :] = jnp.concatenate(dqs, axis=1)
        dk_ref[pl.ds(k0, KB), :] += jnp.concatenate(dks, axis=1)
        dv_ref[pl.ds(k0, KB), :] += jnp.concatenate(dvs, axis=1)
        dk_ref[0:N_META, :] += jnp.concatenate(dkms, axis=1)
        dv_ref[0:N_META, :] += jnp.concatenate(dvms, axis=1)

        def add_bias_grad(pat):
            for hh in range(2):
                for i in range(Q_ROWS):
                    for jj in range(K_ROWS):
                        if pat[i][jj] >= 0:
                            dtb_ref[hh, pat[i][jj]] += dss[hh][i * GRID_W:(i + 1) * GRID_W,
                                                               jj * GRID_W:(jj + 1) * GRID_W]

        @pl.when(r == 0)
        def _():
            add_bias_grad(pats[0])

        @pl.when((r > 0) & (r < n_steps - 1))
        def _():
            add_bias_grad(pats[1])

        @pl.when(r == n_steps - 1)
        def _():
            add_bias_grad(pats[2])

    def bias_map(hp, r):
        return (jnp.where(r == 0, 0, jnp.where(r == n_steps - 1, 2, 1)), hp, 0, 0)

    col = pl.BlockSpec((tp, 2 * HEAD_DIM), lambda hp, r: (0, hp))
    n_dr = 2 * KH - 1
    return _call(
        body, comm, bounds, (q, k, v, bias, do), name="attn_bwd", grid=(N_HEADS // 2, n_steps),
        in_specs=[col, col, col, pl.BlockSpec((1, 2, QB, KB), bias_map), col],
        out_specs=[col, col, col, pl.BlockSpec((2, n_dr, GRID_W, GRID_W), lambda hp, r: (hp, 0, 0, 0))],
        out_shape=[_out((tp, NA_W), F32)] * 3 +
                  [_out((N_HEADS, n_dr, GRID_W, GRID_W), F32)],
        compiler_params=_cp(("arbitrary", "arbitrary"), 48))


def _repeat_onehot():
    return np.repeat(np.eye(2 * S5_G, dtype=np.float32), S5_H, axis=0)


def _s5_disc_math(lam_re, lam_im, log_dt, b_re, b_im, rep):
    dt = jnp.exp(log_dt)
    ea = jnp.exp(lam_re * dt)
    a_re = ea * jnp.cos(lam_im * dt)
    a_im = ea * jnp.sin(lam_im * dt)
    den = lam_re * lam_re + lam_im * lam_im
    c_re = ((a_re - 1.0) * lam_re + a_im * lam_im) / den
    c_im = (a_im * lam_re - (a_re - 1.0) * lam_im) / den
    ce_re = jnp.dot(rep, c_re, preferred_element_type=F32, precision=lax.Precision.HIGHEST)
    ce_im = jnp.dot(rep, c_im, preferred_element_type=F32, precision=lax.Precision.HIGHEST)
    return a_re, a_im, ce_re * b_re - ce_im * b_im, ce_re * b_im + ce_im * b_re


def _s5_blocks():
    gl = S5_G // N_BUNDLE
    half = gl * S5_P
    out = []
    for d in range(2):
        for g in range(S5_G):
            b, k = divmod(g, gl)
            dg = d * S5_G + g
            out.append((d, b, slice(k * S5_H, (k + 1) * S5_H), slice(k * S5_P, (k + 1) * S5_P),
                        slice(half + k * S5_P, half + (k + 1) * S5_P), slice(dg * S5_H, (dg + 1) * S5_H),
                        slice(dg, dg + 1)))
    return out


def _s5_params(lam_re, lam_im, log_dt, b_re, b_im, c_re, c_im):
    cw, sw = S5_W // N_BUNDLE, 2 * (S5_G // N_BUNDLE) * S5_P

    def body(lr, li, ld, br, bi, cr, ci, rep_ref, a1_ref, a2_ref, bm_ref, cm_ref):
        a_re, a_im, bb_re, bb_im = _s5_disc_math(lr[...], li[...], ld[...], br[...], bi[...], rep_ref[...])
        cc_re = cr[...]
        cc_im = ci[...]
        bm_ref[...] = jnp.zeros_like(bm_ref)
        cm_ref[...] = jnp.zeros_like(cm_ref)
        for d, b, rows, re, im, nat, one in _s5_blocks():
            bm_ref[d, b, rows, re] = bb_re[nat, :].astype(BF16)
            bm_ref[d, b, rows, im] = bb_im[nat, :].astype(BF16)
            cm_ref[d, b, rows, re] = cc_re[nat, :].astype(BF16)
            cm_ref[d, b, rows, im] = (-cc_im[nat, :]).astype(BF16)
            k = rows.start // S5_H
            lanes = slice((k % 2) * S5_P, (k % 2 + 1) * S5_P)
            for part, (v1, v2) in enumerate(((a_re[one, :], a_im[one, :]), (a_re[one, :], -a_im[one, :]))):
                sub = slice(4 * part + k // 2, 4 * part + k // 2 + 1)
                a1_ref[d, b, sub, lanes] = v1
                a2_ref[d, b, sub, lanes] = v2

    args = (lam_re, lam_im, log_dt, b_re, b_im, c_re, c_im, jnp.asarray(_repeat_onehot()))
    outs = [((2, N_BUNDLE, 8, 128), F32)] * 2 + [((2, N_BUNDLE, cw, sw), BF16)] * 2
    return pl.pallas_call(
        body, name="s5_params", grid=(1,), in_specs=[_full(a.shape) for a in args],
        out_specs=[_full(s) for s, _ in outs], out_shape=[_out(s, dt) for s, dt in outs],
    )(*_in_hbm(*args))


def _s5_params_bwd(lam_re, lam_im, log_dt, b_re, b_im, da, dbm, dcm):
    n, nb = 2 * S5_G, 2 * S5_G * S5_H

    def body(lr, li, ld, br, bi, rep_ref, da_ref, dbm_ref, dcm_ref, o_lr, o_li, o_ld, o_br, o_bi, o_cr, o_ci,
             dar_s, dai_s, dbr_s, dbi_s):
        for d, b, rows, re, im, nat, one in _s5_blocks():
            dbr_s[nat, :] = dbm_ref[d, b, rows, re]
            dbi_s[nat, :] = dbm_ref[d, b, rows, im]
            o_cr[nat, :] = dcm_ref[d, b, rows, re]
            o_ci[nat, :] = -dcm_ref[d, b, rows, im]
            dar_s[one, :] = da_ref[d, b, :, re]
            dai_s[one, :] = da_ref[d, b, :, im]
        rep = rep_ref[...]
        _, vjp = jax.vjp(lambda p, q, r, s, t: _s5_disc_math(p, q, r, s, t, rep),
                         lr[...], li[...], ld[...], br[...], bi[...])
        o_lr[...], o_li[...], o_ld[...], o_br[...], o_bi[...] = vjp((dar_s[...], dai_s[...], dbr_s[...], dbi_s[...]))

    args = (lam_re, lam_im, log_dt, b_re, b_im, jnp.asarray(_repeat_onehot()), da, dbm, dcm)
    outs = [(n, S5_P)] * 2 + [(n, 1)] + [(nb, S5_P)] * 4
    return pl.pallas_call(
        body, name="s5_params_bwd", grid=(1,), in_specs=[_full(a.shape) for a in args],
        out_specs=[_full(s) for s in outs], out_shape=[_out(s, F32) for s in outs],
        scratch_shapes=[pltpu.VMEM((n, S5_P), F32)] * 2 + [pltpu.VMEM((nb, S5_P), F32)] * 2,
    )(*_in_hbm(*args))


def _tiles_store(ref, base, val):
    for i in range(val.shape[0] // 8):
        for c in range(8):
            ref[pl.ds(base + (8 * i + c) * 8, 8), :] = val[8 * i:8 * i + 8, 128 * c:128 * (c + 1)]


def _tiles_load(ref, base, n):
    return jnp.concatenate(
        [jnp.concatenate([ref[pl.ds(base + (8 * i + c) * 8, 8), :] for c in range(8)], axis=1) for i in range(n // 8)],
        axis=0)


def _time_rows(base, t):
    return pl.ds(base + (t // 8) * 64 + t % 8, 8, stride=8)


def _scan(chains, n):
    xs = [c["x"] for c in chains]
    for k in range(n):
        for ci, c in enumerate(chains):
            t = n - 1 - k if c["reverse"] else k
            if c["prev"] is not None:
                c["prev"][_time_rows(c["prev_base"], t), :] = xs[ci]
            xs[ci] = c["a1"] * xs[ci] + pltpu.roll(c["a2"] * xs[ci], 4, axis=0) + c["src"][_time_rows(0, t), :]
            if c["dst"] is not None:
                c["dst"][_time_rows(0, t), :] = xs[ci]
    return xs


def _chain(x, a1, a2, src, dst=None, prev=None, prev_base=0, reverse=False):
    return dict(x=x, a1=a1, a2=a2, src=src, dst=dst, prev=prev, prev_base=prev_base, reverse=reverse)


def _s5_fwd(u, d_skip, a1, a2, bm, cm, length, comm=None, bounds=()):
    tp = u.shape[0]
    cw = S5_W // N_BUNDLE
    sw = bm.shape[-1]
    n_full, n_tail = divmod(length, SCAN_CHUNK)
    t_tail = n_full * SCAN_CHUNK

    nbs = N_BUNDLE

    def body(u_ref, d_ref, a1_ref, a2_ref, bm_ref, cm_ref, y_ref, bnd_ref, *scratch):
        y_ref[...] = u_ref[...] * d_ref[...]
        ins, xss = (scratch[0:nbs], scratch[nbs:2 * nbs]), (scratch[2 * nbs:3 * nbs], scratch[3 * nbs:])
        cols = [slice(b * cw, (b + 1) * cw) for b in range(nbs)]

        def keep(dr, chunk, xs):
            for b in range(nbs):
                bnd_ref[dr, b, chunk] = xs[b]

        def load(dr, t0, n):
            for b in range(nbs):
                _tiles_store(ins[dr][b], 0, _dot(u_ref[pl.ds(t0, n), cols[b]].astype(BF16), bm_ref[dr, b]))

        def chains(dr, xs):
            return [_chain(xs[b], a1_ref[dr, b], a2_ref[dr, b], ins[dr][b], dst=xss[dr][b], reverse=dr == 1)
                    for b in range(nbs)]

        def emit(dr, t0, n):
            for b in range(nbs):
                y_ref[pl.ds(t0, n), cols[b]] += _dg(_tiles_load(xss[dr][b], 0, n).astype(BF16), cm_ref[dr, b], NT)

        zero = (jnp.zeros((8, 128), F32),) * nbs
        xb = zero
        if n_tail:
            keep(1, n_full, xb)
            load(1, t_tail, n_tail)
            xb = tuple(_scan(chains(1, xb), n_tail))
            emit(1, t_tail, n_tail)

        def pair(i, carry):
            j = n_full - 1 - i
            t0s = (pl.multiple_of(i * SCAN_CHUNK, SCAN_CHUNK), pl.multiple_of(j * SCAN_CHUNK, SCAN_CHUNK))
            keep(0, i, carry[0])
            keep(1, j, carry[1])
            for dr in range(2):
                load(dr, t0s[dr], SCAN_CHUNK)
            out = _scan(chains(0, carry[0]) + chains(1, carry[1]), SCAN_CHUNK)
            for dr in range(2):
                emit(dr, t0s[dr], SCAN_CHUNK)
            return tuple(out[:nbs]), tuple(out[nbs:])

        xf, _ = lax.fori_loop(0, n_full, pair, (zero, xb))
        if n_tail:
            keep(0, n_full, xf)
            load(0, t_tail, n_tail)
            _scan(chains(0, xf), n_tail)
            emit(0, t_tail, n_tail)

    n_chunks = n_full + (1 if n_tail else 0)
    tile = pl.BlockSpec((2, nbs, 8, 128), lambda b: (0, b, 0, 0))
    return _call(
        body, comm, bounds, (u, d_skip, a1, a2, bm, cm), name="s5_fwd", grid=(N_BUNDLE // nbs,),
        in_specs=[pl.BlockSpec((tp, nbs * cw), lambda b: (0, b)), pl.BlockSpec((1, nbs * cw), lambda b: (0, b)),
                  tile, tile, pl.BlockSpec((2, nbs, cw, sw), lambda b: (0, b, 0, 0)),
                  pl.BlockSpec((2, nbs, cw, sw), lambda b: (0, b, 0, 0))],
        out_specs=[pl.BlockSpec((tp, nbs * cw), lambda b: (0, b)),
                   pl.BlockSpec((2, nbs, n_chunks, 8, 128), lambda b: (0, b, 0, 0, 0))],
        out_shape=[_out((tp, S5_W), F32), _out((2, N_BUNDLE, n_chunks, 8, 128), F32)],
        scratch_shapes=[pltpu.VMEM((SCAN_CHUNK * 8, 128), F32)] * (4 * nbs),
        compiler_params=_cp(("arbitrary",), 48))


def _s5_bwd(u, dy, d_skip, a1, a2, bm, cm, bnd, length):
    tp = u.shape[0]
    cw = S5_W // N_BUNDLE
    sw = bm.shape[-1]
    half = sw // 2
    n_full, n_tail = divmod(length, SCAN_CHUNK)
    t_tail = n_full * SCAN_CHUNK
    n_chunks = bnd.shape[2]
    nbs = 2

    def body(u_ref, dy_ref, d_ref, a1_ref, a2_ref, bm_ref, cm_ref, bnd_ref, du_ref, dd_ref, dbm_ref, dcm_ref,
             da_ref, *scratch):
        du_ref[...] = dy_ref[...] * d_ref[...]
        dd_ref[...] = jnp.sum(dy_ref[...] * u_ref[...], axis=0, keepdims=True)
        dbm_ref[...] = jnp.zeros_like(dbm_ref)
        dcm_ref[...] = jnp.zeros_like(dcm_ref)
        da_ref[...] = jnp.zeros_like(da_ref)
        bu_s, dx_s, g_s, xp_s, x_s = ([scratch[(k * 2 + dr) * nbs:(k * 2 + dr + 1) * nbs] for dr in range(2)]
                                      for k in range(5))
        cols = [slice(b * cw, (b + 1) * cw) for b in range(nbs)]

        def chains(dr, chunk, t0, n, gs):
            out = []
            for b in range(nbs):
                _tiles_store(bu_s[dr][b], 0, _dot(u_ref[pl.ds(t0, n), cols[b]].astype(BF16), bm_ref[dr, b]))
                _tiles_store(dx_s[dr][b], 0, _dot(dy_ref[pl.ds(t0, n), cols[b]].astype(BF16), cm_ref[dr, b]))
                out.append(_chain(bnd_ref[dr, b, chunk], a1_ref[dr, b], a2_ref[dr, b], bu_s[dr][b],
                                  dst=x_s[dr][b], prev=xp_s[dr][b], reverse=dr == 1))
                out.append(_chain(gs[b], a1_ref[dr, b], -a2_ref[dr, b], dx_s[dr][b], dst=g_s[dr][b], reverse=dr == 0))
            return out

        def emit(dr, t0, n):
            rows = pl.ds(t0, n)
            for b in range(nbs):
                ub = u_ref[rows, cols[b]].astype(BF16)
                dyb = dy_ref[rows, cols[b]].astype(BF16)
                g = _tiles_load(g_s[dr][b], 0, n)
                gb = g.astype(BF16)
                du_ref[rows, cols[b]] += _dg(gb, bm_ref[dr, b], NT)
                dbm_ref[dr, b] += _dg(ub, gb, TN)
                xp = _tiles_load(xp_s[dr][b], 0, n)
                xp_r, xp_i = xp[:, 0:half], xp[:, half:]
                g_r, g_i = g[:, 0:half], g[:, half:]
                dcm_ref[dr, b] += _dg(dyb, _tiles_load(x_s[dr][b], 0, n).astype(BF16), TN)
                da_ref[dr, b] += jnp.concatenate([jnp.sum(g_r * xp_r + g_i * xp_i, axis=0, keepdims=True),
                                                  jnp.sum(g_i * xp_r - g_r * xp_i, axis=0, keepdims=True)], axis=1)

        def adjoints(out):
            return tuple(out[1::2])

        zero = (jnp.zeros((8, 128), F32),) * nbs
        g0 = zero
        if n_tail:
            g0 = adjoints(_scan(chains(0, n_full, t_tail, n_tail, g0), n_tail))
            emit(0, t_tail, n_tail)

        def pair(i, carry):
            j = n_full - 1 - i
            t0 = (pl.multiple_of(j * SCAN_CHUNK, SCAN_CHUNK), pl.multiple_of(i * SCAN_CHUNK, SCAN_CHUNK))
            both = chains(0, j, t0[0], SCAN_CHUNK, carry[0]) + chains(1, i, t0[1], SCAN_CHUNK, carry[1])
            out = _scan(both, SCAN_CHUNK)
            emit(0, t0[0], SCAN_CHUNK)
            emit(1, t0[1], SCAN_CHUNK)
            return adjoints(out[:2 * nbs]), adjoints(out[2 * nbs:])

        _, g1 = lax.fori_loop(0, n_full, pair, (g0, zero))
        if n_tail:
            _scan(chains(1, n_full, t_tail, n_tail, g1), n_tail)
            emit(1, t_tail, n_tail)

    tile = pl.BlockSpec((2, nbs, 8, 128), lambda b: (0, b, 0, 0))
    wide = pl.BlockSpec((2, nbs, cw, sw), lambda b: (0, b, 0, 0))
    col = pl.BlockSpec((tp, nbs * cw), lambda b: (0, b))
    row = pl.BlockSpec((1, nbs * cw), lambda b: (0, b))
    arow = pl.BlockSpec((2, nbs, 1, sw), lambda b: (0, b, 0, 0))
    return pl.pallas_call(
        body, name="s5_bwd", grid=(N_BUNDLE // nbs,),
        in_specs=[col, col, row, tile, tile, wide, wide,
                  pl.BlockSpec((2, nbs, n_chunks, 8, 128), lambda b: (0, b, 0, 0, 0))],
        out_specs=[col, row, wide, wide, arow],
        out_shape=[_out((tp, S5_W), F32), _out((1, S5_W), F32),
                   _out((2, N_BUNDLE, cw, sw), F32), _out((2, N_BUNDLE, cw, sw), F32),
                   _out((2, N_BUNDLE, 1, sw), F32)],
        scratch_shapes=[pltpu.VMEM((SCAN_CHUNK * 8, 128), F32)] * (10 * nbs),
        compiler_params=_cp(("arbitrary",), 56),
    )(*_in_hbm(u, dy, d_skip, a1, a2, bm, cm, bnd))


def _row_tile(tp):
    return max(tm for tm in range(16, 449, 16) if tp % tm == 0)


def _step(x, target, bufs, gains, s5, rpb, c_arr, kc_arr, me_arr):
    n_tok = x.shape[0]
    first = ["ffn1_w_gate", "ffn1_w_up", "ffn1_w_down", "meta_tokens"]
    bias, got = _bias_tables(rpb, n_tok // GRID_W, _gather_comm([bufs[n] for n in first]), (0, N_HEADS - 1))
    w = dict(zip(first, got))
    meta = w["meta_tokens"].transpose(1, 0, 2).reshape(N_META, D)
    length = N_META + n_tok
    tp = length + 16
    tm = _row_tile(tp)
    tmb = tm
    n_rows = n_tok // GRID_W
    pad = jnp.zeros((tp - length, D), F32)
    h0 = jnp.concatenate([meta, x, pad], axis=0)
    tgt = jnp.concatenate([jnp.zeros((N_META, D), F32), target, pad], axis=0)

    lam_re, _ = lax.optimization_barrier((s5["lam_re"], bias))
    s5p = (lam_re, s5["lam_im"], s5["log_dt"].reshape(2 * S5_G, 1), s5["b_re"], s5["b_im"])
    a1_m, a2_m, bm16, cm16 = _s5_params(*s5p, s5["c_re"], s5["c_im"])

    mid = ["w_in", "s5_w_glu", "w_out"]
    (h1, gate1, up1, f1), got = _ffn_fwd(
        "ffn1_fwd", h0, gains["ffn1_pre_g"], gains["ffn1_post_g"], w["ffn1_w_gate"], w["ffn1_w_up"], w["ffn1_w_down"],
        tm, _gather_comm([bufs[n] for n in mid]), (0, (tp // tm) * N_CHIP * 3 // 5))
    w.update(zip(mid, got))
    q, k, v, u = _mix_in(h1, gains["mix_pre_g"], w["w_in"], tm)
    (o_na,), (gate_ici, up_ici) = _attn_fwd(
        q, k, v, bias, n_tok, _gather_comm([bufs["ffn2_w_gate"], bufs["ffn2_w_up"]], pair=False), (0,))
    (y_pre, s5_bnd), (w["ffn2_w_gate"], w["ffn2_w_up"], down_ici) = _s5_fwd(
        u, gains["s5_d"], a1_m, a2_m, bm16, cm16, length,
        _merge_comm(_gather_comm([gate_ici, up_ici], ici=False),
                    _gather_comm([bufs["ffn2_w_down"]], pair=False)), (0,))
    w_glu = w["s5_w_glu"].reshape(S5_W, S5_W)
    w_out = w["w_out"].reshape(D, D)
    (h2, mix), (w["ffn2_w_down"],) = _mix_out(
        o_na, y_pre, h1, w_glu, gains["s5_b_glu"], gains["na_out_g"], gains["s5_out_g"], w_out, gains["mix_post_g"], tm,
        _gather_comm([down_ici], ici=False), (0,))
    (h3, gate2, up2, f2), _ = _ffn_fwd("ffn2_fwd", h2, gains["ffn2_pre_g"], gains["ffn2_post_g"],
                                       w["ffn2_w_gate"], w["ffn2_w_up"], w["ffn2_w_down"], tm)
    dh3, df2, loss, dg_final, dg_post2 = _final_loss(h3, gains["final_g"], tgt, f2, gains["ffn2_post_g"], n_tok, tm)

    ffn2 = ["ffn2_w_gate", "ffn2_w_up", "ffn2_w_down"]
    ffn1 = ["ffn1_w_gate", "ffn1_w_up", "ffn1_w_down"]
    out2, _ = _ffn_bwd("ffn2_bwd", h2, gains["ffn2_pre_g"], df2, gate2, up2,
                       w["ffn2_w_gate"], w["ffn2_w_up"], w["ffn2_w_down"], tmb)
    dxn2 = out2[3]
    sums2 = _chip_sums("chip_sums_ffn2", out2[0:3], out2[4:7], c_arr)
    (dh2, dg_pre2), _ = _ffn_pre_bwd("ffn2_pre_bwd", dh3, dxn2, h2, gains["ffn2_pre_g"], tm)
    do_na, dy_pre, dw_out, dw_glu, dg_mpost, dg_na, dg_s5, db_glu = _mix_out_bwd(
        dh2, mix, o_na, y_pre, w_glu, gains["s5_b_glu"], gains["na_out_g"], gains["s5_out_g"], w_out,
        gains["mix_post_g"], tm)
    (dq, dk, dv, dtb), recv3 = _attn_bwd(q, k, v, bias, do_na, n_tok, _scatter_comm(sums2), (0,))
    totals2 = _total_sums("total_sums_ffn2", sums2, recv3, kc_arr)
    du, dd, dbm, dcm, da_m = _s5_bwd(u, dy_pre, gains["s5_d"], a1_m, a2_m, bm16, cm16, s5_bnd, length)
    (dh1, df1, dw_in, dg_mpre, dg_post1), done2 = _mix_in_bwd(
        dq, dk, dv, du, h1, gains["mix_pre_g"], w["w_in"], dh2, f1, gains["ffn1_post_g"], tm,
        _assemble_comm(totals2), (0,))
    pieces = dict(zip(ffn2, done2))

    e, _ = _diag_onehot()
    n_dr = 2 * KH - 1
    drpb = _rpb_collapse(dtb.reshape(N_HEADS * n_dr, GRID_W * GRID_W), jnp.asarray(e.T))
    drpb = drpb[:, :2 * KW - 1].reshape(N_HEADS, n_dr, 2 * KW - 1).transpose(1, 0, 2).reshape(N_HEADS * n_dr, 2 * KW - 1)
    dlam_re, dlam_im, dlog_dt, db_re, db_im, dc_re, dc_im = _s5_params_bwd(*s5p, da_m, dbm, dcm)
    early = {"ffn1_post_g": dg_post1, "mix_pre_g": dg_mpre, "na_rpb": drpb,
             "s5_lam_re": dlam_re, "s5_lam_im": dlam_im, "s5_log_dt": dlog_dt.reshape(2, S5_G),
             "s5_b_re": db_re, "s5_b_im": db_im, "s5_c_re": dc_re, "s5_c_im": dc_im,
             "s5_d": dd, "s5_b_glu": db_glu, "na_out_g": dg_na,
             "s5_out_g": dg_s5, "mix_post_g": dg_mpost, "ffn2_pre_g": dg_pre2, "ffn2_post_g": dg_post2,
             "final_g": dg_final}
    names = list(early)
    slots = _small_pack([early[n] for n in names], me_arr)

    out1, slots = _ffn_bwd("ffn1_bwd", h0, gains["ffn1_pre_g"], df1, gate1, up1,
                           w["ffn1_w_gate"], w["ffn1_w_up"], w["ffn1_w_down"], tmb, _spread_comm(slots), (0,))
    small = dict(zip(names, _small_total(slots, [early[n].shape for n in names])))
    sums1 = _chip_sums("chip_sums_ffn1", out1[0:3], out1[4:7], c_arr)
    flight1 = _scatter_start("ffn1", sums1)
    token = flight1[4]
    rest = [dw_in, dw_glu.reshape(N_CHIP, S5_W // N_CHIP, S5_W), dw_out.reshape(N_CHIP, D // N_CHIP, D)]
    (dh0, dg_pre1), recv_rest = _ffn_pre_bwd("ffn1_pre_bwd", dh1, out1[3], h0, gains["ffn1_pre_g"] + token[0:1, 0:1],
                                             tm, _exchange_comm(rest), (0,))
    sums = _chip_sums("chip_sums_rest", rest, recv_rest, c_arr)
    flight2 = _scatter_start("rest", sums)
    return loss[0, 0], dh0, pieces, small, {"ffn1_pre_g": dg_pre1}, (ffn1, flight1[:4]), (mid, flight2[:4])


def _mesh_pos():
    return lax.axis_index("x"), lax.axis_index("y"), lax.axis_index("c")


def _other_chips(x, y):
    return [(1 - x, y), (x, 1 - y), (1 - x, 1 - y)]


class _Comm:
    def __init__(self, ins, out_shape, aliases, parts):
        self.ins, self.out_shape, self.aliases, self.parts = list(ins), list(out_shape), dict(aliases), list(parts)
        self.n_sems = sum(p[0] for p in parts)

    def bases(self):
        out, base = [], 0
        for n_sems, _, _ in self.parts:
            out.append(base)
            base += n_sems
        return out


def _run_comm(name, comm):
    n_i, n_o = len(comm.ins), len(comm.out_shape)

    def body(*refs):
        ins, outs = refs[:n_i], refs[n_i:n_i + n_o]
        send_sems, recv_sems = refs[n_i + n_o:]
        for base, (_, start, finish) in zip(comm.bases(), comm.parts):
            start(ins, outs, send_sems, recv_sems, base)
            finish(ins, outs, send_sems, recv_sems, base)

    return pl.pallas_call(
        body, name=name, out_shape=comm.out_shape, in_specs=[ANY] * n_i, out_specs=[ANY] * n_o,
        input_output_aliases=comm.aliases,
        scratch_shapes=[pltpu.SemaphoreType.DMA((comm.n_sems,)), pltpu.SemaphoreType.DMA((comm.n_sems,))],
    )(*_in_hbm(*comm.ins))


def _call(body, comm, bounds, args, *, name, grid, in_specs, out_specs, out_shape, scratch_shapes=(),
          compiler_params=None):
    in_specs, out_specs, out_shape, scratch_shapes = list(in_specs), list(out_specs), list(out_shape), list(scratch_shapes)
    if comm is None:
        return pl.pallas_call(body, name=name, grid=grid, in_specs=in_specs, out_specs=out_specs, out_shape=out_shape,
                              scratch_shapes=scratch_shapes, compiler_params=compiler_params)(*_in_hbm(*args)), []
    n_in, n_out, n_scr = len(in_specs), len(out_specs), len(scratch_shapes)
    n_ci, n_co = len(comm.ins), len(comm.out_shape)
    n_steps = int(np.prod(grid))
    assert len(bounds) == len(comm.parts) and all(0 <= b < n_steps for b in bounds) and list(bounds) == sorted(bounds)

    def fused(*refs):
        a = n_in
        b = a + n_ci
        c = b + n_out
        d = c + n_co
        e = d + n_scr
        cargs = (refs[a:b], refs[c:d], refs[e], refs[e + 1])
        step = pl.program_id(0)
        for ax in range(1, len(grid)):
            step = step * grid[ax] + pl.program_id(ax)
        bases = comm.bases()
        for p, (_, start, finish) in enumerate(comm.parts):
            @pl.when(step == bounds[p])
            def _(p=p, start=start):
                if p > 0:
                    comm.parts[p - 1][2](*cargs, bases[p - 1])
                start(*cargs, bases[p])
        body(*(refs[:a] + refs[b:c] + refs[d:e]))

        @pl.when(step == n_steps - 1)
        def _():
            comm.parts[-1][2](*cargs, bases[-1])

    res = pl.pallas_call(
        fused, name=name, grid=grid, in_specs=in_specs + [ANY] * n_ci, out_specs=out_specs + [ANY] * n_co,
        out_shape=out_shape + comm.out_shape,
        scratch_shapes=scratch_shapes + [pltpu.SemaphoreType.DMA((comm.n_sems,)), pltpu.SemaphoreType.DMA((comm.n_sems,))],
        input_output_aliases={n_in + i: n_out + j for i, j in comm.aliases.items()},
        compiler_params=compiler_params)(*_in_hbm(*args, *comm.ins))
    return res[:n_out], res[n_out:]


def _remote(src, dst, send_sems, recv_sems, idx, to):
    return pltpu.make_async_remote_copy(src_ref=src, dst_ref=dst, send_sem=send_sems.at[idx],
                                        recv_sem=recv_sems.at[idx], device_id=to, device_id_type=MESH_ID)


def _gather_comm(bufs, ici=True, pair=True):
    n = len(bufs)

    def half(ref, k, pc):
        rh = ref.shape[1] // 2
        return ref.at[k, pl.ds(pc * rh, rh), :]

    def ici_start(ins, outs, ss, rs, base):
        x, y, c = _mesh_pos()
        for a in range(n):
            mine = half(outs[a], 2 * x + y, c)
            for j, chip in enumerate(_other_chips(x, y)):
                _remote(mine, mine, ss, rs, base + 3 * a + j, (*chip, c)).start()

    def ici_finish(ins, outs, ss, rs, base):
        x, y, c = _mesh_pos()
        for a in range(n):
            for j, chip in enumerate(_other_chips(x, y)):
                theirs = half(outs[a], 2 * chip[0] + chip[1], c)
                _remote(theirs, theirs, ss, rs, base + 3 * a + j, (*chip, c)).wait()

    def pair_copy(outs, ss, rs, base, a):
        x, y, c = _mesh_pos()
        rh = outs[a].shape[1] // 2
        held = outs[a].at[:, pl.ds(c * rh, rh), :]
        return _remote(held, held, ss, rs, base + a, (x, y, 1 - c))

    def pair_start(ins, outs, ss, rs, base):
        for a in range(n):
            pair_copy(outs, ss, rs, base, a).start()

    def pair_finish(ins, outs, ss, rs, base):
        for a in range(n):
            pair_copy(outs, ss, rs, base, a).wait()

    parts = ([(3 * n, ici_start, ici_finish)] if ici else []) + ([(n, pair_start, pair_finish)] if pair else [])
    return _Comm(bufs, [_out(b.shape, b.dtype) for b in bufs], {a: a for a in range(n)}, parts)


def _merge_comm(*comms):
    ins, shapes, aliases, subs, base = [], [], {}, [], 0
    for cm in comms:
        (n_sems, start, finish), = cm.parts
        i0, o0 = len(ins), len(shapes)
        subs.append((slice(i0, i0 + len(cm.ins)), slice(o0, o0 + len(cm.out_shape)), base, start, finish))
        aliases.update({i0 + i: o0 + j for i, j in cm.aliases.items()})
        ins += cm.ins
        shapes += cm.out_shape
        base += n_sems

    def start_all(ins_r, outs_r, ss, rs, b):
        for si, so, off, start, _ in subs:
            start(ins_r[si], outs_r[so], ss, rs, b + off)

    def finish_all(ins_r, outs_r, ss, rs, b):
        for si, so, off, _, finish in subs:
            finish(ins_r[si], outs_r[so], ss, rs, b + off)

    return _Comm(ins, shapes, aliases, [(base, start_all, finish_all)])


def _own_half_buffers(pieces, dtypes, kc_arr):
    n = len(pieces)

    def body(kc_ref, *refs):
        for a in range(n):
            refs[n + a][0] = refs[a][...].astype(dtypes[a])

    def half(p):
        return p.shape[0] // 2, p.shape[1]

    return pl.pallas_call(
        body, name="own_halves",
        out_shape=[_out((N_CHIP,) + p.shape, dt) for p, dt in zip(pieces, dtypes)],
        grid_spec=pltpu.PrefetchScalarGridSpec(
            num_scalar_prefetch=1, grid=(1,),
            in_specs=[pl.BlockSpec(half(p), lambda i, kc: (kc[1], 0)) for p in pieces],
            out_specs=[pl.BlockSpec((1,) + half(p), lambda i, kc: (kc[0], kc[1], 0)) for p in pieces]),
        compiler_params=_cp(("arbitrary",), 48),
    )(kc_arr, *_in_hbm(*pieces))


def _exchange_comm(grads):
    n = len(grads)

    def copy(ins, outs, ss, rs, base, a):
        x, y, c = _mesh_pos()
        rh = ins[a].shape[1] // 2
        return _remote(ins[a].at[:, pl.ds((1 - c) * rh, rh), :], outs[a], ss, rs, base + a, (x, y, 1 - c))

    def start(ins, outs, ss, rs, base):
        for a in range(n):
            copy(ins, outs, ss, rs, base, a).start()

    def finish(ins, outs, ss, rs, base):
        for a in range(n):
            copy(ins, outs, ss, rs, base, a).wait()

    shapes = [_out((N_CHIP, g.shape[1] // 2, g.shape[2]), g.dtype) for g in grads]
    return _Comm(grads, shapes, {}, [(n, start, finish)])


def _chip_sums(name, grads, recvs, c_arr):
    n = len(grads)
    halves = [(1, g.shape[1] // 2, g.shape[2]) for g in grads]

    def body(c_ref, *refs):
        for a in range(n):
            refs[2 * n + a][...] = (refs[a][...] + refs[n + a][...]).astype(BF16)

    return pl.pallas_call(
        body, name=name, out_shape=[_out((N_CHIP,) + h[1:], BF16) for h in halves],
        grid_spec=pltpu.PrefetchScalarGridSpec(
            num_scalar_prefetch=1, grid=(N_CHIP,),
            in_specs=[pl.BlockSpec(h, lambda j, c_ref: (j, c_ref[0], 0)) for h in halves] +
                     [pl.BlockSpec(h, lambda j, c_ref: (j, 0, 0)) for h in halves],
            out_specs=[pl.BlockSpec(h, lambda j, c_ref: (j, 0, 0)) for h in halves]),
        compiler_params=_cp(("arbitrary",), 40),
    )(c_arr, *_in_hbm(*grads, *recvs))


def _scatter_comm(sums):
    n = len(sums)

    def copies(ins, outs, ss, rs, base):
        x, y, c = _mesh_pos()
        return [_remote(ins[a].at[2 * chip[0] + chip[1]], outs[a].at[j], ss, rs, base + 3 * a + j, (*chip, c))
                for a in range(n) for j, chip in enumerate(_other_chips(x, y))]

    def start(ins, outs, ss, rs, base):
        for cp in copies(ins, outs, ss, rs, base):
            cp.start()

    def finish(ins, outs, ss, rs, base):
        for cp in copies(ins, outs, ss, rs, base):
            cp.wait()

    shapes = [_out((3,) + s.shape[1:], s.dtype) for s in sums]
    return _Comm(sums, shapes, {}, [(3 * n, start, finish)])


def _scatter_copies(ins, lands, send_sems, recv_sems):
    x, y, c = _mesh_pos()
    return [_remote(ins[a].at[2 * chip[0] + chip[1]], lands[a].at[j], send_sems, recv_sems, 3 * a + j, (*chip, c))
            for a in range(len(ins)) for j, chip in enumerate(_other_chips(x, y))]


def _scatter_start(name, sums):
    n = len(sums)
    lands = [lax.empty((3,) + s.shape[1:], s.dtype) for s in sums]
    hbm = pl.BlockSpec(memory_space=pltpu.HBM)
    sem = pl.BlockSpec(memory_space=pltpu.SEMAPHORE)

    def body(*refs):
        ins, land_refs = refs[:n], refs[n:2 * n]
        send_sems, recv_sems = refs[2 * n], refs[2 * n + 1]
        token = refs[-1]
        for cp in _scatter_copies(ins, land_refs, send_sems, recv_sems):
            cp.start()
        token[...] = jnp.zeros_like(token)

    res = pl.pallas_call(
        body, name=name + "_scatter_start",
        out_shape=(pltpu.SemaphoreType.DMA((3 * n,)), pltpu.SemaphoreType.DMA((3 * n,)),
                   *[pltpu.HBM(s.shape, s.dtype) for s in sums], *[pltpu.HBM(ld.shape, ld.dtype) for ld in lands],
                   jax.ShapeDtypeStruct((8, 128), F32)),
        in_specs=[hbm] * (2 * n), out_specs=(sem, sem, *[hbm] * (2 * n), pl.BlockSpec(memory_space=pltpu.VMEM)),
        input_output_aliases={i: 2 + i for i in range(2 * n)},
        compiler_params=pltpu.CompilerParams(has_side_effects=pltpu.SideEffectType.DATAFLOW_SIDE_EFFECTING),
    )(*[pltpu.with_memory_space_constraint(a, pltpu.HBM) for a in list(sums) + lands])
    return res[0], res[1], list(res[2:2 + n]), list(res[2 + n:2 + 2 * n]), res[-1]


def _scatter_wait(name, send_sems, recv_sems, sums, lands, after):
    n = len(sums)
    hbm = pl.BlockSpec(memory_space=pltpu.HBM)
    sem = pl.BlockSpec(memory_space=pltpu.SEMAPHORE)

    def body(*refs):
        ins, land_refs = refs[:n], refs[n:2 * n]
        for cp in _scatter_copies(ins, land_refs, refs[2 * n], refs[2 * n + 1]):
            cp.wait_send()
            cp.wait_recv()

    res = pl.pallas_call(
        body, name=name + "_scatter_wait",
        out_shape=tuple([pltpu.HBM(s.shape, s.dtype) for s in sums] + [pltpu.HBM(ld.shape, ld.dtype) for ld in lands]),
        in_specs=[hbm] * (2 * n) + [sem, sem, pl.BlockSpec(memory_space=pl.ANY)], out_specs=tuple([hbm] * (2 * n)),
        input_output_aliases={i: i for i in range(2 * n)},
        compiler_params=pltpu.CompilerParams(has_side_effects=pltpu.SideEffectType.DATAFLOW_SIDE_EFFECTING),
    )(*sums, *lands, send_sems, recv_sems, after)
    return list(res[:n]), list(res[n:])


def _total_sums(name, sums, recv3, kc_arr):
    n = len(sums)
    dims = [s.shape[1:] for s in sums]

    def body(kc_ref, *refs):
        for a in range(n):
            s_ref, r_ref = refs[a], refs[n + a]
            t = s_ref[0].astype(F32) + r_ref[0].astype(F32)
            t = t + r_ref[1].astype(F32)
            refs[2 * n + a][...] = t + r_ref[2].astype(F32)

    return pl.pallas_call(
        body, name=name, out_shape=[_out((2 * rh, cc), F32) for rh, cc in dims],
        grid_spec=pltpu.PrefetchScalarGridSpec(
            num_scalar_prefetch=1, grid=(1,),
            in_specs=[pl.BlockSpec((1, rh, cc), lambda i, kc_ref: (kc_ref[0], 0, 0)) for rh, cc in dims] +
                     [pl.BlockSpec((3, rh, cc), lambda i, kc_ref: (0, 0, 0)) for rh, cc in dims],
            out_specs=[pl.BlockSpec((rh, cc), lambda i, kc_ref: (kc_ref[1], 0)) for rh, cc in dims]),
        compiler_params=_cp(("arbitrary",), 48),
    )(kc_arr, *_in_hbm(*sums, *recv3))


def _assemble_comm(totals):
    n = len(totals)

    def copy(outs, ss, rs, base, a):
        x, y, c = _mesh_pos()
        rh = outs[a].shape[0] // 2
        here = outs[a].at[pl.ds(c * rh, rh), :]
        return _remote(here, here, ss, rs, base + a, (x, y, 1 - c))

    def start(ins, outs, ss, rs, base):
        for a in range(n):
            copy(outs, ss, rs, base, a).start()

    def finish(ins, outs, ss, rs, base):
        for a in range(n):
            copy(outs, ss, rs, base, a).wait()

    shapes = [_out(t.shape, t.dtype) for t in totals]
    return _Comm(totals, shapes, {a: a for a in range(n)}, [(n, start, finish)])


def _small_layout(shapes):
    n = len(shapes)
    narrow_w = 64
    wide = [a for a in range(n) if shapes[a][1] > narrow_w]
    narrow = sorted((a for a in range(n) if shapes[a][1] <= narrow_w), key=lambda a: -shapes[a][0])
    offs, cols, groups, widths, rows = {}, {}, [], [], []
    if wide:
        r = 0
        for a in wide:
            offs[a], cols[a] = r, 0
            r += shapes[a][0]
        groups.append(wide)
        widths.append(max(shapes[a][1] for a in wide))
        rows.append(-(-r // 8) * 8)
    if narrow:
        heights = [0, 0]
        for a in narrow:
            side = 0 if heights[0] <= heights[1] else 1
            offs[a], cols[a] = heights[side], side * narrow_w
            heights[side] += shapes[a][0]
        groups.append(narrow)
        widths.append(2 * narrow_w)
        rows.append(-(-max(heights) // 8) * 8)

    def window(ref, a):
        return ref.at[offs[a]:offs[a] + shapes[a][0], cols[a]:cols[a] + shapes[a][1]]

    return groups, widths, rows, window


def _small_pack(arrays, me_arr):
    shapes = [a.shape for a in arrays]
    groups, widths, rows, window = _small_layout(shapes)
    n, n_g = len(arrays), len(groups)

    def body(me_ref, *refs):
        ins, outs = refs[:n], refs[n:]
        for gi, g in enumerate(groups):
            outs[gi][...] = jnp.zeros_like(outs[gi])
            for a in g:
                window(outs[gi].at[0], a)[...] = ins[a][...]

    return pl.pallas_call(
        body, name="small_pack", out_shape=[_out((8, r, w), F32) for r, w in zip(rows, widths)],
        grid_spec=pltpu.PrefetchScalarGridSpec(
            num_scalar_prefetch=1, grid=(1,), in_specs=[pl.BlockSpec(s, lambda i, me: (0, 0)) for s in shapes],
            out_specs=[pl.BlockSpec((1, r, w), lambda i, me: (me[0], 0, 0)) for r, w in zip(rows, widths)]),
        compiler_params=_cp(("arbitrary",), 32),
    )(me_arr, *_in_hbm(*arrays))


def _spread_comm(slots):
    n = len(slots)
    flips = [(dx, dy, dc) for dx in range(2) for dy in range(2) for dc in range(2)][1:]

    def copies(outs, ss, rs, base):
        x, y, c = _mesh_pos()
        mine = 4 * x + 2 * y + c
        return [_remote(outs[a].at[mine], outs[a].at[mine], ss, rs, base + 7 * a + f,
                        (x ^ dx, y ^ dy, c ^ dc)) for a in range(n) for f, (dx, dy, dc) in enumerate(flips)]

    def start(ins, outs, ss, rs, base):
        for cp in copies(outs, ss, rs, base):
            cp.start()

    def finish(ins, outs, ss, rs, base):
        for cp in copies(outs, ss, rs, base):
            cp.wait()

    return _Comm(slots, [_out(s.shape, s.dtype) for s in slots], {a: a for a in range(n)}, [(7 * n, start, finish)])


def _small_total(slots, shapes):
    groups, widths, rows, window = _small_layout(shapes)
    n, n_g = len(shapes), len(groups)

    def body(*refs):
        ins, outs, acc = refs[:n_g], refs[n_g:n_g + n], refs[n_g + n:]
        for gi, g in enumerate(groups):
            t = ins[gi][0] + ins[gi][1]
            for d in range(2, 8):
                t = t + ins[gi][d]
            acc[gi][...] = t
            for a in g:
                outs[a][...] = window(acc[gi], a)[...]

    return pl.pallas_call(
        body, name="small_total", grid=(1,), out_shape=[_out(s, F32) for s in shapes],
        in_specs=[_full(s.shape) for s in slots], out_specs=[_full(s) for s in shapes],
        scratch_shapes=[pltpu.VMEM((r, w), F32) for r, w in zip(rows, widths)],
        compiler_params=_cp(("arbitrary",), 48),
    )(*_in_hbm(*slots))


def _small_allreduce(arrays, comm):
    n = len(arrays)
    shapes = [a.shape for a in arrays]
    groups, widths, rows, window = _small_layout(shapes)
    n_g = len(groups)

    def body(*refs):
        ins, outs = refs[:n], refs[n:2 * n]
        pack, sib, csum, every = (refs[2 * n + i * n_g:2 * n + (i + 1) * n_g] for i in range(4))
        send_sems, recv_sems = refs[2 * n + 4 * n_g:]
        x, y, c = _mesh_pos()
        k = 2 * x + y
        for gi, g in enumerate(groups):
            pack[gi][...] = jnp.zeros_like(pack[gi])
            for a in g:
                window(pack[gi], a)[...] = ins[a][...]
        cps = [_remote(pack[gi], sib[gi], send_sems, recv_sems, gi, (x, y, 1 - c)) for gi in range(n_g)]
        for cp in cps:
            cp.start()
        for cp in cps:
            cp.wait()
        for gi in range(n_g):
            csum[gi][...] = pack[gi][...] + sib[gi][...]
            every[gi][k] = csum[gi][...]
        cps = [_remote(csum[gi], every[gi].at[k], send_sems, recv_sems, n_g + 3 * gi + j, (*chip, c))
               for gi in range(n_g) for j, chip in enumerate(_other_chips(x, y))]
        for cp in cps:
            cp.start()
        for cp in cps:
            cp.wait()
        for gi, g in enumerate(groups):
            pack[gi][...] = ((every[gi][0] + every[gi][1]) + every[gi][2]) + every[gi][3]
            for a in g:
                outs[a][...] = window(pack[gi], a)[...]

    bufs = [pltpu.VMEM((r, w), F32) for r, w in zip(rows, widths)]
    return _call(
        body, comm, (0,), arrays, name="small_allreduce", grid=(1,), out_shape=[_out(s, F32) for s in shapes],
        in_specs=[_full(s) for s in shapes], out_specs=[_full(s) for s in shapes],
        scratch_shapes=bufs * 3 + [pltpu.VMEM((N_CHIP, r, w), F32) for r, w in zip(rows, widths)] +
                       [pltpu.SemaphoreType.DMA((4 * n_g,)), pltpu.SemaphoreType.DMA((4 * n_g,))],
        compiler_params=_cp(("arbitrary",), 40))


def _adamw_small(ws, gs, ms, vs, comm):
    n = len(ws)

    def body(*refs):
        w, g, m, v, d, mo, vo = (refs[i * n:(i + 1) * n] for i in range(7))
        for a in range(n):
            d[a][...], mo[a][...], vo[a][...] = _adamw_math(w[a][...], g[a][...], m[a][...], v[a][...])

    specs = [_full(w.shape) for w in ws]
    res, got = _call(
        body, comm, (0,), (*ws, *gs, *ms, *vs), name="adamw_small", grid=(1,),
        out_shape=[_out(w.shape, F32) for w in ws] * 3,
        in_specs=specs * 4, out_specs=specs * 3, compiler_params=_cp(("arbitrary",), 40))
    return (res[:n], res[n:2 * n], res[2 * n:]), got


def _adamw_math(w, g, m, v):
    m = ADAM_B1 * m + (1.0 - ADAM_B1) * g
    v = ADAM_B2 * v + (1.0 - ADAM_B2) * (g * g)
    m_hat = m / (1.0 - ADAM_B1 ** ADAM_STEP)
    v_hat = v / (1.0 - ADAM_B2 ** ADAM_STEP)
    delta = -ADAM_LR * (m_hat / (jnp.sqrt(v_hat) + ADAM_EPS) + ADAM_WD * w)
    return delta, m, v


def _adamw(name, w, g, m, v):
    r, c = w.shape
    tr = max(t for t in range(8, 513, 8) if r % t == 0)

    def body(w_ref, g_ref, m_ref, v_ref, d_ref, mo_ref, vo_ref):
        d_ref[...], mo_ref[...], vo_ref[...] = _adamw_math(w_ref[...], g_ref[...], m_ref[...], v_ref[...])

    return pl.pallas_call(
        body, name=name, grid=(r // tr,), in_specs=[_rows(tr, c)] * 4, out_specs=[_rows(tr, c)] * 3,
        out_shape=[_out((r, c), F32)] * 3, compiler_params=_cp(("arbitrary",), 32),
    )(*_in_hbm(w, g, m, v))


def _as_matrix(name, a):
    if name == "na_rpb":
        return a[0].transpose(1, 0, 2).reshape(N_HEADS * (2 * KH - 1), 2 * KW - 1)
    if name in ("s5_b_re", "s5_b_im"):
        return a.transpose(0, 1, 2, 4, 3).reshape(2 * S5_G * S5_H, S5_P)
    if name in ("s5_c_re", "s5_c_im"):
        return a.reshape(2 * S5_G * S5_H, S5_P)
    if name in ("s5_lam_re", "s5_lam_im"):
        return a.reshape(2 * S5_G, S5_P)
    if name == "s5_log_dt":
        return a.reshape(2, S5_G)
    return a


def _from_matrix(name, m):
    if name == "na_rpb":
        return m.reshape(2 * KH - 1, N_HEADS, 2 * KW - 1).transpose(1, 0, 2)[None]
    if name in ("s5_b_re", "s5_b_im"):
        return m.reshape(1, 2, S5_G, S5_H, S5_P).transpose(0, 1, 2, 4, 3)
    if name in ("s5_c_re", "s5_c_im"):
        return m.reshape(1, 2, S5_G, S5_H, S5_P)
    if name in ("s5_lam_re", "s5_lam_im"):
        return m.reshape(1, 2, S5_G, S5_P)
    if name == "s5_log_dt":
        return m.reshape(1, 2, S5_G)
    return m


WEIGHTS = ["meta_tokens", "ffn1_pre_g", "ffn1_post_g", "ffn1_w_gate", "ffn1_w_up", "ffn1_w_down", "mix_pre_g", "w_in",
           "na_rpb", "s5_lam_re", "s5_lam_im", "s5_log_dt", "s5_b_re", "s5_b_im", "s5_c_re", "s5_c_im", "s5_d",
           "s5_w_glu", "s5_b_glu", "na_out_g", "s5_out_g", "w_out", "mix_post_g", "ffn2_pre_g", "ffn2_post_g",
           "ffn2_w_gate", "ffn2_w_up", "ffn2_w_down", "final_g"]
BIG = ["ffn1_w_gate", "ffn1_w_up", "ffn1_w_down", "w_in", "s5_w_glu", "w_out", "ffn2_w_gate", "ffn2_w_up",
       "ffn2_w_down"]
TRANSPOSED = ["ffn1_w_gate", "ffn1_w_up", "ffn2_w_gate", "ffn2_w_up"]
GAINS = ["ffn1_pre_g", "ffn1_post_g", "mix_pre_g", "s5_d", "s5_b_glu", "na_out_g", "s5_out_g", "mix_post_g",
         "ffn2_pre_g", "ffn2_post_g", "final_g"]
SMALL = [n for n in WEIGHTS if n not in BIG]


def kernel(*args):
    names = ["x"] + WEIGHTS + ["loss_target"] + ["m_" + n for n in WEIGHTS] + ["v_" + n for n in WEIGHTS]
    assert len(args) == len(names)
    given = dict(zip(names, args))
    x_pos, y_pos, c_pos = _mesh_pos()
    k_pos = 2 * x_pos + y_pos
    c_arr = jnp.reshape(c_pos, (1,)).astype(jnp.int32)
    kc_arr = jnp.stack([k_pos, c_pos]).astype(jnp.int32)

    def piece(name, a):
        return a[0].T if name in TRANSPOSED else a[0]

    def unpiece(name, a):
        return a.T[None] if name in TRANSPOSED else a[None]

    placed = BIG + ["meta_tokens"]
    bufs = dict(zip(placed, _own_half_buffers([piece(n, given[n]) for n in BIG] + [given["meta_tokens"]],
                                              [BF16] * len(BIG) + [F32], kc_arr)))

    gains = {n: given[n] for n in GAINS}
    s5 = {n: _as_matrix("s5_" + n, given["s5_" + n])
          for n in ["lam_re", "lam_im", "log_dt", "b_re", "b_im", "c_re", "c_im"]}
    me_arr = jnp.reshape(4 * x_pos + 2 * y_pos + c_pos, (1,)).astype(jnp.int32)
    loss, dh0, pieces, small, late, (ffn1, flight1), (mid, flight2) = _step(
        given["x"][0], given["loss_target"][0], bufs, gains, s5, given["na_rpb"][0], c_arr, kc_arr, me_arr)
    loss = lax.psum(loss, ("x", "y", "c"))
    n_tok = given["x"].shape[1]
    grad_x = dh0[N_META:N_META + n_tok][None]

    late["meta_tokens"] = dh0[:N_META]
    out_g, out_d, out_m, out_v = {}, {}, {}, {}

    def update_big(n):
        g2 = pieces[n]
        d2, m2, v2 = _adamw("adamw_" + n, piece(n, given[n]), g2, piece(n, given["m_" + n]),
                            piece(n, given["v_" + n]))
        out_g[n], out_d[n], out_m[n], out_v[n] = (unpiece(n, t) for t in (g2, d2, m2, v2))
        return v2

    done2 = [update_big(n) for n in pieces]
    sums1, recv1 = _scatter_wait("ffn1", *flight1, done2[-1])
    totals1 = _total_sums("total_sums_ffn1", sums1, recv1, kc_arr)
    pieces.update(zip(ffn1, _run_comm("ffn1_pair_assemble", _assemble_comm(totals1))))
    done1 = [update_big(n) for n in ffn1]
    late_arrays = list(late.values())
    late_arrays[0], _ = lax.optimization_barrier((late_arrays[0], (done1[-1], small["final_g"])))
    red, _ = _small_allreduce(late_arrays, None)
    small.update(zip(late, red))
    mc = D // N_CHIP
    small["meta_tokens"] = lax.dynamic_slice_in_dim(small["meta_tokens"], k_pos * mc, mc, 1)
    sums_rest, recv_rest = _scatter_wait("rest", *flight2, red[0])
    totals = _total_sums("total_sums_rest", sums_rest, recv_rest, kc_arr)
    gs = [small[n] for n in SMALL]
    (d2, m2, v2), done = _adamw_small([_as_matrix(n, given[n]) for n in SMALL], gs,
                                      [_as_matrix(n, given["m_" + n]) for n in SMALL],
                                      [_as_matrix(n, given["v_" + n]) for n in SMALL], _assemble_comm(totals))
    pieces.update(zip(mid, done))

    for n, g, dd, mm, vv in zip(SMALL, gs, d2, m2, v2):
        out_g[n], out_d[n], out_m[n], out_v[n] = (_from_matrix(n, t) for t in (g, dd, mm, vv))
    for n in mid:
        update_big(n)
    return (loss, grad_x, *[out_g[n] for n in WEIGHTS], *[out_d[n] for n in WEIGHTS],
            *[out_m[n] for n in WEIGHTS], *[out_v[n] for n in WEIGHTS])
```

```python
import math

import numpy as np
import jax
import jax.numpy as jnp
from jax import lax
from jax.experimental import pallas as pl
from jax.experimental.pallas import tpu as pltpu

F32 = jnp.float32
BF16 = jnp.bfloat16

D = 1024
N_META = 16
GRID_W = 64
NA_W = 512
S5_W = 512
HEAD_DIM = 64
N_HEADS = 8
KH = 8
KW = 16
S5_G = 32
S5_P = 64
S5_H = 16
N_BUNDLE = 4
FF = 2816
N_CHIP = 4
FC = FF // N_CHIP
EPS = 1e-6
NEG_INF = -1e30
Q_ROWS = 4
K_ROWS = 12
QB = Q_ROWS * GRID_W
KB = K_ROWS * GRID_W
SCAN_CHUNK = 256

ADAM_LR = 0.001
ADAM_B1 = 0.9
ADAM_B2 = 0.999
ADAM_EPS = 1e-08
ADAM_WD = 0.01
ADAM_STEP = 10

NT = (((1,), (1,)), ((), ()))
TN = (((0,), (0,)), ((), ()))
MESH_ID = pl.DeviceIdType.MESH


def _cp(sem=None, vmem_mb=None):
    kw = {}
    if sem is not None:
        kw["dimension_semantics"] = sem
    if vmem_mb is not None:
        kw["vmem_limit_bytes"] = vmem_mb << 20
    return pltpu.CompilerParams(**kw)


def _full(shape):
    n = len(shape)
    return pl.BlockSpec(shape, lambda *_: (0,) * n)


def _rows(tm, w):
    return pl.BlockSpec((tm, w), lambda i: (i, 0))


ANY = pl.BlockSpec(memory_space=pl.ANY)


def _rms(x, g):
    r = lax.rsqrt(jnp.mean(x * x, axis=-1, keepdims=True) + EPS)
    return x * r * g


def _rms_bwd(x, g, dy):
    r = lax.rsqrt(jnp.mean(x * x, axis=-1, keepdims=True) + EPS)
    xh = x * r
    dg = jnp.sum(dy * xh, axis=0, keepdims=True)
    dyg = dy * g
    dx = r * (dyg - xh * jnp.mean(dyg * xh, axis=-1, keepdims=True))
    return dx, dg


def _out(shape, dtype):
    return pltpu.HBM(tuple(shape), dtype)


def _in_hbm(*args):
    return [pltpu.with_memory_space_constraint(a, pltpu.HBM) if jnp.issubdtype(a.dtype, jnp.floating) and a.ndim > 1
            else a for a in args]


def _dot(a, b):
    return jnp.dot(a, b, preferred_element_type=F32)


def _dg(a, b, dims):
    return lax.dot_general(a, b, dims, preferred_element_type=F32)


def _ffn_fwd(name, h, g_pre, g_post, wg, wu, wd, tm, comm=None, bounds=()):
    tp = h.shape[0]
    nt = tp // tm

    def body(h_ref, gp_ref, gq_ref, wg_ref, wu_ref, wd_ref, hn_ref, gate_ref, up_ref, f_ref, xn_s, acc_s):
        c = pl.program_id(1)

        @pl.when(c == 0)
        def _():
            xn_s[...] = _rms(h_ref[...], gp_ref[...]).astype(BF16)
            acc_s[...] = jnp.zeros_like(acc_s)

        xn = xn_s[...]
        gate = _dg(xn, wg_ref[0], NT)
        up = _dg(xn, wu_ref[0], NT)
        gate_ref[0] = gate
        up_ref[0] = up
        act = (gate * jax.nn.sigmoid(gate) * up).astype(BF16)
        acc_s[...] += _dot(act, wd_ref[0])

        @pl.when(c == N_CHIP - 1)
        def _():
            f = acc_s[...]
            f_ref[...] = f
            hn_ref[...] = h_ref[...] + 0.5 * _rms(f, gq_ref[...])

    return _call(
        body, comm, bounds, (h, g_pre, g_post, wg, wu, wd), name=name, grid=(nt, N_CHIP),
        in_specs=[pl.BlockSpec((tm, D), lambda i, c: (i, 0)), _full((1, D)), _full((1, D))] +
                 [pl.BlockSpec((1, FC, D), lambda i, c: (c, 0, 0))] * 3,
        out_specs=[pl.BlockSpec((tm, D), lambda i, c: (i, 0)),
                   pl.BlockSpec((1, tm, FC), lambda i, c: (c, i, 0)),
                   pl.BlockSpec((1, tm, FC), lambda i, c: (c, i, 0)),
                   pl.BlockSpec((tm, D), lambda i, c: (i, 0))],
        out_shape=[_out((tp, D), F32), _out((N_CHIP, tp, FC), F32),
                   _out((N_CHIP, tp, FC), F32), _out((tp, D), F32)],
        scratch_shapes=[pltpu.VMEM((tm, D), BF16), pltpu.VMEM((tm, D), F32)],
        compiler_params=_cp(("arbitrary", "arbitrary"), 48))


def _ffn_bwd(name, h, g_pre, df, gate, up, wg, wu, wd, tm, comm=None, bounds=()):
    tp = h.shape[0]
    nt = tp // tm
    rh = FC // 2

    def body(h_ref, gp_ref, df_ref, gate_ref, up_ref, wg_ref, wu_ref, wd_ref,
             dwg_ref, dwu_ref, dwd_ref, dxn_ref, rg_ref, ru_ref, rd_ref, ag, au, ad, send_sems, recv_sems):
        c = pl.program_id(0)
        i = pl.program_id(1)

        def to_sibling(a, piece):
            x, y, core = _mesh_pos()
            dw_ref, r_ref = ((dwg_ref, rg_ref), (dwu_ref, ru_ref), (dwd_ref, rd_ref))[a]
            return _remote(dw_ref.at[piece, pl.ds((1 - core) * rh, rh), :], r_ref.at[piece], send_sems, recv_sems,
                           3 * piece + a, (x, y, 1 - core))

        @pl.when(i == 0)
        def _():
            ag[...] = jnp.zeros_like(ag)
            au[...] = jnp.zeros_like(au)
            ad[...] = jnp.zeros_like(ad)

        xn = _rms(h_ref[...], gp_ref[...]).astype(BF16)
        dfb = df_ref[...].astype(BF16)
        gt = gate_ref[0]
        u = up_ref[0]
        sg = jax.nn.sigmoid(gt)
        si = gt * sg
        act = (si * u).astype(BF16)
        dact = _dg(dfb, wd_ref[0], NT)
        ad[...] += _dg(act, dfb, TN)
        dgate = (dact * u * (sg * (1.0 + gt * (1.0 - sg)))).astype(BF16)
        dup = (dact * si).astype(BF16)
        ag[...] += _dg(dgate, xn, TN)
        au[...] += _dg(dup, xn, TN)
        dxn_ref[0] = _dot(dgate, wg_ref[0]) + _dot(dup, wu_ref[0])

        @pl.when(i == nt - 1)
        def _():
            pltpu.sync_copy(ag, dwg_ref.at[c])
            pltpu.sync_copy(au, dwu_ref.at[c])
            pltpu.sync_copy(ad, dwd_ref.at[c])
            for a in range(3):
                to_sibling(a, c).start()

        @pl.when((c == N_CHIP - 1) & (i == nt - 1))
        def _():
            for piece in range(N_CHIP):
                for a in range(3):
                    to_sibling(a, piece).wait()

    return _call(
        body, comm, bounds, (h, g_pre, df, gate, up, wg, wu, wd), name=name, grid=(N_CHIP, nt),
        in_specs=[pl.BlockSpec((tm, D), lambda c, i: (i, 0)), _full((1, D)),
                  pl.BlockSpec((tm, D), lambda c, i: (i, 0)),
                  pl.BlockSpec((1, tm, FC), lambda c, i: (c, i, 0)),
                  pl.BlockSpec((1, tm, FC), lambda c, i: (c, i, 0))] +
                 [pl.BlockSpec((1, FC, D), lambda c, i: (c, 0, 0))] * 3,
        out_specs=[ANY, ANY, ANY, pl.BlockSpec((1, tm, D), lambda c, i: (c, i, 0)), ANY, ANY, ANY],
        out_shape=[_out((N_CHIP, FC, D), F32)] * 3 + [_out((N_CHIP, tp, D), F32)] +
                  [_out((N_CHIP, rh, D), F32)] * 3,
        scratch_shapes=[pltpu.VMEM((FC, D), F32)] * 3 +
                       [pltpu.SemaphoreType.DMA((3 * N_CHIP,)), pltpu.SemaphoreType.DMA((3 * N_CHIP,))],
        compiler_params=_cp(("arbitrary", "arbitrary"), 58))


def _ffn_pre_bwd(name, dh, dxn_part, h, g_pre, tm, comm=None, bounds=()):
    tp = h.shape[0]
    nt = tp // tm

    def body(dh_ref, dxn_ref, h_ref, gp_ref, out_ref, dg_ref):
        i = pl.program_id(0)
        dxn = (dxn_ref[0] + dxn_ref[1]) + (dxn_ref[2] + dxn_ref[3])
        dx, dg = _rms_bwd(h_ref[...], gp_ref[...], dxn)
        out_ref[...] = dh_ref[...] + dx

        @pl.when(i == 0)
        def _():
            dg_ref[...] = jnp.zeros_like(dg_ref)

        dg_ref[...] += dg

    return _call(
        body, comm, bounds, (dh, dxn_part, h, g_pre), name=name, grid=(nt,),
        in_specs=[_rows(tm, D), pl.BlockSpec((N_CHIP, tm, D), lambda i: (0, i, 0)), _rows(tm, D), _full((1, D))],
        out_specs=[_rows(tm, D), _full((1, D))],
        out_shape=[_out((tp, D), F32), _out((1, D), F32)],
        compiler_params=_cp(("arbitrary",), 48))


def _mix_in(h, g, w_in, tm):
    tp = h.shape[0]

    def body(h_ref, g_ref, w_ref, q_ref, k_ref, v_ref, u_ref):
        a = _rms(h_ref[...], g_ref[...]).astype(BF16)
        q_ref[...] = _dot(a, w_ref[0]).astype(BF16)
        k_ref[...] = _dot(a, w_ref[1]).astype(BF16)
        v_ref[...] = _dot(a, w_ref[2]).astype(BF16)
        u_ref[...] = _dot(a, w_ref[3])

    return pl.pallas_call(
        body, name="mix_in", grid=(tp // tm,),
        in_specs=[_rows(tm, D), _full((1, D)), _full((N_CHIP, D, NA_W))],
        out_specs=[_rows(tm, NA_W)] * 4,
        out_shape=[_out((tp, NA_W), BF16)] * 3 + [_out((tp, S5_W), F32)],
        compiler_params=_cp(("arbitrary",), 40),
    )(*_in_hbm(h, g, w_in))


def _gelu(x):
    return jax.nn.gelu(x, approximate=True)


def _gelu_grad(x):
    k = math.sqrt(2.0 / math.pi)
    t = jnp.tanh(k * (x + 0.044715 * x * x * x))
    return 0.5 * (1.0 + t) + 0.5 * x * (1.0 - t * t) * k * (1.0 + 3.0 * 0.044715 * x * x)


def _mix_out(o_na, y_pre, h, w_glu, b_glu, g_na, g_s5, w_out, g_post, tm, comm=None, bounds=()):
    tp = h.shape[0]

    def body(ona_ref, yp_ref, h_ref, wglu_ref, bglu_ref, gna_ref, gs5_ref, wout_ref, gpost_ref, hn_ref, mix_ref):
        y = _gelu(yp_ref[...])
        z = _dot(y.astype(BF16), wglu_ref[...]) + bglu_ref[...]
        o_s5 = y * jax.nn.sigmoid(z)
        n1 = _rms(ona_ref[...], gna_ref[...]).astype(BF16)
        n2 = _rms(o_s5, gs5_ref[...]).astype(BF16)
        mix = _dot(n1, wout_ref[0:NA_W, :]) + _dot(n2, wout_ref[NA_W:, :])
        mix_ref[...] = mix
        hn_ref[...] = h_ref[...] + _rms(mix, gpost_ref[...])

    return _call(
        body, comm, bounds, (o_na, y_pre, h, w_glu, b_glu, g_na, g_s5, w_out, g_post), name="mix_out",
        grid=(tp // tm,),
        in_specs=[_rows(tm, NA_W), _rows(tm, S5_W), _rows(tm, D), _full((S5_W, S5_W)), _full((1, S5_W)),
                  _full((1, NA_W)), _full((1, S5_W)), _full((D, D)), _full((1, D))],
        out_specs=[_rows(tm, D), _rows(tm, D)],
        out_shape=[_out((tp, D), F32)] * 2,
        compiler_params=_cp(("arbitrary",), 40))


def _mix_out_bwd(dh, mix, o_na, y_pre, w_glu, b_glu, g_na, g_s5, w_out, g_post, tm):
    tp = dh.shape[0]
    nt = tp // tm

    def body(dh_ref, mix_ref, ona_ref, yp_ref, wglu_ref, bglu_ref, gna_ref, gs5_ref, wout_ref, gpost_ref,
             dona_ref, dyp_ref, dwout_ref, dwglu_ref, dgpost_ref, dgna_ref, dgs5_ref, dbglu_ref, a_out, a_glu):
        i = pl.program_id(0)

        @pl.when(i == 0)
        def _():
            a_out[...] = jnp.zeros_like(a_out)
            a_glu[...] = jnp.zeros_like(a_glu)
            dgpost_ref[...] = jnp.zeros_like(dgpost_ref)
            dgna_ref[...] = jnp.zeros_like(dgna_ref)
            dgs5_ref[...] = jnp.zeros_like(dgs5_ref)
            dbglu_ref[...] = jnp.zeros_like(dbglu_ref)

        dmix, dgpost = _rms_bwd(mix_ref[...], gpost_ref[...], dh_ref[...])
        dgpost_ref[...] += dgpost
        yp = yp_ref[...]
        y = _gelu(yp)
        yb = y.astype(BF16)
        z = _dot(yb, wglu_ref[...]) + bglu_ref[...]
        sg = jax.nn.sigmoid(z)
        o_s5 = y * sg
        o_na = ona_ref[...]
        n1 = _rms(o_na, gna_ref[...]).astype(BF16)
        n2 = _rms(o_s5, gs5_ref[...]).astype(BF16)
        dmb = dmix.astype(BF16)
        a_out[0:NA_W, :] += _dg(n1, dmb, TN)
        a_out[NA_W:, :] += _dg(n2, dmb, TN)
        dn1 = _dg(dmb, wout_ref[0:NA_W, :], NT)
        dn2 = _dg(dmb, wout_ref[NA_W:, :], NT)
        dona, dgna = _rms_bwd(o_na, gna_ref[...], dn1)
        dona_ref[...] = dona
        dgna_ref[...] += dgna
        dos5, dgs5 = _rms_bwd(o_s5, gs5_ref[...], dn2)
        dgs5_ref[...] += dgs5
        dz = dos5 * y * (sg * (1.0 - sg))
        dbglu_ref[...] += jnp.sum(dz, axis=0, keepdims=True)
        dzb = dz.astype(BF16)
        a_glu[...] += _dg(yb, dzb, TN)
        dy = dos5 * sg + _dg(dzb, wglu_ref[...], NT)
        dyp_ref[...] = dy * _gelu_grad(yp)

        @pl.when(i == nt - 1)
        def _():
            pltpu.sync_copy(a_out, dwout_ref)
            pltpu.sync_copy(a_glu, dwglu_ref)

    return pl.pallas_call(
        body, name="mix_out_bwd", grid=(nt,),
        in_specs=[_rows(tm, D), _rows(tm, D), _rows(tm, NA_W), _rows(tm, S5_W), _full((S5_W, S5_W)),
                  _full((1, S5_W)), _full((1, NA_W)), _full((1, S5_W)), _full((D, D)), _full((1, D))],
        out_specs=[_rows(tm, NA_W), _rows(tm, S5_W), ANY, ANY, _full((1, D)), _full((1, NA_W)),
                   _full((1, S5_W)), _full((1, S5_W))],
        out_shape=[_out((tp, NA_W), F32), _out((tp, S5_W), F32),
                   _out((D, D), F32), _out((S5_W, S5_W), F32),
                   _out((1, D), F32), _out((1, NA_W), F32),
                   _out((1, S5_W), F32), _out((1, S5_W), F32)],
        scratch_shapes=[pltpu.VMEM((D, D), F32), pltpu.VMEM((S5_W, S5_W), F32)],
        compiler_params=_cp(("arbitrary",), 48),
    )(*_in_hbm(dh, mix, o_na, y_pre, w_glu, b_glu, g_na, g_s5, w_out, g_post))


def _mix_in_bwd(dq, dk, dv, du, h, g, w_in, dh, f1, g_post1, tm, comm=None, bounds=()):
    tp = h.shape[0]
    nt = tp // tm

    def body(dq_ref, dk_ref, dv_ref, du_ref, h_ref, g_ref, w_ref, dh_ref, f_ref, gq_ref,
             dh1_ref, df_ref, dw_ref, dg_ref, dgq_ref, acc):
        i = pl.program_id(0)

        @pl.when(i == 0)
        def _():
            acc[...] = jnp.zeros_like(acc)
            dg_ref[...] = jnp.zeros_like(dg_ref)
            dgq_ref[...] = jnp.zeros_like(dgq_ref)

        x = h_ref[...]
        a = _rms(x, g_ref[...]).astype(BF16)
        da = jnp.zeros((tm, D), F32)
        for j, r in enumerate((dq_ref, dk_ref, dv_ref, du_ref)):
            dp = r[...].astype(BF16)
            da = da + _dg(dp, w_ref[j], NT)
            acc[j] += _dg(a, dp, TN)
        dx, dg = _rms_bwd(x, g_ref[...], da)
        dh1 = dh_ref[...] + dx
        dh1_ref[...] = dh1
        dg_ref[...] += dg
        df, dgq = _rms_bwd(f_ref[...], gq_ref[...], 0.5 * dh1)
        df_ref[...] = df
        dgq_ref[...] += dgq

        @pl.when(i == nt - 1)
        def _():
            pltpu.sync_copy(acc, dw_ref)

    return _call(
        body, comm, bounds, (dq, dk, dv, du, h, g, w_in, dh, f1, g_post1), name="mix_in_bwd", grid=(nt,),
        in_specs=[_rows(tm, NA_W)] * 4 + [_rows(tm, D), _full((1, D)), _full((N_CHIP, D, NA_W)), _rows(tm, D),
                                         _rows(tm, D), _full((1, D))],
        out_specs=[_rows(tm, D), _rows(tm, D), ANY, _full((1, D)), _full((1, D))],
        out_shape=[_out((tp, D), F32), _out((tp, D), F32),
                   _out((N_CHIP, D, NA_W), F32), _out((1, D), F32),
                   _out((1, D), F32)],
        scratch_shapes=[pltpu.VMEM((N_CHIP, D, NA_W), F32)],
        compiler_params=_cp(("arbitrary",), 48))


def _final_loss(h, g_final, target, f2, g_post2, n_tok, tm):
    tp = h.shape[0]

    def body(h_ref, g_ref, t_ref, f_ref, gq_ref, dh_ref, df_ref, loss_ref, dg_ref, dgq_ref):
        i = pl.program_id(0)

        @pl.when(i == 0)
        def _():
            loss_ref[...] = jnp.zeros_like(loss_ref)
            dg_ref[...] = jnp.zeros_like(dg_ref)
            dgq_ref[...] = jnp.zeros_like(dgq_ref)

        x = h_ref[...]
        y = _rms(x, g_ref[...])
        row = i * tm + lax.broadcasted_iota(jnp.int32, (tm, 1), 0)
        valid = (row >= N_META) & (row < N_META + n_tok)
        e = jnp.where(valid, y - t_ref[...], 0.0)
        loss_ref[...] += 0.5 * jnp.sum(jnp.mean(e * e, axis=-1, keepdims=True), axis=0, keepdims=True)
        dx, dg = _rms_bwd(x, g_ref[...], e * (1.0 / D))
        dh_ref[...] = dx
        dg_ref[...] += dg
        df, dgq = _rms_bwd(f_ref[...], gq_ref[...], 0.5 * dx)
        df_ref[...] = df
        dgq_ref[...] += dgq

    return pl.pallas_call(
        body, name="final_loss", grid=(tp // tm,),
        in_specs=[_rows(tm, D), _full((1, D)), _rows(tm, D), _rows(tm, D), _full((1, D))],
        out_specs=[_rows(tm, D), _rows(tm, D), _full((1, 1)), _full((1, D)), _full((1, D))],
        out_shape=[_out((tp, D), F32), _out((tp, D), F32),
                   _out((1, 1), F32), _out((1, D), F32),
                   _out((1, D), F32)],
        compiler_params=_cp(("arbitrary",), 40),
    )(*_in_hbm(h, g_final, target, f2, g_post2))


def _na_patterns(n_rows):
    pats = []
    for kind in range(3):
        pat = [[-1] * K_ROWS for _ in range(Q_ROWS)]
        for i in range(Q_ROWS):
            for jj in range(K_ROWS):
                if kind == 0 and jj < KH:
                    pat[i][jj] = jj - i + KH - 1
                elif kind == 1 and i <= jj < i + KH:
                    pat[i][jj] = jj - i + 3
                elif kind == 2 and K_ROWS - KH <= jj:
                    pat[i][jj] = jj - i - 1
        pats.append(pat)
    return pats


def _diag_onehot():
    q = np.arange(GRID_W)[:, None]
    kc = np.arange(GRID_W)[None, :]
    start = np.clip(q - KW // 2, 0, GRID_W - KW)
    col_in = (kc >= start) & (kc < start + KW)
    e = np.zeros((32, GRID_W, GRID_W), np.float32)
    for d in range(2 * KW - 1):
        e[d] = ((kc - q + KW - 1) == d) & col_in
    return e.reshape(32, GRID_W * GRID_W), col_in


def _rpb_collapse(dtb2, et):
    def body(d_ref, e_ref, o_ref):
        o_ref[...] = jnp.dot(d_ref[...], e_ref[...], preferred_element_type=F32, precision=lax.Precision.HIGHEST)

    out = (dtb2.shape[0], et.shape[1])
    return pl.pallas_call(
        body, name="rpb_collapse", grid=(1,), out_shape=_out(out, F32),
        in_specs=[_full(dtb2.shape), _full(et.shape)], out_specs=_full(out),
    )(*_in_hbm(dtb2, et))


def _bias_tables(rpb, n_rows, comm=None, bounds=()):
    n_dr, n_dc = 2 * KH - 1, 2 * KW - 1
    pats = _na_patterns(n_rows)

    def body(rpb_ref, o_ref):
        h = pl.program_id(0)
        q = lax.broadcasted_iota(jnp.int32, (GRID_W, GRID_W), 0)
        kc = lax.broadcasted_iota(jnp.int32, (GRID_W, GRID_W), 1)
        start = jnp.clip(q - KW // 2, 0, GRID_W - KW)
        col_in = (kc >= start) & (kc < start + KW)
        diff = kc - q + (KW - 1)
        neg = jnp.full((GRID_W, GRID_W), NEG_INF, F32)
        band = []
        for dr in range(n_dr):
            acc = neg
            for d in range(n_dc):
                acc = jnp.where((diff == d) & col_in, rpb_ref[(h * n_dr + dr) * n_dc + d], acc)
            band.append(acc)
        for kind, pat in enumerate(pats):
            for i in range(Q_ROWS):
                for jj in range(K_ROWS):
                    o_ref[kind, 0, i * GRID_W:(i + 1) * GRID_W, jj * GRID_W:(jj + 1) * GRID_W] = (
                        band[pat[i][jj]] if pat[i][jj] >= 0 else neg)

    (bias,), got = _call(
        body, comm, bounds, (rpb.reshape(-1),), name="bias_tables", grid=(N_HEADS,),
        in_specs=[pl.BlockSpec(memory_space=pltpu.SMEM)],
        out_specs=[pl.BlockSpec((3, 1, QB, KB), lambda h: (0, h, 0, 0))],
        out_shape=[_out((3, N_HEADS, QB, KB), F32)],
        compiler_params=_cp(("arbitrary",), 32))
    return bias, got


def _attn_geometry(n_tok):
    n_rows = n_tok // GRID_W
    assert n_rows % Q_ROWS == 0 and n_rows >= K_ROWS
    return n_rows, n_rows // Q_ROWS


def _attn_probs(qh, kh, kmh, bias, scale):
    s = _dg(qh, kh, NT) * scale + bias
    sm = _dg(qh, kmh, NT) * scale
    m = jnp.maximum(jnp.max(s, axis=-1, keepdims=True), jnp.max(sm, axis=-1, keepdims=True))
    p = jnp.exp(s - m)
    pm = jnp.exp(sm - m)
    inv = 1.0 / (jnp.sum(p, axis=-1, keepdims=True) + jnp.sum(pm, axis=-1, keepdims=True))
    return p * inv, pm * inv


def _meta_probs(qmh, kmh, scale):
    s = _dg(qmh, kmh, NT) * scale
    p = jnp.exp(s - jnp.max(s, axis=-1, keepdims=True))
    return p / jnp.sum(p, axis=-1, keepdims=True)


def _step_rows(r, n_rows):
    q0 = pl.multiple_of(N_META + r * QB, 16)
    k0 = pl.multiple_of(N_META + jnp.clip(Q_ROWS * r - (K_ROWS - KH), 0, n_rows - K_ROWS) * GRID_W, 16)
    return q0, k0


def _attn_fwd(q, k, v, bias, n_tok, comm=None, bounds=()):
    tp = q.shape[0]
    n_rows, n_steps = _attn_geometry(n_tok)
    scale = HEAD_DIM ** -0.5

    def body(q_ref, k_ref, v_ref, b_ref, o_ref):
        r = pl.program_id(1)
        km = k_ref[0:N_META, :]
        vm = v_ref[0:N_META, :]

        @pl.when(r == 0)
        def _():
            qm = q_ref[0:N_META, :]
            outs = []
            for hh in range(2):
                sl = slice(hh * HEAD_DIM, (hh + 1) * HEAD_DIM)
                p = _meta_probs(qm[:, sl], km[:, sl], scale)
                outs.append(_dot(p.astype(BF16), vm[:, sl]))
            o_ref[0:N_META, :] = jnp.concatenate(outs, axis=1)
            o_ref[N_META + n_tok:, :] = jnp.zeros((tp - N_META - n_tok, 2 * HEAD_DIM), F32)

        q0, k0 = _step_rows(r, n_rows)
        qb = q_ref[pl.ds(q0, QB), :]
        kb = k_ref[pl.ds(k0, KB), :]
        vb = v_ref[pl.ds(k0, KB), :]
        outs = []
        for hh in range(2):
            sl = slice(hh * HEAD_DIM, (hh + 1) * HEAD_DIM)
            p, pm = _attn_probs(qb[:, sl], kb[:, sl], km[:, sl], b_ref[0, hh], scale)
            outs.append(_dot(p.astype(BF16), vb[:, sl]) + _dot(pm.astype(BF16), vm[:, sl]))
        o_ref[pl.ds(q0, QB), :] = jnp.concatenate(outs, axis=1)

    def bias_map(hp, r):
        return (jnp.where(r == 0, 0, jnp.where(r == n_steps - 1, 2, 1)), hp, 0, 0)

    col = pl.BlockSpec((tp, 2 * HEAD_DIM), lambda hp, r: (0, hp))
    return _call(
        body, comm, bounds, (q, k, v, bias), name="attn_fwd", grid=(N_HEADS // 2, n_steps),
        in_specs=[col, col, col, pl.BlockSpec((1, 2, QB, KB), bias_map)],
        out_specs=[col], out_shape=[_out((tp, NA_W), F32)],
        compiler_params=_cp(("arbitrary", "arbitrary"), 40))


def _attn_bwd(q, k, v, bias, do, n_tok, comm=None, bounds=()):
    tp = q.shape[0]
    n_rows, n_steps = _attn_geometry(n_tok)
    scale = HEAD_DIM ** -0.5
    pats = _na_patterns(n_rows)

    def body(q_ref, k_ref, v_ref, b_ref, do_ref, dq_ref, dk_ref, dv_ref, dtb_ref):
        r = pl.program_id(1)
        km = k_ref[0:N_META, :]
        vm = v_ref[0:N_META, :]

        @pl.when(r == 0)
        def _():
            dk_ref[...] = jnp.zeros_like(dk_ref)
            dv_ref[...] = jnp.zeros_like(dv_ref)
            dtb_ref[...] = jnp.zeros_like(dtb_ref)
            dq_ref[N_META + n_tok:, :] = jnp.zeros((tp - N_META - n_tok, 2 * HEAD_DIM), F32)
            qm = q_ref[0:N_META, :]
            dom = do_ref[0:N_META, :].astype(BF16)
            dqs, dks, dvs = [], [], []
            for hh in range(2):
                sl = slice(hh * HEAD_DIM, (hh + 1) * HEAD_DIM)
                p = _meta_probs(qm[:, sl], km[:, sl], scale)
                dp = _dg(dom[:, sl], vm[:, sl], NT)
                ds = (p * (dp - jnp.sum(dp * p, axis=-1, keepdims=True))).astype(BF16)
                dvs.append(_dg(p.astype(BF16), dom[:, sl], TN))
                dqs.append(_dot(ds, km[:, sl]) * scale)
                dks.append(_dg(ds, qm[:, sl], TN) * scale)
            dq_ref[0:N_META, :] = jnp.concatenate(dqs, axis=1)
            dk_ref[0:N_META, :] += jnp.concatenate(dks, axis=1)
            dv_ref[0:N_META, :] += jnp.concatenate(dvs, axis=1)

        q0, k0 = _step_rows(r, n_rows)
        qb = q_ref[pl.ds(q0, QB), :]
        kb = k_ref[pl.ds(k0, KB), :]
        vb = v_ref[pl.ds(k0, KB), :]
        dob = do_ref[pl.ds(q0, QB), :].astype(BF16)
        dqs, dks, dvs, dkms, dvms, dss = [], [], [], [], [], []
        for hh in range(2):
            sl = slice(hh * HEAD_DIM, (hh + 1) * HEAD_DIM)
            qh, kh, vh, kmh, vmh, doh = qb[:, sl], kb[:, sl], vb[:, sl], km[:, sl], vm[:, sl], dob[:, sl]
            p, pm = _attn_probs(qh, kh, kmh, b_ref[0, hh], scale)
            dp = _dg(doh, vh, NT)
            dpm = _dg(doh, vmh, NT)
            delta = jnp.sum(dp * p, axis=-1, keepdims=True) + jnp.sum(dpm * pm, axis=-1, keepdims=True)
            ds = p * (dp - delta)
            dsb = ds.astype(BF16)
            dsmb = (pm * (dpm - delta)).astype(BF16)
            dss.append(ds)
            dvs.append(_dg(p.astype(BF16), doh, TN))
            dvms.append(_dg(pm.astype(BF16), doh, TN))
            dqs.append((_dot(dsb, kh) + _dot(dsmb, kmh)) * scale)
            dks.append(_dg(dsb, qh, TN) * scale)
            dkms.append(_dg(dsmb, qh, TN) * scale)
        dq_ref[pl.ds(q0, QB), :] = jnp.concatenate(dqs, axis=1)
        dk_ref[pl.ds(k0, KB), :] += jnp.concatenate(dks, axis=1)
        dv_ref[pl.ds(k0, KB), :] += jnp.concatenate(dvs, axis=1)
        dk_ref[0:N_META, :] += jnp.concatenate(dkms, axis=1)
        dv_ref[0:N_META, :] += jnp.concatenate(dvms, axis=1)

        def add_bias_grad(pat):
            for hh in range(2):
                for i in range(Q_ROWS):
                    for jj in range(K_ROWS):
                        if pat[i][jj] >= 0:
                            dtb_ref[hh, pat[i][jj]] += dss[hh][i * GRID_W:(i + 1) * GRID_W,
                                                               jj * GRID_W:(jj + 1) * GRID_W]

        @pl.when(r == 0)
        def _():
            add_bias_grad(pats[0])

        @pl.when((r > 0) & (r < n_steps - 1))
        def _():
            add_bias_grad(pats[1])

        @pl.when(r == n_steps - 1)
        def _():
            add_bias_grad(pats[2])

    def bias_map(hp, r):
        return (jnp.where(r == 0, 0, jnp.where(r == n_steps - 1, 2, 1)), hp, 0, 0)

    col = pl.BlockSpec((tp, 2 * HEAD_DIM), lambda hp, r: (0, hp))
    n_dr = 2 * KH - 1
    return _call(
        body, comm, bounds, (q, k, v, bias, do), name="attn_bwd", grid=(N_HEADS // 2, n_steps),
        in_specs=[col, col, col, pl.BlockSpec((1, 2, QB, KB), bias_map), col],
        out_specs=[col, col, col, pl.BlockSpec((2, n_dr, GRID_W, GRID_W), lambda hp, r: (hp, 0, 0, 0))],
        out_shape=[_out((tp, NA_W), F32)] * 3 +
                  [_out((N_HEADS, n_dr, GRID_W, GRID_W), F32)],
        compiler_params=_cp(("arbitrary", "arbitrary"), 48))


def _repeat_onehot():
    return np.repeat(np.eye(2 * S5_G, dtype=np.float32), S5_H, axis=0)


def _s5_disc_math(lam_re, lam_im, log_dt, b_re, b_im, rep):
    dt = jnp.exp(log_dt)
    ea = jnp.exp(lam_re * dt)
    a_re = ea * jnp.cos(lam_im * dt)
    a_im = ea * jnp.sin(lam_im * dt)
    den = lam_re * lam_re + lam_im * lam_im
    c_re = ((a_re - 1.0) * lam_re + a_im * lam_im) / den
    c_im = (a_im * lam_re - (a_re - 1.0) * lam_im) / den
    ce_re = jnp.dot(rep, c_re, preferred_element_type=F32, precision=lax.Precision.HIGHEST)
    ce_im = jnp.dot(rep, c_im, preferred_element_type=F32, precision=lax.Precision.HIGHEST)
    return a_re, a_im, ce_re * b_re - ce_im * b_im, ce_re * b_im + ce_im * b_re


def _s5_blocks():
    gl = S5_G // N_BUNDLE
    half = gl * S5_P
    out = []
    for d in range(2):
        for g in range(S5_G):
            b, k = divmod(g, gl)
            dg = d * S5_G + g
            out.append((d, b, slice(k * S5_H, (k + 1) * S5_H), slice(k * S5_P, (k + 1) * S5_P),
                        slice(half + k * S5_P, half + (k + 1) * S5_P), slice(dg * S5_H, (dg + 1) * S5_H),
                        slice(dg, dg + 1)))
    return out


def _s5_params(lam_re, lam_im, log_dt, b_re, b_im, c_re, c_im):
    cw, sw = S5_W // N_BUNDLE, 2 * (S5_G // N_BUNDLE) * S5_P

    def body(lr, li, ld, br, bi, cr, ci, rep_ref, a1_ref, a2_ref, bm_ref, cm_ref):
        a_re, a_im, bb_re, bb_im = _s5_disc_math(lr[...], li[...], ld[...], br[...], bi[...], rep_ref[...])
        cc_re = cr[...]
        cc_im = ci[...]
        bm_ref[...] = jnp.zeros_like(bm_ref)
        cm_ref[...] = jnp.zeros_like(cm_ref)
        for d, b, rows, re, im, nat, one in _s5_blocks():
            bm_ref[d, b, rows, re] = bb_re[nat, :].astype(BF16)
            bm_ref[d, b, rows, im] = bb_im[nat, :].astype(BF16)
            cm_ref[d, b, rows, re] = cc_re[nat, :].astype(BF16)
            cm_ref[d, b, rows, im] = (-cc_im[nat, :]).astype(BF16)
            k = rows.start // S5_H
            lanes = slice((k % 2) * S5_P, (k % 2 + 1) * S5_P)
            for part, (v1, v2) in enumerate(((a_re[one, :], a_im[one, :]), (a_re[one, :], -a_im[one, :]))):
                sub = slice(4 * part + k // 2, 4 * part + k // 2 + 1)
                a1_ref[d, b, sub, lanes] = v1
                a2_ref[d, b, sub, lanes] = v2

    args = (lam_re, lam_im, log_dt, b_re, b_im, c_re, c_im, jnp.asarray(_repeat_onehot()))
    outs = [((2, N_BUNDLE, 8, 128), F32)] * 2 + [((2, N_BUNDLE, cw, sw), BF16)] * 2
    return pl.pallas_call(
        body, name="s5_params", grid=(1,), in_specs=[_full(a.shape) for a in args],
        out_specs=[_full(s) for s, _ in outs], out_shape=[_out(s, dt) for s, dt in outs],
    )(*_in_hbm(*args))


def _s5_params_bwd(lam_re, lam_im, log_dt, b_re, b_im, da, dbm, dcm):
    n, nb = 2 * S5_G, 2 * S5_G * S5_H

    def body(lr, li, ld, br, bi, rep_ref, da_ref, dbm_ref, dcm_ref, o_lr, o_li, o_ld, o_br, o_bi, o_cr, o_ci,
             dar_s, dai_s, dbr_s, dbi_s):
        for d, b, rows, re, im, nat, one in _s5_blocks():
            dbr_s[nat, :] = dbm_ref[d, b, rows, re]
            dbi_s[nat, :] = dbm_ref[d, b, rows, im]
            o_cr[nat, :] = dcm_ref[d, b, rows, re]
            o_ci[nat, :] = -dcm_ref[d, b, rows, im]
            dar_s[one, :] = da_ref[d, b, :, re]
            dai_s[one, :] = da_ref[d, b, :, im]
        rep = rep_ref[...]
        _, vjp = jax.vjp(lambda p, q, r, s, t: _s5_disc_math(p, q, r, s, t, rep),
                         lr[...], li[...], ld[...], br[...], bi[...])
        o_lr[...], o_li[...], o_ld[...], o_br[...], o_bi[...] = vjp((dar_s[...], dai_s[...], dbr_s[...], dbi_s[...]))

    args = (lam_re, lam_im, log_dt, b_re, b_im, jnp.asarray(_repeat_onehot()), da, dbm, dcm)
    outs = [(n, S5_P)] * 2 + [(n, 1)] + [(nb, S5_P)] * 4
    return pl.pallas_call(
        body, name="s5_params_bwd", grid=(1,), in_specs=[_full(a.shape) for a in args],
        out_specs=[_full(s) for s in outs], out_shape=[_out(s, F32) for s in outs],
        scratch_shapes=[pltpu.VMEM((n, S5_P), F32)] * 2 + [pltpu.VMEM((nb, S5_P), F32)] * 2,
    )(*_in_hbm(*args))


def _tiles_store(ref, base, val):
    for i in range(val.shape[0] // 8):
        for c in range(8):
            ref[pl.ds(base + (8 * i + c) * 8, 8), :] = val[8 * i:8 * i + 8, 128 * c:128 * (c + 1)]


def _tiles_load(ref, base, n):
    return jnp.concatenate(
        [jnp.concatenate([ref[pl.ds(base + (8 * i + c) * 8, 8), :] for c in range(8)], axis=1) for i in range(n // 8)],
        axis=0)


def _time_rows(base, t):
    return pl.ds(base + (t // 8) * 64 + t % 8, 8, stride=8)


def _scan(chains, n):
    xs = [c["x"] for c in chains]
    for k in range(n):
        for ci, c in enumerate(chains):
            t = n - 1 - k if c["reverse"] else k
            if c["prev"] is not None:
                c["prev"][_time_rows(c["prev_base"], t), :] = xs[ci]
            xs[ci] = c["a1"] * xs[ci] + pltpu.roll(c["a2"] * xs[ci], 4, axis=0) + c["src"][_time_rows(0, t), :]
            if c["dst"] is not None:
                c["dst"][_time_rows(0, t), :] = xs[ci]
    return xs


def _chain(x, a1, a2, src, dst=None, prev=None, prev_base=0, reverse=False):
    return dict(x=x, a1=a1, a2=a2, src=src, dst=dst, prev=prev, prev_base=prev_base, reverse=reverse)


def _s5_fwd(u, d_skip, a1, a2, bm, cm, length, comm=None, bounds=()):
    tp = u.shape[0]
    cw = S5_W // N_BUNDLE
    sw = bm.shape[-1]
    n_full, n_tail = divmod(length, SCAN_CHUNK)
    t_tail = n_full * SCAN_CHUNK

    nbs = N_BUNDLE

    def body(u_ref, d_ref, a1_ref, a2_ref, bm_ref, cm_ref, y_ref, bnd_ref, *scratch):
        y_ref[...] = u_ref[...] * d_ref[...]
        ins, xss = (scratch[0:nbs], scratch[nbs:2 * nbs]), (scratch[2 * nbs:3 * nbs], scratch[3 * nbs:])
        cols = [slice(b * cw, (b + 1) * cw) for b in range(nbs)]

        def keep(dr, chunk, xs):
            for b in range(nbs):
                bnd_ref[dr, b, chunk] = xs[b]

        def load(dr, t0, n):
            for b in range(nbs):
                _tiles_store(ins[dr][b], 0, _dot(u_ref[pl.ds(t0, n), cols[b]].astype(BF16), bm_ref[dr, b]))

        def chains(dr, xs):
            return [_chain(xs[b], a1_ref[dr, b], a2_ref[dr, b], ins[dr][b], dst=xss[dr][b], reverse=dr == 1)
                    for b in range(nbs)]

        def emit(dr, t0, n):
            for b in range(nbs):
                y_ref[pl.ds(t0, n), cols[b]] += _dg(_tiles_load(xss[dr][b], 0, n).astype(BF16), cm_ref[dr, b], NT)

        zero = (jnp.zeros((8, 128), F32),) * nbs
        xb = zero
        if n_tail:
            keep(1, n_full, xb)
            load(1, t_tail, n_tail)
            xb = tuple(_scan(chains(1, xb), n_tail))
            emit(1, t_tail, n_tail)

        def pair(i, carry):
            j = n_full - 1 - i
            t0s = (pl.multiple_of(i * SCAN_CHUNK, SCAN_CHUNK), pl.multiple_of(j * SCAN_CHUNK, SCAN_CHUNK))
            keep(0, i, carry[0])
            keep(1, j, carry[1])
            for dr in range(2):
                load(dr, t0s[dr], SCAN_CHUNK)
            out = _scan(chains(0, carry[0]) + chains(1, carry[1]), SCAN_CHUNK)
            for dr in range(2):
                emit(dr, t0s[dr], SCAN_CHUNK)
            return tuple(out[:nbs]), tuple(out[nbs:])

        xf, _ = lax.fori_loop(0, n_full, pair, (zero, xb))
        if n_tail:
            keep(0, n_full, xf)
            load(0, t_tail, n_tail)
            _scan(chains(0, xf), n_tail)
            emit(0, t_tail, n_tail)

    n_chunks = n_full + (1 if n_tail else 0)
    tile = pl.BlockSpec((2, nbs, 8, 128), lambda b: (0, b, 0, 0))
    return _call(
        body, comm, bounds, (u, d_skip, a1, a2, bm, cm), name="s5_fwd", grid=(N_BUNDLE // nbs,),
        in_specs=[pl.BlockSpec((tp, nbs * cw), lambda b: (0, b)), pl.BlockSpec((1, nbs * cw), lambda b: (0, b)),
                  tile, tile, pl.BlockSpec((2, nbs, cw, sw), lambda b: (0, b, 0, 0)),
                  pl.BlockSpec((2, nbs, cw, sw), lambda b: (0, b, 0, 0))],
        out_specs=[pl.BlockSpec((tp, nbs * cw), lambda b: (0, b)),
                   pl.BlockSpec((2, nbs, n_chunks, 8, 128), lambda b: (0, b, 0, 0, 0))],
        out_shape=[_out((tp, S5_W), F32), _out((2, N_BUNDLE, n_chunks, 8, 128), F32)],
        scratch_shapes=[pltpu.VMEM((SCAN_CHUNK * 8, 128), F32)] * (4 * nbs),
        compiler_params=_cp(("arbitrary",), 48))


def _s5_bwd(u, dy, d_skip, a1, a2, bm, cm, bnd, length):
    tp = u.shape[0]
    cw = S5_W // N_BUNDLE
    sw = bm.shape[-1]
    half = sw // 2
    n_full, n_tail = divmod(length, SCAN_CHUNK)
    t_tail = n_full * SCAN_CHUNK
    n_chunks = bnd.shape[2]
    nbs = 2

    def body(u_ref, dy_ref, d_ref, a1_ref, a2_ref, bm_ref, cm_ref, bnd_ref, du_ref, dd_ref, dbm_ref, dcm_ref,
             da_ref, *scratch):
        du_ref[...] = dy_ref[...] * d_ref[...]
        dd_ref[...] = jnp.sum(dy_ref[...] * u_ref[...], axis=0, keepdims=True)
        dbm_ref[...] = jnp.zeros_like(dbm_ref)
        dcm_ref[...] = jnp.zeros_like(dcm_ref)
        da_ref[...] = jnp.zeros_like(da_ref)
        bu_s, dx_s, g_s, xp_s, x_s = ([scratch[(k * 2 + dr) * nbs:(k * 2 + dr + 1) * nbs] for dr in range(2)]
                                      for k in range(5))
        cols = [slice(b * cw, (b + 1) * cw) for b in range(nbs)]

        def chains(dr, chunk, t0, n, gs):
            out = []
            for b in range(nbs):
                _tiles_store(bu_s[dr][b], 0, _dot(u_ref[pl.ds(t0, n), cols[b]].astype(BF16), bm_ref[dr, b]))
                _tiles_store(dx_s[dr][b], 0, _dot(dy_ref[pl.ds(t0, n), cols[b]].astype(BF16), cm_ref[dr, b]))
                out.append(_chain(bnd_ref[dr, b, chunk], a1_ref[dr, b], a2_ref[dr, b], bu_s[dr][b],
                                  dst=x_s[dr][b], prev=xp_s[dr][b], reverse=dr == 1))
                out.append(_chain(gs[b], a1_ref[dr, b], -a2_ref[dr, b], dx_s[dr][b], dst=g_s[dr][b], reverse=dr == 0))
            return out

        def emit(dr, t0, n):
            rows = pl.ds(t0, n)
            for b in range(nbs):
                ub = u_ref[rows, cols[b]].astype(BF16)
                dyb = dy_ref[rows, cols[b]].astype(BF16)
                g = _tiles_load(g_s[dr][b], 0, n)
                gb = g.astype(BF16)
                du_ref[rows, cols[b]] += _dg(gb, bm_ref[dr, b], NT)
                dbm_ref[dr, b] += _dg(ub, gb, TN)
                xp = _tiles_load(xp_s[dr][b], 0, n)
                xp_r, xp_i = xp[:, 0:half], xp[:, half:]
                g_r, g_i = g[:, 0:half], g[:, half:]
                dcm_ref[dr, b] += _dg(dyb, _tiles_load(x_s[dr][b], 0, n).astype(BF16), TN)
                da_ref[dr, b] += jnp.concatenate([jnp.sum(g_r * xp_r + g_i * xp_i, axis=0, keepdims=True),
                                                  jnp.sum(g_i * xp_r - g_r * xp_i, axis=0, keepdims=True)], axis=1)

        def adjoints(out):
            return tuple(out[1::2])

        zero = (jnp.zeros((8, 128), F32),) * nbs
        g0 = zero
        if n_tail:
            g0 = adjoints(_scan(chains(0, n_full, t_tail, n_tail, g0), n_tail))
            emit(0, t_tail, n_tail)

        def pair(i, carry):
            j = n_full - 1 - i
            t0 = (pl.multiple_of(j * SCAN_CHUNK, SCAN_CHUNK), pl.multiple_of(i * SCAN_CHUNK, SCAN_CHUNK))
            both = chains(0, j, t0[0], SCAN_CHUNK, carry[0]) + chains(1, i, t0[1], SCAN_CHUNK, carry[1])
            out = _scan(both, SCAN_CHUNK)
            emit(0, t0[0], SCAN_CHUNK)
            emit(1, t0[1], SCAN_CHUNK)
            return adjoints(out[:2 * nbs]), adjoints(out[2 * nbs:])

        _, g1 = lax.fori_loop(0, n_full, pair, (g0, zero))
        if n_tail:
            _scan(chains(1, n_full, t_tail, n_tail, g1), n_tail)
            emit(1, t_tail, n_tail)

    tile = pl.BlockSpec((2, nbs, 8, 128), lambda b: (0, b, 0, 0))
    wide = pl.BlockSpec((2, nbs, cw, sw), lambda b: (0, b, 0, 0))
    col = pl.BlockSpec((tp, nbs * cw), lambda b: (0, b))
    row = pl.BlockSpec((1, nbs * cw), lambda b: (0, b))
    arow = pl.BlockSpec((2, nbs, 1, sw), lambda b: (0, b, 0, 0))
    return pl.pallas_call(
        body, name="s5_bwd", grid=(N_BUNDLE // nbs,),
        in_specs=[col, col, row, tile, tile, wide, wide,
                  pl.BlockSpec((2, nbs, n_chunks, 8, 128), lambda b: (0, b, 0, 0, 0))],
        out_specs=[col, row, wide, wide, arow],
        out_shape=[_out((tp, S5_W), F32), _out((1, S5_W), F32),
                   _out((2, N_BUNDLE, cw, sw), F32), _out((2, N_BUNDLE, cw, sw), F32),
                   _out((2, N_BUNDLE, 1, sw), F32)],
        scratch_shapes=[pltpu.VMEM((SCAN_CHUNK * 8, 128), F32)] * (10 * nbs),
        compiler_params=_cp(("arbitrary",), 56),
    )(*_in_hbm(u, dy, d_skip, a1, a2, bm, cm, bnd))


def _row_tile(tp):
    return max(tm for tm in range(16, 449, 16) if tp % tm == 0)


def _step(x, target, bufs, gains, s5, rpb, c_arr, kc_arr, me_arr):
    n_tok = x.shape[0]
    first = ["ffn1_w_gate", "ffn1_w_up", "ffn1_w_down", "meta_tokens"]
    bias, got = _bias_tables(rpb, n_tok // GRID_W, _gather_comm([bufs[n] for n in first]), (0, N_HEADS - 1))
    w = dict(zip(first, got))
    meta = w["meta_tokens"].transpose(1, 0, 2).reshape(N_META, D)
    length = N_META + n_tok
    tp = length + 16
    tm = _row_tile(tp)
    tmb = tm
    n_rows = n_tok // GRID_W
    pad = jnp.zeros((tp - length, D), F32)
    h0 = jnp.concatenate([meta, x, pad], axis=0)
    tgt = jnp.concatenate([jnp.zeros((N_META, D), F32), target, pad], axis=0)

    lam_re, _ = lax.optimization_barrier((s5["lam_re"], bias))
    s5p = (lam_re, s5["lam_im"], s5["log_dt"].reshape(2 * S5_G, 1), s5["b_re"], s5["b_im"])
    a1_m, a2_m, bm16, cm16 = _s5_params(*s5p, s5["c_re"], s5["c_im"])

    mid = ["w_in", "s5_w_glu", "w_out"]
    (h1, gate1, up1, f1), got = _ffn_fwd(
        "ffn1_fwd", h0, gains["ffn1_pre_g"], gains["ffn1_post_g"], w["ffn1_w_gate"], w["ffn1_w_up"], w["ffn1_w_down"],
        tm, _gather_comm([bufs[n] for n in mid]), (0, (tp // tm) * N_CHIP * 3 // 5))
    w.update(zip(mid, got))
    q, k, v, u = _mix_in(h1, gains["mix_pre_g"], w["w_in"], tm)
    (o_na,), (gate_ici, up_ici) = _attn_fwd(
        q, k, v, bias, n_tok, _gather_comm([bufs["ffn2_w_gate"], bufs["ffn2_w_up"]], pair=False), (0,))
    (y_pre, s5_bnd), (w["ffn2_w_gate"], w["ffn2_w_up"], down_ici) = _s5_fwd(
        u, gains["s5_d"], a1_m, a2_m, bm16, cm16, length,
        _merge_comm(_gather_comm([gate_ici, up_ici], ici=False),
                    _gather_comm([bufs["ffn2_w_down"]], pair=False)), (0,))
    w_glu = w["s5_w_glu"].reshape(S5_W, S5_W)
    w_out = w["w_out"].reshape(D, D)
    (h2, mix), (w["ffn2_w_down"],) = _mix_out(
        o_na, y_pre, h1, w_glu, gains["s5_b_glu"], gains["na_out_g"], gains["s5_out_g"], w_out, gains["mix_post_g"], tm,
        _gather_comm([down_ici], ici=False), (0,))
    (h3, gate2, up2, f2), _ = _ffn_fwd("ffn2_fwd", h2, gains["ffn2_pre_g"], gains["ffn2_post_g"],
                                       w["ffn2_w_gate"], w["ffn2_w_up"], w["ffn2_w_down"], tm)
    dh3, df2, loss, dg_final, dg_post2 = _final_loss(h3, gains["final_g"], tgt, f2, gains["ffn2_post_g"], n_tok, tm)

    ffn2 = ["ffn2_w_gate", "ffn2_w_up", "ffn2_w_down"]
    ffn1 = ["ffn1_w_gate", "ffn1_w_up", "ffn1_w_down"]
    out2, _ = _ffn_bwd("ffn2_bwd", h2, gains["ffn2_pre_g"], df2, gate2, up2,
                       w["ffn2_w_gate"], w["ffn2_w_up"], w["ffn2_w_down"], tmb)
    dxn2 = out2[3]
    sums2 = _chip_sums("chip_sums_ffn2", out2[0:3], out2[4:7], c_arr)
    (dh2, dg_pre2), _ = _ffn_pre_bwd("ffn2_pre_bwd", dh3, dxn2, h2, gains["ffn2_pre_g"], tm)
    do_na, dy_pre, dw_out, dw_glu, dg_mpost, dg_na, dg_s5, db_glu = _mix_out_bwd(
        dh2, mix, o_na, y_pre, w_glu, gains["s5_b_glu"], gains["na_out_g"], gains["s5_out_g"], w_out,
        gains["mix_post_g"], tm)
    (dq, dk, dv, dtb), recv3 = _attn_bwd(q, k, v, bias, do_na, n_tok, _scatter_comm(sums2), (0,))
    totals2 = _total_sums("total_sums_ffn2", sums2, recv3, kc_arr)
    du, dd, dbm, dcm, da_m = _s5_bwd(u, dy_pre, gains["s5_d"], a1_m, a2_m, bm16, cm16, s5_bnd, length)
    (dh1, df1, dw_in, dg_mpre, dg_post1), done2 = _mix_in_bwd(
        dq, dk, dv, du, h1, gains["mix_pre_g"], w["w_in"], dh2, f1, gains["ffn1_post_g"], tm,
        _assemble_comm(totals2), (0,))
    pieces = dict(zip(ffn2, done2))

    e, _ = _diag_onehot()
    n_dr = 2 * KH - 1
    drpb = _rpb_collapse(dtb.reshape(N_HEADS * n_dr, GRID_W * GRID_W), jnp.asarray(e.T))
    drpb = drpb[:, :2 * KW - 1].reshape(N_HEADS, n_dr, 2 * KW - 1).transpose(1, 0, 2).reshape(N_HEADS * n_dr, 2 * KW - 1)
    dlam_re, dlam_im, dlog_dt, db_re, db_im, dc_re, dc_im = _s5_params_bwd(*s5p, da_m, dbm, dcm)
    early = {"ffn1_post_g": dg_post1, "mix_pre_g": dg_mpre, "na_rpb": drpb,
             "s5_lam_re": dlam_re, "s5_lam_im": dlam_im, "s5_log_dt": dlog_dt.reshape(2, S5_G),
             "s5_b_re": db_re, "s5_b_im": db_im, "s5_c_re": dc_re, "s5_c_im": dc_im,
             "s5_d": dd, "s5_b_glu": db_glu, "na_out_g": dg_na,
             "s5_out_g": dg_s5, "mix_post_g": dg_mpost, "ffn2_pre_g": dg_pre2, "ffn2_post_g": dg_post2,
             "final_g": dg_final}
    names = list(early)
    slots = _small_pack([early[n] for n in names], me_arr)

    out1, slots = _ffn_bwd("ffn1_bwd", h0, gains["ffn1_pre_g"], df1, gate1, up1,
                           w["ffn1_w_gate"], w["ffn1_w_up"], w["ffn1_w_down"], tmb, _spread_comm(slots), (0,))
    small = dict(zip(names, _small_total(slots, [early[n].shape for n in names])))
    sums1 = _chip_sums("chip_sums_ffn1", out1[0:3], out1[4:7], c_arr)
    flight1 = _scatter_start("ffn1", sums1)
    token = flight1[4]
    rest = [dw_in, dw_glu.reshape(N_CHIP, S5_W // N_CHIP, S5_W), dw_out.reshape(N_CHIP, D // N_CHIP, D)]
    (dh0, dg_pre1), recv_rest = _ffn_pre_bwd("ffn1_pre_bwd", dh1, out1[3], h0, gains["ffn1_pre_g"] + token[0:1, 0:1],
                                             tm, _exchange_comm(rest), (0,))
    sums = _chip_sums("chip_sums_rest", rest, recv_rest, c_arr)
    flight2 = _scatter_start("rest", sums)
    return loss[0, 0], dh0, pieces, small, {"ffn1_pre_g": dg_pre1}, (ffn1, flight1[:4]), (mid, flight2[:4])


def _mesh_pos():
    return lax.axis_index("x"), lax.axis_index("y"), lax.axis_index("c")


def _other_chips(x, y):
    return [(1 - x, y), (x, 1 - y), (1 - x, 1 - y)]


class _Comm:
    def __init__(self, ins, out_shape, aliases, parts):
        self.ins, self.out_shape, self.aliases, self.parts = list(ins), list(out_shape), dict(aliases), list(parts)
        self.n_sems = sum(p[0] for p in parts)

    def bases(self):
        out, base = [], 0
        for n_sems, _, _ in self.parts:
            out.append(base)
            base += n_sems
        return out


def _run_comm(name, comm):
    n_i, n_o = len(comm.ins), len(comm.out_shape)

    def body(*refs):
        ins, outs = refs[:n_i], refs[n_i:n_i + n_o]
        send_sems, recv_sems = refs[n_i + n_o:]
        for base, (_, start, finish) in zip(comm.bases(), comm.parts):
            start(ins, outs, send_sems, recv_sems, base)
            finish(ins, outs, send_sems, recv_sems, base)

    return pl.pallas_call(
        body, name=name, out_shape=comm.out_shape, in_specs=[ANY] * n_i, out_specs=[ANY] * n_o,
        input_output_aliases=comm.aliases,
        scratch_shapes=[pltpu.SemaphoreType.DMA((comm.n_sems,)), pltpu.SemaphoreType.DMA((comm.n_sems,))],
    )(*_in_hbm(*comm.ins))


def _call(body, comm, bounds, args, *, name, grid, in_specs, out_specs, out_shape, scratch_shapes=(),
          compiler_params=None):
    in_specs, out_specs, out_shape, scratch_shapes = list(in_specs), list(out_specs), list(out_shape), list(scratch_shapes)
    if comm is None:
        return pl.pallas_call(body, name=name, grid=grid, in_specs=in_specs, out_specs=out_specs, out_shape=out_shape,
                              scratch_shapes=scratch_shapes, compiler_params=compiler_params)(*_in_hbm(*args)), []
    n_in, n_out, n_scr = len(in_specs), len(out_specs), len(scratch_shapes)
    n_ci, n_co = len(comm.ins), len(comm.out_shape)
    n_steps = int(np.prod(grid))
    assert len(bounds) == len(comm.parts) and all(0 <= b < n_steps for b in bounds) and list(bounds) == sorted(bounds)

    def fused(*refs):
        a = n_in
        b = a + n_ci
        c = b + n_out
        d = c + n_co
        e = d + n_scr
        cargs = (refs[a:b], refs[c:d], refs[e], refs[e + 1])
        step = pl.program_id(0)
        for ax in range(1, len(grid)):
            step = step * grid[ax] + pl.program_id(ax)
        bases = comm.bases()
        for p, (_, start, finish) in enumerate(comm.parts):
            @pl.when(step == bounds[p])
            def _(p=p, start=start):
                if p > 0:
                    comm.parts[p - 1][2](*cargs, bases[p - 1])
                start(*cargs, bases[p])
        body(*(refs[:a] + refs[b:c] + refs[d:e]))

        @pl.when(step == n_steps - 1)
        def _():
            comm.parts[-1][2](*cargs, bases[-1])

    res = pl.pallas_call(
        fused, name=name, grid=grid, in_specs=in_specs + [ANY] * n_ci, out_specs=out_specs + [ANY] * n_co,
        out_shape=out_shape + comm.out_shape,
        scratch_shapes=scratch_shapes + [pltpu.SemaphoreType.DMA((comm.n_sems,)), pltpu.SemaphoreType.DMA((comm.n_sems,))],
        input_output_aliases={n_in + i: n_out + j for i, j in comm.aliases.items()},
        compiler_params=compiler_params)(*_in_hbm(*args, *comm.ins))
    return res[:n_out], res[n_out:]


def _remote(src, dst, send_sems, recv_sems, idx, to):
    return pltpu.make_async_remote_copy(src_ref=src, dst_ref=dst, send_sem=send_sems.at[idx],
                                        recv_sem=recv_sems.at[idx], device_id=to, device_id_type=MESH_ID)


def _gather_comm(bufs, ici=True, pair=True):
    n = len(bufs)

    def half(ref, k, pc):
        rh = ref.shape[1] // 2
        return ref.at[k, pl.ds(pc * rh, rh), :]

    def ici_start(ins, outs, ss, rs, base):
        x, y, c = _mesh_pos()
        for a in range(n):
            mine = half(outs[a], 2 * x + y, c)
            for j, chip in enumerate(_other_chips(x, y)):
                _remote(mine, mine, ss, rs, base + 3 * a + j, (*chip, c)).start()

    def ici_finish(ins, outs, ss, rs, base):
        x, y, c = _mesh_pos()
        for a in range(n):
            for j, chip in enumerate(_other_chips(x, y)):
                theirs = half(outs[a], 2 * chip[0] + chip[1], c)
                _remote(theirs, theirs, ss, rs, base + 3 * a + j, (*chip, c)).wait()

    def pair_copy(outs, ss, rs, base, a):
        x, y, c = _mesh_pos()
        rh = outs[a].shape[1] // 2
        held = outs[a].at[:, pl.ds(c * rh, rh), :]
        return _remote(held, held, ss, rs, base + a, (x, y, 1 - c))

    def pair_start(ins, outs, ss, rs, base):
        for a in range(n):
            pair_copy(outs, ss, rs, base, a).start()

    def pair_finish(ins, outs, ss, rs, base):
        for a in range(n):
            pair_copy(outs, ss, rs, base, a).wait()

    parts = ([(3 * n, ici_start, ici_finish)] if ici else []) + ([(n, pair_start, pair_finish)] if pair else [])
    return _Comm(bufs, [_out(b.shape, b.dtype) for b in bufs], {a: a for a in range(n)}, parts)


def _merge_comm(*comms):
    ins, shapes, aliases, subs, base = [], [], {}, [], 0
    for cm in comms:
        (n_sems, start, finish), = cm.parts
        i0, o0 = len(ins), len(shapes)
        subs.append((slice(i0, i0 + len(cm.ins)), slice(o0, o0 + len(cm.out_shape)), base, start, finish))
        aliases.update({i0 + i: o0 + j for i, j in cm.aliases.items()})
        ins += cm.ins
        shapes += cm.out_shape
        base += n_sems

    def start_all(ins_r, outs_r, ss, rs, b):
        for si, so, off, start, _ in subs:
            start(ins_r[si], outs_r[so], ss, rs, b + off)

    def finish_all(ins_r, outs_r, ss, rs, b):
        for si, so, off, _, finish in subs:
            finish(ins_r[si], outs_r[so], ss, rs, b + off)

    return _Comm(ins, shapes, aliases, [(base, start_all, finish_all)])


def _own_half_buffers(pieces, dtypes, kc_arr):
    n = len(pieces)

    def body(kc_ref, *refs):
        for a in range(n):
            refs[n + a][0] = refs[a][...].astype(dtypes[a])

    def half(p):
        return p.shape[0] // 2, p.shape[1]

    return pl.pallas_call(
        body, name="own_halves",
        out_shape=[_out((N_CHIP,) + p.shape, dt) for p, dt in zip(pieces, dtypes)],
        grid_spec=pltpu.PrefetchScalarGridSpec(
            num_scalar_prefetch=1, grid=(1,),
            in_specs=[pl.BlockSpec(half(p), lambda i, kc: (kc[1], 0)) for p in pieces],
            out_specs=[pl.BlockSpec((1,) + half(p), lambda i, kc: (kc[0], kc[1], 0)) for p in pieces]),
        compiler_params=_cp(("arbitrary",), 48),
    )(kc_arr, *_in_hbm(*pieces))


def _exchange_comm(grads):
    n = len(grads)

    def copy(ins, outs, ss, rs, base, a):
        x, y, c = _mesh_pos()
        rh = ins[a].shape[1] // 2
        return _remote(ins[a].at[:, pl.ds((1 - c) * rh, rh), :], outs[a], ss, rs, base + a, (x, y, 1 - c))

    def start(ins, outs, ss, rs, base):
        for a in range(n):
            copy(ins, outs, ss, rs, base, a).start()

    def finish(ins, outs, ss, rs, base):
        for a in range(n):
            copy(ins, outs, ss, rs, base, a).wait()

    shapes = [_out((N_CHIP, g.shape[1] // 2, g.shape[2]), g.dtype) for g in grads]
    return _Comm(grads, shapes, {}, [(n, start, finish)])


def _chip_sums(name, grads, recvs, c_arr):
    n = len(grads)
    halves = [(1, g.shape[1] // 2, g.shape[2]) for g in grads]

    def body(c_ref, *refs):
        for a in range(n):
            refs[2 * n + a][...] = (refs[a][...] + refs[n + a][...]).astype(BF16)

    return pl.pallas_call(
        body, name=name, out_shape=[_out((N_CHIP,) + h[1:], BF16) for h in halves],
        grid_spec=pltpu.PrefetchScalarGridSpec(
            num_scalar_prefetch=1, grid=(N_CHIP,),
            in_specs=[pl.BlockSpec(h, lambda j, c_ref: (j, c_ref[0], 0)) for h in halves] +
                     [pl.BlockSpec(h, lambda j, c_ref: (j, 0, 0)) for h in halves],
            out_specs=[pl.BlockSpec(h, lambda j, c_ref: (j, 0, 0)) for h in halves]),
        compiler_params=_cp(("arbitrary",), 40),
    )(c_arr, *_in_hbm(*grads, *recvs))


def _scatter_comm(sums):
    n = len(sums)

    def copies(ins, outs, ss, rs, base):
        x, y, c = _mesh_pos()
        return [_remote(ins[a].at[2 * chip[0] + chip[1]], outs[a].at[j], ss, rs, base + 3 * a + j, (*chip, c))
                for a in range(n) for j, chip in enumerate(_other_chips(x, y))]

    def start(ins, outs, ss, rs, base):
        for cp in copies(ins, outs, ss, rs, base):
            cp.start()

    def finish(ins, outs, ss, rs, base):
        for cp in copies(ins, outs, ss, rs, base):
            cp.wait()

    shapes = [_out((3,) + s.shape[1:], s.dtype) for s in sums]
    return _Comm(sums, shapes, {}, [(3 * n, start, finish)])


def _scatter_copies(ins, lands, send_sems, recv_sems):
    x, y, c = _mesh_pos()
    return [_remote(ins[a].at[2 * chip[0] + chip[1]], lands[a].at[j], send_sems, recv_sems, 3 * a + j, (*chip, c))
            for a in range(len(ins)) for j, chip in enumerate(_other_chips(x, y))]


def _scatter_start(name, sums):
    n = len(sums)
    lands = [lax.empty((3,) + s.shape[1:], s.dtype) for s in sums]
    hbm = pl.BlockSpec(memory_space=pltpu.HBM)
    sem = pl.BlockSpec(memory_space=pltpu.SEMAPHORE)

    def body(*refs):
        ins, land_refs = refs[:n], refs[n:2 * n]
        send_sems, recv_sems = refs[2 * n], refs[2 * n + 1]
        token = refs[-1]
        for cp in _scatter_copies(ins, land_refs, send_sems, recv_sems):
            cp.start()
        token[...] = jnp.zeros_like(token)

    res = pl.pallas_call(
        body, name=name + "_scatter_start",
        out_shape=(pltpu.SemaphoreType.DMA((3 * n,)), pltpu.SemaphoreType.DMA((3 * n,)),
                   *[pltpu.HBM(s.shape, s.dtype) for s in sums], *[pltpu.HBM(ld.shape, ld.dtype) for ld in lands],
                   jax.ShapeDtypeStruct((8, 128), F32)),
        in_specs=[hbm] * (2 * n), out_specs=(sem, sem, *[hbm] * (2 * n), pl.BlockSpec(memory_space=pltpu.VMEM)),
        input_output_aliases={i: 2 + i for i in range(2 * n)},
        compiler_params=pltpu.CompilerParams(has_side_effects=pltpu.SideEffectType.DATAFLOW_SIDE_EFFECTING),
    )(*[pltpu.with_memory_space_constraint(a, pltpu.HBM) for a in list(sums) + lands])
    return res[0], res[1], list(res[2:2 + n]), list(res[2 + n:2 + 2 * n]), res[-1]


def _scatter_wait(name, send_sems, recv_sems, sums, lands, after):
    n = len(sums)
    hbm = pl.BlockSpec(memory_space=pltpu.HBM)
    sem = pl.BlockSpec(memory_space=pltpu.SEMAPHORE)

    def body(*refs):
        ins, land_refs = refs[:n], refs[n:2 * n]
        for cp in _scatter_copies(ins, land_refs, refs[2 * n], refs[2 * n + 1]):
            cp.wait_send()
            cp.wait_recv()

    res = pl.pallas_call(
        body, name=name + "_scatter_wait",
        out_shape=tuple([pltpu.HBM(s.shape, s.dtype) for s in sums] + [pltpu.HBM(ld.shape, ld.dtype) for ld in lands]),
        in_specs=[hbm] * (2 * n) + [sem, sem, pl.BlockSpec(memory_space=pl.ANY)], out_specs=tuple([hbm] * (2 * n)),
        input_output_aliases={i: i for i in range(2 * n)},
        compiler_params=pltpu.CompilerParams(has_side_effects=pltpu.SideEffectType.DATAFLOW_SIDE_EFFECTING),
    )(*sums, *lands, send_sems, recv_sems, after)
    return list(res[:n]), list(res[n:])


def _total_sums(name, sums, recv3, kc_arr):
    n = len(sums)
    dims = [s.shape[1:] for s in sums]

    def body(kc_ref, *refs):
        for a in range(n):
            s_ref, r_ref = refs[a], refs[n + a]
            t = s_ref[0].astype(F32) + r_ref[0].astype(F32)
            t = t + r_ref[1].astype(F32)
            refs[2 * n + a][...] = t + r_ref[2].astype(F32)

    return pl.pallas_call(
        body, name=name, out_shape=[_out((2 * rh, cc), F32) for rh, cc in dims],
        grid_spec=pltpu.PrefetchScalarGridSpec(
            num_scalar_prefetch=1, grid=(1,),
            in_specs=[pl.BlockSpec((1, rh, cc), lambda i, kc_ref: (kc_ref[0], 0, 0)) for rh, cc in dims] +
                     [pl.BlockSpec((3, rh, cc), lambda i, kc_ref: (0, 0, 0)) for rh, cc in dims],
            out_specs=[pl.BlockSpec((rh, cc), lambda i, kc_ref: (kc_ref[1], 0)) for rh, cc in dims]),
        compiler_params=_cp(("arbitrary",), 48),
    )(kc_arr, *_in_hbm(*sums, *recv3))


def _assemble_comm(totals):
    n = len(totals)

    def copy(outs, ss, rs, base, a):
        x, y, c = _mesh_pos()
        rh = outs[a].shape[0] // 2
        here = outs[a].at[pl.ds(c * rh, rh), :]
        return _remote(here, here, ss, rs, base + a, (x, y, 1 - c))

    def start(ins, outs, ss, rs, base):
        for a in range(n):
            copy(outs, ss, rs, base, a).start()

    def finish(ins, outs, ss, rs, base):
        for a in range(n):
            copy(outs, ss, rs, base, a).wait()

    shapes = [_out(t.shape, t.dtype) for t in totals]
    return _Comm(totals, shapes, {a: a for a in range(n)}, [(n, start, finish)])


def _small_layout(shapes):
    n = len(shapes)
    narrow_w = 64
    wide = [a for a in range(n) if shapes[a][1] > narrow_w]
    narrow = sorted((a for a in range(n) if shapes[a][1] <= narrow_w), key=lambda a: -shapes[a][0])
    offs, cols, groups, widths, rows = {}, {}, [], [], []
    if wide:
        r = 0
        for a in wide:
            offs[a], cols[a] = r, 0
            r += shapes[a][0]
        groups.append(wide)
        widths.append(max(shapes[a][1] for a in wide))
        rows.append(-(-r // 8) * 8)
    if narrow:
        heights = [0, 0]
        for a in narrow:
            side = 0 if heights[0] <= heights[1] else 1
            offs[a], cols[a] = heights[side], side * narrow_w
            heights[side] += shapes[a][0]
        groups.append(narrow)
        widths.append(2 * narrow_w)
        rows.append(-(-max(heights) // 8) * 8)

    def window(ref, a):
        return ref.at[offs[a]:offs[a] + shapes[a][0], cols[a]:cols[a] + shapes[a][1]]

    return groups, widths, rows, window


def _small_pack(arrays, me_arr):
    shapes = [a.shape for a in arrays]
    groups, widths, rows, window = _small_layout(shapes)
    n, n_g = len(arrays), len(groups)

    def body(me_ref, *refs):
        ins, outs = refs[:n], refs[n:]
        for gi, g in enumerate(groups):
            outs[gi][...] = jnp.zeros_like(outs[gi])
            for a in g:
                window(outs[gi].at[0], a)[...] = ins[a][...]

    return pl.pallas_call(
        body, name="small_pack", out_shape=[_out((8, r, w), F32) for r, w in zip(rows, widths)],
        grid_spec=pltpu.PrefetchScalarGridSpec(
            num_scalar_prefetch=1, grid=(1,), in_specs=[pl.BlockSpec(s, lambda i, me: (0, 0)) for s in shapes],
            out_specs=[pl.BlockSpec((1, r, w), lambda i, me: (me[0], 0, 0)) for r, w in zip(rows, widths)]),
        compiler_params=_cp(("arbitrary",), 32),
    )(me_arr, *_in_hbm(*arrays))


def _spread_comm(slots):
    n = len(slots)
    flips = [(dx, dy, dc) for dx in range(2) for dy in range(2) for dc in range(2)][1:]

    def copies(outs, ss, rs, base):
        x, y, c = _mesh_pos()
        mine = 4 * x + 2 * y + c
        return [_remote(outs[a].at[mine], outs[a].at[mine], ss, rs, base + 7 * a + f,
                        (x ^ dx, y ^ dy, c ^ dc)) for a in range(n) for f, (dx, dy, dc) in enumerate(flips)]

    def start(ins, outs, ss, rs, base):
        for cp in copies(outs, ss, rs, base):
            cp.start()

    def finish(ins, outs, ss, rs, base):
        for cp in copies(outs, ss, rs, base):
            cp.wait()

    return _Comm(slots, [_out(s.shape, s.dtype) for s in slots], {a: a for a in range(n)}, [(7 * n, start, finish)])


def _small_total(slots, shapes):
    groups, widths, rows, window = _small_layout(shapes)
    n, n_g = len(shapes), len(groups)

    def body(*refs):
        ins, outs, acc = refs[:n_g], refs[n_g:n_g + n], refs[n_g + n:]
        for gi, g in enumerate(groups):
            t = ins[gi][0] + ins[gi][1]
            for d in range(2, 8):
                t = t + ins[gi][d]
            acc[gi][...] = t
            for a in g:
                outs[a][...] = window(acc[gi], a)[...]

    return pl.pallas_call(
        body, name="small_total", grid=(1,), out_shape=[_out(s, F32) for s in shapes],
        in_specs=[_full(s.shape) for s in slots], out_specs=[_full(s) for s in shapes],
        scratch_shapes=[pltpu.VMEM((r, w), F32) for r, w in zip(rows, widths)],
        compiler_params=_cp(("arbitrary",), 48),
    )(*_in_hbm(*slots))


def _small_allreduce(arrays, comm):
    n = len(arrays)
    shapes = [a.shape for a in arrays]
    groups, widths, rows, window = _small_layout(shapes)
    n_g = len(groups)

    def body(*refs):
        ins, outs = refs[:n], refs[n:2 * n]
        pack, sib, csum, every = (refs[2 * n + i * n_g:2 * n + (i + 1) * n_g] for i in range(4))
        send_sems, recv_sems = refs[2 * n + 4 * n_g:]
        x, y, c = _mesh_pos()
        k = 2 * x + y
        for gi, g in enumerate(groups):
            pack[gi][...] = jnp.zeros_like(pack[gi])
            for a in g:
                window(pack[gi], a)[...] = ins[a][...]
        cps = [_remote(pack[gi], sib[gi], send_sems, recv_sems, gi, (x, y, 1 - c)) for gi in range(n_g)]
        for cp in cps:
            cp.start()
        for cp in cps:
            cp.wait()
        for gi in range(n_g):
            csum[gi][...] = pack[gi][...] + sib[gi][...]
            every[gi][k] = csum[gi][...]
        cps = [_remote(csum[gi], every[gi].at[k], send_sems, recv_sems, n_g + 3 * gi + j, (*chip, c))
               for gi in range(n_g) for j, chip in enumerate(_other_chips(x, y))]
        for cp in cps:
            cp.start()
        for cp in cps:
            cp.wait()
        for gi, g in enumerate(groups):
            pack[gi][...] = ((every[gi][0] + every[gi][1]) + every[gi][2]) + every[gi][3]
            for a in g:
                outs[a][...] = window(pack[gi], a)[...]

    bufs = [pltpu.VMEM((r, w), F32) for r, w in zip(rows, widths)]
    return _call(
        body, comm, (0,), arrays, name="small_allreduce", grid=(1,), out_shape=[_out(s, F32) for s in shapes],
        in_specs=[_full(s) for s in shapes], out_specs=[_full(s) for s in shapes],
        scratch_shapes=bufs * 3 + [pltpu.VMEM((N_CHIP, r, w), F32) for r, w in zip(rows, widths)] +
                       [pltpu.SemaphoreType.DMA((4 * n_g,)), pltpu.SemaphoreType.DMA((4 * n_g,))],
        compiler_params=_cp(("arbitrary",), 40))


def _adamw_small(ws, gs, ms, vs, comm):
    n = len(ws)

    def body(*refs):
        w, g, m, v, d, mo, vo = (refs[i * n:(i + 1) * n] for i in range(7))
        for a in range(n):
            d[a][...], mo[a][...], vo[a][...] = _adamw_math(w[a][...], g[a][...], m[a][...], v[a][...])

    specs = [_full(w.shape) for w in ws]
    res, got = _call(
        body, comm, (0,), (*ws, *gs, *ms, *vs), name="adamw_small", grid=(1,),
        out_shape=[_out(w.shape, F32) for w in ws] * 3,
        in_specs=specs * 4, out_specs=specs * 3, compiler_params=_cp(("arbitrary",), 40))
    return (res[:n], res[n:2 * n], res[2 * n:]), got


def _adamw_math(w, g, m, v):
    m = ADAM_B1 * m + (1.0 - ADAM_B1) * g
    v = ADAM_B2 * v + (1.0 - ADAM_B2) * (g * g)
    m_hat = m / (1.0 - ADAM_B1 ** ADAM_STEP)
    v_hat = v / (1.0 - ADAM_B2 ** ADAM_STEP)
    delta = -ADAM_LR * (m_hat / (jnp.sqrt(v_hat) + ADAM_EPS) + ADAM_WD * w)
    return delta, m, v


def _adamw_group(name, ws, gs, ms, vs):
    n = len(ws)
    steps = 8
    specs = [_rows(w.shape[0] // steps, w.shape[1]) for w in ws]
    assert all(w.shape[0] % (8 * steps) == 0 for w in ws)

    def body(*refs):
        w, g, m, v, d, mo, vo = (refs[i * n:(i + 1) * n] for i in range(7))
        for a in range(n):
            d[a][...], mo[a][...], vo[a][...] = _adamw_math(w[a][...], g[a][...], m[a][...], v[a][...])

    res = pl.pallas_call(
        body, name=name, grid=(steps,), in_specs=specs * 4, out_specs=specs * 3,
        out_shape=[_out(w.shape, F32) for w in ws] * 3, compiler_params=_cp(("arbitrary",), 40),
    )(*_in_hbm(*ws, *gs, *ms, *vs))
    return res[:n], res[n:2 * n], res[2 * n:]


def _as_matrix(name, a):
    if name == "na_rpb":
        return a[0].transpose(1, 0, 2).reshape(N_HEADS * (2 * KH - 1), 2 * KW - 1)
    if name in ("s5_b_re", "s5_b_im"):
        return a.transpose(0, 1, 2, 4, 3).reshape(2 * S5_G * S5_H, S5_P)
    if name in ("s5_c_re", "s5_c_im"):
        return a.reshape(2 * S5_G * S5_H, S5_P)
    if name in ("s5_lam_re", "s5_lam_im"):
        return a.reshape(2 * S5_G, S5_P)
    if name == "s5_log_dt":
        return a.reshape(2, S5_G)
    return a


def _from_matrix(name, m):
    if name == "na_rpb":
        return m.reshape(2 * KH - 1, N_HEADS, 2 * KW - 1).transpose(1, 0, 2)[None]
    if name in ("s5_b_re", "s5_b_im"):
        return m.reshape(1, 2, S5_G, S5_H, S5_P).transpose(0, 1, 2, 4, 3)
    if name in ("s5_c_re", "s5_c_im"):
        return m.reshape(1, 2, S5_G, S5_H, S5_P)
    if name in ("s5_lam_re", "s5_lam_im"):
        return m.reshape(1, 2, S5_G, S5_P)
    if name == "s5_log_dt":
        return m.reshape(1, 2, S5_G)
    return m


WEIGHTS = ["meta_tokens", "ffn1_pre_g", "ffn1_post_g", "ffn1_w_gate", "ffn1_w_up", "ffn1_w_down", "mix_pre_g", "w_in",
           "na_rpb", "s5_lam_re", "s5_lam_im", "s5_log_dt", "s5_b_re", "s5_b_im", "s5_c_re", "s5_c_im", "s5_d",
           "s5_w_glu", "s5_b_glu", "na_out_g", "s5_out_g", "w_out", "mix_post_g", "ffn2_pre_g", "ffn2_post_g",
           "ffn2_w_gate", "ffn2_w_up", "ffn2_w_down", "final_g"]
BIG = ["ffn1_w_gate", "ffn1_w_up", "ffn1_w_down", "w_in", "s5_w_glu", "w_out", "ffn2_w_gate", "ffn2_w_up",
       "ffn2_w_down"]
TRANSPOSED = ["ffn1_w_gate", "ffn1_w_up", "ffn2_w_gate", "ffn2_w_up"]
GAINS = ["ffn1_pre_g", "ffn1_post_g", "mix_pre_g", "s5_d", "s5_b_glu", "na_out_g", "s5_out_g", "mix_post_g",
         "ffn2_pre_g", "ffn2_post_g", "final_g"]
SMALL = [n for n in WEIGHTS if n not in BIG]


def kernel(*args):
    names = ["x"] + WEIGHTS + ["loss_target"] + ["m_" + n for n in WEIGHTS] + ["v_" + n for n in WEIGHTS]
    assert len(args) == len(names)
    given = dict(zip(names, args))
    x_pos, y_pos, c_pos = _mesh_pos()
    k_pos = 2 * x_pos + y_pos
    c_arr = jnp.reshape(c_pos, (1,)).astype(jnp.int32)
    kc_arr = jnp.stack([k_pos, c_pos]).astype(jnp.int32)

    def piece(name, a):
        return a[0].T if name in TRANSPOSED else a[0]

    def unpiece(name, a):
        return a.T[None] if name in TRANSPOSED else a[None]

    placed = BIG + ["meta_tokens"]
    bufs = dict(zip(placed, _own_half_buffers([piece(n, given[n]) for n in BIG] + [given["meta_tokens"]],
                                              [BF16] * len(BIG) + [F32], kc_arr)))

    gains = {n: given[n] for n in GAINS}
    s5 = {n: _as_matrix("s5_" + n, given["s5_" + n])
          for n in ["lam_re", "lam_im", "log_dt", "b_re", "b_im", "c_re", "c_im"]}
    me_arr = jnp.reshape(4 * x_pos + 2 * y_pos + c_pos, (1,)).astype(jnp.int32)
    loss, dh0, pieces, small, late, (ffn1, flight1), (mid, flight2) = _step(
        given["x"][0], given["loss_target"][0], bufs, gains, s5, given["na_rpb"][0], c_arr, kc_arr, me_arr)
    loss = lax.psum(loss, ("x", "y", "c"))
    n_tok = given["x"].shape[1]
    grad_x = dh0[N_META:N_META + n_tok][None]

    late["meta_tokens"] = dh0[:N_META]
    out_g, out_d, out_m, out_v = {}, {}, {}, {}

    def update_big(group, names):
        g2 = [pieces[n] for n in names]
        d2, m2, v2 = _adamw_group("adamw_" + group, [piece(n, given[n]) for n in names], g2,
                                  [piece(n, given["m_" + n]) for n in names], [piece(n, given["v_" + n]) for n in names])
        for n, g, dd, mm, vv in zip(names, g2, d2, m2, v2):
            out_g[n], out_d[n], out_m[n], out_v[n] = (unpiece(n, t) for t in (g, dd, mm, vv))
        return v2

    done2 = update_big("ffn2", list(pieces))
    sums1, recv1 = _scatter_wait("ffn1", *flight1, done2[-1])
    totals1 = _total_sums("total_sums_ffn1", sums1, recv1, kc_arr)
    pieces.update(zip(ffn1, _run_comm("ffn1_pair_assemble", _assemble_comm(totals1))))
    done1 = update_big("ffn1", ffn1)
    late_arrays = list(late.values())
    late_arrays[0], _ = lax.optimization_barrier((late_arrays[0], (done1[-1], small["final_g"])))
    red, _ = _small_allreduce(late_arrays, None)
    small.update(zip(late, red))
    mc = D // N_CHIP
    small["meta_tokens"] = lax.dynamic_slice_in_dim(small["meta_tokens"], k_pos * mc, mc, 1)
    sums_rest, recv_rest = _scatter_wait("rest", *flight2, red[0])
    totals = _total_sums("total_sums_rest", sums_rest, recv_rest, kc_arr)
    gs = [small[n] for n in SMALL]
    (d2, m2, v2), done = _adamw_small([_as_matrix(n, given[n]) for n in SMALL], gs,
                                      [_as_matrix(n, given["m_" + n]) for n in SMALL],
                                      [_as_matrix(n, given["v_" + n]) for n in SMALL], _assemble_comm(totals))
    pieces.update(zip(mid, done))

    for n, g, dd, mm, vv in zip(SMALL, gs, d2, m2, v2):
        out_g[n], out_d[n], out_m[n], out_v[n] = (_from_matrix(n, t) for t in (g, dd, mm, vv))
    update_big("rest", mid)
    return (loss, grad_x, *[out_g[n] for n in WEIGHTS], *[out_d[n] for n in WEIGHTS],
            *[out_m[n] for n in WEIGHTS], *[out_v[n] for n in WEIGHTS])
```

```python
import math

import numpy as np
import jax
import jax.numpy as jnp
from jax import lax
from jax.experimental import pallas as pl
from jax.experimental.pallas import tpu as pltpu

F32 = jnp.float32
BF16 = jnp.bfloat16

D = 1024
N_META = 16
GRID_W = 64
NA_W = 512
S5_W = 512
HEAD_DIM = 64
N_HEADS = 8
KH = 8
KW = 16
S5_G = 32
S5_P = 64
S5_H = 16
N_BUNDLE = 4
FF = 2816
N_CHIP = 4
FC = FF // N_CHIP
EPS = 1e-6
NEG_INF = -1e30
Q_ROWS = 4
K_ROWS = 12
QB = Q_ROWS * GRID_W
KB = K_ROWS * GRID_W
SCAN_CHUNK = 256

ADAM_LR = 0.001
ADAM_B1 = 0.9
ADAM_B2 = 0.999
ADAM_EPS = 1e-08
ADAM_WD = 0.01
ADAM_STEP = 10

NT = (((1,), (1,)), ((), ()))
TN = (((0,), (0,)), ((), ()))
MESH_ID = pl.DeviceIdType.MESH


def _cp(sem=None, vmem_mb=None):
    kw = {}
    if sem is not None:
        kw["dimension_semantics"] = sem
    if vmem_mb is not None:
        kw["vmem_limit_bytes"] = vmem_mb << 20
    return pltpu.CompilerParams(**kw)


def _full(shape):
    n = len(shape)
    return pl.BlockSpec(shape, lambda *_: (0,) * n)


def _rows(tm, w):
    return pl.BlockSpec((tm, w), lambda i: (i, 0))


ANY = pl.BlockSpec(memory_space=pl.ANY)


def _rms(x, g):
    r = lax.rsqrt(jnp.mean(x * x, axis=-1, keepdims=True) + EPS)
    return x * r * g


def _rms_bwd(x, g, dy):
    r = lax.rsqrt(jnp.mean(x * x, axis=-1, keepdims=True) + EPS)
    xh = x * r
    dg = jnp.sum(dy * xh, axis=0, keepdims=True)
    dyg = dy * g
    dx = r * (dyg - xh * jnp.mean(dyg * xh, axis=-1, keepdims=True))
    return dx, dg


def _out(shape, dtype):
    return pltpu.HBM(tuple(shape), dtype)


def _in_hbm(*args):
    return [pltpu.with_memory_space_constraint(a, pltpu.HBM) if jnp.issubdtype(a.dtype, jnp.floating) and a.ndim > 1
            else a for a in args]


def _dot(a, b):
    return jnp.dot(a, b, preferred_element_type=F32)


def _dg(a, b, dims):
    return lax.dot_general(a, b, dims, preferred_element_type=F32)


def _row_halves(tm):
    assert tm % 32 == 0
    return (slice(0, tm // 2), slice(tm // 2, tm))


def _ffn_fwd(name, h, g_pre, g_post, wg, wu, wd, tm, comm=None, bounds=()):
    tp = h.shape[0]
    nt = tp // tm

    def body(h_ref, gp_ref, gq_ref, wg_ref, wu_ref, wd_ref, hn_ref, gate_ref, up_ref, f_ref, xn_s, acc_s):
        c = pl.program_id(1)

        @pl.when(c == 0)
        def _():
            xn_s[...] = _rms(h_ref[...], gp_ref[...]).astype(BF16)
            acc_s[...] = jnp.zeros_like(acc_s)

        for rows in _row_halves(tm):
            xn = xn_s[rows, :]
            gate = _dg(xn, wg_ref[0], NT)
            up = _dg(xn, wu_ref[0], NT)
            gate_ref[0, rows, :] = gate
            up_ref[0, rows, :] = up
            act = (gate * jax.nn.sigmoid(gate) * up).astype(BF16)
            acc_s[rows, :] += _dot(act, wd_ref[0])

        @pl.when(c == N_CHIP - 1)
        def _():
            f = acc_s[...]
            f_ref[...] = f
            hn_ref[...] = h_ref[...] + 0.5 * _rms(f, gq_ref[...])

    return _call(
        body, comm, bounds, (h, g_pre, g_post, wg, wu, wd), name=name, grid=(nt, N_CHIP),
        in_specs=[pl.BlockSpec((tm, D), lambda i, c: (i, 0)), _full((1, D)), _full((1, D))] +
                 [pl.BlockSpec((1, FC, D), lambda i, c: (c, 0, 0))] * 3,
        out_specs=[pl.BlockSpec((tm, D), lambda i, c: (i, 0)),
                   pl.BlockSpec((1, tm, FC), lambda i, c: (c, i, 0)),
                   pl.BlockSpec((1, tm, FC), lambda i, c: (c, i, 0)),
                   pl.BlockSpec((tm, D), lambda i, c: (i, 0))],
        out_shape=[_out((tp, D), F32), _out((N_CHIP, tp, FC), F32),
                   _out((N_CHIP, tp, FC), F32), _out((tp, D), F32)],
        scratch_shapes=[pltpu.VMEM((tm, D), BF16), pltpu.VMEM((tm, D), F32)],
        compiler_params=_cp(("arbitrary", "arbitrary"), 48))


def _ffn_bwd(name, h, g_pre, df, gate, up, wg, wu, wd, tm, comm=None, bounds=()):
    tp = h.shape[0]
    nt = tp // tm
    rh = FC // 2

    def body(h_ref, gp_ref, df_ref, gate_ref, up_ref, wg_ref, wu_ref, wd_ref,
             dwg_ref, dwu_ref, dwd_ref, dxn_ref, rg_ref, ru_ref, rd_ref, ag, au, ad, send_sems, recv_sems):
        c = pl.program_id(0)
        i = pl.program_id(1)

        def to_sibling(a, piece):
            x, y, core = _mesh_pos()
            dw_ref, r_ref = ((dwg_ref, rg_ref), (dwu_ref, ru_ref), (dwd_ref, rd_ref))[a]
            return _remote(dw_ref.at[piece, pl.ds((1 - core) * rh, rh), :], r_ref.at[piece], send_sems, recv_sems,
                           3 * piece + a, (x, y, 1 - core))

        @pl.when(i == 0)
        def _():
            ag[...] = jnp.zeros_like(ag)
            au[...] = jnp.zeros_like(au)
            ad[...] = jnp.zeros_like(ad)

        parts = []
        for rows in _row_halves(tm):
            xn = _rms(h_ref[rows, :], gp_ref[...]).astype(BF16)
            dfb = df_ref[rows, :].astype(BF16)
            gt = gate_ref[0, rows, :]
            u = up_ref[0, rows, :]
            sg = jax.nn.sigmoid(gt)
            si = gt * sg
            act = (si * u).astype(BF16)
            dact = _dg(dfb, wd_ref[0], NT)
            dgate = (dact * u * (sg * (1.0 + gt * (1.0 - sg)))).astype(BF16)
            dup = (dact * si).astype(BF16)
            dxn_ref[0, rows, :] = _dot(dgate, wg_ref[0]) + _dot(dup, wu_ref[0])
            parts.append((xn, dfb, act, dgate, dup))
        xn, dfb, act, dgate, dup = (jnp.concatenate(p, axis=0) for p in zip(*parts))
        ad[...] += _dg(act, dfb, TN)
        ag[...] += _dg(dgate, xn, TN)
        au[...] += _dg(dup, xn, TN)

        @pl.when(i == nt - 1)
        def _():
            pltpu.sync_copy(ag, dwg_ref.at[c])
            pltpu.sync_copy(au, dwu_ref.at[c])
            pltpu.sync_copy(ad, dwd_ref.at[c])
            for a in range(3):
                to_sibling(a, c).start()

        @pl.when((c == N_CHIP - 1) & (i == nt - 1))
        def _():
            for piece in range(N_CHIP):
                for a in range(3):
                    to_sibling(a, piece).wait()

    return _call(
        body, comm, bounds, (h, g_pre, df, gate, up, wg, wu, wd), name=name, grid=(N_CHIP, nt),
        in_specs=[pl.BlockSpec((tm, D), lambda c, i: (i, 0)), _full((1, D)),
                  pl.BlockSpec((tm, D), lambda c, i: (i, 0)),
                  pl.BlockSpec((1, tm, FC), lambda c, i: (c, i, 0)),
                  pl.BlockSpec((1, tm, FC), lambda c, i: (c, i, 0))] +
                 [pl.BlockSpec((1, FC, D), lambda c, i: (c, 0, 0))] * 3,
        out_specs=[ANY, ANY, ANY, pl.BlockSpec((1, tm, D), lambda c, i: (c, i, 0)), ANY, ANY, ANY],
        out_shape=[_out((N_CHIP, FC, D), F32)] * 3 + [_out((N_CHIP, tp, D), F32)] +
                  [_out((N_CHIP, rh, D), F32)] * 3,
        scratch_shapes=[pltpu.VMEM((FC, D), F32)] * 3 +
                       [pltpu.SemaphoreType.DMA((3 * N_CHIP,)), pltpu.SemaphoreType.DMA((3 * N_CHIP,))],
        compiler_params=_cp(("arbitrary", "arbitrary"), 58))


def _ffn_pre_bwd(name, dh, dxn_part, h, g_pre, tm, comm=None, bounds=()):
    tp = h.shape[0]
    nt = tp // tm

    def body(dh_ref, dxn_ref, h_ref, gp_ref, out_ref, dg_ref):
        i = pl.program_id(0)
        dxn = (dxn_ref[0] + dxn_ref[1]) + (dxn_ref[2] + dxn_ref[3])
        dx, dg = _rms_bwd(h_ref[...], gp_ref[...], dxn)
        out_ref[...] = dh_ref[...] + dx

        @pl.when(i == 0)
        def _():
            dg_ref[...] = jnp.zeros_like(dg_ref)

        dg_ref[...] += dg

    return _call(
        body, comm, bounds, (dh, dxn_part, h, g_pre), name=name, grid=(nt,),
        in_specs=[_rows(tm, D), pl.BlockSpec((N_CHIP, tm, D), lambda i: (0, i, 0)), _rows(tm, D), _full((1, D))],
        out_specs=[_rows(tm, D), _full((1, D))],
        out_shape=[_out((tp, D), F32), _out((1, D), F32)],
        compiler_params=_cp(("arbitrary",), 48))


def _mix_in(h, g, w_in, tm):
    tp = h.shape[0]

    def body(h_ref, g_ref, w_ref, q_ref, k_ref, v_ref, u_ref):
        a = _rms(h_ref[...], g_ref[...]).astype(BF16)
        q_ref[...] = _dot(a, w_ref[0]).astype(BF16)
        k_ref[...] = _dot(a, w_ref[1]).astype(BF16)
        v_ref[...] = _dot(a, w_ref[2]).astype(BF16)
        u_ref[...] = _dot(a, w_ref[3])

    return pl.pallas_call(
        body, name="mix_in", grid=(tp // tm,),
        in_specs=[_rows(tm, D), _full((1, D)), _full((N_CHIP, D, NA_W))],
        out_specs=[_rows(tm, NA_W)] * 4,
        out_shape=[_out((tp, NA_W), BF16)] * 3 + [_out((tp, S5_W), F32)],
        compiler_params=_cp(("arbitrary",), 40),
    )(*_in_hbm(h, g, w_in))


def _gelu(x):
    return jax.nn.gelu(x, approximate=True)


def _gelu_grad(x):
    k = math.sqrt(2.0 / math.pi)
    t = jnp.tanh(k * (x + 0.044715 * x * x * x))
    return 0.5 * (1.0 + t) + 0.5 * x * (1.0 - t * t) * k * (1.0 + 3.0 * 0.044715 * x * x)


def _mix_out(o_na, y_pre, h, w_glu, b_glu, g_na, g_s5, w_out, g_post, tm, comm=None, bounds=()):
    tp = h.shape[0]

    def body(ona_ref, yp_ref, h_ref, wglu_ref, bglu_ref, gna_ref, gs5_ref, wout_ref, gpost_ref, hn_ref, mix_ref):
        y = _gelu(yp_ref[...])
        z = _dot(y.astype(BF16), wglu_ref[...]) + bglu_ref[...]
        o_s5 = y * jax.nn.sigmoid(z)
        n1 = _rms(ona_ref[...], gna_ref[...]).astype(BF16)
        n2 = _rms(o_s5, gs5_ref[...]).astype(BF16)
        mix = _dot(n1, wout_ref[0:NA_W, :]) + _dot(n2, wout_ref[NA_W:, :])
        mix_ref[...] = mix
        hn_ref[...] = h_ref[...] + _rms(mix, gpost_ref[...])

    return _call(
        body, comm, bounds, (o_na, y_pre, h, w_glu, b_glu, g_na, g_s5, w_out, g_post), name="mix_out",
        grid=(tp // tm,),
        in_specs=[_rows(tm, NA_W), _rows(tm, S5_W), _rows(tm, D), _full((S5_W, S5_W)), _full((1, S5_W)),
                  _full((1, NA_W)), _full((1, S5_W)), _full((D, D)), _full((1, D))],
        out_specs=[_rows(tm, D), _rows(tm, D)],
        out_shape=[_out((tp, D), F32)] * 2,
        compiler_params=_cp(("arbitrary",), 40))


def _mix_out_bwd(dh, mix, o_na, y_pre, w_glu, b_glu, g_na, g_s5, w_out, g_post, tm):
    tp = dh.shape[0]
    nt = tp // tm

    def body(dh_ref, mix_ref, ona_ref, yp_ref, wglu_ref, bglu_ref, gna_ref, gs5_ref, wout_ref, gpost_ref,
             dona_ref, dyp_ref, dwout_ref, dwglu_ref, dgpost_ref, dgna_ref, dgs5_ref, dbglu_ref, a_out, a_glu):
        i = pl.program_id(0)

        @pl.when(i == 0)
        def _():
            a_out[...] = jnp.zeros_like(a_out)
            a_glu[...] = jnp.zeros_like(a_glu)
            dgpost_ref[...] = jnp.zeros_like(dgpost_ref)
            dgna_ref[...] = jnp.zeros_like(dgna_ref)
            dgs5_ref[...] = jnp.zeros_like(dgs5_ref)
            dbglu_ref[...] = jnp.zeros_like(dbglu_ref)

        dmix, dgpost = _rms_bwd(mix_ref[...], gpost_ref[...], dh_ref[...])
        dgpost_ref[...] += dgpost
        yp = yp_ref[...]
        y = _gelu(yp)
        yb = y.astype(BF16)
        z = _dot(yb, wglu_ref[...]) + bglu_ref[...]
        sg = jax.nn.sigmoid(z)
        o_s5 = y * sg
        o_na = ona_ref[...]
        n1 = _rms(o_na, gna_ref[...]).astype(BF16)
        n2 = _rms(o_s5, gs5_ref[...]).astype(BF16)
        dmb = dmix.astype(BF16)
        a_out[0:NA_W, :] += _dg(n1, dmb, TN)
        a_out[NA_W:, :] += _dg(n2, dmb, TN)
        dn1 = _dg(dmb, wout_ref[0:NA_W, :], NT)
        dn2 = _dg(dmb, wout_ref[NA_W:, :], NT)
        dona, dgna = _rms_bwd(o_na, gna_ref[...], dn1)
        dona_ref[...] = dona
        dgna_ref[...] += dgna
        dos5, dgs5 = _rms_bwd(o_s5, gs5_ref[...], dn2)
        dgs5_ref[...] += dgs5
        dz = dos5 * y * (sg * (1.0 - sg))
        dbglu_ref[...] += jnp.sum(dz, axis=0, keepdims=True)
        dzb = dz.astype(BF16)
        a_glu[...] += _dg(yb, dzb, TN)
        dy = dos5 * sg + _dg(dzb, wglu_ref[...], NT)
        dyp_ref[...] = dy * _gelu_grad(yp)

        @pl.when(i == nt - 1)
        def _():
            pltpu.sync_copy(a_out, dwout_ref)
            pltpu.sync_copy(a_glu, dwglu_ref)

    return pl.pallas_call(
        body, name="mix_out_bwd", grid=(nt,),
        in_specs=[_rows(tm, D), _rows(tm, D), _rows(tm, NA_W), _rows(tm, S5_W), _full((S5_W, S5_W)),
                  _full((1, S5_W)), _full((1, NA_W)), _full((1, S5_W)), _full((D, D)), _full((1, D))],
        out_specs=[_rows(tm, NA_W), _rows(tm, S5_W), ANY, ANY, _full((1, D)), _full((1, NA_W)),
                   _full((1, S5_W)), _full((1, S5_W))],
        out_shape=[_out((tp, NA_W), F32), _out((tp, S5_W), F32),
                   _out((D, D), F32), _out((S5_W, S5_W), F32),
                   _out((1, D), F32), _out((1, NA_W), F32),
                   _out((1, S5_W), F32), _out((1, S5_W), F32)],
        scratch_shapes=[pltpu.VMEM((D, D), F32), pltpu.VMEM((S5_W, S5_W), F32)],
        compiler_params=_cp(("arbitrary",), 48),
    )(*_in_hbm(dh, mix, o_na, y_pre, w_glu, b_glu, g_na, g_s5, w_out, g_post))


def _mix_in_bwd(dq, dk, dv, du, h, g, w_in, dh, f1, g_post1, tm, comm=None, bounds=()):
    tp = h.shape[0]
    nt = tp // tm

    def body(dq_ref, dk_ref, dv_ref, du_ref, h_ref, g_ref, w_ref, dh_ref, f_ref, gq_ref,
             dh1_ref, df_ref, dw_ref, dg_ref, dgq_ref, acc):
        i = pl.program_id(0)

        @pl.when(i == 0)
        def _():
            acc[...] = jnp.zeros_like(acc)
            dg_ref[...] = jnp.zeros_like(dg_ref)
            dgq_ref[...] = jnp.zeros_like(dgq_ref)

        x = h_ref[...]
        a = _rms(x, g_ref[...]).astype(BF16)
        da = jnp.zeros((tm, D), F32)
        for j, r in enumerate((dq_ref, dk_ref, dv_ref, du_ref)):
            dp = r[...].astype(BF16)
            da = da + _dg(dp, w_ref[j], NT)
            acc[j] += _dg(a, dp, TN)
        dx, dg = _rms_bwd(x, g_ref[...], da)
        dh1 = dh_ref[...] + dx
        dh1_ref[...] = dh1
        dg_ref[...] += dg
        df, dgq = _rms_bwd(f_ref[...], gq_ref[...], 0.5 * dh1)
        df_ref[...] = df
        dgq_ref[...] += dgq

        @pl.when(i == nt - 1)
        def _():
            pltpu.sync_copy(acc, dw_ref)

    return _call(
        body, comm, bounds, (dq, dk, dv, du, h, g, w_in, dh, f1, g_post1), name="mix_in_bwd", grid=(nt,),
        in_specs=[_rows(tm, NA_W)] * 4 + [_rows(tm, D), _full((1, D)), _full((N_CHIP, D, NA_W)), _rows(tm, D),
                                         _rows(tm, D), _full((1, D))],
        out_specs=[_rows(tm, D), _rows(tm, D), ANY, _full((1, D)), _full((1, D))],
        out_shape=[_out((tp, D), F32), _out((tp, D), F32),
                   _out((N_CHIP, D, NA_W), F32), _out((1, D), F32),
                   _out((1, D), F32)],
        scratch_shapes=[pltpu.VMEM((N_CHIP, D, NA_W), F32)],
        compiler_params=_cp(("arbitrary",), 48))


def _final_loss(h, g_final, target, f2, g_post2, n_tok, tm):
    tp = h.shape[0]

    def body(h_ref, g_ref, t_ref, f_ref, gq_ref, dh_ref, df_ref, loss_ref, dg_ref, dgq_ref):
        i = pl.program_id(0)

        @pl.when(i == 0)
        def _():
            loss_ref[...] = jnp.zeros_like(loss_ref)
            dg_ref[...] = jnp.zeros_like(dg_ref)
            dgq_ref[...] = jnp.zeros_like(dgq_ref)

        x = h_ref[...]
        y = _rms(x, g_ref[...])
        row = i * tm + lax.broadcasted_iota(jnp.int32, (tm, 1), 0)
        valid = (row >= N_META) & (row < N_META + n_tok)
        e = jnp.where(valid, y - t_ref[...], 0.0)
        loss_ref[...] += 0.5 * jnp.sum(jnp.mean(e * e, axis=-1, keepdims=True), axis=0, keepdims=True)
        dx, dg = _rms_bwd(x, g_ref[...], e * (1.0 / D))
        dh_ref[...] = dx
        dg_ref[...] += dg
        df, dgq = _rms_bwd(f_ref[...], gq_ref[...], 0.5 * dx)
        df_ref[...] = df
        dgq_ref[...] += dgq

    return pl.pallas_call(
        body, name="final_loss", grid=(tp // tm,),
        in_specs=[_rows(tm, D), _full((1, D)), _rows(tm, D), _rows(tm, D), _full((1, D))],
        out_specs=[_rows(tm, D), _rows(tm, D), _full((1, 1)), _full((1, D)), _full((1, D))],
        out_shape=[_out((tp, D), F32), _out((tp, D), F32),
                   _out((1, 1), F32), _out((1, D), F32),
                   _out((1, D), F32)],
        compiler_params=_cp(("arbitrary",), 40),
    )(*_in_hbm(h, g_final, target, f2, g_post2))


def _na_patterns(n_rows):
    pats = []
    for kind in range(3):
        pat = [[-1] * K_ROWS for _ in range(Q_ROWS)]
        for i in range(Q_ROWS):
            for jj in range(K_ROWS):
                if kind == 0 and jj < KH:
                    pat[i][jj] = jj - i + KH - 1
                elif kind == 1 and i <= jj < i + KH:
                    pat[i][jj] = jj - i + 3
                elif kind == 2 and K_ROWS - KH <= jj:
                    pat[i][jj] = jj - i - 1
        pats.append(pat)
    return pats


def _diag_onehot():
    q = np.arange(GRID_W)[:, None]
    kc = np.arange(GRID_W)[None, :]
    start = np.clip(q - KW // 2, 0, GRID_W - KW)
    col_in = (kc >= start) & (kc < start + KW)
    e = np.zeros((32, GRID_W, GRID_W), np.float32)
    for d in range(2 * KW - 1):
        e[d] = ((kc - q + KW - 1) == d) & col_in
    return e.reshape(32, GRID_W * GRID_W), col_in


def _rpb_collapse(dtb2, et):
    def body(d_ref, e_ref, o_ref):
        o_ref[...] = jnp.dot(d_ref[...], e_ref[...], preferred_element_type=F32, precision=lax.Precision.HIGHEST)

    out = (dtb2.shape[0], et.shape[1])
    return pl.pallas_call(
        body, name="rpb_collapse", grid=(1,), out_shape=_out(out, F32),
        in_specs=[_full(dtb2.shape), _full(et.shape)], out_specs=_full(out),
    )(*_in_hbm(dtb2, et))


def _bias_tables(rpb, n_rows, comm=None, bounds=()):
    n_dr, n_dc = 2 * KH - 1, 2 * KW - 1
    pats = _na_patterns(n_rows)

    def body(rpb_ref, o_ref):
        h = pl.program_id(0)
        q = lax.broadcasted_iota(jnp.int32, (GRID_W, GRID_W), 0)
        kc = lax.broadcasted_iota(jnp.int32, (GRID_W, GRID_W), 1)
        start = jnp.clip(q - KW // 2, 0, GRID_W - KW)
        col_in = (kc >= start) & (kc < start + KW)
        diff = kc - q + (KW - 1)
        neg = jnp.full((GRID_W, GRID_W), NEG_INF, F32)
        band = []
        for dr in range(n_dr):
            acc = neg
            for d in range(n_dc):
                acc = jnp.where((diff == d) & col_in, rpb_ref[(h * n_dr + dr) * n_dc + d], acc)
            band.append(acc)
        for kind, pat in enumerate(pats):
            for i in range(Q_ROWS):
                for jj in range(K_ROWS):
                    o_ref[kind, 0, i * GRID_W:(i + 1) * GRID_W, jj * GRID_W:(jj + 1) * GRID_W] = (
                        band[pat[i][jj]] if pat[i][jj] >= 0 else neg)

    (bias,), got = _call(
        body, comm, bounds, (rpb.reshape(-1),), name="bias_tables", grid=(N_HEADS,),
        in_specs=[pl.BlockSpec(memory_space=pltpu.SMEM)],
        out_specs=[pl.BlockSpec((3, 1, QB, KB), lambda h: (0, h, 0, 0))],
        out_shape=[_out((3, N_HEADS, QB, KB), F32)],
        compiler_params=_cp(("arbitrary",), 32))
    return bias, got


def _attn_geometry(n_tok):
    n_rows = n_tok // GRID_W
    assert n_rows % Q_ROWS == 0 and n_rows >= K_ROWS
    return n_rows, n_rows // Q_ROWS


def _attn_probs(qh, kh, kmh, bias, scale):
    s = _dg(qh, kh, NT) * scale + bias
    sm = _dg(qh, kmh, NT) * scale
    m = jnp.maximum(jnp.max(s, axis=-1, keepdims=True), jnp.max(sm, axis=-1, keepdims=True))
    p = jnp.exp(s - m)
    pm = jnp.exp(sm - m)
    inv = 1.0 / (jnp.sum(p, axis=-1, keepdims=True) + jnp.sum(pm, axis=-1, keepdims=True))
    return p * inv, pm * inv


def _meta_probs(qmh, kmh, scale):
    s = _dg(qmh, kmh, NT) * scale
    p = jnp.exp(s - jnp.max(s, axis=-1, keepdims=True))
    return p / jnp.sum(p, axis=-1, keepdims=True)


def _step_rows(r, n_rows):
    q0 = pl.multiple_of(N_META + r * QB, 16)
    k0 = pl.multiple_of(N_META + jnp.clip(Q_ROWS * r - (K_ROWS - KH), 0, n_rows - K_ROWS) * GRID_W, 16)
    return q0, k0


def _attn_fwd(q, k, v, bias, n_tok, comm=None, bounds=()):
    tp = q.shape[0]
    n_rows, n_steps = _attn_geometry(n_tok)
    scale = HEAD_DIM ** -0.5

    def body(q_ref, k_ref, v_ref, b_ref, o_ref):
        r = pl.program_id(1)
        km = k_ref[0:N_META, :]
        vm = v_ref[0:N_META, :]

        @pl.when(r == 0)
        def _():
            qm = q_ref[0:N_META, :]
            outs = []
            for hh in range(2):
                sl = slice(hh * HEAD_DIM, (hh + 1) * HEAD_DIM)
                p = _meta_probs(qm[:, sl], km[:, sl], scale)
                outs.append(_dot(p.astype(BF16), vm[:, sl]))
            o_ref[0:N_META, :] = jnp.concatenate(outs, axis=1)
            o_ref[N_META + n_tok:, :] = jnp.zeros((tp - N_META - n_tok, 2 * HEAD_DIM), F32)

        q0, k0 = _step_rows(r, n_rows)
        qb = q_ref[pl.ds(q0, QB), :]
        kb = k_ref[pl.ds(k0, KB), :]
        vb = v_ref[pl.ds(k0, KB), :]
        outs = []
        for hh in range(2):
            sl = slice(hh * HEAD_DIM, (hh + 1) * HEAD_DIM)
            p, pm = _attn_probs(qb[:, sl], kb[:, sl], km[:, sl], b_ref[0, hh], scale)
            outs.append(_dot(p.astype(BF16), vb[:, sl]) + _dot(pm.astype(BF16), vm[:, sl]))
        o_ref[pl.ds(q0, QB), :] = jnp.concatenate(outs, axis=1)

    def bias_map(hp, r):
        return (jnp.where(r == 0, 0, jnp.where(r == n_steps - 1, 2, 1)), hp, 0, 0)

    col = pl.BlockSpec((tp, 2 * HEAD_DIM), lambda hp, r: (0, hp))
    return _call(
        body, comm, bounds, (q, k, v, bias), name="attn_fwd", grid=(N_HEADS // 2, n_steps),
        in_specs=[col, col, col, pl.BlockSpec((1, 2, QB, KB), bias_map)],
        out_specs=[col], out_shape=[_out((tp, NA_W), F32)],
        compiler_params=_cp(("arbitrary", "arbitrary"), 40))


def _attn_bwd(q, k, v, bias, do, n_tok, comm=None, bounds=()):
    tp = q.shape[0]
    n_rows, n_steps = _attn_geometry(n_tok)
    scale = HEAD_DIM ** -0.5
    pats = _na_patterns(n_rows)

    def body(q_ref, k_ref, v_ref, b_ref, do_ref, dq_ref, dk_ref, dv_ref, dtb_ref):
        r = pl.program_id(1)
        km = k_ref[0:N_META, :]
        vm = v_ref[0:N_META, :]

        @pl.when(r == 0)
        def _():
            dk_ref[...] = jnp.zeros_like(dk_ref)
            dv_ref[...] = jnp.zeros_like(dv_ref)
            dtb_ref[...] = jnp.zeros_like(dtb_ref)
            dq_ref[N_META + n_tok:, :] = jnp.zeros((tp - N_META - n_tok, 2 * HEAD_DIM), F32)
            qm = q_ref[0:N_META, :]
            dom = do_ref[0:N_META, :].astype(BF16)
            dqs, dks, dvs = [], [], []
            for hh in range(2):
                sl = slice(hh * HEAD_DIM, (hh + 1) * HEAD_DIM)
                p = _meta_probs(qm[:, sl], km[:, sl], scale)
                dp = _dg(dom[:, sl], vm[:, sl], NT)
                ds = (p * (dp - jnp.sum(dp * p, axis=-1, keepdims=True))).astype(BF16)
                dvs.append(_dg(p.astype(BF16), dom[:, sl], TN))
                dqs.append(_dot(ds, km[:, sl]) * scale)
                dks.append(_dg(ds, qm[:, sl], TN) * scale)
            dq_ref[0:N_META, :] = jnp.concatenate(dqs, axis=1)
            dk_ref[0:N_META, :] += jnp.concatenate(dks, axis=1)
            dv_ref[0:N_META, :] += jnp.concatenate(dvs, axis=1)

        q0, k0 = _step_rows(r, n_rows)
        qb = q_ref[pl.ds(q0, QB), :]
        kb = k_ref[pl.ds(k0, KB), :]
        vb = v_ref[pl.ds(k0, KB), :]
        dob = do_ref[pl.ds(q0, QB), :].astype(BF16)
        dqs, dks, dvs, dkms, dvms, dss = [], [], [], [], [], []
        for hh in range(2):
            sl = slice(hh * HEAD_DIM, (hh + 1) * HEAD_DIM)
            qh, kh, vh, kmh, vmh, doh = qb[:, sl], kb[:, sl], vb[:, sl], km[:, sl], vm[:, sl], dob[:, sl]
            p, pm = _attn_probs(qh, kh, kmh, b_ref[0, hh], scale)
            dp = _dg(doh, vh, NT)
            dpm = _dg(doh, vmh, NT)
            delta = jnp.sum(dp * p, axis=-1, keepdims=True) + jnp.sum(dpm * pm, axis=-1, keepdims=True)
            ds = p * (dp - delta)
            dsb = ds.astype(BF16)
            dsmb = (pm * (dpm - delta)).astype(BF16)
            dss.append(ds)
            dvs.append(_dg(p.astype(BF16), doh, TN))
            dvms.append(_dg(pm.astype(BF16), doh, TN))
            dqs.append((_dot(dsb, kh) + _dot(dsmb, kmh)) * scale)
            dks.append(_dg(dsb, qh, TN) * scale)
            dkms.append(_dg(dsmb, qh, TN) * scale)
        dq_ref[pl.ds(q0, QB), :] = jnp.concatenate(dqs, axis=1)
        dk_ref[pl.ds(k0, KB), :] += jnp.concatenate(dks, axis=1)
        dv_ref[pl.ds(k0, KB), :] += jnp.concatenate(dvs, axis=1)
        dk_ref[0:N_META, :] += jnp.concatenate(dkms, axis=1)
        dv_ref[0:N_META, :] += jnp.concatenate(dvms, axis=1)

        def add_bias_grad(pat):
            for hh in range(2):
                for i in range(Q_ROWS):
                    for jj in range(K_ROWS):
                        if pat[i][jj] >= 0:
                            dtb_ref[hh, pat[i][jj]] += dss[hh][i * GRID_W:(i + 1) * GRID_W,
                                                               jj * GRID_W:(jj + 1) * GRID_W]

        @pl.when(r == 0)
        def _():
            add_bias_grad(pats[0])

        @pl.when((r > 0) & (r < n_steps - 1))
        def _():
            add_bias_grad(pats[1])

        @pl.when(r == n_steps - 1)
        def _():
            add_bias_grad(pats[2])

    def bias_map(hp, r):
        return (jnp.where(r == 0, 0, jnp.where(r == n_steps - 1, 2, 1)), hp, 0, 0)

    col = pl.BlockSpec((tp, 2 * HEAD_DIM), lambda hp, r: (0, hp))
    n_dr = 2 * KH - 1
    return _call(
        body, comm, bounds, (q, k, v, bias, do), name="attn_bwd", grid=(N_HEADS // 2, n_steps),
        in_specs=[col, col, col, pl.BlockSpec((1, 2, QB, KB), bias_map), col],
        out_specs=[col, col, col, pl.BlockSpec((2, n_dr, GRID_W, GRID_W), lambda hp, r: (hp, 0, 0, 0))],
        out_shape=[_out((tp, NA_W), F32)] * 3 +
                  [_out((N_HEADS, n_dr, GRID_W, GRID_W), F32)],
        compiler_params=_cp(("arbitrary", "arbitrary"), 48))


def _repeat_onehot():
    return np.repeat(np.eye(2 * S5_G, dtype=np.float32), S5_H, axis=0)


def _s5_disc_math(lam_re, lam_im, log_dt, b_re, b_im, rep):
    dt = jnp.exp(log_dt)
    ea = jnp.exp(lam_re * dt)
    a_re = ea * jnp.cos(lam_im * dt)
    a_im = ea * jnp.sin(lam_im * dt)
    den = lam_re * lam_re + lam_im * lam_im
    c_re = ((a_re - 1.0) * lam_re + a_im * lam_im) / den
    c_im = (a_im * lam_re - (a_re - 1.0) * lam_im) / den
    ce_re = jnp.dot(rep, c_re, preferred_element_type=F32, precision=lax.Precision.HIGHEST)
    ce_im = jnp.dot(rep, c_im, preferred_element_type=F32, precision=lax.Precision.HIGHEST)
    return a_re, a_im, ce_re * b_re - ce_im * b_im, ce_re * b_im + ce_im * b_re


def _s5_blocks():
    gl = S5_G // N_BUNDLE
    half = gl * S5_P
    out = []
    for d in range(2):
        for g in range(S5_G):
            b, k = divmod(g, gl)
            dg = d * S5_G + g
            out.append((d, b, slice(k * S5_H, (k + 1) * S5_H), slice(k * S5_P, (k + 1) * S5_P),
                        slice(half + k * S5_P, half + (k + 1) * S5_P), slice(dg * S5_H, (dg + 1) * S5_H),
                        slice(dg, dg + 1)))
    return out


def _s5_params(lam_re, lam_im, log_dt, b_re, b_im, c_re, c_im):
    cw, sw = S5_W // N_BUNDLE, 2 * (S5_G // N_BUNDLE) * S5_P

    def body(lr, li, ld, br, bi, cr, ci, rep_ref, a1_ref, a2_ref, bm_ref, cm_ref):
        a_re, a_im, bb_re, bb_im = _s5_disc_math(lr[...], li[...], ld[...], br[...], bi[...], rep_ref[...])
        cc_re = cr[...]
        cc_im = ci[...]
        bm_ref[...] = jnp.zeros_like(bm_ref)
        cm_ref[...] = jnp.zeros_like(cm_ref)
        for d, b, rows, re, im, nat, one in _s5_blocks():
            bm_ref[d, b, rows, re] = bb_re[nat, :].astype(BF16)
            bm_ref[d, b, rows, im] = bb_im[nat, :].astype(BF16)
            cm_ref[d, b, rows, re] = cc_re[nat, :].astype(BF16)
            cm_ref[d, b, rows, im] = (-cc_im[nat, :]).astype(BF16)
            k = rows.start // S5_H
            lanes = slice((k % 2) * S5_P, (k % 2 + 1) * S5_P)
            for part, (v1, v2) in enumerate(((a_re[one, :], a_im[one, :]), (a_re[one, :], -a_im[one, :]))):
                sub = slice(4 * part + k // 2, 4 * part + k // 2 + 1)
                a1_ref[d, b, sub, lanes] = v1
                a2_ref[d, b, sub, lanes] = v2

    args = (lam_re, lam_im, log_dt, b_re, b_im, c_re, c_im, jnp.asarray(_repeat_onehot()))
    outs = [((2, N_BUNDLE, 8, 128), F32)] * 2 + [((2, N_BUNDLE, cw, sw), BF16)] * 2
    return pl.pallas_call(
        body, name="s5_params", grid=(1,), in_specs=[_full(a.shape) for a in args],
        out_specs=[_full(s) for s, _ in outs], out_shape=[_out(s, dt) for s, dt in outs],
    )(*_in_hbm(*args))


def _s5_params_bwd(lam_re, lam_im, log_dt, b_re, b_im, da, dbm, dcm):
    n, nb = 2 * S5_G, 2 * S5_G * S5_H

    def body(lr, li, ld, br, bi, rep_ref, da_ref, dbm_ref, dcm_ref, o_lr, o_li, o_ld, o_br, o_bi, o_cr, o_ci,
             dar_s, dai_s, dbr_s, dbi_s):
        for d, b, rows, re, im, nat, one in _s5_blocks():
            dbr_s[nat, :] = dbm_ref[d, b, rows, re]
            dbi_s[nat, :] = dbm_ref[d, b, rows, im]
            o_cr[nat, :] = dcm_ref[d, b, rows, re]
            o_ci[nat, :] = -dcm_ref[d, b, rows, im]
            dar_s[one, :] = da_ref[d, b, :, re]
            dai_s[one, :] = da_ref[d, b, :, im]
        rep = rep_ref[...]
        _, vjp = jax.vjp(lambda p, q, r, s, t: _s5_disc_math(p, q, r, s, t, rep),
                         lr[...], li[...], ld[...], br[...], bi[...])
        o_lr[...], o_li[...], o_ld[...], o_br[...], o_bi[...] = vjp((dar_s[...], dai_s[...], dbr_s[...], dbi_s[...]))

    args = (lam_re, lam_im, log_dt, b_re, b_im, jnp.asarray(_repeat_onehot()), da, dbm, dcm)
    outs = [(n, S5_P)] * 2 + [(n, 1)] + [(nb, S5_P)] * 4
    return pl.pallas_call(
        body, name="s5_params_bwd", grid=(1,), in_specs=[_full(a.shape) for a in args],
        out_specs=[_full(s) for s in outs], out_shape=[_out(s, F32) for s in outs],
        scratch_shapes=[pltpu.VMEM((n, S5_P), F32)] * 2 + [pltpu.VMEM((nb, S5_P), F32)] * 2,
    )(*_in_hbm(*args))


def _tiles_store(ref, base, val):
    for i in range(val.shape[0] // 8):
        for c in range(8):
            ref[pl.ds(base + (8 * i + c) * 8, 8), :] = val[8 * i:8 * i + 8, 128 * c:128 * (c + 1)]


def _tiles_load(ref, base, n):
    return jnp.concatenate(
        [jnp.concatenate([ref[pl.ds(base + (8 * i + c) * 8, 8), :] for c in range(8)], axis=1) for i in range(n // 8)],
        axis=0)


def _time_rows(base, t):
    return pl.ds(base + (t // 8) * 64 + t % 8, 8, stride=8)


def _scan(chains, n):
    xs = [c["x"] for c in chains]
    for k in range(n):
        for ci, c in enumerate(chains):
            t = n - 1 - k if c["reverse"] else k
            if c["prev"] is not None:
                c["prev"][_time_rows(c["prev_base"], t), :] = xs[ci]
            xs[ci] = c["a1"] * xs[ci] + pltpu.roll(c["a2"] * xs[ci], 4, axis=0) + c["src"][_time_rows(0, t), :]
            if c["dst"] is not None:
                c["dst"][_time_rows(0, t), :] = xs[ci]
    return xs


def _chain(x, a1, a2, src, dst=None, prev=None, prev_base=0, reverse=False):
    return dict(x=x, a1=a1, a2=a2, src=src, dst=dst, prev=prev, prev_base=prev_base, reverse=reverse)


def _s5_fwd(u, d_skip, a1, a2, bm, cm, length, comm=None, bounds=()):
    tp = u.shape[0]
    cw = S5_W // N_BUNDLE
    sw = bm.shape[-1]
    n_full, n_tail = divmod(length, SCAN_CHUNK)
    t_tail = n_full * SCAN_CHUNK

    nbs = N_BUNDLE

    def body(u_ref, d_ref, a1_ref, a2_ref, bm_ref, cm_ref, y_ref, bnd_ref, *scratch):
        y_ref[...] = u_ref[...] * d_ref[...]
        ins, xss = (scratch[0:nbs], scratch[nbs:2 * nbs]), (scratch[2 * nbs:3 * nbs], scratch[3 * nbs:])
        cols = [slice(b * cw, (b + 1) * cw) for b in range(nbs)]

        def keep(dr, chunk, xs):
            for b in range(nbs):
                bnd_ref[dr, b, chunk] = xs[b]

        def load(dr, t0, n):
            for b in range(nbs):
                _tiles_store(ins[dr][b], 0, _dot(u_ref[pl.ds(t0, n), cols[b]].astype(BF16), bm_ref[dr, b]))

        def chains(dr, xs):
            return [_chain(xs[b], a1_ref[dr, b], a2_ref[dr, b], ins[dr][b], dst=xss[dr][b], reverse=dr == 1)
                    for b in range(nbs)]

        def emit(dr, t0, n):
            for b in range(nbs):
                y_ref[pl.ds(t0, n), cols[b]] += _dg(_tiles_load(xss[dr][b], 0, n).astype(BF16), cm_ref[dr, b], NT)

        zero = (jnp.zeros((8, 128), F32),) * nbs
        xb = zero
        if n_tail:
            keep(1, n_full, xb)
            load(1, t_tail, n_tail)
            xb = tuple(_scan(chains(1, xb), n_tail))
            emit(1, t_tail, n_tail)

        def pair(i, carry):
            j = n_full - 1 - i
            t0s = (pl.multiple_of(i * SCAN_CHUNK, SCAN_CHUNK), pl.multiple_of(j * SCAN_CHUNK, SCAN_CHUNK))
            keep(0, i, carry[0])
            keep(1, j, carry[1])
            for dr in range(2):
                load(dr, t0s[dr], SCAN_CHUNK)
            out = _scan(chains(0, carry[0]) + chains(1, carry[1]), SCAN_CHUNK)
            for dr in range(2):
                emit(dr, t0s[dr], SCAN_CHUNK)
            return tuple(out[:nbs]), tuple(out[nbs:])

        xf, _ = lax.fori_loop(0, n_full, pair, (zero, xb))
        if n_tail:
            keep(0, n_full, xf)
            load(0, t_tail, n_tail)
            _scan(chains(0, xf), n_tail)
            emit(0, t_tail, n_tail)

    n_chunks = n_full + (1 if n_tail else 0)
    tile = pl.BlockSpec((2, nbs, 8, 128), lambda b: (0, b, 0, 0))
    return _call(
        body, comm, bounds, (u, d_skip, a1, a2, bm, cm), name="s5_fwd", grid=(N_BUNDLE // nbs,),
        in_specs=[pl.BlockSpec((tp, nbs * cw), lambda b: (0, b)), pl.BlockSpec((1, nbs * cw), lambda b: (0, b)),
                  tile, tile, pl.BlockSpec((2, nbs, cw, sw), lambda b: (0, b, 0, 0)),
                  pl.BlockSpec((2, nbs, cw, sw), lambda b: (0, b, 0, 0))],
        out_specs=[pl.BlockSpec((tp, nbs * cw), lambda b: (0, b)),
                   pl.BlockSpec((2, nbs, n_chunks, 8, 128), lambda b: (0, b, 0, 0, 0))],
        out_shape=[_out((tp, S5_W), F32), _out((2, N_BUNDLE, n_chunks, 8, 128), F32)],
        scratch_shapes=[pltpu.VMEM((SCAN_CHUNK * 8, 128), F32)] * (4 * nbs),
        compiler_params=_cp(("arbitrary",), 48))


def _s5_bwd(u, dy, d_skip, a1, a2, bm, cm, bnd, length):
    tp = u.shape[0]
    cw = S5_W // N_BUNDLE
    sw = bm.shape[-1]
    half = sw // 2
    n_full, n_tail = divmod(length, SCAN_CHUNK)
    t_tail = n_full * SCAN_CHUNK
    n_chunks = bnd.shape[2]
    nbs = 2

    def body(u_ref, dy_ref, d_ref, a1_ref, a2_ref, bm_ref, cm_ref, bnd_ref, du_ref, dd_ref, dbm_ref, dcm_ref,
             da_ref, *scratch):
        du_ref[...] = dy_ref[...] * d_ref[...]
        dd_ref[...] = jnp.sum(dy_ref[...] * u_ref[...], axis=0, keepdims=True)
        dbm_ref[...] = jnp.zeros_like(dbm_ref)
        dcm_ref[...] = jnp.zeros_like(dcm_ref)
        da_ref[...] = jnp.zeros_like(da_ref)
        bu_s, dx_s, g_s, xp_s, x_s = ([scratch[(k * 2 + dr) * nbs:(k * 2 + dr + 1) * nbs] for dr in range(2)]
                                      for k in range(5))
        cols = [slice(b * cw, (b + 1) * cw) for b in range(nbs)]

        def chains(dr, chunk, t0, n, gs):
            out = []
            for b in range(nbs):
                _tiles_store(bu_s[dr][b], 0, _dot(u_ref[pl.ds(t0, n), cols[b]].astype(BF16), bm_ref[dr, b]))
                _tiles_store(dx_s[dr][b], 0, _dot(dy_ref[pl.ds(t0, n), cols[b]].astype(BF16), cm_ref[dr, b]))
                out.append(_chain(bnd_ref[dr, b, chunk], a1_ref[dr, b], a2_ref[dr, b], bu_s[dr][b],
                                  dst=x_s[dr][b], prev=xp_s[dr][b], reverse=dr == 1))
                out.append(_chain(gs[b], a1_ref[dr, b], -a2_ref[dr, b], dx_s[dr][b], dst=g_s[dr][b], reverse=dr == 0))
            return out

        def emit(dr, t0, n):
            rows = pl.ds(t0, n)
            for b in range(nbs):
                ub = u_ref[rows, cols[b]].astype(BF16)
                dyb = dy_ref[rows, cols[b]].astype(BF16)
                g = _tiles_load(g_s[dr][b], 0, n)
                gb = g.astype(BF16)
                du_ref[rows, cols[b]] += _dg(gb, bm_ref[dr, b], NT)
                dbm_ref[dr, b] += _dg(ub, gb, TN)
                xp = _tiles_load(xp_s[dr][b], 0, n)
                xp_r, xp_i = xp[:, 0:half], xp[:, half:]
                g_r, g_i = g[:, 0:half], g[:, half:]
                dcm_ref[dr, b] += _dg(dyb, _tiles_load(x_s[dr][b], 0, n).astype(BF16), TN)
                da_ref[dr, b] += jnp.concatenate([jnp.sum(g_r * xp_r + g_i * xp_i, axis=0, keepdims=True),
                                                  jnp.sum(g_i * xp_r - g_r * xp_i, axis=0, keepdims=True)], axis=1)

        def adjoints(out):
            return tuple(out[1::2])

        zero = (jnp.zeros((8, 128), F32),) * nbs
        g0 = zero
        if n_tail:
            g0 = adjoints(_scan(chains(0, n_full, t_tail, n_tail, g0), n_tail))
            emit(0, t_tail, n_tail)

        def pair(i, carry):
            j = n_full - 1 - i
            t0 = (pl.multiple_of(j * SCAN_CHUNK, SCAN_CHUNK), pl.multiple_of(i * SCAN_CHUNK, SCAN_CHUNK))
            both = chains(0, j, t0[0], SCAN_CHUNK, carry[0]) + chains(1, i, t0[1], SCAN_CHUNK, carry[1])
            out = _scan(both, SCAN_CHUNK)
            emit(0, t0[0], SCAN_CHUNK)
            emit(1, t0[1], SCAN_CHUNK)
            return adjoints(out[:2 * nbs]), adjoints(out[2 * nbs:])

        _, g1 = lax.fori_loop(0, n_full, pair, (g0, zero))
        if n_tail:
            _scan(chains(1, n_full, t_tail, n_tail, g1), n_tail)
            emit(1, t_tail, n_tail)

    tile = pl.BlockSpec((2, nbs, 8, 128), lambda b: (0, b, 0, 0))
    wide = pl.BlockSpec((2, nbs, cw, sw), lambda b: (0, b, 0, 0))
    col = pl.BlockSpec((tp, nbs * cw), lambda b: (0, b))
    row = pl.BlockSpec((1, nbs * cw), lambda b: (0, b))
    arow = pl.BlockSpec((2, nbs, 1, sw), lambda b: (0, b, 0, 0))
    return pl.pallas_call(
        body, name="s5_bwd", grid=(N_BUNDLE // nbs,),
        in_specs=[col, col, row, tile, tile, wide, wide,
                  pl.BlockSpec((2, nbs, n_chunks, 8, 128), lambda b: (0, b, 0, 0, 0))],
        out_specs=[col, row, wide, wide, arow],
        out_shape=[_out((tp, S5_W), F32), _out((1, S5_W), F32),
                   _out((2, N_BUNDLE, cw, sw), F32), _out((2, N_BUNDLE, cw, sw), F32),
                   _out((2, N_BUNDLE, 1, sw), F32)],
        scratch_shapes=[pltpu.VMEM((SCAN_CHUNK * 8, 128), F32)] * (10 * nbs),
        compiler_params=_cp(("arbitrary",), 56),
    )(*_in_hbm(u, dy, d_skip, a1, a2, bm, cm, bnd))


def _row_tile(tp):
    return max(tm for tm in range(16, 449, 16) if tp % tm == 0)


def _step(x, target, bufs, gains, s5, rpb, c_arr, kc_arr, me_arr):
    n_tok = x.shape[0]
    first = ["ffn1_w_gate", "ffn1_w_up", "ffn1_w_down", "meta_tokens"]
    bias, got = _bias_tables(rpb, n_tok // GRID_W, _gather_comm([bufs[n] for n in first]), (0, N_HEADS - 1))
    w = dict(zip(first, got))
    meta = w["meta_tokens"].transpose(1, 0, 2).reshape(N_META, D)
    length = N_META + n_tok
    tp = length + 16
    tm = _row_tile(tp)
    tmb = tm
    n_rows = n_tok // GRID_W
    pad = jnp.zeros((tp - length, D), F32)
    h0 = jnp.concatenate([meta, x, pad], axis=0)
    tgt = jnp.concatenate([jnp.zeros((N_META, D), F32), target, pad], axis=0)

    lam_re, _ = lax.optimization_barrier((s5["lam_re"], bias))
    s5p = (lam_re, s5["lam_im"], s5["log_dt"].reshape(2 * S5_G, 1), s5["b_re"], s5["b_im"])
    a1_m, a2_m, bm16, cm16 = _s5_params(*s5p, s5["c_re"], s5["c_im"])

    mid = ["w_in", "s5_w_glu", "w_out"]
    (h1, gate1, up1, f1), got = _ffn_fwd(
        "ffn1_fwd", h0, gains["ffn1_pre_g"], gains["ffn1_post_g"], w["ffn1_w_gate"], w["ffn1_w_up"], w["ffn1_w_down"],
        tm, _gather_comm([bufs[n] for n in mid]), (0, (tp // tm) * N_CHIP * 3 // 5))
    w.update(zip(mid, got))
    q, k, v, u = _mix_in(h1, gains["mix_pre_g"], w["w_in"], tm)
    (o_na,), (gate_ici, up_ici) = _attn_fwd(
        q, k, v, bias, n_tok, _gather_comm([bufs["ffn2_w_gate"], bufs["ffn2_w_up"]], pair=False), (0,))
    (y_pre, s5_bnd), (w["ffn2_w_gate"], w["ffn2_w_up"], down_ici) = _s5_fwd(
        u, gains["s5_d"], a1_m, a2_m, bm16, cm16, length,
        _merge_comm(_gather_comm([gate_ici, up_ici], ici=False),
                    _gather_comm([bufs["ffn2_w_down"]], pair=False)), (0,))
    w_glu = w["s5_w_glu"].reshape(S5_W, S5_W)
    w_out = w["w_out"].reshape(D, D)
    (h2, mix), (w["ffn2_w_down"],) = _mix_out(
        o_na, y_pre, h1, w_glu, gains["s5_b_glu"], gains["na_out_g"], gains["s5_out_g"], w_out, gains["mix_post_g"], tm,
        _gather_comm([down_ici], ici=False), (0,))
    (h3, gate2, up2, f2), _ = _ffn_fwd("ffn2_fwd", h2, gains["ffn2_pre_g"], gains["ffn2_post_g"],
                                       w["ffn2_w_gate"], w["ffn2_w_up"], w["ffn2_w_down"], tm)
    dh3, df2, loss, dg_final, dg_post2 = _final_loss(h3, gains["final_g"], tgt, f2, gains["ffn2_post_g"], n_tok, tm)

    ffn2 = ["ffn2_w_gate", "ffn2_w_up", "ffn2_w_down"]
    ffn1 = ["ffn1_w_gate", "ffn1_w_up", "ffn1_w_down"]
    out2, _ = _ffn_bwd("ffn2_bwd", h2, gains["ffn2_pre_g"], df2, gate2, up2,
                       w["ffn2_w_gate"], w["ffn2_w_up"], w["ffn2_w_down"], tmb)
    dxn2 = out2[3]
    sums2 = _chip_sums("chip_sums_ffn2", out2[0:3], out2[4:7], c_arr)
    (dh2, dg_pre2), _ = _ffn_pre_bwd("ffn2_pre_bwd", dh3, dxn2, h2, gains["ffn2_pre_g"], tm)
    do_na, dy_pre, dw_out, dw_glu, dg_mpost, dg_na, dg_s5, db_glu = _mix_out_bwd(
        dh2, mix, o_na, y_pre, w_glu, gains["s5_b_glu"], gains["na_out_g"], gains["s5_out_g"], w_out,
        gains["mix_post_g"], tm)
    (dq, dk, dv, dtb), recv3 = _attn_bwd(q, k, v, bias, do_na, n_tok, _scatter_comm(sums2), (0,))
    totals2 = _total_sums("total_sums_ffn2", sums2, recv3, kc_arr)
    du, dd, dbm, dcm, da_m = _s5_bwd(u, dy_pre, gains["s5_d"], a1_m, a2_m, bm16, cm16, s5_bnd, length)
    (dh1, df1, dw_in, dg_mpre, dg_post1), done2 = _mix_in_bwd(
        dq, dk, dv, du, h1, gains["mix_pre_g"], w["w_in"], dh2, f1, gains["ffn1_post_g"], tm,
        _assemble_comm(totals2), (0,))
    pieces = dict(zip(ffn2, done2))

    e, _ = _diag_onehot()
    n_dr = 2 * KH - 1
    drpb = _rpb_collapse(dtb.reshape(N_HEADS * n_dr, GRID_W * GRID_W), jnp.asarray(e.T))
    drpb = drpb[:, :2 * KW - 1].reshape(N_HEADS, n_dr, 2 * KW - 1).transpose(1, 0, 2).reshape(N_HEADS * n_dr, 2 * KW - 1)
    dlam_re, dlam_im, dlog_dt, db_re, db_im, dc_re, dc_im = _s5_params_bwd(*s5p, da_m, dbm, dcm)
    early = {"ffn1_post_g": dg_post1, "mix_pre_g": dg_mpre, "na_rpb": drpb,
             "s5_lam_re": dlam_re, "s5_lam_im": dlam_im, "s5_log_dt": dlog_dt.reshape(2, S5_G),
             "s5_b_re": db_re, "s5_b_im": db_im, "s5_c_re": dc_re, "s5_c_im": dc_im,
             "s5_d": dd, "s5_b_glu": db_glu, "na_out_g": dg_na,
             "s5_out_g": dg_s5, "mix_post_g": dg_mpost, "ffn2_pre_g": dg_pre2, "ffn2_post_g": dg_post2,
             "final_g": dg_final}
    names = list(early)
    slots = _small_pack([early[n] for n in names], me_arr)

    out1, slots = _ffn_bwd("ffn1_bwd", h0, gains["ffn1_pre_g"], df1, gate1, up1,
                           w["ffn1_w_gate"], w["ffn1_w_up"], w["ffn1_w_down"], tmb, _spread_comm(slots), (0,))
    small = dict(zip(names, _small_total(slots, [early[n].shape for n in names])))
    sums1 = _chip_sums("chip_sums_ffn1", out1[0:3], out1[4:7], c_arr)
    flight1 = _scatter_start("ffn1", sums1)
    token = flight1[4]
    rest = [dw_in, dw_glu.reshape(N_CHIP, S5_W // N_CHIP, S5_W), dw_out.reshape(N_CHIP, D // N_CHIP, D)]
    (dh0, dg_pre1), recv_rest = _ffn_pre_bwd("ffn1_pre_bwd", dh1, out1[3], h0, gains["ffn1_pre_g"] + token[0:1, 0:1],
                                             tm, _exchange_comm(rest), (0,))
    sums = _chip_sums("chip_sums_rest", rest, recv_rest, c_arr)
    flight2 = _scatter_start("rest", sums)
    return loss[0, 0], dh0, pieces, small, {"ffn1_pre_g": dg_pre1}, (ffn1, flight1[:4]), (mid, flight2[:4])


def _mesh_pos():
    return lax.axis_index("x"), lax.axis_index("y"), lax.axis_index("c")


def _other_chips(x, y):
    return [(1 - x, y), (x, 1 - y), (1 - x, 1 - y)]


class _Comm:
    def __init__(self, ins, out_shape, aliases, parts):
        self.ins, self.out_shape, self.aliases, self.parts = list(ins), list(out_shape), dict(aliases), list(parts)
        self.n_sems = sum(p[0] for p in parts)

    def bases(self):
        out, base = [], 0
        for n_sems, _, _ in self.parts:
            out.append(base)
            base += n_sems
        return out


def _run_comm(name, comm):
    n_i, n_o = len(comm.ins), len(comm.out_shape)

    def body(*refs):
        ins, outs = refs[:n_i], refs[n_i:n_i + n_o]
        send_sems, recv_sems = refs[n_i + n_o:]
        for base, (_, start, finish) in zip(comm.bases(), comm.parts):
            start(ins, outs, send_sems, recv_sems, base)
            finish(ins, outs, send_sems, recv_sems, base)

    return pl.pallas_call(
        body, name=name, out_shape=comm.out_shape, in_specs=[ANY] * n_i, out_specs=[ANY] * n_o,
        input_output_aliases=comm.aliases,
        scratch_shapes=[pltpu.SemaphoreType.DMA((comm.n_sems,)), pltpu.SemaphoreType.DMA((comm.n_sems,))],
    )(*_in_hbm(*comm.ins))


def _call(body, comm, bounds, args, *, name, grid, in_specs, out_specs, out_shape, scratch_shapes=(),
          compiler_params=None):
    in_specs, out_specs, out_shape, scratch_shapes = list(in_specs), list(out_specs), list(out_shape), list(scratch_shapes)
    if comm is None:
        return pl.pallas_call(body, name=name, grid=grid, in_specs=in_specs, out_specs=out_specs, out_shape=out_shape,
                              scratch_shapes=scratch_shapes, compiler_params=compiler_params)(*_in_hbm(*args)), []
    n_in, n_out, n_scr = len(in_specs), len(out_specs), len(scratch_shapes)
    n_ci, n_co = len(comm.ins), len(comm.out_shape)
    n_steps = int(np.prod(grid))
    assert len(bounds) == len(comm.parts) and all(0 <= b < n_steps for b in bounds) and list(bounds) == sorted(bounds)

    def fused(*refs):
        a = n_in
        b = a + n_ci
        c = b + n_out
        d = c + n_co
        e = d + n_scr
        cargs = (refs[a:b], refs[c:d], refs[e], refs[e + 1])
        step = pl.program_id(0)
        for ax in range(1, len(grid)):
            step = step * grid[ax] + pl.program_id(ax)
        bases = comm.bases()
        for p, (_, start, finish) in enumerate(comm.parts):
            @pl.when(step == bounds[p])
            def _(p=p, start=start):
                if p > 0:
                    comm.parts[p - 1][2](*cargs, bases[p - 1])
                start(*cargs, bases[p])
        body(*(refs[:a] + refs[b:c] + refs[d:e]))

        @pl.when(step == n_steps - 1)
        def _():
            comm.parts[-1][2](*cargs, bases[-1])

    res = pl.pallas_call(
        fused, name=name, grid=grid, in_specs=in_specs + [ANY] * n_ci, out_specs=out_specs + [ANY] * n_co,
        out_shape=out_shape + comm.out_shape,
        scratch_shapes=scratch_shapes + [pltpu.SemaphoreType.DMA((comm.n_sems,)), pltpu.SemaphoreType.DMA((comm.n_sems,))],
        input_output_aliases={n_in + i: n_out + j for i, j in comm.aliases.items()},
        compiler_params=compiler_params)(*_in_hbm(*args, *comm.ins))
    return res[:n_out], res[n_out:]


def _remote(src, dst, send_sems, recv_sems, idx, to):
    return pltpu.make_async_remote_copy(src_ref=src, dst_ref=dst, send_sem=send_sems.at[idx],
                                        recv_sem=recv_sems.at[idx], device_id=to, device_id_type=MESH_ID)


def _gather_comm(bufs, ici=True, pair=True):
    n = len(bufs)

    def half(ref, k, pc):
        rh = ref.shape[1] // 2
        return ref.at[k, pl.ds(pc * rh, rh), :]

    def ici_start(ins, outs, ss, rs, base):
        x, y, c = _mesh_pos()
        for a in range(n):
            mine = half(outs[a], 2 * x + y, c)
            for j, chip in enumerate(_other_chips(x, y)):
                _remote(mine, mine, ss, rs, base + 3 * a + j, (*chip, c)).start()

    def ici_finish(ins, outs, ss, rs, base):
        x, y, c = _mesh_pos()
        for a in range(n):
            for j, chip in enumerate(_other_chips(x, y)):
                theirs = half(outs[a], 2 * chip[0] + chip[1], c)
                _remote(theirs, theirs, ss, rs, base + 3 * a + j, (*chip, c)).wait()

    def pair_copy(outs, ss, rs, base, a):
        x, y, c = _mesh_pos()
        rh = outs[a].shape[1] // 2
        held = outs[a].at[:, pl.ds(c * rh, rh), :]
        return _remote(held, held, ss, rs, base + a, (x, y, 1 - c))

    def pair_start(ins, outs, ss, rs, base):
        for a in range(n):
            pair_copy(outs, ss, rs, base, a).start()

    def pair_finish(ins, outs, ss, rs, base):
        for a in range(n):
            pair_copy(outs, ss, rs, base, a).wait()

    parts = ([(3 * n, ici_start, ici_finish)] if ici else []) + ([(n, pair_start, pair_finish)] if pair else [])
    return _Comm(bufs, [_out(b.shape, b.dtype) for b in bufs], {a: a for a in range(n)}, parts)


def _merge_comm(*comms):
    ins, shapes, aliases, subs, base = [], [], {}, [], 0
    for cm in comms:
        (n_sems, start, finish), = cm.parts
        i0, o0 = len(ins), len(shapes)
        subs.append((slice(i0, i0 + len(cm.ins)), slice(o0, o0 + len(cm.out_shape)), base, start, finish))
        aliases.update({i0 + i: o0 + j for i, j in cm.aliases.items()})
        ins += cm.ins
        shapes += cm.out_shape
        base += n_sems

    def start_all(ins_r, outs_r, ss, rs, b):
        for si, so, off, start, _ in subs:
            start(ins_r[si], outs_r[so], ss, rs, b + off)

    def finish_all(ins_r, outs_r, ss, rs, b):
        for si, so, off, _, finish in subs:
            finish(ins_r[si], outs_r[so], ss, rs, b + off)

    return _Comm(ins, shapes, aliases, [(base, start_all, finish_all)])


def _own_half_buffers(pieces, dtypes, kc_arr):
    n = len(pieces)

    def body(kc_ref, *refs):
        for a in range(n):
            refs[n + a][0] = refs[a][...].astype(dtypes[a])

    def half(p):
        return p.shape[0] // 2, p.shape[1]

    return pl.pallas_call(
        body, name="own_halves",
        out_shape=[_out((N_CHIP,) + p.shape, dt) for p, dt in zip(pieces, dtypes)],
        grid_spec=pltpu.PrefetchScalarGridSpec(
            num_scalar_prefetch=1, grid=(1,),
            in_specs=[pl.BlockSpec(half(p), lambda i, kc: (kc[1], 0)) for p in pieces],
            out_specs=[pl.BlockSpec((1,) + half(p), lambda i, kc: (kc[0], kc[1], 0)) for p in pieces]),
        compiler_params=_cp(("arbitrary",), 48),
    )(kc_arr, *_in_hbm(*pieces))


def _exchange_comm(grads):
    n = len(grads)

    def copy(ins, outs, ss, rs, base, a):
        x, y, c = _mesh_pos()
        rh = ins[a].shape[1] // 2
        return _remote(ins[a].at[:, pl.ds((1 - c) * rh, rh), :], outs[a], ss, rs, base + a, (x, y, 1 - c))

    def start(ins, outs, ss, rs, base):
        for a in range(n):
            copy(ins, outs, ss, rs, base, a).start()

    def finish(ins, outs, ss, rs, base):
        for a in range(n):
            copy(ins, outs, ss, rs, base, a).wait()

    shapes = [_out((N_CHIP, g.shape[1] // 2, g.shape[2]), g.dtype) for g in grads]
    return _Comm(grads, shapes, {}, [(n, start, finish)])


def _chip_sums(name, grads, recvs, c_arr):
    n = len(grads)
    halves = [(1, g.shape[1] // 2, g.shape[2]) for g in grads]

    def body(c_ref, *refs):
        for a in range(n):
            refs[2 * n + a][...] = (refs[a][...] + refs[n + a][...]).astype(BF16)

    return pl.pallas_call(
        body, name=name, out_shape=[_out((N_CHIP,) + h[1:], BF16) for h in halves],
        grid_spec=pltpu.PrefetchScalarGridSpec(
            num_scalar_prefetch=1, grid=(N_CHIP,),
            in_specs=[pl.BlockSpec(h, lambda j, c_ref: (j, c_ref[0], 0)) for h in halves] +
                     [pl.BlockSpec(h, lambda j, c_ref: (j, 0, 0)) for h in halves],
            out_specs=[pl.BlockSpec(h, lambda j, c_ref: (j, 0, 0)) for h in halves]),
        compiler_params=_cp(("arbitrary",), 40),
    )(c_arr, *_in_hbm(*grads, *recvs))


def _scatter_comm(sums):
    n = len(sums)

    def copies(ins, outs, ss, rs, base):
        x, y, c = _mesh_pos()
        return [_remote(ins[a].at[2 * chip[0] + chip[1]], outs[a].at[j], ss, rs, base + 3 * a + j, (*chip, c))
                for a in range(n) for j, chip in enumerate(_other_chips(x, y))]

    def start(ins, outs, ss, rs, base):
        for cp in copies(ins, outs, ss, rs, base):
            cp.start()

    def finish(ins, outs, ss, rs, base):
        for cp in copies(ins, outs, ss, rs, base):
            cp.wait()

    shapes = [_out((3,) + s.shape[1:], s.dtype) for s in sums]
    return _Comm(sums, shapes, {}, [(3 * n, start, finish)])


def _scatter_copies(ins, lands, send_sems, recv_sems):
    x, y, c = _mesh_pos()
    return [_remote(ins[a].at[2 * chip[0] + chip[1]], lands[a].at[j], send_sems, recv_sems, 3 * a + j, (*chip, c))
            for a in range(len(ins)) for j, chip in enumerate(_other_chips(x, y))]


def _scatter_start(name, sums):
    n = len(sums)
    lands = [lax.empty((3,) + s.shape[1:], s.dtype) for s in sums]
    hbm = pl.BlockSpec(memory_space=pltpu.HBM)
    sem = pl.BlockSpec(memory_space=pltpu.SEMAPHORE)

    def body(*refs):
        ins, land_refs = refs[:n], refs[n:2 * n]
        send_sems, recv_sems = refs[2 * n], refs[2 * n + 1]
        token = refs[-1]
        for cp in _scatter_copies(ins, land_refs, send_sems, recv_sems):
            cp.start()
        token[...] = jnp.zeros_like(token)

    res = pl.pallas_call(
        body, name=name + "_scatter_start",
        out_shape=(pltpu.SemaphoreType.DMA((3 * n,)), pltpu.SemaphoreType.DMA((3 * n,)),
                   *[pltpu.HBM(s.shape, s.dtype) for s in sums], *[pltpu.HBM(ld.shape, ld.dtype) for ld in lands],
                   jax.ShapeDtypeStruct((8, 128), F32)),
        in_specs=[hbm] * (2 * n), out_specs=(sem, sem, *[hbm] * (2 * n), pl.BlockSpec(memory_space=pltpu.VMEM)),
        input_output_aliases={i: 2 + i for i in range(2 * n)},
        compiler_params=pltpu.CompilerParams(has_side_effects=pltpu.SideEffectType.DATAFLOW_SIDE_EFFECTING),
    )(*[pltpu.with_memory_space_constraint(a, pltpu.HBM) for a in list(sums) + lands])
    return res[0], res[1], list(res[2:2 + n]), list(res[2 + n:2 + 2 * n]), res[-1]


def _scatter_wait(name, send_sems, recv_sems, sums, lands, after):
    n = len(sums)
    hbm = pl.BlockSpec(memory_space=pltpu.HBM)
    sem = pl.BlockSpec(memory_space=pltpu.SEMAPHORE)

    def body(*refs):
        ins, land_refs = refs[:n], refs[n:2 * n]
        for cp in _scatter_copies(ins, land_refs, refs[2 * n], refs[2 * n + 1]):
            cp.wait_send()
            cp.wait_recv()

    res = pl.pallas_call(
        body, name=name + "_scatter_wait",
        out_shape=tuple([pltpu.HBM(s.shape, s.dtype) for s in sums] + [pltpu.HBM(ld.shape, ld.dtype) for ld in lands]),
        in_specs=[hbm] * (2 * n) + [sem, sem, pl.BlockSpec(memory_space=pl.ANY)], out_specs=tuple([hbm] * (2 * n)),
        input_output_aliases={i: i for i in range(2 * n)},
        compiler_params=pltpu.CompilerParams(has_side_effects=pltpu.SideEffectType.DATAFLOW_SIDE_EFFECTING),
    )(*sums, *lands, send_sems, recv_sems, after)
    return list(res[:n]), list(res[n:])


def _total_sums(name, sums, recv3, kc_arr):
    n = len(sums)
    dims = [s.shape[1:] for s in sums]

    def body(kc_ref, *refs):
        for a in range(n):
            s_ref, r_ref = refs[a], refs[n + a]
            t = s_ref[0].astype(F32) + r_ref[0].astype(F32)
            t = t + r_ref[1].astype(F32)
            refs[2 * n + a][...] = t + r_ref[2].astype(F32)

    return pl.pallas_call(
        body, name=name, out_shape=[_out((2 * rh, cc), F32) for rh, cc in dims],
        grid_spec=pltpu.PrefetchScalarGridSpec(
            num_scalar_prefetch=1, grid=(1,),
            in_specs=[pl.BlockSpec((1, rh, cc), lambda i, kc_ref: (kc_ref[0], 0, 0)) for rh, cc in dims] +
                     [pl.BlockSpec((3, rh, cc), lambda i, kc_ref: (0, 0, 0)) for rh, cc in dims],
            out_specs=[pl.BlockSpec((rh, cc), lambda i, kc_ref: (kc_ref[1], 0)) for rh, cc in dims]),
        compiler_params=_cp(("arbitrary",), 48),
    )(kc_arr, *_in_hbm(*sums, *recv3))


def _assemble_comm(totals):
    n = len(totals)

    def copy(outs, ss, rs, base, a):
        x, y, c = _mesh_pos()
        rh = outs[a].shape[0] // 2
        here = outs[a].at[pl.ds(c * rh, rh), :]
        return _remote(here, here, ss, rs, base + a, (x, y, 1 - c))

    def start(ins, outs, ss, rs, base):
        for a in range(n):
            copy(outs, ss, rs, base, a).start()

    def finish(ins, outs, ss, rs, base):
        for a in range(n):
            copy(outs, ss, rs, base, a).wait()

    shapes = [_out(t.shape, t.dtype) for t in totals]
    return _Comm(totals, shapes, {a: a for a in range(n)}, [(n, start, finish)])


def _small_layout(shapes):
    n = len(shapes)
    narrow_w = 64
    wide = [a for a in range(n) if shapes[a][1] > narrow_w]
    narrow = sorted((a for a in range(n) if shapes[a][1] <= narrow_w), key=lambda a: -shapes[a][0])
    offs, cols, groups, widths, rows = {}, {}, [], [], []
    if wide:
        r = 0
        for a in wide:
            offs[a], cols[a] = r, 0
            r += shapes[a][0]
        groups.append(wide)
        widths.append(max(shapes[a][1] for a in wide))
        rows.append(-(-r // 8) * 8)
    if narrow:
        heights = [0, 0]
        for a in narrow:
            side = 0 if heights[0] <= heights[1] else 1
            offs[a], cols[a] = heights[side], side * narrow_w
            heights[side] += shapes[a][0]
        groups.append(narrow)
        widths.append(2 * narrow_w)
        rows.append(-(-max(heights) // 8) * 8)

    def window(ref, a):
        return ref.at[offs[a]:offs[a] + shapes[a][0], cols[a]:cols[a] + shapes[a][1]]

    return groups, widths, rows, window


def _small_pack(arrays, me_arr):
    shapes = [a.shape for a in arrays]
    groups, widths, rows, window = _small_layout(shapes)
    n, n_g = len(arrays), len(groups)

    def body(me_ref, *refs):
        ins, outs = refs[:n], refs[n:]
        for gi, g in enumerate(groups):
            outs[gi][...] = jnp.zeros_like(outs[gi])
            for a in g:
                window(outs[gi].at[0], a)[...] = ins[a][...]

    return pl.pallas_call(
        body, name="small_pack", out_shape=[_out((8, r, w), F32) for r, w in zip(rows, widths)],
        grid_spec=pltpu.PrefetchScalarGridSpec(
            num_scalar_prefetch=1, grid=(1,), in_specs=[pl.BlockSpec(s, lambda i, me: (0, 0)) for s in shapes],
            out_specs=[pl.BlockSpec((1, r, w), lambda i, me: (me[0], 0, 0)) for r, w in zip(rows, widths)]),
        compiler_params=_cp(("arbitrary",), 32),
    )(me_arr, *_in_hbm(*arrays))


def _spread_comm(slots):
    n = len(slots)
    flips = [(dx, dy, dc) for dx in range(2) for dy in range(2) for dc in range(2)][1:]

    def copies(outs, ss, rs, base):
        x, y, c = _mesh_pos()
        mine = 4 * x + 2 * y + c
        return [_remote(outs[a].at[mine], outs[a].at[mine], ss, rs, base + 7 * a + f,
                        (x ^ dx, y ^ dy, c ^ dc)) for a in range(n) for f, (dx, dy, dc) in enumerate(flips)]

    def start(ins, outs, ss, rs, base):
        for cp in copies(outs, ss, rs, base):
            cp.start()

    def finish(ins, outs, ss, rs, base):
        for cp in copies(outs, ss, rs, base):
            cp.wait()

    return _Comm(slots, [_out(s.shape, s.dtype) for s in slots], {a: a for a in range(n)}, [(7 * n, start, finish)])


def _small_total(slots, shapes):
    groups, widths, rows, window = _small_layout(shapes)
    n, n_g = len(shapes), len(groups)

    def body(*refs):
        ins, outs, acc = refs[:n_g], refs[n_g:n_g + n], refs[n_g + n:]
        for gi, g in enumerate(groups):
            t = ins[gi][0] + ins[gi][1]
            for d in range(2, 8):
                t = t + ins[gi][d]
            acc[gi][...] = t
            for a in g:
                outs[a][...] = window(acc[gi], a)[...]

    return pl.pallas_call(
        body, name="small_total", grid=(1,), out_shape=[_out(s, F32) for s in shapes],
        in_specs=[_full(s.shape) for s in slots], out_specs=[_full(s) for s in shapes],
        scratch_shapes=[pltpu.VMEM((r, w), F32) for r, w in zip(rows, widths)],
        compiler_params=_cp(("arbitrary",), 48),
    )(*_in_hbm(*slots))


def _small_allreduce(arrays, comm):
    n = len(arrays)
    shapes = [a.shape for a in arrays]
    groups, widths, rows, window = _small_layout(shapes)
    n_g = len(groups)

    def body(*refs):
        ins, outs = refs[:n], refs[n:2 * n]
        pack, sib, csum, every = (refs[2 * n + i * n_g:2 * n + (i + 1) * n_g] for i in range(4))
        send_sems, recv_sems = refs[2 * n + 4 * n_g:]
        x, y, c = _mesh_pos()
        k = 2 * x + y
        for gi, g in enumerate(groups):
            pack[gi][...] = jnp.zeros_like(pack[gi])
            for a in g:
                window(pack[gi], a)[...] = ins[a][...]
        cps = [_remote(pack[gi], sib[gi], send_sems, recv_sems, gi, (x, y, 1 - c)) for gi in range(n_g)]
        for cp in cps:
            cp.start()
        for cp in cps:
            cp.wait()
        for gi in range(n_g):
            csum[gi][...] = pack[gi][...] + sib[gi][...]
            every[gi][k] = csum[gi][...]
        cps = [_remote(csum[gi], every[gi].at[k], send_sems, recv_sems, n_g + 3 * gi + j, (*chip, c))
               for gi in range(n_g) for j, chip in enumerate(_other_chips(x, y))]
        for cp in cps:
            cp.start()
        for cp in cps:
            cp.wait()
        for gi, g in enumerate(groups):
            pack[gi][...] = ((every[gi][0] + every[gi][1]) + every[gi][2]) + every[gi][3]
            for a in g:
                outs[a][...] = window(pack[gi], a)[...]

    bufs = [pltpu.VMEM((r, w), F32) for r, w in zip(rows, widths)]
    return _call(
        body, comm, (0,), arrays, name="small_allreduce", grid=(1,), out_shape=[_out(s, F32) for s in shapes],
        in_specs=[_full(s) for s in shapes], out_specs=[_full(s) for s in shapes],
        scratch_shapes=bufs * 3 + [pltpu.VMEM((N_CHIP, r, w), F32) for r, w in zip(rows, widths)] +
                       [pltpu.SemaphoreType.DMA((4 * n_g,)), pltpu.SemaphoreType.DMA((4 * n_g,))],
        compiler_params=_cp(("arbitrary",), 40))


def _adamw_small(ws, gs, ms, vs, comm):
    n = len(ws)

    def body(*refs):
        w, g, m, v, d, mo, vo = (refs[i * n:(i + 1) * n] for i in range(7))
        for a in range(n):
            d[a][...], mo[a][...], vo[a][...] = _adamw_math(w[a][...], g[a][...], m[a][...], v[a][...])

    specs = [_full(w.shape) for w in ws]
    res, got = _call(
        body, comm, (0,), (*ws, *gs, *ms, *vs), name="adamw_small", grid=(1,),
        out_shape=[_out(w.shape, F32) for w in ws] * 3,
        in_specs=specs * 4, out_specs=specs * 3, compiler_params=_cp(("arbitrary",), 40))
    return (res[:n], res[n:2 * n], res[2 * n:]), got


def _adamw_math(w, g, m, v):
    m = ADAM_B1 * m + (1.0 - ADAM_B1) * g
    v = ADAM_B2 * v + (1.0 - ADAM_B2) * (g * g)
    m_hat = m / (1.0 - ADAM_B1 ** ADAM_STEP)
    v_hat = v / (1.0 - ADAM_B2 ** ADAM_STEP)
    delta = -ADAM_LR * (m_hat / (jnp.sqrt(v_hat) + ADAM_EPS) + ADAM_WD * w)
    return delta, m, v


def _adamw_group(name, ws, gs, ms, vs):
    n = len(ws)
    steps = 8
    specs = [_rows(w.shape[0] // steps, w.shape[1]) for w in ws]
    assert all(w.shape[0] % (8 * steps) == 0 for w in ws)

    def body(*refs):
        w, g, m, v, d, mo, vo = (refs[i * n:(i + 1) * n] for i in range(7))
        for a in range(n):
            d[a][...], mo[a][...], vo[a][...] = _adamw_math(w[a][...], g[a][...], m[a][...], v[a][...])

    res = pl.pallas_call(
        body, name=name, grid=(steps,), in_specs=specs * 4, out_specs=specs * 3,
        out_shape=[_out(w.shape, F32) for w in ws] * 3, compiler_params=_cp(("arbitrary",), 40),
    )(*_in_hbm(*ws, *gs, *ms, *vs))
    return res[:n], res[n:2 * n], res[2 * n:]


def _as_matrix(name, a):
    if name == "na_rpb":
        return a[0].transpose(1, 0, 2).reshape(N_HEADS * (2 * KH - 1), 2 * KW - 1)
    if name in ("s5_b_re", "s5_b_im"):
        return a.transpose(0, 1, 2, 4, 3).reshape(2 * S5_G * S5_H, S5_P)
    if name in ("s5_c_re", "s5_c_im"):
        return a.reshape(2 * S5_G * S5_H, S5_P)
    if name in ("s5_lam_re", "s5_lam_im"):
        return a.reshape(2 * S5_G, S5_P)
    if name == "s5_log_dt":
        return a.reshape(2, S5_G)
    return a


def _from_matrix(name, m):
    if name == "na_rpb":
        return m.reshape(2 * KH - 1, N_HEADS, 2 * KW - 1).transpose(1, 0, 2)[None]
    if name in ("s5_b_re", "s5_b_im"):
        return m.reshape(1, 2, S5_G, S5_H, S5_P).transpose(0, 1, 2, 4, 3)
    if name in ("s5_c_re", "s5_c_im"):
        return m.reshape(1, 2, S5_G, S5_H, S5_P)
    if name in ("s5_lam_re", "s5_lam_im"):
        return m.reshape(1, 2, S5_G, S5_P)
    if name == "s5_log_dt":
        return m.reshape(1, 2, S5_G)
    return m


WEIGHTS = ["meta_tokens", "ffn1_pre_g", "ffn1_post_g", "ffn1_w_gate", "ffn1_w_up", "ffn1_w_down", "mix_pre_g", "w_in",
           "na_rpb", "s5_lam_re", "s5_lam_im", "s5_log_dt", "s5_b_re", "s5_b_im", "s5_c_re", "s5_c_im", "s5_d",
           "s5_w_glu", "s5_b_glu", "na_out_g", "s5_out_g", "w_out", "mix_post_g", "ffn2_pre_g", "ffn2_post_g",
           "ffn2_w_gate", "ffn2_w_up", "ffn2_w_down", "final_g"]
BIG = ["ffn1_w_gate", "ffn1_w_up", "ffn1_w_down", "w_in", "s5_w_glu", "w_out", "ffn2_w_gate", "ffn2_w_up",
       "ffn2_w_down"]
TRANSPOSED = ["ffn1_w_gate", "ffn1_w_up", "ffn2_w_gate", "ffn2_w_up"]
GAINS = ["ffn1_pre_g", "ffn1_post_g", "mix_pre_g", "s5_d", "s5_b_glu", "na_out_g", "s5_out_g", "mix_post_g",
         "ffn2_pre_g", "ffn2_post_g", "final_g"]
SMALL = [n for n in WEIGHTS if n not in BIG]


def kernel(*args):
    names = ["x"] + WEIGHTS + ["loss_target"] + ["m_" + n for n in WEIGHTS] + ["v_" + n for n in WEIGHTS]
    assert len(args) == len(names)
    given = dict(zip(names, args))
    x_pos, y_pos, c_pos = _mesh_pos()
    k_pos = 2 * x_pos + y_pos
    c_arr = jnp.reshape(c_pos, (1,)).astype(jnp.int32)
    kc_arr = jnp.stack([k_pos, c_pos]).astype(jnp.int32)

    def piece(name, a):
        return a[0].T if name in TRANSPOSED else a[0]

    def unpiece(name, a):
        return a.T[None] if name in TRANSPOSED else a[None]

    placed = BIG + ["meta_tokens"]
    bufs = dict(zip(placed, _own_half_buffers([piece(n, given[n]) for n in BIG] + [given["meta_tokens"]],
                                              [BF16] * len(BIG) + [F32], kc_arr)))

    gains = {n: given[n] for n in GAINS}
    s5 = {n: _as_matrix("s5_" + n, given["s5_" + n])
          for n in ["lam_re", "lam_im", "log_dt", "b_re", "b_im", "c_re", "c_im"]}
    me_arr = jnp.reshape(4 * x_pos + 2 * y_pos + c_pos, (1,)).astype(jnp.int32)
    loss, dh0, pieces, small, late, (ffn1, flight1), (mid, flight2) = _step(
        given["x"][0], given["loss_target"][0], bufs, gains, s5, given["na_rpb"][0], c_arr, kc_arr, me_arr)
    loss = lax.psum(loss, ("x", "y", "c"))
    n_tok = given["x"].shape[1]
    grad_x = dh0[N_META:N_META + n_tok][None]

    late["meta_tokens"] = dh0[:N_META]
    out_g, out_d, out_m, out_v = {}, {}, {}, {}

    def update_big(group, names):
        g2 = [pieces[n] for n in names]
        d2, m2, v2 = _adamw_group("adamw_" + group, [piece(n, given[n]) for n in names], g2,
                                  [piece(n, given["m_" + n]) for n in names], [piece(n, given["v_" + n]) for n in names])
        for n, g, dd, mm, vv in zip(names, g2, d2, m2, v2):
            out_g[n], out_d[n], out_m[n], out_v[n] = (unpiece(n, t) for t in (g, dd, mm, vv))
        return v2

    done2 = update_big("ffn2", list(pieces))
    sums1, recv1 = _scatter_wait("ffn1", *flight1, done2[-1])
    totals1 = _total_sums("total_sums_ffn1", sums1, recv1, kc_arr)
    pieces.update(zip(ffn1, _run_comm("ffn1_pair_assemble", _assemble_comm(totals1))))
    done1 = update_big("ffn1", ffn1)
    late_arrays = list(late.values())
    late_arrays[0], _ = lax.optimization_barrier((late_arrays[0], (done1[-1], small["final_g"])))
    red, _ = _small_allreduce(late_arrays, None)
    small.update(zip(late, red))
    mc = D // N_CHIP
    small["meta_tokens"] = lax.dynamic_slice_in_dim(small["meta_tokens"], k_pos * mc, mc, 1)
    sums_rest, recv_rest = _scatter_wait("rest", *flight2, red[0])
    totals = _total_sums("total_sums_rest", sums_rest, recv_rest, kc_arr)
    gs = [small[n] for n in SMALL]
    (d2, m2, v2), done = _adamw_small([_as_matrix(n, given[n]) for n in SMALL], gs,
                                      [_as_matrix(n, given["m_" + n]) for n in SMALL],
                                      [_as_matrix(n, given["v_" + n]) for n in SMALL], _assemble_comm(totals))
    pieces.update(zip(mid, done))

    for n, g, dd, mm, vv in zip(SMALL, gs, d2, m2, v2):
        out_g[n], out_d[n], out_m[n], out_v[n] = (_from_matrix(n, t) for t in (g, dd, mm, vv))
    update_big("rest", mid)
    return (loss, grad_x, *[out_g[n] for n in WEIGHTS], *[out_d[n] for n in WEIGHTS],
            *[out_m[n] for n in WEIGHTS], *[out_v[n] for n in WEIGHTS])
```

```python
import math

import numpy as np
import jax
import jax.numpy as jnp
from jax import lax
from jax.experimental import pallas as pl
from jax.experimental.pallas import tpu as pltpu

F32 = jnp.float32
BF16 = jnp.bfloat16

D = 1024
N_META = 16
GRID_W = 64
NA_W = 512
S5_W = 512
HEAD_DIM = 64
N_HEADS = 8
KH = 8
KW = 16
S5_G = 32
S5_P = 64
S5_H = 16
N_BUNDLE = 4
FF = 2816
N_CHIP = 4
FC = FF // N_CHIP
EPS = 1e-6
NEG_INF = -1e30
Q_ROWS = 4
K_ROWS = 12
QB = Q_ROWS * GRID_W
KB = K_ROWS * GRID_W
SCAN_CHUNK = 256

ADAM_LR = 0.001
ADAM_B1 = 0.9
ADAM_B2 = 0.999
ADAM_EPS = 1e-08
ADAM_WD = 0.01
ADAM_STEP = 10

NT = (((1,), (1,)), ((), ()))
TN = (((0,), (0,)), ((), ()))
MESH_ID = pl.DeviceIdType.MESH


def _cp(sem=None, vmem_mb=None):
    kw = {}
    if sem is not None:
        kw["dimension_semantics"] = sem
    if vmem_mb is not None:
        kw["vmem_limit_bytes"] = vmem_mb << 20
    return pltpu.CompilerParams(**kw)


def _full(shape):
    n = len(shape)
    return pl.BlockSpec(shape, lambda *_: (0,) * n)


def _rows(tm, w):
    return pl.BlockSpec((tm, w), lambda i: (i, 0))


ANY = pl.BlockSpec(memory_space=pl.ANY)


def _rms(x, g):
    r = lax.rsqrt(jnp.mean(x * x, axis=-1, keepdims=True) + EPS)
    return x * r * g


def _rms_bwd(x, g, dy):
    r = lax.rsqrt(jnp.mean(x * x, axis=-1, keepdims=True) + EPS)
    xh = x * r
    dg = jnp.sum(dy * xh, axis=0, keepdims=True)
    dyg = dy * g
    dx = r * (dyg - xh * jnp.mean(dyg * xh, axis=-1, keepdims=True))
    return dx, dg


def _out(shape, dtype):
    return pltpu.HBM(tuple(shape), dtype)


def _in_hbm(*args):
    return [pltpu.with_memory_space_constraint(a, pltpu.HBM) if jnp.issubdtype(a.dtype, jnp.floating) and a.ndim > 1
            else a for a in args]


def _dot(a, b):
    return jnp.dot(a, b, preferred_element_type=F32)


def _dg(a, b, dims):
    return lax.dot_general(a, b, dims, preferred_element_type=F32)


def _ffn_fwd(name, h, g_pre, g_post, wg, wu, wd, tm, comm=None, bounds=()):
    tp = h.shape[0]
    nt = tp // tm

    def body(h_ref, gp_ref, gq_ref, wg_ref, wu_ref, wd_ref, hn_ref, gate_ref, up_ref, f_ref, xn_s, acc_s):
        c = pl.program_id(1)

        @pl.when(c == 0)
        def _():
            xn_s[...] = _rms(h_ref[...], gp_ref[...]).astype(BF16)
            acc_s[...] = jnp.zeros_like(acc_s)

        xn = xn_s[...]
        gate = _dg(xn, wg_ref[0], NT)
        up = _dg(xn, wu_ref[0], NT)
        gate_ref[0] = gate
        up_ref[0] = up
        act = (gate * jax.nn.sigmoid(gate) * up).astype(BF16)
        acc_s[...] += _dot(act, wd_ref[0])

        @pl.when(c == N_CHIP - 1)
        def _():
            f = acc_s[...]
            f_ref[...] = f
            hn_ref[...] = h_ref[...] + 0.5 * _rms(f, gq_ref[...])

    return _call(
        body, comm, bounds, (h, g_pre, g_post, wg, wu, wd), name=name, grid=(nt, N_CHIP),
        in_specs=[pl.BlockSpec((tm, D), lambda i, c: (i, 0)), _full((1, D)), _full((1, D))] +
                 [pl.BlockSpec((1, FC, D), lambda i, c: (c, 0, 0))] * 3,
        out_specs=[pl.BlockSpec((tm, D), lambda i, c: (i, 0)),
                   pl.BlockSpec((1, tm, FC), lambda i, c: (c, i, 0)),
                   pl.BlockSpec((1, tm, FC), lambda i, c: (c, i, 0)),
                   pl.BlockSpec((tm, D), lambda i, c: (i, 0))],
        out_shape=[_out((tp, D), F32), _out((N_CHIP, tp, FC), F32),
                   _out((N_CHIP, tp, FC), F32), _out((tp, D), F32)],
        scratch_shapes=[pltpu.VMEM((tm, D), BF16), pltpu.VMEM((tm, D), F32)],
        compiler_params=_cp(("arbitrary", "arbitrary"), 48))


def _ffn_bwd(name, h, g_pre, df, gate, up, wg, wu, wd, tm, comm=None, bounds=()):
    tp = h.shape[0]
    nt = tp // tm
    rh = FC // 2

    def body(h_ref, gp_ref, df_ref, gate_ref, up_ref, wg_ref, wu_ref, wd_ref,
             dwg_ref, dwu_ref, dwd_ref, dxn_ref, rg_ref, ru_ref, rd_ref, ag, au, ad, send_sems, recv_sems):
        c = pl.program_id(0)
        i = pl.program_id(1)

        def to_sibling(a, piece):
            x, y, core = _mesh_pos()
            dw_ref, r_ref = ((dwg_ref, rg_ref), (dwu_ref, ru_ref), (dwd_ref, rd_ref))[a]
            return _remote(dw_ref.at[piece, pl.ds((1 - core) * rh, rh), :], r_ref.at[piece], send_sems, recv_sems,
                           3 * piece + a, (x, y, 1 - core))

        @pl.when(i == 0)
        def _():
            ag[...] = jnp.zeros_like(ag)
            au[...] = jnp.zeros_like(au)
            ad[...] = jnp.zeros_like(ad)

        xn = _rms(h_ref[...], gp_ref[...]).astype(BF16)
        dfb = df_ref[...].astype(BF16)
        gt = gate_ref[0]
        u = up_ref[0]
        sg = jax.nn.sigmoid(gt)
        si = gt * sg
        act = (si * u).astype(BF16)
        dact = _dg(dfb, wd_ref[0], NT)
        dgate = (dact * u * (sg * (1.0 + gt * (1.0 - sg)))).astype(BF16)
        dup = (dact * si).astype(BF16)
        dxn_ref[0] = _dot(dgate, wg_ref[0]) + _dot(dup, wu_ref[0])
        ad[...] += _dg(act, dfb, TN)
        ag[...] += _dg(dgate, xn, TN)
        au[...] += _dg(dup, xn, TN)

        @pl.when(i == nt - 1)
        def _():
            pltpu.sync_copy(ag, dwg_ref.at[c])
            pltpu.sync_copy(au, dwu_ref.at[c])
            pltpu.sync_copy(ad, dwd_ref.at[c])
            for a in range(3):
                to_sibling(a, c).start()

        @pl.when((c == N_CHIP - 1) & (i == nt - 1))
        def _():
            for piece in range(N_CHIP):
                for a in range(3):
                    to_sibling(a, piece).wait()

    return _call(
        body, comm, bounds, (h, g_pre, df, gate, up, wg, wu, wd), name=name, grid=(N_CHIP, nt),
        in_specs=[pl.BlockSpec((tm, D), lambda c, i: (i, 0)), _full((1, D)),
                  pl.BlockSpec((tm, D), lambda c, i: (i, 0)),
                  pl.BlockSpec((1, tm, FC), lambda c, i: (c, i, 0)),
                  pl.BlockSpec((1, tm, FC), lambda c, i: (c, i, 0))] +
                 [pl.BlockSpec((1, FC, D), lambda c, i: (c, 0, 0))] * 3,
        out_specs=[ANY, ANY, ANY, pl.BlockSpec((1, tm, D), lambda c, i: (c, i, 0)), ANY, ANY, ANY],
        out_shape=[_out((N_CHIP, FC, D), F32)] * 3 + [_out((N_CHIP, tp, D), F32)] +
                  [_out((N_CHIP, rh, D), F32)] * 3,
        scratch_shapes=[pltpu.VMEM((FC, D), F32)] * 3 +
                       [pltpu.SemaphoreType.DMA((3 * N_CHIP,)), pltpu.SemaphoreType.DMA((3 * N_CHIP,))],
        compiler_params=_cp(("arbitrary", "arbitrary"), 58))


def _ffn_pre_bwd(name, dh, dxn_part, h, g_pre, tm, comm=None, bounds=()):
    tp = h.shape[0]
    nt = tp // tm

    def body(dh_ref, dxn_ref, h_ref, gp_ref, out_ref, dg_ref):
        i = pl.program_id(0)
        dxn = (dxn_ref[0] + dxn_ref[1]) + (dxn_ref[2] + dxn_ref[3])
        dx, dg = _rms_bwd(h_ref[...], gp_ref[...], dxn)
        out_ref[...] = dh_ref[...] + dx

        @pl.when(i == 0)
        def _():
            dg_ref[...] = jnp.zeros_like(dg_ref)

        dg_ref[...] += dg

    return _call(
        body, comm, bounds, (dh, dxn_part, h, g_pre), name=name, grid=(nt,),
        in_specs=[_rows(tm, D), pl.BlockSpec((N_CHIP, tm, D), lambda i: (0, i, 0)), _rows(tm, D), _full((1, D))],
        out_specs=[_rows(tm, D), _full((1, D))],
        out_shape=[_out((tp, D), F32), _out((1, D), F32)],
        compiler_params=_cp(("arbitrary",), 48))


def _mix_in(h, g, w_in, tm):
    tp = h.shape[0]

    def body(h_ref, g_ref, w_ref, q_ref, k_ref, v_ref, u_ref):
        a = _rms(h_ref[...], g_ref[...]).astype(BF16)
        q_ref[...] = _dot(a, w_ref[0]).astype(BF16)
        k_ref[...] = _dot(a, w_ref[1]).astype(BF16)
        v_ref[...] = _dot(a, w_ref[2]).astype(BF16)
        u_ref[...] = _dot(a, w_ref[3])

    return pl.pallas_call(
        body, name="mix_in", grid=(tp // tm,),
        in_specs=[_rows(tm, D), _full((1, D)), _full((N_CHIP, D, NA_W))],
        out_specs=[_rows(tm, NA_W)] * 4,
        out_shape=[_out((tp, NA_W), BF16)] * 3 + [_out((tp, S5_W), F32)],
        compiler_params=_cp(("arbitrary",), 40),
    )(*_in_hbm(h, g, w_in))


def _gelu(x):
    return jax.nn.gelu(x, approximate=True)


def _gelu_grad(x):
    k = math.sqrt(2.0 / math.pi)
    t = jnp.tanh(k * (x + 0.044715 * x * x * x))
    return 0.5 * (1.0 + t) + 0.5 * x * (1.0 - t * t) * k * (1.0 + 3.0 * 0.044715 * x * x)


def _mix_out(o_na, y_pre, h, w_glu, b_glu, g_na, g_s5, w_out, g_post, tm, comm=None, bounds=()):
    tp = h.shape[0]

    def body(ona_ref, yp_ref, h_ref, wglu_ref, bglu_ref, gna_ref, gs5_ref, wout_ref, gpost_ref, hn_ref, mix_ref):
        y = _gelu(yp_ref[...])
        z = _dot(y.astype(BF16), wglu_ref[...]) + bglu_ref[...]
        o_s5 = y * jax.nn.sigmoid(z)
        n1 = _rms(ona_ref[...], gna_ref[...]).astype(BF16)
        n2 = _rms(o_s5, gs5_ref[...]).astype(BF16)
        mix = _dot(n1, wout_ref[0:NA_W, :]) + _dot(n2, wout_ref[NA_W:, :])
        mix_ref[...] = mix
        hn_ref[...] = h_ref[...] + _rms(mix, gpost_ref[...])

    return _call(
        body, comm, bounds, (o_na, y_pre, h, w_glu, b_glu, g_na, g_s5, w_out, g_post), name="mix_out",
        grid=(tp // tm,),
        in_specs=[_rows(tm, NA_W), _rows(tm, S5_W), _rows(tm, D), _full((S5_W, S5_W)), _full((1, S5_W)),
                  _full((1, NA_W)), _full((1, S5_W)), _full((D, D)), _full((1, D))],
        out_specs=[_rows(tm, D), _rows(tm, D)],
        out_shape=[_out((tp, D), F32)] * 2,
        compiler_params=_cp(("arbitrary",), 40))


def _mix_out_bwd(dh, mix, o_na, y_pre, w_glu, b_glu, g_na, g_s5, w_out, g_post, tm):
    tp = dh.shape[0]
    nt = tp // tm

    def body(dh_ref, mix_ref, ona_ref, yp_ref, wglu_ref, bglu_ref, gna_ref, gs5_ref, wout_ref, gpost_ref,
             dona_ref, dyp_ref, dwout_ref, dwglu_ref, dgpost_ref, dgna_ref, dgs5_ref, dbglu_ref, a_out, a_glu):
        i = pl.program_id(0)

        @pl.when(i == 0)
        def _():
            a_out[...] = jnp.zeros_like(a_out)
            a_glu[...] = jnp.zeros_like(a_glu)
            dgpost_ref[...] = jnp.zeros_like(dgpost_ref)
            dgna_ref[...] = jnp.zeros_like(dgna_ref)
            dgs5_ref[...] = jnp.zeros_like(dgs5_ref)
            dbglu_ref[...] = jnp.zeros_like(dbglu_ref)

        dmix, dgpost = _rms_bwd(mix_ref[...], gpost_ref[...], dh_ref[...])
        dgpost_ref[...] += dgpost
        yp = yp_ref[...]
        y = _gelu(yp)
        yb = y.astype(BF16)
        z = _dot(yb, wglu_ref[...]) + bglu_ref[...]
        sg = jax.nn.sigmoid(z)
        o_s5 = y * sg
        o_na = ona_ref[...]
        n1 = _rms(o_na, gna_ref[...]).astype(BF16)
        n2 = _rms(o_s5, gs5_ref[...]).astype(BF16)
        dmb = dmix.astype(BF16)
        a_out[0:NA_W, :] += _dg(n1, dmb, TN)
        a_out[NA_W:, :] += _dg(n2, dmb, TN)
        dn1 = _dg(dmb, wout_ref[0:NA_W, :], NT)
        dn2 = _dg(dmb, wout_ref[NA_W:, :], NT)
        dona, dgna = _rms_bwd(o_na, gna_ref[...], dn1)
        dona_ref[...] = dona
        dgna_ref[...] += dgna
        dos5, dgs5 = _rms_bwd(o_s5, gs5_ref[...], dn2)
        dgs5_ref[...] += dgs5
        dz = dos5 * y * (sg * (1.0 - sg))
        dbglu_ref[...] += jnp.sum(dz, axis=0, keepdims=True)
        dzb = dz.astype(BF16)
        a_glu[...] += _dg(yb, dzb, TN)
        dy = dos5 * sg + _dg(dzb, wglu_ref[...], NT)
        dyp_ref[...] = dy * _gelu_grad(yp)

        @pl.when(i == nt - 1)
        def _():
            pltpu.sync_copy(a_out, dwout_ref)
            pltpu.sync_copy(a_glu, dwglu_ref)

    return pl.pallas_call(
        body, name="mix_out_bwd", grid=(nt,),
        in_specs=[_rows(tm, D), _rows(tm, D), _rows(tm, NA_W), _rows(tm, S5_W), _full((S5_W, S5_W)),
                  _full((1, S5_W)), _full((1, NA_W)), _full((1, S5_W)), _full((D, D)), _full((1, D))],
        out_specs=[_rows(tm, NA_W), _rows(tm, S5_W), ANY, ANY, _full((1, D)), _full((1, NA_W)),
                   _full((1, S5_W)), _full((1, S5_W))],
        out_shape=[_out((tp, NA_W), F32), _out((tp, S5_W), F32),
                   _out((D, D), F32), _out((S5_W, S5_W), F32),
                   _out((1, D), F32), _out((1, NA_W), F32),
                   _out((1, S5_W), F32), _out((1, S5_W), F32)],
        scratch_shapes=[pltpu.VMEM((D, D), F32), pltpu.VMEM((S5_W, S5_W), F32)],
        compiler_params=_cp(("arbitrary",), 48),
    )(*_in_hbm(dh, mix, o_na, y_pre, w_glu, b_glu, g_na, g_s5, w_out, g_post))


def _mix_in_bwd(dq, dk, dv, du, h, g, w_in, dh, f1, g_post1, tm, comm=None, bounds=()):
    tp = h.shape[0]
    nt = tp // tm

    def body(dq_ref, dk_ref, dv_ref, du_ref, h_ref, g_ref, w_ref, dh_ref, f_ref, gq_ref,
             dh1_ref, df_ref, dw_ref, dg_ref, dgq_ref, acc):
        i = pl.program_id(0)

        @pl.when(i == 0)
        def _():
            acc[...] = jnp.zeros_like(acc)
            dg_ref[...] = jnp.zeros_like(dg_ref)
            dgq_ref[...] = jnp.zeros_like(dgq_ref)

        x = h_ref[...]
        a = _rms(x, g_ref[...]).astype(BF16)
        da = jnp.zeros((tm, D), F32)
        for j, r in enumerate((dq_ref, dk_ref, dv_ref, du_ref)):
            dp = r[...].astype(BF16)
            da = da + _dg(dp, w_ref[j], NT)
            acc[j] += _dg(a, dp, TN)
        dx, dg = _rms_bwd(x, g_ref[...], da)
        dh1 = dh_ref[...] + dx
        dh1_ref[...] = dh1
        dg_ref[...] += dg
        df, dgq = _rms_bwd(f_ref[...], gq_ref[...], 0.5 * dh1)
        df_ref[...] = df
        dgq_ref[...] += dgq

        @pl.when(i == nt - 1)
        def _():
            pltpu.sync_copy(acc, dw_ref)

    return _call(
        body, comm, bounds, (dq, dk, dv, du, h, g, w_in, dh, f1, g_post1), name="mix_in_bwd", grid=(nt,),
        in_specs=[_rows(tm, NA_W)] * 4 + [_rows(tm, D), _full((1, D)), _full((N_CHIP, D, NA_W)), _rows(tm, D),
                                         _rows(tm, D), _full((1, D))],
        out_specs=[_rows(tm, D), _rows(tm, D), ANY, _full((1, D)), _full((1, D))],
        out_shape=[_out((tp, D), F32), _out((tp, D), F32),
                   _out((N_CHIP, D, NA_W), F32), _out((1, D), F32),
                   _out((1, D), F32)],
        scratch_shapes=[pltpu.VMEM((N_CHIP, D, NA_W), F32)],
        compiler_params=_cp(("arbitrary",), 48))


def _final_loss(h, g_final, target, f2, g_post2, n_tok, tm):
    tp = h.shape[0]

    def body(h_ref, g_ref, t_ref, f_ref, gq_ref, dh_ref, df_ref, loss_ref, dg_ref, dgq_ref):
        i = pl.program_id(0)

        @pl.when(i == 0)
        def _():
            loss_ref[...] = jnp.zeros_like(loss_ref)
            dg_ref[...] = jnp.zeros_like(dg_ref)
            dgq_ref[...] = jnp.zeros_like(dgq_ref)

        x = h_ref[...]
        y = _rms(x, g_ref[...])
        row = i * tm + lax.broadcasted_iota(jnp.int32, (tm, 1), 0)
        valid = (row >= N_META) & (row < N_META + n_tok)
        e = jnp.where(valid, y - t_ref[...], 0.0)
        loss_ref[...] += 0.5 * jnp.sum(jnp.mean(e * e, axis=-1, keepdims=True), axis=0, keepdims=True)
        dx, dg = _rms_bwd(x, g_ref[...], e * (1.0 / D))
        dh_ref[...] = dx
        dg_ref[...] += dg
        df, dgq = _rms_bwd(f_ref[...], gq_ref[...], 0.5 * dx)
        df_ref[...] = df
        dgq_ref[...] += dgq

    return pl.pallas_call(
        body, name="final_loss", grid=(tp // tm,),
        in_specs=[_rows(tm, D), _full((1, D)), _rows(tm, D), _rows(tm, D), _full((1, D))],
        out_specs=[_rows(tm, D), _rows(tm, D), _full((1, 1)), _full((1, D)), _full((1, D))],
        out_shape=[_out((tp, D), F32), _out((tp, D), F32),
                   _out((1, 1), F32), _out((1, D), F32),
                   _out((1, D), F32)],
        compiler_params=_cp(("arbitrary",), 40),
    )(*_in_hbm(h, g_final, target, f2, g_post2))


def _na_patterns(n_rows):
    pats = []
    for kind in range(3):
        pat = [[-1] * K_ROWS for _ in range(Q_ROWS)]
        for i in range(Q_ROWS):
            for jj in range(K_ROWS):
                if kind == 0 and jj < KH:
                    pat[i][jj] = jj - i + KH - 1
                elif kind == 1 and i <= jj < i + KH:
                    pat[i][jj] = jj - i + 3
                elif kind == 2 and K_ROWS - KH <= jj:
                    pat[i][jj] = jj - i - 1
        pats.append(pat)
    return pats


def _diag_onehot():
    q = np.arange(GRID_W)[:, None]
    kc = np.arange(GRID_W)[None, :]
    start = np.clip(q - KW // 2, 0, GRID_W - KW)
    col_in = (kc >= start) & (kc < start + KW)
    e = np.zeros((32, GRID_W, GRID_W), np.float32)
    for d in range(2 * KW - 1):
        e[d] = ((kc - q + KW - 1) == d) & col_in
    return e.reshape(32, GRID_W * GRID_W), col_in


def _rpb_collapse(dtb2, et):
    def body(d_ref, e_ref, o_ref):
        o_ref[...] = jnp.dot(d_ref[...], e_ref[...], preferred_element_type=F32, precision=lax.Precision.HIGHEST)

    out = (dtb2.shape[0], et.shape[1])
    return pl.pallas_call(
        body, name="rpb_collapse", grid=(1,), out_shape=_out(out, F32),
        in_specs=[_full(dtb2.shape), _full(et.shape)], out_specs=_full(out),
    )(*_in_hbm(dtb2, et))


def _bias_tables(rpb, n_rows, comm=None, bounds=()):
    n_dr, n_dc = 2 * KH - 1, 2 * KW - 1
    pats = _na_patterns(n_rows)

    def body(rpb_ref, o_ref):
        h = pl.program_id(0)
        q = lax.broadcasted_iota(jnp.int32, (GRID_W, GRID_W), 0)
        kc = lax.broadcasted_iota(jnp.int32, (GRID_W, GRID_W), 1)
        start = jnp.clip(q - KW // 2, 0, GRID_W - KW)
        col_in = (kc >= start) & (kc < start + KW)
        diff = kc - q + (KW - 1)
        neg = jnp.full((GRID_W, GRID_W), NEG_INF, F32)
        band = []
        for dr in range(n_dr):
            acc = neg
            for d in range(n_dc):
                acc = jnp.where((diff == d) & col_in, rpb_ref[(h * n_dr + dr) * n_dc + d], acc)
            band.append(acc)
        for kind, pat in enumerate(pats):
            for i in range(Q_ROWS):
                for jj in range(K_ROWS):
                    o_ref[kind, 0, i * GRID_W:(i + 1) * GRID_W, jj * GRID_W:(jj + 1) * GRID_W] = (
                        band[pat[i][jj]] if pat[i][jj] >= 0 else neg)

    (bias,), got = _call(
        body, comm, bounds, (rpb.reshape(-1),), name="bias_tables", grid=(N_HEADS,),
        in_specs=[pl.BlockSpec(memory_space=pltpu.SMEM)],
        out_specs=[pl.BlockSpec((3, 1, QB, KB), lambda h: (0, h, 0, 0))],
        out_shape=[_out((3, N_HEADS, QB, KB), F32)],
        compiler_params=_cp(("arbitrary",), 32))
    return bias, got


def _attn_geometry(n_tok):
    n_rows = n_tok // GRID_W
    assert n_rows % Q_ROWS == 0 and n_rows >= K_ROWS
    return n_rows, n_rows // Q_ROWS


def _attn_probs(qh, kh, kmh, bias, scale):
    s = _dg(qh, kh, NT) * scale + bias
    sm = _dg(qh, kmh, NT) * scale
    m = jnp.maximum(jnp.max(s, axis=-1, keepdims=True), jnp.max(sm, axis=-1, keepdims=True))
    p = jnp.exp(s - m)
    pm = jnp.exp(sm - m)
    inv = 1.0 / (jnp.sum(p, axis=-1, keepdims=True) + jnp.sum(pm, axis=-1, keepdims=True))
    return p * inv, pm * inv


def _meta_probs(qmh, kmh, scale):
    s = _dg(qmh, kmh, NT) * scale
    p = jnp.exp(s - jnp.max(s, axis=-1, keepdims=True))
    return p / jnp.sum(p, axis=-1, keepdims=True)


def _step_rows(r, n_rows):
    q0 = pl.multiple_of(N_META + r * QB, 16)
    k0 = pl.multiple_of(N_META + jnp.clip(Q_ROWS * r - (K_ROWS - KH), 0, n_rows - K_ROWS) * GRID_W, 16)
    return q0, k0


def _attn_fwd(q, k, v, bias, n_tok, comm=None, bounds=()):
    tp = q.shape[0]
    n_rows, n_steps = _attn_geometry(n_tok)
    scale = HEAD_DIM ** -0.5

    def body(q_ref, k_ref, v_ref, b_ref, o_ref):
        r = pl.program_id(1)
        km = k_ref[0:N_META, :]
        vm = v_ref[0:N_META, :]

        @pl.when(r == 0)
        def _():
            qm = q_ref[0:N_META, :]
            outs = []
            for hh in range(2):
                sl = slice(hh * HEAD_DIM, (hh + 1) * HEAD_DIM)
                p = _meta_probs(qm[:, sl], km[:, sl], scale)
                outs.append(_dot(p.astype(BF16), vm[:, sl]))
            o_ref[0:N_META, :] = jnp.concatenate(outs, axis=1)
            o_ref[N_META + n_tok:, :] = jnp.zeros((tp - N_META - n_tok, 2 * HEAD_DIM), F32)

        q0, k0 = _step_rows(r, n_rows)
        qb = q_ref[pl.ds(q0, QB), :]
        kb = k_ref[pl.ds(k0, KB), :]
        vb = v_ref[pl.ds(k0, KB), :]
        outs = []
        for hh in range(2):
            sl = slice(hh * HEAD_DIM, (hh + 1) * HEAD_DIM)
            p, pm = _attn_probs(qb[:, sl], kb[:, sl], km[:, sl], b_ref[0, hh], scale)
            outs.append(_dot(p.astype(BF16), vb[:, sl]) + _dot(pm.astype(BF16), vm[:, sl]))
        o_ref[pl.ds(q0, QB), :] = jnp.concatenate(outs, axis=1)

    def bias_map(hp, r):
        return (jnp.where(r == 0, 0, jnp.where(r == n_steps - 1, 2, 1)), hp, 0, 0)

    col = pl.BlockSpec((tp, 2 * HEAD_DIM), lambda hp, r: (0, hp))
    return _call(
        body, comm, bounds, (q, k, v, bias), name="attn_fwd", grid=(N_HEADS // 2, n_steps),
        in_specs=[col, col, col, pl.BlockSpec((1, 2, QB, KB), bias_map)],
        out_specs=[col], out_shape=[_out((tp, NA_W), F32)],
        compiler_params=_cp(("arbitrary", "arbitrary"), 40))


def _attn_bwd(q, k, v, bias, do, n_tok, comm=None, bounds=()):
    tp = q.shape[0]
    n_rows, n_steps = _attn_geometry(n_tok)
    scale = HEAD_DIM ** -0.5
    pats = _na_patterns(n_rows)

    def body(q_ref, k_ref, v_ref, b_ref, do_ref, dq_ref, dk_ref, dv_ref, dtb_ref):
        r = pl.program_id(1)
        km = k_ref[0:N_META, :]
        vm = v_ref[0:N_META, :]

        @pl.when(r == 0)
        def _():
            dk_ref[...] = jnp.zeros_like(dk_ref)
            dv_ref[...] = jnp.zeros_like(dv_ref)
            dtb_ref[...] = jnp.zeros_like(dtb_ref)
            dq_ref[N_META + n_tok:, :] = jnp.zeros((tp - N_META - n_tok, 2 * HEAD_DIM), F32)
            qm = q_ref[0:N_META, :]
            dom = do_ref[0:N_META, :].astype(BF16)
            dqs, dks, dvs = [], [], []
            for hh in range(2):
                sl = slice(hh * HEAD_DIM, (hh + 1) * HEAD_DIM)
                p = _meta_probs(qm[:, sl], km[:, sl], scale)
                dp = _dg(dom[:, sl], vm[:, sl], NT)
                ds = (p * (dp - jnp.sum(dp * p, axis=-1, keepdims=True))).astype(BF16)
                dvs.append(_dg(p.astype(BF16), dom[:, sl], TN))
                dqs.append(_dot(ds, km[:, sl]) * scale)
                dks.append(_dg(ds, qm[:, sl], TN) * scale)
            dq_ref[0:N_META, :] = jnp.concatenate(dqs, axis=1)
            dk_ref[0:N_META, :] += jnp.concatenate(dks, axis=1)
            dv_ref[0:N_META, :] += jnp.concatenate(dvs, axis=1)

        q0, k0 = _step_rows(r, n_rows)
        qb = q_ref[pl.ds(q0, QB), :]
        kb = k_ref[pl.ds(k0, KB), :]
        vb = v_ref[pl.ds(k0, KB), :]
        dob = do_ref[pl.ds(q0, QB), :].astype(BF16)
        dqs, dks, dvs, dkms, dvms, dss = [], [], [], [], [], []
        for hh in range(2):
            sl = slice(hh * HEAD_DIM, (hh + 1) * HEAD_DIM)
            qh, kh, vh, kmh, vmh, doh = qb[:, sl], kb[:, sl], vb[:, sl], km[:, sl], vm[:, sl], dob[:, sl]
            p, pm = _attn_probs(qh, kh, kmh, b_ref[0, hh], scale)
            dp = _dg(doh, vh, NT)
            dpm = _dg(doh, vmh, NT)
            delta = jnp.sum(dp * p, axis=-1, keepdims=True) + jnp.sum(dpm * pm, axis=-1, keepdims=True)
            ds = p * (dp - delta)
            dsb = ds.astype(BF16)
            dsmb = (pm * (dpm - delta)).astype(BF16)
            dss.append(ds)
            dvs.append(_dg(p.astype(BF16), doh, TN))
            dvms.append(_dg(pm.astype(BF16), doh, TN))
            dqs.append((_dot(dsb, kh) + _dot(dsmb, kmh)) * scale)
            dks.append(_dg(dsb, qh, TN) * scale)
            dkms.append(_dg(dsmb, qh, TN) * scale)
        dq_ref[pl.ds(q0, QB), :] = jnp.concatenate(dqs, axis=1)
        dk_ref[pl.ds(k0, KB), :] += jnp.concatenate(dks, axis=1)
        dv_ref[pl.ds(k0, KB), :] += jnp.concatenate(dvs, axis=1)
        dk_ref[0:N_META, :] += jnp.concatenate(dkms, axis=1)
        dv_ref[0:N_META, :] += jnp.concatenate(dvms, axis=1)

        def add_bias_grad(pat):
            for hh in range(2):
                for i in range(Q_ROWS):
                    for jj in range(K_ROWS):
                        if pat[i][jj] >= 0:
                            dtb_ref[hh, pat[i][jj]] += dss[hh][i * GRID_W:(i + 1) * GRID_W,
                                                               jj * GRID_W:(jj + 1) * GRID_W]

        @pl.when(r == 0)
        def _():
            add_bias_grad(pats[0])

        @pl.when((r > 0) & (r < n_steps - 1))
        def _():
            add_bias_grad(pats[1])

        @pl.when(r == n_steps - 1)
        def _():
            add_bias_grad(pats[2])

    def bias_map(hp, r):
        return (jnp.where(r == 0, 0, jnp.where(r == n_steps - 1, 2, 1)), hp, 0, 0)

    col = pl.BlockSpec((tp, 2 * HEAD_DIM), lambda hp, r: (0, hp))
    n_dr = 2 * KH - 1
    return _call(
        body, comm, bounds, (q, k, v, bias, do), name="attn_bwd", grid=(N_HEADS // 2, n_steps),
        in_specs=[col, col, col, pl.BlockSpec((1, 2, QB, KB), bias_map), col],
        out_specs=[col, col, col, pl.BlockSpec((2, n_dr, GRID_W, GRID_W), lambda hp, r: (hp, 0, 0, 0))],
        out_shape=[_out((tp, NA_W), F32)] * 3 +
                  [_out((N_HEADS, n_dr, GRID_W, GRID_W), F32)],
        compiler_params=_cp(("arbitrary", "arbitrary"), 48))


def _repeat_onehot():
    return np.repeat(np.eye(2 * S5_G, dtype=np.float32), S5_H, axis=0)


def _s5_disc_math(lam_re, lam_im, log_dt, b_re, b_im, rep):
    dt = jnp.exp(log_dt)
    ea = jnp.exp(lam_re * dt)
    a_re = ea * jnp.cos(lam_im * dt)
    a_im = ea * jnp.sin(lam_im * dt)
    den = lam_re * lam_re + lam_im * lam_im
    c_re = ((a_re - 1.0) * lam_re + a_im * lam_im) / den
    c_im = (a_im * lam_re - (a_re - 1.0) * lam_im) / den
    ce_re = jnp.dot(rep, c_re, preferred_element_type=F32, precision=lax.Precision.HIGHEST)
    ce_im = jnp.dot(rep, c_im, preferred_element_type=F32, precision=lax.Precision.HIGHEST)
    return a_re, a_im, ce_re * b_re - ce_im * b_im, ce_re * b_im + ce_im * b_re


def _s5_blocks():
    gl = S5_G // N_BUNDLE
    half = gl * S5_P
    out = []
    for d in range(2):
        for g in range(S5_G):
            b, k = divmod(g, gl)
            dg = d * S5_G + g
            out.append((d, b, slice(k * S5_H, (k + 1) * S5_H), slice(k * S5_P, (k + 1) * S5_P),
                        slice(half + k * S5_P, half + (k + 1) * S5_P), slice(dg * S5_H, (dg + 1) * S5_H),
                        slice(dg, dg + 1)))
    return out


def _s5_params(lam_re, lam_im, log_dt, b_re, b_im, c_re, c_im):
    cw, sw = S5_W // N_BUNDLE, 2 * (S5_G // N_BUNDLE) * S5_P

    def body(lr, li, ld, br, bi, cr, ci, rep_ref, a1_ref, a2_ref, bm_ref, cm_ref):
        a_re, a_im, bb_re, bb_im = _s5_disc_math(lr[...], li[...], ld[...], br[...], bi[...], rep_ref[...])
        cc_re = cr[...]
        cc_im = ci[...]
        bm_ref[...] = jnp.zeros_like(bm_ref)
        cm_ref[...] = jnp.zeros_like(cm_ref)
        for d, b, rows, re, im, nat, one in _s5_blocks():
            bm_ref[d, b, rows, re] = bb_re[nat, :].astype(BF16)
            bm_ref[d, b, rows, im] = bb_im[nat, :].astype(BF16)
            cm_ref[d, b, rows, re] = cc_re[nat, :].astype(BF16)
            cm_ref[d, b, rows, im] = (-cc_im[nat, :]).astype(BF16)
            k = rows.start // S5_H
            lanes = slice((k % 2) * S5_P, (k % 2 + 1) * S5_P)
            for part, (v1, v2) in enumerate(((a_re[one, :], a_im[one, :]), (a_re[one, :], -a_im[one, :]))):
                sub = slice(4 * part + k // 2, 4 * part + k // 2 + 1)
                a1_ref[d, b, sub, lanes] = v1
                a2_ref[d, b, sub, lanes] = v2

    args = (lam_re, lam_im, log_dt, b_re, b_im, c_re, c_im, jnp.asarray(_repeat_onehot()))
    outs = [((2, N_BUNDLE, 8, 128), F32)] * 2 + [((2, N_BUNDLE, cw, sw), BF16)] * 2
    return pl.pallas_call(
        body, name="s5_params", grid=(1,), in_specs=[_full(a.shape) for a in args],
        out_specs=[_full(s) for s, _ in outs], out_shape=[_out(s, dt) for s, dt in outs],
    )(*_in_hbm(*args))


def _s5_params_bwd(lam_re, lam_im, log_dt, b_re, b_im, da, dbm, dcm):
    n, nb = 2 * S5_G, 2 * S5_G * S5_H

    def body(lr, li, ld, br, bi, rep_ref, da_ref, dbm_ref, dcm_ref, o_lr, o_li, o_ld, o_br, o_bi, o_cr, o_ci,
             dar_s, dai_s, dbr_s, dbi_s):
        for d, b, rows, re, im, nat, one in _s5_blocks():
            dbr_s[nat, :] = dbm_ref[d, b, rows, re]
            dbi_s[nat, :] = dbm_ref[d, b, rows, im]
            o_cr[nat, :] = dcm_ref[d, b, rows, re]
            o_ci[nat, :] = -dcm_ref[d, b, rows, im]
            dar_s[one, :] = da_ref[d, b, :, re]
            dai_s[one, :] = da_ref[d, b, :, im]
        rep = rep_ref[...]
        _, vjp = jax.vjp(lambda p, q, r, s, t: _s5_disc_math(p, q, r, s, t, rep),
                         lr[...], li[...], ld[...], br[...], bi[...])
        o_lr[...], o_li[...], o_ld[...], o_br[...], o_bi[...] = vjp((dar_s[...], dai_s[...], dbr_s[...], dbi_s[...]))

    args = (lam_re, lam_im, log_dt, b_re, b_im, jnp.asarray(_repeat_onehot()), da, dbm, dcm)
    outs = [(n, S5_P)] * 2 + [(n, 1)] + [(nb, S5_P)] * 4
    return pl.pallas_call(
        body, name="s5_params_bwd", grid=(1,), in_specs=[_full(a.shape) for a in args],
        out_specs=[_full(s) for s in outs], out_shape=[_out(s, F32) for s in outs],
        scratch_shapes=[pltpu.VMEM((n, S5_P), F32)] * 2 + [pltpu.VMEM((nb, S5_P), F32)] * 2,
    )(*_in_hbm(*args))


def _tiles_store(ref, base, val):
    for i in range(val.shape[0] // 8):
        for c in range(8):
            ref[pl.ds(base + (8 * i + c) * 8, 8), :] = val[8 * i:8 * i + 8, 128 * c:128 * (c + 1)]


def _tiles_load(ref, base, n):
    return jnp.concatenate(
        [jnp.concatenate([ref[pl.ds(base + (8 * i + c) * 8, 8), :] for c in range(8)], axis=1) for i in range(n // 8)],
        axis=0)


def _time_rows(base, t):
    return pl.ds(base + (t // 8) * 64 + t % 8, 8, stride=8)


def _scan(chains, n):
    xs = [c["x"] for c in chains]
    for k in range(n):
        for ci, c in enumerate(chains):
            t = n - 1 - k if c["reverse"] else k
            if c["prev"] is not None:
                c["prev"][_time_rows(c["prev_base"], t), :] = xs[ci]
            xs[ci] = c["a1"] * xs[ci] + pltpu.roll(c["a2"] * xs[ci], 4, axis=0) + c["src"][_time_rows(0, t), :]
            if c["dst"] is not None:
                c["dst"][_time_rows(0, t), :] = xs[ci]
    return xs


def _chain(x, a1, a2, src, dst=None, prev=None, prev_base=0, reverse=False):
    return dict(x=x, a1=a1, a2=a2, src=src, dst=dst, prev=prev, prev_base=prev_base, reverse=reverse)


def _s5_fwd(u, d_skip, a1, a2, bm, cm, length, comm=None, bounds=()):
    tp = u.shape[0]
    cw = S5_W // N_BUNDLE
    sw = bm.shape[-1]
    n_full, n_tail = divmod(length, SCAN_CHUNK)
    t_tail = n_full * SCAN_CHUNK

    nbs = N_BUNDLE

    def body(u_ref, d_ref, a1_ref, a2_ref, bm_ref, cm_ref, y_ref, bnd_ref, *scratch):
        y_ref[...] = u_ref[...] * d_ref[...]
        ins, xss = (scratch[0:nbs], scratch[nbs:2 * nbs]), (scratch[2 * nbs:3 * nbs], scratch[3 * nbs:])
        cols = [slice(b * cw, (b + 1) * cw) for b in range(nbs)]

        def keep(dr, chunk, xs):
            for b in range(nbs):
                bnd_ref[dr, b, chunk] = xs[b]

        def load(dr, t0, n):
            for b in range(nbs):
                _tiles_store(ins[dr][b], 0, _dot(u_ref[pl.ds(t0, n), cols[b]].astype(BF16), bm_ref[dr, b]))

        def chains(dr, xs):
            return [_chain(xs[b], a1_ref[dr, b], a2_ref[dr, b], ins[dr][b], dst=xss[dr][b], reverse=dr == 1)
                    for b in range(nbs)]

        def emit(dr, t0, n):
            for b in range(nbs):
                y_ref[pl.ds(t0, n), cols[b]] += _dg(_tiles_load(xss[dr][b], 0, n).astype(BF16), cm_ref[dr, b], NT)

        zero = (jnp.zeros((8, 128), F32),) * nbs
        xb = zero
        if n_tail:
            keep(1, n_full, xb)
            load(1, t_tail, n_tail)
            xb = tuple(_scan(chains(1, xb), n_tail))
            emit(1, t_tail, n_tail)

        def pair(i, carry):
            j = n_full - 1 - i
            t0s = (pl.multiple_of(i * SCAN_CHUNK, SCAN_CHUNK), pl.multiple_of(j * SCAN_CHUNK, SCAN_CHUNK))
            keep(0, i, carry[0])
            keep(1, j, carry[1])
            for dr in range(2):
                load(dr, t0s[dr], SCAN_CHUNK)
            out = _scan(chains(0, carry[0]) + chains(1, carry[1]), SCAN_CHUNK)
            for dr in range(2):
                emit(dr, t0s[dr], SCAN_CHUNK)
            return tuple(out[:nbs]), tuple(out[nbs:])

        xf, _ = lax.fori_loop(0, n_full, pair, (zero, xb))
        if n_tail:
            keep(0, n_full, xf)
            load(0, t_tail, n_tail)
            _scan(chains(0, xf), n_tail)
            emit(0, t_tail, n_tail)

    n_chunks = n_full + (1 if n_tail else 0)
    tile = pl.BlockSpec((2, nbs, 8, 128), lambda b: (0, b, 0, 0))
    return _call(
        body, comm, bounds, (u, d_skip, a1, a2, bm, cm), name="s5_fwd", grid=(N_BUNDLE // nbs,),
        in_specs=[pl.BlockSpec((tp, nbs * cw), lambda b: (0, b)), pl.BlockSpec((1, nbs * cw), lambda b: (0, b)),
                  tile, tile, pl.BlockSpec((2, nbs, cw, sw), lambda b: (0, b, 0, 0)),
                  pl.BlockSpec((2, nbs, cw, sw), lambda b: (0, b, 0, 0))],
        out_specs=[pl.BlockSpec((tp, nbs * cw), lambda b: (0, b)),
                   pl.BlockSpec((2, nbs, n_chunks, 8, 128), lambda b: (0, b, 0, 0, 0))],
        out_shape=[_out((tp, S5_W), F32), _out((2, N_BUNDLE, n_chunks, 8, 128), F32)],
        scratch_shapes=[pltpu.VMEM((SCAN_CHUNK * 8, 128), F32)] * (4 * nbs),
        compiler_params=_cp(("arbitrary",), 48))


def _s5_bwd(u, dy, d_skip, a1, a2, bm, cm, bnd, length):
    tp = u.shape[0]
    cw = S5_W // N_BUNDLE
    sw = bm.shape[-1]
    half = sw // 2
    n_full, n_tail = divmod(length, SCAN_CHUNK)
    t_tail = n_full * SCAN_CHUNK
    n_chunks = bnd.shape[2]
    nbs = 2

    def body(u_ref, dy_ref, d_ref, a1_ref, a2_ref, bm_ref, cm_ref, bnd_ref, du_ref, dd_ref, dbm_ref, dcm_ref,
             da_ref, *scratch):
        du_ref[...] = dy_ref[...] * d_ref[...]
        dd_ref[...] = jnp.sum(dy_ref[...] * u_ref[...], axis=0, keepdims=True)
        dbm_ref[...] = jnp.zeros_like(dbm_ref)
        dcm_ref[...] = jnp.zeros_like(dcm_ref)
        da_ref[...] = jnp.zeros_like(da_ref)
        bu_s, dx_s, g_s, xp_s, x_s = ([scratch[(k * 2 + dr) * nbs:(k * 2 + dr + 1) * nbs] for dr in range(2)]
                                      for k in range(5))
        cols = [slice(b * cw, (b + 1) * cw) for b in range(nbs)]

        def chains(dr, chunk, t0, n, gs):
            out = []
            for b in range(nbs):
                _tiles_store(bu_s[dr][b], 0, _dot(u_ref[pl.ds(t0, n), cols[b]].astype(BF16), bm_ref[dr, b]))
                _tiles_store(dx_s[dr][b], 0, _dot(dy_ref[pl.ds(t0, n), cols[b]].astype(BF16), cm_ref[dr, b]))
                out.append(_chain(bnd_ref[dr, b, chunk], a1_ref[dr, b], a2_ref[dr, b], bu_s[dr][b],
                                  dst=x_s[dr][b], prev=xp_s[dr][b], reverse=dr == 1))
                out.append(_chain(gs[b], a1_ref[dr, b], -a2_ref[dr, b], dx_s[dr][b], dst=g_s[dr][b], reverse=dr == 0))
            return out

        def emit(dr, t0, n):
            rows = pl.ds(t0, n)
            for b in range(nbs):
                ub = u_ref[rows, cols[b]].astype(BF16)
                dyb = dy_ref[rows, cols[b]].astype(BF16)
                g = _tiles_load(g_s[dr][b], 0, n)
                gb = g.astype(BF16)
                du_ref[rows, cols[b]] += _dg(gb, bm_ref[dr, b], NT)
                dbm_ref[dr, b] += _dg(ub, gb, TN)
                xp = _tiles_load(xp_s[dr][b], 0, n)
                xp_r, xp_i = xp[:, 0:half], xp[:, half:]
                g_r, g_i = g[:, 0:half], g[:, half:]
                dcm_ref[dr, b] += _dg(dyb, _tiles_load(x_s[dr][b], 0, n).astype(BF16), TN)
                da_ref[dr, b] += jnp.concatenate([jnp.sum(g_r * xp_r + g_i * xp_i, axis=0, keepdims=True),
                                                  jnp.sum(g_i * xp_r - g_r * xp_i, axis=0, keepdims=True)], axis=1)

        def adjoints(out):
            return tuple(out[1::2])

        zero = (jnp.zeros((8, 128), F32),) * nbs
        g0 = zero
        if n_tail:
            g0 = adjoints(_scan(chains(0, n_full, t_tail, n_tail, g0), n_tail))
            emit(0, t_tail, n_tail)

        def pair(i, carry):
            j = n_full - 1 - i
            t0 = (pl.multiple_of(j * SCAN_CHUNK, SCAN_CHUNK), pl.multiple_of(i * SCAN_CHUNK, SCAN_CHUNK))
            both = chains(0, j, t0[0], SCAN_CHUNK, carry[0]) + chains(1, i, t0[1], SCAN_CHUNK, carry[1])
            out = _scan(both, SCAN_CHUNK)
            emit(0, t0[0], SCAN_CHUNK)
            emit(1, t0[1], SCAN_CHUNK)
            return adjoints(out[:2 * nbs]), adjoints(out[2 * nbs:])

        _, g1 = lax.fori_loop(0, n_full, pair, (g0, zero))
        if n_tail:
            _scan(chains(1, n_full, t_tail, n_tail, g1), n_tail)
            emit(1, t_tail, n_tail)

    tile = pl.BlockSpec((2, nbs, 8, 128), lambda b: (0, b, 0, 0))
    wide = pl.BlockSpec((2, nbs, cw, sw), lambda b: (0, b, 0, 0))
    col = pl.BlockSpec((tp, nbs * cw), lambda b: (0, b))
    row = pl.BlockSpec((1, nbs * cw), lambda b: (0, b))
    arow = pl.BlockSpec((2, nbs, 1, sw), lambda b: (0, b, 0, 0))
    return pl.pallas_call(
        body, name="s5_bwd", grid=(N_BUNDLE // nbs,),
        in_specs=[col, col, row, tile, tile, wide, wide,
                  pl.BlockSpec((2, nbs, n_chunks, 8, 128), lambda b: (0, b, 0, 0, 0))],
        out_specs=[col, row, wide, wide, arow],
        out_shape=[_out((tp, S5_W), F32), _out((1, S5_W), F32),
                   _out((2, N_BUNDLE, cw, sw), F32), _out((2, N_BUNDLE, cw, sw), F32),
                   _out((2, N_BUNDLE, 1, sw), F32)],
        scratch_shapes=[pltpu.VMEM((SCAN_CHUNK * 8, 128), F32)] * (10 * nbs),
        compiler_params=_cp(("arbitrary",), 56),
    )(*_in_hbm(u, dy, d_skip, a1, a2, bm, cm, bnd))


def _row_tile(tp):
    return max(tm for tm in range(16, 449, 16) if tp % tm == 0)


def _step(x, target, bufs, gains, s5, rpb, c_arr, kc_arr, me_arr):
    n_tok = x.shape[0]
    first = ["ffn1_w_gate", "ffn1_w_up", "ffn1_w_down", "meta_tokens"]
    bias, got = _bias_tables(rpb, n_tok // GRID_W, _gather_comm([bufs[n] for n in first]), (0, N_HEADS - 1))
    w = dict(zip(first, got))
    meta = w["meta_tokens"].transpose(1, 0, 2).reshape(N_META, D)
    length = N_META + n_tok
    tp = length + 16
    tm = _row_tile(tp)
    tmb = tm
    n_rows = n_tok // GRID_W
    pad = jnp.zeros((tp - length, D), F32)
    h0 = jnp.concatenate([meta, x, pad], axis=0)
    tgt = jnp.concatenate([jnp.zeros((N_META, D), F32), target, pad], axis=0)

    lam_re, _ = lax.optimization_barrier((s5["lam_re"], bias))
    s5p = (lam_re, s5["lam_im"], s5["log_dt"].reshape(2 * S5_G, 1), s5["b_re"], s5["b_im"])
    a1_m, a2_m, bm16, cm16 = _s5_params(*s5p, s5["c_re"], s5["c_im"])

    mid = ["w_in", "s5_w_glu", "w_out"]
    (h1, gate1, up1, f1), got = _ffn_fwd(
        "ffn1_fwd", h0, gains["ffn1_pre_g"], gains["ffn1_post_g"], w["ffn1_w_gate"], w["ffn1_w_up"], w["ffn1_w_down"],
        tm, _gather_comm([bufs[n] for n in mid]), (0, (tp // tm) * N_CHIP * 3 // 5))
    w.update(zip(mid, got))
    q, k, v, u = _mix_in(h1, gains["mix_pre_g"], w["w_in"], tm)
    (o_na,), (gate_ici, up_ici) = _attn_fwd(
        q, k, v, bias, n_tok, _gather_comm([bufs["ffn2_w_gate"], bufs["ffn2_w_up"]], pair=False), (0,))
    (y_pre, s5_bnd), (w["ffn2_w_gate"], w["ffn2_w_up"], down_ici) = _s5_fwd(
        u, gains["s5_d"], a1_m, a2_m, bm16, cm16, length,
        _merge_comm(_gather_comm([gate_ici, up_ici], ici=False),
                    _gather_comm([bufs["ffn2_w_down"]], pair=False)), (0,))
    w_glu = w["s5_w_glu"].reshape(S5_W, S5_W)
    w_out = w["w_out"].reshape(D, D)
    (h2, mix), (w["ffn2_w_down"],) = _mix_out(
        o_na, y_pre, h1, w_glu, gains["s5_b_glu"], gains["na_out_g"], gains["s5_out_g"], w_out, gains["mix_post_g"], tm,
        _gather_comm([down_ici], ici=False), (0,))
    (h3, gate2, up2, f2), _ = _ffn_fwd("ffn2_fwd", h2, gains["ffn2_pre_g"], gains["ffn2_post_g"],
                                       w["ffn2_w_gate"], w["ffn2_w_up"], w["ffn2_w_down"], tm)
    dh3, df2, loss, dg_final, dg_post2 = _final_loss(h3, gains["final_g"], tgt, f2, gains["ffn2_post_g"], n_tok, tm)

    ffn2 = ["ffn2_w_gate", "ffn2_w_up", "ffn2_w_down"]
    ffn1 = ["ffn1_w_gate", "ffn1_w_up", "ffn1_w_down"]
    out2, _ = _ffn_bwd("ffn2_bwd", h2, gains["ffn2_pre_g"], df2, gate2, up2,
                       w["ffn2_w_gate"], w["ffn2_w_up"], w["ffn2_w_down"], tmb)
    dxn2 = out2[3]
    sums2 = _chip_sums("chip_sums_ffn2", out2[0:3], out2[4:7], c_arr)
    (dh2, dg_pre2), _ = _ffn_pre_bwd("ffn2_pre_bwd", dh3, dxn2, h2, gains["ffn2_pre_g"], tm)
    do_na, dy_pre, dw_out, dw_glu, dg_mpost, dg_na, dg_s5, db_glu = _mix_out_bwd(
        dh2, mix, o_na, y_pre, w_glu, gains["s5_b_glu"], gains["na_out_g"], gains["s5_out_g"], w_out,
        gains["mix_post_g"], tm)
    (dq, dk, dv, dtb), recv3 = _attn_bwd(q, k, v, bias, do_na, n_tok, _scatter_comm(sums2), (0,))
    totals2 = _total_sums("total_sums_ffn2", sums2, recv3, kc_arr)
    du, dd, dbm, dcm, da_m = _s5_bwd(u, dy_pre, gains["s5_d"], a1_m, a2_m, bm16, cm16, s5_bnd, length)
    (dh1, df1, dw_in, dg_mpre, dg_post1), done2 = _mix_in_bwd(
        dq, dk, dv, du, h1, gains["mix_pre_g"], w["w_in"], dh2, f1, gains["ffn1_post_g"], tm,
        _assemble_comm(totals2), (0,))
    pieces = dict(zip(ffn2, done2))

    e, _ = _diag_onehot()
    n_dr = 2 * KH - 1
    drpb = _rpb_collapse(dtb.reshape(N_HEADS * n_dr, GRID_W * GRID_W), jnp.asarray(e.T))
    drpb = drpb[:, :2 * KW - 1].reshape(N_HEADS, n_dr, 2 * KW - 1).transpose(1, 0, 2).reshape(N_HEADS * n_dr, 2 * KW - 1)
    dlam_re, dlam_im, dlog_dt, db_re, db_im, dc_re, dc_im = _s5_params_bwd(*s5p, da_m, dbm, dcm)
    early = {"ffn1_post_g": dg_post1, "mix_pre_g": dg_mpre, "na_rpb": drpb,
             "s5_lam_re": dlam_re, "s5_lam_im": dlam_im, "s5_log_dt": dlog_dt.reshape(2, S5_G),
             "s5_b_re": db_re, "s5_b_im": db_im, "s5_c_re": dc_re, "s5_c_im": dc_im,
             "s5_d": dd, "s5_b_glu": db_glu, "na_out_g": dg_na,
             "s5_out_g": dg_s5, "mix_post_g": dg_mpost, "ffn2_pre_g": dg_pre2, "ffn2_post_g": dg_post2,
             "final_g": dg_final}
    names = list(early)
    slots = _small_pack([early[n] for n in names], me_arr)

    out1, slots = _ffn_bwd("ffn1_bwd", h0, gains["ffn1_pre_g"], df1, gate1, up1,
                           w["ffn1_w_gate"], w["ffn1_w_up"], w["ffn1_w_down"], tmb, _spread_comm(slots), (0,))
    small = dict(zip(names, _small_total(slots, [early[n].shape for n in names])))
    sums1 = _chip_sums("chip_sums_ffn1", out1[0:3], out1[4:7], c_arr)
    flight1 = _scatter_start("ffn1", sums1)
    token = flight1[4]
    rest = [dw_in, dw_glu.reshape(N_CHIP, S5_W // N_CHIP, S5_W), dw_out.reshape(N_CHIP, D // N_CHIP, D)]
    (dh0, dg_pre1), recv_rest = _ffn_pre_bwd("ffn1_pre_bwd", dh1, out1[3], h0, gains["ffn1_pre_g"] + token[0:1, 0:1],
                                             tm, _exchange_comm(rest), (0,))
    sums = _chip_sums("chip_sums_rest", rest, recv_rest, c_arr)
    flight2 = _scatter_start("rest", sums)
    return loss[0, 0], dh0, pieces, small, {"ffn1_pre_g": dg_pre1}, (ffn1, flight1[:4]), (mid, flight2[:4])


def _mesh_pos():
    return lax.axis_index("x"), lax.axis_index("y"), lax.axis_index("c")


def _other_chips(x, y):
    return [(1 - x, y), (x, 1 - y), (1 - x, 1 - y)]


class _Comm:
    def __init__(self, ins, out_shape, aliases, parts):
        self.ins, self.out_shape, self.aliases, self.parts = list(ins), list(out_shape), dict(aliases), list(parts)
        self.n_sems = sum(p[0] for p in parts)

    def bases(self):
        out, base = [], 0
        for n_sems, _, _ in self.parts:
            out.append(base)
            base += n_sems
        return out


def _run_comm(name, comm):
    n_i, n_o = len(comm.ins), len(comm.out_shape)

    def body(*refs):
        ins, outs = refs[:n_i], refs[n_i:n_i + n_o]
        send_sems, recv_sems = refs[n_i + n_o:]
        for base, (_, start, finish) in zip(comm.bases(), comm.parts):
            start(ins, outs, send_sems, recv_sems, base)
            finish(ins, outs, send_sems, recv_sems, base)

    return pl.pallas_call(
        body, name=name, out_shape=comm.out_shape, in_specs=[ANY] * n_i, out_specs=[ANY] * n_o,
        input_output_aliases=comm.aliases,
        scratch_shapes=[pltpu.SemaphoreType.DMA((comm.n_sems,)), pltpu.SemaphoreType.DMA((comm.n_sems,))],
    )(*_in_hbm(*comm.ins))


def _call(body, comm, bounds, args, *, name, grid, in_specs, out_specs, out_shape, scratch_shapes=(),
          compiler_params=None):
    in_specs, out_specs, out_shape, scratch_shapes = list(in_specs), list(out_specs), list(out_shape), list(scratch_shapes)
    if comm is None:
        return pl.pallas_call(body, name=name, grid=grid, in_specs=in_specs, out_specs=out_specs, out_shape=out_shape,
                              scratch_shapes=scratch_shapes, compiler_params=compiler_params)(*_in_hbm(*args)), []
    n_in, n_out, n_scr = len(in_specs), len(out_specs), len(scratch_shapes)
    n_ci, n_co = len(comm.ins), len(comm.out_shape)
    n_steps = int(np.prod(grid))
    assert len(bounds) == len(comm.parts) and all(0 <= b < n_steps for b in bounds) and list(bounds) == sorted(bounds)

    def fused(*refs):
        a = n_in
        b = a + n_ci
        c = b + n_out
        d = c + n_co
        e = d + n_scr
        cargs = (refs[a:b], refs[c:d], refs[e], refs[e + 1])
        step = pl.program_id(0)
        for ax in range(1, len(grid)):
            step = step * grid[ax] + pl.program_id(ax)
        bases = comm.bases()
        for p, (_, start, finish) in enumerate(comm.parts):
            @pl.when(step == bounds[p])
            def _(p=p, start=start):
                if p > 0:
                    comm.parts[p - 1][2](*cargs, bases[p - 1])
                start(*cargs, bases[p])
        body(*(refs[:a] + refs[b:c] + refs[d:e]))

        @pl.when(step == n_steps - 1)
        def _():
            comm.parts[-1][2](*cargs, bases[-1])

    res = pl.pallas_call(
        fused, name=name, grid=grid, in_specs=in_specs + [ANY] * n_ci, out_specs=out_specs + [ANY] * n_co,
        out_shape=out_shape + comm.out_shape,
        scratch_shapes=scratch_shapes + [pltpu.SemaphoreType.DMA((comm.n_sems,)), pltpu.SemaphoreType.DMA((comm.n_sems,))],
        input_output_aliases={n_in + i: n_out + j for i, j in comm.aliases.items()},
        compiler_params=compiler_params)(*_in_hbm(*args, *comm.ins))
    return res[:n_out], res[n_out:]


def _remote(src, dst, send_sems, recv_sems, idx, to):
    return pltpu.make_async_remote_copy(src_ref=src, dst_ref=dst, send_sem=send_sems.at[idx],
                                        recv_sem=recv_sems.at[idx], device_id=to, device_id_type=MESH_ID)


def _gather_comm(bufs, ici=True, pair=True):
    n = len(bufs)

    def half(ref, k, pc):
        rh = ref.shape[1] // 2
        return ref.at[k, pl.ds(pc * rh, rh), :]

    def ici_start(ins, outs, ss, rs, base):
        x, y, c = _mesh_pos()
        for a in range(n):
            mine = half(outs[a], 2 * x + y, c)
            for j, chip in enumerate(_other_chips(x, y)):
                _remote(mine, mine, ss, rs, base + 3 * a + j, (*chip, c)).start()

    def ici_finish(ins, outs, ss, rs, base):
        x, y, c = _mesh_pos()
        for a in range(n):
            for j, chip in enumerate(_other_chips(x, y)):
                theirs = half(outs[a], 2 * chip[0] + chip[1], c)
                _remote(theirs, theirs, ss, rs, base + 3 * a + j, (*chip, c)).wait()

    def pair_copy(outs, ss, rs, base, a):
        x, y, c = _mesh_pos()
        rh = outs[a].shape[1] // 2
        held = outs[a].at[:, pl.ds(c * rh, rh), :]
        return _remote(held, held, ss, rs, base + a, (x, y, 1 - c))

    def pair_start(ins, outs, ss, rs, base):
        for a in range(n):
            pair_copy(outs, ss, rs, base, a).start()

    def pair_finish(ins, outs, ss, rs, base):
        for a in range(n):
            pair_copy(outs, ss, rs, base, a).wait()

    parts = ([(3 * n, ici_start, ici_finish)] if ici else []) + ([(n, pair_start, pair_finish)] if pair else [])
    return _Comm(bufs, [_out(b.shape, b.dtype) for b in bufs], {a: a for a in range(n)}, parts)


def _merge_comm(*comms):
    ins, shapes, aliases, subs, base = [], [], {}, [], 0
    for cm in comms:
        (n_sems, start, finish), = cm.parts
        i0, o0 = len(ins), len(shapes)
        subs.append((slice(i0, i0 + len(cm.ins)), slice(o0, o0 + len(cm.out_shape)), base, start, finish))
        aliases.update({i0 + i: o0 + j for i, j in cm.aliases.items()})
        ins += cm.ins
        shapes += cm.out_shape
        base += n_sems

    def start_all(ins_r, outs_r, ss, rs, b):
        for si, so, off, start, _ in subs:
            start(ins_r[si], outs_r[so], ss, rs, b + off)

    def finish_all(ins_r, outs_r, ss, rs, b):
        for si, so, off, _, finish in subs:
            finish(ins_r[si], outs_r[so], ss, rs, b + off)

    return _Comm(ins, shapes, aliases, [(base, start_all, finish_all)])


def _own_half_buffers(pieces, dtypes, kc_arr):
    n = len(pieces)

    def body(kc_ref, *refs):
        for a in range(n):
            refs[n + a][0] = refs[a][...].astype(dtypes[a])

    def half(p):
        return p.shape[0] // 2, p.shape[1]

    return pl.pallas_call(
        body, name="own_halves",
        out_shape=[_out((N_CHIP,) + p.shape, dt) for p, dt in zip(pieces, dtypes)],
        grid_spec=pltpu.PrefetchScalarGridSpec(
            num_scalar_prefetch=1, grid=(1,),
            in_specs=[pl.BlockSpec(half(p), lambda i, kc: (kc[1], 0)) for p in pieces],
            out_specs=[pl.BlockSpec((1,) + half(p), lambda i, kc: (kc[0], kc[1], 0)) for p in pieces]),
        compiler_params=_cp(("arbitrary",), 48),
    )(kc_arr, *_in_hbm(*pieces))


def _exchange_comm(grads):
    n = len(grads)

    def copy(ins, outs, ss, rs, base, a):
        x, y, c = _mesh_pos()
        rh = ins[a].shape[1] // 2
        return _remote(ins[a].at[:, pl.ds((1 - c) * rh, rh), :], outs[a], ss, rs, base + a, (x, y, 1 - c))

    def start(ins, outs, ss, rs, base):
        for a in range(n):
            copy(ins, outs, ss, rs, base, a).start()

    def finish(ins, outs, ss, rs, base):
        for a in range(n):
            copy(ins, outs, ss, rs, base, a).wait()

    shapes = [_out((N_CHIP, g.shape[1] // 2, g.shape[2]), g.dtype) for g in grads]
    return _Comm(grads, shapes, {}, [(n, start, finish)])


def _chip_sums(name, grads, recvs, c_arr):
    n = len(grads)
    halves = [(1, g.shape[1] // 2, g.shape[2]) for g in grads]

    def body(c_ref, *refs):
        for a in range(n):
            refs[2 * n + a][...] = (refs[a][...] + refs[n + a][...]).astype(BF16)

    return pl.pallas_call(
        body, name=name, out_shape=[_out((N_CHIP,) + h[1:], BF16) for h in halves],
        grid_spec=pltpu.PrefetchScalarGridSpec(
            num_scalar_prefetch=1, grid=(N_CHIP,),
            in_specs=[pl.BlockSpec(h, lambda j, c_ref: (j, c_ref[0], 0)) for h in halves] +
                     [pl.BlockSpec(h, lambda j, c_ref: (j, 0, 0)) for h in halves],
            out_specs=[pl.BlockSpec(h, lambda j, c_ref: (j, 0, 0)) for h in halves]),
        compiler_params=_cp(("arbitrary",), 40),
    )(c_arr, *_in_hbm(*grads, *recvs))


def _scatter_comm(sums):
    n = len(sums)

    def copies(ins, outs, ss, rs, base):
        x, y, c = _mesh_pos()
        return [_remote(ins[a].at[2 * chip[0] + chip[1]], outs[a].at[j], ss, rs, base + 3 * a + j, (*chip, c))
                for a in range(n) for j, chip in enumerate(_other_chips(x, y))]

    def start(ins, outs, ss, rs, base):
        for cp in copies(ins, outs, ss, rs, base):
            cp.start()

    def finish(ins, outs, ss, rs, base):
        for cp in copies(ins, outs, ss, rs, base):
            cp.wait()

    shapes = [_out((3,) + s.shape[1:], s.dtype) for s in sums]
    return _Comm(sums, shapes, {}, [(3 * n, start, finish)])


def _scatter_copies(ins, lands, send_sems, recv_sems):
    x, y, c = _mesh_pos()
    return [_remote(ins[a].at[2 * chip[0] + chip[1]], lands[a].at[j], send_sems, recv_sems, 3 * a + j, (*chip, c))
            for a in range(len(ins)) for j, chip in enumerate(_other_chips(x, y))]


def _scatter_start(name, sums):
    n = len(sums)
    lands = [lax.empty((3,) + s.shape[1:], s.dtype) for s in sums]
    hbm = pl.BlockSpec(memory_space=pltpu.HBM)
    sem = pl.BlockSpec(memory_space=pltpu.SEMAPHORE)

    def body(*refs):
        ins, land_refs = refs[:n], refs[n:2 * n]
        send_sems, recv_sems = refs[2 * n], refs[2 * n + 1]
        token = refs[-1]
        for cp in _scatter_copies(ins, land_refs, send_sems, recv_sems):
            cp.start()
        token[...] = jnp.zeros_like(token)

    res = pl.pallas_call(
        body, name=name + "_scatter_start",
        out_shape=(pltpu.SemaphoreType.DMA((3 * n,)), pltpu.SemaphoreType.DMA((3 * n,)),
                   *[pltpu.HBM(s.shape, s.dtype) for s in sums], *[pltpu.HBM(ld.shape, ld.dtype) for ld in lands],
                   jax.ShapeDtypeStruct((8, 128), F32)),
        in_specs=[hbm] * (2 * n), out_specs=(sem, sem, *[hbm] * (2 * n), pl.BlockSpec(memory_space=pltpu.VMEM)),
        input_output_aliases={i: 2 + i for i in range(2 * n)},
        compiler_params=pltpu.CompilerParams(has_side_effects=pltpu.SideEffectType.DATAFLOW_SIDE_EFFECTING),
    )(*[pltpu.with_memory_space_constraint(a, pltpu.HBM) for a in list(sums) + lands])
    return res[0], res[1], list(res[2:2 + n]), list(res[2 + n:2 + 2 * n]), res[-1]


def _scatter_wait(name, send_sems, recv_sems, sums, lands, after):
    n = len(sums)
    hbm = pl.BlockSpec(memory_space=pltpu.HBM)
    sem = pl.BlockSpec(memory_space=pltpu.SEMAPHORE)

    def body(*refs):
        ins, land_refs = refs[:n], refs[n:2 * n]
        for cp in _scatter_copies(ins, land_refs, refs[2 * n], refs[2 * n + 1]):
            cp.wait_send()
            cp.wait_recv()

    res = pl.pallas_call(
        body, name=name + "_scatter_wait",
        out_shape=tuple([pltpu.HBM(s.shape, s.dtype) for s in sums] + [pltpu.HBM(ld.shape, ld.dtype) for ld in lands]),
        in_specs=[hbm] * (2 * n) + [sem, sem, pl.BlockSpec(memory_space=pl.ANY)], out_specs=tuple([hbm] * (2 * n)),
        input_output_aliases={i: i for i in range(2 * n)},
        compiler_params=pltpu.CompilerParams(has_side_effects=pltpu.SideEffectType.DATAFLOW_SIDE_EFFECTING),
    )(*sums, *lands, send_sems, recv_sems, after)
    return list(res[:n]), list(res[n:])


def _total_sums(name, sums, recv3, kc_arr):
    n = len(sums)
    dims = [s.shape[1:] for s in sums]

    def body(kc_ref, *refs):
        for a in range(n):
            s_ref, r_ref = refs[a], refs[n + a]
            t = s_ref[0].astype(F32) + r_ref[0].astype(F32)
            t = t + r_ref[1].astype(F32)
            refs[2 * n + a][...] = t + r_ref[2].astype(F32)

    return pl.pallas_call(
        body, name=name, out_shape=[_out((2 * rh, cc), F32) for rh, cc in dims],
        grid_spec=pltpu.PrefetchScalarGridSpec(
            num_scalar_prefetch=1, grid=(1,),
            in_specs=[pl.BlockSpec((1, rh, cc), lambda i, kc_ref: (kc_ref[0], 0, 0)) for rh, cc in dims] +
                     [pl.BlockSpec((3, rh, cc), lambda i, kc_ref: (0, 0, 0)) for rh, cc in dims],
            out_specs=[pl.BlockSpec((rh, cc), lambda i, kc_ref: (kc_ref[1], 0)) for rh, cc in dims]),
        compiler_params=_cp(("arbitrary",), 48),
    )(kc_arr, *_in_hbm(*sums, *recv3))


def _assemble_comm(totals):
    n = len(totals)

    def copy(outs, ss, rs, base, a):
        x, y, c = _mesh_pos()
        rh = outs[a].shape[0] // 2
        here = outs[a].at[pl.ds(c * rh, rh), :]
        return _remote(here, here, ss, rs, base + a, (x, y, 1 - c))

    def start(ins, outs, ss, rs, base):
        for a in range(n):
            copy(outs, ss, rs, base, a).start()

    def finish(ins, outs, ss, rs, base):
        for a in range(n):
            copy(outs, ss, rs, base, a).wait()

    shapes = [_out(t.shape, t.dtype) for t in totals]
    return _Comm(totals, shapes, {a: a for a in range(n)}, [(n, start, finish)])


def _small_layout(shapes):
    n = len(shapes)
    narrow_w = 64
    wide = [a for a in range(n) if shapes[a][1] > narrow_w]
    narrow = sorted((a for a in range(n) if shapes[a][1] <= narrow_w), key=lambda a: -shapes[a][0])
    offs, cols, groups, widths, rows = {}, {}, [], [], []
    if wide:
        r = 0
        for a in wide:
            offs[a], cols[a] = r, 0
            r += shapes[a][0]
        groups.append(wide)
        widths.append(max(shapes[a][1] for a in wide))
        rows.append(-(-r // 8) * 8)
    if narrow:
        heights = [0, 0]
        for a in narrow:
            side = 0 if heights[0] <= heights[1] else 1
            offs[a], cols[a] = heights[side], side * narrow_w
            heights[side] += shapes[a][0]
        groups.append(narrow)
        widths.append(2 * narrow_w)
        rows.append(-(-max(heights) // 8) * 8)

    def window(ref, a):
        return ref.at[offs[a]:offs[a] + shapes[a][0], cols[a]:cols[a] + shapes[a][1]]

    return groups, widths, rows, window


def _small_pack(arrays, me_arr):
    shapes = [a.shape for a in arrays]
    groups, widths, rows, window = _small_layout(shapes)
    n, n_g = len(arrays), len(groups)

    def body(me_ref, *refs):
        ins, outs = refs[:n], refs[n:]
        for gi, g in enumerate(groups):
            outs[gi][...] = jnp.zeros_like(outs[gi])
            for a in g:
                window(outs[gi].at[0], a)[...] = ins[a][...]

    return pl.pallas_call(
        body, name="small_pack", out_shape=[_out((8, r, w), F32) for r, w in zip(rows, widths)],
        grid_spec=pltpu.PrefetchScalarGridSpec(
            num_scalar_prefetch=1, grid=(1,), in_specs=[pl.BlockSpec(s, lambda i, me: (0, 0)) for s in shapes],
            out_specs=[pl.BlockSpec((1, r, w), lambda i, me: (me[0], 0, 0)) for r, w in zip(rows, widths)]),
        compiler_params=_cp(("arbitrary",), 32),
    )(me_arr, *_in_hbm(*arrays))


def _spread_comm(slots):
    n = len(slots)
    flips = [(dx, dy, dc) for dx in range(2) for dy in range(2) for dc in range(2)][1:]

    def copies(outs, ss, rs, base):
        x, y, c = _mesh_pos()
        mine = 4 * x + 2 * y + c
        return [_remote(outs[a].at[mine], outs[a].at[mine], ss, rs, base + 7 * a + f,
                        (x ^ dx, y ^ dy, c ^ dc)) for a in range(n) for f, (dx, dy, dc) in enumerate(flips)]

    def start(ins, outs, ss, rs, base):
        for cp in copies(outs, ss, rs, base):
            cp.start()

    def finish(ins, outs, ss, rs, base):
        for cp in copies(outs, ss, rs, base):
            cp.wait()

    return _Comm(slots, [_out(s.shape, s.dtype) for s in slots], {a: a for a in range(n)}, [(7 * n, start, finish)])


def _small_total(slots, shapes):
    groups, widths, rows, window = _small_layout(shapes)
    n, n_g = len(shapes), len(groups)

    def body(*refs):
        ins, outs, acc = refs[:n_g], refs[n_g:n_g + n], refs[n_g + n:]
        for gi, g in enumerate(groups):
            t = ins[gi][0] + ins[gi][1]
            for d in range(2, 8):
                t = t + ins[gi][d]
            acc[gi][...] = t
            for a in g:
                outs[a][...] = window(acc[gi], a)[...]

    return pl.pallas_call(
        body, name="small_total", grid=(1,), out_shape=[_out(s, F32) for s in shapes],
        in_specs=[_full(s.shape) for s in slots], out_specs=[_full(s) for s in shapes],
        scratch_shapes=[pltpu.VMEM((r, w), F32) for r, w in zip(rows, widths)],
        compiler_params=_cp(("arbitrary",), 48),
    )(*_in_hbm(*slots))


def _small_allreduce(arrays, comm):
    n = len(arrays)
    shapes = [a.shape for a in arrays]
    groups, widths, rows, window = _small_layout(shapes)
    n_g = len(groups)

    def body(*refs):
        ins, outs = refs[:n], refs[n:2 * n]
        pack, sib, csum, every = (refs[2 * n + i * n_g:2 * n + (i + 1) * n_g] for i in range(4))
        send_sems, recv_sems = refs[2 * n + 4 * n_g:]
        x, y, c = _mesh_pos()
        k = 2 * x + y
        for gi, g in enumerate(groups):
            pack[gi][...] = jnp.zeros_like(pack[gi])
            for a in g:
                window(pack[gi], a)[...] = ins[a][...]
        cps = [_remote(pack[gi], sib[gi], send_sems, recv_sems, gi, (x, y, 1 - c)) for gi in range(n_g)]
        for cp in cps:
            cp.start()
        for cp in cps:
            cp.wait()
        for gi in range(n_g):
            csum[gi][...] = pack[gi][...] + sib[gi][...]
            every[gi][k] = csum[gi][...]
        cps = [_remote(csum[gi], every[gi].at[k], send_sems, recv_sems, n_g + 3 * gi + j, (*chip, c))
               for gi in range(n_g) for j, chip in enumerate(_other_chips(x, y))]
        for cp in cps:
            cp.start()
        for cp in cps:
            cp.wait()
        for gi, g in enumerate(groups):
            pack[gi][...] = ((every[gi][0] + every[gi][1]) + every[gi][2]) + every[gi][3]
            for a in g:
                outs[a][...] = window(pack[gi], a)[...]

    bufs = [pltpu.VMEM((r, w), F32) for r, w in zip(rows, widths)]
    return _call(
        body, comm, (0,), arrays, name="small_allreduce", grid=(1,), out_shape=[_out(s, F32) for s in shapes],
        in_specs=[_full(s) for s in shapes], out_specs=[_full(s) for s in shapes],
        scratch_shapes=bufs * 3 + [pltpu.VMEM((N_CHIP, r, w), F32) for r, w in zip(rows, widths)] +
                       [pltpu.SemaphoreType.DMA((4 * n_g,)), pltpu.SemaphoreType.DMA((4 * n_g,))],
        compiler_params=_cp(("arbitrary",), 40))


def _adamw_small(ws, gs, ms, vs, comm):
    n = len(ws)

    def body(*refs):
        w, g, m, v, d, mo, vo = (refs[i * n:(i + 1) * n] for i in range(7))
        for a in range(n):
            d[a][...], mo[a][...], vo[a][...] = _adamw_math(w[a][...], g[a][...], m[a][...], v[a][...])

    specs = [_full(w.shape) for w in ws]
    res, got = _call(
        body, comm, (0,), (*ws, *gs, *ms, *vs), name="adamw_small", grid=(1,),
        out_shape=[_out(w.shape, F32) for w in ws] * 3,
        in_specs=specs * 4, out_specs=specs * 3, compiler_params=_cp(("arbitrary",), 40))
    return (res[:n], res[n:2 * n], res[2 * n:]), got


def _adamw_math(w, g, m, v):
    m = ADAM_B1 * m + (1.0 - ADAM_B1) * g
    v = ADAM_B2 * v + (1.0 - ADAM_B2) * (g * g)
    m_hat = m / (1.0 - ADAM_B1 ** ADAM_STEP)
    v_hat = v / (1.0 - ADAM_B2 ** ADAM_STEP)
    delta = -ADAM_LR * (m_hat / (jnp.sqrt(v_hat) + ADAM_EPS) + ADAM_WD * w)
    return delta, m, v


def _adamw_group(name, ws, gs, ms, vs):
    n = len(ws)
    steps = 8
    specs = [_rows(w.shape[0] // steps, w.shape[1]) for w in ws]
    assert all(w.shape[0] % (8 * steps) == 0 for w in ws)

    def body(*refs):
        w, g, m, v, d, mo, vo = (refs[i * n:(i + 1) * n] for i in range(7))
        for a in range(n):
            d[a][...], mo[a][...], vo[a][...] = _adamw_math(w[a][...], g[a][...], m[a][...], v[a][...])

    res = pl.pallas_call(
        body, name=name, grid=(steps,), in_specs=specs * 4, out_specs=specs * 3,
        out_shape=[_out(w.shape, F32) for w in ws] * 3, compiler_params=_cp(("arbitrary",), 40),
    )(*_in_hbm(*ws, *gs, *ms, *vs))
    return res[:n], res[n:2 * n], res[2 * n:]


def _as_matrix(name, a):
    if name == "na_rpb":
        return a[0].transpose(1, 0, 2).reshape(N_HEADS * (2 * KH - 1), 2 * KW - 1)
    if name in ("s5_b_re", "s5_b_im"):
        return a.transpose(0, 1, 2, 4, 3).reshape(2 * S5_G * S5_H, S5_P)
    if name in ("s5_c_re", "s5_c_im"):
        return a.reshape(2 * S5_G * S5_H, S5_P)
    if name in ("s5_lam_re", "s5_lam_im"):
        return a.reshape(2 * S5_G, S5_P)
    if name == "s5_log_dt":
        return a.reshape(2, S5_G)
    return a


def _from_matrix(name, m):
    if name == "na_rpb":
        return m.reshape(2 * KH - 1, N_HEADS, 2 * KW - 1).transpose(1, 0, 2)[None]
    if name in ("s5_b_re", "s5_b_im"):
        return m.reshape(1, 2, S5_G, S5_H, S5_P).transpose(0, 1, 2, 4, 3)
    if name in ("s5_c_re", "s5_c_im"):
        return m.reshape(1, 2, S5_G, S5_H, S5_P)
    if name in ("s5_lam_re", "s5_lam_im"):
        return m.reshape(1, 2, S5_G, S5_P)
    if name == "s5_log_dt":
        return m.reshape(1, 2, S5_G)
    return m


WEIGHTS = ["meta_tokens", "ffn1_pre_g", "ffn1_post_g", "ffn1_w_gate", "ffn1_w_up", "ffn1_w_down", "mix_pre_g", "w_in",
           "na_rpb", "s5_lam_re", "s5_lam_im", "s5_log_dt", "s5_b_re", "s5_b_im", "s5_c_re", "s5_c_im", "s5_d",
           "s5_w_glu", "s5_b_glu", "na_out_g", "s5_out_g", "w_out", "mix_post_g", "ffn2_pre_g", "ffn2_post_g",
           "ffn2_w_gate", "ffn2_w_up", "ffn2_w_down", "final_g"]
BIG = ["ffn1_w_gate", "ffn1_w_up", "ffn1_w_down", "w_in", "s5_w_glu", "w_out", "ffn2_w_gate", "ffn2_w_up",
       "ffn2_w_down"]
TRANSPOSED = ["ffn1_w_gate", "ffn1_w_up", "ffn2_w_gate", "ffn2_w_up"]
GAINS = ["ffn1_pre_g", "ffn1_post_g", "mix_pre_g", "s5_d", "s5_b_glu", "na_out_g", "s5_out_g", "mix_post_g",
         "ffn2_pre_g", "ffn2_post_g", "final_g"]
SMALL = [n for n in WEIGHTS if n not in BIG]


def kernel(*args):
    names = ["x"] + WEIGHTS + ["loss_target"] + ["m_" + n for n in WEIGHTS] + ["v_" + n for n in WEIGHTS]
    assert len(args) == len(names)
    given = dict(zip(names, args))
    x_pos, y_pos, c_pos = _mesh_pos()
    k_pos = 2 * x_pos + y_pos
    c_arr = jnp.reshape(c_pos, (1,)).astype(jnp.int32)
    kc_arr = jnp.stack([k_pos, c_pos]).astype(jnp.int32)

    def piece(name, a):
        return a[0].T if name in TRANSPOSED else a[0]

    def unpiece(name, a):
        return a.T[None] if name in TRANSPOSED else a[None]

    placed = BIG + ["meta_tokens"]
    bufs = dict(zip(placed, _own_half_buffers([piece(n, given[n]) for n in BIG] + [given["meta_tokens"]],
                                              [BF16] * len(BIG) + [F32], kc_arr)))

    gains = {n: given[n] for n in GAINS}
    s5 = {n: _as_matrix("s5_" + n, given["s5_" + n])
          for n in ["lam_re", "lam_im", "log_dt", "b_re", "b_im", "c_re", "c_im"]}
    me_arr = jnp.reshape(4 * x_pos + 2 * y_pos + c_pos, (1,)).astype(jnp.int32)
    loss, dh0, pieces, small, late, (ffn1, flight1), (mid, flight2) = _step(
        given["x"][0], given["loss_target"][0], bufs, gains, s5, given["na_rpb"][0], c_arr, kc_arr, me_arr)
    loss = lax.psum(loss, ("x", "y", "c"))
    n_tok = given["x"].shape[1]
    grad_x = dh0[N_META:N_META + n_tok][None]

    late["meta_tokens"] = dh0[:N_META]
    out_g, out_d, out_m, out_v = {}, {}, {}, {}

    def update_big(group, names):
        g2 = [pieces[n] for n in names]
        d2, m2, v2 = _adamw_group("adamw_" + group, [piece(n, given[n]) for n in names], g2,
                                  [piece(n, given["m_" + n]) for n in names], [piece(n, given["v_" + n]) for n in names])
        for n, g, dd, mm, vv in zip(names, g2, d2, m2, v2):
            out_g[n], out_d[n], out_m[n], out_v[n] = (unpiece(n, t) for t in (g, dd, mm, vv))
        return v2

    done2 = update_big("ffn2", list(pieces))
    sums1, recv1 = _scatter_wait("ffn1", *flight1, done2[-1])
    totals1 = _total_sums("total_sums_ffn1", sums1, recv1, kc_arr)
    pieces.update(zip(ffn1, _run_comm("ffn1_pair_assemble", _assemble_comm(totals1))))
    done1 = update_big("ffn1", ffn1)
    late_arrays = list(late.values())
    late_arrays[0], _ = lax.optimization_barrier((late_arrays[0], (done1[-1], small["final_g"])))
    red, _ = _small_allreduce(late_arrays, None)
    small.update(zip(late, red))
    mc = D // N_CHIP
    small["meta_tokens"] = lax.dynamic_slice_in_dim(small["meta_tokens"], k_pos * mc, mc, 1)
    sums_rest, recv_rest = _scatter_wait("rest", *flight2, red[0])
    totals = _total_sums("total_sums_rest", sums_rest, recv_rest, kc_arr)
    gs = [small[n] for n in SMALL]
    (d2, m2, v2), done = _adamw_small([_as_matrix(n, given[n]) for n in SMALL], gs,
                                      [_as_matrix(n, given["m_" + n]) for n in SMALL],
                                      [_as_matrix(n, given["v_" + n]) for n in SMALL], _assemble_comm(totals))
    pieces.update(zip(mid, done))

    for n, g, dd, mm, vv in zip(SMALL, gs, d2, m2, v2):
        out_g[n], out_d[n], out_m[n], out_v[n] = (_from_matrix(n, t) for t in (g, dd, mm, vv))
    update_big("rest", mid)
    return (loss, grad_x, *[out_g[n] for n in WEIGHTS], *[out_d[n] for n in WEIGHTS],
            *[out_m[n] for n in WEIGHTS], *[out_v[n] for n in WEIGHTS])
```

```python
import math

import numpy as np
import jax
import jax.numpy as jnp
from jax import lax
from jax.experimental import pallas as pl
from jax.experimental.pallas import tpu as pltpu

F32 = jnp.float32
BF16 = jnp.bfloat16

D = 1024
N_META = 16
GRID_W = 64
NA_W = 512
S5_W = 512
HEAD_DIM = 64
N_HEADS = 8
KH = 8
KW = 16
S5_G = 32
S5_P = 64
S5_H = 16
N_BUNDLE = 4
FF = 2816
N_CHIP = 4
FC = FF // N_CHIP
EPS = 1e-6
NEG_INF = -1e30
Q_ROWS = 4
K_ROWS = 12
QB = Q_ROWS * GRID_W
KB = K_ROWS * GRID_W
SCAN_CHUNK = 256

ADAM_LR = 0.001
ADAM_B1 = 0.9
ADAM_B2 = 0.999
ADAM_EPS = 1e-08
ADAM_WD = 0.01
ADAM_STEP = 10

NT = (((1,), (1,)), ((), ()))
TN = (((0,), (0,)), ((), ()))
MESH_ID = pl.DeviceIdType.MESH


def _cp(sem=None, vmem_mb=None):
    kw = {}
    if sem is not None:
        kw["dimension_semantics"] = sem
    if vmem_mb is not None:
        kw["vmem_limit_bytes"] = vmem_mb << 20
    return pltpu.CompilerParams(**kw)


def _full(shape):
    n = len(shape)
    return pl.BlockSpec(shape, lambda *_: (0,) * n)


def _rows(tm, w):
    return pl.BlockSpec((tm, w), lambda i: (i, 0))


ANY = pl.BlockSpec(memory_space=pl.ANY)


def _rms(x, g):
    r = lax.rsqrt(jnp.mean(x * x, axis=-1, keepdims=True) + EPS)
    return x * r * g


def _rms_bwd(x, g, dy):
    r = lax.rsqrt(jnp.mean(x * x, axis=-1, keepdims=True) + EPS)
    xh = x * r
    dg = jnp.sum(dy * xh, axis=0, keepdims=True)
    dyg = dy * g
    dx = r * (dyg - xh * jnp.mean(dyg * xh, axis=-1, keepdims=True))
    return dx, dg


def _out(shape, dtype):
    return pltpu.HBM(tuple(shape), dtype)


def _in_hbm(*args):
    return [pltpu.with_memory_space_constraint(a, pltpu.HBM) if jnp.issubdtype(a.dtype, jnp.floating) and a.ndim > 1
            else a for a in args]


def _dot(a, b):
    return jnp.dot(a, b, preferred_element_type=F32)


def _dg(a, b, dims):
    return lax.dot_general(a, b, dims, preferred_element_type=F32)


def _ffn_fwd(name, h, g_pre, g_post, wg, wu, wd, tm, comm=None, bounds=()):
    tp = h.shape[0]
    nt = tp // tm

    def body(h_ref, gp_ref, gq_ref, wg_ref, wu_ref, wd_ref, hn_ref, gate_ref, up_ref, f_ref, xn_s, acc_s):
        c = pl.program_id(1)

        @pl.when(c == 0)
        def _():
            xn_s[...] = _rms(h_ref[...], gp_ref[...]).astype(BF16)
            acc_s[...] = jnp.zeros_like(acc_s)

        xn = xn_s[...]
        gate = _dg(xn, wg_ref[0], NT)
        up = _dg(xn, wu_ref[0], NT)
        gate_ref[0] = gate
        up_ref[0] = up
        act = (gate * jax.nn.sigmoid(gate) * up).astype(BF16)
        acc_s[...] += _dot(act, wd_ref[0])

        @pl.when(c == N_CHIP - 1)
        def _():
            f = acc_s[...]
            f_ref[...] = f
            hn_ref[...] = h_ref[...] + 0.5 * _rms(f, gq_ref[...])

    return _call(
        body, comm, bounds, (h, g_pre, g_post, wg, wu, wd), name=name, grid=(nt, N_CHIP),
        in_specs=[pl.BlockSpec((tm, D), lambda i, c: (i, 0)), _full((1, D)), _full((1, D))] +
                 [pl.BlockSpec((1, FC, D), lambda i, c: (c, 0, 0))] * 3,
        out_specs=[pl.BlockSpec((tm, D), lambda i, c: (i, 0)),
                   pl.BlockSpec((1, tm, FC), lambda i, c: (c, i, 0)),
                   pl.BlockSpec((1, tm, FC), lambda i, c: (c, i, 0)),
                   pl.BlockSpec((tm, D), lambda i, c: (i, 0))],
        out_shape=[_out((tp, D), F32), _out((N_CHIP, tp, FC), F32),
                   _out((N_CHIP, tp, FC), F32), _out((tp, D), F32)],
        scratch_shapes=[pltpu.VMEM((tm, D), BF16), pltpu.VMEM((tm, D), F32)],
        compiler_params=_cp(("arbitrary", "arbitrary"), 48))


def _ffn_bwd(name, h, g_pre, df, gate, up, wg, wu, wd, tm, comm=None, bounds=()):
    tp = h.shape[0]
    nt = tp // tm
    rh = FC // 2

    def body(h_ref, gp_ref, df_ref, gate_ref, up_ref, wg_ref, wu_ref, wd_ref,
             dwg_ref, dwu_ref, dwd_ref, dxn_ref, rg_ref, ru_ref, rd_ref, ag, au, ad, send_sems, recv_sems):
        c = pl.program_id(0)
        i = pl.program_id(1)

        def to_sibling(a, piece):
            x, y, core = _mesh_pos()
            dw_ref, r_ref = ((dwg_ref, rg_ref), (dwu_ref, ru_ref), (dwd_ref, rd_ref))[a]
            return _remote(dw_ref.at[piece, pl.ds((1 - core) * rh, rh), :], r_ref.at[piece], send_sems, recv_sems,
                           3 * piece + a, (x, y, 1 - core))

        @pl.when(i == 0)
        def _():
            ag[...] = jnp.zeros_like(ag)
            au[...] = jnp.zeros_like(au)
            ad[...] = jnp.zeros_like(ad)

        xn = _rms(h_ref[...], gp_ref[...]).astype(BF16)
        dfb = df_ref[...].astype(BF16)
        gt = gate_ref[0]
        u = up_ref[0]
        sg = jax.nn.sigmoid(gt)
        si = gt * sg
        act = (si * u).astype(BF16)
        dact = _dg(dfb, wd_ref[0], NT)
        ad[...] += _dg(act, dfb, TN)
        dgate = (dact * u * (sg * (1.0 + gt * (1.0 - sg)))).astype(BF16)
        ag[...] += _dg(dgate, xn, TN)
        dup = (dact * si).astype(BF16)
        au[...] += _dg(dup, xn, TN)
        dxn_ref[0] = _dot(dgate, wg_ref[0]) + _dot(dup, wu_ref[0])

        @pl.when(i == nt - 1)
        def _():
            pltpu.sync_copy(ag, dwg_ref.at[c])
            pltpu.sync_copy(au, dwu_ref.at[c])
            pltpu.sync_copy(ad, dwd_ref.at[c])
            for a in range(3):
                to_sibling(a, c).start()

        @pl.when((c == N_CHIP - 1) & (i == nt - 1))
        def _():
            for piece in range(N_CHIP):
                for a in range(3):
                    to_sibling(a, piece).wait()

    return _call(
        body, comm, bounds, (h, g_pre, df, gate, up, wg, wu, wd), name=name, grid=(N_CHIP, nt),
        in_specs=[pl.BlockSpec((tm, D), lambda c, i: (i, 0)), _full((1, D)),
                  pl.BlockSpec((tm, D), lambda c, i: (i, 0)),
                  pl.BlockSpec((1, tm, FC), lambda c, i: (c, i, 0)),
                  pl.BlockSpec((1, tm, FC), lambda c, i: (c, i, 0))] +
                 [pl.BlockSpec((1, FC, D), lambda c, i: (c, 0, 0))] * 3,
        out_specs=[ANY, ANY, ANY, pl.BlockSpec((1, tm, D), lambda c, i: (c, i, 0)), ANY, ANY, ANY],
        out_shape=[_out((N_CHIP, FC, D), F32)] * 3 + [_out((N_CHIP, tp, D), F32)] +
                  [_out((N_CHIP, rh, D), F32)] * 3,
        scratch_shapes=[pltpu.VMEM((FC, D), F32)] * 3 +
                       [pltpu.SemaphoreType.DMA((3 * N_CHIP,)), pltpu.SemaphoreType.DMA((3 * N_CHIP,))],
        compiler_params=_cp(("arbitrary", "arbitrary"), 58))


def _ffn_pre_bwd(name, dh, dxn_part, h, g_pre, tm, comm=None, bounds=()):
    tp = h.shape[0]
    nt = tp // tm

    def body(dh_ref, dxn_ref, h_ref, gp_ref, out_ref, dg_ref):
        i = pl.program_id(0)
        dxn = (dxn_ref[0] + dxn_ref[1]) + (dxn_ref[2] + dxn_ref[3])
        dx, dg = _rms_bwd(h_ref[...], gp_ref[...], dxn)
        out_ref[...] = dh_ref[...] + dx

        @pl.when(i == 0)
        def _():
            dg_ref[...] = jnp.zeros_like(dg_ref)

        dg_ref[...] += dg

    return _call(
        body, comm, bounds, (dh, dxn_part, h, g_pre), name=name, grid=(nt,),
        in_specs=[_rows(tm, D), pl.BlockSpec((N_CHIP, tm, D), lambda i: (0, i, 0)), _rows(tm, D), _full((1, D))],
        out_specs=[_rows(tm, D), _full((1, D))],
        out_shape=[_out((tp, D), F32), _out((1, D), F32)],
        compiler_params=_cp(("arbitrary",), 48))


def _mix_in(h, g, w_in, tm):
    tp = h.shape[0]

    def body(h_ref, g_ref, w_ref, q_ref, k_ref, v_ref, u_ref):
        a = _rms(h_ref[...], g_ref[...]).astype(BF16)
        q_ref[...] = _dot(a, w_ref[0]).astype(BF16)
        k_ref[...] = _dot(a, w_ref[1]).astype(BF16)
        v_ref[...] = _dot(a, w_ref[2]).astype(BF16)
        u_ref[...] = _dot(a, w_ref[3])

    return pl.pallas_call(
        body, name="mix_in", grid=(tp // tm,),
        in_specs=[_rows(tm, D), _full((1, D)), _full((N_CHIP, D, NA_W))],
        out_specs=[_rows(tm, NA_W)] * 4,
        out_shape=[_out((tp, NA_W), BF16)] * 3 + [_out((tp, S5_W), F32)],
        compiler_params=_cp(("arbitrary",), 40),
    )(*_in_hbm(h, g, w_in))


def _gelu(x):
    return jax.nn.gelu(x, approximate=True)


def _gelu_grad(x):
    k = math.sqrt(2.0 / math.pi)
    t = jnp.tanh(k * (x + 0.044715 * x * x * x))
    return 0.5 * (1.0 + t) + 0.5 * x * (1.0 - t * t) * k * (1.0 + 3.0 * 0.044715 * x * x)


def _mix_out(o_na, y_pre, h, w_glu, b_glu, g_na, g_s5, w_out, g_post, tm, comm=None, bounds=()):
    tp = h.shape[0]

    def body(ona_ref, yp_ref, h_ref, wglu_ref, bglu_ref, gna_ref, gs5_ref, wout_ref, gpost_ref, hn_ref, mix_ref):
        y = _gelu(yp_ref[...])
        z = _dot(y.astype(BF16), wglu_ref[...]) + bglu_ref[...]
        o_s5 = y * jax.nn.sigmoid(z)
        n1 = _rms(ona_ref[...], gna_ref[...]).astype(BF16)
        n2 = _rms(o_s5, gs5_ref[...]).astype(BF16)
        mix = _dot(n1, wout_ref[0:NA_W, :]) + _dot(n2, wout_ref[NA_W:, :])
        mix_ref[...] = mix
        hn_ref[...] = h_ref[...] + _rms(mix, gpost_ref[...])

    return _call(
        body, comm, bounds, (o_na, y_pre, h, w_glu, b_glu, g_na, g_s5, w_out, g_post), name="mix_out",
        grid=(tp // tm,),
        in_specs=[_rows(tm, NA_W), _rows(tm, S5_W), _rows(tm, D), _full((S5_W, S5_W)), _full((1, S5_W)),
                  _full((1, NA_W)), _full((1, S5_W)), _full((D, D)), _full((1, D))],
        out_specs=[_rows(tm, D), _rows(tm, D)],
        out_shape=[_out((tp, D), F32)] * 2,
        compiler_params=_cp(("arbitrary",), 40))


def _mix_out_bwd(dh, mix, o_na, y_pre, w_glu, b_glu, g_na, g_s5, w_out, g_post, tm):
    tp = dh.shape[0]
    nt = tp // tm

    def body(dh_ref, mix_ref, ona_ref, yp_ref, wglu_ref, bglu_ref, gna_ref, gs5_ref, wout_ref, gpost_ref,
             dona_ref, dyp_ref, dwout_ref, dwglu_ref, dgpost_ref, dgna_ref, dgs5_ref, dbglu_ref, a_out, a_glu):
        i = pl.program_id(0)

        @pl.when(i == 0)
        def _():
            a_out[...] = jnp.zeros_like(a_out)
            a_glu[...] = jnp.zeros_like(a_glu)
            dgpost_ref[...] = jnp.zeros_like(dgpost_ref)
            dgna_ref[...] = jnp.zeros_like(dgna_ref)
            dgs5_ref[...] = jnp.zeros_like(dgs5_ref)
            dbglu_ref[...] = jnp.zeros_like(dbglu_ref)

        dmix, dgpost = _rms_bwd(mix_ref[...], gpost_ref[...], dh_ref[...])
        dgpost_ref[...] += dgpost
        yp = yp_ref[...]
        y = _gelu(yp)
        yb = y.astype(BF16)
        z = _dot(yb, wglu_ref[...]) + bglu_ref[...]
        sg = jax.nn.sigmoid(z)
        o_s5 = y * sg
        o_na = ona_ref[...]
        n1 = _rms(o_na, gna_ref[...]).astype(BF16)
        n2 = _rms(o_s5, gs5_ref[...]).astype(BF16)
        dmb = dmix.astype(BF16)
        a_out[0:NA_W, :] += _dg(n1, dmb, TN)
        a_out[NA_W:, :] += _dg(n2, dmb, TN)
        dn1 = _dg(dmb, wout_ref[0:NA_W, :], NT)
        dn2 = _dg(dmb, wout_ref[NA_W:, :], NT)
        dona, dgna = _rms_bwd(o_na, gna_ref[...], dn1)
        dona_ref[...] = dona
        dgna_ref[...] += dgna
        dos5, dgs5 = _rms_bwd(o_s5, gs5_ref[...], dn2)
        dgs5_ref[...] += dgs5
        dz = dos5 * y * (sg * (1.0 - sg))
        dbglu_ref[...] += jnp.sum(dz, axis=0, keepdims=True)
        dzb = dz.astype(BF16)
        a_glu[...] += _dg(yb, dzb, TN)
        dy = dos5 * sg + _dg(dzb, wglu_ref[...], NT)
        dyp_ref[...] = dy * _gelu_grad(yp)

        @pl.when(i == nt - 1)
        def _():
            pltpu.sync_copy(a_out, dwout_ref)
            pltpu.sync_copy(a_glu, dwglu_ref)

    return pl.pallas_call(
        body, name="mix_out_bwd", grid=(nt,),
        in_specs=[_rows(tm, D), _rows(tm, D), _rows(tm, NA_W), _rows(tm, S5_W), _full((S5_W, S5_W)),
                  _full((1, S5_W)), _full((1, NA_W)), _full((1, S5_W)), _full((D, D)), _full((1, D))],
        out_specs=[_rows(tm, NA_W), _rows(tm, S5_W), ANY, ANY, _full((1, D)), _full((1, NA_W)),
                   _full((1, S5_W)), _full((1, S5_W))],
        out_shape=[_out((tp, NA_W), F32), _out((tp, S5_W), F32),
                   _out((D, D), F32), _out((S5_W, S5_W), F32),
                   _out((1, D), F32), _out((1, NA_W), F32),
                   _out((1, S5_W), F32), _out((1, S5_W), F32)],
        scratch_shapes=[pltpu.VMEM((D, D), F32), pltpu.VMEM((S5_W, S5_W), F32)],
        compiler_params=_cp(("arbitrary",), 48),
    )(*_in_hbm(dh, mix, o_na, y_pre, w_glu, b_glu, g_na, g_s5, w_out, g_post))


def _mix_in_bwd(dq, dk, dv, du, h, g, w_in, dh, f1, g_post1, tm, comm=None, bounds=()):
    tp = h.shape[0]
    nt = tp // tm

    def body(dq_ref, dk_ref, dv_ref, du_ref, h_ref, g_ref, w_ref, dh_ref, f_ref, gq_ref,
             dh1_ref, df_ref, dw_ref, dg_ref, dgq_ref, acc):
        i = pl.program_id(0)

        @pl.when(i == 0)
        def _():
            acc[...] = jnp.zeros_like(acc)
            dg_ref[...] = jnp.zeros_like(dg_ref)
            dgq_ref[...] = jnp.zeros_like(dgq_ref)

        x = h_ref[...]
        a = _rms(x, g_ref[...]).astype(BF16)
        da = jnp.zeros((tm, D), F32)
        for j, r in enumerate((dq_ref, dk_ref, dv_ref, du_ref)):
            dp = r[...].astype(BF16)
            da = da + _dg(dp, w_ref[j], NT)
            acc[j] += _dg(a, dp, TN)
        dx, dg = _rms_bwd(x, g_ref[...], da)
        dh1 = dh_ref[...] + dx
        dh1_ref[...] = dh1
        dg_ref[...] += dg
        df, dgq = _rms_bwd(f_ref[...], gq_ref[...], 0.5 * dh1)
        df_ref[...] = df
        dgq_ref[...] += dgq

        @pl.when(i == nt - 1)
        def _():
            pltpu.sync_copy(acc, dw_ref)

    return _call(
        body, comm, bounds, (dq, dk, dv, du, h, g, w_in, dh, f1, g_post1), name="mix_in_bwd", grid=(nt,),
        in_specs=[_rows(tm, NA_W)] * 4 + [_rows(tm, D), _full((1, D)), _full((N_CHIP, D, NA_W)), _rows(tm, D),
                                         _rows(tm, D), _full((1, D))],
        out_specs=[_rows(tm, D), _rows(tm, D), ANY, _full((1, D)), _full((1, D))],
        out_shape=[_out((tp, D), F32), _out((tp, D), F32),
                   _out((N_CHIP, D, NA_W), F32), _out((1, D), F32),
                   _out((1, D), F32)],
        scratch_shapes=[pltpu.VMEM((N_CHIP, D, NA_W), F32)],
        compiler_params=_cp(("arbitrary",), 48))


def _final_loss(h, g_final, target, f2, g_post2, n_tok, tm):
    tp = h.shape[0]

    def body(h_ref, g_ref, t_ref, f_ref, gq_ref, dh_ref, df_ref, loss_ref, dg_ref, dgq_ref):
        i = pl.program_id(0)

        @pl.when(i == 0)
        def _():
            loss_ref[...] = jnp.zeros_like(loss_ref)
            dg_ref[...] = jnp.zeros_like(dg_ref)
            dgq_ref[...] = jnp.zeros_like(dgq_ref)

        x = h_ref[...]
        y = _rms(x, g_ref[...])
        row = i * tm + lax.broadcasted_iota(jnp.int32, (tm, 1), 0)
        valid = (row >= N_META) & (row < N_META + n_tok)
        e = jnp.where(valid, y - t_ref[...], 0.0)
        loss_ref[...] += 0.5 * jnp.sum(jnp.mean(e * e, axis=-1, keepdims=True), axis=0, keepdims=True)
        dx, dg = _rms_bwd(x, g_ref[...], e * (1.0 / D))
        dh_ref[...] = dx
        dg_ref[...] += dg
        df, dgq = _rms_bwd(f_ref[...], gq_ref[...], 0.5 * dx)
        df_ref[...] = df
        dgq_ref[...] += dgq

    return pl.pallas_call(
        body, name="final_loss", grid=(tp // tm,),
        in_specs=[_rows(tm, D), _full((1, D)), _rows(tm, D), _rows(tm, D), _full((1, D))],
        out_specs=[_rows(tm, D), _rows(tm, D), _full((1, 1)), _full((1, D)), _full((1, D))],
        out_shape=[_out((tp, D), F32), _out((tp, D), F32),
                   _out((1, 1), F32), _out((1, D), F32),
                   _out((1, D), F32)],
        compiler_params=_cp(("arbitrary",), 40),
    )(*_in_hbm(h, g_final, target, f2, g_post2))


def _na_patterns(n_rows):
    pats = []
    for kind in range(3):
        pat = [[-1] * K_ROWS for _ in range(Q_ROWS)]
        for i in range(Q_ROWS):
            for jj in range(K_ROWS):
                if kind == 0 and jj < KH:
                    pat[i][jj] = jj - i + KH - 1
                elif kind == 1 and i <= jj < i + KH:
                    pat[i][jj] = jj - i + 3
                elif kind == 2 and K_ROWS - KH <= jj:
                    pat[i][jj] = jj - i - 1
        pats.append(pat)
    return pats


def _diag_onehot():
    q = np.arange(GRID_W)[:, None]
    kc = np.arange(GRID_W)[None, :]
    start = np.clip(q - KW // 2, 0, GRID_W - KW)
    col_in = (kc >= start) & (kc < start + KW)
    e = np.zeros((32, GRID_W, GRID_W), np.float32)
    for d in range(2 * KW - 1):
        e[d] = ((kc - q + KW - 1) == d) & col_in
    return e.reshape(32, GRID_W * GRID_W), col_in


def _rpb_collapse(dtb2, et):
    def body(d_ref, e_ref, o_ref):
        o_ref[...] = jnp.dot(d_ref[...], e_ref[...], preferred_element_type=F32, precision=lax.Precision.HIGHEST)

    out = (dtb2.shape[0], et.shape[1])
    return pl.pallas_call(
        body, name="rpb_collapse", grid=(1,), out_shape=_out(out, F32),
        in_specs=[_full(dtb2.shape), _full(et.shape)], out_specs=_full(out),
    )(*_in_hbm(dtb2, et))


def _bias_tables(rpb, n_rows, comm=None, bounds=()):
    n_dr, n_dc = 2 * KH - 1, 2 * KW - 1
    pats = _na_patterns(n_rows)

    def body(rpb_ref, o_ref):
        h = pl.program_id(0)
        q = lax.broadcasted_iota(jnp.int32, (GRID_W, GRID_W), 0)
        kc = lax.broadcasted_iota(jnp.int32, (GRID_W, GRID_W), 1)
        start = jnp.clip(q - KW // 2, 0, GRID_W - KW)
        col_in = (kc >= start) & (kc < start + KW)
        diff = kc - q + (KW - 1)
        neg = jnp.full((GRID_W, GRID_W), NEG_INF, F32)
        band = []
        for dr in range(n_dr):
            acc = neg
            for d in range(n_dc):
                acc = jnp.where((diff == d) & col_in, rpb_ref[(h * n_dr + dr) * n_dc + d], acc)
            band.append(acc)
        for kind, pat in enumerate(pats):
            for i in range(Q_ROWS):
                for jj in range(K_ROWS):
                    o_ref[kind, 0, i * GRID_W:(i + 1) * GRID_W, jj * GRID_W:(jj + 1) * GRID_W] = (
                        band[pat[i][jj]] if pat[i][jj] >= 0 else neg)

    (bias,), got = _call(
        body, comm, bounds, (rpb.reshape(-1),), name="bias_tables", grid=(N_HEADS,),
        in_specs=[pl.BlockSpec(memory_space=pltpu.SMEM)],
        out_specs=[pl.BlockSpec((3, 1, QB, KB), lambda h: (0, h, 0, 0))],
        out_shape=[_out((3, N_HEADS, QB, KB), F32)],
        compiler_params=_cp(("arbitrary",), 32))
    return bias, got


def _attn_geometry(n_tok):
    n_rows = n_tok // GRID_W
    assert n_rows % Q_ROWS == 0 and n_rows >= K_ROWS
    return n_rows, n_rows // Q_ROWS


def _attn_probs(qh, kh, kmh, bias, scale):
    s = _dg(qh, kh, NT) * scale + bias
    sm = _dg(qh, kmh, NT) * scale
    m = jnp.maximum(jnp.max(s, axis=-1, keepdims=True), jnp.max(sm, axis=-1, keepdims=True))
    p = jnp.exp(s - m)
    pm = jnp.exp(sm - m)
    inv = 1.0 / (jnp.sum(p, axis=-1, keepdims=True) + jnp.sum(pm, axis=-1, keepdims=True))
    return p * inv, pm * inv


def _meta_probs(qmh, kmh, scale):
    s = _dg(qmh, kmh, NT) * scale
    p = jnp.exp(s - jnp.max(s, axis=-1, keepdims=True))
    return p / jnp.sum(p, axis=-1, keepdims=True)


def _step_rows(r, n_rows):
    q0 = pl.multiple_of(N_META + r * QB, 16)
    k0 = pl.multiple_of(N_META + jnp.clip(Q_ROWS * r - (K_ROWS - KH), 0, n_rows - K_ROWS) * GRID_W, 16)
    return q0, k0


def _attn_fwd(q, k, v, bias, n_tok, comm=None, bounds=()):
    tp = q.shape[0]
    n_rows, n_steps = _attn_geometry(n_tok)
    scale = HEAD_DIM ** -0.5

    def body(q_ref, k_ref, v_ref, b_ref, o_ref):
        r = pl.program_id(1)
        km = k_ref[0:N_META, :]
        vm = v_ref[0:N_META, :]

        @pl.when(r == 0)
        def _():
            qm = q_ref[0:N_META, :]
            outs = []
            for hh in range(2):
                sl = slice(hh * HEAD_DIM, (hh + 1) * HEAD_DIM)
                p = _meta_probs(qm[:, sl], km[:, sl], scale)
                outs.append(_dot(p.astype(BF16), vm[:, sl]))
            o_ref[0:N_META, :] = jnp.concatenate(outs, axis=1)
            o_ref[N_META + n_tok:, :] = jnp.zeros((tp - N_META - n_tok, 2 * HEAD_DIM), F32)

        q0, k0 = _step_rows(r, n_rows)
        qb = q_ref[pl.ds(q0, QB), :]
        kb = k_ref[pl.ds(k0, KB), :]
        vb = v_ref[pl.ds(k0, KB), :]
        outs = []
        for hh in range(2):
            sl = slice(hh * HEAD_DIM, (hh + 1) * HEAD_DIM)
            p, pm = _attn_probs(qb[:, sl], kb[:, sl], km[:, sl], b_ref[0, hh], scale)
            outs.append(_dot(p.astype(BF16), vb[:, sl]) + _dot(pm.astype(BF16), vm[:, sl]))
        o_ref[pl.ds(q0, QB), :] = jnp.concatenate(outs, axis=1)

    def bias_map(hp, r):
        return (jnp.where(r == 0, 0, jnp.where(r == n_steps - 1, 2, 1)), hp, 0, 0)

    col = pl.BlockSpec((tp, 2 * HEAD_DIM), lambda hp, r: (0, hp))
    return _call(
        body, comm, bounds, (q, k, v, bias), name="attn_fwd", grid=(N_HEADS // 2, n_steps),
        in_specs=[col, col, col, pl.BlockSpec((1, 2, QB, KB), bias_map)],
        out_specs=[col], out_shape=[_out((tp, NA_W), F32)],
        compiler_params=_cp(("arbitrary", "arbitrary"), 40))


def _attn_bwd(q, k, v, bias, do, n_tok, comm=None, bounds=()):
    tp = q.shape[0]
    n_rows, n_steps = _attn_geometry(n_tok)
    scale = HEAD_DIM ** -0.5
    pats = _na_patterns(n_rows)

    def body(q_ref, k_ref, v_ref, b_ref, do_ref, dq_ref, dk_ref, dv_ref, dtb_ref):
        r = pl.program_id(1)
        km = k_ref[0:N_META, :]
        vm = v_ref[0:N_META, :]

        @pl.when(r == 0)
        def _():
            dk_ref[...] = jnp.zeros_like(dk_ref)
            dv_ref[...] = jnp.zeros_like(dv_ref)
            dtb_ref[...] = jnp.zeros_like(dtb_ref)
            dq_ref[N_META + n_tok:, :] = jnp.zeros((tp - N_META - n_tok, 2 * HEAD_DIM), F32)
            qm = q_ref[0:N_META, :]
            dom = do_ref[0:N_META, :].astype(BF16)
            dqs, dks, dvs = [], [], []
            for hh in range(2):
                sl = slice(hh * HEAD_DIM, (hh + 1) * HEAD_DIM)
                p = _meta_probs(qm[:, sl], km[:, sl], scale)
                dp = _dg(dom[:, sl], vm[:, sl], NT)
                ds = (p * (dp - jnp.sum(dp * p, axis=-1, keepdims=True))).astype(BF16)
                dvs.append(_dg(p.astype(BF16), dom[:, sl], TN))
                dqs.append(_dot(ds, km[:, sl]) * scale)
                dks.append(_dg(ds, qm[:, sl], TN) * scale)
            dq_ref[0:N_META, :] = jnp.concatenate(dqs, axis=1)
            dk_ref[0:N_META, :] += jnp.concatenate(dks, axis=1)
            dv_ref[0:N_META, :] += jnp.concatenate(dvs, axis=1)

        q0, k0 = _step_rows(r, n_rows)
        qb = q_ref[pl.ds(q0, QB), :]
        kb = k_ref[pl.ds(k0, KB), :]
        vb = v_ref[pl.ds(k0, KB), :]
        dob = do_ref[pl.ds(q0, QB), :].astype(BF16)
        dqs, dks, dvs, dkms, dvms, dss = [], [], [], [], [], []
        for hh in range(2):
            sl = slice(hh * HEAD_DIM, (hh + 1) * HEAD_DIM)
            qh, kh, vh, kmh, vmh, doh = qb[:, sl], kb[:, sl], vb[:, sl], km[:, sl], vm[:, sl], dob[:, sl]
            p, pm = _attn_probs(qh, kh, kmh, b_ref[0, hh], scale)
            dp = _dg(doh, vh, NT)
            dpm = _dg(doh, vmh, NT)
            delta = jnp.sum(dp * p, axis=-1, keepdims=True) + jnp.sum(dpm * pm, axis=-1, keepdims=True)
            ds = p * (dp - delta)
            dsb = ds.astype(BF16)
            dsmb = (pm * (dpm - delta)).astype(BF16)
            dss.append(ds)
            dvs.append(_dg(p.astype(BF16), doh, TN))
            dvms.append(_dg(pm.astype(BF16), doh, TN))
            dqs.append((_dot(dsb, kh) + _dot(dsmb, kmh)) * scale)
            dks.append(_dg(dsb, qh, TN) * scale)
            dkms.append(_dg(dsmb, qh, TN) * scale)
        dq_ref[pl.ds(q0, QB), :] = jnp.concatenate(dqs, axis=1)
        dk_ref[pl.ds(k0, KB), :] += jnp.concatenate(dks, axis=1)
        dv_ref[pl.ds(k0, KB), :] += jnp.concatenate(dvs, axis=1)
        dk_ref[0:N_META, :] += jnp.concatenate(dkms, axis=1)
        dv_ref[0:N_META, :] += jnp.concatenate(dvms, axis=1)

        def add_bias_grad(pat):
            for hh in range(2):
                for i in range(Q_ROWS):
                    for jj in range(K_ROWS):
                        if pat[i][jj] >= 0:
                            dtb_ref[hh, pat[i][jj]] += dss[hh][i * GRID_W:(i + 1) * GRID_W,
                                                               jj * GRID_W:(jj + 1) * GRID_W]

        @pl.when(r == 0)
        def _():
            add_bias_grad(pats[0])

        @pl.when((r > 0) & (r < n_steps - 1))
        def _():
            add_bias_grad(pats[1])

        @pl.when(r == n_steps - 1)
        def _():
            add_bias_grad(pats[2])

    def bias_map(hp, r):
        return (jnp.where(r == 0, 0, jnp.where(r == n_steps - 1, 2, 1)), hp, 0, 0)

    col = pl.BlockSpec((tp, 2 * HEAD_DIM), lambda hp, r: (0, hp))
    n_dr = 2 * KH - 1
    return _call(
        body, comm, bounds, (q, k, v, bias, do), name="attn_bwd", grid=(N_HEADS // 2, n_steps),
        in_specs=[col, col, col, pl.BlockSpec((1, 2, QB, KB), bias_map), col],
        out_specs=[col, col, col, pl.BlockSpec((2, n_dr, GRID_W, GRID_W), lambda hp, r: (hp, 0, 0, 0))],
        out_shape=[_out((tp, NA_W), F32)] * 3 +
                  [_out((N_HEADS, n_dr, GRID_W, GRID_W), F32)],
        compiler_params=_cp(("arbitrary", "arbitrary"), 48))


def _repeat_onehot():
    return np.repeat(np.eye(2 * S5_G, dtype=np.float32), S5_H, axis=0)


def _s5_disc_math(lam_re, lam_im, log_dt, b_re, b_im, rep):
    dt = jnp.exp(log_dt)
    ea = jnp.exp(lam_re * dt)
    a_re = ea * jnp.cos(lam_im * dt)
    a_im = ea * jnp.sin(lam_im * dt)
    den = lam_re * lam_re + lam_im * lam_im
    c_re = ((a_re - 1.0) * lam_re + a_im * lam_im) / den
    c_im = (a_im * lam_re - (a_re - 1.0) * lam_im) / den
    ce_re = jnp.dot(rep, c_re, preferred_element_type=F32, precision=lax.Precision.HIGHEST)
    ce_im = jnp.dot(rep, c_im, preferred_element_type=F32, precision=lax.Precision.HIGHEST)
    return a_re, a_im, ce_re * b_re - ce_im * b_im, ce_re * b_im + ce_im * b_re


def _s5_blocks():
    gl = S5_G // N_BUNDLE
    half = gl * S5_P
    out = []
    for d in range(2):
        for g in range(S5_G):
            b, k = divmod(g, gl)
            dg = d * S5_G + g
            out.append((d, b, slice(k * S5_H, (k + 1) * S5_H), slice(k * S5_P, (k + 1) * S5_P),
                        slice(half + k * S5_P, half + (k + 1) * S5_P), slice(dg * S5_H, (dg + 1) * S5_H),
                        slice(dg, dg + 1)))
    return out


def _s5_params(lam_re, lam_im, log_dt, b_re, b_im, c_re, c_im):
    cw, sw = S5_W // N_BUNDLE, 2 * (S5_G // N_BUNDLE) * S5_P

    def body(lr, li, ld, br, bi, cr, ci, rep_ref, a1_ref, a2_ref, bm_ref, cm_ref):
        a_re, a_im, bb_re, bb_im = _s5_disc_math(lr[...], li[...], ld[...], br[...], bi[...], rep_ref[...])
        cc_re = cr[...]
        cc_im = ci[...]
        bm_ref[...] = jnp.zeros_like(bm_ref)
        cm_ref[...] = jnp.zeros_like(cm_ref)
        for d, b, rows, re, im, nat, one in _s5_blocks():
            bm_ref[d, b, rows, re] = bb_re[nat, :].astype(BF16)
            bm_ref[d, b, rows, im] = bb_im[nat, :].astype(BF16)
            cm_ref[d, b, rows, re] = cc_re[nat, :].astype(BF16)
            cm_ref[d, b, rows, im] = (-cc_im[nat, :]).astype(BF16)
            k = rows.start // S5_H
            lanes = slice((k % 2) * S5_P, (k % 2 + 1) * S5_P)
            for part, (v1, v2) in enumerate(((a_re[one, :], a_im[one, :]), (a_re[one, :], -a_im[one, :]))):
                sub = slice(4 * part + k // 2, 4 * part + k // 2 + 1)
                a1_ref[d, b, sub, lanes] = v1
                a2_ref[d, b, sub, lanes] = v2

    args = (lam_re, lam_im, log_dt, b_re, b_im, c_re, c_im, jnp.asarray(_repeat_onehot()))
    outs = [((2, N_BUNDLE, 8, 128), F32)] * 2 + [((2, N_BUNDLE, cw, sw), BF16)] * 2
    return pl.pallas_call(
        body, name="s5_params", grid=(1,), in_specs=[_full(a.shape) for a in args],
        out_specs=[_full(s) for s, _ in outs], out_shape=[_out(s, dt) for s, dt in outs],
    )(*_in_hbm(*args))


def _s5_params_bwd(lam_re, lam_im, log_dt, b_re, b_im, da, dbm, dcm):
    n, nb = 2 * S5_G, 2 * S5_G * S5_H

    def body(lr, li, ld, br, bi, rep_ref, da_ref, dbm_ref, dcm_ref, o_lr, o_li, o_ld, o_br, o_bi, o_cr, o_ci,
             dar_s, dai_s, dbr_s, dbi_s):
        for d, b, rows, re, im, nat, one in _s5_blocks():
            dbr_s[nat, :] = dbm_ref[d, b, rows, re]
            dbi_s[nat, :] = dbm_ref[d, b, rows, im]
            o_cr[nat, :] = dcm_ref[d, b, rows, re]
            o_ci[nat, :] = -dcm_ref[d, b, rows, im]
            dar_s[one, :] = da_ref[d, b, :, re]
            dai_s[one, :] = da_ref[d, b, :, im]
        rep = rep_ref[...]
        _, vjp = jax.vjp(lambda p, q, r, s, t: _s5_disc_math(p, q, r, s, t, rep),
                         lr[...], li[...], ld[...], br[...], bi[...])
        o_lr[...], o_li[...], o_ld[...], o_br[...], o_bi[...] = vjp((dar_s[...], dai_s[...], dbr_s[...], dbi_s[...]))

    args = (lam_re, lam_im, log_dt, b_re, b_im, jnp.asarray(_repeat_onehot()), da, dbm, dcm)
    outs = [(n, S5_P)] * 2 + [(n, 1)] + [(nb, S5_P)] * 4
    return pl.pallas_call(
        body, name="s5_params_bwd", grid=(1,), in_specs=[_full(a.shape) for a in args],
        out_specs=[_full(s) for s in outs], out_shape=[_out(s, F32) for s in outs],
        scratch_shapes=[pltpu.VMEM((n, S5_P), F32)] * 2 + [pltpu.VMEM((nb, S5_P), F32)] * 2,
    )(*_in_hbm(*args))


def _tiles_store(ref, base, val):
    for i in range(val.shape[0] // 8):
        for c in range(8):
            ref[pl.ds(base + (8 * i + c) * 8, 8), :] = val[8 * i:8 * i + 8, 128 * c:128 * (c + 1)]


def _tiles_load(ref, base, n):
    return jnp.concatenate(
        [jnp.concatenate([ref[pl.ds(base + (8 * i + c) * 8, 8), :] for c in range(8)], axis=1) for i in range(n // 8)],
        axis=0)


def _time_rows(base, t):
    return pl.ds(base + (t // 8) * 64 + t % 8, 8, stride=8)


def _scan(chains, n):
    xs = [c["x"] for c in chains]
    for k in range(n):
        for ci, c in enumerate(chains):
            t = n - 1 - k if c["reverse"] else k
            if c["prev"] is not None:
                c["prev"][_time_rows(c["prev_base"], t), :] = xs[ci]
            xs[ci] = c["a1"] * xs[ci] + pltpu.roll(c["a2"] * xs[ci], 4, axis=0) + c["src"][_time_rows(0, t), :]
            if c["dst"] is not None:
                c["dst"][_time_rows(0, t), :] = xs[ci]
    return xs


def _chain(x, a1, a2, src, dst=None, prev=None, prev_base=0, reverse=False):
    return dict(x=x, a1=a1, a2=a2, src=src, dst=dst, prev=prev, prev_base=prev_base, reverse=reverse)


def _s5_fwd(u, d_skip, a1, a2, bm, cm, length, comm=None, bounds=()):
    tp = u.shape[0]
    cw = S5_W // N_BUNDLE
    sw = bm.shape[-1]
    n_full, n_tail = divmod(length, SCAN_CHUNK)
    t_tail = n_full * SCAN_CHUNK

    nbs = N_BUNDLE

    def body(u_ref, d_ref, a1_ref, a2_ref, bm_ref, cm_ref, y_ref, bnd_ref, *scratch):
        y_ref[...] = u_ref[...] * d_ref[...]
        ins, xss = (scratch[0:nbs], scratch[nbs:2 * nbs]), (scratch[2 * nbs:3 * nbs], scratch[3 * nbs:])
        cols = [slice(b * cw, (b + 1) * cw) for b in range(nbs)]

        def keep(dr, chunk, xs):
            for b in range(nbs):
                bnd_ref[dr, b, chunk] = xs[b]

        def load(dr, t0, n):
            for b in range(nbs):
                _tiles_store(ins[dr][b], 0, _dot(u_ref[pl.ds(t0, n), cols[b]].astype(BF16), bm_ref[dr, b]))

        def chains(dr, xs):
            return [_chain(xs[b], a1_ref[dr, b], a2_ref[dr, b], ins[dr][b], dst=xss[dr][b], reverse=dr == 1)
                    for b in range(nbs)]

        def emit(dr, t0, n):
            for b in range(nbs):
                y_ref[pl.ds(t0, n), cols[b]] += _dg(_tiles_load(xss[dr][b], 0, n).astype(BF16), cm_ref[dr, b], NT)

        zero = (jnp.zeros((8, 128), F32),) * nbs
        xb = zero
        if n_tail:
            keep(1, n_full, xb)
            load(1, t_tail, n_tail)
            xb = tuple(_scan(chains(1, xb), n_tail))
            emit(1, t_tail, n_tail)

        def pair(i, carry):
            j = n_full - 1 - i
            t0s = (pl.multiple_of(i * SCAN_CHUNK, SCAN_CHUNK), pl.multiple_of(j * SCAN_CHUNK, SCAN_CHUNK))
            keep(0, i, carry[0])
            keep(1, j, carry[1])
            for dr in range(2):
                load(dr, t0s[dr], SCAN_CHUNK)
            out = _scan(chains(0, carry[0]) + chains(1, carry[1]), SCAN_CHUNK)
            for dr in range(2):
                emit(dr, t0s[dr], SCAN_CHUNK)
            return tuple(out[:nbs]), tuple(out[nbs:])

        xf, _ = lax.fori_loop(0, n_full, pair, (zero, xb))
        if n_tail:
            keep(0, n_full, xf)
            load(0, t_tail, n_tail)
            _scan(chains(0, xf), n_tail)
            emit(0, t_tail, n_tail)

    n_chunks = n_full + (1 if n_tail else 0)
    tile = pl.BlockSpec((2, nbs, 8, 128), lambda b: (0, b, 0, 0))
    return _call(
        body, comm, bounds, (u, d_skip, a1, a2, bm, cm), name="s5_fwd", grid=(N_BUNDLE // nbs,),
        in_specs=[pl.BlockSpec((tp, nbs * cw), lambda b: (0, b)), pl.BlockSpec((1, nbs * cw), lambda b: (0, b)),
                  tile, tile, pl.BlockSpec((2, nbs, cw, sw), lambda b: (0, b, 0, 0)),
                  pl.BlockSpec((2, nbs, cw, sw), lambda b: (0, b, 0, 0))],
        out_specs=[pl.BlockSpec((tp, nbs * cw), lambda b: (0, b)),
                   pl.BlockSpec((2, nbs, n_chunks, 8, 128), lambda b: (0, b, 0, 0, 0))],
        out_shape=[_out((tp, S5_W), F32), _out((2, N_BUNDLE, n_chunks, 8, 128), F32)],
        scratch_shapes=[pltpu.VMEM((SCAN_CHUNK * 8, 128), F32)] * (4 * nbs),
        compiler_params=_cp(("arbitrary",), 48))


def _s5_bwd(u, dy, d_skip, a1, a2, bm, cm, bnd, length):
    tp = u.shape[0]
    cw = S5_W // N_BUNDLE
    sw = bm.shape[-1]
    half = sw // 2
    n_full, n_tail = divmod(length, SCAN_CHUNK)
    t_tail = n_full * SCAN_CHUNK
    n_chunks = bnd.shape[2]
    nbs = 2

    def body(u_ref, dy_ref, d_ref, a1_ref, a2_ref, bm_ref, cm_ref, bnd_ref, du_ref, dd_ref, dbm_ref, dcm_ref,
             da_ref, *scratch):
        du_ref[...] = dy_ref[...] * d_ref[...]
        dd_ref[...] = jnp.sum(dy_ref[...] * u_ref[...], axis=0, keepdims=True)
        dbm_ref[...] = jnp.zeros_like(dbm_ref)
        dcm_ref[...] = jnp.zeros_like(dcm_ref)
        da_ref[...] = jnp.zeros_like(da_ref)
        bu_s, dx_s, g_s, xp_s, x_s = ([scratch[(k * 2 + dr) * nbs:(k * 2 + dr + 1) * nbs] for dr in range(2)]
                                      for k in range(5))
        cols = [slice(b * cw, (b + 1) * cw) for b in range(nbs)]

        def chains(dr, chunk, t0, n, gs):
            out = []
            for b in range(nbs):
                _tiles_store(bu_s[dr][b], 0, _dot(u_ref[pl.ds(t0, n), cols[b]].astype(BF16), bm_ref[dr, b]))
                _tiles_store(dx_s[dr][b], 0, _dot(dy_ref[pl.ds(t0, n), cols[b]].astype(BF16), cm_ref[dr, b]))
                out.append(_chain(bnd_ref[dr, b, chunk], a1_ref[dr, b], a2_ref[dr, b], bu_s[dr][b],
                                  dst=x_s[dr][b], prev=xp_s[dr][b], reverse=dr == 1))
                out.append(_chain(gs[b], a1_ref[dr, b], -a2_ref[dr, b], dx_s[dr][b], dst=g_s[dr][b], reverse=dr == 0))
            return out

        def emit(dr, t0, n):
            rows = pl.ds(t0, n)
            for b in range(nbs):
                ub = u_ref[rows, cols[b]].astype(BF16)
                dyb = dy_ref[rows, cols[b]].astype(BF16)
                g = _tiles_load(g_s[dr][b], 0, n)
                gb = g.astype(BF16)
                du_ref[rows, cols[b]] += _dg(gb, bm_ref[dr, b], NT)
                dbm_ref[dr, b] += _dg(ub, gb, TN)
                xp = _tiles_load(xp_s[dr][b], 0, n)
                xp_r, xp_i = xp[:, 0:half], xp[:, half:]
                g_r, g_i = g[:, 0:half], g[:, half:]
                dcm_ref[dr, b] += _dg(dyb, _tiles_load(x_s[dr][b], 0, n).astype(BF16), TN)
                da_ref[dr, b] += jnp.concatenate([jnp.sum(g_r * xp_r + g_i * xp_i, axis=0, keepdims=True),
                                                  jnp.sum(g_i * xp_r - g_r * xp_i, axis=0, keepdims=True)], axis=1)

        def adjoints(out):
            return tuple(out[1::2])

        zero = (jnp.zeros((8, 128), F32),) * nbs
        g0 = zero
        if n_tail:
            g0 = adjoints(_scan(chains(0, n_full, t_tail, n_tail, g0), n_tail))
            emit(0, t_tail, n_tail)

        def pair(i, carry):
            j = n_full - 1 - i
            t0 = (pl.multiple_of(j * SCAN_CHUNK, SCAN_CHUNK), pl.multiple_of(i * SCAN_CHUNK, SCAN_CHUNK))
            both = chains(0, j, t0[0], SCAN_CHUNK, carry[0]) + chains(1, i, t0[1], SCAN_CHUNK, carry[1])
            out = _scan(both, SCAN_CHUNK)
            emit(0, t0[0], SCAN_CHUNK)
            emit(1, t0[1], SCAN_CHUNK)
            return adjoints(out[:2 * nbs]), adjoints(out[2 * nbs:])

        _, g1 = lax.fori_loop(0, n_full, pair, (g0, zero))
        if n_tail:
            _scan(chains(1, n_full, t_tail, n_tail, g1), n_tail)
            emit(1, t_tail, n_tail)

    tile = pl.BlockSpec((2, nbs, 8, 128), lambda b: (0, b, 0, 0))
    wide = pl.BlockSpec((2, nbs, cw, sw), lambda b: (0, b, 0, 0))
    col = pl.BlockSpec((tp, nbs * cw), lambda b: (0, b))
    row = pl.BlockSpec((1, nbs * cw), lambda b: (0, b))
    arow = pl.BlockSpec((2, nbs, 1, sw), lambda b: (0, b, 0, 0))
    return pl.pallas_call(
        body, name="s5_bwd", grid=(N_BUNDLE // nbs,),
        in_specs=[col, col, row, tile, tile, wide, wide,
                  pl.BlockSpec((2, nbs, n_chunks, 8, 128), lambda b: (0, b, 0, 0, 0))],
        out_specs=[col, row, wide, wide, arow],
        out_shape=[_out((tp, S5_W), F32), _out((1, S5_W), F32),
                   _out((2, N_BUNDLE, cw, sw), F32), _out((2, N_BUNDLE, cw, sw), F32),
                   _out((2, N_BUNDLE, 1, sw), F32)],
        scratch_shapes=[pltpu.VMEM((SCAN_CHUNK * 8, 128), F32)] * (10 * nbs),
        compiler_params=_cp(("arbitrary",), 56),
    )(*_in_hbm(u, dy, d_skip, a1, a2, bm, cm, bnd))


def _row_tile(tp):
    return max(tm for tm in range(16, 449, 16) if tp % tm == 0)


def _step(x, target, bufs, gains, s5, rpb, c_arr, kc_arr, me_arr):
    n_tok = x.shape[0]
    first = ["ffn1_w_gate", "ffn1_w_up", "ffn1_w_down", "meta_tokens"]
    bias, got = _bias_tables(rpb, n_tok // GRID_W, _gather_comm([bufs[n] for n in first]), (0, N_HEADS - 1))
    w = dict(zip(first, got))
    meta = w["meta_tokens"].transpose(1, 0, 2).reshape(N_META, D)
    length = N_META + n_tok
    tp = length + 16
    tm = _row_tile(tp)
    tmb = tm
    n_rows = n_tok // GRID_W
    pad = jnp.zeros((tp - length, D), F32)
    h0 = jnp.concatenate([meta, x, pad], axis=0)
    tgt = jnp.concatenate([jnp.zeros((N_META, D), F32), target, pad], axis=0)

    lam_re, _ = lax.optimization_barrier((s5["lam_re"], bias))
    s5p = (lam_re, s5["lam_im"], s5["log_dt"].reshape(2 * S5_G, 1), s5["b_re"], s5["b_im"])
    a1_m, a2_m, bm16, cm16 = _s5_params(*s5p, s5["c_re"], s5["c_im"])

    mid = ["w_in", "s5_w_glu", "w_out"]
    (h1, gate1, up1, f1), got = _ffn_fwd(
        "ffn1_fwd", h0, gains["ffn1_pre_g"], gains["ffn1_post_g"], w["ffn1_w_gate"], w["ffn1_w_up"], w["ffn1_w_down"],
        tm, _gather_comm([bufs[n] for n in mid]), (0, (tp // tm) * N_CHIP * 3 // 5))
    w.update(zip(mid, got))
    q, k, v, u = _mix_in(h1, gains["mix_pre_g"], w["w_in"], tm)
    (o_na,), (gate_ici, up_ici) = _attn_fwd(
        q, k, v, bias, n_tok, _gather_comm([bufs["ffn2_w_gate"], bufs["ffn2_w_up"]], pair=False), (0,))
    (y_pre, s5_bnd), (w["ffn2_w_gate"], w["ffn2_w_up"], down_ici) = _s5_fwd(
        u, gains["s5_d"], a1_m, a2_m, bm16, cm16, length,
        _merge_comm(_gather_comm([gate_ici, up_ici], ici=False),
                    _gather_comm([bufs["ffn2_w_down"]], pair=False)), (0,))
    w_glu = w["s5_w_glu"].reshape(S5_W, S5_W)
    w_out = w["w_out"].reshape(D, D)
    (h2, mix), (w["ffn2_w_down"],) = _mix_out(
        o_na, y_pre, h1, w_glu, gains["s5_b_glu"], gains["na_out_g"], gains["s5_out_g"], w_out, gains["mix_post_g"], tm,
        _gather_comm([down_ici], ici=False), (0,))
    (h3, gate2, up2, f2), _ = _ffn_fwd("ffn2_fwd", h2, gains["ffn2_pre_g"], gains["ffn2_post_g"],
                                       w["ffn2_w_gate"], w["ffn2_w_up"], w["ffn2_w_down"], tm)
    dh3, df2, loss, dg_final, dg_post2 = _final_loss(h3, gains["final_g"], tgt, f2, gains["ffn2_post_g"], n_tok, tm)

    ffn2 = ["ffn2_w_gate", "ffn2_w_up", "ffn2_w_down"]
    ffn1 = ["ffn1_w_gate", "ffn1_w_up", "ffn1_w_down"]
    out2, _ = _ffn_bwd("ffn2_bwd", h2, gains["ffn2_pre_g"], df2, gate2, up2,
                       w["ffn2_w_gate"], w["ffn2_w_up"], w["ffn2_w_down"], tmb)
    dxn2 = out2[3]
    sums2 = _chip_sums("chip_sums_ffn2", out2[0:3], out2[4:7], c_arr)
    (dh2, dg_pre2), _ = _ffn_pre_bwd("ffn2_pre_bwd", dh3, dxn2, h2, gains["ffn2_pre_g"], tm)
    do_na, dy_pre, dw_out, dw_glu, dg_mpost, dg_na, dg_s5, db_glu = _mix_out_bwd(
        dh2, mix, o_na, y_pre, w_glu, gains["s5_b_glu"], gains["na_out_g"], gains["s5_out_g"], w_out,
        gains["mix_post_g"], tm)
    (dq, dk, dv, dtb), recv3 = _attn_bwd(q, k, v, bias, do_na, n_tok, _scatter_comm(sums2), (0,))
    totals2 = _total_sums("total_sums_ffn2", sums2, recv3, kc_arr)
    du, dd, dbm, dcm, da_m = _s5_bwd(u, dy_pre, gains["s5_d"], a1_m, a2_m, bm16, cm16, s5_bnd, length)
    (dh1, df1, dw_in, dg_mpre, dg_post1), done2 = _mix_in_bwd(
        dq, dk, dv, du, h1, gains["mix_pre_g"], w["w_in"], dh2, f1, gains["ffn1_post_g"], tm,
        _assemble_comm(totals2), (0,))
    pieces = dict(zip(ffn2, done2))

    e, _ = _diag_onehot()
    n_dr = 2 * KH - 1
    drpb = _rpb_collapse(dtb.reshape(N_HEADS * n_dr, GRID_W * GRID_W), jnp.asarray(e.T))
    drpb = drpb[:, :2 * KW - 1].reshape(N_HEADS, n_dr, 2 * KW - 1).transpose(1, 0, 2).reshape(N_HEADS * n_dr, 2 * KW - 1)
    dlam_re, dlam_im, dlog_dt, db_re, db_im, dc_re, dc_im = _s5_params_bwd(*s5p, da_m, dbm, dcm)
    early = {"ffn1_post_g": dg_post1, "mix_pre_g": dg_mpre, "na_rpb": drpb,
             "s5_lam_re": dlam_re, "s5_lam_im": dlam_im, "s5_log_dt": dlog_dt.reshape(2, S5_G),
             "s5_b_re": db_re, "s5_b_im": db_im, "s5_c_re": dc_re, "s5_c_im": dc_im,
             "s5_d": dd, "s5_b_glu": db_glu, "na_out_g": dg_na,
             "s5_out_g": dg_s5, "mix_post_g": dg_mpost, "ffn2_pre_g": dg_pre2, "ffn2_post_g": dg_post2,
             "final_g": dg_final}
    names = list(early)
    slots = _small_pack([early[n] for n in names], me_arr)

    out1, slots = _ffn_bwd("ffn1_bwd", h0, gains["ffn1_pre_g"], df1, gate1, up1,
                           w["ffn1_w_gate"], w["ffn1_w_up"], w["ffn1_w_down"], tmb, _spread_comm(slots), (0,))
    small = dict(zip(names, _small_total(slots, [early[n].shape for n in names])))
    sums1 = _chip_sums("chip_sums_ffn1", out1[0:3], out1[4:7], c_arr)
    flight1 = _scatter_start("ffn1", sums1)
    token = flight1[4]
    rest = [dw_in, dw_glu.reshape(N_CHIP, S5_W // N_CHIP, S5_W), dw_out.reshape(N_CHIP, D // N_CHIP, D)]
    (dh0, dg_pre1), recv_rest = _ffn_pre_bwd("ffn1_pre_bwd", dh1, out1[3], h0, gains["ffn1_pre_g"] + token[0:1, 0:1],
                                             tm, _exchange_comm(rest), (0,))
    sums = _chip_sums("chip_sums_rest", rest, recv_rest, c_arr)
    flight2 = _scatter_start("rest", sums)
    return loss[0, 0], dh0, pieces, small, {"ffn1_pre_g": dg_pre1}, (ffn1, flight1[:4]), (mid, flight2[:4])


def _mesh_pos():
    return lax.axis_index("x"), lax.axis_index("y"), lax.axis_index("c")


def _other_chips(x, y):
    return [(1 - x, y), (x, 1 - y), (1 - x, 1 - y)]


class _Comm:
    def __init__(self, ins, out_shape, aliases, parts):
        self.ins, self.out_shape, self.aliases, self.parts = list(ins), list(out_shape), dict(aliases), list(parts)
        self.n_sems = sum(p[0] for p in parts)

    def bases(self):
        out, base = [], 0
        for n_sems, _, _ in self.parts:
            out.append(base)
            base += n_sems
        return out


def _run_comm(name, comm):
    n_i, n_o = len(comm.ins), len(comm.out_shape)

    def body(*refs):
        ins, outs = refs[:n_i], refs[n_i:n_i + n_o]
        send_sems, recv_sems = refs[n_i + n_o:]
        for base, (_, start, finish) in zip(comm.bases(), comm.parts):
            start(ins, outs, send_sems, recv_sems, base)
            finish(ins, outs, send_sems, recv_sems, base)

    return pl.pallas_call(
        body, name=name, out_shape=comm.out_shape, in_specs=[ANY] * n_i, out_specs=[ANY] * n_o,
        input_output_aliases=comm.aliases,
        scratch_shapes=[pltpu.SemaphoreType.DMA((comm.n_sems,)), pltpu.SemaphoreType.DMA((comm.n_sems,))],
    )(*_in_hbm(*comm.ins))


def _call(body, comm, bounds, args, *, name, grid, in_specs, out_specs, out_shape, scratch_shapes=(),
          compiler_params=None):
    in_specs, out_specs, out_shape, scratch_shapes = list(in_specs), list(out_specs), list(out_shape), list(scratch_shapes)
    if comm is None:
        return pl.pallas_call(body, name=name, grid=grid, in_specs=in_specs, out_specs=out_specs, out_shape=out_shape,
                              scratch_shapes=scratch_shapes, compiler_params=compiler_params)(*_in_hbm(*args)), []
    n_in, n_out, n_scr = len(in_specs), len(out_specs), len(scratch_shapes)
    n_ci, n_co = len(comm.ins), len(comm.out_shape)
    n_steps = int(np.prod(grid))
    assert len(bounds) == len(comm.parts) and all(0 <= b < n_steps for b in bounds) and list(bounds) == sorted(bounds)

    def fused(*refs):
        a = n_in
        b = a + n_ci
        c = b + n_out
        d = c + n_co
        e = d + n_scr
        cargs = (refs[a:b], refs[c:d], refs[e], refs[e + 1])
        step = pl.program_id(0)
        for ax in range(1, len(grid)):
            step = step * grid[ax] + pl.program_id(ax)
        bases = comm.bases()
        for p, (_, start, finish) in enumerate(comm.parts):
            @pl.when(step == bounds[p])
            def _(p=p, start=start):
                if p > 0:
                    comm.parts[p - 1][2](*cargs, bases[p - 1])
                start(*cargs, bases[p])
        body(*(refs[:a] + refs[b:c] + refs[d:e]))

        @pl.when(step == n_steps - 1)
        def _():
            comm.parts[-1][2](*cargs, bases[-1])

    res = pl.pallas_call(
        fused, name=name, grid=grid, in_specs=in_specs + [ANY] * n_ci, out_specs=out_specs + [ANY] * n_co,
        out_shape=out_shape + comm.out_shape,
        scratch_shapes=scratch_shapes + [pltpu.SemaphoreType.DMA((comm.n_sems,)), pltpu.SemaphoreType.DMA((comm.n_sems,))],
        input_output_aliases={n_in + i: n_out + j for i, j in comm.aliases.items()},
        compiler_params=compiler_params)(*_in_hbm(*args, *comm.ins))
    return res[:n_out], res[n_out:]


def _remote(src, dst, send_sems, recv_sems, idx, to):
    return pltpu.make_async_remote_copy(src_ref=src, dst_ref=dst, send_sem=send_sems.at[idx],
                                        recv_sem=recv_sems.at[idx], device_id=to, device_id_type=MESH_ID)


def _gather_comm(bufs, ici=True, pair=True):
    n = len(bufs)

    def half(ref, k, pc):
        rh = ref.shape[1] // 2
        return ref.at[k, pl.ds(pc * rh, rh), :]

    def ici_start(ins, outs, ss, rs, base):
        x, y, c = _mesh_pos()
        for a in range(n):
            mine = half(outs[a], 2 * x + y, c)
            for j, chip in enumerate(_other_chips(x, y)):
                _remote(mine, mine, ss, rs, base + 3 * a + j, (*chip, c)).start()

    def ici_finish(ins, outs, ss, rs, base):
        x, y, c = _mesh_pos()
        for a in range(n):
            for j, chip in enumerate(_other_chips(x, y)):
                theirs = half(outs[a], 2 * chip[0] + chip[1], c)
                _remote(theirs, theirs, ss, rs, base + 3 * a + j, (*chip, c)).wait()

    def pair_copy(outs, ss, rs, base, a):
        x, y, c = _mesh_pos()
        rh = outs[a].shape[1] // 2
        held = outs[a].at[:, pl.ds(c * rh, rh), :]
        return _remote(held, held, ss, rs, base + a, (x, y, 1 - c))

    def pair_start(ins, outs, ss, rs, base):
        for a in range(n):
            pair_copy(outs, ss, rs, base, a).start()

    def pair_finish(ins, outs, ss, rs, base):
        for a in range(n):
            pair_copy(outs, ss, rs, base, a).wait()

    parts = ([(3 * n, ici_start, ici_finish)] if ici else []) + ([(n, pair_start, pair_finish)] if pair else [])
    return _Comm(bufs, [_out(b.shape, b.dtype) for b in bufs], {a: a for a in range(n)}, parts)


def _merge_comm(*comms):
    ins, shapes, aliases, subs, base = [], [], {}, [], 0
    for cm in comms:
        (n_sems, start, finish), = cm.parts
        i0, o0 = len(ins), len(shapes)
        subs.append((slice(i0, i0 + len(cm.ins)), slice(o0, o0 + len(cm.out_shape)), base, start, finish))
        aliases.update({i0 + i: o0 + j for i, j in cm.aliases.items()})
        ins += cm.ins
        shapes += cm.out_shape
        base += n_sems

    def start_all(ins_r, outs_r, ss, rs, b):
        for si, so, off, start, _ in subs:
            start(ins_r[si], outs_r[so], ss, rs, b + off)

    def finish_all(ins_r, outs_r, ss, rs, b):
        for si, so, off, _, finish in subs:
            finish(ins_r[si], outs_r[so], ss, rs, b + off)

    return _Comm(ins, shapes, aliases, [(base, start_all, finish_all)])


def _own_half_buffers(pieces, dtypes, kc_arr):
    n = len(pieces)

    def body(kc_ref, *refs):
        for a in range(n):
            refs[n + a][0] = refs[a][...].astype(dtypes[a])

    def half(p):
        return p.shape[0] // 2, p.shape[1]

    return pl.pallas_call(
        body, name="own_halves",
        out_shape=[_out((N_CHIP,) + p.shape, dt) for p, dt in zip(pieces, dtypes)],
        grid_spec=pltpu.PrefetchScalarGridSpec(
            num_scalar_prefetch=1, grid=(1,),
            in_specs=[pl.BlockSpec(half(p), lambda i, kc: (kc[1], 0)) for p in pieces],
            out_specs=[pl.BlockSpec((1,) + half(p), lambda i, kc: (kc[0], kc[1], 0)) for p in pieces]),
        compiler_params=_cp(("arbitrary",), 48),
    )(kc_arr, *_in_hbm(*pieces))


def _exchange_comm(grads):
    n = len(grads)

    def copy(ins, outs, ss, rs, base, a):
        x, y, c = _mesh_pos()
        rh = ins[a].shape[1] // 2
        return _remote(ins[a].at[:, pl.ds((1 - c) * rh, rh), :], outs[a], ss, rs, base + a, (x, y, 1 - c))

    def start(ins, outs, ss, rs, base):
        for a in range(n):
            copy(ins, outs, ss, rs, base, a).start()

    def finish(ins, outs, ss, rs, base):
        for a in range(n):
            copy(ins, outs, ss, rs, base, a).wait()

    shapes = [_out((N_CHIP, g.shape[1] // 2, g.shape[2]), g.dtype) for g in grads]
    return _Comm(grads, shapes, {}, [(n, start, finish)])


def _chip_sums(name, grads, recvs, c_arr):
    n = len(grads)
    halves = [(1, g.shape[1] // 2, g.shape[2]) for g in grads]

    def body(c_ref, *refs):
        for a in range(n):
            refs[2 * n + a][...] = (refs[a][...] + refs[n + a][...]).astype(BF16)

    return pl.pallas_call(
        body, name=name, out_shape=[_out((N_CHIP,) + h[1:], BF16) for h in halves],
        grid_spec=pltpu.PrefetchScalarGridSpec(
            num_scalar_prefetch=1, grid=(N_CHIP,),
            in_specs=[pl.BlockSpec(h, lambda j, c_ref: (j, c_ref[0], 0)) for h in halves] +
                     [pl.BlockSpec(h, lambda j, c_ref: (j, 0, 0)) for h in halves],
            out_specs=[pl.BlockSpec(h, lambda j, c_ref: (j, 0, 0)) for h in halves]),
        compiler_params=_cp(("arbitrary",), 40),
    )(c_arr, *_in_hbm(*grads, *recvs))


def _scatter_comm(sums):
    n = len(sums)

    def copies(ins, outs, ss, rs, base):
        x, y, c = _mesh_pos()
        return [_remote(ins[a].at[2 * chip[0] + chip[1]], outs[a].at[j], ss, rs, base + 3 * a + j, (*chip, c))
                for a in range(n) for j, chip in enumerate(_other_chips(x, y))]

    def start(ins, outs, ss, rs, base):
        for cp in copies(ins, outs, ss, rs, base):
            cp.start()

    def finish(ins, outs, ss, rs, base):
        for cp in copies(ins, outs, ss, rs, base):
            cp.wait()

    shapes = [_out((3,) + s.shape[1:], s.dtype) for s in sums]
    return _Comm(sums, shapes, {}, [(3 * n, start, finish)])


def _scatter_copies(ins, lands, send_sems, recv_sems):
    x, y, c = _mesh_pos()
    return [_remote(ins[a].at[2 * chip[0] + chip[1]], lands[a].at[j], send_sems, recv_sems, 3 * a + j, (*chip, c))
            for a in range(len(ins)) for j, chip in enumerate(_other_chips(x, y))]


def _scatter_start(name, sums):
    n = len(sums)
    lands = [lax.empty((3,) + s.shape[1:], s.dtype) for s in sums]
    hbm = pl.BlockSpec(memory_space=pltpu.HBM)
    sem = pl.BlockSpec(memory_space=pltpu.SEMAPHORE)

    def body(*refs):
        ins, land_refs = refs[:n], refs[n:2 * n]
        send_sems, recv_sems = refs[2 * n], refs[2 * n + 1]
        token = refs[-1]
        for cp in _scatter_copies(ins, land_refs, send_sems, recv_sems):
            cp.start()
        token[...] = jnp.zeros_like(token)

    res = pl.pallas_call(
        body, name=name + "_scatter_start",
        out_shape=(pltpu.SemaphoreType.DMA((3 * n,)), pltpu.SemaphoreType.DMA((3 * n,)),
                   *[pltpu.HBM(s.shape, s.dtype) for s in sums], *[pltpu.HBM(ld.shape, ld.dtype) for ld in lands],
                   jax.ShapeDtypeStruct((8, 128), F32)),
        in_specs=[hbm] * (2 * n), out_specs=(sem, sem, *[hbm] * (2 * n), pl.BlockSpec(memory_space=pltpu.VMEM)),
        input_output_aliases={i: 2 + i for i in range(2 * n)},
        compiler_params=pltpu.CompilerParams(has_side_effects=pltpu.SideEffectType.DATAFLOW_SIDE_EFFECTING),
    )(*[pltpu.with_memory_space_constraint(a, pltpu.HBM) for a in list(sums) + lands])
    return res[0], res[1], list(res[2:2 + n]), list(res[2 + n:2 + 2 * n]), res[-1]


def _scatter_wait(name, send_sems, recv_sems, sums, lands, after):
    n = len(sums)
    hbm = pl.BlockSpec(memory_space=pltpu.HBM)
    sem = pl.BlockSpec(memory_space=pltpu.SEMAPHORE)

    def body(*refs):
        ins, land_refs = refs[:n], refs[n:2 * n]
        for cp in _scatter_copies(ins, land_refs, refs[2 * n], refs[2 * n + 1]):
            cp.wait_send()
            cp.wait_recv()

    res = pl.pallas_call(
        body, name=name + "_scatter_wait",
        out_shape=tuple([pltpu.HBM(s.shape, s.dtype) for s in sums] + [pltpu.HBM(ld.shape, ld.dtype) for ld in lands]),
        in_specs=[hbm] * (2 * n) + [sem, sem, pl.BlockSpec(memory_space=pl.ANY)], out_specs=tuple([hbm] * (2 * n)),
        input_output_aliases={i: i for i in range(2 * n)},
        compiler_params=pltpu.CompilerParams(has_side_effects=pltpu.SideEffectType.DATAFLOW_SIDE_EFFECTING),
    )(*sums, *lands, send_sems, recv_sems, after)
    return list(res[:n]), list(res[n:])


def _total_sums(name, sums, recv3, kc_arr):
    n = len(sums)
    dims = [s.shape[1:] for s in sums]

    def body(kc_ref, *refs):
        for a in range(n):
            s_ref, r_ref = refs[a], refs[n + a]
            t = s_ref[0].astype(F32) + r_ref[0].astype(F32)
            t = t + r_ref[1].astype(F32)
            refs[2 * n + a][...] = t + r_ref[2].astype(F32)

    return pl.pallas_call(
        body, name=name, out_shape=[_out((2 * rh, cc), F32) for rh, cc in dims],
        grid_spec=pltpu.PrefetchScalarGridSpec(
            num_scalar_prefetch=1, grid=(1,),
            in_specs=[pl.BlockSpec((1, rh, cc), lambda i, kc_ref: (kc_ref[0], 0, 0)) for rh, cc in dims] +
                     [pl.BlockSpec((3, rh, cc), lambda i, kc_ref: (0, 0, 0)) for rh, cc in dims],
            out_specs=[pl.BlockSpec((rh, cc), lambda i, kc_ref: (kc_ref[1], 0)) for rh, cc in dims]),
        compiler_params=_cp(("arbitrary",), 48),
    )(kc_arr, *_in_hbm(*sums, *recv3))


def _assemble_comm(totals):
    n = len(totals)

    def copy(outs, ss, rs, base, a):
        x, y, c = _mesh_pos()
        rh = outs[a].shape[0] // 2
        here = outs[a].at[pl.ds(c * rh, rh), :]
        return _remote(here, here, ss, rs, base + a, (x, y, 1 - c))

    def start(ins, outs, ss, rs, base):
        for a in range(n):
            copy(outs, ss, rs, base, a).start()

    def finish(ins, outs, ss, rs, base):
        for a in range(n):
            copy(outs, ss, rs, base, a).wait()

    shapes = [_out(t.shape, t.dtype) for t in totals]
    return _Comm(totals, shapes, {a: a for a in range(n)}, [(n, start, finish)])


def _small_layout(shapes):
    n = len(shapes)
    narrow_w = 64
    wide = [a for a in range(n) if shapes[a][1] > narrow_w]
    narrow = sorted((a for a in range(n) if shapes[a][1] <= narrow_w), key=lambda a: -shapes[a][0])
    offs, cols, groups, widths, rows = {}, {}, [], [], []
    if wide:
        r = 0
        for a in wide:
            offs[a], cols[a] = r, 0
            r += shapes[a][0]
        groups.append(wide)
        widths.append(max(shapes[a][1] for a in wide))
        rows.append(-(-r // 8) * 8)
    if narrow:
        heights = [0, 0]
        for a in narrow:
            side = 0 if heights[0] <= heights[1] else 1
            offs[a], cols[a] = heights[side], side * narrow_w
            heights[side] += shapes[a][0]
        groups.append(narrow)
        widths.append(2 * narrow_w)
        rows.append(-(-max(heights) // 8) * 8)

    def window(ref, a):
        return ref.at[offs[a]:offs[a] + shapes[a][0], cols[a]:cols[a] + shapes[a][1]]

    return groups, widths, rows, window


def _small_pack(arrays, me_arr):
    shapes = [a.shape for a in arrays]
    groups, widths, rows, window = _small_layout(shapes)
    n, n_g = len(arrays), len(groups)

    def body(me_ref, *refs):
        ins, outs = refs[:n], refs[n:]
        for gi, g in enumerate(groups):
            outs[gi][...] = jnp.zeros_like(outs[gi])
            for a in g:
                window(outs[gi].at[0], a)[...] = ins[a][...]

    return pl.pallas_call(
        body, name="small_pack", out_shape=[_out((8, r, w), F32) for r, w in zip(rows, widths)],
        grid_spec=pltpu.PrefetchScalarGridSpec(
            num_scalar_prefetch=1, grid=(1,), in_specs=[pl.BlockSpec(s, lambda i, me: (0, 0)) for s in shapes],
            out_specs=[pl.BlockSpec((1, r, w), lambda i, me: (me[0], 0, 0)) for r, w in zip(rows, widths)]),
        compiler_params=_cp(("arbitrary",), 32),
    )(me_arr, *_in_hbm(*arrays))


def _spread_comm(slots):
    n = len(slots)
    flips = [(dx, dy, dc) for dx in range(2) for dy in range(2) for dc in range(2)][1:]

    def copies(outs, ss, rs, base):
        x, y, c = _mesh_pos()
        mine = 4 * x + 2 * y + c
        return [_remote(outs[a].at[mine], outs[a].at[mine], ss, rs, base + 7 * a + f,
                        (x ^ dx, y ^ dy, c ^ dc)) for a in range(n) for f, (dx, dy, dc) in enumerate(flips)]

    def start(ins, outs, ss, rs, base):
        for cp in copies(outs, ss, rs, base):
            cp.start()

    def finish(ins, outs, ss, rs, base):
        for cp in copies(outs, ss, rs, base):
            cp.wait()

    return _Comm(slots, [_out(s.shape, s.dtype) for s in slots], {a: a for a in range(n)}, [(7 * n, start, finish)])


def _small_total(slots, shapes):
    groups, widths, rows, window = _small_layout(shapes)
    n, n_g = len(shapes), len(groups)

    def body(*refs):
        ins, outs, acc = refs[:n_g], refs[n_g:n_g + n], refs[n_g + n:]
        for gi, g in enumerate(groups):
            t = ins[gi][0] + ins[gi][1]
            for d in range(2, 8):
                t = t + ins[gi][d]
            acc[gi][...] = t
            for a in g:
                outs[a][...] = window(acc[gi], a)[...]

    return pl.pallas_call(
        body, name="small_total", grid=(1,), out_shape=[_out(s, F32) for s in shapes],
        in_specs=[_full(s.shape) for s in slots], out_specs=[_full(s) for s in shapes],
        scratch_shapes=[pltpu.VMEM((r, w), F32) for r, w in zip(rows, widths)],
        compiler_params=_cp(("arbitrary",), 48),
    )(*_in_hbm(*slots))


def _small_allreduce(arrays, comm):
    n = len(arrays)
    shapes = [a.shape for a in arrays]
    groups, widths, rows, window = _small_layout(shapes)
    n_g = len(groups)

    def body(*refs):
        ins, outs = refs[:n], refs[n:2 * n]
        pack, sib, csum, every = (refs[2 * n + i * n_g:2 * n + (i + 1) * n_g] for i in range(4))
        send_sems, recv_sems = refs[2 * n + 4 * n_g:]
        x, y, c = _mesh_pos()
        k = 2 * x + y
        for gi, g in enumerate(groups):
            pack[gi][...] = jnp.zeros_like(pack[gi])
            for a in g:
                window(pack[gi], a)[...] = ins[a][...]
        cps = [_remote(pack[gi], sib[gi], send_sems, recv_sems, gi, (x, y, 1 - c)) for gi in range(n_g)]
        for cp in cps:
            cp.start()
        for cp in cps:
            cp.wait()
        for gi in range(n_g):
            csum[gi][...] = pack[gi][...] + sib[gi][...]
            every[gi][k] = csum[gi][...]
        cps = [_remote(csum[gi], every[gi].at[k], send_sems, recv_sems, n_g + 3 * gi + j, (*chip, c))
               for gi in range(n_g) for j, chip in enumerate(_other_chips(x, y))]
        for cp in cps:
            cp.start()
        for cp in cps:
            cp.wait()
        for gi, g in enumerate(groups):
            pack[gi][...] = ((every[gi][0] + every[gi][1]) + every[gi][2]) + every[gi][3]
            for a in g:
                outs[a][...] = window(pack[gi], a)[...]

    bufs = [pltpu.VMEM((r, w), F32) for r, w in zip(rows, widths)]
    return _call(
        body, comm, (0,), arrays, name="small_allreduce", grid=(1,), out_shape=[_out(s, F32) for s in shapes],
        in_specs=[_full(s) for s in shapes], out_specs=[_full(s) for s in shapes],
        scratch_shapes=bufs * 3 + [pltpu.VMEM((N_CHIP, r, w), F32) for r, w in zip(rows, widths)] +
                       [pltpu.SemaphoreType.DMA((4 * n_g,)), pltpu.SemaphoreType.DMA((4 * n_g,))],
        compiler_params=_cp(("arbitrary",), 40))


def _adamw_small(ws, gs, ms, vs, comm):
    n = len(ws)

    def body(*refs):
        w, g, m, v, d, mo, vo = (refs[i * n:(i + 1) * n] for i in range(7))
        for a in range(n):
            d[a][...], mo[a][...], vo[a][...] = _adamw_math(w[a][...], g[a][...], m[a][...], v[a][...])

    specs = [_full(w.shape) for w in ws]
    res, got = _call(
        body, comm, (0,), (*ws, *gs, *ms, *vs), name="adamw_small", grid=(1,),
        out_shape=[_out(w.shape, F32) for w in ws] * 3,
        in_specs=specs * 4, out_specs=specs * 3, compiler_params=_cp(("arbitrary",), 40))
    return (res[:n], res[n:2 * n], res[2 * n:]), got


def _adamw_math(w, g, m, v):
    m = ADAM_B1 * m + (1.0 - ADAM_B1) * g
    v = ADAM_B2 * v + (1.0 - ADAM_B2) * (g * g)
    m_hat = m / (1.0 - ADAM_B1 ** ADAM_STEP)
    v_hat = v / (1.0 - ADAM_B2 ** ADAM_STEP)
    delta = -ADAM_LR * (m_hat / (jnp.sqrt(v_hat) + ADAM_EPS) + ADAM_WD * w)
    return delta, m, v


def _adamw_group(name, ws, gs, ms, vs):
    n = len(ws)
    steps = 8
    specs = [_rows(w.shape[0] // steps, w.shape[1]) for w in ws]
    assert all(w.shape[0] % (8 * steps) == 0 for w in ws)

    def body(*refs):
        w, g, m, v, d, mo, vo = (refs[i * n:(i + 1) * n] for i in range(7))
        for a in range(n):
            d[a][...], mo[a][...], vo[a][...] = _adamw_math(w[a][...], g[a][...], m[a][...], v[a][...])

    res = pl.pallas_call(
        body, name=name, grid=(steps,), in_specs=specs * 4, out_specs=specs * 3,
        out_shape=[_out(w.shape, F32) for w in ws] * 3, compiler_params=_cp(("arbitrary",), 40),
    )(*_in_hbm(*ws, *gs, *ms, *vs))
    return res[:n], res[n:2 * n], res[2 * n:]


def _as_matrix(name, a):
    if name == "na_rpb":
        return a[0].transpose(1, 0, 2).reshape(N_HEADS * (2 * KH - 1), 2 * KW - 1)
    if name in ("s5_b_re", "s5_b_im"):
        return a.transpose(0, 1, 2, 4, 3).reshape(2 * S5_G * S5_H, S5_P)
    if name in ("s5_c_re", "s5_c_im"):
        return a.reshape(2 * S5_G * S5_H, S5_P)
    if name in ("s5_lam_re", "s5_lam_im"):
        return a.reshape(2 * S5_G, S5_P)
    if name == "s5_log_dt":
        return a.reshape(2, S5_G)
    return a


def _from_matrix(name, m):
    if name == "na_rpb":
        return m.reshape(2 * KH - 1, N_HEADS, 2 * KW - 1).transpose(1, 0, 2)[None]
    if name in ("s5_b_re", "s5_b_im"):
        return m.reshape(1, 2, S5_G, S5_H, S5_P).transpose(0, 1, 2, 4, 3)
    if name in ("s5_c_re", "s5_c_im"):
        return m.reshape(1, 2, S5_G, S5_H, S5_P)
    if name in ("s5_lam_re", "s5_lam_im"):
        return m.reshape(1, 2, S5_G, S5_P)
    if name == "s5_log_dt":
        return m.reshape(1, 2, S5_G)
    return m


WEIGHTS = ["meta_tokens", "ffn1_pre_g", "ffn1_post_g", "ffn1_w_gate", "ffn1_w_up", "ffn1_w_down", "mix_pre_g", "w_in",
           "na_rpb", "s5_lam_re", "s5_lam_im", "s5_log_dt", "s5_b_re", "s5_b_im", "s5_c_re", "s5_c_im", "s5_d",
           "s5_w_glu", "s5_b_glu", "na_out_g", "s5_out_g", "w_out", "mix_post_g", "ffn2_pre_g", "ffn2_post_g",
           "ffn2_w_gate", "ffn2_w_up", "ffn2_w_down", "final_g"]
BIG = ["ffn1_w_gate", "ffn1_w_up", "ffn1_w_down", "w_in", "s5_w_glu", "w_out", "ffn2_w_gate", "ffn2_w_up",
       "ffn2_w_down"]
TRANSPOSED = ["ffn1_w_gate", "ffn1_w_up", "ffn2_w_gate", "ffn2_w_up"]
GAINS = ["ffn1_pre_g", "ffn1_post_g", "mix_pre_g", "s5_d", "s5_b_glu", "na_out_g", "s5_out_g", "mix_post_g",
         "ffn2_pre_g", "ffn2_post_g", "final_g"]
SMALL = [n for n in WEIGHTS if n not in BIG]


def kernel(*args):
    names = ["x"] + WEIGHTS + ["loss_target"] + ["m_" + n for n in WEIGHTS] + ["v_" + n for n in WEIGHTS]
    assert len(args) == len(names)
    given = dict(zip(names, args))
    x_pos, y_pos, c_pos = _mesh_pos()
    k_pos = 2 * x_pos + y_pos
    c_arr = jnp.reshape(c_pos, (1,)).astype(jnp.int32)
    kc_arr = jnp.stack([k_pos, c_pos]).astype(jnp.int32)

    def piece(name, a):
        return a[0].T if name in TRANSPOSED else a[0]

    def unpiece(name, a):
        return a.T[None] if name in TRANSPOSED else a[None]

    placed = BIG + ["meta_tokens"]
    bufs = dict(zip(placed, _own_half_buffers([piece(n, given[n]) for n in BIG] + [given["meta_tokens"]],
                                              [BF16] * len(BIG) + [F32], kc_arr)))

    gains = {n: given[n] for n in GAINS}
    s5 = {n: _as_matrix("s5_" + n, given["s5_" + n])
          for n in ["lam_re", "lam_im", "log_dt", "b_re", "b_im", "c_re", "c_im"]}
    me_arr = jnp.reshape(4 * x_pos + 2 * y_pos + c_pos, (1,)).astype(jnp.int32)
    loss, dh0, pieces, small, late, (ffn1, flight1), (mid, flight2) = _step(
        given["x"][0], given["loss_target"][0], bufs, gains, s5, given["na_rpb"][0], c_arr, kc_arr, me_arr)
    loss = lax.psum(loss, ("x", "y", "c"))
    n_tok = given["x"].shape[1]
    grad_x = dh0[N_META:N_META + n_tok][None]

    late["meta_tokens"] = dh0[:N_META]
    out_g, out_d, out_m, out_v = {}, {}, {}, {}

    def update_big(group, names):
        g2 = [pieces[n] for n in names]
        d2, m2, v2 = _adamw_group("adamw_" + group, [piece(n, given[n]) for n in names], g2,
                                  [piece(n, given["m_" + n]) for n in names], [piece(n, given["v_" + n]) for n in names])
        for n, g, dd, mm, vv in zip(names, g2, d2, m2, v2):
            out_g[n], out_d[n], out_m[n], out_v[n] = (unpiece(n, t) for t in (g, dd, mm, vv))
        return v2

    done2 = update_big("ffn2", list(pieces))
    sums1, recv1 = _scatter_wait("ffn1", *flight1, done2[-1])
    totals1 = _total_sums("total_sums_ffn1", sums1, recv1, kc_arr)
    pieces.update(zip(ffn1, _run_comm("ffn1_pair_assemble", _assemble_comm(totals1))))
    done1 = update_big("ffn1", ffn1)
    late_arrays = list(late.values())
    late_arrays[0], _ = lax.optimization_barrier((late_arrays[0], (done1[-1], small["final_g"])))
    red, _ = _small_allreduce(late_arrays, None)
    small.update(zip(late, red))
    mc = D // N_CHIP
    small["meta_tokens"] = lax.dynamic_slice_in_dim(small["meta_tokens"], k_pos * mc, mc, 1)
    sums_rest, recv_rest = _scatter_wait("rest", *flight2, red[0])
    totals = _total_sums("total_sums_rest", sums_rest, recv_rest, kc_arr)
    gs = [small[n] for n in SMALL]
    (d2, m2, v2), done = _adamw_small([_as_matrix(n, given[n]) for n in SMALL], gs,
                                      [_as_matrix(n, given["m_" + n]) for n in SMALL],
                                      [_as_matrix(n, given["v_" + n]) for n in SMALL], _assemble_comm(totals))
    pieces.update(zip(mid, done))

    for n, g, dd, mm, vv in zip(SMALL, gs, d2, m2, v2):
        out_g[n], out_d[n], out_m[n], out_v[n] = (_from_matrix(n, t) for t in (g, dd, mm, vv))
    update_big("rest", mid)
    return (loss, grad_x, *[out_g[n] for n in WEIGHTS], *[out_d[n] for n in WEIGHTS],
            *[out_m[n] for n in WEIGHTS], *[out_v[n] for n in WEIGHTS])
```
